```python
import jax, jax.numpy as jnp
from jax import lax
import numpy as np

D_MODEL = 1024
BATCH = 8
SEQ = 4096
DEPTH = 1

CHUNK = 64
RMS_EPS = 1e-6
ROPE_BASE = 10000.0
RET_HEADS = 4
RET_V = D_MODEL // 2
RET_QK = RET_V // 2
GLA_HEADS = 4
GLA_V = D_MODEL - RET_V
GLA_QK = GLA_V // 2
GLA_GATE_RANK = 16
GLA_GATE_NORM = 16.0
MIX_WIDTH = RET_V + GLA_V
IN_SIZES = (RET_QK, RET_QK, RET_V, RET_V, GLA_QK, GLA_QK, GLA_V, GLA_V, GLA_GATE_RANK)
IN_WIDTH = sum(IN_SIZES)
D_FF = ((8 * D_MODEL // 3 + 127) // 128) * 128

kernel_name = "macaron_retention_gla_hybrid"


def rms_norm(x, g):
    xf = x.astype(jnp.float32)
    y = xf * lax.rsqrt(jnp.mean(xf * xf, axis=-1, keepdims=True) + RMS_EPS)
    return (y * g.astype(jnp.float32)).astype(x.dtype)


def swiglu(h, w_gate, w_up, w_down):
    return (jax.nn.silu(h @ w_gate) * (h @ w_up)) @ w_down


def split_heads(t, n_heads):
    b, t_len, w = t.shape
    return t.reshape(b, t_len, n_heads, w // n_heads).transpose(0, 2, 1, 3)


def merge_heads(t):
    b, h, t_len, d = t.shape
    return t.transpose(0, 2, 1, 3).reshape(b, t_len, h * d)


def rotary(t, pos):
    dk = t.shape[-1]
    half = dk // 2
    inv = ROPE_BASE ** (-jnp.arange(half, dtype=jnp.float32) * 2.0 / dk)
    ang = pos[:, None] * inv[None, :]
    cos, sin = jnp.cos(ang), jnp.sin(ang)
    t1, t2 = t[..., :half], t[..., half:]
    return jnp.concatenate([t1 * cos - t2 * sin, t1 * sin + t2 * cos], axis=-1)


def head_rms(o):
    return o * lax.rsqrt(jnp.mean(o * o, axis=-1, keepdims=True) + RMS_EPS)


def chunk_decay_attention(q, k, v, log_a):
    bsz, n_h, t_len, dk = q.shape
    n_chunks = t_len // CHUNK
    per_key = log_a.shape[-1] != 1
    qf, kf, vf = (z.astype(jnp.float32) for z in (q, k, v))
    la = log_a.astype(jnp.float32)

    def to_chunks(z):
        return z.reshape(z.shape[0], z.shape[1], n_chunks, CHUNK, z.shape[-1]).transpose(2, 0, 1, 3, 4)

    b_cum = jnp.cumsum(to_chunks(la), axis=3)
    xs = (to_chunks(qf), to_chunks(kf), to_chunks(vf), b_cum)

    def step(state, inp):
        q_c, k_c, v_c, b_c = inp
        decay = jnp.exp(-jnp.abs(b_c[..., :, None, :] - b_c[..., None, :, :]))
        if per_key:
            scores = jnp.einsum('bhik,bhjk,bhijk->bhij', q_c, k_c, decay)
        else:
            scores = jnp.einsum('bhik,bhjk->bhij', q_c, k_c) * decay[..., 0]
        out = (jnp.einsum('bhij,bhjv->bhiv', scores, v_c)
               + jnp.einsum('bhik,bhkv->bhiv', q_c * jnp.exp(b_c), state))
        b_last = b_c[..., -1:, :]
        state = (jnp.exp(b_last[..., 0, :])[..., None] * state
                 + jnp.einsum('bhjk,bhjv->bhkv', k_c * jnp.exp(b_last - b_c), v_c))
        return state, out

    s0 = jnp.zeros((bsz, n_h, dk, v.shape[-1]), jnp.float32)
    _, outs = lax.scan(step, s0, xs)
    return outs.transpose(1, 2, 0, 3, 4).reshape(bsz, n_h, t_len, v.shape[-1])


def hybrid_mixer(h, w_in, ret_norm_g, gla_w_a2, gla_b_a, gla_norm_g, w_out):
    t_len = h.shape[1]
    pos = jnp.arange(t_len, dtype=jnp.float32)
    proj = h @ w_in
    offsets = [int(o) for o in np.cumsum(IN_SIZES)[:-1]]
    (r_q, r_k, r_v, r_g, g_q, g_k, g_v, g_g, g_low) = jnp.split(proj, offsets, axis=-1)

    rdk = RET_QK // RET_HEADS
    rq = rotary(split_heads(r_q, RET_HEADS), pos)
    rk = rotary(split_heads(r_k, RET_HEADS), pos) * (rdk ** -0.5)
    rv = split_heads(r_v, RET_HEADS)
    log_gamma = jnp.log(1.0 - 2.0 ** (-5.0 - jnp.arange(RET_HEADS, dtype=jnp.float32)))
    ret_log_a = jnp.broadcast_to(log_gamma[None, :, None, None], (1, RET_HEADS, t_len, 1))
    o_ret = head_rms(chunk_decay_attention(rq, rk, rv, ret_log_a))
    o_ret = (merge_heads(o_ret) * ret_norm_g.astype(jnp.float32)).astype(h.dtype) * jax.nn.silu(r_g)

    gdk = GLA_QK // GLA_HEADS
    gq = split_heads(g_q, GLA_HEADS) * (gdk ** -0.5)
    gk = split_heads(g_k, GLA_HEADS)
    gv = split_heads(g_v, GLA_HEADS)
    gate_logit = (g_low @ gla_w_a2 + gla_b_a).astype(jnp.float32)
    gla_log_a = split_heads(jax.nn.log_sigmoid(gate_logit) / GLA_GATE_NORM, GLA_HEADS)
    o_gla = head_rms(chunk_decay_attention(gq, gk, gv, gla_log_a))
    o_gla = (merge_heads(o_gla) * gla_norm_g.astype(jnp.float32)).astype(h.dtype) * jax.nn.silu(g_g)

    return jnp.concatenate([o_ret, o_gla], axis=-1) @ w_out


def _fwd_setup_inputs(seed: int = 0) -> dict:
    key = jax.random.key(seed)
    ks = jax.random.split(key, 20)
    f32 = jnp.float32

    def nrm(k, shape, fan_in):
        return jax.random.normal(k, shape, f32) * (fan_in ** -0.5)

    def gain(k, shape):
        return 1.0 + 0.02 * jax.random.normal(k, shape, f32)

    return {
        "x": jax.random.normal(ks[0], (BATCH, SEQ, D_MODEL), f32),
        "ffn1_norm_g": gain(ks[1], (DEPTH, D_MODEL)),
        "ffn1_w_gate": nrm(ks[2], (DEPTH, D_MODEL, D_FF), D_MODEL),
        "ffn1_w_up": nrm(ks[3], (DEPTH, D_MODEL, D_FF), D_MODEL),
        "ffn1_w_down": nrm(ks[4], (DEPTH, D_FF, D_MODEL), D_FF),
        "mix_norm_g": gain(ks[5], (DEPTH, D_MODEL)),
        "w_in": nrm(ks[6], (DEPTH, D_MODEL, IN_WIDTH), D_MODEL),
        "ret_norm_g": gain(ks[7], (DEPTH, RET_V)),
        "gla_w_a2": nrm(ks[8], (DEPTH, GLA_GATE_RANK, GLA_QK), GLA_GATE_RANK),
        "gla_b_a": 0.1 * jax.random.normal(ks[9], (DEPTH, GLA_QK), f32),
        "gla_norm_g": gain(ks[10], (DEPTH, GLA_V)),
        "w_out": nrm(ks[11], (DEPTH, MIX_WIDTH, D_MODEL), MIX_WIDTH),
        "ffn2_norm_g": gain(ks[12], (DEPTH, D_MODEL)),
        "ffn2_w_gate": nrm(ks[13], (DEPTH, D_MODEL, D_FF), D_MODEL),
        "ffn2_w_up": nrm(ks[14], (DEPTH, D_MODEL, D_FF), D_MODEL),
        "ffn2_w_down": nrm(ks[15], (DEPTH, D_FF, D_MODEL), D_FF),
        "final_norm_g": gain(ks[16], (D_MODEL,)),
    }


def _fwd_reference(x, ffn1_norm_g, ffn1_w_gate, ffn1_w_up, ffn1_w_down, mix_norm_g, w_in,
              ret_norm_g, gla_w_a2, gla_b_a, gla_norm_g, w_out, ffn2_norm_g,
              ffn2_w_gate, ffn2_w_up, ffn2_w_down, final_norm_g):
    for l in range(DEPTH):
        x = x + 0.5 * swiglu(rms_norm(x, ffn1_norm_g[l]), ffn1_w_gate[l], ffn1_w_up[l], ffn1_w_down[l])
        x = x + hybrid_mixer(rms_norm(x, mix_norm_g[l]), w_in[l], ret_norm_g[l], gla_w_a2[l],
                             gla_b_a[l], gla_norm_g[l], w_out[l])
        x = x + 0.5 * swiglu(rms_norm(x, ffn2_norm_g[l]), ffn2_w_gate[l], ffn2_w_up[l], ffn2_w_down[l])
    return rms_norm(x, final_norm_g)


import jax as _jax
import jax.numpy as _jnp

TWIN_FORMAT = 'train_step'
FWD_PARAMS = ['x', 'ffn1_norm_g', 'ffn1_w_gate', 'ffn1_w_up', 'ffn1_w_down', 'mix_norm_g', 'w_in', 'ret_norm_g', 'gla_w_a2', 'gla_b_a', 'gla_norm_g', 'w_out', 'ffn2_norm_g', 'ffn2_w_gate', 'ffn2_w_up', 'ffn2_w_down', 'final_norm_g']
TWIN_WEIGHTS = ['ffn1_norm_g', 'ffn1_w_gate', 'ffn1_w_up', 'ffn1_w_down', 'mix_norm_g', 'w_in', 'ret_norm_g', 'gla_w_a2', 'gla_b_a', 'gla_norm_g', 'w_out', 'ffn2_norm_g', 'ffn2_w_gate', 'ffn2_w_up', 'ffn2_w_down', 'final_norm_g']
TWIN_DIFF_INPUT = 'x'
TWIN_INPUTS = ['x', 'ffn1_norm_g', 'ffn1_w_gate', 'ffn1_w_up', 'ffn1_w_down', 'mix_norm_g', 'w_in', 'ret_norm_g', 'gla_w_a2', 'gla_b_a', 'gla_norm_g', 'w_out', 'ffn2_norm_g', 'ffn2_w_gate', 'ffn2_w_up', 'ffn2_w_down', 'final_norm_g', 'loss_target', 'm_ffn1_norm_g', 'm_ffn1_w_gate', 'm_ffn1_w_up', 'm_ffn1_w_down', 'm_mix_norm_g', 'm_w_in', 'm_ret_norm_g', 'm_gla_w_a2', 'm_gla_b_a', 'm_gla_norm_g', 'm_w_out', 'm_ffn2_norm_g', 'm_ffn2_w_gate', 'm_ffn2_w_up', 'm_ffn2_w_down', 'm_final_norm_g', 'v_ffn1_norm_g', 'v_ffn1_w_gate', 'v_ffn1_w_up', 'v_ffn1_w_down', 'v_mix_norm_g', 'v_w_in', 'v_ret_norm_g', 'v_gla_w_a2', 'v_gla_b_a', 'v_gla_norm_g', 'v_w_out', 'v_ffn2_norm_g', 'v_ffn2_w_gate', 'v_ffn2_w_up', 'v_ffn2_w_down', 'v_final_norm_g']
TWIN_OUTPUTS = ['loss', 'grad_x', 'grad_ffn1_norm_g', 'grad_ffn1_w_gate', 'grad_ffn1_w_up', 'grad_ffn1_w_down', 'grad_mix_norm_g', 'grad_w_in', 'grad_ret_norm_g', 'grad_gla_w_a2', 'grad_gla_b_a', 'grad_gla_norm_g', 'grad_w_out', 'grad_ffn2_norm_g', 'grad_ffn2_w_gate', 'grad_ffn2_w_up', 'grad_ffn2_w_down', 'grad_final_norm_g', 'delta_ffn1_norm_g', 'delta_ffn1_w_gate', 'delta_ffn1_w_up', 'delta_ffn1_w_down', 'delta_mix_norm_g', 'delta_w_in', 'delta_ret_norm_g', 'delta_gla_w_a2', 'delta_gla_b_a', 'delta_gla_norm_g', 'delta_w_out', 'delta_ffn2_norm_g', 'delta_ffn2_w_gate', 'delta_ffn2_w_up', 'delta_ffn2_w_down', 'delta_final_norm_g', 'new_m_ffn1_norm_g', 'new_m_ffn1_w_gate', 'new_m_ffn1_w_up', 'new_m_ffn1_w_down', 'new_m_mix_norm_g', 'new_m_w_in', 'new_m_ret_norm_g', 'new_m_gla_w_a2', 'new_m_gla_b_a', 'new_m_gla_norm_g', 'new_m_w_out', 'new_m_ffn2_norm_g', 'new_m_ffn2_w_gate', 'new_m_ffn2_w_up', 'new_m_ffn2_w_down', 'new_m_final_norm_g', 'new_v_ffn1_norm_g', 'new_v_ffn1_w_gate', 'new_v_ffn1_w_up', 'new_v_ffn1_w_down', 'new_v_mix_norm_g', 'new_v_w_in', 'new_v_ret_norm_g', 'new_v_gla_w_a2', 'new_v_gla_b_a', 'new_v_gla_norm_g', 'new_v_w_out', 'new_v_ffn2_norm_g', 'new_v_ffn2_w_gate', 'new_v_ffn2_w_up', 'new_v_ffn2_w_down', 'new_v_final_norm_g']
TWIN_LEAF_KINDS = {'loss': 'loss', 'grad_x': 'grad_x', 'grad_ffn1_norm_g': 'grad_w', 'grad_ffn1_w_gate': 'grad_w', 'grad_ffn1_w_up': 'grad_w', 'grad_ffn1_w_down': 'grad_w', 'grad_mix_norm_g': 'grad_w', 'grad_w_in': 'grad_w', 'grad_ret_norm_g': 'grad_w', 'grad_gla_w_a2': 'grad_w', 'grad_gla_b_a': 'grad_w', 'grad_gla_norm_g': 'grad_w', 'grad_w_out': 'grad_w', 'grad_ffn2_norm_g': 'grad_w', 'grad_ffn2_w_gate': 'grad_w', 'grad_ffn2_w_up': 'grad_w', 'grad_ffn2_w_down': 'grad_w', 'grad_final_norm_g': 'grad_w', 'delta_ffn1_norm_g': 'delta_w', 'delta_ffn1_w_gate': 'delta_w', 'delta_ffn1_w_up': 'delta_w', 'delta_ffn1_w_down': 'delta_w', 'delta_mix_norm_g': 'delta_w', 'delta_w_in': 'delta_w', 'delta_ret_norm_g': 'delta_w', 'delta_gla_w_a2': 'delta_w', 'delta_gla_b_a': 'delta_w', 'delta_gla_norm_g': 'delta_w', 'delta_w_out': 'delta_w', 'delta_ffn2_norm_g': 'delta_w', 'delta_ffn2_w_gate': 'delta_w', 'delta_ffn2_w_up': 'delta_w', 'delta_ffn2_w_down': 'delta_w', 'delta_final_norm_g': 'delta_w', 'new_m_ffn1_norm_g': 'new_m', 'new_m_ffn1_w_gate': 'new_m', 'new_m_ffn1_w_up': 'new_m', 'new_m_ffn1_w_down': 'new_m', 'new_m_mix_norm_g': 'new_m', 'new_m_w_in': 'new_m', 'new_m_ret_norm_g': 'new_m', 'new_m_gla_w_a2': 'new_m', 'new_m_gla_b_a': 'new_m', 'new_m_gla_norm_g': 'new_m', 'new_m_w_out': 'new_m', 'new_m_ffn2_norm_g': 'new_m', 'new_m_ffn2_w_gate': 'new_m', 'new_m_ffn2_w_up': 'new_m', 'new_m_ffn2_w_down': 'new_m', 'new_m_final_norm_g': 'new_m', 'new_v_ffn1_norm_g': 'new_v', 'new_v_ffn1_w_gate': 'new_v', 'new_v_ffn1_w_up': 'new_v', 'new_v_ffn1_w_down': 'new_v', 'new_v_mix_norm_g': 'new_v', 'new_v_w_in': 'new_v', 'new_v_ret_norm_g': 'new_v', 'new_v_gla_w_a2': 'new_v', 'new_v_gla_b_a': 'new_v', 'new_v_gla_norm_g': 'new_v', 'new_v_w_out': 'new_v', 'new_v_ffn2_norm_g': 'new_v', 'new_v_ffn2_w_gate': 'new_v', 'new_v_ffn2_w_up': 'new_v', 'new_v_ffn2_w_down': 'new_v', 'new_v_final_norm_g': 'new_v'}


def _forward(args):
    return _fwd_reference(*[args[k] for k in FWD_PARAMS])


def _output_shape():
    def fwd():
        inp = _fwd_setup_inputs(0)
        return _fwd_reference(*[inp[k] for k in FWD_PARAMS])
    out = _jax.eval_shape(fwd)
    return out.shape, out.dtype

N_MICROBATCH = 1
ADAM_LR = 0.001
ADAM_B1 = 0.9
ADAM_B2 = 0.999
ADAM_EPS = 1e-08
ADAM_WD = 0.01
ADAM_STEP = 10
PER_EXAMPLE_BATCH_AXIS = {'x': 0, 'loss_target': 0}
SHARED_INPUTS = []
_WEIGHT_DTYPES = {'ffn1_norm_g': _jnp.float32, 'ffn1_w_gate': _jnp.float32, 'ffn1_w_up': _jnp.float32, 'ffn1_w_down': _jnp.float32, 'mix_norm_g': _jnp.float32, 'w_in': _jnp.float32, 'ret_norm_g': _jnp.float32, 'gla_w_a2': _jnp.float32, 'gla_b_a': _jnp.float32, 'gla_norm_g': _jnp.float32, 'w_out': _jnp.float32, 'ffn2_norm_g': _jnp.float32, 'ffn2_w_gate': _jnp.float32, 'ffn2_w_up': _jnp.float32, 'ffn2_w_down': _jnp.float32, 'final_norm_g': _jnp.float32}
MOMENT_SCALE = {'ffn1_norm_g': 1.097227e-01, 'ffn1_w_gate': 4.299770e-02, 'ffn1_w_up': 4.155135e-02, 'ffn1_w_down': 6.901641e-02, 'mix_norm_g': 1.895874e-01, 'w_in': 1.062720e-01, 'ret_norm_g': 9.617952e-02, 'gla_w_a2': 1.407094e-02, 'gla_b_a': 5.249266e-02, 'gla_norm_g': 9.767522e-02, 'w_out': 9.047820e-02, 'ffn2_norm_g': 6.273964e-02, 'ffn2_w_gate': 2.667275e-02, 'ffn2_w_up': 2.586614e-02, 'ffn2_w_down': 4.282935e-02, 'final_norm_g': 3.201487e+01}


def _to_microbatches(a, axis):
    t = _jnp.moveaxis(a, axis, 0)
    t = t.reshape((N_MICROBATCH, t.shape[0] // N_MICROBATCH) + t.shape[1:])
    return _jnp.moveaxis(t, 1, axis + 1)


def setup_inputs(seed: int = 0) -> dict:
    inp = _fwd_setup_inputs(seed)
    key = _jax.random.fold_in(_jax.random.key(seed), 7919)
    shape, _ = _output_shape()
    out = dict(inp)
    out["loss_target"] = _jax.random.normal(_jax.random.fold_in(key, 0), shape, _jnp.float32)
    for i, name in enumerate(TWIN_WEIGHTS):
        w = inp[name].astype(_jnp.float32)
        if MOMENT_SCALE is None:
            s = _jnp.sqrt(_jnp.mean(_jnp.square(w)) + 1e-30)
        else:
            s = MOMENT_SCALE[name]
        km, kv = _jax.random.split(_jax.random.fold_in(key, i + 1))
        out[name] = w
        out["m_" + name] = s * _jax.random.normal(km, w.shape, _jnp.float32)
        out["v_" + name] = (s * s) * _jax.random.uniform(kv, w.shape, _jnp.float32, 0.5, 1.5)
    if N_MICROBATCH > 1:
        for name, axis in PER_EXAMPLE_BATCH_AXIS.items():
            out[name] = _to_microbatches(out[name], axis)
    return {'x': out['x'], 'ffn1_norm_g': out['ffn1_norm_g'], 'ffn1_w_gate': out['ffn1_w_gate'], 'ffn1_w_up': out['ffn1_w_up'], 'ffn1_w_down': out['ffn1_w_down'], 'mix_norm_g': out['mix_norm_g'], 'w_in': out['w_in'], 'ret_norm_g': out['ret_norm_g'], 'gla_w_a2': out['gla_w_a2'], 'gla_b_a': out['gla_b_a'], 'gla_norm_g': out['gla_norm_g'], 'w_out': out['w_out'], 'ffn2_norm_g': out['ffn2_norm_g'], 'ffn2_w_gate': out['ffn2_w_gate'], 'ffn2_w_up': out['ffn2_w_up'], 'ffn2_w_down': out['ffn2_w_down'], 'final_norm_g': out['final_norm_g'], 'loss_target': out['loss_target'], 'm_ffn1_norm_g': out['m_ffn1_norm_g'], 'm_ffn1_w_gate': out['m_ffn1_w_gate'], 'm_ffn1_w_up': out['m_ffn1_w_up'], 'm_ffn1_w_down': out['m_ffn1_w_down'], 'm_mix_norm_g': out['m_mix_norm_g'], 'm_w_in': out['m_w_in'], 'm_ret_norm_g': out['m_ret_norm_g'], 'm_gla_w_a2': out['m_gla_w_a2'], 'm_gla_b_a': out['m_gla_b_a'], 'm_gla_norm_g': out['m_gla_norm_g'], 'm_w_out': out['m_w_out'], 'm_ffn2_norm_g': out['m_ffn2_norm_g'], 'm_ffn2_w_gate': out['m_ffn2_w_gate'], 'm_ffn2_w_up': out['m_ffn2_w_up'], 'm_ffn2_w_down': out['m_ffn2_w_down'], 'm_final_norm_g': out['m_final_norm_g'], 'v_ffn1_norm_g': out['v_ffn1_norm_g'], 'v_ffn1_w_gate': out['v_ffn1_w_gate'], 'v_ffn1_w_up': out['v_ffn1_w_up'], 'v_ffn1_w_down': out['v_ffn1_w_down'], 'v_mix_norm_g': out['v_mix_norm_g'], 'v_w_in': out['v_w_in'], 'v_ret_norm_g': out['v_ret_norm_g'], 'v_gla_w_a2': out['v_gla_w_a2'], 'v_gla_b_a': out['v_gla_b_a'], 'v_gla_norm_g': out['v_gla_norm_g'], 'v_w_out': out['v_w_out'], 'v_ffn2_norm_g': out['v_ffn2_norm_g'], 'v_ffn2_w_gate': out['v_ffn2_w_gate'], 'v_ffn2_w_up': out['v_ffn2_w_up'], 'v_ffn2_w_down': out['v_ffn2_w_down'], 'v_final_norm_g': out['v_final_norm_g']}


def _loss(weights, diff, rest, loss_target):
    with _jax.named_scope("forward"):
        args = {**rest, TWIN_DIFF_INPUT: diff, **{k: w.astype(_WEIGHT_DTYPES[k]) for k, w in weights.items()}}
        y = _forward(args)
    with _jax.named_scope("loss_head"):
        err = _jnp.square(y.astype(_jnp.float32) - loss_target)
        return 0.5 * _jnp.sum(_jnp.mean(err, axis=-1)) if err.ndim else 0.5 * err


def _adamw(w, g, m, v):
    m = ADAM_B1 * m + (1.0 - ADAM_B1) * g
    v = ADAM_B2 * v + (1.0 - ADAM_B2) * _jnp.square(g)
    m_hat = m / (1.0 - ADAM_B1 ** ADAM_STEP)
    v_hat = v / (1.0 - ADAM_B2 ** ADAM_STEP)
    delta = -ADAM_LR * (m_hat / (_jnp.sqrt(v_hat) + ADAM_EPS) + ADAM_WD * w)
    return delta, m, v


def reference(x, ffn1_norm_g, ffn1_w_gate, ffn1_w_up, ffn1_w_down, mix_norm_g, w_in, ret_norm_g, gla_w_a2, gla_b_a, gla_norm_g, w_out, ffn2_norm_g, ffn2_w_gate, ffn2_w_up, ffn2_w_down, final_norm_g, loss_target, m_ffn1_norm_g, m_ffn1_w_gate, m_ffn1_w_up, m_ffn1_w_down, m_mix_norm_g, m_w_in, m_ret_norm_g, m_gla_w_a2, m_gla_b_a, m_gla_norm_g, m_w_out, m_ffn2_norm_g, m_ffn2_w_gate, m_ffn2_w_up, m_ffn2_w_down, m_final_norm_g, v_ffn1_norm_g, v_ffn1_w_gate, v_ffn1_w_up, v_ffn1_w_down, v_mix_norm_g, v_w_in, v_ret_norm_g, v_gla_w_a2, v_gla_b_a, v_gla_norm_g, v_w_out, v_ffn2_norm_g, v_ffn2_w_gate, v_ffn2_w_up, v_ffn2_w_down, v_final_norm_g):
    given = dict(x=x, ffn1_norm_g=ffn1_norm_g, ffn1_w_gate=ffn1_w_gate, ffn1_w_up=ffn1_w_up, ffn1_w_down=ffn1_w_down, mix_norm_g=mix_norm_g, w_in=w_in, ret_norm_g=ret_norm_g, gla_w_a2=gla_w_a2, gla_b_a=gla_b_a, gla_norm_g=gla_norm_g, w_out=w_out, ffn2_norm_g=ffn2_norm_g, ffn2_w_gate=ffn2_w_gate, ffn2_w_up=ffn2_w_up, ffn2_w_down=ffn2_w_down, final_norm_g=final_norm_g, loss_target=loss_target, m_ffn1_norm_g=m_ffn1_norm_g, m_ffn1_w_gate=m_ffn1_w_gate, m_ffn1_w_up=m_ffn1_w_up, m_ffn1_w_down=m_ffn1_w_down, m_mix_norm_g=m_mix_norm_g, m_w_in=m_w_in, m_ret_norm_g=m_ret_norm_g, m_gla_w_a2=m_gla_w_a2, m_gla_b_a=m_gla_b_a, m_gla_norm_g=m_gla_norm_g, m_w_out=m_w_out, m_ffn2_norm_g=m_ffn2_norm_g, m_ffn2_w_gate=m_ffn2_w_gate, m_ffn2_w_up=m_ffn2_w_up, m_ffn2_w_down=m_ffn2_w_down, m_final_norm_g=m_final_norm_g, v_ffn1_norm_g=v_ffn1_norm_g, v_ffn1_w_gate=v_ffn1_w_gate, v_ffn1_w_up=v_ffn1_w_up, v_ffn1_w_down=v_ffn1_w_down, v_mix_norm_g=v_mix_norm_g, v_w_in=v_w_in, v_ret_norm_g=v_ret_norm_g, v_gla_w_a2=v_gla_w_a2, v_gla_b_a=v_gla_b_a, v_gla_norm_g=v_gla_norm_g, v_w_out=v_w_out, v_ffn2_norm_g=v_ffn2_norm_g, v_ffn2_w_gate=v_ffn2_w_gate, v_ffn2_w_up=v_ffn2_w_up, v_ffn2_w_down=v_ffn2_w_down, v_final_norm_g=v_final_norm_g)
    weights = {n: given[n] for n in TWIN_WEIGHTS}
    shared = {n: given[n] for n in SHARED_INPUTS}
    per_example = {n: given[n] for n in ['x']}
    grad_fn = _jax.value_and_grad(_loss, argnums=(0, 1))

    def one_microbatch(ex, loss_target):
        ex = dict(ex)
        diff = ex.pop(TWIN_DIFF_INPUT)
        return grad_fn(weights, diff, {**shared, **ex}, loss_target)

    if N_MICROBATCH == 1:
        loss, (grad_w, grad_x) = one_microbatch(per_example, given["loss_target"])
    else:
        def body(carry, xs):
            loss_sum, grad_sum = carry
            l_k, (gw_k, gx_k) = one_microbatch(xs[0], xs[1])
            with _jax.named_scope("update"):
                return (loss_sum + l_k, _jax.tree.map(_jnp.add, grad_sum, gw_k)), gx_k

        init = (_jnp.zeros((), _jnp.float32), _jax.tree.map(_jnp.zeros_like, weights))
        (loss, grad_w), grad_x = _jax.lax.scan(body, init, (per_example, given["loss_target"]))
    with _jax.named_scope("update"):
        delta_w, new_m, new_v = {}, {}, {}
        for n in TWIN_WEIGHTS:
            delta_w[n], new_m[n], new_v[n] = _adamw(weights[n], grad_w[n], given["m_" + n], given["v_" + n])
    return (loss, grad_x, *[grad_w[n] for n in TWIN_WEIGHTS], *[delta_w[n] for n in TWIN_WEIGHTS],
            *[new_m[n] for n in TWIN_WEIGHTS], *[new_v[n] for n in TWIN_WEIGHTS])
```

```python
import functools

import jax
import jax.numpy as jnp
from jax import lax
from jax.experimental import pallas as pl
from jax.experimental.pallas import tpu as pltpu

F32, BF16 = jnp.float32, jnp.bfloat16
MESH = pl.DeviceIdType.MESH
ANY = pl.BlockSpec(memory_space=pl.ANY)

D_MODEL = 1024
D_FF = 2816
N_CHIPS = 4
FF_SHARD = D_FF // N_CHIPS
IN_WIDTH = 3088
IN_SHARD = IN_WIDTH // N_CHIPS
CHUNK = 64
HEADS = 4
LANES = 128
HEAD_BLOCK = 4 * LANES
PROJ_P = 2 * HEADS * HEAD_BLOCK + LANES
GATE_RANK = 16
QK_SCALE = 0.125
GATE_NORM = 16.0
RMS_EPS = 1e-6
ROPE_BASE = 10000.0
ADAM_LR, ADAM_B1, ADAM_B2, ADAM_EPS, ADAM_WD, ADAM_STEP = 0.001, 0.9, 0.999, 1e-08, 0.01, 10
SMALL_ROWS = 32
TOKEN_TILE = 512
ATTN_TILE = 512

_ARB2 = pltpu.CompilerParams(dimension_semantics=("arbitrary", "arbitrary"))
_ARB1 = pltpu.CompilerParams(dimension_semantics=("arbitrary",))
_ARB3 = pltpu.CompilerParams(dimension_semantics=("arbitrary", "arbitrary", "arbitrary"))


def _dot(a, b):
    return jnp.dot(a, b, preferred_element_type=F32)


def _dot_nt(a, b):
    return lax.dot_general(a, b, (((1,), (1,)), ((), ())), preferred_element_type=F32)


def _dot_tn(a, b):
    return lax.dot_general(a, b, (((0,), (0,)), ((), ())), preferred_element_type=F32)


def _rms_scale(xv):
    return lax.rsqrt(jnp.mean(xv * xv, axis=-1, keepdims=True) + RMS_EPS)


def _rms_bwd(dh, xv, g):
    r = _rms_scale(xv)
    xhat = xv * r
    dxhat = dh * g
    dx = r * (dxhat - xhat * jnp.mean(dxhat * xhat, axis=-1, keepdims=True))
    return dx, jnp.sum(dh * xhat, axis=0, keepdims=True)


def _silu_grad(a, sg):
    return sg * (1.0 + a * (1.0 - sg))


def _ffn_fwd(x, g, gu, dn, which, name):
    t = x.shape[0]
    tm = min(t, TOKEN_TILE)

    def body(x_ref, g_ref, wg_ref, wu_ref, wd_ref, xo_ref, a_ref, u_ref, h_ref, acc_ref):
        j = pl.program_id(1)

        @pl.when(j == 0)
        def _():
            xv = x_ref[...]
            h_ref[...] = ((xv * _rms_scale(xv)) * g_ref[...]).astype(BF16)
            acc_ref[...] = jnp.zeros_like(acc_ref)

        h = h_ref[...]
        a = _dot(h, wg_ref[...])
        u = _dot(h, wu_ref[...])
        a_ref[...] = a.astype(BF16)
        u_ref[...] = u.astype(BF16)
        hid = (a * jax.nn.sigmoid(a)) * u
        acc_ref[...] += _dot(hid.astype(BF16), wd_ref[...])

        @pl.when(j == N_CHIPS - 1)
        def _():
            xo_ref[...] = x_ref[...] + 0.5 * acc_ref[...]

    tok = pl.BlockSpec((tm, D_MODEL), lambda i, j: (i, 0))
    act = pl.BlockSpec((None, tm, FF_SHARD), lambda i, j: (j, i, 0))
    return pl.pallas_call(
        body, name=name, grid=(t // tm, N_CHIPS),
        in_specs=[tok, pl.BlockSpec((1, D_MODEL), lambda i, j: (0, 0)),
                  pl.BlockSpec((None, None, D_MODEL, FF_SHARD), lambda i, j: (j, 2 * which, 0, 0)),
                  pl.BlockSpec((None, None, D_MODEL, FF_SHARD), lambda i, j: (j, 2 * which + 1, 0, 0)),
                  pl.BlockSpec((None, None, FF_SHARD, D_MODEL), lambda i, j: (j, which, 0, 0))],
        out_specs=[tok, act, act, tok],
        out_shape=[jax.ShapeDtypeStruct((t, D_MODEL), F32),
                   jax.ShapeDtypeStruct((N_CHIPS, t, FF_SHARD), BF16),
                   jax.ShapeDtypeStruct((N_CHIPS, t, FF_SHARD), BF16),
                   jax.ShapeDtypeStruct((t, D_MODEL), BF16)],
        scratch_shapes=[pltpu.VMEM((tm, D_MODEL), F32)],
        compiler_params=_ARB2,
    )(x, g, gu, gu, dn)


def _ffn_bwd(dxo, x, g, a4, u4, gu, dn, which, name):
    t = x.shape[0]
    tm = min(t, TOKEN_TILE)

    def body(dxo_ref, x_ref, g_ref, a_ref, u_ref, wg_ref, wu_ref, wd_ref,
             da_ref, du_ref, hid_ref, dob_ref, dx_ref, dg_ref, acc_ref):
        i, j = pl.program_id(0), pl.program_id(1)

        @pl.when(j == 0)
        def _():
            dob_ref[...] = (0.5 * dxo_ref[...]).astype(BF16)
            acc_ref[...] = jnp.zeros_like(acc_ref)

        @pl.when((i == 0) & (j == 0))
        def _():
            dg_ref[...] = jnp.zeros_like(dg_ref)

        dhid = _dot_nt(dob_ref[...], wd_ref[...])
        a = a_ref[...].astype(F32)
        u = u_ref[...].astype(F32)
        sg = jax.nn.sigmoid(a)
        s = a * sg
        hid_ref[...] = (s * u).astype(BF16)
        du = (dhid * s).astype(BF16)
        da = (dhid * u * _silu_grad(a, sg)).astype(BF16)
        du_ref[...] = du
        da_ref[...] = da
        acc_ref[...] += _dot_nt(da, wg_ref[...]) + _dot_nt(du, wu_ref[...])

        @pl.when(j == N_CHIPS - 1)
        def _():
            dx, dg = _rms_bwd(acc_ref[...], x_ref[...], g_ref[...])
            dx_ref[...] = dxo_ref[...] + dx
            dg_ref[...] += dg

    tok = pl.BlockSpec((tm, D_MODEL), lambda i, j: (i, 0))
    act = pl.BlockSpec((None, tm, FF_SHARD), lambda i, j: (j, i, 0))
    row = pl.BlockSpec((1, D_MODEL), lambda i, j: (0, 0))
    act_shape = jax.ShapeDtypeStruct((N_CHIPS, t, FF_SHARD), BF16)
    return pl.pallas_call(
        body, name=name, grid=(t // tm, N_CHIPS),
        in_specs=[tok, tok, row, act, act,
                  pl.BlockSpec((None, None, D_MODEL, FF_SHARD), lambda i, j: (j, 2 * which, 0, 0)),
                  pl.BlockSpec((None, None, D_MODEL, FF_SHARD), lambda i, j: (j, 2 * which + 1, 0, 0)),
                  pl.BlockSpec((None, None, FF_SHARD, D_MODEL), lambda i, j: (j, which, 0, 0))],
        out_specs=[act, act, act, tok, tok, row],
        out_shape=[act_shape, act_shape, act_shape,
                   jax.ShapeDtypeStruct((t, D_MODEL), BF16),
                   jax.ShapeDtypeStruct((t, D_MODEL), F32),
                   jax.ShapeDtypeStruct((1, D_MODEL), F32)],
        scratch_shapes=[pltpu.VMEM((tm, D_MODEL), F32)],
        compiler_params=_ARB2,
    )(dxo, x, g, a4, u4, gu, gu, dn)


def _matmul_tn(a, b, name, tn=None):
    a3, b3 = a.ndim == 3, b.ndim == 3
    nb = a.shape[0] if a3 else (b.shape[0] if b3 else 1)
    t, ka, n = a.shape[-2], a.shape[-1], b.shape[-1]
    tn = n if tn is None else tn
    tk = min(t, TOKEN_TILE)

    def body(a_ref, b_ref, o_ref):
        @pl.when(pl.program_id(2) == 0)
        def _():
            o_ref[...] = jnp.zeros_like(o_ref)

        o_ref[...] += _dot_tn(a_ref[...].astype(BF16), b_ref[...].astype(BF16))

    a_spec = (pl.BlockSpec((None, tk, ka), lambda i, j, k: (i, k, 0)) if a3
              else pl.BlockSpec((tk, ka), lambda i, j, k: (k, 0)))
    b_spec = (pl.BlockSpec((None, tk, tn), lambda i, j, k: (i, k, j)) if b3
              else pl.BlockSpec((tk, tn), lambda i, j, k: (k, j)))
    return pl.pallas_call(
        body, name=name, grid=(nb, n // tn, t // tk),
        in_specs=[a_spec, b_spec],
        out_specs=pl.BlockSpec((None, ka, tn), lambda i, j, k: (i, 0, j)),
        out_shape=jax.ShapeDtypeStruct((nb, ka, n), F32),
        compiler_params=_ARB3,
    )(a, b)


def _matmul_nt(a, w, name, out_dtype=F32):
    t, k = a.shape
    n = w.shape[0]
    tm = min(t, TOKEN_TILE)

    def body(a_ref, w_ref, o_ref):
        o_ref[...] = _dot_nt(a_ref[...].astype(BF16), w_ref[...]).astype(out_dtype)

    return pl.pallas_call(
        body, name=name, grid=(t // tm,),
        in_specs=[pl.BlockSpec((tm, k), lambda i: (i, 0)), pl.BlockSpec((n, k), lambda i: (0, 0))],
        out_specs=pl.BlockSpec((tm, n), lambda i: (i, 0)),
        out_shape=jax.ShapeDtypeStruct((t, n), out_dtype),
        compiler_params=_ARB1,
    )(a, w)


def _mixer_in_bwd(dproj, w_in_p, dres, x, g, name):
    t, k = dproj.shape
    tm = min(t, TOKEN_TILE)

    def body(a_ref, w_ref, dres_ref, x_ref, g_ref, dx_ref, dg_ref):
        @pl.when(pl.program_id(0) == 0)
        def _():
            dg_ref[...] = jnp.zeros_like(dg_ref)

        dh = _dot_nt(a_ref[...], w_ref[...])
        dx, dg = _rms_bwd(dh, x_ref[...], g_ref[...])
        dx_ref[...] = dres_ref[...] + dx
        dg_ref[...] += dg

    tok = pl.BlockSpec((tm, D_MODEL), lambda i: (i, 0))
    row = pl.BlockSpec((1, D_MODEL), lambda i: (0, 0))
    return pl.pallas_call(
        body, name=name, grid=(t // tm,),
        in_specs=[pl.BlockSpec((tm, k), lambda i: (i, 0)), pl.BlockSpec((D_MODEL, k), lambda i: (0, 0)), tok, tok, row],
        out_specs=[tok, row],
        out_shape=[jax.ShapeDtypeStruct((t, D_MODEL), F32), jax.ShapeDtypeStruct((1, D_MODEL), F32)],
        compiler_params=_ARB1,
    )(dproj, w_in_p, dres, x, g)


def _mixer_in_fwd(x, g, w_in_p, name):
    t = x.shape[0]
    tm = min(t, TOKEN_TILE)
    tn = PROJ_P // 3

    def body(x_ref, g_ref, w_ref, p_ref, h_ref):
        @pl.when(pl.program_id(1) == 0)
        def _():
            xv = x_ref[...]
            h_ref[...] = ((xv * _rms_scale(xv)) * g_ref[...]).astype(BF16)

        p_ref[...] = _dot(h_ref[...], w_ref[...])

    tok = pl.BlockSpec((tm, D_MODEL), lambda i, j: (i, 0))
    return pl.pallas_call(
        body, name=name, grid=(t // tm, 3),
        in_specs=[tok, pl.BlockSpec((1, D_MODEL), lambda i, j: (0, 0)),
                  pl.BlockSpec((D_MODEL, tn), lambda i, j: (0, j))],
        out_specs=[pl.BlockSpec((tm, tn), lambda i, j: (i, j)), tok],
        out_shape=[jax.ShapeDtypeStruct((t, PROJ_P), F32), jax.ShapeDtypeStruct((t, D_MODEL), BF16)],
        compiler_params=_ARB2,
    )(x, g, w_in_p)


def _mixer_out_fwd(o_ret, o_gla, w_out, x, name):
    t = x.shape[0]
    tm = min(t, TOKEN_TILE)
    half = HEADS * LANES

    def body(a_ref, b_ref, w_ref, x_ref, o_ref):
        o_ref[...] = x_ref[...] + _dot(a_ref[...], w_ref[0:half, :]) + _dot(b_ref[...], w_ref[half:2 * half, :])

    tok = pl.BlockSpec((tm, D_MODEL), lambda i: (i, 0))
    hb = pl.BlockSpec((tm, half), lambda i: (i, 0))
    return pl.pallas_call(
        body, name=name, grid=(t // tm,),
        in_specs=[hb, hb, pl.BlockSpec((2 * half, D_MODEL), lambda i: (0, 0)), tok],
        out_specs=tok, out_shape=jax.ShapeDtypeStruct((t, D_MODEL), F32),
        compiler_params=_ARB1,
    )(o_ret, o_gla, w_out, x)


def _final_loss(x, g, target, name):
    t = x.shape[0]
    tm = min(t, TOKEN_TILE)

    def body(x_ref, g_ref, t_ref, l_ref, dx_ref, dg_ref):
        @pl.when(pl.program_id(0) == 0)
        def _():
            l_ref[...] = jnp.zeros_like(l_ref)
            dg_ref[...] = jnp.zeros_like(dg_ref)

        xv = x_ref[...]
        gv = g_ref[...]
        err = (xv * _rms_scale(xv)) * gv - t_ref[...]
        l_ref[...] += 0.5 * jnp.sum(jnp.mean(err * err, axis=-1, keepdims=True), axis=0, keepdims=True)
        dx, dg = _rms_bwd(err * (1.0 / D_MODEL), xv, gv)
        dx_ref[...] = dx
        dg_ref[...] += dg

    tok = pl.BlockSpec((tm, D_MODEL), lambda i: (i, 0))
    row = pl.BlockSpec((1, D_MODEL), lambda i: (0, 0))
    return pl.pallas_call(
        body, name=name, grid=(t // tm,),
        in_specs=[tok, row, tok],
        out_specs=[pl.BlockSpec((8, LANES), lambda i: (0, 0)), tok, row],
        out_shape=[jax.ShapeDtypeStruct((8, LANES), F32), jax.ShapeDtypeStruct((t, D_MODEL), F32),
                   jax.ShapeDtypeStruct((1, D_MODEL), F32)],
        compiler_params=_ARB1,
    )(x, g, target)


def _rot(v, cos, sa, sb):
    return v * cos + pltpu.roll(v, 96, 1) * sa + pltpu.roll(v, 32, 1) * sb


def _rot_t(d, cos, sa, sb):
    return d * cos + pltpu.roll(d * sa, 32, 1) + pltpu.roll(d * sb, 96, 1)


def _chunk_inputs(is_ret, rows, proj_ref, aux):
    q_raw = proj_ref[rows, 0:LANES]
    k_raw = proj_ref[rows, LANES:2 * LANES]
    v = proj_ref[rows, 2 * LANES:3 * LANES]
    gate = proj_ref[rows, 3 * LANES:4 * LANES]
    ri = lax.broadcasted_iota(jnp.int32, (CHUNK, CHUNK), 0)
    ci = lax.broadcasted_iota(jnp.int32, (CHUNK, CHUNK), 1)
    if is_ret:
        cos_ref, sa_ref, sb_ref, lg_ref = aux
        cos, sa, sb = cos_ref[rows, :], sa_ref[rows, :], sb_ref[rows, :]
        q = _rot(q_raw, cos, sa, sb)
        k = _rot(k_raw, cos, sa, sb) * QK_SCALE
        steps = (lax.broadcasted_iota(jnp.int32, (CHUNK, LANES), 0) + 1).astype(F32)
        b = steps * lg_ref[...]
        logit = None
    else:
        glow_ref, wa2_ref, ba_ref = aux
        logit = _dot(glow_ref[rows, :].astype(BF16), wa2_ref[...]) + ba_ref[...]
        la = (jnp.minimum(logit, 0.0) - jnp.log1p(jnp.exp(-jnp.abs(logit)))) * (1.0 / GATE_NORM)
        lower = (ci <= ri).astype(F32)
        b = jnp.dot(lower, la, precision=lax.Precision.HIGHEST, preferred_element_type=F32)
        q = q_raw * QK_SCALE
        k = k_raw
    return q, k, v, gate, b, logit, ri, ci


def _chunk_scores(q, k, b, ri, ci):
    mid = b[CHUNK // 2 - 1:CHUNK // 2, :]
    ep = jnp.exp(b - mid)
    en = jnp.exp(mid - b)
    qt, kt, qh, kh = q * ep, k * en, q * en, k * ep
    low = _dot_nt(qt.astype(BF16), kt.astype(BF16))
    upp = _dot_nt(qh.astype(BF16), kh.astype(BF16))
    scores = jnp.where(ci <= ri, low, upp)
    return scores, ep, en, qt, kt, qh, kh


def _attn_specs(is_ret, t, tb, imap_t):
    nb = t // tb
    base = 0 if is_ret else HEADS
    proj = pl.BlockSpec((tb, HEAD_BLOCK), lambda h, i: (imap_t(i), base + h))
    lane_t = pl.BlockSpec((tb, LANES), lambda h, i: (imap_t(i), 0))
    if is_ret:
        aux = [lane_t, lane_t, lane_t, pl.BlockSpec((None, 1, LANES), lambda h, i: (h, 0, 0))]
    else:
        aux = [pl.BlockSpec((tb, LANES), lambda h, i: (imap_t(i), PROJ_P // LANES - 1)),
               pl.BlockSpec((LANES, LANES), lambda h, i: (0, h)),
               pl.BlockSpec((1, LANES), lambda h, i: (0, h))]
    gain = pl.BlockSpec((1, LANES), lambda h, i: (0, h))
    head_t = pl.BlockSpec((tb, LANES), lambda h, i: (imap_t(i), h))
    state = pl.BlockSpec((None, tb // CHUNK, LANES, LANES), lambda h, i: (h, imap_t(i), 0, 0))
    return nb, proj, aux, gain, head_t, state


def _attn_fwd(is_ret, proj, aux_arrays, gain, name):
    t = proj.shape[0]
    tb = min(t, ATTN_TILE)
    n_aux = 4 if is_ret else 3
    nb, proj_spec, aux_specs, gain_spec, head_t, state_spec = _attn_specs(is_ret, t, tb, lambda i: i)

    def body(*refs):
        proj_ref = refs[0]
        aux = refs[1:1 + n_aux]
        gn_ref, ofin_ref, oraw_ref, st_ref, state = refs[1 + n_aux:]

        @pl.when(pl.program_id(1) == 0)
        def _():
            state[...] = jnp.zeros_like(state)

        def chunk(c, carry):
            rows = pl.ds(pl.multiple_of(c * CHUNK, CHUNK), CHUNK)
            q, k, v, gate, b, _, ri, ci = _chunk_inputs(is_ret, rows, proj_ref, aux)
            scores = _chunk_scores(q, k, b, ri, ci)[0]
            st = state[...]
            st_ref[c] = st
            vb = v.astype(BF16)
            out = _dot(scores.astype(BF16), vb) + _dot_nt((q * jnp.exp(b)).astype(BF16), st.astype(BF16))
            b_last = b[CHUNK - 1:CHUNK, :]
            kd = k * jnp.exp(b_last - b)
            state[...] = st * jnp.exp(b_last) + _dot_tn(vb, kd.astype(BF16))
            oraw_ref[rows, :] = out
            normed = out * _rms_scale(out)
            ofin_ref[rows, :] = ((normed * gn_ref[...]) * (gate * jax.nn.sigmoid(gate))).astype(BF16)
            return carry

        lax.fori_loop(0, tb // CHUNK, chunk, 0)

    width = HEADS * LANES
    return pl.pallas_call(
        body, name=name, grid=(HEADS, nb),
        in_specs=[proj_spec] + aux_specs + [gain_spec],
        out_specs=[head_t, head_t, state_spec],
        out_shape=[jax.ShapeDtypeStruct((t, width), BF16), jax.ShapeDtypeStruct((t, width), F32),
                   jax.ShapeDtypeStruct((HEADS, t // CHUNK, LANES, LANES), F32)],
        scratch_shapes=[pltpu.VMEM((LANES, LANES), F32)],
        compiler_params=_ARB2,
    )(proj, *aux_arrays, gain)


def _attn_bwd(is_ret, proj, aux_arrays, gain, o_raw, states, d_out, name):
    t = proj.shape[0]
    tb = min(t, ATTN_TILE)
    nc = tb // CHUNK
    n_aux = 4 if is_ret else 3
    nblk = t // tb
    nb, proj_spec, aux_specs, gain_spec, head_t, state_spec = _attn_specs(is_ret, t, tb, lambda i: nblk - 1 - i)
    base = 0 if is_ret else HEADS
    dout_spec = pl.BlockSpec((tb, LANES), lambda h, i: (nblk - 1 - i, base + h))

    def body(*refs):
        proj_ref = refs[0]
        aux = refs[1:1 + n_aux]
        gn_ref, oraw_ref, st_ref, dfin_ref = refs[1 + n_aux:5 + n_aux]
        if is_ret:
            dproj_ref, dgn_ref, dstate = refs[5 + n_aux:]
        else:
            dproj_ref, dgn_ref, dlogit_ref, dba_ref, dstate = refs[5 + n_aux:]

        @pl.when(pl.program_id(1) == 0)
        def _():
            dstate[...] = jnp.zeros_like(dstate)
            dgn_ref[...] = jnp.zeros_like(dgn_ref)
            if not is_ret:
                dba_ref[...] = jnp.zeros_like(dba_ref)

        def chunk(step, carry):
            c = nc - 1 - step
            rows = pl.ds(pl.multiple_of(c * CHUNK, CHUNK), CHUNK)
            q, k, v, gate, b, logit, ri, ci = _chunk_inputs(is_ret, rows, proj_ref, aux)
            scores, ep, en, qt, kt, qh, kh = _chunk_scores(q, k, b, ri, ci)
            st = st_ref[c]
            dst = dstate[...]
            eb = jnp.exp(b)
            qe = q * eb
            b_last = b[CHUNK - 1:CHUNK, :]
            e_last = jnp.exp(b_last)
            ekd = jnp.exp(b_last - b)
            kd = k * ekd

            gn = gn_ref[...]
            out = oraw_ref[rows, :]
            r = _rms_scale(out)
            normed = out * r
            sg = jax.nn.sigmoid(gate)
            dfin = dfin_ref[rows, :]
            dgate = dfin * (normed * gn) * _silu_grad(gate, sg)
            dpre = dfin * (gate * sg)
            dgn_ref[...] += jnp.sum(dpre * normed, axis=0, keepdims=True)
            dnormed = dpre * gn
            d_o = r * (dnormed - normed * jnp.mean(dnormed * normed, axis=-1, keepdims=True))

            dob, vb = d_o.astype(BF16), v.astype(BF16)
            stb, dstb = st.astype(BF16), dst.astype(BF16)
            dv = _dot_tn(scores.astype(BF16), dob) + _dot_nt(kd.astype(BF16), dstb)
            dsc = _dot_nt(dob, vb)
            dqe = _dot(dob, stb)
            dkd = _dot(vb, dstb)
            dlow = jnp.where(ci <= ri, dsc, 0.0).astype(BF16)
            dupp = jnp.where(ci <= ri, 0.0, dsc).astype(BF16)
            dqt = _dot(dlow, kt.astype(BF16))
            dkt = _dot_tn(dlow, qt.astype(BF16))
            dqh = _dot(dupp, kh.astype(BF16))
            dkh = _dot_tn(dupp, qh.astype(BF16))
            dq = dqt * ep + dqh * en + dqe * eb
            dk = dkt * en + dkh * ep + dkd * ekd
            dstate[...] = dst * e_last + _dot_tn(dob, qe.astype(BF16))

            if is_ret:
                cos_ref, sa_ref, sb_ref, _ = aux
                cos, sa, sb = cos_ref[rows, :], sa_ref[rows, :], sb_ref[rows, :]
                dq_raw = _rot_t(dq, cos, sa, sb)
                dk_raw = _rot_t(dk, cos, sa, sb) * QK_SCALE
            else:
                dq_raw = dq * QK_SCALE
                dk_raw = dk
                db = dqt * qt - dkt * kt - dqh * qh + dkh * kh + dqe * qe - dkd * kd
                db_last = (jnp.sum(dkd * kd, axis=0, keepdims=True)
                           + jnp.sum(dst * st, axis=0, keepdims=True) * e_last)
                last_row = lax.broadcasted_iota(jnp.int32, (CHUNK, LANES), 0) == CHUNK - 1
                db = db + jnp.where(last_row, db_last, 0.0)
                upper = (ci >= ri).astype(F32)
                dla = jnp.dot(upper, db, precision=lax.Precision.HIGHEST, preferred_element_type=F32)
                dlogit = dla * (1.0 / GATE_NORM) * jax.nn.sigmoid(-logit)
                dlogit_ref[rows, :] = dlogit.astype(BF16)
                dba_ref[...] += jnp.sum(dlogit, axis=0, keepdims=True)

            dproj_ref[rows, 0:LANES] = dq_raw.astype(BF16)
            dproj_ref[rows, LANES:2 * LANES] = dk_raw.astype(BF16)
            dproj_ref[rows, 2 * LANES:3 * LANES] = dv.astype(BF16)
            dproj_ref[rows, 3 * LANES:4 * LANES] = dgate.astype(BF16)
            return carry

        lax.fori_loop(0, nc, chunk, 0)

    width = HEADS * LANES
    row_out = pl.BlockSpec((1, LANES), lambda h, i: (0, h))
    out_specs = [pl.BlockSpec((tb, HEAD_BLOCK), lambda h, i: (nblk - 1 - i, h)), row_out]
    out_shape = [jax.ShapeDtypeStruct((t, HEADS * HEAD_BLOCK), BF16), jax.ShapeDtypeStruct((1, width), F32)]
    if not is_ret:
        out_specs += [head_t, row_out]
        out_shape += [jax.ShapeDtypeStruct((t, width), BF16), jax.ShapeDtypeStruct((1, width), F32)]
    return pl.pallas_call(
        body, name=name, grid=(HEADS, nblk),
        in_specs=[proj_spec] + aux_specs + [gain_spec, head_t, state_spec, dout_spec],
        out_specs=out_specs, out_shape=out_shape,
        scratch_shapes=[pltpu.VMEM((LANES, LANES), F32)],
        compiler_params=_ARB2,
    )(proj, *aux_arrays, gain, o_raw, states, d_out)


def _place():
    x, y, c = lax.axis_index("x"), lax.axis_index("y"), lax.axis_index("c")
    chips = [(1 - x, y), (x, 1 - y), (1 - x, 1 - y)]
    return x, y, c, 2 * x + y, chips


def _gather_weights(arrs):
    na = len(arrs)

    def body(*refs):
        ins, outs = refs[:na], refs[na:2 * na]
        send_sems, recv_sems, local_sems = refs[2 * na:]
        x, y, c, me, chips = _place()
        sibling = (x, y, 1 - c)

        def ici(a, j, src_chip, to):
            return pltpu.make_async_remote_copy(
                src_ref=ins[a].at[:, c], dst_ref=outs[a].at[src_chip, :, c],
                send_sem=send_sems.at[6 * a + j], recv_sem=recv_sems.at[6 * a + j], device_id=to, device_id_type=MESH)

        def d2d(a, j, src_chip, half):
            blk = outs[a].at[src_chip, :, half]
            return pltpu.make_async_remote_copy(
                src_ref=blk, dst_ref=blk, send_sem=send_sems.at[6 * a + 3 + j], recv_sem=recv_sems.at[6 * a + 3 + j],
                device_id=sibling, device_id_type=MESH)

        local = [pltpu.make_async_copy(ins[a], outs[a].at[me], local_sems.at[a]) for a in range(na)]
        for cp in local:
            cp.start()
        sends = [ici(a, j, me, (px, py, c)) for a in range(na) for j, (px, py) in enumerate(chips)]
        for cp in sends:
            cp.start()
        passed = []
        for a in range(na):
            for j, (px, py) in enumerate(chips):
                ici(a, j, 2 * px + py, (px, py, c)).wait_recv()
                cp = d2d(a, j, 2 * px + py, c)
                cp.start()
                passed.append(cp)
        for a in range(na):
            for j, (px, py) in enumerate(chips):
                d2d(a, j, 2 * px + py, 1 - c).wait_recv()
        for cp in sends + passed:
            cp.wait_send()
        for cp in local:
            cp.wait()

    return pl.pallas_call(
        body, name="gather_weights",
        in_specs=[ANY] * na, out_specs=[ANY] * na,
        out_shape=[jax.ShapeDtypeStruct((N_CHIPS,) + a.shape, a.dtype) for a in arrs],
        scratch_shapes=[pltpu.SemaphoreType.DMA((6 * na,)), pltpu.SemaphoreType.DMA((6 * na,)),
                        pltpu.SemaphoreType.DMA((na,))],
    )(*arrs)


def _pair_exchange(grads):
    na = len(grads)

    def body(*refs):
        ins, outs = refs[:na], refs[na:2 * na]
        send_sems, recv_sems = refs[2 * na:]
        x, y, c, _, _ = _place()
        copies = [pltpu.make_async_remote_copy(
            src_ref=ins[a].at[:, :, 1 - c], dst_ref=outs[a], send_sem=send_sems.at[a], recv_sem=recv_sems.at[a],
            device_id=(x, y, 1 - c), device_id_type=MESH) for a in range(na)]
        for cp in copies:
            cp.start()
        for cp in copies:
            cp.wait()

    return pl.pallas_call(
        body, name="pair_exchange",
        in_specs=[ANY] * na, out_specs=[ANY] * na,
        out_shape=[jax.ShapeDtypeStruct(g.shape[:2] + g.shape[3:], g.dtype) for g in grads],
        scratch_shapes=[pltpu.SemaphoreType.DMA((na,)), pltpu.SemaphoreType.DMA((na,))],
    )(*grads)


def _pair_add(grad, recv, c_arr, name):
    _, n, _, r, cols = grad.shape

    def body(c_ref, g_ref, r_ref, o_ref):
        o_ref[...] = (g_ref[...] + r_ref[...]).astype(BF16)

    return pl.pallas_call(
        body, name=name,
        grid_spec=pltpu.PrefetchScalarGridSpec(
            num_scalar_prefetch=1, grid=(N_CHIPS, n),
            in_specs=[pl.BlockSpec((None, None, None, r, cols), lambda p, k, c_ref: (p, k, c_ref[0], 0, 0)),
                      pl.BlockSpec((None, None, r, cols), lambda p, k, c_ref: (p, k, 0, 0))],
            out_specs=pl.BlockSpec((None, None, r, cols), lambda p, k, c_ref: (p, k, 0, 0))),
        out_shape=jax.ShapeDtypeStruct((N_CHIPS, n, r, cols), BF16),
        compiler_params=_ARB2,
    )(c_arr, grad, recv)


def _chip_exchange(sums):
    na = len(sums)

    def body(*refs):
        ins, outs = refs[:na], refs[na:2 * na]
        send_sems, recv_sems, local_sems = refs[2 * na:]
        x, y, c, me, chips = _place()
        local = [pltpu.make_async_copy(ins[a].at[me], outs[a].at[me], local_sems.at[a]) for a in range(na)]
        for cp in local:
            cp.start()
        sends = [pltpu.make_async_remote_copy(
            src_ref=ins[a].at[2 * px + py], dst_ref=outs[a].at[me],
            send_sem=send_sems.at[3 * a + j], recv_sem=recv_sems.at[3 * a + j],
            device_id=(px, py, c), device_id_type=MESH) for a in range(na) for j, (px, py) in enumerate(chips)]
        for cp in sends:
            cp.start()
        for a in range(na):
            for j, (px, py) in enumerate(chips):
                pltpu.make_async_remote_copy(
                    src_ref=ins[a].at[me], dst_ref=outs[a].at[2 * px + py],
                    send_sem=send_sems.at[3 * a + j], recv_sem=recv_sems.at[3 * a + j],
                    device_id=(px, py, c), device_id_type=MESH).wait_recv()
        for cp in sends:
            cp.wait_send()
        for cp in local:
            cp.wait()

    return pl.pallas_call(
        body, name="chip_exchange",
        in_specs=[ANY] * na, out_specs=[ANY] * na,
        out_shape=[jax.ShapeDtypeStruct(s.shape, s.dtype) for s in sums],
        scratch_shapes=[pltpu.SemaphoreType.DMA((3 * na,)), pltpu.SemaphoreType.DMA((3 * na,)),
                        pltpu.SemaphoreType.DMA((na,))],
    )(*sums)


def _chip_sum(recv, name):
    _, n, r, cols = recv.shape

    def body(r_ref, o_ref):
        acc = r_ref[0].astype(F32) + r_ref[1].astype(F32)
        acc = acc + r_ref[2].astype(F32)
        o_ref[...] = acc + r_ref[3].astype(F32)

    return pl.pallas_call(
        body, name=name, grid=(n,),
        in_specs=[pl.BlockSpec((N_CHIPS, None, r, cols), lambda k: (0, k, 0, 0))],
        out_specs=pl.BlockSpec((None, r, cols), lambda k: (k, 0, 0)),
        out_shape=jax.ShapeDtypeStruct((n, r, cols), F32),
        compiler_params=_ARB1,
    )(recv)


def _pair_share(halves):
    na = len(halves)

    def body(*refs):
        ins, outs = refs[:na], refs[na:2 * na]
        send_sems, recv_sems, local_sems = refs[2 * na:]
        x, y, c, _, _ = _place()
        local = [pltpu.make_async_copy(ins[a], outs[a].at[:, c], local_sems.at[a]) for a in range(na)]
        for cp in local:
            cp.start()
        sends = [pltpu.make_async_remote_copy(
            src_ref=ins[a], dst_ref=outs[a].at[:, c], send_sem=send_sems.at[a], recv_sem=recv_sems.at[a],
            device_id=(x, y, 1 - c), device_id_type=MESH) for a in range(na)]
        for cp in sends:
            cp.start()
        for a in range(na):
            pltpu.make_async_remote_copy(
                src_ref=ins[a], dst_ref=outs[a].at[:, 1 - c], send_sem=send_sems.at[a], recv_sem=recv_sems.at[a],
                device_id=(x, y, 1 - c), device_id_type=MESH).wait_recv()
        for cp in sends:
            cp.wait_send()
        for cp in local:
            cp.wait()

    return pl.pallas_call(
        body, name="pair_share",
        in_specs=[ANY] * na, out_specs=[ANY] * na,
        out_shape=[jax.ShapeDtypeStruct((h.shape[0], 2) + h.shape[1:], h.dtype) for h in halves],
        scratch_shapes=[pltpu.SemaphoreType.DMA((na,)), pltpu.SemaphoreType.DMA((na,)),
                        pltpu.SemaphoreType.DMA((na,))],
    )(*halves)


def _small_allreduce(block):
    m, n = block.shape

    def body(x_ref, all_ref, sum_ref, send_sems, recv_sems, local_sem):
        x, y, c, _, chips = _place()
        me, sibling = (x, y, c), (x, y, 1 - c)

        def rows(px, py, pc):
            return all_ref.at[pl.ds((4 * px + 2 * py + pc) * m, m), :]

        def copy(k, blk, to, src=None):
            return pltpu.make_async_remote_copy(
                src_ref=rows(*blk) if src is None else src, dst_ref=rows(*blk),
                send_sem=send_sems.at[k], recv_sem=recv_sems.at[k], device_id=to, device_id_type=MESH)

        mine = pltpu.make_async_copy(x_ref, rows(*me), local_sem)
        mine.start()
        first = [copy(0, me, sibling, src=x_ref)]
        first += [copy(1 + j, me, (*chip, c), src=x_ref) for j, chip in enumerate(chips)]
        for cp in first:
            cp.start()
        passed = [copy(4 + j, (*chip, c), sibling) for j, chip in enumerate(chips)]
        for j, chip in enumerate(chips):
            copy(1 + j, (*chip, c), me).wait_recv()
            passed[j].start()
        copy(0, sibling, me).wait_recv()
        for j, chip in enumerate(chips):
            copy(4 + j, (*chip, 1 - c), me).wait_recv()
        for cp in first + passed:
            cp.wait_send()
        mine.wait()
        acc = all_ref[0:m, :]
        for d in range(1, 8):
            acc = acc + all_ref[d * m:(d + 1) * m, :]
        sum_ref[...] = acc

    vmem = pl.BlockSpec(memory_space=pltpu.VMEM)
    return pl.pallas_call(
        body, name="small_allreduce",
        in_specs=[vmem], out_specs=[vmem, vmem],
        out_shape=[jax.ShapeDtypeStruct((8 * m, n), F32), jax.ShapeDtypeStruct((m, n), F32)],
        scratch_shapes=[pltpu.SemaphoreType.DMA((7,)), pltpu.SemaphoreType.DMA((7,)), pltpu.SemaphoreType.DMA],
    )(block)[1]


def _row_tile(rows):
    best = rows
    for cand in range(8, min(rows, 512) + 1, 8):
        if rows % cand == 0:
            best = cand
    return best


def _adamw(w, g, m, v, name):
    rows, cols = w.shape
    tr = _row_tile(rows)

    def body(w_ref, g_ref, m_ref, v_ref, d_ref, nm_ref, nv_ref):
        gv = g_ref[...]
        m2 = ADAM_B1 * m_ref[...] + (1.0 - ADAM_B1) * gv
        v2 = ADAM_B2 * v_ref[...] + (1.0 - ADAM_B2) * (gv * gv)
        m_hat = m2 / (1.0 - ADAM_B1 ** ADAM_STEP)
        v_hat = v2 / (1.0 - ADAM_B2 ** ADAM_STEP)
        d_ref[...] = -ADAM_LR * (m_hat / (jnp.sqrt(v_hat) + ADAM_EPS) + ADAM_WD * w_ref[...])
        nm_ref[...] = m2
        nv_ref[...] = v2

    spec = pl.BlockSpec((tr, cols), lambda i: (i, 0))
    shape = jax.ShapeDtypeStruct((rows, cols), F32)
    return pl.pallas_call(
        body, name=name, grid=(rows // tr,),
        in_specs=[spec] * 4, out_specs=[spec] * 3, out_shape=[shape] * 3,
        compiler_params=_ARB1,
    )(w, g, m, v)


def _in_columns():
    pieces = []
    for group in range(2):
        q0, k0, v0, g0 = (0, 256, 512, 1024) if group == 0 else (1536, 1792, 2048, 2560)
        for h in range(HEADS):
            pieces += [(q0 + 64 * h, 64), (k0 + 64 * h, 64), (v0 + 128 * h, 128), (g0 + 128 * h, 128)]
    pieces.append((3072, GATE_RANK))
    return pieces


def _pad_w_in(w_in):
    parts = []
    for start, width in _in_columns():
        parts.append(w_in[:, start:start + width])
        if width < LANES:
            parts.append(jnp.zeros((w_in.shape[0], LANES - width), w_in.dtype))
    return jnp.concatenate(parts, axis=1)


def _unpad_w_in(w_p):
    cols = {}
    offset = 0
    for start, width in _in_columns():
        cols[start] = w_p[:, offset:offset + width]
        offset += LANES
    return jnp.concatenate([cols[s] for s in sorted(cols)], axis=1)


def _rope_tables(t):
    half = 32
    inv = ROPE_BASE ** (-jnp.arange(half, dtype=F32) * 2.0 / 64)
    ang = jnp.arange(t, dtype=F32)[:, None] * inv[None, :]
    cos, sin = jnp.cos(ang), jnp.sin(ang)
    z32, z64 = jnp.zeros((t, 32), F32), jnp.zeros((t, 64), F32)
    return (jnp.concatenate([cos, cos, z64], axis=1),
            jnp.concatenate([-sin, z32, z64], axis=1),
            jnp.concatenate([z32, sin, z64], axis=1))


def _halves(w):
    n, rows, cols = w.shape
    return w.reshape(n, 2, rows // 2, cols)


def _pack_small(n1, nm, n2, nf, nret, ngla, ba, wa2, wa2_cols, extra=None):
    z = lambda k: jnp.zeros((1, k), F32)
    rows = [n1.reshape(1, -1), nm.reshape(1, -1), n2.reshape(1, -1), nf.reshape(1, -1),
            jnp.concatenate([nret.reshape(1, -1), ngla.reshape(1, -1)], axis=1),
            jnp.concatenate([ba.reshape(1, -1), z(D_MODEL - 256)], axis=1),
            jnp.zeros((1, D_MODEL), F32) if extra is None else extra,
            jnp.zeros((1, D_MODEL), F32),
            jnp.concatenate([wa2.reshape(GATE_RANK, wa2_cols), jnp.zeros((GATE_RANK, D_MODEL - wa2_cols), F32)], axis=1),
            jnp.zeros((SMALL_ROWS - 8 - GATE_RANK, D_MODEL), F32)]
    return jnp.concatenate(rows, axis=0)


def _unpack_small(p, wa2_cols):
    return (p[0:1], p[1:2], p[2:3], p[3], p[4:5, 0:512], p[4:5, 512:1024], p[5:6, 0:256],
            p[8:8 + GATE_RANK, 0:wa2_cols].reshape(1, GATE_RANK, wa2_cols))


def _forward_backward(xs, target, gu, dn, w_in_p, w_out_full, wa2_p, ba_p, ffn1_norm_g, mix_norm_g, ret_norm_g,
                      gla_norm_g, ffn2_norm_g, final_norm_g):
    t = xs.shape[0]
    cos_t, sa_t, sb_t = _rope_tables(t)
    log_gamma = jnp.log(1.0 - 2.0 ** (-5.0 - jnp.arange(HEADS, dtype=F32)))
    lg_t = jnp.broadcast_to(log_gamma[:, None, None], (HEADS, 1, LANES))
    ret_aux = [cos_t, sa_t, sb_t, lg_t]

    x1, a1, u1, h1 = _ffn_fwd(xs, ffn1_norm_g, gu, dn, 0, "ffn1_fwd")
    proj, h_mix = _mixer_in_fwd(x1, mix_norm_g, w_in_p, "mixer_in_fwd")
    gla_aux = [proj, wa2_p, ba_p]
    o_ret, raw_ret, st_ret = _attn_fwd(True, proj, ret_aux, ret_norm_g, "ret_fwd")
    o_gla, raw_gla, st_gla = _attn_fwd(False, proj, gla_aux, gla_norm_g, "gla_fwd")
    x2 = _mixer_out_fwd(o_ret, o_gla, w_out_full, x1, "mixer_out_fwd")
    x3, a2, u2, h2 = _ffn_fwd(x2, ffn2_norm_g, gu, dn, 1, "ffn2_fwd")
    loss_blk, dx3, d_final_g = _final_loss(x3, final_norm_g, target, "final_loss")

    da2, du2, hid2, dob2, dx2, d_ffn2_g = _ffn_bwd(dx3, x2, ffn2_norm_g, a2, u2, gu, dn, 1, "ffn2_bwd")
    g_gate2 = _matmul_tn(h2, da2, "ffn2_dgate")
    g_up2 = _matmul_tn(h2, du2, "ffn2_dup")
    g_down2 = _matmul_tn(hid2, dob2, "ffn2_ddown")

    d_o = _matmul_nt(dx2, w_out_full, "mixer_out_bwd")
    g_wout_ret = _matmul_tn(o_ret, dx2, "wout_grad_ret")
    g_wout_gla = _matmul_tn(o_gla, dx2, "wout_grad_gla")
    dproj_ret, d_ret_g = _attn_bwd(True, proj, ret_aux, ret_norm_g, raw_ret, st_ret, d_o, "ret_bwd")
    dproj_gla, d_gla_g, dlogit, d_ba_p = _attn_bwd(False, proj, gla_aux, gla_norm_g, raw_gla, st_gla, d_o, "gla_bwd")
    d_glow = _matmul_nt(dlogit, wa2_p, "gate_low_bwd", out_dtype=BF16)
    g_wa2_p = _matmul_tn(proj[:, PROJ_P - LANES:], dlogit, "gate_w_grad")
    dproj = jnp.concatenate([dproj_ret, dproj_gla, d_glow], axis=1)
    g_win_p = _matmul_tn(h_mix, dproj, "w_in_grad", tn=PROJ_P // 3)
    dx1, d_mix_g = _mixer_in_bwd(dproj, w_in_p, dx2, x1, mix_norm_g, "mixer_in_bwd")

    da1, du1, hid1, dob1, grad_x, d_ffn1_g = _ffn_bwd(dx1, xs, ffn1_norm_g, a1, u1, gu, dn, 0, "ffn1_bwd")
    g_gate1 = _matmul_tn(h1, da1, "ffn1_dgate")
    g_up1 = _matmul_tn(h1, du1, "ffn1_dup")
    g_down1 = _matmul_tn(hid1, dob1, "ffn1_ddown")

    return (loss_blk, grad_x, g_gate1, g_up1, g_down1, g_gate2, g_up2, g_down2, g_win_p, g_wout_ret, g_wout_gla, g_wa2_p,
            d_ba_p, d_ffn1_g, d_mix_g, d_ffn2_g, d_final_g, d_ret_g, d_gla_g)


def kernel(x, ffn1_norm_g, ffn1_w_gate, ffn1_w_up, ffn1_w_down, mix_norm_g, w_in, ret_norm_g, gla_w_a2, gla_b_a, gla_norm_g, w_out, ffn2_norm_g, ffn2_w_gate, ffn2_w_up, ffn2_w_down, final_norm_g, loss_target, m_ffn1_norm_g, m_ffn1_w_gate, m_ffn1_w_up, m_ffn1_w_down, m_mix_norm_g, m_w_in, m_ret_norm_g, m_gla_w_a2, m_gla_b_a, m_gla_norm_g, m_w_out, m_ffn2_norm_g, m_ffn2_w_gate, m_ffn2_w_up, m_ffn2_w_down, m_final_norm_g, v_ffn1_norm_g, v_ffn1_w_gate, v_ffn1_w_up, v_ffn1_w_down, v_mix_norm_g, v_w_in, v_ret_norm_g, v_gla_w_a2, v_gla_b_a, v_gla_norm_g, v_w_out, v_ffn2_norm_g, v_ffn2_w_gate, v_ffn2_w_up, v_ffn2_w_down, v_final_norm_g):
    t = x.shape[1]
    xs = x.reshape(t, D_MODEL)
    target = loss_target.reshape(t, D_MODEL)
    chip = 2 * lax.axis_index("x") + lax.axis_index("y")
    c_arr = lax.axis_index("c").astype(jnp.int32).reshape(1)

    gu_shard = jnp.concatenate([ffn1_w_gate, ffn1_w_up, ffn2_w_gate, ffn2_w_up], axis=0).astype(BF16)
    dn_shard = jnp.concatenate([ffn1_w_down, ffn2_w_down], axis=0).astype(BF16)
    wa2_shard = jnp.concatenate([gla_w_a2.reshape(GATE_RANK, 64), jnp.zeros((GATE_RANK, 64), F32)], axis=1)
    gu_all, dn_all, win_all, wout_all, wa2_all = _gather_weights(
        [_halves(gu_shard), _halves(dn_shard), _halves(w_in.astype(BF16)), _halves(w_out.astype(BF16)),
         wa2_shard.reshape(1, 2, 8, LANES)])
    gu = gu_all.reshape(N_CHIPS, 4, D_MODEL, FF_SHARD)
    dn = dn_all.reshape(N_CHIPS, 2, FF_SHARD, D_MODEL)
    w_in_full = jnp.concatenate([win_all.reshape(N_CHIPS, D_MODEL, IN_SHARD)[p] for p in range(N_CHIPS)], axis=1)
    w_in_p = _pad_w_in(w_in_full)
    w_out_full = wout_all.reshape(D_MODEL, D_MODEL)
    wa2_p = jnp.pad(wa2_all.reshape(N_CHIPS, GATE_RANK, LANES).transpose(1, 0, 2).reshape(GATE_RANK, HEADS * LANES),
                    ((0, LANES - GATE_RANK), (0, 0))).astype(BF16)
    ba_p = jnp.pad(gla_b_a.reshape(HEADS, 64), ((0, 0), (0, 64))).reshape(1, HEADS * LANES)
    fb = _forward_backward(xs, target, gu, dn, w_in_p, w_out_full, wa2_p, ba_p, ffn1_norm_g, mix_norm_g, ret_norm_g,
                           gla_norm_g, ffn2_norm_g, final_norm_g.reshape(1, D_MODEL))
    (loss_blk, grad_x, g_gate1, g_up1, g_down1, g_gate2, g_up2, g_down2, g_win_p, g_wout_ret, g_wout_gla, g_wa2_p,
     d_ba_p, d_ffn1_g, d_mix_g, d_ffn2_g, d_final_g, d_ret_g, d_gla_g) = fb

    def fam(parts):
        stacked = jnp.stack(parts, axis=1)
        p, n, rows, cols = stacked.shape
        return stacked.reshape(p, n, 2, rows // 2, cols)

    g_win = _unpad_w_in(g_win_p[0])
    fams = [fam([g_gate1, g_up1, g_gate2, g_up2]), fam([g_down1, g_down2]),
            fam([jnp.stack([g_win[:, IN_SHARD * p:IN_SHARD * (p + 1)] for p in range(N_CHIPS)], axis=0)]),
            fam([jnp.concatenate([g_wout_ret[0], g_wout_gla[0]], axis=0).reshape(N_CHIPS, D_MODEL // N_CHIPS, D_MODEL)])]
    recv = _pair_exchange(fams)
    sums = [_pair_add(g, r, c_arr, "pair_add_%d" % k) for k, (g, r) in enumerate(zip(fams, recv))]
    gathered = _chip_exchange(sums)
    halves = [_chip_sum(r, "chip_sum_%d" % k) for k, r in enumerate(gathered)]
    shards = _pair_share(halves)
    gu_grad = shards[0].reshape(4, D_MODEL, FF_SHARD)
    dn_grad = shards[1].reshape(2, FF_SHARD, D_MODEL)
    win_grad = shards[2].reshape(D_MODEL, IN_SHARD)
    wout_grad = shards[3].reshape(D_MODEL // N_CHIPS, D_MODEL)

    g_wa2 = g_wa2_p[0][0:GATE_RANK].reshape(GATE_RANK, HEADS, LANES)[:, :, 0:64].reshape(GATE_RANK, 256)
    d_ba = d_ba_p.reshape(HEADS, LANES)[:, 0:64].reshape(1, 256)
    loss_row = jnp.pad(loss_blk[0:1, 0:1], ((0, 0), (0, D_MODEL - 1)))
    small_local = _pack_small(d_ffn1_g, d_mix_g, d_ffn2_g, d_final_g, d_ret_g, d_gla_g, d_ba, g_wa2, 256, loss_row)
    small_sum = _small_allreduce(small_local)
    loss = small_sum[6, 0]
    sg = _unpack_small(small_sum, 256)
    wa2_grad = lax.dynamic_slice(sg[7], (0, 0, 64 * chip), (1, GATE_RANK, 64))
    small_g = _pack_small(*sg[:7], wa2_grad, 64)
    small_w = _pack_small(ffn1_norm_g, mix_norm_g, ffn2_norm_g, final_norm_g, ret_norm_g, gla_norm_g, gla_b_a, gla_w_a2, 64)
    small_m = _pack_small(m_ffn1_norm_g, m_mix_norm_g, m_ffn2_norm_g, m_final_norm_g, m_ret_norm_g, m_gla_norm_g,
                          m_gla_b_a, m_gla_w_a2, 64)
    small_v = _pack_small(v_ffn1_norm_g, v_mix_norm_g, v_ffn2_norm_g, v_final_norm_g, v_ret_norm_g, v_gla_norm_g,
                          v_gla_b_a, v_gla_w_a2, 64)
    small_out = _adamw(small_w, small_g, small_m, small_v, "adamw_small")
    s_grad = _unpack_small(small_g, 64)
    s_delta, s_m, s_v = (_unpack_small(o, 64) for o in small_out)

    def big(w, g, m, v, name):
        d, nm, nv = _adamw(w.reshape(g.shape), g, m.reshape(g.shape), v.reshape(g.shape), name)
        return [z.reshape(w.shape) for z in (g, d, nm, nv)]

    r_g1 = big(ffn1_w_gate, gu_grad[0], m_ffn1_w_gate, v_ffn1_w_gate, "adamw_ffn1_gate")
    r_u1 = big(ffn1_w_up, gu_grad[1], m_ffn1_w_up, v_ffn1_w_up, "adamw_ffn1_up")
    r_d1 = big(ffn1_w_down, dn_grad[0], m_ffn1_w_down, v_ffn1_w_down, "adamw_ffn1_down")
    r_in = big(w_in, win_grad, m_w_in, v_w_in, "adamw_w_in")
    r_out = big(w_out, wout_grad, m_w_out, v_w_out, "adamw_w_out")
    r_g2 = big(ffn2_w_gate, gu_grad[2], m_ffn2_w_gate, v_ffn2_w_gate, "adamw_ffn2_gate")
    r_u2 = big(ffn2_w_up, gu_grad[3], m_ffn2_w_up, v_ffn2_w_up, "adamw_ffn2_up")
    r_d2 = big(ffn2_w_down, dn_grad[1], m_ffn2_w_down, v_ffn2_w_down, "adamw_ffn2_down")

    def leaves(k, smalls):
        n1, nm, n2, nf, nret, ngla, ba, wa2 = smalls
        return [n1, r_g1[k], r_u1[k], r_d1[k], nm, r_in[k], nret, wa2, ba, ngla, r_out[k], n2, r_g2[k], r_u2[k], r_d2[k], nf]

    outs = [loss, grad_x.reshape(x.shape)]
    outs += leaves(0, s_grad) + leaves(1, s_delta) + leaves(2, s_m) + leaves(3, s_v)
    return tuple(outs)
```

```python
import functools

import jax
import jax.numpy as jnp
from jax import lax
from jax.experimental import pallas as pl
from jax.experimental.pallas import tpu as pltpu

F32, BF16 = jnp.float32, jnp.bfloat16
MESH = pl.DeviceIdType.MESH
ANY = pl.BlockSpec(memory_space=pl.ANY)

D_MODEL = 1024
D_FF = 2816
N_CHIPS = 4
FF_SHARD = D_FF // N_CHIPS
IN_WIDTH = 3088
IN_SHARD = IN_WIDTH // N_CHIPS
IN_ROWS = 800
CHUNK = 64
HEADS = 4
LANES = 128
HEAD_BLOCK = 4 * LANES
PROJ_P = 2 * HEADS * HEAD_BLOCK + LANES
GATE_RANK = 16
QK_SCALE = 0.125
GATE_NORM = 16.0
RMS_EPS = 1e-6
ROPE_BASE = 10000.0
ADAM_LR, ADAM_B1, ADAM_B2, ADAM_EPS, ADAM_WD, ADAM_STEP = 0.001, 0.9, 0.999, 1e-08, 0.01, 10
SMALL_ROWS = 32
TOKEN_TILE = 512
ATTN_TILE = 512

_ARB2 = pltpu.CompilerParams(dimension_semantics=("arbitrary", "arbitrary"))
_ARB1 = pltpu.CompilerParams(dimension_semantics=("arbitrary",))
_ARB3 = pltpu.CompilerParams(dimension_semantics=("arbitrary", "arbitrary", "arbitrary"))


def _dot(a, b):
    return jnp.dot(a, b, preferred_element_type=F32)


def _dot_nt(a, b):
    return lax.dot_general(a, b, (((1,), (1,)), ((), ())), preferred_element_type=F32)


def _dot_tn(a, b):
    return lax.dot_general(a, b, (((0,), (0,)), ((), ())), preferred_element_type=F32)


def _rms_scale(xv):
    return lax.rsqrt(jnp.mean(xv * xv, axis=-1, keepdims=True) + RMS_EPS)


def _rms_bwd(dh, xv, g):
    r = _rms_scale(xv)
    xhat = xv * r
    dxhat = dh * g
    dx = r * (dxhat - xhat * jnp.mean(dxhat * xhat, axis=-1, keepdims=True))
    return dx, jnp.sum(dh * xhat, axis=0, keepdims=True)


def _silu_grad(a, sg):
    return sg * (1.0 + a * (1.0 - sg))


def _ffn_weight_specs(which):
    blk = (None, None, FF_SHARD, D_MODEL)
    return [pl.BlockSpec(blk, lambda i, j, k=3 * which + kind: (j, k, 0, 0)) for kind in range(3)]


def _ffn_fwd(x, g, ffn_w, which, name):
    t = x.shape[0]
    tm = min(t, TOKEN_TILE)

    def body(x_ref, g_ref, wg_ref, wu_ref, wd_ref, xo_ref, a_ref, u_ref, h_ref, acc_ref):
        j = pl.program_id(1)

        @pl.when(j == 0)
        def _():
            xv = x_ref[...]
            h_ref[...] = ((xv * _rms_scale(xv)) * g_ref[...]).astype(BF16)
            acc_ref[...] = jnp.zeros_like(acc_ref)

        h = h_ref[...]
        a = _dot_nt(h, wg_ref[...])
        u = _dot_nt(h, wu_ref[...])
        a_ref[...] = a.astype(BF16)
        u_ref[...] = u.astype(BF16)
        hid = (a * jax.nn.sigmoid(a)) * u
        acc_ref[...] += _dot(hid.astype(BF16), wd_ref[...])

        @pl.when(j == N_CHIPS - 1)
        def _():
            xo_ref[...] = x_ref[...] + 0.5 * acc_ref[...]

    tok = pl.BlockSpec((tm, D_MODEL), lambda i, j: (i, 0))
    act = pl.BlockSpec((None, tm, FF_SHARD), lambda i, j: (j, i, 0))
    return pl.pallas_call(
        body, name=name, grid=(t // tm, N_CHIPS),
        in_specs=[tok, pl.BlockSpec((1, D_MODEL), lambda i, j: (0, 0))] + _ffn_weight_specs(which),
        out_specs=[tok, act, act, tok],
        out_shape=[jax.ShapeDtypeStruct((t, D_MODEL), F32),
                   jax.ShapeDtypeStruct((N_CHIPS, t, FF_SHARD), BF16),
                   jax.ShapeDtypeStruct((N_CHIPS, t, FF_SHARD), BF16),
                   jax.ShapeDtypeStruct((t, D_MODEL), BF16)],
        scratch_shapes=[pltpu.VMEM((tm, D_MODEL), F32)],
        compiler_params=_ARB2,
    )(x, g, ffn_w, ffn_w, ffn_w)


def _ffn_bwd(dxo, x, g, a4, u4, ffn_w, which, name):
    t = x.shape[0]
    tm = min(t, TOKEN_TILE)

    def body(dxo_ref, x_ref, g_ref, a_ref, u_ref, wg_ref, wu_ref, wd_ref,
             da_ref, du_ref, hid_ref, dob_ref, dx_ref, dg_ref, acc_ref):
        i, j = pl.program_id(0), pl.program_id(1)

        @pl.when(j == 0)
        def _():
            dob_ref[...] = (0.5 * dxo_ref[...]).astype(BF16)
            acc_ref[...] = jnp.zeros_like(acc_ref)

        @pl.when((i == 0) & (j == 0))
        def _():
            dg_ref[...] = jnp.zeros_like(dg_ref)

        dhid = _dot_nt(dob_ref[...], wd_ref[...])
        a = a_ref[...].astype(F32)
        u = u_ref[...].astype(F32)
        sg = jax.nn.sigmoid(a)
        s = a * sg
        hid_ref[...] = (s * u).astype(BF16)
        du = (dhid * s).astype(BF16)
        da = (dhid * u * _silu_grad(a, sg)).astype(BF16)
        du_ref[...] = du
        da_ref[...] = da
        acc_ref[...] += _dot(da, wg_ref[...]) + _dot(du, wu_ref[...])

        @pl.when(j == N_CHIPS - 1)
        def _():
            dx, dg = _rms_bwd(acc_ref[...], x_ref[...], g_ref[...])
            dx_ref[...] = dxo_ref[...] + dx
            dg_ref[...] += dg

    tok = pl.BlockSpec((tm, D_MODEL), lambda i, j: (i, 0))
    act = pl.BlockSpec((None, tm, FF_SHARD), lambda i, j: (j, i, 0))
    row = pl.BlockSpec((1, D_MODEL), lambda i, j: (0, 0))
    act_shape = jax.ShapeDtypeStruct((N_CHIPS, t, FF_SHARD), BF16)
    return pl.pallas_call(
        body, name=name, grid=(t // tm, N_CHIPS),
        in_specs=[tok, tok, row, act, act] + _ffn_weight_specs(which),
        out_specs=[act, act, act, tok, tok, row],
        out_shape=[act_shape, act_shape, act_shape,
                   jax.ShapeDtypeStruct((t, D_MODEL), BF16),
                   jax.ShapeDtypeStruct((t, D_MODEL), F32),
                   jax.ShapeDtypeStruct((1, D_MODEL), F32)],
        scratch_shapes=[pltpu.VMEM((tm, D_MODEL), F32)],
        compiler_params=_ARB2,
    )(dxo, x, g, a4, u4, ffn_w, ffn_w, ffn_w)


def _matmul_tn(a, b, name, tka=None):
    a3, b3 = a.ndim == 3, b.ndim == 3
    nb = a.shape[0] if a3 else (b.shape[0] if b3 else 1)
    t, ka, n = a.shape[-2], a.shape[-1], b.shape[-1]
    tka = ka if tka is None else tka
    tk = min(t, TOKEN_TILE)

    def body(a_ref, b_ref, o_ref):
        @pl.when(pl.program_id(2) == 0)
        def _():
            o_ref[...] = jnp.zeros_like(o_ref)

        o_ref[...] += _dot_tn(a_ref[...].astype(BF16), b_ref[...].astype(BF16))

    a_spec = (pl.BlockSpec((None, tk, tka), lambda i, j, k: (i, k, j)) if a3
              else pl.BlockSpec((tk, tka), lambda i, j, k: (k, j)))
    b_spec = (pl.BlockSpec((None, tk, n), lambda i, j, k: (i, k, 0)) if b3
              else pl.BlockSpec((tk, n), lambda i, j, k: (k, 0)))
    return pl.pallas_call(
        body, name=name, grid=(nb, ka // tka, t // tk),
        in_specs=[a_spec, b_spec],
        out_specs=pl.BlockSpec((None, tka, n), lambda i, j, k: (i, j, 0)),
        out_shape=jax.ShapeDtypeStruct((nb, ka, n), F32),
        compiler_params=_ARB3,
    )(a, b)


def _matmul_nt(a, w, name, out_dtype=F32):
    t, k = a.shape
    n = w.shape[0]
    tm = min(t, TOKEN_TILE)

    def body(a_ref, w_ref, o_ref):
        o_ref[...] = _dot_nt(a_ref[...].astype(BF16), w_ref[...]).astype(out_dtype)

    return pl.pallas_call(
        body, name=name, grid=(t // tm,),
        in_specs=[pl.BlockSpec((tm, k), lambda i: (i, 0)), pl.BlockSpec((n, k), lambda i: (0, 0))],
        out_specs=pl.BlockSpec((tm, n), lambda i: (i, 0)),
        out_shape=jax.ShapeDtypeStruct((t, n), out_dtype),
        compiler_params=_ARB1,
    )(a, w)


def _mixer_in_bwd(dproj, w_in_pt, dres, x, g, name):
    t, k = dproj.shape
    tm = min(t, TOKEN_TILE)

    def body(a_ref, w_ref, dres_ref, x_ref, g_ref, dx_ref, dg_ref):
        @pl.when(pl.program_id(0) == 0)
        def _():
            dg_ref[...] = jnp.zeros_like(dg_ref)

        dh = _dot(a_ref[...], w_ref[...])
        dx, dg = _rms_bwd(dh, x_ref[...], g_ref[...])
        dx_ref[...] = dres_ref[...] + dx
        dg_ref[...] += dg

    tok = pl.BlockSpec((tm, D_MODEL), lambda i: (i, 0))
    row = pl.BlockSpec((1, D_MODEL), lambda i: (0, 0))
    return pl.pallas_call(
        body, name=name, grid=(t // tm,),
        in_specs=[pl.BlockSpec((tm, k), lambda i: (i, 0)), pl.BlockSpec((k, D_MODEL), lambda i: (0, 0)), tok, tok, row],
        out_specs=[tok, row],
        out_shape=[jax.ShapeDtypeStruct((t, D_MODEL), F32), jax.ShapeDtypeStruct((1, D_MODEL), F32)],
        compiler_params=_ARB1,
    )(dproj, w_in_pt, dres, x, g)


def _mixer_in_fwd(x, g, w_in_pt, name):
    t = x.shape[0]
    tm = min(t, TOKEN_TILE)
    tn = PROJ_P // 3

    def body(x_ref, g_ref, w_ref, p_ref, h_ref):
        @pl.when(pl.program_id(1) == 0)
        def _():
            xv = x_ref[...]
            h_ref[...] = ((xv * _rms_scale(xv)) * g_ref[...]).astype(BF16)

        p_ref[...] = _dot_nt(h_ref[...], w_ref[...])

    tok = pl.BlockSpec((tm, D_MODEL), lambda i, j: (i, 0))
    return pl.pallas_call(
        body, name=name, grid=(t // tm, 3),
        in_specs=[tok, pl.BlockSpec((1, D_MODEL), lambda i, j: (0, 0)),
                  pl.BlockSpec((tn, D_MODEL), lambda i, j: (j, 0))],
        out_specs=[pl.BlockSpec((tm, tn), lambda i, j: (i, j)), tok],
        out_shape=[jax.ShapeDtypeStruct((t, PROJ_P), F32), jax.ShapeDtypeStruct((t, D_MODEL), BF16)],
        compiler_params=_ARB2,
    )(x, g, w_in_pt)


def _mixer_out_fwd(o_ret, o_gla, w_out, x, name):
    t = x.shape[0]
    tm = min(t, TOKEN_TILE)
    half = HEADS * LANES

    def body(a_ref, b_ref, w_ref, x_ref, o_ref):
        o_ref[...] = x_ref[...] + _dot(a_ref[...], w_ref[0:half, :]) + _dot(b_ref[...], w_ref[half:2 * half, :])

    tok = pl.BlockSpec((tm, D_MODEL), lambda i: (i, 0))
    hb = pl.BlockSpec((tm, half), lambda i: (i, 0))
    return pl.pallas_call(
        body, name=name, grid=(t // tm,),
        in_specs=[hb, hb, pl.BlockSpec((2 * half, D_MODEL), lambda i: (0, 0)), tok],
        out_specs=tok, out_shape=jax.ShapeDtypeStruct((t, D_MODEL), F32),
        compiler_params=_ARB1,
    )(o_ret, o_gla, w_out, x)


def _final_loss(x, g, target, name):
    t = x.shape[0]
    tm = min(t, TOKEN_TILE)

    def body(x_ref, g_ref, t_ref, l_ref, dx_ref, dg_ref):
        @pl.when(pl.program_id(0) == 0)
        def _():
            l_ref[...] = jnp.zeros_like(l_ref)
            dg_ref[...] = jnp.zeros_like(dg_ref)

        xv = x_ref[...]
        gv = g_ref[...]
        err = (xv * _rms_scale(xv)) * gv - t_ref[...]
        l_ref[...] += 0.5 * jnp.sum(jnp.mean(err * err, axis=-1, keepdims=True), axis=0, keepdims=True)
        dx, dg = _rms_bwd(err * (1.0 / D_MODEL), xv, gv)
        dx_ref[...] = dx
        dg_ref[...] += dg

    tok = pl.BlockSpec((tm, D_MODEL), lambda i: (i, 0))
    row = pl.BlockSpec((1, D_MODEL), lambda i: (0, 0))
    return pl.pallas_call(
        body, name=name, grid=(t // tm,),
        in_specs=[tok, row, tok],
        out_specs=[pl.BlockSpec((8, LANES), lambda i: (0, 0)), tok, row],
        out_shape=[jax.ShapeDtypeStruct((8, LANES), F32), jax.ShapeDtypeStruct((t, D_MODEL), F32),
                   jax.ShapeDtypeStruct((1, D_MODEL), F32)],
        compiler_params=_ARB1,
    )(x, g, target)


def _rot(v, cos, sa, sb):
    return v * cos + pltpu.roll(v, 96, 1) * sa + pltpu.roll(v, 32, 1) * sb


def _rot_t(d, cos, sa, sb):
    return d * cos + pltpu.roll(d * sa, 32, 1) + pltpu.roll(d * sb, 96, 1)


def _chunk_inputs(is_ret, rows, proj_ref, aux):
    q_raw = proj_ref[rows, 0:LANES]
    k_raw = proj_ref[rows, LANES:2 * LANES]
    v = proj_ref[rows, 2 * LANES:3 * LANES]
    gate = proj_ref[rows, 3 * LANES:4 * LANES]
    ri = lax.broadcasted_iota(jnp.int32, (CHUNK, CHUNK), 0)
    ci = lax.broadcasted_iota(jnp.int32, (CHUNK, CHUNK), 1)
    if is_ret:
        cos_ref, sa_ref, sb_ref, lg_ref = aux
        cos, sa, sb = cos_ref[rows, :], sa_ref[rows, :], sb_ref[rows, :]
        q = _rot(q_raw, cos, sa, sb)
        k = _rot(k_raw, cos, sa, sb) * QK_SCALE
        steps = (lax.broadcasted_iota(jnp.int32, (CHUNK, LANES), 0) + 1).astype(F32)
        b = steps * lg_ref[...]
        logit = None
    else:
        glow_ref, wa2_ref, ba_ref = aux
        logit = _dot(glow_ref[rows, :].astype(BF16), wa2_ref[...]) + ba_ref[...]
        la = (jnp.minimum(logit, 0.0) - jnp.log1p(jnp.exp(-jnp.abs(logit)))) * (1.0 / GATE_NORM)
        lower = (ci <= ri).astype(F32)
        b = jnp.dot(lower, la, precision=lax.Precision.HIGHEST, preferred_element_type=F32)
        q = q_raw * QK_SCALE
        k = k_raw
    return q, k, v, gate, b, logit, ri, ci


def _chunk_scores(q, k, b, ri, ci):
    mid = b[CHUNK // 2 - 1:CHUNK // 2, :]
    ep = jnp.exp(b - mid)
    en = jnp.exp(mid - b)
    qt, kt, qh, kh = q * ep, k * en, q * en, k * ep
    low = _dot_nt(qt.astype(BF16), kt.astype(BF16))
    upp = _dot_nt(qh.astype(BF16), kh.astype(BF16))
    scores = jnp.where(ci <= ri, low, upp)
    return scores, ep, en, qt, kt, qh, kh


def _attn_specs(is_ret, t, tb, imap_t):
    nb = t // tb
    base = 0 if is_ret else HEADS
    proj = pl.BlockSpec((tb, HEAD_BLOCK), lambda h, i: (imap_t(i), base + h))
    lane_t = pl.BlockSpec((tb, LANES), lambda h, i: (imap_t(i), 0))
    if is_ret:
        aux = [lane_t, lane_t, lane_t, pl.BlockSpec((None, 1, LANES), lambda h, i: (h, 0, 0))]
    else:
        aux = [pl.BlockSpec((tb, LANES), lambda h, i: (imap_t(i), PROJ_P // LANES - 1)),
               pl.BlockSpec((LANES, LANES), lambda h, i: (0, h)),
               pl.BlockSpec((1, LANES), lambda h, i: (0, h))]
    gain = pl.BlockSpec((1, LANES), lambda h, i: (0, h))
    head_t = pl.BlockSpec((tb, LANES), lambda h, i: (imap_t(i), h))
    state = pl.BlockSpec((None, tb // CHUNK, LANES, LANES), lambda h, i: (h, imap_t(i), 0, 0))
    return nb, proj, aux, gain, head_t, state


def _attn_fwd(is_ret, proj, aux_arrays, gain, name):
    t = proj.shape[0]
    tb = min(t, ATTN_TILE)
    n_aux = 4 if is_ret else 3
    nb, proj_spec, aux_specs, gain_spec, head_t, state_spec = _attn_specs(is_ret, t, tb, lambda i: i)

    def body(*refs):
        proj_ref = refs[0]
        aux = refs[1:1 + n_aux]
        gn_ref, ofin_ref, oraw_ref, st_ref, state = refs[1 + n_aux:]

        @pl.when(pl.program_id(1) == 0)
        def _():
            state[...] = jnp.zeros_like(state)

        def chunk(c, carry):
            rows = pl.ds(pl.multiple_of(c * CHUNK, CHUNK), CHUNK)
            q, k, v, gate, b, _, ri, ci = _chunk_inputs(is_ret, rows, proj_ref, aux)
            scores = _chunk_scores(q, k, b, ri, ci)[0]
            st = state[...]
            st_ref[c] = st
            vb = v.astype(BF16)
            out = _dot(scores.astype(BF16), vb) + _dot_nt((q * jnp.exp(b)).astype(BF16), st.astype(BF16))
            b_last = b[CHUNK - 1:CHUNK, :]
            kd = k * jnp.exp(b_last - b)
            state[...] = st * jnp.exp(b_last) + _dot_tn(vb, kd.astype(BF16))
            oraw_ref[rows, :] = out
            normed = out * _rms_scale(out)
            ofin_ref[rows, :] = ((normed * gn_ref[...]) * (gate * jax.nn.sigmoid(gate))).astype(BF16)
            return carry

        lax.fori_loop(0, tb // CHUNK, chunk, 0)

    width = HEADS * LANES
    return pl.pallas_call(
        body, name=name, grid=(HEADS, nb),
        in_specs=[proj_spec] + aux_specs + [gain_spec],
        out_specs=[head_t, head_t, state_spec],
        out_shape=[jax.ShapeDtypeStruct((t, width), BF16), jax.ShapeDtypeStruct((t, width), F32),
                   jax.ShapeDtypeStruct((HEADS, t // CHUNK, LANES, LANES), F32)],
        scratch_shapes=[pltpu.VMEM((LANES, LANES), F32)],
        compiler_params=_ARB2,
    )(proj, *aux_arrays, gain)


def _attn_bwd(is_ret, proj, aux_arrays, gain, o_raw, states, d_out, name):
    t = proj.shape[0]
    tb = min(t, ATTN_TILE)
    nc = tb // CHUNK
    n_aux = 4 if is_ret else 3
    nblk = t // tb
    nb, proj_spec, aux_specs, gain_spec, head_t, state_spec = _attn_specs(is_ret, t, tb, lambda i: nblk - 1 - i)
    base = 0 if is_ret else HEADS
    dout_spec = pl.BlockSpec((tb, LANES), lambda h, i: (nblk - 1 - i, base + h))

    def body(*refs):
        proj_ref = refs[0]
        aux = refs[1:1 + n_aux]
        gn_ref, oraw_ref, st_ref, dfin_ref = refs[1 + n_aux:5 + n_aux]
        if is_ret:
            dproj_ref, dgn_ref, dstate = refs[5 + n_aux:]
        else:
            dproj_ref, dgn_ref, dlogit_ref, dba_ref, dstate = refs[5 + n_aux:]

        @pl.when(pl.program_id(1) == 0)
        def _():
            dstate[...] = jnp.zeros_like(dstate)
            dgn_ref[...] = jnp.zeros_like(dgn_ref)
            if not is_ret:
                dba_ref[...] = jnp.zeros_like(dba_ref)

        def chunk(step, carry):
            c = nc - 1 - step
            rows = pl.ds(pl.multiple_of(c * CHUNK, CHUNK), CHUNK)
            q, k, v, gate, b, logit, ri, ci = _chunk_inputs(is_ret, rows, proj_ref, aux)
            scores, ep, en, qt, kt, qh, kh = _chunk_scores(q, k, b, ri, ci)
            st = st_ref[c]
            dst = dstate[...]
            eb = jnp.exp(b)
            qe = q * eb
            b_last = b[CHUNK - 1:CHUNK, :]
            e_last = jnp.exp(b_last)
            ekd = jnp.exp(b_last - b)
            kd = k * ekd

            gn = gn_ref[...]
            out = oraw_ref[rows, :]
            r = _rms_scale(out)
            normed = out * r
            sg = jax.nn.sigmoid(gate)
            dfin = dfin_ref[rows, :]
            dgate = dfin * (normed * gn) * _silu_grad(gate, sg)
            dpre = dfin * (gate * sg)
            dgn_ref[...] += jnp.sum(dpre * normed, axis=0, keepdims=True)
            dnormed = dpre * gn
            d_o = r * (dnormed - normed * jnp.mean(dnormed * normed, axis=-1, keepdims=True))

            dob, vb = d_o.astype(BF16), v.astype(BF16)
            stb, dstb = st.astype(BF16), dst.astype(BF16)
            dv = _dot_tn(scores.astype(BF16), dob) + _dot_nt(kd.astype(BF16), dstb)
            dsc = _dot_nt(dob, vb)
            dqe = _dot(dob, stb)
            dkd = _dot(vb, dstb)
            dlow = jnp.where(ci <= ri, dsc, 0.0).astype(BF16)
            dupp = jnp.where(ci <= ri, 0.0, dsc).astype(BF16)
            dqt = _dot(dlow, kt.astype(BF16))
            dkt = _dot_tn(dlow, qt.astype(BF16))
            dqh = _dot(dupp, kh.astype(BF16))
            dkh = _dot_tn(dupp, qh.astype(BF16))
            dq = dqt * ep + dqh * en + dqe * eb
            dk = dkt * en + dkh * ep + dkd * ekd
            dstate[...] = dst * e_last + _dot_tn(dob, qe.astype(BF16))

            if is_ret:
                cos_ref, sa_ref, sb_ref, _ = aux
                cos, sa, sb = cos_ref[rows, :], sa_ref[rows, :], sb_ref[rows, :]
                dq_raw = _rot_t(dq, cos, sa, sb)
                dk_raw = _rot_t(dk, cos, sa, sb) * QK_SCALE
            else:
                dq_raw = dq * QK_SCALE
                dk_raw = dk
                db = dqt * qt - dkt * kt - dqh * qh + dkh * kh + dqe * qe - dkd * kd
                db_last = (jnp.sum(dkd * kd, axis=0, keepdims=True)
                           + jnp.sum(dst * st, axis=0, keepdims=True) * e_last)
                last_row = lax.broadcasted_iota(jnp.int32, (CHUNK, LANES), 0) == CHUNK - 1
                db = db + jnp.where(last_row, db_last, 0.0)
                upper = (ci >= ri).astype(F32)
                dla = jnp.dot(upper, db, precision=lax.Precision.HIGHEST, preferred_element_type=F32)
                dlogit = dla * (1.0 / GATE_NORM) * jax.nn.sigmoid(-logit)
                dlogit_ref[rows, :] = dlogit.astype(BF16)
                dba_ref[...] += jnp.sum(dlogit, axis=0, keepdims=True)

            dproj_ref[rows, 0:LANES] = dq_raw.astype(BF16)
            dproj_ref[rows, LANES:2 * LANES] = dk_raw.astype(BF16)
            dproj_ref[rows, 2 * LANES:3 * LANES] = dv.astype(BF16)
            dproj_ref[rows, 3 * LANES:4 * LANES] = dgate.astype(BF16)
            return carry

        lax.fori_loop(0, nc, chunk, 0)

    width = HEADS * LANES
    row_out = pl.BlockSpec((1, LANES), lambda h, i: (0, h))
    out_specs = [pl.BlockSpec((tb, HEAD_BLOCK), lambda h, i: (nblk - 1 - i, h)), row_out]
    out_shape = [jax.ShapeDtypeStruct((t, HEADS * HEAD_BLOCK), BF16), jax.ShapeDtypeStruct((1, width), F32)]
    if not is_ret:
        out_specs += [head_t, row_out]
        out_shape += [jax.ShapeDtypeStruct((t, width), BF16), jax.ShapeDtypeStruct((1, width), F32)]
    return pl.pallas_call(
        body, name=name, grid=(HEADS, nblk),
        in_specs=[proj_spec] + aux_specs + [gain_spec, head_t, state_spec, dout_spec],
        out_specs=out_specs, out_shape=out_shape,
        scratch_shapes=[pltpu.VMEM((LANES, LANES), F32)],
        compiler_params=_ARB2,
    )(proj, *aux_arrays, gain, o_raw, states, d_out)


def _place():
    x, y, c = lax.axis_index("x"), lax.axis_index("y"), lax.axis_index("c")
    chips = [(1 - x, y), (x, 1 - y), (1 - x, 1 - y)]
    return x, y, c, 2 * x + y, chips


def _gather_weights(arrs):
    na = len(arrs)

    def body(*refs):
        ins, outs = refs[:na], refs[na:2 * na]
        send_sems, recv_sems = refs[2 * na:]
        x, y, c, me, chips = _place()
        sibling = (x, y, 1 - c)

        def ici(a, j, src_chip, to):
            return pltpu.make_async_remote_copy(
                src_ref=ins[a].at[:, c], dst_ref=outs[a].at[src_chip, :, c],
                send_sem=send_sems.at[6 * a + j], recv_sem=recv_sems.at[6 * a + j], device_id=to, device_id_type=MESH)

        def d2d(a, j, src_chip, half):
            blk = outs[a].at[src_chip, :, half]
            return pltpu.make_async_remote_copy(
                src_ref=blk, dst_ref=blk, send_sem=send_sems.at[6 * a + 3 + j], recv_sem=recv_sems.at[6 * a + 3 + j],
                device_id=sibling, device_id_type=MESH)

        sends = [ici(a, j, me, (px, py, c)) for a in range(na) for j, (px, py) in enumerate(chips)]
        for cp in sends:
            cp.start()
        passed = []
        for a in range(na):
            for j, (px, py) in enumerate(chips):
                ici(a, j, 2 * px + py, (px, py, c)).wait_recv()
                cp = d2d(a, j, 2 * px + py, c)
                cp.start()
                passed.append(cp)
        for a in range(na):
            for j, (px, py) in enumerate(chips):
                d2d(a, j, 2 * px + py, 1 - c).wait_recv()
        for cp in sends + passed:
            cp.wait_send()

    return pl.pallas_call(
        body, name="gather_weights",
        in_specs=[ANY] * na, out_specs=[ANY] * na,
        out_shape=[jax.ShapeDtypeStruct((N_CHIPS,) + a.shape, a.dtype) for a in arrs],
        scratch_shapes=[pltpu.SemaphoreType.DMA((6 * na,)), pltpu.SemaphoreType.DMA((6 * na,))],
    )(*arrs)


def _pair_exchange(grads):
    na = len(grads)

    def body(*refs):
        ins, outs = refs[:na], refs[na:2 * na]
        send_sems, recv_sems = refs[2 * na:]
        x, y, c, _, _ = _place()
        copies = [pltpu.make_async_remote_copy(
            src_ref=ins[a].at[:, 1 - c], dst_ref=outs[a], send_sem=send_sems.at[a], recv_sem=recv_sems.at[a],
            device_id=(x, y, 1 - c), device_id_type=MESH) for a in range(na)]
        for cp in copies:
            cp.start()
        for cp in copies:
            cp.wait()

    return pl.pallas_call(
        body, name="pair_exchange",
        in_specs=[ANY] * na, out_specs=[ANY] * na,
        out_shape=[jax.ShapeDtypeStruct(g.shape[:1] + g.shape[2:], g.dtype) for g in grads],
        scratch_shapes=[pltpu.SemaphoreType.DMA((na,)), pltpu.SemaphoreType.DMA((na,))],
    )(*grads)


def _pair_add(grad, recv, c_arr, name):
    _, _, r, cols = grad.shape

    def body(c_ref, g_ref, r_ref, o_ref):
        o_ref[...] = (g_ref[...] + r_ref[...]).astype(BF16)

    return pl.pallas_call(
        body, name=name,
        grid_spec=pltpu.PrefetchScalarGridSpec(
            num_scalar_prefetch=1, grid=(N_CHIPS,),
            in_specs=[pl.BlockSpec((None, None, r, cols), lambda p, c_ref: (p, c_ref[0], 0, 0)),
                      pl.BlockSpec((None, r, cols), lambda p, c_ref: (p, 0, 0))],
            out_specs=pl.BlockSpec((None, r, cols), lambda p, c_ref: (p, 0, 0))),
        out_shape=jax.ShapeDtypeStruct((N_CHIPS, r, cols), BF16),
        compiler_params=_ARB1,
    )(c_arr, grad, recv)


def _chip_exchange(sums):
    na = len(sums)

    def body(*refs):
        ins, outs = refs[:na], refs[na:2 * na]
        send_sems, recv_sems = refs[2 * na:]
        x, y, c, me, chips = _place()
        sends = [pltpu.make_async_remote_copy(
            src_ref=ins[a].at[2 * px + py], dst_ref=outs[a].at[me],
            send_sem=send_sems.at[3 * a + j], recv_sem=recv_sems.at[3 * a + j],
            device_id=(px, py, c), device_id_type=MESH) for a in range(na) for j, (px, py) in enumerate(chips)]
        for cp in sends:
            cp.start()
        for a in range(na):
            for j, (px, py) in enumerate(chips):
                pltpu.make_async_remote_copy(
                    src_ref=ins[a].at[me], dst_ref=outs[a].at[2 * px + py],
                    send_sem=send_sems.at[3 * a + j], recv_sem=recv_sems.at[3 * a + j],
                    device_id=(px, py, c), device_id_type=MESH).wait_recv()
        for cp in sends:
            cp.wait_send()

    return pl.pallas_call(
        body, name="chip_exchange",
        in_specs=[ANY] * na, out_specs=[ANY] * na,
        out_shape=[jax.ShapeDtypeStruct(s.shape, s.dtype) for s in sums],
        scratch_shapes=[pltpu.SemaphoreType.DMA((3 * na,)), pltpu.SemaphoreType.DMA((3 * na,))],
    )(*sums)


def _chip_sum(own, recv, me_arr, name):
    _, r, cols = recv.shape

    def body(me_ref, own_ref, r_ref, o_ref):
        o_ref[...] = jnp.zeros_like(o_ref)
        for q in range(N_CHIPS):
            @pl.when(me_ref[0] == q)
            def _():
                o_ref[...] += own_ref[...].astype(F32)

            @pl.when(me_ref[0] != q)
            def _():
                o_ref[...] += r_ref[q].astype(F32)

    return pl.pallas_call(
        body, name=name,
        grid_spec=pltpu.PrefetchScalarGridSpec(
            num_scalar_prefetch=1, grid=(1,),
            in_specs=[pl.BlockSpec((None, r, cols), lambda i, me_ref: (me_ref[0], 0, 0)),
                      pl.BlockSpec((N_CHIPS, r, cols), lambda i, me_ref: (0, 0, 0))],
            out_specs=pl.BlockSpec((r, cols), lambda i, me_ref: (0, 0))),
        out_shape=jax.ShapeDtypeStruct((r, cols), F32),
        compiler_params=_ARB1,
    )(me_arr, own, recv)


def _pair_share(halves):
    na = len(halves)

    def body(*refs):
        ins, outs = refs[:na], refs[na:2 * na]
        send_sems, recv_sems = refs[2 * na:]
        x, y, c, _, _ = _place()
        copies = [pltpu.make_async_remote_copy(
            src_ref=ins[a], dst_ref=outs[a], send_sem=send_sems.at[a], recv_sem=recv_sems.at[a],
            device_id=(x, y, 1 - c), device_id_type=MESH) for a in range(na)]
        for cp in copies:
            cp.start()
        for cp in copies:
            cp.wait()

    return pl.pallas_call(
        body, name="pair_share",
        in_specs=[ANY] * na, out_specs=[ANY] * na,
        out_shape=[jax.ShapeDtypeStruct(h.shape, h.dtype) for h in halves],
        scratch_shapes=[pltpu.SemaphoreType.DMA((na,)), pltpu.SemaphoreType.DMA((na,))],
    )(*halves)


def _small_allreduce(block):
    m, n = block.shape

    def body(x_ref, all_ref, sum_ref, send_sems, recv_sems, local_sem):
        x, y, c, _, chips = _place()
        me, sibling = (x, y, c), (x, y, 1 - c)

        def rows(px, py, pc):
            return all_ref.at[pl.ds((4 * px + 2 * py + pc) * m, m), :]

        def copy(k, blk, to, src=None):
            return pltpu.make_async_remote_copy(
                src_ref=rows(*blk) if src is None else src, dst_ref=rows(*blk),
                send_sem=send_sems.at[k], recv_sem=recv_sems.at[k], device_id=to, device_id_type=MESH)

        mine = pltpu.make_async_copy(x_ref, rows(*me), local_sem)
        mine.start()
        first = [copy(0, me, sibling, src=x_ref)]
        first += [copy(1 + j, me, (*chip, c), src=x_ref) for j, chip in enumerate(chips)]
        for cp in first:
            cp.start()
        passed = [copy(4 + j, (*chip, c), sibling) for j, chip in enumerate(chips)]
        for j, chip in enumerate(chips):
            copy(1 + j, (*chip, c), me).wait_recv()
            passed[j].start()
        copy(0, sibling, me).wait_recv()
        for j, chip in enumerate(chips):
            copy(4 + j, (*chip, 1 - c), me).wait_recv()
        for cp in first + passed:
            cp.wait_send()
        mine.wait()
        acc = all_ref[0:m, :]
        for d in range(1, 8):
            acc = acc + all_ref[d * m:(d + 1) * m, :]
        sum_ref[...] = acc

    vmem = pl.BlockSpec(memory_space=pltpu.VMEM)
    return pl.pallas_call(
        body, name="small_allreduce",
        in_specs=[vmem], out_specs=[vmem, vmem],
        out_shape=[jax.ShapeDtypeStruct((8 * m, n), F32), jax.ShapeDtypeStruct((m, n), F32)],
        scratch_shapes=[pltpu.SemaphoreType.DMA((7,)), pltpu.SemaphoreType.DMA((7,)), pltpu.SemaphoreType.DMA],
    )(block)[1]


def _row_tile(rows):
    best = rows
    for cand in range(8, min(rows, 512) + 1, 8):
        if rows % cand == 0:
            best = cand
    return best


def _adamw_math(w, g, m, v):
    m2 = ADAM_B1 * m + (1.0 - ADAM_B1) * g
    v2 = ADAM_B2 * v + (1.0 - ADAM_B2) * (g * g)
    m_hat = m2 / (1.0 - ADAM_B1 ** ADAM_STEP)
    v_hat = v2 / (1.0 - ADAM_B2 ** ADAM_STEP)
    return -ADAM_LR * (m_hat / (jnp.sqrt(v_hat) + ADAM_EPS) + ADAM_WD * w), m2, v2


def _adamw_halves(w, g_mine, g_other, m, v, c_arr, name):
    rows, cols = w.shape
    r = rows // 2
    tr = _row_tile(r)
    nt = r // tr

    def body(c_ref, w_ref, gm_ref, go_ref, m_ref, v_ref, g_ref, d_ref, nm_ref, nv_ref):
        gv = jnp.where(pl.program_id(0) == c_ref[0], gm_ref[...], go_ref[...])
        g_ref[...] = gv
        d_ref[...], nm_ref[...], nv_ref[...] = _adamw_math(w_ref[...], gv, m_ref[...], v_ref[...])

    full = pl.BlockSpec((tr, cols), lambda h, i, c_ref: (h * nt + i, 0))
    half = pl.BlockSpec((tr, cols), lambda h, i, c_ref: (i, 0))
    shape = jax.ShapeDtypeStruct((rows, cols), F32)
    return pl.pallas_call(
        body, name=name,
        grid_spec=pltpu.PrefetchScalarGridSpec(
            num_scalar_prefetch=1, grid=(2, nt),
            in_specs=[full, half, half, full, full], out_specs=[full] * 4),
        out_shape=[shape] * 4,
        compiler_params=_ARB2,
    )(c_arr, w, g_mine, g_other, m, v)


def _adamw(w, g, m, v, name):
    rows, cols = w.shape
    tr = _row_tile(rows)

    def body(w_ref, g_ref, m_ref, v_ref, d_ref, nm_ref, nv_ref):
        d_ref[...], nm_ref[...], nv_ref[...] = _adamw_math(w_ref[...], g_ref[...], m_ref[...], v_ref[...])

    spec = pl.BlockSpec((tr, cols), lambda i: (i, 0))
    shape = jax.ShapeDtypeStruct((rows, cols), F32)
    return pl.pallas_call(
        body, name=name, grid=(rows // tr,),
        in_specs=[spec] * 4, out_specs=[spec] * 3, out_shape=[shape] * 3,
        compiler_params=_ARB1,
    )(w, g, m, v)


def _in_columns():
    pieces = []
    for group in range(2):
        q0, k0, v0, g0 = (0, 256, 512, 1024) if group == 0 else (1536, 1792, 2048, 2560)
        for h in range(HEADS):
            pieces += [(q0 + 64 * h, 64), (k0 + 64 * h, 64), (v0 + 128 * h, 128), (g0 + 128 * h, 128)]
    pieces.append((3072, GATE_RANK))
    return pieces


def _pad_w_in_t(w_in_t):
    parts = []
    for start, width in _in_columns():
        parts.append(w_in_t[start:start + width])
        if width < LANES:
            parts.append(jnp.zeros((LANES - width, w_in_t.shape[1]), w_in_t.dtype))
    return jnp.concatenate(parts, axis=0)


def _unpad_w_in_t(w_pt):
    rows = {}
    offset = 0
    for start, width in _in_columns():
        rows[start] = w_pt[offset:offset + width]
        offset += LANES
    return jnp.concatenate([rows[s] for s in sorted(rows)], axis=0)


def _rope_tables(t):
    half = 32
    inv = ROPE_BASE ** (-jnp.arange(half, dtype=F32) * 2.0 / 64)
    ang = jnp.arange(t, dtype=F32)[:, None] * inv[None, :]
    cos, sin = jnp.cos(ang), jnp.sin(ang)
    z32, z64 = jnp.zeros((t, 32), F32), jnp.zeros((t, 64), F32)
    return (jnp.concatenate([cos, cos, z64], axis=1),
            jnp.concatenate([-sin, z32, z64], axis=1),
            jnp.concatenate([z32, sin, z64], axis=1))


def _halves(w):
    n, rows, cols = w.shape
    return w.reshape(n, 2, rows // 2, cols)


def _pack_small(n1, nm, n2, nf, nret, ngla, ba, wa2, wa2_cols, extra=None):
    z = lambda k: jnp.zeros((1, k), F32)
    rows = [n1.reshape(1, -1), nm.reshape(1, -1), n2.reshape(1, -1), nf.reshape(1, -1),
            jnp.concatenate([nret.reshape(1, -1), ngla.reshape(1, -1)], axis=1),
            jnp.concatenate([ba.reshape(1, -1), z(D_MODEL - 256)], axis=1),
            jnp.zeros((1, D_MODEL), F32) if extra is None else extra,
            jnp.zeros((1, D_MODEL), F32),
            jnp.concatenate([wa2.reshape(GATE_RANK, wa2_cols), jnp.zeros((GATE_RANK, D_MODEL - wa2_cols), F32)], axis=1),
            jnp.zeros((SMALL_ROWS - 8 - GATE_RANK, D_MODEL), F32)]
    return jnp.concatenate(rows, axis=0)


def _unpack_small(p, wa2_cols):
    return (p[0:1], p[1:2], p[2:3], p[3], p[4:5, 0:512], p[4:5, 512:1024], p[5:6, 0:256],
            p[8:8 + GATE_RANK, 0:wa2_cols].reshape(1, GATE_RANK, wa2_cols))


def _forward_backward(xs, target, ffn_w, w_in_pt, w_out_full, wa2_p, ba_p, ffn1_norm_g, mix_norm_g, ret_norm_g,
                      gla_norm_g, ffn2_norm_g, final_norm_g):
    t = xs.shape[0]
    cos_t, sa_t, sb_t = _rope_tables(t)
    log_gamma = jnp.log(1.0 - 2.0 ** (-5.0 - jnp.arange(HEADS, dtype=F32)))
    lg_t = jnp.broadcast_to(log_gamma[:, None, None], (HEADS, 1, LANES))
    ret_aux = [cos_t, sa_t, sb_t, lg_t]

    x1, a1, u1, h1 = _ffn_fwd(xs, ffn1_norm_g, ffn_w, 0, "ffn1_fwd")
    proj, h_mix = _mixer_in_fwd(x1, mix_norm_g, w_in_pt, "mixer_in_fwd")
    gla_aux = [proj, wa2_p, ba_p]
    o_ret, raw_ret, st_ret = _attn_fwd(True, proj, ret_aux, ret_norm_g, "ret_fwd")
    o_gla, raw_gla, st_gla = _attn_fwd(False, proj, gla_aux, gla_norm_g, "gla_fwd")
    x2 = _mixer_out_fwd(o_ret, o_gla, w_out_full, x1, "mixer_out_fwd")
    x3, a2, u2, h2 = _ffn_fwd(x2, ffn2_norm_g, ffn_w, 1, "ffn2_fwd")
    loss_blk, dx3, d_final_g = _final_loss(x3, final_norm_g, target, "final_loss")

    da2, du2, hid2, dob2, dx2, d_ffn2_g = _ffn_bwd(dx3, x2, ffn2_norm_g, a2, u2, ffn_w, 1, "ffn2_bwd")
    g_gate2 = _matmul_tn(da2, h2, "ffn2_dgate")
    g_up2 = _matmul_tn(du2, h2, "ffn2_dup")
    g_down2 = _matmul_tn(hid2, dob2, "ffn2_ddown")

    d_o = _matmul_nt(dx2, w_out_full, "mixer_out_bwd")
    g_wout_ret = _matmul_tn(o_ret, dx2, "wout_grad_ret")
    g_wout_gla = _matmul_tn(o_gla, dx2, "wout_grad_gla")
    dproj_ret, d_ret_g = _attn_bwd(True, proj, ret_aux, ret_norm_g, raw_ret, st_ret, d_o, "ret_bwd")
    dproj_gla, d_gla_g, dlogit, d_ba_p = _attn_bwd(False, proj, gla_aux, gla_norm_g, raw_gla, st_gla, d_o, "gla_bwd")
    d_glow = _matmul_nt(dlogit, wa2_p, "gate_low_bwd", out_dtype=BF16)
    g_wa2_p = _matmul_tn(proj[:, PROJ_P - LANES:], dlogit, "gate_w_grad")
    dproj = jnp.concatenate([dproj_ret, dproj_gla, d_glow], axis=1)
    g_win_p = _matmul_tn(dproj, h_mix, "w_in_grad", tka=PROJ_P // 3)
    dx1, d_mix_g = _mixer_in_bwd(dproj, w_in_pt, dx2, x1, mix_norm_g, "mixer_in_bwd")

    da1, du1, hid1, dob1, grad_x, d_ffn1_g = _ffn_bwd(dx1, xs, ffn1_norm_g, a1, u1, ffn_w, 0, "ffn1_bwd")
    g_gate1 = _matmul_tn(da1, h1, "ffn1_dgate")
    g_up1 = _matmul_tn(du1, h1, "ffn1_dup")
    g_down1 = _matmul_tn(hid1, dob1, "ffn1_ddown")

    return (loss_blk, grad_x, g_gate1, g_up1, g_down1, g_gate2, g_up2, g_down2, g_win_p, g_wout_ret, g_wout_gla, g_wa2_p,
            d_ba_p, d_ffn1_g, d_mix_g, d_ffn2_g, d_final_g, d_ret_g, d_gla_g)


def kernel(x, ffn1_norm_g, ffn1_w_gate, ffn1_w_up, ffn1_w_down, mix_norm_g, w_in, ret_norm_g, gla_w_a2, gla_b_a, gla_norm_g, w_out, ffn2_norm_g, ffn2_w_gate, ffn2_w_up, ffn2_w_down, final_norm_g, loss_target, m_ffn1_norm_g, m_ffn1_w_gate, m_ffn1_w_up, m_ffn1_w_down, m_mix_norm_g, m_w_in, m_ret_norm_g, m_gla_w_a2, m_gla_b_a, m_gla_norm_g, m_w_out, m_ffn2_norm_g, m_ffn2_w_gate, m_ffn2_w_up, m_ffn2_w_down, m_final_norm_g, v_ffn1_norm_g, v_ffn1_w_gate, v_ffn1_w_up, v_ffn1_w_down, v_mix_norm_g, v_w_in, v_ret_norm_g, v_gla_w_a2, v_gla_b_a, v_gla_norm_g, v_w_out, v_ffn2_norm_g, v_ffn2_w_gate, v_ffn2_w_up, v_ffn2_w_down, v_final_norm_g):
    t = x.shape[1]
    xs = x.reshape(t, D_MODEL)
    target = loss_target.reshape(t, D_MODEL)
    chip = 2 * lax.axis_index("x") + lax.axis_index("y")
    c_arr = lax.axis_index("c").astype(jnp.int32).reshape(1)

    me_arr = chip.astype(jnp.int32).reshape(1)

    def pad_rows(w_t):
        return jnp.pad(w_t, ((0, IN_ROWS - IN_SHARD), (0, 0)))

    ffn_params = [ffn1_w_gate[0].T, ffn1_w_up[0].T, ffn1_w_down[0], ffn2_w_gate[0].T, ffn2_w_up[0].T, ffn2_w_down[0]]
    shards_in = [_halves(jnp.stack(ffn_params, axis=0).astype(BF16)),
                 _halves(pad_rows(w_in[0].T).astype(BF16)[None]),
                 _halves(w_out.astype(BF16)),
                 jnp.concatenate([gla_w_a2.reshape(GATE_RANK, 64), jnp.zeros((GATE_RANK, 64), F32)], axis=1).reshape(
                     1, 2, 8, LANES)]
    gathered_w = _gather_weights(shards_in)
    ffn_all, win_all, wout_all, wa2_all = [
        lax.dynamic_update_slice(g, s[None], (chip,) + (0,) * s.ndim) for g, s in zip(gathered_w, shards_in)]
    ffn_w = ffn_all.reshape(N_CHIPS, 6, FF_SHARD, D_MODEL)
    win_t = win_all.reshape(N_CHIPS, IN_ROWS, D_MODEL)
    w_in_pt = _pad_w_in_t(jnp.concatenate([win_t[p, 0:IN_SHARD] for p in range(N_CHIPS)], axis=0))
    w_out_full = wout_all.reshape(D_MODEL, D_MODEL)
    wa2_p = jnp.pad(wa2_all.reshape(N_CHIPS, GATE_RANK, LANES).transpose(1, 0, 2).reshape(GATE_RANK, HEADS * LANES),
                    ((0, LANES - GATE_RANK), (0, 0))).astype(BF16)
    ba_p = jnp.pad(gla_b_a.reshape(HEADS, 64), ((0, 0), (0, 64))).reshape(1, HEADS * LANES)
    fb = _forward_backward(xs, target, ffn_w, w_in_pt, w_out_full, wa2_p, ba_p, ffn1_norm_g, mix_norm_g, ret_norm_g,
                           gla_norm_g, ffn2_norm_g, final_norm_g.reshape(1, D_MODEL))
    (loss_blk, grad_x, g_gate1, g_up1, g_down1, g_gate2, g_up2, g_down2, g_win_p, g_wout_ret, g_wout_gla, g_wa2_p,
     d_ba_p, d_ffn1_g, d_mix_g, d_ffn2_g, d_final_g, d_ret_g, d_gla_g) = fb

    def by_halves(g):
        p, rows, cols = g.shape
        return g.reshape(p, 2, rows // 2, cols)

    g_win_t = _unpad_w_in_t(g_win_p[0])
    g_win = jnp.stack([pad_rows(g_win_t[IN_SHARD * p:IN_SHARD * (p + 1)]) for p in range(N_CHIPS)], axis=0)
    g_wout = jnp.concatenate([g_wout_ret[0], g_wout_gla[0]], axis=0).reshape(N_CHIPS, D_MODEL // N_CHIPS, D_MODEL)
    grads = [by_halves(g) for g in (g_gate1, g_up1, g_down1, g_gate2, g_up2, g_down2, g_win, g_wout)]
    recv = _pair_exchange(grads)
    sums = [_pair_add(g, r, c_arr, "pair_add_%d" % k) for k, (g, r) in enumerate(zip(grads, recv))]
    arrived = _chip_exchange(sums)
    mine = [_chip_sum(s, r, me_arr, "chip_sum_%d" % k) for k, (s, r) in enumerate(zip(sums, arrived))]
    other = _pair_share(mine)

    g_wa2 = g_wa2_p[0][0:GATE_RANK].reshape(GATE_RANK, HEADS, LANES)[:, :, 0:64].reshape(GATE_RANK, 256)
    d_ba = d_ba_p.reshape(HEADS, LANES)[:, 0:64].reshape(1, 256)
    loss_row = jnp.pad(loss_blk[0:1, 0:1], ((0, 0), (0, D_MODEL - 1)))
    small_local = _pack_small(d_ffn1_g, d_mix_g, d_ffn2_g, d_final_g, d_ret_g, d_gla_g, d_ba, g_wa2, 256, loss_row)
    small_sum = _small_allreduce(small_local)
    loss = small_sum[6, 0]
    sg = _unpack_small(small_sum, 256)
    wa2_grad = lax.dynamic_slice(sg[7], (0, 0, 64 * chip), (1, GATE_RANK, 64))
    small_g = _pack_small(*sg[:7], wa2_grad, 64)
    small_w = _pack_small(ffn1_norm_g, mix_norm_g, ffn2_norm_g, final_norm_g, ret_norm_g, gla_norm_g, gla_b_a, gla_w_a2, 64)
    small_m = _pack_small(m_ffn1_norm_g, m_mix_norm_g, m_ffn2_norm_g, m_final_norm_g, m_ret_norm_g, m_gla_norm_g,
                          m_gla_b_a, m_gla_w_a2, 64)
    small_v = _pack_small(v_ffn1_norm_g, v_mix_norm_g, v_ffn2_norm_g, v_final_norm_g, v_ret_norm_g, v_gla_norm_g,
                          v_gla_b_a, v_gla_w_a2, 64)
    small_out = _adamw(small_w, small_g, small_m, small_v, "adamw_small")
    s_grad = _unpack_small(small_g, 64)
    s_delta, s_m, s_v = (_unpack_small(o, 64) for o in small_out)

    def big(k, w, m, v, name, to_2d, from_2d):
        outs4 = _adamw_halves(to_2d(w), mine[k], other[k], to_2d(m), to_2d(v), c_arr, name)
        return [from_2d(z) for z in outs4]

    plain = (lambda w: w[0], lambda z: z[None])
    transposed = (lambda w: w[0].T, lambda z: z.T[None])
    in_proj = (lambda w: pad_rows(w[0].T), lambda z: z[0:IN_SHARD].T[None])
    r_g1 = big(0, ffn1_w_gate, m_ffn1_w_gate, v_ffn1_w_gate, "adamw_ffn1_gate", *transposed)
    r_u1 = big(1, ffn1_w_up, m_ffn1_w_up, v_ffn1_w_up, "adamw_ffn1_up", *transposed)
    r_d1 = big(2, ffn1_w_down, m_ffn1_w_down, v_ffn1_w_down, "adamw_ffn1_down", *plain)
    r_g2 = big(3, ffn2_w_gate, m_ffn2_w_gate, v_ffn2_w_gate, "adamw_ffn2_gate", *transposed)
    r_u2 = big(4, ffn2_w_up, m_ffn2_w_up, v_ffn2_w_up, "adamw_ffn2_up", *transposed)
    r_d2 = big(5, ffn2_w_down, m_ffn2_w_down, v_ffn2_w_down, "adamw_ffn2_down", *plain)
    r_in = big(6, w_in, m_w_in, v_w_in, "adamw_w_in", *in_proj)
    r_out = big(7, w_out, m_w_out, v_w_out, "adamw_w_out", *plain)

    def leaves(k, smalls):
        n1, nm, n2, nf, nret, ngla, ba, wa2 = smalls
        return [n1, r_g1[k], r_u1[k], r_d1[k], nm, r_in[k], nret, wa2, ba, ngla, r_out[k], n2, r_g2[k], r_u2[k], r_d2[k], nf]

    outs = [loss, grad_x.reshape(x.shape)]
    outs += leaves(0, s_grad) + leaves(1, s_delta) + leaves(2, s_m) + leaves(3, s_v)
    return tuple(outs)
```

```python
import functools

import jax
import jax.numpy as jnp
from jax import lax
from jax.experimental import pallas as pl
from jax.experimental.pallas import tpu as pltpu

F32, BF16 = jnp.float32, jnp.bfloat16
MESH = pl.DeviceIdType.MESH
ANY = pl.BlockSpec(memory_space=pl.ANY)

D_MODEL = 1024
D_FF = 2816
N_CHIPS = 4
FF_SHARD = D_FF // N_CHIPS
IN_WIDTH = 3088
IN_SHARD = IN_WIDTH // N_CHIPS
IN_ROWS = 800
CHUNK = 64
HEADS = 4
LANES = 128
HEAD_BLOCK = 4 * LANES
PROJ_P = 2 * HEADS * HEAD_BLOCK + LANES
GATE_RANK = 16
QK_SCALE = 0.125
GATE_NORM = 16.0
RMS_EPS = 1e-6
ROPE_BASE = 10000.0
ADAM_LR, ADAM_B1, ADAM_B2, ADAM_EPS, ADAM_WD, ADAM_STEP = 0.001, 0.9, 0.999, 1e-08, 0.01, 10
SMALL_ROWS = 32
TOKEN_TILE = 512
ATTN_TILE = 512

_ARB2 = pltpu.CompilerParams(dimension_semantics=("arbitrary", "arbitrary"))
_ARB1 = pltpu.CompilerParams(dimension_semantics=("arbitrary",))
_ARB3 = pltpu.CompilerParams(dimension_semantics=("arbitrary", "arbitrary", "arbitrary"))


def _dot(a, b):
    return jnp.dot(a, b, preferred_element_type=F32)


def _dot_nt(a, b):
    return lax.dot_general(a, b, (((1,), (1,)), ((), ())), preferred_element_type=F32)


def _dot_tn(a, b):
    return lax.dot_general(a, b, (((0,), (0,)), ((), ())), preferred_element_type=F32)


def _rms_scale(xv):
    return lax.rsqrt(jnp.mean(xv * xv, axis=-1, keepdims=True) + RMS_EPS)


def _rms_bwd(dh, xv, g):
    r = _rms_scale(xv)
    xhat = xv * r
    dxhat = dh * g
    dx = r * (dxhat - xhat * jnp.mean(dxhat * xhat, axis=-1, keepdims=True))
    return dx, jnp.sum(dh * xhat, axis=0, keepdims=True)


def _silu_grad(a, sg):
    return sg * (1.0 + a * (1.0 - sg))


def _ffn_weight_specs(which):
    blk = (None, None, FF_SHARD, D_MODEL)
    return [pl.BlockSpec(blk, lambda i, j, k=3 * which + kind: (j, k, 0, 0)) for kind in range(3)]


def _ffn_fwd(x, g, ffn_w, which, name):
    t = x.shape[0]
    tm = min(t, TOKEN_TILE)

    def body(x_ref, g_ref, wg_ref, wu_ref, wd_ref, xo_ref, a_ref, u_ref, h_ref, acc_ref):
        j = pl.program_id(1)

        @pl.when(j == 0)
        def _():
            xv = x_ref[...]
            h_ref[...] = ((xv * _rms_scale(xv)) * g_ref[...]).astype(BF16)
            acc_ref[...] = jnp.zeros_like(acc_ref)

        h = h_ref[...]
        a = _dot_nt(h, wg_ref[...])
        u = _dot_nt(h, wu_ref[...])
        a_ref[...] = a.astype(BF16)
        u_ref[...] = u.astype(BF16)
        hid = (a * jax.nn.sigmoid(a)) * u
        acc_ref[...] += _dot(hid.astype(BF16), wd_ref[...])

        @pl.when(j == N_CHIPS - 1)
        def _():
            xo_ref[...] = x_ref[...] + 0.5 * acc_ref[...]

    tok = pl.BlockSpec((tm, D_MODEL), lambda i, j: (i, 0))
    act = pl.BlockSpec((None, tm, FF_SHARD), lambda i, j: (j, i, 0))
    return pl.pallas_call(
        body, name=name, grid=(t // tm, N_CHIPS),
        in_specs=[tok, pl.BlockSpec((1, D_MODEL), lambda i, j: (0, 0))] + _ffn_weight_specs(which),
        out_specs=[tok, act, act, tok],
        out_shape=[jax.ShapeDtypeStruct((t, D_MODEL), F32),
                   jax.ShapeDtypeStruct((N_CHIPS, t, FF_SHARD), BF16),
                   jax.ShapeDtypeStruct((N_CHIPS, t, FF_SHARD), BF16),
                   jax.ShapeDtypeStruct((t, D_MODEL), BF16)],
        scratch_shapes=[pltpu.VMEM((tm, D_MODEL), F32)],
        compiler_params=_ARB2,
    )(x, g, ffn_w, ffn_w, ffn_w)


def _ffn_bwd(dxo, x, g, a4, u4, ffn_w, which, name):
    t = x.shape[0]
    tm = min(t, TOKEN_TILE)

    def body(dxo_ref, x_ref, g_ref, a_ref, u_ref, wg_ref, wu_ref, wd_ref,
             da_ref, du_ref, hid_ref, dob_ref, dx_ref, dg_ref, acc_ref):
        i, j = pl.program_id(0), pl.program_id(1)

        @pl.when(j == 0)
        def _():
            dob_ref[...] = (0.5 * dxo_ref[...]).astype(BF16)
            acc_ref[...] = jnp.zeros_like(acc_ref)

        @pl.when((i == 0) & (j == 0))
        def _():
            dg_ref[...] = jnp.zeros_like(dg_ref)

        dhid = _dot_nt(dob_ref[...], wd_ref[...])
        a = a_ref[...].astype(F32)
        u = u_ref[...].astype(F32)
        sg = jax.nn.sigmoid(a)
        s = a * sg
        hid_ref[...] = (s * u).astype(BF16)
        du = (dhid * s).astype(BF16)
        da = (dhid * u * _silu_grad(a, sg)).astype(BF16)
        du_ref[...] = du
        da_ref[...] = da
        acc_ref[...] += _dot(da, wg_ref[...]) + _dot(du, wu_ref[...])

        @pl.when(j == N_CHIPS - 1)
        def _():
            dx, dg = _rms_bwd(acc_ref[...], x_ref[...], g_ref[...])
            dx_ref[...] = dxo_ref[...] + dx
            dg_ref[...] += dg

    tok = pl.BlockSpec((tm, D_MODEL), lambda i, j: (i, 0))
    act = pl.BlockSpec((None, tm, FF_SHARD), lambda i, j: (j, i, 0))
    row = pl.BlockSpec((1, D_MODEL), lambda i, j: (0, 0))
    act_shape = jax.ShapeDtypeStruct((N_CHIPS, t, FF_SHARD), BF16)
    return pl.pallas_call(
        body, name=name, grid=(t // tm, N_CHIPS),
        in_specs=[tok, tok, row, act, act] + _ffn_weight_specs(which),
        out_specs=[act, act, act, tok, tok, row],
        out_shape=[act_shape, act_shape, act_shape,
                   jax.ShapeDtypeStruct((t, D_MODEL), BF16),
                   jax.ShapeDtypeStruct((t, D_MODEL), F32),
                   jax.ShapeDtypeStruct((1, D_MODEL), F32)],
        scratch_shapes=[pltpu.VMEM((tm, D_MODEL), F32)],
        compiler_params=_ARB2,
    )(dxo, x, g, a4, u4, ffn_w, ffn_w, ffn_w)


def _matmul_tn(a, b, name, tka=None):
    a3, b3 = a.ndim == 3, b.ndim == 3
    nb = a.shape[0] if a3 else (b.shape[0] if b3 else 1)
    t, ka, n = a.shape[-2], a.shape[-1], b.shape[-1]
    tka = ka if tka is None else tka
    tk = min(t, TOKEN_TILE)

    def body(a_ref, b_ref, o_ref):
        @pl.when(pl.program_id(2) == 0)
        def _():
            o_ref[...] = jnp.zeros_like(o_ref)

        o_ref[...] += _dot_tn(a_ref[...].astype(BF16), b_ref[...].astype(BF16))

    a_spec = (pl.BlockSpec((None, tk, tka), lambda i, j, k: (i, k, j)) if a3
              else pl.BlockSpec((tk, tka), lambda i, j, k: (k, j)))
    b_spec = (pl.BlockSpec((None, tk, n), lambda i, j, k: (i, k, 0)) if b3
              else pl.BlockSpec((tk, n), lambda i, j, k: (k, 0)))
    return pl.pallas_call(
        body, name=name, grid=(nb, ka // tka, t // tk),
        in_specs=[a_spec, b_spec],
        out_specs=pl.BlockSpec((None, tka, n), lambda i, j, k: (i, j, 0)),
        out_shape=jax.ShapeDtypeStruct((nb, ka, n), F32),
        compiler_params=_ARB3,
    )(a, b)


def _matmul_nt(a, w, name, out_dtype=F32):
    t, k = a.shape
    n = w.shape[0]
    tm = min(t, TOKEN_TILE)

    def body(a_ref, w_ref, o_ref):
        o_ref[...] = _dot_nt(a_ref[...].astype(BF16), w_ref[...]).astype(out_dtype)

    return pl.pallas_call(
        body, name=name, grid=(t // tm,),
        in_specs=[pl.BlockSpec((tm, k), lambda i: (i, 0)), pl.BlockSpec((n, k), lambda i: (0, 0))],
        out_specs=pl.BlockSpec((tm, n), lambda i: (i, 0)),
        out_shape=jax.ShapeDtypeStruct((t, n), out_dtype),
        compiler_params=_ARB1,
    )(a, w)


def _mixer_in_bwd(dproj, w_in_pt, dres, x, g, name):
    t, k = dproj.shape
    tm = min(t, TOKEN_TILE)

    def body(a_ref, w_ref, dres_ref, x_ref, g_ref, dx_ref, dg_ref):
        @pl.when(pl.program_id(0) == 0)
        def _():
            dg_ref[...] = jnp.zeros_like(dg_ref)

        dh = _dot(a_ref[...], w_ref[...])
        dx, dg = _rms_bwd(dh, x_ref[...], g_ref[...])
        dx_ref[...] = dres_ref[...] + dx
        dg_ref[...] += dg

    tok = pl.BlockSpec((tm, D_MODEL), lambda i: (i, 0))
    row = pl.BlockSpec((1, D_MODEL), lambda i: (0, 0))
    return pl.pallas_call(
        body, name=name, grid=(t // tm,),
        in_specs=[pl.BlockSpec((tm, k), lambda i: (i, 0)), pl.BlockSpec((k, D_MODEL), lambda i: (0, 0)), tok, tok, row],
        out_specs=[tok, row],
        out_shape=[jax.ShapeDtypeStruct((t, D_MODEL), F32), jax.ShapeDtypeStruct((1, D_MODEL), F32)],
        compiler_params=_ARB1,
    )(dproj, w_in_pt, dres, x, g)


def _mixer_in_fwd(x, g, w_in_pt, name):
    t = x.shape[0]
    tm = min(t, TOKEN_TILE)
    tn = PROJ_P // 3

    def body(x_ref, g_ref, w_ref, p_ref, h_ref):
        @pl.when(pl.program_id(1) == 0)
        def _():
            xv = x_ref[...]
            h_ref[...] = ((xv * _rms_scale(xv)) * g_ref[...]).astype(BF16)

        p_ref[...] = _dot_nt(h_ref[...], w_ref[...])

    tok = pl.BlockSpec((tm, D_MODEL), lambda i, j: (i, 0))
    return pl.pallas_call(
        body, name=name, grid=(t // tm, 3),
        in_specs=[tok, pl.BlockSpec((1, D_MODEL), lambda i, j: (0, 0)),
                  pl.BlockSpec((tn, D_MODEL), lambda i, j: (j, 0))],
        out_specs=[pl.BlockSpec((tm, tn), lambda i, j: (i, j)), tok],
        out_shape=[jax.ShapeDtypeStruct((t, PROJ_P), F32), jax.ShapeDtypeStruct((t, D_MODEL), BF16)],
        compiler_params=_ARB2,
    )(x, g, w_in_pt)


def _mixer_out_fwd(o_ret, o_gla, w_out, x, name):
    t = x.shape[0]
    tm = min(t, TOKEN_TILE)
    half = HEADS * LANES

    def body(a_ref, b_ref, w_ref, x_ref, o_ref):
        o_ref[...] = x_ref[...] + _dot(a_ref[...], w_ref[0:half, :]) + _dot(b_ref[...], w_ref[half:2 * half, :])

    tok = pl.BlockSpec((tm, D_MODEL), lambda i: (i, 0))
    hb = pl.BlockSpec((tm, half), lambda i: (i, 0))
    return pl.pallas_call(
        body, name=name, grid=(t // tm,),
        in_specs=[hb, hb, pl.BlockSpec((2 * half, D_MODEL), lambda i: (0, 0)), tok],
        out_specs=tok, out_shape=jax.ShapeDtypeStruct((t, D_MODEL), F32),
        compiler_params=_ARB1,
    )(o_ret, o_gla, w_out, x)


def _final_loss(x, g, target, name):
    t = x.shape[0]
    tm = min(t, TOKEN_TILE)

    def body(x_ref, g_ref, t_ref, l_ref, dx_ref, dg_ref):
        @pl.when(pl.program_id(0) == 0)
        def _():
            l_ref[...] = jnp.zeros_like(l_ref)
            dg_ref[...] = jnp.zeros_like(dg_ref)

        xv = x_ref[...]
        gv = g_ref[...]
        err = (xv * _rms_scale(xv)) * gv - t_ref[...]
        l_ref[...] += 0.5 * jnp.sum(jnp.mean(err * err, axis=-1, keepdims=True), axis=0, keepdims=True)
        dx, dg = _rms_bwd(err * (1.0 / D_MODEL), xv, gv)
        dx_ref[...] = dx
        dg_ref[...] += dg

    tok = pl.BlockSpec((tm, D_MODEL), lambda i: (i, 0))
    row = pl.BlockSpec((1, D_MODEL), lambda i: (0, 0))
    return pl.pallas_call(
        body, name=name, grid=(t // tm,),
        in_specs=[tok, row, tok],
        out_specs=[pl.BlockSpec((8, LANES), lambda i: (0, 0)), tok, row],
        out_shape=[jax.ShapeDtypeStruct((8, LANES), F32), jax.ShapeDtypeStruct((t, D_MODEL), F32),
                   jax.ShapeDtypeStruct((1, D_MODEL), F32)],
        compiler_params=_ARB1,
    )(x, g, target)


def _rot(v, cos, sa, sb):
    return v * cos + pltpu.roll(v, 96, 1) * sa + pltpu.roll(v, 32, 1) * sb


def _rot_t(d, cos, sa, sb):
    return d * cos + pltpu.roll(d * sa, 32, 1) + pltpu.roll(d * sb, 96, 1)


def _bmm(a, b):
    return jnp.einsum("cik,ckj->cij", a, b, preferred_element_type=F32)


def _bmm_nt(a, b):
    return jnp.einsum("cik,cjk->cij", a, b, preferred_element_type=F32)


def _bmm_tn(a, b):
    return jnp.einsum("cki,ckj->cij", a, b, preferred_element_type=F32)


def _masked_sum(mask, x):
    hi = x.astype(BF16)
    r1 = x - hi.astype(F32)
    mid = r1.astype(BF16)
    lo = (r1 - mid.astype(F32)).astype(BF16)
    return _bmm(mask, hi) + _bmm(mask, mid) + _bmm(mask, lo)


def _tile_inputs(is_ret, proj_ref, aux, nc):
    shape3 = (nc, CHUNK, LANES)
    q_raw = proj_ref[:, 0:LANES]
    k_raw = proj_ref[:, LANES:2 * LANES]
    v = proj_ref[:, 2 * LANES:3 * LANES]
    gate = proj_ref[:, 3 * LANES:4 * LANES]
    ri = lax.broadcasted_iota(jnp.int32, (nc, CHUNK, CHUNK), 1)
    ci = lax.broadcasted_iota(jnp.int32, (nc, CHUNK, CHUNK), 2)
    if is_ret:
        cos_ref, sa_ref, sb_ref, lg_ref = aux
        cos, sa, sb = cos_ref[...], sa_ref[...], sb_ref[...]
        q = _rot(q_raw, cos, sa, sb)
        k = _rot(k_raw, cos, sa, sb) * QK_SCALE
        steps = (lax.broadcasted_iota(jnp.int32, shape3, 1) + 1).astype(F32)
        b = steps * lg_ref[...]
        logit = None
    else:
        glow_ref, wa2_ref, ba_ref = aux
        logit = _dot(glow_ref[...].astype(BF16), wa2_ref[...]) + ba_ref[...]
        la = (jnp.minimum(logit, 0.0) - jnp.log1p(jnp.exp(-jnp.abs(logit)))) * (1.0 / GATE_NORM)
        b = _masked_sum((ci <= ri).astype(BF16), la.reshape(shape3))
        q = q_raw * QK_SCALE
        k = k_raw
    return q.reshape(shape3), k.reshape(shape3), v.reshape(shape3), gate, b, logit, ri, ci


def _tile_scores(q, k, b, ri, ci):
    mid = b[:, CHUNK // 2 - 1:CHUNK // 2, :]
    ep = jnp.exp(b - mid)
    en = jnp.exp(mid - b)
    qt, kt, qh, kh = q * ep, k * en, q * en, k * ep
    low = _bmm_nt(qt.astype(BF16), kt.astype(BF16))
    upp = _bmm_nt(qh.astype(BF16), kh.astype(BF16))
    scores = jnp.where(ci <= ri, low, upp)
    return scores, ep, en, qt, kt, qh, kh


def _attn_specs(is_ret, t, tb, imap_t):
    nb = t // tb
    base = 0 if is_ret else HEADS
    proj = pl.BlockSpec((tb, HEAD_BLOCK), lambda h, i: (imap_t(i), base + h))
    lane_t = pl.BlockSpec((tb, LANES), lambda h, i: (imap_t(i), 0))
    if is_ret:
        aux = [lane_t, lane_t, lane_t, pl.BlockSpec((None, 1, LANES), lambda h, i: (h, 0, 0))]
    else:
        aux = [pl.BlockSpec((tb, LANES), lambda h, i: (imap_t(i), PROJ_P // LANES - 1)),
               pl.BlockSpec((LANES, LANES), lambda h, i: (0, h)),
               pl.BlockSpec((1, LANES), lambda h, i: (0, h))]
    gain = pl.BlockSpec((1, LANES), lambda h, i: (0, h))
    head_t = pl.BlockSpec((tb, LANES), lambda h, i: (imap_t(i), h))
    state = pl.BlockSpec((None, tb // CHUNK, LANES, LANES), lambda h, i: (h, imap_t(i), 0, 0))
    return nb, proj, aux, gain, head_t, state


def _attn_fwd(is_ret, proj, aux_arrays, gain, name):
    t = proj.shape[0]
    tb = min(t, ATTN_TILE)
    nc = tb // CHUNK
    n_aux = 4 if is_ret else 3
    nb, proj_spec, aux_specs, gain_spec, head_t, state_spec = _attn_specs(is_ret, t, tb, lambda i: i)

    def body(*refs):
        proj_ref = refs[0]
        aux = refs[1:1 + n_aux]
        gn_ref, ofin_ref, oraw_ref, st_ref, state = refs[1 + n_aux:]

        @pl.when(pl.program_id(1) == 0)
        def _():
            state[...] = jnp.zeros_like(state)

        q, k, v, gate, b, _, ri, ci = _tile_inputs(is_ret, proj_ref, aux, nc)
        scores = _tile_scores(q, k, b, ri, ci)[0]
        vb = v.astype(BF16)
        intra = _bmm(scores.astype(BF16), vb)
        b_last = b[:, CHUNK - 1:CHUNK, :]
        e_last = jnp.exp(b_last)
        grow = _bmm_tn(vb, (k * jnp.exp(b_last - b)).astype(BF16))
        st = state[...]
        for c in range(nc):
            st_ref[c] = st
            st = st * e_last[c] + grow[c]
        state[...] = st
        inter = _bmm_nt((q * jnp.exp(b)).astype(BF16), st_ref[...].astype(BF16))
        out = (intra + inter).reshape(tb, LANES)
        oraw_ref[...] = out
        normed = out * _rms_scale(out)
        ofin_ref[...] = ((normed * gn_ref[...]) * (gate * jax.nn.sigmoid(gate))).astype(BF16)

    width = HEADS * LANES
    return pl.pallas_call(
        body, name=name, grid=(HEADS, nb),
        in_specs=[proj_spec] + aux_specs + [gain_spec],
        out_specs=[head_t, head_t, state_spec],
        out_shape=[jax.ShapeDtypeStruct((t, width), BF16), jax.ShapeDtypeStruct((t, width), F32),
                   jax.ShapeDtypeStruct((HEADS, t // CHUNK, LANES, LANES), F32)],
        scratch_shapes=[pltpu.VMEM((LANES, LANES), F32)],
        compiler_params=_ARB2,
    )(proj, *aux_arrays, gain)


def _attn_bwd(is_ret, proj, aux_arrays, gain, o_raw, states, d_out, name):
    t = proj.shape[0]
    tb = min(t, ATTN_TILE)
    nc = tb // CHUNK
    n_aux = 4 if is_ret else 3
    nblk = t // tb
    nb, proj_spec, aux_specs, gain_spec, head_t, state_spec = _attn_specs(is_ret, t, tb, lambda i: nblk - 1 - i)
    base = 0 if is_ret else HEADS
    dout_spec = pl.BlockSpec((tb, LANES), lambda h, i: (nblk - 1 - i, base + h))

    def body(*refs):
        proj_ref = refs[0]
        aux = refs[1:1 + n_aux]
        gn_ref, oraw_ref, st_ref, dfin_ref = refs[1 + n_aux:5 + n_aux]
        if is_ret:
            dproj_ref, dgn_ref, dstate, dafter_ref = refs[5 + n_aux:]
        else:
            dproj_ref, dgn_ref, dlogit_ref, dba_ref, dstate, dafter_ref = refs[5 + n_aux:]

        @pl.when(pl.program_id(1) == 0)
        def _():
            dstate[...] = jnp.zeros_like(dstate)
            dgn_ref[...] = jnp.zeros_like(dgn_ref)
            if not is_ret:
                dba_ref[...] = jnp.zeros_like(dba_ref)

        shape3 = (nc, CHUNK, LANES)
        q, k, v, gate, b, logit, ri, ci = _tile_inputs(is_ret, proj_ref, aux, nc)
        scores, ep, en, qt, kt, qh, kh = _tile_scores(q, k, b, ri, ci)
        eb = jnp.exp(b)
        qe = q * eb
        b_last = b[:, CHUNK - 1:CHUNK, :]
        e_last = jnp.exp(b_last)
        ekd = jnp.exp(b_last - b)
        kd = k * ekd

        gn = gn_ref[...]
        out = oraw_ref[...]
        r = _rms_scale(out)
        normed = out * r
        sg = jax.nn.sigmoid(gate)
        dfin = dfin_ref[...]
        dgate = dfin * (normed * gn) * _silu_grad(gate, sg)
        dpre = dfin * (gate * sg)
        dgn_ref[...] += jnp.sum(dpre * normed, axis=0, keepdims=True)
        dnormed = dpre * gn
        d_o = r * (dnormed - normed * jnp.mean(dnormed * normed, axis=-1, keepdims=True))
        dob, vb = d_o.reshape(shape3).astype(BF16), v.astype(BF16)

        dgrow = _bmm_tn(dob, qe.astype(BF16))
        dst = dstate[...]
        for c in reversed(range(nc)):
            dafter_ref[c] = dst
            dst = dst * e_last[c] + dgrow[c]
        dstate[...] = dst
        st = st_ref[...]
        dafter = dafter_ref[...]
        stb, dafter_b = st.astype(BF16), dafter.astype(BF16)

        qtb, ktb, qhb, khb = qt.astype(BF16), kt.astype(BF16), qh.astype(BF16), kh.astype(BF16)
        scores_t = jnp.where(ci >= ri, _bmm_nt(ktb, qtb), _bmm_nt(khb, qhb))
        dv = _bmm(scores_t.astype(BF16), dob) + _bmm_nt(kd.astype(BF16), dafter_b)
        dsc = _bmm_nt(dob, vb)
        dsc_t = _bmm_nt(vb, dob)
        dqe = _bmm(dob, stb)
        dkd = _bmm(vb, dafter_b)
        dqt = _bmm(jnp.where(ci <= ri, dsc, 0.0).astype(BF16), ktb)
        dqh = _bmm(jnp.where(ci <= ri, 0.0, dsc).astype(BF16), khb)
        dkt = _bmm(jnp.where(ci >= ri, dsc_t, 0.0).astype(BF16), qtb)
        dkh = _bmm(jnp.where(ci >= ri, 0.0, dsc_t).astype(BF16), qhb)
        dq = (dqt * ep + dqh * en + dqe * eb).reshape(tb, LANES)
        dk = (dkt * en + dkh * ep + dkd * ekd).reshape(tb, LANES)

        if is_ret:
            cos_ref, sa_ref, sb_ref, _ = aux
            cos, sa, sb = cos_ref[...], sa_ref[...], sb_ref[...]
            dq_raw = _rot_t(dq, cos, sa, sb)
            dk_raw = _rot_t(dk, cos, sa, sb) * QK_SCALE
        else:
            dq_raw = dq * QK_SCALE
            dk_raw = dk
            db = dqt * qt - dkt * kt - dqh * qh + dkh * kh + dqe * qe - dkd * kd
            db_last = (jnp.sum(dkd * kd, axis=1, keepdims=True)
                       + jnp.sum(dafter * st, axis=1, keepdims=True) * e_last)
            last_row = lax.broadcasted_iota(jnp.int32, shape3, 1) == CHUNK - 1
            db = db + jnp.where(last_row, db_last, 0.0)
            dla = _masked_sum((ci >= ri).astype(BF16), db).reshape(tb, LANES)
            dlogit = dla * (1.0 / GATE_NORM) * jax.nn.sigmoid(-logit)
            dlogit_ref[...] = dlogit.astype(BF16)
            dba_ref[...] += jnp.sum(dlogit, axis=0, keepdims=True)

        dproj_ref[:, 0:LANES] = dq_raw.astype(BF16)
        dproj_ref[:, LANES:2 * LANES] = dk_raw.astype(BF16)
        dproj_ref[:, 2 * LANES:3 * LANES] = dv.reshape(tb, LANES).astype(BF16)
        dproj_ref[:, 3 * LANES:4 * LANES] = dgate.astype(BF16)

    width = HEADS * LANES
    row_out = pl.BlockSpec((1, LANES), lambda h, i: (0, h))
    out_specs = [pl.BlockSpec((tb, HEAD_BLOCK), lambda h, i: (nblk - 1 - i, h)), row_out]
    out_shape = [jax.ShapeDtypeStruct((t, HEADS * HEAD_BLOCK), BF16), jax.ShapeDtypeStruct((1, width), F32)]
    if not is_ret:
        out_specs += [head_t, row_out]
        out_shape += [jax.ShapeDtypeStruct((t, width), BF16), jax.ShapeDtypeStruct((1, width), F32)]
    return pl.pallas_call(
        body, name=name, grid=(HEADS, nblk),
        in_specs=[proj_spec] + aux_specs + [gain_spec, head_t, state_spec, dout_spec],
        out_specs=out_specs, out_shape=out_shape,
        scratch_shapes=[pltpu.VMEM((LANES, LANES), F32), pltpu.VMEM((nc, LANES, LANES), F32)],
        compiler_params=_ARB2,
    )(proj, *aux_arrays, gain, o_raw, states, d_out)


def _place():
    x, y, c = lax.axis_index("x"), lax.axis_index("y"), lax.axis_index("c")
    chips = [(1 - x, y), (x, 1 - y), (1 - x, 1 - y)]
    return x, y, c, 2 * x + y, chips


def _gather_weights(arrs):
    na = len(arrs)

    def body(*refs):
        ins, outs = refs[:na], refs[na:2 * na]
        send_sems, recv_sems = refs[2 * na:]
        x, y, c, me, chips = _place()
        sibling = (x, y, 1 - c)

        def ici(a, j, src_chip, to):
            return pltpu.make_async_remote_copy(
                src_ref=ins[a].at[:, c], dst_ref=outs[a].at[src_chip, :, c],
                send_sem=send_sems.at[6 * a + j], recv_sem=recv_sems.at[6 * a + j], device_id=to, device_id_type=MESH)

        def d2d(a, j, src_chip, half):
            blk = outs[a].at[src_chip, :, half]
            return pltpu.make_async_remote_copy(
                src_ref=blk, dst_ref=blk, send_sem=send_sems.at[6 * a + 3 + j], recv_sem=recv_sems.at[6 * a + 3 + j],
                device_id=sibling, device_id_type=MESH)

        sends = [ici(a, j, me, (px, py, c)) for a in range(na) for j, (px, py) in enumerate(chips)]
        for cp in sends:
            cp.start()
        passed = []
        for a in range(na):
            for j, (px, py) in enumerate(chips):
                ici(a, j, 2 * px + py, (px, py, c)).wait_recv()
                cp = d2d(a, j, 2 * px + py, c)
                cp.start()
                passed.append(cp)
        for a in range(na):
            for j, (px, py) in enumerate(chips):
                d2d(a, j, 2 * px + py, 1 - c).wait_recv()
        for cp in sends + passed:
            cp.wait_send()

    return pl.pallas_call(
        body, name="gather_weights",
        in_specs=[ANY] * na, out_specs=[ANY] * na,
        out_shape=[jax.ShapeDtypeStruct((N_CHIPS,) + a.shape, a.dtype) for a in arrs],
        scratch_shapes=[pltpu.SemaphoreType.DMA((6 * na,)), pltpu.SemaphoreType.DMA((6 * na,))],
    )(*arrs)


def _pair_exchange(grads):
    na = len(grads)

    def body(*refs):
        ins, outs = refs[:na], refs[na:2 * na]
        send_sems, recv_sems = refs[2 * na:]
        x, y, c, _, _ = _place()
        copies = [pltpu.make_async_remote_copy(
            src_ref=ins[a].at[:, 1 - c], dst_ref=outs[a], send_sem=send_sems.at[a], recv_sem=recv_sems.at[a],
            device_id=(x, y, 1 - c), device_id_type=MESH) for a in range(na)]
        for cp in copies:
            cp.start()
        for cp in copies:
            cp.wait()

    return pl.pallas_call(
        body, name="pair_exchange",
        in_specs=[ANY] * na, out_specs=[ANY] * na,
        out_shape=[jax.ShapeDtypeStruct(g.shape[:1] + g.shape[2:], g.dtype) for g in grads],
        scratch_shapes=[pltpu.SemaphoreType.DMA((na,)), pltpu.SemaphoreType.DMA((na,))],
    )(*grads)


def _pair_add(grad, recv, c_arr, name):
    _, _, r, cols = grad.shape

    def body(c_ref, g_ref, r_ref, o_ref):
        o_ref[...] = (g_ref[...] + r_ref[...]).astype(BF16)

    return pl.pallas_call(
        body, name=name,
        grid_spec=pltpu.PrefetchScalarGridSpec(
            num_scalar_prefetch=1, grid=(N_CHIPS,),
            in_specs=[pl.BlockSpec((None, None, r, cols), lambda p, c_ref: (p, c_ref[0], 0, 0)),
                      pl.BlockSpec((None, r, cols), lambda p, c_ref: (p, 0, 0))],
            out_specs=pl.BlockSpec((None, r, cols), lambda p, c_ref: (p, 0, 0))),
        out_shape=jax.ShapeDtypeStruct((N_CHIPS, r, cols), BF16),
        compiler_params=_ARB1,
    )(c_arr, grad, recv)


def _chip_exchange(sums):
    na = len(sums)

    def body(*refs):
        ins, outs = refs[:na], refs[na:2 * na]
        send_sems, recv_sems = refs[2 * na:]
        x, y, c, me, chips = _place()
        sends = [pltpu.make_async_remote_copy(
            src_ref=ins[a].at[2 * px + py], dst_ref=outs[a].at[me],
            send_sem=send_sems.at[3 * a + j], recv_sem=recv_sems.at[3 * a + j],
            device_id=(px, py, c), device_id_type=MESH) for a in range(na) for j, (px, py) in enumerate(chips)]
        for cp in sends:
            cp.start()
        for a in range(na):
            for j, (px, py) in enumerate(chips):
                pltpu.make_async_remote_copy(
                    src_ref=ins[a].at[me], dst_ref=outs[a].at[2 * px + py],
                    send_sem=send_sems.at[3 * a + j], recv_sem=recv_sems.at[3 * a + j],
                    device_id=(px, py, c), device_id_type=MESH).wait_recv()
        for cp in sends:
            cp.wait_send()

    return pl.pallas_call(
        body, name="chip_exchange",
        in_specs=[ANY] * na, out_specs=[ANY] * na,
        out_shape=[jax.ShapeDtypeStruct(s.shape, s.dtype) for s in sums],
        scratch_shapes=[pltpu.SemaphoreType.DMA((3 * na,)), pltpu.SemaphoreType.DMA((3 * na,))],
    )(*sums)


def _chip_sum(own, recv, me_arr, name):
    _, r, cols = recv.shape

    def body(me_ref, own_ref, r_ref, o_ref):
        o_ref[...] = jnp.zeros_like(o_ref)
        for q in range(N_CHIPS):
            @pl.when(me_ref[0] == q)
            def _():
                o_ref[...] += own_ref[...].astype(F32)

            @pl.when(me_ref[0] != q)
            def _():
                o_ref[...] += r_ref[q].astype(F32)

    return pl.pallas_call(
        body, name=name,
        grid_spec=pltpu.PrefetchScalarGridSpec(
            num_scalar_prefetch=1, grid=(1,),
            in_specs=[pl.BlockSpec((None, r, cols), lambda i, me_ref: (me_ref[0], 0, 0)),
                      pl.BlockSpec((N_CHIPS, r, cols), lambda i, me_ref: (0, 0, 0))],
            out_specs=pl.BlockSpec((r, cols), lambda i, me_ref: (0, 0))),
        out_shape=jax.ShapeDtypeStruct((r, cols), F32),
        compiler_params=_ARB1,
    )(me_arr, own, recv)


def _pair_share(halves):
    na = len(halves)

    def body(*refs):
        ins, outs = refs[:na], refs[na:2 * na]
        send_sems, recv_sems = refs[2 * na:]
        x, y, c, _, _ = _place()
        copies = [pltpu.make_async_remote_copy(
            src_ref=ins[a], dst_ref=outs[a], send_sem=send_sems.at[a], recv_sem=recv_sems.at[a],
            device_id=(x, y, 1 - c), device_id_type=MESH) for a in range(na)]
        for cp in copies:
            cp.start()
        for cp in copies:
            cp.wait()

    return pl.pallas_call(
        body, name="pair_share",
        in_specs=[ANY] * na, out_specs=[ANY] * na,
        out_shape=[jax.ShapeDtypeStruct(h.shape, h.dtype) for h in halves],
        scratch_shapes=[pltpu.SemaphoreType.DMA((na,)), pltpu.SemaphoreType.DMA((na,))],
    )(*halves)


def _small_allreduce(block):
    m, n = block.shape

    def body(x_ref, all_ref, sum_ref, send_sems, recv_sems, local_sem):
        x, y, c, _, chips = _place()
        me, sibling = (x, y, c), (x, y, 1 - c)

        def rows(px, py, pc):
            return all_ref.at[pl.ds((4 * px + 2 * py + pc) * m, m), :]

        def copy(k, blk, to, src=None):
            return pltpu.make_async_remote_copy(
                src_ref=rows(*blk) if src is None else src, dst_ref=rows(*blk),
                send_sem=send_sems.at[k], recv_sem=recv_sems.at[k], device_id=to, device_id_type=MESH)

        mine = pltpu.make_async_copy(x_ref, rows(*me), local_sem)
        mine.start()
        first = [copy(0, me, sibling, src=x_ref)]
        first += [copy(1 + j, me, (*chip, c), src=x_ref) for j, chip in enumerate(chips)]
        for cp in first:
            cp.start()
        passed = [copy(4 + j, (*chip, c), sibling) for j, chip in enumerate(chips)]
        for j, chip in enumerate(chips):
            copy(1 + j, (*chip, c), me).wait_recv()
            passed[j].start()
        copy(0, sibling, me).wait_recv()
        for j, chip in enumerate(chips):
            copy(4 + j, (*chip, 1 - c), me).wait_recv()
        for cp in first + passed:
            cp.wait_send()
        mine.wait()
        acc = all_ref[0:m, :]
        for d in range(1, 8):
            acc = acc + all_ref[d * m:(d + 1) * m, :]
        sum_ref[...] = acc

    vmem = pl.BlockSpec(memory_space=pltpu.VMEM)
    return pl.pallas_call(
        body, name="small_allreduce",
        in_specs=[vmem], out_specs=[vmem, vmem],
        out_shape=[jax.ShapeDtypeStruct((8 * m, n), F32), jax.ShapeDtypeStruct((m, n), F32)],
        scratch_shapes=[pltpu.SemaphoreType.DMA((7,)), pltpu.SemaphoreType.DMA((7,)), pltpu.SemaphoreType.DMA],
    )(block)[1]


def _row_tile(rows):
    best = rows
    for cand in range(8, min(rows, 512) + 1, 8):
        if rows % cand == 0:
            best = cand
    return best


def _adamw_math(w, g, m, v):
    m2 = ADAM_B1 * m + (1.0 - ADAM_B1) * g
    v2 = ADAM_B2 * v + (1.0 - ADAM_B2) * (g * g)
    m_hat = m2 / (1.0 - ADAM_B1 ** ADAM_STEP)
    v_hat = v2 / (1.0 - ADAM_B2 ** ADAM_STEP)
    return -ADAM_LR * (m_hat / (jnp.sqrt(v_hat) + ADAM_EPS) + ADAM_WD * w), m2, v2


def _adamw_halves(w, g_mine, g_other, m, v, c_arr, name):
    rows, cols = w.shape
    r = rows // 2
    tr = _row_tile(r)
    nt = r // tr

    def body(c_ref, w_ref, gm_ref, go_ref, m_ref, v_ref, g_ref, d_ref, nm_ref, nv_ref):
        gv = jnp.where(pl.program_id(0) == c_ref[0], gm_ref[...], go_ref[...])
        g_ref[...] = gv
        d_ref[...], nm_ref[...], nv_ref[...] = _adamw_math(w_ref[...], gv, m_ref[...], v_ref[...])

    full = pl.BlockSpec((tr, cols), lambda h, i, c_ref: (h * nt + i, 0))
    half = pl.BlockSpec((tr, cols), lambda h, i, c_ref: (i, 0))
    shape = jax.ShapeDtypeStruct((rows, cols), F32)
    return pl.pallas_call(
        body, name=name,
        grid_spec=pltpu.PrefetchScalarGridSpec(
            num_scalar_prefetch=1, grid=(2, nt),
            in_specs=[full, half, half, full, full], out_specs=[full] * 4),
        out_shape=[shape] * 4,
        compiler_params=_ARB2,
    )(c_arr, w, g_mine, g_other, m, v)


def _adamw(w, g, m, v, name):
    rows, cols = w.shape
    tr = _row_tile(rows)

    def body(w_ref, g_ref, m_ref, v_ref, d_ref, nm_ref, nv_ref):
        d_ref[...], nm_ref[...], nv_ref[...] = _adamw_math(w_ref[...], g_ref[...], m_ref[...], v_ref[...])

    spec = pl.BlockSpec((tr, cols), lambda i: (i, 0))
    shape = jax.ShapeDtypeStruct((rows, cols), F32)
    return pl.pallas_call(
        body, name=name, grid=(rows // tr,),
        in_specs=[spec] * 4, out_specs=[spec] * 3, out_shape=[shape] * 3,
        compiler_params=_ARB1,
    )(w, g, m, v)


def _in_columns():
    pieces = []
    for group in range(2):
        q0, k0, v0, g0 = (0, 256, 512, 1024) if group == 0 else (1536, 1792, 2048, 2560)
        for h in range(HEADS):
            pieces += [(q0 + 64 * h, 64), (k0 + 64 * h, 64), (v0 + 128 * h, 128), (g0 + 128 * h, 128)]
    pieces.append((3072, GATE_RANK))
    return pieces


def _pad_w_in_t(w_in_t):
    parts = []
    for start, width in _in_columns():
        parts.append(w_in_t[start:start + width])
        if width < LANES:
            parts.append(jnp.zeros((LANES - width, w_in_t.shape[1]), w_in_t.dtype))
    return jnp.concatenate(parts, axis=0)


def _unpad_w_in_t(w_pt):
    rows = {}
    offset = 0
    for start, width in _in_columns():
        rows[start] = w_pt[offset:offset + width]
        offset += LANES
    return jnp.concatenate([rows[s] for s in sorted(rows)], axis=0)


def _rope_tables(t):
    half = 32
    inv = ROPE_BASE ** (-jnp.arange(half, dtype=F32) * 2.0 / 64)
    ang = jnp.arange(t, dtype=F32)[:, None] * inv[None, :]
    cos, sin = jnp.cos(ang), jnp.sin(ang)
    z32, z64 = jnp.zeros((t, 32), F32), jnp.zeros((t, 64), F32)
    return (jnp.concatenate([cos, cos, z64], axis=1),
            jnp.concatenate([-sin, z32, z64], axis=1),
            jnp.concatenate([z32, sin, z64], axis=1))


def _halves(w):
    n, rows, cols = w.shape
    return w.reshape(n, 2, rows // 2, cols)


def _pack_small(n1, nm, n2, nf, nret, ngla, ba, wa2, wa2_cols, extra=None):
    z = lambda k: jnp.zeros((1, k), F32)
    rows = [n1.reshape(1, -1), nm.reshape(1, -1), n2.reshape(1, -1), nf.reshape(1, -1),
            jnp.concatenate([nret.reshape(1, -1), ngla.reshape(1, -1)], axis=1),
            jnp.concatenate([ba.reshape(1, -1), z(D_MODEL - 256)], axis=1),
            jnp.zeros((1, D_MODEL), F32) if extra is None else extra,
            jnp.zeros((1, D_MODEL), F32),
            jnp.concatenate([wa2.reshape(GATE_RANK, wa2_cols), jnp.zeros((GATE_RANK, D_MODEL - wa2_cols), F32)], axis=1),
            jnp.zeros((SMALL_ROWS - 8 - GATE_RANK, D_MODEL), F32)]
    return jnp.concatenate(rows, axis=0)


def _unpack_small(p, wa2_cols):
    return (p[0:1], p[1:2], p[2:3], p[3], p[4:5, 0:512], p[4:5, 512:1024], p[5:6, 0:256],
            p[8:8 + GATE_RANK, 0:wa2_cols].reshape(1, GATE_RANK, wa2_cols))


def _forward_backward(xs, target, ffn_w, w_in_pt, w_out_full, wa2_p, ba_p, ffn1_norm_g, mix_norm_g, ret_norm_g,
                      gla_norm_g, ffn2_norm_g, final_norm_g):
    t = xs.shape[0]
    cos_t, sa_t, sb_t = _rope_tables(t)
    log_gamma = jnp.log(1.0 - 2.0 ** (-5.0 - jnp.arange(HEADS, dtype=F32)))
    lg_t = jnp.broadcast_to(log_gamma[:, None, None], (HEADS, 1, LANES))
    ret_aux = [cos_t, sa_t, sb_t, lg_t]

    x1, a1, u1, h1 = _ffn_fwd(xs, ffn1_norm_g, ffn_w, 0, "ffn1_fwd")
    proj, h_mix = _mixer_in_fwd(x1, mix_norm_g, w_in_pt, "mixer_in_fwd")
    gla_aux = [proj, wa2_p, ba_p]
    o_ret, raw_ret, st_ret = _attn_fwd(True, proj, ret_aux, ret_norm_g, "ret_fwd")
    o_gla, raw_gla, st_gla = _attn_fwd(False, proj, gla_aux, gla_norm_g, "gla_fwd")
    x2 = _mixer_out_fwd(o_ret, o_gla, w_out_full, x1, "mixer_out_fwd")
    x3, a2, u2, h2 = _ffn_fwd(x2, ffn2_norm_g, ffn_w, 1, "ffn2_fwd")
    loss_blk, dx3, d_final_g = _final_loss(x3, final_norm_g, target, "final_loss")

    da2, du2, hid2, dob2, dx2, d_ffn2_g = _ffn_bwd(dx3, x2, ffn2_norm_g, a2, u2, ffn_w, 1, "ffn2_bwd")
    g_gate2 = _matmul_tn(da2, h2, "ffn2_dgate")
    g_up2 = _matmul_tn(du2, h2, "ffn2_dup")
    g_down2 = _matmul_tn(hid2, dob2, "ffn2_ddown")

    d_o = _matmul_nt(dx2, w_out_full, "mixer_out_bwd")
    g_wout_ret = _matmul_tn(o_ret, dx2, "wout_grad_ret")
    g_wout_gla = _matmul_tn(o_gla, dx2, "wout_grad_gla")
    dproj_ret, d_ret_g = _attn_bwd(True, proj, ret_aux, ret_norm_g, raw_ret, st_ret, d_o, "ret_bwd")
    dproj_gla, d_gla_g, dlogit, d_ba_p = _attn_bwd(False, proj, gla_aux, gla_norm_g, raw_gla, st_gla, d_o, "gla_bwd")
    d_glow = _matmul_nt(dlogit, wa2_p, "gate_low_bwd", out_dtype=BF16)
    g_wa2_p = _matmul_tn(proj[:, PROJ_P - LANES:], dlogit, "gate_w_grad")
    dproj = jnp.concatenate([dproj_ret, dproj_gla, d_glow], axis=1)
    g_win_p = _matmul_tn(dproj, h_mix, "w_in_grad", tka=PROJ_P // 3)
    dx1, d_mix_g = _mixer_in_bwd(dproj, w_in_pt, dx2, x1, mix_norm_g, "mixer_in_bwd")

    da1, du1, hid1, dob1, grad_x, d_ffn1_g = _ffn_bwd(dx1, xs, ffn1_norm_g, a1, u1, ffn_w, 0, "ffn1_bwd")
    g_gate1 = _matmul_tn(da1, h1, "ffn1_dgate")
    g_up1 = _matmul_tn(du1, h1, "ffn1_dup")
    g_down1 = _matmul_tn(hid1, dob1, "ffn1_ddown")

    return (loss_blk, grad_x, g_gate1, g_up1, g_down1, g_gate2, g_up2, g_down2, g_win_p, g_wout_ret, g_wout_gla, g_wa2_p,
            d_ba_p, d_ffn1_g, d_mix_g, d_ffn2_g, d_final_g, d_ret_g, d_gla_g)


def kernel(x, ffn1_norm_g, ffn1_w_gate, ffn1_w_up, ffn1_w_down, mix_norm_g, w_in, ret_norm_g, gla_w_a2, gla_b_a, gla_norm_g, w_out, ffn2_norm_g, ffn2_w_gate, ffn2_w_up, ffn2_w_down, final_norm_g, loss_target, m_ffn1_norm_g, m_ffn1_w_gate, m_ffn1_w_up, m_ffn1_w_down, m_mix_norm_g, m_w_in, m_ret_norm_g, m_gla_w_a2, m_gla_b_a, m_gla_norm_g, m_w_out, m_ffn2_norm_g, m_ffn2_w_gate, m_ffn2_w_up, m_ffn2_w_down, m_final_norm_g, v_ffn1_norm_g, v_ffn1_w_gate, v_ffn1_w_up, v_ffn1_w_down, v_mix_norm_g, v_w_in, v_ret_norm_g, v_gla_w_a2, v_gla_b_a, v_gla_norm_g, v_w_out, v_ffn2_norm_g, v_ffn2_w_gate, v_ffn2_w_up, v_ffn2_w_down, v_final_norm_g):
    t = x.shape[1]
    xs = x.reshape(t, D_MODEL)
    target = loss_target.reshape(t, D_MODEL)
    chip = 2 * lax.axis_index("x") + lax.axis_index("y")
    c_arr = lax.axis_index("c").astype(jnp.int32).reshape(1)

    me_arr = chip.astype(jnp.int32).reshape(1)

    def pad_rows(w_t):
        return jnp.pad(w_t, ((0, IN_ROWS - IN_SHARD), (0, 0)))

    ffn_params = [ffn1_w_gate[0].T, ffn1_w_up[0].T, ffn1_w_down[0], ffn2_w_gate[0].T, ffn2_w_up[0].T, ffn2_w_down[0]]
    shards_in = [_halves(jnp.stack(ffn_params, axis=0).astype(BF16)),
                 _halves(pad_rows(w_in[0].T).astype(BF16)[None]),
                 _halves(w_out.astype(BF16)),
                 jnp.concatenate([gla_w_a2.reshape(GATE_RANK, 64), jnp.zeros((GATE_RANK, 64), F32)], axis=1).reshape(
                     1, 2, 8, LANES)]
    gathered_w = _gather_weights(shards_in)
    ffn_all, win_all, wout_all, wa2_all = [
        lax.dynamic_update_slice(g, s[None], (chip,) + (0,) * s.ndim) for g, s in zip(gathered_w, shards_in)]
    ffn_w = ffn_all.reshape(N_CHIPS, 6, FF_SHARD, D_MODEL)
    win_t = win_all.reshape(N_CHIPS, IN_ROWS, D_MODEL)
    w_in_pt = _pad_w_in_t(jnp.concatenate([win_t[p, 0:IN_SHARD] for p in range(N_CHIPS)], axis=0))
    w_out_full = wout_all.reshape(D_MODEL, D_MODEL)
    wa2_p = jnp.pad(wa2_all.reshape(N_CHIPS, GATE_RANK, LANES).transpose(1, 0, 2).reshape(GATE_RANK, HEADS * LANES),
                    ((0, LANES - GATE_RANK), (0, 0))).astype(BF16)
    ba_p = jnp.pad(gla_b_a.reshape(HEADS, 64), ((0, 0), (0, 64))).reshape(1, HEADS * LANES)
    fb = _forward_backward(xs, target, ffn_w, w_in_pt, w_out_full, wa2_p, ba_p, ffn1_norm_g, mix_norm_g, ret_norm_g,
                           gla_norm_g, ffn2_norm_g, final_norm_g.reshape(1, D_MODEL))
    (loss_blk, grad_x, g_gate1, g_up1, g_down1, g_gate2, g_up2, g_down2, g_win_p, g_wout_ret, g_wout_gla, g_wa2_p,
     d_ba_p, d_ffn1_g, d_mix_g, d_ffn2_g, d_final_g, d_ret_g, d_gla_g) = fb

    def by_halves(g):
        p, rows, cols = g.shape
        return g.reshape(p, 2, rows // 2, cols)

    g_win_t = _unpad_w_in_t(g_win_p[0])
    g_win = jnp.stack([pad_rows(g_win_t[IN_SHARD * p:IN_SHARD * (p + 1)]) for p in range(N_CHIPS)], axis=0)
    g_wout = jnp.concatenate([g_wout_ret[0], g_wout_gla[0]], axis=0).reshape(N_CHIPS, D_MODEL // N_CHIPS, D_MODEL)
    grads = [by_halves(g) for g in (g_gate1, g_up1, g_down1, g_gate2, g_up2, g_down2, g_win, g_wout)]
    recv = _pair_exchange(grads)
    sums = [_pair_add(g, r, c_arr, "pair_add_%d" % k) for k, (g, r) in enumerate(zip(grads, recv))]
    arrived = _chip_exchange(sums)
    mine = [_chip_sum(s, r, me_arr, "chip_sum_%d" % k) for k, (s, r) in enumerate(zip(sums, arrived))]
    other = _pair_share(mine)

    g_wa2 = g_wa2_p[0][0:GATE_RANK].reshape(GATE_RANK, HEADS, LANES)[:, :, 0:64].reshape(GATE_RANK, 256)
    d_ba = d_ba_p.reshape(HEADS, LANES)[:, 0:64].reshape(1, 256)
    loss_row = jnp.pad(loss_blk[0:1, 0:1], ((0, 0), (0, D_MODEL - 1)))
    small_local = _pack_small(d_ffn1_g, d_mix_g, d_ffn2_g, d_final_g, d_ret_g, d_gla_g, d_ba, g_wa2, 256, loss_row)
    small_sum = _small_allreduce(small_local)
    loss = small_sum[6, 0]
    sg = _unpack_small(small_sum, 256)
    wa2_grad = lax.dynamic_slice(sg[7], (0, 0, 64 * chip), (1, GATE_RANK, 64))
    small_g = _pack_small(*sg[:7], wa2_grad, 64)
    small_w = _pack_small(ffn1_norm_g, mix_norm_g, ffn2_norm_g, final_norm_g, ret_norm_g, gla_norm_g, gla_b_a, gla_w_a2, 64)
    small_m = _pack_small(m_ffn1_norm_g, m_mix_norm_g, m_ffn2_norm_g, m_final_norm_g, m_ret_norm_g, m_gla_norm_g,
                          m_gla_b_a, m_gla_w_a2, 64)
    small_v = _pack_small(v_ffn1_norm_g, v_mix_norm_g, v_ffn2_norm_g, v_final_norm_g, v_ret_norm_g, v_gla_norm_g,
                          v_gla_b_a, v_gla_w_a2, 64)
    small_out = _adamw(small_w, small_g, small_m, small_v, "adamw_small")
    s_grad = _unpack_small(small_g, 64)
    s_delta, s_m, s_v = (_unpack_small(o, 64) for o in small_out)

    def big(k, w, m, v, name, to_2d, from_2d):
        outs4 = _adamw_halves(to_2d(w), mine[k], other[k], to_2d(m), to_2d(v), c_arr, name)
        return [from_2d(z) for z in outs4]

    plain = (lambda w: w[0], lambda z: z[None])
    transposed = (lambda w: w[0].T, lambda z: z.T[None])
    in_proj = (lambda w: pad_rows(w[0].T), lambda z: z[0:IN_SHARD].T[None])
    r_g1 = big(0, ffn1_w_gate, m_ffn1_w_gate, v_ffn1_w_gate, "adamw_ffn1_gate", *transposed)
    r_u1 = big(1, ffn1_w_up, m_ffn1_w_up, v_ffn1_w_up, "adamw_ffn1_up", *transposed)
    r_d1 = big(2, ffn1_w_down, m_ffn1_w_down, v_ffn1_w_down, "adamw_ffn1_down", *plain)
    r_g2 = big(3, ffn2_w_gate, m_ffn2_w_gate, v_ffn2_w_gate, "adamw_ffn2_gate", *transposed)
    r_u2 = big(4, ffn2_w_up, m_ffn2_w_up, v_ffn2_w_up, "adamw_ffn2_up", *transposed)
    r_d2 = big(5, ffn2_w_down, m_ffn2_w_down, v_ffn2_w_down, "adamw_ffn2_down", *plain)
    r_in = big(6, w_in, m_w_in, v_w_in, "adamw_w_in", *in_proj)
    r_out = big(7, w_out, m_w_out, v_w_out, "adamw_w_out", *plain)

    def leaves(k, smalls):
        n1, nm, n2, nf, nret, ngla, ba, wa2 = smalls
        return [n1, r_g1[k], r_u1[k], r_d1[k], nm, r_in[k], nret, wa2, ba, ngla, r_out[k], n2, r_g2[k], r_u2[k], r_d2[k], nf]

    outs = [loss, grad_x.reshape(x.shape)]
    outs += leaves(0, s_grad) + leaves(1, s_delta) + leaves(2, s_m) + leaves(3, s_v)
    return tuple(outs)
```

```python
import functools

import jax
import jax.numpy as jnp
from jax import lax
from jax.experimental import pallas as pl
from jax.experimental.pallas import tpu as pltpu

F32, BF16 = jnp.float32, jnp.bfloat16
MESH = pl.DeviceIdType.MESH
ANY = pl.BlockSpec(memory_space=pl.ANY)

D_MODEL = 1024
D_FF = 2816
N_CHIPS = 4
FF_SHARD = D_FF // N_CHIPS
IN_WIDTH = 3088
IN_SHARD = IN_WIDTH // N_CHIPS
IN_ROWS = 800
CHUNK = 64
HEADS = 4
LANES = 128
HEAD_BLOCK = 4 * LANES
PROJ_P = 2 * HEADS * HEAD_BLOCK + LANES
GATE_RANK = 16
QK_SCALE = 0.125
GATE_NORM = 16.0
RMS_EPS = 1e-6
ROPE_BASE = 10000.0
ADAM_LR, ADAM_B1, ADAM_B2, ADAM_EPS, ADAM_WD, ADAM_STEP = 0.001, 0.9, 0.999, 1e-08, 0.01, 10
SMALL_ROWS = 32
TOKEN_TILE = 512
ATTN_TILE = 512

_ARB2 = pltpu.CompilerParams(dimension_semantics=("arbitrary", "arbitrary"))
_ARB1 = pltpu.CompilerParams(dimension_semantics=("arbitrary",))
_ARB3 = pltpu.CompilerParams(dimension_semantics=("arbitrary", "arbitrary", "arbitrary"))


def _dot(a, b):
    return jnp.dot(a, b, preferred_element_type=F32)


def _dot_nt(a, b):
    return lax.dot_general(a, b, (((1,), (1,)), ((), ())), preferred_element_type=F32)


def _dot_tn(a, b):
    return lax.dot_general(a, b, (((0,), (0,)), ((), ())), preferred_element_type=F32)


def _rms_scale(xv):
    return lax.rsqrt(jnp.mean(xv * xv, axis=-1, keepdims=True) + RMS_EPS)


def _rms_bwd(dh, xv, g):
    r = _rms_scale(xv)
    xhat = xv * r
    dxhat = dh * g
    dx = r * (dxhat - xhat * jnp.mean(dxhat * xhat, axis=-1, keepdims=True))
    return dx, jnp.sum(dh * xhat, axis=0, keepdims=True)


def _silu_grad(a, sg):
    return sg * (1.0 + a * (1.0 - sg))


class _Hosted:
    def __init__(self, arrays, out_shapes, n_sems, start, finish):
        self.arrays, self.out_shapes, self.n_sems = list(arrays), list(out_shapes), n_sems
        self.start, self.finish = start, finish


def _call(body, args, *, name, grid, in_specs, out_specs, out_shape, scratch_shapes, compiler_params, hosted=None):
    if hosted is None:
        outs = pl.pallas_call(body, name=name, grid=grid, in_specs=in_specs, out_specs=out_specs, out_shape=out_shape,
                              scratch_shapes=scratch_shapes, compiler_params=compiler_params)(*args)
        return list(outs), []
    n_in, n_out, n_sc, nh = len(in_specs), len(out_specs), len(scratch_shapes), len(hosted.arrays)

    def wrapped(*refs):
        ins, h_in = refs[:n_in], refs[n_in:n_in + nh]
        outs, h_out = refs[n_in + nh:n_in + nh + n_out], refs[n_in + nh + n_out:n_in + 2 * nh + n_out]
        rest = refs[n_in + 2 * nh + n_out:]
        scratch, (send_sems, recv_sems) = rest[:n_sc], rest[n_sc:]
        first = functools.reduce(jnp.logical_and, [pl.program_id(d) == 0 for d in range(len(grid))])
        last = functools.reduce(jnp.logical_and, [pl.program_id(d) == n - 1 for d, n in enumerate(grid)])

        @pl.when(first)
        def _():
            hosted.start(h_in, h_out, send_sems, recv_sems)

        body(*ins, *outs, *scratch)

        @pl.when(last)
        def _():
            hosted.finish(h_in, h_out, send_sems, recv_sems)

    sems = [pltpu.SemaphoreType.DMA((hosted.n_sems,)), pltpu.SemaphoreType.DMA((hosted.n_sems,))]
    outs = pl.pallas_call(
        wrapped, name=name, grid=grid, in_specs=list(in_specs) + [ANY] * nh, out_specs=list(out_specs) + [ANY] * nh,
        out_shape=list(out_shape) + hosted.out_shapes, scratch_shapes=list(scratch_shapes) + sems,
        compiler_params=compiler_params)(*args, *hosted.arrays)
    return list(outs[:n_out]), list(outs[n_out:])


def _run_hosted(hosted, name):
    nh = len(hosted.arrays)

    def body(*refs):
        h_in, h_out, (send_sems, recv_sems) = refs[:nh], refs[nh:2 * nh], refs[2 * nh:]
        hosted.start(h_in, h_out, send_sems, recv_sems)
        hosted.finish(h_in, h_out, send_sems, recv_sems)

    sems = [pltpu.SemaphoreType.DMA((hosted.n_sems,)), pltpu.SemaphoreType.DMA((hosted.n_sems,))]
    return list(pl.pallas_call(body, name=name, in_specs=[ANY] * nh, out_specs=[ANY] * nh,
                               out_shape=hosted.out_shapes, scratch_shapes=sems)(*hosted.arrays))


def _ffn_weight_specs():
    blk = (None, None, FF_SHARD, D_MODEL)
    return [pl.BlockSpec(blk, lambda i, j, k=kind: (j, k, 0, 0)) for kind in range(3)]


def _ffn_fwd(x, g, ffn_w, name, hosted=None):
    t = x.shape[0]
    tm = min(t, TOKEN_TILE)

    def body(x_ref, g_ref, wg_ref, wu_ref, wd_ref, xo_ref, a_ref, u_ref, h_ref, acc_ref):
        j = pl.program_id(1)

        @pl.when(j == 0)
        def _():
            xv = x_ref[...]
            h_ref[...] = ((xv * _rms_scale(xv)) * g_ref[...]).astype(BF16)
            acc_ref[...] = jnp.zeros_like(acc_ref)

        h = h_ref[...]
        a = _dot_nt(h, wg_ref[...])
        u = _dot_nt(h, wu_ref[...])
        a_ref[...] = a.astype(BF16)
        u_ref[...] = u.astype(BF16)
        hid = (a * jax.nn.sigmoid(a)) * u
        acc_ref[...] += _dot(hid.astype(BF16), wd_ref[...])

        @pl.when(j == N_CHIPS - 1)
        def _():
            xo_ref[...] = x_ref[...] + 0.5 * acc_ref[...]

    tok = pl.BlockSpec((tm, D_MODEL), lambda i, j: (i, 0))
    act = pl.BlockSpec((None, tm, FF_SHARD), lambda i, j: (j, i, 0))
    return _call(
        body, (x, g, ffn_w, ffn_w, ffn_w), name=name, grid=(t // tm, N_CHIPS),
        in_specs=[tok, pl.BlockSpec((1, D_MODEL), lambda i, j: (0, 0))] + _ffn_weight_specs(),
        out_specs=[tok, act, act, tok],
        out_shape=[jax.ShapeDtypeStruct((t, D_MODEL), F32),
                   jax.ShapeDtypeStruct((N_CHIPS, t, FF_SHARD), BF16),
                   jax.ShapeDtypeStruct((N_CHIPS, t, FF_SHARD), BF16),
                   jax.ShapeDtypeStruct((t, D_MODEL), BF16)],
        scratch_shapes=[pltpu.VMEM((tm, D_MODEL), F32)],
        compiler_params=_ARB2, hosted=hosted)


def _ffn_bwd(dxo, x, g, a4, u4, ffn_w, name, hosted=None):
    t = x.shape[0]
    tm = min(t, TOKEN_TILE)

    def body(dxo_ref, x_ref, g_ref, a_ref, u_ref, wg_ref, wu_ref, wd_ref,
             da_ref, du_ref, hid_ref, dob_ref, dx_ref, dg_ref, acc_ref):
        i, j = pl.program_id(0), pl.program_id(1)

        @pl.when(j == 0)
        def _():
            dob_ref[...] = (0.5 * dxo_ref[...]).astype(BF16)
            acc_ref[...] = jnp.zeros_like(acc_ref)

        @pl.when((i == 0) & (j == 0))
        def _():
            dg_ref[...] = jnp.zeros_like(dg_ref)

        dhid = _dot_nt(dob_ref[...], wd_ref[...])
        a = a_ref[...].astype(F32)
        u = u_ref[...].astype(F32)
        sg = jax.nn.sigmoid(a)
        s = a * sg
        hid_ref[...] = (s * u).astype(BF16)
        du = (dhid * s).astype(BF16)
        da = (dhid * u * _silu_grad(a, sg)).astype(BF16)
        du_ref[...] = du
        da_ref[...] = da
        acc_ref[...] += _dot(da, wg_ref[...]) + _dot(du, wu_ref[...])

        @pl.when(j == N_CHIPS - 1)
        def _():
            dx, dg = _rms_bwd(acc_ref[...], x_ref[...], g_ref[...])
            dx_ref[...] = dxo_ref[...] + dx
            dg_ref[...] += dg

    tok = pl.BlockSpec((tm, D_MODEL), lambda i, j: (i, 0))
    act = pl.BlockSpec((None, tm, FF_SHARD), lambda i, j: (j, i, 0))
    row = pl.BlockSpec((1, D_MODEL), lambda i, j: (0, 0))
    act_shape = jax.ShapeDtypeStruct((N_CHIPS, t, FF_SHARD), BF16)
    return _call(
        body, (dxo, x, g, a4, u4, ffn_w, ffn_w, ffn_w), name=name, grid=(t // tm, N_CHIPS),
        in_specs=[tok, tok, row, act, act] + _ffn_weight_specs(),
        out_specs=[act, act, act, tok, tok, row],
        out_shape=[act_shape, act_shape, act_shape,
                   jax.ShapeDtypeStruct((t, D_MODEL), BF16),
                   jax.ShapeDtypeStruct((t, D_MODEL), F32),
                   jax.ShapeDtypeStruct((1, D_MODEL), F32)],
        scratch_shapes=[pltpu.VMEM((tm, D_MODEL), F32)],
        compiler_params=_ARB2, hosted=hosted)


def _matmul_tn(a, b, name, tka=None):
    a3, b3 = a.ndim == 3, b.ndim == 3
    nb = a.shape[0] if a3 else (b.shape[0] if b3 else 1)
    t, ka, n = a.shape[-2], a.shape[-1], b.shape[-1]
    tka = ka if tka is None else tka
    tk = min(t, TOKEN_TILE)

    def body(a_ref, b_ref, o_ref):
        @pl.when(pl.program_id(2) == 0)
        def _():
            o_ref[...] = jnp.zeros_like(o_ref)

        o_ref[...] += _dot_tn(a_ref[...].astype(BF16), b_ref[...].astype(BF16))

    a_spec = (pl.BlockSpec((None, tk, tka), lambda i, j, k: (i, k, j)) if a3
              else pl.BlockSpec((tk, tka), lambda i, j, k: (k, j)))
    b_spec = (pl.BlockSpec((None, tk, n), lambda i, j, k: (i, k, 0)) if b3
              else pl.BlockSpec((tk, n), lambda i, j, k: (k, 0)))
    return pl.pallas_call(
        body, name=name, grid=(nb, ka // tka, t // tk),
        in_specs=[a_spec, b_spec],
        out_specs=pl.BlockSpec((None, tka, n), lambda i, j, k: (i, j, 0)),
        out_shape=jax.ShapeDtypeStruct((nb, ka, n), F32),
        compiler_params=_ARB3,
    )(a, b)


def _matmul_nt(a, w, name, out_dtype=F32):
    t, k = a.shape
    n = w.shape[0]
    tm = min(t, TOKEN_TILE)

    def body(a_ref, w_ref, o_ref):
        o_ref[...] = _dot_nt(a_ref[...].astype(BF16), w_ref[...]).astype(out_dtype)

    return pl.pallas_call(
        body, name=name, grid=(t // tm,),
        in_specs=[pl.BlockSpec((tm, k), lambda i: (i, 0)), pl.BlockSpec((n, k), lambda i: (0, 0))],
        out_specs=pl.BlockSpec((tm, n), lambda i: (i, 0)),
        out_shape=jax.ShapeDtypeStruct((t, n), out_dtype),
        compiler_params=_ARB1,
    )(a, w)


def _mixer_in_bwd(dproj, w_in_pt, dres, x, g, name):
    t, k = dproj.shape
    tm = min(t, TOKEN_TILE)

    def body(a_ref, w_ref, dres_ref, x_ref, g_ref, dx_ref, dg_ref):
        @pl.when(pl.program_id(0) == 0)
        def _():
            dg_ref[...] = jnp.zeros_like(dg_ref)

        dh = _dot(a_ref[...], w_ref[...])
        dx, dg = _rms_bwd(dh, x_ref[...], g_ref[...])
        dx_ref[...] = dres_ref[...] + dx
        dg_ref[...] += dg

    tok = pl.BlockSpec((tm, D_MODEL), lambda i: (i, 0))
    row = pl.BlockSpec((1, D_MODEL), lambda i: (0, 0))
    return pl.pallas_call(
        body, name=name, grid=(t // tm,),
        in_specs=[pl.BlockSpec((tm, k), lambda i: (i, 0)), pl.BlockSpec((k, D_MODEL), lambda i: (0, 0)), tok, tok, row],
        out_specs=[tok, row],
        out_shape=[jax.ShapeDtypeStruct((t, D_MODEL), F32), jax.ShapeDtypeStruct((1, D_MODEL), F32)],
        compiler_params=_ARB1,
    )(dproj, w_in_pt, dres, x, g)


def _mixer_in_fwd(x, g, w_in_pt, name):
    t = x.shape[0]
    tm = min(t, TOKEN_TILE)
    tn = PROJ_P // 3

    def body(x_ref, g_ref, w_ref, p_ref, h_ref):
        @pl.when(pl.program_id(1) == 0)
        def _():
            xv = x_ref[...]
            h_ref[...] = ((xv * _rms_scale(xv)) * g_ref[...]).astype(BF16)

        p_ref[...] = _dot_nt(h_ref[...], w_ref[...])

    tok = pl.BlockSpec((tm, D_MODEL), lambda i, j: (i, 0))
    return pl.pallas_call(
        body, name=name, grid=(t // tm, 3),
        in_specs=[tok, pl.BlockSpec((1, D_MODEL), lambda i, j: (0, 0)),
                  pl.BlockSpec((tn, D_MODEL), lambda i, j: (j, 0))],
        out_specs=[pl.BlockSpec((tm, tn), lambda i, j: (i, j)), tok],
        out_shape=[jax.ShapeDtypeStruct((t, PROJ_P), F32), jax.ShapeDtypeStruct((t, D_MODEL), BF16)],
        compiler_params=_ARB2,
    )(x, g, w_in_pt)


def _mixer_out_fwd(o_ret, o_gla, w_out, x, name):
    t = x.shape[0]
    tm = min(t, TOKEN_TILE)
    half = HEADS * LANES

    def body(a_ref, b_ref, w_ref, x_ref, o_ref):
        o_ref[...] = x_ref[...] + _dot(a_ref[...], w_ref[0:half, :]) + _dot(b_ref[...], w_ref[half:2 * half, :])

    tok = pl.BlockSpec((tm, D_MODEL), lambda i: (i, 0))
    hb = pl.BlockSpec((tm, half), lambda i: (i, 0))
    return pl.pallas_call(
        body, name=name, grid=(t // tm,),
        in_specs=[hb, hb, pl.BlockSpec((2 * half, D_MODEL), lambda i: (0, 0)), tok],
        out_specs=tok, out_shape=jax.ShapeDtypeStruct((t, D_MODEL), F32),
        compiler_params=_ARB1,
    )(o_ret, o_gla, w_out, x)


def _final_loss(x, g, target, name):
    t = x.shape[0]
    tm = min(t, TOKEN_TILE)

    def body(x_ref, g_ref, t_ref, l_ref, dx_ref, dg_ref):
        @pl.when(pl.program_id(0) == 0)
        def _():
            l_ref[...] = jnp.zeros_like(l_ref)
            dg_ref[...] = jnp.zeros_like(dg_ref)

        xv = x_ref[...]
        gv = g_ref[...]
        err = (xv * _rms_scale(xv)) * gv - t_ref[...]
        l_ref[...] += 0.5 * jnp.sum(jnp.mean(err * err, axis=-1, keepdims=True), axis=0, keepdims=True)
        dx, dg = _rms_bwd(err * (1.0 / D_MODEL), xv, gv)
        dx_ref[...] = dx
        dg_ref[...] += dg

    tok = pl.BlockSpec((tm, D_MODEL), lambda i: (i, 0))
    row = pl.BlockSpec((1, D_MODEL), lambda i: (0, 0))
    return pl.pallas_call(
        body, name=name, grid=(t // tm,),
        in_specs=[tok, row, tok],
        out_specs=[pl.BlockSpec((8, LANES), lambda i: (0, 0)), tok, row],
        out_shape=[jax.ShapeDtypeStruct((8, LANES), F32), jax.ShapeDtypeStruct((t, D_MODEL), F32),
                   jax.ShapeDtypeStruct((1, D_MODEL), F32)],
        compiler_params=_ARB1,
    )(x, g, target)


def _rot(v, cos, sa, sb):
    return v * cos + pltpu.roll(v, 96, 1) * sa + pltpu.roll(v, 32, 1) * sb


def _rot_t(d, cos, sa, sb):
    return d * cos + pltpu.roll(d * sa, 32, 1) + pltpu.roll(d * sb, 96, 1)


def _bmm(a, b):
    return jnp.einsum("cik,ckj->cij", a, b, preferred_element_type=F32)


def _bmm_nt(a, b):
    return jnp.einsum("cik,cjk->cij", a, b, preferred_element_type=F32)


def _bmm_tn(a, b):
    return jnp.einsum("cki,ckj->cij", a, b, preferred_element_type=F32)


def _masked_sum(mask, x):
    hi = x.astype(BF16)
    r1 = x - hi.astype(F32)
    mid = r1.astype(BF16)
    lo = (r1 - mid.astype(F32)).astype(BF16)
    return _bmm(mask, hi) + _bmm(mask, mid) + _bmm(mask, lo)


def _tile_inputs(is_ret, proj_ref, aux, nc):
    shape3 = (nc, CHUNK, LANES)
    q_raw = proj_ref[:, 0:LANES]
    k_raw = proj_ref[:, LANES:2 * LANES]
    v = proj_ref[:, 2 * LANES:3 * LANES]
    gate = proj_ref[:, 3 * LANES:4 * LANES]
    ri = lax.broadcasted_iota(jnp.int32, (nc, CHUNK, CHUNK), 1)
    ci = lax.broadcasted_iota(jnp.int32, (nc, CHUNK, CHUNK), 2)
    if is_ret:
        cos_ref, sa_ref, sb_ref, lg_ref = aux
        cos, sa, sb = cos_ref[...], sa_ref[...], sb_ref[...]
        q = _rot(q_raw, cos, sa, sb)
        k = _rot(k_raw, cos, sa, sb) * QK_SCALE
        steps = (lax.broadcasted_iota(jnp.int32, shape3, 1) + 1).astype(F32)
        b = steps * lg_ref[...]
        logit = None
    else:
        glow_ref, wa2_ref, ba_ref = aux
        logit = _dot(glow_ref[...].astype(BF16), wa2_ref[...]) + ba_ref[...]
        la = (jnp.minimum(logit, 0.0) - jnp.log1p(jnp.exp(-jnp.abs(logit)))) * (1.0 / GATE_NORM)
        b = _masked_sum((ci <= ri).astype(BF16), la.reshape(shape3))
        q = q_raw * QK_SCALE
        k = k_raw
    return q.reshape(shape3), k.reshape(shape3), v.reshape(shape3), gate, b, logit, ri, ci


def _tile_scores(q, k, b, ri, ci):
    mid = b[:, CHUNK // 2 - 1:CHUNK // 2, :]
    ep = jnp.exp(b - mid)
    en = jnp.exp(mid - b)
    qt, kt, qh, kh = q * ep, k * en, q * en, k * ep
    low = _bmm_nt(qt.astype(BF16), kt.astype(BF16))
    upp = _bmm_nt(qh.astype(BF16), kh.astype(BF16))
    scores = jnp.where(ci <= ri, low, upp)
    return scores, ep, en, qt, kt, qh, kh


def _attn_specs(is_ret, t, tb, imap_t):
    nb = t // tb
    base = 0 if is_ret else HEADS
    proj = pl.BlockSpec((tb, HEAD_BLOCK), lambda h, i: (imap_t(i), base + h))
    lane_t = pl.BlockSpec((tb, LANES), lambda h, i: (imap_t(i), 0))
    if is_ret:
        aux = [lane_t, lane_t, lane_t, pl.BlockSpec((None, 1, LANES), lambda h, i: (h, 0, 0))]
    else:
        aux = [pl.BlockSpec((tb, LANES), lambda h, i: (imap_t(i), PROJ_P // LANES - 1)),
               pl.BlockSpec((LANES, LANES), lambda h, i: (0, h)),
               pl.BlockSpec((1, LANES), lambda h, i: (0, h))]
    gain = pl.BlockSpec((1, LANES), lambda h, i: (0, h))
    head_t = pl.BlockSpec((tb, LANES), lambda h, i: (imap_t(i), h))
    state = pl.BlockSpec((None, tb // CHUNK, LANES, LANES), lambda h, i: (h, imap_t(i), 0, 0))
    return nb, proj, aux, gain, head_t, state


def _attn_fwd(is_ret, proj, aux_arrays, gain, name):
    t = proj.shape[0]
    tb = min(t, ATTN_TILE)
    nc = tb // CHUNK
    n_aux = 4 if is_ret else 3
    nb, proj_spec, aux_specs, gain_spec, head_t, state_spec = _attn_specs(is_ret, t, tb, lambda i: i)

    def body(*refs):
        proj_ref = refs[0]
        aux = refs[1:1 + n_aux]
        gn_ref, ofin_ref, oraw_ref, st_ref, state = refs[1 + n_aux:]

        @pl.when(pl.program_id(1) == 0)
        def _():
            state[...] = jnp.zeros_like(state)

        q, k, v, gate, b, _, ri, ci = _tile_inputs(is_ret, proj_ref, aux, nc)
        scores = _tile_scores(q, k, b, ri, ci)[0]
        vb = v.astype(BF16)
        intra = _bmm(scores.astype(BF16), vb)
        b_last = b[:, CHUNK - 1:CHUNK, :]
        e_last = jnp.exp(b_last)
        grow = _bmm_tn(vb, (k * jnp.exp(b_last - b)).astype(BF16))
        st = state[...]
        for c in range(nc):
            st_ref[c] = st
            st = st * e_last[c] + grow[c]
        state[...] = st
        inter = _bmm_nt((q * jnp.exp(b)).astype(BF16), st_ref[...].astype(BF16))
        out = (intra + inter).reshape(tb, LANES)
        oraw_ref[...] = out
        normed = out * _rms_scale(out)
        ofin_ref[...] = ((normed * gn_ref[...]) * (gate * jax.nn.sigmoid(gate))).astype(BF16)

    width = HEADS * LANES
    return pl.pallas_call(
        body, name=name, grid=(HEADS, nb),
        in_specs=[proj_spec] + aux_specs + [gain_spec],
        out_specs=[head_t, head_t, state_spec],
        out_shape=[jax.ShapeDtypeStruct((t, width), BF16), jax.ShapeDtypeStruct((t, width), F32),
                   jax.ShapeDtypeStruct((HEADS, t // CHUNK, LANES, LANES), F32)],
        scratch_shapes=[pltpu.VMEM((LANES, LANES), F32)],
        compiler_params=_ARB2,
    )(proj, *aux_arrays, gain)


def _attn_bwd(is_ret, proj, aux_arrays, gain, o_raw, states, d_out, name):
    t = proj.shape[0]
    tb = min(t, ATTN_TILE)
    nc = tb // CHUNK
    n_aux = 4 if is_ret else 3
    nblk = t // tb
    nb, proj_spec, aux_specs, gain_spec, head_t, state_spec = _attn_specs(is_ret, t, tb, lambda i: nblk - 1 - i)
    base = 0 if is_ret else HEADS
    dout_spec = pl.BlockSpec((tb, LANES), lambda h, i: (nblk - 1 - i, base + h))

    def body(*refs):
        proj_ref = refs[0]
        aux = refs[1:1 + n_aux]
        gn_ref, oraw_ref, st_ref, dfin_ref = refs[1 + n_aux:5 + n_aux]
        if is_ret:
            dproj_ref, dgn_ref, dstate, dafter_ref = refs[5 + n_aux:]
        else:
            dproj_ref, dgn_ref, dlogit_ref, dba_ref, dstate, dafter_ref = refs[5 + n_aux:]

        @pl.when(pl.program_id(1) == 0)
        def _():
            dstate[...] = jnp.zeros_like(dstate)
            dgn_ref[...] = jnp.zeros_like(dgn_ref)
            if not is_ret:
                dba_ref[...] = jnp.zeros_like(dba_ref)

        shape3 = (nc, CHUNK, LANES)
        q, k, v, gate, b, logit, ri, ci = _tile_inputs(is_ret, proj_ref, aux, nc)
        scores, ep, en, qt, kt, qh, kh = _tile_scores(q, k, b, ri, ci)
        eb = jnp.exp(b)
        qe = q * eb
        b_last = b[:, CHUNK - 1:CHUNK, :]
        e_last = jnp.exp(b_last)
        ekd = jnp.exp(b_last - b)
        kd = k * ekd

        gn = gn_ref[...]
        out = oraw_ref[...]
        r = _rms_scale(out)
        normed = out * r
        sg = jax.nn.sigmoid(gate)
        dfin = dfin_ref[...]
        dgate = dfin * (normed * gn) * _silu_grad(gate, sg)
        dpre = dfin * (gate * sg)
        dgn_ref[...] += jnp.sum(dpre * normed, axis=0, keepdims=True)
        dnormed = dpre * gn
        d_o = r * (dnormed - normed * jnp.mean(dnormed * normed, axis=-1, keepdims=True))
        dob, vb = d_o.reshape(shape3).astype(BF16), v.astype(BF16)

        dgrow = _bmm_tn(dob, qe.astype(BF16))
        dst = dstate[...]
        for c in reversed(range(nc)):
            dafter_ref[c] = dst
            dst = dst * e_last[c] + dgrow[c]
        dstate[...] = dst
        st = st_ref[...]
        dafter = dafter_ref[...]
        stb, dafter_b = st.astype(BF16), dafter.astype(BF16)

        qtb, ktb, qhb, khb = qt.astype(BF16), kt.astype(BF16), qh.astype(BF16), kh.astype(BF16)
        scores_t = jnp.where(ci >= ri, _bmm_nt(ktb, qtb), _bmm_nt(khb, qhb))
        dv = _bmm(scores_t.astype(BF16), dob) + _bmm_nt(kd.astype(BF16), dafter_b)
        dsc = _bmm_nt(dob, vb)
        dsc_t = _bmm_nt(vb, dob)
        dqe = _bmm(dob, stb)
        dkd = _bmm(vb, dafter_b)
        dqt = _bmm(jnp.where(ci <= ri, dsc, 0.0).astype(BF16), ktb)
        dqh = _bmm(jnp.where(ci <= ri, 0.0, dsc).astype(BF16), khb)
        dkt = _bmm(jnp.where(ci >= ri, dsc_t, 0.0).astype(BF16), qtb)
        dkh = _bmm(jnp.where(ci >= ri, 0.0, dsc_t).astype(BF16), qhb)
        dq = (dqt * ep + dqh * en + dqe * eb).reshape(tb, LANES)
        dk = (dkt * en + dkh * ep + dkd * ekd).reshape(tb, LANES)

        if is_ret:
            cos_ref, sa_ref, sb_ref, _ = aux
            cos, sa, sb = cos_ref[...], sa_ref[...], sb_ref[...]
            dq_raw = _rot_t(dq, cos, sa, sb)
            dk_raw = _rot_t(dk, cos, sa, sb) * QK_SCALE
        else:
            dq_raw = dq * QK_SCALE
            dk_raw = dk
            db = dqt * qt - dkt * kt - dqh * qh + dkh * kh + dqe * qe - dkd * kd
            db_last = (jnp.sum(dkd * kd, axis=1, keepdims=True)
                       + jnp.sum(dafter * st, axis=1, keepdims=True) * e_last)
            last_row = lax.broadcasted_iota(jnp.int32, shape3, 1) == CHUNK - 1
            db = db + jnp.where(last_row, db_last, 0.0)
            dla = _masked_sum((ci >= ri).astype(BF16), db).reshape(tb, LANES)
            dlogit = dla * (1.0 / GATE_NORM) * jax.nn.sigmoid(-logit)
            dlogit_ref[...] = dlogit.astype(BF16)
            dba_ref[...] += jnp.sum(dlogit, axis=0, keepdims=True)

        dproj_ref[:, 0:LANES] = dq_raw.astype(BF16)
        dproj_ref[:, LANES:2 * LANES] = dk_raw.astype(BF16)
        dproj_ref[:, 2 * LANES:3 * LANES] = dv.reshape(tb, LANES).astype(BF16)
        dproj_ref[:, 3 * LANES:4 * LANES] = dgate.astype(BF16)

    width = HEADS * LANES
    row_out = pl.BlockSpec((1, LANES), lambda h, i: (0, h))
    out_specs = [pl.BlockSpec((tb, HEAD_BLOCK), lambda h, i: (nblk - 1 - i, h)), row_out]
    out_shape = [jax.ShapeDtypeStruct((t, HEADS * HEAD_BLOCK), BF16), jax.ShapeDtypeStruct((1, width), F32)]
    if not is_ret:
        out_specs += [head_t, row_out]
        out_shape += [jax.ShapeDtypeStruct((t, width), BF16), jax.ShapeDtypeStruct((1, width), F32)]
    return pl.pallas_call(
        body, name=name, grid=(HEADS, nblk),
        in_specs=[proj_spec] + aux_specs + [gain_spec, head_t, state_spec, dout_spec],
        out_specs=out_specs, out_shape=out_shape,
        scratch_shapes=[pltpu.VMEM((LANES, LANES), F32), pltpu.VMEM((nc, LANES, LANES), F32)],
        compiler_params=_ARB2,
    )(proj, *aux_arrays, gain, o_raw, states, d_out)


def _place():
    x, y, c = lax.axis_index("x"), lax.axis_index("y"), lax.axis_index("c")
    chips = [(1 - x, y), (x, 1 - y), (1 - x, 1 - y)]
    return x, y, c, 2 * x + y, chips


def _gather_plan(arrs):
    na = len(arrs)

    def copies(ins, outs, send_sems, recv_sems):
        x, y, c, me, chips = _place()

        def ici(a, j, src_chip, to):
            return pltpu.make_async_remote_copy(
                src_ref=ins[a].at[:, c], dst_ref=outs[a].at[src_chip, :, c],
                send_sem=send_sems.at[6 * a + j], recv_sem=recv_sems.at[6 * a + j], device_id=to, device_id_type=MESH)

        def d2d(a, j, src_chip, half):
            blk = outs[a].at[src_chip, :, half]
            return pltpu.make_async_remote_copy(
                src_ref=blk, dst_ref=blk, send_sem=send_sems.at[6 * a + 3 + j], recv_sem=recv_sems.at[6 * a + 3 + j],
                device_id=(x, y, 1 - c), device_id_type=MESH)

        peers = [(a, j, px, py) for a in range(na) for j, (px, py) in enumerate(chips)]
        return c, me, peers, ici, d2d

    def start(*refs):
        c, me, peers, ici, _ = copies(*refs)
        for a, j, px, py in peers:
            ici(a, j, me, (px, py, c)).start()

    def finish(*refs):
        c, me, peers, ici, d2d = copies(*refs)
        for a, j, px, py in peers:
            ici(a, j, 2 * px + py, (px, py, c)).wait_recv()
            d2d(a, j, 2 * px + py, c).start()
        for a, j, px, py in peers:
            d2d(a, j, 2 * px + py, 1 - c).wait_recv()
        for a, j, px, py in peers:
            ici(a, j, me, (px, py, c)).wait_send()
            d2d(a, j, 2 * px + py, c).wait_send()

    return _Hosted(arrs, [jax.ShapeDtypeStruct((N_CHIPS,) + a.shape, a.dtype) for a in arrs], 6 * na, start, finish)


def _pair_exchange(grads, name):
    na = len(grads)

    def body(*refs):
        ins, outs = refs[:na], refs[na:2 * na]
        send_sems, recv_sems = refs[2 * na:]
        x, y, c, _, _ = _place()
        copies = [pltpu.make_async_remote_copy(
            src_ref=ins[a].at[:, 1 - c], dst_ref=outs[a], send_sem=send_sems.at[a], recv_sem=recv_sems.at[a],
            device_id=(x, y, 1 - c), device_id_type=MESH) for a in range(na)]
        for cp in copies:
            cp.start()
        for cp in copies:
            cp.wait()

    return pl.pallas_call(
        body, name=name,
        in_specs=[ANY] * na, out_specs=[ANY] * na,
        out_shape=[jax.ShapeDtypeStruct(g.shape[:1] + g.shape[2:], g.dtype) for g in grads],
        scratch_shapes=[pltpu.SemaphoreType.DMA((na,)), pltpu.SemaphoreType.DMA((na,))],
    )(*grads)


def _pair_add(grad, recv, c_arr, name):
    _, _, r, cols = grad.shape

    def body(c_ref, g_ref, r_ref, o_ref):
        o_ref[...] = (g_ref[...] + r_ref[...]).astype(BF16)

    return pl.pallas_call(
        body, name=name,
        grid_spec=pltpu.PrefetchScalarGridSpec(
            num_scalar_prefetch=1, grid=(N_CHIPS,),
            in_specs=[pl.BlockSpec((None, None, r, cols), lambda p, c_ref: (p, c_ref[0], 0, 0)),
                      pl.BlockSpec((None, r, cols), lambda p, c_ref: (p, 0, 0))],
            out_specs=pl.BlockSpec((None, r, cols), lambda p, c_ref: (p, 0, 0))),
        out_shape=jax.ShapeDtypeStruct((N_CHIPS, r, cols), BF16),
        compiler_params=_ARB1,
    )(c_arr, grad, recv)


def _chip_exchange_plan(sums):
    na = len(sums)

    def copies(ins, outs, send_sems, recv_sems):
        x, y, c, me, chips = _place()

        def copy(a, j, px, py, block, slot):
            return pltpu.make_async_remote_copy(
                src_ref=ins[a].at[block], dst_ref=outs[a].at[slot],
                send_sem=send_sems.at[3 * a + j], recv_sem=recv_sems.at[3 * a + j],
                device_id=(px, py, c), device_id_type=MESH)

        peers = [(a, j, px, py) for a in range(na) for j, (px, py) in enumerate(chips)]
        return me, peers, copy

    def start(*refs):
        me, peers, copy = copies(*refs)
        for a, j, px, py in peers:
            copy(a, j, px, py, 2 * px + py, me).start()

    def finish(*refs):
        me, peers, copy = copies(*refs)
        for a, j, px, py in peers:
            copy(a, j, px, py, me, 2 * px + py).wait_recv()
        for a, j, px, py in peers:
            copy(a, j, px, py, 2 * px + py, me).wait_send()

    return _Hosted(sums, [jax.ShapeDtypeStruct(s.shape, s.dtype) for s in sums], 3 * na, start, finish)


def _chip_sum(own, recv, me_arr, name):
    _, r, cols = recv.shape

    def body(me_ref, own_ref, r_ref, o_ref):
        o_ref[...] = jnp.zeros_like(o_ref)
        for q in range(N_CHIPS):
            @pl.when(me_ref[0] == q)
            def _():
                o_ref[...] += own_ref[...].astype(F32)

            @pl.when(me_ref[0] != q)
            def _():
                o_ref[...] += r_ref[q].astype(F32)

    return pl.pallas_call(
        body, name=name,
        grid_spec=pltpu.PrefetchScalarGridSpec(
            num_scalar_prefetch=1, grid=(1,),
            in_specs=[pl.BlockSpec((None, r, cols), lambda i, me_ref: (me_ref[0], 0, 0)),
                      pl.BlockSpec((N_CHIPS, r, cols), lambda i, me_ref: (0, 0, 0))],
            out_specs=pl.BlockSpec((r, cols), lambda i, me_ref: (0, 0))),
        out_shape=jax.ShapeDtypeStruct((r, cols), F32),
        compiler_params=_ARB1,
    )(me_arr, own, recv)


def _pair_share(halves):
    na = len(halves)

    def body(*refs):
        ins, outs = refs[:na], refs[na:2 * na]
        send_sems, recv_sems = refs[2 * na:]
        x, y, c, _, _ = _place()
        copies = [pltpu.make_async_remote_copy(
            src_ref=ins[a], dst_ref=outs[a], send_sem=send_sems.at[a], recv_sem=recv_sems.at[a],
            device_id=(x, y, 1 - c), device_id_type=MESH) for a in range(na)]
        for cp in copies:
            cp.start()
        for cp in copies:
            cp.wait()

    return pl.pallas_call(
        body, name="pair_share",
        in_specs=[ANY] * na, out_specs=[ANY] * na,
        out_shape=[jax.ShapeDtypeStruct(h.shape, h.dtype) for h in halves],
        scratch_shapes=[pltpu.SemaphoreType.DMA((na,)), pltpu.SemaphoreType.DMA((na,))],
    )(*halves)


def _small_allreduce(block):
    m, n = block.shape

    def body(x_ref, all_ref, sum_ref, send_sems, recv_sems, local_sem):
        x, y, c, _, chips = _place()
        me, sibling = (x, y, c), (x, y, 1 - c)

        def rows(px, py, pc):
            return all_ref.at[pl.ds((4 * px + 2 * py + pc) * m, m), :]

        def copy(k, blk, to, src=None):
            return pltpu.make_async_remote_copy(
                src_ref=rows(*blk) if src is None else src, dst_ref=rows(*blk),
                send_sem=send_sems.at[k], recv_sem=recv_sems.at[k], device_id=to, device_id_type=MESH)

        mine = pltpu.make_async_copy(x_ref, rows(*me), local_sem)
        mine.start()
        first = [copy(0, me, sibling, src=x_ref)]
        first += [copy(1 + j, me, (*chip, c), src=x_ref) for j, chip in enumerate(chips)]
        for cp in first:
            cp.start()
        passed = [copy(4 + j, (*chip, c), sibling) for j, chip in enumerate(chips)]
        for j, chip in enumerate(chips):
            copy(1 + j, (*chip, c), me).wait_recv()
            passed[j].start()
        copy(0, sibling, me).wait_recv()
        for j, chip in enumerate(chips):
            copy(4 + j, (*chip, 1 - c), me).wait_recv()
        for cp in first + passed:
            cp.wait_send()
        mine.wait()
        acc = all_ref[0:m, :]
        for d in range(1, 8):
            acc = acc + all_ref[d * m:(d + 1) * m, :]
        sum_ref[...] = acc

    vmem = pl.BlockSpec(memory_space=pltpu.VMEM)
    return pl.pallas_call(
        body, name="small_allreduce",
        in_specs=[vmem], out_specs=[vmem, vmem],
        out_shape=[jax.ShapeDtypeStruct((8 * m, n), F32), jax.ShapeDtypeStruct((m, n), F32)],
        scratch_shapes=[pltpu.SemaphoreType.DMA((7,)), pltpu.SemaphoreType.DMA((7,)), pltpu.SemaphoreType.DMA],
    )(block)[1]


def _row_tile(rows):
    best = rows
    for cand in range(8, min(rows, 512) + 1, 8):
        if rows % cand == 0:
            best = cand
    return best


def _adamw_math(w, g, m, v):
    m2 = ADAM_B1 * m + (1.0 - ADAM_B1) * g
    v2 = ADAM_B2 * v + (1.0 - ADAM_B2) * (g * g)
    m_hat = m2 / (1.0 - ADAM_B1 ** ADAM_STEP)
    v_hat = v2 / (1.0 - ADAM_B2 ** ADAM_STEP)
    return -ADAM_LR * (m_hat / (jnp.sqrt(v_hat) + ADAM_EPS) + ADAM_WD * w), m2, v2


def _adamw_halves(w, g_mine, g_other, m, v, c_arr, name):
    rows, cols = w.shape
    r = rows // 2
    tr = _row_tile(r)
    nt = r // tr

    def body(c_ref, w_ref, gm_ref, go_ref, m_ref, v_ref, g_ref, d_ref, nm_ref, nv_ref):
        gv = jnp.where(pl.program_id(0) == c_ref[0], gm_ref[...], go_ref[...])
        g_ref[...] = gv
        d_ref[...], nm_ref[...], nv_ref[...] = _adamw_math(w_ref[...], gv, m_ref[...], v_ref[...])

    full = pl.BlockSpec((tr, cols), lambda h, i, c_ref: (h * nt + i, 0))
    half = pl.BlockSpec((tr, cols), lambda h, i, c_ref: (i, 0))
    shape = jax.ShapeDtypeStruct((rows, cols), F32)
    return pl.pallas_call(
        body, name=name,
        grid_spec=pltpu.PrefetchScalarGridSpec(
            num_scalar_prefetch=1, grid=(2, nt),
            in_specs=[full, half, half, full, full], out_specs=[full] * 4),
        out_shape=[shape] * 4,
        compiler_params=_ARB2,
    )(c_arr, w, g_mine, g_other, m, v)


def _adamw(w, g, m, v, name):
    rows, cols = w.shape
    tr = _row_tile(rows)

    def body(w_ref, g_ref, m_ref, v_ref, d_ref, nm_ref, nv_ref):
        d_ref[...], nm_ref[...], nv_ref[...] = _adamw_math(w_ref[...], g_ref[...], m_ref[...], v_ref[...])

    spec = pl.BlockSpec((tr, cols), lambda i: (i, 0))
    shape = jax.ShapeDtypeStruct((rows, cols), F32)
    return pl.pallas_call(
        body, name=name, grid=(rows // tr,),
        in_specs=[spec] * 4, out_specs=[spec] * 3, out_shape=[shape] * 3,
        compiler_params=_ARB1,
    )(w, g, m, v)


def _in_columns():
    pieces = []
    for group in range(2):
        q0, k0, v0, g0 = (0, 256, 512, 1024) if group == 0 else (1536, 1792, 2048, 2560)
        for h in range(HEADS):
            pieces += [(q0 + 64 * h, 64), (k0 + 64 * h, 64), (v0 + 128 * h, 128), (g0 + 128 * h, 128)]
    pieces.append((3072, GATE_RANK))
    return pieces


def _pad_w_in_t(w_in_t):
    parts = []
    for start, width in _in_columns():
        parts.append(w_in_t[start:start + width])
        if width < LANES:
            parts.append(jnp.zeros((LANES - width, w_in_t.shape[1]), w_in_t.dtype))
    return jnp.concatenate(parts, axis=0)


def _unpad_w_in_t(w_pt):
    rows = {}
    offset = 0
    for start, width in _in_columns():
        rows[start] = w_pt[offset:offset + width]
        offset += LANES
    return jnp.concatenate([rows[s] for s in sorted(rows)], axis=0)


def _rope_tables(t):
    half = 32
    inv = ROPE_BASE ** (-jnp.arange(half, dtype=F32) * 2.0 / 64)
    ang = jnp.arange(t, dtype=F32)[:, None] * inv[None, :]
    cos, sin = jnp.cos(ang), jnp.sin(ang)
    z32, z64 = jnp.zeros((t, 32), F32), jnp.zeros((t, 64), F32)
    return (jnp.concatenate([cos, cos, z64], axis=1),
            jnp.concatenate([-sin, z32, z64], axis=1),
            jnp.concatenate([z32, sin, z64], axis=1))


def _halves(w):
    n, rows, cols = w.shape
    return w.reshape(n, 2, rows // 2, cols)


def _pack_small(n1, nm, n2, nf, nret, ngla, ba, wa2, wa2_cols, extra=None):
    z = lambda k: jnp.zeros((1, k), F32)
    rows = [n1.reshape(1, -1), nm.reshape(1, -1), n2.reshape(1, -1), nf.reshape(1, -1),
            jnp.concatenate([nret.reshape(1, -1), ngla.reshape(1, -1)], axis=1),
            jnp.concatenate([ba.reshape(1, -1), z(D_MODEL - 256)], axis=1),
            jnp.zeros((1, D_MODEL), F32) if extra is None else extra,
            jnp.zeros((1, D_MODEL), F32),
            jnp.concatenate([wa2.reshape(GATE_RANK, wa2_cols), jnp.zeros((GATE_RANK, D_MODEL - wa2_cols), F32)], axis=1),
            jnp.zeros((SMALL_ROWS - 8 - GATE_RANK, D_MODEL), F32)]
    return jnp.concatenate(rows, axis=0)


def _unpack_small(p, wa2_cols):
    return (p[0:1], p[1:2], p[2:3], p[3], p[4:5, 0:512], p[4:5, 512:1024], p[5:6, 0:256],
            p[8:8 + GATE_RANK, 0:wa2_cols].reshape(1, GATE_RANK, wa2_cols))


def _pad_in_rows(w_t):
    return jnp.pad(w_t, ((0, IN_ROWS - IN_SHARD), (0, 0)))


def _forward_backward(xs, target, ffn1_w, rest, ba_p, ffn1_norm_g, mix_norm_g, ret_norm_g, gla_norm_g, ffn2_norm_g,
                      final_norm_g, rest_plan=None, rest_weights=None, early=None):
    t = xs.shape[0]
    cos_t, sa_t, sb_t = _rope_tables(t)
    log_gamma = jnp.log(1.0 - 2.0 ** (-5.0 - jnp.arange(HEADS, dtype=F32)))
    lg_t = jnp.broadcast_to(log_gamma[:, None, None], (HEADS, 1, LANES))
    ret_aux = [cos_t, sa_t, sb_t, lg_t]

    (x1, a1, u1, h1), gathered = _ffn_fwd(xs, ffn1_norm_g, ffn1_w, "ffn1_fwd", hosted=rest_plan)
    ffn2_w, w_in_pt, w_out_full, wa2_p = rest if rest_plan is None else rest_weights(gathered)
    proj, h_mix = _mixer_in_fwd(x1, mix_norm_g, w_in_pt, "mixer_in_fwd")
    gla_aux = [proj, wa2_p, ba_p]
    o_ret, raw_ret, st_ret = _attn_fwd(True, proj, ret_aux, ret_norm_g, "ret_fwd")
    o_gla, raw_gla, st_gla = _attn_fwd(False, proj, gla_aux, gla_norm_g, "gla_fwd")
    x2 = _mixer_out_fwd(o_ret, o_gla, w_out_full, x1, "mixer_out_fwd")
    (x3, a2, u2, h2), _ = _ffn_fwd(x2, ffn2_norm_g, ffn2_w, "ffn2_fwd")
    loss_blk, dx3, d_final_g = _final_loss(x3, final_norm_g, target, "final_loss")

    (da2, du2, hid2, dob2, dx2, d_ffn2_g), _ = _ffn_bwd(dx3, x2, ffn2_norm_g, a2, u2, ffn2_w, "ffn2_bwd")
    g_gate2 = _matmul_tn(da2, h2, "ffn2_dgate")
    g_up2 = _matmul_tn(du2, h2, "ffn2_dup")
    g_down2 = _matmul_tn(hid2, dob2, "ffn2_ddown")

    d_o = _matmul_nt(dx2, w_out_full, "mixer_out_bwd")
    g_wout_ret = _matmul_tn(o_ret, dx2, "wout_grad_ret")
    g_wout_gla = _matmul_tn(o_gla, dx2, "wout_grad_gla")
    dproj_ret, d_ret_g = _attn_bwd(True, proj, ret_aux, ret_norm_g, raw_ret, st_ret, d_o, "ret_bwd")
    dproj_gla, d_gla_g, dlogit, d_ba_p = _attn_bwd(False, proj, gla_aux, gla_norm_g, raw_gla, st_gla, d_o, "gla_bwd")
    d_glow = _matmul_nt(dlogit, wa2_p, "gate_low_bwd", out_dtype=BF16)
    g_wa2_p = _matmul_tn(proj[:, PROJ_P - LANES:], dlogit, "gate_w_grad")
    dproj = jnp.concatenate([dproj_ret, dproj_gla, d_glow], axis=1)
    g_win_p = _matmul_tn(dproj, h_mix, "w_in_grad", tka=PROJ_P // 3)
    dx1, d_mix_g = _mixer_in_bwd(dproj, w_in_pt, dx2, x1, mix_norm_g, "mixer_in_bwd")
    g_win_t = _unpad_w_in_t(g_win_p[0])
    g_win = jnp.stack([_pad_in_rows(g_win_t[IN_SHARD * p:IN_SHARD * (p + 1)]) for p in range(N_CHIPS)], axis=0)
    g_wout = jnp.concatenate([g_wout_ret[0], g_wout_gla[0]], axis=0).reshape(N_CHIPS, D_MODEL // N_CHIPS, D_MODEL)

    early_plan = None if early is None else early([g_gate2, g_up2, g_down2, g_win, g_wout])
    (da1, du1, hid1, dob1, grad_x, d_ffn1_g), arrived = _ffn_bwd(dx1, xs, ffn1_norm_g, a1, u1, ffn1_w, "ffn1_bwd",
                                                                hosted=early_plan)
    g_gate1 = _matmul_tn(da1, h1, "ffn1_dgate")
    g_up1 = _matmul_tn(du1, h1, "ffn1_dup")
    g_down1 = _matmul_tn(hid1, dob1, "ffn1_ddown")

    return (loss_blk, grad_x, g_gate1, g_up1, g_down1, g_gate2, g_up2, g_down2, g_win, g_wout, g_wa2_p,
            d_ba_p, d_ffn1_g, d_mix_g, d_ffn2_g, d_final_g, d_ret_g, d_gla_g, arrived)


def kernel(x, ffn1_norm_g, ffn1_w_gate, ffn1_w_up, ffn1_w_down, mix_norm_g, w_in, ret_norm_g, gla_w_a2, gla_b_a, gla_norm_g, w_out, ffn2_norm_g, ffn2_w_gate, ffn2_w_up, ffn2_w_down, final_norm_g, loss_target, m_ffn1_norm_g, m_ffn1_w_gate, m_ffn1_w_up, m_ffn1_w_down, m_mix_norm_g, m_w_in, m_ret_norm_g, m_gla_w_a2, m_gla_b_a, m_gla_norm_g, m_w_out, m_ffn2_norm_g, m_ffn2_w_gate, m_ffn2_w_up, m_ffn2_w_down, m_final_norm_g, v_ffn1_norm_g, v_ffn1_w_gate, v_ffn1_w_up, v_ffn1_w_down, v_mix_norm_g, v_w_in, v_ret_norm_g, v_gla_w_a2, v_gla_b_a, v_gla_norm_g, v_w_out, v_ffn2_norm_g, v_ffn2_w_gate, v_ffn2_w_up, v_ffn2_w_down, v_final_norm_g):
    t = x.shape[1]
    xs = x.reshape(t, D_MODEL)
    target = loss_target.reshape(t, D_MODEL)
    chip = 2 * lax.axis_index("x") + lax.axis_index("y")
    c_arr = lax.axis_index("c").astype(jnp.int32).reshape(1)

    me_arr = chip.astype(jnp.int32).reshape(1)

    pad_rows = _pad_in_rows

    def own_block(gathered, shard):
        return lax.dynamic_update_slice(gathered, shard[None], (chip,) + (0,) * shard.ndim)

    ffn1_shard = _halves(jnp.stack([ffn1_w_gate[0].T, ffn1_w_up[0].T, ffn1_w_down[0]], axis=0).astype(BF16))
    rest_shards = [_halves(jnp.stack([ffn2_w_gate[0].T, ffn2_w_up[0].T, ffn2_w_down[0]], axis=0).astype(BF16)),
                   _halves(pad_rows(w_in[0].T).astype(BF16)[None]),
                   _halves(w_out.astype(BF16)),
                   jnp.concatenate([gla_w_a2.reshape(GATE_RANK, 64), jnp.zeros((GATE_RANK, 64), F32)],
                                   axis=1).reshape(1, 2, 8, LANES)]
    ffn1_all = _run_hosted(_gather_plan([ffn1_shard]), "gather_ffn1")[0]
    ffn1_w = own_block(ffn1_all, ffn1_shard).reshape(N_CHIPS, 3, FF_SHARD, D_MODEL)

    def rest_weights(gathered):
        ffn2_all, win_all, wout_all, wa2_all = [own_block(g, s) for g, s in zip(gathered, rest_shards)]
        win_t = win_all.reshape(N_CHIPS, IN_ROWS, D_MODEL)
        w_in_pt = _pad_w_in_t(jnp.concatenate([win_t[p, 0:IN_SHARD] for p in range(N_CHIPS)], axis=0))
        wa2_p = jnp.pad(
            wa2_all.reshape(N_CHIPS, GATE_RANK, LANES).transpose(1, 0, 2).reshape(GATE_RANK, HEADS * LANES),
            ((0, LANES - GATE_RANK), (0, 0))).astype(BF16)
        return (ffn2_all.reshape(N_CHIPS, 3, FF_SHARD, D_MODEL), w_in_pt, wout_all.reshape(D_MODEL, D_MODEL), wa2_p)

    def pair_sums(grads, tag):
        halves = [g.reshape(g.shape[0], 2, g.shape[1] // 2, g.shape[2]) for g in grads]
        recv = _pair_exchange(halves, "pair_exchange_" + tag)
        return [_pair_add(g, r, c_arr, "pair_add_%s%d" % (tag, k)) for k, (g, r) in enumerate(zip(halves, recv))]

    early_sums = []

    def early(grads):
        early_sums.extend(pair_sums(grads, "early"))
        return _chip_exchange_plan(early_sums)

    ba_p = jnp.pad(gla_b_a.reshape(HEADS, 64), ((0, 0), (0, 64))).reshape(1, HEADS * LANES)
    fb = _forward_backward(xs, target, ffn1_w, None, ba_p, ffn1_norm_g, mix_norm_g, ret_norm_g, gla_norm_g,
                           ffn2_norm_g, final_norm_g.reshape(1, D_MODEL), rest_plan=_gather_plan(rest_shards),
                           rest_weights=rest_weights, early=early)
    (loss_blk, grad_x, g_gate1, g_up1, g_down1, _, _, _, _, _, g_wa2_p,
     d_ba_p, d_ffn1_g, d_mix_g, d_ffn2_g, d_final_g, d_ret_g, d_gla_g, early_arrived) = fb
    late_sums = pair_sums([g_gate1, g_up1, g_down1], "late")
    late_arrived = _run_hosted(_chip_exchange_plan(late_sums), "chip_exchange_late")
    sums, arrived = late_sums + early_sums, late_arrived + early_arrived
    mine = [_chip_sum(s, r, me_arr, "chip_sum_%d" % k) for k, (s, r) in enumerate(zip(sums, arrived))]
    other = _pair_share(mine)

    g_wa2 = g_wa2_p[0][0:GATE_RANK].reshape(GATE_RANK, HEADS, LANES)[:, :, 0:64].reshape(GATE_RANK, 256)
    d_ba = d_ba_p.reshape(HEADS, LANES)[:, 0:64].reshape(1, 256)
    loss_row = jnp.pad(loss_blk[0:1, 0:1], ((0, 0), (0, D_MODEL - 1)))
    small_local = _pack_small(d_ffn1_g, d_mix_g, d_ffn2_g, d_final_g, d_ret_g, d_gla_g, d_ba, g_wa2, 256, loss_row)
    small_sum = _small_allreduce(small_local)
    loss = small_sum[6, 0]
    sg = _unpack_small(small_sum, 256)
    wa2_grad = lax.dynamic_slice(sg[7], (0, 0, 64 * chip), (1, GATE_RANK, 64))
    small_g = _pack_small(*sg[:7], wa2_grad, 64)
    small_w = _pack_small(ffn1_norm_g, mix_norm_g, ffn2_norm_g, final_norm_g, ret_norm_g, gla_norm_g, gla_b_a, gla_w_a2, 64)
    small_m = _pack_small(m_ffn1_norm_g, m_mix_norm_g, m_ffn2_norm_g, m_final_norm_g, m_ret_norm_g, m_gla_norm_g,
                          m_gla_b_a, m_gla_w_a2, 64)
    small_v = _pack_small(v_ffn1_norm_g, v_mix_norm_g, v_ffn2_norm_g, v_final_norm_g, v_ret_norm_g, v_gla_norm_g,
                          v_gla_b_a, v_gla_w_a2, 64)
    small_out = _adamw(small_w, small_g, small_m, small_v, "adamw_small")
    s_grad = _unpack_small(small_g, 64)
    s_delta, s_m, s_v = (_unpack_small(o, 64) for o in small_out)

    def big(k, w, m, v, name, to_2d, from_2d):
        outs4 = _adamw_halves(to_2d(w), mine[k], other[k], to_2d(m), to_2d(v), c_arr, name)
        return [from_2d(z) for z in outs4]

    plain = (lambda w: w[0], lambda z: z[None])
    transposed = (lambda w: w[0].T, lambda z: z.T[None])
    in_proj = (lambda w: pad_rows(w[0].T), lambda z: z[0:IN_SHARD].T[None])
    r_g1 = big(0, ffn1_w_gate, m_ffn1_w_gate, v_ffn1_w_gate, "adamw_ffn1_gate", *transposed)
    r_u1 = big(1, ffn1_w_up, m_ffn1_w_up, v_ffn1_w_up, "adamw_ffn1_up", *transposed)
    r_d1 = big(2, ffn1_w_down, m_ffn1_w_down, v_ffn1_w_down, "adamw_ffn1_down", *plain)
    r_g2 = big(3, ffn2_w_gate, m_ffn2_w_gate, v_ffn2_w_gate, "adamw_ffn2_gate", *transposed)
    r_u2 = big(4, ffn2_w_up, m_ffn2_w_up, v_ffn2_w_up, "adamw_ffn2_up", *transposed)
    r_d2 = big(5, ffn2_w_down, m_ffn2_w_down, v_ffn2_w_down, "adamw_ffn2_down", *plain)
    r_in = big(6, w_in, m_w_in, v_w_in, "adamw_w_in", *in_proj)
    r_out = big(7, w_out, m_w_out, v_w_out, "adamw_w_out", *plain)

    def leaves(k, smalls):
        n1, nm, n2, nf, nret, ngla, ba, wa2 = smalls
        return [n1, r_g1[k], r_u1[k], r_d1[k], nm, r_in[k], nret, wa2, ba, ngla, r_out[k], n2, r_g2[k], r_u2[k], r_d2[k], nf]

    outs = [loss, grad_x.reshape(x.shape)]
    outs += leaves(0, s_grad) + leaves(1, s_delta) + leaves(2, s_m) + leaves(3, s_v)
    return tuple(outs)
```

```python
import functools

import jax
import jax.numpy as jnp
from jax import lax
from jax.experimental import pallas as pl
from jax.experimental.pallas import tpu as pltpu

F32, BF16 = jnp.float32, jnp.bfloat16
MESH = pl.DeviceIdType.MESH
ANY = pl.BlockSpec(memory_space=pl.ANY)

D_MODEL = 1024
D_FF = 2816
N_CHIPS = 4
FF_SHARD = D_FF // N_CHIPS
IN_WIDTH = 3088
IN_SHARD = IN_WIDTH // N_CHIPS
IN_ROWS = 800
CHUNK = 64
HEADS = 4
LANES = 128
HEAD_BLOCK = 4 * LANES
PROJ_P = 2 * HEADS * HEAD_BLOCK + LANES
GATE_RANK = 16
QK_SCALE = 0.125
GATE_NORM = 16.0
RMS_EPS = 1e-6
ROPE_BASE = 10000.0
ADAM_LR, ADAM_B1, ADAM_B2, ADAM_EPS, ADAM_WD, ADAM_STEP = 0.001, 0.9, 0.999, 1e-08, 0.01, 10
SMALL_ROWS = 32
TOKEN_TILE = 512
ATTN_TILE = 512

_ARB2 = pltpu.CompilerParams(dimension_semantics=("arbitrary", "arbitrary"))
_ARB1 = pltpu.CompilerParams(dimension_semantics=("arbitrary",))
_ARB3 = pltpu.CompilerParams(dimension_semantics=("arbitrary", "arbitrary", "arbitrary"))


def _dot(a, b):
    return jnp.dot(a, b, preferred_element_type=F32)


def _dot_nt(a, b):
    return lax.dot_general(a, b, (((1,), (1,)), ((), ())), preferred_element_type=F32)


def _dot_tn(a, b):
    return lax.dot_general(a, b, (((0,), (0,)), ((), ())), preferred_element_type=F32)


def _rms_scale(xv):
    return lax.rsqrt(jnp.mean(xv * xv, axis=-1, keepdims=True) + RMS_EPS)


def _rms_bwd(dh, xv, g):
    r = _rms_scale(xv)
    xhat = xv * r
    dxhat = dh * g
    dx = r * (dxhat - xhat * jnp.mean(dxhat * xhat, axis=-1, keepdims=True))
    return dx, jnp.sum(dh * xhat, axis=0, keepdims=True)


def _silu_grad(a, sg):
    return sg * (1.0 + a * (1.0 - sg))


class _Hosted:
    def __init__(self, arrays, out_shapes, n_sems, start, finish):
        self.arrays, self.out_shapes, self.n_sems = list(arrays), list(out_shapes), n_sems
        self.start, self.finish = start, finish


def _call(body, args, *, name, grid, in_specs, out_specs, out_shape, scratch_shapes, compiler_params, hosted=None):
    if hosted is None:
        outs = pl.pallas_call(body, name=name, grid=grid, in_specs=in_specs, out_specs=out_specs, out_shape=out_shape,
                              scratch_shapes=scratch_shapes, compiler_params=compiler_params)(*args)
        return list(outs), []
    n_in, n_out, n_sc, nh = len(in_specs), len(out_specs), len(scratch_shapes), len(hosted.arrays)

    def wrapped(*refs):
        ins, h_in = refs[:n_in], refs[n_in:n_in + nh]
        outs, h_out = refs[n_in + nh:n_in + nh + n_out], refs[n_in + nh + n_out:n_in + 2 * nh + n_out]
        rest = refs[n_in + 2 * nh + n_out:]
        scratch, (send_sems, recv_sems) = rest[:n_sc], rest[n_sc:]
        first = functools.reduce(jnp.logical_and, [pl.program_id(d) == 0 for d in range(len(grid))])
        last = functools.reduce(jnp.logical_and, [pl.program_id(d) == n - 1 for d, n in enumerate(grid)])

        @pl.when(first)
        def _():
            hosted.start(h_in, h_out, send_sems, recv_sems)

        body(*ins, *outs, *scratch)

        @pl.when(last)
        def _():
            hosted.finish(h_in, h_out, send_sems, recv_sems)

    sems = [pltpu.SemaphoreType.DMA((hosted.n_sems,)), pltpu.SemaphoreType.DMA((hosted.n_sems,))]
    outs = pl.pallas_call(
        wrapped, name=name, grid=grid, in_specs=list(in_specs) + [ANY] * nh, out_specs=list(out_specs) + [ANY] * nh,
        out_shape=list(out_shape) + hosted.out_shapes, scratch_shapes=list(scratch_shapes) + sems,
        compiler_params=compiler_params)(*args, *hosted.arrays)
    return list(outs[:n_out]), list(outs[n_out:])


def _run_hosted(hosted, name):
    nh = len(hosted.arrays)

    def body(*refs):
        h_in, h_out, (send_sems, recv_sems) = refs[:nh], refs[nh:2 * nh], refs[2 * nh:]
        hosted.start(h_in, h_out, send_sems, recv_sems)
        hosted.finish(h_in, h_out, send_sems, recv_sems)

    sems = [pltpu.SemaphoreType.DMA((hosted.n_sems,)), pltpu.SemaphoreType.DMA((hosted.n_sems,))]
    return list(pl.pallas_call(body, name=name, in_specs=[ANY] * nh, out_specs=[ANY] * nh,
                               out_shape=hosted.out_shapes, scratch_shapes=sems)(*hosted.arrays))


def _ffn_weight_specs():
    blk = (None, None, FF_SHARD, D_MODEL)
    return [pl.BlockSpec(blk, lambda i, j, k=kind: (j, k, 0, 0)) for kind in range(3)]


def _ffn_fwd(x, g, ffn_w, name, hosted=None):
    t = x.shape[0]
    tm = min(t, TOKEN_TILE)

    def body(x_ref, g_ref, wg_ref, wu_ref, wd_ref, xo_ref, a_ref, u_ref, h_ref, acc_ref):
        j = pl.program_id(1)

        @pl.when(j == 0)
        def _():
            xv = x_ref[...]
            h_ref[...] = ((xv * _rms_scale(xv)) * g_ref[...]).astype(BF16)
            acc_ref[...] = jnp.zeros_like(acc_ref)

        h = h_ref[...]
        a = _dot_nt(h, wg_ref[...])
        u = _dot_nt(h, wu_ref[...])
        a_ref[...] = a.astype(BF16)
        u_ref[...] = u.astype(BF16)
        hid = (a * jax.nn.sigmoid(a)) * u
        acc_ref[...] += _dot(hid.astype(BF16), wd_ref[...])

        @pl.when(j == N_CHIPS - 1)
        def _():
            xo_ref[...] = x_ref[...] + 0.5 * acc_ref[...]

    tok = pl.BlockSpec((tm, D_MODEL), lambda i, j: (i, 0))
    act = pl.BlockSpec((None, tm, FF_SHARD), lambda i, j: (j, i, 0))
    return _call(
        body, (x, g, ffn_w, ffn_w, ffn_w), name=name, grid=(t // tm, N_CHIPS),
        in_specs=[tok, pl.BlockSpec((1, D_MODEL), lambda i, j: (0, 0))] + _ffn_weight_specs(),
        out_specs=[tok, act, act, tok],
        out_shape=[jax.ShapeDtypeStruct((t, D_MODEL), F32),
                   jax.ShapeDtypeStruct((N_CHIPS, t, FF_SHARD), BF16),
                   jax.ShapeDtypeStruct((N_CHIPS, t, FF_SHARD), BF16),
                   jax.ShapeDtypeStruct((t, D_MODEL), BF16)],
        scratch_shapes=[pltpu.VMEM((tm, D_MODEL), F32)],
        compiler_params=_ARB2, hosted=hosted)


def _ffn_bwd(dxo, x, g, a4, u4, ffn_w, name, hosted=None):
    t = x.shape[0]
    tm = min(t, TOKEN_TILE)

    def body(dxo_ref, x_ref, g_ref, a_ref, u_ref, wg_ref, wu_ref, wd_ref,
             da_ref, du_ref, hid_ref, dob_ref, dx_ref, dg_ref, acc_ref):
        i, j = pl.program_id(0), pl.program_id(1)

        @pl.when(j == 0)
        def _():
            dob_ref[...] = (0.5 * dxo_ref[...]).astype(BF16)
            acc_ref[...] = jnp.zeros_like(acc_ref)

        @pl.when((i == 0) & (j == 0))
        def _():
            dg_ref[...] = jnp.zeros_like(dg_ref)

        dhid = _dot_nt(dob_ref[...], wd_ref[...])
        a = a_ref[...].astype(F32)
        u = u_ref[...].astype(F32)
        sg = jax.nn.sigmoid(a)
        s = a * sg
        hid_ref[...] = (s * u).astype(BF16)
        du = (dhid * s).astype(BF16)
        da = (dhid * u * _silu_grad(a, sg)).astype(BF16)
        du_ref[...] = du
        da_ref[...] = da
        acc_ref[...] += _dot(da, wg_ref[...]) + _dot(du, wu_ref[...])

        @pl.when(j == N_CHIPS - 1)
        def _():
            dx, dg = _rms_bwd(acc_ref[...], x_ref[...], g_ref[...])
            dx_ref[...] = dxo_ref[...] + dx
            dg_ref[...] += dg

    tok = pl.BlockSpec((tm, D_MODEL), lambda i, j: (i, 0))
    act = pl.BlockSpec((None, tm, FF_SHARD), lambda i, j: (j, i, 0))
    row = pl.BlockSpec((1, D_MODEL), lambda i, j: (0, 0))
    act_shape = jax.ShapeDtypeStruct((N_CHIPS, t, FF_SHARD), BF16)
    return _call(
        body, (dxo, x, g, a4, u4, ffn_w, ffn_w, ffn_w), name=name, grid=(t // tm, N_CHIPS),
        in_specs=[tok, tok, row, act, act] + _ffn_weight_specs(),
        out_specs=[act, act, act, tok, tok, row],
        out_shape=[act_shape, act_shape, act_shape,
                   jax.ShapeDtypeStruct((t, D_MODEL), BF16),
                   jax.ShapeDtypeStruct((t, D_MODEL), F32),
                   jax.ShapeDtypeStruct((1, D_MODEL), F32)],
        scratch_shapes=[pltpu.VMEM((tm, D_MODEL), F32)],
        compiler_params=_ARB2, hosted=hosted)


def _matmul_tn(a, b, name, tka=None, out_dtype=F32):
    a3, b3 = a.ndim == 3, b.ndim == 3
    nb = a.shape[0] if a3 else (b.shape[0] if b3 else 1)
    t, ka, n = a.shape[-2], a.shape[-1], b.shape[-1]
    tka = ka if tka is None else tka
    tk = min(t, 2 * TOKEN_TILE)
    nk = t // tk

    def body(a_ref, b_ref, o_ref, acc_ref):
        k = pl.program_id(2)

        @pl.when(k == 0)
        def _():
            acc_ref[...] = jnp.zeros_like(acc_ref)

        acc_ref[...] += _dot_tn(a_ref[...].astype(BF16), b_ref[...].astype(BF16))

        @pl.when(k == nk - 1)
        def _():
            o_ref[...] = acc_ref[...].astype(out_dtype)

    a_spec = (pl.BlockSpec((None, tk, tka), lambda i, j, k: (i, k, j)) if a3
              else pl.BlockSpec((tk, tka), lambda i, j, k: (k, j)))
    b_spec = (pl.BlockSpec((None, tk, n), lambda i, j, k: (i, k, 0)) if b3
              else pl.BlockSpec((tk, n), lambda i, j, k: (k, 0)))
    return pl.pallas_call(
        body, name=name, grid=(nb, ka // tka, t // tk),
        in_specs=[a_spec, b_spec],
        out_specs=pl.BlockSpec((None, tka, n), lambda i, j, k: (i, j, 0)),
        out_shape=jax.ShapeDtypeStruct((nb, ka, n), out_dtype),
        scratch_shapes=[pltpu.VMEM((tka, n), F32)],
        compiler_params=_ARB3,
    )(a, b)


def _matmul_nt(a, w, name, out_dtype=F32):
    t, k = a.shape
    n = w.shape[0]
    tm = min(t, TOKEN_TILE)

    def body(a_ref, w_ref, o_ref):
        o_ref[...] = _dot_nt(a_ref[...].astype(BF16), w_ref[...]).astype(out_dtype)

    return pl.pallas_call(
        body, name=name, grid=(t // tm,),
        in_specs=[pl.BlockSpec((tm, k), lambda i: (i, 0)), pl.BlockSpec((n, k), lambda i: (0, 0))],
        out_specs=pl.BlockSpec((tm, n), lambda i: (i, 0)),
        out_shape=jax.ShapeDtypeStruct((t, n), out_dtype),
        compiler_params=_ARB1,
    )(a, w)


def _mixer_in_bwd(dproj, w_in_pt, dres, x, g, name):
    t, k = dproj.shape
    tm = min(t, TOKEN_TILE)

    def body(a_ref, w_ref, dres_ref, x_ref, g_ref, dx_ref, dg_ref):
        @pl.when(pl.program_id(0) == 0)
        def _():
            dg_ref[...] = jnp.zeros_like(dg_ref)

        dh = _dot(a_ref[...], w_ref[...])
        dx, dg = _rms_bwd(dh, x_ref[...], g_ref[...])
        dx_ref[...] = dres_ref[...] + dx
        dg_ref[...] += dg

    tok = pl.BlockSpec((tm, D_MODEL), lambda i: (i, 0))
    row = pl.BlockSpec((1, D_MODEL), lambda i: (0, 0))
    return pl.pallas_call(
        body, name=name, grid=(t // tm,),
        in_specs=[pl.BlockSpec((tm, k), lambda i: (i, 0)), pl.BlockSpec((k, D_MODEL), lambda i: (0, 0)), tok, tok, row],
        out_specs=[tok, row],
        out_shape=[jax.ShapeDtypeStruct((t, D_MODEL), F32), jax.ShapeDtypeStruct((1, D_MODEL), F32)],
        compiler_params=_ARB1,
    )(dproj, w_in_pt, dres, x, g)


def _mixer_in_fwd(x, g, w_in_pt, name):
    t = x.shape[0]
    tm = min(t, TOKEN_TILE)
    tn = PROJ_P // 3

    def body(x_ref, g_ref, w_ref, p_ref, h_ref):
        @pl.when(pl.program_id(1) == 0)
        def _():
            xv = x_ref[...]
            h_ref[...] = ((xv * _rms_scale(xv)) * g_ref[...]).astype(BF16)

        p_ref[...] = _dot_nt(h_ref[...], w_ref[...])

    tok = pl.BlockSpec((tm, D_MODEL), lambda i, j: (i, 0))
    return pl.pallas_call(
        body, name=name, grid=(t // tm, 3),
        in_specs=[tok, pl.BlockSpec((1, D_MODEL), lambda i, j: (0, 0)),
                  pl.BlockSpec((tn, D_MODEL), lambda i, j: (j, 0))],
        out_specs=[pl.BlockSpec((tm, tn), lambda i, j: (i, j)), tok],
        out_shape=[jax.ShapeDtypeStruct((t, PROJ_P), F32), jax.ShapeDtypeStruct((t, D_MODEL), BF16)],
        compiler_params=_ARB2,
    )(x, g, w_in_pt)


def _mixer_out_fwd(o_ret, o_gla, w_out, x, name):
    t = x.shape[0]
    tm = min(t, TOKEN_TILE)
    half = HEADS * LANES

    def body(a_ref, b_ref, w_ref, x_ref, o_ref):
        o_ref[...] = x_ref[...] + _dot(a_ref[...], w_ref[0:half, :]) + _dot(b_ref[...], w_ref[half:2 * half, :])

    tok = pl.BlockSpec((tm, D_MODEL), lambda i: (i, 0))
    hb = pl.BlockSpec((tm, half), lambda i: (i, 0))
    return pl.pallas_call(
        body, name=name, grid=(t // tm,),
        in_specs=[hb, hb, pl.BlockSpec((2 * half, D_MODEL), lambda i: (0, 0)), tok],
        out_specs=tok, out_shape=jax.ShapeDtypeStruct((t, D_MODEL), F32),
        compiler_params=_ARB1,
    )(o_ret, o_gla, w_out, x)


def _final_loss(x, g, target, name):
    t = x.shape[0]
    tm = min(t, TOKEN_TILE)

    def body(x_ref, g_ref, t_ref, l_ref, dx_ref, dg_ref):
        @pl.when(pl.program_id(0) == 0)
        def _():
            l_ref[...] = jnp.zeros_like(l_ref)
            dg_ref[...] = jnp.zeros_like(dg_ref)

        xv = x_ref[...]
        gv = g_ref[...]
        err = (xv * _rms_scale(xv)) * gv - t_ref[...]
        l_ref[...] += 0.5 * jnp.sum(jnp.mean(err * err, axis=-1, keepdims=True), axis=0, keepdims=True)
        dx, dg = _rms_bwd(err * (1.0 / D_MODEL), xv, gv)
        dx_ref[...] = dx
        dg_ref[...] += dg

    tok = pl.BlockSpec((tm, D_MODEL), lambda i: (i, 0))
    row = pl.BlockSpec((1, D_MODEL), lambda i: (0, 0))
    return pl.pallas_call(
        body, name=name, grid=(t // tm,),
        in_specs=[tok, row, tok],
        out_specs=[pl.BlockSpec((8, LANES), lambda i: (0, 0)), tok, row],
        out_shape=[jax.ShapeDtypeStruct((8, LANES), F32), jax.ShapeDtypeStruct((t, D_MODEL), F32),
                   jax.ShapeDtypeStruct((1, D_MODEL), F32)],
        compiler_params=_ARB1,
    )(x, g, target)


def _rot(v, cos, sa, sb):
    return v * cos + pltpu.roll(v, 96, 1) * sa + pltpu.roll(v, 32, 1) * sb


def _rot_t(d, cos, sa, sb):
    return d * cos + pltpu.roll(d * sa, 32, 1) + pltpu.roll(d * sb, 96, 1)


def _bmm(a, b):
    return jnp.einsum("cik,ckj->cij", a, b, preferred_element_type=F32)


def _bmm_nt(a, b):
    return jnp.einsum("cik,cjk->cij", a, b, preferred_element_type=F32)


def _bmm_tn(a, b):
    return jnp.einsum("cki,ckj->cij", a, b, preferred_element_type=F32)


def _masked_sum(mask, x):
    hi = x.astype(BF16)
    r1 = x - hi.astype(F32)
    mid = r1.astype(BF16)
    lo = (r1 - mid.astype(F32)).astype(BF16)
    return _bmm(mask, hi) + _bmm(mask, mid) + _bmm(mask, lo)


def _tile_inputs(is_ret, proj_ref, aux, nc):
    shape3 = (nc, CHUNK, LANES)
    q_raw = proj_ref[:, 0:LANES]
    k_raw = proj_ref[:, LANES:2 * LANES]
    v = proj_ref[:, 2 * LANES:3 * LANES]
    gate = proj_ref[:, 3 * LANES:4 * LANES]
    ri = lax.broadcasted_iota(jnp.int32, (nc, CHUNK, CHUNK), 1)
    ci = lax.broadcasted_iota(jnp.int32, (nc, CHUNK, CHUNK), 2)
    if is_ret:
        cos_ref, sa_ref, sb_ref, lg_ref = aux
        cos, sa, sb = cos_ref[...], sa_ref[...], sb_ref[...]
        q = _rot(q_raw, cos, sa, sb)
        k = _rot(k_raw, cos, sa, sb) * QK_SCALE
        steps = (lax.broadcasted_iota(jnp.int32, shape3, 1) + 1).astype(F32)
        b = steps * lg_ref[...]
        logit = None
    else:
        glow_ref, wa2_ref, ba_ref = aux
        logit = _dot(glow_ref[...].astype(BF16), wa2_ref[...]) + ba_ref[...]
        la = (jnp.minimum(logit, 0.0) - jnp.log1p(jnp.exp(-jnp.abs(logit)))) * (1.0 / GATE_NORM)
        b = _masked_sum((ci <= ri).astype(BF16), la.reshape(shape3))
        q = q_raw * QK_SCALE
        k = k_raw
    return q.reshape(shape3), k.reshape(shape3), v.reshape(shape3), gate, b, logit, ri, ci


def _tile_scores(q, k, b, ri, ci):
    mid = b[:, CHUNK // 2 - 1:CHUNK // 2, :]
    ep = jnp.exp(b - mid)
    en = jnp.exp(mid - b)
    qt, kt, qh, kh = q * ep, k * en, q * en, k * ep
    low = _bmm_nt(qt.astype(BF16), kt.astype(BF16))
    upp = _bmm_nt(qh.astype(BF16), kh.astype(BF16))
    scores = jnp.where(ci <= ri, low, upp)
    return scores, ep, en, qt, kt, qh, kh


def _attn_specs(is_ret, t, tb, imap_t):
    nb = t // tb
    base = 0 if is_ret else HEADS
    proj = pl.BlockSpec((tb, HEAD_BLOCK), lambda h, i: (imap_t(i), base + h))
    lane_t = pl.BlockSpec((tb, LANES), lambda h, i: (imap_t(i), 0))
    if is_ret:
        aux = [lane_t, lane_t, lane_t, pl.BlockSpec((None, 1, LANES), lambda h, i: (h, 0, 0))]
    else:
        aux = [pl.BlockSpec((tb, LANES), lambda h, i: (imap_t(i), PROJ_P // LANES - 1)),
               pl.BlockSpec((LANES, LANES), lambda h, i: (0, h)),
               pl.BlockSpec((1, LANES), lambda h, i: (0, h))]
    gain = pl.BlockSpec((1, LANES), lambda h, i: (0, h))
    head_t = pl.BlockSpec((tb, LANES), lambda h, i: (imap_t(i), h))
    state = pl.BlockSpec((None, tb // CHUNK, LANES, LANES), lambda h, i: (h, imap_t(i), 0, 0))
    return nb, proj, aux, gain, head_t, state


def _attn_fwd(is_ret, proj, aux_arrays, gain, name):
    t = proj.shape[0]
    tb = min(t, ATTN_TILE)
    nc = tb // CHUNK
    n_aux = 4 if is_ret else 3
    nb, proj_spec, aux_specs, gain_spec, head_t, state_spec = _attn_specs(is_ret, t, tb, lambda i: i)

    def body(*refs):
        proj_ref = refs[0]
        aux = refs[1:1 + n_aux]
        gn_ref, ofin_ref, oraw_ref, st_ref, state = refs[1 + n_aux:]

        @pl.when(pl.program_id(1) == 0)
        def _():
            state[...] = jnp.zeros_like(state)

        q, k, v, gate, b, _, ri, ci = _tile_inputs(is_ret, proj_ref, aux, nc)
        scores = _tile_scores(q, k, b, ri, ci)[0]
        vb = v.astype(BF16)
        intra = _bmm(scores.astype(BF16), vb)
        b_last = b[:, CHUNK - 1:CHUNK, :]
        e_last = jnp.exp(b_last)
        grow = _bmm_tn(vb, (k * jnp.exp(b_last - b)).astype(BF16))
        st = state[...]
        for c in range(nc):
            st_ref[c] = st
            st = st * e_last[c] + grow[c]
        state[...] = st
        inter = _bmm_nt((q * jnp.exp(b)).astype(BF16), st_ref[...].astype(BF16))
        out = (intra + inter).reshape(tb, LANES)
        oraw_ref[...] = out
        normed = out * _rms_scale(out)
        ofin_ref[...] = ((normed * gn_ref[...]) * (gate * jax.nn.sigmoid(gate))).astype(BF16)

    width = HEADS * LANES
    return pl.pallas_call(
        body, name=name, grid=(HEADS, nb),
        in_specs=[proj_spec] + aux_specs + [gain_spec],
        out_specs=[head_t, head_t, state_spec],
        out_shape=[jax.ShapeDtypeStruct((t, width), BF16), jax.ShapeDtypeStruct((t, width), F32),
                   jax.ShapeDtypeStruct((HEADS, t // CHUNK, LANES, LANES), F32)],
        scratch_shapes=[pltpu.VMEM((LANES, LANES), F32)],
        compiler_params=_ARB2,
    )(proj, *aux_arrays, gain)


def _attn_bwd(is_ret, proj, aux_arrays, gain, o_raw, states, d_out, name):
    t = proj.shape[0]
    tb = min(t, ATTN_TILE)
    nc = tb // CHUNK
    n_aux = 4 if is_ret else 3
    nblk = t // tb
    nb, proj_spec, aux_specs, gain_spec, head_t, state_spec = _attn_specs(is_ret, t, tb, lambda i: nblk - 1 - i)
    base = 0 if is_ret else HEADS
    dout_spec = pl.BlockSpec((tb, LANES), lambda h, i: (nblk - 1 - i, base + h))

    def body(*refs):
        proj_ref = refs[0]
        aux = refs[1:1 + n_aux]
        gn_ref, oraw_ref, st_ref, dfin_ref = refs[1 + n_aux:5 + n_aux]
        if is_ret:
            dproj_ref, dgn_ref, dstate, dafter_ref = refs[5 + n_aux:]
        else:
            dproj_ref, dgn_ref, dlogit_ref, dba_ref, dstate, dafter_ref = refs[5 + n_aux:]

        @pl.when(pl.program_id(1) == 0)
        def _():
            dstate[...] = jnp.zeros_like(dstate)
            dgn_ref[...] = jnp.zeros_like(dgn_ref)
            if not is_ret:
                dba_ref[...] = jnp.zeros_like(dba_ref)

        shape3 = (nc, CHUNK, LANES)
        q, k, v, gate, b, logit, ri, ci = _tile_inputs(is_ret, proj_ref, aux, nc)
        scores, ep, en, qt, kt, qh, kh = _tile_scores(q, k, b, ri, ci)
        eb = jnp.exp(b)
        qe = q * eb
        b_last = b[:, CHUNK - 1:CHUNK, :]
        e_last = jnp.exp(b_last)
        ekd = jnp.exp(b_last - b)
        kd = k * ekd

        gn = gn_ref[...]
        out = oraw_ref[...]
        r = _rms_scale(out)
        normed = out * r
        sg = jax.nn.sigmoid(gate)
        dfin = dfin_ref[...]
        dgate = dfin * (normed * gn) * _silu_grad(gate, sg)
        dpre = dfin * (gate * sg)
        dgn_ref[...] += jnp.sum(dpre * normed, axis=0, keepdims=True)
        dnormed = dpre * gn
        d_o = r * (dnormed - normed * jnp.mean(dnormed * normed, axis=-1, keepdims=True))
        dob, vb = d_o.reshape(shape3).astype(BF16), v.astype(BF16)

        dgrow = _bmm_tn(dob, qe.astype(BF16))
        dst = dstate[...]
        for c in reversed(range(nc)):
            dafter_ref[c] = dst
            dst = dst * e_last[c] + dgrow[c]
        dstate[...] = dst
        st = st_ref[...]
        dafter = dafter_ref[...]
        stb, dafter_b = st.astype(BF16), dafter.astype(BF16)

        qtb, ktb, qhb, khb = qt.astype(BF16), kt.astype(BF16), qh.astype(BF16), kh.astype(BF16)
        scores_t = jnp.where(ci >= ri, _bmm_nt(ktb, qtb), _bmm_nt(khb, qhb))
        dv = _bmm(scores_t.astype(BF16), dob) + _bmm_nt(kd.astype(BF16), dafter_b)
        dsc = _bmm_nt(dob, vb)
        dsc_t = _bmm_nt(vb, dob)
        dqe = _bmm(dob, stb)
        dkd = _bmm(vb, dafter_b)
        dqt = _bmm(jnp.where(ci <= ri, dsc, 0.0).astype(BF16), ktb)
        dqh = _bmm(jnp.where(ci <= ri, 0.0, dsc).astype(BF16), khb)
        dkt = _bmm(jnp.where(ci >= ri, dsc_t, 0.0).astype(BF16), qtb)
        dkh = _bmm(jnp.where(ci >= ri, 0.0, dsc_t).astype(BF16), qhb)
        dq = (dqt * ep + dqh * en + dqe * eb).reshape(tb, LANES)
        dk = (dkt * en + dkh * ep + dkd * ekd).reshape(tb, LANES)

        if is_ret:
            cos_ref, sa_ref, sb_ref, _ = aux
            cos, sa, sb = cos_ref[...], sa_ref[...], sb_ref[...]
            dq_raw = _rot_t(dq, cos, sa, sb)
            dk_raw = _rot_t(dk, cos, sa, sb) * QK_SCALE
        else:
            dq_raw = dq * QK_SCALE
            dk_raw = dk
            db = dqt * qt - dkt * kt - dqh * qh + dkh * kh + dqe * qe - dkd * kd
            db_last = (jnp.sum(dkd * kd, axis=1, keepdims=True)
                       + jnp.sum(dafter * st, axis=1, keepdims=True) * e_last)
            last_row = lax.broadcasted_iota(jnp.int32, shape3, 1) == CHUNK - 1
            db = db + jnp.where(last_row, db_last, 0.0)
            dla = _masked_sum((ci >= ri).astype(BF16), db).reshape(tb, LANES)
            dlogit = dla * (1.0 / GATE_NORM) * jax.nn.sigmoid(-logit)
            dlogit_ref[...] = dlogit.astype(BF16)
            dba_ref[...] += jnp.sum(dlogit, axis=0, keepdims=True)

        dproj_ref[:, 0:LANES] = dq_raw.astype(BF16)
        dproj_ref[:, LANES:2 * LANES] = dk_raw.astype(BF16)
        dproj_ref[:, 2 * LANES:3 * LANES] = dv.reshape(tb, LANES).astype(BF16)
        dproj_ref[:, 3 * LANES:4 * LANES] = dgate.astype(BF16)

    width = HEADS * LANES
    row_out = pl.BlockSpec((1, LANES), lambda h, i: (0, h))
    out_specs = [pl.BlockSpec((tb, HEAD_BLOCK), lambda h, i: (nblk - 1 - i, h)), row_out]
    out_shape = [jax.ShapeDtypeStruct((t, HEADS * HEAD_BLOCK), BF16), jax.ShapeDtypeStruct((1, width), F32)]
    if not is_ret:
        out_specs += [head_t, row_out]
        out_shape += [jax.ShapeDtypeStruct((t, width), BF16), jax.ShapeDtypeStruct((1, width), F32)]
    return pl.pallas_call(
        body, name=name, grid=(HEADS, nblk),
        in_specs=[proj_spec] + aux_specs + [gain_spec, head_t, state_spec, dout_spec],
        out_specs=out_specs, out_shape=out_shape,
        scratch_shapes=[pltpu.VMEM((LANES, LANES), F32), pltpu.VMEM((nc, LANES, LANES), F32)],
        compiler_params=_ARB2,
    )(proj, *aux_arrays, gain, o_raw, states, d_out)


def _place():
    x, y, c = lax.axis_index("x"), lax.axis_index("y"), lax.axis_index("c")
    chips = [(1 - x, y), (x, 1 - y), (1 - x, 1 - y)]
    return x, y, c, 2 * x + y, chips


def _gather_plan(arrs):
    na = len(arrs)

    def copies(ins, outs, send_sems, recv_sems):
        x, y, c, me, chips = _place()

        def ici(a, j, src_chip, to):
            return pltpu.make_async_remote_copy(
                src_ref=ins[a].at[:, c], dst_ref=outs[a].at[src_chip, :, c],
                send_sem=send_sems.at[6 * a + j], recv_sem=recv_sems.at[6 * a + j], device_id=to, device_id_type=MESH)

        def d2d(a, j, src_chip, half):
            blk = outs[a].at[src_chip, :, half]
            return pltpu.make_async_remote_copy(
                src_ref=blk, dst_ref=blk, send_sem=send_sems.at[6 * a + 3 + j], recv_sem=recv_sems.at[6 * a + 3 + j],
                device_id=(x, y, 1 - c), device_id_type=MESH)

        peers = [(a, j, px, py) for a in range(na) for j, (px, py) in enumerate(chips)]
        return c, me, peers, ici, d2d

    def start(*refs):
        c, me, peers, ici, _ = copies(*refs)
        for a, j, px, py in peers:
            ici(a, j, me, (px, py, c)).start()

    def finish(*refs):
        c, me, peers, ici, d2d = copies(*refs)
        for a, j, px, py in peers:
            ici(a, j, 2 * px + py, (px, py, c)).wait_recv()
            d2d(a, j, 2 * px + py, c).start()
        for a, j, px, py in peers:
            d2d(a, j, 2 * px + py, 1 - c).wait_recv()
        for a, j, px, py in peers:
            ici(a, j, me, (px, py, c)).wait_send()
            d2d(a, j, 2 * px + py, c).wait_send()

    return _Hosted(arrs, [jax.ShapeDtypeStruct((N_CHIPS,) + a.shape, a.dtype) for a in arrs], 6 * na, start, finish)


def _pair_exchange(grads, name):
    na = len(grads)

    def body(*refs):
        ins, outs = refs[:na], refs[na:2 * na]
        send_sems, recv_sems = refs[2 * na:]
        x, y, c, _, _ = _place()
        copies = [pltpu.make_async_remote_copy(
            src_ref=ins[a].at[:, 1 - c], dst_ref=outs[a], send_sem=send_sems.at[a], recv_sem=recv_sems.at[a],
            device_id=(x, y, 1 - c), device_id_type=MESH) for a in range(na)]
        for cp in copies:
            cp.start()
        for cp in copies:
            cp.wait()

    return pl.pallas_call(
        body, name=name,
        in_specs=[ANY] * na, out_specs=[ANY] * na,
        out_shape=[jax.ShapeDtypeStruct(g.shape[:1] + g.shape[2:], g.dtype) for g in grads],
        scratch_shapes=[pltpu.SemaphoreType.DMA((na,)), pltpu.SemaphoreType.DMA((na,))],
    )(*grads)


def _pair_add(grad, recv, c_arr, name):
    _, _, r, cols = grad.shape

    def body(c_ref, g_ref, r_ref, o_ref):
        o_ref[...] = (g_ref[...].astype(F32) + r_ref[...].astype(F32)).astype(BF16)

    return pl.pallas_call(
        body, name=name,
        grid_spec=pltpu.PrefetchScalarGridSpec(
            num_scalar_prefetch=1, grid=(N_CHIPS,),
            in_specs=[pl.BlockSpec((None, None, r, cols), lambda p, c_ref: (p, c_ref[0], 0, 0)),
                      pl.BlockSpec((None, r, cols), lambda p, c_ref: (p, 0, 0))],
            out_specs=pl.BlockSpec((None, r, cols), lambda p, c_ref: (p, 0, 0))),
        out_shape=jax.ShapeDtypeStruct((N_CHIPS, r, cols), BF16),
        compiler_params=_ARB1,
    )(c_arr, grad, recv)


def _chip_exchange_plan(sums):
    na = len(sums)

    def copies(ins, outs, send_sems, recv_sems):
        x, y, c, me, chips = _place()

        def copy(a, j, px, py, block, slot):
            return pltpu.make_async_remote_copy(
                src_ref=ins[a].at[block], dst_ref=outs[a].at[slot],
                send_sem=send_sems.at[3 * a + j], recv_sem=recv_sems.at[3 * a + j],
                device_id=(px, py, c), device_id_type=MESH)

        peers = [(a, j, px, py) for a in range(na) for j, (px, py) in enumerate(chips)]
        return me, peers, copy

    def start(*refs):
        me, peers, copy = copies(*refs)
        for a, j, px, py in peers:
            copy(a, j, px, py, 2 * px + py, me).start()

    def finish(*refs):
        me, peers, copy = copies(*refs)
        for a, j, px, py in peers:
            copy(a, j, px, py, me, 2 * px + py).wait_recv()
        for a, j, px, py in peers:
            copy(a, j, px, py, 2 * px + py, me).wait_send()

    return _Hosted(sums, [jax.ShapeDtypeStruct(s.shape, s.dtype) for s in sums], 3 * na, start, finish)


def _chip_sum(own, recv, me_arr, name):
    _, r, cols = recv.shape

    def body(me_ref, own_ref, r_ref, o_ref):
        o_ref[...] = jnp.zeros_like(o_ref)
        for q in range(N_CHIPS):
            @pl.when(me_ref[0] == q)
            def _():
                o_ref[...] += own_ref[...].astype(F32)

            @pl.when(me_ref[0] != q)
            def _():
                o_ref[...] += r_ref[q].astype(F32)

    return pl.pallas_call(
        body, name=name,
        grid_spec=pltpu.PrefetchScalarGridSpec(
            num_scalar_prefetch=1, grid=(1,),
            in_specs=[pl.BlockSpec((None, r, cols), lambda i, me_ref: (me_ref[0], 0, 0)),
                      pl.BlockSpec((N_CHIPS, r, cols), lambda i, me_ref: (0, 0, 0))],
            out_specs=pl.BlockSpec((r, cols), lambda i, me_ref: (0, 0))),
        out_shape=jax.ShapeDtypeStruct((r, cols), F32),
        compiler_params=_ARB1,
    )(me_arr, own, recv)


def _pair_share(halves):
    na = len(halves)

    def body(*refs):
        ins, outs = refs[:na], refs[na:2 * na]
        send_sems, recv_sems = refs[2 * na:]
        x, y, c, _, _ = _place()
        copies = [pltpu.make_async_remote_copy(
            src_ref=ins[a], dst_ref=outs[a], send_sem=send_sems.at[a], recv_sem=recv_sems.at[a],
            device_id=(x, y, 1 - c), device_id_type=MESH) for a in range(na)]
        for cp in copies:
            cp.start()
        for cp in copies:
            cp.wait()

    return pl.pallas_call(
        body, name="pair_share",
        in_specs=[ANY] * na, out_specs=[ANY] * na,
        out_shape=[jax.ShapeDtypeStruct(h.shape, h.dtype) for h in halves],
        scratch_shapes=[pltpu.SemaphoreType.DMA((na,)), pltpu.SemaphoreType.DMA((na,))],
    )(*halves)


def _small_allreduce(block):
    m, n = block.shape

    def body(x_ref, all_ref, sum_ref, send_sems, recv_sems, local_sem):
        x, y, c, _, chips = _place()
        me, sibling = (x, y, c), (x, y, 1 - c)

        def rows(px, py, pc):
            return all_ref.at[pl.ds((4 * px + 2 * py + pc) * m, m), :]

        def copy(k, blk, to, src=None):
            return pltpu.make_async_remote_copy(
                src_ref=rows(*blk) if src is None else src, dst_ref=rows(*blk),
                send_sem=send_sems.at[k], recv_sem=recv_sems.at[k], device_id=to, device_id_type=MESH)

        mine = pltpu.make_async_copy(x_ref, rows(*me), local_sem)
        mine.start()
        first = [copy(0, me, sibling, src=x_ref)]
        first += [copy(1 + j, me, (*chip, c), src=x_ref) for j, chip in enumerate(chips)]
        for cp in first:
            cp.start()
        passed = [copy(4 + j, (*chip, c), sibling) for j, chip in enumerate(chips)]
        for j, chip in enumerate(chips):
            copy(1 + j, (*chip, c), me).wait_recv()
            passed[j].start()
        copy(0, sibling, me).wait_recv()
        for j, chip in enumerate(chips):
            copy(4 + j, (*chip, 1 - c), me).wait_recv()
        for cp in first + passed:
            cp.wait_send()
        mine.wait()
        acc = all_ref[0:m, :]
        for d in range(1, 8):
            acc = acc + all_ref[d * m:(d + 1) * m, :]
        sum_ref[...] = acc

    vmem = pl.BlockSpec(memory_space=pltpu.VMEM)
    return pl.pallas_call(
        body, name="small_allreduce",
        in_specs=[vmem], out_specs=[vmem, vmem],
        out_shape=[jax.ShapeDtypeStruct((8 * m, n), F32), jax.ShapeDtypeStruct((m, n), F32)],
        scratch_shapes=[pltpu.SemaphoreType.DMA((7,)), pltpu.SemaphoreType.DMA((7,)), pltpu.SemaphoreType.DMA],
    )(block)[1]


def _row_tile(rows):
    best = rows
    for cand in range(8, min(rows, 512) + 1, 8):
        if rows % cand == 0:
            best = cand
    return best


def _adamw_math(w, g, m, v):
    m2 = ADAM_B1 * m + (1.0 - ADAM_B1) * g
    v2 = ADAM_B2 * v + (1.0 - ADAM_B2) * (g * g)
    m_hat = m2 / (1.0 - ADAM_B1 ** ADAM_STEP)
    v_hat = v2 / (1.0 - ADAM_B2 ** ADAM_STEP)
    return -ADAM_LR * (m_hat / (jnp.sqrt(v_hat) + ADAM_EPS) + ADAM_WD * w), m2, v2


def _adamw_halves(w, g_mine, g_other, m, v, c_arr, name):
    rows, cols = w.shape
    r = rows // 2
    tr = _row_tile(r)
    nt = r // tr

    def body(c_ref, w_ref, gm_ref, go_ref, m_ref, v_ref, g_ref, d_ref, nm_ref, nv_ref):
        gv = jnp.where(pl.program_id(0) == c_ref[0], gm_ref[...], go_ref[...])
        g_ref[...] = gv
        d_ref[...], nm_ref[...], nv_ref[...] = _adamw_math(w_ref[...], gv, m_ref[...], v_ref[...])

    full = pl.BlockSpec((tr, cols), lambda h, i, c_ref: (h * nt + i, 0))
    half = pl.BlockSpec((tr, cols), lambda h, i, c_ref: (i, 0))
    shape = jax.ShapeDtypeStruct((rows, cols), F32)
    return pl.pallas_call(
        body, name=name,
        grid_spec=pltpu.PrefetchScalarGridSpec(
            num_scalar_prefetch=1, grid=(2, nt),
            in_specs=[full, half, half, full, full], out_specs=[full] * 4),
        out_shape=[shape] * 4,
        compiler_params=_ARB2,
    )(c_arr, w, g_mine, g_other, m, v)


def _adamw(w, g, m, v, name):
    rows, cols = w.shape
    tr = _row_tile(rows)

    def body(w_ref, g_ref, m_ref, v_ref, d_ref, nm_ref, nv_ref):
        d_ref[...], nm_ref[...], nv_ref[...] = _adamw_math(w_ref[...], g_ref[...], m_ref[...], v_ref[...])

    spec = pl.BlockSpec((tr, cols), lambda i: (i, 0))
    shape = jax.ShapeDtypeStruct((rows, cols), F32)
    return pl.pallas_call(
        body, name=name, grid=(rows // tr,),
        in_specs=[spec] * 4, out_specs=[spec] * 3, out_shape=[shape] * 3,
        compiler_params=_ARB1,
    )(w, g, m, v)


def _in_columns():
    pieces = []
    for group in range(2):
        q0, k0, v0, g0 = (0, 256, 512, 1024) if group == 0 else (1536, 1792, 2048, 2560)
        for h in range(HEADS):
            pieces += [(q0 + 64 * h, 64), (k0 + 64 * h, 64), (v0 + 128 * h, 128), (g0 + 128 * h, 128)]
    pieces.append((3072, GATE_RANK))
    return pieces


def _pad_w_in_t(w_in_t):
    parts = []
    for start, width in _in_columns():
        parts.append(w_in_t[start:start + width])
        if width < LANES:
            parts.append(jnp.zeros((LANES - width, w_in_t.shape[1]), w_in_t.dtype))
    return jnp.concatenate(parts, axis=0)


def _unpad_w_in_t(w_pt):
    rows = {}
    offset = 0
    for start, width in _in_columns():
        rows[start] = w_pt[offset:offset + width]
        offset += LANES
    return jnp.concatenate([rows[s] for s in sorted(rows)], axis=0)


def _rope_tables(t):
    half = 32
    inv = ROPE_BASE ** (-jnp.arange(half, dtype=F32) * 2.0 / 64)
    ang = jnp.arange(t, dtype=F32)[:, None] * inv[None, :]
    cos, sin = jnp.cos(ang), jnp.sin(ang)
    z32, z64 = jnp.zeros((t, 32), F32), jnp.zeros((t, 64), F32)
    return (jnp.concatenate([cos, cos, z64], axis=1),
            jnp.concatenate([-sin, z32, z64], axis=1),
            jnp.concatenate([z32, sin, z64], axis=1))


def _halves(w):
    n, rows, cols = w.shape
    return w.reshape(n, 2, rows // 2, cols)


def _pack_small(n1, nm, n2, nf, nret, ngla, ba, wa2, wa2_cols, extra=None):
    z = lambda k: jnp.zeros((1, k), F32)
    rows = [n1.reshape(1, -1), nm.reshape(1, -1), n2.reshape(1, -1), nf.reshape(1, -1),
            jnp.concatenate([nret.reshape(1, -1), ngla.reshape(1, -1)], axis=1),
            jnp.concatenate([ba.reshape(1, -1), z(D_MODEL - 256)], axis=1),
            jnp.zeros((1, D_MODEL), F32) if extra is None else extra,
            jnp.zeros((1, D_MODEL), F32),
            jnp.concatenate([wa2.reshape(GATE_RANK, wa2_cols), jnp.zeros((GATE_RANK, D_MODEL - wa2_cols), F32)], axis=1),
            jnp.zeros((SMALL_ROWS - 8 - GATE_RANK, D_MODEL), F32)]
    return jnp.concatenate(rows, axis=0)


def _unpack_small(p, wa2_cols):
    return (p[0:1], p[1:2], p[2:3], p[3], p[4:5, 0:512], p[4:5, 512:1024], p[5:6, 0:256],
            p[8:8 + GATE_RANK, 0:wa2_cols].reshape(1, GATE_RANK, wa2_cols))


def _pad_in_rows(w_t):
    return jnp.pad(w_t, ((0, IN_ROWS - IN_SHARD), (0, 0)))


def _forward_backward(xs, target, ffn1_w, rest, ba_p, ffn1_norm_g, mix_norm_g, ret_norm_g, gla_norm_g, ffn2_norm_g,
                      final_norm_g, rest_plan=None, rest_weights=None, early=None):
    t = xs.shape[0]
    cos_t, sa_t, sb_t = _rope_tables(t)
    log_gamma = jnp.log(1.0 - 2.0 ** (-5.0 - jnp.arange(HEADS, dtype=F32)))
    lg_t = jnp.broadcast_to(log_gamma[:, None, None], (HEADS, 1, LANES))
    ret_aux = [cos_t, sa_t, sb_t, lg_t]

    (x1, a1, u1, h1), gathered = _ffn_fwd(xs, ffn1_norm_g, ffn1_w, "ffn1_fwd", hosted=rest_plan)
    ffn2_w, w_in_pt, w_out_full, wa2_p = rest if rest_plan is None else rest_weights(gathered)
    proj, h_mix = _mixer_in_fwd(x1, mix_norm_g, w_in_pt, "mixer_in_fwd")
    gla_aux = [proj, wa2_p, ba_p]
    o_ret, raw_ret, st_ret = _attn_fwd(True, proj, ret_aux, ret_norm_g, "ret_fwd")
    o_gla, raw_gla, st_gla = _attn_fwd(False, proj, gla_aux, gla_norm_g, "gla_fwd")
    x2 = _mixer_out_fwd(o_ret, o_gla, w_out_full, x1, "mixer_out_fwd")
    (x3, a2, u2, h2), _ = _ffn_fwd(x2, ffn2_norm_g, ffn2_w, "ffn2_fwd")
    loss_blk, dx3, d_final_g = _final_loss(x3, final_norm_g, target, "final_loss")

    (da2, du2, hid2, dob2, dx2, d_ffn2_g), _ = _ffn_bwd(dx3, x2, ffn2_norm_g, a2, u2, ffn2_w, "ffn2_bwd")
    g_gate2 = _matmul_tn(da2, h2, "ffn2_dgate", out_dtype=BF16)
    g_up2 = _matmul_tn(du2, h2, "ffn2_dup", out_dtype=BF16)
    g_down2 = _matmul_tn(hid2, dob2, "ffn2_ddown", out_dtype=BF16)

    d_o = _matmul_nt(dx2, w_out_full, "mixer_out_bwd")
    g_wout_ret = _matmul_tn(o_ret, dx2, "wout_grad_ret", out_dtype=BF16)
    g_wout_gla = _matmul_tn(o_gla, dx2, "wout_grad_gla", out_dtype=BF16)
    dproj_ret, d_ret_g = _attn_bwd(True, proj, ret_aux, ret_norm_g, raw_ret, st_ret, d_o, "ret_bwd")
    dproj_gla, d_gla_g, dlogit, d_ba_p = _attn_bwd(False, proj, gla_aux, gla_norm_g, raw_gla, st_gla, d_o, "gla_bwd")
    d_glow = _matmul_nt(dlogit, wa2_p, "gate_low_bwd", out_dtype=BF16)
    g_wa2_p = _matmul_tn(proj[:, PROJ_P - LANES:], dlogit, "gate_w_grad")
    dproj = jnp.concatenate([dproj_ret, dproj_gla, d_glow], axis=1)
    g_win_p = _matmul_tn(dproj, h_mix, "w_in_grad", tka=PROJ_P // 3, out_dtype=BF16)
    dx1, d_mix_g = _mixer_in_bwd(dproj, w_in_pt, dx2, x1, mix_norm_g, "mixer_in_bwd")
    g_win_t = _unpad_w_in_t(g_win_p[0])
    g_win = jnp.stack([_pad_in_rows(g_win_t[IN_SHARD * p:IN_SHARD * (p + 1)]) for p in range(N_CHIPS)], axis=0)
    g_wout = jnp.concatenate([g_wout_ret[0], g_wout_gla[0]], axis=0).reshape(N_CHIPS, D_MODEL // N_CHIPS, D_MODEL)

    early_plan = None if early is None else early([g_gate2, g_up2, g_down2, g_win, g_wout])
    (da1, du1, hid1, dob1, grad_x, d_ffn1_g), arrived = _ffn_bwd(dx1, xs, ffn1_norm_g, a1, u1, ffn1_w, "ffn1_bwd",
                                                                hosted=early_plan)
    g_gate1 = _matmul_tn(da1, h1, "ffn1_dgate", out_dtype=BF16)
    g_up1 = _matmul_tn(du1, h1, "ffn1_dup", out_dtype=BF16)
    g_down1 = _matmul_tn(hid1, dob1, "ffn1_ddown", out_dtype=BF16)

    return (loss_blk, grad_x, g_gate1, g_up1, g_down1, g_gate2, g_up2, g_down2, g_win, g_wout, g_wa2_p,
            d_ba_p, d_ffn1_g, d_mix_g, d_ffn2_g, d_final_g, d_ret_g, d_gla_g, arrived)


def kernel(x, ffn1_norm_g, ffn1_w_gate, ffn1_w_up, ffn1_w_down, mix_norm_g, w_in, ret_norm_g, gla_w_a2, gla_b_a, gla_norm_g, w_out, ffn2_norm_g, ffn2_w_gate, ffn2_w_up, ffn2_w_down, final_norm_g, loss_target, m_ffn1_norm_g, m_ffn1_w_gate, m_ffn1_w_up, m_ffn1_w_down, m_mix_norm_g, m_w_in, m_ret_norm_g, m_gla_w_a2, m_gla_b_a, m_gla_norm_g, m_w_out, m_ffn2_norm_g, m_ffn2_w_gate, m_ffn2_w_up, m_ffn2_w_down, m_final_norm_g, v_ffn1_norm_g, v_ffn1_w_gate, v_ffn1_w_up, v_ffn1_w_down, v_mix_norm_g, v_w_in, v_ret_norm_g, v_gla_w_a2, v_gla_b_a, v_gla_norm_g, v_w_out, v_ffn2_norm_g, v_ffn2_w_gate, v_ffn2_w_up, v_ffn2_w_down, v_final_norm_g):
    t = x.shape[1]
    xs = x.reshape(t, D_MODEL)
    target = loss_target.reshape(t, D_MODEL)
    chip = 2 * lax.axis_index("x") + lax.axis_index("y")
    c_arr = lax.axis_index("c").astype(jnp.int32).reshape(1)

    me_arr = chip.astype(jnp.int32).reshape(1)

    pad_rows = _pad_in_rows

    def own_block(gathered, shard):
        return lax.dynamic_update_slice(gathered, shard[None], (chip,) + (0,) * shard.ndim)

    ffn1_shard = _halves(jnp.stack([ffn1_w_gate[0].T, ffn1_w_up[0].T, ffn1_w_down[0]], axis=0).astype(BF16))
    rest_shards = [_halves(jnp.stack([ffn2_w_gate[0].T, ffn2_w_up[0].T, ffn2_w_down[0]], axis=0).astype(BF16)),
                   _halves(pad_rows(w_in[0].T).astype(BF16)[None]),
                   _halves(w_out.astype(BF16)),
                   jnp.concatenate([gla_w_a2.reshape(GATE_RANK, 64), jnp.zeros((GATE_RANK, 64), F32)],
                                   axis=1).reshape(1, 2, 8, LANES)]
    ffn1_all = _run_hosted(_gather_plan([ffn1_shard]), "gather_ffn1")[0]
    ffn1_w = own_block(ffn1_all, ffn1_shard).reshape(N_CHIPS, 3, FF_SHARD, D_MODEL)

    def rest_weights(gathered):
        ffn2_all, win_all, wout_all, wa2_all = [own_block(g, s) for g, s in zip(gathered, rest_shards)]
        win_t = win_all.reshape(N_CHIPS, IN_ROWS, D_MODEL)
        w_in_pt = _pad_w_in_t(jnp.concatenate([win_t[p, 0:IN_SHARD] for p in range(N_CHIPS)], axis=0))
        wa2_p = jnp.pad(
            wa2_all.reshape(N_CHIPS, GATE_RANK, LANES).transpose(1, 0, 2).reshape(GATE_RANK, HEADS * LANES),
            ((0, LANES - GATE_RANK), (0, 0))).astype(BF16)
        return (ffn2_all.reshape(N_CHIPS, 3, FF_SHARD, D_MODEL), w_in_pt, wout_all.reshape(D_MODEL, D_MODEL), wa2_p)

    def pair_sums(grads, tag):
        halves = [g.reshape(g.shape[0], 2, g.shape[1] // 2, g.shape[2]) for g in grads]
        recv = _pair_exchange(halves, "pair_exchange_" + tag)
        return [_pair_add(g, r, c_arr, "pair_add_%s%d" % (tag, k)) for k, (g, r) in enumerate(zip(halves, recv))]

    early_sums = []

    def early(grads):
        early_sums.extend(pair_sums(grads, "early"))
        return _chip_exchange_plan(early_sums)

    ba_p = jnp.pad(gla_b_a.reshape(HEADS, 64), ((0, 0), (0, 64))).reshape(1, HEADS * LANES)
    fb = _forward_backward(xs, target, ffn1_w, None, ba_p, ffn1_norm_g, mix_norm_g, ret_norm_g, gla_norm_g,
                           ffn2_norm_g, final_norm_g.reshape(1, D_MODEL), rest_plan=_gather_plan(rest_shards),
                           rest_weights=rest_weights, early=early)
    (loss_blk, grad_x, g_gate1, g_up1, g_down1, _, _, _, _, _, g_wa2_p,
     d_ba_p, d_ffn1_g, d_mix_g, d_ffn2_g, d_final_g, d_ret_g, d_gla_g, early_arrived) = fb
    late_sums = pair_sums([g_gate1, g_up1, g_down1], "late")
    late_arrived = _run_hosted(_chip_exchange_plan(late_sums), "chip_exchange_late")
    sums, arrived = late_sums + early_sums, late_arrived + early_arrived
    mine = [_chip_sum(s, r, me_arr, "chip_sum_%d" % k) for k, (s, r) in enumerate(zip(sums, arrived))]
    other = _pair_share(mine)

    g_wa2 = g_wa2_p[0][0:GATE_RANK].reshape(GATE_RANK, HEADS, LANES)[:, :, 0:64].reshape(GATE_RANK, 256)
    d_ba = d_ba_p.reshape(HEADS, LANES)[:, 0:64].reshape(1, 256)
    loss_row = jnp.pad(loss_blk[0:1, 0:1], ((0, 0), (0, D_MODEL - 1)))
    small_local = _pack_small(d_ffn1_g, d_mix_g, d_ffn2_g, d_final_g, d_ret_g, d_gla_g, d_ba, g_wa2, 256, loss_row)
    small_sum = _small_allreduce(small_local)
    loss = small_sum[6, 0]
    sg = _unpack_small(small_sum, 256)
    wa2_grad = lax.dynamic_slice(sg[7], (0, 0, 64 * chip), (1, GATE_RANK, 64))
    small_g = _pack_small(*sg[:7], wa2_grad, 64)
    small_w = _pack_small(ffn1_norm_g, mix_norm_g, ffn2_norm_g, final_norm_g, ret_norm_g, gla_norm_g, gla_b_a, gla_w_a2, 64)
    small_m = _pack_small(m_ffn1_norm_g, m_mix_norm_g, m_ffn2_norm_g, m_final_norm_g, m_ret_norm_g, m_gla_norm_g,
                          m_gla_b_a, m_gla_w_a2, 64)
    small_v = _pack_small(v_ffn1_norm_g, v_mix_norm_g, v_ffn2_norm_g, v_final_norm_g, v_ret_norm_g, v_gla_norm_g,
                          v_gla_b_a, v_gla_w_a2, 64)
    small_out = _adamw(small_w, small_g, small_m, small_v, "adamw_small")
    s_grad = _unpack_small(small_g, 64)
    s_delta, s_m, s_v = (_unpack_small(o, 64) for o in small_out)

    def big(k, w, m, v, name, to_2d, from_2d):
        outs4 = _adamw_halves(to_2d(w), mine[k], other[k], to_2d(m), to_2d(v), c_arr, name)
        return [from_2d(z) for z in outs4]

    plain = (lambda w: w[0], lambda z: z[None])
    transposed = (lambda w: w[0].T, lambda z: z.T[None])
    in_proj = (lambda w: pad_rows(w[0].T), lambda z: z[0:IN_SHARD].T[None])
    r_g1 = big(0, ffn1_w_gate, m_ffn1_w_gate, v_ffn1_w_gate, "adamw_ffn1_gate", *transposed)
    r_u1 = big(1, ffn1_w_up, m_ffn1_w_up, v_ffn1_w_up, "adamw_ffn1_up", *transposed)
    r_d1 = big(2, ffn1_w_down, m_ffn1_w_down, v_ffn1_w_down, "adamw_ffn1_down", *plain)
    r_g2 = big(3, ffn2_w_gate, m_ffn2_w_gate, v_ffn2_w_gate, "adamw_ffn2_gate", *transposed)
    r_u2 = big(4, ffn2_w_up, m_ffn2_w_up, v_ffn2_w_up, "adamw_ffn2_up", *transposed)
    r_d2 = big(5, ffn2_w_down, m_ffn2_w_down, v_ffn2_w_down, "adamw_ffn2_down", *plain)
    r_in = big(6, w_in, m_w_in, v_w_in, "adamw_w_in", *in_proj)
    r_out = big(7, w_out, m_w_out, v_w_out, "adamw_w_out", *plain)

    def leaves(k, smalls):
        n1, nm, n2, nf, nret, ngla, ba, wa2 = smalls
        return [n1, r_g1[k], r_u1[k], r_d1[k], nm, r_in[k], nret, wa2, ba, ngla, r_out[k], n2, r_g2[k], r_u2[k], r_d2[k], nf]

    outs = [loss, grad_x.reshape(x.shape)]
    outs += leaves(0, s_grad) + leaves(1, s_delta) + leaves(2, s_m) + leaves(3, s_v)
    return tuple(outs)
```

```python
import functools

import jax
import jax.numpy as jnp
from jax import lax
from jax.experimental import pallas as pl
from jax.experimental.pallas import tpu as pltpu

F32, BF16 = jnp.float32, jnp.bfloat16
MESH = pl.DeviceIdType.MESH
ANY = pl.BlockSpec(memory_space=pl.ANY)

D_MODEL = 1024
D_FF = 2816
N_CHIPS = 4
FF_SHARD = D_FF // N_CHIPS
IN_WIDTH = 3088
IN_SHARD = IN_WIDTH // N_CHIPS
IN_ROWS = 800
CHUNK = 64
HEADS = 4
LANES = 128
HEAD_BLOCK = 4 * LANES
PROJ_P = 2 * HEADS * HEAD_BLOCK + LANES
GATE_RANK = 16
QK_SCALE = 0.125
GATE_NORM = 16.0
RMS_EPS = 1e-6
ROPE_BASE = 10000.0
ADAM_LR, ADAM_B1, ADAM_B2, ADAM_EPS, ADAM_WD, ADAM_STEP = 0.001, 0.9, 0.999, 1e-08, 0.01, 10
SMALL_ROWS = 32
TOKEN_TILE = 512
ATTN_TILE = 512

_ARB2 = pltpu.CompilerParams(dimension_semantics=("arbitrary", "arbitrary"))
_ARB1 = pltpu.CompilerParams(dimension_semantics=("arbitrary",))
_ARB3 = pltpu.CompilerParams(dimension_semantics=("arbitrary", "arbitrary", "arbitrary"))


def _dot(a, b):
    return jnp.dot(a, b, preferred_element_type=F32)


def _dot_nt(a, b):
    return lax.dot_general(a, b, (((1,), (1,)), ((), ())), preferred_element_type=F32)


def _dot_tn(a, b):
    return lax.dot_general(a, b, (((0,), (0,)), ((), ())), preferred_element_type=F32)


def _rms_scale(xv):
    return lax.rsqrt(jnp.mean(xv * xv, axis=-1, keepdims=True) + RMS_EPS)


def _rms_bwd(dh, xv, g):
    r = _rms_scale(xv)
    xhat = xv * r
    dxhat = dh * g
    dx = r * (dxhat - xhat * jnp.mean(dxhat * xhat, axis=-1, keepdims=True))
    return dx, jnp.sum(dh * xhat, axis=0, keepdims=True)


def _silu_grad(a, sg):
    return sg * (1.0 + a * (1.0 - sg))


class _Hosted:
    def __init__(self, arrays, out_shapes, n_sems, start, finish):
        self.arrays, self.out_shapes, self.n_sems = list(arrays), list(out_shapes), n_sems
        self.start, self.finish = start, finish


def _call(body, args, *, name, grid, in_specs, out_specs, out_shape, scratch_shapes, compiler_params, hosted=None):
    if hosted is None:
        outs = pl.pallas_call(body, name=name, grid=grid, in_specs=in_specs, out_specs=out_specs, out_shape=out_shape,
                              scratch_shapes=scratch_shapes, compiler_params=compiler_params)(*args)
        return list(outs), []
    n_in, n_out, n_sc, nh = len(in_specs), len(out_specs), len(scratch_shapes), len(hosted.arrays)

    def wrapped(*refs):
        ins, h_in = refs[:n_in], refs[n_in:n_in + nh]
        outs, h_out = refs[n_in + nh:n_in + nh + n_out], refs[n_in + nh + n_out:n_in + 2 * nh + n_out]
        rest = refs[n_in + 2 * nh + n_out:]
        scratch, (send_sems, recv_sems) = rest[:n_sc], rest[n_sc:]
        first = functools.reduce(jnp.logical_and, [pl.program_id(d) == 0 for d in range(len(grid))])
        last = functools.reduce(jnp.logical_and, [pl.program_id(d) == n - 1 for d, n in enumerate(grid)])

        @pl.when(first)
        def _():
            hosted.start(h_in, h_out, send_sems, recv_sems)

        body(*ins, *outs, *scratch)

        @pl.when(last)
        def _():
            hosted.finish(h_in, h_out, send_sems, recv_sems)

    sems = [pltpu.SemaphoreType.DMA((hosted.n_sems,)), pltpu.SemaphoreType.DMA((hosted.n_sems,))]
    outs = pl.pallas_call(
        wrapped, name=name, grid=grid, in_specs=list(in_specs) + [ANY] * nh, out_specs=list(out_specs) + [ANY] * nh,
        out_shape=list(out_shape) + hosted.out_shapes, scratch_shapes=list(scratch_shapes) + sems,
        compiler_params=compiler_params)(*args, *hosted.arrays)
    return list(outs[:n_out]), list(outs[n_out:])


def _run_hosted(hosted, name):
    nh = len(hosted.arrays)

    def body(*refs):
        h_in, h_out, (send_sems, recv_sems) = refs[:nh], refs[nh:2 * nh], refs[2 * nh:]
        hosted.start(h_in, h_out, send_sems, recv_sems)
        hosted.finish(h_in, h_out, send_sems, recv_sems)

    sems = [pltpu.SemaphoreType.DMA((hosted.n_sems,)), pltpu.SemaphoreType.DMA((hosted.n_sems,))]
    return list(pl.pallas_call(body, name=name, in_specs=[ANY] * nh, out_specs=[ANY] * nh,
                               out_shape=hosted.out_shapes, scratch_shapes=sems)(*hosted.arrays))


def _pipeline_items(steps):
    def cur(s):
        c = jnp.minimum(s, steps - 1)
        return c // N_CHIPS, c % N_CHIPS

    def prev(s):
        p = jnp.maximum(s - 1, 0)
        return p // N_CHIPS, p % N_CHIPS

    return cur, prev


def _ffn_fwd(x, g, ffn_w, name, hosted=None):
    t = x.shape[0]
    tm = min(t, TOKEN_TILE)

    def body(x_ref, g_ref, wg_ref, wu_ref, wd_ref, xo_ref, a_ref, u_ref, h_ref, acc_ref):
        j = pl.program_id(1)

        @pl.when(j == 0)
        def _():
            xv = x_ref[...]
            h_ref[...] = ((xv * _rms_scale(xv)) * g_ref[...]).astype(BF16)
            acc_ref[...] = jnp.zeros_like(acc_ref)

        h = h_ref[...]
        a = _dot_nt(h, wg_ref[...])
        u = _dot_nt(h, wu_ref[...])
        a_ref[...] = a.astype(BF16)
        u_ref[...] = u.astype(BF16)
        hid = (a * jax.nn.sigmoid(a)) * u
        acc_ref[...] += _dot(hid.astype(BF16), wd_ref[...])

        @pl.when(j == N_CHIPS - 1)
        def _():
            xo_ref[...] = x_ref[...] + 0.5 * acc_ref[...]

    tok = pl.BlockSpec((tm, D_MODEL), lambda i, j: (i, 0))
    act = pl.BlockSpec((None, tm, FF_SHARD), lambda i, j: (j, i, 0))
    wblk = (None, None, FF_SHARD, D_MODEL)
    weights = [pl.BlockSpec(wblk, lambda i, j, k=kind: (j, k, 0, 0)) for kind in range(3)]
    return _call(
        body, (x, g, ffn_w, ffn_w, ffn_w), name=name, grid=(t // tm, N_CHIPS),
        in_specs=[tok, pl.BlockSpec((1, D_MODEL), lambda i, j: (0, 0))] + weights,
        out_specs=[tok, act, act, tok],
        out_shape=[jax.ShapeDtypeStruct((t, D_MODEL), F32),
                   jax.ShapeDtypeStruct((N_CHIPS, t, FF_SHARD), BF16),
                   jax.ShapeDtypeStruct((N_CHIPS, t, FF_SHARD), BF16),
                   jax.ShapeDtypeStruct((t, D_MODEL), BF16)],
        scratch_shapes=[pltpu.VMEM((tm, D_MODEL), F32)],
        compiler_params=_ARB2, hosted=hosted)


def _ffn_bwd(dxo, x, g, a4, u4, ffn_w, name, hosted=None):
    t = x.shape[0]
    tm = min(t, TOKEN_TILE)
    steps = (t // tm) * N_CHIPS
    cur, prev = _pipeline_items(steps)


    def body(dxo_ref, dxo_prev_ref, x_ref, g_ref, a_ref, u_ref, wg_ref, wu_ref, wd_ref,
             da_ref, du_ref, hid_ref, dob_ref, dx_ref, dg_ref, acc_ref, da_slots, du_slots):
        s = pl.program_id(0)
        jc, jp = cur(s)[1], prev(s)[1]
        slot = s % 2

        @pl.when(s == 0)
        def _():
            dg_ref[...] = jnp.zeros_like(dg_ref)
            acc_ref[...] = jnp.zeros_like(acc_ref)
            da_slots[...] = jnp.zeros_like(da_slots)
            du_slots[...] = jnp.zeros_like(du_slots)

        @pl.when(jc == 0)
        def _():
            dob_ref[...] = (0.5 * dxo_ref[...]).astype(BF16)

        dhid = _dot_nt(dob_ref[...], wd_ref[...])
        a = a_ref[...].astype(F32)
        u = u_ref[...].astype(F32)
        sg = jax.nn.sigmoid(a)
        sl = a * sg
        hid_ref[...] = (sl * u).astype(BF16)
        du = (dhid * sl).astype(BF16)
        da = (dhid * u * _silu_grad(a, sg)).astype(BF16)
        du_ref[...] = du
        da_ref[...] = da
        acc_ref[...] += _dot(da_slots[1 - slot], wg_ref[...]) + _dot(du_slots[1 - slot], wu_ref[...])
        da_slots[slot] = da
        du_slots[slot] = du

        @pl.when((jp == N_CHIPS - 1) & (s > 0))
        def _():
            dx, dg = _rms_bwd(acc_ref[...], x_ref[...], g_ref[...])
            dx_ref[...] = dxo_prev_ref[...] + dx
            dg_ref[...] += dg
            acc_ref[...] = jnp.zeros_like(acc_ref)

    tok_cur = pl.BlockSpec((tm, D_MODEL), lambda s: (cur(s)[0], 0))
    tok_prev = pl.BlockSpec((tm, D_MODEL), lambda s: (prev(s)[0], 0))
    act = pl.BlockSpec((None, tm, FF_SHARD), lambda s: (cur(s)[1], cur(s)[0], 0))
    row = pl.BlockSpec((1, D_MODEL), lambda s: (0, 0))
    wblk = (None, None, FF_SHARD, D_MODEL)
    weights = [pl.BlockSpec(wblk, lambda s: (prev(s)[1], 0, 0, 0)), pl.BlockSpec(wblk, lambda s: (prev(s)[1], 1, 0, 0)),
               pl.BlockSpec(wblk, lambda s: (cur(s)[1], 2, 0, 0))]
    act_shape = jax.ShapeDtypeStruct((N_CHIPS, t, FF_SHARD), BF16)
    return _call(
        body, (dxo, dxo, x, g, a4, u4, ffn_w, ffn_w, ffn_w), name=name, grid=(steps + 1,),
        in_specs=[tok_cur, tok_prev, tok_prev, row, act, act] + weights,
        out_specs=[act, act, act, tok_cur, tok_prev, row],
        out_shape=[act_shape, act_shape, act_shape,
                   jax.ShapeDtypeStruct((t, D_MODEL), BF16),
                   jax.ShapeDtypeStruct((t, D_MODEL), F32),
                   jax.ShapeDtypeStruct((1, D_MODEL), F32)],
        scratch_shapes=[pltpu.VMEM((tm, D_MODEL), F32), pltpu.VMEM((2, tm, FF_SHARD), BF16),
                        pltpu.VMEM((2, tm, FF_SHARD), BF16)],
        compiler_params=_ARB1, hosted=hosted)


def _matmul_tn(a, b, name, tka=None, out_dtype=F32):
    a3, b3 = a.ndim == 3, b.ndim == 3
    nb = a.shape[0] if a3 else (b.shape[0] if b3 else 1)
    t, ka, n = a.shape[-2], a.shape[-1], b.shape[-1]
    tka = ka if tka is None else tka
    tk = min(t, 2 * TOKEN_TILE)
    nk = t // tk

    def body(a_ref, b_ref, o_ref, acc_ref):
        k = pl.program_id(2)

        @pl.when(k == 0)
        def _():
            acc_ref[...] = jnp.zeros_like(acc_ref)

        acc_ref[...] += _dot_tn(a_ref[...].astype(BF16), b_ref[...].astype(BF16))

        @pl.when(k == nk - 1)
        def _():
            o_ref[...] = acc_ref[...].astype(out_dtype)

    a_spec = (pl.BlockSpec((None, tk, tka), lambda i, j, k: (i, k, j)) if a3
              else pl.BlockSpec((tk, tka), lambda i, j, k: (k, j)))
    b_spec = (pl.BlockSpec((None, tk, n), lambda i, j, k: (i, k, 0)) if b3
              else pl.BlockSpec((tk, n), lambda i, j, k: (k, 0)))
    return pl.pallas_call(
        body, name=name, grid=(nb, ka // tka, t // tk),
        in_specs=[a_spec, b_spec],
        out_specs=pl.BlockSpec((None, tka, n), lambda i, j, k: (i, j, 0)),
        out_shape=jax.ShapeDtypeStruct((nb, ka, n), out_dtype),
        scratch_shapes=[pltpu.VMEM((tka, n), F32)],
        compiler_params=_ARB3,
    )(a, b)


def _matmul_nt(a, w, name, out_dtype=F32):
    t, k = a.shape
    n = w.shape[0]
    tm = min(t, TOKEN_TILE)

    def body(a_ref, w_ref, o_ref):
        o_ref[...] = _dot_nt(a_ref[...].astype(BF16), w_ref[...]).astype(out_dtype)

    return pl.pallas_call(
        body, name=name, grid=(t // tm,),
        in_specs=[pl.BlockSpec((tm, k), lambda i: (i, 0)), pl.BlockSpec((n, k), lambda i: (0, 0))],
        out_specs=pl.BlockSpec((tm, n), lambda i: (i, 0)),
        out_shape=jax.ShapeDtypeStruct((t, n), out_dtype),
        compiler_params=_ARB1,
    )(a, w)


def _mixer_in_bwd(dproj, w_in_pt, dres, x, g, name):
    t, k = dproj.shape
    tm = min(t, TOKEN_TILE)

    def body(a_ref, w_ref, dres_ref, x_ref, g_ref, dx_ref, dg_ref):
        @pl.when(pl.program_id(0) == 0)
        def _():
            dg_ref[...] = jnp.zeros_like(dg_ref)

        dh = _dot(a_ref[...], w_ref[...])
        dx, dg = _rms_bwd(dh, x_ref[...], g_ref[...])
        dx_ref[...] = dres_ref[...] + dx
        dg_ref[...] += dg

    tok = pl.BlockSpec((tm, D_MODEL), lambda i: (i, 0))
    row = pl.BlockSpec((1, D_MODEL), lambda i: (0, 0))
    return pl.pallas_call(
        body, name=name, grid=(t // tm,),
        in_specs=[pl.BlockSpec((tm, k), lambda i: (i, 0)), pl.BlockSpec((k, D_MODEL), lambda i: (0, 0)), tok, tok, row],
        out_specs=[tok, row],
        out_shape=[jax.ShapeDtypeStruct((t, D_MODEL), F32), jax.ShapeDtypeStruct((1, D_MODEL), F32)],
        compiler_params=_ARB1,
    )(dproj, w_in_pt, dres, x, g)


def _mixer_in_fwd(x, g, w_in_pt, name):
    t = x.shape[0]
    tm = min(t, TOKEN_TILE)
    tn = PROJ_P // 3

    def body(x_ref, g_ref, w_ref, p_ref, h_ref):
        @pl.when(pl.program_id(1) == 0)
        def _():
            xv = x_ref[...]
            h_ref[...] = ((xv * _rms_scale(xv)) * g_ref[...]).astype(BF16)

        p_ref[...] = _dot_nt(h_ref[...], w_ref[...])

    tok = pl.BlockSpec((tm, D_MODEL), lambda i, j: (i, 0))
    return pl.pallas_call(
        body, name=name, grid=(t // tm, 3),
        in_specs=[tok, pl.BlockSpec((1, D_MODEL), lambda i, j: (0, 0)),
                  pl.BlockSpec((tn, D_MODEL), lambda i, j: (j, 0))],
        out_specs=[pl.BlockSpec((tm, tn), lambda i, j: (i, j)), tok],
        out_shape=[jax.ShapeDtypeStruct((t, PROJ_P), F32), jax.ShapeDtypeStruct((t, D_MODEL), BF16)],
        compiler_params=_ARB2,
    )(x, g, w_in_pt)


def _mixer_out_fwd(o_ret, o_gla, w_out, x, name):
    t = x.shape[0]
    tm = min(t, TOKEN_TILE)
    half = HEADS * LANES

    def body(a_ref, b_ref, w_ref, x_ref, o_ref):
        o_ref[...] = x_ref[...] + _dot(a_ref[...], w_ref[0:half, :]) + _dot(b_ref[...], w_ref[half:2 * half, :])

    tok = pl.BlockSpec((tm, D_MODEL), lambda i: (i, 0))
    hb = pl.BlockSpec((tm, half), lambda i: (i, 0))
    return pl.pallas_call(
        body, name=name, grid=(t // tm,),
        in_specs=[hb, hb, pl.BlockSpec((2 * half, D_MODEL), lambda i: (0, 0)), tok],
        out_specs=tok, out_shape=jax.ShapeDtypeStruct((t, D_MODEL), F32),
        compiler_params=_ARB1,
    )(o_ret, o_gla, w_out, x)


def _final_loss(x, g, target, name):
    t = x.shape[0]
    tm = min(t, TOKEN_TILE)

    def body(x_ref, g_ref, t_ref, l_ref, dx_ref, dg_ref):
        @pl.when(pl.program_id(0) == 0)
        def _():
            l_ref[...] = jnp.zeros_like(l_ref)
            dg_ref[...] = jnp.zeros_like(dg_ref)

        xv = x_ref[...]
        gv = g_ref[...]
        err = (xv * _rms_scale(xv)) * gv - t_ref[...]
        l_ref[...] += 0.5 * jnp.sum(jnp.mean(err * err, axis=-1, keepdims=True), axis=0, keepdims=True)
        dx, dg = _rms_bwd(err * (1.0 / D_MODEL), xv, gv)
        dx_ref[...] = dx
        dg_ref[...] += dg

    tok = pl.BlockSpec((tm, D_MODEL), lambda i: (i, 0))
    row = pl.BlockSpec((1, D_MODEL), lambda i: (0, 0))
    return pl.pallas_call(
        body, name=name, grid=(t // tm,),
        in_specs=[tok, row, tok],
        out_specs=[pl.BlockSpec((8, LANES), lambda i: (0, 0)), tok, row],
        out_shape=[jax.ShapeDtypeStruct((8, LANES), F32), jax.ShapeDtypeStruct((t, D_MODEL), F32),
                   jax.ShapeDtypeStruct((1, D_MODEL), F32)],
        compiler_params=_ARB1,
    )(x, g, target)


def _rot(v, cos, sa, sb):
    return v * cos + pltpu.roll(v, 96, 1) * sa + pltpu.roll(v, 32, 1) * sb


def _rot_t(d, cos, sa, sb):
    return d * cos + pltpu.roll(d * sa, 32, 1) + pltpu.roll(d * sb, 96, 1)


def _bmm(a, b):
    return jnp.einsum("cik,ckj->cij", a, b, preferred_element_type=F32)


def _bmm_nt(a, b):
    return jnp.einsum("cik,cjk->cij", a, b, preferred_element_type=F32)


def _bmm_tn(a, b):
    return jnp.einsum("cki,ckj->cij", a, b, preferred_element_type=F32)


def _masked_sum(mask, x):
    hi = x.astype(BF16)
    r1 = x - hi.astype(F32)
    mid = r1.astype(BF16)
    lo = (r1 - mid.astype(F32)).astype(BF16)
    return _bmm(mask, hi) + _bmm(mask, mid) + _bmm(mask, lo)


def _tile_inputs(is_ret, proj_ref, aux, nc):
    shape3 = (nc, CHUNK, LANES)
    q_raw = proj_ref[:, 0:LANES]
    k_raw = proj_ref[:, LANES:2 * LANES]
    v = proj_ref[:, 2 * LANES:3 * LANES]
    gate = proj_ref[:, 3 * LANES:4 * LANES]
    ri = lax.broadcasted_iota(jnp.int32, (nc, CHUNK, CHUNK), 1)
    ci = lax.broadcasted_iota(jnp.int32, (nc, CHUNK, CHUNK), 2)
    if is_ret:
        cos_ref, sa_ref, sb_ref, lg_ref = aux
        cos, sa, sb = cos_ref[...], sa_ref[...], sb_ref[...]
        q = _rot(q_raw, cos, sa, sb)
        k = _rot(k_raw, cos, sa, sb) * QK_SCALE
        steps = (lax.broadcasted_iota(jnp.int32, shape3, 1) + 1).astype(F32)
        b = steps * lg_ref[...]
        logit = None
    else:
        glow_ref, wa2_ref, ba_ref = aux
        logit = _dot(glow_ref[...].astype(BF16), wa2_ref[...]) + ba_ref[...]
        la = (jnp.minimum(logit, 0.0) - jnp.log1p(jnp.exp(-jnp.abs(logit)))) * (1.0 / GATE_NORM)
        b = _masked_sum((ci <= ri).astype(BF16), la.reshape(shape3))
        q = q_raw * QK_SCALE
        k = k_raw
    return q.reshape(shape3), k.reshape(shape3), v.reshape(shape3), gate, b, logit, ri, ci


def _tile_scores(q, k, b, ri, ci):
    mid = b[:, CHUNK // 2 - 1:CHUNK // 2, :]
    ep = jnp.exp(b - mid)
    en = jnp.exp(mid - b)
    qt, kt, qh, kh = q * ep, k * en, q * en, k * ep
    low = _bmm_nt(qt.astype(BF16), kt.astype(BF16))
    upp = _bmm_nt(qh.astype(BF16), kh.astype(BF16))
    scores = jnp.where(ci <= ri, low, upp)
    return scores, ep, en, qt, kt, qh, kh


def _attn_specs(is_ret, t, tb, imap_t):
    nb = t // tb
    base = 0 if is_ret else HEADS
    proj = pl.BlockSpec((tb, HEAD_BLOCK), lambda h, i: (imap_t(i), base + h))
    lane_t = pl.BlockSpec((tb, LANES), lambda h, i: (imap_t(i), 0))
    if is_ret:
        aux = [lane_t, lane_t, lane_t, pl.BlockSpec((None, 1, LANES), lambda h, i: (h, 0, 0))]
    else:
        aux = [pl.BlockSpec((tb, LANES), lambda h, i: (imap_t(i), PROJ_P // LANES - 1)),
               pl.BlockSpec((LANES, LANES), lambda h, i: (0, h)),
               pl.BlockSpec((1, LANES), lambda h, i: (0, h))]
    gain = pl.BlockSpec((1, LANES), lambda h, i: (0, h))
    head_t = pl.BlockSpec((tb, LANES), lambda h, i: (imap_t(i), h))
    state = pl.BlockSpec((None, tb // CHUNK, LANES, LANES), lambda h, i: (h, imap_t(i), 0, 0))
    return nb, proj, aux, gain, head_t, state


def _attn_fwd(is_ret, proj, aux_arrays, gain, name):
    t = proj.shape[0]
    tb = min(t, ATTN_TILE)
    nc = tb // CHUNK
    n_aux = 4 if is_ret else 3
    nb, proj_spec, aux_specs, gain_spec, head_t, state_spec = _attn_specs(is_ret, t, tb, lambda i: i)

    def body(*refs):
        proj_ref = refs[0]
        aux = refs[1:1 + n_aux]
        gn_ref, ofin_ref, oraw_ref, st_ref, state = refs[1 + n_aux:]

        @pl.when(pl.program_id(1) == 0)
        def _():
            state[...] = jnp.zeros_like(state)

        q, k, v, gate, b, _, ri, ci = _tile_inputs(is_ret, proj_ref, aux, nc)
        scores = _tile_scores(q, k, b, ri, ci)[0]
        vb = v.astype(BF16)
        intra = _bmm(scores.astype(BF16), vb)
        b_last = b[:, CHUNK - 1:CHUNK, :]
        e_last = jnp.exp(b_last)
        grow = _bmm_tn(vb, (k * jnp.exp(b_last - b)).astype(BF16))
        st = state[...]
        for c in range(nc):
            st_ref[c] = st
            st = st * e_last[c] + grow[c]
        state[...] = st
        inter = _bmm_nt((q * jnp.exp(b)).astype(BF16), st_ref[...].astype(BF16))
        out = (intra + inter).reshape(tb, LANES)
        oraw_ref[...] = out
        normed = out * _rms_scale(out)
        ofin_ref[...] = ((normed * gn_ref[...]) * (gate * jax.nn.sigmoid(gate))).astype(BF16)

    width = HEADS * LANES
    return pl.pallas_call(
        body, name=name, grid=(HEADS, nb),
        in_specs=[proj_spec] + aux_specs + [gain_spec],
        out_specs=[head_t, head_t, state_spec],
        out_shape=[jax.ShapeDtypeStruct((t, width), BF16), jax.ShapeDtypeStruct((t, width), F32),
                   jax.ShapeDtypeStruct((HEADS, t // CHUNK, LANES, LANES), F32)],
        scratch_shapes=[pltpu.VMEM((LANES, LANES), F32)],
        compiler_params=_ARB2,
    )(proj, *aux_arrays, gain)


def _attn_bwd(is_ret, proj, aux_arrays, gain, o_raw, states, d_out, name):
    t = proj.shape[0]
    tb = min(t, ATTN_TILE)
    nc = tb // CHUNK
    n_aux = 4 if is_ret else 3
    nblk = t // tb
    nb, proj_spec, aux_specs, gain_spec, head_t, state_spec = _attn_specs(is_ret, t, tb, lambda i: nblk - 1 - i)
    base = 0 if is_ret else HEADS
    dout_spec = pl.BlockSpec((tb, LANES), lambda h, i: (nblk - 1 - i, base + h))

    def body(*refs):
        proj_ref = refs[0]
        aux = refs[1:1 + n_aux]
        gn_ref, oraw_ref, st_ref, dfin_ref = refs[1 + n_aux:5 + n_aux]
        if is_ret:
            dproj_ref, dgn_ref, dstate, dafter_ref = refs[5 + n_aux:]
        else:
            dproj_ref, dgn_ref, dlogit_ref, dba_ref, dstate, dafter_ref = refs[5 + n_aux:]

        @pl.when(pl.program_id(1) == 0)
        def _():
            dstate[...] = jnp.zeros_like(dstate)
            dgn_ref[...] = jnp.zeros_like(dgn_ref)
            if not is_ret:
                dba_ref[...] = jnp.zeros_like(dba_ref)

        shape3 = (nc, CHUNK, LANES)
        q, k, v, gate, b, logit, ri, ci = _tile_inputs(is_ret, proj_ref, aux, nc)
        scores, ep, en, qt, kt, qh, kh = _tile_scores(q, k, b, ri, ci)
        eb = jnp.exp(b)
        qe = q * eb
        b_last = b[:, CHUNK - 1:CHUNK, :]
        e_last = jnp.exp(b_last)
        ekd = jnp.exp(b_last - b)
        kd = k * ekd

        gn = gn_ref[...]
        out = oraw_ref[...]
        r = _rms_scale(out)
        normed = out * r
        sg = jax.nn.sigmoid(gate)
        dfin = dfin_ref[...]
        dgate = dfin * (normed * gn) * _silu_grad(gate, sg)
        dpre = dfin * (gate * sg)
        dgn_ref[...] += jnp.sum(dpre * normed, axis=0, keepdims=True)
        dnormed = dpre * gn
        d_o = r * (dnormed - normed * jnp.mean(dnormed * normed, axis=-1, keepdims=True))
        dob, vb = d_o.reshape(shape3).astype(BF16), v.astype(BF16)

        dgrow = _bmm_tn(dob, qe.astype(BF16))
        dst = dstate[...]
        for c in reversed(range(nc)):
            dafter_ref[c] = dst
            dst = dst * e_last[c] + dgrow[c]
        dstate[...] = dst
        st = st_ref[...]
        dafter = dafter_ref[...]
        stb, dafter_b = st.astype(BF16), dafter.astype(BF16)

        qtb, ktb, qhb, khb = qt.astype(BF16), kt.astype(BF16), qh.astype(BF16), kh.astype(BF16)
        scores_t = jnp.where(ci >= ri, _bmm_nt(ktb, qtb), _bmm_nt(khb, qhb))
        dv = _bmm(scores_t.astype(BF16), dob) + _bmm_nt(kd.astype(BF16), dafter_b)
        dsc = _bmm_nt(dob, vb)
        dsc_t = _bmm_nt(vb, dob)
        dqe = _bmm(dob, stb)
        dkd = _bmm(vb, dafter_b)
        dqt = _bmm(jnp.where(ci <= ri, dsc, 0.0).astype(BF16), ktb)
        dqh = _bmm(jnp.where(ci <= ri, 0.0, dsc).astype(BF16), khb)
        dkt = _bmm(jnp.where(ci >= ri, dsc_t, 0.0).astype(BF16), qtb)
        dkh = _bmm(jnp.where(ci >= ri, 0.0, dsc_t).astype(BF16), qhb)
        dq = (dqt * ep + dqh * en + dqe * eb).reshape(tb, LANES)
        dk = (dkt * en + dkh * ep + dkd * ekd).reshape(tb, LANES)

        if is_ret:
            cos_ref, sa_ref, sb_ref, _ = aux
            cos, sa, sb = cos_ref[...], sa_ref[...], sb_ref[...]
            dq_raw = _rot_t(dq, cos, sa, sb)
            dk_raw = _rot_t(dk, cos, sa, sb) * QK_SCALE
        else:
            dq_raw = dq * QK_SCALE
            dk_raw = dk
            db = dqt * qt - dkt * kt - dqh * qh + dkh * kh + dqe * qe - dkd * kd
            db_last = (jnp.sum(dkd * kd, axis=1, keepdims=True)
                       + jnp.sum(dafter * st, axis=1, keepdims=True) * e_last)
            last_row = lax.broadcasted_iota(jnp.int32, shape3, 1) == CHUNK - 1
            db = db + jnp.where(last_row, db_last, 0.0)
            dla = _masked_sum((ci >= ri).astype(BF16), db).reshape(tb, LANES)
            dlogit = dla * (1.0 / GATE_NORM) * jax.nn.sigmoid(-logit)
            dlogit_ref[...] = dlogit.astype(BF16)
            dba_ref[...] += jnp.sum(dlogit, axis=0, keepdims=True)

        dproj_ref[:, 0:LANES] = dq_raw.astype(BF16)
        dproj_ref[:, LANES:2 * LANES] = dk_raw.astype(BF16)
        dproj_ref[:, 2 * LANES:3 * LANES] = dv.reshape(tb, LANES).astype(BF16)
        dproj_ref[:, 3 * LANES:4 * LANES] = dgate.astype(BF16)

    width = HEADS * LANES
    row_out = pl.BlockSpec((1, LANES), lambda h, i: (0, h))
    out_specs = [pl.BlockSpec((tb, HEAD_BLOCK), lambda h, i: (nblk - 1 - i, h)), row_out]
    out_shape = [jax.ShapeDtypeStruct((t, HEADS * HEAD_BLOCK), BF16), jax.ShapeDtypeStruct((1, width), F32)]
    if not is_ret:
        out_specs += [head_t, row_out]
        out_shape += [jax.ShapeDtypeStruct((t, width), BF16), jax.ShapeDtypeStruct((1, width), F32)]
    return pl.pallas_call(
        body, name=name, grid=(HEADS, nblk),
        in_specs=[proj_spec] + aux_specs + [gain_spec, head_t, state_spec, dout_spec],
        out_specs=out_specs, out_shape=out_shape,
        scratch_shapes=[pltpu.VMEM((LANES, LANES), F32), pltpu.VMEM((nc, LANES, LANES), F32)],
        compiler_params=_ARB2,
    )(proj, *aux_arrays, gain, o_raw, states, d_out)


def _place():
    x, y, c = lax.axis_index("x"), lax.axis_index("y"), lax.axis_index("c")
    chips = [(1 - x, y), (x, 1 - y), (1 - x, 1 - y)]
    return x, y, c, 2 * x + y, chips


def _gather_plan(arrs):
    na = len(arrs)

    def copies(ins, outs, send_sems, recv_sems):
        x, y, c, me, chips = _place()

        def ici(a, j, src_chip, to):
            return pltpu.make_async_remote_copy(
                src_ref=ins[a].at[:, c], dst_ref=outs[a].at[src_chip, :, c],
                send_sem=send_sems.at[6 * a + j], recv_sem=recv_sems.at[6 * a + j], device_id=to, device_id_type=MESH)

        def d2d(a, j, src_chip, half):
            blk = outs[a].at[src_chip, :, half]
            return pltpu.make_async_remote_copy(
                src_ref=blk, dst_ref=blk, send_sem=send_sems.at[6 * a + 3 + j], recv_sem=recv_sems.at[6 * a + 3 + j],
                device_id=(x, y, 1 - c), device_id_type=MESH)

        peers = [(a, j, px, py) for a in range(na) for j, (px, py) in enumerate(chips)]
        return c, me, peers, ici, d2d

    def start(*refs):
        c, me, peers, ici, _ = copies(*refs)
        for a, j, px, py in peers:
            ici(a, j, me, (px, py, c)).start()

    def finish(*refs):
        c, me, peers, ici, d2d = copies(*refs)
        for a, j, px, py in peers:
            ici(a, j, 2 * px + py, (px, py, c)).wait_recv()
            d2d(a, j, 2 * px + py, c).start()
        for a, j, px, py in peers:
            d2d(a, j, 2 * px + py, 1 - c).wait_recv()
        for a, j, px, py in peers:
            ici(a, j, me, (px, py, c)).wait_send()
            d2d(a, j, 2 * px + py, c).wait_send()

    return _Hosted(arrs, [jax.ShapeDtypeStruct((N_CHIPS,) + a.shape, a.dtype) for a in arrs], 6 * na, start, finish)


def _pair_exchange(grads, name):
    na = len(grads)

    def body(*refs):
        ins, outs = refs[:na], refs[na:2 * na]
        send_sems, recv_sems = refs[2 * na:]
        x, y, c, _, _ = _place()
        copies = [pltpu.make_async_remote_copy(
            src_ref=ins[a].at[:, 1 - c], dst_ref=outs[a], send_sem=send_sems.at[a], recv_sem=recv_sems.at[a],
            device_id=(x, y, 1 - c), device_id_type=MESH) for a in range(na)]
        for cp in copies:
            cp.start()
        for cp in copies:
            cp.wait()

    return pl.pallas_call(
        body, name=name,
        in_specs=[ANY] * na, out_specs=[ANY] * na,
        out_shape=[jax.ShapeDtypeStruct(g.shape[:1] + g.shape[2:], g.dtype) for g in grads],
        scratch_shapes=[pltpu.SemaphoreType.DMA((na,)), pltpu.SemaphoreType.DMA((na,))],
    )(*grads)


def _pair_add(grad, recv, c_arr, name):
    _, _, r, cols = grad.shape

    def body(c_ref, g_ref, r_ref, o_ref):
        o_ref[...] = (g_ref[...].astype(F32) + r_ref[...].astype(F32)).astype(BF16)

    return pl.pallas_call(
        body, name=name,
        grid_spec=pltpu.PrefetchScalarGridSpec(
            num_scalar_prefetch=1, grid=(N_CHIPS,),
            in_specs=[pl.BlockSpec((None, None, r, cols), lambda p, c_ref: (p, c_ref[0], 0, 0)),
                      pl.BlockSpec((None, r, cols), lambda p, c_ref: (p, 0, 0))],
            out_specs=pl.BlockSpec((None, r, cols), lambda p, c_ref: (p, 0, 0))),
        out_shape=jax.ShapeDtypeStruct((N_CHIPS, r, cols), BF16),
        compiler_params=_ARB1,
    )(c_arr, grad, recv)


def _chip_exchange_plan(sums):
    na = len(sums)

    def copies(ins, outs, send_sems, recv_sems):
        x, y, c, me, chips = _place()

        def copy(a, j, px, py, block, slot):
            return pltpu.make_async_remote_copy(
                src_ref=ins[a].at[block], dst_ref=outs[a].at[slot],
                send_sem=send_sems.at[3 * a + j], recv_sem=recv_sems.at[3 * a + j],
                device_id=(px, py, c), device_id_type=MESH)

        peers = [(a, j, px, py) for a in range(na) for j, (px, py) in enumerate(chips)]
        return me, peers, copy

    def start(*refs):
        me, peers, copy = copies(*refs)
        for a, j, px, py in peers:
            copy(a, j, px, py, 2 * px + py, me).start()

    def finish(*refs):
        me, peers, copy = copies(*refs)
        for a, j, px, py in peers:
            copy(a, j, px, py, me, 2 * px + py).wait_recv()
        for a, j, px, py in peers:
            copy(a, j, px, py, 2 * px + py, me).wait_send()

    return _Hosted(sums, [jax.ShapeDtypeStruct(s.shape, s.dtype) for s in sums], 3 * na, start, finish)


def _chip_sum(own, recv, me_arr, name):
    _, r, cols = recv.shape

    def body(me_ref, own_ref, r_ref, o_ref):
        o_ref[...] = jnp.zeros_like(o_ref)
        for q in range(N_CHIPS):
            @pl.when(me_ref[0] == q)
            def _():
                o_ref[...] += own_ref[...].astype(F32)

            @pl.when(me_ref[0] != q)
            def _():
                o_ref[...] += r_ref[q].astype(F32)

    return pl.pallas_call(
        body, name=name,
        grid_spec=pltpu.PrefetchScalarGridSpec(
            num_scalar_prefetch=1, grid=(1,),
            in_specs=[pl.BlockSpec((None, r, cols), lambda i, me_ref: (me_ref[0], 0, 0)),
                      pl.BlockSpec((N_CHIPS, r, cols), lambda i, me_ref: (0, 0, 0))],
            out_specs=pl.BlockSpec((r, cols), lambda i, me_ref: (0, 0))),
        out_shape=jax.ShapeDtypeStruct((r, cols), F32),
        compiler_params=_ARB1,
    )(me_arr, own, recv)


def _pair_share(halves):
    na = len(halves)

    def body(*refs):
        ins, outs = refs[:na], refs[na:2 * na]
        send_sems, recv_sems = refs[2 * na:]
        x, y, c, _, _ = _place()
        copies = [pltpu.make_async_remote_copy(
            src_ref=ins[a], dst_ref=outs[a], send_sem=send_sems.at[a], recv_sem=recv_sems.at[a],
            device_id=(x, y, 1 - c), device_id_type=MESH) for a in range(na)]
        for cp in copies:
            cp.start()
        for cp in copies:
            cp.wait()

    return pl.pallas_call(
        body, name="pair_share",
        in_specs=[ANY] * na, out_specs=[ANY] * na,
        out_shape=[jax.ShapeDtypeStruct(h.shape, h.dtype) for h in halves],
        scratch_shapes=[pltpu.SemaphoreType.DMA((na,)), pltpu.SemaphoreType.DMA((na,))],
    )(*halves)


def _small_allreduce(block):
    m, n = block.shape

    def body(x_ref, all_ref, sum_ref, send_sems, recv_sems, local_sem):
        x, y, c, _, chips = _place()
        me, sibling = (x, y, c), (x, y, 1 - c)

        def rows(px, py, pc):
            return all_ref.at[pl.ds((4 * px + 2 * py + pc) * m, m), :]

        def copy(k, blk, to, src=None):
            return pltpu.make_async_remote_copy(
                src_ref=rows(*blk) if src is None else src, dst_ref=rows(*blk),
                send_sem=send_sems.at[k], recv_sem=recv_sems.at[k], device_id=to, device_id_type=MESH)

        mine = pltpu.make_async_copy(x_ref, rows(*me), local_sem)
        mine.start()
        first = [copy(0, me, sibling, src=x_ref)]
        first += [copy(1 + j, me, (*chip, c), src=x_ref) for j, chip in enumerate(chips)]
        for cp in first:
            cp.start()
        passed = [copy(4 + j, (*chip, c), sibling) for j, chip in enumerate(chips)]
        for j, chip in enumerate(chips):
            copy(1 + j, (*chip, c), me).wait_recv()
            passed[j].start()
        copy(0, sibling, me).wait_recv()
        for j, chip in enumerate(chips):
            copy(4 + j, (*chip, 1 - c), me).wait_recv()
        for cp in first + passed:
            cp.wait_send()
        mine.wait()
        acc = all_ref[0:m, :]
        for d in range(1, 8):
            acc = acc + all_ref[d * m:(d + 1) * m, :]
        sum_ref[...] = acc

    vmem = pl.BlockSpec(memory_space=pltpu.VMEM)
    return pl.pallas_call(
        body, name="small_allreduce",
        in_specs=[vmem], out_specs=[vmem, vmem],
        out_shape=[jax.ShapeDtypeStruct((8 * m, n), F32), jax.ShapeDtypeStruct((m, n), F32)],
        scratch_shapes=[pltpu.SemaphoreType.DMA((7,)), pltpu.SemaphoreType.DMA((7,)), pltpu.SemaphoreType.DMA],
    )(block)[1]


def _row_tile(rows):
    best = rows
    for cand in range(8, min(rows, 512) + 1, 8):
        if rows % cand == 0:
            best = cand
    return best


def _adamw_math(w, g, m, v):
    m2 = ADAM_B1 * m + (1.0 - ADAM_B1) * g
    v2 = ADAM_B2 * v + (1.0 - ADAM_B2) * (g * g)
    m_hat = m2 / (1.0 - ADAM_B1 ** ADAM_STEP)
    v_hat = v2 / (1.0 - ADAM_B2 ** ADAM_STEP)
    return -ADAM_LR * (m_hat / (jnp.sqrt(v_hat) + ADAM_EPS) + ADAM_WD * w), m2, v2


def _adamw_halves(w, g_mine, g_other, m, v, c_arr, name):
    rows, cols = w.shape
    r = rows // 2
    tr = _row_tile(r)
    nt = r // tr

    def body(c_ref, w_ref, gm_ref, go_ref, m_ref, v_ref, g_ref, d_ref, nm_ref, nv_ref):
        gv = jnp.where(pl.program_id(0) == c_ref[0], gm_ref[...], go_ref[...])
        g_ref[...] = gv
        d_ref[...], nm_ref[...], nv_ref[...] = _adamw_math(w_ref[...], gv, m_ref[...], v_ref[...])

    full = pl.BlockSpec((tr, cols), lambda h, i, c_ref: (h * nt + i, 0))
    half = pl.BlockSpec((tr, cols), lambda h, i, c_ref: (i, 0))
    shape = jax.ShapeDtypeStruct((rows, cols), F32)
    return pl.pallas_call(
        body, name=name,
        grid_spec=pltpu.PrefetchScalarGridSpec(
            num_scalar_prefetch=1, grid=(2, nt),
            in_specs=[full, half, half, full, full], out_specs=[full] * 4),
        out_shape=[shape] * 4,
        compiler_params=_ARB2,
    )(c_arr, w, g_mine, g_other, m, v)


def _adamw(w, g, m, v, name):
    rows, cols = w.shape
    tr = _row_tile(rows)

    def body(w_ref, g_ref, m_ref, v_ref, d_ref, nm_ref, nv_ref):
        d_ref[...], nm_ref[...], nv_ref[...] = _adamw_math(w_ref[...], g_ref[...], m_ref[...], v_ref[...])

    spec = pl.BlockSpec((tr, cols), lambda i: (i, 0))
    shape = jax.ShapeDtypeStruct((rows, cols), F32)
    return pl.pallas_call(
        body, name=name, grid=(rows // tr,),
        in_specs=[spec] * 4, out_specs=[spec] * 3, out_shape=[shape] * 3,
        compiler_params=_ARB1,
    )(w, g, m, v)


def _in_columns():
    pieces = []
    for group in range(2):
        q0, k0, v0, g0 = (0, 256, 512, 1024) if group == 0 else (1536, 1792, 2048, 2560)
        for h in range(HEADS):
            pieces += [(q0 + 64 * h, 64), (k0 + 64 * h, 64), (v0 + 128 * h, 128), (g0 + 128 * h, 128)]
    pieces.append((3072, GATE_RANK))
    return pieces


def _pad_w_in_t(w_in_t):
    parts = []
    for start, width in _in_columns():
        parts.append(w_in_t[start:start + width])
        if width < LANES:
            parts.append(jnp.zeros((LANES - width, w_in_t.shape[1]), w_in_t.dtype))
    return jnp.concatenate(parts, axis=0)


def _unpad_w_in_t(w_pt):
    rows = {}
    offset = 0
    for start, width in _in_columns():
        rows[start] = w_pt[offset:offset + width]
        offset += LANES
    return jnp.concatenate([rows[s] for s in sorted(rows)], axis=0)


def _rope_tables(t):
    half = 32
    inv = ROPE_BASE ** (-jnp.arange(half, dtype=F32) * 2.0 / 64)
    ang = jnp.arange(t, dtype=F32)[:, None] * inv[None, :]
    cos, sin = jnp.cos(ang), jnp.sin(ang)
    z32, z64 = jnp.zeros((t, 32), F32), jnp.zeros((t, 64), F32)
    return (jnp.concatenate([cos, cos, z64], axis=1),
            jnp.concatenate([-sin, z32, z64], axis=1),
            jnp.concatenate([z32, sin, z64], axis=1))


def _halves(w):
    n, rows, cols = w.shape
    return w.reshape(n, 2, rows // 2, cols)


def _pack_small(n1, nm, n2, nf, nret, ngla, ba, wa2, wa2_cols, extra=None):
    z = lambda k: jnp.zeros((1, k), F32)
    rows = [n1.reshape(1, -1), nm.reshape(1, -1), n2.reshape(1, -1), nf.reshape(1, -1),
            jnp.concatenate([nret.reshape(1, -1), ngla.reshape(1, -1)], axis=1),
            jnp.concatenate([ba.reshape(1, -1), z(D_MODEL - 256)], axis=1),
            jnp.zeros((1, D_MODEL), F32) if extra is None else extra,
            jnp.zeros((1, D_MODEL), F32),
            jnp.concatenate([wa2.reshape(GATE_RANK, wa2_cols), jnp.zeros((GATE_RANK, D_MODEL - wa2_cols), F32)], axis=1),
            jnp.zeros((SMALL_ROWS - 8 - GATE_RANK, D_MODEL), F32)]
    return jnp.concatenate(rows, axis=0)


def _unpack_small(p, wa2_cols):
    return (p[0:1], p[1:2], p[2:3], p[3], p[4:5, 0:512], p[4:5, 512:1024], p[5:6, 0:256],
            p[8:8 + GATE_RANK, 0:wa2_cols].reshape(1, GATE_RANK, wa2_cols))


def _pad_in_rows(w_t):
    return jnp.pad(w_t, ((0, IN_ROWS - IN_SHARD), (0, 0)))


def _forward_backward(xs, target, ffn1_w, rest, ba_p, ffn1_norm_g, mix_norm_g, ret_norm_g, gla_norm_g, ffn2_norm_g,
                      final_norm_g, rest_plan=None, rest_weights=None, early=None):
    t = xs.shape[0]
    cos_t, sa_t, sb_t = _rope_tables(t)
    log_gamma = jnp.log(1.0 - 2.0 ** (-5.0 - jnp.arange(HEADS, dtype=F32)))
    lg_t = jnp.broadcast_to(log_gamma[:, None, None], (HEADS, 1, LANES))
    ret_aux = [cos_t, sa_t, sb_t, lg_t]

    (x1, a1, u1, h1), gathered = _ffn_fwd(xs, ffn1_norm_g, ffn1_w, "ffn1_fwd", hosted=rest_plan)
    ffn2_w, w_in_pt, w_out_full, wa2_p = rest if rest_plan is None else rest_weights(gathered)
    proj, h_mix = _mixer_in_fwd(x1, mix_norm_g, w_in_pt, "mixer_in_fwd")
    gla_aux = [proj, wa2_p, ba_p]
    o_ret, raw_ret, st_ret = _attn_fwd(True, proj, ret_aux, ret_norm_g, "ret_fwd")
    o_gla, raw_gla, st_gla = _attn_fwd(False, proj, gla_aux, gla_norm_g, "gla_fwd")
    x2 = _mixer_out_fwd(o_ret, o_gla, w_out_full, x1, "mixer_out_fwd")
    (x3, a2, u2, h2), _ = _ffn_fwd(x2, ffn2_norm_g, ffn2_w, "ffn2_fwd")
    loss_blk, dx3, d_final_g = _final_loss(x3, final_norm_g, target, "final_loss")

    (da2, du2, hid2, dob2, dx2, d_ffn2_g), _ = _ffn_bwd(dx3, x2, ffn2_norm_g, a2, u2, ffn2_w, "ffn2_bwd")
    g_gate2 = _matmul_tn(da2, h2, "ffn2_dgate", out_dtype=BF16)
    g_up2 = _matmul_tn(du2, h2, "ffn2_dup", out_dtype=BF16)
    g_down2 = _matmul_tn(hid2, dob2, "ffn2_ddown", out_dtype=BF16)

    d_o = _matmul_nt(dx2, w_out_full, "mixer_out_bwd")
    g_wout_ret = _matmul_tn(o_ret, dx2, "wout_grad_ret", out_dtype=BF16)
    g_wout_gla = _matmul_tn(o_gla, dx2, "wout_grad_gla", out_dtype=BF16)
    dproj_ret, d_ret_g = _attn_bwd(True, proj, ret_aux, ret_norm_g, raw_ret, st_ret, d_o, "ret_bwd")
    dproj_gla, d_gla_g, dlogit, d_ba_p = _attn_bwd(False, proj, gla_aux, gla_norm_g, raw_gla, st_gla, d_o, "gla_bwd")
    d_glow = _matmul_nt(dlogit, wa2_p, "gate_low_bwd", out_dtype=BF16)
    g_wa2_p = _matmul_tn(proj[:, PROJ_P - LANES:], dlogit, "gate_w_grad")
    dproj = jnp.concatenate([dproj_ret, dproj_gla, d_glow], axis=1)
    g_win_p = _matmul_tn(dproj, h_mix, "w_in_grad", tka=PROJ_P // 3, out_dtype=BF16)
    dx1, d_mix_g = _mixer_in_bwd(dproj, w_in_pt, dx2, x1, mix_norm_g, "mixer_in_bwd")
    g_win_t = _unpad_w_in_t(g_win_p[0])
    g_win = jnp.stack([_pad_in_rows(g_win_t[IN_SHARD * p:IN_SHARD * (p + 1)]) for p in range(N_CHIPS)], axis=0)
    g_wout = jnp.concatenate([g_wout_ret[0], g_wout_gla[0]], axis=0).reshape(N_CHIPS, D_MODEL // N_CHIPS, D_MODEL)

    early_plan = None if early is None else early([g_gate2, g_up2, g_down2, g_win, g_wout])
    (da1, du1, hid1, dob1, grad_x, d_ffn1_g), arrived = _ffn_bwd(dx1, xs, ffn1_norm_g, a1, u1, ffn1_w, "ffn1_bwd",
                                                                hosted=early_plan)
    g_gate1 = _matmul_tn(da1, h1, "ffn1_dgate", out_dtype=BF16)
    g_up1 = _matmul_tn(du1, h1, "ffn1_dup", out_dtype=BF16)
    g_down1 = _matmul_tn(hid1, dob1, "ffn1_ddown", out_dtype=BF16)

    return (loss_blk, grad_x, g_gate1, g_up1, g_down1, g_gate2, g_up2, g_down2, g_win, g_wout, g_wa2_p,
            d_ba_p, d_ffn1_g, d_mix_g, d_ffn2_g, d_final_g, d_ret_g, d_gla_g, arrived)


def kernel(x, ffn1_norm_g, ffn1_w_gate, ffn1_w_up, ffn1_w_down, mix_norm_g, w_in, ret_norm_g, gla_w_a2, gla_b_a, gla_norm_g, w_out, ffn2_norm_g, ffn2_w_gate, ffn2_w_up, ffn2_w_down, final_norm_g, loss_target, m_ffn1_norm_g, m_ffn1_w_gate, m_ffn1_w_up, m_ffn1_w_down, m_mix_norm_g, m_w_in, m_ret_norm_g, m_gla_w_a2, m_gla_b_a, m_gla_norm_g, m_w_out, m_ffn2_norm_g, m_ffn2_w_gate, m_ffn2_w_up, m_ffn2_w_down, m_final_norm_g, v_ffn1_norm_g, v_ffn1_w_gate, v_ffn1_w_up, v_ffn1_w_down, v_mix_norm_g, v_w_in, v_ret_norm_g, v_gla_w_a2, v_gla_b_a, v_gla_norm_g, v_w_out, v_ffn2_norm_g, v_ffn2_w_gate, v_ffn2_w_up, v_ffn2_w_down, v_final_norm_g):
    t = x.shape[1]
    xs = x.reshape(t, D_MODEL)
    target = loss_target.reshape(t, D_MODEL)
    chip = 2 * lax.axis_index("x") + lax.axis_index("y")
    c_arr = lax.axis_index("c").astype(jnp.int32).reshape(1)

    me_arr = chip.astype(jnp.int32).reshape(1)

    pad_rows = _pad_in_rows

    def own_block(gathered, shard):
        return lax.dynamic_update_slice(gathered, shard[None], (chip,) + (0,) * shard.ndim)

    ffn1_shard = _halves(jnp.stack([ffn1_w_gate[0].T, ffn1_w_up[0].T, ffn1_w_down[0]], axis=0).astype(BF16))
    rest_shards = [_halves(jnp.stack([ffn2_w_gate[0].T, ffn2_w_up[0].T, ffn2_w_down[0]], axis=0).astype(BF16)),
                   _halves(pad_rows(w_in[0].T).astype(BF16)[None]),
                   _halves(w_out.astype(BF16)),
                   jnp.concatenate([gla_w_a2.reshape(GATE_RANK, 64), jnp.zeros((GATE_RANK, 64), F32)],
                                   axis=1).reshape(1, 2, 8, LANES)]
    ffn1_all = _run_hosted(_gather_plan([ffn1_shard]), "gather_ffn1")[0]
    ffn1_w = own_block(ffn1_all, ffn1_shard).reshape(N_CHIPS, 3, FF_SHARD, D_MODEL)

    def rest_weights(gathered):
        ffn2_all, win_all, wout_all, wa2_all = [own_block(g, s) for g, s in zip(gathered, rest_shards)]
        win_t = win_all.reshape(N_CHIPS, IN_ROWS, D_MODEL)
        w_in_pt = _pad_w_in_t(jnp.concatenate([win_t[p, 0:IN_SHARD] for p in range(N_CHIPS)], axis=0))
        wa2_p = jnp.pad(
            wa2_all.reshape(N_CHIPS, GATE_RANK, LANES).transpose(1, 0, 2).reshape(GATE_RANK, HEADS * LANES),
            ((0, LANES - GATE_RANK), (0, 0))).astype(BF16)
        return (ffn2_all.reshape(N_CHIPS, 3, FF_SHARD, D_MODEL), w_in_pt, wout_all.reshape(D_MODEL, D_MODEL), wa2_p)

    def pair_sums(grads, tag):
        halves = [g.reshape(g.shape[0], 2, g.shape[1] // 2, g.shape[2]) for g in grads]
        recv = _pair_exchange(halves, "pair_exchange_" + tag)
        return [_pair_add(g, r, c_arr, "pair_add_%s%d" % (tag, k)) for k, (g, r) in enumerate(zip(halves, recv))]

    early_sums = []

    def early(grads):
        early_sums.extend(pair_sums(grads, "early"))
        return _chip_exchange_plan(early_sums)

    ba_p = jnp.pad(gla_b_a.reshape(HEADS, 64), ((0, 0), (0, 64))).reshape(1, HEADS * LANES)
    fb = _forward_backward(xs, target, ffn1_w, None, ba_p, ffn1_norm_g, mix_norm_g, ret_norm_g, gla_norm_g,
                           ffn2_norm_g, final_norm_g.reshape(1, D_MODEL), rest_plan=_gather_plan(rest_shards),
                           rest_weights=rest_weights, early=early)
    (loss_blk, grad_x, g_gate1, g_up1, g_down1, _, _, _, _, _, g_wa2_p,
     d_ba_p, d_ffn1_g, d_mix_g, d_ffn2_g, d_final_g, d_ret_g, d_gla_g, early_arrived) = fb
    late_sums = pair_sums([g_gate1, g_up1, g_down1], "late")
    late_arrived = _run_hosted(_chip_exchange_plan(late_sums), "chip_exchange_late")
    sums, arrived = late_sums + early_sums, late_arrived + early_arrived
    mine = [_chip_sum(s, r, me_arr, "chip_sum_%d" % k) for k, (s, r) in enumerate(zip(sums, arrived))]
    other = _pair_share(mine)

    g_wa2 = g_wa2_p[0][0:GATE_RANK].reshape(GATE_RANK, HEADS, LANES)[:, :, 0:64].reshape(GATE_RANK, 256)
    d_ba = d_ba_p.reshape(HEADS, LANES)[:, 0:64].reshape(1, 256)
    loss_row = jnp.pad(loss_blk[0:1, 0:1], ((0, 0), (0, D_MODEL - 1)))
    small_local = _pack_small(d_ffn1_g, d_mix_g, d_ffn2_g, d_final_g, d_ret_g, d_gla_g, d_ba, g_wa2, 256, loss_row)
    small_sum = _small_allreduce(small_local)
    loss = small_sum[6, 0]
    sg = _unpack_small(small_sum, 256)
    wa2_grad = lax.dynamic_slice(sg[7], (0, 0, 64 * chip), (1, GATE_RANK, 64))
    small_g = _pack_small(*sg[:7], wa2_grad, 64)
    small_w = _pack_small(ffn1_norm_g, mix_norm_g, ffn2_norm_g, final_norm_g, ret_norm_g, gla_norm_g, gla_b_a, gla_w_a2, 64)
    small_m = _pack_small(m_ffn1_norm_g, m_mix_norm_g, m_ffn2_norm_g, m_final_norm_g, m_ret_norm_g, m_gla_norm_g,
                          m_gla_b_a, m_gla_w_a2, 64)
    small_v = _pack_small(v_ffn1_norm_g, v_mix_norm_g, v_ffn2_norm_g, v_final_norm_g, v_ret_norm_g, v_gla_norm_g,
                          v_gla_b_a, v_gla_w_a2, 64)
    small_out = _adamw(small_w, small_g, small_m, small_v, "adamw_small")
    s_grad = _unpack_small(small_g, 64)
    s_delta, s_m, s_v = (_unpack_small(o, 64) for o in small_out)

    def big(k, w, m, v, name, to_2d, from_2d):
        outs4 = _adamw_halves(to_2d(w), mine[k], other[k], to_2d(m), to_2d(v), c_arr, name)
        return [from_2d(z) for z in outs4]

    plain = (lambda w: w[0], lambda z: z[None])
    transposed = (lambda w: w[0].T, lambda z: z.T[None])
    in_proj = (lambda w: pad_rows(w[0].T), lambda z: z[0:IN_SHARD].T[None])
    r_g1 = big(0, ffn1_w_gate, m_ffn1_w_gate, v_ffn1_w_gate, "adamw_ffn1_gate", *transposed)
    r_u1 = big(1, ffn1_w_up, m_ffn1_w_up, v_ffn1_w_up, "adamw_ffn1_up", *transposed)
    r_d1 = big(2, ffn1_w_down, m_ffn1_w_down, v_ffn1_w_down, "adamw_ffn1_down", *plain)
    r_g2 = big(3, ffn2_w_gate, m_ffn2_w_gate, v_ffn2_w_gate, "adamw_ffn2_gate", *transposed)
    r_u2 = big(4, ffn2_w_up, m_ffn2_w_up, v_ffn2_w_up, "adamw_ffn2_up", *transposed)
    r_d2 = big(5, ffn2_w_down, m_ffn2_w_down, v_ffn2_w_down, "adamw_ffn2_down", *plain)
    r_in = big(6, w_in, m_w_in, v_w_in, "adamw_w_in", *in_proj)
    r_out = big(7, w_out, m_w_out, v_w_out, "adamw_w_out", *plain)

    def leaves(k, smalls):
        n1, nm, n2, nf, nret, ngla, ba, wa2 = smalls
        return [n1, r_g1[k], r_u1[k], r_d1[k], nm, r_in[k], nret, wa2, ba, ngla, r_out[k], n2, r_g2[k], r_u2[k], r_d2[k], nf]

    outs = [loss, grad_x.reshape(x.shape)]
    outs += leaves(0, s_grad) + leaves(1, s_delta) + leaves(2, s_m) + leaves(3, s_v)
    return tuple(outs)
```

```python
import functools

import jax
import jax.numpy as jnp
from jax import lax
from jax.experimental import pallas as pl
from jax.experimental.pallas import tpu as pltpu

F32, BF16 = jnp.float32, jnp.bfloat16
MESH = pl.DeviceIdType.MESH
ANY = pl.BlockSpec(memory_space=pl.ANY)

D_MODEL = 1024
D_FF = 2816
N_CHIPS = 4
FF_SHARD = D_FF // N_CHIPS
IN_WIDTH = 3088
IN_SHARD = IN_WIDTH // N_CHIPS
IN_ROWS = 800
CHUNK = 64
HEADS = 4
LANES = 128
HEAD_BLOCK = 4 * LANES
PROJ_P = 2 * HEADS * HEAD_BLOCK + LANES
GATE_RANK = 16
QK_SCALE = 0.125
GATE_NORM = 16.0
RMS_EPS = 1e-6
ROPE_BASE = 10000.0
ADAM_LR, ADAM_B1, ADAM_B2, ADAM_EPS, ADAM_WD, ADAM_STEP = 0.001, 0.9, 0.999, 1e-08, 0.01, 10
SMALL_ROWS = 32
TOKEN_TILE = 512
ATTN_TILE = 512

_ARB2 = pltpu.CompilerParams(dimension_semantics=("arbitrary", "arbitrary"))
_ARB1 = pltpu.CompilerParams(dimension_semantics=("arbitrary",))
_ARB3 = pltpu.CompilerParams(dimension_semantics=("arbitrary", "arbitrary", "arbitrary"))


def _dot(a, b):
    return jnp.dot(a, b, preferred_element_type=F32)


def _dot_nt(a, b):
    return lax.dot_general(a, b, (((1,), (1,)), ((), ())), preferred_element_type=F32)


def _dot_tn(a, b):
    return lax.dot_general(a, b, (((0,), (0,)), ((), ())), preferred_element_type=F32)


def _rms_scale(xv):
    return lax.rsqrt(jnp.mean(xv * xv, axis=-1, keepdims=True) + RMS_EPS)


def _rms_bwd(dh, xv, g):
    r = _rms_scale(xv)
    xhat = xv * r
    dxhat = dh * g
    dx = r * (dxhat - xhat * jnp.mean(dxhat * xhat, axis=-1, keepdims=True))
    return dx, jnp.sum(dh * xhat, axis=0, keepdims=True)


def _silu_grad(a, sg):
    return sg * (1.0 + a * (1.0 - sg))


class _Hosted:
    def __init__(self, arrays, out_shapes, n_sems, start, finish):
        self.arrays, self.out_shapes, self.n_sems = list(arrays), list(out_shapes), n_sems
        self.start, self.finish = start, finish


def _call(body, args, *, name, grid, in_specs, out_specs, out_shape, scratch_shapes, compiler_params, hosted=None):
    if hosted is None:
        outs = pl.pallas_call(body, name=name, grid=grid, in_specs=in_specs, out_specs=out_specs, out_shape=out_shape,
                              scratch_shapes=scratch_shapes, compiler_params=compiler_params)(*args)
        return list(outs), []
    n_in, n_out, n_sc, nh = len(in_specs), len(out_specs), len(scratch_shapes), len(hosted.arrays)

    def wrapped(*refs):
        ins, h_in = refs[:n_in], refs[n_in:n_in + nh]
        outs, h_out = refs[n_in + nh:n_in + nh + n_out], refs[n_in + nh + n_out:n_in + 2 * nh + n_out]
        rest = refs[n_in + 2 * nh + n_out:]
        scratch, (send_sems, recv_sems) = rest[:n_sc], rest[n_sc:]
        first = functools.reduce(jnp.logical_and, [pl.program_id(d) == 0 for d in range(len(grid))])
        last = functools.reduce(jnp.logical_and, [pl.program_id(d) == n - 1 for d, n in enumerate(grid)])

        @pl.when(first)
        def _():
            hosted.start(h_in, h_out, send_sems, recv_sems)

        body(*ins, *outs, *scratch)

        @pl.when(last)
        def _():
            hosted.finish(h_in, h_out, send_sems, recv_sems)

    sems = [pltpu.SemaphoreType.DMA((hosted.n_sems,)), pltpu.SemaphoreType.DMA((hosted.n_sems,))]
    outs = pl.pallas_call(
        wrapped, name=name, grid=grid, in_specs=list(in_specs) + [ANY] * nh, out_specs=list(out_specs) + [ANY] * nh,
        out_shape=list(out_shape) + hosted.out_shapes, scratch_shapes=list(scratch_shapes) + sems,
        compiler_params=compiler_params)(*args, *hosted.arrays)
    return list(outs[:n_out]), list(outs[n_out:])


def _run_hosted(hosted, name):
    nh = len(hosted.arrays)

    def body(*refs):
        h_in, h_out, (send_sems, recv_sems) = refs[:nh], refs[nh:2 * nh], refs[2 * nh:]
        hosted.start(h_in, h_out, send_sems, recv_sems)
        hosted.finish(h_in, h_out, send_sems, recv_sems)

    sems = [pltpu.SemaphoreType.DMA((hosted.n_sems,)), pltpu.SemaphoreType.DMA((hosted.n_sems,))]
    return list(pl.pallas_call(body, name=name, in_specs=[ANY] * nh, out_specs=[ANY] * nh,
                               out_shape=hosted.out_shapes, scratch_shapes=sems)(*hosted.arrays))


def _pipeline_items(steps):
    def cur(s):
        c = jnp.minimum(s, steps - 1)
        return c // N_CHIPS, c % N_CHIPS

    def prev(s):
        p = jnp.maximum(s - 1, 0)
        return p // N_CHIPS, p % N_CHIPS

    return cur, prev


def _ffn_fwd(x, g, ffn_w, name, hosted=None):
    t = x.shape[0]
    tm = min(t, TOKEN_TILE)

    def body(x_ref, g_ref, wg_ref, wu_ref, wd_ref, xo_ref, a_ref, u_ref, h_ref, acc_ref):
        j = pl.program_id(1)

        @pl.when(j == 0)
        def _():
            xv = x_ref[...]
            h_ref[...] = ((xv * _rms_scale(xv)) * g_ref[...]).astype(BF16)
            acc_ref[...] = jnp.zeros_like(acc_ref)

        h = h_ref[...]
        a = _dot_nt(h, wg_ref[...])
        u = _dot_nt(h, wu_ref[...])
        a_ref[...] = a.astype(BF16)
        u_ref[...] = u.astype(BF16)
        hid = (a * jax.nn.sigmoid(a)) * u
        acc_ref[...] += _dot(hid.astype(BF16), wd_ref[...])

        @pl.when(j == N_CHIPS - 1)
        def _():
            xo_ref[...] = x_ref[...] + 0.5 * acc_ref[...]

    tok = pl.BlockSpec((tm, D_MODEL), lambda i, j: (i, 0))
    act = pl.BlockSpec((None, tm, FF_SHARD), lambda i, j: (j, i, 0))
    wblk = (None, None, FF_SHARD, D_MODEL)
    weights = [pl.BlockSpec(wblk, lambda i, j, k=kind: (j, k, 0, 0)) for kind in range(3)]
    return _call(
        body, (x, g, ffn_w, ffn_w, ffn_w), name=name, grid=(t // tm, N_CHIPS),
        in_specs=[tok, pl.BlockSpec((1, D_MODEL), lambda i, j: (0, 0))] + weights,
        out_specs=[tok, act, act, tok],
        out_shape=[jax.ShapeDtypeStruct((t, D_MODEL), F32),
                   jax.ShapeDtypeStruct((N_CHIPS, t, FF_SHARD), BF16),
                   jax.ShapeDtypeStruct((N_CHIPS, t, FF_SHARD), BF16),
                   jax.ShapeDtypeStruct((t, D_MODEL), BF16)],
        scratch_shapes=[pltpu.VMEM((tm, D_MODEL), F32)],
        compiler_params=_ARB2, hosted=hosted)


def _ffn_bwd(dxo, x, g, a4, u4, ffn_w, name, hosted=None):
    t = x.shape[0]
    tm = min(t, TOKEN_TILE)
    steps = (t // tm) * N_CHIPS
    cur, prev = _pipeline_items(steps)


    def body(dxo_ref, dxo_prev_ref, x_ref, g_ref, a_ref, u_ref, wg_ref, wu_ref, wd_ref,
             da_ref, du_ref, hid_ref, dob_ref, dx_ref, dg_ref, acc_ref, da_slots, du_slots):
        s = pl.program_id(0)
        jc, jp = cur(s)[1], prev(s)[1]
        slot = s % 2

        @pl.when(s == 0)
        def _():
            dg_ref[...] = jnp.zeros_like(dg_ref)
            acc_ref[...] = jnp.zeros_like(acc_ref)
            da_slots[...] = jnp.zeros_like(da_slots)
            du_slots[...] = jnp.zeros_like(du_slots)

        @pl.when(jc == 0)
        def _():
            dob_ref[...] = (0.5 * dxo_ref[...]).astype(BF16)

        dhid = _dot_nt(dob_ref[...], wd_ref[...])
        a = a_ref[...].astype(F32)
        u = u_ref[...].astype(F32)
        sg = jax.nn.sigmoid(a)
        sl = a * sg
        hid_ref[...] = (sl * u).astype(BF16)
        du = (dhid * sl).astype(BF16)
        da = (dhid * u * _silu_grad(a, sg)).astype(BF16)
        du_ref[...] = du
        da_ref[...] = da
        acc_ref[...] += _dot(da_slots[1 - slot], wg_ref[...]) + _dot(du_slots[1 - slot], wu_ref[...])
        da_slots[slot] = da
        du_slots[slot] = du

        @pl.when((jp == N_CHIPS - 1) & (s > 0))
        def _():
            dx, dg = _rms_bwd(acc_ref[...], x_ref[...], g_ref[...])
            dx_ref[...] = dxo_prev_ref[...] + dx
            dg_ref[...] += dg
            acc_ref[...] = jnp.zeros_like(acc_ref)

    tok_cur = pl.BlockSpec((tm, D_MODEL), lambda s: (cur(s)[0], 0))
    tok_prev = pl.BlockSpec((tm, D_MODEL), lambda s: (prev(s)[0], 0))
    act = pl.BlockSpec((None, tm, FF_SHARD), lambda s: (cur(s)[1], cur(s)[0], 0))
    row = pl.BlockSpec((1, D_MODEL), lambda s: (0, 0))
    wblk = (None, None, FF_SHARD, D_MODEL)
    weights = [pl.BlockSpec(wblk, lambda s: (prev(s)[1], 0, 0, 0)), pl.BlockSpec(wblk, lambda s: (prev(s)[1], 1, 0, 0)),
               pl.BlockSpec(wblk, lambda s: (cur(s)[1], 2, 0, 0))]
    act_shape = jax.ShapeDtypeStruct((N_CHIPS, t, FF_SHARD), BF16)
    return _call(
        body, (dxo, dxo, x, g, a4, u4, ffn_w, ffn_w, ffn_w), name=name, grid=(steps + 1,),
        in_specs=[tok_cur, tok_prev, tok_prev, row, act, act] + weights,
        out_specs=[act, act, act, tok_cur, tok_prev, row],
        out_shape=[act_shape, act_shape, act_shape,
                   jax.ShapeDtypeStruct((t, D_MODEL), BF16),
                   jax.ShapeDtypeStruct((t, D_MODEL), F32),
                   jax.ShapeDtypeStruct((1, D_MODEL), F32)],
        scratch_shapes=[pltpu.VMEM((tm, D_MODEL), F32), pltpu.VMEM((2, tm, FF_SHARD), BF16),
                        pltpu.VMEM((2, tm, FF_SHARD), BF16)],
        compiler_params=_ARB1, hosted=hosted)


def _matmul_tn(a, b, name, tka=None, out_dtype=F32, hosted=None):
    a3, b3 = a.ndim == 3, b.ndim == 3
    nb = a.shape[0] if a3 else (b.shape[0] if b3 else 1)
    t, ka, n = a.shape[-2], a.shape[-1], b.shape[-1]
    tka = ka if tka is None else tka
    tk = min(t, 2 * TOKEN_TILE)
    nk = t // tk

    def body(a_ref, b_ref, o_ref, acc_ref):
        k = pl.program_id(2)

        @pl.when(k == 0)
        def _():
            acc_ref[...] = jnp.zeros_like(acc_ref)

        acc_ref[...] += _dot_tn(a_ref[...].astype(BF16), b_ref[...].astype(BF16))

        @pl.when(k == nk - 1)
        def _():
            o_ref[...] = acc_ref[...].astype(out_dtype)

    a_spec = (pl.BlockSpec((None, tk, tka), lambda i, j, k: (i, k, j)) if a3
              else pl.BlockSpec((tk, tka), lambda i, j, k: (k, j)))
    b_spec = (pl.BlockSpec((None, tk, n), lambda i, j, k: (i, k, 0)) if b3
              else pl.BlockSpec((tk, n), lambda i, j, k: (k, 0)))
    outs, carried = _call(
        body, (a, b), name=name, grid=(nb, ka // tka, t // tk),
        in_specs=[a_spec, b_spec],
        out_specs=[pl.BlockSpec((None, tka, n), lambda i, j, k: (i, j, 0))],
        out_shape=[jax.ShapeDtypeStruct((nb, ka, n), out_dtype)],
        scratch_shapes=[pltpu.VMEM((tka, n), F32)],
        compiler_params=_ARB3, hosted=hosted)
    return outs[0] if hosted is None else (outs[0], carried)


def _matmul_nt(a, w, name, out_dtype=F32):
    t, k = a.shape
    n = w.shape[0]
    tm = min(t, TOKEN_TILE)

    def body(a_ref, w_ref, o_ref):
        o_ref[...] = _dot_nt(a_ref[...].astype(BF16), w_ref[...]).astype(out_dtype)

    return pl.pallas_call(
        body, name=name, grid=(t // tm,),
        in_specs=[pl.BlockSpec((tm, k), lambda i: (i, 0)), pl.BlockSpec((n, k), lambda i: (0, 0))],
        out_specs=pl.BlockSpec((tm, n), lambda i: (i, 0)),
        out_shape=jax.ShapeDtypeStruct((t, n), out_dtype),
        compiler_params=_ARB1,
    )(a, w)


def _mixer_in_bwd(dproj, w_in_pt, dres, x, g, name):
    t, k = dproj.shape
    tm = min(t, TOKEN_TILE)

    def body(a_ref, w_ref, dres_ref, x_ref, g_ref, dx_ref, dg_ref):
        @pl.when(pl.program_id(0) == 0)
        def _():
            dg_ref[...] = jnp.zeros_like(dg_ref)

        dh = _dot(a_ref[...], w_ref[...])
        dx, dg = _rms_bwd(dh, x_ref[...], g_ref[...])
        dx_ref[...] = dres_ref[...] + dx
        dg_ref[...] += dg

    tok = pl.BlockSpec((tm, D_MODEL), lambda i: (i, 0))
    row = pl.BlockSpec((1, D_MODEL), lambda i: (0, 0))
    return pl.pallas_call(
        body, name=name, grid=(t // tm,),
        in_specs=[pl.BlockSpec((tm, k), lambda i: (i, 0)), pl.BlockSpec((k, D_MODEL), lambda i: (0, 0)), tok, tok, row],
        out_specs=[tok, row],
        out_shape=[jax.ShapeDtypeStruct((t, D_MODEL), F32), jax.ShapeDtypeStruct((1, D_MODEL), F32)],
        compiler_params=_ARB1,
    )(dproj, w_in_pt, dres, x, g)


def _mixer_in_fwd(x, g, w_in_pt, name):
    t = x.shape[0]
    tm = min(t, TOKEN_TILE)
    tn = PROJ_P // 3

    def body(x_ref, g_ref, w_ref, p_ref, h_ref):
        @pl.when(pl.program_id(1) == 0)
        def _():
            xv = x_ref[...]
            h_ref[...] = ((xv * _rms_scale(xv)) * g_ref[...]).astype(BF16)

        p_ref[...] = _dot_nt(h_ref[...], w_ref[...])

    tok = pl.BlockSpec((tm, D_MODEL), lambda i, j: (i, 0))
    return pl.pallas_call(
        body, name=name, grid=(t // tm, 3),
        in_specs=[tok, pl.BlockSpec((1, D_MODEL), lambda i, j: (0, 0)),
                  pl.BlockSpec((tn, D_MODEL), lambda i, j: (j, 0))],
        out_specs=[pl.BlockSpec((tm, tn), lambda i, j: (i, j)), tok],
        out_shape=[jax.ShapeDtypeStruct((t, PROJ_P), F32), jax.ShapeDtypeStruct((t, D_MODEL), BF16)],
        compiler_params=_ARB2,
    )(x, g, w_in_pt)


def _mixer_out_fwd(o_ret, o_gla, w_out, x, name):
    t = x.shape[0]
    tm = min(t, TOKEN_TILE)
    half = HEADS * LANES

    def body(a_ref, b_ref, w_ref, x_ref, o_ref):
        o_ref[...] = x_ref[...] + _dot(a_ref[...], w_ref[0:half, :]) + _dot(b_ref[...], w_ref[half:2 * half, :])

    tok = pl.BlockSpec((tm, D_MODEL), lambda i: (i, 0))
    hb = pl.BlockSpec((tm, half), lambda i: (i, 0))
    return pl.pallas_call(
        body, name=name, grid=(t // tm,),
        in_specs=[hb, hb, pl.BlockSpec((2 * half, D_MODEL), lambda i: (0, 0)), tok],
        out_specs=tok, out_shape=jax.ShapeDtypeStruct((t, D_MODEL), F32),
        compiler_params=_ARB1,
    )(o_ret, o_gla, w_out, x)


def _final_loss(x, g, target, name):
    t = x.shape[0]
    tm = min(t, TOKEN_TILE)

    def body(x_ref, g_ref, t_ref, l_ref, dx_ref, dg_ref):
        @pl.when(pl.program_id(0) == 0)
        def _():
            l_ref[...] = jnp.zeros_like(l_ref)
            dg_ref[...] = jnp.zeros_like(dg_ref)

        xv = x_ref[...]
        gv = g_ref[...]
        err = (xv * _rms_scale(xv)) * gv - t_ref[...]
        l_ref[...] += 0.5 * jnp.sum(jnp.mean(err * err, axis=-1, keepdims=True), axis=0, keepdims=True)
        dx, dg = _rms_bwd(err * (1.0 / D_MODEL), xv, gv)
        dx_ref[...] = dx
        dg_ref[...] += dg

    tok = pl.BlockSpec((tm, D_MODEL), lambda i: (i, 0))
    row = pl.BlockSpec((1, D_MODEL), lambda i: (0, 0))
    return pl.pallas_call(
        body, name=name, grid=(t // tm,),
        in_specs=[tok, row, tok],
        out_specs=[pl.BlockSpec((8, LANES), lambda i: (0, 0)), tok, row],
        out_shape=[jax.ShapeDtypeStruct((8, LANES), F32), jax.ShapeDtypeStruct((t, D_MODEL), F32),
                   jax.ShapeDtypeStruct((1, D_MODEL), F32)],
        compiler_params=_ARB1,
    )(x, g, target)


def _rot(v, cos, sa, sb):
    return v * cos + pltpu.roll(v, 96, 1) * sa + pltpu.roll(v, 32, 1) * sb


def _rot_t(d, cos, sa, sb):
    return d * cos + pltpu.roll(d * sa, 32, 1) + pltpu.roll(d * sb, 96, 1)


def _bmm(a, b):
    return jnp.einsum("cik,ckj->cij", a, b, preferred_element_type=F32)


def _bmm_nt(a, b):
    return jnp.einsum("cik,cjk->cij", a, b, preferred_element_type=F32)


def _bmm_tn(a, b):
    return jnp.einsum("cki,ckj->cij", a, b, preferred_element_type=F32)


def _masked_sum(mask, x):
    hi = x.astype(BF16)
    r1 = x - hi.astype(F32)
    mid = r1.astype(BF16)
    lo = (r1 - mid.astype(F32)).astype(BF16)
    return _bmm(mask, hi) + _bmm(mask, mid) + _bmm(mask, lo)


def _tile_inputs(is_ret, proj_ref, aux, nc):
    shape3 = (nc, CHUNK, LANES)
    q_raw = proj_ref[:, 0:LANES]
    k_raw = proj_ref[:, LANES:2 * LANES]
    v = proj_ref[:, 2 * LANES:3 * LANES]
    gate = proj_ref[:, 3 * LANES:4 * LANES]
    ri = lax.broadcasted_iota(jnp.int32, (nc, CHUNK, CHUNK), 1)
    ci = lax.broadcasted_iota(jnp.int32, (nc, CHUNK, CHUNK), 2)
    if is_ret:
        cos_ref, sa_ref, sb_ref, lg_ref = aux
        cos, sa, sb = cos_ref[...], sa_ref[...], sb_ref[...]
        q = _rot(q_raw, cos, sa, sb)
        k = _rot(k_raw, cos, sa, sb) * QK_SCALE
        steps = (lax.broadcasted_iota(jnp.int32, shape3, 1) + 1).astype(F32)
        b = steps * lg_ref[...]
        logit = None
    else:
        glow_ref, wa2_ref, ba_ref = aux
        logit = _dot(glow_ref[...].astype(BF16), wa2_ref[...]) + ba_ref[...]
        la = (jnp.minimum(logit, 0.0) - jnp.log1p(jnp.exp(-jnp.abs(logit)))) * (1.0 / GATE_NORM)
        b = _masked_sum((ci <= ri).astype(BF16), la.reshape(shape3))
        q = q_raw * QK_SCALE
        k = k_raw
    return q.reshape(shape3), k.reshape(shape3), v.reshape(shape3), gate, b, logit, ri, ci


def _tile_scores(q, k, b, ri, ci):
    mid = b[:, CHUNK // 2 - 1:CHUNK // 2, :]
    ep = jnp.exp(b - mid)
    en = jnp.exp(mid - b)
    qt, kt, qh, kh = q * ep, k * en, q * en, k * ep
    low = _bmm_nt(qt.astype(BF16), kt.astype(BF16))
    upp = _bmm_nt(qh.astype(BF16), kh.astype(BF16))
    scores = jnp.where(ci <= ri, low, upp)
    return scores, ep, en, qt, kt, qh, kh


def _attn_specs(is_ret, t, tb, imap_t):
    nb = t // tb
    base = 0 if is_ret else HEADS
    proj = pl.BlockSpec((tb, HEAD_BLOCK), lambda h, i: (imap_t(i), base + h))
    lane_t = pl.BlockSpec((tb, LANES), lambda h, i: (imap_t(i), 0))
    if is_ret:
        aux = [lane_t, lane_t, lane_t, pl.BlockSpec((None, 1, LANES), lambda h, i: (h, 0, 0))]
    else:
        aux = [pl.BlockSpec((tb, LANES), lambda h, i: (imap_t(i), PROJ_P // LANES - 1)),
               pl.BlockSpec((LANES, LANES), lambda h, i: (0, h)),
               pl.BlockSpec((1, LANES), lambda h, i: (0, h))]
    gain = pl.BlockSpec((1, LANES), lambda h, i: (0, h))
    head_t = pl.BlockSpec((tb, LANES), lambda h, i: (imap_t(i), h))
    state = pl.BlockSpec((None, tb // CHUNK, LANES, LANES), lambda h, i: (h, imap_t(i), 0, 0))
    return nb, proj, aux, gain, head_t, state


def _attn_fwd(is_ret, proj, aux_arrays, gain, name):
    t = proj.shape[0]
    tb = min(t, ATTN_TILE)
    nc = tb // CHUNK
    n_aux = 4 if is_ret else 3
    nb, proj_spec, aux_specs, gain_spec, head_t, state_spec = _attn_specs(is_ret, t, tb, lambda i: i)

    def body(*refs):
        proj_ref = refs[0]
        aux = refs[1:1 + n_aux]
        gn_ref, ofin_ref, oraw_ref, st_ref, state = refs[1 + n_aux:]

        @pl.when(pl.program_id(1) == 0)
        def _():
            state[...] = jnp.zeros_like(state)

        q, k, v, gate, b, _, ri, ci = _tile_inputs(is_ret, proj_ref, aux, nc)
        scores = _tile_scores(q, k, b, ri, ci)[0]
        vb = v.astype(BF16)
        intra = _bmm(scores.astype(BF16), vb)
        b_last = b[:, CHUNK - 1:CHUNK, :]
        e_last = jnp.exp(b_last)
        grow = _bmm_tn(vb, (k * jnp.exp(b_last - b)).astype(BF16))
        st = state[...]
        for c in range(nc):
            st_ref[c] = st
            st = st * e_last[c] + grow[c]
        state[...] = st
        inter = _bmm_nt((q * jnp.exp(b)).astype(BF16), st_ref[...].astype(BF16))
        out = (intra + inter).reshape(tb, LANES)
        oraw_ref[...] = out
        normed = out * _rms_scale(out)
        ofin_ref[...] = ((normed * gn_ref[...]) * (gate * jax.nn.sigmoid(gate))).astype(BF16)

    width = HEADS * LANES
    return pl.pallas_call(
        body, name=name, grid=(HEADS, nb),
        in_specs=[proj_spec] + aux_specs + [gain_spec],
        out_specs=[head_t, head_t, state_spec],
        out_shape=[jax.ShapeDtypeStruct((t, width), BF16), jax.ShapeDtypeStruct((t, width), F32),
                   jax.ShapeDtypeStruct((HEADS, t // CHUNK, LANES, LANES), F32)],
        scratch_shapes=[pltpu.VMEM((LANES, LANES), F32)],
        compiler_params=_ARB2,
    )(proj, *aux_arrays, gain)


def _attn_bwd(is_ret, proj, aux_arrays, gain, o_raw, states, d_out, name):
    t = proj.shape[0]
    tb = min(t, ATTN_TILE)
    nc = tb // CHUNK
    n_aux = 4 if is_ret else 3
    nblk = t // tb
    nb, proj_spec, aux_specs, gain_spec, head_t, state_spec = _attn_specs(is_ret, t, tb, lambda i: nblk - 1 - i)
    base = 0 if is_ret else HEADS
    dout_spec = pl.BlockSpec((tb, LANES), lambda h, i: (nblk - 1 - i, base + h))

    def body(*refs):
        proj_ref = refs[0]
        aux = refs[1:1 + n_aux]
        gn_ref, oraw_ref, st_ref, dfin_ref = refs[1 + n_aux:5 + n_aux]
        if is_ret:
            dproj_ref, dgn_ref, dstate, dafter_ref = refs[5 + n_aux:]
        else:
            dproj_ref, dgn_ref, dlogit_ref, dba_ref, dstate, dafter_ref = refs[5 + n_aux:]

        @pl.when(pl.program_id(1) == 0)
        def _():
            dstate[...] = jnp.zeros_like(dstate)
            dgn_ref[...] = jnp.zeros_like(dgn_ref)
            if not is_ret:
                dba_ref[...] = jnp.zeros_like(dba_ref)

        shape3 = (nc, CHUNK, LANES)
        q, k, v, gate, b, logit, ri, ci = _tile_inputs(is_ret, proj_ref, aux, nc)
        scores, ep, en, qt, kt, qh, kh = _tile_scores(q, k, b, ri, ci)
        eb = jnp.exp(b)
        qe = q * eb
        b_last = b[:, CHUNK - 1:CHUNK, :]
        e_last = jnp.exp(b_last)
        ekd = jnp.exp(b_last - b)
        kd = k * ekd

        gn = gn_ref[...]
        out = oraw_ref[...]
        r = _rms_scale(out)
        normed = out * r
        sg = jax.nn.sigmoid(gate)
        dfin = dfin_ref[...]
        dgate = dfin * (normed * gn) * _silu_grad(gate, sg)
        dpre = dfin * (gate * sg)
        dgn_ref[...] += jnp.sum(dpre * normed, axis=0, keepdims=True)
        dnormed = dpre * gn
        d_o = r * (dnormed - normed * jnp.mean(dnormed * normed, axis=-1, keepdims=True))
        dob, vb = d_o.reshape(shape3).astype(BF16), v.astype(BF16)

        dgrow = _bmm_tn(dob, qe.astype(BF16))
        dst = dstate[...]
        for c in reversed(range(nc)):
            dafter_ref[c] = dst
            dst = dst * e_last[c] + dgrow[c]
        dstate[...] = dst
        st = st_ref[...]
        dafter = dafter_ref[...]
        stb, dafter_b = st.astype(BF16), dafter.astype(BF16)

        qtb, ktb, qhb, khb = qt.astype(BF16), kt.astype(BF16), qh.astype(BF16), kh.astype(BF16)
        scores_t = jnp.where(ci >= ri, _bmm_nt(ktb, qtb), _bmm_nt(khb, qhb))
        dv = _bmm(scores_t.astype(BF16), dob) + _bmm_nt(kd.astype(BF16), dafter_b)
        dsc = _bmm_nt(dob, vb)
        dsc_t = _bmm_nt(vb, dob)
        dqe = _bmm(dob, stb)
        dkd = _bmm(vb, dafter_b)
        dqt = _bmm(jnp.where(ci <= ri, dsc, 0.0).astype(BF16), ktb)
        dqh = _bmm(jnp.where(ci <= ri, 0.0, dsc).astype(BF16), khb)
        dkt = _bmm(jnp.where(ci >= ri, dsc_t, 0.0).astype(BF16), qtb)
        dkh = _bmm(jnp.where(ci >= ri, 0.0, dsc_t).astype(BF16), qhb)
        dq = (dqt * ep + dqh * en + dqe * eb).reshape(tb, LANES)
        dk = (dkt * en + dkh * ep + dkd * ekd).reshape(tb, LANES)

        if is_ret:
            cos_ref, sa_ref, sb_ref, _ = aux
            cos, sa, sb = cos_ref[...], sa_ref[...], sb_ref[...]
            dq_raw = _rot_t(dq, cos, sa, sb)
            dk_raw = _rot_t(dk, cos, sa, sb) * QK_SCALE
        else:
            dq_raw = dq * QK_SCALE
            dk_raw = dk
            db = dqt * qt - dkt * kt - dqh * qh + dkh * kh + dqe * qe - dkd * kd
            db_last = (jnp.sum(dkd * kd, axis=1, keepdims=True)
                       + jnp.sum(dafter * st, axis=1, keepdims=True) * e_last)
            last_row = lax.broadcasted_iota(jnp.int32, shape3, 1) == CHUNK - 1
            db = db + jnp.where(last_row, db_last, 0.0)
            dla = _masked_sum((ci >= ri).astype(BF16), db).reshape(tb, LANES)
            dlogit = dla * (1.0 / GATE_NORM) * jax.nn.sigmoid(-logit)
            dlogit_ref[...] = dlogit.astype(BF16)
            dba_ref[...] += jnp.sum(dlogit, axis=0, keepdims=True)

        dproj_ref[:, 0:LANES] = dq_raw.astype(BF16)
        dproj_ref[:, LANES:2 * LANES] = dk_raw.astype(BF16)
        dproj_ref[:, 2 * LANES:3 * LANES] = dv.reshape(tb, LANES).astype(BF16)
        dproj_ref[:, 3 * LANES:4 * LANES] = dgate.astype(BF16)

    width = HEADS * LANES
    row_out = pl.BlockSpec((1, LANES), lambda h, i: (0, h))
    out_specs = [pl.BlockSpec((tb, HEAD_BLOCK), lambda h, i: (nblk - 1 - i, h)), row_out]
    out_shape = [jax.ShapeDtypeStruct((t, HEADS * HEAD_BLOCK), BF16), jax.ShapeDtypeStruct((1, width), F32)]
    if not is_ret:
        out_specs += [head_t, row_out]
        out_shape += [jax.ShapeDtypeStruct((t, width), BF16), jax.ShapeDtypeStruct((1, width), F32)]
    return pl.pallas_call(
        body, name=name, grid=(HEADS, nblk),
        in_specs=[proj_spec] + aux_specs + [gain_spec, head_t, state_spec, dout_spec],
        out_specs=out_specs, out_shape=out_shape,
        scratch_shapes=[pltpu.VMEM((LANES, LANES), F32), pltpu.VMEM((nc, LANES, LANES), F32)],
        compiler_params=_ARB2,
    )(proj, *aux_arrays, gain, o_raw, states, d_out)


def _place():
    x, y, c = lax.axis_index("x"), lax.axis_index("y"), lax.axis_index("c")
    chips = [(1 - x, y), (x, 1 - y), (1 - x, 1 - y)]
    return x, y, c, 2 * x + y, chips


def _gather_plan(arrs):
    na = len(arrs)

    def copies(ins, outs, send_sems, recv_sems):
        x, y, c, me, chips = _place()

        def ici(a, j, src_chip, to):
            return pltpu.make_async_remote_copy(
                src_ref=ins[a].at[:, c], dst_ref=outs[a].at[src_chip, :, c],
                send_sem=send_sems.at[6 * a + j], recv_sem=recv_sems.at[6 * a + j], device_id=to, device_id_type=MESH)

        def d2d(a, j, src_chip, half):
            blk = outs[a].at[src_chip, :, half]
            return pltpu.make_async_remote_copy(
                src_ref=blk, dst_ref=blk, send_sem=send_sems.at[6 * a + 3 + j], recv_sem=recv_sems.at[6 * a + 3 + j],
                device_id=(x, y, 1 - c), device_id_type=MESH)

        peers = [(a, j, px, py) for a in range(na) for j, (px, py) in enumerate(chips)]
        return c, me, peers, ici, d2d

    def start(*refs):
        c, me, peers, ici, _ = copies(*refs)
        for a, j, px, py in peers:
            ici(a, j, me, (px, py, c)).start()

    def finish(*refs):
        c, me, peers, ici, d2d = copies(*refs)
        for a, j, px, py in peers:
            ici(a, j, 2 * px + py, (px, py, c)).wait_recv()
            d2d(a, j, 2 * px + py, c).start()
        for a, j, px, py in peers:
            d2d(a, j, 2 * px + py, 1 - c).wait_recv()
        for a, j, px, py in peers:
            ici(a, j, me, (px, py, c)).wait_send()
            d2d(a, j, 2 * px + py, c).wait_send()

    return _Hosted(arrs, [jax.ShapeDtypeStruct((N_CHIPS,) + a.shape, a.dtype) for a in arrs], 6 * na, start, finish)


def _pair_exchange(grads, name):
    na = len(grads)

    def body(*refs):
        ins, outs = refs[:na], refs[na:2 * na]
        send_sems, recv_sems = refs[2 * na:]
        x, y, c, _, _ = _place()
        copies = [pltpu.make_async_remote_copy(
            src_ref=ins[a].at[:, 1 - c], dst_ref=outs[a], send_sem=send_sems.at[a], recv_sem=recv_sems.at[a],
            device_id=(x, y, 1 - c), device_id_type=MESH) for a in range(na)]
        for cp in copies:
            cp.start()
        for cp in copies:
            cp.wait()

    return pl.pallas_call(
        body, name=name,
        in_specs=[ANY] * na, out_specs=[ANY] * na,
        out_shape=[jax.ShapeDtypeStruct(g.shape[:1] + g.shape[2:], g.dtype) for g in grads],
        scratch_shapes=[pltpu.SemaphoreType.DMA((na,)), pltpu.SemaphoreType.DMA((na,))],
    )(*grads)


def _pair_add(grad, recv, c_arr, name):
    _, _, r, cols = grad.shape

    def body(c_ref, g_ref, r_ref, o_ref):
        o_ref[...] = (g_ref[...].astype(F32) + r_ref[...].astype(F32)).astype(BF16)

    return pl.pallas_call(
        body, name=name,
        grid_spec=pltpu.PrefetchScalarGridSpec(
            num_scalar_prefetch=1, grid=(N_CHIPS,),
            in_specs=[pl.BlockSpec((None, None, r, cols), lambda p, c_ref: (p, c_ref[0], 0, 0)),
                      pl.BlockSpec((None, r, cols), lambda p, c_ref: (p, 0, 0))],
            out_specs=pl.BlockSpec((None, r, cols), lambda p, c_ref: (p, 0, 0))),
        out_shape=jax.ShapeDtypeStruct((N_CHIPS, r, cols), BF16),
        compiler_params=_ARB1,
    )(c_arr, grad, recv)


def _chip_exchange_plan(sums):
    na = len(sums)

    def copies(ins, outs, send_sems, recv_sems):
        x, y, c, me, chips = _place()

        def copy(a, j, px, py, block, slot):
            return pltpu.make_async_remote_copy(
                src_ref=ins[a].at[block], dst_ref=outs[a].at[slot],
                send_sem=send_sems.at[3 * a + j], recv_sem=recv_sems.at[3 * a + j],
                device_id=(px, py, c), device_id_type=MESH)

        peers = [(a, j, px, py) for a in range(na) for j, (px, py) in enumerate(chips)]
        return me, peers, copy

    def start(*refs):
        me, peers, copy = copies(*refs)
        for a, j, px, py in peers:
            copy(a, j, px, py, 2 * px + py, me).start()

    def finish(*refs):
        me, peers, copy = copies(*refs)
        for a, j, px, py in peers:
            copy(a, j, px, py, me, 2 * px + py).wait_recv()
        for a, j, px, py in peers:
            copy(a, j, px, py, 2 * px + py, me).wait_send()

    return _Hosted(sums, [jax.ShapeDtypeStruct(s.shape, s.dtype) for s in sums], 3 * na, start, finish)


def _chip_sum(own, recv, me_arr, name):
    _, r, cols = recv.shape

    def body(me_ref, own_ref, r_ref, o_ref):
        o_ref[...] = jnp.zeros_like(o_ref)
        for q in range(N_CHIPS):
            @pl.when(me_ref[0] == q)
            def _():
                o_ref[...] += own_ref[...].astype(F32)

            @pl.when(me_ref[0] != q)
            def _():
                o_ref[...] += r_ref[q].astype(F32)

    return pl.pallas_call(
        body, name=name,
        grid_spec=pltpu.PrefetchScalarGridSpec(
            num_scalar_prefetch=1, grid=(1,),
            in_specs=[pl.BlockSpec((None, r, cols), lambda i, me_ref: (me_ref[0], 0, 0)),
                      pl.BlockSpec((N_CHIPS, r, cols), lambda i, me_ref: (0, 0, 0))],
            out_specs=pl.BlockSpec((r, cols), lambda i, me_ref: (0, 0))),
        out_shape=jax.ShapeDtypeStruct((r, cols), F32),
        compiler_params=_ARB1,
    )(me_arr, own, recv)


def _pair_share(halves):
    na = len(halves)

    def body(*refs):
        ins, outs = refs[:na], refs[na:2 * na]
        send_sems, recv_sems = refs[2 * na:]
        x, y, c, _, _ = _place()
        copies = [pltpu.make_async_remote_copy(
            src_ref=ins[a], dst_ref=outs[a], send_sem=send_sems.at[a], recv_sem=recv_sems.at[a],
            device_id=(x, y, 1 - c), device_id_type=MESH) for a in range(na)]
        for cp in copies:
            cp.start()
        for cp in copies:
            cp.wait()

    return pl.pallas_call(
        body, name="pair_share",
        in_specs=[ANY] * na, out_specs=[ANY] * na,
        out_shape=[jax.ShapeDtypeStruct(h.shape, h.dtype) for h in halves],
        scratch_shapes=[pltpu.SemaphoreType.DMA((na,)), pltpu.SemaphoreType.DMA((na,))],
    )(*halves)


def _small_allreduce(block):
    m, n = block.shape

    def body(x_ref, all_ref, sum_ref, send_sems, recv_sems, local_sem):
        x, y, c, _, chips = _place()
        me, sibling = (x, y, c), (x, y, 1 - c)

        def rows(px, py, pc):
            return all_ref.at[pl.ds((4 * px + 2 * py + pc) * m, m), :]

        def copy(k, blk, to, src=None):
            return pltpu.make_async_remote_copy(
                src_ref=rows(*blk) if src is None else src, dst_ref=rows(*blk),
                send_sem=send_sems.at[k], recv_sem=recv_sems.at[k], device_id=to, device_id_type=MESH)

        mine = pltpu.make_async_copy(x_ref, rows(*me), local_sem)
        mine.start()
        first = [copy(0, me, sibling, src=x_ref)]
        first += [copy(1 + j, me, (*chip, c), src=x_ref) for j, chip in enumerate(chips)]
        for cp in first:
            cp.start()
        passed = [copy(4 + j, (*chip, c), sibling) for j, chip in enumerate(chips)]
        for j, chip in enumerate(chips):
            copy(1 + j, (*chip, c), me).wait_recv()
            passed[j].start()
        copy(0, sibling, me).wait_recv()
        for j, chip in enumerate(chips):
            copy(4 + j, (*chip, 1 - c), me).wait_recv()
        for cp in first + passed:
            cp.wait_send()
        mine.wait()
        acc = all_ref[0:m, :]
        for d in range(1, 8):
            acc = acc + all_ref[d * m:(d + 1) * m, :]
        sum_ref[...] = acc

    vmem = pl.BlockSpec(memory_space=pltpu.VMEM)
    return pl.pallas_call(
        body, name="small_allreduce",
        in_specs=[vmem], out_specs=[vmem, vmem],
        out_shape=[jax.ShapeDtypeStruct((8 * m, n), F32), jax.ShapeDtypeStruct((m, n), F32)],
        scratch_shapes=[pltpu.SemaphoreType.DMA((7,)), pltpu.SemaphoreType.DMA((7,)), pltpu.SemaphoreType.DMA],
    )(block)[1]


def _row_tile(rows):
    best = rows
    for cand in range(8, min(rows, 512) + 1, 8):
        if rows % cand == 0:
            best = cand
    return best


def _adamw_math(w, g, m, v):
    m2 = ADAM_B1 * m + (1.0 - ADAM_B1) * g
    v2 = ADAM_B2 * v + (1.0 - ADAM_B2) * (g * g)
    m_hat = m2 / (1.0 - ADAM_B1 ** ADAM_STEP)
    v_hat = v2 / (1.0 - ADAM_B2 ** ADAM_STEP)
    return -ADAM_LR * (m_hat / (jnp.sqrt(v_hat) + ADAM_EPS) + ADAM_WD * w), m2, v2


def _adamw_halves(w, g_mine, g_other, m, v, c_arr, name):
    rows, cols = w.shape
    r = rows // 2
    tr = _row_tile(r)
    nt = r // tr

    def body(c_ref, w_ref, gm_ref, go_ref, m_ref, v_ref, g_ref, d_ref, nm_ref, nv_ref):
        gv = jnp.where(pl.program_id(0) == c_ref[0], gm_ref[...], go_ref[...])
        g_ref[...] = gv
        d_ref[...], nm_ref[...], nv_ref[...] = _adamw_math(w_ref[...], gv, m_ref[...], v_ref[...])

    full = pl.BlockSpec((tr, cols), lambda h, i, c_ref: (h * nt + i, 0))
    half = pl.BlockSpec((tr, cols), lambda h, i, c_ref: (i, 0))
    shape = jax.ShapeDtypeStruct((rows, cols), F32)
    return pl.pallas_call(
        body, name=name,
        grid_spec=pltpu.PrefetchScalarGridSpec(
            num_scalar_prefetch=1, grid=(2, nt),
            in_specs=[full, half, half, full, full], out_specs=[full] * 4),
        out_shape=[shape] * 4,
        compiler_params=_ARB2,
    )(c_arr, w, g_mine, g_other, m, v)


def _adamw(w, g, m, v, name):
    rows, cols = w.shape
    tr = _row_tile(rows)

    def body(w_ref, g_ref, m_ref, v_ref, d_ref, nm_ref, nv_ref):
        d_ref[...], nm_ref[...], nv_ref[...] = _adamw_math(w_ref[...], g_ref[...], m_ref[...], v_ref[...])

    spec = pl.BlockSpec((tr, cols), lambda i: (i, 0))
    shape = jax.ShapeDtypeStruct((rows, cols), F32)
    return pl.pallas_call(
        body, name=name, grid=(rows // tr,),
        in_specs=[spec] * 4, out_specs=[spec] * 3, out_shape=[shape] * 3,
        compiler_params=_ARB1,
    )(w, g, m, v)


def _in_columns():
    pieces = []
    for group in range(2):
        q0, k0, v0, g0 = (0, 256, 512, 1024) if group == 0 else (1536, 1792, 2048, 2560)
        for h in range(HEADS):
            pieces += [(q0 + 64 * h, 64), (k0 + 64 * h, 64), (v0 + 128 * h, 128), (g0 + 128 * h, 128)]
    pieces.append((3072, GATE_RANK))
    return pieces


def _pad_w_in_t(w_in_t):
    parts = []
    for start, width in _in_columns():
        parts.append(w_in_t[start:start + width])
        if width < LANES:
            parts.append(jnp.zeros((LANES - width, w_in_t.shape[1]), w_in_t.dtype))
    return jnp.concatenate(parts, axis=0)


def _unpad_w_in_t(w_pt):
    rows = {}
    offset = 0
    for start, width in _in_columns():
        rows[start] = w_pt[offset:offset + width]
        offset += LANES
    return jnp.concatenate([rows[s] for s in sorted(rows)], axis=0)


def _rope_tables(t):
    half = 32
    inv = ROPE_BASE ** (-jnp.arange(half, dtype=F32) * 2.0 / 64)
    ang = jnp.arange(t, dtype=F32)[:, None] * inv[None, :]
    cos, sin = jnp.cos(ang), jnp.sin(ang)
    z32, z64 = jnp.zeros((t, 32), F32), jnp.zeros((t, 64), F32)
    return (jnp.concatenate([cos, cos, z64], axis=1),
            jnp.concatenate([-sin, z32, z64], axis=1),
            jnp.concatenate([z32, sin, z64], axis=1))


def _halves(w):
    n, rows, cols = w.shape
    return w.reshape(n, 2, rows // 2, cols)


def _pack_small(n1, nm, n2, nf, nret, ngla, ba, wa2, wa2_cols, extra=None):
    z = lambda k: jnp.zeros((1, k), F32)
    rows = [n1.reshape(1, -1), nm.reshape(1, -1), n2.reshape(1, -1), nf.reshape(1, -1),
            jnp.concatenate([nret.reshape(1, -1), ngla.reshape(1, -1)], axis=1),
            jnp.concatenate([ba.reshape(1, -1), z(D_MODEL - 256)], axis=1),
            jnp.zeros((1, D_MODEL), F32) if extra is None else extra,
            jnp.zeros((1, D_MODEL), F32),
            jnp.concatenate([wa2.reshape(GATE_RANK, wa2_cols), jnp.zeros((GATE_RANK, D_MODEL - wa2_cols), F32)], axis=1),
            jnp.zeros((SMALL_ROWS - 8 - GATE_RANK, D_MODEL), F32)]
    return jnp.concatenate(rows, axis=0)


def _unpack_small(p, wa2_cols):
    return (p[0:1], p[1:2], p[2:3], p[3], p[4:5, 0:512], p[4:5, 512:1024], p[5:6, 0:256],
            p[8:8 + GATE_RANK, 0:wa2_cols].reshape(1, GATE_RANK, wa2_cols))


def _pad_in_rows(w_t):
    return jnp.pad(w_t, ((0, IN_ROWS - IN_SHARD), (0, 0)))


def _forward_backward(xs, target, ffn1_w, rest, ba_p, ffn1_norm_g, mix_norm_g, ret_norm_g, gla_norm_g, ffn2_norm_g,
                      final_norm_g, rest_plan=None, rest_weights=None, early=None, late=None):
    t = xs.shape[0]
    cos_t, sa_t, sb_t = _rope_tables(t)
    log_gamma = jnp.log(1.0 - 2.0 ** (-5.0 - jnp.arange(HEADS, dtype=F32)))
    lg_t = jnp.broadcast_to(log_gamma[:, None, None], (HEADS, 1, LANES))
    ret_aux = [cos_t, sa_t, sb_t, lg_t]

    (x1, a1, u1, h1), gathered = _ffn_fwd(xs, ffn1_norm_g, ffn1_w, "ffn1_fwd", hosted=rest_plan)
    ffn2_w, w_in_pt, w_out_full, wa2_p = rest if rest_plan is None else rest_weights(gathered)
    proj, h_mix = _mixer_in_fwd(x1, mix_norm_g, w_in_pt, "mixer_in_fwd")
    gla_aux = [proj, wa2_p, ba_p]
    o_ret, raw_ret, st_ret = _attn_fwd(True, proj, ret_aux, ret_norm_g, "ret_fwd")
    o_gla, raw_gla, st_gla = _attn_fwd(False, proj, gla_aux, gla_norm_g, "gla_fwd")
    x2 = _mixer_out_fwd(o_ret, o_gla, w_out_full, x1, "mixer_out_fwd")
    (x3, a2, u2, h2), _ = _ffn_fwd(x2, ffn2_norm_g, ffn2_w, "ffn2_fwd")
    loss_blk, dx3, d_final_g = _final_loss(x3, final_norm_g, target, "final_loss")

    (da2, du2, hid2, dob2, dx2, d_ffn2_g), _ = _ffn_bwd(dx3, x2, ffn2_norm_g, a2, u2, ffn2_w, "ffn2_bwd")
    g_gate2 = _matmul_tn(da2, h2, "ffn2_dgate", out_dtype=BF16)
    g_up2 = _matmul_tn(du2, h2, "ffn2_dup", out_dtype=BF16)
    g_down2 = _matmul_tn(hid2, dob2, "ffn2_ddown", out_dtype=BF16)

    d_o = _matmul_nt(dx2, w_out_full, "mixer_out_bwd")
    g_wout_ret = _matmul_tn(o_ret, dx2, "wout_grad_ret", out_dtype=BF16)
    g_wout_gla = _matmul_tn(o_gla, dx2, "wout_grad_gla", out_dtype=BF16)
    dproj_ret, d_ret_g = _attn_bwd(True, proj, ret_aux, ret_norm_g, raw_ret, st_ret, d_o, "ret_bwd")
    dproj_gla, d_gla_g, dlogit, d_ba_p = _attn_bwd(False, proj, gla_aux, gla_norm_g, raw_gla, st_gla, d_o, "gla_bwd")
    d_glow = _matmul_nt(dlogit, wa2_p, "gate_low_bwd", out_dtype=BF16)
    g_wa2_p = _matmul_tn(proj[:, PROJ_P - LANES:], dlogit, "gate_w_grad")
    dproj = jnp.concatenate([dproj_ret, dproj_gla, d_glow], axis=1)
    g_win_p = _matmul_tn(dproj, h_mix, "w_in_grad", tka=PROJ_P // 3, out_dtype=BF16)
    dx1, d_mix_g = _mixer_in_bwd(dproj, w_in_pt, dx2, x1, mix_norm_g, "mixer_in_bwd")
    g_win_t = _unpad_w_in_t(g_win_p[0])
    g_win = jnp.stack([_pad_in_rows(g_win_t[IN_SHARD * p:IN_SHARD * (p + 1)]) for p in range(N_CHIPS)], axis=0)
    g_wout = jnp.concatenate([g_wout_ret[0], g_wout_gla[0]], axis=0).reshape(N_CHIPS, D_MODEL // N_CHIPS, D_MODEL)

    early_plan = None if early is None else early([g_gate2, g_up2, g_down2, g_win, g_wout])
    (da1, du1, hid1, dob1, grad_x, d_ffn1_g), arrived = _ffn_bwd(dx1, xs, ffn1_norm_g, a1, u1, ffn1_w, "ffn1_bwd",
                                                                hosted=early_plan)
    late_grads, late_arrived = [], []
    for lhs, rhs, name in ((da1, h1, "ffn1_dgate"), (du1, h1, "ffn1_dup"), (hid1, dob1, "ffn1_ddown")):
        plan = None if late is None or not late_grads else late(late_grads[-1], len(late_grads))
        res = _matmul_tn(lhs, rhs, name, out_dtype=BF16, hosted=plan)
        if plan is not None:
            res, carried = res
            late_arrived += carried
        late_grads.append(res)
    g_gate1, g_up1, g_down1 = late_grads

    return (loss_blk, grad_x, g_gate1, g_up1, g_down1, g_gate2, g_up2, g_down2, g_win, g_wout, g_wa2_p,
            d_ba_p, d_ffn1_g, d_mix_g, d_ffn2_g, d_final_g, d_ret_g, d_gla_g, arrived, late_arrived)


def kernel(x, ffn1_norm_g, ffn1_w_gate, ffn1_w_up, ffn1_w_down, mix_norm_g, w_in, ret_norm_g, gla_w_a2, gla_b_a, gla_norm_g, w_out, ffn2_norm_g, ffn2_w_gate, ffn2_w_up, ffn2_w_down, final_norm_g, loss_target, m_ffn1_norm_g, m_ffn1_w_gate, m_ffn1_w_up, m_ffn1_w_down, m_mix_norm_g, m_w_in, m_ret_norm_g, m_gla_w_a2, m_gla_b_a, m_gla_norm_g, m_w_out, m_ffn2_norm_g, m_ffn2_w_gate, m_ffn2_w_up, m_ffn2_w_down, m_final_norm_g, v_ffn1_norm_g, v_ffn1_w_gate, v_ffn1_w_up, v_ffn1_w_down, v_mix_norm_g, v_w_in, v_ret_norm_g, v_gla_w_a2, v_gla_b_a, v_gla_norm_g, v_w_out, v_ffn2_norm_g, v_ffn2_w_gate, v_ffn2_w_up, v_ffn2_w_down, v_final_norm_g):
    t = x.shape[1]
    xs = x.reshape(t, D_MODEL)
    target = loss_target.reshape(t, D_MODEL)
    chip = 2 * lax.axis_index("x") + lax.axis_index("y")
    c_arr = lax.axis_index("c").astype(jnp.int32).reshape(1)

    me_arr = chip.astype(jnp.int32).reshape(1)

    pad_rows = _pad_in_rows

    def own_block(gathered, shard):
        return lax.dynamic_update_slice(gathered, shard[None], (chip,) + (0,) * shard.ndim)

    ffn1_shard = _halves(jnp.stack([ffn1_w_gate[0].T, ffn1_w_up[0].T, ffn1_w_down[0]], axis=0).astype(BF16))
    rest_shards = [_halves(jnp.stack([ffn2_w_gate[0].T, ffn2_w_up[0].T, ffn2_w_down[0]], axis=0).astype(BF16)),
                   _halves(pad_rows(w_in[0].T).astype(BF16)[None]),
                   _halves(w_out.astype(BF16)),
                   jnp.concatenate([gla_w_a2.reshape(GATE_RANK, 64), jnp.zeros((GATE_RANK, 64), F32)],
                                   axis=1).reshape(1, 2, 8, LANES)]
    ffn1_all = _run_hosted(_gather_plan([ffn1_shard]), "gather_ffn1")[0]
    ffn1_w = own_block(ffn1_all, ffn1_shard).reshape(N_CHIPS, 3, FF_SHARD, D_MODEL)

    def rest_weights(gathered):
        ffn2_all, win_all, wout_all, wa2_all = [own_block(g, s) for g, s in zip(gathered, rest_shards)]
        win_t = win_all.reshape(N_CHIPS, IN_ROWS, D_MODEL)
        w_in_pt = _pad_w_in_t(jnp.concatenate([win_t[p, 0:IN_SHARD] for p in range(N_CHIPS)], axis=0))
        wa2_p = jnp.pad(
            wa2_all.reshape(N_CHIPS, GATE_RANK, LANES).transpose(1, 0, 2).reshape(GATE_RANK, HEADS * LANES),
            ((0, LANES - GATE_RANK), (0, 0))).astype(BF16)
        return (ffn2_all.reshape(N_CHIPS, 3, FF_SHARD, D_MODEL), w_in_pt, wout_all.reshape(D_MODEL, D_MODEL), wa2_p)

    def pair_sums(grads, tag):
        halves = [g.reshape(g.shape[0], 2, g.shape[1] // 2, g.shape[2]) for g in grads]
        recv = _pair_exchange(halves, "pair_exchange_" + tag)
        return [_pair_add(g, r, c_arr, "pair_add_%s%d" % (tag, k)) for k, (g, r) in enumerate(zip(halves, recv))]

    early_sums = []

    def early(grads):
        early_sums.extend(pair_sums(grads, "early"))
        return _chip_exchange_plan(early_sums)

    late_sums = []

    def late(grad, number):
        late_sums.extend(pair_sums([grad], "late%d" % number))
        return _chip_exchange_plan(late_sums[-1:])

    ba_p = jnp.pad(gla_b_a.reshape(HEADS, 64), ((0, 0), (0, 64))).reshape(1, HEADS * LANES)
    fb = _forward_backward(xs, target, ffn1_w, None, ba_p, ffn1_norm_g, mix_norm_g, ret_norm_g, gla_norm_g,
                           ffn2_norm_g, final_norm_g.reshape(1, D_MODEL), rest_plan=_gather_plan(rest_shards),
                           rest_weights=rest_weights, early=early, late=late)
    (loss_blk, grad_x, _, _, g_down1, _, _, _, _, _, g_wa2_p,
     d_ba_p, d_ffn1_g, d_mix_g, d_ffn2_g, d_final_g, d_ret_g, d_gla_g, early_arrived, late_arrived) = fb
    late_arrived = late_arrived + _run_hosted(late(g_down1, 3), "chip_exchange_late")
    sums, arrived = late_sums + early_sums, late_arrived + early_arrived
    mine = [_chip_sum(s, r, me_arr, "chip_sum_%d" % k) for k, (s, r) in enumerate(zip(sums, arrived))]
    other = _pair_share(mine)

    g_wa2 = g_wa2_p[0][0:GATE_RANK].reshape(GATE_RANK, HEADS, LANES)[:, :, 0:64].reshape(GATE_RANK, 256)
    d_ba = d_ba_p.reshape(HEADS, LANES)[:, 0:64].reshape(1, 256)
    loss_row = jnp.pad(loss_blk[0:1, 0:1], ((0, 0), (0, D_MODEL - 1)))
    small_local = _pack_small(d_ffn1_g, d_mix_g, d_ffn2_g, d_final_g, d_ret_g, d_gla_g, d_ba, g_wa2, 256, loss_row)
    small_sum = _small_allreduce(small_local)
    loss = small_sum[6, 0]
    sg = _unpack_small(small_sum, 256)
    wa2_grad = lax.dynamic_slice(sg[7], (0, 0, 64 * chip), (1, GATE_RANK, 64))
    small_g = _pack_small(*sg[:7], wa2_grad, 64)
    small_w = _pack_small(ffn1_norm_g, mix_norm_g, ffn2_norm_g, final_norm_g, ret_norm_g, gla_norm_g, gla_b_a, gla_w_a2, 64)
    small_m = _pack_small(m_ffn1_norm_g, m_mix_norm_g, m_ffn2_norm_g, m_final_norm_g, m_ret_norm_g, m_gla_norm_g,
                          m_gla_b_a, m_gla_w_a2, 64)
    small_v = _pack_small(v_ffn1_norm_g, v_mix_norm_g, v_ffn2_norm_g, v_final_norm_g, v_ret_norm_g, v_gla_norm_g,
                          v_gla_b_a, v_gla_w_a2, 64)
    small_out = _adamw(small_w, small_g, small_m, small_v, "adamw_small")
    s_grad = _unpack_small(small_g, 64)
    s_delta, s_m, s_v = (_unpack_small(o, 64) for o in small_out)

    def big(k, w, m, v, name, to_2d, from_2d):
        outs4 = _adamw_halves(to_2d(w), mine[k], other[k], to_2d(m), to_2d(v), c_arr, name)
        return [from_2d(z) for z in outs4]

    plain = (lambda w: w[0], lambda z: z[None])
    transposed = (lambda w: w[0].T, lambda z: z.T[None])
    in_proj = (lambda w: pad_rows(w[0].T), lambda z: z[0:IN_SHARD].T[None])
    r_g1 = big(0, ffn1_w_gate, m_ffn1_w_gate, v_ffn1_w_gate, "adamw_ffn1_gate", *transposed)
    r_u1 = big(1, ffn1_w_up, m_ffn1_w_up, v_ffn1_w_up, "adamw_ffn1_up", *transposed)
    r_d1 = big(2, ffn1_w_down, m_ffn1_w_down, v_ffn1_w_down, "adamw_ffn1_down", *plain)
    r_g2 = big(3, ffn2_w_gate, m_ffn2_w_gate, v_ffn2_w_gate, "adamw_ffn2_gate", *transposed)
    r_u2 = big(4, ffn2_w_up, m_ffn2_w_up, v_ffn2_w_up, "adamw_ffn2_up", *transposed)
    r_d2 = big(5, ffn2_w_down, m_ffn2_w_down, v_ffn2_w_down, "adamw_ffn2_down", *plain)
    r_in = big(6, w_in, m_w_in, v_w_in, "adamw_w_in", *in_proj)
    r_out = big(7, w_out, m_w_out, v_w_out, "adamw_w_out", *plain)

    def leaves(k, smalls):
        n1, nm, n2, nf, nret, ngla, ba, wa2 = smalls
        return [n1, r_g1[k], r_u1[k], r_d1[k], nm, r_in[k], nret, wa2, ba, ngla, r_out[k], n2, r_g2[k], r_u2[k], r_d2[k], nf]

    outs = [loss, grad_x.reshape(x.shape)]
    outs += leaves(0, s_grad) + leaves(1, s_delta) + leaves(2, s_m) + leaves(3, s_v)
    return tuple(outs)
```

```python
import functools

import jax
import jax.numpy as jnp
from jax import lax
from jax.experimental import pallas as pl
from jax.experimental.pallas import tpu as pltpu

F32, BF16 = jnp.float32, jnp.bfloat16
MESH = pl.DeviceIdType.MESH
ANY = pl.BlockSpec(memory_space=pl.ANY)

D_MODEL = 1024
D_FF = 2816
N_CHIPS = 4
FF_SHARD = D_FF // N_CHIPS
IN_WIDTH = 3088
IN_SHARD = IN_WIDTH // N_CHIPS
IN_ROWS = 800
CHUNK = 64
HEADS = 4
LANES = 128
HEAD_BLOCK = 4 * LANES
PROJ_P = 2 * HEADS * HEAD_BLOCK + LANES
GATE_RANK = 16
QK_SCALE = 0.125
GATE_NORM = 16.0
RMS_EPS = 1e-6
ROPE_BASE = 10000.0
ADAM_LR, ADAM_B1, ADAM_B2, ADAM_EPS, ADAM_WD, ADAM_STEP = 0.001, 0.9, 0.999, 1e-08, 0.01, 10
SMALL_ROWS = 32
TOKEN_TILE = 512
ATTN_TILE = 512

_ARB2 = pltpu.CompilerParams(dimension_semantics=("arbitrary", "arbitrary"))
_ARB1 = pltpu.CompilerParams(dimension_semantics=("arbitrary",))
_ARB3 = pltpu.CompilerParams(dimension_semantics=("arbitrary", "arbitrary", "arbitrary"))


def _dot(a, b):
    return jnp.dot(a, b, preferred_element_type=F32)


def _dot_nt(a, b):
    return lax.dot_general(a, b, (((1,), (1,)), ((), ())), preferred_element_type=F32)


def _dot_tn(a, b):
    return lax.dot_general(a, b, (((0,), (0,)), ((), ())), preferred_element_type=F32)


def _rms_scale(xv):
    return lax.rsqrt(jnp.mean(xv * xv, axis=-1, keepdims=True) + RMS_EPS)


def _rms_bwd(dh, xv, g):
    r = _rms_scale(xv)
    xhat = xv * r
    dxhat = dh * g
    dx = r * (dxhat - xhat * jnp.mean(dxhat * xhat, axis=-1, keepdims=True))
    return dx, jnp.sum(dh * xhat, axis=0, keepdims=True)


def _silu_grad(a, sg):
    return sg * (1.0 + a * (1.0 - sg))


class _Hosted:
    def __init__(self, arrays, out_shapes, n_sems, start, finish):
        self.arrays, self.out_shapes, self.n_sems = list(arrays), list(out_shapes), n_sems
        self.start, self.finish = start, finish


def _call(body, args, *, name, grid, in_specs, out_specs, out_shape, scratch_shapes, compiler_params, hosted=None):
    if hosted is None:
        outs = pl.pallas_call(body, name=name, grid=grid, in_specs=in_specs, out_specs=out_specs, out_shape=out_shape,
                              scratch_shapes=scratch_shapes, compiler_params=compiler_params)(*args)
        return list(outs), []
    n_in, n_out, n_sc, nh = len(in_specs), len(out_specs), len(scratch_shapes), len(hosted.arrays)

    def wrapped(*refs):
        ins, h_in = refs[:n_in], refs[n_in:n_in + nh]
        outs, h_out = refs[n_in + nh:n_in + nh + n_out], refs[n_in + nh + n_out:n_in + 2 * nh + n_out]
        rest = refs[n_in + 2 * nh + n_out:]
        scratch, (send_sems, recv_sems) = rest[:n_sc], rest[n_sc:]
        first = functools.reduce(jnp.logical_and, [pl.program_id(d) == 0 for d in range(len(grid))])
        last = functools.reduce(jnp.logical_and, [pl.program_id(d) == n - 1 for d, n in enumerate(grid)])

        @pl.when(first)
        def _():
            hosted.start(h_in, h_out, send_sems, recv_sems)

        body(*ins, *outs, *scratch)

        @pl.when(last)
        def _():
            hosted.finish(h_in, h_out, send_sems, recv_sems)

    sems = [pltpu.SemaphoreType.DMA((hosted.n_sems,)), pltpu.SemaphoreType.DMA((hosted.n_sems,))]
    outs = pl.pallas_call(
        wrapped, name=name, grid=grid, in_specs=list(in_specs) + [ANY] * nh, out_specs=list(out_specs) + [ANY] * nh,
        out_shape=list(out_shape) + hosted.out_shapes, scratch_shapes=list(scratch_shapes) + sems,
        compiler_params=compiler_params)(*args, *hosted.arrays)
    return list(outs[:n_out]), list(outs[n_out:])


def _run_hosted(hosted, name):
    nh = len(hosted.arrays)

    def body(*refs):
        h_in, h_out, (send_sems, recv_sems) = refs[:nh], refs[nh:2 * nh], refs[2 * nh:]
        hosted.start(h_in, h_out, send_sems, recv_sems)
        hosted.finish(h_in, h_out, send_sems, recv_sems)

    sems = [pltpu.SemaphoreType.DMA((hosted.n_sems,)), pltpu.SemaphoreType.DMA((hosted.n_sems,))]
    return list(pl.pallas_call(body, name=name, in_specs=[ANY] * nh, out_specs=[ANY] * nh,
                               out_shape=hosted.out_shapes, scratch_shapes=sems)(*hosted.arrays))


def _ffn_weight_operands(ffn_w, chunk_maps):
    if isinstance(ffn_w, (list, tuple)):
        specs = [pl.BlockSpec((None, FF_SHARD, D_MODEL), lambda *g, m=m: (m(*g), 0, 0)) for m in chunk_maps]
        return list(ffn_w), specs
    specs = [pl.BlockSpec((None, None, FF_SHARD, D_MODEL), lambda *g, m=m, k=kind: (m(*g), k, 0, 0))
             for kind, m in enumerate(chunk_maps)]
    return [ffn_w] * 3, specs


def _pipeline_items(steps):
    def cur(s):
        c = jnp.minimum(s, steps - 1)
        return c // N_CHIPS, c % N_CHIPS

    def prev(s):
        p = jnp.maximum(s - 1, 0)
        return p // N_CHIPS, p % N_CHIPS

    return cur, prev


def _ffn_fwd(x, g, ffn_w, name, hosted=None):
    t = x.shape[0]
    tm = min(t, TOKEN_TILE)

    def body(x_ref, g_ref, wg_ref, wu_ref, wd_ref, xo_ref, a_ref, u_ref, h_ref, acc_ref):
        j = pl.program_id(1)

        @pl.when(j == 0)
        def _():
            xv = x_ref[...]
            h_ref[...] = ((xv * _rms_scale(xv)) * g_ref[...]).astype(BF16)
            acc_ref[...] = jnp.zeros_like(acc_ref)

        h = h_ref[...]
        a = _dot_nt(h, wg_ref[...])
        u = _dot_nt(h, wu_ref[...])
        a_ref[...] = a.astype(BF16)
        u_ref[...] = u.astype(BF16)
        hid = (a * jax.nn.sigmoid(a)) * u
        acc_ref[...] += _dot(hid.astype(BF16), wd_ref[...])

        @pl.when(j == N_CHIPS - 1)
        def _():
            xo_ref[...] = x_ref[...] + 0.5 * acc_ref[...]

    tok = pl.BlockSpec((tm, D_MODEL), lambda i, j: (i, 0))
    act = pl.BlockSpec((None, tm, FF_SHARD), lambda i, j: (j, i, 0))
    w_arrays, weights = _ffn_weight_operands(ffn_w, [lambda i, j: j] * 3)
    return _call(
        body, (x, g, *w_arrays), name=name, grid=(t // tm, N_CHIPS),
        in_specs=[tok, pl.BlockSpec((1, D_MODEL), lambda i, j: (0, 0))] + weights,
        out_specs=[tok, act, act, tok],
        out_shape=[jax.ShapeDtypeStruct((t, D_MODEL), F32),
                   jax.ShapeDtypeStruct((N_CHIPS, t, FF_SHARD), BF16),
                   jax.ShapeDtypeStruct((N_CHIPS, t, FF_SHARD), BF16),
                   jax.ShapeDtypeStruct((t, D_MODEL), BF16)],
        scratch_shapes=[pltpu.VMEM((tm, D_MODEL), F32)],
        compiler_params=_ARB2, hosted=hosted)


def _ffn_bwd(dxo, x, g, a4, u4, ffn_w, name, hosted=None):
    t = x.shape[0]
    tm = min(t, TOKEN_TILE)
    steps = (t // tm) * N_CHIPS
    cur, prev = _pipeline_items(steps)


    def body(dxo_ref, dxo_prev_ref, x_ref, g_ref, a_ref, u_ref, wg_ref, wu_ref, wd_ref,
             da_ref, du_ref, hid_ref, dob_ref, dx_ref, dg_ref, acc_ref, da_slots, du_slots):
        s = pl.program_id(0)
        jc, jp = cur(s)[1], prev(s)[1]
        slot = s % 2

        @pl.when(s == 0)
        def _():
            dg_ref[...] = jnp.zeros_like(dg_ref)
            acc_ref[...] = jnp.zeros_like(acc_ref)
            da_slots[...] = jnp.zeros_like(da_slots)
            du_slots[...] = jnp.zeros_like(du_slots)

        @pl.when(jc == 0)
        def _():
            dob_ref[...] = (0.5 * dxo_ref[...]).astype(BF16)

        dhid = _dot_nt(dob_ref[...], wd_ref[...])
        a = a_ref[...].astype(F32)
        u = u_ref[...].astype(F32)
        sg = jax.nn.sigmoid(a)
        sl = a * sg
        hid_ref[...] = (sl * u).astype(BF16)
        du = (dhid * sl).astype(BF16)
        da = (dhid * u * _silu_grad(a, sg)).astype(BF16)
        du_ref[...] = du
        da_ref[...] = da
        acc_ref[...] += _dot(da_slots[1 - slot], wg_ref[...]) + _dot(du_slots[1 - slot], wu_ref[...])
        da_slots[slot] = da
        du_slots[slot] = du

        @pl.when((jp == N_CHIPS - 1) & (s > 0))
        def _():
            dx, dg = _rms_bwd(acc_ref[...], x_ref[...], g_ref[...])
            dx_ref[...] = dxo_prev_ref[...] + dx
            dg_ref[...] += dg
            acc_ref[...] = jnp.zeros_like(acc_ref)

    tok_cur = pl.BlockSpec((tm, D_MODEL), lambda s: (cur(s)[0], 0))
    tok_prev = pl.BlockSpec((tm, D_MODEL), lambda s: (prev(s)[0], 0))
    act = pl.BlockSpec((None, tm, FF_SHARD), lambda s: (cur(s)[1], cur(s)[0], 0))
    row = pl.BlockSpec((1, D_MODEL), lambda s: (0, 0))
    w_arrays, weights = _ffn_weight_operands(ffn_w, [lambda s: prev(s)[1], lambda s: prev(s)[1], lambda s: cur(s)[1]])
    act_shape = jax.ShapeDtypeStruct((N_CHIPS, t, FF_SHARD), BF16)
    return _call(
        body, (dxo, dxo, x, g, a4, u4, *w_arrays), name=name, grid=(steps + 1,),
        in_specs=[tok_cur, tok_prev, tok_prev, row, act, act] + weights,
        out_specs=[act, act, act, tok_cur, tok_prev, row],
        out_shape=[act_shape, act_shape, act_shape,
                   jax.ShapeDtypeStruct((t, D_MODEL), BF16),
                   jax.ShapeDtypeStruct((t, D_MODEL), F32),
                   jax.ShapeDtypeStruct((1, D_MODEL), F32)],
        scratch_shapes=[pltpu.VMEM((tm, D_MODEL), F32), pltpu.VMEM((2, tm, FF_SHARD), BF16),
                        pltpu.VMEM((2, tm, FF_SHARD), BF16)],
        compiler_params=_ARB1, hosted=hosted)


def _matmul_tn(a, b, name, tka=None, out_dtype=F32, hosted=None):
    a3, b3 = a.ndim == 3, b.ndim == 3
    nb = a.shape[0] if a3 else (b.shape[0] if b3 else 1)
    t, ka, n = a.shape[-2], a.shape[-1], b.shape[-1]
    tka = ka if tka is None else tka
    tk = min(t, 2 * TOKEN_TILE)
    nk = t // tk

    def body(a_ref, b_ref, o_ref, acc_ref):
        k = pl.program_id(2)

        @pl.when(k == 0)
        def _():
            acc_ref[...] = jnp.zeros_like(acc_ref)

        acc_ref[...] += _dot_tn(a_ref[...].astype(BF16), b_ref[...].astype(BF16))

        @pl.when(k == nk - 1)
        def _():
            o_ref[...] = acc_ref[...].astype(out_dtype)

    a_spec = (pl.BlockSpec((None, tk, tka), lambda i, j, k: (i, k, j)) if a3
              else pl.BlockSpec((tk, tka), lambda i, j, k: (k, j)))
    b_spec = (pl.BlockSpec((None, tk, n), lambda i, j, k: (i, k, 0)) if b3
              else pl.BlockSpec((tk, n), lambda i, j, k: (k, 0)))
    outs, carried = _call(
        body, (a, b), name=name, grid=(nb, ka // tka, t // tk),
        in_specs=[a_spec, b_spec],
        out_specs=[pl.BlockSpec((None, tka, n), lambda i, j, k: (i, j, 0))],
        out_shape=[jax.ShapeDtypeStruct((nb, ka, n), out_dtype)],
        scratch_shapes=[pltpu.VMEM((tka, n), F32)],
        compiler_params=_ARB3, hosted=hosted)
    return outs[0] if hosted is None else (outs[0], carried)


def _matmul_nt(a, w, name, out_dtype=F32):
    t, k = a.shape
    n = w.shape[0]
    tm = min(t, TOKEN_TILE)

    def body(a_ref, w_ref, o_ref):
        o_ref[...] = _dot_nt(a_ref[...].astype(BF16), w_ref[...]).astype(out_dtype)

    return pl.pallas_call(
        body, name=name, grid=(t // tm,),
        in_specs=[pl.BlockSpec((tm, k), lambda i: (i, 0)), pl.BlockSpec((n, k), lambda i: (0, 0))],
        out_specs=pl.BlockSpec((tm, n), lambda i: (i, 0)),
        out_shape=jax.ShapeDtypeStruct((t, n), out_dtype),
        compiler_params=_ARB1,
    )(a, w)


def _mixer_in_bwd(dproj, w_in_pt, dres, x, g, name):
    t, k = dproj.shape
    tm = min(t, TOKEN_TILE)

    def body(a_ref, w_ref, dres_ref, x_ref, g_ref, dx_ref, dg_ref):
        @pl.when(pl.program_id(0) == 0)
        def _():
            dg_ref[...] = jnp.zeros_like(dg_ref)

        dh = _dot(a_ref[...], w_ref[...])
        dx, dg = _rms_bwd(dh, x_ref[...], g_ref[...])
        dx_ref[...] = dres_ref[...] + dx
        dg_ref[...] += dg

    tok = pl.BlockSpec((tm, D_MODEL), lambda i: (i, 0))
    row = pl.BlockSpec((1, D_MODEL), lambda i: (0, 0))
    return pl.pallas_call(
        body, name=name, grid=(t // tm,),
        in_specs=[pl.BlockSpec((tm, k), lambda i: (i, 0)), pl.BlockSpec((k, D_MODEL), lambda i: (0, 0)), tok, tok, row],
        out_specs=[tok, row],
        out_shape=[jax.ShapeDtypeStruct((t, D_MODEL), F32), jax.ShapeDtypeStruct((1, D_MODEL), F32)],
        compiler_params=_ARB1,
    )(dproj, w_in_pt, dres, x, g)


def _mixer_in_fwd(x, g, w_in_pt, name, hosted=None):
    t = x.shape[0]
    tm = min(t, TOKEN_TILE)
    tn = PROJ_P // 3

    def body(x_ref, g_ref, w_ref, p_ref, h_ref):
        @pl.when(pl.program_id(1) == 0)
        def _():
            xv = x_ref[...]
            h_ref[...] = ((xv * _rms_scale(xv)) * g_ref[...]).astype(BF16)

        p_ref[...] = _dot_nt(h_ref[...], w_ref[...])

    tok = pl.BlockSpec((tm, D_MODEL), lambda i, j: (i, 0))
    return _call(
        body, (x, g, w_in_pt), name=name, grid=(t // tm, 3),
        in_specs=[tok, pl.BlockSpec((1, D_MODEL), lambda i, j: (0, 0)),
                  pl.BlockSpec((tn, D_MODEL), lambda i, j: (j, 0))],
        out_specs=[pl.BlockSpec((tm, tn), lambda i, j: (i, j)), tok],
        out_shape=[jax.ShapeDtypeStruct((t, PROJ_P), F32), jax.ShapeDtypeStruct((t, D_MODEL), BF16)],
        scratch_shapes=[], compiler_params=_ARB2, hosted=hosted)


def _mixer_out_fwd(o_ret, o_gla, w_out, x, name):
    t = x.shape[0]
    tm = min(t, TOKEN_TILE)
    half = HEADS * LANES

    def body(a_ref, b_ref, w_ref, x_ref, o_ref):
        o_ref[...] = x_ref[...] + _dot(a_ref[...], w_ref[0:half, :]) + _dot(b_ref[...], w_ref[half:2 * half, :])

    tok = pl.BlockSpec((tm, D_MODEL), lambda i: (i, 0))
    hb = pl.BlockSpec((tm, half), lambda i: (i, 0))
    return pl.pallas_call(
        body, name=name, grid=(t // tm,),
        in_specs=[hb, hb, pl.BlockSpec((2 * half, D_MODEL), lambda i: (0, 0)), tok],
        out_specs=tok, out_shape=jax.ShapeDtypeStruct((t, D_MODEL), F32),
        compiler_params=_ARB1,
    )(o_ret, o_gla, w_out, x)


def _final_loss(x, g, target, name):
    t = x.shape[0]
    tm = min(t, TOKEN_TILE)

    def body(x_ref, g_ref, t_ref, l_ref, dx_ref, dg_ref):
        @pl.when(pl.program_id(0) == 0)
        def _():
            l_ref[...] = jnp.zeros_like(l_ref)
            dg_ref[...] = jnp.zeros_like(dg_ref)

        xv = x_ref[...]
        gv = g_ref[...]
        err = (xv * _rms_scale(xv)) * gv - t_ref[...]
        l_ref[...] += 0.5 * jnp.sum(jnp.mean(err * err, axis=-1, keepdims=True), axis=0, keepdims=True)
        dx, dg = _rms_bwd(err * (1.0 / D_MODEL), xv, gv)
        dx_ref[...] = dx
        dg_ref[...] += dg

    tok = pl.BlockSpec((tm, D_MODEL), lambda i: (i, 0))
    row = pl.BlockSpec((1, D_MODEL), lambda i: (0, 0))
    return pl.pallas_call(
        body, name=name, grid=(t // tm,),
        in_specs=[tok, row, tok],
        out_specs=[pl.BlockSpec((8, LANES), lambda i: (0, 0)), tok, row],
        out_shape=[jax.ShapeDtypeStruct((8, LANES), F32), jax.ShapeDtypeStruct((t, D_MODEL), F32),
                   jax.ShapeDtypeStruct((1, D_MODEL), F32)],
        compiler_params=_ARB1,
    )(x, g, target)


def _rot(v, cos, sa, sb):
    return v * cos + pltpu.roll(v, 96, 1) * sa + pltpu.roll(v, 32, 1) * sb


def _rot_t(d, cos, sa, sb):
    return d * cos + pltpu.roll(d * sa, 32, 1) + pltpu.roll(d * sb, 96, 1)


def _bmm(a, b):
    return jnp.einsum("cik,ckj->cij", a, b, preferred_element_type=F32)


def _bmm_nt(a, b):
    return jnp.einsum("cik,cjk->cij", a, b, preferred_element_type=F32)


def _bmm_tn(a, b):
    return jnp.einsum("cki,ckj->cij", a, b, preferred_element_type=F32)


def _masked_sum(mask, x):
    hi = x.astype(BF16)
    r1 = x - hi.astype(F32)
    mid = r1.astype(BF16)
    lo = (r1 - mid.astype(F32)).astype(BF16)
    return _bmm(mask, hi) + _bmm(mask, mid) + _bmm(mask, lo)


def _tile_inputs(is_ret, proj_ref, aux, nc):
    shape3 = (nc, CHUNK, LANES)
    q_raw = proj_ref[:, 0:LANES]
    k_raw = proj_ref[:, LANES:2 * LANES]
    v = proj_ref[:, 2 * LANES:3 * LANES]
    gate = proj_ref[:, 3 * LANES:4 * LANES]
    ri = lax.broadcasted_iota(jnp.int32, (nc, CHUNK, CHUNK), 1)
    ci = lax.broadcasted_iota(jnp.int32, (nc, CHUNK, CHUNK), 2)
    if is_ret:
        cos_ref, sa_ref, sb_ref, lg_ref = aux
        cos, sa, sb = cos_ref[...], sa_ref[...], sb_ref[...]
        q = _rot(q_raw, cos, sa, sb)
        k = _rot(k_raw, cos, sa, sb) * QK_SCALE
        steps = (lax.broadcasted_iota(jnp.int32, shape3, 1) + 1).astype(F32)
        b = steps * lg_ref[...]
        logit = None
    else:
        glow_ref, wa2_ref, ba_ref = aux
        logit = _dot(glow_ref[...].astype(BF16), wa2_ref[...]) + ba_ref[...]
        la = (jnp.minimum(logit, 0.0) - jnp.log1p(jnp.exp(-jnp.abs(logit)))) * (1.0 / GATE_NORM)
        b = _masked_sum((ci <= ri).astype(BF16), la.reshape(shape3))
        q = q_raw * QK_SCALE
        k = k_raw
    return q.reshape(shape3), k.reshape(shape3), v.reshape(shape3), gate, b, logit, ri, ci


def _tile_scores(q, k, b, ri, ci):
    mid = b[:, CHUNK // 2 - 1:CHUNK // 2, :]
    ep = jnp.exp(b - mid)
    en = jnp.exp(mid - b)
    qt, kt, qh, kh = q * ep, k * en, q * en, k * ep
    low = _bmm_nt(qt.astype(BF16), kt.astype(BF16))
    upp = _bmm_nt(qh.astype(BF16), kh.astype(BF16))
    scores = jnp.where(ci <= ri, low, upp)
    return scores, ep, en, qt, kt, qh, kh


def _attn_specs(is_ret, t, tb, imap_t):
    nb = t // tb
    base = 0 if is_ret else HEADS
    proj = pl.BlockSpec((tb, HEAD_BLOCK), lambda h, i: (imap_t(i), base + h))
    lane_t = pl.BlockSpec((tb, LANES), lambda h, i: (imap_t(i), 0))
    if is_ret:
        aux = [lane_t, lane_t, lane_t, pl.BlockSpec((None, 1, LANES), lambda h, i: (h, 0, 0))]
    else:
        aux = [pl.BlockSpec((tb, LANES), lambda h, i: (imap_t(i), PROJ_P // LANES - 1)),
               pl.BlockSpec((LANES, LANES), lambda h, i: (0, h)),
               pl.BlockSpec((1, LANES), lambda h, i: (0, h))]
    gain = pl.BlockSpec((1, LANES), lambda h, i: (0, h))
    head_t = pl.BlockSpec((tb, LANES), lambda h, i: (imap_t(i), h))
    state = pl.BlockSpec((None, tb // CHUNK, LANES, LANES), lambda h, i: (h, imap_t(i), 0, 0))
    return nb, proj, aux, gain, head_t, state


def _attn_fwd(is_ret, proj, aux_arrays, gain, name, hosted=None):
    t = proj.shape[0]
    tb = min(t, ATTN_TILE)
    nc = tb // CHUNK
    n_aux = 4 if is_ret else 3
    nb, proj_spec, aux_specs, gain_spec, head_t, state_spec = _attn_specs(is_ret, t, tb, lambda i: i)

    def body(*refs):
        proj_ref = refs[0]
        aux = refs[1:1 + n_aux]
        gn_ref, ofin_ref, oraw_ref, st_ref, state = refs[1 + n_aux:]

        @pl.when(pl.program_id(1) == 0)
        def _():
            state[...] = jnp.zeros_like(state)

        q, k, v, gate, b, _, ri, ci = _tile_inputs(is_ret, proj_ref, aux, nc)
        scores = _tile_scores(q, k, b, ri, ci)[0]
        vb = v.astype(BF16)
        intra = _bmm(scores.astype(BF16), vb)
        b_last = b[:, CHUNK - 1:CHUNK, :]
        e_last = jnp.exp(b_last)
        grow = _bmm_tn(vb, (k * jnp.exp(b_last - b)).astype(BF16))
        st = state[...]
        for c in range(nc):
            st_ref[c] = st
            st = st * e_last[c] + grow[c]
        state[...] = st
        inter = _bmm_nt((q * jnp.exp(b)).astype(BF16), st_ref[...].astype(BF16))
        out = (intra + inter).reshape(tb, LANES)
        oraw_ref[...] = out
        normed = out * _rms_scale(out)
        ofin_ref[...] = ((normed * gn_ref[...]) * (gate * jax.nn.sigmoid(gate))).astype(BF16)

    width = HEADS * LANES
    return _call(
        body, (proj, *aux_arrays, gain), name=name, grid=(HEADS, nb),
        in_specs=[proj_spec] + aux_specs + [gain_spec],
        out_specs=[head_t, head_t, state_spec],
        out_shape=[jax.ShapeDtypeStruct((t, width), BF16), jax.ShapeDtypeStruct((t, width), F32),
                   jax.ShapeDtypeStruct((HEADS, t // CHUNK, LANES, LANES), F32)],
        scratch_shapes=[pltpu.VMEM((LANES, LANES), F32)],
        compiler_params=_ARB2, hosted=hosted)


def _attn_bwd(is_ret, proj, aux_arrays, gain, o_raw, states, d_out, name):
    t = proj.shape[0]
    tb = min(t, ATTN_TILE)
    nc = tb // CHUNK
    n_aux = 4 if is_ret else 3
    nblk = t // tb
    nb, proj_spec, aux_specs, gain_spec, head_t, state_spec = _attn_specs(is_ret, t, tb, lambda i: nblk - 1 - i)
    base = 0 if is_ret else HEADS
    dout_spec = pl.BlockSpec((tb, LANES), lambda h, i: (nblk - 1 - i, base + h))

    def body(*refs):
        proj_ref = refs[0]
        aux = refs[1:1 + n_aux]
        gn_ref, oraw_ref, st_ref, dfin_ref = refs[1 + n_aux:5 + n_aux]
        if is_ret:
            dproj_ref, dgn_ref, dstate, dafter_ref = refs[5 + n_aux:]
        else:
            dproj_ref, dgn_ref, dlogit_ref, dba_ref, dstate, dafter_ref = refs[5 + n_aux:]

        @pl.when(pl.program_id(1) == 0)
        def _():
            dstate[...] = jnp.zeros_like(dstate)
            dgn_ref[...] = jnp.zeros_like(dgn_ref)
            if not is_ret:
                dba_ref[...] = jnp.zeros_like(dba_ref)

        shape3 = (nc, CHUNK, LANES)
        q, k, v, gate, b, logit, ri, ci = _tile_inputs(is_ret, proj_ref, aux, nc)
        scores, ep, en, qt, kt, qh, kh = _tile_scores(q, k, b, ri, ci)
        eb = jnp.exp(b)
        qe = q * eb
        b_last = b[:, CHUNK - 1:CHUNK, :]
        e_last = jnp.exp(b_last)
        ekd = jnp.exp(b_last - b)
        kd = k * ekd

        gn = gn_ref[...]
        out = oraw_ref[...]
        r = _rms_scale(out)
        normed = out * r
        sg = jax.nn.sigmoid(gate)
        dfin = dfin_ref[...]
        dgate = dfin * (normed * gn) * _silu_grad(gate, sg)
        dpre = dfin * (gate * sg)
        dgn_ref[...] += jnp.sum(dpre * normed, axis=0, keepdims=True)
        dnormed = dpre * gn
        d_o = r * (dnormed - normed * jnp.mean(dnormed * normed, axis=-1, keepdims=True))
        dob, vb = d_o.reshape(shape3).astype(BF16), v.astype(BF16)

        dgrow = _bmm_tn(dob, qe.astype(BF16))
        dst = dstate[...]
        for c in reversed(range(nc)):
            dafter_ref[c] = dst
            dst = dst * e_last[c] + dgrow[c]
        dstate[...] = dst
        st = st_ref[...]
        dafter = dafter_ref[...]
        stb, dafter_b = st.astype(BF16), dafter.astype(BF16)

        qtb, ktb, qhb, khb = qt.astype(BF16), kt.astype(BF16), qh.astype(BF16), kh.astype(BF16)
        scores_t = jnp.where(ci >= ri, _bmm_nt(ktb, qtb), _bmm_nt(khb, qhb))
        dv = _bmm(scores_t.astype(BF16), dob) + _bmm_nt(kd.astype(BF16), dafter_b)
        dsc = _bmm_nt(dob, vb)
        dsc_t = _bmm_nt(vb, dob)
        dqe = _bmm(dob, stb)
        dkd = _bmm(vb, dafter_b)
        dqt = _bmm(jnp.where(ci <= ri, dsc, 0.0).astype(BF16), ktb)
        dqh = _bmm(jnp.where(ci <= ri, 0.0, dsc).astype(BF16), khb)
        dkt = _bmm(jnp.where(ci >= ri, dsc_t, 0.0).astype(BF16), qtb)
        dkh = _bmm(jnp.where(ci >= ri, 0.0, dsc_t).astype(BF16), qhb)
        dq = (dqt * ep + dqh * en + dqe * eb).reshape(tb, LANES)
        dk = (dkt * en + dkh * ep + dkd * ekd).reshape(tb, LANES)

        if is_ret:
            cos_ref, sa_ref, sb_ref, _ = aux
            cos, sa, sb = cos_ref[...], sa_ref[...], sb_ref[...]
            dq_raw = _rot_t(dq, cos, sa, sb)
            dk_raw = _rot_t(dk, cos, sa, sb) * QK_SCALE
        else:
            dq_raw = dq * QK_SCALE
            dk_raw = dk
            db = dqt * qt - dkt * kt - dqh * qh + dkh * kh + dqe * qe - dkd * kd
            db_last = (jnp.sum(dkd * kd, axis=1, keepdims=True)
                       + jnp.sum(dafter * st, axis=1, keepdims=True) * e_last)
            last_row = lax.broadcasted_iota(jnp.int32, shape3, 1) == CHUNK - 1
            db = db + jnp.where(last_row, db_last, 0.0)
            dla = _masked_sum((ci >= ri).astype(BF16), db).reshape(tb, LANES)
            dlogit = dla * (1.0 / GATE_NORM) * jax.nn.sigmoid(-logit)
            dlogit_ref[...] = dlogit.astype(BF16)
            dba_ref[...] += jnp.sum(dlogit, axis=0, keepdims=True)

        dproj_ref[:, 0:LANES] = dq_raw.astype(BF16)
        dproj_ref[:, LANES:2 * LANES] = dk_raw.astype(BF16)
        dproj_ref[:, 2 * LANES:3 * LANES] = dv.reshape(tb, LANES).astype(BF16)
        dproj_ref[:, 3 * LANES:4 * LANES] = dgate.astype(BF16)

    width = HEADS * LANES
    row_out = pl.BlockSpec((1, LANES), lambda h, i: (0, h))
    out_specs = [pl.BlockSpec((tb, HEAD_BLOCK), lambda h, i: (nblk - 1 - i, h)), row_out]
    out_shape = [jax.ShapeDtypeStruct((t, HEADS * HEAD_BLOCK), BF16), jax.ShapeDtypeStruct((1, width), F32)]
    if not is_ret:
        out_specs += [head_t, row_out]
        out_shape += [jax.ShapeDtypeStruct((t, width), BF16), jax.ShapeDtypeStruct((1, width), F32)]
    return pl.pallas_call(
        body, name=name, grid=(HEADS, nblk),
        in_specs=[proj_spec] + aux_specs + [gain_spec, head_t, state_spec, dout_spec],
        out_specs=out_specs, out_shape=out_shape,
        scratch_shapes=[pltpu.VMEM((LANES, LANES), F32), pltpu.VMEM((nc, LANES, LANES), F32)],
        compiler_params=_ARB2,
    )(proj, *aux_arrays, gain, o_raw, states, d_out)


def _place():
    x, y, c = lax.axis_index("x"), lax.axis_index("y"), lax.axis_index("c")
    chips = [(1 - x, y), (x, 1 - y), (1 - x, 1 - y)]
    return x, y, c, 2 * x + y, chips


def _gather_plan(arrs):
    na = len(arrs)

    def copies(ins, outs, send_sems, recv_sems):
        x, y, c, me, chips = _place()

        def ici(a, j, src_chip, to):
            return pltpu.make_async_remote_copy(
                src_ref=ins[a].at[:, c], dst_ref=outs[a].at[src_chip, :, c],
                send_sem=send_sems.at[6 * a + j], recv_sem=recv_sems.at[6 * a + j], device_id=to, device_id_type=MESH)

        def d2d(a, j, src_chip, half):
            blk = outs[a].at[src_chip, :, half]
            return pltpu.make_async_remote_copy(
                src_ref=blk, dst_ref=blk, send_sem=send_sems.at[6 * a + 3 + j], recv_sem=recv_sems.at[6 * a + 3 + j],
                device_id=(x, y, 1 - c), device_id_type=MESH)

        peers = [(a, j, px, py) for a in range(na) for j, (px, py) in enumerate(chips)]
        return c, me, peers, ici, d2d

    def start(*refs):
        c, me, peers, ici, _ = copies(*refs)
        for a, j, px, py in peers:
            ici(a, j, me, (px, py, c)).start()

    def finish(*refs):
        c, me, peers, ici, d2d = copies(*refs)
        for a, j, px, py in peers:
            ici(a, j, 2 * px + py, (px, py, c)).wait_recv()
            d2d(a, j, 2 * px + py, c).start()
        for a, j, px, py in peers:
            d2d(a, j, 2 * px + py, 1 - c).wait_recv()
        for a, j, px, py in peers:
            ici(a, j, me, (px, py, c)).wait_send()
            d2d(a, j, 2 * px + py, c).wait_send()

    return _Hosted(arrs, [jax.ShapeDtypeStruct((N_CHIPS,) + a.shape, a.dtype) for a in arrs], 6 * na, start, finish)


def _pair_exchange(grads, name):
    na = len(grads)

    def body(*refs):
        ins, outs = refs[:na], refs[na:2 * na]
        send_sems, recv_sems = refs[2 * na:]
        x, y, c, _, _ = _place()
        copies = [pltpu.make_async_remote_copy(
            src_ref=ins[a].at[:, 1 - c], dst_ref=outs[a], send_sem=send_sems.at[a], recv_sem=recv_sems.at[a],
            device_id=(x, y, 1 - c), device_id_type=MESH) for a in range(na)]
        for cp in copies:
            cp.start()
        for cp in copies:
            cp.wait()

    return pl.pallas_call(
        body, name=name,
        in_specs=[ANY] * na, out_specs=[ANY] * na,
        out_shape=[jax.ShapeDtypeStruct(g.shape[:1] + g.shape[2:], g.dtype) for g in grads],
        scratch_shapes=[pltpu.SemaphoreType.DMA((na,)), pltpu.SemaphoreType.DMA((na,))],
    )(*grads)


def _pair_add(grad, recv, c_arr, name):
    _, _, r, cols = grad.shape

    def body(c_ref, g_ref, r_ref, o_ref):
        o_ref[...] = (g_ref[...].astype(F32) + r_ref[...].astype(F32)).astype(BF16)

    return pl.pallas_call(
        body, name=name,
        grid_spec=pltpu.PrefetchScalarGridSpec(
            num_scalar_prefetch=1, grid=(N_CHIPS,),
            in_specs=[pl.BlockSpec((None, None, r, cols), lambda p, c_ref: (p, c_ref[0], 0, 0)),
                      pl.BlockSpec((None, r, cols), lambda p, c_ref: (p, 0, 0))],
            out_specs=pl.BlockSpec((None, r, cols), lambda p, c_ref: (p, 0, 0))),
        out_shape=jax.ShapeDtypeStruct((N_CHIPS, r, cols), BF16),
        compiler_params=_ARB1,
    )(c_arr, grad, recv)


def _chip_exchange_plan(sums):
    na = len(sums)

    def copies(ins, outs, send_sems, recv_sems):
        x, y, c, me, chips = _place()

        def copy(a, j, px, py, block, slot):
            return pltpu.make_async_remote_copy(
                src_ref=ins[a].at[block], dst_ref=outs[a].at[slot],
                send_sem=send_sems.at[3 * a + j], recv_sem=recv_sems.at[3 * a + j],
                device_id=(px, py, c), device_id_type=MESH)

        peers = [(a, j, px, py) for a in range(na) for j, (px, py) in enumerate(chips)]
        return me, peers, copy

    def start(*refs):
        me, peers, copy = copies(*refs)
        for a, j, px, py in peers:
            copy(a, j, px, py, 2 * px + py, me).start()

    def finish(*refs):
        me, peers, copy = copies(*refs)
        for a, j, px, py in peers:
            copy(a, j, px, py, me, 2 * px + py).wait_recv()
        for a, j, px, py in peers:
            copy(a, j, px, py, 2 * px + py, me).wait_send()

    return _Hosted(sums, [jax.ShapeDtypeStruct(s.shape, s.dtype) for s in sums], 3 * na, start, finish)


def _chip_sum(own, recv, me_arr, name):
    _, r, cols = recv.shape

    def body(me_ref, own_ref, r_ref, o_ref):
        o_ref[...] = jnp.zeros_like(o_ref)
        for q in range(N_CHIPS):
            @pl.when(me_ref[0] == q)
            def _():
                o_ref[...] += own_ref[...].astype(F32)

            @pl.when(me_ref[0] != q)
            def _():
                o_ref[...] += r_ref[q].astype(F32)

    return pl.pallas_call(
        body, name=name,
        grid_spec=pltpu.PrefetchScalarGridSpec(
            num_scalar_prefetch=1, grid=(1,),
            in_specs=[pl.BlockSpec((None, r, cols), lambda i, me_ref: (me_ref[0], 0, 0)),
                      pl.BlockSpec((N_CHIPS, r, cols), lambda i, me_ref: (0, 0, 0))],
            out_specs=pl.BlockSpec((r, cols), lambda i, me_ref: (0, 0))),
        out_shape=jax.ShapeDtypeStruct((r, cols), F32),
        compiler_params=_ARB1,
    )(me_arr, own, recv)


def _pair_share(halves):
    na = len(halves)

    def body(*refs):
        ins, outs = refs[:na], refs[na:2 * na]
        send_sems, recv_sems = refs[2 * na:]
        x, y, c, _, _ = _place()
        copies = [pltpu.make_async_remote_copy(
            src_ref=ins[a], dst_ref=outs[a], send_sem=send_sems.at[a], recv_sem=recv_sems.at[a],
            device_id=(x, y, 1 - c), device_id_type=MESH) for a in range(na)]
        for cp in copies:
            cp.start()
        for cp in copies:
            cp.wait()

    return pl.pallas_call(
        body, name="pair_share",
        in_specs=[ANY] * na, out_specs=[ANY] * na,
        out_shape=[jax.ShapeDtypeStruct(h.shape, h.dtype) for h in halves],
        scratch_shapes=[pltpu.SemaphoreType.DMA((na,)), pltpu.SemaphoreType.DMA((na,))],
    )(*halves)


def _small_allreduce(block):
    m, n = block.shape

    def body(x_ref, all_ref, sum_ref, send_sems, recv_sems, local_sem):
        x, y, c, _, chips = _place()
        me, sibling = (x, y, c), (x, y, 1 - c)

        def rows(px, py, pc):
            return all_ref.at[pl.ds((4 * px + 2 * py + pc) * m, m), :]

        def copy(k, blk, to, src=None):
            return pltpu.make_async_remote_copy(
                src_ref=rows(*blk) if src is None else src, dst_ref=rows(*blk),
                send_sem=send_sems.at[k], recv_sem=recv_sems.at[k], device_id=to, device_id_type=MESH)

        mine = pltpu.make_async_copy(x_ref, rows(*me), local_sem)
        mine.start()
        first = [copy(0, me, sibling, src=x_ref)]
        first += [copy(1 + j, me, (*chip, c), src=x_ref) for j, chip in enumerate(chips)]
        for cp in first:
            cp.start()
        passed = [copy(4 + j, (*chip, c), sibling) for j, chip in enumerate(chips)]
        for j, chip in enumerate(chips):
            copy(1 + j, (*chip, c), me).wait_recv()
            passed[j].start()
        copy(0, sibling, me).wait_recv()
        for j, chip in enumerate(chips):
            copy(4 + j, (*chip, 1 - c), me).wait_recv()
        for cp in first + passed:
            cp.wait_send()
        mine.wait()
        acc = all_ref[0:m, :]
        for d in range(1, 8):
            acc = acc + all_ref[d * m:(d + 1) * m, :]
        sum_ref[...] = acc

    vmem = pl.BlockSpec(memory_space=pltpu.VMEM)
    return pl.pallas_call(
        body, name="small_allreduce",
        in_specs=[vmem], out_specs=[vmem, vmem],
        out_shape=[jax.ShapeDtypeStruct((8 * m, n), F32), jax.ShapeDtypeStruct((m, n), F32)],
        scratch_shapes=[pltpu.SemaphoreType.DMA((7,)), pltpu.SemaphoreType.DMA((7,)), pltpu.SemaphoreType.DMA],
    )(block)[1]


def _row_tile(rows):
    best = rows
    for cand in range(8, min(rows, 512) + 1, 8):
        if rows % cand == 0:
            best = cand
    return best


def _adamw_math(w, g, m, v):
    m2 = ADAM_B1 * m + (1.0 - ADAM_B1) * g
    v2 = ADAM_B2 * v + (1.0 - ADAM_B2) * (g * g)
    m_hat = m2 / (1.0 - ADAM_B1 ** ADAM_STEP)
    v_hat = v2 / (1.0 - ADAM_B2 ** ADAM_STEP)
    return -ADAM_LR * (m_hat / (jnp.sqrt(v_hat) + ADAM_EPS) + ADAM_WD * w), m2, v2


def _adamw_halves(w, g_mine, g_other, m, v, c_arr, name):
    rows, cols = w.shape
    r = rows // 2
    tr = _row_tile(r)
    nt = r // tr

    def body(c_ref, w_ref, gm_ref, go_ref, m_ref, v_ref, g_ref, d_ref, nm_ref, nv_ref):
        gv = jnp.where(pl.program_id(0) == c_ref[0], gm_ref[...], go_ref[...])
        g_ref[...] = gv
        d_ref[...], nm_ref[...], nv_ref[...] = _adamw_math(w_ref[...], gv, m_ref[...], v_ref[...])

    full = pl.BlockSpec((tr, cols), lambda h, i, c_ref: (h * nt + i, 0))
    half = pl.BlockSpec((tr, cols), lambda h, i, c_ref: (i, 0))
    shape = jax.ShapeDtypeStruct((rows, cols), F32)
    return pl.pallas_call(
        body, name=name,
        grid_spec=pltpu.PrefetchScalarGridSpec(
            num_scalar_prefetch=1, grid=(2, nt),
            in_specs=[full, half, half, full, full], out_specs=[full] * 4),
        out_shape=[shape] * 4,
        compiler_params=_ARB2,
    )(c_arr, w, g_mine, g_other, m, v)


def _adamw(w, g, m, v, name):
    rows, cols = w.shape
    tr = _row_tile(rows)

    def body(w_ref, g_ref, m_ref, v_ref, d_ref, nm_ref, nv_ref):
        d_ref[...], nm_ref[...], nv_ref[...] = _adamw_math(w_ref[...], g_ref[...], m_ref[...], v_ref[...])

    spec = pl.BlockSpec((tr, cols), lambda i: (i, 0))
    shape = jax.ShapeDtypeStruct((rows, cols), F32)
    return pl.pallas_call(
        body, name=name, grid=(rows // tr,),
        in_specs=[spec] * 4, out_specs=[spec] * 3, out_shape=[shape] * 3,
        compiler_params=_ARB1,
    )(w, g, m, v)


def _in_columns():
    pieces = []
    for group in range(2):
        q0, k0, v0, g0 = (0, 256, 512, 1024) if group == 0 else (1536, 1792, 2048, 2560)
        for h in range(HEADS):
            pieces += [(q0 + 64 * h, 64), (k0 + 64 * h, 64), (v0 + 128 * h, 128), (g0 + 128 * h, 128)]
    pieces.append((3072, GATE_RANK))
    return pieces


def _pad_w_in_t(w_in_t):
    parts = []
    for start, width in _in_columns():
        parts.append(w_in_t[start:start + width])
        if width < LANES:
            parts.append(jnp.zeros((LANES - width, w_in_t.shape[1]), w_in_t.dtype))
    return jnp.concatenate(parts, axis=0)


def _unpad_w_in_t(w_pt):
    rows = {}
    offset = 0
    for start, width in _in_columns():
        rows[start] = w_pt[offset:offset + width]
        offset += LANES
    return jnp.concatenate([rows[s] for s in sorted(rows)], axis=0)


def _rope_tables(t):
    half = 32
    inv = ROPE_BASE ** (-jnp.arange(half, dtype=F32) * 2.0 / 64)
    ang = jnp.arange(t, dtype=F32)[:, None] * inv[None, :]
    cos, sin = jnp.cos(ang), jnp.sin(ang)
    z32, z64 = jnp.zeros((t, 32), F32), jnp.zeros((t, 64), F32)
    return (jnp.concatenate([cos, cos, z64], axis=1),
            jnp.concatenate([-sin, z32, z64], axis=1),
            jnp.concatenate([z32, sin, z64], axis=1))


def _halves(w):
    n, rows, cols = w.shape
    return w.reshape(n, 2, rows // 2, cols)


def _pack_small(n1, nm, n2, nf, nret, ngla, ba, wa2, wa2_cols, extra=None):
    z = lambda k: jnp.zeros((1, k), F32)
    rows = [n1.reshape(1, -1), nm.reshape(1, -1), n2.reshape(1, -1), nf.reshape(1, -1),
            jnp.concatenate([nret.reshape(1, -1), ngla.reshape(1, -1)], axis=1),
            jnp.concatenate([ba.reshape(1, -1), z(D_MODEL - 256)], axis=1),
            jnp.zeros((1, D_MODEL), F32) if extra is None else extra,
            jnp.zeros((1, D_MODEL), F32),
            jnp.concatenate([wa2.reshape(GATE_RANK, wa2_cols), jnp.zeros((GATE_RANK, D_MODEL - wa2_cols), F32)], axis=1),
            jnp.zeros((SMALL_ROWS - 8 - GATE_RANK, D_MODEL), F32)]
    return jnp.concatenate(rows, axis=0)


def _unpack_small(p, wa2_cols):
    return (p[0:1], p[1:2], p[2:3], p[3], p[4:5, 0:512], p[4:5, 512:1024], p[5:6, 0:256],
            p[8:8 + GATE_RANK, 0:wa2_cols].reshape(1, GATE_RANK, wa2_cols))


def _pad_in_rows(w_t):
    return jnp.pad(w_t, ((0, IN_ROWS - IN_SHARD), (0, 0)))


def _forward_backward(xs, target, ffn1_w, rest, ba_p, ffn1_norm_g, mix_norm_g, ret_norm_g, gla_norm_g, ffn2_norm_g,
                      final_norm_g, rest_plan=None, rest_weights=None, ffn2_plans=None, ffn2_weights=None,
                      early=None, late=None):
    t = xs.shape[0]
    cos_t, sa_t, sb_t = _rope_tables(t)
    log_gamma = jnp.log(1.0 - 2.0 ** (-5.0 - jnp.arange(HEADS, dtype=F32)))
    lg_t = jnp.broadcast_to(log_gamma[:, None, None], (HEADS, 1, LANES))
    ret_aux = [cos_t, sa_t, sb_t, lg_t]

    (x1, a1, u1, h1), gathered = _ffn_fwd(xs, ffn1_norm_g, ffn1_w, "ffn1_fwd", hosted=rest_plan)
    ffn2_w, w_in_pt, w_out_full, wa2_p = rest if rest_plan is None else rest_weights(gathered)
    plans = [None] * 3 if ffn2_plans is None else ffn2_plans
    (proj, h_mix), got_gate = _mixer_in_fwd(x1, mix_norm_g, w_in_pt, "mixer_in_fwd", hosted=plans[0])
    gla_aux = [proj, wa2_p, ba_p]
    (o_ret, raw_ret, st_ret), got_up = _attn_fwd(True, proj, ret_aux, ret_norm_g, "ret_fwd", hosted=plans[1])
    (o_gla, raw_gla, st_gla), got_down = _attn_fwd(False, proj, gla_aux, gla_norm_g, "gla_fwd", hosted=plans[2])
    if ffn2_plans is not None:
        ffn2_w = ffn2_weights(got_gate + got_up + got_down)
    x2 = _mixer_out_fwd(o_ret, o_gla, w_out_full, x1, "mixer_out_fwd")
    (x3, a2, u2, h2), _ = _ffn_fwd(x2, ffn2_norm_g, ffn2_w, "ffn2_fwd")
    loss_blk, dx3, d_final_g = _final_loss(x3, final_norm_g, target, "final_loss")

    (da2, du2, hid2, dob2, dx2, d_ffn2_g), _ = _ffn_bwd(dx3, x2, ffn2_norm_g, a2, u2, ffn2_w, "ffn2_bwd")
    g_gate2 = _matmul_tn(da2, h2, "ffn2_dgate", out_dtype=BF16)
    g_up2 = _matmul_tn(du2, h2, "ffn2_dup", out_dtype=BF16)
    g_down2 = _matmul_tn(hid2, dob2, "ffn2_ddown", out_dtype=BF16)

    d_o = _matmul_nt(dx2, w_out_full, "mixer_out_bwd")
    g_wout_ret = _matmul_tn(o_ret, dx2, "wout_grad_ret", out_dtype=BF16)
    g_wout_gla = _matmul_tn(o_gla, dx2, "wout_grad_gla", out_dtype=BF16)
    dproj_ret, d_ret_g = _attn_bwd(True, proj, ret_aux, ret_norm_g, raw_ret, st_ret, d_o, "ret_bwd")
    dproj_gla, d_gla_g, dlogit, d_ba_p = _attn_bwd(False, proj, gla_aux, gla_norm_g, raw_gla, st_gla, d_o, "gla_bwd")
    d_glow = _matmul_nt(dlogit, wa2_p, "gate_low_bwd", out_dtype=BF16)
    g_wa2_p = _matmul_tn(proj[:, PROJ_P - LANES:], dlogit, "gate_w_grad")
    dproj = jnp.concatenate([dproj_ret, dproj_gla, d_glow], axis=1)
    g_win_p = _matmul_tn(dproj, h_mix, "w_in_grad", tka=PROJ_P // 3, out_dtype=BF16)
    dx1, d_mix_g = _mixer_in_bwd(dproj, w_in_pt, dx2, x1, mix_norm_g, "mixer_in_bwd")
    g_win_t = _unpad_w_in_t(g_win_p[0])
    g_win = jnp.stack([_pad_in_rows(g_win_t[IN_SHARD * p:IN_SHARD * (p + 1)]) for p in range(N_CHIPS)], axis=0)
    g_wout = jnp.concatenate([g_wout_ret[0], g_wout_gla[0]], axis=0).reshape(N_CHIPS, D_MODEL // N_CHIPS, D_MODEL)

    early_plan = None if early is None else early([g_gate2, g_up2, g_down2, g_win, g_wout])
    (da1, du1, hid1, dob1, grad_x, d_ffn1_g), arrived = _ffn_bwd(dx1, xs, ffn1_norm_g, a1, u1, ffn1_w, "ffn1_bwd",
                                                                hosted=early_plan)
    late_grads, late_arrived = [], []
    for lhs, rhs, name in ((da1, h1, "ffn1_dgate"), (du1, h1, "ffn1_dup"), (hid1, dob1, "ffn1_ddown")):
        plan = None if late is None or not late_grads else late(late_grads[-1], len(late_grads))
        res = _matmul_tn(lhs, rhs, name, out_dtype=BF16, hosted=plan)
        if plan is not None:
            res, carried = res
            late_arrived += carried
        late_grads.append(res)
    g_gate1, g_up1, g_down1 = late_grads

    return (loss_blk, grad_x, g_gate1, g_up1, g_down1, g_gate2, g_up2, g_down2, g_win, g_wout, g_wa2_p,
            d_ba_p, d_ffn1_g, d_mix_g, d_ffn2_g, d_final_g, d_ret_g, d_gla_g, arrived, late_arrived)


def kernel(x, ffn1_norm_g, ffn1_w_gate, ffn1_w_up, ffn1_w_down, mix_norm_g, w_in, ret_norm_g, gla_w_a2, gla_b_a, gla_norm_g, w_out, ffn2_norm_g, ffn2_w_gate, ffn2_w_up, ffn2_w_down, final_norm_g, loss_target, m_ffn1_norm_g, m_ffn1_w_gate, m_ffn1_w_up, m_ffn1_w_down, m_mix_norm_g, m_w_in, m_ret_norm_g, m_gla_w_a2, m_gla_b_a, m_gla_norm_g, m_w_out, m_ffn2_norm_g, m_ffn2_w_gate, m_ffn2_w_up, m_ffn2_w_down, m_final_norm_g, v_ffn1_norm_g, v_ffn1_w_gate, v_ffn1_w_up, v_ffn1_w_down, v_mix_norm_g, v_w_in, v_ret_norm_g, v_gla_w_a2, v_gla_b_a, v_gla_norm_g, v_w_out, v_ffn2_norm_g, v_ffn2_w_gate, v_ffn2_w_up, v_ffn2_w_down, v_final_norm_g):
    t = x.shape[1]
    xs = x.reshape(t, D_MODEL)
    target = loss_target.reshape(t, D_MODEL)
    chip = 2 * lax.axis_index("x") + lax.axis_index("y")
    c_arr = lax.axis_index("c").astype(jnp.int32).reshape(1)

    me_arr = chip.astype(jnp.int32).reshape(1)

    pad_rows = _pad_in_rows

    def own_block(gathered, shard):
        return lax.dynamic_update_slice(gathered, shard[None], (chip,) + (0,) * shard.ndim)

    ffn1_shard = _halves(jnp.stack([ffn1_w_gate[0].T, ffn1_w_up[0].T, ffn1_w_down[0]], axis=0).astype(BF16))
    rest_shards = [_halves(pad_rows(w_in[0].T).astype(BF16)[None]),
                   _halves(w_out.astype(BF16)),
                   jnp.concatenate([gla_w_a2.reshape(GATE_RANK, 64), jnp.zeros((GATE_RANK, 64), F32)],
                                   axis=1).reshape(1, 2, 8, LANES)]
    ffn2_shards = [_halves(w.astype(BF16)[None]) for w in (ffn2_w_gate[0].T, ffn2_w_up[0].T, ffn2_w_down[0])]
    ffn1_all = _run_hosted(_gather_plan([ffn1_shard]), "gather_ffn1")[0]
    ffn1_w = own_block(ffn1_all, ffn1_shard).reshape(N_CHIPS, 3, FF_SHARD, D_MODEL)

    def rest_weights(gathered):
        win_all, wout_all, wa2_all = [own_block(g, s) for g, s in zip(gathered, rest_shards)]
        win_t = win_all.reshape(N_CHIPS, IN_ROWS, D_MODEL)
        w_in_pt = _pad_w_in_t(jnp.concatenate([win_t[p, 0:IN_SHARD] for p in range(N_CHIPS)], axis=0))
        wa2_p = jnp.pad(
            wa2_all.reshape(N_CHIPS, GATE_RANK, LANES).transpose(1, 0, 2).reshape(GATE_RANK, HEADS * LANES),
            ((0, LANES - GATE_RANK), (0, 0))).astype(BF16)
        return (None, w_in_pt, wout_all.reshape(D_MODEL, D_MODEL), wa2_p)

    def ffn2_weights(gathered):
        return [own_block(g, s).reshape(N_CHIPS, FF_SHARD, D_MODEL) for g, s in zip(gathered, ffn2_shards)]

    def pair_sums(grads, tag):
        halves = [g.reshape(g.shape[0], 2, g.shape[1] // 2, g.shape[2]) for g in grads]
        recv = _pair_exchange(halves, "pair_exchange_" + tag)
        return [_pair_add(g, r, c_arr, "pair_add_%s%d" % (tag, k)) for k, (g, r) in enumerate(zip(halves, recv))]

    early_sums = []

    def early(grads):
        early_sums.extend(pair_sums(grads, "early"))
        return _chip_exchange_plan(early_sums)

    late_sums = []

    def late(grad, number):
        late_sums.extend(pair_sums([grad], "late%d" % number))
        return _chip_exchange_plan(late_sums[-1:])

    ba_p = jnp.pad(gla_b_a.reshape(HEADS, 64), ((0, 0), (0, 64))).reshape(1, HEADS * LANES)
    fb = _forward_backward(xs, target, ffn1_w, None, ba_p, ffn1_norm_g, mix_norm_g, ret_norm_g, gla_norm_g,
                           ffn2_norm_g, final_norm_g.reshape(1, D_MODEL), rest_plan=_gather_plan(rest_shards),
                           rest_weights=rest_weights, ffn2_plans=[_gather_plan([s]) for s in ffn2_shards],
                           ffn2_weights=ffn2_weights, early=early, late=late)
    (loss_blk, grad_x, _, _, g_down1, _, _, _, _, _, g_wa2_p,
     d_ba_p, d_ffn1_g, d_mix_g, d_ffn2_g, d_final_g, d_ret_g, d_gla_g, early_arrived, late_arrived) = fb
    late_arrived = late_arrived + _run_hosted(late(g_down1, 3), "chip_exchange_late")
    sums, arrived = late_sums + early_sums, late_arrived + early_arrived
    mine = [_chip_sum(s, r, me_arr, "chip_sum_%d" % k) for k, (s, r) in enumerate(zip(sums, arrived))]
    other = _pair_share(mine)

    g_wa2 = g_wa2_p[0][0:GATE_RANK].reshape(GATE_RANK, HEADS, LANES)[:, :, 0:64].reshape(GATE_RANK, 256)
    d_ba = d_ba_p.reshape(HEADS, LANES)[:, 0:64].reshape(1, 256)
    loss_row = jnp.pad(loss_blk[0:1, 0:1], ((0, 0), (0, D_MODEL - 1)))
    small_local = _pack_small(d_ffn1_g, d_mix_g, d_ffn2_g, d_final_g, d_ret_g, d_gla_g, d_ba, g_wa2, 256, loss_row)
    small_sum = _small_allreduce(small_local)
    loss = small_sum[6, 0]
    sg = _unpack_small(small_sum, 256)
    wa2_grad = lax.dynamic_slice(sg[7], (0, 0, 64 * chip), (1, GATE_RANK, 64))
    small_g = _pack_small(*sg[:7], wa2_grad, 64)
    small_w = _pack_small(ffn1_norm_g, mix_norm_g, ffn2_norm_g, final_norm_g, ret_norm_g, gla_norm_g, gla_b_a, gla_w_a2, 64)
    small_m = _pack_small(m_ffn1_norm_g, m_mix_norm_g, m_ffn2_norm_g, m_final_norm_g, m_ret_norm_g, m_gla_norm_g,
                          m_gla_b_a, m_gla_w_a2, 64)
    small_v = _pack_small(v_ffn1_norm_g, v_mix_norm_g, v_ffn2_norm_g, v_final_norm_g, v_ret_norm_g, v_gla_norm_g,
                          v_gla_b_a, v_gla_w_a2, 64)
    small_out = _adamw(small_w, small_g, small_m, small_v, "adamw_small")
    s_grad = _unpack_small(small_g, 64)
    s_delta, s_m, s_v = (_unpack_small(o, 64) for o in small_out)

    def big(k, w, m, v, name, to_2d, from_2d):
        outs4 = _adamw_halves(to_2d(w), mine[k], other[k], to_2d(m), to_2d(v), c_arr, name)
        return [from_2d(z) for z in outs4]

    plain = (lambda w: w[0], lambda z: z[None])
    transposed = (lambda w: w[0].T, lambda z: z.T[None])
    in_proj = (lambda w: pad_rows(w[0].T), lambda z: z[0:IN_SHARD].T[None])
    r_g1 = big(0, ffn1_w_gate, m_ffn1_w_gate, v_ffn1_w_gate, "adamw_ffn1_gate", *transposed)
    r_u1 = big(1, ffn1_w_up, m_ffn1_w_up, v_ffn1_w_up, "adamw_ffn1_up", *transposed)
    r_d1 = big(2, ffn1_w_down, m_ffn1_w_down, v_ffn1_w_down, "adamw_ffn1_down", *plain)
    r_g2 = big(3, ffn2_w_gate, m_ffn2_w_gate, v_ffn2_w_gate, "adamw_ffn2_gate", *transposed)
    r_u2 = big(4, ffn2_w_up, m_ffn2_w_up, v_ffn2_w_up, "adamw_ffn2_up", *transposed)
    r_d2 = big(5, ffn2_w_down, m_ffn2_w_down, v_ffn2_w_down, "adamw_ffn2_down", *plain)
    r_in = big(6, w_in, m_w_in, v_w_in, "adamw_w_in", *in_proj)
    r_out = big(7, w_out, m_w_out, v_w_out, "adamw_w_out", *plain)

    def leaves(k, smalls):
        n1, nm, n2, nf, nret, ngla, ba, wa2 = smalls
        return [n1, r_g1[k], r_u1[k], r_d1[k], nm, r_in[k], nret, wa2, ba, ngla, r_out[k], n2, r_g2[k], r_u2[k], r_d2[k], nf]

    outs = [loss, grad_x.reshape(x.shape)]
    outs += leaves(0, s_grad) + leaves(1, s_delta) + leaves(2, s_m) + leaves(3, s_v)
    return tuple(outs)
```

```python
import functools

import jax
import jax.numpy as jnp
from jax import lax
from jax.experimental import pallas as pl
from jax.experimental.pallas import tpu as pltpu

F32, BF16 = jnp.float32, jnp.bfloat16
MESH = pl.DeviceIdType.MESH
ANY = pl.BlockSpec(memory_space=pl.ANY)

D_MODEL = 1024
D_FF = 2816
N_CHIPS = 4
FF_SHARD = D_FF // N_CHIPS
IN_WIDTH = 3088
IN_SHARD = IN_WIDTH // N_CHIPS
IN_ROWS = 800
CHUNK = 64
HEADS = 4
LANES = 128
PROJ_P = (2 * 4 * HEADS + 1) * LANES
GATE_RANK = 16
QK_SCALE = 0.125
GATE_NORM = 16.0
RMS_EPS = 1e-6
ROPE_BASE = 10000.0
ADAM_LR, ADAM_B1, ADAM_B2, ADAM_EPS, ADAM_WD, ADAM_STEP = 0.001, 0.9, 0.999, 1e-08, 0.01, 10
SMALL_ROWS = 32
TOKEN_TILE = 512
ATTN_TILE = 512

_ARB2 = pltpu.CompilerParams(dimension_semantics=("arbitrary", "arbitrary"))
_ARB1 = pltpu.CompilerParams(dimension_semantics=("arbitrary",))
_ARB3 = pltpu.CompilerParams(dimension_semantics=("arbitrary", "arbitrary", "arbitrary"))


def _dot(a, b):
    return jnp.dot(a, b, preferred_element_type=F32)


def _dot_nt(a, b):
    return lax.dot_general(a, b, (((1,), (1,)), ((), ())), preferred_element_type=F32)


def _dot_tn(a, b):
    return lax.dot_general(a, b, (((0,), (0,)), ((), ())), preferred_element_type=F32)


def _rms_scale(xv):
    return lax.rsqrt(jnp.mean(xv * xv, axis=-1, keepdims=True) + RMS_EPS)


def _rms_bwd(dh, xv, g):
    r = _rms_scale(xv)
    xhat = xv * r
    dxhat = dh * g
    dx = r * (dxhat - xhat * jnp.mean(dxhat * xhat, axis=-1, keepdims=True))
    return dx, jnp.sum(dh * xhat, axis=0, keepdims=True)


def _silu_grad(a, sg):
    return sg * (1.0 + a * (1.0 - sg))


class _Hosted:
    def __init__(self, arrays, out_shapes, n_sems, start, finish):
        self.arrays, self.out_shapes, self.n_sems = list(arrays), list(out_shapes), n_sems
        self.start, self.finish = start, finish


def _call(body, args, *, name, grid, in_specs, out_specs, out_shape, scratch_shapes, compiler_params, hosted=None):
    if hosted is None:
        outs = pl.pallas_call(body, name=name, grid=grid, in_specs=in_specs, out_specs=out_specs, out_shape=out_shape,
                              scratch_shapes=scratch_shapes, compiler_params=compiler_params)(*args)
        return list(outs), []
    n_in, n_out, n_sc, nh = len(in_specs), len(out_specs), len(scratch_shapes), len(hosted.arrays)

    def wrapped(*refs):
        ins, h_in = refs[:n_in], refs[n_in:n_in + nh]
        outs, h_out = refs[n_in + nh:n_in + nh + n_out], refs[n_in + nh + n_out:n_in + 2 * nh + n_out]
        rest = refs[n_in + 2 * nh + n_out:]
        scratch, (send_sems, recv_sems) = rest[:n_sc], rest[n_sc:]
        first = functools.reduce(jnp.logical_and, [pl.program_id(d) == 0 for d in range(len(grid))])
        last = functools.reduce(jnp.logical_and, [pl.program_id(d) == n - 1 for d, n in enumerate(grid)])

        @pl.when(first)
        def _():
            hosted.start(h_in, h_out, send_sems, recv_sems)

        body(*ins, *outs, *scratch)

        @pl.when(last)
        def _():
            hosted.finish(h_in, h_out, send_sems, recv_sems)

    sems = [pltpu.SemaphoreType.DMA((hosted.n_sems,)), pltpu.SemaphoreType.DMA((hosted.n_sems,))]
    outs = pl.pallas_call(
        wrapped, name=name, grid=grid, in_specs=list(in_specs) + [ANY] * nh, out_specs=list(out_specs) + [ANY] * nh,
        out_shape=list(out_shape) + hosted.out_shapes, scratch_shapes=list(scratch_shapes) + sems,
        compiler_params=compiler_params)(*args, *hosted.arrays)
    return list(outs[:n_out]), list(outs[n_out:])


def _run_hosted(hosted, name):
    nh = len(hosted.arrays)

    def body(*refs):
        h_in, h_out, (send_sems, recv_sems) = refs[:nh], refs[nh:2 * nh], refs[2 * nh:]
        hosted.start(h_in, h_out, send_sems, recv_sems)
        hosted.finish(h_in, h_out, send_sems, recv_sems)

    sems = [pltpu.SemaphoreType.DMA((hosted.n_sems,)), pltpu.SemaphoreType.DMA((hosted.n_sems,))]
    return list(pl.pallas_call(body, name=name, in_specs=[ANY] * nh, out_specs=[ANY] * nh,
                               out_shape=hosted.out_shapes, scratch_shapes=sems)(*hosted.arrays))


def _ffn_weight_operands(ffn_w, chunk_maps):
    if isinstance(ffn_w, (list, tuple)):
        specs = [pl.BlockSpec((None, FF_SHARD, D_MODEL), lambda *g, m=m: (m(*g), 0, 0)) for m in chunk_maps]
        return list(ffn_w), specs
    specs = [pl.BlockSpec((None, None, FF_SHARD, D_MODEL), lambda *g, m=m, k=kind: (m(*g), k, 0, 0))
             for kind, m in enumerate(chunk_maps)]
    return [ffn_w] * 3, specs


def _pipeline_items(steps):
    def cur(s):
        c = jnp.minimum(s, steps - 1)
        return c // N_CHIPS, c % N_CHIPS

    def prev(s):
        p = jnp.maximum(s - 1, 0)
        return p // N_CHIPS, p % N_CHIPS

    return cur, prev


def _ffn_fwd(x, g, ffn_w, name, hosted=None):
    t = x.shape[0]
    tm = min(t, TOKEN_TILE)

    def body(x_ref, g_ref, wg_ref, wu_ref, wd_ref, xo_ref, a_ref, u_ref, h_ref, acc_ref):
        j = pl.program_id(1)

        @pl.when(j == 0)
        def _():
            xv = x_ref[...]
            h_ref[...] = ((xv * _rms_scale(xv)) * g_ref[...]).astype(BF16)
            acc_ref[...] = jnp.zeros_like(acc_ref)

        h = h_ref[...]
        a = _dot_nt(h, wg_ref[...])
        u = _dot_nt(h, wu_ref[...])
        a_ref[...] = a.astype(BF16)
        u_ref[...] = u.astype(BF16)
        hid = (a * jax.nn.sigmoid(a)) * u
        acc_ref[...] += _dot(hid.astype(BF16), wd_ref[...])

        @pl.when(j == N_CHIPS - 1)
        def _():
            xo_ref[...] = x_ref[...] + 0.5 * acc_ref[...]

    tok = pl.BlockSpec((tm, D_MODEL), lambda i, j: (i, 0))
    act = pl.BlockSpec((None, tm, FF_SHARD), lambda i, j: (j, i, 0))
    w_arrays, weights = _ffn_weight_operands(ffn_w, [lambda i, j: j] * 3)
    return _call(
        body, (x, g, *w_arrays), name=name, grid=(t // tm, N_CHIPS),
        in_specs=[tok, pl.BlockSpec((1, D_MODEL), lambda i, j: (0, 0))] + weights,
        out_specs=[tok, act, act, tok],
        out_shape=[jax.ShapeDtypeStruct((t, D_MODEL), F32),
                   jax.ShapeDtypeStruct((N_CHIPS, t, FF_SHARD), BF16),
                   jax.ShapeDtypeStruct((N_CHIPS, t, FF_SHARD), BF16),
                   jax.ShapeDtypeStruct((t, D_MODEL), BF16)],
        scratch_shapes=[pltpu.VMEM((tm, D_MODEL), F32)],
        compiler_params=_ARB2, hosted=hosted)


def _ffn_bwd(dxo, x, g, a4, u4, ffn_w, name, hosted=None):
    t = x.shape[0]
    tm = min(t, TOKEN_TILE)
    steps = (t // tm) * N_CHIPS
    cur, prev = _pipeline_items(steps)


    def body(dxo_ref, dxo_prev_ref, x_ref, g_ref, a_ref, u_ref, wg_ref, wu_ref, wd_ref,
             da_ref, du_ref, hid_ref, dob_ref, dx_ref, dg_ref, acc_ref, da_slots, du_slots):
        s = pl.program_id(0)
        jc, jp = cur(s)[1], prev(s)[1]
        slot = s % 2

        @pl.when(s == 0)
        def _():
            dg_ref[...] = jnp.zeros_like(dg_ref)
            acc_ref[...] = jnp.zeros_like(acc_ref)
            da_slots[...] = jnp.zeros_like(da_slots)
            du_slots[...] = jnp.zeros_like(du_slots)

        @pl.when(jc == 0)
        def _():
            dob_ref[...] = (0.5 * dxo_ref[...]).astype(BF16)

        dhid = _dot_nt(dob_ref[...], wd_ref[...])
        a = a_ref[...].astype(F32)
        u = u_ref[...].astype(F32)
        sg = jax.nn.sigmoid(a)
        sl = a * sg
        hid_ref[...] = (sl * u).astype(BF16)
        du = (dhid * sl).astype(BF16)
        da = (dhid * u * _silu_grad(a, sg)).astype(BF16)
        du_ref[...] = du
        da_ref[...] = da
        acc_ref[...] += _dot(da_slots[1 - slot], wg_ref[...]) + _dot(du_slots[1 - slot], wu_ref[...])
        da_slots[slot] = da
        du_slots[slot] = du

        @pl.when((jp == N_CHIPS - 1) & (s > 0))
        def _():
            dx, dg = _rms_bwd(acc_ref[...], x_ref[...], g_ref[...])
            dx_ref[...] = dxo_prev_ref[...] + dx
            dg_ref[...] += dg
            acc_ref[...] = jnp.zeros_like(acc_ref)

    tok_cur = pl.BlockSpec((tm, D_MODEL), lambda s: (cur(s)[0], 0))
    tok_prev = pl.BlockSpec((tm, D_MODEL), lambda s: (prev(s)[0], 0))
    act = pl.BlockSpec((None, tm, FF_SHARD), lambda s: (cur(s)[1], cur(s)[0], 0))
    row = pl.BlockSpec((1, D_MODEL), lambda s: (0, 0))
    w_arrays, weights = _ffn_weight_operands(ffn_w, [lambda s: prev(s)[1], lambda s: prev(s)[1], lambda s: cur(s)[1]])
    act_shape = jax.ShapeDtypeStruct((N_CHIPS, t, FF_SHARD), BF16)
    return _call(
        body, (dxo, dxo, x, g, a4, u4, *w_arrays), name=name, grid=(steps + 1,),
        in_specs=[tok_cur, tok_prev, tok_prev, row, act, act] + weights,
        out_specs=[act, act, act, tok_cur, tok_prev, row],
        out_shape=[act_shape, act_shape, act_shape,
                   jax.ShapeDtypeStruct((t, D_MODEL), BF16),
                   jax.ShapeDtypeStruct((t, D_MODEL), F32),
                   jax.ShapeDtypeStruct((1, D_MODEL), F32)],
        scratch_shapes=[pltpu.VMEM((tm, D_MODEL), F32), pltpu.VMEM((2, tm, FF_SHARD), BF16),
                        pltpu.VMEM((2, tm, FF_SHARD), BF16)],
        compiler_params=_ARB1, hosted=hosted)


def _matmul_tn(a, b, name, tka=None, out_dtype=F32, hosted=None):
    a3, b3 = a.ndim == 3, b.ndim == 3
    nb = a.shape[0] if a3 else (b.shape[0] if b3 else 1)
    t, ka, n = a.shape[-2], a.shape[-1], b.shape[-1]
    tka = ka if tka is None else tka
    tk = min(t, 2 * TOKEN_TILE)
    nk = t // tk

    def body(a_ref, b_ref, o_ref, acc_ref):
        k = pl.program_id(2)

        @pl.when(k == 0)
        def _():
            acc_ref[...] = jnp.zeros_like(acc_ref)

        acc_ref[...] += _dot_tn(a_ref[...].astype(BF16), b_ref[...].astype(BF16))

        @pl.when(k == nk - 1)
        def _():
            o_ref[...] = acc_ref[...].astype(out_dtype)

    a_spec = (pl.BlockSpec((None, tk, tka), lambda i, j, k: (i, k, j)) if a3
              else pl.BlockSpec((tk, tka), lambda i, j, k: (k, j)))
    b_spec = (pl.BlockSpec((None, tk, n), lambda i, j, k: (i, k, 0)) if b3
              else pl.BlockSpec((tk, n), lambda i, j, k: (k, 0)))
    outs, carried = _call(
        body, (a, b), name=name, grid=(nb, ka // tka, t // tk),
        in_specs=[a_spec, b_spec],
        out_specs=[pl.BlockSpec((None, tka, n), lambda i, j, k: (i, j, 0))],
        out_shape=[jax.ShapeDtypeStruct((nb, ka, n), out_dtype)],
        scratch_shapes=[pltpu.VMEM((tka, n), F32)],
        compiler_params=_ARB3, hosted=hosted)
    return outs[0] if hosted is None else (outs[0], carried)


def _matmul_nt(a, w, name, out_dtype=F32):
    t, k = a.shape
    n = w.shape[0]
    tm = min(t, TOKEN_TILE)

    def body(a_ref, w_ref, o_ref):
        o_ref[...] = _dot_nt(a_ref[...].astype(BF16), w_ref[...]).astype(out_dtype)

    return pl.pallas_call(
        body, name=name, grid=(t // tm,),
        in_specs=[pl.BlockSpec((tm, k), lambda i: (i, 0)), pl.BlockSpec((n, k), lambda i: (0, 0))],
        out_specs=pl.BlockSpec((tm, n), lambda i: (i, 0)),
        out_shape=jax.ShapeDtypeStruct((t, n), out_dtype),
        compiler_params=_ARB1,
    )(a, w)


def _mixer_in_bwd(dproj, w_in_pt, dres, x, g, name):
    t, k = dproj.shape
    tm = min(t, TOKEN_TILE)

    def body(a_ref, w_ref, dres_ref, x_ref, g_ref, dx_ref, dg_ref):
        @pl.when(pl.program_id(0) == 0)
        def _():
            dg_ref[...] = jnp.zeros_like(dg_ref)

        dh = _dot(a_ref[...], w_ref[...])
        dx, dg = _rms_bwd(dh, x_ref[...], g_ref[...])
        dx_ref[...] = dres_ref[...] + dx
        dg_ref[...] += dg

    tok = pl.BlockSpec((tm, D_MODEL), lambda i: (i, 0))
    row = pl.BlockSpec((1, D_MODEL), lambda i: (0, 0))
    return pl.pallas_call(
        body, name=name, grid=(t // tm,),
        in_specs=[pl.BlockSpec((tm, k), lambda i: (i, 0)), pl.BlockSpec((k, D_MODEL), lambda i: (0, 0)), tok, tok, row],
        out_specs=[tok, row],
        out_shape=[jax.ShapeDtypeStruct((t, D_MODEL), F32), jax.ShapeDtypeStruct((1, D_MODEL), F32)],
        compiler_params=_ARB1,
    )(dproj, w_in_pt, dres, x, g)


def _mixer_in_fwd(x, g, w_in_pt, name, hosted=None):
    t = x.shape[0]
    tm = min(t, TOKEN_TILE)
    tn = PROJ_P // 3

    def body(x_ref, g_ref, w_ref, p_ref, h_ref):
        @pl.when(pl.program_id(1) == 0)
        def _():
            xv = x_ref[...]
            h_ref[...] = ((xv * _rms_scale(xv)) * g_ref[...]).astype(BF16)

        p_ref[...] = _dot_nt(h_ref[...], w_ref[...])

    tok = pl.BlockSpec((tm, D_MODEL), lambda i, j: (i, 0))
    return _call(
        body, (x, g, w_in_pt), name=name, grid=(t // tm, 3),
        in_specs=[tok, pl.BlockSpec((1, D_MODEL), lambda i, j: (0, 0)),
                  pl.BlockSpec((tn, D_MODEL), lambda i, j: (j, 0))],
        out_specs=[pl.BlockSpec((tm, tn), lambda i, j: (i, j)), tok],
        out_shape=[jax.ShapeDtypeStruct((t, PROJ_P), F32), jax.ShapeDtypeStruct((t, D_MODEL), BF16)],
        scratch_shapes=[], compiler_params=_ARB2, hosted=hosted)


def _mixer_out_fwd(o_ret, o_gla, w_out, x, name):
    t = x.shape[0]
    tm = min(t, TOKEN_TILE)
    half = HEADS * LANES

    def body(a_ref, b_ref, w_ref, x_ref, o_ref):
        o_ref[...] = x_ref[...] + _dot(a_ref[...], w_ref[0:half, :]) + _dot(b_ref[...], w_ref[half:2 * half, :])

    tok = pl.BlockSpec((tm, D_MODEL), lambda i: (i, 0))
    hb = pl.BlockSpec((tm, half), lambda i: (i, 0))
    return pl.pallas_call(
        body, name=name, grid=(t // tm,),
        in_specs=[hb, hb, pl.BlockSpec((2 * half, D_MODEL), lambda i: (0, 0)), tok],
        out_specs=tok, out_shape=jax.ShapeDtypeStruct((t, D_MODEL), F32),
        compiler_params=_ARB1,
    )(o_ret, o_gla, w_out, x)


def _final_loss(x, g, target, name):
    t = x.shape[0]
    tm = min(t, TOKEN_TILE)

    def body(x_ref, g_ref, t_ref, l_ref, dx_ref, dg_ref):
        @pl.when(pl.program_id(0) == 0)
        def _():
            l_ref[...] = jnp.zeros_like(l_ref)
            dg_ref[...] = jnp.zeros_like(dg_ref)

        xv = x_ref[...]
        gv = g_ref[...]
        err = (xv * _rms_scale(xv)) * gv - t_ref[...]
        l_ref[...] += 0.5 * jnp.sum(jnp.mean(err * err, axis=-1, keepdims=True), axis=0, keepdims=True)
        dx, dg = _rms_bwd(err * (1.0 / D_MODEL), xv, gv)
        dx_ref[...] = dx
        dg_ref[...] += dg

    tok = pl.BlockSpec((tm, D_MODEL), lambda i: (i, 0))
    row = pl.BlockSpec((1, D_MODEL), lambda i: (0, 0))
    return pl.pallas_call(
        body, name=name, grid=(t // tm,),
        in_specs=[tok, row, tok],
        out_specs=[pl.BlockSpec((8, LANES), lambda i: (0, 0)), tok, row],
        out_shape=[jax.ShapeDtypeStruct((8, LANES), F32), jax.ShapeDtypeStruct((t, D_MODEL), F32),
                   jax.ShapeDtypeStruct((1, D_MODEL), F32)],
        compiler_params=_ARB1,
    )(x, g, target)


def _rot(v, cos, sa, sb):
    return v * cos + pltpu.roll(v, 96, 1) * sa + pltpu.roll(v, 32, 1) * sb


def _rot_t(d, cos, sa, sb):
    return d * cos + pltpu.roll(d * sa, 32, 1) + pltpu.roll(d * sb, 96, 1)


def _bmm(a, b):
    return jnp.einsum("cik,ckj->cij", a, b, preferred_element_type=F32)


def _bmm_nt(a, b):
    return jnp.einsum("cik,cjk->cij", a, b, preferred_element_type=F32)


def _bmm_tn(a, b):
    return jnp.einsum("cki,ckj->cij", a, b, preferred_element_type=F32)


def _masked_sum(mask, x):
    hi = x.astype(BF16)
    r1 = x - hi.astype(F32)
    mid = r1.astype(BF16)
    lo = (r1 - mid.astype(F32)).astype(BF16)
    return _bmm(mask, hi) + _bmm(mask, mid) + _bmm(mask, lo)


def _tile_inputs(is_ret, qkvg_refs, aux, nc):
    shape3 = (nc, CHUNK, LANES)
    q_raw, k_raw, v, gate = (r[...] for r in qkvg_refs)
    ri = lax.broadcasted_iota(jnp.int32, (nc, CHUNK, CHUNK), 1)
    ci = lax.broadcasted_iota(jnp.int32, (nc, CHUNK, CHUNK), 2)
    if is_ret:
        cos_ref, sa_ref, sb_ref, lg_ref = aux
        cos, sa, sb = cos_ref[...], sa_ref[...], sb_ref[...]
        q = _rot(q_raw, cos, sa, sb)
        k = _rot(k_raw, cos, sa, sb) * QK_SCALE
        steps = (lax.broadcasted_iota(jnp.int32, shape3, 1) + 1).astype(F32)
        b = steps * lg_ref[...]
        logit = None
    else:
        glow_ref, wa2_ref, ba_ref = aux
        logit = _dot(glow_ref[...].astype(BF16), wa2_ref[...]) + ba_ref[...]
        la = (jnp.minimum(logit, 0.0) - jnp.log1p(jnp.exp(-jnp.abs(logit)))) * (1.0 / GATE_NORM)
        b = _masked_sum((ci <= ri).astype(BF16), la.reshape(shape3))
        q = q_raw * QK_SCALE
        k = k_raw
    return q.reshape(shape3), k.reshape(shape3), v.reshape(shape3), gate, b, logit, ri, ci


def _tile_scores(q, k, b, ri, ci):
    mid = b[:, CHUNK // 2 - 1:CHUNK // 2, :]
    ep = jnp.exp(b - mid)
    en = jnp.exp(mid - b)
    qt, kt, qh, kh = q * ep, k * en, q * en, k * ep
    low = _bmm_nt(qt.astype(BF16), kt.astype(BF16))
    upp = _bmm_nt(qh.astype(BF16), kh.astype(BF16))
    scores = jnp.where(ci <= ri, low, upp)
    return scores, ep, en, qt, kt, qh, kh


def _attn_specs(is_ret, t, tb, imap_t):
    nb = t // tb
    base = 0 if is_ret else 4 * HEADS
    proj = [pl.BlockSpec((tb, LANES), lambda h, i, s=sec: (imap_t(i), base + HEADS * s + h)) for sec in range(4)]
    lane_t = pl.BlockSpec((tb, LANES), lambda h, i: (imap_t(i), 0))
    if is_ret:
        aux = [lane_t, lane_t, lane_t, pl.BlockSpec((None, 1, LANES), lambda h, i: (h, 0, 0))]
    else:
        aux = [pl.BlockSpec((tb, LANES), lambda h, i: (imap_t(i), PROJ_P // LANES - 1)),
               pl.BlockSpec((LANES, LANES), lambda h, i: (0, h)),
               pl.BlockSpec((1, LANES), lambda h, i: (0, h))]
    gain = pl.BlockSpec((1, LANES), lambda h, i: (0, h))
    head_t = pl.BlockSpec((tb, LANES), lambda h, i: (imap_t(i), h))
    state = pl.BlockSpec((None, tb // CHUNK, LANES, LANES), lambda h, i: (h, imap_t(i), 0, 0))
    return nb, proj, aux, gain, head_t, state


def _attn_fwd(is_ret, proj, aux_arrays, gain, name, hosted=None):
    t = proj.shape[0]
    tb = min(t, ATTN_TILE)
    nc = tb // CHUNK
    n_aux = 4 if is_ret else 3
    nb, proj_spec, aux_specs, gain_spec, head_t, state_spec = _attn_specs(is_ret, t, tb, lambda i: i)

    def body(*refs):
        qkvg_refs = refs[0:4]
        aux = refs[4:4 + n_aux]
        gn_ref, ofin_ref, oraw_ref, st_ref, state = refs[4 + n_aux:]

        @pl.when(pl.program_id(1) == 0)
        def _():
            state[...] = jnp.zeros_like(state)

        q, k, v, gate, b, _, ri, ci = _tile_inputs(is_ret, qkvg_refs, aux, nc)
        scores = _tile_scores(q, k, b, ri, ci)[0]
        vb = v.astype(BF16)
        intra = _bmm(scores.astype(BF16), vb)
        b_last = b[:, CHUNK - 1:CHUNK, :]
        e_last = jnp.exp(b_last)
        grow = _bmm_tn(vb, (k * jnp.exp(b_last - b)).astype(BF16))
        st = state[...]
        for c in range(nc):
            st_ref[c] = st
            st = st * e_last[c] + grow[c]
        state[...] = st
        inter = _bmm_nt((q * jnp.exp(b)).astype(BF16), st_ref[...].astype(BF16))
        out = (intra + inter).reshape(tb, LANES)
        oraw_ref[...] = out
        normed = out * _rms_scale(out)
        ofin_ref[...] = ((normed * gn_ref[...]) * (gate * jax.nn.sigmoid(gate))).astype(BF16)

    width = HEADS * LANES
    return _call(
        body, (proj, proj, proj, proj, *aux_arrays, gain), name=name, grid=(HEADS, nb),
        in_specs=proj_spec + aux_specs + [gain_spec],
        out_specs=[head_t, head_t, state_spec],
        out_shape=[jax.ShapeDtypeStruct((t, width), BF16), jax.ShapeDtypeStruct((t, width), F32),
                   jax.ShapeDtypeStruct((HEADS, t // CHUNK, LANES, LANES), F32)],
        scratch_shapes=[pltpu.VMEM((LANES, LANES), F32)],
        compiler_params=_ARB2, hosted=hosted)


def _attn_bwd(is_ret, proj, aux_arrays, gain, o_raw, states, d_out, name):
    t = proj.shape[0]
    tb = min(t, ATTN_TILE)
    nc = tb // CHUNK
    n_aux = 4 if is_ret else 3
    nblk = t // tb
    nb, proj_spec, aux_specs, gain_spec, head_t, state_spec = _attn_specs(is_ret, t, tb, lambda i: nblk - 1 - i)
    base = 0 if is_ret else HEADS
    dout_spec = pl.BlockSpec((tb, LANES), lambda h, i: (nblk - 1 - i, base + h))

    def body(*refs):
        qkvg_refs = refs[0:4]
        aux = refs[4:4 + n_aux]
        gn_ref, oraw_ref, st_ref, dfin_ref = refs[4 + n_aux:8 + n_aux]
        dq_ref, dk_ref, dv_ref, dgate_ref, dgn_ref = refs[8 + n_aux:13 + n_aux]
        if is_ret:
            dstate, dafter_ref = refs[13 + n_aux:]
        else:
            dlogit_ref, dba_ref, dstate, dafter_ref = refs[13 + n_aux:]

        @pl.when(pl.program_id(1) == 0)
        def _():
            dstate[...] = jnp.zeros_like(dstate)
            dgn_ref[...] = jnp.zeros_like(dgn_ref)
            if not is_ret:
                dba_ref[...] = jnp.zeros_like(dba_ref)

        shape3 = (nc, CHUNK, LANES)
        q, k, v, gate, b, logit, ri, ci = _tile_inputs(is_ret, qkvg_refs, aux, nc)
        scores, ep, en, qt, kt, qh, kh = _tile_scores(q, k, b, ri, ci)
        eb = jnp.exp(b)
        qe = q * eb
        b_last = b[:, CHUNK - 1:CHUNK, :]
        e_last = jnp.exp(b_last)
        ekd = jnp.exp(b_last - b)
        kd = k * ekd

        gn = gn_ref[...]
        out = oraw_ref[...]
        r = _rms_scale(out)
        normed = out * r
        sg = jax.nn.sigmoid(gate)
        dfin = dfin_ref[...]
        dgate = dfin * (normed * gn) * _silu_grad(gate, sg)
        dpre = dfin * (gate * sg)
        dgn_ref[...] += jnp.sum(dpre * normed, axis=0, keepdims=True)
        dnormed = dpre * gn
        d_o = r * (dnormed - normed * jnp.mean(dnormed * normed, axis=-1, keepdims=True))
        dob, vb = d_o.reshape(shape3).astype(BF16), v.astype(BF16)

        dgrow = _bmm_tn(dob, qe.astype(BF16))
        dst = dstate[...]
        for c in reversed(range(nc)):
            dafter_ref[c] = dst
            dst = dst * e_last[c] + dgrow[c]
        dstate[...] = dst
        st = st_ref[...]
        dafter = dafter_ref[...]
        stb, dafter_b = st.astype(BF16), dafter.astype(BF16)

        qtb, ktb, qhb, khb = qt.astype(BF16), kt.astype(BF16), qh.astype(BF16), kh.astype(BF16)
        scores_t = jnp.where(ci >= ri, _bmm_nt(ktb, qtb), _bmm_nt(khb, qhb))
        dv = _bmm(scores_t.astype(BF16), dob) + _bmm_nt(kd.astype(BF16), dafter_b)
        dsc = _bmm_nt(dob, vb)
        dsc_t = _bmm_nt(vb, dob)
        dqe = _bmm(dob, stb)
        dkd = _bmm(vb, dafter_b)
        dqt = _bmm(jnp.where(ci <= ri, dsc, 0.0).astype(BF16), ktb)
        dqh = _bmm(jnp.where(ci <= ri, 0.0, dsc).astype(BF16), khb)
        dkt = _bmm(jnp.where(ci >= ri, dsc_t, 0.0).astype(BF16), qtb)
        dkh = _bmm(jnp.where(ci >= ri, 0.0, dsc_t).astype(BF16), qhb)
        dq = (dqt * ep + dqh * en + dqe * eb).reshape(tb, LANES)
        dk = (dkt * en + dkh * ep + dkd * ekd).reshape(tb, LANES)

        if is_ret:
            cos_ref, sa_ref, sb_ref, _ = aux
            cos, sa, sb = cos_ref[...], sa_ref[...], sb_ref[...]
            dq_raw = _rot_t(dq, cos, sa, sb)
            dk_raw = _rot_t(dk, cos, sa, sb) * QK_SCALE
        else:
            dq_raw = dq * QK_SCALE
            dk_raw = dk
            db = dqt * qt - dkt * kt - dqh * qh + dkh * kh + dqe * qe - dkd * kd
            db_last = (jnp.sum(dkd * kd, axis=1, keepdims=True)
                       + jnp.sum(dafter * st, axis=1, keepdims=True) * e_last)
            last_row = lax.broadcasted_iota(jnp.int32, shape3, 1) == CHUNK - 1
            db = db + jnp.where(last_row, db_last, 0.0)
            dla = _masked_sum((ci >= ri).astype(BF16), db).reshape(tb, LANES)
            dlogit = dla * (1.0 / GATE_NORM) * jax.nn.sigmoid(-logit)
            dlogit_ref[...] = dlogit.astype(BF16)
            dba_ref[...] += jnp.sum(dlogit, axis=0, keepdims=True)

        dq_ref[...] = dq_raw.astype(BF16)
        dk_ref[...] = dk_raw.astype(BF16)
        dv_ref[...] = dv.reshape(tb, LANES).astype(BF16)
        dgate_ref[...] = dgate.astype(BF16)

    width = HEADS * LANES
    row_out = pl.BlockSpec((1, LANES), lambda h, i: (0, h))
    out_specs = [head_t] * 4 + [row_out]
    out_shape = [jax.ShapeDtypeStruct((t, width), BF16)] * 4 + [jax.ShapeDtypeStruct((1, width), F32)]
    if not is_ret:
        out_specs += [head_t, row_out]
        out_shape += [jax.ShapeDtypeStruct((t, width), BF16), jax.ShapeDtypeStruct((1, width), F32)]
    return pl.pallas_call(
        body, name=name, grid=(HEADS, nblk),
        in_specs=proj_spec + aux_specs + [gain_spec, head_t, state_spec, dout_spec],
        out_specs=out_specs, out_shape=out_shape,
        scratch_shapes=[pltpu.VMEM((LANES, LANES), F32), pltpu.VMEM((nc, LANES, LANES), F32)],
        compiler_params=_ARB2,
    )(proj, proj, proj, proj, *aux_arrays, gain, o_raw, states, d_out)


def _place():
    x, y, c = lax.axis_index("x"), lax.axis_index("y"), lax.axis_index("c")
    chips = [(1 - x, y), (x, 1 - y), (1 - x, 1 - y)]
    return x, y, c, 2 * x + y, chips


def _gather_plan(arrs):
    na = len(arrs)

    def copies(ins, outs, send_sems, recv_sems):
        x, y, c, me, chips = _place()

        def ici(a, j, src_chip, to):
            return pltpu.make_async_remote_copy(
                src_ref=ins[a].at[:, c], dst_ref=outs[a].at[src_chip, :, c],
                send_sem=send_sems.at[6 * a + j], recv_sem=recv_sems.at[6 * a + j], device_id=to, device_id_type=MESH)

        def d2d(a, j, src_chip, half):
            blk = outs[a].at[src_chip, :, half]
            return pltpu.make_async_remote_copy(
                src_ref=blk, dst_ref=blk, send_sem=send_sems.at[6 * a + 3 + j], recv_sem=recv_sems.at[6 * a + 3 + j],
                device_id=(x, y, 1 - c), device_id_type=MESH)

        peers = [(a, j, px, py) for a in range(na) for j, (px, py) in enumerate(chips)]
        return c, me, peers, ici, d2d

    def start(*refs):
        c, me, peers, ici, _ = copies(*refs)
        for a, j, px, py in peers:
            ici(a, j, me, (px, py, c)).start()

    def finish(*refs):
        c, me, peers, ici, d2d = copies(*refs)
        for a, j, px, py in peers:
            ici(a, j, 2 * px + py, (px, py, c)).wait_recv()
            d2d(a, j, 2 * px + py, c).start()
        for a, j, px, py in peers:
            d2d(a, j, 2 * px + py, 1 - c).wait_recv()
        for a, j, px, py in peers:
            ici(a, j, me, (px, py, c)).wait_send()
            d2d(a, j, 2 * px + py, c).wait_send()

    return _Hosted(arrs, [jax.ShapeDtypeStruct((N_CHIPS,) + a.shape, a.dtype) for a in arrs], 6 * na, start, finish)


def _pair_exchange(grads, name):
    na = len(grads)

    def body(*refs):
        ins, outs = refs[:na], refs[na:2 * na]
        send_sems, recv_sems = refs[2 * na:]
        x, y, c, _, _ = _place()
        copies = [pltpu.make_async_remote_copy(
            src_ref=ins[a].at[:, 1 - c], dst_ref=outs[a], send_sem=send_sems.at[a], recv_sem=recv_sems.at[a],
            device_id=(x, y, 1 - c), device_id_type=MESH) for a in range(na)]
        for cp in copies:
            cp.start()
        for cp in copies:
            cp.wait()

    return pl.pallas_call(
        body, name=name,
        in_specs=[ANY] * na, out_specs=[ANY] * na,
        out_shape=[jax.ShapeDtypeStruct(g.shape[:1] + g.shape[2:], g.dtype) for g in grads],
        scratch_shapes=[pltpu.SemaphoreType.DMA((na,)), pltpu.SemaphoreType.DMA((na,))],
    )(*grads)


def _pair_add(grad, recv, c_arr, name):
    _, _, r, cols = grad.shape

    def body(c_ref, g_ref, r_ref, o_ref):
        o_ref[...] = (g_ref[...].astype(F32) + r_ref[...].astype(F32)).astype(BF16)

    return pl.pallas_call(
        body, name=name,
        grid_spec=pltpu.PrefetchScalarGridSpec(
            num_scalar_prefetch=1, grid=(N_CHIPS,),
            in_specs=[pl.BlockSpec((None, None, r, cols), lambda p, c_ref: (p, c_ref[0], 0, 0)),
                      pl.BlockSpec((None, r, cols), lambda p, c_ref: (p, 0, 0))],
            out_specs=pl.BlockSpec((None, r, cols), lambda p, c_ref: (p, 0, 0))),
        out_shape=jax.ShapeDtypeStruct((N_CHIPS, r, cols), BF16),
        compiler_params=_ARB1,
    )(c_arr, grad, recv)


def _chip_exchange_plan(sums):
    na = len(sums)

    def copies(ins, outs, send_sems, recv_sems):
        x, y, c, me, chips = _place()

        def copy(a, j, px, py, block, slot):
            return pltpu.make_async_remote_copy(
                src_ref=ins[a].at[block], dst_ref=outs[a].at[slot],
                send_sem=send_sems.at[3 * a + j], recv_sem=recv_sems.at[3 * a + j],
                device_id=(px, py, c), device_id_type=MESH)

        peers = [(a, j, px, py) for a in range(na) for j, (px, py) in enumerate(chips)]
        return me, peers, copy

    def start(*refs):
        me, peers, copy = copies(*refs)
        for a, j, px, py in peers:
            copy(a, j, px, py, 2 * px + py, me).start()

    def finish(*refs):
        me, peers, copy = copies(*refs)
        for a, j, px, py in peers:
            copy(a, j, px, py, me, 2 * px + py).wait_recv()
        for a, j, px, py in peers:
            copy(a, j, px, py, 2 * px + py, me).wait_send()

    return _Hosted(sums, [jax.ShapeDtypeStruct(s.shape, s.dtype) for s in sums], 3 * na, start, finish)


def _chip_sum(own, recv, me_arr, name):
    _, r, cols = recv.shape

    def body(me_ref, own_ref, r_ref, o_ref):
        o_ref[...] = jnp.zeros_like(o_ref)
        for q in range(N_CHIPS):
            @pl.when(me_ref[0] == q)
            def _():
                o_ref[...] += own_ref[...].astype(F32)

            @pl.when(me_ref[0] != q)
            def _():
                o_ref[...] += r_ref[q].astype(F32)

    return pl.pallas_call(
        body, name=name,
        grid_spec=pltpu.PrefetchScalarGridSpec(
            num_scalar_prefetch=1, grid=(1,),
            in_specs=[pl.BlockSpec((None, r, cols), lambda i, me_ref: (me_ref[0], 0, 0)),
                      pl.BlockSpec((N_CHIPS, r, cols), lambda i, me_ref: (0, 0, 0))],
            out_specs=pl.BlockSpec((r, cols), lambda i, me_ref: (0, 0))),
        out_shape=jax.ShapeDtypeStruct((r, cols), F32),
        compiler_params=_ARB1,
    )(me_arr, own, recv)


def _pair_share(halves):
    na = len(halves)

    def body(*refs):
        ins, outs = refs[:na], refs[na:2 * na]
        send_sems, recv_sems = refs[2 * na:]
        x, y, c, _, _ = _place()
        copies = [pltpu.make_async_remote_copy(
            src_ref=ins[a], dst_ref=outs[a], send_sem=send_sems.at[a], recv_sem=recv_sems.at[a],
            device_id=(x, y, 1 - c), device_id_type=MESH) for a in range(na)]
        for cp in copies:
            cp.start()
        for cp in copies:
            cp.wait()

    return pl.pallas_call(
        body, name="pair_share",
        in_specs=[ANY] * na, out_specs=[ANY] * na,
        out_shape=[jax.ShapeDtypeStruct(h.shape, h.dtype) for h in halves],
        scratch_shapes=[pltpu.SemaphoreType.DMA((na,)), pltpu.SemaphoreType.DMA((na,))],
    )(*halves)


def _small_allreduce(block):
    m, n = block.shape

    def body(x_ref, all_ref, sum_ref, send_sems, recv_sems, local_sem):
        x, y, c, _, chips = _place()
        me, sibling = (x, y, c), (x, y, 1 - c)

        def rows(px, py, pc):
            return all_ref.at[pl.ds((4 * px + 2 * py + pc) * m, m), :]

        def copy(k, blk, to, src=None):
            return pltpu.make_async_remote_copy(
                src_ref=rows(*blk) if src is None else src, dst_ref=rows(*blk),
                send_sem=send_sems.at[k], recv_sem=recv_sems.at[k], device_id=to, device_id_type=MESH)

        mine = pltpu.make_async_copy(x_ref, rows(*me), local_sem)
        mine.start()
        first = [copy(0, me, sibling, src=x_ref)]
        first += [copy(1 + j, me, (*chip, c), src=x_ref) for j, chip in enumerate(chips)]
        for cp in first:
            cp.start()
        passed = [copy(4 + j, (*chip, c), sibling) for j, chip in enumerate(chips)]
        for j, chip in enumerate(chips):
            copy(1 + j, (*chip, c), me).wait_recv()
            passed[j].start()
        copy(0, sibling, me).wait_recv()
        for j, chip in enumerate(chips):
            copy(4 + j, (*chip, 1 - c), me).wait_recv()
        for cp in first + passed:
            cp.wait_send()
        mine.wait()
        acc = all_ref[0:m, :]
        for d in range(1, 8):
            acc = acc + all_ref[d * m:(d + 1) * m, :]
        sum_ref[...] = acc

    vmem = pl.BlockSpec(memory_space=pltpu.VMEM)
    return pl.pallas_call(
        body, name="small_allreduce",
        in_specs=[vmem], out_specs=[vmem, vmem],
        out_shape=[jax.ShapeDtypeStruct((8 * m, n), F32), jax.ShapeDtypeStruct((m, n), F32)],
        scratch_shapes=[pltpu.SemaphoreType.DMA((7,)), pltpu.SemaphoreType.DMA((7,)), pltpu.SemaphoreType.DMA],
    )(block)[1]


def _row_tile(rows):
    best = rows
    for cand in range(8, min(rows, 512) + 1, 8):
        if rows % cand == 0:
            best = cand
    return best


def _adamw_math(w, g, m, v):
    m2 = ADAM_B1 * m + (1.0 - ADAM_B1) * g
    v2 = ADAM_B2 * v + (1.0 - ADAM_B2) * (g * g)
    m_hat = m2 / (1.0 - ADAM_B1 ** ADAM_STEP)
    v_hat = v2 / (1.0 - ADAM_B2 ** ADAM_STEP)
    return -ADAM_LR * (m_hat / (jnp.sqrt(v_hat) + ADAM_EPS) + ADAM_WD * w), m2, v2


def _adamw_halves(w, g_mine, g_other, m, v, c_arr, name):
    rows, cols = w.shape
    r = rows // 2
    tr = _row_tile(r)
    nt = r // tr

    def body(c_ref, w_ref, gm_ref, go_ref, m_ref, v_ref, g_ref, d_ref, nm_ref, nv_ref):
        gv = jnp.where(pl.program_id(0) == c_ref[0], gm_ref[...], go_ref[...])
        g_ref[...] = gv
        d_ref[...], nm_ref[...], nv_ref[...] = _adamw_math(w_ref[...], gv, m_ref[...], v_ref[...])

    full = pl.BlockSpec((tr, cols), lambda h, i, c_ref: (h * nt + i, 0))
    half = pl.BlockSpec((tr, cols), lambda h, i, c_ref: (i, 0))
    shape = jax.ShapeDtypeStruct((rows, cols), F32)
    return pl.pallas_call(
        body, name=name,
        grid_spec=pltpu.PrefetchScalarGridSpec(
            num_scalar_prefetch=1, grid=(2, nt),
            in_specs=[full, half, half, full, full], out_specs=[full] * 4),
        out_shape=[shape] * 4,
        compiler_params=_ARB2,
    )(c_arr, w, g_mine, g_other, m, v)


def _adamw(w, g, m, v, name):
    rows, cols = w.shape
    tr = _row_tile(rows)

    def body(w_ref, g_ref, m_ref, v_ref, d_ref, nm_ref, nv_ref):
        d_ref[...], nm_ref[...], nv_ref[...] = _adamw_math(w_ref[...], g_ref[...], m_ref[...], v_ref[...])

    spec = pl.BlockSpec((tr, cols), lambda i: (i, 0))
    shape = jax.ShapeDtypeStruct((rows, cols), F32)
    return pl.pallas_call(
        body, name=name, grid=(rows // tr,),
        in_specs=[spec] * 4, out_specs=[spec] * 3, out_shape=[shape] * 3,
        compiler_params=_ARB1,
    )(w, g, m, v)


def _pad_w_in_t(w_in_t):
    def heads_padded(sec):
        return jnp.pad(sec.reshape(HEADS, 64, -1), ((0, 0), (0, LANES - 64), (0, 0))).reshape(HEADS * LANES, -1)

    w = w_in_t
    return jnp.concatenate([
        heads_padded(w[0:256]), heads_padded(w[256:512]), w[512:1536],
        heads_padded(w[1536:1792]), heads_padded(w[1792:2048]), w[2048:3072],
        jnp.pad(w[3072:3088], ((0, LANES - GATE_RANK), (0, 0)))], axis=0)


def _unpad_w_in_t(w_pt):
    def heads_unpadded(sec):
        return sec.reshape(HEADS, LANES, -1)[:, 0:64].reshape(HEADS * 64, -1)

    p = w_pt
    return jnp.concatenate([
        heads_unpadded(p[0:512]), heads_unpadded(p[512:1024]), p[1024:2048],
        heads_unpadded(p[2048:2560]), heads_unpadded(p[2560:3072]), p[3072:4096],
        p[4096:4096 + GATE_RANK]], axis=0)


def _rope_tables(t):
    half = 32
    inv = ROPE_BASE ** (-jnp.arange(half, dtype=F32) * 2.0 / 64)
    ang = jnp.arange(t, dtype=F32)[:, None] * inv[None, :]
    cos, sin = jnp.cos(ang), jnp.sin(ang)
    z32, z64 = jnp.zeros((t, 32), F32), jnp.zeros((t, 64), F32)
    return (jnp.concatenate([cos, cos, z64], axis=1),
            jnp.concatenate([-sin, z32, z64], axis=1),
            jnp.concatenate([z32, sin, z64], axis=1))


def _halves(w):
    n, rows, cols = w.shape
    return w.reshape(n, 2, rows // 2, cols)


_VMEM = pl.BlockSpec(memory_space=pltpu.VMEM)


def _pack_small(n1, nm, n2, nf, nret, ngla, ba, wa2_p, loss_blk):
    def body(n1_ref, nm_ref, n2_ref, nf_ref, nret_ref, ngla_ref, ba_ref, wa2_ref, loss_ref, o_ref):
        o_ref[...] = jnp.zeros_like(o_ref)
        o_ref[0:1, :] = n1_ref[...]
        o_ref[1:2, :] = nm_ref[...]
        o_ref[2:3, :] = n2_ref[...]
        o_ref[3:4, :] = nf_ref[...]
        o_ref[4:5, 0:512] = nret_ref[...]
        o_ref[4:5, 512:1024] = ngla_ref[...]
        o_ref[5:6, 0:256] = ba_ref[...]
        o_ref[6:7, 0:LANES] = loss_ref[0:1, :]
        o_ref[8:8 + GATE_RANK, 0:HEADS * LANES] = wa2_ref[0:GATE_RANK, :]

    return pl.pallas_call(
        body, name="pack_small", in_specs=[_VMEM] * 9, out_specs=_VMEM,
        out_shape=jax.ShapeDtypeStruct((SMALL_ROWS, D_MODEL), F32),
    )(n1, nm, n2, nf, nret, ngla, ba, wa2_p, loss_blk)


def _small_update(summed, chip_arr, ws, ms, vs):
    n = len(ws)

    def body(chip_ref, s_ref, *refs):
        w_refs, m_refs, v_refs = refs[0:n], refs[n:2 * n], refs[2 * n:3 * n]
        outs = refs[3 * n:]
        wa2_all = s_ref[8:8 + GATE_RANK, 0:HEADS * LANES]
        wa2_g = jnp.zeros((GATE_RANK, 64), F32)
        for p in range(N_CHIPS):
            wa2_g = jnp.where(chip_ref[0] == p, wa2_all[:, LANES * p:LANES * p + 64], wa2_g)
        grads = [s_ref[0:1, :], s_ref[1:2, :], s_ref[2:3, :], s_ref[3:4, :], s_ref[4:5, 0:512],
                 s_ref[4:5, 512:1024], s_ref[5:6, 0:256], wa2_g]
        for k in range(n):
            d, m2, v2 = _adamw_math(w_refs[k][...], grads[k], m_refs[k][...], v_refs[k][...])
            outs[k][...] = grads[k]
            outs[n + k][...] = d
            outs[2 * n + k][...] = m2
            outs[3 * n + k][...] = v2

    shapes = [jax.ShapeDtypeStruct(w.shape, F32) for w in ws] * 4
    smem = pl.BlockSpec(memory_space=pltpu.SMEM)
    outs = pl.pallas_call(
        body, name="small_update", in_specs=[smem] + [_VMEM] * (1 + 3 * n), out_specs=[_VMEM] * (4 * n),
        out_shape=shapes,
    )(chip_arr, summed, *ws, *ms, *vs)
    return outs[0:n], outs[n:2 * n], outs[2 * n:3 * n], outs[3 * n:4 * n]


def _pad_in_rows(w_t):
    return jnp.pad(w_t, ((0, IN_ROWS - IN_SHARD), (0, 0)))


def _forward_backward(xs, target, ffn1_w, rest, ba_p, ffn1_norm_g, mix_norm_g, ret_norm_g, gla_norm_g, ffn2_norm_g,
                      final_norm_g, rest_plan=None, rest_weights=None, ffn2_plans=None, ffn2_weights=None,
                      early=None, late=None):
    t = xs.shape[0]
    cos_t, sa_t, sb_t = _rope_tables(t)
    log_gamma = jnp.log(1.0 - 2.0 ** (-5.0 - jnp.arange(HEADS, dtype=F32)))
    lg_t = jnp.broadcast_to(log_gamma[:, None, None], (HEADS, 1, LANES))
    ret_aux = [cos_t, sa_t, sb_t, lg_t]

    (x1, a1, u1, h1), gathered = _ffn_fwd(xs, ffn1_norm_g, ffn1_w, "ffn1_fwd", hosted=rest_plan)
    ffn2_w, w_in_pt, w_out_full, wa2_p = rest if rest_plan is None else rest_weights(gathered)
    plans = [None] * 3 if ffn2_plans is None else ffn2_plans
    (proj, h_mix), got_gate = _mixer_in_fwd(x1, mix_norm_g, w_in_pt, "mixer_in_fwd", hosted=plans[0])
    gla_aux = [proj, wa2_p, ba_p]
    (o_ret, raw_ret, st_ret), got_up = _attn_fwd(True, proj, ret_aux, ret_norm_g, "ret_fwd", hosted=plans[1])
    (o_gla, raw_gla, st_gla), got_down = _attn_fwd(False, proj, gla_aux, gla_norm_g, "gla_fwd", hosted=plans[2])
    if ffn2_plans is not None:
        ffn2_w = ffn2_weights(got_gate + got_up + got_down)
    x2 = _mixer_out_fwd(o_ret, o_gla, w_out_full, x1, "mixer_out_fwd")
    (x3, a2, u2, h2), _ = _ffn_fwd(x2, ffn2_norm_g, ffn2_w, "ffn2_fwd")
    loss_blk, dx3, d_final_g = _final_loss(x3, final_norm_g, target, "final_loss")

    (da2, du2, hid2, dob2, dx2, d_ffn2_g), _ = _ffn_bwd(dx3, x2, ffn2_norm_g, a2, u2, ffn2_w, "ffn2_bwd")
    g_gate2 = _matmul_tn(da2, h2, "ffn2_dgate", out_dtype=BF16)
    g_up2 = _matmul_tn(du2, h2, "ffn2_dup", out_dtype=BF16)
    g_down2 = _matmul_tn(hid2, dob2, "ffn2_ddown", out_dtype=BF16)

    d_o = _matmul_nt(dx2, w_out_full, "mixer_out_bwd")
    g_wout_ret = _matmul_tn(o_ret, dx2, "wout_grad_ret", out_dtype=BF16)
    g_wout_gla = _matmul_tn(o_gla, dx2, "wout_grad_gla", out_dtype=BF16)
    *dproj_ret, d_ret_g = _attn_bwd(True, proj, ret_aux, ret_norm_g, raw_ret, st_ret, d_o, "ret_bwd")
    *dproj_gla, d_gla_g, dlogit, d_ba_p = _attn_bwd(False, proj, gla_aux, gla_norm_g, raw_gla, st_gla, d_o, "gla_bwd")
    d_glow = _matmul_nt(dlogit, wa2_p, "gate_low_bwd", out_dtype=BF16)
    g_wa2_p = _matmul_tn(proj[:, PROJ_P - LANES:], dlogit, "gate_w_grad")
    dproj = jnp.concatenate(dproj_ret + dproj_gla + [d_glow], axis=1)
    g_win_p = _matmul_tn(dproj, h_mix, "w_in_grad", tka=PROJ_P // 3, out_dtype=BF16)
    dx1, d_mix_g = _mixer_in_bwd(dproj, w_in_pt, dx2, x1, mix_norm_g, "mixer_in_bwd")
    g_win_t = _unpad_w_in_t(g_win_p[0])
    g_win = jnp.stack([_pad_in_rows(g_win_t[IN_SHARD * p:IN_SHARD * (p + 1)]) for p in range(N_CHIPS)], axis=0)
    g_wout = jnp.concatenate([g_wout_ret[0], g_wout_gla[0]], axis=0).reshape(N_CHIPS, D_MODEL // N_CHIPS, D_MODEL)

    early_plan = None if early is None else early([g_gate2, g_up2, g_down2, g_win, g_wout])
    (da1, du1, hid1, dob1, grad_x, d_ffn1_g), arrived = _ffn_bwd(dx1, xs, ffn1_norm_g, a1, u1, ffn1_w, "ffn1_bwd",
                                                                hosted=early_plan)
    late_grads, late_arrived = [], []
    for lhs, rhs, name in ((da1, h1, "ffn1_dgate"), (du1, h1, "ffn1_dup"), (hid1, dob1, "ffn1_ddown")):
        plan = None if late is None or not late_grads else late(late_grads[-1], len(late_grads))
        res = _matmul_tn(lhs, rhs, name, out_dtype=BF16, hosted=plan)
        if plan is not None:
            res, carried = res
            late_arrived += carried
        late_grads.append(res)
    g_gate1, g_up1, g_down1 = late_grads

    return (loss_blk, grad_x, g_gate1, g_up1, g_down1, g_gate2, g_up2, g_down2, g_win, g_wout, g_wa2_p,
            d_ba_p, d_ffn1_g, d_mix_g, d_ffn2_g, d_final_g, d_ret_g, d_gla_g, arrived, late_arrived)


def kernel(x, ffn1_norm_g, ffn1_w_gate, ffn1_w_up, ffn1_w_down, mix_norm_g, w_in, ret_norm_g, gla_w_a2, gla_b_a, gla_norm_g, w_out, ffn2_norm_g, ffn2_w_gate, ffn2_w_up, ffn2_w_down, final_norm_g, loss_target, m_ffn1_norm_g, m_ffn1_w_gate, m_ffn1_w_up, m_ffn1_w_down, m_mix_norm_g, m_w_in, m_ret_norm_g, m_gla_w_a2, m_gla_b_a, m_gla_norm_g, m_w_out, m_ffn2_norm_g, m_ffn2_w_gate, m_ffn2_w_up, m_ffn2_w_down, m_final_norm_g, v_ffn1_norm_g, v_ffn1_w_gate, v_ffn1_w_up, v_ffn1_w_down, v_mix_norm_g, v_w_in, v_ret_norm_g, v_gla_w_a2, v_gla_b_a, v_gla_norm_g, v_w_out, v_ffn2_norm_g, v_ffn2_w_gate, v_ffn2_w_up, v_ffn2_w_down, v_final_norm_g):
    t = x.shape[1]
    xs = x.reshape(t, D_MODEL)
    target = loss_target.reshape(t, D_MODEL)
    chip = 2 * lax.axis_index("x") + lax.axis_index("y")
    c_arr = lax.axis_index("c").astype(jnp.int32).reshape(1)

    me_arr = chip.astype(jnp.int32).reshape(1)

    pad_rows = _pad_in_rows

    def own_block(gathered, shard):
        return lax.dynamic_update_slice(gathered, shard[None], (chip,) + (0,) * shard.ndim)

    ffn1_shard = _halves(jnp.stack([ffn1_w_gate[0].T, ffn1_w_up[0].T, ffn1_w_down[0]], axis=0).astype(BF16))
    rest_shards = [_halves(pad_rows(w_in[0].T).astype(BF16)[None]),
                   _halves(w_out.astype(BF16)),
                   jnp.concatenate([gla_w_a2.reshape(GATE_RANK, 64), jnp.zeros((GATE_RANK, 64), F32)],
                                   axis=1).reshape(1, 2, 8, LANES)]
    ffn2_shards = [_halves(w.astype(BF16)[None]) for w in (ffn2_w_gate[0].T, ffn2_w_up[0].T, ffn2_w_down[0])]
    ffn1_all = _run_hosted(_gather_plan([ffn1_shard]), "gather_ffn1")[0]
    ffn1_w = own_block(ffn1_all, ffn1_shard).reshape(N_CHIPS, 3, FF_SHARD, D_MODEL)

    def rest_weights(gathered):
        win_all, wout_all, wa2_all = [own_block(g, s) for g, s in zip(gathered, rest_shards)]
        win_t = win_all.reshape(N_CHIPS, IN_ROWS, D_MODEL)
        w_in_pt = _pad_w_in_t(jnp.concatenate([win_t[p, 0:IN_SHARD] for p in range(N_CHIPS)], axis=0))
        wa2_p = jnp.pad(
            wa2_all.reshape(N_CHIPS, GATE_RANK, LANES).transpose(1, 0, 2).reshape(GATE_RANK, HEADS * LANES),
            ((0, LANES - GATE_RANK), (0, 0))).astype(BF16)
        return (None, w_in_pt, wout_all.reshape(D_MODEL, D_MODEL), wa2_p)

    def ffn2_weights(gathered):
        return [own_block(g, s).reshape(N_CHIPS, FF_SHARD, D_MODEL) for g, s in zip(gathered, ffn2_shards)]

    def pair_sums(grads, tag):
        halves = [g.reshape(g.shape[0], 2, g.shape[1] // 2, g.shape[2]) for g in grads]
        recv = _pair_exchange(halves, "pair_exchange_" + tag)
        return [_pair_add(g, r, c_arr, "pair_add_%s%d" % (tag, k)) for k, (g, r) in enumerate(zip(halves, recv))]

    early_sums = []

    def early(grads):
        early_sums.extend(pair_sums(grads, "early"))
        return _chip_exchange_plan(early_sums)

    late_sums = []

    def late(grad, number):
        late_sums.extend(pair_sums([grad], "late%d" % number))
        return _chip_exchange_plan(late_sums[-1:])

    ba_p = jnp.pad(gla_b_a.reshape(HEADS, 64), ((0, 0), (0, 64))).reshape(1, HEADS * LANES)
    fb = _forward_backward(xs, target, ffn1_w, None, ba_p, ffn1_norm_g, mix_norm_g, ret_norm_g, gla_norm_g,
                           ffn2_norm_g, final_norm_g.reshape(1, D_MODEL), rest_plan=_gather_plan(rest_shards),
                           rest_weights=rest_weights, ffn2_plans=[_gather_plan([s]) for s in ffn2_shards],
                           ffn2_weights=ffn2_weights, early=early, late=late)
    (loss_blk, grad_x, _, _, g_down1, _, _, _, _, _, g_wa2_p,
     d_ba_p, d_ffn1_g, d_mix_g, d_ffn2_g, d_final_g, d_ret_g, d_gla_g, early_arrived, late_arrived) = fb
    late_arrived = late_arrived + _run_hosted(late(g_down1, 3), "chip_exchange_late")
    sums, arrived = late_sums + early_sums, late_arrived + early_arrived
    mine = [_chip_sum(s, r, me_arr, "chip_sum_%d" % k) for k, (s, r) in enumerate(zip(sums, arrived))]
    other = _pair_share(mine)

    d_ba = d_ba_p.reshape(HEADS, LANES)[:, 0:64].reshape(1, 256)
    small_local = _pack_small(d_ffn1_g, d_mix_g, d_ffn2_g, d_final_g, d_ret_g, d_gla_g, d_ba, g_wa2_p[0], loss_blk)
    small_sum = _small_allreduce(small_local)
    loss = small_sum[6, 0]

    def rows(n1, nm, n2, nf, nret, ngla, ba, wa2):
        return [n1, nm, n2, nf.reshape(1, D_MODEL), nret, ngla, ba, wa2.reshape(GATE_RANK, 64)]

    small = _small_update(
        small_sum, me_arr,
        rows(ffn1_norm_g, mix_norm_g, ffn2_norm_g, final_norm_g, ret_norm_g, gla_norm_g, gla_b_a, gla_w_a2),
        rows(m_ffn1_norm_g, m_mix_norm_g, m_ffn2_norm_g, m_final_norm_g, m_ret_norm_g, m_gla_norm_g, m_gla_b_a,
             m_gla_w_a2),
        rows(v_ffn1_norm_g, v_mix_norm_g, v_ffn2_norm_g, v_final_norm_g, v_ret_norm_g, v_gla_norm_g, v_gla_b_a,
             v_gla_w_a2))
    s_grad, s_delta, s_m, s_v = [
        [*o[0:3], o[3].reshape(D_MODEL), *o[4:7], o[7].reshape(1, GATE_RANK, 64)] for o in small]

    def big(k, w, m, v, name, to_2d, from_2d):
        outs4 = _adamw_halves(to_2d(w), mine[k], other[k], to_2d(m), to_2d(v), c_arr, name)
        return [from_2d(z) for z in outs4]

    plain = (lambda w: w[0], lambda z: z[None])
    transposed = (lambda w: w[0].T, lambda z: z.T[None])
    in_proj = (lambda w: pad_rows(w[0].T), lambda z: z[0:IN_SHARD].T[None])
    r_g1 = big(0, ffn1_w_gate, m_ffn1_w_gate, v_ffn1_w_gate, "adamw_ffn1_gate", *transposed)
    r_u1 = big(1, ffn1_w_up, m_ffn1_w_up, v_ffn1_w_up, "adamw_ffn1_up", *transposed)
    r_d1 = big(2, ffn1_w_down, m_ffn1_w_down, v_ffn1_w_down, "adamw_ffn1_down", *plain)
    r_g2 = big(3, ffn2_w_gate, m_ffn2_w_gate, v_ffn2_w_gate, "adamw_ffn2_gate", *transposed)
    r_u2 = big(4, ffn2_w_up, m_ffn2_w_up, v_ffn2_w_up, "adamw_ffn2_up", *transposed)
    r_d2 = big(5, ffn2_w_down, m_ffn2_w_down, v_ffn2_w_down, "adamw_ffn2_down", *plain)
    r_in = big(6, w_in, m_w_in, v_w_in, "adamw_w_in", *in_proj)
    r_out = big(7, w_out, m_w_out, v_w_out, "adamw_w_out", *plain)

    def leaves(k, smalls):
        n1, nm, n2, nf, nret, ngla, ba, wa2 = smalls
        return [n1, r_g1[k], r_u1[k], r_d1[k], nm, r_in[k], nret, wa2, ba, ngla, r_out[k], n2, r_g2[k], r_u2[k], r_d2[k], nf]

    outs = [loss, grad_x.reshape(x.shape)]
    outs += leaves(0, s_grad) + leaves(1, s_delta) + leaves(2, s_m) + leaves(3, s_v)
    return tuple(outs)
```

```python
import functools

import jax
import jax.numpy as jnp
from jax import lax
from jax.experimental import pallas as pl
from jax.experimental.pallas import tpu as pltpu

F32, BF16 = jnp.float32, jnp.bfloat16
MESH = pl.DeviceIdType.MESH
ANY = pl.BlockSpec(memory_space=pl.ANY)

D_MODEL = 1024
D_FF = 2816
N_CHIPS = 4
FF_SHARD = D_FF // N_CHIPS
IN_WIDTH = 3088
IN_SHARD = IN_WIDTH // N_CHIPS
IN_ROWS = 800
CHUNK = 64
HEADS = 4
LANES = 128
PROJ_P = (2 * 4 * HEADS + 1) * LANES
GATE_RANK = 16
QK_SCALE = 0.125
GATE_NORM = 16.0
RMS_EPS = 1e-6
ROPE_BASE = 10000.0
ADAM_LR, ADAM_B1, ADAM_B2, ADAM_EPS, ADAM_WD, ADAM_STEP = 0.001, 0.9, 0.999, 1e-08, 0.01, 10
SMALL_ROWS = 32
TOKEN_TILE = 512
ATTN_TILE = 512

_ARB2 = pltpu.CompilerParams(dimension_semantics=("arbitrary", "arbitrary"))
_ARB1 = pltpu.CompilerParams(dimension_semantics=("arbitrary",))
_ARB3 = pltpu.CompilerParams(dimension_semantics=("arbitrary", "arbitrary", "arbitrary"))


def _dot(a, b):
    return jnp.dot(a, b, preferred_element_type=F32)


def _dot_nt(a, b):
    return lax.dot_general(a, b, (((1,), (1,)), ((), ())), preferred_element_type=F32)


def _dot_tn(a, b):
    return lax.dot_general(a, b, (((0,), (0,)), ((), ())), preferred_element_type=F32)


def _rms_scale(xv):
    return lax.rsqrt(jnp.mean(xv * xv, axis=-1, keepdims=True) + RMS_EPS)


def _rms_bwd(dh, xv, g):
    r = _rms_scale(xv)
    xhat = xv * r
    dxhat = dh * g
    dx = r * (dxhat - xhat * jnp.mean(dxhat * xhat, axis=-1, keepdims=True))
    return dx, jnp.sum(dh * xhat, axis=0, keepdims=True)


def _silu_grad(a, sg):
    return sg * (1.0 + a * (1.0 - sg))


class _Hosted:
    def __init__(self, arrays, out_shapes, n_sems, start, finish):
        self.arrays, self.out_shapes, self.n_sems = list(arrays), list(out_shapes), n_sems
        self.start, self.finish = start, finish


def _call(body, args, *, name, grid, in_specs, out_specs, out_shape, scratch_shapes, compiler_params, hosted=None):
    if hosted is None:
        outs = pl.pallas_call(body, name=name, grid=grid, in_specs=in_specs, out_specs=out_specs, out_shape=out_shape,
                              scratch_shapes=scratch_shapes, compiler_params=compiler_params)(*args)
        return list(outs), []
    n_in, n_out, n_sc, nh = len(in_specs), len(out_specs), len(scratch_shapes), len(hosted.arrays)

    def wrapped(*refs):
        ins, h_in = refs[:n_in], refs[n_in:n_in + nh]
        outs, h_out = refs[n_in + nh:n_in + nh + n_out], refs[n_in + nh + n_out:n_in + 2 * nh + n_out]
        rest = refs[n_in + 2 * nh + n_out:]
        scratch, (send_sems, recv_sems) = rest[:n_sc], rest[n_sc:]
        first = functools.reduce(jnp.logical_and, [pl.program_id(d) == 0 for d in range(len(grid))])
        last = functools.reduce(jnp.logical_and, [pl.program_id(d) == n - 1 for d, n in enumerate(grid)])

        @pl.when(first)
        def _():
            hosted.start(h_in, h_out, send_sems, recv_sems)

        body(*ins, *outs, *scratch)

        @pl.when(last)
        def _():
            hosted.finish(h_in, h_out, send_sems, recv_sems)

    sems = [pltpu.SemaphoreType.DMA((hosted.n_sems,)), pltpu.SemaphoreType.DMA((hosted.n_sems,))]
    outs = pl.pallas_call(
        wrapped, name=name, grid=grid, in_specs=list(in_specs) + [ANY] * nh, out_specs=list(out_specs) + [ANY] * nh,
        out_shape=list(out_shape) + hosted.out_shapes, scratch_shapes=list(scratch_shapes) + sems,
        compiler_params=compiler_params)(*args, *hosted.arrays)
    return list(outs[:n_out]), list(outs[n_out:])


def _run_hosted(hosted, name):
    nh = len(hosted.arrays)

    def body(*refs):
        h_in, h_out, (send_sems, recv_sems) = refs[:nh], refs[nh:2 * nh], refs[2 * nh:]
        hosted.start(h_in, h_out, send_sems, recv_sems)
        hosted.finish(h_in, h_out, send_sems, recv_sems)

    sems = [pltpu.SemaphoreType.DMA((hosted.n_sems,)), pltpu.SemaphoreType.DMA((hosted.n_sems,))]
    return list(pl.pallas_call(body, name=name, in_specs=[ANY] * nh, out_specs=[ANY] * nh,
                               out_shape=hosted.out_shapes, scratch_shapes=sems)(*hosted.arrays))


def _ffn_weight_operands(ffn_w, chunk_maps):
    if isinstance(ffn_w, (list, tuple)):
        specs = [pl.BlockSpec((None, FF_SHARD, D_MODEL), lambda *g, m=m: (m(*g), 0, 0)) for m in chunk_maps]
        return list(ffn_w), specs
    specs = [pl.BlockSpec((None, None, FF_SHARD, D_MODEL), lambda *g, m=m, k=kind: (m(*g), k, 0, 0))
             for kind, m in enumerate(chunk_maps)]
    return [ffn_w] * 3, specs


def _pipeline_items(steps):
    def cur(s):
        c = jnp.minimum(s, steps - 1)
        return c // N_CHIPS, c % N_CHIPS

    def prev(s):
        p = jnp.maximum(s - 1, 0)
        return p // N_CHIPS, p % N_CHIPS

    return cur, prev


def _ffn_fwd(x, g, ffn_w, name, hosted=None):
    t = x.shape[0]
    tm = min(t, TOKEN_TILE)

    def body(x_ref, g_ref, wg_ref, wu_ref, wd_ref, xo_ref, a_ref, u_ref, h_ref, acc_ref):
        j = pl.program_id(1)

        @pl.when(j == 0)
        def _():
            xv = x_ref[...]
            h_ref[...] = ((xv * _rms_scale(xv)) * g_ref[...]).astype(BF16)
            acc_ref[...] = jnp.zeros_like(acc_ref)

        h = h_ref[...]
        a = _dot_nt(h, wg_ref[...])
        u = _dot_nt(h, wu_ref[...])
        a_ref[...] = a.astype(BF16)
        u_ref[...] = u.astype(BF16)
        hid = (a * jax.nn.sigmoid(a)) * u
        acc_ref[...] += _dot(hid.astype(BF16), wd_ref[...])

        @pl.when(j == N_CHIPS - 1)
        def _():
            xo_ref[...] = x_ref[...] + 0.5 * acc_ref[...]

    tok = pl.BlockSpec((tm, D_MODEL), lambda i, j: (i, 0))
    act = pl.BlockSpec((None, tm, FF_SHARD), lambda i, j: (j, i, 0))
    w_arrays, weights = _ffn_weight_operands(ffn_w, [lambda i, j: j] * 3)
    return _call(
        body, (x, g, *w_arrays), name=name, grid=(t // tm, N_CHIPS),
        in_specs=[tok, pl.BlockSpec((1, D_MODEL), lambda i, j: (0, 0))] + weights,
        out_specs=[tok, act, act, tok],
        out_shape=[jax.ShapeDtypeStruct((t, D_MODEL), F32),
                   jax.ShapeDtypeStruct((N_CHIPS, t, FF_SHARD), BF16),
                   jax.ShapeDtypeStruct((N_CHIPS, t, FF_SHARD), BF16),
                   jax.ShapeDtypeStruct((t, D_MODEL), BF16)],
        scratch_shapes=[pltpu.VMEM((tm, D_MODEL), F32)],
        compiler_params=_ARB2, hosted=hosted)


def _ffn_bwd(dxo, x, g, a4, u4, ffn_w, name, hosted=None):
    t = x.shape[0]
    tm = min(t, TOKEN_TILE)
    steps = (t // tm) * N_CHIPS
    cur, prev = _pipeline_items(steps)


    def body(dxo_ref, dxo_prev_ref, x_ref, g_ref, a_ref, u_ref, wg_ref, wu_ref, wd_ref,
             da_ref, du_ref, hid_ref, dob_ref, dx_ref, dg_ref, acc_ref, da_slots, du_slots):
        s = pl.program_id(0)
        jc, jp = cur(s)[1], prev(s)[1]
        slot = s % 2

        @pl.when(s == 0)
        def _():
            dg_ref[...] = jnp.zeros_like(dg_ref)
            acc_ref[...] = jnp.zeros_like(acc_ref)
            da_slots[...] = jnp.zeros_like(da_slots)
            du_slots[...] = jnp.zeros_like(du_slots)

        @pl.when(jc == 0)
        def _():
            dob_ref[...] = (0.5 * dxo_ref[...]).astype(BF16)

        dhid = _dot_nt(dob_ref[...], wd_ref[...])
        a = a_ref[...].astype(F32)
        u = u_ref[...].astype(F32)
        sg = jax.nn.sigmoid(a)
        sl = a * sg
        hid_ref[...] = (sl * u).astype(BF16)
        du = (dhid * sl).astype(BF16)
        da = (dhid * u * _silu_grad(a, sg)).astype(BF16)
        du_ref[...] = du
        da_ref[...] = da
        acc_ref[...] += _dot(da_slots[1 - slot], wg_ref[...]) + _dot(du_slots[1 - slot], wu_ref[...])
        da_slots[slot] = da
        du_slots[slot] = du

        @pl.when((jp == N_CHIPS - 1) & (s > 0))
        def _():
            dx, dg = _rms_bwd(acc_ref[...], x_ref[...], g_ref[...])
            dx_ref[...] = dxo_prev_ref[...] + dx
            dg_ref[...] += dg
            acc_ref[...] = jnp.zeros_like(acc_ref)

    tok_cur = pl.BlockSpec((tm, D_MODEL), lambda s: (cur(s)[0], 0))
    tok_prev = pl.BlockSpec((tm, D_MODEL), lambda s: (prev(s)[0], 0))
    act = pl.BlockSpec((None, tm, FF_SHARD), lambda s: (cur(s)[1], cur(s)[0], 0))
    row = pl.BlockSpec((1, D_MODEL), lambda s: (0, 0))
    w_arrays, weights = _ffn_weight_operands(ffn_w, [lambda s: prev(s)[1], lambda s: prev(s)[1], lambda s: cur(s)[1]])
    act_shape = jax.ShapeDtypeStruct((N_CHIPS, t, FF_SHARD), BF16)
    return _call(
        body, (dxo, dxo, x, g, a4, u4, *w_arrays), name=name, grid=(steps + 1,),
        in_specs=[tok_cur, tok_prev, tok_prev, row, act, act] + weights,
        out_specs=[act, act, act, tok_cur, tok_prev, row],
        out_shape=[act_shape, act_shape, act_shape,
                   jax.ShapeDtypeStruct((t, D_MODEL), BF16),
                   jax.ShapeDtypeStruct((t, D_MODEL), F32),
                   jax.ShapeDtypeStruct((1, D_MODEL), F32)],
        scratch_shapes=[pltpu.VMEM((tm, D_MODEL), F32), pltpu.VMEM((2, tm, FF_SHARD), BF16),
                        pltpu.VMEM((2, tm, FF_SHARD), BF16)],
        compiler_params=_ARB1, hosted=hosted)


def _matmul_tn(a, b, name, tka=None, out_dtype=F32, hosted=None):
    a3, b3 = a.ndim == 3, b.ndim == 3
    nb = a.shape[0] if a3 else (b.shape[0] if b3 else 1)
    t, ka, n = a.shape[-2], a.shape[-1], b.shape[-1]
    tka = ka if tka is None else tka
    tk = min(t, 2 * TOKEN_TILE)
    nk = t // tk

    def body(a_ref, b_ref, o_ref, acc_ref):
        k = pl.program_id(2)

        @pl.when(k == 0)
        def _():
            acc_ref[...] = jnp.zeros_like(acc_ref)

        acc_ref[...] += _dot_tn(a_ref[...].astype(BF16), b_ref[...].astype(BF16))

        @pl.when(k == nk - 1)
        def _():
            o_ref[...] = acc_ref[...].astype(out_dtype)

    a_spec = (pl.BlockSpec((None, tk, tka), lambda i, j, k: (i, k, j)) if a3
              else pl.BlockSpec((tk, tka), lambda i, j, k: (k, j)))
    b_spec = (pl.BlockSpec((None, tk, n), lambda i, j, k: (i, k, 0)) if b3
              else pl.BlockSpec((tk, n), lambda i, j, k: (k, 0)))
    outs, carried = _call(
        body, (a, b), name=name, grid=(nb, ka // tka, t // tk),
        in_specs=[a_spec, b_spec],
        out_specs=[pl.BlockSpec((None, tka, n), lambda i, j, k: (i, j, 0))],
        out_shape=[jax.ShapeDtypeStruct((nb, ka, n), out_dtype)],
        scratch_shapes=[pltpu.VMEM((tka, n), F32)],
        compiler_params=_ARB3, hosted=hosted)
    return outs[0] if hosted is None else (outs[0], carried)


def _matmul_nt(a, w, name, out_dtype=F32):
    t, k = a.shape
    n = w.shape[0]
    tm = min(t, TOKEN_TILE)

    def body(a_ref, w_ref, o_ref):
        o_ref[...] = _dot_nt(a_ref[...].astype(BF16), w_ref[...]).astype(out_dtype)

    return pl.pallas_call(
        body, name=name, grid=(t // tm,),
        in_specs=[pl.BlockSpec((tm, k), lambda i: (i, 0)), pl.BlockSpec((n, k), lambda i: (0, 0))],
        out_specs=pl.BlockSpec((tm, n), lambda i: (i, 0)),
        out_shape=jax.ShapeDtypeStruct((t, n), out_dtype),
        compiler_params=_ARB1,
    )(a, w)


def _mixer_in_bwd(dproj, w_in_pt, dres, x, g, name):
    t, k = dproj.shape
    tm = min(t, TOKEN_TILE)

    def body(a_ref, w_ref, dres_ref, x_ref, g_ref, dx_ref, dg_ref):
        @pl.when(pl.program_id(0) == 0)
        def _():
            dg_ref[...] = jnp.zeros_like(dg_ref)

        dh = _dot(a_ref[...], w_ref[...])
        dx, dg = _rms_bwd(dh, x_ref[...], g_ref[...])
        dx_ref[...] = dres_ref[...] + dx
        dg_ref[...] += dg

    tok = pl.BlockSpec((tm, D_MODEL), lambda i: (i, 0))
    row = pl.BlockSpec((1, D_MODEL), lambda i: (0, 0))
    return pl.pallas_call(
        body, name=name, grid=(t // tm,),
        in_specs=[pl.BlockSpec((tm, k), lambda i: (i, 0)), pl.BlockSpec((k, D_MODEL), lambda i: (0, 0)), tok, tok, row],
        out_specs=[tok, row],
        out_shape=[jax.ShapeDtypeStruct((t, D_MODEL), F32), jax.ShapeDtypeStruct((1, D_MODEL), F32)],
        compiler_params=_ARB1,
    )(dproj, w_in_pt, dres, x, g)


def _mixer_in_fwd(x, g, w_in_pt, name, hosted=None):
    t = x.shape[0]
    tm = min(t, TOKEN_TILE)
    tn = PROJ_P // 3

    def body(x_ref, g_ref, w_ref, p_ref, h_ref):
        @pl.when(pl.program_id(1) == 0)
        def _():
            xv = x_ref[...]
            h_ref[...] = ((xv * _rms_scale(xv)) * g_ref[...]).astype(BF16)

        p_ref[...] = _dot_nt(h_ref[...], w_ref[...])

    tok = pl.BlockSpec((tm, D_MODEL), lambda i, j: (i, 0))
    return _call(
        body, (x, g, w_in_pt), name=name, grid=(t // tm, 3),
        in_specs=[tok, pl.BlockSpec((1, D_MODEL), lambda i, j: (0, 0)),
                  pl.BlockSpec((tn, D_MODEL), lambda i, j: (j, 0))],
        out_specs=[pl.BlockSpec((tm, tn), lambda i, j: (i, j)), tok],
        out_shape=[jax.ShapeDtypeStruct((t, PROJ_P), F32), jax.ShapeDtypeStruct((t, D_MODEL), BF16)],
        scratch_shapes=[], compiler_params=_ARB2, hosted=hosted)


def _mixer_out_fwd(o_ret, o_gla, w_out, x, name):
    t = x.shape[0]
    tm = min(t, TOKEN_TILE)
    half = HEADS * LANES

    def body(a_ref, b_ref, w_ref, x_ref, o_ref):
        o_ref[...] = x_ref[...] + _dot(a_ref[...], w_ref[0:half, :]) + _dot(b_ref[...], w_ref[half:2 * half, :])

    tok = pl.BlockSpec((tm, D_MODEL), lambda i: (i, 0))
    hb = pl.BlockSpec((tm, half), lambda i: (i, 0))
    return pl.pallas_call(
        body, name=name, grid=(t // tm,),
        in_specs=[hb, hb, pl.BlockSpec((2 * half, D_MODEL), lambda i: (0, 0)), tok],
        out_specs=tok, out_shape=jax.ShapeDtypeStruct((t, D_MODEL), F32),
        compiler_params=_ARB1,
    )(o_ret, o_gla, w_out, x)


def _final_loss(x, g, target, name):
    t = x.shape[0]
    tm = min(t, TOKEN_TILE)

    def body(x_ref, g_ref, t_ref, l_ref, dx_ref, dg_ref):
        @pl.when(pl.program_id(0) == 0)
        def _():
            l_ref[...] = jnp.zeros_like(l_ref)
            dg_ref[...] = jnp.zeros_like(dg_ref)

        xv = x_ref[...]
        gv = g_ref[...]
        err = (xv * _rms_scale(xv)) * gv - t_ref[...]
        l_ref[...] += 0.5 * jnp.sum(jnp.mean(err * err, axis=-1, keepdims=True), axis=0, keepdims=True)
        dx, dg = _rms_bwd(err * (1.0 / D_MODEL), xv, gv)
        dx_ref[...] = dx
        dg_ref[...] += dg

    tok = pl.BlockSpec((tm, D_MODEL), lambda i: (i, 0))
    row = pl.BlockSpec((1, D_MODEL), lambda i: (0, 0))
    return pl.pallas_call(
        body, name=name, grid=(t // tm,),
        in_specs=[tok, row, tok],
        out_specs=[pl.BlockSpec((8, LANES), lambda i: (0, 0)), tok, row],
        out_shape=[jax.ShapeDtypeStruct((8, LANES), F32), jax.ShapeDtypeStruct((t, D_MODEL), F32),
                   jax.ShapeDtypeStruct((1, D_MODEL), F32)],
        compiler_params=_ARB1,
    )(x, g, target)


def _rot(v, cos, sa, sb):
    return v * cos + pltpu.roll(v, 96, 1) * sa + pltpu.roll(v, 32, 1) * sb


def _rot_t(d, cos, sa, sb):
    return d * cos + pltpu.roll(d * sa, 32, 1) + pltpu.roll(d * sb, 96, 1)


def _bmm(a, b):
    return jnp.einsum("cik,ckj->cij", a, b, preferred_element_type=F32)


def _bmm_nt(a, b):
    return jnp.einsum("cik,cjk->cij", a, b, preferred_element_type=F32)


def _bmm_tn(a, b):
    return jnp.einsum("cki,ckj->cij", a, b, preferred_element_type=F32)


def _masked_sum(mask, x):
    hi = x.astype(BF16)
    r1 = x - hi.astype(F32)
    mid = r1.astype(BF16)
    lo = (r1 - mid.astype(F32)).astype(BF16)
    return _bmm(mask, hi) + _bmm(mask, mid) + _bmm(mask, lo)


def _tile_inputs(is_ret, qkvg_refs, aux, nc):
    shape3 = (nc, CHUNK, LANES)
    q_raw, k_raw, v, gate = (r[...] for r in qkvg_refs)
    ri = lax.broadcasted_iota(jnp.int32, (nc, CHUNK, CHUNK), 1)
    ci = lax.broadcasted_iota(jnp.int32, (nc, CHUNK, CHUNK), 2)
    if is_ret:
        cos_ref, sa_ref, sb_ref, lg_ref = aux
        cos, sa, sb = cos_ref[...], sa_ref[...], sb_ref[...]
        q = _rot(q_raw, cos, sa, sb)
        k = _rot(k_raw, cos, sa, sb) * QK_SCALE
        steps = (lax.broadcasted_iota(jnp.int32, shape3, 1) + 1).astype(F32)
        b = steps * lg_ref[...]
        logit = jnp.exp(jnp.abs(ri - ci).astype(F32) * lg_ref[:, 0:CHUNK])
    else:
        glow_ref, wa2_ref, ba_ref = aux
        logit = _dot(glow_ref[...].astype(BF16), wa2_ref[...]) + ba_ref[...]
        la = (jnp.minimum(logit, 0.0) - jnp.log1p(jnp.exp(-jnp.abs(logit)))) * (1.0 / GATE_NORM)
        b = _masked_sum((ci <= ri).astype(BF16), la.reshape(shape3))
        q = q_raw * QK_SCALE
        k = k_raw
    return q.reshape(shape3), k.reshape(shape3), v.reshape(shape3), gate, b, logit, ri, ci


def _tile_scores(q, k, b, ri, ci):
    mid = b[:, CHUNK // 2 - 1:CHUNK // 2, :]
    ep = jnp.exp(b - mid)
    en = jnp.exp(mid - b)
    qt, kt, qh, kh = q * ep, k * en, q * en, k * ep
    low = _bmm_nt(qt.astype(BF16), kt.astype(BF16))
    upp = _bmm_nt(qh.astype(BF16), kh.astype(BF16))
    scores = jnp.where(ci <= ri, low, upp)
    return scores, ep, en, qt, kt, qh, kh


def _attn_specs(is_ret, t, tb, imap_t):
    nb = t // tb
    base = 0 if is_ret else 4 * HEADS
    proj = [pl.BlockSpec((tb, LANES), lambda h, i, s=sec: (imap_t(i), base + HEADS * s + h)) for sec in range(4)]
    lane_t = pl.BlockSpec((tb, LANES), lambda h, i: (imap_t(i), 0))
    if is_ret:
        aux = [lane_t, lane_t, lane_t, pl.BlockSpec((None, 1, LANES), lambda h, i: (h, 0, 0))]
    else:
        aux = [pl.BlockSpec((tb, LANES), lambda h, i: (imap_t(i), PROJ_P // LANES - 1)),
               pl.BlockSpec((LANES, LANES), lambda h, i: (0, h)),
               pl.BlockSpec((1, LANES), lambda h, i: (0, h))]
    gain = pl.BlockSpec((1, LANES), lambda h, i: (0, h))
    head_t = pl.BlockSpec((tb, LANES), lambda h, i: (imap_t(i), h))
    state = pl.BlockSpec((None, tb // CHUNK, LANES, LANES), lambda h, i: (h, imap_t(i), 0, 0))
    return nb, proj, aux, gain, head_t, state


def _attn_fwd(is_ret, proj, aux_arrays, gain, name, hosted=None):
    t = proj.shape[0]
    tb = min(t, ATTN_TILE)
    nc = tb // CHUNK
    n_aux = 4 if is_ret else 3
    nb, proj_spec, aux_specs, gain_spec, head_t, state_spec = _attn_specs(is_ret, t, tb, lambda i: i)

    def body(*refs):
        qkvg_refs = refs[0:4]
        aux = refs[4:4 + n_aux]
        gn_ref, ofin_ref, oraw_ref, st_ref, state = refs[4 + n_aux:]

        @pl.when(pl.program_id(1) == 0)
        def _():
            state[...] = jnp.zeros_like(state)

        q, k, v, gate, b, decay, ri, ci = _tile_inputs(is_ret, qkvg_refs, aux, nc)
        if is_ret:
            scores = _bmm_nt(q.astype(BF16), k.astype(BF16)) * decay
        else:
            scores = _tile_scores(q, k, b, ri, ci)[0]
        vb = v.astype(BF16)
        intra = _bmm(scores.astype(BF16), vb)
        b_last = b[:, CHUNK - 1:CHUNK, :]
        e_last = jnp.exp(b_last)
        grow = _bmm_tn(vb, (k * jnp.exp(b_last - b)).astype(BF16))
        st = state[...]
        for c in range(nc):
            st_ref[c] = st
            st = st * e_last[c] + grow[c]
        state[...] = st
        inter = _bmm_nt((q * jnp.exp(b)).astype(BF16), st_ref[...].astype(BF16))
        out = (intra + inter).reshape(tb, LANES)
        oraw_ref[...] = out
        normed = out * _rms_scale(out)
        ofin_ref[...] = ((normed * gn_ref[...]) * (gate * jax.nn.sigmoid(gate))).astype(BF16)

    width = HEADS * LANES
    return _call(
        body, (proj, proj, proj, proj, *aux_arrays, gain), name=name, grid=(HEADS, nb),
        in_specs=proj_spec + aux_specs + [gain_spec],
        out_specs=[head_t, head_t, state_spec],
        out_shape=[jax.ShapeDtypeStruct((t, width), BF16), jax.ShapeDtypeStruct((t, width), F32),
                   jax.ShapeDtypeStruct((HEADS, t // CHUNK, LANES, LANES), F32)],
        scratch_shapes=[pltpu.VMEM((LANES, LANES), F32)],
        compiler_params=_ARB2, hosted=hosted)


def _attn_bwd(is_ret, proj, aux_arrays, gain, o_raw, states, d_out, name):
    t = proj.shape[0]
    tb = min(t, ATTN_TILE)
    nc = tb // CHUNK
    n_aux = 4 if is_ret else 3
    nblk = t // tb
    nb, proj_spec, aux_specs, gain_spec, head_t, state_spec = _attn_specs(is_ret, t, tb, lambda i: nblk - 1 - i)
    base = 0 if is_ret else HEADS
    dout_spec = pl.BlockSpec((tb, LANES), lambda h, i: (nblk - 1 - i, base + h))

    def body(*refs):
        qkvg_refs = refs[0:4]
        aux = refs[4:4 + n_aux]
        gn_ref, oraw_ref, st_ref, dfin_ref = refs[4 + n_aux:8 + n_aux]
        dq_ref, dk_ref, dv_ref, dgate_ref, dgn_ref = refs[8 + n_aux:13 + n_aux]
        if is_ret:
            dstate, dafter_ref = refs[13 + n_aux:]
        else:
            dlogit_ref, dba_ref, dstate, dafter_ref = refs[13 + n_aux:]

        @pl.when(pl.program_id(1) == 0)
        def _():
            dstate[...] = jnp.zeros_like(dstate)
            dgn_ref[...] = jnp.zeros_like(dgn_ref)
            if not is_ret:
                dba_ref[...] = jnp.zeros_like(dba_ref)

        shape3 = (nc, CHUNK, LANES)
        q, k, v, gate, b, logit, ri, ci = _tile_inputs(is_ret, qkvg_refs, aux, nc)
        eb = jnp.exp(b)
        qe = q * eb
        b_last = b[:, CHUNK - 1:CHUNK, :]
        e_last = jnp.exp(b_last)
        ekd = jnp.exp(b_last - b)
        kd = k * ekd

        gn = gn_ref[...]
        out = oraw_ref[...]
        r = _rms_scale(out)
        normed = out * r
        sg = jax.nn.sigmoid(gate)
        dfin = dfin_ref[...]
        dgate = dfin * (normed * gn) * _silu_grad(gate, sg)
        dpre = dfin * (gate * sg)
        dgn_ref[...] += jnp.sum(dpre * normed, axis=0, keepdims=True)
        dnormed = dpre * gn
        d_o = r * (dnormed - normed * jnp.mean(dnormed * normed, axis=-1, keepdims=True))
        dob, vb = d_o.reshape(shape3).astype(BF16), v.astype(BF16)

        dgrow = _bmm_tn(dob, qe.astype(BF16))
        dst = dstate[...]
        for c in reversed(range(nc)):
            dafter_ref[c] = dst
            dst = dst * e_last[c] + dgrow[c]
        dstate[...] = dst
        st = st_ref[...]
        dafter = dafter_ref[...]
        stb, dafter_b = st.astype(BF16), dafter.astype(BF16)

        dsc = _bmm_nt(dob, vb)
        dsc_t = _bmm_nt(vb, dob)
        dqe = _bmm(dob, stb)
        dkd = _bmm(vb, dafter_b)
        if is_ret:
            decay, qb, kb = logit, q.astype(BF16), k.astype(BF16)
            scores_t = _bmm_nt(kb, qb) * decay
            dq = _bmm((dsc * decay).astype(BF16), kb) + dqe * eb
            dk = _bmm((dsc_t * decay).astype(BF16), qb) + dkd * ekd
        else:
            _, ep, en, qt, kt, qh, kh = _tile_scores(q, k, b, ri, ci)
            qtb, ktb, qhb, khb = qt.astype(BF16), kt.astype(BF16), qh.astype(BF16), kh.astype(BF16)
            scores_t = jnp.where(ci >= ri, _bmm_nt(ktb, qtb), _bmm_nt(khb, qhb))
            dqt = _bmm(jnp.where(ci <= ri, dsc, 0.0).astype(BF16), ktb)
            dqh = _bmm(jnp.where(ci <= ri, 0.0, dsc).astype(BF16), khb)
            dkt = _bmm(jnp.where(ci >= ri, dsc_t, 0.0).astype(BF16), qtb)
            dkh = _bmm(jnp.where(ci >= ri, 0.0, dsc_t).astype(BF16), qhb)
            dq = dqt * ep + dqh * en + dqe * eb
            dk = dkt * en + dkh * ep + dkd * ekd
        dv = _bmm(scores_t.astype(BF16), dob) + _bmm_nt(kd.astype(BF16), dafter_b)
        dq, dk = dq.reshape(tb, LANES), dk.reshape(tb, LANES)

        if is_ret:
            cos_ref, sa_ref, sb_ref, _ = aux
            cos, sa, sb = cos_ref[...], sa_ref[...], sb_ref[...]
            dq_raw = _rot_t(dq, cos, sa, sb)
            dk_raw = _rot_t(dk, cos, sa, sb) * QK_SCALE
        else:
            dq_raw = dq * QK_SCALE
            dk_raw = dk
            db = dqt * qt - dkt * kt - dqh * qh + dkh * kh + dqe * qe - dkd * kd
            db_last = (jnp.sum(dkd * kd, axis=1, keepdims=True)
                       + jnp.sum(dafter * st, axis=1, keepdims=True) * e_last)
            last_row = lax.broadcasted_iota(jnp.int32, shape3, 1) == CHUNK - 1
            db = db + jnp.where(last_row, db_last, 0.0)
            dla = _masked_sum((ci >= ri).astype(BF16), db).reshape(tb, LANES)
            dlogit = dla * (1.0 / GATE_NORM) * jax.nn.sigmoid(-logit)
            dlogit_ref[...] = dlogit.astype(BF16)
            dba_ref[...] += jnp.sum(dlogit, axis=0, keepdims=True)

        dq_ref[...] = dq_raw.astype(BF16)
        dk_ref[...] = dk_raw.astype(BF16)
        dv_ref[...] = dv.reshape(tb, LANES).astype(BF16)
        dgate_ref[...] = dgate.astype(BF16)

    width = HEADS * LANES
    row_out = pl.BlockSpec((1, LANES), lambda h, i: (0, h))
    out_specs = [head_t] * 4 + [row_out]
    out_shape = [jax.ShapeDtypeStruct((t, width), BF16)] * 4 + [jax.ShapeDtypeStruct((1, width), F32)]
    if not is_ret:
        out_specs += [head_t, row_out]
        out_shape += [jax.ShapeDtypeStruct((t, width), BF16), jax.ShapeDtypeStruct((1, width), F32)]
    return pl.pallas_call(
        body, name=name, grid=(HEADS, nblk),
        in_specs=proj_spec + aux_specs + [gain_spec, head_t, state_spec, dout_spec],
        out_specs=out_specs, out_shape=out_shape,
        scratch_shapes=[pltpu.VMEM((LANES, LANES), F32), pltpu.VMEM((nc, LANES, LANES), F32)],
        compiler_params=_ARB2,
    )(proj, proj, proj, proj, *aux_arrays, gain, o_raw, states, d_out)


def _place():
    x, y, c = lax.axis_index("x"), lax.axis_index("y"), lax.axis_index("c")
    chips = [(1 - x, y), (x, 1 - y), (1 - x, 1 - y)]
    return x, y, c, 2 * x + y, chips


def _gather_plan(arrs):
    na = len(arrs)

    def copies(ins, outs, send_sems, recv_sems):
        x, y, c, me, chips = _place()

        def ici(a, j, src_chip, to):
            return pltpu.make_async_remote_copy(
                src_ref=ins[a].at[:, c], dst_ref=outs[a].at[src_chip, :, c],
                send_sem=send_sems.at[6 * a + j], recv_sem=recv_sems.at[6 * a + j], device_id=to, device_id_type=MESH)

        def d2d(a, j, src_chip, half):
            blk = outs[a].at[src_chip, :, half]
            return pltpu.make_async_remote_copy(
                src_ref=blk, dst_ref=blk, send_sem=send_sems.at[6 * a + 3 + j], recv_sem=recv_sems.at[6 * a + 3 + j],
                device_id=(x, y, 1 - c), device_id_type=MESH)

        peers = [(a, j, px, py) for a in range(na) for j, (px, py) in enumerate(chips)]
        return c, me, peers, ici, d2d

    def start(*refs):
        c, me, peers, ici, _ = copies(*refs)
        for a, j, px, py in peers:
            ici(a, j, me, (px, py, c)).start()

    def finish(*refs):
        c, me, peers, ici, d2d = copies(*refs)
        for a, j, px, py in peers:
            ici(a, j, 2 * px + py, (px, py, c)).wait_recv()
            d2d(a, j, 2 * px + py, c).start()
        for a, j, px, py in peers:
            d2d(a, j, 2 * px + py, 1 - c).wait_recv()
        for a, j, px, py in peers:
            ici(a, j, me, (px, py, c)).wait_send()
            d2d(a, j, 2 * px + py, c).wait_send()

    return _Hosted(arrs, [jax.ShapeDtypeStruct((N_CHIPS,) + a.shape, a.dtype) for a in arrs], 6 * na, start, finish)


def _pair_exchange(grads, name):
    na = len(grads)

    def body(*refs):
        ins, outs = refs[:na], refs[na:2 * na]
        send_sems, recv_sems = refs[2 * na:]
        x, y, c, _, _ = _place()
        copies = [pltpu.make_async_remote_copy(
            src_ref=ins[a].at[:, 1 - c], dst_ref=outs[a], send_sem=send_sems.at[a], recv_sem=recv_sems.at[a],
            device_id=(x, y, 1 - c), device_id_type=MESH) for a in range(na)]
        for cp in copies:
            cp.start()
        for cp in copies:
            cp.wait()

    return pl.pallas_call(
        body, name=name,
        in_specs=[ANY] * na, out_specs=[ANY] * na,
        out_shape=[jax.ShapeDtypeStruct(g.shape[:1] + g.shape[2:], g.dtype) for g in grads],
        scratch_shapes=[pltpu.SemaphoreType.DMA((na,)), pltpu.SemaphoreType.DMA((na,))],
    )(*grads)


def _pair_add(grad, recv, c_arr, name):
    _, _, r, cols = grad.shape

    def body(c_ref, g_ref, r_ref, o_ref):
        o_ref[...] = (g_ref[...].astype(F32) + r_ref[...].astype(F32)).astype(BF16)

    return pl.pallas_call(
        body, name=name,
        grid_spec=pltpu.PrefetchScalarGridSpec(
            num_scalar_prefetch=1, grid=(N_CHIPS,),
            in_specs=[pl.BlockSpec((None, None, r, cols), lambda p, c_ref: (p, c_ref[0], 0, 0)),
                      pl.BlockSpec((None, r, cols), lambda p, c_ref: (p, 0, 0))],
            out_specs=pl.BlockSpec((None, r, cols), lambda p, c_ref: (p, 0, 0))),
        out_shape=jax.ShapeDtypeStruct((N_CHIPS, r, cols), BF16),
        compiler_params=_ARB1,
    )(c_arr, grad, recv)


def _chip_exchange_plan(sums):
    na = len(sums)

    def copies(ins, outs, send_sems, recv_sems):
        x, y, c, me, chips = _place()

        def copy(a, j, px, py, block, slot):
            return pltpu.make_async_remote_copy(
                src_ref=ins[a].at[block], dst_ref=outs[a].at[slot],
                send_sem=send_sems.at[3 * a + j], recv_sem=recv_sems.at[3 * a + j],
                device_id=(px, py, c), device_id_type=MESH)

        peers = [(a, j, px, py) for a in range(na) for j, (px, py) in enumerate(chips)]
        return me, peers, copy

    def start(*refs):
        me, peers, copy = copies(*refs)
        for a, j, px, py in peers:
            copy(a, j, px, py, 2 * px + py, me).start()

    def finish(*refs):
        me, peers, copy = copies(*refs)
        for a, j, px, py in peers:
            copy(a, j, px, py, me, 2 * px + py).wait_recv()
        for a, j, px, py in peers:
            copy(a, j, px, py, 2 * px + py, me).wait_send()

    return _Hosted(sums, [jax.ShapeDtypeStruct(s.shape, s.dtype) for s in sums], 3 * na, start, finish)


def _chip_sum(own, recv, me_arr, name):
    _, r, cols = recv.shape

    def body(me_ref, own_ref, r_ref, o_ref):
        o_ref[...] = jnp.zeros_like(o_ref)
        for q in range(N_CHIPS):
            @pl.when(me_ref[0] == q)
            def _():
                o_ref[...] += own_ref[...].astype(F32)

            @pl.when(me_ref[0] != q)
            def _():
                o_ref[...] += r_ref[q].astype(F32)

    return pl.pallas_call(
        body, name=name,
        grid_spec=pltpu.PrefetchScalarGridSpec(
            num_scalar_prefetch=1, grid=(1,),
            in_specs=[pl.BlockSpec((None, r, cols), lambda i, me_ref: (me_ref[0], 0, 0)),
                      pl.BlockSpec((N_CHIPS, r, cols), lambda i, me_ref: (0, 0, 0))],
            out_specs=pl.BlockSpec((r, cols), lambda i, me_ref: (0, 0))),
        out_shape=jax.ShapeDtypeStruct((r, cols), F32),
        compiler_params=_ARB1,
    )(me_arr, own, recv)


def _pair_share(halves):
    na = len(halves)

    def body(*refs):
        ins, outs = refs[:na], refs[na:2 * na]
        send_sems, recv_sems = refs[2 * na:]
        x, y, c, _, _ = _place()
        copies = [pltpu.make_async_remote_copy(
            src_ref=ins[a], dst_ref=outs[a], send_sem=send_sems.at[a], recv_sem=recv_sems.at[a],
            device_id=(x, y, 1 - c), device_id_type=MESH) for a in range(na)]
        for cp in copies:
            cp.start()
        for cp in copies:
            cp.wait()

    return pl.pallas_call(
        body, name="pair_share",
        in_specs=[ANY] * na, out_specs=[ANY] * na,
        out_shape=[jax.ShapeDtypeStruct(h.shape, h.dtype) for h in halves],
        scratch_shapes=[pltpu.SemaphoreType.DMA((na,)), pltpu.SemaphoreType.DMA((na,))],
    )(*halves)


def _small_allreduce(block):
    m, n = block.shape

    def body(x_ref, all_ref, sum_ref, send_sems, recv_sems, local_sem):
        x, y, c, _, chips = _place()
        me, sibling = (x, y, c), (x, y, 1 - c)

        def rows(px, py, pc):
            return all_ref.at[pl.ds((4 * px + 2 * py + pc) * m, m), :]

        def copy(k, blk, to, src=None):
            return pltpu.make_async_remote_copy(
                src_ref=rows(*blk) if src is None else src, dst_ref=rows(*blk),
                send_sem=send_sems.at[k], recv_sem=recv_sems.at[k], device_id=to, device_id_type=MESH)

        mine = pltpu.make_async_copy(x_ref, rows(*me), local_sem)
        mine.start()
        first = [copy(0, me, sibling, src=x_ref)]
        first += [copy(1 + j, me, (*chip, c), src=x_ref) for j, chip in enumerate(chips)]
        for cp in first:
            cp.start()
        passed = [copy(4 + j, (*chip, c), sibling) for j, chip in enumerate(chips)]
        for j, chip in enumerate(chips):
            copy(1 + j, (*chip, c), me).wait_recv()
            passed[j].start()
        copy(0, sibling, me).wait_recv()
        for j, chip in enumerate(chips):
            copy(4 + j, (*chip, 1 - c), me).wait_recv()
        for cp in first + passed:
            cp.wait_send()
        mine.wait()
        acc = all_ref[0:m, :]
        for d in range(1, 8):
            acc = acc + all_ref[d * m:(d + 1) * m, :]
        sum_ref[...] = acc

    vmem = pl.BlockSpec(memory_space=pltpu.VMEM)
    return pl.pallas_call(
        body, name="small_allreduce",
        in_specs=[vmem], out_specs=[vmem, vmem],
        out_shape=[jax.ShapeDtypeStruct((8 * m, n), F32), jax.ShapeDtypeStruct((m, n), F32)],
        scratch_shapes=[pltpu.SemaphoreType.DMA((7,)), pltpu.SemaphoreType.DMA((7,)), pltpu.SemaphoreType.DMA],
    )(block)[1]


def _row_tile(rows):
    best = rows
    for cand in range(8, min(rows, 512) + 1, 8):
        if rows % cand == 0:
            best = cand
    return best


def _adamw_math(w, g, m, v):
    m2 = ADAM_B1 * m + (1.0 - ADAM_B1) * g
    v2 = ADAM_B2 * v + (1.0 - ADAM_B2) * (g * g)
    m_hat = m2 / (1.0 - ADAM_B1 ** ADAM_STEP)
    v_hat = v2 / (1.0 - ADAM_B2 ** ADAM_STEP)
    return -ADAM_LR * (m_hat / (jnp.sqrt(v_hat) + ADAM_EPS) + ADAM_WD * w), m2, v2


def _adamw_halves(w, g_mine, g_other, m, v, c_arr, name):
    rows, cols = w.shape
    r = rows // 2
    tr = _row_tile(r)
    nt = r // tr

    def body(c_ref, w_ref, gm_ref, go_ref, m_ref, v_ref, g_ref, d_ref, nm_ref, nv_ref):
        gv = jnp.where(pl.program_id(0) == c_ref[0], gm_ref[...], go_ref[...])
        g_ref[...] = gv
        d_ref[...], nm_ref[...], nv_ref[...] = _adamw_math(w_ref[...], gv, m_ref[...], v_ref[...])

    full = pl.BlockSpec((tr, cols), lambda h, i, c_ref: (h * nt + i, 0))
    half = pl.BlockSpec((tr, cols), lambda h, i, c_ref: (i, 0))
    shape = jax.ShapeDtypeStruct((rows, cols), F32)
    return pl.pallas_call(
        body, name=name,
        grid_spec=pltpu.PrefetchScalarGridSpec(
            num_scalar_prefetch=1, grid=(2, nt),
            in_specs=[full, half, half, full, full], out_specs=[full] * 4),
        out_shape=[shape] * 4,
        compiler_params=_ARB2,
    )(c_arr, w, g_mine, g_other, m, v)


def _adamw(w, g, m, v, name):
    rows, cols = w.shape
    tr = _row_tile(rows)

    def body(w_ref, g_ref, m_ref, v_ref, d_ref, nm_ref, nv_ref):
        d_ref[...], nm_ref[...], nv_ref[...] = _adamw_math(w_ref[...], g_ref[...], m_ref[...], v_ref[...])

    spec = pl.BlockSpec((tr, cols), lambda i: (i, 0))
    shape = jax.ShapeDtypeStruct((rows, cols), F32)
    return pl.pallas_call(
        body, name=name, grid=(rows // tr,),
        in_specs=[spec] * 4, out_specs=[spec] * 3, out_shape=[shape] * 3,
        compiler_params=_ARB1,
    )(w, g, m, v)


def _pad_w_in_t(w_in_t):
    def heads_padded(sec):
        return jnp.pad(sec.reshape(HEADS, 64, -1), ((0, 0), (0, LANES - 64), (0, 0))).reshape(HEADS * LANES, -1)

    w = w_in_t
    return jnp.concatenate([
        heads_padded(w[0:256]), heads_padded(w[256:512]), w[512:1536],
        heads_padded(w[1536:1792]), heads_padded(w[1792:2048]), w[2048:3072],
        jnp.pad(w[3072:3088], ((0, LANES - GATE_RANK), (0, 0)))], axis=0)


def _unpad_w_in_t(w_pt):
    def heads_unpadded(sec):
        return sec.reshape(HEADS, LANES, -1)[:, 0:64].reshape(HEADS * 64, -1)

    p = w_pt
    return jnp.concatenate([
        heads_unpadded(p[0:512]), heads_unpadded(p[512:1024]), p[1024:2048],
        heads_unpadded(p[2048:2560]), heads_unpadded(p[2560:3072]), p[3072:4096],
        p[4096:4096 + GATE_RANK]], axis=0)


def _rope_tables(t):
    half = 32
    inv = ROPE_BASE ** (-jnp.arange(half, dtype=F32) * 2.0 / 64)
    ang = jnp.arange(t, dtype=F32)[:, None] * inv[None, :]
    cos, sin = jnp.cos(ang), jnp.sin(ang)
    z32, z64 = jnp.zeros((t, 32), F32), jnp.zeros((t, 64), F32)
    return (jnp.concatenate([cos, cos, z64], axis=1),
            jnp.concatenate([-sin, z32, z64], axis=1),
            jnp.concatenate([z32, sin, z64], axis=1))


def _halves(w):
    n, rows, cols = w.shape
    return w.reshape(n, 2, rows // 2, cols)


_VMEM = pl.BlockSpec(memory_space=pltpu.VMEM)


def _pack_small(n1, nm, n2, nf, nret, ngla, ba, wa2_p, loss_blk):
    def body(n1_ref, nm_ref, n2_ref, nf_ref, nret_ref, ngla_ref, ba_ref, wa2_ref, loss_ref, o_ref):
        o_ref[...] = jnp.zeros_like(o_ref)
        o_ref[0:1, :] = n1_ref[...]
        o_ref[1:2, :] = nm_ref[...]
        o_ref[2:3, :] = n2_ref[...]
        o_ref[3:4, :] = nf_ref[...]
        o_ref[4:5, 0:512] = nret_ref[...]
        o_ref[4:5, 512:1024] = ngla_ref[...]
        o_ref[5:6, 0:256] = ba_ref[...]
        o_ref[6:7, 0:LANES] = loss_ref[0:1, :]
        o_ref[8:8 + GATE_RANK, 0:HEADS * LANES] = wa2_ref[0:GATE_RANK, :]

    return pl.pallas_call(
        body, name="pack_small", in_specs=[_VMEM] * 9, out_specs=_VMEM,
        out_shape=jax.ShapeDtypeStruct((SMALL_ROWS, D_MODEL), F32),
    )(n1, nm, n2, nf, nret, ngla, ba, wa2_p, loss_blk)


def _small_update(summed, chip_arr, ws, ms, vs):
    n = len(ws)

    def body(chip_ref, s_ref, *refs):
        w_refs, m_refs, v_refs = refs[0:n], refs[n:2 * n], refs[2 * n:3 * n]
        outs = refs[3 * n:]
        wa2_all = s_ref[8:8 + GATE_RANK, 0:HEADS * LANES]
        wa2_g = jnp.zeros((GATE_RANK, 64), F32)
        for p in range(N_CHIPS):
            wa2_g = jnp.where(chip_ref[0] == p, wa2_all[:, LANES * p:LANES * p + 64], wa2_g)
        grads = [s_ref[0:1, :], s_ref[1:2, :], s_ref[2:3, :], s_ref[3:4, :], s_ref[4:5, 0:512],
                 s_ref[4:5, 512:1024], s_ref[5:6, 0:256], wa2_g]
        for k in range(n):
            d, m2, v2 = _adamw_math(w_refs[k][...], grads[k], m_refs[k][...], v_refs[k][...])
            outs[k][...] = grads[k]
            outs[n + k][...] = d
            outs[2 * n + k][...] = m2
            outs[3 * n + k][...] = v2

    shapes = [jax.ShapeDtypeStruct(w.shape, F32) for w in ws] * 4
    smem = pl.BlockSpec(memory_space=pltpu.SMEM)
    outs = pl.pallas_call(
        body, name="small_update", in_specs=[smem] + [_VMEM] * (1 + 3 * n), out_specs=[_VMEM] * (4 * n),
        out_shape=shapes,
    )(chip_arr, summed, *ws, *ms, *vs)
    return outs[0:n], outs[n:2 * n], outs[2 * n:3 * n], outs[3 * n:4 * n]


def _pad_in_rows(w_t):
    return jnp.pad(w_t, ((0, IN_ROWS - IN_SHARD), (0, 0)))


def _forward_backward(xs, target, ffn1_w, rest, ba_p, ffn1_norm_g, mix_norm_g, ret_norm_g, gla_norm_g, ffn2_norm_g,
                      final_norm_g, rest_plan=None, rest_weights=None, ffn2_plans=None, ffn2_weights=None,
                      early=None, late=None):
    t = xs.shape[0]
    cos_t, sa_t, sb_t = _rope_tables(t)
    log_gamma = jnp.log(1.0 - 2.0 ** (-5.0 - jnp.arange(HEADS, dtype=F32)))
    lg_t = jnp.broadcast_to(log_gamma[:, None, None], (HEADS, 1, LANES))
    ret_aux = [cos_t, sa_t, sb_t, lg_t]

    (x1, a1, u1, h1), gathered = _ffn_fwd(xs, ffn1_norm_g, ffn1_w, "ffn1_fwd", hosted=rest_plan)
    ffn2_w, w_in_pt, w_out_full, wa2_p = rest if rest_plan is None else rest_weights(gathered)
    plans = [None] * 3 if ffn2_plans is None else ffn2_plans
    (proj, h_mix), got_gate = _mixer_in_fwd(x1, mix_norm_g, w_in_pt, "mixer_in_fwd", hosted=plans[0])
    gla_aux = [proj, wa2_p, ba_p]
    (o_ret, raw_ret, st_ret), got_up = _attn_fwd(True, proj, ret_aux, ret_norm_g, "ret_fwd", hosted=plans[1])
    (o_gla, raw_gla, st_gla), got_down = _attn_fwd(False, proj, gla_aux, gla_norm_g, "gla_fwd", hosted=plans[2])
    if ffn2_plans is not None:
        ffn2_w = ffn2_weights(got_gate + got_up + got_down)
    x2 = _mixer_out_fwd(o_ret, o_gla, w_out_full, x1, "mixer_out_fwd")
    (x3, a2, u2, h2), _ = _ffn_fwd(x2, ffn2_norm_g, ffn2_w, "ffn2_fwd")
    loss_blk, dx3, d_final_g = _final_loss(x3, final_norm_g, target, "final_loss")

    (da2, du2, hid2, dob2, dx2, d_ffn2_g), _ = _ffn_bwd(dx3, x2, ffn2_norm_g, a2, u2, ffn2_w, "ffn2_bwd")
    g_gate2 = _matmul_tn(da2, h2, "ffn2_dgate", out_dtype=BF16)
    g_up2 = _matmul_tn(du2, h2, "ffn2_dup", out_dtype=BF16)
    g_down2 = _matmul_tn(hid2, dob2, "ffn2_ddown", out_dtype=BF16)

    d_o = _matmul_nt(dx2, w_out_full, "mixer_out_bwd")
    g_wout_ret = _matmul_tn(o_ret, dx2, "wout_grad_ret", out_dtype=BF16)
    g_wout_gla = _matmul_tn(o_gla, dx2, "wout_grad_gla", out_dtype=BF16)
    *dproj_ret, d_ret_g = _attn_bwd(True, proj, ret_aux, ret_norm_g, raw_ret, st_ret, d_o, "ret_bwd")
    *dproj_gla, d_gla_g, dlogit, d_ba_p = _attn_bwd(False, proj, gla_aux, gla_norm_g, raw_gla, st_gla, d_o, "gla_bwd")
    d_glow = _matmul_nt(dlogit, wa2_p, "gate_low_bwd", out_dtype=BF16)
    g_wa2_p = _matmul_tn(proj[:, PROJ_P - LANES:], dlogit, "gate_w_grad")
    dproj = jnp.concatenate(dproj_ret + dproj_gla + [d_glow], axis=1)
    g_win_p = _matmul_tn(dproj, h_mix, "w_in_grad", tka=PROJ_P // 3, out_dtype=BF16)
    dx1, d_mix_g = _mixer_in_bwd(dproj, w_in_pt, dx2, x1, mix_norm_g, "mixer_in_bwd")
    g_win_t = _unpad_w_in_t(g_win_p[0])
    g_win = jnp.stack([_pad_in_rows(g_win_t[IN_SHARD * p:IN_SHARD * (p + 1)]) for p in range(N_CHIPS)], axis=0)
    g_wout = jnp.concatenate([g_wout_ret[0], g_wout_gla[0]], axis=0).reshape(N_CHIPS, D_MODEL // N_CHIPS, D_MODEL)

    early_plan = None if early is None else early([g_gate2, g_up2, g_down2, g_win, g_wout])
    (da1, du1, hid1, dob1, grad_x, d_ffn1_g), arrived = _ffn_bwd(dx1, xs, ffn1_norm_g, a1, u1, ffn1_w, "ffn1_bwd",
                                                                hosted=early_plan)
    late_grads, late_arrived = [], []
    for lhs, rhs, name in ((da1, h1, "ffn1_dgate"), (du1, h1, "ffn1_dup"), (hid1, dob1, "ffn1_ddown")):
        plan = None if late is None or not late_grads else late(late_grads[-1], len(late_grads))
        res = _matmul_tn(lhs, rhs, name, out_dtype=BF16, hosted=plan)
        if plan is not None:
            res, carried = res
            late_arrived += carried
        late_grads.append(res)
    g_gate1, g_up1, g_down1 = late_grads

    return (loss_blk, grad_x, g_gate1, g_up1, g_down1, g_gate2, g_up2, g_down2, g_win, g_wout, g_wa2_p,
            d_ba_p, d_ffn1_g, d_mix_g, d_ffn2_g, d_final_g, d_ret_g, d_gla_g, arrived, late_arrived)


def kernel(x, ffn1_norm_g, ffn1_w_gate, ffn1_w_up, ffn1_w_down, mix_norm_g, w_in, ret_norm_g, gla_w_a2, gla_b_a, gla_norm_g, w_out, ffn2_norm_g, ffn2_w_gate, ffn2_w_up, ffn2_w_down, final_norm_g, loss_target, m_ffn1_norm_g, m_ffn1_w_gate, m_ffn1_w_up, m_ffn1_w_down, m_mix_norm_g, m_w_in, m_ret_norm_g, m_gla_w_a2, m_gla_b_a, m_gla_norm_g, m_w_out, m_ffn2_norm_g, m_ffn2_w_gate, m_ffn2_w_up, m_ffn2_w_down, m_final_norm_g, v_ffn1_norm_g, v_ffn1_w_gate, v_ffn1_w_up, v_ffn1_w_down, v_mix_norm_g, v_w_in, v_ret_norm_g, v_gla_w_a2, v_gla_b_a, v_gla_norm_g, v_w_out, v_ffn2_norm_g, v_ffn2_w_gate, v_ffn2_w_up, v_ffn2_w_down, v_final_norm_g):
    t = x.shape[1]
    xs = x.reshape(t, D_MODEL)
    target = loss_target.reshape(t, D_MODEL)
    chip = 2 * lax.axis_index("x") + lax.axis_index("y")
    c_arr = lax.axis_index("c").astype(jnp.int32).reshape(1)

    me_arr = chip.astype(jnp.int32).reshape(1)

    pad_rows = _pad_in_rows

    def own_block(gathered, shard):
        return lax.dynamic_update_slice(gathered, shard[None], (chip,) + (0,) * shard.ndim)

    ffn1_shard = _halves(jnp.stack([ffn1_w_gate[0].T, ffn1_w_up[0].T, ffn1_w_down[0]], axis=0).astype(BF16))
    rest_shards = [_halves(pad_rows(w_in[0].T).astype(BF16)[None]),
                   _halves(w_out.astype(BF16)),
                   jnp.concatenate([gla_w_a2.reshape(GATE_RANK, 64), jnp.zeros((GATE_RANK, 64), F32)],
                                   axis=1).reshape(1, 2, 8, LANES)]
    ffn2_shards = [_halves(w.astype(BF16)[None]) for w in (ffn2_w_gate[0].T, ffn2_w_up[0].T, ffn2_w_down[0])]
    ffn1_all = _run_hosted(_gather_plan([ffn1_shard]), "gather_ffn1")[0]
    ffn1_w = own_block(ffn1_all, ffn1_shard).reshape(N_CHIPS, 3, FF_SHARD, D_MODEL)

    def rest_weights(gathered):
        win_all, wout_all, wa2_all = [own_block(g, s) for g, s in zip(gathered, rest_shards)]
        win_t = win_all.reshape(N_CHIPS, IN_ROWS, D_MODEL)
        w_in_pt = _pad_w_in_t(jnp.concatenate([win_t[p, 0:IN_SHARD] for p in range(N_CHIPS)], axis=0))
        wa2_p = jnp.pad(
            wa2_all.reshape(N_CHIPS, GATE_RANK, LANES).transpose(1, 0, 2).reshape(GATE_RANK, HEADS * LANES),
            ((0, LANES - GATE_RANK), (0, 0))).astype(BF16)
        return (None, w_in_pt, wout_all.reshape(D_MODEL, D_MODEL), wa2_p)

    def ffn2_weights(gathered):
        return [own_block(g, s).reshape(N_CHIPS, FF_SHARD, D_MODEL) for g, s in zip(gathered, ffn2_shards)]

    def pair_sums(grads, tag):
        halves = [g.reshape(g.shape[0], 2, g.shape[1] // 2, g.shape[2]) for g in grads]
        recv = _pair_exchange(halves, "pair_exchange_" + tag)
        return [_pair_add(g, r, c_arr, "pair_add_%s%d" % (tag, k)) for k, (g, r) in enumerate(zip(halves, recv))]

    early_sums = []

    def early(grads):
        early_sums.extend(pair_sums(grads, "early"))
        return _chip_exchange_plan(early_sums)

    late_sums = []

    def late(grad, number):
        late_sums.extend(pair_sums([grad], "late%d" % number))
        return _chip_exchange_plan(late_sums[-1:])

    ba_p = jnp.pad(gla_b_a.reshape(HEADS, 64), ((0, 0), (0, 64))).reshape(1, HEADS * LANES)
    fb = _forward_backward(xs, target, ffn1_w, None, ba_p, ffn1_norm_g, mix_norm_g, ret_norm_g, gla_norm_g,
                           ffn2_norm_g, final_norm_g.reshape(1, D_MODEL), rest_plan=_gather_plan(rest_shards),
                           rest_weights=rest_weights, ffn2_plans=[_gather_plan([s]) for s in ffn2_shards],
                           ffn2_weights=ffn2_weights, early=early, late=late)
    (loss_blk, grad_x, _, _, g_down1, _, _, _, _, _, g_wa2_p,
     d_ba_p, d_ffn1_g, d_mix_g, d_ffn2_g, d_final_g, d_ret_g, d_gla_g, early_arrived, late_arrived) = fb
    late_arrived = late_arrived + _run_hosted(late(g_down1, 3), "chip_exchange_late")
    sums, arrived = late_sums + early_sums, late_arrived + early_arrived
    mine = [_chip_sum(s, r, me_arr, "chip_sum_%d" % k) for k, (s, r) in enumerate(zip(sums, arrived))]
    other = _pair_share(mine)

    d_ba = d_ba_p.reshape(HEADS, LANES)[:, 0:64].reshape(1, 256)
    small_local = _pack_small(d_ffn1_g, d_mix_g, d_ffn2_g, d_final_g, d_ret_g, d_gla_g, d_ba, g_wa2_p[0], loss_blk)
    small_sum = _small_allreduce(small_local)
    loss = small_sum[6, 0]

    def rows(n1, nm, n2, nf, nret, ngla, ba, wa2):
        return [n1, nm, n2, nf.reshape(1, D_MODEL), nret, ngla, ba, wa2.reshape(GATE_RANK, 64)]

    small = _small_update(
        small_sum, me_arr,
        rows(ffn1_norm_g, mix_norm_g, ffn2_norm_g, final_norm_g, ret_norm_g, gla_norm_g, gla_b_a, gla_w_a2),
        rows(m_ffn1_norm_g, m_mix_norm_g, m_ffn2_norm_g, m_final_norm_g, m_ret_norm_g, m_gla_norm_g, m_gla_b_a,
             m_gla_w_a2),
        rows(v_ffn1_norm_g, v_mix_norm_g, v_ffn2_norm_g, v_final_norm_g, v_ret_norm_g, v_gla_norm_g, v_gla_b_a,
             v_gla_w_a2))
    s_grad, s_delta, s_m, s_v = [
        [*o[0:3], o[3].reshape(D_MODEL), *o[4:7], o[7].reshape(1, GATE_RANK, 64)] for o in small]

    def big(k, w, m, v, name, to_2d, from_2d):
        outs4 = _adamw_halves(to_2d(w), mine[k], other[k], to_2d(m), to_2d(v), c_arr, name)
        return [from_2d(z) for z in outs4]

    plain = (lambda w: w[0], lambda z: z[None])
    transposed = (lambda w: w[0].T, lambda z: z.T[None])
    in_proj = (lambda w: pad_rows(w[0].T), lambda z: z[0:IN_SHARD].T[None])
    r_g1 = big(0, ffn1_w_gate, m_ffn1_w_gate, v_ffn1_w_gate, "adamw_ffn1_gate", *transposed)
    r_u1 = big(1, ffn1_w_up, m_ffn1_w_up, v_ffn1_w_up, "adamw_ffn1_up", *transposed)
    r_d1 = big(2, ffn1_w_down, m_ffn1_w_down, v_ffn1_w_down, "adamw_ffn1_down", *plain)
    r_g2 = big(3, ffn2_w_gate, m_ffn2_w_gate, v_ffn2_w_gate, "adamw_ffn2_gate", *transposed)
    r_u2 = big(4, ffn2_w_up, m_ffn2_w_up, v_ffn2_w_up, "adamw_ffn2_up", *transposed)
    r_d2 = big(5, ffn2_w_down, m_ffn2_w_down, v_ffn2_w_down, "adamw_ffn2_down", *plain)
    r_in = big(6, w_in, m_w_in, v_w_in, "adamw_w_in", *in_proj)
    r_out = big(7, w_out, m_w_out, v_w_out, "adamw_w_out", *plain)

    def leaves(k, smalls):
        n1, nm, n2, nf, nret, ngla, ba, wa2 = smalls
        return [n1, r_g1[k], r_u1[k], r_d1[k], nm, r_in[k], nret, wa2, ba, ngla, r_out[k], n2, r_g2[k], r_u2[k], r_d2[k], nf]

    outs = [loss, grad_x.reshape(x.shape)]
    outs += leaves(0, s_grad) + leaves(1, s_delta) + leaves(2, s_m) + leaves(3, s_v)
    return tuple(outs)
```

```python
import functools

import jax
import jax.numpy as jnp
from jax import lax
from jax.experimental import pallas as pl
from jax.experimental.pallas import tpu as pltpu

F32, BF16 = jnp.float32, jnp.bfloat16
MESH = pl.DeviceIdType.MESH
ANY = pl.BlockSpec(memory_space=pl.ANY)

D_MODEL = 1024
D_FF = 2816
N_CHIPS = 4
FF_SHARD = D_FF // N_CHIPS
IN_WIDTH = 3088
IN_SHARD = IN_WIDTH // N_CHIPS
IN_ROWS = 800
CHUNK = 64
HEADS = 4
LANES = 128
PROJ_P = (2 * 4 * HEADS + 1) * LANES
GATE_RANK = 16
QK_SCALE = 0.125
GATE_NORM = 16.0
RMS_EPS = 1e-6
ROPE_BASE = 10000.0
ADAM_LR, ADAM_B1, ADAM_B2, ADAM_EPS, ADAM_WD, ADAM_STEP = 0.001, 0.9, 0.999, 1e-08, 0.01, 10
SMALL_ROWS = 32
TOKEN_TILE = 512
ATTN_TILE = 512

_ARB2 = pltpu.CompilerParams(dimension_semantics=("arbitrary", "arbitrary"))
_ARB1 = pltpu.CompilerParams(dimension_semantics=("arbitrary",))
_ARB3 = pltpu.CompilerParams(dimension_semantics=("arbitrary", "arbitrary", "arbitrary"))


def _dot(a, b):
    return jnp.dot(a, b, preferred_element_type=F32)


def _dot_nt(a, b):
    return lax.dot_general(a, b, (((1,), (1,)), ((), ())), preferred_element_type=F32)


def _dot_tn(a, b):
    return lax.dot_general(a, b, (((0,), (0,)), ((), ())), preferred_element_type=F32)


def _rms_scale(xv):
    return lax.rsqrt(jnp.mean(xv * xv, axis=-1, keepdims=True) + RMS_EPS)


def _rms_bwd(dh, xv, g):
    r = _rms_scale(xv)
    xhat = xv * r
    dxhat = dh * g
    dx = r * (dxhat - xhat * jnp.mean(dxhat * xhat, axis=-1, keepdims=True))
    return dx, jnp.sum(dh * xhat, axis=0, keepdims=True)


def _silu_grad(a, sg):
    return sg * (1.0 + a * (1.0 - sg))


class _Hosted:
    def __init__(self, arrays, out_shapes, n_sems, start, finish):
        self.arrays, self.out_shapes, self.n_sems = list(arrays), list(out_shapes), n_sems
        self.start, self.finish = start, finish


def _call(body, args, *, name, grid, in_specs, out_specs, out_shape, scratch_shapes, compiler_params, hosted=None):
    if hosted is None:
        outs = pl.pallas_call(body, name=name, grid=grid, in_specs=in_specs, out_specs=out_specs, out_shape=out_shape,
                              scratch_shapes=scratch_shapes, compiler_params=compiler_params)(*args)
        return list(outs), []
    n_in, n_out, n_sc, nh = len(in_specs), len(out_specs), len(scratch_shapes), len(hosted.arrays)

    def wrapped(*refs):
        ins, h_in = refs[:n_in], refs[n_in:n_in + nh]
        outs, h_out = refs[n_in + nh:n_in + nh + n_out], refs[n_in + nh + n_out:n_in + 2 * nh + n_out]
        rest = refs[n_in + 2 * nh + n_out:]
        scratch, (send_sems, recv_sems) = rest[:n_sc], rest[n_sc:]
        first = functools.reduce(jnp.logical_and, [pl.program_id(d) == 0 for d in range(len(grid))])
        last = functools.reduce(jnp.logical_and, [pl.program_id(d) == n - 1 for d, n in enumerate(grid)])

        @pl.when(first)
        def _():
            hosted.start(h_in, h_out, send_sems, recv_sems)

        body(*ins, *outs, *scratch)

        @pl.when(last)
        def _():
            hosted.finish(h_in, h_out, send_sems, recv_sems)

    sems = [pltpu.SemaphoreType.DMA((hosted.n_sems,)), pltpu.SemaphoreType.DMA((hosted.n_sems,))]
    outs = pl.pallas_call(
        wrapped, name=name, grid=grid, in_specs=list(in_specs) + [ANY] * nh, out_specs=list(out_specs) + [ANY] * nh,
        out_shape=list(out_shape) + hosted.out_shapes, scratch_shapes=list(scratch_shapes) + sems,
        compiler_params=compiler_params)(*args, *hosted.arrays)
    return list(outs[:n_out]), list(outs[n_out:])


def _run_hosted(hosted, name):
    nh = len(hosted.arrays)

    def body(*refs):
        h_in, h_out, (send_sems, recv_sems) = refs[:nh], refs[nh:2 * nh], refs[2 * nh:]
        hosted.start(h_in, h_out, send_sems, recv_sems)
        hosted.finish(h_in, h_out, send_sems, recv_sems)

    sems = [pltpu.SemaphoreType.DMA((hosted.n_sems,)), pltpu.SemaphoreType.DMA((hosted.n_sems,))]
    return list(pl.pallas_call(body, name=name, in_specs=[ANY] * nh, out_specs=[ANY] * nh,
                               out_shape=hosted.out_shapes, scratch_shapes=sems)(*hosted.arrays))


def _ffn_weight_operands(ffn_w, chunk_maps):
    if isinstance(ffn_w, (list, tuple)):
        specs = [pl.BlockSpec((None, FF_SHARD, D_MODEL), lambda *g, m=m: (m(*g), 0, 0)) for m in chunk_maps]
        return list(ffn_w), specs
    specs = [pl.BlockSpec((None, None, FF_SHARD, D_MODEL), lambda *g, m=m, k=kind: (m(*g), k, 0, 0))
             for kind, m in enumerate(chunk_maps)]
    return [ffn_w] * 3, specs


def _pipeline_items(steps):
    def cur(s):
        c = jnp.minimum(s, steps - 1)
        return c // N_CHIPS, c % N_CHIPS

    def prev(s):
        p = jnp.maximum(s - 1, 0)
        return p // N_CHIPS, p % N_CHIPS

    return cur, prev


def _ffn_fwd(x, g, ffn_w, name, hosted=None):
    t = x.shape[0]
    tm = min(t, TOKEN_TILE)

    def body(x_ref, g_ref, wg_ref, wu_ref, wd_ref, xo_ref, a_ref, u_ref, h_ref, acc_ref):
        j = pl.program_id(1)

        @pl.when(j == 0)
        def _():
            xv = x_ref[...]
            h_ref[...] = ((xv * _rms_scale(xv)) * g_ref[...]).astype(BF16)
            acc_ref[...] = jnp.zeros_like(acc_ref)

        h = h_ref[...]
        a = _dot_nt(h, wg_ref[...])
        u = _dot_nt(h, wu_ref[...])
        a_ref[...] = a.astype(BF16)
        u_ref[...] = u.astype(BF16)
        hid = (a * jax.nn.sigmoid(a)) * u
        acc_ref[...] += _dot(hid.astype(BF16), wd_ref[...])

        @pl.when(j == N_CHIPS - 1)
        def _():
            xo_ref[...] = x_ref[...] + 0.5 * acc_ref[...]

    tok = pl.BlockSpec((tm, D_MODEL), lambda i, j: (i, 0))
    act = pl.BlockSpec((None, tm, FF_SHARD), lambda i, j: (j, i, 0))
    w_arrays, weights = _ffn_weight_operands(ffn_w, [lambda i, j: j] * 3)
    return _call(
        body, (x, g, *w_arrays), name=name, grid=(t // tm, N_CHIPS),
        in_specs=[tok, pl.BlockSpec((1, D_MODEL), lambda i, j: (0, 0))] + weights,
        out_specs=[tok, act, act, tok],
        out_shape=[jax.ShapeDtypeStruct((t, D_MODEL), F32),
                   jax.ShapeDtypeStruct((N_CHIPS, t, FF_SHARD), BF16),
                   jax.ShapeDtypeStruct((N_CHIPS, t, FF_SHARD), BF16),
                   jax.ShapeDtypeStruct((t, D_MODEL), BF16)],
        scratch_shapes=[pltpu.VMEM((tm, D_MODEL), F32)],
        compiler_params=_ARB2, hosted=hosted)


def _ffn1_fwd_gathering(x, g, shard, name, hosted=None):
    t = x.shape[0]
    tm = min(t, TOKEN_TILE)
    nt = t // tm
    nh = 0 if hosted is None else len(hosted.arrays)

    def body(*refs):
        x_ref, g_ref, shard_ref = refs[0:3]
        h_in = refs[3:3 + nh]
        xo_ref, a_ref, u_ref, h_ref, wall = refs[3 + nh:8 + nh]
        h_out = refs[8 + nh:8 + 2 * nh]
        acc, h_all, wbuf, load_sem, send_sems, recv_sems = refs[8 + 2 * nh:14 + 2 * nh]
        carried_sems = refs[14 + 2 * nh:]
        k, i = pl.program_id(0), pl.program_id(1)
        px_, py_, c, _, chips = _place()

        def ici(j, to):
            return pltpu.make_async_remote_copy(
                src_ref=shard_ref.at[:, c], dst_ref=wall.at[PEER_SLOT[j], :, c],
                send_sem=send_sems.at[j], recv_sem=recv_sems.at[j], device_id=to, device_id_type=MESH)

        def d2d(j, half):
            blk = wall.at[PEER_SLOT[j], :, half]
            return pltpu.make_async_remote_copy(
                src_ref=blk, dst_ref=blk, send_sem=send_sems.at[3 + j], recv_sem=recv_sems.at[3 + j],
                device_id=(px_, py_, 1 - c), device_id_type=MESH)

        @pl.when((k == 0) & (i == 0))
        def _():
            for j, (qx, qy) in enumerate(chips):
                ici(j, (qx, qy, c)).start()
            own = pltpu.make_async_copy(shard_ref, wbuf, load_sem)
            own.start()
            own.wait()
            if hosted is not None:
                hosted.start(h_in, h_out, *carried_sems)

        for j, (qx, qy) in enumerate(chips):
            @pl.when((k == PEER_SLOT[j]) & (i == 0))
            def _():
                ici(j, (qx, qy, c)).wait_recv()
                d2d(j, c).start()
                d2d(j, 1 - c).wait_recv()
                load = pltpu.make_async_copy(wall.at[PEER_SLOT[j]], wbuf, load_sem)
                load.start()
                load.wait()

        @pl.when(k == 0)
        def _():
            xv = x_ref[...]
            h0 = ((xv * _rms_scale(xv)) * g_ref[...]).astype(BF16)
            h_all[i] = h0
            h_ref[...] = h0

        h = h_all[i]
        wg, wu, wd = (wbuf[kind].reshape(FF_SHARD, D_MODEL) for kind in range(3))
        a = _dot_nt(h, wg)
        u = _dot_nt(h, wu)
        a_ref[...] = a.astype(BF16)
        u_ref[...] = u.astype(BF16)
        part = _dot(((a * jax.nn.sigmoid(a)) * u).astype(BF16), wd)

        @pl.when(k == 0)
        def _():
            acc[i] = part

        @pl.when(k > 0)
        def _():
            acc[i] += part

        @pl.when(k == N_CHIPS - 1)
        def _():
            xo_ref[...] = x_ref[...] + 0.5 * acc[i]

        @pl.when((k == N_CHIPS - 1) & (i == nt - 1))
        def _():
            for j, (qx, qy) in enumerate(chips):
                ici(j, (qx, qy, c)).wait_send()
                d2d(j, c).wait_send()
            if hosted is not None:
                hosted.finish(h_in, h_out, *carried_sems)

    def first_or_last(k):
        return (k == 0) | (k == N_CHIPS - 1)

    tok = lambda keep: pl.BlockSpec((tm, D_MODEL), lambda k, i: (jnp.where(keep(k), i, 0), 0))
    act = pl.BlockSpec((None, tm, FF_SHARD), lambda k, i: (k, i, 0))
    act_shape = jax.ShapeDtypeStruct((N_CHIPS, t, FF_SHARD), BF16)
    carried = [] if hosted is None else [pltpu.SemaphoreType.DMA((hosted.n_sems,))] * 2
    outs = pl.pallas_call(
        body, name=name, grid=(N_CHIPS, nt),
        in_specs=[tok(first_or_last), pl.BlockSpec((1, D_MODEL), lambda k, i: (0, 0)), ANY] + [ANY] * nh,
        out_specs=[tok(lambda k: k == N_CHIPS - 1), act, act,
                   pl.BlockSpec((tm, D_MODEL), lambda k, i: (jnp.where(k == 0, i, nt - 1), 0)), ANY] + [ANY] * nh,
        out_shape=[jax.ShapeDtypeStruct((t, D_MODEL), F32), act_shape, act_shape,
                   jax.ShapeDtypeStruct((t, D_MODEL), BF16),
                   jax.ShapeDtypeStruct((N_CHIPS,) + shard.shape, shard.dtype)]
                  + ([] if hosted is None else hosted.out_shapes),
        scratch_shapes=[pltpu.VMEM((nt, tm, D_MODEL), F32), pltpu.VMEM((nt, tm, D_MODEL), BF16),
                        pltpu.VMEM(shard.shape, shard.dtype), pltpu.SemaphoreType.DMA,
                        pltpu.SemaphoreType.DMA((6,)), pltpu.SemaphoreType.DMA((6,))] + carried,
        compiler_params=_ARB2,
    )(x, g, shard, *([] if hosted is None else hosted.arrays))
    return list(outs[:5]), list(outs[5:])


def _ffn_bwd(dxo, x, g, a4, u4, ffn_w, name, hosted=None):
    t = x.shape[0]
    tm = min(t, TOKEN_TILE)
    steps = (t // tm) * N_CHIPS
    cur, prev = _pipeline_items(steps)


    def body(dxo_ref, dxo_prev_ref, x_ref, g_ref, a_ref, u_ref, wg_ref, wu_ref, wd_ref,
             da_ref, du_ref, hid_ref, dob_ref, dx_ref, dg_ref, acc_ref, da_slots, du_slots):
        s = pl.program_id(0)
        jc, jp = cur(s)[1], prev(s)[1]
        slot = s % 2

        @pl.when(s == 0)
        def _():
            dg_ref[...] = jnp.zeros_like(dg_ref)
            acc_ref[...] = jnp.zeros_like(acc_ref)
            da_slots[...] = jnp.zeros_like(da_slots)
            du_slots[...] = jnp.zeros_like(du_slots)

        @pl.when(jc == 0)
        def _():
            dob_ref[...] = (0.5 * dxo_ref[...]).astype(BF16)

        dhid = _dot_nt(dob_ref[...], wd_ref[...])
        a = a_ref[...].astype(F32)
        u = u_ref[...].astype(F32)
        sg = jax.nn.sigmoid(a)
        sl = a * sg
        hid_ref[...] = (sl * u).astype(BF16)
        du = (dhid * sl).astype(BF16)
        da = (dhid * u * _silu_grad(a, sg)).astype(BF16)
        du_ref[...] = du
        da_ref[...] = da
        acc_ref[...] += _dot(da_slots[1 - slot], wg_ref[...]) + _dot(du_slots[1 - slot], wu_ref[...])
        da_slots[slot] = da
        du_slots[slot] = du

        @pl.when((jp == N_CHIPS - 1) & (s > 0))
        def _():
            dx, dg = _rms_bwd(acc_ref[...], x_ref[...], g_ref[...])
            dx_ref[...] = dxo_prev_ref[...] + dx
            dg_ref[...] += dg
            acc_ref[...] = jnp.zeros_like(acc_ref)

    tok_cur = pl.BlockSpec((tm, D_MODEL), lambda s: (cur(s)[0], 0))
    tok_prev = pl.BlockSpec((tm, D_MODEL), lambda s: (prev(s)[0], 0))
    act = pl.BlockSpec((None, tm, FF_SHARD), lambda s: (cur(s)[1], cur(s)[0], 0))
    row = pl.BlockSpec((1, D_MODEL), lambda s: (0, 0))
    w_arrays, weights = _ffn_weight_operands(ffn_w, [lambda s: prev(s)[1], lambda s: prev(s)[1], lambda s: cur(s)[1]])
    act_shape = jax.ShapeDtypeStruct((N_CHIPS, t, FF_SHARD), BF16)
    return _call(
        body, (dxo, dxo, x, g, a4, u4, *w_arrays), name=name, grid=(steps + 1,),
        in_specs=[tok_cur, tok_prev, tok_prev, row, act, act] + weights,
        out_specs=[act, act, act, tok_cur, tok_prev, row],
        out_shape=[act_shape, act_shape, act_shape,
                   jax.ShapeDtypeStruct((t, D_MODEL), BF16),
                   jax.ShapeDtypeStruct((t, D_MODEL), F32),
                   jax.ShapeDtypeStruct((1, D_MODEL), F32)],
        scratch_shapes=[pltpu.VMEM((tm, D_MODEL), F32), pltpu.VMEM((2, tm, FF_SHARD), BF16),
                        pltpu.VMEM((2, tm, FF_SHARD), BF16)],
        compiler_params=_ARB1, hosted=hosted)


def _matmul_tn(a, b, name, tka=None, out_dtype=F32, hosted=None):
    a3, b3 = a.ndim == 3, b.ndim == 3
    nb = a.shape[0] if a3 else (b.shape[0] if b3 else 1)
    t, ka, n = a.shape[-2], a.shape[-1], b.shape[-1]
    tka = ka if tka is None else tka
    tk = min(t, 2 * TOKEN_TILE)
    nk = t // tk

    def body(a_ref, b_ref, o_ref, acc_ref):
        k = pl.program_id(2)

        @pl.when(k == 0)
        def _():
            acc_ref[...] = jnp.zeros_like(acc_ref)

        acc_ref[...] += _dot_tn(a_ref[...].astype(BF16), b_ref[...].astype(BF16))

        @pl.when(k == nk - 1)
        def _():
            o_ref[...] = acc_ref[...].astype(out_dtype)

    a_spec = (pl.BlockSpec((None, tk, tka), lambda i, j, k: (i, k, j)) if a3
              else pl.BlockSpec((tk, tka), lambda i, j, k: (k, j)))
    b_spec = (pl.BlockSpec((None, tk, n), lambda i, j, k: (i, k, 0)) if b3
              else pl.BlockSpec((tk, n), lambda i, j, k: (k, 0)))
    outs, carried = _call(
        body, (a, b), name=name, grid=(nb, ka // tka, t // tk),
        in_specs=[a_spec, b_spec],
        out_specs=[pl.BlockSpec((None, tka, n), lambda i, j, k: (i, j, 0))],
        out_shape=[jax.ShapeDtypeStruct((nb, ka, n), out_dtype)],
        scratch_shapes=[pltpu.VMEM((tka, n), F32)],
        compiler_params=_ARB3, hosted=hosted)
    return outs[0] if hosted is None else (outs[0], carried)


def _matmul_nt(a, w, name, out_dtype=F32):
    t, k = a.shape
    n = w.shape[0]
    tm = min(t, TOKEN_TILE)

    def body(a_ref, w_ref, o_ref):
        o_ref[...] = _dot_nt(a_ref[...].astype(BF16), w_ref[...]).astype(out_dtype)

    return pl.pallas_call(
        body, name=name, grid=(t // tm,),
        in_specs=[pl.BlockSpec((tm, k), lambda i: (i, 0)), pl.BlockSpec((n, k), lambda i: (0, 0))],
        out_specs=pl.BlockSpec((tm, n), lambda i: (i, 0)),
        out_shape=jax.ShapeDtypeStruct((t, n), out_dtype),
        compiler_params=_ARB1,
    )(a, w)


def _mixer_in_bwd(dproj, w_in_pt, dres, x, g, name):
    t, k = dproj.shape
    tm = min(t, TOKEN_TILE)

    def body(a_ref, w_ref, dres_ref, x_ref, g_ref, dx_ref, dg_ref):
        @pl.when(pl.program_id(0) == 0)
        def _():
            dg_ref[...] = jnp.zeros_like(dg_ref)

        dh = _dot(a_ref[...], w_ref[...])
        dx, dg = _rms_bwd(dh, x_ref[...], g_ref[...])
        dx_ref[...] = dres_ref[...] + dx
        dg_ref[...] += dg

    tok = pl.BlockSpec((tm, D_MODEL), lambda i: (i, 0))
    row = pl.BlockSpec((1, D_MODEL), lambda i: (0, 0))
    return pl.pallas_call(
        body, name=name, grid=(t // tm,),
        in_specs=[pl.BlockSpec((tm, k), lambda i: (i, 0)), pl.BlockSpec((k, D_MODEL), lambda i: (0, 0)), tok, tok, row],
        out_specs=[tok, row],
        out_shape=[jax.ShapeDtypeStruct((t, D_MODEL), F32), jax.ShapeDtypeStruct((1, D_MODEL), F32)],
        compiler_params=_ARB1,
    )(dproj, w_in_pt, dres, x, g)


def _mixer_in_fwd(x, g, w_in_pt, name, hosted=None):
    t = x.shape[0]
    tm = min(t, TOKEN_TILE)
    tn = PROJ_P // 3

    def body(x_ref, g_ref, w_ref, p_ref, h_ref):
        @pl.when(pl.program_id(1) == 0)
        def _():
            xv = x_ref[...]
            h_ref[...] = ((xv * _rms_scale(xv)) * g_ref[...]).astype(BF16)

        p_ref[...] = _dot_nt(h_ref[...], w_ref[...])

    tok = pl.BlockSpec((tm, D_MODEL), lambda i, j: (i, 0))
    return _call(
        body, (x, g, w_in_pt), name=name, grid=(t // tm, 3),
        in_specs=[tok, pl.BlockSpec((1, D_MODEL), lambda i, j: (0, 0)),
                  pl.BlockSpec((tn, D_MODEL), lambda i, j: (j, 0))],
        out_specs=[pl.BlockSpec((tm, tn), lambda i, j: (i, j)), tok],
        out_shape=[jax.ShapeDtypeStruct((t, PROJ_P), F32), jax.ShapeDtypeStruct((t, D_MODEL), BF16)],
        scratch_shapes=[], compiler_params=_ARB2, hosted=hosted)


def _mixer_out_fwd(o_ret, o_gla, w_out, x, name):
    t = x.shape[0]
    tm = min(t, TOKEN_TILE)
    half = HEADS * LANES

    def body(a_ref, b_ref, w_ref, x_ref, o_ref):
        o_ref[...] = x_ref[...] + _dot(a_ref[...], w_ref[0:half, :]) + _dot(b_ref[...], w_ref[half:2 * half, :])

    tok = pl.BlockSpec((tm, D_MODEL), lambda i: (i, 0))
    hb = pl.BlockSpec((tm, half), lambda i: (i, 0))
    return pl.pallas_call(
        body, name=name, grid=(t // tm,),
        in_specs=[hb, hb, pl.BlockSpec((2 * half, D_MODEL), lambda i: (0, 0)), tok],
        out_specs=tok, out_shape=jax.ShapeDtypeStruct((t, D_MODEL), F32),
        compiler_params=_ARB1,
    )(o_ret, o_gla, w_out, x)


def _final_loss(x, g, target, name):
    t = x.shape[0]
    tm = min(t, TOKEN_TILE)

    def body(x_ref, g_ref, t_ref, l_ref, dx_ref, dg_ref):
        @pl.when(pl.program_id(0) == 0)
        def _():
            l_ref[...] = jnp.zeros_like(l_ref)
            dg_ref[...] = jnp.zeros_like(dg_ref)

        xv = x_ref[...]
        gv = g_ref[...]
        err = (xv * _rms_scale(xv)) * gv - t_ref[...]
        l_ref[...] += 0.5 * jnp.sum(jnp.mean(err * err, axis=-1, keepdims=True), axis=0, keepdims=True)
        dx, dg = _rms_bwd(err * (1.0 / D_MODEL), xv, gv)
        dx_ref[...] = dx
        dg_ref[...] += dg

    tok = pl.BlockSpec((tm, D_MODEL), lambda i: (i, 0))
    row = pl.BlockSpec((1, D_MODEL), lambda i: (0, 0))
    return pl.pallas_call(
        body, name=name, grid=(t // tm,),
        in_specs=[tok, row, tok],
        out_specs=[pl.BlockSpec((8, LANES), lambda i: (0, 0)), tok, row],
        out_shape=[jax.ShapeDtypeStruct((8, LANES), F32), jax.ShapeDtypeStruct((t, D_MODEL), F32),
                   jax.ShapeDtypeStruct((1, D_MODEL), F32)],
        compiler_params=_ARB1,
    )(x, g, target)


def _rot(v, cos, sa, sb):
    return v * cos + pltpu.roll(v, 96, 1) * sa + pltpu.roll(v, 32, 1) * sb


def _rot_t(d, cos, sa, sb):
    return d * cos + pltpu.roll(d * sa, 32, 1) + pltpu.roll(d * sb, 96, 1)


def _bmm(a, b):
    return jnp.einsum("cik,ckj->cij", a, b, preferred_element_type=F32)


def _bmm_nt(a, b):
    return jnp.einsum("cik,cjk->cij", a, b, preferred_element_type=F32)


def _bmm_tn(a, b):
    return jnp.einsum("cki,ckj->cij", a, b, preferred_element_type=F32)


def _masked_sum(mask, x):
    hi = x.astype(BF16)
    r1 = x - hi.astype(F32)
    mid = r1.astype(BF16)
    lo = (r1 - mid.astype(F32)).astype(BF16)
    return _bmm(mask, hi) + _bmm(mask, mid) + _bmm(mask, lo)


def _tile_inputs(is_ret, qkvg_refs, aux, nc):
    shape3 = (nc, CHUNK, LANES)
    q_raw, k_raw, v, gate = (r[...] for r in qkvg_refs)
    ri = lax.broadcasted_iota(jnp.int32, (nc, CHUNK, CHUNK), 1)
    ci = lax.broadcasted_iota(jnp.int32, (nc, CHUNK, CHUNK), 2)
    if is_ret:
        cos_ref, sa_ref, sb_ref, lg_ref = aux
        cos, sa, sb = cos_ref[...], sa_ref[...], sb_ref[...]
        q = _rot(q_raw, cos, sa, sb)
        k = _rot(k_raw, cos, sa, sb) * QK_SCALE
        steps = (lax.broadcasted_iota(jnp.int32, shape3, 1) + 1).astype(F32)
        b = steps * lg_ref[...]
        logit = jnp.exp(jnp.abs(ri - ci).astype(F32) * lg_ref[:, 0:CHUNK])
    else:
        glow_ref, wa2_ref, ba_ref = aux
        logit = _dot(glow_ref[...].astype(BF16), wa2_ref[...]) + ba_ref[...]
        la = (jnp.minimum(logit, 0.0) - jnp.log1p(jnp.exp(-jnp.abs(logit)))) * (1.0 / GATE_NORM)
        b = _masked_sum((ci <= ri).astype(BF16), la.reshape(shape3))
        q = q_raw * QK_SCALE
        k = k_raw
    return q.reshape(shape3), k.reshape(shape3), v.reshape(shape3), gate, b, logit, ri, ci


def _tile_scores(q, k, b, ri, ci):
    mid = b[:, CHUNK // 2 - 1:CHUNK // 2, :]
    ep = jnp.exp(b - mid)
    en = jnp.exp(mid - b)
    qt, kt, qh, kh = q * ep, k * en, q * en, k * ep
    low = _bmm_nt(qt.astype(BF16), kt.astype(BF16))
    upp = _bmm_nt(qh.astype(BF16), kh.astype(BF16))
    scores = jnp.where(ci <= ri, low, upp)
    return scores, ep, en, qt, kt, qh, kh


def _attn_specs(is_ret, t, tb, imap_t):
    nb = t // tb
    base = 0 if is_ret else 4 * HEADS
    proj = [pl.BlockSpec((tb, LANES), lambda h, i, s=sec: (imap_t(i), base + HEADS * s + h)) for sec in range(4)]
    lane_t = pl.BlockSpec((tb, LANES), lambda h, i: (imap_t(i), 0))
    if is_ret:
        aux = [lane_t, lane_t, lane_t, pl.BlockSpec((None, 1, LANES), lambda h, i: (h, 0, 0))]
    else:
        aux = [pl.BlockSpec((tb, LANES), lambda h, i: (imap_t(i), PROJ_P // LANES - 1)),
               pl.BlockSpec((LANES, LANES), lambda h, i: (0, h)),
               pl.BlockSpec((1, LANES), lambda h, i: (0, h))]
    gain = pl.BlockSpec((1, LANES), lambda h, i: (0, h))
    head_t = pl.BlockSpec((tb, LANES), lambda h, i: (imap_t(i), h))
    state = pl.BlockSpec((None, tb // CHUNK, LANES, LANES), lambda h, i: (h, imap_t(i), 0, 0))
    return nb, proj, aux, gain, head_t, state


def _attn_fwd(is_ret, proj, aux_arrays, gain, name, hosted=None):
    t = proj.shape[0]
    tb = min(t, ATTN_TILE)
    nc = tb // CHUNK
    n_aux = 4 if is_ret else 3
    nb, proj_spec, aux_specs, gain_spec, head_t, state_spec = _attn_specs(is_ret, t, tb, lambda i: i)

    def body(*refs):
        qkvg_refs = refs[0:4]
        aux = refs[4:4 + n_aux]
        gn_ref, ofin_ref, oraw_ref, st_ref, state = refs[4 + n_aux:]

        @pl.when(pl.program_id(1) == 0)
        def _():
            state[...] = jnp.zeros_like(state)

        q, k, v, gate, b, decay, ri, ci = _tile_inputs(is_ret, qkvg_refs, aux, nc)
        if is_ret:
            scores = _bmm_nt(q.astype(BF16), k.astype(BF16)) * decay
        else:
            scores = _tile_scores(q, k, b, ri, ci)[0]
        vb = v.astype(BF16)
        intra = _bmm(scores.astype(BF16), vb)
        b_last = b[:, CHUNK - 1:CHUNK, :]
        e_last = jnp.exp(b_last)
        grow = _bmm_tn(vb, (k * jnp.exp(b_last - b)).astype(BF16))
        st = state[...]
        for c in range(nc):
            st_ref[c] = st
            st = st * e_last[c] + grow[c]
        state[...] = st
        inter = _bmm_nt((q * jnp.exp(b)).astype(BF16), st_ref[...].astype(BF16))
        out = (intra + inter).reshape(tb, LANES)
        oraw_ref[...] = out
        normed = out * _rms_scale(out)
        ofin_ref[...] = ((normed * gn_ref[...]) * (gate * jax.nn.sigmoid(gate))).astype(BF16)

    width = HEADS * LANES
    return _call(
        body, (proj, proj, proj, proj, *aux_arrays, gain), name=name, grid=(HEADS, nb),
        in_specs=proj_spec + aux_specs + [gain_spec],
        out_specs=[head_t, head_t, state_spec],
        out_shape=[jax.ShapeDtypeStruct((t, width), BF16), jax.ShapeDtypeStruct((t, width), F32),
                   jax.ShapeDtypeStruct((HEADS, t // CHUNK, LANES, LANES), F32)],
        scratch_shapes=[pltpu.VMEM((LANES, LANES), F32)],
        compiler_params=_ARB2, hosted=hosted)


def _attn_bwd(is_ret, proj, aux_arrays, gain, o_raw, states, d_out, name):
    t = proj.shape[0]
    tb = min(t, ATTN_TILE)
    nc = tb // CHUNK
    n_aux = 4 if is_ret else 3
    nblk = t // tb
    nb, proj_spec, aux_specs, gain_spec, head_t, state_spec = _attn_specs(is_ret, t, tb, lambda i: nblk - 1 - i)
    base = 0 if is_ret else HEADS
    dout_spec = pl.BlockSpec((tb, LANES), lambda h, i: (nblk - 1 - i, base + h))

    def body(*refs):
        qkvg_refs = refs[0:4]
        aux = refs[4:4 + n_aux]
        gn_ref, oraw_ref, st_ref, dfin_ref = refs[4 + n_aux:8 + n_aux]
        dq_ref, dk_ref, dv_ref, dgate_ref, dgn_ref = refs[8 + n_aux:13 + n_aux]
        if is_ret:
            dstate, dafter_ref = refs[13 + n_aux:]
        else:
            dlogit_ref, dba_ref, dstate, dafter_ref = refs[13 + n_aux:]

        @pl.when(pl.program_id(1) == 0)
        def _():
            dstate[...] = jnp.zeros_like(dstate)
            dgn_ref[...] = jnp.zeros_like(dgn_ref)
            if not is_ret:
                dba_ref[...] = jnp.zeros_like(dba_ref)

        shape3 = (nc, CHUNK, LANES)
        q, k, v, gate, b, logit, ri, ci = _tile_inputs(is_ret, qkvg_refs, aux, nc)
        eb = jnp.exp(b)
        qe = q * eb
        b_last = b[:, CHUNK - 1:CHUNK, :]
        e_last = jnp.exp(b_last)
        ekd = jnp.exp(b_last - b)
        kd = k * ekd

        gn = gn_ref[...]
        out = oraw_ref[...]
        r = _rms_scale(out)
        normed = out * r
        sg = jax.nn.sigmoid(gate)
        dfin = dfin_ref[...]
        dgate = dfin * (normed * gn) * _silu_grad(gate, sg)
        dpre = dfin * (gate * sg)
        dgn_ref[...] += jnp.sum(dpre * normed, axis=0, keepdims=True)
        dnormed = dpre * gn
        d_o = r * (dnormed - normed * jnp.mean(dnormed * normed, axis=-1, keepdims=True))
        dob, vb = d_o.reshape(shape3).astype(BF16), v.astype(BF16)

        dgrow = _bmm_tn(dob, qe.astype(BF16))
        dst = dstate[...]
        for c in reversed(range(nc)):
            dafter_ref[c] = dst
            dst = dst * e_last[c] + dgrow[c]
        dstate[...] = dst
        st = st_ref[...]
        dafter = dafter_ref[...]
        stb, dafter_b = st.astype(BF16), dafter.astype(BF16)

        dsc = _bmm_nt(dob, vb)
        dsc_t = _bmm_nt(vb, dob)
        dqe = _bmm(dob, stb)
        dkd = _bmm(vb, dafter_b)
        if is_ret:
            decay, qb, kb = logit, q.astype(BF16), k.astype(BF16)
            scores_t = _bmm_nt(kb, qb) * decay
            dq = _bmm((dsc * decay).astype(BF16), kb) + dqe * eb
            dk = _bmm((dsc_t * decay).astype(BF16), qb) + dkd * ekd
        else:
            _, ep, en, qt, kt, qh, kh = _tile_scores(q, k, b, ri, ci)
            qtb, ktb, qhb, khb = qt.astype(BF16), kt.astype(BF16), qh.astype(BF16), kh.astype(BF16)
            scores_t = jnp.where(ci >= ri, _bmm_nt(ktb, qtb), _bmm_nt(khb, qhb))
            dqt = _bmm(jnp.where(ci <= ri, dsc, 0.0).astype(BF16), ktb)
            dqh = _bmm(jnp.where(ci <= ri, 0.0, dsc).astype(BF16), khb)
            dkt = _bmm(jnp.where(ci >= ri, dsc_t, 0.0).astype(BF16), qtb)
            dkh = _bmm(jnp.where(ci >= ri, 0.0, dsc_t).astype(BF16), qhb)
            dq = dqt * ep + dqh * en + dqe * eb
            dk = dkt * en + dkh * ep + dkd * ekd
        dv = _bmm(scores_t.astype(BF16), dob) + _bmm_nt(kd.astype(BF16), dafter_b)
        dq, dk = dq.reshape(tb, LANES), dk.reshape(tb, LANES)

        if is_ret:
            cos_ref, sa_ref, sb_ref, _ = aux
            cos, sa, sb = cos_ref[...], sa_ref[...], sb_ref[...]
            dq_raw = _rot_t(dq, cos, sa, sb)
            dk_raw = _rot_t(dk, cos, sa, sb) * QK_SCALE
        else:
            dq_raw = dq * QK_SCALE
            dk_raw = dk
            db = dqt * qt - dkt * kt - dqh * qh + dkh * kh + dqe * qe - dkd * kd
            db_last = (jnp.sum(dkd * kd, axis=1, keepdims=True)
                       + jnp.sum(dafter * st, axis=1, keepdims=True) * e_last)
            last_row = lax.broadcasted_iota(jnp.int32, shape3, 1) == CHUNK - 1
            db = db + jnp.where(last_row, db_last, 0.0)
            dla = _masked_sum((ci >= ri).astype(BF16), db).reshape(tb, LANES)
            dlogit = dla * (1.0 / GATE_NORM) * jax.nn.sigmoid(-logit)
            dlogit_ref[...] = dlogit.astype(BF16)
            dba_ref[...] += jnp.sum(dlogit, axis=0, keepdims=True)

        dq_ref[...] = dq_raw.astype(BF16)
        dk_ref[...] = dk_raw.astype(BF16)
        dv_ref[...] = dv.reshape(tb, LANES).astype(BF16)
        dgate_ref[...] = dgate.astype(BF16)

    width = HEADS * LANES
    row_out = pl.BlockSpec((1, LANES), lambda h, i: (0, h))
    out_specs = [head_t] * 4 + [row_out]
    out_shape = [jax.ShapeDtypeStruct((t, width), BF16)] * 4 + [jax.ShapeDtypeStruct((1, width), F32)]
    if not is_ret:
        out_specs += [head_t, row_out]
        out_shape += [jax.ShapeDtypeStruct((t, width), BF16), jax.ShapeDtypeStruct((1, width), F32)]
    return pl.pallas_call(
        body, name=name, grid=(HEADS, nblk),
        in_specs=proj_spec + aux_specs + [gain_spec, head_t, state_spec, dout_spec],
        out_specs=out_specs, out_shape=out_shape,
        scratch_shapes=[pltpu.VMEM((LANES, LANES), F32), pltpu.VMEM((nc, LANES, LANES), F32)],
        compiler_params=_ARB2,
    )(proj, proj, proj, proj, *aux_arrays, gain, o_raw, states, d_out)


PEER_SLOT = (2, 1, 3)


def _place():
    x, y, c = lax.axis_index("x"), lax.axis_index("y"), lax.axis_index("c")
    chips = [(1 - x, y), (x, 1 - y), (1 - x, 1 - y)]
    return x, y, c, 2 * x + y, chips


def _gather_plan(arrs):
    na = len(arrs)

    def copies(ins, outs, send_sems, recv_sems):
        x, y, c, me, chips = _place()

        def ici(a, j, src_chip, to):
            return pltpu.make_async_remote_copy(
                src_ref=ins[a].at[:, c], dst_ref=outs[a].at[src_chip, :, c],
                send_sem=send_sems.at[6 * a + j], recv_sem=recv_sems.at[6 * a + j], device_id=to, device_id_type=MESH)

        def d2d(a, j, src_chip, half):
            blk = outs[a].at[src_chip, :, half]
            return pltpu.make_async_remote_copy(
                src_ref=blk, dst_ref=blk, send_sem=send_sems.at[6 * a + 3 + j], recv_sem=recv_sems.at[6 * a + 3 + j],
                device_id=(x, y, 1 - c), device_id_type=MESH)

        peers = [(a, j, px, py) for a in range(na) for j, (px, py) in enumerate(chips)]
        return c, me, peers, ici, d2d

    def start(*refs):
        c, me, peers, ici, _ = copies(*refs)
        for a, j, px, py in peers:
            ici(a, j, me, (px, py, c)).start()

    def finish(*refs):
        c, me, peers, ici, d2d = copies(*refs)
        for a, j, px, py in peers:
            ici(a, j, 2 * px + py, (px, py, c)).wait_recv()
            d2d(a, j, 2 * px + py, c).start()
        for a, j, px, py in peers:
            d2d(a, j, 2 * px + py, 1 - c).wait_recv()
        for a, j, px, py in peers:
            ici(a, j, me, (px, py, c)).wait_send()
            d2d(a, j, 2 * px + py, c).wait_send()

    return _Hosted(arrs, [jax.ShapeDtypeStruct((N_CHIPS,) + a.shape, a.dtype) for a in arrs], 6 * na, start, finish)


def _pair_exchange(grads, name):
    na = len(grads)

    def body(*refs):
        ins, outs = refs[:na], refs[na:2 * na]
        send_sems, recv_sems = refs[2 * na:]
        x, y, c, _, _ = _place()
        copies = [pltpu.make_async_remote_copy(
            src_ref=ins[a].at[:, 1 - c], dst_ref=outs[a], send_sem=send_sems.at[a], recv_sem=recv_sems.at[a],
            device_id=(x, y, 1 - c), device_id_type=MESH) for a in range(na)]
        for cp in copies:
            cp.start()
        for cp in copies:
            cp.wait()

    return pl.pallas_call(
        body, name=name,
        in_specs=[ANY] * na, out_specs=[ANY] * na,
        out_shape=[jax.ShapeDtypeStruct(g.shape[:1] + g.shape[2:], g.dtype) for g in grads],
        scratch_shapes=[pltpu.SemaphoreType.DMA((na,)), pltpu.SemaphoreType.DMA((na,))],
    )(*grads)


def _pair_add(grad, recv, c_arr, name):
    _, _, r, cols = grad.shape

    def body(c_ref, g_ref, r_ref, o_ref):
        o_ref[...] = (g_ref[...].astype(F32) + r_ref[...].astype(F32)).astype(BF16)

    return pl.pallas_call(
        body, name=name,
        grid_spec=pltpu.PrefetchScalarGridSpec(
            num_scalar_prefetch=1, grid=(N_CHIPS,),
            in_specs=[pl.BlockSpec((None, None, r, cols), lambda p, c_ref: (p, c_ref[0], 0, 0)),
                      pl.BlockSpec((None, r, cols), lambda p, c_ref: (p, 0, 0))],
            out_specs=pl.BlockSpec((None, r, cols), lambda p, c_ref: (p, 0, 0))),
        out_shape=jax.ShapeDtypeStruct((N_CHIPS, r, cols), BF16),
        compiler_params=_ARB1,
    )(c_arr, grad, recv)


def _chip_exchange_plan(sums, by_peer=False):
    na = len(sums)

    def copies(ins, outs, send_sems, recv_sems):
        x, y, c, me, chips = _place()

        def copy(a, j, px, py, block, slot):
            return pltpu.make_async_remote_copy(
                src_ref=ins[a].at[block], dst_ref=outs[a].at[slot],
                send_sem=send_sems.at[3 * a + j], recv_sem=recv_sems.at[3 * a + j],
                device_id=(px, py, c), device_id_type=MESH)

        peers = [(a, j, px, py) for a in range(na) for j, (px, py) in enumerate(chips)]
        return me, peers, copy

    def start(*refs):
        me, peers, copy = copies(*refs)
        for a, j, px, py in peers:
            if by_peer:
                copy(a, j, px, py, PEER_SLOT[j], PEER_SLOT[j]).start()
            else:
                copy(a, j, px, py, 2 * px + py, me).start()

    def finish(*refs):
        me, peers, copy = copies(*refs)
        for a, j, px, py in peers:
            if by_peer:
                copy(a, j, px, py, PEER_SLOT[j], PEER_SLOT[j]).wait_recv()
            else:
                copy(a, j, px, py, me, 2 * px + py).wait_recv()
        for a, j, px, py in peers:
            if by_peer:
                copy(a, j, px, py, PEER_SLOT[j], PEER_SLOT[j]).wait_send()
            else:
                copy(a, j, px, py, 2 * px + py, me).wait_send()

    return _Hosted(sums, [jax.ShapeDtypeStruct(s.shape, s.dtype) for s in sums], 3 * na, start, finish)


def _chip_sum(own, recv, me_arr, name):
    _, r, cols = recv.shape

    def body(me_ref, own_ref, r_ref, o_ref):
        o_ref[...] = jnp.zeros_like(o_ref)
        for q in range(N_CHIPS):
            @pl.when(me_ref[0] == q)
            def _():
                o_ref[...] += own_ref[...].astype(F32)

            @pl.when(me_ref[0] != q)
            def _():
                o_ref[...] += r_ref[q].astype(F32)

    return pl.pallas_call(
        body, name=name,
        grid_spec=pltpu.PrefetchScalarGridSpec(
            num_scalar_prefetch=1, grid=(1,),
            in_specs=[pl.BlockSpec((None, r, cols), lambda i, me_ref: (me_ref[0], 0, 0)),
                      pl.BlockSpec((N_CHIPS, r, cols), lambda i, me_ref: (0, 0, 0))],
            out_specs=pl.BlockSpec((r, cols), lambda i, me_ref: (0, 0))),
        out_shape=jax.ShapeDtypeStruct((r, cols), F32),
        compiler_params=_ARB1,
    )(me_arr, own, recv)


def _peer_sum(own, recv, name):
    _, r, cols = recv.shape

    def body(own_ref, r_ref, o_ref):
        acc = own_ref[...].astype(F32) + r_ref[1].astype(F32)
        acc = acc + r_ref[2].astype(F32)
        o_ref[...] = acc + r_ref[3].astype(F32)

    return pl.pallas_call(
        body, name=name, grid=(1,),
        in_specs=[pl.BlockSpec((None, r, cols), lambda i: (0, 0, 0)), pl.BlockSpec((N_CHIPS, r, cols), lambda i: (0, 0, 0))],
        out_specs=pl.BlockSpec((r, cols), lambda i: (0, 0)),
        out_shape=jax.ShapeDtypeStruct((r, cols), F32),
        compiler_params=_ARB1,
    )(own, recv)


def _pair_share(halves):
    na = len(halves)

    def body(*refs):
        ins, outs = refs[:na], refs[na:2 * na]
        send_sems, recv_sems = refs[2 * na:]
        x, y, c, _, _ = _place()
        copies = [pltpu.make_async_remote_copy(
            src_ref=ins[a], dst_ref=outs[a], send_sem=send_sems.at[a], recv_sem=recv_sems.at[a],
            device_id=(x, y, 1 - c), device_id_type=MESH) for a in range(na)]
        for cp in copies:
            cp.start()
        for cp in copies:
            cp.wait()

    return pl.pallas_call(
        body, name="pair_share",
        in_specs=[ANY] * na, out_specs=[ANY] * na,
        out_shape=[jax.ShapeDtypeStruct(h.shape, h.dtype) for h in halves],
        scratch_shapes=[pltpu.SemaphoreType.DMA((na,)), pltpu.SemaphoreType.DMA((na,))],
    )(*halves)


def _small_allreduce(block):
    m, n = block.shape

    def body(x_ref, all_ref, sum_ref, send_sems, recv_sems, local_sem):
        x, y, c, _, chips = _place()
        me, sibling = (x, y, c), (x, y, 1 - c)

        def rows(px, py, pc):
            return all_ref.at[pl.ds((4 * px + 2 * py + pc) * m, m), :]

        def copy(k, blk, to, src=None):
            return pltpu.make_async_remote_copy(
                src_ref=rows(*blk) if src is None else src, dst_ref=rows(*blk),
                send_sem=send_sems.at[k], recv_sem=recv_sems.at[k], device_id=to, device_id_type=MESH)

        mine = pltpu.make_async_copy(x_ref, rows(*me), local_sem)
        mine.start()
        first = [copy(0, me, sibling, src=x_ref)]
        first += [copy(1 + j, me, (*chip, c), src=x_ref) for j, chip in enumerate(chips)]
        for cp in first:
            cp.start()
        passed = [copy(4 + j, (*chip, c), sibling) for j, chip in enumerate(chips)]
        for j, chip in enumerate(chips):
            copy(1 + j, (*chip, c), me).wait_recv()
            passed[j].start()
        copy(0, sibling, me).wait_recv()
        for j, chip in enumerate(chips):
            copy(4 + j, (*chip, 1 - c), me).wait_recv()
        for cp in first + passed:
            cp.wait_send()
        mine.wait()
        acc = all_ref[0:m, :]
        for d in range(1, 8):
            acc = acc + all_ref[d * m:(d + 1) * m, :]
        sum_ref[...] = acc

    vmem = pl.BlockSpec(memory_space=pltpu.VMEM)
    return pl.pallas_call(
        body, name="small_allreduce",
        in_specs=[vmem], out_specs=[vmem, vmem],
        out_shape=[jax.ShapeDtypeStruct((8 * m, n), F32), jax.ShapeDtypeStruct((m, n), F32)],
        scratch_shapes=[pltpu.SemaphoreType.DMA((7,)), pltpu.SemaphoreType.DMA((7,)), pltpu.SemaphoreType.DMA],
    )(block)[1]


def _row_tile(rows):
    best = rows
    for cand in range(8, min(rows, 512) + 1, 8):
        if rows % cand == 0:
            best = cand
    return best


def _adamw_math(w, g, m, v):
    m2 = ADAM_B1 * m + (1.0 - ADAM_B1) * g
    v2 = ADAM_B2 * v + (1.0 - ADAM_B2) * (g * g)
    m_hat = m2 / (1.0 - ADAM_B1 ** ADAM_STEP)
    v_hat = v2 / (1.0 - ADAM_B2 ** ADAM_STEP)
    return -ADAM_LR * (m_hat / (jnp.sqrt(v_hat) + ADAM_EPS) + ADAM_WD * w), m2, v2


def _adamw_halves(w, g_mine, g_other, m, v, c_arr, name):
    rows, cols = w.shape
    r = rows // 2
    tr = _row_tile(r)
    nt = r // tr

    def body(c_ref, w_ref, gm_ref, go_ref, m_ref, v_ref, g_ref, d_ref, nm_ref, nv_ref):
        gv = jnp.where(pl.program_id(0) == c_ref[0], gm_ref[...], go_ref[...])
        g_ref[...] = gv
        d_ref[...], nm_ref[...], nv_ref[...] = _adamw_math(w_ref[...], gv, m_ref[...], v_ref[...])

    full = pl.BlockSpec((tr, cols), lambda h, i, c_ref: (h * nt + i, 0))
    half = pl.BlockSpec((tr, cols), lambda h, i, c_ref: (i, 0))
    shape = jax.ShapeDtypeStruct((rows, cols), F32)
    return pl.pallas_call(
        body, name=name,
        grid_spec=pltpu.PrefetchScalarGridSpec(
            num_scalar_prefetch=1, grid=(2, nt),
            in_specs=[full, half, half, full, full], out_specs=[full] * 4),
        out_shape=[shape] * 4,
        compiler_params=_ARB2,
    )(c_arr, w, g_mine, g_other, m, v)


def _adamw(w, g, m, v, name):
    rows, cols = w.shape
    tr = _row_tile(rows)

    def body(w_ref, g_ref, m_ref, v_ref, d_ref, nm_ref, nv_ref):
        d_ref[...], nm_ref[...], nv_ref[...] = _adamw_math(w_ref[...], g_ref[...], m_ref[...], v_ref[...])

    spec = pl.BlockSpec((tr, cols), lambda i: (i, 0))
    shape = jax.ShapeDtypeStruct((rows, cols), F32)
    return pl.pallas_call(
        body, name=name, grid=(rows // tr,),
        in_specs=[spec] * 4, out_specs=[spec] * 3, out_shape=[shape] * 3,
        compiler_params=_ARB1,
    )(w, g, m, v)


def _pad_w_in_t(w_in_t):
    def heads_padded(sec):
        return jnp.pad(sec.reshape(HEADS, 64, -1), ((0, 0), (0, LANES - 64), (0, 0))).reshape(HEADS * LANES, -1)

    w = w_in_t
    return jnp.concatenate([
        heads_padded(w[0:256]), heads_padded(w[256:512]), w[512:1536],
        heads_padded(w[1536:1792]), heads_padded(w[1792:2048]), w[2048:3072],
        jnp.pad(w[3072:3088], ((0, LANES - GATE_RANK), (0, 0)))], axis=0)


def _unpad_w_in_t(w_pt):
    def heads_unpadded(sec):
        return sec.reshape(HEADS, LANES, -1)[:, 0:64].reshape(HEADS * 64, -1)

    p = w_pt
    return jnp.concatenate([
        heads_unpadded(p[0:512]), heads_unpadded(p[512:1024]), p[1024:2048],
        heads_unpadded(p[2048:2560]), heads_unpadded(p[2560:3072]), p[3072:4096],
        p[4096:4096 + GATE_RANK]], axis=0)


def _rope_tables(t):
    half = 32
    inv = ROPE_BASE ** (-jnp.arange(half, dtype=F32) * 2.0 / 64)
    ang = jnp.arange(t, dtype=F32)[:, None] * inv[None, :]
    cos, sin = jnp.cos(ang), jnp.sin(ang)
    z32, z64 = jnp.zeros((t, 32), F32), jnp.zeros((t, 64), F32)
    return (jnp.concatenate([cos, cos, z64], axis=1),
            jnp.concatenate([-sin, z32, z64], axis=1),
            jnp.concatenate([z32, sin, z64], axis=1))


def _halves(w):
    n, rows, cols = w.shape
    return w.reshape(n, 2, rows // 2, cols)


_VMEM = pl.BlockSpec(memory_space=pltpu.VMEM)


def _pack_small(n1, nm, n2, nf, nret, ngla, ba, wa2_p, loss_blk):
    def body(n1_ref, nm_ref, n2_ref, nf_ref, nret_ref, ngla_ref, ba_ref, wa2_ref, loss_ref, o_ref):
        o_ref[...] = jnp.zeros_like(o_ref)
        o_ref[0:1, :] = n1_ref[...]
        o_ref[1:2, :] = nm_ref[...]
        o_ref[2:3, :] = n2_ref[...]
        o_ref[3:4, :] = nf_ref[...]
        o_ref[4:5, 0:512] = nret_ref[...]
        o_ref[4:5, 512:1024] = ngla_ref[...]
        o_ref[5:6, 0:256] = ba_ref[...]
        o_ref[6:7, 0:LANES] = loss_ref[0:1, :]
        o_ref[8:8 + GATE_RANK, 0:HEADS * LANES] = wa2_ref[0:GATE_RANK, :]

    return pl.pallas_call(
        body, name="pack_small", in_specs=[_VMEM] * 9, out_specs=_VMEM,
        out_shape=jax.ShapeDtypeStruct((SMALL_ROWS, D_MODEL), F32),
    )(n1, nm, n2, nf, nret, ngla, ba, wa2_p, loss_blk)


def _small_update(summed, chip_arr, ws, ms, vs):
    n = len(ws)

    def body(chip_ref, s_ref, *refs):
        w_refs, m_refs, v_refs = refs[0:n], refs[n:2 * n], refs[2 * n:3 * n]
        outs = refs[3 * n:]
        wa2_all = s_ref[8:8 + GATE_RANK, 0:HEADS * LANES]
        wa2_g = jnp.zeros((GATE_RANK, 64), F32)
        for p in range(N_CHIPS):
            wa2_g = jnp.where(chip_ref[0] == p, wa2_all[:, LANES * p:LANES * p + 64], wa2_g)
        grads = [s_ref[0:1, :], s_ref[1:2, :], s_ref[2:3, :], s_ref[3:4, :], s_ref[4:5, 0:512],
                 s_ref[4:5, 512:1024], s_ref[5:6, 0:256], wa2_g]
        for k in range(n):
            d, m2, v2 = _adamw_math(w_refs[k][...], grads[k], m_refs[k][...], v_refs[k][...])
            outs[k][...] = grads[k]
            outs[n + k][...] = d
            outs[2 * n + k][...] = m2
            outs[3 * n + k][...] = v2

    shapes = [jax.ShapeDtypeStruct(w.shape, F32) for w in ws] * 4
    smem = pl.BlockSpec(memory_space=pltpu.SMEM)
    outs = pl.pallas_call(
        body, name="small_update", in_specs=[smem] + [_VMEM] * (1 + 3 * n), out_specs=[_VMEM] * (4 * n),
        out_shape=shapes,
    )(chip_arr, summed, *ws, *ms, *vs)
    return outs[0:n], outs[n:2 * n], outs[2 * n:3 * n], outs[3 * n:4 * n]


def _pad_in_rows(w_t):
    return jnp.pad(w_t, ((0, IN_ROWS - IN_SHARD), (0, 0)))


def _forward_backward(xs, target, ffn1_w, rest, ba_p, ffn1_norm_g, mix_norm_g, ret_norm_g, gla_norm_g, ffn2_norm_g,
                      final_norm_g, ffn1_gather=None, rest_plan=None, rest_weights=None, ffn2_plans=None,
                      ffn2_weights=None, early=None, late=None):
    t = xs.shape[0]
    cos_t, sa_t, sb_t = _rope_tables(t)
    log_gamma = jnp.log(1.0 - 2.0 ** (-5.0 - jnp.arange(HEADS, dtype=F32)))
    lg_t = jnp.broadcast_to(log_gamma[:, None, None], (HEADS, 1, LANES))
    ret_aux = [cos_t, sa_t, sb_t, lg_t]

    if ffn1_gather is None:
        (x1, a1, u1, h1), gathered = _ffn_fwd(xs, ffn1_norm_g, ffn1_w, "ffn1_fwd", hosted=rest_plan)
    else:
        ffn1_shard, ffn1_weights = ffn1_gather
        (x1, a1, u1, h1, wall), gathered = _ffn1_fwd_gathering(xs, ffn1_norm_g, ffn1_shard, "ffn1_fwd",
                                                               hosted=rest_plan)
        ffn1_w = ffn1_weights(wall)
    ffn2_w, w_in_pt, w_out_full, wa2_p = rest if rest_plan is None else rest_weights(gathered)
    plans = [None] * 3 if ffn2_plans is None else ffn2_plans
    (proj, h_mix), got_gate = _mixer_in_fwd(x1, mix_norm_g, w_in_pt, "mixer_in_fwd", hosted=plans[0])
    gla_aux = [proj, wa2_p, ba_p]
    (o_ret, raw_ret, st_ret), got_up = _attn_fwd(True, proj, ret_aux, ret_norm_g, "ret_fwd", hosted=plans[1])
    (o_gla, raw_gla, st_gla), got_down = _attn_fwd(False, proj, gla_aux, gla_norm_g, "gla_fwd", hosted=plans[2])
    if ffn2_plans is not None:
        ffn2_w = ffn2_weights(got_gate + got_up + got_down)
    x2 = _mixer_out_fwd(o_ret, o_gla, w_out_full, x1, "mixer_out_fwd")
    (x3, a2, u2, h2), _ = _ffn_fwd(x2, ffn2_norm_g, ffn2_w, "ffn2_fwd")
    loss_blk, dx3, d_final_g = _final_loss(x3, final_norm_g, target, "final_loss")

    (da2, du2, hid2, dob2, dx2, d_ffn2_g), _ = _ffn_bwd(dx3, x2, ffn2_norm_g, a2, u2, ffn2_w, "ffn2_bwd")
    g_gate2 = _matmul_tn(da2, h2, "ffn2_dgate", out_dtype=BF16)
    g_up2 = _matmul_tn(du2, h2, "ffn2_dup", out_dtype=BF16)
    g_down2 = _matmul_tn(hid2, dob2, "ffn2_ddown", out_dtype=BF16)

    d_o = _matmul_nt(dx2, w_out_full, "mixer_out_bwd")
    g_wout_ret = _matmul_tn(o_ret, dx2, "wout_grad_ret", out_dtype=BF16)
    g_wout_gla = _matmul_tn(o_gla, dx2, "wout_grad_gla", out_dtype=BF16)
    *dproj_ret, d_ret_g = _attn_bwd(True, proj, ret_aux, ret_norm_g, raw_ret, st_ret, d_o, "ret_bwd")
    *dproj_gla, d_gla_g, dlogit, d_ba_p = _attn_bwd(False, proj, gla_aux, gla_norm_g, raw_gla, st_gla, d_o, "gla_bwd")
    d_glow = _matmul_nt(dlogit, wa2_p, "gate_low_bwd", out_dtype=BF16)
    g_wa2_p = _matmul_tn(proj[:, PROJ_P - LANES:], dlogit, "gate_w_grad")
    dproj = jnp.concatenate(dproj_ret + dproj_gla + [d_glow], axis=1)
    g_win_p = _matmul_tn(dproj, h_mix, "w_in_grad", tka=PROJ_P // 3, out_dtype=BF16)
    dx1, d_mix_g = _mixer_in_bwd(dproj, w_in_pt, dx2, x1, mix_norm_g, "mixer_in_bwd")
    g_win_t = _unpad_w_in_t(g_win_p[0])
    g_win = jnp.stack([_pad_in_rows(g_win_t[IN_SHARD * p:IN_SHARD * (p + 1)]) for p in range(N_CHIPS)], axis=0)
    g_wout = jnp.concatenate([g_wout_ret[0], g_wout_gla[0]], axis=0).reshape(N_CHIPS, D_MODEL // N_CHIPS, D_MODEL)

    early_plan = None if early is None else early([g_gate2, g_up2, g_down2, g_win, g_wout])
    (da1, du1, hid1, dob1, grad_x, d_ffn1_g), arrived = _ffn_bwd(dx1, xs, ffn1_norm_g, a1, u1, ffn1_w, "ffn1_bwd",
                                                                hosted=early_plan)
    late_grads, late_arrived = [], []
    for lhs, rhs, name in ((da1, h1, "ffn1_dgate"), (du1, h1, "ffn1_dup"), (hid1, dob1, "ffn1_ddown")):
        plan = None if late is None or not late_grads else late(late_grads[-1], len(late_grads))
        res = _matmul_tn(lhs, rhs, name, out_dtype=BF16, hosted=plan)
        if plan is not None:
            res, carried = res
            late_arrived += carried
        late_grads.append(res)
    g_gate1, g_up1, g_down1 = late_grads

    return (loss_blk, grad_x, g_gate1, g_up1, g_down1, g_gate2, g_up2, g_down2, g_win, g_wout, g_wa2_p,
            d_ba_p, d_ffn1_g, d_mix_g, d_ffn2_g, d_final_g, d_ret_g, d_gla_g, arrived, late_arrived)


def kernel(x, ffn1_norm_g, ffn1_w_gate, ffn1_w_up, ffn1_w_down, mix_norm_g, w_in, ret_norm_g, gla_w_a2, gla_b_a, gla_norm_g, w_out, ffn2_norm_g, ffn2_w_gate, ffn2_w_up, ffn2_w_down, final_norm_g, loss_target, m_ffn1_norm_g, m_ffn1_w_gate, m_ffn1_w_up, m_ffn1_w_down, m_mix_norm_g, m_w_in, m_ret_norm_g, m_gla_w_a2, m_gla_b_a, m_gla_norm_g, m_w_out, m_ffn2_norm_g, m_ffn2_w_gate, m_ffn2_w_up, m_ffn2_w_down, m_final_norm_g, v_ffn1_norm_g, v_ffn1_w_gate, v_ffn1_w_up, v_ffn1_w_down, v_mix_norm_g, v_w_in, v_ret_norm_g, v_gla_w_a2, v_gla_b_a, v_gla_norm_g, v_w_out, v_ffn2_norm_g, v_ffn2_w_gate, v_ffn2_w_up, v_ffn2_w_down, v_final_norm_g):
    t = x.shape[1]
    xs = x.reshape(t, D_MODEL)
    target = loss_target.reshape(t, D_MODEL)
    chip = 2 * lax.axis_index("x") + lax.axis_index("y")
    c_arr = lax.axis_index("c").astype(jnp.int32).reshape(1)

    me_arr = chip.astype(jnp.int32).reshape(1)

    pad_rows = _pad_in_rows

    def own_block(gathered, shard):
        return lax.dynamic_update_slice(gathered, shard[None], (chip,) + (0,) * shard.ndim)

    ffn1_shard = _halves(jnp.stack([ffn1_w_gate[0].T, ffn1_w_up[0].T, ffn1_w_down[0]], axis=0).astype(BF16))
    rest_shards = [_halves(pad_rows(w_in[0].T).astype(BF16)[None]),
                   _halves(w_out.astype(BF16)),
                   jnp.concatenate([gla_w_a2.reshape(GATE_RANK, 64), jnp.zeros((GATE_RANK, 64), F32)],
                                   axis=1).reshape(1, 2, 8, LANES)]
    ffn2_shards = [_halves(w.astype(BF16)[None]) for w in (ffn2_w_gate[0].T, ffn2_w_up[0].T, ffn2_w_down[0])]
    def ffn1_weights(gathered):
        return lax.dynamic_update_slice(gathered, ffn1_shard[None], (0,) * 5).reshape(N_CHIPS, 3, FF_SHARD, D_MODEL)

    def rest_weights(gathered):
        win_all, wout_all, wa2_all = [own_block(g, s) for g, s in zip(gathered, rest_shards)]
        win_t = win_all.reshape(N_CHIPS, IN_ROWS, D_MODEL)
        w_in_pt = _pad_w_in_t(jnp.concatenate([win_t[p, 0:IN_SHARD] for p in range(N_CHIPS)], axis=0))
        wa2_p = jnp.pad(
            wa2_all.reshape(N_CHIPS, GATE_RANK, LANES).transpose(1, 0, 2).reshape(GATE_RANK, HEADS * LANES),
            ((0, LANES - GATE_RANK), (0, 0))).astype(BF16)
        return (None, w_in_pt, wout_all.reshape(D_MODEL, D_MODEL), wa2_p)

    def ffn2_weights(gathered):
        return [own_block(g, s).reshape(N_CHIPS, FF_SHARD, D_MODEL) for g, s in zip(gathered, ffn2_shards)]

    def pair_sums(grads, tag):
        halves = [g.reshape(g.shape[0], 2, g.shape[1] // 2, g.shape[2]) for g in grads]
        recv = _pair_exchange(halves, "pair_exchange_" + tag)
        return [_pair_add(g, r, c_arr, "pair_add_%s%d" % (tag, k)) for k, (g, r) in enumerate(zip(halves, recv))]

    early_sums = []

    def early(grads):
        early_sums.extend(pair_sums(grads, "early"))
        return _chip_exchange_plan(early_sums)

    late_sums = []

    def late(grad, number):
        late_sums.extend(pair_sums([grad], "late%d" % number))
        return _chip_exchange_plan(late_sums[-1:], by_peer=True)

    ba_p = jnp.pad(gla_b_a.reshape(HEADS, 64), ((0, 0), (0, 64))).reshape(1, HEADS * LANES)
    fb = _forward_backward(xs, target, None, None, ba_p, ffn1_norm_g, mix_norm_g, ret_norm_g, gla_norm_g,
                           ffn2_norm_g, final_norm_g.reshape(1, D_MODEL), ffn1_gather=(ffn1_shard, ffn1_weights),
                           rest_plan=_gather_plan(rest_shards), rest_weights=rest_weights,
                           ffn2_plans=[_gather_plan([s]) for s in ffn2_shards], ffn2_weights=ffn2_weights,
                           early=early, late=late)
    (loss_blk, grad_x, _, _, g_down1, _, _, _, _, _, g_wa2_p,
     d_ba_p, d_ffn1_g, d_mix_g, d_ffn2_g, d_final_g, d_ret_g, d_gla_g, early_arrived, late_arrived) = fb
    late_arrived = late_arrived + _run_hosted(late(g_down1, 3), "chip_exchange_late")
    mine = [_peer_sum(s, r, "chip_sum_%d" % k) for k, (s, r) in enumerate(zip(late_sums, late_arrived))]
    mine += [_chip_sum(s, r, me_arr, "chip_sum_%d" % (3 + k)) for k, (s, r) in enumerate(zip(early_sums, early_arrived))]
    other = _pair_share(mine)

    d_ba = d_ba_p.reshape(HEADS, LANES)[:, 0:64].reshape(1, 256)
    small_local = _pack_small(d_ffn1_g, d_mix_g, d_ffn2_g, d_final_g, d_ret_g, d_gla_g, d_ba, g_wa2_p[0], loss_blk)
    small_sum = _small_allreduce(small_local)
    loss = small_sum[6, 0]

    def rows(n1, nm, n2, nf, nret, ngla, ba, wa2):
        return [n1, nm, n2, nf.reshape(1, D_MODEL), nret, ngla, ba, wa2.reshape(GATE_RANK, 64)]

    small = _small_update(
        small_sum, me_arr,
        rows(ffn1_norm_g, mix_norm_g, ffn2_norm_g, final_norm_g, ret_norm_g, gla_norm_g, gla_b_a, gla_w_a2),
        rows(m_ffn1_norm_g, m_mix_norm_g, m_ffn2_norm_g, m_final_norm_g, m_ret_norm_g, m_gla_norm_g, m_gla_b_a,
             m_gla_w_a2),
        rows(v_ffn1_norm_g, v_mix_norm_g, v_ffn2_norm_g, v_final_norm_g, v_ret_norm_g, v_gla_norm_g, v_gla_b_a,
             v_gla_w_a2))
    s_grad, s_delta, s_m, s_v = [
        [*o[0:3], o[3].reshape(D_MODEL), *o[4:7], o[7].reshape(1, GATE_RANK, 64)] for o in small]

    def big(k, w, m, v, name, to_2d, from_2d):
        outs4 = _adamw_halves(to_2d(w), mine[k], other[k], to_2d(m), to_2d(v), c_arr, name)
        return [from_2d(z) for z in outs4]

    plain = (lambda w: w[0], lambda z: z[None])
    transposed = (lambda w: w[0].T, lambda z: z.T[None])
    in_proj = (lambda w: pad_rows(w[0].T), lambda z: z[0:IN_SHARD].T[None])
    r_g1 = big(0, ffn1_w_gate, m_ffn1_w_gate, v_ffn1_w_gate, "adamw_ffn1_gate", *transposed)
    r_u1 = big(1, ffn1_w_up, m_ffn1_w_up, v_ffn1_w_up, "adamw_ffn1_up", *transposed)
    r_d1 = big(2, ffn1_w_down, m_ffn1_w_down, v_ffn1_w_down, "adamw_ffn1_down", *plain)
    r_g2 = big(3, ffn2_w_gate, m_ffn2_w_gate, v_ffn2_w_gate, "adamw_ffn2_gate", *transposed)
    r_u2 = big(4, ffn2_w_up, m_ffn2_w_up, v_ffn2_w_up, "adamw_ffn2_up", *transposed)
    r_d2 = big(5, ffn2_w_down, m_ffn2_w_down, v_ffn2_w_down, "adamw_ffn2_down", *plain)
    r_in = big(6, w_in, m_w_in, v_w_in, "adamw_w_in", *in_proj)
    r_out = big(7, w_out, m_w_out, v_w_out, "adamw_w_out", *plain)

    def leaves(k, smalls):
        n1, nm, n2, nf, nret, ngla, ba, wa2 = smalls
        return [n1, r_g1[k], r_u1[k], r_d1[k], nm, r_in[k], nret, wa2, ba, ngla, r_out[k], n2, r_g2[k], r_u2[k], r_d2[k], nf]

    outs = [loss, grad_x.reshape(x.shape)]
    outs += leaves(0, s_grad) + leaves(1, s_delta) + leaves(2, s_m) + leaves(3, s_v)
    return tuple(outs)
```

```python
import functools

import jax
import jax.numpy as jnp
from jax import lax
from jax.experimental import pallas as pl
from jax.experimental.pallas import tpu as pltpu

F32, BF16 = jnp.float32, jnp.bfloat16
MESH = pl.DeviceIdType.MESH
ANY = pl.BlockSpec(memory_space=pl.ANY)

D_MODEL = 1024
D_FF = 2816
N_CHIPS = 4
FF_SHARD = D_FF // N_CHIPS
IN_WIDTH = 3088
IN_SHARD = IN_WIDTH // N_CHIPS
IN_ROWS = 800
CHUNK = 64
HEADS = 4
LANES = 128
PROJ_P = (2 * 4 * HEADS + 1) * LANES
GATE_RANK = 16
QK_SCALE = 0.125
GATE_NORM = 16.0
RMS_EPS = 1e-6
ROPE_BASE = 10000.0
ADAM_LR, ADAM_B1, ADAM_B2, ADAM_EPS, ADAM_WD, ADAM_STEP = 0.001, 0.9, 0.999, 1e-08, 0.01, 10
SMALL_ROWS = 32
TOKEN_TILE = 512
ATTN_TILE = 512

_ARB2 = pltpu.CompilerParams(dimension_semantics=("arbitrary", "arbitrary"))
_ARB1 = pltpu.CompilerParams(dimension_semantics=("arbitrary",))
_ARB3 = pltpu.CompilerParams(dimension_semantics=("arbitrary", "arbitrary", "arbitrary"))


def _dot(a, b):
    return jnp.dot(a, b, preferred_element_type=F32)


def _dot_nt(a, b):
    return lax.dot_general(a, b, (((1,), (1,)), ((), ())), preferred_element_type=F32)


def _dot_tn(a, b):
    return lax.dot_general(a, b, (((0,), (0,)), ((), ())), preferred_element_type=F32)


def _rms_scale(xv):
    return lax.rsqrt(jnp.mean(xv * xv, axis=-1, keepdims=True) + RMS_EPS)


def _rms_bwd(dh, xv, g):
    r = _rms_scale(xv)
    xhat = xv * r
    dxhat = dh * g
    dx = r * (dxhat - xhat * jnp.mean(dxhat * xhat, axis=-1, keepdims=True))
    return dx, jnp.sum(dh * xhat, axis=0, keepdims=True)


def _silu_grad(a, sg):
    return sg * (1.0 + a * (1.0 - sg))


class _Hosted:
    def __init__(self, arrays, out_shapes, n_sems, start, finish, middle=None):
        self.arrays, self.out_shapes, self.n_sems = list(arrays), list(out_shapes), n_sems
        self.start, self.finish = start, finish
        self.middle = middle if middle is not None else (lambda *refs: None)


def _call(body, args, *, name, grid, in_specs, out_specs, out_shape, scratch_shapes, compiler_params, hosted=None):
    if hosted is None:
        outs = pl.pallas_call(body, name=name, grid=grid, in_specs=in_specs, out_specs=out_specs, out_shape=out_shape,
                              scratch_shapes=scratch_shapes, compiler_params=compiler_params)(*args)
        return list(outs), []
    n_in, n_out, n_sc, nh = len(in_specs), len(out_specs), len(scratch_shapes), len(hosted.arrays)

    def wrapped(*refs):
        ins, h_in = refs[:n_in], refs[n_in:n_in + nh]
        outs, h_out = refs[n_in + nh:n_in + nh + n_out], refs[n_in + nh + n_out:n_in + 2 * nh + n_out]
        rest = refs[n_in + 2 * nh + n_out:]
        scratch, (send_sems, recv_sems) = rest[:n_sc], rest[n_sc:]
        step = functools.reduce(lambda flat, d: flat * grid[d] + pl.program_id(d), range(len(grid)), 0)
        total = functools.reduce(lambda a, b: a * b, grid)

        @pl.when(step == 0)
        def _():
            hosted.start(h_in, h_out, send_sems, recv_sems)

        @pl.when(step == total // 2)
        def _():
            hosted.middle(h_in, h_out, send_sems, recv_sems)

        body(*ins, *outs, *scratch)
        last = step == total - 1

        @pl.when(last)
        def _():
            hosted.finish(h_in, h_out, send_sems, recv_sems)

    sems = [pltpu.SemaphoreType.DMA((hosted.n_sems,)), pltpu.SemaphoreType.DMA((hosted.n_sems,))]
    outs = pl.pallas_call(
        wrapped, name=name, grid=grid, in_specs=list(in_specs) + [ANY] * nh, out_specs=list(out_specs) + [ANY] * nh,
        out_shape=list(out_shape) + hosted.out_shapes, scratch_shapes=list(scratch_shapes) + sems,
        compiler_params=compiler_params)(*args, *hosted.arrays)
    return list(outs[:n_out]), list(outs[n_out:])


def _run_hosted(hosted, name):
    nh = len(hosted.arrays)

    def body(*refs):
        h_in, h_out, (send_sems, recv_sems) = refs[:nh], refs[nh:2 * nh], refs[2 * nh:]
        hosted.start(h_in, h_out, send_sems, recv_sems)
        hosted.middle(h_in, h_out, send_sems, recv_sems)
        hosted.finish(h_in, h_out, send_sems, recv_sems)

    sems = [pltpu.SemaphoreType.DMA((hosted.n_sems,)), pltpu.SemaphoreType.DMA((hosted.n_sems,))]
    return list(pl.pallas_call(body, name=name, in_specs=[ANY] * nh, out_specs=[ANY] * nh,
                               out_shape=hosted.out_shapes, scratch_shapes=sems)(*hosted.arrays))


def _ffn_weight_operands(ffn_w, chunk_maps):
    if isinstance(ffn_w, (list, tuple)):
        specs = [pl.BlockSpec((None, FF_SHARD, D_MODEL), lambda *g, m=m: (m(*g), 0, 0)) for m in chunk_maps]
        return list(ffn_w), specs
    specs = [pl.BlockSpec((None, None, FF_SHARD, D_MODEL), lambda *g, m=m, k=kind: (m(*g), k, 0, 0))
             for kind, m in enumerate(chunk_maps)]
    return [ffn_w] * 3, specs


def _pipeline_items(steps):
    def cur(s):
        c = jnp.minimum(s, steps - 1)
        return c // N_CHIPS, c % N_CHIPS

    def prev(s):
        p = jnp.maximum(s - 1, 0)
        return p // N_CHIPS, p % N_CHIPS

    return cur, prev


def _ffn_fwd(x, g, ffn_w, name, hosted=None):
    t = x.shape[0]
    tm = min(t, TOKEN_TILE)

    def body(x_ref, g_ref, wg_ref, wu_ref, wd_ref, xo_ref, a_ref, u_ref, h_ref, acc_ref):
        j = pl.program_id(1)

        @pl.when(j == 0)
        def _():
            xv = x_ref[...]
            h_ref[...] = ((xv * _rms_scale(xv)) * g_ref[...]).astype(BF16)
            acc_ref[...] = jnp.zeros_like(acc_ref)

        h = h_ref[...]
        a = _dot_nt(h, wg_ref[...])
        u = _dot_nt(h, wu_ref[...])
        a_ref[...] = a.astype(BF16)
        u_ref[...] = u.astype(BF16)
        hid = (a * jax.nn.sigmoid(a)) * u
        acc_ref[...] += _dot(hid.astype(BF16), wd_ref[...])

        @pl.when(j == N_CHIPS - 1)
        def _():
            xo_ref[...] = x_ref[...] + 0.5 * acc_ref[...]

    tok = pl.BlockSpec((tm, D_MODEL), lambda i, j: (i, 0))
    act = pl.BlockSpec((None, tm, FF_SHARD), lambda i, j: (j, i, 0))
    w_arrays, weights = _ffn_weight_operands(ffn_w, [lambda i, j: j] * 3)
    return _call(
        body, (x, g, *w_arrays), name=name, grid=(t // tm, N_CHIPS),
        in_specs=[tok, pl.BlockSpec((1, D_MODEL), lambda i, j: (0, 0))] + weights,
        out_specs=[tok, act, act, tok],
        out_shape=[jax.ShapeDtypeStruct((t, D_MODEL), F32),
                   jax.ShapeDtypeStruct((N_CHIPS, t, FF_SHARD), BF16),
                   jax.ShapeDtypeStruct((N_CHIPS, t, FF_SHARD), BF16),
                   jax.ShapeDtypeStruct((t, D_MODEL), BF16)],
        scratch_shapes=[pltpu.VMEM((tm, D_MODEL), F32)],
        compiler_params=_ARB2, hosted=hosted)


def _ffn1_fwd_gathering(x, g, shard, name, hosted=None):
    t = x.shape[0]
    tm = min(t, TOKEN_TILE)
    nt = t // tm
    nh = 0 if hosted is None else len(hosted.arrays)

    def body(*refs):
        x_ref, g_ref, shard_ref = refs[0:3]
        h_in = refs[3:3 + nh]
        xo_ref, a_ref, u_ref, h_ref, wall = refs[3 + nh:8 + nh]
        h_out = refs[8 + nh:8 + 2 * nh]
        acc, h_all, wbuf, load_sem, send_sems, recv_sems = refs[8 + 2 * nh:14 + 2 * nh]
        carried_sems = refs[14 + 2 * nh:]
        k, i = pl.program_id(0), pl.program_id(1)
        legs, _ = _gather_legs(shard_ref, wall, send_sems, recv_sems, 0, True)
        begin, pass_on, _ = _gather_steps(legs, True)

        def load_chunk(src):
            load = pltpu.make_async_copy(src, wbuf, load_sem)
            load.start()
            load.wait()

        @pl.when((k == 0) & (i == 0))
        def _():
            begin()
            load_chunk(shard_ref)
            if hosted is not None:
                hosted.start(h_in, h_out, *carried_sems)

        @pl.when((k == PEER_SLOT[1]) & (i == 0))
        def _():
            pass_on()
            legs["pass_y"][1].wait_recv()
            load_chunk(wall.at[PEER_SLOT[1]])

        @pl.when((k == PEER_SLOT[0]) & (i == 0))
        def _():
            legs["pass_x"][1].wait_recv()
            load_chunk(wall.at[PEER_SLOT[0]])
            if hosted is not None:
                hosted.middle(h_in, h_out, *carried_sems)

        @pl.when((k == PEER_SLOT[2]) & (i == 0))
        def _():
            legs["fwd_y"][1].wait_recv()
            legs["pass_d0"][0].start()
            legs["fwd_x"][1].wait_recv()
            legs["pass_d1"][0].start()
            legs["pass_d0"][1].wait_recv()
            legs["pass_d1"][1].wait_recv()
            load_chunk(wall.at[PEER_SLOT[2]])

        @pl.when(k == 0)
        def _():
            xv = x_ref[...]
            h0 = ((xv * _rms_scale(xv)) * g_ref[...]).astype(BF16)
            h_all[i] = h0
            h_ref[...] = h0

        h = h_all[i]
        wg, wu, wd = (wbuf[kind].reshape(FF_SHARD, D_MODEL) for kind in range(3))
        a = _dot_nt(h, wg)
        u = _dot_nt(h, wu)
        a_ref[...] = a.astype(BF16)
        u_ref[...] = u.astype(BF16)
        part = _dot(((a * jax.nn.sigmoid(a)) * u).astype(BF16), wd)

        @pl.when(k == 0)
        def _():
            acc[i] = part

        @pl.when(k > 0)
        def _():
            acc[i] += part

        @pl.when(k == N_CHIPS - 1)
        def _():
            xo_ref[...] = x_ref[...] + 0.5 * acc[i]

        @pl.when((k == N_CHIPS - 1) & (i == nt - 1))
        def _():
            for pair in legs.values():
                pair[0].wait_send()
            if hosted is not None:
                hosted.finish(h_in, h_out, *carried_sems)

    def first_or_last(k):
        return (k == 0) | (k == N_CHIPS - 1)

    tok = lambda keep: pl.BlockSpec((tm, D_MODEL), lambda k, i: (jnp.where(keep(k), i, 0), 0))
    act = pl.BlockSpec((None, tm, FF_SHARD), lambda k, i: (k, i, 0))
    act_shape = jax.ShapeDtypeStruct((N_CHIPS, t, FF_SHARD), BF16)
    carried = [] if hosted is None else [pltpu.SemaphoreType.DMA((hosted.n_sems,))] * 2
    outs = pl.pallas_call(
        body, name=name, grid=(N_CHIPS, nt),
        in_specs=[tok(first_or_last), pl.BlockSpec((1, D_MODEL), lambda k, i: (0, 0)), ANY] + [ANY] * nh,
        out_specs=[tok(lambda k: k == N_CHIPS - 1), act, act,
                   pl.BlockSpec((tm, D_MODEL), lambda k, i: (jnp.where(k == 0, i, nt - 1), 0)), ANY] + [ANY] * nh,
        out_shape=[jax.ShapeDtypeStruct((t, D_MODEL), F32), act_shape, act_shape,
                   jax.ShapeDtypeStruct((t, D_MODEL), BF16),
                   jax.ShapeDtypeStruct((N_CHIPS,) + shard.shape, shard.dtype)]
                  + ([] if hosted is None else hosted.out_shapes),
        scratch_shapes=[pltpu.VMEM((nt, tm, D_MODEL), F32), pltpu.VMEM((nt, tm, D_MODEL), BF16),
                        pltpu.VMEM(shard.shape, shard.dtype), pltpu.SemaphoreType.DMA,
                        pltpu.SemaphoreType.DMA((8,)), pltpu.SemaphoreType.DMA((8,))] + carried,
        compiler_params=_ARB2,
    )(x, g, shard, *([] if hosted is None else hosted.arrays))
    return list(outs[:5]), list(outs[5:])


def _ffn_bwd(dxo, x, g, a4, u4, ffn_w, name, hosted=None):
    t = x.shape[0]
    tm = min(t, TOKEN_TILE)
    steps = (t // tm) * N_CHIPS
    cur, prev = _pipeline_items(steps)


    def body(dxo_ref, dxo_prev_ref, x_ref, g_ref, a_ref, u_ref, wg_ref, wu_ref, wd_ref,
             da_ref, du_ref, hid_ref, dob_ref, dx_ref, dg_ref, acc_ref, da_slots, du_slots):
        s = pl.program_id(0)
        jc, jp = cur(s)[1], prev(s)[1]
        slot = s % 2

        @pl.when(s == 0)
        def _():
            dg_ref[...] = jnp.zeros_like(dg_ref)
            acc_ref[...] = jnp.zeros_like(acc_ref)
            da_slots[...] = jnp.zeros_like(da_slots)
            du_slots[...] = jnp.zeros_like(du_slots)

        @pl.when(jc == 0)
        def _():
            dob_ref[...] = (0.5 * dxo_ref[...]).astype(BF16)

        dhid = _dot_nt(dob_ref[...], wd_ref[...])
        a = a_ref[...].astype(F32)
        u = u_ref[...].astype(F32)
        sg = jax.nn.sigmoid(a)
        sl = a * sg
        hid_ref[...] = (sl * u).astype(BF16)
        du = (dhid * sl).astype(BF16)
        da = (dhid * u * _silu_grad(a, sg)).astype(BF16)
        du_ref[...] = du
        da_ref[...] = da
        acc_ref[...] += _dot(da_slots[1 - slot], wg_ref[...]) + _dot(du_slots[1 - slot], wu_ref[...])
        da_slots[slot] = da
        du_slots[slot] = du

        @pl.when((jp == N_CHIPS - 1) & (s > 0))
        def _():
            dx, dg = _rms_bwd(acc_ref[...], x_ref[...], g_ref[...])
            dx_ref[...] = dxo_prev_ref[...] + dx
            dg_ref[...] += dg
            acc_ref[...] = jnp.zeros_like(acc_ref)

    tok_cur = pl.BlockSpec((tm, D_MODEL), lambda s: (cur(s)[0], 0))
    tok_prev = pl.BlockSpec((tm, D_MODEL), lambda s: (prev(s)[0], 0))
    act = pl.BlockSpec((None, tm, FF_SHARD), lambda s: (cur(s)[1], cur(s)[0], 0))
    row = pl.BlockSpec((1, D_MODEL), lambda s: (0, 0))
    w_arrays, weights = _ffn_weight_operands(ffn_w, [lambda s: prev(s)[1], lambda s: prev(s)[1], lambda s: cur(s)[1]])
    act_shape = jax.ShapeDtypeStruct((N_CHIPS, t, FF_SHARD), BF16)
    return _call(
        body, (dxo, dxo, x, g, a4, u4, *w_arrays), name=name, grid=(steps + 1,),
        in_specs=[tok_cur, tok_prev, tok_prev, row, act, act] + weights,
        out_specs=[act, act, act, tok_cur, tok_prev, row],
        out_shape=[act_shape, act_shape, act_shape,
                   jax.ShapeDtypeStruct((t, D_MODEL), BF16),
                   jax.ShapeDtypeStruct((t, D_MODEL), F32),
                   jax.ShapeDtypeStruct((1, D_MODEL), F32)],
        scratch_shapes=[pltpu.VMEM((tm, D_MODEL), F32), pltpu.VMEM((2, tm, FF_SHARD), BF16),
                        pltpu.VMEM((2, tm, FF_SHARD), BF16)],
        compiler_params=_ARB1, hosted=hosted)


def _matmul_tn(a, b, name, tka=None, out_dtype=F32, hosted=None):
    a3, b3 = a.ndim == 3, b.ndim == 3
    nb = a.shape[0] if a3 else (b.shape[0] if b3 else 1)
    t, ka, n = a.shape[-2], a.shape[-1], b.shape[-1]
    tka = ka if tka is None else tka
    tk = min(t, 2 * TOKEN_TILE)
    nk = t // tk

    def body(a_ref, b_ref, o_ref, acc_ref):
        k = pl.program_id(2)

        @pl.when(k == 0)
        def _():
            acc_ref[...] = jnp.zeros_like(acc_ref)

        acc_ref[...] += _dot_tn(a_ref[...].astype(BF16), b_ref[...].astype(BF16))

        @pl.when(k == nk - 1)
        def _():
            o_ref[...] = acc_ref[...].astype(out_dtype)

    a_spec = (pl.BlockSpec((None, tk, tka), lambda i, j, k: (i, k, j)) if a3
              else pl.BlockSpec((tk, tka), lambda i, j, k: (k, j)))
    b_spec = (pl.BlockSpec((None, tk, n), lambda i, j, k: (i, k, 0)) if b3
              else pl.BlockSpec((tk, n), lambda i, j, k: (k, 0)))
    outs, carried = _call(
        body, (a, b), name=name, grid=(nb, ka // tka, t // tk),
        in_specs=[a_spec, b_spec],
        out_specs=[pl.BlockSpec((None, tka, n), lambda i, j, k: (i, j, 0))],
        out_shape=[jax.ShapeDtypeStruct((nb, ka, n), out_dtype)],
        scratch_shapes=[pltpu.VMEM((tka, n), F32)],
        compiler_params=_ARB3, hosted=hosted)
    return outs[0] if hosted is None else (outs[0], carried)


def _matmul_nt(a, w, name, out_dtype=F32):
    t, k = a.shape
    n = w.shape[0]
    tm = min(t, TOKEN_TILE)

    def body(a_ref, w_ref, o_ref):
        o_ref[...] = _dot_nt(a_ref[...].astype(BF16), w_ref[...]).astype(out_dtype)

    return pl.pallas_call(
        body, name=name, grid=(t // tm,),
        in_specs=[pl.BlockSpec((tm, k), lambda i: (i, 0)), pl.BlockSpec((n, k), lambda i: (0, 0))],
        out_specs=pl.BlockSpec((tm, n), lambda i: (i, 0)),
        out_shape=jax.ShapeDtypeStruct((t, n), out_dtype),
        compiler_params=_ARB1,
    )(a, w)


def _mixer_in_bwd(dproj, w_in_pt, dres, x, g, name):
    t, k = dproj.shape
    tm = min(t, TOKEN_TILE)

    def body(a_ref, w_ref, dres_ref, x_ref, g_ref, dx_ref, dg_ref):
        @pl.when(pl.program_id(0) == 0)
        def _():
            dg_ref[...] = jnp.zeros_like(dg_ref)

        dh = _dot(a_ref[...], w_ref[...])
        dx, dg = _rms_bwd(dh, x_ref[...], g_ref[...])
        dx_ref[...] = dres_ref[...] + dx
        dg_ref[...] += dg

    tok = pl.BlockSpec((tm, D_MODEL), lambda i: (i, 0))
    row = pl.BlockSpec((1, D_MODEL), lambda i: (0, 0))
    return pl.pallas_call(
        body, name=name, grid=(t // tm,),
        in_specs=[pl.BlockSpec((tm, k), lambda i: (i, 0)), pl.BlockSpec((k, D_MODEL), lambda i: (0, 0)), tok, tok, row],
        out_specs=[tok, row],
        out_shape=[jax.ShapeDtypeStruct((t, D_MODEL), F32), jax.ShapeDtypeStruct((1, D_MODEL), F32)],
        compiler_params=_ARB1,
    )(dproj, w_in_pt, dres, x, g)


def _mixer_in_fwd(x, g, w_in_pt, name, hosted=None):
    t = x.shape[0]
    tm = min(t, TOKEN_TILE)
    tn = PROJ_P // 3

    def body(x_ref, g_ref, w_ref, p_ref, h_ref):
        @pl.when(pl.program_id(1) == 0)
        def _():
            xv = x_ref[...]
            h_ref[...] = ((xv * _rms_scale(xv)) * g_ref[...]).astype(BF16)

        p_ref[...] = _dot_nt(h_ref[...], w_ref[...])

    tok = pl.BlockSpec((tm, D_MODEL), lambda i, j: (i, 0))
    return _call(
        body, (x, g, w_in_pt), name=name, grid=(t // tm, 3),
        in_specs=[tok, pl.BlockSpec((1, D_MODEL), lambda i, j: (0, 0)),
                  pl.BlockSpec((tn, D_MODEL), lambda i, j: (j, 0))],
        out_specs=[pl.BlockSpec((tm, tn), lambda i, j: (i, j)), tok],
        out_shape=[jax.ShapeDtypeStruct((t, PROJ_P), F32), jax.ShapeDtypeStruct((t, D_MODEL), BF16)],
        scratch_shapes=[], compiler_params=_ARB2, hosted=hosted)


def _mixer_out_fwd(o_ret, o_gla, w_out, x, name):
    t = x.shape[0]
    tm = min(t, TOKEN_TILE)
    half = HEADS * LANES

    def body(a_ref, b_ref, w_ref, x_ref, o_ref):
        o_ref[...] = x_ref[...] + _dot(a_ref[...], w_ref[0:half, :]) + _dot(b_ref[...], w_ref[half:2 * half, :])

    tok = pl.BlockSpec((tm, D_MODEL), lambda i: (i, 0))
    hb = pl.BlockSpec((tm, half), lambda i: (i, 0))
    return pl.pallas_call(
        body, name=name, grid=(t // tm,),
        in_specs=[hb, hb, pl.BlockSpec((2 * half, D_MODEL), lambda i: (0, 0)), tok],
        out_specs=tok, out_shape=jax.ShapeDtypeStruct((t, D_MODEL), F32),
        compiler_params=_ARB1,
    )(o_ret, o_gla, w_out, x)


def _final_loss(x, g, target, name):
    t = x.shape[0]
    tm = min(t, TOKEN_TILE)

    def body(x_ref, g_ref, t_ref, l_ref, dx_ref, dg_ref):
        @pl.when(pl.program_id(0) == 0)
        def _():
            l_ref[...] = jnp.zeros_like(l_ref)
            dg_ref[...] = jnp.zeros_like(dg_ref)

        xv = x_ref[...]
        gv = g_ref[...]
        err = (xv * _rms_scale(xv)) * gv - t_ref[...]
        l_ref[...] += 0.5 * jnp.sum(jnp.mean(err * err, axis=-1, keepdims=True), axis=0, keepdims=True)
        dx, dg = _rms_bwd(err * (1.0 / D_MODEL), xv, gv)
        dx_ref[...] = dx
        dg_ref[...] += dg

    tok = pl.BlockSpec((tm, D_MODEL), lambda i: (i, 0))
    row = pl.BlockSpec((1, D_MODEL), lambda i: (0, 0))
    return pl.pallas_call(
        body, name=name, grid=(t // tm,),
        in_specs=[tok, row, tok],
        out_specs=[pl.BlockSpec((8, LANES), lambda i: (0, 0)), tok, row],
        out_shape=[jax.ShapeDtypeStruct((8, LANES), F32), jax.ShapeDtypeStruct((t, D_MODEL), F32),
                   jax.ShapeDtypeStruct((1, D_MODEL), F32)],
        compiler_params=_ARB1,
    )(x, g, target)


def _rot(v, cos, sa, sb):
    return v * cos + pltpu.roll(v, 96, 1) * sa + pltpu.roll(v, 32, 1) * sb


def _rot_t(d, cos, sa, sb):
    return d * cos + pltpu.roll(d * sa, 32, 1) + pltpu.roll(d * sb, 96, 1)


def _bmm(a, b):
    return jnp.einsum("cik,ckj->cij", a, b, preferred_element_type=F32)


def _bmm_nt(a, b):
    return jnp.einsum("cik,cjk->cij", a, b, preferred_element_type=F32)


def _bmm_tn(a, b):
    return jnp.einsum("cki,ckj->cij", a, b, preferred_element_type=F32)


def _masked_sum(mask, x):
    hi = x.astype(BF16)
    r1 = x - hi.astype(F32)
    mid = r1.astype(BF16)
    lo = (r1 - mid.astype(F32)).astype(BF16)
    return _bmm(mask, hi) + _bmm(mask, mid) + _bmm(mask, lo)


def _tile_inputs(is_ret, qkvg_refs, aux, nc):
    shape3 = (nc, CHUNK, LANES)
    q_raw, k_raw, v, gate = (r[...] for r in qkvg_refs)
    ri = lax.broadcasted_iota(jnp.int32, (nc, CHUNK, CHUNK), 1)
    ci = lax.broadcasted_iota(jnp.int32, (nc, CHUNK, CHUNK), 2)
    if is_ret:
        cos_ref, sa_ref, sb_ref, lg_ref = aux
        cos, sa, sb = cos_ref[...], sa_ref[...], sb_ref[...]
        q = _rot(q_raw, cos, sa, sb)
        k = _rot(k_raw, cos, sa, sb) * QK_SCALE
        steps = (lax.broadcasted_iota(jnp.int32, shape3, 1) + 1).astype(F32)
        b = steps * lg_ref[...]
        logit = jnp.exp(jnp.abs(ri - ci).astype(F32) * lg_ref[:, 0:CHUNK])
    else:
        glow_ref, wa2_ref, ba_ref = aux
        logit = _dot(glow_ref[...].astype(BF16), wa2_ref[...]) + ba_ref[...]
        la = (jnp.minimum(logit, 0.0) - jnp.log1p(jnp.exp(-jnp.abs(logit)))) * (1.0 / GATE_NORM)
        b = _masked_sum((ci <= ri).astype(BF16), la.reshape(shape3))
        q = q_raw * QK_SCALE
        k = k_raw
    return q.reshape(shape3), k.reshape(shape3), v.reshape(shape3), gate, b, logit, ri, ci


def _tile_scores(q, k, b, ri, ci):
    mid = b[:, CHUNK // 2 - 1:CHUNK // 2, :]
    ep = jnp.exp(b - mid)
    en = jnp.exp(mid - b)
    qt, kt, qh, kh = q * ep, k * en, q * en, k * ep
    low = _bmm_nt(qt.astype(BF16), kt.astype(BF16))
    upp = _bmm_nt(qh.astype(BF16), kh.astype(BF16))
    scores = jnp.where(ci <= ri, low, upp)
    return scores, ep, en, qt, kt, qh, kh


def _attn_specs(is_ret, t, tb, imap_t):
    nb = t // tb
    base = 0 if is_ret else 4 * HEADS
    proj = [pl.BlockSpec((tb, LANES), lambda h, i, s=sec: (imap_t(i), base + HEADS * s + h)) for sec in range(4)]
    lane_t = pl.BlockSpec((tb, LANES), lambda h, i: (imap_t(i), 0))
    if is_ret:
        aux = [lane_t, lane_t, lane_t, pl.BlockSpec((None, 1, LANES), lambda h, i: (h, 0, 0))]
    else:
        aux = [pl.BlockSpec((tb, LANES), lambda h, i: (imap_t(i), PROJ_P // LANES - 1)),
               pl.BlockSpec((LANES, LANES), lambda h, i: (0, h)),
               pl.BlockSpec((1, LANES), lambda h, i: (0, h))]
    gain = pl.BlockSpec((1, LANES), lambda h, i: (0, h))
    head_t = pl.BlockSpec((tb, LANES), lambda h, i: (imap_t(i), h))
    state = pl.BlockSpec((None, tb // CHUNK, LANES, LANES), lambda h, i: (h, imap_t(i), 0, 0))
    return nb, proj, aux, gain, head_t, state


def _attn_fwd(is_ret, proj, aux_arrays, gain, name, hosted=None):
    t = proj.shape[0]
    tb = min(t, ATTN_TILE)
    nc = tb // CHUNK
    n_aux = 4 if is_ret else 3
    nb, proj_spec, aux_specs, gain_spec, head_t, state_spec = _attn_specs(is_ret, t, tb, lambda i: i)

    def body(*refs):
        qkvg_refs = refs[0:4]
        aux = refs[4:4 + n_aux]
        gn_ref, ofin_ref, oraw_ref, st_ref, state = refs[4 + n_aux:]

        @pl.when(pl.program_id(1) == 0)
        def _():
            state[...] = jnp.zeros_like(state)

        q, k, v, gate, b, decay, ri, ci = _tile_inputs(is_ret, qkvg_refs, aux, nc)
        if is_ret:
            scores = _bmm_nt(q.astype(BF16), k.astype(BF16)) * decay
        else:
            scores = _tile_scores(q, k, b, ri, ci)[0]
        vb = v.astype(BF16)
        intra = _bmm(scores.astype(BF16), vb)
        b_last = b[:, CHUNK - 1:CHUNK, :]
        e_last = jnp.exp(b_last)
        grow = _bmm_tn(vb, (k * jnp.exp(b_last - b)).astype(BF16))
        st = state[...]
        for c in range(nc):
            st_ref[c] = st
            st = st * e_last[c] + grow[c]
        state[...] = st
        inter = _bmm_nt((q * jnp.exp(b)).astype(BF16), st_ref[...].astype(BF16))
        out = (intra + inter).reshape(tb, LANES)
        oraw_ref[...] = out
        normed = out * _rms_scale(out)
        ofin_ref[...] = ((normed * gn_ref[...]) * (gate * jax.nn.sigmoid(gate))).astype(BF16)

    width = HEADS * LANES
    return _call(
        body, (proj, proj, proj, proj, *aux_arrays, gain), name=name, grid=(HEADS, nb),
        in_specs=proj_spec + aux_specs + [gain_spec],
        out_specs=[head_t, head_t, state_spec],
        out_shape=[jax.ShapeDtypeStruct((t, width), BF16), jax.ShapeDtypeStruct((t, width), F32),
                   jax.ShapeDtypeStruct((HEADS, t // CHUNK, LANES, LANES), F32)],
        scratch_shapes=[pltpu.VMEM((LANES, LANES), F32)],
        compiler_params=_ARB2, hosted=hosted)


def _attn_bwd(is_ret, proj, aux_arrays, gain, o_raw, states, d_out, name):
    t = proj.shape[0]
    tb = min(t, ATTN_TILE)
    nc = tb // CHUNK
    n_aux = 4 if is_ret else 3
    nblk = t // tb
    nb, proj_spec, aux_specs, gain_spec, head_t, state_spec = _attn_specs(is_ret, t, tb, lambda i: nblk - 1 - i)
    base = 0 if is_ret else HEADS
    dout_spec = pl.BlockSpec((tb, LANES), lambda h, i: (nblk - 1 - i, base + h))

    def body(*refs):
        qkvg_refs = refs[0:4]
        aux = refs[4:4 + n_aux]
        gn_ref, oraw_ref, st_ref, dfin_ref = refs[4 + n_aux:8 + n_aux]
        dq_ref, dk_ref, dv_ref, dgate_ref, dgn_ref = refs[8 + n_aux:13 + n_aux]
        if is_ret:
            dstate, dafter_ref = refs[13 + n_aux:]
        else:
            dlogit_ref, dba_ref, dstate, dafter_ref = refs[13 + n_aux:]

        @pl.when(pl.program_id(1) == 0)
        def _():
            dstate[...] = jnp.zeros_like(dstate)
            dgn_ref[...] = jnp.zeros_like(dgn_ref)
            if not is_ret:
                dba_ref[...] = jnp.zeros_like(dba_ref)

        shape3 = (nc, CHUNK, LANES)
        q, k, v, gate, b, logit, ri, ci = _tile_inputs(is_ret, qkvg_refs, aux, nc)
        eb = jnp.exp(b)
        qe = q * eb
        b_last = b[:, CHUNK - 1:CHUNK, :]
        e_last = jnp.exp(b_last)
        ekd = jnp.exp(b_last - b)
        kd = k * ekd

        gn = gn_ref[...]
        out = oraw_ref[...]
        r = _rms_scale(out)
        normed = out * r
        sg = jax.nn.sigmoid(gate)
        dfin = dfin_ref[...]
        dgate = dfin * (normed * gn) * _silu_grad(gate, sg)
        dpre = dfin * (gate * sg)
        dgn_ref[...] += jnp.sum(dpre * normed, axis=0, keepdims=True)
        dnormed = dpre * gn
        d_o = r * (dnormed - normed * jnp.mean(dnormed * normed, axis=-1, keepdims=True))
        dob, vb = d_o.reshape(shape3).astype(BF16), v.astype(BF16)

        dgrow = _bmm_tn(dob, qe.astype(BF16))
        dst = dstate[...]
        for c in reversed(range(nc)):
            dafter_ref[c] = dst
            dst = dst * e_last[c] + dgrow[c]
        dstate[...] = dst
        st = st_ref[...]
        dafter = dafter_ref[...]
        stb, dafter_b = st.astype(BF16), dafter.astype(BF16)

        dsc = _bmm_nt(dob, vb)
        dsc_t = _bmm_nt(vb, dob)
        dqe = _bmm(dob, stb)
        dkd = _bmm(vb, dafter_b)
        if is_ret:
            decay, qb, kb = logit, q.astype(BF16), k.astype(BF16)
            scores_t = _bmm_nt(kb, qb) * decay
            dq = _bmm((dsc * decay).astype(BF16), kb) + dqe * eb
            dk = _bmm((dsc_t * decay).astype(BF16), qb) + dkd * ekd
        else:
            _, ep, en, qt, kt, qh, kh = _tile_scores(q, k, b, ri, ci)
            qtb, ktb, qhb, khb = qt.astype(BF16), kt.astype(BF16), qh.astype(BF16), kh.astype(BF16)
            scores_t = jnp.where(ci >= ri, _bmm_nt(ktb, qtb), _bmm_nt(khb, qhb))
            dqt = _bmm(jnp.where(ci <= ri, dsc, 0.0).astype(BF16), ktb)
            dqh = _bmm(jnp.where(ci <= ri, 0.0, dsc).astype(BF16), khb)
            dkt = _bmm(jnp.where(ci >= ri, dsc_t, 0.0).astype(BF16), qtb)
            dkh = _bmm(jnp.where(ci >= ri, 0.0, dsc_t).astype(BF16), qhb)
            dq = dqt * ep + dqh * en + dqe * eb
            dk = dkt * en + dkh * ep + dkd * ekd
        dv = _bmm(scores_t.astype(BF16), dob) + _bmm_nt(kd.astype(BF16), dafter_b)
        dq, dk = dq.reshape(tb, LANES), dk.reshape(tb, LANES)

        if is_ret:
            cos_ref, sa_ref, sb_ref, _ = aux
            cos, sa, sb = cos_ref[...], sa_ref[...], sb_ref[...]
            dq_raw = _rot_t(dq, cos, sa, sb)
            dk_raw = _rot_t(dk, cos, sa, sb) * QK_SCALE
        else:
            dq_raw = dq * QK_SCALE
            dk_raw = dk
            db = dqt * qt - dkt * kt - dqh * qh + dkh * kh + dqe * qe - dkd * kd
            db_last = (jnp.sum(dkd * kd, axis=1, keepdims=True)
                       + jnp.sum(dafter * st, axis=1, keepdims=True) * e_last)
            last_row = lax.broadcasted_iota(jnp.int32, shape3, 1) == CHUNK - 1
            db = db + jnp.where(last_row, db_last, 0.0)
            dla = _masked_sum((ci >= ri).astype(BF16), db).reshape(tb, LANES)
            dlogit = dla * (1.0 / GATE_NORM) * jax.nn.sigmoid(-logit)
            dlogit_ref[...] = dlogit.astype(BF16)
            dba_ref[...] += jnp.sum(dlogit, axis=0, keepdims=True)

        dq_ref[...] = dq_raw.astype(BF16)
        dk_ref[...] = dk_raw.astype(BF16)
        dv_ref[...] = dv.reshape(tb, LANES).astype(BF16)
        dgate_ref[...] = dgate.astype(BF16)

    width = HEADS * LANES
    row_out = pl.BlockSpec((1, LANES), lambda h, i: (0, h))
    out_specs = [head_t] * 4 + [row_out]
    out_shape = [jax.ShapeDtypeStruct((t, width), BF16)] * 4 + [jax.ShapeDtypeStruct((1, width), F32)]
    if not is_ret:
        out_specs += [head_t, row_out]
        out_shape += [jax.ShapeDtypeStruct((t, width), BF16), jax.ShapeDtypeStruct((1, width), F32)]
    return pl.pallas_call(
        body, name=name, grid=(HEADS, nblk),
        in_specs=proj_spec + aux_specs + [gain_spec, head_t, state_spec, dout_spec],
        out_specs=out_specs, out_shape=out_shape,
        scratch_shapes=[pltpu.VMEM((LANES, LANES), F32), pltpu.VMEM((nc, LANES, LANES), F32)],
        compiler_params=_ARB2,
    )(proj, proj, proj, proj, *aux_arrays, gain, o_raw, states, d_out)


PEER_SLOT = (2, 1, 3)


def _place():
    x, y, c = lax.axis_index("x"), lax.axis_index("y"), lax.axis_index("c")
    chips = [(1 - x, y), (x, 1 - y), (1 - x, 1 - y)]
    return x, y, c, 2 * x + y, chips


def _route_split(rows, dtype):
    tile = 16 if dtype == BF16 else 8
    if rows < 2 * tile:
        return None
    return -(-(rows // 2) // tile) * tile


def _routes(by_peer):
    x, y, c, me, chips = _place()
    (xx, xy), (yx, yy), (dx, dy) = chips
    if by_peer:
        slots = dict(own=0, from_x=PEER_SLOT[0], from_y=PEER_SLOT[1], diag=PEER_SLOT[2],
                     mine_on_x=PEER_SLOT[0], mine_on_y=PEER_SLOT[1])
    else:
        slots = dict(own=me, from_x=2 * xx + xy, from_y=2 * yx + yy, diag=2 * dx + dy, mine_on_x=me, mine_on_y=me)
    return c, (xx, xy, c), (yx, yy, c), (dx, dy, c), (x, y, 1 - c), slots


def _gather_legs(src, out, send_sems, recv_sems, base, by_peer):
    c, to_x, to_y, to_d, sibling, s = _routes(by_peer)
    r0 = _route_split(src.shape[2], src.dtype)

    def cp(k, src_ref, dst_ref, to):
        return pltpu.make_async_remote_copy(src_ref=src_ref, dst_ref=dst_ref, send_sem=send_sems.at[base + k],
                                            recv_sem=recv_sems.at[base + k], device_id=to, device_id_type=MESH)

    mine = src.at[:, c]
    legs = dict(
        x=(cp(0, mine, out.at[s["mine_on_x"], :, c], to_x), cp(0, mine, out.at[s["from_x"], :, c], to_x)),
        y=(cp(1, mine, out.at[s["mine_on_y"], :, c], to_y), cp(1, mine, out.at[s["from_y"], :, c], to_y)),
        pass_x=(cp(4, out.at[s["from_x"], :, c], out.at[s["from_x"], :, c], sibling),
                cp(4, mine, out.at[s["from_x"], :, 1 - c], sibling)),
        pass_y=(cp(5, out.at[s["from_y"], :, c], out.at[s["from_y"], :, c], sibling),
                cp(5, mine, out.at[s["from_y"], :, 1 - c], sibling)))
    if r0 is None:
        mine_on_d = s["diag"] if by_peer else s["own"]
        legs["d"] = (cp(2, mine, out.at[mine_on_d, :, c], to_d), cp(2, mine, out.at[s["diag"], :, c], to_d))
        legs["pass_d"] = (cp(6, out.at[s["diag"], :, c], out.at[s["diag"], :, c], sibling),
                          cp(6, mine, out.at[s["diag"], :, 1 - c], sibling))
        return legs, False
    lo, hi = pl.ds(0, r0), pl.ds(r0, src.shape[2] - r0)
    fx_on_y = s["diag"] if by_peer else s["from_x"]
    fy_on_x = s["diag"] if by_peer else s["from_y"]
    legs.update(
        fwd_y=(cp(2, out.at[s["from_x"], :, c, lo], out.at[fx_on_y, :, c, lo], to_y),
               cp(2, mine.at[:, lo], out.at[s["diag"], :, c, lo], to_y)),
        fwd_x=(cp(3, out.at[s["from_y"], :, c, hi], out.at[fy_on_x, :, c, hi], to_x),
               cp(3, mine.at[:, hi], out.at[s["diag"], :, c, hi], to_x)),
        pass_d0=(cp(6, out.at[s["diag"], :, c, lo], out.at[s["diag"], :, c, lo], sibling),
                 cp(6, mine.at[:, lo], out.at[s["diag"], :, 1 - c, lo], sibling)),
        pass_d1=(cp(7, out.at[s["diag"], :, c, hi], out.at[s["diag"], :, c, hi], sibling),
                 cp(7, mine.at[:, hi], out.at[s["diag"], :, 1 - c, hi], sibling)))
    return legs, True


def _gather_steps(legs, routed):
    def start():
        legs["x"][0].start()
        legs["y"][0].start()
        if not routed:
            legs["d"][0].start()

    def middle():
        legs["x"][1].wait_recv()
        if routed:
            legs["fwd_y"][0].start()
        legs["pass_x"][0].start()
        legs["y"][1].wait_recv()
        if routed:
            legs["fwd_x"][0].start()
        legs["pass_y"][0].start()

    def finish():
        last = ["pass_d0", "pass_d1"] if routed else ["pass_d"]
        if routed:
            legs["fwd_y"][1].wait_recv()
            legs["pass_d0"][0].start()
            legs["fwd_x"][1].wait_recv()
            legs["pass_d1"][0].start()
        else:
            legs["d"][1].wait_recv()
            legs["pass_d"][0].start()
        for name in ["pass_x", "pass_y"] + last:
            legs[name][1].wait_recv()
        for name in ["x", "y", "pass_x", "pass_y"] + last + (["fwd_y", "fwd_x"] if routed else ["d"]):
            legs[name][0].wait_send()

    return start, middle, finish


def _gather_plan(arrs):
    na = len(arrs)

    def steps(ins, outs, send_sems, recv_sems):
        return [_gather_steps(*_gather_legs(ins[a], outs[a], send_sems, recv_sems, 8 * a, False)) for a in range(na)]

    def run(which):
        def hook(*refs):
            for step in steps(*refs):
                step[which]()
        return hook

    return _Hosted(arrs, [jax.ShapeDtypeStruct((N_CHIPS,) + a.shape, a.dtype) for a in arrs], 8 * na,
                   run(0), run(2), middle=run(1))


def _pair_exchange(grads, name):
    na = len(grads)

    def body(*refs):
        ins, outs = refs[:na], refs[na:2 * na]
        send_sems, recv_sems = refs[2 * na:]
        x, y, c, _, _ = _place()
        copies = [pltpu.make_async_remote_copy(
            src_ref=ins[a].at[:, 1 - c], dst_ref=outs[a], send_sem=send_sems.at[a], recv_sem=recv_sems.at[a],
            device_id=(x, y, 1 - c), device_id_type=MESH) for a in range(na)]
        for cp in copies:
            cp.start()
        for cp in copies:
            cp.wait()

    return pl.pallas_call(
        body, name=name,
        in_specs=[ANY] * na, out_specs=[ANY] * na,
        out_shape=[jax.ShapeDtypeStruct(g.shape[:1] + g.shape[2:], g.dtype) for g in grads],
        scratch_shapes=[pltpu.SemaphoreType.DMA((na,)), pltpu.SemaphoreType.DMA((na,))],
    )(*grads)


def _pair_add(grad, recv, c_arr, name):
    _, _, r, cols = grad.shape

    def body(c_ref, g_ref, r_ref, o_ref):
        o_ref[...] = (g_ref[...].astype(F32) + r_ref[...].astype(F32)).astype(BF16)

    return pl.pallas_call(
        body, name=name,
        grid_spec=pltpu.PrefetchScalarGridSpec(
            num_scalar_prefetch=1, grid=(N_CHIPS,),
            in_specs=[pl.BlockSpec((None, None, r, cols), lambda p, c_ref: (p, c_ref[0], 0, 0)),
                      pl.BlockSpec((None, r, cols), lambda p, c_ref: (p, 0, 0))],
            out_specs=pl.BlockSpec((None, r, cols), lambda p, c_ref: (p, 0, 0))),
        out_shape=jax.ShapeDtypeStruct((N_CHIPS, r, cols), BF16),
        compiler_params=_ARB1,
    )(c_arr, grad, recv)


def _chip_exchange_plan(sums, by_peer=False):
    na = len(sums)

    def legs(ins, outs, send_sems, recv_sems, a):
        c, to_x, to_y, _, _, s = _routes(by_peer)
        src, out = ins[a], outs[a]
        rows = src.shape[1]
        r0 = _route_split(rows, src.dtype)
        lo, hi = pl.ds(0, r0), pl.ds(r0, rows - r0)

        def cp(k, src_ref, dst_ref, to):
            return pltpu.make_async_remote_copy(src_ref=src_ref, dst_ref=dst_ref, send_sem=send_sems.at[6 * a + k],
                                                recv_sem=recv_sems.at[6 * a + k], device_id=to, device_id_type=MESH)

        return dict(
            x=(cp(0, src.at[s["from_x"]], out.at[s["mine_on_x"]], to_x), cp(0, src.at[0], out.at[s["from_x"]], to_x)),
            y=(cp(1, src.at[s["from_y"]], out.at[s["mine_on_y"]], to_y), cp(1, src.at[0], out.at[s["from_y"]], to_y)),
            stage_x=(cp(2, src.at[s["diag"], lo], out.at[s["from_x"] if not by_peer else 0, lo], to_x),
                     cp(2, src.at[0, lo], out.at[s["own"], lo], to_x)),
            stage_y=(cp(3, src.at[s["diag"], hi], out.at[s["from_y"] if not by_peer else 0, hi], to_y),
                     cp(3, src.at[0, hi], out.at[s["own"], hi], to_y)),
            fwd_y=(cp(4, out.at[s["own"], lo], out.at[s["diag"] if by_peer else s["from_x"], lo], to_y),
                   cp(4, src.at[0, lo], out.at[s["diag"], lo], to_y)),
            fwd_x=(cp(5, out.at[s["own"], hi], out.at[s["diag"] if by_peer else s["from_y"], hi], to_x),
                   cp(5, src.at[0, hi], out.at[s["diag"], hi], to_x)))

    def run(which):
        def hook(ins, outs, send_sems, recv_sems):
            for a in range(na):
                g = legs(ins, outs, send_sems, recv_sems, a)
                if which == 0:
                    for name in ("x", "y", "stage_x", "stage_y"):
                        g[name][0].start()
                elif which == 1:
                    g["stage_x"][1].wait_recv()
                    g["fwd_y"][0].start()
                    g["stage_y"][1].wait_recv()
                    g["fwd_x"][0].start()
                else:
                    for name in ("x", "y", "fwd_y", "fwd_x"):
                        g[name][1].wait_recv()
                    for name in ("x", "y", "stage_x", "stage_y", "fwd_y", "fwd_x"):
                        g[name][0].wait_send()
        return hook

    return _Hosted(sums, [jax.ShapeDtypeStruct(s.shape, s.dtype) for s in sums], 6 * na, run(0), run(2),
                   middle=run(1))


def _chip_sum(own, recv, me_arr, name):
    _, r, cols = recv.shape

    def body(me_ref, own_ref, r_ref, o_ref):
        o_ref[...] = jnp.zeros_like(o_ref)
        for q in range(N_CHIPS):
            @pl.when(me_ref[0] == q)
            def _():
                o_ref[...] += own_ref[...].astype(F32)

            @pl.when(me_ref[0] != q)
            def _():
                o_ref[...] += r_ref[q].astype(F32)

    return pl.pallas_call(
        body, name=name,
        grid_spec=pltpu.PrefetchScalarGridSpec(
            num_scalar_prefetch=1, grid=(1,),
            in_specs=[pl.BlockSpec((None, r, cols), lambda i, me_ref: (me_ref[0], 0, 0)),
                      pl.BlockSpec((N_CHIPS, r, cols), lambda i, me_ref: (0, 0, 0))],
            out_specs=pl.BlockSpec((r, cols), lambda i, me_ref: (0, 0))),
        out_shape=jax.ShapeDtypeStruct((r, cols), F32),
        compiler_params=_ARB1,
    )(me_arr, own, recv)


def _peer_sum(own, recv, name):
    _, r, cols = recv.shape

    def body(own_ref, r_ref, o_ref):
        acc = own_ref[...].astype(F32) + r_ref[1].astype(F32)
        acc = acc + r_ref[2].astype(F32)
        o_ref[...] = acc + r_ref[3].astype(F32)

    return pl.pallas_call(
        body, name=name, grid=(1,),
        in_specs=[pl.BlockSpec((None, r, cols), lambda i: (0, 0, 0)), pl.BlockSpec((N_CHIPS, r, cols), lambda i: (0, 0, 0))],
        out_specs=pl.BlockSpec((r, cols), lambda i: (0, 0)),
        out_shape=jax.ShapeDtypeStruct((r, cols), F32),
        compiler_params=_ARB1,
    )(own, recv)


def _pair_share(halves):
    na = len(halves)

    def body(*refs):
        ins, outs = refs[:na], refs[na:2 * na]
        send_sems, recv_sems = refs[2 * na:]
        x, y, c, _, _ = _place()
        copies = [pltpu.make_async_remote_copy(
            src_ref=ins[a], dst_ref=outs[a], send_sem=send_sems.at[a], recv_sem=recv_sems.at[a],
            device_id=(x, y, 1 - c), device_id_type=MESH) for a in range(na)]
        for cp in copies:
            cp.start()
        for cp in copies:
            cp.wait()

    return pl.pallas_call(
        body, name="pair_share",
        in_specs=[ANY] * na, out_specs=[ANY] * na,
        out_shape=[jax.ShapeDtypeStruct(h.shape, h.dtype) for h in halves],
        scratch_shapes=[pltpu.SemaphoreType.DMA((na,)), pltpu.SemaphoreType.DMA((na,))],
    )(*halves)


def _small_allreduce(block):
    m, n = block.shape

    def body(x_ref, all_ref, sum_ref, send_sems, recv_sems, local_sem):
        x, y, c, _, chips = _place()
        me, sibling = (x, y, c), (x, y, 1 - c)

        def rows(px, py, pc):
            return all_ref.at[pl.ds((4 * px + 2 * py + pc) * m, m), :]

        def copy(k, blk, to, src=None):
            return pltpu.make_async_remote_copy(
                src_ref=rows(*blk) if src is None else src, dst_ref=rows(*blk),
                send_sem=send_sems.at[k], recv_sem=recv_sems.at[k], device_id=to, device_id_type=MESH)

        mine = pltpu.make_async_copy(x_ref, rows(*me), local_sem)
        mine.start()
        first = [copy(0, me, sibling, src=x_ref)]
        first += [copy(1 + j, me, (*chip, c), src=x_ref) for j, chip in enumerate(chips)]
        for cp in first:
            cp.start()
        passed = [copy(4 + j, (*chip, c), sibling) for j, chip in enumerate(chips)]
        for j, chip in enumerate(chips):
            copy(1 + j, (*chip, c), me).wait_recv()
            passed[j].start()
        copy(0, sibling, me).wait_recv()
        for j, chip in enumerate(chips):
            copy(4 + j, (*chip, 1 - c), me).wait_recv()
        for cp in first + passed:
            cp.wait_send()
        mine.wait()
        acc = all_ref[0:m, :]
        for d in range(1, 8):
            acc = acc + all_ref[d * m:(d + 1) * m, :]
        sum_ref[...] = acc

    vmem = pl.BlockSpec(memory_space=pltpu.VMEM)
    return pl.pallas_call(
        body, name="small_allreduce",
        in_specs=[vmem], out_specs=[vmem, vmem],
        out_shape=[jax.ShapeDtypeStruct((8 * m, n), F32), jax.ShapeDtypeStruct((m, n), F32)],
        scratch_shapes=[pltpu.SemaphoreType.DMA((7,)), pltpu.SemaphoreType.DMA((7,)), pltpu.SemaphoreType.DMA],
    )(block)[1]


def _row_tile(rows):
    best = rows
    for cand in range(8, min(rows, 512) + 1, 8):
        if rows % cand == 0:
            best = cand
    return best


def _adamw_math(w, g, m, v):
    m2 = ADAM_B1 * m + (1.0 - ADAM_B1) * g
    v2 = ADAM_B2 * v + (1.0 - ADAM_B2) * (g * g)
    m_hat = m2 / (1.0 - ADAM_B1 ** ADAM_STEP)
    v_hat = v2 / (1.0 - ADAM_B2 ** ADAM_STEP)
    return -ADAM_LR * (m_hat / (jnp.sqrt(v_hat) + ADAM_EPS) + ADAM_WD * w), m2, v2


def _adamw_halves(w, g_mine, g_other, m, v, c_arr, name):
    rows, cols = w.shape
    r = rows // 2
    tr = _row_tile(r)
    nt = r // tr

    def body(c_ref, w_ref, gm_ref, go_ref, m_ref, v_ref, g_ref, d_ref, nm_ref, nv_ref):
        gv = jnp.where(pl.program_id(0) == c_ref[0], gm_ref[...], go_ref[...])
        g_ref[...] = gv
        d_ref[...], nm_ref[...], nv_ref[...] = _adamw_math(w_ref[...], gv, m_ref[...], v_ref[...])

    full = pl.BlockSpec((tr, cols), lambda h, i, c_ref: (h * nt + i, 0))
    half = pl.BlockSpec((tr, cols), lambda h, i, c_ref: (i, 0))
    shape = jax.ShapeDtypeStruct((rows, cols), F32)
    return pl.pallas_call(
        body, name=name,
        grid_spec=pltpu.PrefetchScalarGridSpec(
            num_scalar_prefetch=1, grid=(2, nt),
            in_specs=[full, half, half, full, full], out_specs=[full] * 4),
        out_shape=[shape] * 4,
        compiler_params=_ARB2,
    )(c_arr, w, g_mine, g_other, m, v)


def _adamw(w, g, m, v, name):
    rows, cols = w.shape
    tr = _row_tile(rows)

    def body(w_ref, g_ref, m_ref, v_ref, d_ref, nm_ref, nv_ref):
        d_ref[...], nm_ref[...], nv_ref[...] = _adamw_math(w_ref[...], g_ref[...], m_ref[...], v_ref[...])

    spec = pl.BlockSpec((tr, cols), lambda i: (i, 0))
    shape = jax.ShapeDtypeStruct((rows, cols), F32)
    return pl.pallas_call(
        body, name=name, grid=(rows // tr,),
        in_specs=[spec] * 4, out_specs=[spec] * 3, out_shape=[shape] * 3,
        compiler_params=_ARB1,
    )(w, g, m, v)


def _pad_w_in_t(w_in_t):
    def heads_padded(sec):
        return jnp.pad(sec.reshape(HEADS, 64, -1), ((0, 0), (0, LANES - 64), (0, 0))).reshape(HEADS * LANES, -1)

    w = w_in_t
    return jnp.concatenate([
        heads_padded(w[0:256]), heads_padded(w[256:512]), w[512:1536],
        heads_padded(w[1536:1792]), heads_padded(w[1792:2048]), w[2048:3072],
        jnp.pad(w[3072:3088], ((0, LANES - GATE_RANK), (0, 0)))], axis=0)


def _unpad_w_in_t(w_pt):
    def heads_unpadded(sec):
        return sec.reshape(HEADS, LANES, -1)[:, 0:64].reshape(HEADS * 64, -1)

    p = w_pt
    return jnp.concatenate([
        heads_unpadded(p[0:512]), heads_unpadded(p[512:1024]), p[1024:2048],
        heads_unpadded(p[2048:2560]), heads_unpadded(p[2560:3072]), p[3072:4096],
        p[4096:4096 + GATE_RANK]], axis=0)


def _rope_tables(t):
    half = 32
    inv = ROPE_BASE ** (-jnp.arange(half, dtype=F32) * 2.0 / 64)
    ang = jnp.arange(t, dtype=F32)[:, None] * inv[None, :]
    cos, sin = jnp.cos(ang), jnp.sin(ang)
    z32, z64 = jnp.zeros((t, 32), F32), jnp.zeros((t, 64), F32)
    return (jnp.concatenate([cos, cos, z64], axis=1),
            jnp.concatenate([-sin, z32, z64], axis=1),
            jnp.concatenate([z32, sin, z64], axis=1))


def _halves(w):
    n, rows, cols = w.shape
    return w.reshape(n, 2, rows // 2, cols)


_VMEM = pl.BlockSpec(memory_space=pltpu.VMEM)


def _pack_small(n1, nm, n2, nf, nret, ngla, ba, wa2_p, loss_blk):
    def body(n1_ref, nm_ref, n2_ref, nf_ref, nret_ref, ngla_ref, ba_ref, wa2_ref, loss_ref, o_ref):
        o_ref[...] = jnp.zeros_like(o_ref)
        o_ref[0:1, :] = n1_ref[...]
        o_ref[1:2, :] = nm_ref[...]
        o_ref[2:3, :] = n2_ref[...]
        o_ref[3:4, :] = nf_ref[...]
        o_ref[4:5, 0:512] = nret_ref[...]
        o_ref[4:5, 512:1024] = ngla_ref[...]
        o_ref[5:6, 0:256] = ba_ref[...]
        o_ref[6:7, 0:LANES] = loss_ref[0:1, :]
        o_ref[8:8 + GATE_RANK, 0:HEADS * LANES] = wa2_ref[0:GATE_RANK, :]

    return pl.pallas_call(
        body, name="pack_small", in_specs=[_VMEM] * 9, out_specs=_VMEM,
        out_shape=jax.ShapeDtypeStruct((SMALL_ROWS, D_MODEL), F32),
    )(n1, nm, n2, nf, nret, ngla, ba, wa2_p, loss_blk)


def _small_update(summed, chip_arr, ws, ms, vs):
    n = len(ws)

    def body(chip_ref, s_ref, *refs):
        w_refs, m_refs, v_refs = refs[0:n], refs[n:2 * n], refs[2 * n:3 * n]
        outs = refs[3 * n:]
        wa2_all = s_ref[8:8 + GATE_RANK, 0:HEADS * LANES]
        wa2_g = jnp.zeros((GATE_RANK, 64), F32)
        for p in range(N_CHIPS):
            wa2_g = jnp.where(chip_ref[0] == p, wa2_all[:, LANES * p:LANES * p + 64], wa2_g)
        grads = [s_ref[0:1, :], s_ref[1:2, :], s_ref[2:3, :], s_ref[3:4, :], s_ref[4:5, 0:512],
                 s_ref[4:5, 512:1024], s_ref[5:6, 0:256], wa2_g]
        for k in range(n):
            d, m2, v2 = _adamw_math(w_refs[k][...], grads[k], m_refs[k][...], v_refs[k][...])
            outs[k][...] = grads[k]
            outs[n + k][...] = d
            outs[2 * n + k][...] = m2
            outs[3 * n + k][...] = v2

    shapes = [jax.ShapeDtypeStruct(w.shape, F32) for w in ws] * 4
    smem = pl.BlockSpec(memory_space=pltpu.SMEM)
    outs = pl.pallas_call(
        body, name="small_update", in_specs=[smem] + [_VMEM] * (1 + 3 * n), out_specs=[_VMEM] * (4 * n),
        out_shape=shapes,
    )(chip_arr, summed, *ws, *ms, *vs)
    return outs[0:n], outs[n:2 * n], outs[2 * n:3 * n], outs[3 * n:4 * n]


def _pad_in_rows(w_t):
    return jnp.pad(w_t, ((0, IN_ROWS - IN_SHARD), (0, 0)))


def _forward_backward(xs, target, ffn1_w, rest, ba_p, ffn1_norm_g, mix_norm_g, ret_norm_g, gla_norm_g, ffn2_norm_g,
                      final_norm_g, ffn1_gather=None, rest_plan=None, rest_weights=None, ffn2_plans=None,
                      ffn2_weights=None, early=None, late=None):
    t = xs.shape[0]
    cos_t, sa_t, sb_t = _rope_tables(t)
    log_gamma = jnp.log(1.0 - 2.0 ** (-5.0 - jnp.arange(HEADS, dtype=F32)))
    lg_t = jnp.broadcast_to(log_gamma[:, None, None], (HEADS, 1, LANES))
    ret_aux = [cos_t, sa_t, sb_t, lg_t]

    if ffn1_gather is None:
        (x1, a1, u1, h1), gathered = _ffn_fwd(xs, ffn1_norm_g, ffn1_w, "ffn1_fwd", hosted=rest_plan)
    else:
        ffn1_shard, ffn1_weights = ffn1_gather
        (x1, a1, u1, h1, wall), gathered = _ffn1_fwd_gathering(xs, ffn1_norm_g, ffn1_shard, "ffn1_fwd",
                                                               hosted=rest_plan)
        ffn1_w = ffn1_weights(wall)
    ffn2_w, w_in_pt, w_out_full, wa2_p = rest if rest_plan is None else rest_weights(gathered)
    plans = [None] * 3 if ffn2_plans is None else ffn2_plans
    (proj, h_mix), got_gate = _mixer_in_fwd(x1, mix_norm_g, w_in_pt, "mixer_in_fwd", hosted=plans[0])
    gla_aux = [proj, wa2_p, ba_p]
    (o_ret, raw_ret, st_ret), got_up = _attn_fwd(True, proj, ret_aux, ret_norm_g, "ret_fwd", hosted=plans[1])
    (o_gla, raw_gla, st_gla), got_down = _attn_fwd(False, proj, gla_aux, gla_norm_g, "gla_fwd", hosted=plans[2])
    if ffn2_plans is not None:
        ffn2_w = ffn2_weights(got_gate + got_up + got_down)
    x2 = _mixer_out_fwd(o_ret, o_gla, w_out_full, x1, "mixer_out_fwd")
    (x3, a2, u2, h2), _ = _ffn_fwd(x2, ffn2_norm_g, ffn2_w, "ffn2_fwd")
    loss_blk, dx3, d_final_g = _final_loss(x3, final_norm_g, target, "final_loss")

    (da2, du2, hid2, dob2, dx2, d_ffn2_g), _ = _ffn_bwd(dx3, x2, ffn2_norm_g, a2, u2, ffn2_w, "ffn2_bwd")
    g_gate2 = _matmul_tn(da2, h2, "ffn2_dgate", out_dtype=BF16)
    g_up2 = _matmul_tn(du2, h2, "ffn2_dup", out_dtype=BF16)
    g_down2 = _matmul_tn(hid2, dob2, "ffn2_ddown", out_dtype=BF16)

    d_o = _matmul_nt(dx2, w_out_full, "mixer_out_bwd")
    g_wout_ret = _matmul_tn(o_ret, dx2, "wout_grad_ret", out_dtype=BF16)
    g_wout_gla = _matmul_tn(o_gla, dx2, "wout_grad_gla", out_dtype=BF16)
    *dproj_ret, d_ret_g = _attn_bwd(True, proj, ret_aux, ret_norm_g, raw_ret, st_ret, d_o, "ret_bwd")
    *dproj_gla, d_gla_g, dlogit, d_ba_p = _attn_bwd(False, proj, gla_aux, gla_norm_g, raw_gla, st_gla, d_o, "gla_bwd")
    d_glow = _matmul_nt(dlogit, wa2_p, "gate_low_bwd", out_dtype=BF16)
    g_wa2_p = _matmul_tn(proj[:, PROJ_P - LANES:], dlogit, "gate_w_grad")
    dproj = jnp.concatenate(dproj_ret + dproj_gla + [d_glow], axis=1)
    g_win_p = _matmul_tn(dproj, h_mix, "w_in_grad", tka=PROJ_P // 3, out_dtype=BF16)
    dx1, d_mix_g = _mixer_in_bwd(dproj, w_in_pt, dx2, x1, mix_norm_g, "mixer_in_bwd")
    g_win_t = _unpad_w_in_t(g_win_p[0])
    g_win = jnp.stack([_pad_in_rows(g_win_t[IN_SHARD * p:IN_SHARD * (p + 1)]) for p in range(N_CHIPS)], axis=0)
    g_wout = jnp.concatenate([g_wout_ret[0], g_wout_gla[0]], axis=0).reshape(N_CHIPS, D_MODEL // N_CHIPS, D_MODEL)

    early_plan = None if early is None else early([g_gate2, g_up2, g_down2, g_win, g_wout])
    (da1, du1, hid1, dob1, grad_x, d_ffn1_g), arrived = _ffn_bwd(dx1, xs, ffn1_norm_g, a1, u1, ffn1_w, "ffn1_bwd",
                                                                hosted=early_plan)
    late_grads, late_arrived = [], []
    for lhs, rhs, name in ((da1, h1, "ffn1_dgate"), (du1, h1, "ffn1_dup"), (hid1, dob1, "ffn1_ddown")):
        plan = None if late is None or not late_grads else late(late_grads[-1], len(late_grads))
        res = _matmul_tn(lhs, rhs, name, out_dtype=BF16, hosted=plan)
        if plan is not None:
            res, carried = res
            late_arrived += carried
        late_grads.append(res)
    g_gate1, g_up1, g_down1 = late_grads

    return (loss_blk, grad_x, g_gate1, g_up1, g_down1, g_gate2, g_up2, g_down2, g_win, g_wout, g_wa2_p,
            d_ba_p, d_ffn1_g, d_mix_g, d_ffn2_g, d_final_g, d_ret_g, d_gla_g, arrived, late_arrived)


def kernel(x, ffn1_norm_g, ffn1_w_gate, ffn1_w_up, ffn1_w_down, mix_norm_g, w_in, ret_norm_g, gla_w_a2, gla_b_a, gla_norm_g, w_out, ffn2_norm_g, ffn2_w_gate, ffn2_w_up, ffn2_w_down, final_norm_g, loss_target, m_ffn1_norm_g, m_ffn1_w_gate, m_ffn1_w_up, m_ffn1_w_down, m_mix_norm_g, m_w_in, m_ret_norm_g, m_gla_w_a2, m_gla_b_a, m_gla_norm_g, m_w_out, m_ffn2_norm_g, m_ffn2_w_gate, m_ffn2_w_up, m_ffn2_w_down, m_final_norm_g, v_ffn1_norm_g, v_ffn1_w_gate, v_ffn1_w_up, v_ffn1_w_down, v_mix_norm_g, v_w_in, v_ret_norm_g, v_gla_w_a2, v_gla_b_a, v_gla_norm_g, v_w_out, v_ffn2_norm_g, v_ffn2_w_gate, v_ffn2_w_up, v_ffn2_w_down, v_final_norm_g):
    t = x.shape[1]
    xs = x.reshape(t, D_MODEL)
    target = loss_target.reshape(t, D_MODEL)
    chip = 2 * lax.axis_index("x") + lax.axis_index("y")
    c_arr = lax.axis_index("c").astype(jnp.int32).reshape(1)

    me_arr = chip.astype(jnp.int32).reshape(1)

    pad_rows = _pad_in_rows

    def own_block(gathered, shard):
        return lax.dynamic_update_slice(gathered, shard[None], (chip,) + (0,) * shard.ndim)

    ffn1_shard = _halves(jnp.stack([ffn1_w_gate[0].T, ffn1_w_up[0].T, ffn1_w_down[0]], axis=0).astype(BF16))
    rest_shards = [_halves(pad_rows(w_in[0].T).astype(BF16)[None]),
                   _halves(w_out.astype(BF16)),
                   jnp.concatenate([gla_w_a2.reshape(GATE_RANK, 64), jnp.zeros((GATE_RANK, 64), F32)],
                                   axis=1).reshape(1, 2, 8, LANES)]
    ffn2_shards = [_halves(w.astype(BF16)[None]) for w in (ffn2_w_gate[0].T, ffn2_w_up[0].T, ffn2_w_down[0])]
    def ffn1_weights(gathered):
        return lax.dynamic_update_slice(gathered, ffn1_shard[None], (0,) * 5).reshape(N_CHIPS, 3, FF_SHARD, D_MODEL)

    def rest_weights(gathered):
        win_all, wout_all, wa2_all = [own_block(g, s) for g, s in zip(gathered, rest_shards)]
        win_t = win_all.reshape(N_CHIPS, IN_ROWS, D_MODEL)
        w_in_pt = _pad_w_in_t(jnp.concatenate([win_t[p, 0:IN_SHARD] for p in range(N_CHIPS)], axis=0))
        wa2_p = jnp.pad(
            wa2_all.reshape(N_CHIPS, GATE_RANK, LANES).transpose(1, 0, 2).reshape(GATE_RANK, HEADS * LANES),
            ((0, LANES - GATE_RANK), (0, 0))).astype(BF16)
        return (None, w_in_pt, wout_all.reshape(D_MODEL, D_MODEL), wa2_p)

    def ffn2_weights(gathered):
        return [own_block(g, s).reshape(N_CHIPS, FF_SHARD, D_MODEL) for g, s in zip(gathered, ffn2_shards)]

    def pair_sums(grads, tag):
        halves = [g.reshape(g.shape[0], 2, g.shape[1] // 2, g.shape[2]) for g in grads]
        recv = _pair_exchange(halves, "pair_exchange_" + tag)
        return [_pair_add(g, r, c_arr, "pair_add_%s%d" % (tag, k)) for k, (g, r) in enumerate(zip(halves, recv))]

    early_sums = []

    def early(grads):
        early_sums.extend(pair_sums(grads, "early"))
        return _chip_exchange_plan(early_sums)

    late_sums = []

    def late(grad, number):
        late_sums.extend(pair_sums([grad], "late%d" % number))
        return _chip_exchange_plan(late_sums[-1:], by_peer=True)

    ba_p = jnp.pad(gla_b_a.reshape(HEADS, 64), ((0, 0), (0, 64))).reshape(1, HEADS * LANES)
    fb = _forward_backward(xs, target, None, None, ba_p, ffn1_norm_g, mix_norm_g, ret_norm_g, gla_norm_g,
                           ffn2_norm_g, final_norm_g.reshape(1, D_MODEL), ffn1_gather=(ffn1_shard, ffn1_weights),
                           rest_plan=_gather_plan(rest_shards), rest_weights=rest_weights,
                           ffn2_plans=[_gather_plan([s]) for s in ffn2_shards], ffn2_weights=ffn2_weights,
                           early=early, late=late)
    (loss_blk, grad_x, _, _, g_down1, _, _, _, _, _, g_wa2_p,
     d_ba_p, d_ffn1_g, d_mix_g, d_ffn2_g, d_final_g, d_ret_g, d_gla_g, early_arrived, late_arrived) = fb
    late_arrived = late_arrived + _run_hosted(late(g_down1, 3), "chip_exchange_late")
    mine = [_peer_sum(s, r, "chip_sum_%d" % k) for k, (s, r) in enumerate(zip(late_sums, late_arrived))]
    mine += [_chip_sum(s, r, me_arr, "chip_sum_%d" % (3 + k)) for k, (s, r) in enumerate(zip(early_sums, early_arrived))]
    other = _pair_share(mine)

    d_ba = d_ba_p.reshape(HEADS, LANES)[:, 0:64].reshape(1, 256)
    small_local = _pack_small(d_ffn1_g, d_mix_g, d_ffn2_g, d_final_g, d_ret_g, d_gla_g, d_ba, g_wa2_p[0], loss_blk)
    small_sum = _small_allreduce(small_local)
    loss = small_sum[6, 0]

    def rows(n1, nm, n2, nf, nret, ngla, ba, wa2):
        return [n1, nm, n2, nf.reshape(1, D_MODEL), nret, ngla, ba, wa2.reshape(GATE_RANK, 64)]

    small = _small_update(
        small_sum, me_arr,
        rows(ffn1_norm_g, mix_norm_g, ffn2_norm_g, final_norm_g, ret_norm_g, gla_norm_g, gla_b_a, gla_w_a2),
        rows(m_ffn1_norm_g, m_mix_norm_g, m_ffn2_norm_g, m_final_norm_g, m_ret_norm_g, m_gla_norm_g, m_gla_b_a,
             m_gla_w_a2),
        rows(v_ffn1_norm_g, v_mix_norm_g, v_ffn2_norm_g, v_final_norm_g, v_ret_norm_g, v_gla_norm_g, v_gla_b_a,
             v_gla_w_a2))
    s_grad, s_delta, s_m, s_v = [
        [*o[0:3], o[3].reshape(D_MODEL), *o[4:7], o[7].reshape(1, GATE_RANK, 64)] for o in small]

    def big(k, w, m, v, name, to_2d, from_2d):
        outs4 = _adamw_halves(to_2d(w), mine[k], other[k], to_2d(m), to_2d(v), c_arr, name)
        return [from_2d(z) for z in outs4]

    plain = (lambda w: w[0], lambda z: z[None])
    transposed = (lambda w: w[0].T, lambda z: z.T[None])
    in_proj = (lambda w: pad_rows(w[0].T), lambda z: z[0:IN_SHARD].T[None])
    r_g1 = big(0, ffn1_w_gate, m_ffn1_w_gate, v_ffn1_w_gate, "adamw_ffn1_gate", *transposed)
    r_u1 = big(1, ffn1_w_up, m_ffn1_w_up, v_ffn1_w_up, "adamw_ffn1_up", *transposed)
    r_d1 = big(2, ffn1_w_down, m_ffn1_w_down, v_ffn1_w_down, "adamw_ffn1_down", *plain)
    r_g2 = big(3, ffn2_w_gate, m_ffn2_w_gate, v_ffn2_w_gate, "adamw_ffn2_gate", *transposed)
    r_u2 = big(4, ffn2_w_up, m_ffn2_w_up, v_ffn2_w_up, "adamw_ffn2_up", *transposed)
    r_d2 = big(5, ffn2_w_down, m_ffn2_w_down, v_ffn2_w_down, "adamw_ffn2_down", *plain)
    r_in = big(6, w_in, m_w_in, v_w_in, "adamw_w_in", *in_proj)
    r_out = big(7, w_out, m_w_out, v_w_out, "adamw_w_out", *plain)

    def leaves(k, smalls):
        n1, nm, n2, nf, nret, ngla, ba, wa2 = smalls
        return [n1, r_g1[k], r_u1[k], r_d1[k], nm, r_in[k], nret, wa2, ba, ngla, r_out[k], n2, r_g2[k], r_u2[k], r_d2[k], nf]

    outs = [loss, grad_x.reshape(x.shape)]
    outs += leaves(0, s_grad) + leaves(1, s_delta) + leaves(2, s_m) + leaves(3, s_v)
    return tuple(outs)
```

```python
import functools

import jax
import jax.numpy as jnp
from jax import lax
from jax.experimental import pallas as pl
from jax.experimental.pallas import tpu as pltpu

F32, BF16 = jnp.float32, jnp.bfloat16
MESH = pl.DeviceIdType.MESH
ANY = pl.BlockSpec(memory_space=pl.ANY)

D_MODEL = 1024
D_FF = 2816
N_CHIPS = 4
FF_SHARD = D_FF // N_CHIPS
IN_WIDTH = 3088
IN_SHARD = IN_WIDTH // N_CHIPS
IN_ROWS = 800
CHUNK = 64
HEADS = 4
LANES = 128
PROJ_P = (2 * 4 * HEADS + 1) * LANES
GATE_RANK = 16
QK_SCALE = 0.125
GATE_NORM = 16.0
RMS_EPS = 1e-6
ROPE_BASE = 10000.0
ADAM_LR, ADAM_B1, ADAM_B2, ADAM_EPS, ADAM_WD, ADAM_STEP = 0.001, 0.9, 0.999, 1e-08, 0.01, 10
SMALL_ROWS = 32
TOKEN_TILE = 512
ATTN_TILE = 512

_ARB2 = pltpu.CompilerParams(dimension_semantics=("arbitrary", "arbitrary"))
_ARB1 = pltpu.CompilerParams(dimension_semantics=("arbitrary",))
_ARB3 = pltpu.CompilerParams(dimension_semantics=("arbitrary", "arbitrary", "arbitrary"))


def _dot(a, b):
    return jnp.dot(a, b, preferred_element_type=F32)


def _dot_nt(a, b):
    return lax.dot_general(a, b, (((1,), (1,)), ((), ())), preferred_element_type=F32)


def _dot_tn(a, b):
    return lax.dot_general(a, b, (((0,), (0,)), ((), ())), preferred_element_type=F32)


def _rms_scale(xv):
    return lax.rsqrt(jnp.mean(xv * xv, axis=-1, keepdims=True) + RMS_EPS)


def _rms_bwd(dh, xv, g):
    r = _rms_scale(xv)
    xhat = xv * r
    dxhat = dh * g
    dx = r * (dxhat - xhat * jnp.mean(dxhat * xhat, axis=-1, keepdims=True))
    return dx, jnp.sum(dh * xhat, axis=0, keepdims=True)


def _silu_grad(a, sg):
    return sg * (1.0 + a * (1.0 - sg))


class _Hosted:
    def __init__(self, arrays, out_shapes, n_sems, start, finish, middle=None):
        self.arrays, self.out_shapes, self.n_sems = list(arrays), list(out_shapes), n_sems
        self.start, self.finish = start, finish
        self.middle = middle if middle is not None else (lambda *refs: None)


def _call(body, args, *, name, grid, in_specs, out_specs, out_shape, scratch_shapes, compiler_params, hosted=None):
    if hosted is None:
        outs = pl.pallas_call(body, name=name, grid=grid, in_specs=in_specs, out_specs=out_specs, out_shape=out_shape,
                              scratch_shapes=scratch_shapes, compiler_params=compiler_params)(*args)
        return list(outs), []
    n_in, n_out, n_sc, nh = len(in_specs), len(out_specs), len(scratch_shapes), len(hosted.arrays)

    def wrapped(*refs):
        ins, h_in = refs[:n_in], refs[n_in:n_in + nh]
        outs, h_out = refs[n_in + nh:n_in + nh + n_out], refs[n_in + nh + n_out:n_in + 2 * nh + n_out]
        rest = refs[n_in + 2 * nh + n_out:]
        scratch, (send_sems, recv_sems) = rest[:n_sc], rest[n_sc:]
        step = functools.reduce(lambda flat, d: flat * grid[d] + pl.program_id(d), range(len(grid)), 0)
        total = functools.reduce(lambda a, b: a * b, grid)

        @pl.when(step == 0)
        def _():
            hosted.start(h_in, h_out, send_sems, recv_sems)

        @pl.when(step == total // 2)
        def _():
            hosted.middle(h_in, h_out, send_sems, recv_sems)

        body(*ins, *outs, *scratch)
        last = step == total - 1

        @pl.when(last)
        def _():
            hosted.finish(h_in, h_out, send_sems, recv_sems)

    sems = [pltpu.SemaphoreType.DMA((hosted.n_sems,)), pltpu.SemaphoreType.DMA((hosted.n_sems,))]
    outs = pl.pallas_call(
        wrapped, name=name, grid=grid, in_specs=list(in_specs) + [ANY] * nh, out_specs=list(out_specs) + [ANY] * nh,
        out_shape=list(out_shape) + hosted.out_shapes, scratch_shapes=list(scratch_shapes) + sems,
        compiler_params=compiler_params)(*args, *hosted.arrays)
    return list(outs[:n_out]), list(outs[n_out:])


def _run_hosted(hosted, name):
    nh = len(hosted.arrays)

    def body(*refs):
        h_in, h_out, (send_sems, recv_sems) = refs[:nh], refs[nh:2 * nh], refs[2 * nh:]
        hosted.start(h_in, h_out, send_sems, recv_sems)
        hosted.middle(h_in, h_out, send_sems, recv_sems)
        hosted.finish(h_in, h_out, send_sems, recv_sems)

    sems = [pltpu.SemaphoreType.DMA((hosted.n_sems,)), pltpu.SemaphoreType.DMA((hosted.n_sems,))]
    return list(pl.pallas_call(body, name=name, in_specs=[ANY] * nh, out_specs=[ANY] * nh,
                               out_shape=hosted.out_shapes, scratch_shapes=sems)(*hosted.arrays))


def _ffn_weight_operands(ffn_w, chunk_maps):
    if isinstance(ffn_w, (list, tuple)):
        specs = [pl.BlockSpec((None, FF_SHARD, D_MODEL), lambda *g, m=m: (m(*g), 0, 0)) for m in chunk_maps]
        return list(ffn_w), specs
    specs = [pl.BlockSpec((None, None, FF_SHARD, D_MODEL), lambda *g, m=m, k=kind: (m(*g), k, 0, 0))
             for kind, m in enumerate(chunk_maps)]
    return [ffn_w] * 3, specs


def _pipeline_items(steps):
    def cur(s):
        c = jnp.minimum(s, steps - 1)
        return c // N_CHIPS, c % N_CHIPS

    def prev(s):
        p = jnp.maximum(s - 1, 0)
        return p // N_CHIPS, p % N_CHIPS

    return cur, prev


def _ffn_fwd(x, g, ffn_w, name, hosted=None):
    t = x.shape[0]
    tm = min(t, TOKEN_TILE)

    def body(x_ref, g_ref, wg_ref, wu_ref, wd_ref, xo_ref, a_ref, u_ref, h_ref, acc_ref):
        j = pl.program_id(1)

        @pl.when(j == 0)
        def _():
            xv = x_ref[...]
            h_ref[...] = ((xv * _rms_scale(xv)) * g_ref[...]).astype(BF16)
            acc_ref[...] = jnp.zeros_like(acc_ref)

        h = h_ref[...]
        a = _dot_nt(h, wg_ref[...])
        u = _dot_nt(h, wu_ref[...])
        a_ref[...] = a.astype(BF16)
        u_ref[...] = u.astype(BF16)
        hid = (a * jax.nn.sigmoid(a)) * u
        acc_ref[...] += _dot(hid.astype(BF16), wd_ref[...])

        @pl.when(j == N_CHIPS - 1)
        def _():
            xo_ref[...] = x_ref[...] + 0.5 * acc_ref[...]

    tok = pl.BlockSpec((tm, D_MODEL), lambda i, j: (i, 0))
    act = pl.BlockSpec((None, tm, FF_SHARD), lambda i, j: (j, i, 0))
    w_arrays, weights = _ffn_weight_operands(ffn_w, [lambda i, j: j] * 3)
    return _call(
        body, (x, g, *w_arrays), name=name, grid=(t // tm, N_CHIPS),
        in_specs=[tok, pl.BlockSpec((1, D_MODEL), lambda i, j: (0, 0))] + weights,
        out_specs=[tok, act, act, tok],
        out_shape=[jax.ShapeDtypeStruct((t, D_MODEL), F32),
                   jax.ShapeDtypeStruct((N_CHIPS, t, FF_SHARD), BF16),
                   jax.ShapeDtypeStruct((N_CHIPS, t, FF_SHARD), BF16),
                   jax.ShapeDtypeStruct((t, D_MODEL), BF16)],
        scratch_shapes=[pltpu.VMEM((tm, D_MODEL), F32)],
        compiler_params=_ARB2, hosted=hosted)


def _ffn1_fwd_gathering(x, g, shard, name, hosted=None):
    t = x.shape[0]
    tm = min(t, TOKEN_TILE)
    nt = t // tm
    nh = 0 if hosted is None else len(hosted.arrays)

    def body(*refs):
        x_ref, g_ref, shard_ref = refs[0:3]
        h_in = refs[3:3 + nh]
        xo_ref, a_ref, u_ref, h_ref, wall = refs[3 + nh:8 + nh]
        h_out = refs[8 + nh:8 + 2 * nh]
        acc, h_all, wbuf, load_sem, send_sems, recv_sems = refs[8 + 2 * nh:14 + 2 * nh]
        carried_sems = refs[14 + 2 * nh:]
        k, i = pl.program_id(0), pl.program_id(1)
        legs, _ = _gather_legs(shard_ref, wall, send_sems, recv_sems, 0, True)
        begin, pass_on, _ = _gather_steps(legs, True)

        def load_chunk(src):
            load = pltpu.make_async_copy(src, wbuf, load_sem)
            load.start()
            load.wait()

        @pl.when((k == 0) & (i == 0))
        def _():
            begin()
            load_chunk(shard_ref)
            if hosted is not None:
                hosted.start(h_in, h_out, *carried_sems)

        @pl.when((k == PEER_SLOT[1]) & (i == 0))
        def _():
            pass_on()
            legs["pass_y"][1].wait_recv()
            load_chunk(wall.at[PEER_SLOT[1]])

        @pl.when((k == PEER_SLOT[0]) & (i == 0))
        def _():
            legs["pass_x"][1].wait_recv()
            load_chunk(wall.at[PEER_SLOT[0]])
            if hosted is not None:
                hosted.middle(h_in, h_out, *carried_sems)

        @pl.when((k == PEER_SLOT[2]) & (i == 0))
        def _():
            legs["fwd_y"][1].wait_recv()
            legs["pass_d0"][0].start()
            legs["fwd_x"][1].wait_recv()
            legs["pass_d1"][0].start()
            legs["pass_d0"][1].wait_recv()
            legs["pass_d1"][1].wait_recv()
            load_chunk(wall.at[PEER_SLOT[2]])

        @pl.when(k == 0)
        def _():
            xv = x_ref[...]
            h0 = ((xv * _rms_scale(xv)) * g_ref[...]).astype(BF16)
            h_all[i] = h0
            h_ref[...] = h0

        h = h_all[i]
        wg, wu, wd = (wbuf[kind].reshape(FF_SHARD, D_MODEL) for kind in range(3))
        a = _dot_nt(h, wg)
        u = _dot_nt(h, wu)
        a_ref[...] = a.astype(BF16)
        u_ref[...] = u.astype(BF16)
        part = _dot(((a * jax.nn.sigmoid(a)) * u).astype(BF16), wd)

        @pl.when(k == 0)
        def _():
            acc[i] = part

        @pl.when(k > 0)
        def _():
            acc[i] += part

        @pl.when(k == N_CHIPS - 1)
        def _():
            xo_ref[...] = x_ref[...] + 0.5 * acc[i]

        @pl.when((k == N_CHIPS - 1) & (i == nt - 1))
        def _():
            for pair in legs.values():
                pair[0].wait_send()
            if hosted is not None:
                hosted.finish(h_in, h_out, *carried_sems)

    def first_or_last(k):
        return (k == 0) | (k == N_CHIPS - 1)

    tok = lambda keep: pl.BlockSpec((tm, D_MODEL), lambda k, i: (jnp.where(keep(k), i, 0), 0))
    act = pl.BlockSpec((None, tm, FF_SHARD), lambda k, i: (k, i, 0))
    act_shape = jax.ShapeDtypeStruct((N_CHIPS, t, FF_SHARD), BF16)
    carried = [] if hosted is None else [pltpu.SemaphoreType.DMA((hosted.n_sems,))] * 2
    outs = pl.pallas_call(
        body, name=name, grid=(N_CHIPS, nt),
        in_specs=[tok(first_or_last), pl.BlockSpec((1, D_MODEL), lambda k, i: (0, 0)), ANY] + [ANY] * nh,
        out_specs=[tok(lambda k: k == N_CHIPS - 1), act, act,
                   pl.BlockSpec((tm, D_MODEL), lambda k, i: (jnp.where(k == 0, i, nt - 1), 0)), ANY] + [ANY] * nh,
        out_shape=[jax.ShapeDtypeStruct((t, D_MODEL), F32), act_shape, act_shape,
                   jax.ShapeDtypeStruct((t, D_MODEL), BF16),
                   jax.ShapeDtypeStruct((N_CHIPS,) + shard.shape, shard.dtype)]
                  + ([] if hosted is None else hosted.out_shapes),
        scratch_shapes=[pltpu.VMEM((nt, tm, D_MODEL), F32), pltpu.VMEM((nt, tm, D_MODEL), BF16),
                        pltpu.VMEM(shard.shape, shard.dtype), pltpu.SemaphoreType.DMA,
                        pltpu.SemaphoreType.DMA((8,)), pltpu.SemaphoreType.DMA((8,))] + carried,
        compiler_params=_ARB2,
    )(x, g, shard, *([] if hosted is None else hosted.arrays))
    return list(outs[:5]), list(outs[5:])


def _ffn_bwd(dxo, x, g, a4, u4, ffn_w, name, hosted=None):
    t = x.shape[0]
    tm = min(t, TOKEN_TILE)
    steps = (t // tm) * N_CHIPS
    cur, prev = _pipeline_items(steps)


    def body(dxo_ref, dxo_prev_ref, x_ref, g_ref, a_ref, u_ref, wg_ref, wu_ref, wd_ref,
             da_ref, du_ref, hid_ref, dob_ref, dx_ref, dg_ref, acc_ref, da_slots, du_slots):
        s = pl.program_id(0)
        jc, jp = cur(s)[1], prev(s)[1]
        slot = s % 2

        @pl.when(s == 0)
        def _():
            dg_ref[...] = jnp.zeros_like(dg_ref)
            acc_ref[...] = jnp.zeros_like(acc_ref)
            da_slots[...] = jnp.zeros_like(da_slots)
            du_slots[...] = jnp.zeros_like(du_slots)

        @pl.when(jc == 0)
        def _():
            dob_ref[...] = (0.5 * dxo_ref[...]).astype(BF16)

        dhid = _dot_nt(dob_ref[...], wd_ref[...])
        a = a_ref[...].astype(F32)
        u = u_ref[...].astype(F32)
        sg = jax.nn.sigmoid(a)
        sl = a * sg
        hid_ref[...] = (sl * u).astype(BF16)
        du = (dhid * sl).astype(BF16)
        da = (dhid * u * _silu_grad(a, sg)).astype(BF16)
        du_ref[...] = du
        da_ref[...] = da
        acc_ref[...] += _dot(da_slots[1 - slot], wg_ref[...]) + _dot(du_slots[1 - slot], wu_ref[...])
        da_slots[slot] = da
        du_slots[slot] = du

        @pl.when((jp == N_CHIPS - 1) & (s > 0))
        def _():
            dx, dg = _rms_bwd(acc_ref[...], x_ref[...], g_ref[...])
            dx_ref[...] = dxo_prev_ref[...] + dx
            dg_ref[...] += dg
            acc_ref[...] = jnp.zeros_like(acc_ref)

    tok_cur = pl.BlockSpec((tm, D_MODEL), lambda s: (cur(s)[0], 0))
    tok_prev = pl.BlockSpec((tm, D_MODEL), lambda s: (prev(s)[0], 0))
    act = pl.BlockSpec((None, tm, FF_SHARD), lambda s: (cur(s)[1], cur(s)[0], 0))
    row = pl.BlockSpec((1, D_MODEL), lambda s: (0, 0))
    w_arrays, weights = _ffn_weight_operands(ffn_w, [lambda s: prev(s)[1], lambda s: prev(s)[1], lambda s: cur(s)[1]])
    act_shape = jax.ShapeDtypeStruct((N_CHIPS, t, FF_SHARD), BF16)
    return _call(
        body, (dxo, dxo, x, g, a4, u4, *w_arrays), name=name, grid=(steps + 1,),
        in_specs=[tok_cur, tok_prev, tok_prev, row, act, act] + weights,
        out_specs=[act, act, act, tok_cur, tok_prev, row],
        out_shape=[act_shape, act_shape, act_shape,
                   jax.ShapeDtypeStruct((t, D_MODEL), BF16),
                   jax.ShapeDtypeStruct((t, D_MODEL), F32),
                   jax.ShapeDtypeStruct((1, D_MODEL), F32)],
        scratch_shapes=[pltpu.VMEM((tm, D_MODEL), F32), pltpu.VMEM((2, tm, FF_SHARD), BF16),
                        pltpu.VMEM((2, tm, FF_SHARD), BF16)],
        compiler_params=_ARB1, hosted=hosted)


def _matmul_tn(a, b, name, tka=None, out_dtype=F32, hosted=None):
    a3, b3 = a.ndim == 3, b.ndim == 3
    nb = a.shape[0] if a3 else (b.shape[0] if b3 else 1)
    t, ka, n = a.shape[-2], a.shape[-1], b.shape[-1]
    tka = ka if tka is None else tka
    tk = min(t, 2 * TOKEN_TILE)
    nk = t // tk

    def body(a_ref, b_ref, o_ref, acc_ref):
        k = pl.program_id(2)

        @pl.when(k == 0)
        def _():
            acc_ref[...] = jnp.zeros_like(acc_ref)

        acc_ref[...] += _dot_tn(a_ref[...].astype(BF16), b_ref[...].astype(BF16))

        @pl.when(k == nk - 1)
        def _():
            o_ref[...] = acc_ref[...].astype(out_dtype)

    a_spec = (pl.BlockSpec((None, tk, tka), lambda i, j, k: (i, k, j)) if a3
              else pl.BlockSpec((tk, tka), lambda i, j, k: (k, j)))
    b_spec = (pl.BlockSpec((None, tk, n), lambda i, j, k: (i, k, 0)) if b3
              else pl.BlockSpec((tk, n), lambda i, j, k: (k, 0)))
    outs, carried = _call(
        body, (a, b), name=name, grid=(nb, ka // tka, t // tk),
        in_specs=[a_spec, b_spec],
        out_specs=[pl.BlockSpec((None, tka, n), lambda i, j, k: (i, j, 0))],
        out_shape=[jax.ShapeDtypeStruct((nb, ka, n), out_dtype)],
        scratch_shapes=[pltpu.VMEM((tka, n), F32)],
        compiler_params=_ARB3, hosted=hosted)
    return outs[0] if hosted is None else (outs[0], carried)


def _matmul_nt(a, w, name, out_dtype=F32):
    t, k = a.shape
    n = w.shape[0]
    tm = min(t, TOKEN_TILE)

    def body(a_ref, w_ref, o_ref):
        o_ref[...] = _dot_nt(a_ref[...].astype(BF16), w_ref[...]).astype(out_dtype)

    return pl.pallas_call(
        body, name=name, grid=(t // tm,),
        in_specs=[pl.BlockSpec((tm, k), lambda i: (i, 0)), pl.BlockSpec((n, k), lambda i: (0, 0))],
        out_specs=pl.BlockSpec((tm, n), lambda i: (i, 0)),
        out_shape=jax.ShapeDtypeStruct((t, n), out_dtype),
        compiler_params=_ARB1,
    )(a, w)


def _mixer_in_bwd(dproj, w_in_pt, dres, x, g, name):
    t, k = dproj.shape
    tm = min(t, TOKEN_TILE)

    def body(a_ref, w_ref, dres_ref, x_ref, g_ref, dx_ref, dg_ref):
        @pl.when(pl.program_id(0) == 0)
        def _():
            dg_ref[...] = jnp.zeros_like(dg_ref)

        dh = _dot(a_ref[...], w_ref[...])
        dx, dg = _rms_bwd(dh, x_ref[...], g_ref[...])
        dx_ref[...] = dres_ref[...] + dx
        dg_ref[...] += dg

    tok = pl.BlockSpec((tm, D_MODEL), lambda i: (i, 0))
    row = pl.BlockSpec((1, D_MODEL), lambda i: (0, 0))
    return pl.pallas_call(
        body, name=name, grid=(t // tm,),
        in_specs=[pl.BlockSpec((tm, k), lambda i: (i, 0)), pl.BlockSpec((k, D_MODEL), lambda i: (0, 0)), tok, tok, row],
        out_specs=[tok, row],
        out_shape=[jax.ShapeDtypeStruct((t, D_MODEL), F32), jax.ShapeDtypeStruct((1, D_MODEL), F32)],
        compiler_params=_ARB1,
    )(dproj, w_in_pt, dres, x, g)


def _mixer_in_fwd(x, g, w_in_pt, name, hosted=None):
    t = x.shape[0]
    tm = min(t, TOKEN_TILE)
    tn = PROJ_P // 3

    def body(x_ref, g_ref, w_ref, p_ref, h_ref):
        @pl.when(pl.program_id(1) == 0)
        def _():
            xv = x_ref[...]
            h_ref[...] = ((xv * _rms_scale(xv)) * g_ref[...]).astype(BF16)

        p_ref[...] = _dot_nt(h_ref[...], w_ref[...])

    tok = pl.BlockSpec((tm, D_MODEL), lambda i, j: (i, 0))
    return _call(
        body, (x, g, w_in_pt), name=name, grid=(t // tm, 3),
        in_specs=[tok, pl.BlockSpec((1, D_MODEL), lambda i, j: (0, 0)),
                  pl.BlockSpec((tn, D_MODEL), lambda i, j: (j, 0))],
        out_specs=[pl.BlockSpec((tm, tn), lambda i, j: (i, j)), tok],
        out_shape=[jax.ShapeDtypeStruct((t, PROJ_P), F32), jax.ShapeDtypeStruct((t, D_MODEL), BF16)],
        scratch_shapes=[], compiler_params=_ARB2, hosted=hosted)


def _mixer_out_fwd(o_ret, o_gla, w_out, x, name):
    t = x.shape[0]
    tm = min(t, TOKEN_TILE)
    half = HEADS * LANES

    def body(a_ref, b_ref, w_ref, x_ref, o_ref):
        o_ref[...] = x_ref[...] + _dot(a_ref[...], w_ref[0:half, :]) + _dot(b_ref[...], w_ref[half:2 * half, :])

    tok = pl.BlockSpec((tm, D_MODEL), lambda i: (i, 0))
    hb = pl.BlockSpec((tm, half), lambda i: (i, 0))
    return pl.pallas_call(
        body, name=name, grid=(t // tm,),
        in_specs=[hb, hb, pl.BlockSpec((2 * half, D_MODEL), lambda i: (0, 0)), tok],
        out_specs=tok, out_shape=jax.ShapeDtypeStruct((t, D_MODEL), F32),
        compiler_params=_ARB1,
    )(o_ret, o_gla, w_out, x)


def _final_loss(x, g, target, name):
    t = x.shape[0]
    tm = min(t, TOKEN_TILE)

    def body(x_ref, g_ref, t_ref, l_ref, dx_ref, dg_ref):
        @pl.when(pl.program_id(0) == 0)
        def _():
            l_ref[...] = jnp.zeros_like(l_ref)
            dg_ref[...] = jnp.zeros_like(dg_ref)

        xv = x_ref[...]
        gv = g_ref[...]
        err = (xv * _rms_scale(xv)) * gv - t_ref[...]
        l_ref[...] += 0.5 * jnp.sum(jnp.mean(err * err, axis=-1, keepdims=True), axis=0, keepdims=True)
        dx, dg = _rms_bwd(err * (1.0 / D_MODEL), xv, gv)
        dx_ref[...] = dx
        dg_ref[...] += dg

    tok = pl.BlockSpec((tm, D_MODEL), lambda i: (i, 0))
    row = pl.BlockSpec((1, D_MODEL), lambda i: (0, 0))
    return pl.pallas_call(
        body, name=name, grid=(t // tm,),
        in_specs=[tok, row, tok],
        out_specs=[pl.BlockSpec((8, LANES), lambda i: (0, 0)), tok, row],
        out_shape=[jax.ShapeDtypeStruct((8, LANES), F32), jax.ShapeDtypeStruct((t, D_MODEL), F32),
                   jax.ShapeDtypeStruct((1, D_MODEL), F32)],
        compiler_params=_ARB1,
    )(x, g, target)


def _rot(v, cos, sa, sb):
    return v * cos + pltpu.roll(v, 96, 1) * sa + pltpu.roll(v, 32, 1) * sb


def _rot_t(d, cos, sa, sb):
    return d * cos + pltpu.roll(d * sa, 32, 1) + pltpu.roll(d * sb, 96, 1)


def _bmm(a, b):
    return jnp.einsum("cik,ckj->cij", a, b, preferred_element_type=F32)


def _bmm_nt(a, b):
    return jnp.einsum("cik,cjk->cij", a, b, preferred_element_type=F32)


def _bmm_tn(a, b):
    return jnp.einsum("cki,ckj->cij", a, b, preferred_element_type=F32)


def _masked_sum(mask, x):
    hi = x.astype(BF16)
    r1 = x - hi.astype(F32)
    mid = r1.astype(BF16)
    lo = (r1 - mid.astype(F32)).astype(BF16)
    return _bmm(mask, hi) + _bmm(mask, mid) + _bmm(mask, lo)


def _tile_inputs(is_ret, qkvg_refs, aux, nc):
    shape3 = (nc, CHUNK, LANES)
    q_raw, k_raw, v, gate = (r[...] for r in qkvg_refs)
    ri = lax.broadcasted_iota(jnp.int32, (nc, CHUNK, CHUNK), 1)
    ci = lax.broadcasted_iota(jnp.int32, (nc, CHUNK, CHUNK), 2)
    if is_ret:
        cos_ref, sa_ref, sb_ref, lg_ref = aux
        cos, sa, sb = cos_ref[...], sa_ref[...], sb_ref[...]
        q = _rot(q_raw, cos, sa, sb)
        k = _rot(k_raw, cos, sa, sb) * QK_SCALE
        steps = (lax.broadcasted_iota(jnp.int32, shape3, 1) + 1).astype(F32)
        b = steps * lg_ref[...]
        logit = jnp.exp(jnp.abs(ri - ci).astype(F32) * lg_ref[:, 0:CHUNK])
    else:
        glow_ref, wa2_ref, ba_ref = aux
        logit = _dot(glow_ref[...].astype(BF16), wa2_ref[...]) + ba_ref[...]
        la = (jnp.minimum(logit, 0.0) - jnp.log1p(jnp.exp(-jnp.abs(logit)))) * (1.0 / GATE_NORM)
        b = _masked_sum((ci <= ri).astype(BF16), la.reshape(shape3))
        q = q_raw * QK_SCALE
        k = k_raw
    return q.reshape(shape3), k.reshape(shape3), v.reshape(shape3), gate, b, logit, ri, ci


def _tile_scores(q, k, b, ri, ci):
    mid = b[:, CHUNK // 2 - 1:CHUNK // 2, :]
    ep = jnp.exp(b - mid)
    en = jnp.exp(mid - b)
    qt, kt, qh, kh = q * ep, k * en, q * en, k * ep
    low = _bmm_nt(qt.astype(BF16), kt.astype(BF16))
    upp = _bmm_nt(qh.astype(BF16), kh.astype(BF16))
    scores = jnp.where(ci <= ri, low, upp)
    return scores, ep, en, qt, kt, qh, kh


def _attn_specs(is_ret, t, tb, imap_t):
    nb = t // tb
    base = 0 if is_ret else 4 * HEADS
    proj = [pl.BlockSpec((tb, LANES), lambda h, i, s=sec: (imap_t(i), base + HEADS * s + h)) for sec in range(4)]
    lane_t = pl.BlockSpec((tb, LANES), lambda h, i: (imap_t(i), 0))
    if is_ret:
        aux = [lane_t, lane_t, lane_t, pl.BlockSpec((None, 1, LANES), lambda h, i: (h, 0, 0))]
    else:
        aux = [pl.BlockSpec((tb, LANES), lambda h, i: (imap_t(i), PROJ_P // LANES - 1)),
               pl.BlockSpec((LANES, LANES), lambda h, i: (0, h)),
               pl.BlockSpec((1, LANES), lambda h, i: (0, h))]
    gain = pl.BlockSpec((1, LANES), lambda h, i: (0, h))
    head_t = pl.BlockSpec((tb, LANES), lambda h, i: (imap_t(i), h))
    state = pl.BlockSpec((None, tb // CHUNK, LANES, LANES), lambda h, i: (h, imap_t(i), 0, 0))
    return nb, proj, aux, gain, head_t, state


def _attn_fwd(is_ret, proj, aux_arrays, gain, name, hosted=None):
    t = proj.shape[0]
    tb = min(t, ATTN_TILE)
    nc = tb // CHUNK
    n_aux = 4 if is_ret else 3
    nb, proj_spec, aux_specs, gain_spec, head_t, state_spec = _attn_specs(is_ret, t, tb, lambda i: i)

    def body(*refs):
        qkvg_refs = refs[0:4]
        aux = refs[4:4 + n_aux]
        gn_ref, ofin_ref, oraw_ref, st_ref, state = refs[4 + n_aux:]

        @pl.when(pl.program_id(1) == 0)
        def _():
            state[...] = jnp.zeros_like(state)

        q, k, v, gate, b, decay, ri, ci = _tile_inputs(is_ret, qkvg_refs, aux, nc)
        if is_ret:
            scores = _bmm_nt(q.astype(BF16), k.astype(BF16)) * decay
        else:
            scores = _tile_scores(q, k, b, ri, ci)[0]
        vb = v.astype(BF16)
        intra = _bmm(scores.astype(BF16), vb)
        b_last = b[:, CHUNK - 1:CHUNK, :]
        e_last = jnp.exp(b_last)
        grow = _bmm_tn(vb, (k * jnp.exp(b_last - b)).astype(BF16))
        st = state[...]
        for c in range(nc):
            st_ref[c] = st
            st = st * e_last[c] + grow[c]
        state[...] = st
        inter = _bmm_nt((q * jnp.exp(b)).astype(BF16), st_ref[...].astype(BF16))
        out = (intra + inter).reshape(tb, LANES)
        oraw_ref[...] = out
        normed = out * _rms_scale(out)
        ofin_ref[...] = ((normed * gn_ref[...]) * (gate * jax.nn.sigmoid(gate))).astype(BF16)

    width = HEADS * LANES
    return _call(
        body, (proj, proj, proj, proj, *aux_arrays, gain), name=name, grid=(HEADS, nb),
        in_specs=proj_spec + aux_specs + [gain_spec],
        out_specs=[head_t, head_t, state_spec],
        out_shape=[jax.ShapeDtypeStruct((t, width), BF16), jax.ShapeDtypeStruct((t, width), F32),
                   jax.ShapeDtypeStruct((HEADS, t // CHUNK, LANES, LANES), F32)],
        scratch_shapes=[pltpu.VMEM((LANES, LANES), F32)],
        compiler_params=_ARB2, hosted=hosted)


def _attn_bwd(is_ret, proj, aux_arrays, gain, o_raw, states, d_out, name):
    t = proj.shape[0]
    tb = min(t, ATTN_TILE)
    nc = tb // CHUNK
    n_aux = 4 if is_ret else 3
    nblk = t // tb
    nb, proj_spec, aux_specs, gain_spec, head_t, state_spec = _attn_specs(is_ret, t, tb, lambda i: nblk - 1 - i)
    base = 0 if is_ret else HEADS
    dout_spec = pl.BlockSpec((tb, LANES), lambda h, i: (nblk - 1 - i, base + h))

    def body(*refs):
        qkvg_refs = refs[0:4]
        aux = refs[4:4 + n_aux]
        gn_ref, oraw_ref, st_ref, dfin_ref = refs[4 + n_aux:8 + n_aux]
        dq_ref, dk_ref, dv_ref, dgate_ref, dgn_ref = refs[8 + n_aux:13 + n_aux]
        if is_ret:
            dstate, dafter_ref = refs[13 + n_aux:]
        else:
            dlogit_ref, dba_ref, dstate, dafter_ref = refs[13 + n_aux:]

        @pl.when(pl.program_id(1) == 0)
        def _():
            dstate[...] = jnp.zeros_like(dstate)
            dgn_ref[...] = jnp.zeros_like(dgn_ref)
            if not is_ret:
                dba_ref[...] = jnp.zeros_like(dba_ref)

        shape3 = (nc, CHUNK, LANES)
        q, k, v, gate, b, logit, ri, ci = _tile_inputs(is_ret, qkvg_refs, aux, nc)
        eb = jnp.exp(b)
        qe = q * eb
        b_last = b[:, CHUNK - 1:CHUNK, :]
        e_last = jnp.exp(b_last)
        ekd = jnp.exp(b_last - b)
        kd = k * ekd

        gn = gn_ref[...]
        out = oraw_ref[...]
        r = _rms_scale(out)
        normed = out * r
        sg = jax.nn.sigmoid(gate)
        dfin = dfin_ref[...]
        dgate = dfin * (normed * gn) * _silu_grad(gate, sg)
        dpre = dfin * (gate * sg)
        dgn_ref[...] += jnp.sum(dpre * normed, axis=0, keepdims=True)
        dnormed = dpre * gn
        d_o = r * (dnormed - normed * jnp.mean(dnormed * normed, axis=-1, keepdims=True))
        dob, vb = d_o.reshape(shape3).astype(BF16), v.astype(BF16)

        dgrow = _bmm_tn(dob, qe.astype(BF16))
        dst = dstate[...]
        for c in reversed(range(nc)):
            dafter_ref[c] = dst
            dst = dst * e_last[c] + dgrow[c]
        dstate[...] = dst
        st = st_ref[...]
        dafter = dafter_ref[...]
        stb, dafter_b = st.astype(BF16), dafter.astype(BF16)

        dsc = _bmm_nt(dob, vb)
        dsc_t = _bmm_nt(vb, dob)
        dqe = _bmm(dob, stb)
        dkd = _bmm(vb, dafter_b)
        if is_ret:
            decay, qb, kb = logit, q.astype(BF16), k.astype(BF16)
            scores_t = _bmm_nt(kb, qb) * decay
            dq = _bmm((dsc * decay).astype(BF16), kb) + dqe * eb
            dk = _bmm((dsc_t * decay).astype(BF16), qb) + dkd * ekd
        else:
            _, ep, en, qt, kt, qh, kh = _tile_scores(q, k, b, ri, ci)
            qtb, ktb, qhb, khb = qt.astype(BF16), kt.astype(BF16), qh.astype(BF16), kh.astype(BF16)
            scores_t = jnp.where(ci >= ri, _bmm_nt(ktb, qtb), _bmm_nt(khb, qhb))
            dqt = _bmm(jnp.where(ci <= ri, dsc, 0.0).astype(BF16), ktb)
            dqh = _bmm(jnp.where(ci <= ri, 0.0, dsc).astype(BF16), khb)
            dkt = _bmm(jnp.where(ci >= ri, dsc_t, 0.0).astype(BF16), qtb)
            dkh = _bmm(jnp.where(ci >= ri, 0.0, dsc_t).astype(BF16), qhb)
            dq = dqt * ep + dqh * en + dqe * eb
            dk = dkt * en + dkh * ep + dkd * ekd
        dv = _bmm(scores_t.astype(BF16), dob) + _bmm_nt(kd.astype(BF16), dafter_b)
        dq, dk = dq.reshape(tb, LANES), dk.reshape(tb, LANES)

        if is_ret:
            cos_ref, sa_ref, sb_ref, _ = aux
            cos, sa, sb = cos_ref[...], sa_ref[...], sb_ref[...]
            dq_raw = _rot_t(dq, cos, sa, sb)
            dk_raw = _rot_t(dk, cos, sa, sb) * QK_SCALE
        else:
            dq_raw = dq * QK_SCALE
            dk_raw = dk
            db = dqt * qt - dkt * kt - dqh * qh + dkh * kh + dqe * qe - dkd * kd
            db_last = (jnp.sum(dkd * kd, axis=1, keepdims=True)
                       + jnp.sum(dafter * st, axis=1, keepdims=True) * e_last)
            last_row = lax.broadcasted_iota(jnp.int32, shape3, 1) == CHUNK - 1
            db = db + jnp.where(last_row, db_last, 0.0)
            dla = _masked_sum((ci >= ri).astype(BF16), db).reshape(tb, LANES)
            dlogit = dla * (1.0 / GATE_NORM) * jax.nn.sigmoid(-logit)
            dlogit_ref[...] = dlogit.astype(BF16)
            dba_ref[...] += jnp.sum(dlogit, axis=0, keepdims=True)

        dq_ref[...] = dq_raw.astype(BF16)
        dk_ref[...] = dk_raw.astype(BF16)
        dv_ref[...] = dv.reshape(tb, LANES).astype(BF16)
        dgate_ref[...] = dgate.astype(BF16)

    width = HEADS * LANES
    row_out = pl.BlockSpec((1, LANES), lambda h, i: (0, h))
    out_specs = [head_t] * 4 + [row_out]
    out_shape = [jax.ShapeDtypeStruct((t, width), BF16)] * 4 + [jax.ShapeDtypeStruct((1, width), F32)]
    if not is_ret:
        out_specs += [head_t, row_out]
        out_shape += [jax.ShapeDtypeStruct((t, width), BF16), jax.ShapeDtypeStruct((1, width), F32)]
    return pl.pallas_call(
        body, name=name, grid=(HEADS, nblk),
        in_specs=proj_spec + aux_specs + [gain_spec, head_t, state_spec, dout_spec],
        out_specs=out_specs, out_shape=out_shape,
        scratch_shapes=[pltpu.VMEM((LANES, LANES), F32), pltpu.VMEM((nc, LANES, LANES), F32)],
        compiler_params=_ARB2,
    )(proj, proj, proj, proj, *aux_arrays, gain, o_raw, states, d_out)


PEER_SLOT = (2, 1, 3)


def _place():
    x, y, c = lax.axis_index("x"), lax.axis_index("y"), lax.axis_index("c")
    chips = [(1 - x, y), (x, 1 - y), (1 - x, 1 - y)]
    return x, y, c, 2 * x + y, chips


def _route_split(rows, dtype):
    tile = 16 if dtype == BF16 else 8
    if rows < 2 * tile:
        return None
    return -(-(rows // 2) // tile) * tile


def _routes(by_peer):
    x, y, c, me, chips = _place()
    (xx, xy), (yx, yy), (dx, dy) = chips
    if by_peer:
        slots = dict(own=0, from_x=PEER_SLOT[0], from_y=PEER_SLOT[1], diag=PEER_SLOT[2],
                     mine_on_x=PEER_SLOT[0], mine_on_y=PEER_SLOT[1])
    else:
        slots = dict(own=me, from_x=2 * xx + xy, from_y=2 * yx + yy, diag=2 * dx + dy, mine_on_x=me, mine_on_y=me)
    return c, (xx, xy, c), (yx, yy, c), (dx, dy, c), (x, y, 1 - c), slots


def _gather_legs(src, out, send_sems, recv_sems, base, by_peer):
    c, to_x, to_y, to_d, sibling, s = _routes(by_peer)
    r0 = _route_split(src.shape[2], src.dtype)

    def cp(k, src_ref, dst_ref, to):
        return pltpu.make_async_remote_copy(src_ref=src_ref, dst_ref=dst_ref, send_sem=send_sems.at[base + k],
                                            recv_sem=recv_sems.at[base + k], device_id=to, device_id_type=MESH)

    mine = src.at[:, c]
    legs = dict(
        x=(cp(0, mine, out.at[s["mine_on_x"], :, c], to_x), cp(0, mine, out.at[s["from_x"], :, c], to_x)),
        y=(cp(1, mine, out.at[s["mine_on_y"], :, c], to_y), cp(1, mine, out.at[s["from_y"], :, c], to_y)),
        pass_x=(cp(4, out.at[s["from_x"], :, c], out.at[s["from_x"], :, c], sibling),
                cp(4, mine, out.at[s["from_x"], :, 1 - c], sibling)),
        pass_y=(cp(5, out.at[s["from_y"], :, c], out.at[s["from_y"], :, c], sibling),
                cp(5, mine, out.at[s["from_y"], :, 1 - c], sibling)))
    if r0 is None:
        mine_on_d = s["diag"] if by_peer else s["own"]
        legs["d"] = (cp(2, mine, out.at[mine_on_d, :, c], to_d), cp(2, mine, out.at[s["diag"], :, c], to_d))
        legs["pass_d"] = (cp(6, out.at[s["diag"], :, c], out.at[s["diag"], :, c], sibling),
                          cp(6, mine, out.at[s["diag"], :, 1 - c], sibling))
        return legs, False
    lo, hi = pl.ds(0, r0), pl.ds(r0, src.shape[2] - r0)
    fx_on_y = s["diag"] if by_peer else s["from_x"]
    fy_on_x = s["diag"] if by_peer else s["from_y"]
    legs.update(
        fwd_y=(cp(2, out.at[s["from_x"], :, c, lo], out.at[fx_on_y, :, c, lo], to_y),
               cp(2, mine.at[:, lo], out.at[s["diag"], :, c, lo], to_y)),
        fwd_x=(cp(3, out.at[s["from_y"], :, c, hi], out.at[fy_on_x, :, c, hi], to_x),
               cp(3, mine.at[:, hi], out.at[s["diag"], :, c, hi], to_x)),
        pass_d0=(cp(6, out.at[s["diag"], :, c, lo], out.at[s["diag"], :, c, lo], sibling),
                 cp(6, mine.at[:, lo], out.at[s["diag"], :, 1 - c, lo], sibling)),
        pass_d1=(cp(7, out.at[s["diag"], :, c, hi], out.at[s["diag"], :, c, hi], sibling),
                 cp(7, mine.at[:, hi], out.at[s["diag"], :, 1 - c, hi], sibling)))
    return legs, True


def _gather_steps(legs, routed):
    def start():
        legs["x"][0].start()
        legs["y"][0].start()
        if not routed:
            legs["d"][0].start()

    def middle():
        legs["x"][1].wait_recv()
        if routed:
            legs["fwd_y"][0].start()
        legs["pass_x"][0].start()
        legs["y"][1].wait_recv()
        if routed:
            legs["fwd_x"][0].start()
        legs["pass_y"][0].start()

    def finish():
        last = ["pass_d0", "pass_d1"] if routed else ["pass_d"]
        if routed:
            legs["fwd_y"][1].wait_recv()
            legs["pass_d0"][0].start()
            legs["fwd_x"][1].wait_recv()
            legs["pass_d1"][0].start()
        else:
            legs["d"][1].wait_recv()
            legs["pass_d"][0].start()
        for name in ["pass_x", "pass_y"] + last:
            legs[name][1].wait_recv()
        for name in ["x", "y", "pass_x", "pass_y"] + last + (["fwd_y", "fwd_x"] if routed else ["d"]):
            legs[name][0].wait_send()

    return start, middle, finish


def _gather_plan(arrs):
    na = len(arrs)

    def steps(ins, outs, send_sems, recv_sems):
        return [_gather_steps(*_gather_legs(ins[a], outs[a], send_sems, recv_sems, 8 * a, False)) for a in range(na)]

    def run(which):
        def hook(*refs):
            for step in steps(*refs):
                step[which]()
        return hook

    return _Hosted(arrs, [jax.ShapeDtypeStruct((N_CHIPS,) + a.shape, a.dtype) for a in arrs], 8 * na,
                   run(0), run(2), middle=run(1))


def _pair_exchange(grads, name):
    na = len(grads)

    def body(*refs):
        ins, outs = refs[:na], refs[na:2 * na]
        send_sems, recv_sems = refs[2 * na:]
        x, y, c, _, _ = _place()
        copies = [pltpu.make_async_remote_copy(
            src_ref=ins[a].at[:, 1 - c], dst_ref=outs[a], send_sem=send_sems.at[a], recv_sem=recv_sems.at[a],
            device_id=(x, y, 1 - c), device_id_type=MESH) for a in range(na)]
        for cp in copies:
            cp.start()
        for cp in copies:
            cp.wait()

    return pl.pallas_call(
        body, name=name,
        in_specs=[ANY] * na, out_specs=[ANY] * na,
        out_shape=[jax.ShapeDtypeStruct(g.shape[:1] + g.shape[2:], g.dtype) for g in grads],
        scratch_shapes=[pltpu.SemaphoreType.DMA((na,)), pltpu.SemaphoreType.DMA((na,))],
    )(*grads)


def _pair_add(grad, recv, c_arr, name):
    _, _, r, cols = grad.shape

    def body(c_ref, g_ref, r_ref, o_ref):
        o_ref[...] = (g_ref[...].astype(F32) + r_ref[...].astype(F32)).astype(BF16)

    return pl.pallas_call(
        body, name=name,
        grid_spec=pltpu.PrefetchScalarGridSpec(
            num_scalar_prefetch=1, grid=(N_CHIPS,),
            in_specs=[pl.BlockSpec((None, None, r, cols), lambda p, c_ref: (p, c_ref[0], 0, 0)),
                      pl.BlockSpec((None, r, cols), lambda p, c_ref: (p, 0, 0))],
            out_specs=pl.BlockSpec((None, r, cols), lambda p, c_ref: (p, 0, 0))),
        out_shape=jax.ShapeDtypeStruct((N_CHIPS, r, cols), BF16),
        compiler_params=_ARB1,
    )(c_arr, grad, recv)


def _chip_exchange_plan(sums, by_peer=False):
    na = len(sums)

    def copies(ins, outs, send_sems, recv_sems):
        x, y, c, me, chips = _place()

        def copy(a, j, px, py, block, slot):
            return pltpu.make_async_remote_copy(
                src_ref=ins[a].at[block], dst_ref=outs[a].at[slot],
                send_sem=send_sems.at[3 * a + j], recv_sem=recv_sems.at[3 * a + j],
                device_id=(px, py, c), device_id_type=MESH)

        peers = [(a, j, px, py) for a in range(na) for j, (px, py) in enumerate(chips)]
        return me, peers, copy

    def start(*refs):
        me, peers, copy = copies(*refs)
        for a, j, px, py in peers:
            if by_peer:
                copy(a, j, px, py, PEER_SLOT[j], PEER_SLOT[j]).start()
            else:
                copy(a, j, px, py, 2 * px + py, me).start()

    def finish(*refs):
        me, peers, copy = copies(*refs)
        for a, j, px, py in peers:
            if by_peer:
                copy(a, j, px, py, PEER_SLOT[j], PEER_SLOT[j]).wait_recv()
            else:
                copy(a, j, px, py, me, 2 * px + py).wait_recv()
        for a, j, px, py in peers:
            if by_peer:
                copy(a, j, px, py, PEER_SLOT[j], PEER_SLOT[j]).wait_send()
            else:
                copy(a, j, px, py, 2 * px + py, me).wait_send()

    return _Hosted(sums, [jax.ShapeDtypeStruct(s.shape, s.dtype) for s in sums], 3 * na, start, finish)


def _chip_sum(own, recv, me_arr, name):
    _, r, cols = recv.shape

    def body(me_ref, own_ref, r_ref, o_ref):
        o_ref[...] = jnp.zeros_like(o_ref)
        for q in range(N_CHIPS):
            @pl.when(me_ref[0] == q)
            def _():
                o_ref[...] += own_ref[...].astype(F32)

            @pl.when(me_ref[0] != q)
            def _():
                o_ref[...] += r_ref[q].astype(F32)

    return pl.pallas_call(
        body, name=name,
        grid_spec=pltpu.PrefetchScalarGridSpec(
            num_scalar_prefetch=1, grid=(1,),
            in_specs=[pl.BlockSpec((None, r, cols), lambda i, me_ref: (me_ref[0], 0, 0)),
                      pl.BlockSpec((N_CHIPS, r, cols), lambda i, me_ref: (0, 0, 0))],
            out_specs=pl.BlockSpec((r, cols), lambda i, me_ref: (0, 0))),
        out_shape=jax.ShapeDtypeStruct((r, cols), F32),
        compiler_params=_ARB1,
    )(me_arr, own, recv)


def _peer_sum(own, recv, name):
    _, r, cols = recv.shape

    def body(own_ref, r_ref, o_ref):
        acc = own_ref[...].astype(F32) + r_ref[1].astype(F32)
        acc = acc + r_ref[2].astype(F32)
        o_ref[...] = acc + r_ref[3].astype(F32)

    return pl.pallas_call(
        body, name=name, grid=(1,),
        in_specs=[pl.BlockSpec((None, r, cols), lambda i: (0, 0, 0)), pl.BlockSpec((N_CHIPS, r, cols), lambda i: (0, 0, 0))],
        out_specs=pl.BlockSpec((r, cols), lambda i: (0, 0)),
        out_shape=jax.ShapeDtypeStruct((r, cols), F32),
        compiler_params=_ARB1,
    )(own, recv)


def _pair_share(halves):
    na = len(halves)

    def body(*refs):
        ins, outs = refs[:na], refs[na:2 * na]
        send_sems, recv_sems = refs[2 * na:]
        x, y, c, _, _ = _place()
        copies = [pltpu.make_async_remote_copy(
            src_ref=ins[a], dst_ref=outs[a], send_sem=send_sems.at[a], recv_sem=recv_sems.at[a],
            device_id=(x, y, 1 - c), device_id_type=MESH) for a in range(na)]
        for cp in copies:
            cp.start()
        for cp in copies:
            cp.wait()

    return pl.pallas_call(
        body, name="pair_share",
        in_specs=[ANY] * na, out_specs=[ANY] * na,
        out_shape=[jax.ShapeDtypeStruct(h.shape, h.dtype) for h in halves],
        scratch_shapes=[pltpu.SemaphoreType.DMA((na,)), pltpu.SemaphoreType.DMA((na,))],
    )(*halves)


def _small_allreduce(block):
    m, n = block.shape

    def body(x_ref, all_ref, sum_ref, send_sems, recv_sems, local_sem):
        x, y, c, _, chips = _place()
        me, sibling = (x, y, c), (x, y, 1 - c)

        def rows(px, py, pc):
            return all_ref.at[pl.ds((4 * px + 2 * py + pc) * m, m), :]

        def copy(k, blk, to, src=None):
            return pltpu.make_async_remote_copy(
                src_ref=rows(*blk) if src is None else src, dst_ref=rows(*blk),
                send_sem=send_sems.at[k], recv_sem=recv_sems.at[k], device_id=to, device_id_type=MESH)

        mine = pltpu.make_async_copy(x_ref, rows(*me), local_sem)
        mine.start()
        first = [copy(0, me, sibling, src=x_ref)]
        first += [copy(1 + j, me, (*chip, c), src=x_ref) for j, chip in enumerate(chips)]
        for cp in first:
            cp.start()
        passed = [copy(4 + j, (*chip, c), sibling) for j, chip in enumerate(chips)]
        for j, chip in enumerate(chips):
            copy(1 + j, (*chip, c), me).wait_recv()
            passed[j].start()
        copy(0, sibling, me).wait_recv()
        for j, chip in enumerate(chips):
            copy(4 + j, (*chip, 1 - c), me).wait_recv()
        for cp in first + passed:
            cp.wait_send()
        mine.wait()
        acc = all_ref[0:m, :]
        for d in range(1, 8):
            acc = acc + all_ref[d * m:(d + 1) * m, :]
        sum_ref[...] = acc

    vmem = pl.BlockSpec(memory_space=pltpu.VMEM)
    return pl.pallas_call(
        body, name="small_allreduce",
        in_specs=[vmem], out_specs=[vmem, vmem],
        out_shape=[jax.ShapeDtypeStruct((8 * m, n), F32), jax.ShapeDtypeStruct((m, n), F32)],
        scratch_shapes=[pltpu.SemaphoreType.DMA((7,)), pltpu.SemaphoreType.DMA((7,)), pltpu.SemaphoreType.DMA],
    )(block)[1]


def _row_tile(rows):
    best = rows
    for cand in range(8, min(rows, 512) + 1, 8):
        if rows % cand == 0:
            best = cand
    return best


def _adamw_math(w, g, m, v):
    m2 = ADAM_B1 * m + (1.0 - ADAM_B1) * g
    v2 = ADAM_B2 * v + (1.0 - ADAM_B2) * (g * g)
    m_hat = m2 / (1.0 - ADAM_B1 ** ADAM_STEP)
    v_hat = v2 / (1.0 - ADAM_B2 ** ADAM_STEP)
    return -ADAM_LR * (m_hat / (jnp.sqrt(v_hat) + ADAM_EPS) + ADAM_WD * w), m2, v2


def _adamw_halves(w, g_mine, g_other, m, v, c_arr, name):
    rows, cols = w.shape
    r = rows // 2
    tr = _row_tile(r)
    nt = r // tr

    def body(c_ref, w_ref, gm_ref, go_ref, m_ref, v_ref, g_ref, d_ref, nm_ref, nv_ref):
        gv = jnp.where(pl.program_id(0) == c_ref[0], gm_ref[...], go_ref[...])
        g_ref[...] = gv
        d_ref[...], nm_ref[...], nv_ref[...] = _adamw_math(w_ref[...], gv, m_ref[...], v_ref[...])

    full = pl.BlockSpec((tr, cols), lambda h, i, c_ref: (h * nt + i, 0))
    half = pl.BlockSpec((tr, cols), lambda h, i, c_ref: (i, 0))
    shape = jax.ShapeDtypeStruct((rows, cols), F32)
    return pl.pallas_call(
        body, name=name,
        grid_spec=pltpu.PrefetchScalarGridSpec(
            num_scalar_prefetch=1, grid=(2, nt),
            in_specs=[full, half, half, full, full], out_specs=[full] * 4),
        out_shape=[shape] * 4,
        compiler_params=_ARB2,
    )(c_arr, w, g_mine, g_other, m, v)


def _adamw(w, g, m, v, name):
    rows, cols = w.shape
    tr = _row_tile(rows)

    def body(w_ref, g_ref, m_ref, v_ref, d_ref, nm_ref, nv_ref):
        d_ref[...], nm_ref[...], nv_ref[...] = _adamw_math(w_ref[...], g_ref[...], m_ref[...], v_ref[...])

    spec = pl.BlockSpec((tr, cols), lambda i: (i, 0))
    shape = jax.ShapeDtypeStruct((rows, cols), F32)
    return pl.pallas_call(
        body, name=name, grid=(rows // tr,),
        in_specs=[spec] * 4, out_specs=[spec] * 3, out_shape=[shape] * 3,
        compiler_params=_ARB1,
    )(w, g, m, v)


def _pad_w_in_t(w_in_t):
    def heads_padded(sec):
        return jnp.pad(sec.reshape(HEADS, 64, -1), ((0, 0), (0, LANES - 64), (0, 0))).reshape(HEADS * LANES, -1)

    w = w_in_t
    return jnp.concatenate([
        heads_padded(w[0:256]), heads_padded(w[256:512]), w[512:1536],
        heads_padded(w[1536:1792]), heads_padded(w[1792:2048]), w[2048:3072],
        jnp.pad(w[3072:3088], ((0, LANES - GATE_RANK), (0, 0)))], axis=0)


def _unpad_w_in_t(w_pt):
    def heads_unpadded(sec):
        return sec.reshape(HEADS, LANES, -1)[:, 0:64].reshape(HEADS * 64, -1)

    p = w_pt
    return jnp.concatenate([
        heads_unpadded(p[0:512]), heads_unpadded(p[512:1024]), p[1024:2048],
        heads_unpadded(p[2048:2560]), heads_unpadded(p[2560:3072]), p[3072:4096],
        p[4096:4096 + GATE_RANK]], axis=0)


def _rope_tables(t):
    half = 32
    inv = ROPE_BASE ** (-jnp.arange(half, dtype=F32) * 2.0 / 64)
    ang = jnp.arange(t, dtype=F32)[:, None] * inv[None, :]
    cos, sin = jnp.cos(ang), jnp.sin(ang)
    z32, z64 = jnp.zeros((t, 32), F32), jnp.zeros((t, 64), F32)
    return (jnp.concatenate([cos, cos, z64], axis=1),
            jnp.concatenate([-sin, z32, z64], axis=1),
            jnp.concatenate([z32, sin, z64], axis=1))


def _halves(w):
    n, rows, cols = w.shape
    return w.reshape(n, 2, rows // 2, cols)


_VMEM = pl.BlockSpec(memory_space=pltpu.VMEM)


def _pack_small(n1, nm, n2, nf, nret, ngla, ba, wa2_p, loss_blk):
    def body(n1_ref, nm_ref, n2_ref, nf_ref, nret_ref, ngla_ref, ba_ref, wa2_ref, loss_ref, o_ref):
        o_ref[...] = jnp.zeros_like(o_ref)
        o_ref[0:1, :] = n1_ref[...]
        o_ref[1:2, :] = nm_ref[...]
        o_ref[2:3, :] = n2_ref[...]
        o_ref[3:4, :] = nf_ref[...]
        o_ref[4:5, 0:512] = nret_ref[...]
        o_ref[4:5, 512:1024] = ngla_ref[...]
        o_ref[5:6, 0:256] = ba_ref[...]
        o_ref[6:7, 0:LANES] = loss_ref[0:1, :]
        o_ref[8:8 + GATE_RANK, 0:HEADS * LANES] = wa2_ref[0:GATE_RANK, :]

    return pl.pallas_call(
        body, name="pack_small", in_specs=[_VMEM] * 9, out_specs=_VMEM,
        out_shape=jax.ShapeDtypeStruct((SMALL_ROWS, D_MODEL), F32),
    )(n1, nm, n2, nf, nret, ngla, ba, wa2_p, loss_blk)


def _small_update(summed, chip_arr, ws, ms, vs):
    n = len(ws)

    def body(chip_ref, s_ref, *refs):
        w_refs, m_refs, v_refs = refs[0:n], refs[n:2 * n], refs[2 * n:3 * n]
        outs = refs[3 * n:]
        wa2_all = s_ref[8:8 + GATE_RANK, 0:HEADS * LANES]
        wa2_g = jnp.zeros((GATE_RANK, 64), F32)
        for p in range(N_CHIPS):
            wa2_g = jnp.where(chip_ref[0] == p, wa2_all[:, LANES * p:LANES * p + 64], wa2_g)
        grads = [s_ref[0:1, :], s_ref[1:2, :], s_ref[2:3, :], s_ref[3:4, :], s_ref[4:5, 0:512],
                 s_ref[4:5, 512:1024], s_ref[5:6, 0:256], wa2_g]
        for k in range(n):
            d, m2, v2 = _adamw_math(w_refs[k][...], grads[k], m_refs[k][...], v_refs[k][...])
            outs[k][...] = grads[k]
            outs[n + k][...] = d
            outs[2 * n + k][...] = m2
            outs[3 * n + k][...] = v2

    shapes = [jax.ShapeDtypeStruct(w.shape, F32) for w in ws] * 4
    smem = pl.BlockSpec(memory_space=pltpu.SMEM)
    outs = pl.pallas_call(
        body, name="small_update", in_specs=[smem] + [_VMEM] * (1 + 3 * n), out_specs=[_VMEM] * (4 * n),
        out_shape=shapes,
    )(chip_arr, summed, *ws, *ms, *vs)
    return outs[0:n], outs[n:2 * n], outs[2 * n:3 * n], outs[3 * n:4 * n]


def _pad_in_rows(w_t):
    return jnp.pad(w_t, ((0, IN_ROWS - IN_SHARD), (0, 0)))


def _forward_backward(xs, target, ffn1_w, rest, ba_p, ffn1_norm_g, mix_norm_g, ret_norm_g, gla_norm_g, ffn2_norm_g,
                      final_norm_g, ffn1_gather=None, rest_plan=None, rest_weights=None, ffn2_plans=None,
                      ffn2_weights=None, early=None, late=None):
    t = xs.shape[0]
    cos_t, sa_t, sb_t = _rope_tables(t)
    log_gamma = jnp.log(1.0 - 2.0 ** (-5.0 - jnp.arange(HEADS, dtype=F32)))
    lg_t = jnp.broadcast_to(log_gamma[:, None, None], (HEADS, 1, LANES))
    ret_aux = [cos_t, sa_t, sb_t, lg_t]

    if ffn1_gather is None:
        (x1, a1, u1, h1), gathered = _ffn_fwd(xs, ffn1_norm_g, ffn1_w, "ffn1_fwd", hosted=rest_plan)
    else:
        ffn1_shard, ffn1_weights = ffn1_gather
        (x1, a1, u1, h1, wall), gathered = _ffn1_fwd_gathering(xs, ffn1_norm_g, ffn1_shard, "ffn1_fwd",
                                                               hosted=rest_plan)
        ffn1_w = ffn1_weights(wall)
    ffn2_w, w_in_pt, w_out_full, wa2_p = rest if rest_plan is None else rest_weights(gathered)
    plans = [None] * 3 if ffn2_plans is None else ffn2_plans
    (proj, h_mix), got_gate = _mixer_in_fwd(x1, mix_norm_g, w_in_pt, "mixer_in_fwd", hosted=plans[0])
    gla_aux = [proj, wa2_p, ba_p]
    (o_ret, raw_ret, st_ret), got_up = _attn_fwd(True, proj, ret_aux, ret_norm_g, "ret_fwd", hosted=plans[1])
    (o_gla, raw_gla, st_gla), got_down = _attn_fwd(False, proj, gla_aux, gla_norm_g, "gla_fwd", hosted=plans[2])
    if ffn2_plans is not None:
        ffn2_w = ffn2_weights(got_gate + got_up + got_down)
    x2 = _mixer_out_fwd(o_ret, o_gla, w_out_full, x1, "mixer_out_fwd")
    (x3, a2, u2, h2), _ = _ffn_fwd(x2, ffn2_norm_g, ffn2_w, "ffn2_fwd")
    loss_blk, dx3, d_final_g = _final_loss(x3, final_norm_g, target, "final_loss")

    (da2, du2, hid2, dob2, dx2, d_ffn2_g), _ = _ffn_bwd(dx3, x2, ffn2_norm_g, a2, u2, ffn2_w, "ffn2_bwd")
    g_gate2 = _matmul_tn(da2, h2, "ffn2_dgate", out_dtype=BF16)
    g_up2 = _matmul_tn(du2, h2, "ffn2_dup", out_dtype=BF16)
    g_down2 = _matmul_tn(hid2, dob2, "ffn2_ddown", out_dtype=BF16)

    d_o = _matmul_nt(dx2, w_out_full, "mixer_out_bwd")
    g_wout_ret = _matmul_tn(o_ret, dx2, "wout_grad_ret", out_dtype=BF16)
    g_wout_gla = _matmul_tn(o_gla, dx2, "wout_grad_gla", out_dtype=BF16)
    *dproj_ret, d_ret_g = _attn_bwd(True, proj, ret_aux, ret_norm_g, raw_ret, st_ret, d_o, "ret_bwd")
    *dproj_gla, d_gla_g, dlogit, d_ba_p = _attn_bwd(False, proj, gla_aux, gla_norm_g, raw_gla, st_gla, d_o, "gla_bwd")
    d_glow = _matmul_nt(dlogit, wa2_p, "gate_low_bwd", out_dtype=BF16)
    g_wa2_p = _matmul_tn(proj[:, PROJ_P - LANES:], dlogit, "gate_w_grad")
    dproj = jnp.concatenate(dproj_ret + dproj_gla + [d_glow], axis=1)
    g_win_p = _matmul_tn(dproj, h_mix, "w_in_grad", tka=PROJ_P // 3, out_dtype=BF16)
    dx1, d_mix_g = _mixer_in_bwd(dproj, w_in_pt, dx2, x1, mix_norm_g, "mixer_in_bwd")
    g_win_t = _unpad_w_in_t(g_win_p[0])
    g_win = jnp.stack([_pad_in_rows(g_win_t[IN_SHARD * p:IN_SHARD * (p + 1)]) for p in range(N_CHIPS)], axis=0)
    g_wout = jnp.concatenate([g_wout_ret[0], g_wout_gla[0]], axis=0).reshape(N_CHIPS, D_MODEL // N_CHIPS, D_MODEL)

    early_plan = None if early is None else early([g_gate2, g_up2, g_down2, g_win, g_wout])
    (da1, du1, hid1, dob1, grad_x, d_ffn1_g), arrived = _ffn_bwd(dx1, xs, ffn1_norm_g, a1, u1, ffn1_w, "ffn1_bwd",
                                                                hosted=early_plan)
    late_grads, late_arrived = [], []
    for lhs, rhs, name in ((da1, h1, "ffn1_dgate"), (du1, h1, "ffn1_dup"), (hid1, dob1, "ffn1_ddown")):
        plan = None if late is None or not late_grads else late(late_grads[-1], len(late_grads))
        res = _matmul_tn(lhs, rhs, name, out_dtype=BF16, hosted=plan)
        if plan is not None:
            res, carried = res
            late_arrived += carried
        late_grads.append(res)
    g_gate1, g_up1, g_down1 = late_grads

    return (loss_blk, grad_x, g_gate1, g_up1, g_down1, g_gate2, g_up2, g_down2, g_win, g_wout, g_wa2_p,
            d_ba_p, d_ffn1_g, d_mix_g, d_ffn2_g, d_final_g, d_ret_g, d_gla_g, arrived, late_arrived)


def kernel(x, ffn1_norm_g, ffn1_w_gate, ffn1_w_up, ffn1_w_down, mix_norm_g, w_in, ret_norm_g, gla_w_a2, gla_b_a, gla_norm_g, w_out, ffn2_norm_g, ffn2_w_gate, ffn2_w_up, ffn2_w_down, final_norm_g, loss_target, m_ffn1_norm_g, m_ffn1_w_gate, m_ffn1_w_up, m_ffn1_w_down, m_mix_norm_g, m_w_in, m_ret_norm_g, m_gla_w_a2, m_gla_b_a, m_gla_norm_g, m_w_out, m_ffn2_norm_g, m_ffn2_w_gate, m_ffn2_w_up, m_ffn2_w_down, m_final_norm_g, v_ffn1_norm_g, v_ffn1_w_gate, v_ffn1_w_up, v_ffn1_w_down, v_mix_norm_g, v_w_in, v_ret_norm_g, v_gla_w_a2, v_gla_b_a, v_gla_norm_g, v_w_out, v_ffn2_norm_g, v_ffn2_w_gate, v_ffn2_w_up, v_ffn2_w_down, v_final_norm_g):
    t = x.shape[1]
    xs = x.reshape(t, D_MODEL)
    target = loss_target.reshape(t, D_MODEL)
    chip = 2 * lax.axis_index("x") + lax.axis_index("y")
    c_arr = lax.axis_index("c").astype(jnp.int32).reshape(1)

    me_arr = chip.astype(jnp.int32).reshape(1)

    pad_rows = _pad_in_rows

    def own_block(gathered, shard):
        return lax.dynamic_update_slice(gathered, shard[None], (chip,) + (0,) * shard.ndim)

    ffn1_shard = _halves(jnp.stack([ffn1_w_gate[0].T, ffn1_w_up[0].T, ffn1_w_down[0]], axis=0).astype(BF16))
    rest_shards = [_halves(pad_rows(w_in[0].T).astype(BF16)[None]),
                   _halves(w_out.astype(BF16)),
                   jnp.concatenate([gla_w_a2.reshape(GATE_RANK, 64), jnp.zeros((GATE_RANK, 64), F32)],
                                   axis=1).reshape(1, 2, 8, LANES)]
    ffn2_shards = [_halves(w.astype(BF16)[None]) for w in (ffn2_w_gate[0].T, ffn2_w_up[0].T, ffn2_w_down[0])]
    def ffn1_weights(gathered):
        return lax.dynamic_update_slice(gathered, ffn1_shard[None], (0,) * 5).reshape(N_CHIPS, 3, FF_SHARD, D_MODEL)

    def rest_weights(gathered):
        win_all, wout_all, wa2_all = [own_block(g, s) for g, s in zip(gathered, rest_shards)]
        win_t = win_all.reshape(N_CHIPS, IN_ROWS, D_MODEL)
        w_in_pt = _pad_w_in_t(jnp.concatenate([win_t[p, 0:IN_SHARD] for p in range(N_CHIPS)], axis=0))
        wa2_p = jnp.pad(
            wa2_all.reshape(N_CHIPS, GATE_RANK, LANES).transpose(1, 0, 2).reshape(GATE_RANK, HEADS * LANES),
            ((0, LANES - GATE_RANK), (0, 0))).astype(BF16)
        return (None, w_in_pt, wout_all.reshape(D_MODEL, D_MODEL), wa2_p)

    def ffn2_weights(gathered):
        return [own_block(g, s).reshape(N_CHIPS, FF_SHARD, D_MODEL) for g, s in zip(gathered, ffn2_shards)]

    def pair_sums(grads, tag):
        halves = [g.reshape(g.shape[0], 2, g.shape[1] // 2, g.shape[2]) for g in grads]
        recv = _pair_exchange(halves, "pair_exchange_" + tag)
        return [_pair_add(g, r, c_arr, "pair_add_%s%d" % (tag, k)) for k, (g, r) in enumerate(zip(halves, recv))]

    early_sums = []

    def early(grads):
        early_sums.extend(pair_sums(grads, "early"))
        return _chip_exchange_plan(early_sums)

    late_sums = []

    def late(grad, number):
        late_sums.extend(pair_sums([grad], "late%d" % number))
        return _chip_exchange_plan(late_sums[-1:], by_peer=True)

    ba_p = jnp.pad(gla_b_a.reshape(HEADS, 64), ((0, 0), (0, 64))).reshape(1, HEADS * LANES)
    fb = _forward_backward(xs, target, None, None, ba_p, ffn1_norm_g, mix_norm_g, ret_norm_g, gla_norm_g,
                           ffn2_norm_g, final_norm_g.reshape(1, D_MODEL), ffn1_gather=(ffn1_shard, ffn1_weights),
                           rest_plan=_gather_plan(rest_shards), rest_weights=rest_weights,
                           ffn2_plans=[_gather_plan([s]) for s in ffn2_shards], ffn2_weights=ffn2_weights,
                           early=early, late=late)
    (loss_blk, grad_x, _, _, g_down1, _, _, _, _, _, g_wa2_p,
     d_ba_p, d_ffn1_g, d_mix_g, d_ffn2_g, d_final_g, d_ret_g, d_gla_g, early_arrived, late_arrived) = fb
    late_arrived = late_arrived + _run_hosted(late(g_down1, 3), "chip_exchange_late")
    mine = [_peer_sum(s, r, "chip_sum_%d" % k) for k, (s, r) in enumerate(zip(late_sums, late_arrived))]
    mine += [_chip_sum(s, r, me_arr, "chip_sum_%d" % (3 + k)) for k, (s, r) in enumerate(zip(early_sums, early_arrived))]
    other = _pair_share(mine)

    d_ba = d_ba_p.reshape(HEADS, LANES)[:, 0:64].reshape(1, 256)
    small_local = _pack_small(d_ffn1_g, d_mix_g, d_ffn2_g, d_final_g, d_ret_g, d_gla_g, d_ba, g_wa2_p[0], loss_blk)
    small_sum = _small_allreduce(small_local)
    loss = small_sum[6, 0]

    def rows(n1, nm, n2, nf, nret, ngla, ba, wa2):
        return [n1, nm, n2, nf.reshape(1, D_MODEL), nret, ngla, ba, wa2.reshape(GATE_RANK, 64)]

    small = _small_update(
        small_sum, me_arr,
        rows(ffn1_norm_g, mix_norm_g, ffn2_norm_g, final_norm_g, ret_norm_g, gla_norm_g, gla_b_a, gla_w_a2),
        rows(m_ffn1_norm_g, m_mix_norm_g, m_ffn2_norm_g, m_final_norm_g, m_ret_norm_g, m_gla_norm_g, m_gla_b_a,
             m_gla_w_a2),
        rows(v_ffn1_norm_g, v_mix_norm_g, v_ffn2_norm_g, v_final_norm_g, v_ret_norm_g, v_gla_norm_g, v_gla_b_a,
             v_gla_w_a2))
    s_grad, s_delta, s_m, s_v = [
        [*o[0:3], o[3].reshape(D_MODEL), *o[4:7], o[7].reshape(1, GATE_RANK, 64)] for o in small]

    def big(k, w, m, v, name, to_2d, from_2d):
        outs4 = _adamw_halves(to_2d(w), mine[k], other[k], to_2d(m), to_2d(v), c_arr, name)
        return [from_2d(z) for z in outs4]

    plain = (lambda w: w[0], lambda z: z[None])
    transposed = (lambda w: w[0].T, lambda z: z.T[None])
    in_proj = (lambda w: pad_rows(w[0].T), lambda z: z[0:IN_SHARD].T[None])
    r_g1 = big(0, ffn1_w_gate, m_ffn1_w_gate, v_ffn1_w_gate, "adamw_ffn1_gate", *transposed)
    r_u1 = big(1, ffn1_w_up, m_ffn1_w_up, v_ffn1_w_up, "adamw_ffn1_up", *transposed)
    r_d1 = big(2, ffn1_w_down, m_ffn1_w_down, v_ffn1_w_down, "adamw_ffn1_down", *plain)
    r_g2 = big(3, ffn2_w_gate, m_ffn2_w_gate, v_ffn2_w_gate, "adamw_ffn2_gate", *transposed)
    r_u2 = big(4, ffn2_w_up, m_ffn2_w_up, v_ffn2_w_up, "adamw_ffn2_up", *transposed)
    r_d2 = big(5, ffn2_w_down, m_ffn2_w_down, v_ffn2_w_down, "adamw_ffn2_down", *plain)
    r_in = big(6, w_in, m_w_in, v_w_in, "adamw_w_in", *in_proj)
    r_out = big(7, w_out, m_w_out, v_w_out, "adamw_w_out", *plain)

    def leaves(k, smalls):
        n1, nm, n2, nf, nret, ngla, ba, wa2 = smalls
        return [n1, r_g1[k], r_u1[k], r_d1[k], nm, r_in[k], nret, wa2, ba, ngla, r_out[k], n2, r_g2[k], r_u2[k], r_d2[k], nf]

    outs = [loss, grad_x.reshape(x.shape)]
    outs += leaves(0, s_grad) + leaves(1, s_delta) + leaves(2, s_m) + leaves(3, s_v)
    return tuple(outs)
```

```python
import functools

import jax
import jax.numpy as jnp
from jax import lax
from jax.experimental import pallas as pl
from jax.experimental.pallas import tpu as pltpu

F32, BF16 = jnp.float32, jnp.bfloat16
MESH = pl.DeviceIdType.MESH
ANY = pl.BlockSpec(memory_space=pl.ANY)

D_MODEL = 1024
D_FF = 2816
N_CHIPS = 4
FF_SHARD = D_FF // N_CHIPS
IN_WIDTH = 3088
IN_SHARD = IN_WIDTH // N_CHIPS
IN_ROWS = 800
CHUNK = 64
HEADS = 4
LANES = 128
PROJ_P = 3072 + LANES
PROJ_TILES = 5
GATE_RANK = 16
QK_SCALE = 0.125
GATE_NORM = 16.0
RMS_EPS = 1e-6
ROPE_BASE = 10000.0
ADAM_LR, ADAM_B1, ADAM_B2, ADAM_EPS, ADAM_WD, ADAM_STEP = 0.001, 0.9, 0.999, 1e-08, 0.01, 10
SMALL_ROWS = 32
TOKEN_TILE = 512
ATTN_TILE = 512

_ARB2 = pltpu.CompilerParams(dimension_semantics=("arbitrary", "arbitrary"))
_ARB1 = pltpu.CompilerParams(dimension_semantics=("arbitrary",))
_ARB3 = pltpu.CompilerParams(dimension_semantics=("arbitrary", "arbitrary", "arbitrary"))


def _dot(a, b):
    return jnp.dot(a, b, preferred_element_type=F32)


def _dot_nt(a, b):
    return lax.dot_general(a, b, (((1,), (1,)), ((), ())), preferred_element_type=F32)


def _dot_tn(a, b):
    return lax.dot_general(a, b, (((0,), (0,)), ((), ())), preferred_element_type=F32)


def _rms_scale(xv):
    return lax.rsqrt(jnp.mean(xv * xv, axis=-1, keepdims=True) + RMS_EPS)


def _rms_bwd(dh, xv, g):
    r = _rms_scale(xv)
    xhat = xv * r
    dxhat = dh * g
    dx = r * (dxhat - xhat * jnp.mean(dxhat * xhat, axis=-1, keepdims=True))
    return dx, jnp.sum(dh * xhat, axis=0, keepdims=True)


def _silu_grad(a, sg):
    return sg * (1.0 + a * (1.0 - sg))


class _Hosted:
    def __init__(self, arrays, out_shapes, n_sems, start, finish, middle=None):
        self.arrays, self.out_shapes, self.n_sems = list(arrays), list(out_shapes), n_sems
        self.start, self.finish = start, finish
        self.middle = middle if middle is not None else (lambda *refs: None)


def _call(body, args, *, name, grid, in_specs, out_specs, out_shape, scratch_shapes, compiler_params, hosted=None):
    if hosted is None:
        outs = pl.pallas_call(body, name=name, grid=grid, in_specs=in_specs, out_specs=out_specs, out_shape=out_shape,
                              scratch_shapes=scratch_shapes, compiler_params=compiler_params)(*args)
        return list(outs), []
    n_in, n_out, n_sc, nh = len(in_specs), len(out_specs), len(scratch_shapes), len(hosted.arrays)

    def wrapped(*refs):
        ins, h_in = refs[:n_in], refs[n_in:n_in + nh]
        outs, h_out = refs[n_in + nh:n_in + nh + n_out], refs[n_in + nh + n_out:n_in + 2 * nh + n_out]
        rest = refs[n_in + 2 * nh + n_out:]
        scratch, (send_sems, recv_sems) = rest[:n_sc], rest[n_sc:]
        step = functools.reduce(lambda flat, d: flat * grid[d] + pl.program_id(d), range(len(grid)), 0)
        total = functools.reduce(lambda a, b: a * b, grid)

        @pl.when(step == 0)
        def _():
            hosted.start(h_in, h_out, send_sems, recv_sems)

        @pl.when(step == total // 2)
        def _():
            hosted.middle(h_in, h_out, send_sems, recv_sems)

        body(*ins, *outs, *scratch)
        last = step == total - 1

        @pl.when(last)
        def _():
            hosted.finish(h_in, h_out, send_sems, recv_sems)

    sems = [pltpu.SemaphoreType.DMA((hosted.n_sems,)), pltpu.SemaphoreType.DMA((hosted.n_sems,))]
    outs = pl.pallas_call(
        wrapped, name=name, grid=grid, in_specs=list(in_specs) + [ANY] * nh, out_specs=list(out_specs) + [ANY] * nh,
        out_shape=list(out_shape) + hosted.out_shapes, scratch_shapes=list(scratch_shapes) + sems,
        compiler_params=compiler_params)(*args, *hosted.arrays)
    return list(outs[:n_out]), list(outs[n_out:])


def _run_hosted(hosted, name):
    nh = len(hosted.arrays)

    def body(*refs):
        h_in, h_out, (send_sems, recv_sems) = refs[:nh], refs[nh:2 * nh], refs[2 * nh:]
        hosted.start(h_in, h_out, send_sems, recv_sems)
        hosted.middle(h_in, h_out, send_sems, recv_sems)
        hosted.finish(h_in, h_out, send_sems, recv_sems)

    sems = [pltpu.SemaphoreType.DMA((hosted.n_sems,)), pltpu.SemaphoreType.DMA((hosted.n_sems,))]
    return list(pl.pallas_call(body, name=name, in_specs=[ANY] * nh, out_specs=[ANY] * nh,
                               out_shape=hosted.out_shapes, scratch_shapes=sems)(*hosted.arrays))


def _ffn_weight_operands(ffn_w, chunk_maps):
    if isinstance(ffn_w, (list, tuple)):
        specs = [pl.BlockSpec((None, FF_SHARD, D_MODEL), lambda *g, m=m: (m(*g), 0, 0)) for m in chunk_maps]
        return list(ffn_w), specs
    specs = [pl.BlockSpec((None, None, FF_SHARD, D_MODEL), lambda *g, m=m, k=kind: (m(*g), k, 0, 0))
             for kind, m in enumerate(chunk_maps)]
    return [ffn_w] * 3, specs


def _pipeline_items(steps):
    def cur(s):
        c = jnp.minimum(s, steps - 1)
        return c // N_CHIPS, c % N_CHIPS

    def prev(s):
        p = jnp.maximum(s - 1, 0)
        return p // N_CHIPS, p % N_CHIPS

    return cur, prev


def _ffn_fwd(x, g, ffn_w, name, hosted=None):
    t = x.shape[0]
    tm = min(t, TOKEN_TILE)

    def body(x_ref, g_ref, wg_ref, wu_ref, wd_ref, xo_ref, a_ref, u_ref, h_ref, acc_ref):
        j = pl.program_id(1)

        @pl.when(j == 0)
        def _():
            xv = x_ref[...]
            h_ref[...] = ((xv * _rms_scale(xv)) * g_ref[...]).astype(BF16)
            acc_ref[...] = jnp.zeros_like(acc_ref)

        h = h_ref[...]
        a = _dot_nt(h, wg_ref[...])
        u = _dot_nt(h, wu_ref[...])
        a_ref[...] = a.astype(BF16)
        u_ref[...] = u.astype(BF16)
        hid = (a * jax.nn.sigmoid(a)) * u
        acc_ref[...] += _dot(hid.astype(BF16), wd_ref[...])

        @pl.when(j == N_CHIPS - 1)
        def _():
            xo_ref[...] = x_ref[...] + 0.5 * acc_ref[...]

    tok = pl.BlockSpec((tm, D_MODEL), lambda i, j: (i, 0))
    act = pl.BlockSpec((None, tm, FF_SHARD), lambda i, j: (j, i, 0))
    w_arrays, weights = _ffn_weight_operands(ffn_w, [lambda i, j: j] * 3)
    return _call(
        body, (x, g, *w_arrays), name=name, grid=(t // tm, N_CHIPS),
        in_specs=[tok, pl.BlockSpec((1, D_MODEL), lambda i, j: (0, 0))] + weights,
        out_specs=[tok, act, act, tok],
        out_shape=[jax.ShapeDtypeStruct((t, D_MODEL), F32),
                   jax.ShapeDtypeStruct((N_CHIPS, t, FF_SHARD), BF16),
                   jax.ShapeDtypeStruct((N_CHIPS, t, FF_SHARD), BF16),
                   jax.ShapeDtypeStruct((t, D_MODEL), BF16)],
        scratch_shapes=[pltpu.VMEM((tm, D_MODEL), F32)],
        compiler_params=_ARB2, hosted=hosted)


def _ffn1_fwd_gathering(x, g, shard, name, hosted=None):
    t = x.shape[0]
    tm = min(t, TOKEN_TILE)
    nt = t // tm
    nh = 0 if hosted is None else len(hosted.arrays)

    def body(*refs):
        x_ref, g_ref, shard_ref = refs[0:3]
        h_in = refs[3:3 + nh]
        xo_ref, a_ref, u_ref, h_ref, wall = refs[3 + nh:8 + nh]
        h_out = refs[8 + nh:8 + 2 * nh]
        acc, h_all, wbuf, load_sem, send_sems, recv_sems = refs[8 + 2 * nh:14 + 2 * nh]
        carried_sems = refs[14 + 2 * nh:]
        k, i = pl.program_id(0), pl.program_id(1)
        legs, _ = _gather_legs(shard_ref, wall, send_sems, recv_sems, 0, True)
        begin, pass_on, _ = _gather_steps(legs, True)

        def load_chunk(src):
            load = pltpu.make_async_copy(src, wbuf, load_sem)
            load.start()
            load.wait()

        @pl.when((k == 0) & (i == 0))
        def _():
            begin()
            load_chunk(shard_ref)
            if hosted is not None:
                hosted.start(h_in, h_out, *carried_sems)

        @pl.when((k == PEER_SLOT[1]) & (i == 0))
        def _():
            pass_on()
            legs["pass_y"][1].wait_recv()
            load_chunk(wall.at[PEER_SLOT[1]])

        @pl.when((k == PEER_SLOT[0]) & (i == 0))
        def _():
            legs["pass_x"][1].wait_recv()
            load_chunk(wall.at[PEER_SLOT[0]])
            if hosted is not None:
                hosted.middle(h_in, h_out, *carried_sems)

        @pl.when((k == PEER_SLOT[2]) & (i == 0))
        def _():
            legs["fwd_y"][1].wait_recv()
            legs["pass_d0"][0].start()
            legs["fwd_x"][1].wait_recv()
            legs["pass_d1"][0].start()
            legs["pass_d0"][1].wait_recv()
            legs["pass_d1"][1].wait_recv()
            load_chunk(wall.at[PEER_SLOT[2]])

        @pl.when(k == 0)
        def _():
            xv = x_ref[...]
            h0 = ((xv * _rms_scale(xv)) * g_ref[...]).astype(BF16)
            h_all[i] = h0
            h_ref[...] = h0

        h = h_all[i]
        wg, wu, wd = (wbuf[kind].reshape(FF_SHARD, D_MODEL) for kind in range(3))
        a = _dot_nt(h, wg)
        u = _dot_nt(h, wu)
        a_ref[...] = a.astype(BF16)
        u_ref[...] = u.astype(BF16)
        part = _dot(((a * jax.nn.sigmoid(a)) * u).astype(BF16), wd)

        @pl.when(k == 0)
        def _():
            acc[i] = part

        @pl.when(k > 0)
        def _():
            acc[i] += part

        @pl.when(k == N_CHIPS - 1)
        def _():
            xo_ref[...] = x_ref[...] + 0.5 * acc[i]

        @pl.when((k == N_CHIPS - 1) & (i == nt - 1))
        def _():
            for pair in legs.values():
                pair[0].wait_send()
            if hosted is not None:
                hosted.finish(h_in, h_out, *carried_sems)

    def first_or_last(k):
        return (k == 0) | (k == N_CHIPS - 1)

    tok = lambda keep: pl.BlockSpec((tm, D_MODEL), lambda k, i: (jnp.where(keep(k), i, 0), 0))
    act = pl.BlockSpec((None, tm, FF_SHARD), lambda k, i: (k, i, 0))
    act_shape = jax.ShapeDtypeStruct((N_CHIPS, t, FF_SHARD), BF16)
    carried = [] if hosted is None else [pltpu.SemaphoreType.DMA((hosted.n_sems,))] * 2
    outs = pl.pallas_call(
        body, name=name, grid=(N_CHIPS, nt),
        in_specs=[tok(first_or_last), pl.BlockSpec((1, D_MODEL), lambda k, i: (0, 0)), ANY] + [ANY] * nh,
        out_specs=[tok(lambda k: k == N_CHIPS - 1), act, act,
                   pl.BlockSpec((tm, D_MODEL), lambda k, i: (jnp.where(k == 0, i, nt - 1), 0)), ANY] + [ANY] * nh,
        out_shape=[jax.ShapeDtypeStruct((t, D_MODEL), F32), act_shape, act_shape,
                   jax.ShapeDtypeStruct((t, D_MODEL), BF16),
                   jax.ShapeDtypeStruct((N_CHIPS,) + shard.shape, shard.dtype)]
                  + ([] if hosted is None else hosted.out_shapes),
        scratch_shapes=[pltpu.VMEM((nt, tm, D_MODEL), F32), pltpu.VMEM((nt, tm, D_MODEL), BF16),
                        pltpu.VMEM(shard.shape, shard.dtype), pltpu.SemaphoreType.DMA,
                        pltpu.SemaphoreType.DMA((8,)), pltpu.SemaphoreType.DMA((8,))] + carried,
        compiler_params=_ARB2,
    )(x, g, shard, *([] if hosted is None else hosted.arrays))
    return list(outs[:5]), list(outs[5:])


def _ffn_bwd(dxo, x, g, a4, u4, ffn_w, name, hosted=None):
    t = x.shape[0]
    tm = min(t, TOKEN_TILE)
    steps = (t // tm) * N_CHIPS
    cur, prev = _pipeline_items(steps)


    def body(dxo_ref, dxo_prev_ref, x_ref, g_ref, a_ref, u_ref, wg_ref, wu_ref, wd_ref,
             da_ref, du_ref, hid_ref, dob_ref, dx_ref, dg_ref, acc_ref, da_slots, du_slots):
        s = pl.program_id(0)
        jc, jp = cur(s)[1], prev(s)[1]
        slot = s % 2

        @pl.when(s == 0)
        def _():
            dg_ref[...] = jnp.zeros_like(dg_ref)
            acc_ref[...] = jnp.zeros_like(acc_ref)
            da_slots[...] = jnp.zeros_like(da_slots)
            du_slots[...] = jnp.zeros_like(du_slots)

        @pl.when(jc == 0)
        def _():
            dob_ref[...] = (0.5 * dxo_ref[...]).astype(BF16)

        dhid = _dot_nt(dob_ref[...], wd_ref[...])
        a = a_ref[...].astype(F32)
        u = u_ref[...].astype(F32)
        sg = jax.nn.sigmoid(a)
        sl = a * sg
        hid_ref[...] = (sl * u).astype(BF16)
        du = (dhid * sl).astype(BF16)
        da = (dhid * u * _silu_grad(a, sg)).astype(BF16)
        du_ref[...] = du
        da_ref[...] = da
        acc_ref[...] += _dot(da_slots[1 - slot], wg_ref[...]) + _dot(du_slots[1 - slot], wu_ref[...])
        da_slots[slot] = da
        du_slots[slot] = du

        @pl.when((jp == N_CHIPS - 1) & (s > 0))
        def _():
            dx, dg = _rms_bwd(acc_ref[...], x_ref[...], g_ref[...])
            dx_ref[...] = dxo_prev_ref[...] + dx
            dg_ref[...] += dg
            acc_ref[...] = jnp.zeros_like(acc_ref)

    tok_cur = pl.BlockSpec((tm, D_MODEL), lambda s: (cur(s)[0], 0))
    tok_prev = pl.BlockSpec((tm, D_MODEL), lambda s: (prev(s)[0], 0))
    act = pl.BlockSpec((None, tm, FF_SHARD), lambda s: (cur(s)[1], cur(s)[0], 0))
    row = pl.BlockSpec((1, D_MODEL), lambda s: (0, 0))
    w_arrays, weights = _ffn_weight_operands(ffn_w, [lambda s: prev(s)[1], lambda s: prev(s)[1], lambda s: cur(s)[1]])
    act_shape = jax.ShapeDtypeStruct((N_CHIPS, t, FF_SHARD), BF16)
    return _call(
        body, (dxo, dxo, x, g, a4, u4, *w_arrays), name=name, grid=(steps + 1,),
        in_specs=[tok_cur, tok_prev, tok_prev, row, act, act] + weights,
        out_specs=[act, act, act, tok_cur, tok_prev, row],
        out_shape=[act_shape, act_shape, act_shape,
                   jax.ShapeDtypeStruct((t, D_MODEL), BF16),
                   jax.ShapeDtypeStruct((t, D_MODEL), F32),
                   jax.ShapeDtypeStruct((1, D_MODEL), F32)],
        scratch_shapes=[pltpu.VMEM((tm, D_MODEL), F32), pltpu.VMEM((2, tm, FF_SHARD), BF16),
                        pltpu.VMEM((2, tm, FF_SHARD), BF16)],
        compiler_params=_ARB1, hosted=hosted)


def _matmul_tn(a, b, name, tka=None, out_dtype=F32, hosted=None):
    a3, b3 = a.ndim == 3, b.ndim == 3
    nb = a.shape[0] if a3 else (b.shape[0] if b3 else 1)
    t, ka, n = a.shape[-2], a.shape[-1], b.shape[-1]
    tka = ka if tka is None else tka
    tk = min(t, 2 * TOKEN_TILE)
    nk = t // tk

    def body(a_ref, b_ref, o_ref, acc_ref):
        k = pl.program_id(2)

        @pl.when(k == 0)
        def _():
            acc_ref[...] = jnp.zeros_like(acc_ref)

        acc_ref[...] += _dot_tn(a_ref[...].astype(BF16), b_ref[...].astype(BF16))

        @pl.when(k == nk - 1)
        def _():
            o_ref[...] = acc_ref[...].astype(out_dtype)

    a_spec = (pl.BlockSpec((None, tk, tka), lambda i, j, k: (i, k, j)) if a3
              else pl.BlockSpec((tk, tka), lambda i, j, k: (k, j)))
    b_spec = (pl.BlockSpec((None, tk, n), lambda i, j, k: (i, k, 0)) if b3
              else pl.BlockSpec((tk, n), lambda i, j, k: (k, 0)))
    outs, carried = _call(
        body, (a, b), name=name, grid=(nb, ka // tka, t // tk),
        in_specs=[a_spec, b_spec],
        out_specs=[pl.BlockSpec((None, tka, n), lambda i, j, k: (i, j, 0))],
        out_shape=[jax.ShapeDtypeStruct((nb, ka, n), out_dtype)],
        scratch_shapes=[pltpu.VMEM((tka, n), F32)],
        compiler_params=_ARB3, hosted=hosted)
    return outs[0] if hosted is None else (outs[0], carried)


def _matmul_nt(a, w, name, out_dtype=F32):
    t, k = a.shape
    n = w.shape[0]
    tm = min(t, TOKEN_TILE)

    def body(a_ref, w_ref, o_ref):
        o_ref[...] = _dot_nt(a_ref[...].astype(BF16), w_ref[...]).astype(out_dtype)

    return pl.pallas_call(
        body, name=name, grid=(t // tm,),
        in_specs=[pl.BlockSpec((tm, k), lambda i: (i, 0)), pl.BlockSpec((n, k), lambda i: (0, 0))],
        out_specs=pl.BlockSpec((tm, n), lambda i: (i, 0)),
        out_shape=jax.ShapeDtypeStruct((t, n), out_dtype),
        compiler_params=_ARB1,
    )(a, w)


def _mixer_in_bwd(dproj, w_in_pt, dres, x, g, name):
    t, k = dproj.shape
    tm = min(t, TOKEN_TILE)

    def body(a_ref, w_ref, dres_ref, x_ref, g_ref, dx_ref, dg_ref):
        @pl.when(pl.program_id(0) == 0)
        def _():
            dg_ref[...] = jnp.zeros_like(dg_ref)

        dh = _dot(a_ref[...], w_ref[...])
        dx, dg = _rms_bwd(dh, x_ref[...], g_ref[...])
        dx_ref[...] = dres_ref[...] + dx
        dg_ref[...] += dg

    tok = pl.BlockSpec((tm, D_MODEL), lambda i: (i, 0))
    row = pl.BlockSpec((1, D_MODEL), lambda i: (0, 0))
    return pl.pallas_call(
        body, name=name, grid=(t // tm,),
        in_specs=[pl.BlockSpec((tm, k), lambda i: (i, 0)), pl.BlockSpec((k, D_MODEL), lambda i: (0, 0)), tok, tok, row],
        out_specs=[tok, row],
        out_shape=[jax.ShapeDtypeStruct((t, D_MODEL), F32), jax.ShapeDtypeStruct((1, D_MODEL), F32)],
        compiler_params=_ARB1,
    )(dproj, w_in_pt, dres, x, g)


def _mixer_in_fwd(x, g, w_in_pt, name, hosted=None):
    t = x.shape[0]
    tm = min(t, TOKEN_TILE)
    tn = PROJ_P // PROJ_TILES

    def body(x_ref, g_ref, w_ref, p_ref, h_ref):
        @pl.when(pl.program_id(1) == 0)
        def _():
            xv = x_ref[...]
            h_ref[...] = ((xv * _rms_scale(xv)) * g_ref[...]).astype(BF16)

        p_ref[...] = _dot_nt(h_ref[...], w_ref[...])

    tok = pl.BlockSpec((tm, D_MODEL), lambda i, j: (i, 0))
    return _call(
        body, (x, g, w_in_pt), name=name, grid=(t // tm, PROJ_TILES),
        in_specs=[tok, pl.BlockSpec((1, D_MODEL), lambda i, j: (0, 0)),
                  pl.BlockSpec((tn, D_MODEL), lambda i, j: (j, 0))],
        out_specs=[pl.BlockSpec((tm, tn), lambda i, j: (i, j)), tok],
        out_shape=[jax.ShapeDtypeStruct((t, PROJ_P), F32), jax.ShapeDtypeStruct((t, D_MODEL), BF16)],
        scratch_shapes=[], compiler_params=_ARB2, hosted=hosted)


def _mixer_out_fwd(o_ret, o_gla, w_out, x, name):
    t = x.shape[0]
    tm = min(t, TOKEN_TILE)
    half = HEADS * LANES

    def body(a_ref, b_ref, w_ref, x_ref, o_ref):
        o_ref[...] = x_ref[...] + _dot(a_ref[...], w_ref[0:half, :]) + _dot(b_ref[...], w_ref[half:2 * half, :])

    tok = pl.BlockSpec((tm, D_MODEL), lambda i: (i, 0))
    hb = pl.BlockSpec((tm, half), lambda i: (i, 0))
    return pl.pallas_call(
        body, name=name, grid=(t // tm,),
        in_specs=[hb, hb, pl.BlockSpec((2 * half, D_MODEL), lambda i: (0, 0)), tok],
        out_specs=tok, out_shape=jax.ShapeDtypeStruct((t, D_MODEL), F32),
        compiler_params=_ARB1,
    )(o_ret, o_gla, w_out, x)


def _final_loss(x, g, target, name):
    t = x.shape[0]
    tm = min(t, TOKEN_TILE)

    def body(x_ref, g_ref, t_ref, l_ref, dx_ref, dg_ref):
        @pl.when(pl.program_id(0) == 0)
        def _():
            l_ref[...] = jnp.zeros_like(l_ref)
            dg_ref[...] = jnp.zeros_like(dg_ref)

        xv = x_ref[...]
        gv = g_ref[...]
        err = (xv * _rms_scale(xv)) * gv - t_ref[...]
        l_ref[...] += 0.5 * jnp.sum(jnp.mean(err * err, axis=-1, keepdims=True), axis=0, keepdims=True)
        dx, dg = _rms_bwd(err * (1.0 / D_MODEL), xv, gv)
        dx_ref[...] = dx
        dg_ref[...] += dg

    tok = pl.BlockSpec((tm, D_MODEL), lambda i: (i, 0))
    row = pl.BlockSpec((1, D_MODEL), lambda i: (0, 0))
    return pl.pallas_call(
        body, name=name, grid=(t // tm,),
        in_specs=[tok, row, tok],
        out_specs=[pl.BlockSpec((8, LANES), lambda i: (0, 0)), tok, row],
        out_shape=[jax.ShapeDtypeStruct((8, LANES), F32), jax.ShapeDtypeStruct((t, D_MODEL), F32),
                   jax.ShapeDtypeStruct((1, D_MODEL), F32)],
        compiler_params=_ARB1,
    )(x, g, target)


def _rot(v, cos, sa, sb):
    return v * cos + pltpu.roll(v, 96, 1) * sa + pltpu.roll(v, 32, 1) * sb


def _rot_t(d, cos, sa, sb):
    return d * cos + pltpu.roll(d * sa, 32, 1) + pltpu.roll(d * sb, 96, 1)


def _bmm(a, b):
    return jnp.einsum("cik,ckj->cij", a, b, preferred_element_type=F32)


def _bmm_nt(a, b):
    return jnp.einsum("cik,cjk->cij", a, b, preferred_element_type=F32)


def _bmm_tn(a, b):
    return jnp.einsum("cki,ckj->cij", a, b, preferred_element_type=F32)


def _masked_sum(mask, x):
    hi = x.astype(BF16)
    r1 = x - hi.astype(F32)
    mid = r1.astype(BF16)
    lo = (r1 - mid.astype(F32)).astype(BF16)
    return _bmm(mask, hi) + _bmm(mask, mid) + _bmm(mask, lo)


PAIR = 2


def _tile_inputs(is_ret, qkvg_refs, aux, nc):
    shape3 = (nc, CHUNK, LANES)
    q_ref, k_ref, v_ref, g_ref = qkvg_refs
    low_lanes = lax.broadcasted_iota(jnp.int32, (1, LANES), 1) < 64
    ri = lax.broadcasted_iota(jnp.int32, (PAIR * nc, CHUNK, CHUNK), 1)
    ci = lax.broadcasted_iota(jnp.int32, (PAIR * nc, CHUNK, CHUNK), 2)
    qs, ks, vs, bs, gates, extra = [], [], [], [], [], []
    for hd in range(PAIR):
        q_blk, k_blk = q_ref[...], k_ref[...]
        if hd == 1:
            q_blk, k_blk = pltpu.roll(q_blk, 64, 1), pltpu.roll(k_blk, 64, 1)
        q_raw, k_raw = jnp.where(low_lanes, q_blk, 0.0), jnp.where(low_lanes, k_blk, 0.0)
        vs.append(v_ref[:, LANES * hd:LANES * (hd + 1)].reshape(shape3))
        gates.append(g_ref[:, LANES * hd:LANES * (hd + 1)])
        if is_ret:
            cos_ref, sa_ref, sb_ref, lg_ref = aux
            cos, sa, sb = cos_ref[...], sa_ref[...], sb_ref[...]
            q = _rot(q_raw, cos, sa, sb)
            k = _rot(k_raw, cos, sa, sb) * QK_SCALE
            steps = (lax.broadcasted_iota(jnp.int32, shape3, 1) + 1).astype(F32)
            bs.append(steps * lg_ref[hd])
            extra.append(jnp.exp(jnp.abs(ri[0:nc] - ci[0:nc]).astype(F32) * lg_ref[hd][:, 0:CHUNK]))
        else:
            glow_ref, wa2_ref, ba_ref = aux
            lanes = slice(LANES * hd, LANES * (hd + 1))
            logit = _dot(glow_ref[...].astype(BF16), wa2_ref[:, lanes]) + ba_ref[:, lanes]
            la = (jnp.minimum(logit, 0.0) - jnp.log1p(jnp.exp(-jnp.abs(logit)))) * (1.0 / GATE_NORM)
            bs.append(_masked_sum((ci[0:nc] <= ri[0:nc]).astype(BF16), la.reshape(shape3)))
            extra.append(logit)
            q = q_raw * QK_SCALE
            k = k_raw
        qs.append(q.reshape(shape3))
        ks.append(k.reshape(shape3))
    cat = lambda parts: jnp.concatenate(parts, axis=0)
    return cat(qs), cat(ks), cat(vs), gates, cat(bs), extra, ri, ci


def _tile_scores(q, k, b, ri, ci):
    mid = b[:, CHUNK // 2 - 1:CHUNK // 2, :]
    ep = jnp.exp(b - mid)
    en = jnp.exp(mid - b)
    qt, kt, qh, kh = q * ep, k * en, q * en, k * ep
    low = _bmm_nt(qt.astype(BF16), kt.astype(BF16))
    upp = _bmm_nt(qh.astype(BF16), kh.astype(BF16))
    scores = jnp.where(ci <= ri, low, upp)
    return scores, ep, en, qt, kt, qh, kh


def _attn_specs(is_ret, t, tb, imap_t):
    nb = t // tb
    base = 0 if is_ret else 12
    wide = PAIR * LANES
    proj = [pl.BlockSpec((tb, LANES), lambda p, i: (imap_t(i), base + p)),
            pl.BlockSpec((tb, LANES), lambda p, i: (imap_t(i), base + 2 + p)),
            pl.BlockSpec((tb, wide), lambda p, i: (imap_t(i), (base + 4) // 2 + p)),
            pl.BlockSpec((tb, wide), lambda p, i: (imap_t(i), (base + 8) // 2 + p))]
    lane_t = pl.BlockSpec((tb, LANES), lambda p, i: (imap_t(i), 0))
    if is_ret:
        aux = [lane_t, lane_t, lane_t, pl.BlockSpec((PAIR, 1, LANES), lambda p, i: (p, 0, 0))]
    else:
        aux = [pl.BlockSpec((tb, LANES), lambda p, i: (imap_t(i), PROJ_P // LANES - 1)),
               pl.BlockSpec((LANES, wide), lambda p, i: (0, p)),
               pl.BlockSpec((1, wide), lambda p, i: (0, p))]
    gain = pl.BlockSpec((1, wide), lambda p, i: (0, p))
    pair_t = pl.BlockSpec((tb, wide), lambda p, i: (imap_t(i), p))
    narrow_t = pl.BlockSpec((tb, LANES), lambda p, i: (imap_t(i), p))
    state = pl.BlockSpec((PAIR, tb // CHUNK, LANES, LANES), lambda p, i: (p, imap_t(i), 0, 0))
    return nb, proj, aux, gain, pair_t, narrow_t, state


def _attn_fwd(is_ret, proj, aux_arrays, gain, name, hosted=None):
    t = proj.shape[0]
    tb = min(t, ATTN_TILE)
    nc = tb // CHUNK
    n_aux = 4 if is_ret else 3
    nb, proj_spec, aux_specs, gain_spec, pair_t, _, state_spec = _attn_specs(is_ret, t, tb, lambda i: i)

    def body(*refs):
        qkvg_refs = refs[0:4]
        aux = refs[4:4 + n_aux]
        gn_ref, ofin_ref, oraw_ref, st_ref, state = refs[4 + n_aux:]

        @pl.when(pl.program_id(1) == 0)
        def _():
            state[...] = jnp.zeros_like(state)

        q, k, v, gates, b, extra, ri, ci = _tile_inputs(is_ret, qkvg_refs, aux, nc)
        if is_ret:
            scores = _bmm_nt(q.astype(BF16), k.astype(BF16)) * jnp.concatenate(extra, axis=0)
        else:
            scores = _tile_scores(q, k, b, ri, ci)[0]
        vb = v.astype(BF16)
        intra = _bmm(scores.astype(BF16), vb)
        b_last = b[:, CHUNK - 1:CHUNK, :]
        e_last = jnp.exp(b_last)
        grow = _bmm_tn(vb, (k * jnp.exp(b_last - b)).astype(BF16))
        for hd in range(PAIR):
            st = state[hd]
            for c in range(nc):
                st_ref[hd, c] = st
                st = st * e_last[hd * nc + c] + grow[hd * nc + c]
            state[hd] = st
        starts = st_ref[...].reshape(PAIR * nc, LANES, LANES)
        out3 = intra + _bmm_nt((q * jnp.exp(b)).astype(BF16), starts.astype(BF16))
        for hd in range(PAIR):
            lanes = slice(LANES * hd, LANES * (hd + 1))
            out = out3[hd * nc:(hd + 1) * nc].reshape(tb, LANES)
            oraw_ref[:, lanes] = out
            normed = out * _rms_scale(out)
            gate = gates[hd]
            ofin_ref[:, lanes] = ((normed * gn_ref[:, lanes]) * (gate * jax.nn.sigmoid(gate))).astype(BF16)

    width = HEADS * LANES
    return _call(
        body, (proj, proj, proj, proj, *aux_arrays, gain), name=name, grid=(HEADS // PAIR, nb),
        in_specs=proj_spec + aux_specs + [gain_spec],
        out_specs=[pair_t, pair_t, state_spec],
        out_shape=[jax.ShapeDtypeStruct((t, width), BF16), jax.ShapeDtypeStruct((t, width), F32),
                   jax.ShapeDtypeStruct((HEADS, t // CHUNK, LANES, LANES), F32)],
        scratch_shapes=[pltpu.VMEM((PAIR, LANES, LANES), F32)],
        compiler_params=_ARB2, hosted=hosted)


def _attn_bwd(is_ret, proj, aux_arrays, gain, o_raw, states, d_out, name):
    t = proj.shape[0]
    tb = min(t, ATTN_TILE)
    nc = tb // CHUNK
    n_aux = 4 if is_ret else 3
    nblk = t // tb
    nb, proj_spec, aux_specs, gain_spec, pair_t, narrow_t, state_spec = _attn_specs(
        is_ret, t, tb, lambda i: nblk - 1 - i)
    base = 0 if is_ret else HEADS // PAIR
    dout_spec = pl.BlockSpec((tb, PAIR * LANES), lambda p, i: (nblk - 1 - i, base + p))

    def body(*refs):
        qkvg_refs = refs[0:4]
        aux = refs[4:4 + n_aux]
        gn_ref, oraw_ref, st_ref, dfin_ref = refs[4 + n_aux:8 + n_aux]
        dq_ref, dk_ref, dv_ref, dgate_ref, dgn_ref = refs[8 + n_aux:13 + n_aux]
        if is_ret:
            dstate, dafter_ref = refs[13 + n_aux:]
        else:
            dlogit_ref, dba_ref, dstate, dafter_ref = refs[13 + n_aux:]

        @pl.when(pl.program_id(1) == 0)
        def _():
            dstate[...] = jnp.zeros_like(dstate)
            dgn_ref[...] = jnp.zeros_like(dgn_ref)
            if not is_ret:
                dba_ref[...] = jnp.zeros_like(dba_ref)

        shape3 = (nc, CHUNK, LANES)
        q, k, v, gates, b, extra, ri, ci = _tile_inputs(is_ret, qkvg_refs, aux, nc)
        eb = jnp.exp(b)
        qe = q * eb
        b_last = b[:, CHUNK - 1:CHUNK, :]
        e_last = jnp.exp(b_last)
        ekd = jnp.exp(b_last - b)
        kd = k * ekd

        d_os = []
        for hd in range(PAIR):
            lanes = slice(LANES * hd, LANES * (hd + 1))
            gn, gate = gn_ref[:, lanes], gates[hd]
            out = oraw_ref[:, lanes]
            r = _rms_scale(out)
            normed = out * r
            sg = jax.nn.sigmoid(gate)
            dfin = dfin_ref[:, lanes]
            dgate_ref[:, lanes] = (dfin * (normed * gn) * _silu_grad(gate, sg)).astype(BF16)
            dpre = dfin * (gate * sg)
            dgn_ref[:, lanes] += jnp.sum(dpre * normed, axis=0, keepdims=True)
            dnormed = dpre * gn
            d_o = r * (dnormed - normed * jnp.mean(dnormed * normed, axis=-1, keepdims=True))
            d_os.append(d_o.reshape(shape3))
        dob, vb = jnp.concatenate(d_os, axis=0).astype(BF16), v.astype(BF16)

        dgrow = _bmm_tn(dob, qe.astype(BF16))
        for hd in range(PAIR):
            dst = dstate[hd]
            for c in reversed(range(nc)):
                dafter_ref[hd * nc + c] = dst
                dst = dst * e_last[hd * nc + c] + dgrow[hd * nc + c]
            dstate[hd] = dst
        st = st_ref[...].reshape(PAIR * nc, LANES, LANES)
        dafter = dafter_ref[...]
        stb, dafter_b = st.astype(BF16), dafter.astype(BF16)

        dsc = _bmm_nt(dob, vb)
        dsc_t = _bmm_nt(vb, dob)
        dqe = _bmm(dob, stb)
        dkd = _bmm(vb, dafter_b)
        if is_ret:
            decay, qb, kb = jnp.concatenate(extra, axis=0), q.astype(BF16), k.astype(BF16)
            scores_t = _bmm_nt(kb, qb) * decay
            dq = _bmm((dsc * decay).astype(BF16), kb) + dqe * eb
            dk = _bmm((dsc_t * decay).astype(BF16), qb) + dkd * ekd
        else:
            _, ep, en, qt, kt, qh, kh = _tile_scores(q, k, b, ri, ci)
            qtb, ktb, qhb, khb = qt.astype(BF16), kt.astype(BF16), qh.astype(BF16), kh.astype(BF16)
            scores_t = jnp.where(ci >= ri, _bmm_nt(ktb, qtb), _bmm_nt(khb, qhb))
            dqt = _bmm(jnp.where(ci <= ri, dsc, 0.0).astype(BF16), ktb)
            dqh = _bmm(jnp.where(ci <= ri, 0.0, dsc).astype(BF16), khb)
            dkt = _bmm(jnp.where(ci >= ri, dsc_t, 0.0).astype(BF16), qtb)
            dkh = _bmm(jnp.where(ci >= ri, 0.0, dsc_t).astype(BF16), qhb)
            dq = dqt * ep + dqh * en + dqe * eb
            dk = dkt * en + dkh * ep + dkd * ekd
        dv = _bmm(scores_t.astype(BF16), dob) + _bmm_nt(kd.astype(BF16), dafter_b)

        if not is_ret:
            db = dqt * qt - dkt * kt - dqh * qh + dkh * kh + dqe * qe - dkd * kd
            db_last = (jnp.sum(dkd * kd, axis=1, keepdims=True)
                       + jnp.sum(dafter * st, axis=1, keepdims=True) * e_last)
            last_row = lax.broadcasted_iota(jnp.int32, (PAIR * nc, CHUNK, LANES), 1) == CHUNK - 1
            db = db + jnp.where(last_row, db_last, 0.0)
            dla = _masked_sum((ci >= ri).astype(BF16), db)

        dq_pair, dk_pair = [], []
        for hd in range(PAIR):
            lanes = slice(LANES * hd, LANES * (hd + 1))
            rows3 = slice(hd * nc, (hd + 1) * nc)
            dq_h, dk_h = dq[rows3].reshape(tb, LANES), dk[rows3].reshape(tb, LANES)
            if is_ret:
                cos_ref, sa_ref, sb_ref, _ = aux
                cos, sa, sb = cos_ref[...], sa_ref[...], sb_ref[...]
                dq_h = _rot_t(dq_h, cos, sa, sb)
                dk_h = _rot_t(dk_h, cos, sa, sb) * QK_SCALE
            else:
                dq_h = dq_h * QK_SCALE
                dlogit = dla[rows3].reshape(tb, LANES) * (1.0 / GATE_NORM) * jax.nn.sigmoid(-extra[hd])
                dlogit_ref[:, lanes] = dlogit.astype(BF16)
                dba_ref[:, lanes] += jnp.sum(dlogit, axis=0, keepdims=True)
            dq_pair.append(dq_h)
            dk_pair.append(dk_h)
            dv_ref[:, lanes] = dv[rows3].reshape(tb, LANES).astype(BF16)
        dq_ref[...] = (dq_pair[0] + pltpu.roll(dq_pair[1], 64, 1)).astype(BF16)
        dk_ref[...] = (dk_pair[0] + pltpu.roll(dk_pair[1], 64, 1)).astype(BF16)

    width = HEADS * LANES
    row_out = pl.BlockSpec((1, PAIR * LANES), lambda p, i: (0, p))
    out_specs = [narrow_t, narrow_t, pair_t, pair_t, row_out]
    out_shape = ([jax.ShapeDtypeStruct((t, width // 2), BF16)] * 2 + [jax.ShapeDtypeStruct((t, width), BF16)] * 2
                 + [jax.ShapeDtypeStruct((1, width), F32)])
    if not is_ret:
        out_specs += [pair_t, row_out]
        out_shape += [jax.ShapeDtypeStruct((t, width), BF16), jax.ShapeDtypeStruct((1, width), F32)]
    return pl.pallas_call(
        body, name=name, grid=(HEADS // PAIR, nblk),
        in_specs=proj_spec + aux_specs + [gain_spec, pair_t, state_spec, dout_spec],
        out_specs=out_specs, out_shape=out_shape,
        scratch_shapes=[pltpu.VMEM((PAIR, LANES, LANES), F32), pltpu.VMEM((PAIR * nc, LANES, LANES), F32)],
        compiler_params=_ARB2,
    )(proj, proj, proj, proj, *aux_arrays, gain, o_raw, states, d_out)


PEER_SLOT = (2, 1, 3)


def _place():
    x, y, c = lax.axis_index("x"), lax.axis_index("y"), lax.axis_index("c")
    chips = [(1 - x, y), (x, 1 - y), (1 - x, 1 - y)]
    return x, y, c, 2 * x + y, chips


def _route_split(rows, dtype):
    tile = 16 if dtype == BF16 else 8
    if rows < 2 * tile:
        return None
    return -(-(rows // 2) // tile) * tile


def _routes(by_peer):
    x, y, c, me, chips = _place()
    (xx, xy), (yx, yy), (dx, dy) = chips
    if by_peer:
        slots = dict(own=0, from_x=PEER_SLOT[0], from_y=PEER_SLOT[1], diag=PEER_SLOT[2],
                     mine_on_x=PEER_SLOT[0], mine_on_y=PEER_SLOT[1])
    else:
        slots = dict(own=me, from_x=2 * xx + xy, from_y=2 * yx + yy, diag=2 * dx + dy, mine_on_x=me, mine_on_y=me)
    return c, (xx, xy, c), (yx, yy, c), (dx, dy, c), (x, y, 1 - c), slots


def _gather_legs(src, out, send_sems, recv_sems, base, by_peer):
    c, to_x, to_y, to_d, sibling, s = _routes(by_peer)
    r0 = _route_split(src.shape[2], src.dtype)

    def cp(k, src_ref, dst_ref, to):
        return pltpu.make_async_remote_copy(src_ref=src_ref, dst_ref=dst_ref, send_sem=send_sems.at[base + k],
                                            recv_sem=recv_sems.at[base + k], device_id=to, device_id_type=MESH)

    mine = src.at[:, c]
    legs = dict(
        x=(cp(0, mine, out.at[s["mine_on_x"], :, c], to_x), cp(0, mine, out.at[s["from_x"], :, c], to_x)),
        y=(cp(1, mine, out.at[s["mine_on_y"], :, c], to_y), cp(1, mine, out.at[s["from_y"], :, c], to_y)),
        pass_x=(cp(4, out.at[s["from_x"], :, c], out.at[s["from_x"], :, c], sibling),
                cp(4, mine, out.at[s["from_x"], :, 1 - c], sibling)),
        pass_y=(cp(5, out.at[s["from_y"], :, c], out.at[s["from_y"], :, c], sibling),
                cp(5, mine, out.at[s["from_y"], :, 1 - c], sibling)))
    if r0 is None:
        mine_on_d = s["diag"] if by_peer else s["own"]
        legs["d"] = (cp(2, mine, out.at[mine_on_d, :, c], to_d), cp(2, mine, out.at[s["diag"], :, c], to_d))
        legs["pass_d"] = (cp(6, out.at[s["diag"], :, c], out.at[s["diag"], :, c], sibling),
                          cp(6, mine, out.at[s["diag"], :, 1 - c], sibling))
        return legs, False
    lo, hi = pl.ds(0, r0), pl.ds(r0, src.shape[2] - r0)
    fx_on_y = s["diag"] if by_peer else s["from_x"]
    fy_on_x = s["diag"] if by_peer else s["from_y"]
    legs.update(
        fwd_y=(cp(2, out.at[s["from_x"], :, c, lo], out.at[fx_on_y, :, c, lo], to_y),
               cp(2, mine.at[:, lo], out.at[s["diag"], :, c, lo], to_y)),
        fwd_x=(cp(3, out.at[s["from_y"], :, c, hi], out.at[fy_on_x, :, c, hi], to_x),
               cp(3, mine.at[:, hi], out.at[s["diag"], :, c, hi], to_x)),
        pass_d0=(cp(6, out.at[s["diag"], :, c, lo], out.at[s["diag"], :, c, lo], sibling),
                 cp(6, mine.at[:, lo], out.at[s["diag"], :, 1 - c, lo], sibling)),
        pass_d1=(cp(7, out.at[s["diag"], :, c, hi], out.at[s["diag"], :, c, hi], sibling),
                 cp(7, mine.at[:, hi], out.at[s["diag"], :, 1 - c, hi], sibling)))
    return legs, True


def _gather_steps(legs, routed):
    def start():
        legs["x"][0].start()
        legs["y"][0].start()
        if not routed:
            legs["d"][0].start()

    def middle():
        legs["x"][1].wait_recv()
        if routed:
            legs["fwd_y"][0].start()
        legs["pass_x"][0].start()
        legs["y"][1].wait_recv()
        if routed:
            legs["fwd_x"][0].start()
        legs["pass_y"][0].start()

    def finish():
        last = ["pass_d0", "pass_d1"] if routed else ["pass_d"]
        if routed:
            legs["fwd_y"][1].wait_recv()
            legs["pass_d0"][0].start()
            legs["fwd_x"][1].wait_recv()
            legs["pass_d1"][0].start()
        else:
            legs["d"][1].wait_recv()
            legs["pass_d"][0].start()
        for name in ["pass_x", "pass_y"] + last:
            legs[name][1].wait_recv()
        for name in ["x", "y", "pass_x", "pass_y"] + last + (["fwd_y", "fwd_x"] if routed else ["d"]):
            legs[name][0].wait_send()

    return start, middle, finish


def _gather_plan(arrs):
    na = len(arrs)

    def steps(ins, outs, send_sems, recv_sems):
        return [_gather_steps(*_gather_legs(ins[a], outs[a], send_sems, recv_sems, 8 * a, False)) for a in range(na)]

    def run(which):
        def hook(*refs):
            for step in steps(*refs):
                step[which]()
        return hook

    return _Hosted(arrs, [jax.ShapeDtypeStruct((N_CHIPS,) + a.shape, a.dtype) for a in arrs], 8 * na,
                   run(0), run(2), middle=run(1))


def _pair_exchange(grads, name):
    na = len(grads)

    def body(*refs):
        ins, outs = refs[:na], refs[na:2 * na]
        send_sems, recv_sems = refs[2 * na:]
        x, y, c, _, _ = _place()
        copies = [pltpu.make_async_remote_copy(
            src_ref=ins[a].at[:, 1 - c], dst_ref=outs[a], send_sem=send_sems.at[a], recv_sem=recv_sems.at[a],
            device_id=(x, y, 1 - c), device_id_type=MESH) for a in range(na)]
        for cp in copies:
            cp.start()
        for cp in copies:
            cp.wait()

    return pl.pallas_call(
        body, name=name,
        in_specs=[ANY] * na, out_specs=[ANY] * na,
        out_shape=[jax.ShapeDtypeStruct(g.shape[:1] + g.shape[2:], g.dtype) for g in grads],
        scratch_shapes=[pltpu.SemaphoreType.DMA((na,)), pltpu.SemaphoreType.DMA((na,))],
    )(*grads)


def _pair_add(grad, recv, c_arr, name):
    _, _, r, cols = grad.shape

    def body(c_ref, g_ref, r_ref, o_ref):
        o_ref[...] = (g_ref[...].astype(F32) + r_ref[...].astype(F32)).astype(BF16)

    return pl.pallas_call(
        body, name=name,
        grid_spec=pltpu.PrefetchScalarGridSpec(
            num_scalar_prefetch=1, grid=(N_CHIPS,),
            in_specs=[pl.BlockSpec((None, None, r, cols), lambda p, c_ref: (p, c_ref[0], 0, 0)),
                      pl.BlockSpec((None, r, cols), lambda p, c_ref: (p, 0, 0))],
            out_specs=pl.BlockSpec((None, r, cols), lambda p, c_ref: (p, 0, 0))),
        out_shape=jax.ShapeDtypeStruct((N_CHIPS, r, cols), BF16),
        compiler_params=_ARB1,
    )(c_arr, grad, recv)


def _chip_exchange_plan(sums, by_peer=False):
    na = len(sums)

    def copies(ins, outs, send_sems, recv_sems):
        x, y, c, me, chips = _place()

        def copy(a, j, px, py, block, slot):
            return pltpu.make_async_remote_copy(
                src_ref=ins[a].at[block], dst_ref=outs[a].at[slot],
                send_sem=send_sems.at[3 * a + j], recv_sem=recv_sems.at[3 * a + j],
                device_id=(px, py, c), device_id_type=MESH)

        peers = [(a, j, px, py) for a in range(na) for j, (px, py) in enumerate(chips)]
        return me, peers, copy

    def start(*refs):
        me, peers, copy = copies(*refs)
        for a, j, px, py in peers:
            if by_peer:
                copy(a, j, px, py, PEER_SLOT[j], PEER_SLOT[j]).start()
            else:
                copy(a, j, px, py, 2 * px + py, me).start()

    def finish(*refs):
        me, peers, copy = copies(*refs)
        for a, j, px, py in peers:
            if by_peer:
                copy(a, j, px, py, PEER_SLOT[j], PEER_SLOT[j]).wait_recv()
            else:
                copy(a, j, px, py, me, 2 * px + py).wait_recv()
        for a, j, px, py in peers:
            if by_peer:
                copy(a, j, px, py, PEER_SLOT[j], PEER_SLOT[j]).wait_send()
            else:
                copy(a, j, px, py, 2 * px + py, me).wait_send()

    return _Hosted(sums, [jax.ShapeDtypeStruct(s.shape, s.dtype) for s in sums], 3 * na, start, finish)


def _chip_sum(own, recv, me_arr, name):
    _, r, cols = recv.shape

    def body(me_ref, own_ref, r_ref, o_ref):
        o_ref[...] = jnp.zeros_like(o_ref)
        for q in range(N_CHIPS):
            @pl.when(me_ref[0] == q)
            def _():
                o_ref[...] += own_ref[...].astype(F32)

            @pl.when(me_ref[0] != q)
            def _():
                o_ref[...] += r_ref[q].astype(F32)

    return pl.pallas_call(
        body, name=name,
        grid_spec=pltpu.PrefetchScalarGridSpec(
            num_scalar_prefetch=1, grid=(1,),
            in_specs=[pl.BlockSpec((None, r, cols), lambda i, me_ref: (me_ref[0], 0, 0)),
                      pl.BlockSpec((N_CHIPS, r, cols), lambda i, me_ref: (0, 0, 0))],
            out_specs=pl.BlockSpec((r, cols), lambda i, me_ref: (0, 0))),
        out_shape=jax.ShapeDtypeStruct((r, cols), F32),
        compiler_params=_ARB1,
    )(me_arr, own, recv)


def _peer_sum(own, recv, name):
    _, r, cols = recv.shape

    def body(own_ref, r_ref, o_ref):
        acc = own_ref[...].astype(F32) + r_ref[1].astype(F32)
        acc = acc + r_ref[2].astype(F32)
        o_ref[...] = acc + r_ref[3].astype(F32)

    return pl.pallas_call(
        body, name=name, grid=(1,),
        in_specs=[pl.BlockSpec((None, r, cols), lambda i: (0, 0, 0)), pl.BlockSpec((N_CHIPS, r, cols), lambda i: (0, 0, 0))],
        out_specs=pl.BlockSpec((r, cols), lambda i: (0, 0)),
        out_shape=jax.ShapeDtypeStruct((r, cols), F32),
        compiler_params=_ARB1,
    )(own, recv)


def _pair_share(halves):
    na = len(halves)

    def body(*refs):
        ins, outs = refs[:na], refs[na:2 * na]
        send_sems, recv_sems = refs[2 * na:]
        x, y, c, _, _ = _place()
        copies = [pltpu.make_async_remote_copy(
            src_ref=ins[a], dst_ref=outs[a], send_sem=send_sems.at[a], recv_sem=recv_sems.at[a],
            device_id=(x, y, 1 - c), device_id_type=MESH) for a in range(na)]
        for cp in copies:
            cp.start()
        for cp in copies:
            cp.wait()

    return pl.pallas_call(
        body, name="pair_share",
        in_specs=[ANY] * na, out_specs=[ANY] * na,
        out_shape=[jax.ShapeDtypeStruct(h.shape, h.dtype) for h in halves],
        scratch_shapes=[pltpu.SemaphoreType.DMA((na,)), pltpu.SemaphoreType.DMA((na,))],
    )(*halves)


def _small_allreduce(block):
    m, n = block.shape

    def body(x_ref, all_ref, sum_ref, send_sems, recv_sems, local_sem):
        x, y, c, _, chips = _place()
        me, sibling = (x, y, c), (x, y, 1 - c)

        def rows(px, py, pc):
            return all_ref.at[pl.ds((4 * px + 2 * py + pc) * m, m), :]

        def copy(k, blk, to, src=None):
            return pltpu.make_async_remote_copy(
                src_ref=rows(*blk) if src is None else src, dst_ref=rows(*blk),
                send_sem=send_sems.at[k], recv_sem=recv_sems.at[k], device_id=to, device_id_type=MESH)

        mine = pltpu.make_async_copy(x_ref, rows(*me), local_sem)
        mine.start()
        first = [copy(0, me, sibling, src=x_ref)]
        first += [copy(1 + j, me, (*chip, c), src=x_ref) for j, chip in enumerate(chips)]
        for cp in first:
            cp.start()
        passed = [copy(4 + j, (*chip, c), sibling) for j, chip in enumerate(chips)]
        for j, chip in enumerate(chips):
            copy(1 + j, (*chip, c), me).wait_recv()
            passed[j].start()
        copy(0, sibling, me).wait_recv()
        for j, chip in enumerate(chips):
            copy(4 + j, (*chip, 1 - c), me).wait_recv()
        for cp in first + passed:
            cp.wait_send()
        mine.wait()
        acc = all_ref[0:m, :]
        for d in range(1, 8):
            acc = acc + all_ref[d * m:(d + 1) * m, :]
        sum_ref[...] = acc

    vmem = pl.BlockSpec(memory_space=pltpu.VMEM)
    return pl.pallas_call(
        body, name="small_allreduce",
        in_specs=[vmem], out_specs=[vmem, vmem],
        out_shape=[jax.ShapeDtypeStruct((8 * m, n), F32), jax.ShapeDtypeStruct((m, n), F32)],
        scratch_shapes=[pltpu.SemaphoreType.DMA((7,)), pltpu.SemaphoreType.DMA((7,)), pltpu.SemaphoreType.DMA],
    )(block)[1]


def _row_tile(rows):
    best = rows
    for cand in range(8, min(rows, 512) + 1, 8):
        if rows % cand == 0:
            best = cand
    return best


def _adamw_math(w, g, m, v):
    m2 = ADAM_B1 * m + (1.0 - ADAM_B1) * g
    v2 = ADAM_B2 * v + (1.0 - ADAM_B2) * (g * g)
    m_hat = m2 / (1.0 - ADAM_B1 ** ADAM_STEP)
    v_hat = v2 / (1.0 - ADAM_B2 ** ADAM_STEP)
    return -ADAM_LR * (m_hat / (jnp.sqrt(v_hat) + ADAM_EPS) + ADAM_WD * w), m2, v2


def _adamw_halves(w, g_mine, g_other, m, v, c_arr, name):
    rows, cols = w.shape
    r = rows // 2
    tr = _row_tile(r)
    nt = r // tr

    def body(c_ref, w_ref, gm_ref, go_ref, m_ref, v_ref, g_ref, d_ref, nm_ref, nv_ref):
        gv = jnp.where(pl.program_id(0) == c_ref[0], gm_ref[...], go_ref[...])
        g_ref[...] = gv
        d_ref[...], nm_ref[...], nv_ref[...] = _adamw_math(w_ref[...], gv, m_ref[...], v_ref[...])

    full = pl.BlockSpec((tr, cols), lambda h, i, c_ref: (h * nt + i, 0))
    half = pl.BlockSpec((tr, cols), lambda h, i, c_ref: (i, 0))
    shape = jax.ShapeDtypeStruct((rows, cols), F32)
    return pl.pallas_call(
        body, name=name,
        grid_spec=pltpu.PrefetchScalarGridSpec(
            num_scalar_prefetch=1, grid=(2, nt),
            in_specs=[full, half, half, full, full], out_specs=[full] * 4),
        out_shape=[shape] * 4,
        compiler_params=_ARB2,
    )(c_arr, w, g_mine, g_other, m, v)


def _adamw(w, g, m, v, name):
    rows, cols = w.shape
    tr = _row_tile(rows)

    def body(w_ref, g_ref, m_ref, v_ref, d_ref, nm_ref, nv_ref):
        d_ref[...], nm_ref[...], nv_ref[...] = _adamw_math(w_ref[...], g_ref[...], m_ref[...], v_ref[...])

    spec = pl.BlockSpec((tr, cols), lambda i: (i, 0))
    shape = jax.ShapeDtypeStruct((rows, cols), F32)
    return pl.pallas_call(
        body, name=name, grid=(rows // tr,),
        in_specs=[spec] * 4, out_specs=[spec] * 3, out_shape=[shape] * 3,
        compiler_params=_ARB1,
    )(w, g, m, v)


def _pad_w_in_t(w_in_t):
    return jnp.pad(w_in_t, ((0, PROJ_P - IN_WIDTH), (0, 0)))


def _unpad_w_in_t(w_pt):
    return w_pt[0:IN_WIDTH]


def _rope_tables(t):
    half = 32
    inv = ROPE_BASE ** (-jnp.arange(half, dtype=F32) * 2.0 / 64)
    ang = jnp.arange(t, dtype=F32)[:, None] * inv[None, :]
    cos, sin = jnp.cos(ang), jnp.sin(ang)
    z32, z64 = jnp.zeros((t, 32), F32), jnp.zeros((t, 64), F32)
    return (jnp.concatenate([cos, cos, z64], axis=1),
            jnp.concatenate([-sin, z32, z64], axis=1),
            jnp.concatenate([z32, sin, z64], axis=1))


def _halves(w):
    n, rows, cols = w.shape
    return w.reshape(n, 2, rows // 2, cols)


_VMEM = pl.BlockSpec(memory_space=pltpu.VMEM)


def _pack_small(n1, nm, n2, nf, nret, ngla, ba, wa2_p, loss_blk):
    def body(n1_ref, nm_ref, n2_ref, nf_ref, nret_ref, ngla_ref, ba_ref, wa2_ref, loss_ref, o_ref):
        o_ref[...] = jnp.zeros_like(o_ref)
        o_ref[0:1, :] = n1_ref[...]
        o_ref[1:2, :] = nm_ref[...]
        o_ref[2:3, :] = n2_ref[...]
        o_ref[3:4, :] = nf_ref[...]
        o_ref[4:5, 0:512] = nret_ref[...]
        o_ref[4:5, 512:1024] = ngla_ref[...]
        o_ref[5:6, 0:256] = ba_ref[...]
        o_ref[6:7, 0:LANES] = loss_ref[0:1, :]
        o_ref[8:8 + GATE_RANK, 0:HEADS * LANES] = wa2_ref[0:GATE_RANK, :]

    return pl.pallas_call(
        body, name="pack_small", in_specs=[_VMEM] * 9, out_specs=_VMEM,
        out_shape=jax.ShapeDtypeStruct((SMALL_ROWS, D_MODEL), F32),
    )(n1, nm, n2, nf, nret, ngla, ba, wa2_p, loss_blk)


def _small_update(summed, chip_arr, ws, ms, vs):
    n = len(ws)

    def body(chip_ref, s_ref, *refs):
        w_refs, m_refs, v_refs = refs[0:n], refs[n:2 * n], refs[2 * n:3 * n]
        outs = refs[3 * n:]
        wa2_all = s_ref[8:8 + GATE_RANK, 0:HEADS * LANES]
        wa2_g = jnp.zeros((GATE_RANK, 64), F32)
        for p in range(N_CHIPS):
            wa2_g = jnp.where(chip_ref[0] == p, wa2_all[:, LANES * p:LANES * p + 64], wa2_g)
        grads = [s_ref[0:1, :], s_ref[1:2, :], s_ref[2:3, :], s_ref[3:4, :], s_ref[4:5, 0:512],
                 s_ref[4:5, 512:1024], s_ref[5:6, 0:256], wa2_g]
        for k in range(n):
            d, m2, v2 = _adamw_math(w_refs[k][...], grads[k], m_refs[k][...], v_refs[k][...])
            outs[k][...] = grads[k]
            outs[n + k][...] = d
            outs[2 * n + k][...] = m2
            outs[3 * n + k][...] = v2

    shapes = [jax.ShapeDtypeStruct(w.shape, F32) for w in ws] * 4
    smem = pl.BlockSpec(memory_space=pltpu.SMEM)
    outs = pl.pallas_call(
        body, name="small_update", in_specs=[smem] + [_VMEM] * (1 + 3 * n), out_specs=[_VMEM] * (4 * n),
        out_shape=shapes,
    )(chip_arr, summed, *ws, *ms, *vs)
    return outs[0:n], outs[n:2 * n], outs[2 * n:3 * n], outs[3 * n:4 * n]


def _pad_in_rows(w_t):
    return jnp.pad(w_t, ((0, IN_ROWS - IN_SHARD), (0, 0)))


def _forward_backward(xs, target, ffn1_w, rest, ba_p, ffn1_norm_g, mix_norm_g, ret_norm_g, gla_norm_g, ffn2_norm_g,
                      final_norm_g, ffn1_gather=None, rest_plan=None, rest_weights=None, ffn2_plans=None,
                      ffn2_weights=None, early=None, late=None):
    t = xs.shape[0]
    cos_t, sa_t, sb_t = _rope_tables(t)
    log_gamma = jnp.log(1.0 - 2.0 ** (-5.0 - jnp.arange(HEADS, dtype=F32)))
    lg_t = jnp.broadcast_to(log_gamma[:, None, None], (HEADS, 1, LANES))
    ret_aux = [cos_t, sa_t, sb_t, lg_t]

    if ffn1_gather is None:
        (x1, a1, u1, h1), gathered = _ffn_fwd(xs, ffn1_norm_g, ffn1_w, "ffn1_fwd", hosted=rest_plan)
    else:
        ffn1_shard, ffn1_weights = ffn1_gather
        (x1, a1, u1, h1, wall), gathered = _ffn1_fwd_gathering(xs, ffn1_norm_g, ffn1_shard, "ffn1_fwd",
                                                               hosted=rest_plan)
        ffn1_w = ffn1_weights(wall)
    ffn2_w, w_in_pt, w_out_full, wa2_p = rest if rest_plan is None else rest_weights(gathered)
    plans = [None] * 3 if ffn2_plans is None else ffn2_plans
    (proj, h_mix), got_gate = _mixer_in_fwd(x1, mix_norm_g, w_in_pt, "mixer_in_fwd", hosted=plans[0])
    gla_aux = [proj, wa2_p, ba_p]
    (o_ret, raw_ret, st_ret), got_up = _attn_fwd(True, proj, ret_aux, ret_norm_g, "ret_fwd", hosted=plans[1])
    (o_gla, raw_gla, st_gla), got_down = _attn_fwd(False, proj, gla_aux, gla_norm_g, "gla_fwd", hosted=plans[2])
    if ffn2_plans is not None:
        ffn2_w = ffn2_weights(got_gate + got_up + got_down)
    x2 = _mixer_out_fwd(o_ret, o_gla, w_out_full, x1, "mixer_out_fwd")
    (x3, a2, u2, h2), _ = _ffn_fwd(x2, ffn2_norm_g, ffn2_w, "ffn2_fwd")
    loss_blk, dx3, d_final_g = _final_loss(x3, final_norm_g, target, "final_loss")

    (da2, du2, hid2, dob2, dx2, d_ffn2_g), _ = _ffn_bwd(dx3, x2, ffn2_norm_g, a2, u2, ffn2_w, "ffn2_bwd")
    g_gate2 = _matmul_tn(da2, h2, "ffn2_dgate", out_dtype=BF16)
    g_up2 = _matmul_tn(du2, h2, "ffn2_dup", out_dtype=BF16)
    g_down2 = _matmul_tn(hid2, dob2, "ffn2_ddown", out_dtype=BF16)

    d_o = _matmul_nt(dx2, w_out_full, "mixer_out_bwd")
    g_wout_ret = _matmul_tn(o_ret, dx2, "wout_grad_ret", out_dtype=BF16)
    g_wout_gla = _matmul_tn(o_gla, dx2, "wout_grad_gla", out_dtype=BF16)
    *dproj_ret, d_ret_g = _attn_bwd(True, proj, ret_aux, ret_norm_g, raw_ret, st_ret, d_o, "ret_bwd")
    *dproj_gla, d_gla_g, dlogit, d_ba_p = _attn_bwd(False, proj, gla_aux, gla_norm_g, raw_gla, st_gla, d_o, "gla_bwd")
    d_glow = _matmul_nt(dlogit, wa2_p, "gate_low_bwd", out_dtype=BF16)
    g_wa2_p = _matmul_tn(proj[:, PROJ_P - LANES:], dlogit, "gate_w_grad")
    dproj = jnp.concatenate(dproj_ret + dproj_gla + [d_glow], axis=1)
    g_win_p = _matmul_tn(dproj, h_mix, "w_in_grad", tka=PROJ_P // PROJ_TILES, out_dtype=BF16)
    dx1, d_mix_g = _mixer_in_bwd(dproj, w_in_pt, dx2, x1, mix_norm_g, "mixer_in_bwd")
    g_win_t = _unpad_w_in_t(g_win_p[0])
    g_win = jnp.stack([_pad_in_rows(g_win_t[IN_SHARD * p:IN_SHARD * (p + 1)]) for p in range(N_CHIPS)], axis=0)
    g_wout = jnp.concatenate([g_wout_ret[0], g_wout_gla[0]], axis=0).reshape(N_CHIPS, D_MODEL // N_CHIPS, D_MODEL)

    early_plan = None if early is None else early([g_gate2, g_up2, g_down2, g_win, g_wout])
    (da1, du1, hid1, dob1, grad_x, d_ffn1_g), arrived = _ffn_bwd(dx1, xs, ffn1_norm_g, a1, u1, ffn1_w, "ffn1_bwd",
                                                                hosted=early_plan)
    late_grads, late_arrived = [], []
    for lhs, rhs, name in ((da1, h1, "ffn1_dgate"), (du1, h1, "ffn1_dup"), (hid1, dob1, "ffn1_ddown")):
        plan = None if late is None or not late_grads else late(late_grads[-1], len(late_grads))
        res = _matmul_tn(lhs, rhs, name, out_dtype=BF16, hosted=plan)
        if plan is not None:
            res, carried = res
            late_arrived += carried
        late_grads.append(res)
    g_gate1, g_up1, g_down1 = late_grads

    return (loss_blk, grad_x, g_gate1, g_up1, g_down1, g_gate2, g_up2, g_down2, g_win, g_wout, g_wa2_p,
            d_ba_p, d_ffn1_g, d_mix_g, d_ffn2_g, d_final_g, d_ret_g, d_gla_g, arrived, late_arrived)


def kernel(x, ffn1_norm_g, ffn1_w_gate, ffn1_w_up, ffn1_w_down, mix_norm_g, w_in, ret_norm_g, gla_w_a2, gla_b_a, gla_norm_g, w_out, ffn2_norm_g, ffn2_w_gate, ffn2_w_up, ffn2_w_down, final_norm_g, loss_target, m_ffn1_norm_g, m_ffn1_w_gate, m_ffn1_w_up, m_ffn1_w_down, m_mix_norm_g, m_w_in, m_ret_norm_g, m_gla_w_a2, m_gla_b_a, m_gla_norm_g, m_w_out, m_ffn2_norm_g, m_ffn2_w_gate, m_ffn2_w_up, m_ffn2_w_down, m_final_norm_g, v_ffn1_norm_g, v_ffn1_w_gate, v_ffn1_w_up, v_ffn1_w_down, v_mix_norm_g, v_w_in, v_ret_norm_g, v_gla_w_a2, v_gla_b_a, v_gla_norm_g, v_w_out, v_ffn2_norm_g, v_ffn2_w_gate, v_ffn2_w_up, v_ffn2_w_down, v_final_norm_g):
    t = x.shape[1]
    xs = x.reshape(t, D_MODEL)
    target = loss_target.reshape(t, D_MODEL)
    chip = 2 * lax.axis_index("x") + lax.axis_index("y")
    c_arr = lax.axis_index("c").astype(jnp.int32).reshape(1)

    me_arr = chip.astype(jnp.int32).reshape(1)

    pad_rows = _pad_in_rows

    def own_block(gathered, shard):
        return lax.dynamic_update_slice(gathered, shard[None], (chip,) + (0,) * shard.ndim)

    ffn1_shard = _halves(jnp.stack([ffn1_w_gate[0].T, ffn1_w_up[0].T, ffn1_w_down[0]], axis=0).astype(BF16))
    rest_shards = [_halves(pad_rows(w_in[0].T).astype(BF16)[None]),
                   _halves(w_out.astype(BF16)),
                   jnp.concatenate([gla_w_a2.reshape(GATE_RANK, 64), jnp.zeros((GATE_RANK, 64), F32)],
                                   axis=1).reshape(1, 2, 8, LANES)]
    ffn2_shards = [_halves(w.astype(BF16)[None]) for w in (ffn2_w_gate[0].T, ffn2_w_up[0].T, ffn2_w_down[0])]
    def ffn1_weights(gathered):
        return lax.dynamic_update_slice(gathered, ffn1_shard[None], (0,) * 5).reshape(N_CHIPS, 3, FF_SHARD, D_MODEL)

    def rest_weights(gathered):
        win_all, wout_all, wa2_all = [own_block(g, s) for g, s in zip(gathered, rest_shards)]
        win_t = win_all.reshape(N_CHIPS, IN_ROWS, D_MODEL)
        w_in_pt = jnp.zeros((PROJ_P, D_MODEL), BF16)
        for p in range(N_CHIPS):
            w_in_pt = lax.dynamic_update_slice(w_in_pt, win_t[p, 0:IN_SHARD], (IN_SHARD * p, 0))
        wa2_p = jnp.pad(
            wa2_all.reshape(N_CHIPS, GATE_RANK, LANES).transpose(1, 0, 2).reshape(GATE_RANK, HEADS * LANES),
            ((0, LANES - GATE_RANK), (0, 0))).astype(BF16)
        return (None, w_in_pt, wout_all.reshape(D_MODEL, D_MODEL), wa2_p)

    def ffn2_weights(gathered):
        return [own_block(g, s).reshape(N_CHIPS, FF_SHARD, D_MODEL) for g, s in zip(gathered, ffn2_shards)]

    def pair_sums(grads, tag):
        halves = [g.reshape(g.shape[0], 2, g.shape[1] // 2, g.shape[2]) for g in grads]
        recv = _pair_exchange(halves, "pair_exchange_" + tag)
        return [_pair_add(g, r, c_arr, "pair_add_%s%d" % (tag, k)) for k, (g, r) in enumerate(zip(halves, recv))]

    early_sums = []

    def early(grads):
        early_sums.extend(pair_sums(grads, "early"))
        return _chip_exchange_plan(early_sums)

    late_sums = []

    def late(grad, number):
        late_sums.extend(pair_sums([grad], "late%d" % number))
        return _chip_exchange_plan(late_sums[-1:], by_peer=True)

    ba_p = jnp.pad(gla_b_a.reshape(HEADS, 64), ((0, 0), (0, 64))).reshape(1, HEADS * LANES)
    fb = _forward_backward(xs, target, None, None, ba_p, ffn1_norm_g, mix_norm_g, ret_norm_g, gla_norm_g,
                           ffn2_norm_g, final_norm_g.reshape(1, D_MODEL), ffn1_gather=(ffn1_shard, ffn1_weights),
                           rest_plan=_gather_plan(rest_shards), rest_weights=rest_weights,
                           ffn2_plans=[_gather_plan([s]) for s in ffn2_shards], ffn2_weights=ffn2_weights,
                           early=early, late=late)
    (loss_blk, grad_x, _, _, g_down1, _, _, _, _, _, g_wa2_p,
     d_ba_p, d_ffn1_g, d_mix_g, d_ffn2_g, d_final_g, d_ret_g, d_gla_g, early_arrived, late_arrived) = fb
    late_arrived = late_arrived + _run_hosted(late(g_down1, 3), "chip_exchange_late")
    mine = [_peer_sum(s, r, "chip_sum_%d" % k) for k, (s, r) in enumerate(zip(late_sums, late_arrived))]
    mine += [_chip_sum(s, r, me_arr, "chip_sum_%d" % (3 + k)) for k, (s, r) in enumerate(zip(early_sums, early_arrived))]
    other = _pair_share(mine)

    d_ba = d_ba_p.reshape(HEADS, LANES)[:, 0:64].reshape(1, 256)
    small_local = _pack_small(d_ffn1_g, d_mix_g, d_ffn2_g, d_final_g, d_ret_g, d_gla_g, d_ba, g_wa2_p[0], loss_blk)
    small_sum = _small_allreduce(small_local)
    loss = small_sum[6, 0]

    def rows(n1, nm, n2, nf, nret, ngla, ba, wa2):
        return [n1, nm, n2, nf.reshape(1, D_MODEL), nret, ngla, ba, wa2.reshape(GATE_RANK, 64)]

    small = _small_update(
        small_sum, me_arr,
        rows(ffn1_norm_g, mix_norm_g, ffn2_norm_g, final_norm_g, ret_norm_g, gla_norm_g, gla_b_a, gla_w_a2),
        rows(m_ffn1_norm_g, m_mix_norm_g, m_ffn2_norm_g, m_final_norm_g, m_ret_norm_g, m_gla_norm_g, m_gla_b_a,
             m_gla_w_a2),
        rows(v_ffn1_norm_g, v_mix_norm_g, v_ffn2_norm_g, v_final_norm_g, v_ret_norm_g, v_gla_norm_g, v_gla_b_a,
             v_gla_w_a2))
    s_grad, s_delta, s_m, s_v = [
        [*o[0:3], o[3].reshape(D_MODEL), *o[4:7], o[7].reshape(1, GATE_RANK, 64)] for o in small]

    def big(k, w, m, v, name, to_2d, from_2d):
        outs4 = _adamw_halves(to_2d(w), mine[k], other[k], to_2d(m), to_2d(v), c_arr, name)
        return [from_2d(z) for z in outs4]

    plain = (lambda w: w[0], lambda z: z[None])
    transposed = (lambda w: w[0].T, lambda z: z.T[None])
    in_proj = (lambda w: pad_rows(w[0].T), lambda z: z[0:IN_SHARD].T[None])
    r_g1 = big(0, ffn1_w_gate, m_ffn1_w_gate, v_ffn1_w_gate, "adamw_ffn1_gate", *transposed)
    r_u1 = big(1, ffn1_w_up, m_ffn1_w_up, v_ffn1_w_up, "adamw_ffn1_up", *transposed)
    r_d1 = big(2, ffn1_w_down, m_ffn1_w_down, v_ffn1_w_down, "adamw_ffn1_down", *plain)
    r_g2 = big(3, ffn2_w_gate, m_ffn2_w_gate, v_ffn2_w_gate, "adamw_ffn2_gate", *transposed)
    r_u2 = big(4, ffn2_w_up, m_ffn2_w_up, v_ffn2_w_up, "adamw_ffn2_up", *transposed)
    r_d2 = big(5, ffn2_w_down, m_ffn2_w_down, v_ffn2_w_down, "adamw_ffn2_down", *plain)
    r_in = big(6, w_in, m_w_in, v_w_in, "adamw_w_in", *in_proj)
    r_out = big(7, w_out, m_w_out, v_w_out, "adamw_w_out", *plain)

    def leaves(k, smalls):
        n1, nm, n2, nf, nret, ngla, ba, wa2 = smalls
        return [n1, r_g1[k], r_u1[k], r_d1[k], nm, r_in[k], nret, wa2, ba, ngla, r_out[k], n2, r_g2[k], r_u2[k], r_d2[k], nf]

    outs = [loss, grad_x.reshape(x.shape)]
    outs += leaves(0, s_grad) + leaves(1, s_delta) + leaves(2, s_m) + leaves(3, s_v)
    return tuple(outs)
```

```python
import functools

import jax
import jax.numpy as jnp
from jax import lax
from jax.experimental import pallas as pl
from jax.experimental.pallas import tpu as pltpu

F32, BF16 = jnp.float32, jnp.bfloat16
MESH = pl.DeviceIdType.MESH
ANY = pl.BlockSpec(memory_space=pl.ANY)

D_MODEL = 1024
D_FF = 2816
N_CHIPS = 4
FF_SHARD = D_FF // N_CHIPS
IN_WIDTH = 3088
IN_SHARD = IN_WIDTH // N_CHIPS
IN_ROWS = 800
CHUNK = 64
HEADS = 4
LANES = 128
PROJ_P = 3072 + LANES
PROJ_TILES = 5
GATE_RANK = 16
QK_SCALE = 0.125
GATE_NORM = 16.0
RMS_EPS = 1e-6
ROPE_BASE = 10000.0
ADAM_LR, ADAM_B1, ADAM_B2, ADAM_EPS, ADAM_WD, ADAM_STEP = 0.001, 0.9, 0.999, 1e-08, 0.01, 10
SMALL_ROWS = 32
TOKEN_TILE = 512
ATTN_TILE = 512

_ARB2 = pltpu.CompilerParams(dimension_semantics=("arbitrary", "arbitrary"))
_ARB1 = pltpu.CompilerParams(dimension_semantics=("arbitrary",))
_ARB3 = pltpu.CompilerParams(dimension_semantics=("arbitrary", "arbitrary", "arbitrary"))


def _dot(a, b):
    return jnp.dot(a, b, preferred_element_type=F32)


def _dot_nt(a, b):
    return lax.dot_general(a, b, (((1,), (1,)), ((), ())), preferred_element_type=F32)


def _dot_tn(a, b):
    return lax.dot_general(a, b, (((0,), (0,)), ((), ())), preferred_element_type=F32)


def _rms_scale(xv):
    return lax.rsqrt(jnp.mean(xv * xv, axis=-1, keepdims=True) + RMS_EPS)


def _rms_bwd(dh, xv, g):
    r = _rms_scale(xv)
    xhat = xv * r
    dxhat = dh * g
    dx = r * (dxhat - xhat * jnp.mean(dxhat * xhat, axis=-1, keepdims=True))
    return dx, jnp.sum(dh * xhat, axis=0, keepdims=True)


def _silu_grad(a, sg):
    return sg * (1.0 + a * (1.0 - sg))


class _Hosted:
    def __init__(self, arrays, out_shapes, n_sems, start, finish, middle=None):
        self.arrays, self.out_shapes, self.n_sems = list(arrays), list(out_shapes), n_sems
        self.start, self.finish = start, finish
        self.middle = middle if middle is not None else (lambda *refs: None)


def _call(body, args, *, name, grid, in_specs, out_specs, out_shape, scratch_shapes, compiler_params, hosted=None):
    if hosted is None:
        outs = pl.pallas_call(body, name=name, grid=grid, in_specs=in_specs, out_specs=out_specs, out_shape=out_shape,
                              scratch_shapes=scratch_shapes, compiler_params=compiler_params)(*args)
        return list(outs), []
    n_in, n_out, n_sc, nh = len(in_specs), len(out_specs), len(scratch_shapes), len(hosted.arrays)

    def wrapped(*refs):
        ins, h_in = refs[:n_in], refs[n_in:n_in + nh]
        outs, h_out = refs[n_in + nh:n_in + nh + n_out], refs[n_in + nh + n_out:n_in + 2 * nh + n_out]
        rest = refs[n_in + 2 * nh + n_out:]
        scratch, (send_sems, recv_sems) = rest[:n_sc], rest[n_sc:]
        step = functools.reduce(lambda flat, d: flat * grid[d] + pl.program_id(d), range(len(grid)), 0)
        total = functools.reduce(lambda a, b: a * b, grid)

        @pl.when(step == 0)
        def _():
            hosted.start(h_in, h_out, send_sems, recv_sems)

        @pl.when(step == total // 2)
        def _():
            hosted.middle(h_in, h_out, send_sems, recv_sems)

        body(*ins, *outs, *scratch)
        last = step == total - 1

        @pl.when(last)
        def _():
            hosted.finish(h_in, h_out, send_sems, recv_sems)

    sems = [pltpu.SemaphoreType.DMA((hosted.n_sems,)), pltpu.SemaphoreType.DMA((hosted.n_sems,))]
    outs = pl.pallas_call(
        wrapped, name=name, grid=grid, in_specs=list(in_specs) + [ANY] * nh, out_specs=list(out_specs) + [ANY] * nh,
        out_shape=list(out_shape) + hosted.out_shapes, scratch_shapes=list(scratch_shapes) + sems,
        compiler_params=compiler_params)(*args, *hosted.arrays)
    return list(outs[:n_out]), list(outs[n_out:])


def _run_hosted(hosted, name):
    nh = len(hosted.arrays)

    def body(*refs):
        h_in, h_out, (send_sems, recv_sems) = refs[:nh], refs[nh:2 * nh], refs[2 * nh:]
        hosted.start(h_in, h_out, send_sems, recv_sems)
        hosted.middle(h_in, h_out, send_sems, recv_sems)
        hosted.finish(h_in, h_out, send_sems, recv_sems)

    sems = [pltpu.SemaphoreType.DMA((hosted.n_sems,)), pltpu.SemaphoreType.DMA((hosted.n_sems,))]
    return list(pl.pallas_call(body, name=name, in_specs=[ANY] * nh, out_specs=[ANY] * nh,
                               out_shape=hosted.out_shapes, scratch_shapes=sems)(*hosted.arrays))


def _ffn_weight_operands(ffn_w, chunk_maps):
    if isinstance(ffn_w, (list, tuple)):
        specs = [pl.BlockSpec((None, FF_SHARD, D_MODEL), lambda *g, m=m: (m(*g), 0, 0)) for m in chunk_maps]
        return list(ffn_w), specs
    specs = [pl.BlockSpec((None, None, FF_SHARD, D_MODEL), lambda *g, m=m, k=kind: (m(*g), k, 0, 0))
             for kind, m in enumerate(chunk_maps)]
    return [ffn_w] * 3, specs


def _pipeline_items(steps):
    def cur(s):
        c = jnp.minimum(s, steps - 1)
        return c // N_CHIPS, c % N_CHIPS

    def prev(s):
        p = jnp.maximum(s - 1, 0)
        return p // N_CHIPS, p % N_CHIPS

    return cur, prev


def _ffn_fwd(x, g, ffn_w, name, hosted=None):
    t = x.shape[0]
    tm = min(t, TOKEN_TILE)

    def body(x_ref, g_ref, wg_ref, wu_ref, wd_ref, xo_ref, a_ref, u_ref, h_ref, acc_ref):
        j = pl.program_id(1)

        @pl.when(j == 0)
        def _():
            xv = x_ref[...]
            h_ref[...] = ((xv * _rms_scale(xv)) * g_ref[...]).astype(BF16)
            acc_ref[...] = jnp.zeros_like(acc_ref)

        h = h_ref[...]
        a = _dot_nt(h, wg_ref[...])
        u = _dot_nt(h, wu_ref[...])
        a_ref[...] = a.astype(BF16)
        u_ref[...] = u.astype(BF16)
        hid = (a * jax.nn.sigmoid(a)) * u
        acc_ref[...] += _dot(hid.astype(BF16), wd_ref[...])

        @pl.when(j == N_CHIPS - 1)
        def _():
            xo_ref[...] = x_ref[...] + 0.5 * acc_ref[...]

    tok = pl.BlockSpec((tm, D_MODEL), lambda i, j: (i, 0))
    act = pl.BlockSpec((None, tm, FF_SHARD), lambda i, j: (j, i, 0))
    w_arrays, weights = _ffn_weight_operands(ffn_w, [lambda i, j: j] * 3)
    return _call(
        body, (x, g, *w_arrays), name=name, grid=(t // tm, N_CHIPS),
        in_specs=[tok, pl.BlockSpec((1, D_MODEL), lambda i, j: (0, 0))] + weights,
        out_specs=[tok, act, act, tok],
        out_shape=[jax.ShapeDtypeStruct((t, D_MODEL), F32),
                   jax.ShapeDtypeStruct((N_CHIPS, t, FF_SHARD), BF16),
                   jax.ShapeDtypeStruct((N_CHIPS, t, FF_SHARD), BF16),
                   jax.ShapeDtypeStruct((t, D_MODEL), BF16)],
        scratch_shapes=[pltpu.VMEM((tm, D_MODEL), F32)],
        compiler_params=_ARB2, hosted=hosted)


def _ffn1_fwd_gathering(x, g, shard, name, hosted=None):
    t = x.shape[0]
    tm = min(t, TOKEN_TILE)
    nt = t // tm
    nh = 0 if hosted is None else len(hosted.arrays)

    def body(*refs):
        x_ref, g_ref, shard_ref = refs[0:3]
        h_in = refs[3:3 + nh]
        xo_ref, a_ref, u_ref, h_ref, wall = refs[3 + nh:8 + nh]
        h_out = refs[8 + nh:8 + 2 * nh]
        acc, h_all, wbuf, load_sems, send_sems, recv_sems = refs[8 + 2 * nh:14 + 2 * nh]
        carried_sems = refs[14 + 2 * nh:]
        k, i = pl.program_id(0), pl.program_id(1)
        legs, _ = _gather_legs(shard_ref, wall, send_sems, recv_sems, 0, True)
        begin, pass_on, _ = _gather_steps(legs, True)

        def load(chunk, src):
            return pltpu.make_async_copy(src, wbuf.at[chunk % 2], load_sems.at[chunk % 2])

        @pl.when((k == 0) & (i == 0))
        def _():
            begin()
            load(0, shard_ref).start()
            load(0, shard_ref).wait()
            if hosted is not None:
                hosted.start(h_in, h_out, *carried_sems)

        @pl.when((k == 1) & (i == 0))
        def _():
            pass_on()
            legs["pass_y"][1].wait_recv()
            load(1, wall.at[PEER_SLOT[1]]).start()
            load(1, wall.at[PEER_SLOT[1]]).wait()

        @pl.when((k == 1) & (i == nt // 2))
        def _():
            legs["pass_x"][1].wait_recv()
            load(2, wall.at[PEER_SLOT[0]]).start()

        @pl.when((k == 2) & (i == 0))
        def _():
            load(2, wall.at[PEER_SLOT[0]]).wait()
            if hosted is not None:
                hosted.middle(h_in, h_out, *carried_sems)

        @pl.when((k == 2) & (i == nt // 2))
        def _():
            legs["fwd_y"][1].wait_recv()
            legs["pass_d0"][0].start()
            legs["fwd_x"][1].wait_recv()
            legs["pass_d1"][0].start()
            legs["pass_d0"][1].wait_recv()
            legs["pass_d1"][1].wait_recv()
            load(3, wall.at[PEER_SLOT[2]]).start()

        @pl.when((k == 3) & (i == 0))
        def _():
            load(3, wall.at[PEER_SLOT[2]]).wait()

        @pl.when(k == 0)
        def _():
            xv = x_ref[...]
            h0 = ((xv * _rms_scale(xv)) * g_ref[...]).astype(BF16)
            h_all[i] = h0
            h_ref[...] = h0

        h = h_all[i]
        wg, wu, wd = (wbuf[k % 2, kind].reshape(FF_SHARD, D_MODEL) for kind in range(3))
        a = _dot_nt(h, wg)
        u = _dot_nt(h, wu)
        a_ref[...] = a.astype(BF16)
        u_ref[...] = u.astype(BF16)
        part = _dot(((a * jax.nn.sigmoid(a)) * u).astype(BF16), wd)

        @pl.when(k == 0)
        def _():
            acc[i] = part

        @pl.when(k > 0)
        def _():
            acc[i] += part

        @pl.when(k == N_CHIPS - 1)
        def _():
            xo_ref[...] = x_ref[...] + 0.5 * acc[i]

        @pl.when((k == N_CHIPS - 1) & (i == nt - 1))
        def _():
            for pair in legs.values():
                pair[0].wait_send()
            if hosted is not None:
                hosted.finish(h_in, h_out, *carried_sems)

    def first_or_last(k):
        return (k == 0) | (k == N_CHIPS - 1)

    tok = lambda keep: pl.BlockSpec((tm, D_MODEL), lambda k, i: (jnp.where(keep(k), i, 0), 0))
    act = pl.BlockSpec((None, tm, FF_SHARD), lambda k, i: (k, i, 0))
    act_shape = jax.ShapeDtypeStruct((N_CHIPS, t, FF_SHARD), BF16)
    carried = [] if hosted is None else [pltpu.SemaphoreType.DMA((hosted.n_sems,))] * 2
    outs = pl.pallas_call(
        body, name=name, grid=(N_CHIPS, nt),
        in_specs=[tok(first_or_last), pl.BlockSpec((1, D_MODEL), lambda k, i: (0, 0)), ANY] + [ANY] * nh,
        out_specs=[tok(lambda k: k == N_CHIPS - 1), act, act,
                   pl.BlockSpec((tm, D_MODEL), lambda k, i: (jnp.where(k == 0, i, nt - 1), 0)), ANY] + [ANY] * nh,
        out_shape=[jax.ShapeDtypeStruct((t, D_MODEL), F32), act_shape, act_shape,
                   jax.ShapeDtypeStruct((t, D_MODEL), BF16),
                   jax.ShapeDtypeStruct((N_CHIPS,) + shard.shape, shard.dtype)]
                  + ([] if hosted is None else hosted.out_shapes),
        scratch_shapes=[pltpu.VMEM((nt, tm, D_MODEL), F32), pltpu.VMEM((nt, tm, D_MODEL), BF16),
                        pltpu.VMEM((2,) + shard.shape, shard.dtype), pltpu.SemaphoreType.DMA((2,)),
                        pltpu.SemaphoreType.DMA((8,)), pltpu.SemaphoreType.DMA((8,))] + carried,
        compiler_params=_ARB2,
    )(x, g, shard, *([] if hosted is None else hosted.arrays))
    return list(outs[:5]), list(outs[5:])


def _ffn_bwd(dxo, x, g, a4, u4, ffn_w, name, hosted=None):
    t = x.shape[0]
    tm = min(t, TOKEN_TILE)
    steps = (t // tm) * N_CHIPS
    cur, prev = _pipeline_items(steps)


    def body(dxo_ref, dxo_prev_ref, x_ref, g_ref, a_ref, u_ref, wg_ref, wu_ref, wd_ref,
             da_ref, du_ref, hid_ref, dob_ref, dx_ref, dg_ref, acc_ref, da_slots, du_slots):
        s = pl.program_id(0)
        jc, jp = cur(s)[1], prev(s)[1]
        slot = s % 2

        @pl.when(s == 0)
        def _():
            dg_ref[...] = jnp.zeros_like(dg_ref)
            acc_ref[...] = jnp.zeros_like(acc_ref)
            da_slots[...] = jnp.zeros_like(da_slots)
            du_slots[...] = jnp.zeros_like(du_slots)

        @pl.when(jc == 0)
        def _():
            dob_ref[...] = (0.5 * dxo_ref[...]).astype(BF16)

        dhid = _dot_nt(dob_ref[...], wd_ref[...])
        a = a_ref[...].astype(F32)
        u = u_ref[...].astype(F32)
        sg = jax.nn.sigmoid(a)
        sl = a * sg
        hid_ref[...] = (sl * u).astype(BF16)
        du = (dhid * sl).astype(BF16)
        da = (dhid * u * _silu_grad(a, sg)).astype(BF16)
        du_ref[...] = du
        da_ref[...] = da
        acc_ref[...] += _dot(da_slots[1 - slot], wg_ref[...]) + _dot(du_slots[1 - slot], wu_ref[...])
        da_slots[slot] = da
        du_slots[slot] = du

        @pl.when((jp == N_CHIPS - 1) & (s > 0))
        def _():
            dx, dg = _rms_bwd(acc_ref[...], x_ref[...], g_ref[...])
            dx_ref[...] = dxo_prev_ref[...] + dx
            dg_ref[...] += dg
            acc_ref[...] = jnp.zeros_like(acc_ref)

    tok_cur = pl.BlockSpec((tm, D_MODEL), lambda s: (cur(s)[0], 0))
    tok_prev = pl.BlockSpec((tm, D_MODEL), lambda s: (prev(s)[0], 0))
    act = pl.BlockSpec((None, tm, FF_SHARD), lambda s: (cur(s)[1], cur(s)[0], 0))
    row = pl.BlockSpec((1, D_MODEL), lambda s: (0, 0))
    w_arrays, weights = _ffn_weight_operands(ffn_w, [lambda s: prev(s)[1], lambda s: prev(s)[1], lambda s: cur(s)[1]])
    act_shape = jax.ShapeDtypeStruct((N_CHIPS, t, FF_SHARD), BF16)
    return _call(
        body, (dxo, dxo, x, g, a4, u4, *w_arrays), name=name, grid=(steps + 1,),
        in_specs=[tok_cur, tok_prev, tok_prev, row, act, act] + weights,
        out_specs=[act, act, act, tok_cur, tok_prev, row],
        out_shape=[act_shape, act_shape, act_shape,
                   jax.ShapeDtypeStruct((t, D_MODEL), BF16),
                   jax.ShapeDtypeStruct((t, D_MODEL), F32),
                   jax.ShapeDtypeStruct((1, D_MODEL), F32)],
        scratch_shapes=[pltpu.VMEM((tm, D_MODEL), F32), pltpu.VMEM((2, tm, FF_SHARD), BF16),
                        pltpu.VMEM((2, tm, FF_SHARD), BF16)],
        compiler_params=_ARB1, hosted=hosted)


def _matmul_tn(a, b, name, tka=None, out_dtype=F32, hosted=None):
    a3, b3 = a.ndim == 3, b.ndim == 3
    nb = a.shape[0] if a3 else (b.shape[0] if b3 else 1)
    t, ka, n = a.shape[-2], a.shape[-1], b.shape[-1]
    tka = ka if tka is None else tka
    tk = min(t, 4 * TOKEN_TILE)
    nk = t // tk

    def body(a_ref, b_ref, o_ref, acc_ref):
        k = pl.program_id(2)

        @pl.when(k == 0)
        def _():
            acc_ref[...] = jnp.zeros_like(acc_ref)

        acc_ref[...] += _dot_tn(a_ref[...].astype(BF16), b_ref[...].astype(BF16))

        @pl.when(k == nk - 1)
        def _():
            o_ref[...] = acc_ref[...].astype(out_dtype)

    a_spec = (pl.BlockSpec((None, tk, tka), lambda i, j, k: (i, k, j)) if a3
              else pl.BlockSpec((tk, tka), lambda i, j, k: (k, j)))
    b_spec = (pl.BlockSpec((None, tk, n), lambda i, j, k: (i, k, 0)) if b3
              else pl.BlockSpec((tk, n), lambda i, j, k: (k, 0)))
    outs, carried = _call(
        body, (a, b), name=name, grid=(nb, ka // tka, t // tk),
        in_specs=[a_spec, b_spec],
        out_specs=[pl.BlockSpec((None, tka, n), lambda i, j, k: (i, j, 0))],
        out_shape=[jax.ShapeDtypeStruct((nb, ka, n), out_dtype)],
        scratch_shapes=[pltpu.VMEM((tka, n), F32)],
        compiler_params=_ARB3, hosted=hosted)
    return outs[0] if hosted is None else (outs[0], carried)


def _matmul_nt(a, w, name, out_dtype=F32):
    t, k = a.shape
    n = w.shape[0]
    tm = min(t, TOKEN_TILE)

    def body(a_ref, w_ref, o_ref):
        o_ref[...] = _dot_nt(a_ref[...].astype(BF16), w_ref[...]).astype(out_dtype)

    return pl.pallas_call(
        body, name=name, grid=(t // tm,),
        in_specs=[pl.BlockSpec((tm, k), lambda i: (i, 0)), pl.BlockSpec((n, k), lambda i: (0, 0))],
        out_specs=pl.BlockSpec((tm, n), lambda i: (i, 0)),
        out_shape=jax.ShapeDtypeStruct((t, n), out_dtype),
        compiler_params=_ARB1,
    )(a, w)


def _mixer_in_bwd(dproj, w_in_pt, dres, x, g, name):
    t, k = dproj.shape
    tm = min(t, TOKEN_TILE)

    def body(a_ref, w_ref, dres_ref, x_ref, g_ref, dx_ref, dg_ref):
        @pl.when(pl.program_id(0) == 0)
        def _():
            dg_ref[...] = jnp.zeros_like(dg_ref)

        dh = _dot(a_ref[...], w_ref[...])
        dx, dg = _rms_bwd(dh, x_ref[...], g_ref[...])
        dx_ref[...] = dres_ref[...] + dx
        dg_ref[...] += dg

    tok = pl.BlockSpec((tm, D_MODEL), lambda i: (i, 0))
    row = pl.BlockSpec((1, D_MODEL), lambda i: (0, 0))
    return pl.pallas_call(
        body, name=name, grid=(t // tm,),
        in_specs=[pl.BlockSpec((tm, k), lambda i: (i, 0)), pl.BlockSpec((k, D_MODEL), lambda i: (0, 0)), tok, tok, row],
        out_specs=[tok, row],
        out_shape=[jax.ShapeDtypeStruct((t, D_MODEL), F32), jax.ShapeDtypeStruct((1, D_MODEL), F32)],
        compiler_params=_ARB1,
    )(dproj, w_in_pt, dres, x, g)


def _mixer_in_fwd(x, g, w_in_pt, name, hosted=None):
    t = x.shape[0]
    tm = min(t, TOKEN_TILE)
    tn = PROJ_P // PROJ_TILES

    def body(x_ref, g_ref, w_ref, p_ref, h_ref):
        @pl.when(pl.program_id(1) == 0)
        def _():
            xv = x_ref[...]
            h_ref[...] = ((xv * _rms_scale(xv)) * g_ref[...]).astype(BF16)

        p_ref[...] = _dot_nt(h_ref[...], w_ref[...])

    tok = pl.BlockSpec((tm, D_MODEL), lambda i, j: (i, 0))
    return _call(
        body, (x, g, w_in_pt), name=name, grid=(t // tm, PROJ_TILES),
        in_specs=[tok, pl.BlockSpec((1, D_MODEL), lambda i, j: (0, 0)),
                  pl.BlockSpec((tn, D_MODEL), lambda i, j: (j, 0))],
        out_specs=[pl.BlockSpec((tm, tn), lambda i, j: (i, j)), tok],
        out_shape=[jax.ShapeDtypeStruct((t, PROJ_P), F32), jax.ShapeDtypeStruct((t, D_MODEL), BF16)],
        scratch_shapes=[], compiler_params=_ARB2, hosted=hosted)


def _mixer_out_fwd(o_ret, o_gla, w_out, x, name):
    t = x.shape[0]
    tm = min(t, TOKEN_TILE)
    half = HEADS * LANES

    def body(a_ref, b_ref, w_ref, x_ref, o_ref):
        o_ref[...] = x_ref[...] + _dot(a_ref[...], w_ref[0:half, :]) + _dot(b_ref[...], w_ref[half:2 * half, :])

    tok = pl.BlockSpec((tm, D_MODEL), lambda i: (i, 0))
    hb = pl.BlockSpec((tm, half), lambda i: (i, 0))
    return pl.pallas_call(
        body, name=name, grid=(t // tm,),
        in_specs=[hb, hb, pl.BlockSpec((2 * half, D_MODEL), lambda i: (0, 0)), tok],
        out_specs=tok, out_shape=jax.ShapeDtypeStruct((t, D_MODEL), F32),
        compiler_params=_ARB1,
    )(o_ret, o_gla, w_out, x)


def _final_loss(x, g, target, name):
    t = x.shape[0]
    tm = min(t, TOKEN_TILE)

    def body(x_ref, g_ref, t_ref, l_ref, dx_ref, dg_ref):
        @pl.when(pl.program_id(0) == 0)
        def _():
            l_ref[...] = jnp.zeros_like(l_ref)
            dg_ref[...] = jnp.zeros_like(dg_ref)

        xv = x_ref[...]
        gv = g_ref[...]
        err = (xv * _rms_scale(xv)) * gv - t_ref[...]
        l_ref[...] += 0.5 * jnp.sum(jnp.mean(err * err, axis=-1, keepdims=True), axis=0, keepdims=True)
        dx, dg = _rms_bwd(err * (1.0 / D_MODEL), xv, gv)
        dx_ref[...] = dx
        dg_ref[...] += dg

    tok = pl.BlockSpec((tm, D_MODEL), lambda i: (i, 0))
    row = pl.BlockSpec((1, D_MODEL), lambda i: (0, 0))
    return pl.pallas_call(
        body, name=name, grid=(t // tm,),
        in_specs=[tok, row, tok],
        out_specs=[pl.BlockSpec((8, LANES), lambda i: (0, 0)), tok, row],
        out_shape=[jax.ShapeDtypeStruct((8, LANES), F32), jax.ShapeDtypeStruct((t, D_MODEL), F32),
                   jax.ShapeDtypeStruct((1, D_MODEL), F32)],
        compiler_params=_ARB1,
    )(x, g, target)


def _rot(v, cos, sa, sb):
    return v * cos + pltpu.roll(v, 96, 1) * sa + pltpu.roll(v, 32, 1) * sb


def _rot_t(d, cos, sa, sb):
    return d * cos + pltpu.roll(d * sa, 32, 1) + pltpu.roll(d * sb, 96, 1)


def _bmm(a, b):
    return jnp.einsum("cik,ckj->cij", a, b, preferred_element_type=F32)


def _bmm_nt(a, b):
    return jnp.einsum("cik,cjk->cij", a, b, preferred_element_type=F32)


def _bmm_tn(a, b):
    return jnp.einsum("cki,ckj->cij", a, b, preferred_element_type=F32)


def _masked_sum(mask, x):
    hi = x.astype(BF16)
    r1 = x - hi.astype(F32)
    mid = r1.astype(BF16)
    lo = (r1 - mid.astype(F32)).astype(BF16)
    return _bmm(mask, hi) + _bmm(mask, mid) + _bmm(mask, lo)


PAIR = 2


def _tile_inputs(is_ret, qkvg_refs, aux, nc):
    shape3 = (nc, CHUNK, LANES)
    q_ref, k_ref, v_ref, g_ref = qkvg_refs
    low_lanes = lax.broadcasted_iota(jnp.int32, (1, LANES), 1) < 64
    ri = lax.broadcasted_iota(jnp.int32, (PAIR * nc, CHUNK, CHUNK), 1)
    ci = lax.broadcasted_iota(jnp.int32, (PAIR * nc, CHUNK, CHUNK), 2)
    qs, ks, vs, bs, gates, extra = [], [], [], [], [], []
    for hd in range(PAIR):
        q_blk, k_blk = q_ref[...], k_ref[...]
        if hd == 1:
            q_blk, k_blk = pltpu.roll(q_blk, 64, 1), pltpu.roll(k_blk, 64, 1)
        q_raw, k_raw = jnp.where(low_lanes, q_blk, 0.0), jnp.where(low_lanes, k_blk, 0.0)
        vs.append(v_ref[:, LANES * hd:LANES * (hd + 1)].reshape(shape3))
        gates.append(g_ref[:, LANES * hd:LANES * (hd + 1)])
        if is_ret:
            cos_ref, sa_ref, sb_ref, lg_ref = aux
            cos, sa, sb = cos_ref[...], sa_ref[...], sb_ref[...]
            q = _rot(q_raw, cos, sa, sb)
            k = _rot(k_raw, cos, sa, sb) * QK_SCALE
            steps = (lax.broadcasted_iota(jnp.int32, shape3, 1) + 1).astype(F32)
            bs.append(steps * lg_ref[hd])
            extra.append(jnp.exp(jnp.abs(ri[0:nc] - ci[0:nc]).astype(F32) * lg_ref[hd][:, 0:CHUNK]))
        else:
            glow_ref, wa2_ref, ba_ref = aux
            lanes = slice(LANES * hd, LANES * (hd + 1))
            logit = _dot(glow_ref[...].astype(BF16), wa2_ref[:, lanes]) + ba_ref[:, lanes]
            la = (jnp.minimum(logit, 0.0) - jnp.log1p(jnp.exp(-jnp.abs(logit)))) * (1.0 / GATE_NORM)
            bs.append(_masked_sum((ci[0:nc] <= ri[0:nc]).astype(BF16), la.reshape(shape3)))
            extra.append(logit)
            q = q_raw * QK_SCALE
            k = k_raw
        qs.append(q.reshape(shape3))
        ks.append(k.reshape(shape3))
    cat = lambda parts: jnp.concatenate(parts, axis=0)
    return cat(qs), cat(ks), cat(vs), gates, cat(bs), extra, ri, ci


def _tile_scores(q, k, b, ri, ci):
    mid = b[:, CHUNK // 2 - 1:CHUNK // 2, :]
    ep = jnp.exp(b - mid)
    en = jnp.exp(mid - b)
    qt, kt, qh, kh = q * ep, k * en, q * en, k * ep
    low = _bmm_nt(qt.astype(BF16), kt.astype(BF16))
    upp = _bmm_nt(qh.astype(BF16), kh.astype(BF16))
    scores = jnp.where(ci <= ri, low, upp)
    return scores, ep, en, qt, kt, qh, kh


def _attn_specs(is_ret, t, tb, imap_t):
    nb = t // tb
    base = 0 if is_ret else 12
    wide = PAIR * LANES
    proj = [pl.BlockSpec((tb, LANES), lambda p, i: (imap_t(i), base + p)),
            pl.BlockSpec((tb, LANES), lambda p, i: (imap_t(i), base + 2 + p)),
            pl.BlockSpec((tb, wide), lambda p, i: (imap_t(i), (base + 4) // 2 + p)),
            pl.BlockSpec((tb, wide), lambda p, i: (imap_t(i), (base + 8) // 2 + p))]
    lane_t = pl.BlockSpec((tb, LANES), lambda p, i: (imap_t(i), 0))
    if is_ret:
        aux = [lane_t, lane_t, lane_t, pl.BlockSpec((PAIR, 1, LANES), lambda p, i: (p, 0, 0))]
    else:
        aux = [pl.BlockSpec((tb, LANES), lambda p, i: (imap_t(i), PROJ_P // LANES - 1)),
               pl.BlockSpec((LANES, wide), lambda p, i: (0, p)),
               pl.BlockSpec((1, wide), lambda p, i: (0, p))]
    gain = pl.BlockSpec((1, wide), lambda p, i: (0, p))
    pair_t = pl.BlockSpec((tb, wide), lambda p, i: (imap_t(i), p))
    narrow_t = pl.BlockSpec((tb, LANES), lambda p, i: (imap_t(i), p))
    state = pl.BlockSpec((PAIR, tb // CHUNK, LANES, LANES), lambda p, i: (p, imap_t(i), 0, 0))
    return nb, proj, aux, gain, pair_t, narrow_t, state


def _attn_fwd(is_ret, proj, aux_arrays, gain, name, hosted=None):
    t = proj.shape[0]
    tb = min(t, ATTN_TILE)
    nc = tb // CHUNK
    n_aux = 4 if is_ret else 3
    nb, proj_spec, aux_specs, gain_spec, pair_t, _, state_spec = _attn_specs(is_ret, t, tb, lambda i: i)

    def body(*refs):
        qkvg_refs = refs[0:4]
        aux = refs[4:4 + n_aux]
        gn_ref, ofin_ref, oraw_ref, st_ref, state = refs[4 + n_aux:]

        @pl.when(pl.program_id(1) == 0)
        def _():
            state[...] = jnp.zeros_like(state)

        q, k, v, gates, b, extra, ri, ci = _tile_inputs(is_ret, qkvg_refs, aux, nc)
        if is_ret:
            scores = _bmm_nt(q.astype(BF16), k.astype(BF16)) * jnp.concatenate(extra, axis=0)
        else:
            scores = _tile_scores(q, k, b, ri, ci)[0]
        vb = v.astype(BF16)
        intra = _bmm(scores.astype(BF16), vb)
        b_last = b[:, CHUNK - 1:CHUNK, :]
        e_last = jnp.exp(b_last)
        grow = _bmm_tn(vb, (k * jnp.exp(b_last - b)).astype(BF16))
        for hd in range(PAIR):
            st = state[hd]
            for c in range(nc):
                st_ref[hd, c] = st
                st = st * e_last[hd * nc + c] + grow[hd * nc + c]
            state[hd] = st
        starts = st_ref[...].reshape(PAIR * nc, LANES, LANES)
        out3 = intra + _bmm_nt((q * jnp.exp(b)).astype(BF16), starts.astype(BF16))
        for hd in range(PAIR):
            lanes = slice(LANES * hd, LANES * (hd + 1))
            out = out3[hd * nc:(hd + 1) * nc].reshape(tb, LANES)
            oraw_ref[:, lanes] = out
            normed = out * _rms_scale(out)
            gate = gates[hd]
            ofin_ref[:, lanes] = ((normed * gn_ref[:, lanes]) * (gate * jax.nn.sigmoid(gate))).astype(BF16)

    width = HEADS * LANES
    return _call(
        body, (proj, proj, proj, proj, *aux_arrays, gain), name=name, grid=(HEADS // PAIR, nb),
        in_specs=proj_spec + aux_specs + [gain_spec],
        out_specs=[pair_t, pair_t, state_spec],
        out_shape=[jax.ShapeDtypeStruct((t, width), BF16), jax.ShapeDtypeStruct((t, width), F32),
                   jax.ShapeDtypeStruct((HEADS, t // CHUNK, LANES, LANES), F32)],
        scratch_shapes=[pltpu.VMEM((PAIR, LANES, LANES), F32)],
        compiler_params=_ARB2, hosted=hosted)


def _attn_bwd(is_ret, proj, aux_arrays, gain, o_raw, states, d_out, name):
    t = proj.shape[0]
    tb = min(t, ATTN_TILE)
    nc = tb // CHUNK
    n_aux = 4 if is_ret else 3
    nblk = t // tb
    nb, proj_spec, aux_specs, gain_spec, pair_t, narrow_t, state_spec = _attn_specs(
        is_ret, t, tb, lambda i: nblk - 1 - i)
    base = 0 if is_ret else HEADS // PAIR
    dout_spec = pl.BlockSpec((tb, PAIR * LANES), lambda p, i: (nblk - 1 - i, base + p))

    def body(*refs):
        qkvg_refs = refs[0:4]
        aux = refs[4:4 + n_aux]
        gn_ref, oraw_ref, st_ref, dfin_ref = refs[4 + n_aux:8 + n_aux]
        dq_ref, dk_ref, dv_ref, dgate_ref, dgn_ref = refs[8 + n_aux:13 + n_aux]
        if is_ret:
            dstate, dafter_ref = refs[13 + n_aux:]
        else:
            dlogit_ref, dba_ref, dstate, dafter_ref = refs[13 + n_aux:]

        @pl.when(pl.program_id(1) == 0)
        def _():
            dstate[...] = jnp.zeros_like(dstate)
            dgn_ref[...] = jnp.zeros_like(dgn_ref)
            if not is_ret:
                dba_ref[...] = jnp.zeros_like(dba_ref)

        shape3 = (nc, CHUNK, LANES)
        q, k, v, gates, b, extra, ri, ci = _tile_inputs(is_ret, qkvg_refs, aux, nc)
        eb = jnp.exp(b)
        qe = q * eb
        b_last = b[:, CHUNK - 1:CHUNK, :]
        e_last = jnp.exp(b_last)
        ekd = jnp.exp(b_last - b)
        kd = k * ekd

        d_os = []
        for hd in range(PAIR):
            lanes = slice(LANES * hd, LANES * (hd + 1))
            gn, gate = gn_ref[:, lanes], gates[hd]
            out = oraw_ref[:, lanes]
            r = _rms_scale(out)
            normed = out * r
            sg = jax.nn.sigmoid(gate)
            dfin = dfin_ref[:, lanes]
            dgate_ref[:, lanes] = (dfin * (normed * gn) * _silu_grad(gate, sg)).astype(BF16)
            dpre = dfin * (gate * sg)
            dgn_ref[:, lanes] += jnp.sum(dpre * normed, axis=0, keepdims=True)
            dnormed = dpre * gn
            d_o = r * (dnormed - normed * jnp.mean(dnormed * normed, axis=-1, keepdims=True))
            d_os.append(d_o.reshape(shape3))
        dob, vb = jnp.concatenate(d_os, axis=0).astype(BF16), v.astype(BF16)

        dgrow = _bmm_tn(dob, qe.astype(BF16))
        for hd in range(PAIR):
            dst = dstate[hd]
            for c in reversed(range(nc)):
                dafter_ref[hd * nc + c] = dst
                dst = dst * e_last[hd * nc + c] + dgrow[hd * nc + c]
            dstate[hd] = dst
        st = st_ref[...].reshape(PAIR * nc, LANES, LANES)
        dafter = dafter_ref[...]
        stb, dafter_b = st.astype(BF16), dafter.astype(BF16)

        dsc = _bmm_nt(dob, vb)
        dsc_t = _bmm_nt(vb, dob)
        dqe = _bmm(dob, stb)
        dkd = _bmm(vb, dafter_b)
        if is_ret:
            decay, qb, kb = jnp.concatenate(extra, axis=0), q.astype(BF16), k.astype(BF16)
            scores_t = _bmm_nt(kb, qb) * decay
            dq = _bmm((dsc * decay).astype(BF16), kb) + dqe * eb
            dk = _bmm((dsc_t * decay).astype(BF16), qb) + dkd * ekd
        else:
            _, ep, en, qt, kt, qh, kh = _tile_scores(q, k, b, ri, ci)
            qtb, ktb, qhb, khb = qt.astype(BF16), kt.astype(BF16), qh.astype(BF16), kh.astype(BF16)
            scores_t = jnp.where(ci >= ri, _bmm_nt(ktb, qtb), _bmm_nt(khb, qhb))
            dqt = _bmm(jnp.where(ci <= ri, dsc, 0.0).astype(BF16), ktb)
            dqh = _bmm(jnp.where(ci <= ri, 0.0, dsc).astype(BF16), khb)
            dkt = _bmm(jnp.where(ci >= ri, dsc_t, 0.0).astype(BF16), qtb)
            dkh = _bmm(jnp.where(ci >= ri, 0.0, dsc_t).astype(BF16), qhb)
            dq = dqt * ep + dqh * en + dqe * eb
            dk = dkt * en + dkh * ep + dkd * ekd
        dv = _bmm(scores_t.astype(BF16), dob) + _bmm_nt(kd.astype(BF16), dafter_b)

        if not is_ret:
            db = dqt * qt - dkt * kt - dqh * qh + dkh * kh + dqe * qe - dkd * kd
            db_last = (jnp.sum(dkd * kd, axis=1, keepdims=True)
                       + jnp.sum(dafter * st, axis=1, keepdims=True) * e_last)
            last_row = lax.broadcasted_iota(jnp.int32, (PAIR * nc, CHUNK, LANES), 1) == CHUNK - 1
            db = db + jnp.where(last_row, db_last, 0.0)
            dla = _masked_sum((ci >= ri).astype(BF16), db)

        dq_pair, dk_pair = [], []
        for hd in range(PAIR):
            lanes = slice(LANES * hd, LANES * (hd + 1))
            rows3 = slice(hd * nc, (hd + 1) * nc)
            dq_h, dk_h = dq[rows3].reshape(tb, LANES), dk[rows3].reshape(tb, LANES)
            if is_ret:
                cos_ref, sa_ref, sb_ref, _ = aux
                cos, sa, sb = cos_ref[...], sa_ref[...], sb_ref[...]
                dq_h = _rot_t(dq_h, cos, sa, sb)
                dk_h = _rot_t(dk_h, cos, sa, sb) * QK_SCALE
            else:
                dq_h = dq_h * QK_SCALE
                dlogit = dla[rows3].reshape(tb, LANES) * (1.0 / GATE_NORM) * jax.nn.sigmoid(-extra[hd])
                dlogit_ref[:, lanes] = dlogit.astype(BF16)
                dba_ref[:, lanes] += jnp.sum(dlogit, axis=0, keepdims=True)
            dq_pair.append(dq_h)
            dk_pair.append(dk_h)
            dv_ref[:, lanes] = dv[rows3].reshape(tb, LANES).astype(BF16)
        dq_ref[...] = (dq_pair[0] + pltpu.roll(dq_pair[1], 64, 1)).astype(BF16)
        dk_ref[...] = (dk_pair[0] + pltpu.roll(dk_pair[1], 64, 1)).astype(BF16)

    width = HEADS * LANES
    row_out = pl.BlockSpec((1, PAIR * LANES), lambda p, i: (0, p))
    out_specs = [narrow_t, narrow_t, pair_t, pair_t, row_out]
    out_shape = ([jax.ShapeDtypeStruct((t, width // 2), BF16)] * 2 + [jax.ShapeDtypeStruct((t, width), BF16)] * 2
                 + [jax.ShapeDtypeStruct((1, width), F32)])
    if not is_ret:
        out_specs += [pair_t, row_out]
        out_shape += [jax.ShapeDtypeStruct((t, width), BF16), jax.ShapeDtypeStruct((1, width), F32)]
    return pl.pallas_call(
        body, name=name, grid=(HEADS // PAIR, nblk),
        in_specs=proj_spec + aux_specs + [gain_spec, pair_t, state_spec, dout_spec],
        out_specs=out_specs, out_shape=out_shape,
        scratch_shapes=[pltpu.VMEM((PAIR, LANES, LANES), F32), pltpu.VMEM((PAIR * nc, LANES, LANES), F32)],
        compiler_params=_ARB2,
    )(proj, proj, proj, proj, *aux_arrays, gain, o_raw, states, d_out)


PEER_SLOT = (2, 1, 3)


def _place():
    x, y, c = lax.axis_index("x"), lax.axis_index("y"), lax.axis_index("c")
    chips = [(1 - x, y), (x, 1 - y), (1 - x, 1 - y)]
    return x, y, c, 2 * x + y, chips


def _route_split(rows, dtype):
    tile = 16 if dtype == BF16 else 8
    if rows < 2 * tile:
        return None
    return -(-(rows // 2) // tile) * tile


def _routes(by_peer):
    x, y, c, me, chips = _place()
    (xx, xy), (yx, yy), (dx, dy) = chips
    if by_peer:
        slots = dict(own=0, from_x=PEER_SLOT[0], from_y=PEER_SLOT[1], diag=PEER_SLOT[2],
                     mine_on_x=PEER_SLOT[0], mine_on_y=PEER_SLOT[1])
    else:
        slots = dict(own=me, from_x=2 * xx + xy, from_y=2 * yx + yy, diag=2 * dx + dy, mine_on_x=me, mine_on_y=me)
    return c, (xx, xy, c), (yx, yy, c), (dx, dy, c), (x, y, 1 - c), slots


def _gather_legs(src, out, send_sems, recv_sems, base, by_peer):
    c, to_x, to_y, to_d, sibling, s = _routes(by_peer)
    r0 = _route_split(src.shape[2], src.dtype)

    def cp(k, src_ref, dst_ref, to):
        return pltpu.make_async_remote_copy(src_ref=src_ref, dst_ref=dst_ref, send_sem=send_sems.at[base + k],
                                            recv_sem=recv_sems.at[base + k], device_id=to, device_id_type=MESH)

    mine = src.at[:, c]
    legs = dict(
        x=(cp(0, mine, out.at[s["mine_on_x"], :, c], to_x), cp(0, mine, out.at[s["from_x"], :, c], to_x)),
        y=(cp(1, mine, out.at[s["mine_on_y"], :, c], to_y), cp(1, mine, out.at[s["from_y"], :, c], to_y)),
        pass_x=(cp(4, out.at[s["from_x"], :, c], out.at[s["from_x"], :, c], sibling),
                cp(4, mine, out.at[s["from_x"], :, 1 - c], sibling)),
        pass_y=(cp(5, out.at[s["from_y"], :, c], out.at[s["from_y"], :, c], sibling),
                cp(5, mine, out.at[s["from_y"], :, 1 - c], sibling)))
    if r0 is None:
        mine_on_d = s["diag"] if by_peer else s["own"]
        legs["d"] = (cp(2, mine, out.at[mine_on_d, :, c], to_d), cp(2, mine, out.at[s["diag"], :, c], to_d))
        legs["pass_d"] = (cp(6, out.at[s["diag"], :, c], out.at[s["diag"], :, c], sibling),
                          cp(6, mine, out.at[s["diag"], :, 1 - c], sibling))
        return legs, False
    lo, hi = pl.ds(0, r0), pl.ds(r0, src.shape[2] - r0)
    fx_on_y = s["diag"] if by_peer else s["from_x"]
    fy_on_x = s["diag"] if by_peer else s["from_y"]
    legs.update(
        fwd_y=(cp(2, out.at[s["from_x"], :, c, lo], out.at[fx_on_y, :, c, lo], to_y),
               cp(2, mine.at[:, lo], out.at[s["diag"], :, c, lo], to_y)),
        fwd_x=(cp(3, out.at[s["from_y"], :, c, hi], out.at[fy_on_x, :, c, hi], to_x),
               cp(3, mine.at[:, hi], out.at[s["diag"], :, c, hi], to_x)),
        pass_d0=(cp(6, out.at[s["diag"], :, c, lo], out.at[s["diag"], :, c, lo], sibling),
                 cp(6, mine.at[:, lo], out.at[s["diag"], :, 1 - c, lo], sibling)),
        pass_d1=(cp(7, out.at[s["diag"], :, c, hi], out.at[s["diag"], :, c, hi], sibling),
                 cp(7, mine.at[:, hi], out.at[s["diag"], :, 1 - c, hi], sibling)))
    return legs, True


def _gather_steps(legs, routed):
    def start():
        legs["x"][0].start()
        legs["y"][0].start()
        if not routed:
            legs["d"][0].start()

    def middle():
        legs["x"][1].wait_recv()
        if routed:
            legs["fwd_y"][0].start()
        legs["pass_x"][0].start()
        legs["y"][1].wait_recv()
        if routed:
            legs["fwd_x"][0].start()
        legs["pass_y"][0].start()

    def finish():
        last = ["pass_d0", "pass_d1"] if routed else ["pass_d"]
        if routed:
            legs["fwd_y"][1].wait_recv()
            legs["pass_d0"][0].start()
            legs["fwd_x"][1].wait_recv()
            legs["pass_d1"][0].start()
        else:
            legs["d"][1].wait_recv()
            legs["pass_d"][0].start()
        for name in ["pass_x", "pass_y"] + last:
            legs[name][1].wait_recv()
        for name in ["x", "y", "pass_x", "pass_y"] + last + (["fwd_y", "fwd_x"] if routed else ["d"]):
            legs[name][0].wait_send()

    return start, middle, finish


def _gather_plan(arrs):
    na = len(arrs)

    def steps(ins, outs, send_sems, recv_sems):
        return [_gather_steps(*_gather_legs(ins[a], outs[a], send_sems, recv_sems, 8 * a, False)) for a in range(na)]

    def run(which):
        def hook(*refs):
            for step in steps(*refs):
                step[which]()
        return hook

    return _Hosted(arrs, [jax.ShapeDtypeStruct((N_CHIPS,) + a.shape, a.dtype) for a in arrs], 8 * na,
                   run(0), run(2), middle=run(1))


def _pair_exchange(grads, name):
    na = len(grads)

    def body(*refs):
        ins, outs = refs[:na], refs[na:2 * na]
        send_sems, recv_sems = refs[2 * na:]
        x, y, c, _, _ = _place()
        copies = [pltpu.make_async_remote_copy(
            src_ref=ins[a].at[:, 1 - c], dst_ref=outs[a], send_sem=send_sems.at[a], recv_sem=recv_sems.at[a],
            device_id=(x, y, 1 - c), device_id_type=MESH) for a in range(na)]
        for cp in copies:
            cp.start()
        for cp in copies:
            cp.wait()

    return pl.pallas_call(
        body, name=name,
        in_specs=[ANY] * na, out_specs=[ANY] * na,
        out_shape=[jax.ShapeDtypeStruct(g.shape[:1] + g.shape[2:], g.dtype) for g in grads],
        scratch_shapes=[pltpu.SemaphoreType.DMA((na,)), pltpu.SemaphoreType.DMA((na,))],
    )(*grads)


def _pair_add(grad, recv, c_arr, name):
    _, _, r, cols = grad.shape

    def body(c_ref, g_ref, r_ref, o_ref):
        o_ref[...] = (g_ref[...].astype(F32) + r_ref[...].astype(F32)).astype(BF16)

    return pl.pallas_call(
        body, name=name,
        grid_spec=pltpu.PrefetchScalarGridSpec(
            num_scalar_prefetch=1, grid=(N_CHIPS,),
            in_specs=[pl.BlockSpec((None, None, r, cols), lambda p, c_ref: (p, c_ref[0], 0, 0)),
                      pl.BlockSpec((None, r, cols), lambda p, c_ref: (p, 0, 0))],
            out_specs=pl.BlockSpec((None, r, cols), lambda p, c_ref: (p, 0, 0))),
        out_shape=jax.ShapeDtypeStruct((N_CHIPS, r, cols), BF16),
        compiler_params=_ARB1,
    )(c_arr, grad, recv)


def _chip_exchange_plan(sums, by_peer=False):
    na = len(sums)

    def copies(ins, outs, send_sems, recv_sems):
        x, y, c, me, chips = _place()

        def copy(a, j, px, py, block, slot):
            return pltpu.make_async_remote_copy(
                src_ref=ins[a].at[block], dst_ref=outs[a].at[slot],
                send_sem=send_sems.at[3 * a + j], recv_sem=recv_sems.at[3 * a + j],
                device_id=(px, py, c), device_id_type=MESH)

        peers = [(a, j, px, py) for a in range(na) for j, (px, py) in enumerate(chips)]
        return me, peers, copy

    def start(*refs):
        me, peers, copy = copies(*refs)
        for a, j, px, py in peers:
            if by_peer:
                copy(a, j, px, py, PEER_SLOT[j], PEER_SLOT[j]).start()
            else:
                copy(a, j, px, py, 2 * px + py, me).start()

    def finish(*refs):
        me, peers, copy = copies(*refs)
        for a, j, px, py in peers:
            if by_peer:
                copy(a, j, px, py, PEER_SLOT[j], PEER_SLOT[j]).wait_recv()
            else:
                copy(a, j, px, py, me, 2 * px + py).wait_recv()
        for a, j, px, py in peers:
            if by_peer:
                copy(a, j, px, py, PEER_SLOT[j], PEER_SLOT[j]).wait_send()
            else:
                copy(a, j, px, py, 2 * px + py, me).wait_send()

    return _Hosted(sums, [jax.ShapeDtypeStruct(s.shape, s.dtype) for s in sums], 3 * na, start, finish)


def _chip_sum(own, recv, me_arr, name):
    _, r, cols = recv.shape

    def body(me_ref, own_ref, r_ref, o_ref):
        o_ref[...] = jnp.zeros_like(o_ref)
        for q in range(N_CHIPS):
            @pl.when(me_ref[0] == q)
            def _():
                o_ref[...] += own_ref[...].astype(F32)

            @pl.when(me_ref[0] != q)
            def _():
                o_ref[...] += r_ref[q].astype(F32)

    return pl.pallas_call(
        body, name=name,
        grid_spec=pltpu.PrefetchScalarGridSpec(
            num_scalar_prefetch=1, grid=(1,),
            in_specs=[pl.BlockSpec((None, r, cols), lambda i, me_ref: (me_ref[0], 0, 0)),
                      pl.BlockSpec((N_CHIPS, r, cols), lambda i, me_ref: (0, 0, 0))],
            out_specs=pl.BlockSpec((r, cols), lambda i, me_ref: (0, 0))),
        out_shape=jax.ShapeDtypeStruct((r, cols), F32),
        compiler_params=_ARB1,
    )(me_arr, own, recv)


def _peer_sum(own, recv, name):
    _, r, cols = recv.shape

    def body(own_ref, r_ref, o_ref):
        acc = own_ref[...].astype(F32) + r_ref[1].astype(F32)
        acc = acc + r_ref[2].astype(F32)
        o_ref[...] = acc + r_ref[3].astype(F32)

    return pl.pallas_call(
        body, name=name, grid=(1,),
        in_specs=[pl.BlockSpec((None, r, cols), lambda i: (0, 0, 0)), pl.BlockSpec((N_CHIPS, r, cols), lambda i: (0, 0, 0))],
        out_specs=pl.BlockSpec((r, cols), lambda i: (0, 0)),
        out_shape=jax.ShapeDtypeStruct((r, cols), F32),
        compiler_params=_ARB1,
    )(own, recv)


def _pair_share(halves):
    na = len(halves)

    def body(*refs):
        ins, outs = refs[:na], refs[na:2 * na]
        send_sems, recv_sems = refs[2 * na:]
        x, y, c, _, _ = _place()
        copies = [pltpu.make_async_remote_copy(
            src_ref=ins[a], dst_ref=outs[a], send_sem=send_sems.at[a], recv_sem=recv_sems.at[a],
            device_id=(x, y, 1 - c), device_id_type=MESH) for a in range(na)]
        for cp in copies:
            cp.start()
        for cp in copies:
            cp.wait()

    return pl.pallas_call(
        body, name="pair_share",
        in_specs=[ANY] * na, out_specs=[ANY] * na,
        out_shape=[jax.ShapeDtypeStruct(h.shape, h.dtype) for h in halves],
        scratch_shapes=[pltpu.SemaphoreType.DMA((na,)), pltpu.SemaphoreType.DMA((na,))],
    )(*halves)


def _small_allreduce(block):
    m, n = block.shape

    def body(x_ref, all_ref, sum_ref, send_sems, recv_sems, local_sem):
        x, y, c, _, chips = _place()
        me, sibling = (x, y, c), (x, y, 1 - c)

        def rows(px, py, pc):
            return all_ref.at[pl.ds((4 * px + 2 * py + pc) * m, m), :]

        def copy(k, blk, to, src=None):
            return pltpu.make_async_remote_copy(
                src_ref=rows(*blk) if src is None else src, dst_ref=rows(*blk),
                send_sem=send_sems.at[k], recv_sem=recv_sems.at[k], device_id=to, device_id_type=MESH)

        mine = pltpu.make_async_copy(x_ref, rows(*me), local_sem)
        mine.start()
        first = [copy(0, me, sibling, src=x_ref)]
        first += [copy(1 + j, me, (*chip, c), src=x_ref) for j, chip in enumerate(chips)]
        for cp in first:
            cp.start()
        passed = [copy(4 + j, (*chip, c), sibling) for j, chip in enumerate(chips)]
        for j, chip in enumerate(chips):
            copy(1 + j, (*chip, c), me).wait_recv()
            passed[j].start()
        copy(0, sibling, me).wait_recv()
        for j, chip in enumerate(chips):
            copy(4 + j, (*chip, 1 - c), me).wait_recv()
        for cp in first + passed:
            cp.wait_send()
        mine.wait()
        acc = all_ref[0:m, :]
        for d in range(1, 8):
            acc = acc + all_ref[d * m:(d + 1) * m, :]
        sum_ref[...] = acc

    vmem = pl.BlockSpec(memory_space=pltpu.VMEM)
    return pl.pallas_call(
        body, name="small_allreduce",
        in_specs=[vmem], out_specs=[vmem, vmem],
        out_shape=[jax.ShapeDtypeStruct((8 * m, n), F32), jax.ShapeDtypeStruct((m, n), F32)],
        scratch_shapes=[pltpu.SemaphoreType.DMA((7,)), pltpu.SemaphoreType.DMA((7,)), pltpu.SemaphoreType.DMA],
    )(block)[1]


def _row_tile(rows):
    best = rows
    for cand in range(8, min(rows, 512) + 1, 8):
        if rows % cand == 0:
            best = cand
    return best


def _adamw_math(w, g, m, v):
    m2 = ADAM_B1 * m + (1.0 - ADAM_B1) * g
    v2 = ADAM_B2 * v + (1.0 - ADAM_B2) * (g * g)
    m_hat = m2 / (1.0 - ADAM_B1 ** ADAM_STEP)
    v_hat = v2 / (1.0 - ADAM_B2 ** ADAM_STEP)
    return -ADAM_LR * (m_hat / (jnp.sqrt(v_hat) + ADAM_EPS) + ADAM_WD * w), m2, v2


def _adamw_halves(w, g_mine, g_other, m, v, c_arr, name):
    rows, cols = w.shape
    r = rows // 2
    tr = _row_tile(r)
    nt = r // tr

    def body(c_ref, w_ref, gm_ref, go_ref, m_ref, v_ref, g_ref, d_ref, nm_ref, nv_ref):
        gv = jnp.where(pl.program_id(0) == c_ref[0], gm_ref[...], go_ref[...])
        g_ref[...] = gv
        d_ref[...], nm_ref[...], nv_ref[...] = _adamw_math(w_ref[...], gv, m_ref[...], v_ref[...])

    full = pl.BlockSpec((tr, cols), lambda h, i, c_ref: (h * nt + i, 0))
    half = pl.BlockSpec((tr, cols), lambda h, i, c_ref: (i, 0))
    shape = jax.ShapeDtypeStruct((rows, cols), F32)
    return pl.pallas_call(
        body, name=name,
        grid_spec=pltpu.PrefetchScalarGridSpec(
            num_scalar_prefetch=1, grid=(2, nt),
            in_specs=[full, half, half, full, full], out_specs=[full] * 4),
        out_shape=[shape] * 4,
        compiler_params=_ARB2,
    )(c_arr, w, g_mine, g_other, m, v)


def _adamw(w, g, m, v, name):
    rows, cols = w.shape
    tr = _row_tile(rows)

    def body(w_ref, g_ref, m_ref, v_ref, d_ref, nm_ref, nv_ref):
        d_ref[...], nm_ref[...], nv_ref[...] = _adamw_math(w_ref[...], g_ref[...], m_ref[...], v_ref[...])

    spec = pl.BlockSpec((tr, cols), lambda i: (i, 0))
    shape = jax.ShapeDtypeStruct((rows, cols), F32)
    return pl.pallas_call(
        body, name=name, grid=(rows // tr,),
        in_specs=[spec] * 4, out_specs=[spec] * 3, out_shape=[shape] * 3,
        compiler_params=_ARB1,
    )(w, g, m, v)


def _pad_w_in_t(w_in_t):
    return jnp.pad(w_in_t, ((0, PROJ_P - IN_WIDTH), (0, 0)))


def _unpad_w_in_t(w_pt):
    return w_pt[0:IN_WIDTH]


def _rope_tables(t):
    half = 32
    inv = ROPE_BASE ** (-jnp.arange(half, dtype=F32) * 2.0 / 64)
    ang = jnp.arange(t, dtype=F32)[:, None] * inv[None, :]
    cos, sin = jnp.cos(ang), jnp.sin(ang)
    z32, z64 = jnp.zeros((t, 32), F32), jnp.zeros((t, 64), F32)
    return (jnp.concatenate([cos, cos, z64], axis=1),
            jnp.concatenate([-sin, z32, z64], axis=1),
            jnp.concatenate([z32, sin, z64], axis=1))


def _halves(w):
    n, rows, cols = w.shape
    return w.reshape(n, 2, rows // 2, cols)


_VMEM = pl.BlockSpec(memory_space=pltpu.VMEM)


def _pack_small(n1, nm, n2, nf, nret, ngla, ba, wa2_p, loss_blk):
    def body(n1_ref, nm_ref, n2_ref, nf_ref, nret_ref, ngla_ref, ba_ref, wa2_ref, loss_ref, o_ref):
        o_ref[...] = jnp.zeros_like(o_ref)
        o_ref[0:1, :] = n1_ref[...]
        o_ref[1:2, :] = nm_ref[...]
        o_ref[2:3, :] = n2_ref[...]
        o_ref[3:4, :] = nf_ref[...]
        o_ref[4:5, 0:512] = nret_ref[...]
        o_ref[4:5, 512:1024] = ngla_ref[...]
        o_ref[5:6, 0:256] = ba_ref[...]
        o_ref[6:7, 0:LANES] = loss_ref[0:1, :]
        o_ref[8:8 + GATE_RANK, 0:HEADS * LANES] = wa2_ref[0:GATE_RANK, :]

    return pl.pallas_call(
        body, name="pack_small", in_specs=[_VMEM] * 9, out_specs=_VMEM,
        out_shape=jax.ShapeDtypeStruct((SMALL_ROWS, D_MODEL), F32),
    )(n1, nm, n2, nf, nret, ngla, ba, wa2_p, loss_blk)


def _small_update(summed, chip_arr, ws, ms, vs):
    n = len(ws)

    def body(chip_ref, s_ref, *refs):
        w_refs, m_refs, v_refs = refs[0:n], refs[n:2 * n], refs[2 * n:3 * n]
        outs = refs[3 * n:]
        wa2_all = s_ref[8:8 + GATE_RANK, 0:HEADS * LANES]
        wa2_g = jnp.zeros((GATE_RANK, 64), F32)
        for p in range(N_CHIPS):
            wa2_g = jnp.where(chip_ref[0] == p, wa2_all[:, LANES * p:LANES * p + 64], wa2_g)
        grads = [s_ref[0:1, :], s_ref[1:2, :], s_ref[2:3, :], s_ref[3:4, :], s_ref[4:5, 0:512],
                 s_ref[4:5, 512:1024], s_ref[5:6, 0:256], wa2_g]
        for k in range(n):
            d, m2, v2 = _adamw_math(w_refs[k][...], grads[k], m_refs[k][...], v_refs[k][...])
            outs[k][...] = grads[k]
            outs[n + k][...] = d
            outs[2 * n + k][...] = m2
            outs[3 * n + k][...] = v2

    shapes = [jax.ShapeDtypeStruct(w.shape, F32) for w in ws] * 4
    smem = pl.BlockSpec(memory_space=pltpu.SMEM)
    outs = pl.pallas_call(
        body, name="small_update", in_specs=[smem] + [_VMEM] * (1 + 3 * n), out_specs=[_VMEM] * (4 * n),
        out_shape=shapes,
    )(chip_arr, summed, *ws, *ms, *vs)
    return outs[0:n], outs[n:2 * n], outs[2 * n:3 * n], outs[3 * n:4 * n]


def _pad_in_rows(w_t):
    return jnp.pad(w_t, ((0, IN_ROWS - IN_SHARD), (0, 0)))


def _forward_backward(xs, target, ffn1_w, rest, ba_p, ffn1_norm_g, mix_norm_g, ret_norm_g, gla_norm_g, ffn2_norm_g,
                      final_norm_g, ffn1_gather=None, rest_plan=None, rest_weights=None, ffn2_plans=None,
                      ffn2_weights=None, early=None, late=None):
    t = xs.shape[0]
    cos_t, sa_t, sb_t = _rope_tables(t)
    log_gamma = jnp.log(1.0 - 2.0 ** (-5.0 - jnp.arange(HEADS, dtype=F32)))
    lg_t = jnp.broadcast_to(log_gamma[:, None, None], (HEADS, 1, LANES))
    ret_aux = [cos_t, sa_t, sb_t, lg_t]

    if ffn1_gather is None:
        (x1, a1, u1, h1), gathered = _ffn_fwd(xs, ffn1_norm_g, ffn1_w, "ffn1_fwd", hosted=rest_plan)
    else:
        ffn1_shard, ffn1_weights = ffn1_gather
        (x1, a1, u1, h1, wall), gathered = _ffn1_fwd_gathering(xs, ffn1_norm_g, ffn1_shard, "ffn1_fwd",
                                                               hosted=rest_plan)
        ffn1_w = ffn1_weights(wall)
    ffn2_w, w_in_pt, w_out_full, wa2_p = rest if rest_plan is None else rest_weights(gathered)
    plans = [None] * 3 if ffn2_plans is None else ffn2_plans
    (proj, h_mix), got_gate = _mixer_in_fwd(x1, mix_norm_g, w_in_pt, "mixer_in_fwd", hosted=plans[0])
    gla_aux = [proj, wa2_p, ba_p]
    (o_ret, raw_ret, st_ret), got_up = _attn_fwd(True, proj, ret_aux, ret_norm_g, "ret_fwd", hosted=plans[1])
    (o_gla, raw_gla, st_gla), got_down = _attn_fwd(False, proj, gla_aux, gla_norm_g, "gla_fwd", hosted=plans[2])
    if ffn2_plans is not None:
        ffn2_w = ffn2_weights(got_gate + got_up + got_down)
    x2 = _mixer_out_fwd(o_ret, o_gla, w_out_full, x1, "mixer_out_fwd")
    (x3, a2, u2, h2), _ = _ffn_fwd(x2, ffn2_norm_g, ffn2_w, "ffn2_fwd")
    loss_blk, dx3, d_final_g = _final_loss(x3, final_norm_g, target, "final_loss")

    (da2, du2, hid2, dob2, dx2, d_ffn2_g), _ = _ffn_bwd(dx3, x2, ffn2_norm_g, a2, u2, ffn2_w, "ffn2_bwd")
    g_gate2 = _matmul_tn(da2, h2, "ffn2_dgate", out_dtype=BF16)
    g_up2 = _matmul_tn(du2, h2, "ffn2_dup", out_dtype=BF16)
    g_down2 = _matmul_tn(hid2, dob2, "ffn2_ddown", out_dtype=BF16)

    d_o = _matmul_nt(dx2, w_out_full, "mixer_out_bwd")
    g_wout_ret = _matmul_tn(o_ret, dx2, "wout_grad_ret", out_dtype=BF16)
    g_wout_gla = _matmul_tn(o_gla, dx2, "wout_grad_gla", out_dtype=BF16)
    *dproj_ret, d_ret_g = _attn_bwd(True, proj, ret_aux, ret_norm_g, raw_ret, st_ret, d_o, "ret_bwd")
    *dproj_gla, d_gla_g, dlogit, d_ba_p = _attn_bwd(False, proj, gla_aux, gla_norm_g, raw_gla, st_gla, d_o, "gla_bwd")
    d_glow = _matmul_nt(dlogit, wa2_p, "gate_low_bwd", out_dtype=BF16)
    g_wa2_p = _matmul_tn(proj[:, PROJ_P - LANES:], dlogit, "gate_w_grad")
    dproj = jnp.concatenate(dproj_ret + dproj_gla + [d_glow], axis=1)
    g_win_p = _matmul_tn(dproj, h_mix, "w_in_grad", tka=PROJ_P // PROJ_TILES, out_dtype=BF16)
    dx1, d_mix_g = _mixer_in_bwd(dproj, w_in_pt, dx2, x1, mix_norm_g, "mixer_in_bwd")
    g_win_t = _unpad_w_in_t(g_win_p[0])
    g_win = jnp.stack([_pad_in_rows(g_win_t[IN_SHARD * p:IN_SHARD * (p + 1)]) for p in range(N_CHIPS)], axis=0)
    g_wout = jnp.concatenate([g_wout_ret[0], g_wout_gla[0]], axis=0).reshape(N_CHIPS, D_MODEL // N_CHIPS, D_MODEL)

    early_plan = None if early is None else early([g_gate2, g_up2, g_down2, g_win, g_wout])
    (da1, du1, hid1, dob1, grad_x, d_ffn1_g), arrived = _ffn_bwd(dx1, xs, ffn1_norm_g, a1, u1, ffn1_w, "ffn1_bwd",
                                                                hosted=early_plan)
    late_grads, late_arrived = [], []
    for lhs, rhs, name in ((da1, h1, "ffn1_dgate"), (du1, h1, "ffn1_dup"), (hid1, dob1, "ffn1_ddown")):
        plan = None if late is None or not late_grads else late(late_grads[-1], len(late_grads))
        res = _matmul_tn(lhs, rhs, name, out_dtype=BF16, hosted=plan)
        if plan is not None:
            res, carried = res
            late_arrived += carried
        late_grads.append(res)
    g_gate1, g_up1, g_down1 = late_grads

    return (loss_blk, grad_x, g_gate1, g_up1, g_down1, g_gate2, g_up2, g_down2, g_win, g_wout, g_wa2_p,
            d_ba_p, d_ffn1_g, d_mix_g, d_ffn2_g, d_final_g, d_ret_g, d_gla_g, arrived, late_arrived)


def kernel(x, ffn1_norm_g, ffn1_w_gate, ffn1_w_up, ffn1_w_down, mix_norm_g, w_in, ret_norm_g, gla_w_a2, gla_b_a, gla_norm_g, w_out, ffn2_norm_g, ffn2_w_gate, ffn2_w_up, ffn2_w_down, final_norm_g, loss_target, m_ffn1_norm_g, m_ffn1_w_gate, m_ffn1_w_up, m_ffn1_w_down, m_mix_norm_g, m_w_in, m_ret_norm_g, m_gla_w_a2, m_gla_b_a, m_gla_norm_g, m_w_out, m_ffn2_norm_g, m_ffn2_w_gate, m_ffn2_w_up, m_ffn2_w_down, m_final_norm_g, v_ffn1_norm_g, v_ffn1_w_gate, v_ffn1_w_up, v_ffn1_w_down, v_mix_norm_g, v_w_in, v_ret_norm_g, v_gla_w_a2, v_gla_b_a, v_gla_norm_g, v_w_out, v_ffn2_norm_g, v_ffn2_w_gate, v_ffn2_w_up, v_ffn2_w_down, v_final_norm_g):
    t = x.shape[1]
    xs = x.reshape(t, D_MODEL)
    target = loss_target.reshape(t, D_MODEL)
    chip = 2 * lax.axis_index("x") + lax.axis_index("y")
    c_arr = lax.axis_index("c").astype(jnp.int32).reshape(1)

    me_arr = chip.astype(jnp.int32).reshape(1)

    pad_rows = _pad_in_rows

    def own_block(gathered, shard):
        return lax.dynamic_update_slice(gathered, shard[None], (chip,) + (0,) * shard.ndim)

    ffn1_shard = _halves(jnp.stack([ffn1_w_gate[0].T, ffn1_w_up[0].T, ffn1_w_down[0]], axis=0).astype(BF16))
    rest_shards = [_halves(pad_rows(w_in[0].T).astype(BF16)[None]),
                   _halves(w_out.astype(BF16)),
                   jnp.concatenate([gla_w_a2.reshape(GATE_RANK, 64), jnp.zeros((GATE_RANK, 64), F32)],
                                   axis=1).reshape(1, 2, 8, LANES)]
    ffn2_shards = [_halves(w.astype(BF16)[None]) for w in (ffn2_w_gate[0].T, ffn2_w_up[0].T, ffn2_w_down[0])]
    def ffn1_weights(gathered):
        return lax.dynamic_update_slice(gathered, ffn1_shard[None], (0,) * 5).reshape(N_CHIPS, 3, FF_SHARD, D_MODEL)

    def rest_weights(gathered):
        win_all, wout_all, wa2_all = [own_block(g, s) for g, s in zip(gathered, rest_shards)]
        win_t = win_all.reshape(N_CHIPS, IN_ROWS, D_MODEL)
        w_in_pt = jnp.zeros((PROJ_P, D_MODEL), BF16)
        for p in range(N_CHIPS):
            w_in_pt = lax.dynamic_update_slice(w_in_pt, win_t[p, 0:IN_SHARD], (IN_SHARD * p, 0))
        wa2_p = jnp.pad(
            wa2_all.reshape(N_CHIPS, GATE_RANK, LANES).transpose(1, 0, 2).reshape(GATE_RANK, HEADS * LANES),
            ((0, LANES - GATE_RANK), (0, 0))).astype(BF16)
        return (None, w_in_pt, wout_all.reshape(D_MODEL, D_MODEL), wa2_p)

    def ffn2_weights(gathered):
        return [own_block(g, s).reshape(N_CHIPS, FF_SHARD, D_MODEL) for g, s in zip(gathered, ffn2_shards)]

    def pair_sums(grads, tag):
        halves = [g.reshape(g.shape[0], 2, g.shape[1] // 2, g.shape[2]) for g in grads]
        recv = _pair_exchange(halves, "pair_exchange_" + tag)
        return [_pair_add(g, r, c_arr, "pair_add_%s%d" % (tag, k)) for k, (g, r) in enumerate(zip(halves, recv))]

    early_sums = []

    def early(grads):
        early_sums.extend(pair_sums(grads, "early"))
        return _chip_exchange_plan(early_sums)

    late_sums = []

    def late(grad, number):
        late_sums.extend(pair_sums([grad], "late%d" % number))
        return _chip_exchange_plan(late_sums[-1:], by_peer=True)

    ba_p = jnp.pad(gla_b_a.reshape(HEADS, 64), ((0, 0), (0, 64))).reshape(1, HEADS * LANES)
    fb = _forward_backward(xs, target, None, None, ba_p, ffn1_norm_g, mix_norm_g, ret_norm_g, gla_norm_g,
                           ffn2_norm_g, final_norm_g.reshape(1, D_MODEL), ffn1_gather=(ffn1_shard, ffn1_weights),
                           rest_plan=_gather_plan(rest_shards), rest_weights=rest_weights,
                           ffn2_plans=[_gather_plan([s]) for s in ffn2_shards], ffn2_weights=ffn2_weights,
                           early=early, late=late)
    (loss_blk, grad_x, _, _, g_down1, _, _, _, _, _, g_wa2_p,
     d_ba_p, d_ffn1_g, d_mix_g, d_ffn2_g, d_final_g, d_ret_g, d_gla_g, early_arrived, late_arrived) = fb
    late_arrived = late_arrived + _run_hosted(late(g_down1, 3), "chip_exchange_late")
    mine = [_peer_sum(s, r, "chip_sum_%d" % k) for k, (s, r) in enumerate(zip(late_sums, late_arrived))]
    mine += [_chip_sum(s, r, me_arr, "chip_sum_%d" % (3 + k)) for k, (s, r) in enumerate(zip(early_sums, early_arrived))]
    other = _pair_share(mine)

    d_ba = d_ba_p.reshape(HEADS, LANES)[:, 0:64].reshape(1, 256)
    small_local = _pack_small(d_ffn1_g, d_mix_g, d_ffn2_g, d_final_g, d_ret_g, d_gla_g, d_ba, g_wa2_p[0], loss_blk)
    small_sum = _small_allreduce(small_local)
    loss = small_sum[6, 0]

    def rows(n1, nm, n2, nf, nret, ngla, ba, wa2):
        return [n1, nm, n2, nf.reshape(1, D_MODEL), nret, ngla, ba, wa2.reshape(GATE_RANK, 64)]

    small = _small_update(
        small_sum, me_arr,
        rows(ffn1_norm_g, mix_norm_g, ffn2_norm_g, final_norm_g, ret_norm_g, gla_norm_g, gla_b_a, gla_w_a2),
        rows(m_ffn1_norm_g, m_mix_norm_g, m_ffn2_norm_g, m_final_norm_g, m_ret_norm_g, m_gla_norm_g, m_gla_b_a,
             m_gla_w_a2),
        rows(v_ffn1_norm_g, v_mix_norm_g, v_ffn2_norm_g, v_final_norm_g, v_ret_norm_g, v_gla_norm_g, v_gla_b_a,
             v_gla_w_a2))
    s_grad, s_delta, s_m, s_v = [
        [*o[0:3], o[3].reshape(D_MODEL), *o[4:7], o[7].reshape(1, GATE_RANK, 64)] for o in small]

    def big(k, w, m, v, name, to_2d, from_2d):
        outs4 = _adamw_halves(to_2d(w), mine[k], other[k], to_2d(m), to_2d(v), c_arr, name)
        return [from_2d(z) for z in outs4]

    plain = (lambda w: w[0], lambda z: z[None])
    transposed = (lambda w: w[0].T, lambda z: z.T[None])
    in_proj = (lambda w: pad_rows(w[0].T), lambda z: z[0:IN_SHARD].T[None])
    r_g1 = big(0, ffn1_w_gate, m_ffn1_w_gate, v_ffn1_w_gate, "adamw_ffn1_gate", *transposed)
    r_u1 = big(1, ffn1_w_up, m_ffn1_w_up, v_ffn1_w_up, "adamw_ffn1_up", *transposed)
    r_d1 = big(2, ffn1_w_down, m_ffn1_w_down, v_ffn1_w_down, "adamw_ffn1_down", *plain)
    r_g2 = big(3, ffn2_w_gate, m_ffn2_w_gate, v_ffn2_w_gate, "adamw_ffn2_gate", *transposed)
    r_u2 = big(4, ffn2_w_up, m_ffn2_w_up, v_ffn2_w_up, "adamw_ffn2_up", *transposed)
    r_d2 = big(5, ffn2_w_down, m_ffn2_w_down, v_ffn2_w_down, "adamw_ffn2_down", *plain)
    r_in = big(6, w_in, m_w_in, v_w_in, "adamw_w_in", *in_proj)
    r_out = big(7, w_out, m_w_out, v_w_out, "adamw_w_out", *plain)

    def leaves(k, smalls):
        n1, nm, n2, nf, nret, ngla, ba, wa2 = smalls
        return [n1, r_g1[k], r_u1[k], r_d1[k], nm, r_in[k], nret, wa2, ba, ngla, r_out[k], n2, r_g2[k], r_u2[k], r_d2[k], nf]

    outs = [loss, grad_x.reshape(x.shape)]
    outs += leaves(0, s_grad) + leaves(1, s_delta) + leaves(2, s_m) + leaves(3, s_v)
    return tuple(outs)
```

```python
import functools

import jax
import jax.numpy as jnp
from jax import lax
from jax.experimental import pallas as pl
from jax.experimental.pallas import tpu as pltpu

F32, BF16 = jnp.float32, jnp.bfloat16
MESH = pl.DeviceIdType.MESH
ANY = pl.BlockSpec(memory_space=pl.ANY)

D_MODEL = 1024
D_FF = 2816
N_CHIPS = 4
FF_SHARD = D_FF // N_CHIPS
IN_WIDTH = 3088
IN_SHARD = IN_WIDTH // N_CHIPS
IN_ROWS = 800
CHUNK = 64
HEADS = 4
LANES = 128
PROJ_P = 3072 + LANES
PROJ_TILES = 5
GATE_RANK = 16
QK_SCALE = 0.125
GATE_NORM = 16.0
RMS_EPS = 1e-6
ROPE_BASE = 10000.0
ADAM_LR, ADAM_B1, ADAM_B2, ADAM_EPS, ADAM_WD, ADAM_STEP = 0.001, 0.9, 0.999, 1e-08, 0.01, 10
SMALL_ROWS = 32
TOKEN_TILE = 512
ATTN_TILE = 512

_ARB2 = pltpu.CompilerParams(dimension_semantics=("arbitrary", "arbitrary"))
_ARB1 = pltpu.CompilerParams(dimension_semantics=("arbitrary",))
_ARB3 = pltpu.CompilerParams(dimension_semantics=("arbitrary", "arbitrary", "arbitrary"))


def _dot(a, b):
    return jnp.dot(a, b, preferred_element_type=F32)


def _dot_nt(a, b):
    return lax.dot_general(a, b, (((1,), (1,)), ((), ())), preferred_element_type=F32)


def _dot_tn(a, b):
    return lax.dot_general(a, b, (((0,), (0,)), ((), ())), preferred_element_type=F32)


def _rms_scale(xv):
    return lax.rsqrt(jnp.mean(xv * xv, axis=-1, keepdims=True) + RMS_EPS)


def _rms_bwd(dh, xv, g):
    r = _rms_scale(xv)
    xhat = xv * r
    dxhat = dh * g
    dx = r * (dxhat - xhat * jnp.mean(dxhat * xhat, axis=-1, keepdims=True))
    return dx, jnp.sum(dh * xhat, axis=0, keepdims=True)


def _silu_grad(a, sg):
    return sg * (1.0 + a * (1.0 - sg))


class _Hosted:
    def __init__(self, arrays, out_shapes, n_sems, start, finish, middle=None):
        self.arrays, self.out_shapes, self.n_sems = list(arrays), list(out_shapes), n_sems
        self.start, self.finish = start, finish
        self.middle = middle if middle is not None else (lambda *refs: None)


def _call(body, args, *, name, grid, in_specs, out_specs, out_shape, scratch_shapes, compiler_params, hosted=None):
    if hosted is None:
        outs = pl.pallas_call(body, name=name, grid=grid, in_specs=in_specs, out_specs=out_specs, out_shape=out_shape,
                              scratch_shapes=scratch_shapes, compiler_params=compiler_params)(*args)
        return list(outs), []
    n_in, n_out, n_sc, nh = len(in_specs), len(out_specs), len(scratch_shapes), len(hosted.arrays)

    def wrapped(*refs):
        ins, h_in = refs[:n_in], refs[n_in:n_in + nh]
        outs, h_out = refs[n_in + nh:n_in + nh + n_out], refs[n_in + nh + n_out:n_in + 2 * nh + n_out]
        rest = refs[n_in + 2 * nh + n_out:]
        scratch, (send_sems, recv_sems) = rest[:n_sc], rest[n_sc:]
        step = functools.reduce(lambda flat, d: flat * grid[d] + pl.program_id(d), range(len(grid)), 0)
        total = functools.reduce(lambda a, b: a * b, grid)

        @pl.when(step == 0)
        def _():
            hosted.start(h_in, h_out, send_sems, recv_sems)

        @pl.when(step == total // 2)
        def _():
            hosted.middle(h_in, h_out, send_sems, recv_sems)

        body(*ins, *outs, *scratch)
        last = step == total - 1

        @pl.when(last)
        def _():
            hosted.finish(h_in, h_out, send_sems, recv_sems)

    sems = [pltpu.SemaphoreType.DMA((hosted.n_sems,)), pltpu.SemaphoreType.DMA((hosted.n_sems,))]
    outs = pl.pallas_call(
        wrapped, name=name, grid=grid, in_specs=list(in_specs) + [ANY] * nh, out_specs=list(out_specs) + [ANY] * nh,
        out_shape=list(out_shape) + hosted.out_shapes, scratch_shapes=list(scratch_shapes) + sems,
        compiler_params=compiler_params)(*args, *hosted.arrays)
    return list(outs[:n_out]), list(outs[n_out:])


def _run_hosted(hosted, name):
    nh = len(hosted.arrays)

    def body(*refs):
        h_in, h_out, (send_sems, recv_sems) = refs[:nh], refs[nh:2 * nh], refs[2 * nh:]
        hosted.start(h_in, h_out, send_sems, recv_sems)
        hosted.middle(h_in, h_out, send_sems, recv_sems)
        hosted.finish(h_in, h_out, send_sems, recv_sems)

    sems = [pltpu.SemaphoreType.DMA((hosted.n_sems,)), pltpu.SemaphoreType.DMA((hosted.n_sems,))]
    return list(pl.pallas_call(body, name=name, in_specs=[ANY] * nh, out_specs=[ANY] * nh,
                               out_shape=hosted.out_shapes, scratch_shapes=sems)(*hosted.arrays))


def _ffn_weight_operands(ffn_w, chunk_maps):
    if isinstance(ffn_w, (list, tuple)):
        specs = [pl.BlockSpec((None, FF_SHARD, D_MODEL), lambda *g, m=m: (m(*g), 0, 0)) for m in chunk_maps]
        return list(ffn_w), specs
    specs = [pl.BlockSpec((None, None, FF_SHARD, D_MODEL), lambda *g, m=m, k=kind: (m(*g), k, 0, 0))
             for kind, m in enumerate(chunk_maps)]
    return [ffn_w] * 3, specs


def _pipeline_items(steps):
    def cur(s):
        c = jnp.minimum(s, steps - 1)
        return c // N_CHIPS, c % N_CHIPS

    def prev(s):
        p = jnp.maximum(s - 1, 0)
        return p // N_CHIPS, p % N_CHIPS

    return cur, prev


def _ffn_fwd(x, g, ffn_w, name, hosted=None):
    t = x.shape[0]
    tm = min(t, TOKEN_TILE)

    def body(x_ref, g_ref, wg_ref, wu_ref, wd_ref, xo_ref, a_ref, u_ref, h_ref, acc_ref):
        j = pl.program_id(1)

        @pl.when(j == 0)
        def _():
            xv = x_ref[...]
            h_ref[...] = ((xv * _rms_scale(xv)) * g_ref[...]).astype(BF16)
            acc_ref[...] = jnp.zeros_like(acc_ref)

        h = h_ref[...]
        a = _dot_nt(h, wg_ref[...])
        u = _dot_nt(h, wu_ref[...])
        a_ref[...] = a.astype(BF16)
        u_ref[...] = u.astype(BF16)
        hid = (a * jax.nn.sigmoid(a)) * u
        acc_ref[...] += _dot(hid.astype(BF16), wd_ref[...])

        @pl.when(j == N_CHIPS - 1)
        def _():
            xo_ref[...] = x_ref[...] + 0.5 * acc_ref[...]

    tok = pl.BlockSpec((tm, D_MODEL), lambda i, j: (i, 0))
    act = pl.BlockSpec((None, tm, FF_SHARD), lambda i, j: (j, i, 0))
    w_arrays, weights = _ffn_weight_operands(ffn_w, [lambda i, j: j] * 3)
    return _call(
        body, (x, g, *w_arrays), name=name, grid=(t // tm, N_CHIPS),
        in_specs=[tok, pl.BlockSpec((1, D_MODEL), lambda i, j: (0, 0))] + weights,
        out_specs=[tok, act, act, tok],
        out_shape=[jax.ShapeDtypeStruct((t, D_MODEL), F32),
                   jax.ShapeDtypeStruct((N_CHIPS, t, FF_SHARD), BF16),
                   jax.ShapeDtypeStruct((N_CHIPS, t, FF_SHARD), BF16),
                   jax.ShapeDtypeStruct((t, D_MODEL), BF16)],
        scratch_shapes=[pltpu.VMEM((tm, D_MODEL), F32)],
        compiler_params=_ARB2, hosted=hosted)


def _ffn1_fwd_gathering(x, g, shard, name, hosted=None):
    t = x.shape[0]
    tm = min(t, TOKEN_TILE)
    nt = t // tm
    nh = 0 if hosted is None else len(hosted.arrays)

    def body(*refs):
        x_ref, g_ref, shard_ref = refs[0:3]
        h_in = refs[3:3 + nh]
        xo_ref, a_ref, u_ref, h_ref, wall = refs[3 + nh:8 + nh]
        h_out = refs[8 + nh:8 + 2 * nh]
        acc, h_all, wbuf, load_sems, send_sems, recv_sems = refs[8 + 2 * nh:14 + 2 * nh]
        carried_sems = refs[14 + 2 * nh:]
        k, i = pl.program_id(0), pl.program_id(1)
        legs, _ = _gather_legs(shard_ref, wall, send_sems, recv_sems, 0, True)
        begin, pass_on, _ = _gather_steps(legs, True)

        def load(chunk, src):
            return pltpu.make_async_copy(src, wbuf.at[chunk % 2], load_sems.at[chunk % 2])

        @pl.when((k == 0) & (i == 0))
        def _():
            begin()
            load(0, shard_ref).start()
            load(0, shard_ref).wait()
            if hosted is not None:
                hosted.start(h_in, h_out, *carried_sems)

        @pl.when((k == 1) & (i == 0))
        def _():
            pass_on()
            legs["pass_y"][1].wait_recv()
            load(1, wall.at[PEER_SLOT[1]]).start()
            load(1, wall.at[PEER_SLOT[1]]).wait()

        @pl.when((k == 1) & (i == nt // 2))
        def _():
            legs["pass_x"][1].wait_recv()
            load(2, wall.at[PEER_SLOT[0]]).start()

        @pl.when((k == 2) & (i == 0))
        def _():
            load(2, wall.at[PEER_SLOT[0]]).wait()
            if hosted is not None:
                hosted.middle(h_in, h_out, *carried_sems)

        @pl.when((k == 2) & (i == nt // 2))
        def _():
            legs["fwd_y"][1].wait_recv()
            legs["pass_d0"][0].start()
            legs["fwd_x"][1].wait_recv()
            legs["pass_d1"][0].start()
            legs["pass_d0"][1].wait_recv()
            legs["pass_d1"][1].wait_recv()
            load(3, wall.at[PEER_SLOT[2]]).start()

        @pl.when((k == 3) & (i == 0))
        def _():
            load(3, wall.at[PEER_SLOT[2]]).wait()

        @pl.when(k == 0)
        def _():
            xv = x_ref[...]
            h0 = ((xv * _rms_scale(xv)) * g_ref[...]).astype(BF16)
            h_all[i] = h0
            h_ref[...] = h0

        h = h_all[i]
        wg, wu, wd = (wbuf[k % 2, kind].reshape(FF_SHARD, D_MODEL) for kind in range(3))
        a = _dot_nt(h, wg)
        u = _dot_nt(h, wu)
        a_ref[...] = a.astype(BF16)
        u_ref[...] = u.astype(BF16)
        part = _dot(((a * jax.nn.sigmoid(a)) * u).astype(BF16), wd)

        @pl.when(k == 0)
        def _():
            acc[i] = part

        @pl.when(k > 0)
        def _():
            acc[i] += part

        @pl.when(k == N_CHIPS - 1)
        def _():
            xo_ref[...] = x_ref[...] + 0.5 * acc[i]

        @pl.when((k == N_CHIPS - 1) & (i == nt - 1))
        def _():
            for pair in legs.values():
                pair[0].wait_send()
            if hosted is not None:
                hosted.finish(h_in, h_out, *carried_sems)

    def first_or_last(k):
        return (k == 0) | (k == N_CHIPS - 1)

    tok = lambda keep: pl.BlockSpec((tm, D_MODEL), lambda k, i: (jnp.where(keep(k), i, 0), 0))
    act = pl.BlockSpec((None, tm, FF_SHARD), lambda k, i: (k, i, 0))
    act_shape = jax.ShapeDtypeStruct((N_CHIPS, t, FF_SHARD), BF16)
    carried = [] if hosted is None else [pltpu.SemaphoreType.DMA((hosted.n_sems,))] * 2
    outs = pl.pallas_call(
        body, name=name, grid=(N_CHIPS, nt),
        in_specs=[tok(first_or_last), pl.BlockSpec((1, D_MODEL), lambda k, i: (0, 0)), ANY] + [ANY] * nh,
        out_specs=[tok(lambda k: k == N_CHIPS - 1), act, act,
                   pl.BlockSpec((tm, D_MODEL), lambda k, i: (jnp.where(k == 0, i, nt - 1), 0)), ANY] + [ANY] * nh,
        out_shape=[jax.ShapeDtypeStruct((t, D_MODEL), F32), act_shape, act_shape,
                   jax.ShapeDtypeStruct((t, D_MODEL), BF16),
                   jax.ShapeDtypeStruct((N_CHIPS,) + shard.shape, shard.dtype)]
                  + ([] if hosted is None else hosted.out_shapes),
        scratch_shapes=[pltpu.VMEM((nt, tm, D_MODEL), F32), pltpu.VMEM((nt, tm, D_MODEL), BF16),
                        pltpu.VMEM((2,) + shard.shape, shard.dtype), pltpu.SemaphoreType.DMA((2,)),
                        pltpu.SemaphoreType.DMA((8,)), pltpu.SemaphoreType.DMA((8,))] + carried,
        compiler_params=_ARB2,
    )(x, g, shard, *([] if hosted is None else hosted.arrays))
    return list(outs[:5]), list(outs[5:])


def _ffn_bwd(dxo, x, g, a4, u4, ffn_w, name, hosted=None):
    t = x.shape[0]
    tm = min(t, TOKEN_TILE)
    steps = (t // tm) * N_CHIPS
    cur, prev = _pipeline_items(steps)


    def body(dxo_ref, dxo_prev_ref, x_ref, g_ref, a_ref, u_ref, wg_ref, wu_ref, wd_ref,
             da_ref, du_ref, hid_ref, dob_ref, dx_ref, dg_ref, acc_ref, da_slots, du_slots):
        s = pl.program_id(0)
        jc, jp = cur(s)[1], prev(s)[1]
        slot = s % 2

        @pl.when(s == 0)
        def _():
            dg_ref[...] = jnp.zeros_like(dg_ref)
            acc_ref[...] = jnp.zeros_like(acc_ref)
            da_slots[...] = jnp.zeros_like(da_slots)
            du_slots[...] = jnp.zeros_like(du_slots)

        @pl.when(jc == 0)
        def _():
            dob_ref[...] = (0.5 * dxo_ref[...]).astype(BF16)

        dhid = _dot_nt(dob_ref[...], wd_ref[...])
        a = a_ref[...].astype(F32)
        u = u_ref[...].astype(F32)
        sg = jax.nn.sigmoid(a)
        sl = a * sg
        hid_ref[...] = (sl * u).astype(BF16)
        du = (dhid * sl).astype(BF16)
        da = (dhid * u * _silu_grad(a, sg)).astype(BF16)
        du_ref[...] = du
        da_ref[...] = da
        acc_ref[...] += _dot(da_slots[1 - slot], wg_ref[...]) + _dot(du_slots[1 - slot], wu_ref[...])
        da_slots[slot] = da
        du_slots[slot] = du

        @pl.when((jp == N_CHIPS - 1) & (s > 0))
        def _():
            dx, dg = _rms_bwd(acc_ref[...], x_ref[...], g_ref[...])
            dx_ref[...] = dxo_prev_ref[...] + dx
            dg_ref[...] += dg
            acc_ref[...] = jnp.zeros_like(acc_ref)

    tok_cur = pl.BlockSpec((tm, D_MODEL), lambda s: (cur(s)[0], 0))
    tok_prev = pl.BlockSpec((tm, D_MODEL), lambda s: (prev(s)[0], 0))
    act = pl.BlockSpec((None, tm, FF_SHARD), lambda s: (cur(s)[1], cur(s)[0], 0))
    row = pl.BlockSpec((1, D_MODEL), lambda s: (0, 0))
    w_arrays, weights = _ffn_weight_operands(ffn_w, [lambda s: prev(s)[1], lambda s: prev(s)[1], lambda s: cur(s)[1]])
    act_shape = jax.ShapeDtypeStruct((N_CHIPS, t, FF_SHARD), BF16)
    return _call(
        body, (dxo, dxo, x, g, a4, u4, *w_arrays), name=name, grid=(steps + 1,),
        in_specs=[tok_cur, tok_prev, tok_prev, row, act, act] + weights,
        out_specs=[act, act, act, tok_cur, tok_prev, row],
        out_shape=[act_shape, act_shape, act_shape,
                   jax.ShapeDtypeStruct((t, D_MODEL), BF16),
                   jax.ShapeDtypeStruct((t, D_MODEL), F32),
                   jax.ShapeDtypeStruct((1, D_MODEL), F32)],
        scratch_shapes=[pltpu.VMEM((tm, D_MODEL), F32), pltpu.VMEM((2, tm, FF_SHARD), BF16),
                        pltpu.VMEM((2, tm, FF_SHARD), BF16)],
        compiler_params=_ARB1, hosted=hosted)


def _matmul_tn(a, b, name, tka=None, out_dtype=F32, hosted=None):
    a3, b3 = a.ndim == 3, b.ndim == 3
    nb = a.shape[0] if a3 else (b.shape[0] if b3 else 1)
    t, ka, n = a.shape[-2], a.shape[-1], b.shape[-1]
    tka = ka if tka is None else tka
    tk = min(t, 4 * TOKEN_TILE)
    nk = t // tk

    def body(a_ref, b_ref, o_ref, acc_ref):
        k = pl.program_id(2)

        @pl.when(k == 0)
        def _():
            acc_ref[...] = jnp.zeros_like(acc_ref)

        acc_ref[...] += _dot_tn(a_ref[...].astype(BF16), b_ref[...].astype(BF16))

        @pl.when(k == nk - 1)
        def _():
            o_ref[...] = acc_ref[...].astype(out_dtype)

    a_spec = (pl.BlockSpec((None, tk, tka), lambda i, j, k: (i, k, j)) if a3
              else pl.BlockSpec((tk, tka), lambda i, j, k: (k, j)))
    b_spec = (pl.BlockSpec((None, tk, n), lambda i, j, k: (i, k, 0)) if b3
              else pl.BlockSpec((tk, n), lambda i, j, k: (k, 0)))
    outs, carried = _call(
        body, (a, b), name=name, grid=(nb, ka // tka, t // tk),
        in_specs=[a_spec, b_spec],
        out_specs=[pl.BlockSpec((None, tka, n), lambda i, j, k: (i, j, 0))],
        out_shape=[jax.ShapeDtypeStruct((nb, ka, n), out_dtype)],
        scratch_shapes=[pltpu.VMEM((tka, n), F32)],
        compiler_params=_ARB3, hosted=hosted)
    return outs[0] if hosted is None else (outs[0], carried)


def _matmul_nt(a, w, name, out_dtype=F32):
    t, k = a.shape
    n = w.shape[0]
    tm = min(t, TOKEN_TILE)

    def body(a_ref, w_ref, o_ref):
        o_ref[...] = _dot_nt(a_ref[...].astype(BF16), w_ref[...]).astype(out_dtype)

    return pl.pallas_call(
        body, name=name, grid=(t // tm,),
        in_specs=[pl.BlockSpec((tm, k), lambda i: (i, 0)), pl.BlockSpec((n, k), lambda i: (0, 0))],
        out_specs=pl.BlockSpec((tm, n), lambda i: (i, 0)),
        out_shape=jax.ShapeDtypeStruct((t, n), out_dtype),
        compiler_params=_ARB1,
    )(a, w)


def _mixer_in_bwd(dproj, w_in_pt, dres, x, g, name):
    t, k = dproj.shape
    tm = min(t, TOKEN_TILE)

    def body(a_ref, w_ref, dres_ref, x_ref, g_ref, dx_ref, dg_ref):
        @pl.when(pl.program_id(0) == 0)
        def _():
            dg_ref[...] = jnp.zeros_like(dg_ref)

        dh = _dot(a_ref[...], w_ref[...])
        dx, dg = _rms_bwd(dh, x_ref[...], g_ref[...])
        dx_ref[...] = dres_ref[...] + dx
        dg_ref[...] += dg

    tok = pl.BlockSpec((tm, D_MODEL), lambda i: (i, 0))
    row = pl.BlockSpec((1, D_MODEL), lambda i: (0, 0))
    return pl.pallas_call(
        body, name=name, grid=(t // tm,),
        in_specs=[pl.BlockSpec((tm, k), lambda i: (i, 0)), pl.BlockSpec((k, D_MODEL), lambda i: (0, 0)), tok, tok, row],
        out_specs=[tok, row],
        out_shape=[jax.ShapeDtypeStruct((t, D_MODEL), F32), jax.ShapeDtypeStruct((1, D_MODEL), F32)],
        compiler_params=_ARB1,
    )(dproj, w_in_pt, dres, x, g)


def _mixer_in_fwd(x, g, w_in_pt, name, hosted=None):
    t = x.shape[0]
    tm = min(t, TOKEN_TILE)
    tn = PROJ_P // PROJ_TILES

    def body(x_ref, g_ref, w_ref, p_ref, h_ref):
        @pl.when(pl.program_id(1) == 0)
        def _():
            xv = x_ref[...]
            h_ref[...] = ((xv * _rms_scale(xv)) * g_ref[...]).astype(BF16)

        p_ref[...] = _dot_nt(h_ref[...], w_ref[...])

    tok = pl.BlockSpec((tm, D_MODEL), lambda i, j: (i, 0))
    return _call(
        body, (x, g, w_in_pt), name=name, grid=(t // tm, PROJ_TILES),
        in_specs=[tok, pl.BlockSpec((1, D_MODEL), lambda i, j: (0, 0)),
                  pl.BlockSpec((tn, D_MODEL), lambda i, j: (j, 0))],
        out_specs=[pl.BlockSpec((tm, tn), lambda i, j: (i, j)), tok],
        out_shape=[jax.ShapeDtypeStruct((t, PROJ_P), F32), jax.ShapeDtypeStruct((t, D_MODEL), BF16)],
        scratch_shapes=[], compiler_params=_ARB2, hosted=hosted)


def _mixer_out_fwd(o_ret, o_gla, w_out, x, name):
    t = x.shape[0]
    tm = min(t, TOKEN_TILE)
    half = HEADS * LANES

    def body(a_ref, b_ref, w_ref, x_ref, o_ref):
        o_ref[...] = x_ref[...] + _dot(a_ref[...], w_ref[0:half, :]) + _dot(b_ref[...], w_ref[half:2 * half, :])

    tok = pl.BlockSpec((tm, D_MODEL), lambda i: (i, 0))
    hb = pl.BlockSpec((tm, half), lambda i: (i, 0))
    return pl.pallas_call(
        body, name=name, grid=(t // tm,),
        in_specs=[hb, hb, pl.BlockSpec((2 * half, D_MODEL), lambda i: (0, 0)), tok],
        out_specs=tok, out_shape=jax.ShapeDtypeStruct((t, D_MODEL), F32),
        compiler_params=_ARB1,
    )(o_ret, o_gla, w_out, x)


def _final_loss(x, g, target, name):
    t = x.shape[0]
    tm = min(t, TOKEN_TILE)

    def body(x_ref, g_ref, t_ref, l_ref, dx_ref, dg_ref):
        @pl.when(pl.program_id(0) == 0)
        def _():
            l_ref[...] = jnp.zeros_like(l_ref)
            dg_ref[...] = jnp.zeros_like(dg_ref)

        xv = x_ref[...]
        gv = g_ref[...]
        err = (xv * _rms_scale(xv)) * gv - t_ref[...]
        l_ref[...] += 0.5 * jnp.sum(jnp.mean(err * err, axis=-1, keepdims=True), axis=0, keepdims=True)
        dx, dg = _rms_bwd(err * (1.0 / D_MODEL), xv, gv)
        dx_ref[...] = dx
        dg_ref[...] += dg

    tok = pl.BlockSpec((tm, D_MODEL), lambda i: (i, 0))
    row = pl.BlockSpec((1, D_MODEL), lambda i: (0, 0))
    return pl.pallas_call(
        body, name=name, grid=(t // tm,),
        in_specs=[tok, row, tok],
        out_specs=[pl.BlockSpec((8, LANES), lambda i: (0, 0)), tok, row],
        out_shape=[jax.ShapeDtypeStruct((8, LANES), F32), jax.ShapeDtypeStruct((t, D_MODEL), F32),
                   jax.ShapeDtypeStruct((1, D_MODEL), F32)],
        compiler_params=_ARB1,
    )(x, g, target)


def _rot(v, cos, sa, sb):
    return v * cos + pltpu.roll(v, 96, 1) * sa + pltpu.roll(v, 32, 1) * sb


def _rot_t(d, cos, sa, sb):
    return d * cos + pltpu.roll(d * sa, 32, 1) + pltpu.roll(d * sb, 96, 1)


def _bmm(a, b):
    return jnp.einsum("cik,ckj->cij", a, b, preferred_element_type=F32)


def _bmm_nt(a, b):
    return jnp.einsum("cik,cjk->cij", a, b, preferred_element_type=F32)


def _bmm_tn(a, b):
    return jnp.einsum("cki,ckj->cij", a, b, preferred_element_type=F32)


def _masked_sum(mask, x):
    hi = x.astype(BF16)
    r1 = x - hi.astype(F32)
    mid = r1.astype(BF16)
    lo = (r1 - mid.astype(F32)).astype(BF16)
    return _bmm(mask, hi) + _bmm(mask, mid) + _bmm(mask, lo)


PAIR = 2


def _tile_inputs(is_ret, qkvg_refs, aux, nc):
    shape3 = (nc, CHUNK, LANES)
    q_ref, k_ref, v_ref, g_ref = qkvg_refs
    low_lanes = lax.broadcasted_iota(jnp.int32, (1, LANES), 1) < 64
    ri = lax.broadcasted_iota(jnp.int32, (PAIR * nc, CHUNK, CHUNK), 1)
    ci = lax.broadcasted_iota(jnp.int32, (PAIR * nc, CHUNK, CHUNK), 2)
    qs, ks, vs, bs, gates, extra = [], [], [], [], [], []
    for hd in range(PAIR):
        q_blk, k_blk = q_ref[...], k_ref[...]
        if hd == 1:
            q_blk, k_blk = pltpu.roll(q_blk, 64, 1), pltpu.roll(k_blk, 64, 1)
        q_raw, k_raw = jnp.where(low_lanes, q_blk, 0.0), jnp.where(low_lanes, k_blk, 0.0)
        vs.append(v_ref[:, LANES * hd:LANES * (hd + 1)].reshape(shape3))
        gates.append(g_ref[:, LANES * hd:LANES * (hd + 1)])
        if is_ret:
            cos_ref, sa_ref, sb_ref, lg_ref = aux
            cos, sa, sb = cos_ref[...], sa_ref[...], sb_ref[...]
            q = _rot(q_raw, cos, sa, sb)
            k = _rot(k_raw, cos, sa, sb) * QK_SCALE
            steps = (lax.broadcasted_iota(jnp.int32, shape3, 1) + 1).astype(F32)
            bs.append(steps * lg_ref[hd])
            extra.append(jnp.exp(jnp.abs(ri[0:nc] - ci[0:nc]).astype(F32) * lg_ref[hd][:, 0:CHUNK]))
        else:
            glow_ref, wa2_ref, ba_ref = aux
            lanes = slice(LANES * hd, LANES * (hd + 1))
            logit = _dot(glow_ref[...].astype(BF16), wa2_ref[:, lanes]) + ba_ref[:, lanes]
            la = (jnp.minimum(logit, 0.0) - jnp.log1p(jnp.exp(-jnp.abs(logit)))) * (1.0 / GATE_NORM)
            bs.append(_masked_sum((ci[0:nc] <= ri[0:nc]).astype(BF16), la.reshape(shape3)))
            extra.append(logit)
            q = q_raw * QK_SCALE
            k = k_raw
        qs.append(q.reshape(shape3))
        ks.append(k.reshape(shape3))
    cat = lambda parts: jnp.concatenate(parts, axis=0)
    return cat(qs), cat(ks), cat(vs), gates, cat(bs), extra, ri, ci


def _tile_scores(q, k, b, ri, ci):
    mid = b[:, CHUNK // 2 - 1:CHUNK // 2, :]
    ep = jnp.exp(b - mid)
    en = jnp.exp(mid - b)
    qt, kt, qh, kh = q * ep, k * en, q * en, k * ep
    low = _bmm_nt(qt.astype(BF16), kt.astype(BF16))
    upp = _bmm_nt(qh.astype(BF16), kh.astype(BF16))
    scores = jnp.where(ci <= ri, low, upp)
    return scores, ep, en, qt, kt, qh, kh


def _attn_specs(is_ret, t, tb, imap_t):
    nb = t // tb
    base = 0 if is_ret else 12
    wide = PAIR * LANES
    proj = [pl.BlockSpec((tb, LANES), lambda p, i: (imap_t(i), base + p)),
            pl.BlockSpec((tb, LANES), lambda p, i: (imap_t(i), base + 2 + p)),
            pl.BlockSpec((tb, wide), lambda p, i: (imap_t(i), (base + 4) // 2 + p)),
            pl.BlockSpec((tb, wide), lambda p, i: (imap_t(i), (base + 8) // 2 + p))]
    lane_t = pl.BlockSpec((tb, LANES), lambda p, i: (imap_t(i), 0))
    if is_ret:
        aux = [lane_t, lane_t, lane_t, pl.BlockSpec((PAIR, 1, LANES), lambda p, i: (p, 0, 0))]
    else:
        aux = [pl.BlockSpec((tb, LANES), lambda p, i: (imap_t(i), PROJ_P // LANES - 1)),
               pl.BlockSpec((LANES, wide), lambda p, i: (0, p)),
               pl.BlockSpec((1, wide), lambda p, i: (0, p))]
    gain = pl.BlockSpec((1, wide), lambda p, i: (0, p))
    pair_t = pl.BlockSpec((tb, wide), lambda p, i: (imap_t(i), p))
    narrow_t = pl.BlockSpec((tb, LANES), lambda p, i: (imap_t(i), p))
    state = pl.BlockSpec((PAIR, tb // CHUNK, LANES, LANES), lambda p, i: (p, imap_t(i), 0, 0))
    return nb, proj, aux, gain, pair_t, narrow_t, state


def _attn_fwd(is_ret, proj, aux_arrays, gain, name, hosted=None):
    t = proj.shape[0]
    tb = min(t, ATTN_TILE)
    nc = tb // CHUNK
    n_aux = 4 if is_ret else 3
    nb, proj_spec, aux_specs, gain_spec, pair_t, _, state_spec = _attn_specs(is_ret, t, tb, lambda i: i)

    def body(*refs):
        qkvg_refs = refs[0:4]
        aux = refs[4:4 + n_aux]
        gn_ref, ofin_ref, oraw_ref, st_ref, state = refs[4 + n_aux:]

        @pl.when(pl.program_id(1) == 0)
        def _():
            state[...] = jnp.zeros_like(state)

        q, k, v, gates, b, extra, ri, ci = _tile_inputs(is_ret, qkvg_refs, aux, nc)
        if is_ret:
            scores = _bmm_nt(q.astype(BF16), k.astype(BF16)) * jnp.concatenate(extra, axis=0)
        else:
            scores = _tile_scores(q, k, b, ri, ci)[0]
        vb = v.astype(BF16)
        intra = _bmm(scores.astype(BF16), vb)
        b_last = b[:, CHUNK - 1:CHUNK, :]
        e_last = jnp.exp(b_last)
        grow = _bmm_tn(vb, (k * jnp.exp(b_last - b)).astype(BF16))
        for hd in range(PAIR):
            st = state[hd]
            for c in range(nc):
                st_ref[hd, c] = st
                st = st * e_last[hd * nc + c] + grow[hd * nc + c]
            state[hd] = st
        starts = st_ref[...].reshape(PAIR * nc, LANES, LANES)
        out3 = intra + _bmm_nt((q * jnp.exp(b)).astype(BF16), starts.astype(BF16))
        for hd in range(PAIR):
            lanes = slice(LANES * hd, LANES * (hd + 1))
            out = out3[hd * nc:(hd + 1) * nc].reshape(tb, LANES)
            oraw_ref[:, lanes] = out
            normed = out * _rms_scale(out)
            gate = gates[hd]
            ofin_ref[:, lanes] = ((normed * gn_ref[:, lanes]) * (gate * jax.nn.sigmoid(gate))).astype(BF16)

    width = HEADS * LANES
    return _call(
        body, (proj, proj, proj, proj, *aux_arrays, gain), name=name, grid=(HEADS // PAIR, nb),
        in_specs=proj_spec + aux_specs + [gain_spec],
        out_specs=[pair_t, pair_t, state_spec],
        out_shape=[jax.ShapeDtypeStruct((t, width), BF16), jax.ShapeDtypeStruct((t, width), F32),
                   jax.ShapeDtypeStruct((HEADS, t // CHUNK, LANES, LANES), F32)],
        scratch_shapes=[pltpu.VMEM((PAIR, LANES, LANES), F32)],
        compiler_params=_ARB2, hosted=hosted)


def _attn_bwd(is_ret, proj, aux_arrays, gain, o_raw, states, d_out, name, hosted=None):
    t = proj.shape[0]
    tb = min(t, ATTN_TILE)
    nc = tb // CHUNK
    n_aux = 4 if is_ret else 3
    nblk = t // tb
    nb, proj_spec, aux_specs, gain_spec, pair_t, narrow_t, state_spec = _attn_specs(
        is_ret, t, tb, lambda i: nblk - 1 - i)
    base = 0 if is_ret else HEADS // PAIR
    dout_spec = pl.BlockSpec((tb, PAIR * LANES), lambda p, i: (nblk - 1 - i, base + p))

    def body(*refs):
        qkvg_refs = refs[0:4]
        aux = refs[4:4 + n_aux]
        gn_ref, oraw_ref, st_ref, dfin_ref = refs[4 + n_aux:8 + n_aux]
        dq_ref, dk_ref, dv_ref, dgate_ref, dgn_ref = refs[8 + n_aux:13 + n_aux]
        if is_ret:
            dstate, dafter_ref = refs[13 + n_aux:]
        else:
            dlogit_ref, dba_ref, dstate, dafter_ref = refs[13 + n_aux:]

        @pl.when(pl.program_id(1) == 0)
        def _():
            dstate[...] = jnp.zeros_like(dstate)
            dgn_ref[...] = jnp.zeros_like(dgn_ref)
            if not is_ret:
                dba_ref[...] = jnp.zeros_like(dba_ref)

        shape3 = (nc, CHUNK, LANES)
        q, k, v, gates, b, extra, ri, ci = _tile_inputs(is_ret, qkvg_refs, aux, nc)
        eb = jnp.exp(b)
        qe = q * eb
        b_last = b[:, CHUNK - 1:CHUNK, :]
        e_last = jnp.exp(b_last)
        ekd = jnp.exp(b_last - b)
        kd = k * ekd

        d_os = []
        for hd in range(PAIR):
            lanes = slice(LANES * hd, LANES * (hd + 1))
            gn, gate = gn_ref[:, lanes], gates[hd]
            out = oraw_ref[:, lanes]
            r = _rms_scale(out)
            normed = out * r
            sg = jax.nn.sigmoid(gate)
            dfin = dfin_ref[:, lanes]
            dgate_ref[:, lanes] = (dfin * (normed * gn) * _silu_grad(gate, sg)).astype(BF16)
            dpre = dfin * (gate * sg)
            dgn_ref[:, lanes] += jnp.sum(dpre * normed, axis=0, keepdims=True)
            dnormed = dpre * gn
            d_o = r * (dnormed - normed * jnp.mean(dnormed * normed, axis=-1, keepdims=True))
            d_os.append(d_o.reshape(shape3))
        dob, vb = jnp.concatenate(d_os, axis=0).astype(BF16), v.astype(BF16)

        dgrow = _bmm_tn(dob, qe.astype(BF16))
        for hd in range(PAIR):
            dst = dstate[hd]
            for c in reversed(range(nc)):
                dafter_ref[hd * nc + c] = dst
                dst = dst * e_last[hd * nc + c] + dgrow[hd * nc + c]
            dstate[hd] = dst
        st = st_ref[...].reshape(PAIR * nc, LANES, LANES)
        dafter = dafter_ref[...]
        stb, dafter_b = st.astype(BF16), dafter.astype(BF16)

        dsc = _bmm_nt(dob, vb)
        dsc_t = _bmm_nt(vb, dob)
        dqe = _bmm(dob, stb)
        dkd = _bmm(vb, dafter_b)
        if is_ret:
            decay, qb, kb = jnp.concatenate(extra, axis=0), q.astype(BF16), k.astype(BF16)
            scores_t = _bmm_nt(kb, qb) * decay
            dq = _bmm((dsc * decay).astype(BF16), kb) + dqe * eb
            dk = _bmm((dsc_t * decay).astype(BF16), qb) + dkd * ekd
        else:
            _, ep, en, qt, kt, qh, kh = _tile_scores(q, k, b, ri, ci)
            qtb, ktb, qhb, khb = qt.astype(BF16), kt.astype(BF16), qh.astype(BF16), kh.astype(BF16)
            scores_t = jnp.where(ci >= ri, _bmm_nt(ktb, qtb), _bmm_nt(khb, qhb))
            dqt = _bmm(jnp.where(ci <= ri, dsc, 0.0).astype(BF16), ktb)
            dqh = _bmm(jnp.where(ci <= ri, 0.0, dsc).astype(BF16), khb)
            dkt = _bmm(jnp.where(ci >= ri, dsc_t, 0.0).astype(BF16), qtb)
            dkh = _bmm(jnp.where(ci >= ri, 0.0, dsc_t).astype(BF16), qhb)
            dq = dqt * ep + dqh * en + dqe * eb
            dk = dkt * en + dkh * ep + dkd * ekd
        dv = _bmm(scores_t.astype(BF16), dob) + _bmm_nt(kd.astype(BF16), dafter_b)

        if not is_ret:
            db = dqt * qt - dkt * kt - dqh * qh + dkh * kh + dqe * qe - dkd * kd
            db_last = (jnp.sum(dkd * kd, axis=1, keepdims=True)
                       + jnp.sum(dafter * st, axis=1, keepdims=True) * e_last)
            last_row = lax.broadcasted_iota(jnp.int32, (PAIR * nc, CHUNK, LANES), 1) == CHUNK - 1
            db = db + jnp.where(last_row, db_last, 0.0)
            dla = _masked_sum((ci >= ri).astype(BF16), db)

        dq_pair, dk_pair = [], []
        for hd in range(PAIR):
            lanes = slice(LANES * hd, LANES * (hd + 1))
            rows3 = slice(hd * nc, (hd + 1) * nc)
            dq_h, dk_h = dq[rows3].reshape(tb, LANES), dk[rows3].reshape(tb, LANES)
            if is_ret:
                cos_ref, sa_ref, sb_ref, _ = aux
                cos, sa, sb = cos_ref[...], sa_ref[...], sb_ref[...]
                dq_h = _rot_t(dq_h, cos, sa, sb)
                dk_h = _rot_t(dk_h, cos, sa, sb) * QK_SCALE
            else:
                dq_h = dq_h * QK_SCALE
                dlogit = dla[rows3].reshape(tb, LANES) * (1.0 / GATE_NORM) * jax.nn.sigmoid(-extra[hd])
                dlogit_ref[:, lanes] = dlogit.astype(BF16)
                dba_ref[:, lanes] += jnp.sum(dlogit, axis=0, keepdims=True)
            dq_pair.append(dq_h)
            dk_pair.append(dk_h)
            dv_ref[:, lanes] = dv[rows3].reshape(tb, LANES).astype(BF16)
        dq_ref[...] = (dq_pair[0] + pltpu.roll(dq_pair[1], 64, 1)).astype(BF16)
        dk_ref[...] = (dk_pair[0] + pltpu.roll(dk_pair[1], 64, 1)).astype(BF16)

    width = HEADS * LANES
    row_out = pl.BlockSpec((1, PAIR * LANES), lambda p, i: (0, p))
    out_specs = [narrow_t, narrow_t, pair_t, pair_t, row_out]
    out_shape = ([jax.ShapeDtypeStruct((t, width // 2), BF16)] * 2 + [jax.ShapeDtypeStruct((t, width), BF16)] * 2
                 + [jax.ShapeDtypeStruct((1, width), F32)])
    if not is_ret:
        out_specs += [pair_t, row_out]
        out_shape += [jax.ShapeDtypeStruct((t, width), BF16), jax.ShapeDtypeStruct((1, width), F32)]
    return _call(
        body, (proj, proj, proj, proj, *aux_arrays, gain, o_raw, states, d_out), name=name,
        grid=(HEADS // PAIR, nblk),
        in_specs=proj_spec + aux_specs + [gain_spec, pair_t, state_spec, dout_spec],
        out_specs=out_specs, out_shape=out_shape,
        scratch_shapes=[pltpu.VMEM((PAIR, LANES, LANES), F32), pltpu.VMEM((PAIR * nc, LANES, LANES), F32)],
        compiler_params=_ARB2, hosted=hosted)


PEER_SLOT = (2, 1, 3)


def _place():
    x, y, c = lax.axis_index("x"), lax.axis_index("y"), lax.axis_index("c")
    chips = [(1 - x, y), (x, 1 - y), (1 - x, 1 - y)]
    return x, y, c, 2 * x + y, chips


def _route_split(rows, dtype):
    tile = 16 if dtype == BF16 else 8
    if rows < 2 * tile:
        return None
    return -(-(rows // 2) // tile) * tile


def _routes(by_peer):
    x, y, c, me, chips = _place()
    (xx, xy), (yx, yy), (dx, dy) = chips
    if by_peer:
        slots = dict(own=0, from_x=PEER_SLOT[0], from_y=PEER_SLOT[1], diag=PEER_SLOT[2],
                     mine_on_x=PEER_SLOT[0], mine_on_y=PEER_SLOT[1])
    else:
        slots = dict(own=me, from_x=2 * xx + xy, from_y=2 * yx + yy, diag=2 * dx + dy, mine_on_x=me, mine_on_y=me)
    return c, (xx, xy, c), (yx, yy, c), (dx, dy, c), (x, y, 1 - c), slots


def _gather_legs(src, out, send_sems, recv_sems, base, by_peer):
    c, to_x, to_y, to_d, sibling, s = _routes(by_peer)
    r0 = _route_split(src.shape[2], src.dtype)

    def cp(k, src_ref, dst_ref, to):
        return pltpu.make_async_remote_copy(src_ref=src_ref, dst_ref=dst_ref, send_sem=send_sems.at[base + k],
                                            recv_sem=recv_sems.at[base + k], device_id=to, device_id_type=MESH)

    mine = src.at[:, c]
    legs = dict(
        x=(cp(0, mine, out.at[s["mine_on_x"], :, c], to_x), cp(0, mine, out.at[s["from_x"], :, c], to_x)),
        y=(cp(1, mine, out.at[s["mine_on_y"], :, c], to_y), cp(1, mine, out.at[s["from_y"], :, c], to_y)),
        pass_x=(cp(4, out.at[s["from_x"], :, c], out.at[s["from_x"], :, c], sibling),
                cp(4, mine, out.at[s["from_x"], :, 1 - c], sibling)),
        pass_y=(cp(5, out.at[s["from_y"], :, c], out.at[s["from_y"], :, c], sibling),
                cp(5, mine, out.at[s["from_y"], :, 1 - c], sibling)))
    if r0 is None:
        mine_on_d = s["diag"] if by_peer else s["own"]
        legs["d"] = (cp(2, mine, out.at[mine_on_d, :, c], to_d), cp(2, mine, out.at[s["diag"], :, c], to_d))
        legs["pass_d"] = (cp(6, out.at[s["diag"], :, c], out.at[s["diag"], :, c], sibling),
                          cp(6, mine, out.at[s["diag"], :, 1 - c], sibling))
        return legs, False
    lo, hi = pl.ds(0, r0), pl.ds(r0, src.shape[2] - r0)
    fx_on_y = s["diag"] if by_peer else s["from_x"]
    fy_on_x = s["diag"] if by_peer else s["from_y"]
    legs.update(
        fwd_y=(cp(2, out.at[s["from_x"], :, c, lo], out.at[fx_on_y, :, c, lo], to_y),
               cp(2, mine.at[:, lo], out.at[s["diag"], :, c, lo], to_y)),
        fwd_x=(cp(3, out.at[s["from_y"], :, c, hi], out.at[fy_on_x, :, c, hi], to_x),
               cp(3, mine.at[:, hi], out.at[s["diag"], :, c, hi], to_x)),
        pass_d0=(cp(6, out.at[s["diag"], :, c, lo], out.at[s["diag"], :, c, lo], sibling),
                 cp(6, mine.at[:, lo], out.at[s["diag"], :, 1 - c, lo], sibling)),
        pass_d1=(cp(7, out.at[s["diag"], :, c, hi], out.at[s["diag"], :, c, hi], sibling),
                 cp(7, mine.at[:, hi], out.at[s["diag"], :, 1 - c, hi], sibling)))
    return legs, True


def _gather_steps(legs, routed):
    def start():
        legs["x"][0].start()
        legs["y"][0].start()
        if not routed:
            legs["d"][0].start()

    def middle():
        legs["x"][1].wait_recv()
        if routed:
            legs["fwd_y"][0].start()
        legs["pass_x"][0].start()
        legs["y"][1].wait_recv()
        if routed:
            legs["fwd_x"][0].start()
        legs["pass_y"][0].start()

    def finish():
        last = ["pass_d0", "pass_d1"] if routed else ["pass_d"]
        if routed:
            legs["fwd_y"][1].wait_recv()
            legs["pass_d0"][0].start()
            legs["fwd_x"][1].wait_recv()
            legs["pass_d1"][0].start()
        else:
            legs["d"][1].wait_recv()
            legs["pass_d"][0].start()
        for name in ["pass_x", "pass_y"] + last:
            legs[name][1].wait_recv()
        for name in ["x", "y", "pass_x", "pass_y"] + last + (["fwd_y", "fwd_x"] if routed else ["d"]):
            legs[name][0].wait_send()

    return start, middle, finish


def _gather_plan(arrs):
    na = len(arrs)

    def steps(ins, outs, send_sems, recv_sems):
        return [_gather_steps(*_gather_legs(ins[a], outs[a], send_sems, recv_sems, 8 * a, False)) for a in range(na)]

    def run(which):
        def hook(*refs):
            for step in steps(*refs):
                step[which]()
        return hook

    return _Hosted(arrs, [jax.ShapeDtypeStruct((N_CHIPS,) + a.shape, a.dtype) for a in arrs], 8 * na,
                   run(0), run(2), middle=run(1))


def _pair_exchange_plan(grads):
    na = len(grads)

    def copies(ins, outs, send_sems, recv_sems):
        x, y, c, _, _ = _place()
        return [pltpu.make_async_remote_copy(
            src_ref=ins[a].at[:, 1 - c], dst_ref=outs[a], send_sem=send_sems.at[a], recv_sem=recv_sems.at[a],
            device_id=(x, y, 1 - c), device_id_type=MESH) for a in range(na)]

    def start(*refs):
        for cp in copies(*refs):
            cp.start()

    def finish(*refs):
        for cp in copies(*refs):
            cp.wait()

    return _Hosted(grads, [jax.ShapeDtypeStruct(g.shape[:1] + g.shape[2:], g.dtype) for g in grads], na, start, finish)


def _small_gather_plan(block):
    def copies(ins, outs, send_sems, recv_sems):
        x, y, c, _, chips = _place()
        peers = [(x, y, 1 - c)] + [(px, py, pc) for px, py in chips for pc in (c, 1 - c)]
        sends = [pltpu.make_async_remote_copy(
            src_ref=ins[0], dst_ref=outs[0].at[4 * x + 2 * y + c], send_sem=send_sems.at[k], recv_sem=recv_sems.at[k],
            device_id=peer, device_id_type=MESH) for k, peer in enumerate(peers)]
        recvs = [pltpu.make_async_remote_copy(
            src_ref=ins[0], dst_ref=outs[0].at[4 * px + 2 * py + pc], send_sem=send_sems.at[k], recv_sem=recv_sems.at[k],
            device_id=(px, py, pc), device_id_type=MESH) for k, (px, py, pc) in enumerate(peers)]
        return sends, recvs

    def start(*refs):
        for cp in copies(*refs)[0]:
            cp.start()

    def finish(*refs):
        sends, recvs = copies(*refs)
        for cp in recvs:
            cp.wait_recv()
        for cp in sends:
            cp.wait_send()

    return _Hosted([block], [jax.ShapeDtypeStruct((8,) + block.shape, block.dtype)], 7, start, finish)


def _sum_devices(blocks):
    def body(b_ref, o_ref):
        acc = b_ref[0]
        for d in range(1, 8):
            acc = acc + b_ref[d]
        o_ref[...] = acc

    return pl.pallas_call(body, name="sum_devices", in_specs=[_VMEM], out_specs=_VMEM,
                          out_shape=jax.ShapeDtypeStruct(blocks.shape[1:], blocks.dtype))(blocks)


def _pair_add(grad, recv, c_arr, name):
    _, _, r, cols = grad.shape

    def body(c_ref, g_ref, r_ref, o_ref):
        o_ref[...] = (g_ref[...].astype(F32) + r_ref[...].astype(F32)).astype(BF16)

    return pl.pallas_call(
        body, name=name,
        grid_spec=pltpu.PrefetchScalarGridSpec(
            num_scalar_prefetch=1, grid=(N_CHIPS,),
            in_specs=[pl.BlockSpec((None, None, r, cols), lambda p, c_ref: (p, c_ref[0], 0, 0)),
                      pl.BlockSpec((None, r, cols), lambda p, c_ref: (p, 0, 0))],
            out_specs=pl.BlockSpec((None, r, cols), lambda p, c_ref: (p, 0, 0))),
        out_shape=jax.ShapeDtypeStruct((N_CHIPS, r, cols), BF16),
        compiler_params=_ARB1,
    )(c_arr, grad, recv)


def _chip_exchange_plan(sums, by_peer=False):
    na = len(sums)

    def copies(ins, outs, send_sems, recv_sems):
        x, y, c, me, chips = _place()

        def copy(a, j, px, py, block, slot):
            return pltpu.make_async_remote_copy(
                src_ref=ins[a].at[block], dst_ref=outs[a].at[slot],
                send_sem=send_sems.at[3 * a + j], recv_sem=recv_sems.at[3 * a + j],
                device_id=(px, py, c), device_id_type=MESH)

        peers = [(a, j, px, py) for a in range(na) for j, (px, py) in enumerate(chips)]
        return me, peers, copy

    def start(*refs):
        me, peers, copy = copies(*refs)
        for a, j, px, py in peers:
            if by_peer:
                copy(a, j, px, py, PEER_SLOT[j], PEER_SLOT[j]).start()
            else:
                copy(a, j, px, py, 2 * px + py, me).start()

    def finish(*refs):
        me, peers, copy = copies(*refs)
        for a, j, px, py in peers:
            if by_peer:
                copy(a, j, px, py, PEER_SLOT[j], PEER_SLOT[j]).wait_recv()
            else:
                copy(a, j, px, py, me, 2 * px + py).wait_recv()
        for a, j, px, py in peers:
            if by_peer:
                copy(a, j, px, py, PEER_SLOT[j], PEER_SLOT[j]).wait_send()
            else:
                copy(a, j, px, py, 2 * px + py, me).wait_send()

    return _Hosted(sums, [jax.ShapeDtypeStruct(s.shape, s.dtype) for s in sums], 3 * na, start, finish)


def _chip_sum(own, recv, me_arr, name):
    _, r, cols = recv.shape

    def body(me_ref, own_ref, r_ref, o_ref):
        o_ref[...] = jnp.zeros_like(o_ref)
        for q in range(N_CHIPS):
            @pl.when(me_ref[0] == q)
            def _():
                o_ref[...] += own_ref[...].astype(F32)

            @pl.when(me_ref[0] != q)
            def _():
                o_ref[...] += r_ref[q].astype(F32)

    return pl.pallas_call(
        body, name=name,
        grid_spec=pltpu.PrefetchScalarGridSpec(
            num_scalar_prefetch=1, grid=(1,),
            in_specs=[pl.BlockSpec((None, r, cols), lambda i, me_ref: (me_ref[0], 0, 0)),
                      pl.BlockSpec((N_CHIPS, r, cols), lambda i, me_ref: (0, 0, 0))],
            out_specs=pl.BlockSpec((r, cols), lambda i, me_ref: (0, 0))),
        out_shape=jax.ShapeDtypeStruct((r, cols), F32),
        compiler_params=_ARB1,
    )(me_arr, own, recv)


def _peer_sum(own, recv, name):
    _, r, cols = recv.shape

    def body(own_ref, r_ref, o_ref):
        acc = own_ref[...].astype(F32) + r_ref[1].astype(F32)
        acc = acc + r_ref[2].astype(F32)
        o_ref[...] = acc + r_ref[3].astype(F32)

    return pl.pallas_call(
        body, name=name, grid=(1,),
        in_specs=[pl.BlockSpec((None, r, cols), lambda i: (0, 0, 0)), pl.BlockSpec((N_CHIPS, r, cols), lambda i: (0, 0, 0))],
        out_specs=pl.BlockSpec((r, cols), lambda i: (0, 0)),
        out_shape=jax.ShapeDtypeStruct((r, cols), F32),
        compiler_params=_ARB1,
    )(own, recv)


def _pair_share(halves):
    na = len(halves)

    def body(*refs):
        ins, outs = refs[:na], refs[na:2 * na]
        send_sems, recv_sems = refs[2 * na:]
        x, y, c, _, _ = _place()
        copies = [pltpu.make_async_remote_copy(
            src_ref=ins[a], dst_ref=outs[a], send_sem=send_sems.at[a], recv_sem=recv_sems.at[a],
            device_id=(x, y, 1 - c), device_id_type=MESH) for a in range(na)]
        for cp in copies:
            cp.start()
        for cp in copies:
            cp.wait()

    return pl.pallas_call(
        body, name="pair_share",
        in_specs=[ANY] * na, out_specs=[ANY] * na,
        out_shape=[jax.ShapeDtypeStruct(h.shape, h.dtype) for h in halves],
        scratch_shapes=[pltpu.SemaphoreType.DMA((na,)), pltpu.SemaphoreType.DMA((na,))],
    )(*halves)


def _small_allreduce(block):
    m, n = block.shape

    def body(x_ref, all_ref, sum_ref, send_sems, recv_sems, local_sem):
        x, y, c, _, chips = _place()
        me, sibling = (x, y, c), (x, y, 1 - c)

        def rows(px, py, pc):
            return all_ref.at[pl.ds((4 * px + 2 * py + pc) * m, m), :]

        def copy(k, blk, to, src=None):
            return pltpu.make_async_remote_copy(
                src_ref=rows(*blk) if src is None else src, dst_ref=rows(*blk),
                send_sem=send_sems.at[k], recv_sem=recv_sems.at[k], device_id=to, device_id_type=MESH)

        mine = pltpu.make_async_copy(x_ref, rows(*me), local_sem)
        mine.start()
        first = [copy(0, me, sibling, src=x_ref)]
        first += [copy(1 + j, me, (*chip, c), src=x_ref) for j, chip in enumerate(chips)]
        for cp in first:
            cp.start()
        passed = [copy(4 + j, (*chip, c), sibling) for j, chip in enumerate(chips)]
        for j, chip in enumerate(chips):
            copy(1 + j, (*chip, c), me).wait_recv()
            passed[j].start()
        copy(0, sibling, me).wait_recv()
        for j, chip in enumerate(chips):
            copy(4 + j, (*chip, 1 - c), me).wait_recv()
        for cp in first + passed:
            cp.wait_send()
        mine.wait()
        acc = all_ref[0:m, :]
        for d in range(1, 8):
            acc = acc + all_ref[d * m:(d + 1) * m, :]
        sum_ref[...] = acc

    vmem = pl.BlockSpec(memory_space=pltpu.VMEM)
    return pl.pallas_call(
        body, name="small_allreduce",
        in_specs=[vmem], out_specs=[vmem, vmem],
        out_shape=[jax.ShapeDtypeStruct((8 * m, n), F32), jax.ShapeDtypeStruct((m, n), F32)],
        scratch_shapes=[pltpu.SemaphoreType.DMA((7,)), pltpu.SemaphoreType.DMA((7,)), pltpu.SemaphoreType.DMA],
    )(block)[1]


def _row_tile(rows):
    best = rows
    for cand in range(8, min(rows, 512) + 1, 8):
        if rows % cand == 0:
            best = cand
    return best


def _adamw_math(w, g, m, v):
    m2 = ADAM_B1 * m + (1.0 - ADAM_B1) * g
    v2 = ADAM_B2 * v + (1.0 - ADAM_B2) * (g * g)
    m_hat = m2 / (1.0 - ADAM_B1 ** ADAM_STEP)
    v_hat = v2 / (1.0 - ADAM_B2 ** ADAM_STEP)
    return -ADAM_LR * (m_hat / (jnp.sqrt(v_hat) + ADAM_EPS) + ADAM_WD * w), m2, v2


def _adamw_halves(w, g_mine, g_other, m, v, c_arr, name):
    rows, cols = w.shape
    r = rows // 2
    tr = _row_tile(r)
    nt = r // tr

    def body(c_ref, w_ref, gm_ref, go_ref, m_ref, v_ref, g_ref, d_ref, nm_ref, nv_ref):
        gv = jnp.where(pl.program_id(0) == c_ref[0], gm_ref[...], go_ref[...])
        g_ref[...] = gv
        d_ref[...], nm_ref[...], nv_ref[...] = _adamw_math(w_ref[...], gv, m_ref[...], v_ref[...])

    full = pl.BlockSpec((tr, cols), lambda h, i, c_ref: (h * nt + i, 0))
    half = pl.BlockSpec((tr, cols), lambda h, i, c_ref: (i, 0))
    shape = jax.ShapeDtypeStruct((rows, cols), F32)
    return pl.pallas_call(
        body, name=name,
        grid_spec=pltpu.PrefetchScalarGridSpec(
            num_scalar_prefetch=1, grid=(2, nt),
            in_specs=[full, half, half, full, full], out_specs=[full] * 4),
        out_shape=[shape] * 4,
        compiler_params=_ARB2,
    )(c_arr, w, g_mine, g_other, m, v)


def _adamw(w, g, m, v, name):
    rows, cols = w.shape
    tr = _row_tile(rows)

    def body(w_ref, g_ref, m_ref, v_ref, d_ref, nm_ref, nv_ref):
        d_ref[...], nm_ref[...], nv_ref[...] = _adamw_math(w_ref[...], g_ref[...], m_ref[...], v_ref[...])

    spec = pl.BlockSpec((tr, cols), lambda i: (i, 0))
    shape = jax.ShapeDtypeStruct((rows, cols), F32)
    return pl.pallas_call(
        body, name=name, grid=(rows // tr,),
        in_specs=[spec] * 4, out_specs=[spec] * 3, out_shape=[shape] * 3,
        compiler_params=_ARB1,
    )(w, g, m, v)


def _pad_w_in_t(w_in_t):
    return jnp.pad(w_in_t, ((0, PROJ_P - IN_WIDTH), (0, 0)))


def _unpad_w_in_t(w_pt):
    return w_pt[0:IN_WIDTH]


def _rope_tables(t):
    half = 32
    inv = ROPE_BASE ** (-jnp.arange(half, dtype=F32) * 2.0 / 64)
    ang = jnp.arange(t, dtype=F32)[:, None] * inv[None, :]
    cos, sin = jnp.cos(ang), jnp.sin(ang)
    z32, z64 = jnp.zeros((t, 32), F32), jnp.zeros((t, 64), F32)
    return (jnp.concatenate([cos, cos, z64], axis=1),
            jnp.concatenate([-sin, z32, z64], axis=1),
            jnp.concatenate([z32, sin, z64], axis=1))


def _halves(w):
    n, rows, cols = w.shape
    return w.reshape(n, 2, rows // 2, cols)


_VMEM = pl.BlockSpec(memory_space=pltpu.VMEM)


def _pack_small(n1, nm, n2, nf, nret, ngla, ba, wa2_p, loss_blk):
    def body(n1_ref, nm_ref, n2_ref, nf_ref, nret_ref, ngla_ref, ba_ref, wa2_ref, loss_ref, o_ref):
        o_ref[...] = jnp.zeros_like(o_ref)
        o_ref[0:1, :] = n1_ref[...]
        o_ref[1:2, :] = nm_ref[...]
        o_ref[2:3, :] = n2_ref[...]
        o_ref[3:4, :] = nf_ref[...]
        o_ref[4:5, 0:512] = nret_ref[...]
        o_ref[4:5, 512:1024] = ngla_ref[...]
        o_ref[5:6, 0:256] = ba_ref[...]
        o_ref[6:7, 0:LANES] = loss_ref[0:1, :]
        o_ref[8:8 + GATE_RANK, 0:HEADS * LANES] = wa2_ref[0:GATE_RANK, :]

    return pl.pallas_call(
        body, name="pack_small", in_specs=[_VMEM] * 9, out_specs=_VMEM,
        out_shape=jax.ShapeDtypeStruct((SMALL_ROWS, D_MODEL), F32),
    )(n1, nm, n2, nf, nret, ngla, ba, wa2_p, loss_blk)


def _small_update(summed, chip_arr, ws, ms, vs):
    n = len(ws)

    def body(chip_ref, s_ref, *refs):
        w_refs, m_refs, v_refs = refs[0:n], refs[n:2 * n], refs[2 * n:3 * n]
        outs = refs[3 * n:]
        wa2_all = s_ref[8:8 + GATE_RANK, 0:HEADS * LANES]
        wa2_g = jnp.zeros((GATE_RANK, 64), F32)
        for p in range(N_CHIPS):
            wa2_g = jnp.where(chip_ref[0] == p, wa2_all[:, LANES * p:LANES * p + 64], wa2_g)
        grads = [s_ref[0:1, :], s_ref[1:2, :], s_ref[2:3, :], s_ref[3:4, :], s_ref[4:5, 0:512],
                 s_ref[4:5, 512:1024], s_ref[5:6, 0:256], wa2_g]
        for k in range(n):
            d, m2, v2 = _adamw_math(w_refs[k][...], grads[k], m_refs[k][...], v_refs[k][...])
            outs[k][...] = grads[k]
            outs[n + k][...] = d
            outs[2 * n + k][...] = m2
            outs[3 * n + k][...] = v2

    shapes = [jax.ShapeDtypeStruct(w.shape, F32) for w in ws] * 4
    smem = pl.BlockSpec(memory_space=pltpu.SMEM)
    outs = pl.pallas_call(
        body, name="small_update", in_specs=[smem] + [_VMEM] * (1 + 3 * n), out_specs=[_VMEM] * (4 * n),
        out_shape=shapes,
    )(chip_arr, summed, *ws, *ms, *vs)
    return outs[0:n], outs[n:2 * n], outs[2 * n:3 * n], outs[3 * n:4 * n]


def _pad_in_rows(w_t):
    return jnp.pad(w_t, ((0, IN_ROWS - IN_SHARD), (0, 0)))


def _forward_backward(xs, target, ffn1_w, rest, ba_p, ffn1_norm_g, mix_norm_g, ret_norm_g, gla_norm_g, ffn2_norm_g,
                      final_norm_g, ffn1_gather=None, rest_plan=None, rest_weights=None, ffn2_plans=None,
                      ffn2_weights=None, ffn2_pairs=None, ffn2_pairs_done=None, early=None, late=None, small_plan=None):
    t = xs.shape[0]
    cos_t, sa_t, sb_t = _rope_tables(t)
    log_gamma = jnp.log(1.0 - 2.0 ** (-5.0 - jnp.arange(HEADS, dtype=F32)))
    lg_t = jnp.broadcast_to(log_gamma[:, None, None], (HEADS, 1, LANES))
    ret_aux = [cos_t, sa_t, sb_t, lg_t]

    if ffn1_gather is None:
        (x1, a1, u1, h1), gathered = _ffn_fwd(xs, ffn1_norm_g, ffn1_w, "ffn1_fwd", hosted=rest_plan)
    else:
        ffn1_shard, ffn1_weights = ffn1_gather
        (x1, a1, u1, h1, wall), gathered = _ffn1_fwd_gathering(xs, ffn1_norm_g, ffn1_shard, "ffn1_fwd",
                                                               hosted=rest_plan)
        ffn1_w = ffn1_weights(wall)
    ffn2_w, w_in_pt, w_out_full, wa2_p = rest if rest_plan is None else rest_weights(gathered)
    plans = [None] * 3 if ffn2_plans is None else ffn2_plans
    (proj, h_mix), got_gate = _mixer_in_fwd(x1, mix_norm_g, w_in_pt, "mixer_in_fwd", hosted=plans[0])
    gla_aux = [proj, wa2_p, ba_p]
    (o_ret, raw_ret, st_ret), got_up = _attn_fwd(True, proj, ret_aux, ret_norm_g, "ret_fwd", hosted=plans[1])
    (o_gla, raw_gla, st_gla), got_down = _attn_fwd(False, proj, gla_aux, gla_norm_g, "gla_fwd", hosted=plans[2])
    if ffn2_plans is not None:
        ffn2_w = ffn2_weights(got_gate + got_up + got_down)
    x2 = _mixer_out_fwd(o_ret, o_gla, w_out_full, x1, "mixer_out_fwd")
    (x3, a2, u2, h2), _ = _ffn_fwd(x2, ffn2_norm_g, ffn2_w, "ffn2_fwd")
    loss_blk, dx3, d_final_g = _final_loss(x3, final_norm_g, target, "final_loss")

    (da2, du2, hid2, dob2, dx2, d_ffn2_g), _ = _ffn_bwd(dx3, x2, ffn2_norm_g, a2, u2, ffn2_w, "ffn2_bwd")
    g_gate2 = _matmul_tn(da2, h2, "ffn2_dgate", out_dtype=BF16)
    g_up2 = _matmul_tn(du2, h2, "ffn2_dup", out_dtype=BF16)
    g_down2 = _matmul_tn(hid2, dob2, "ffn2_ddown", out_dtype=BF16)

    d_o = _matmul_nt(dx2, w_out_full, "mixer_out_bwd")
    g_wout_ret = _matmul_tn(o_ret, dx2, "wout_grad_ret", out_dtype=BF16)
    g_wout_gla = _matmul_tn(o_gla, dx2, "wout_grad_gla", out_dtype=BF16)
    pairs_plan = None if ffn2_pairs is None else ffn2_pairs([g_gate2, g_up2, g_down2])
    (*dproj_ret, d_ret_g), pair_recv = _attn_bwd(True, proj, ret_aux, ret_norm_g, raw_ret, st_ret, d_o, "ret_bwd",
                                                 hosted=pairs_plan)
    if ffn2_pairs is not None:
        ffn2_pairs_done(pair_recv)
    (*dproj_gla, d_gla_g, dlogit, d_ba_p), _ = _attn_bwd(False, proj, gla_aux, gla_norm_g, raw_gla, st_gla, d_o,
                                                        "gla_bwd")
    d_glow = _matmul_nt(dlogit, wa2_p, "gate_low_bwd", out_dtype=BF16)
    g_wa2_p = _matmul_tn(proj[:, PROJ_P - LANES:], dlogit, "gate_w_grad")
    dproj = jnp.concatenate(dproj_ret + dproj_gla + [d_glow], axis=1)
    g_win_p = _matmul_tn(dproj, h_mix, "w_in_grad", tka=PROJ_P // PROJ_TILES, out_dtype=BF16)
    dx1, d_mix_g = _mixer_in_bwd(dproj, w_in_pt, dx2, x1, mix_norm_g, "mixer_in_bwd")
    g_win_t = _unpad_w_in_t(g_win_p[0])
    g_win = jnp.stack([_pad_in_rows(g_win_t[IN_SHARD * p:IN_SHARD * (p + 1)]) for p in range(N_CHIPS)], axis=0)
    g_wout = jnp.concatenate([g_wout_ret[0], g_wout_gla[0]], axis=0).reshape(N_CHIPS, D_MODEL // N_CHIPS, D_MODEL)

    early_grads = [g_win, g_wout] if ffn2_pairs is not None else [g_gate2, g_up2, g_down2, g_win, g_wout]
    early_plan = None if early is None else early(early_grads)
    (da1, du1, hid1, dob1, grad_x, d_ffn1_g), arrived = _ffn_bwd(dx1, xs, ffn1_norm_g, a1, u1, ffn1_w, "ffn1_bwd",
                                                                hosted=early_plan)
    d_ba = d_ba_p.reshape(HEADS, LANES)[:, 0:64].reshape(1, 256)
    small_local = _pack_small(d_ffn1_g, d_mix_g, d_ffn2_g, d_final_g, d_ret_g, d_gla_g, d_ba, g_wa2_p[0], loss_blk)
    late_grads, late_arrived = [], []
    for lhs, rhs, name in ((da1, h1, "ffn1_dgate"), (du1, h1, "ffn1_dup"), (hid1, dob1, "ffn1_ddown")):
        if late is None:
            plan = None
        else:
            plan = late(late_grads[-1], len(late_grads)) if late_grads else small_plan(small_local)
        res = _matmul_tn(lhs, rhs, name, out_dtype=BF16, hosted=plan)
        if plan is not None:
            res, carried = res
            late_arrived += carried
        late_grads.append(res)
    g_gate1, g_up1, g_down1 = late_grads

    return (small_local, grad_x, g_gate1, g_up1, g_down1, g_gate2, g_up2, g_down2, g_win, g_wout, g_wa2_p,
            d_ba_p, d_ffn1_g, d_mix_g, d_ffn2_g, d_final_g, d_ret_g, d_gla_g, arrived, late_arrived)


def kernel(x, ffn1_norm_g, ffn1_w_gate, ffn1_w_up, ffn1_w_down, mix_norm_g, w_in, ret_norm_g, gla_w_a2, gla_b_a, gla_norm_g, w_out, ffn2_norm_g, ffn2_w_gate, ffn2_w_up, ffn2_w_down, final_norm_g, loss_target, m_ffn1_norm_g, m_ffn1_w_gate, m_ffn1_w_up, m_ffn1_w_down, m_mix_norm_g, m_w_in, m_ret_norm_g, m_gla_w_a2, m_gla_b_a, m_gla_norm_g, m_w_out, m_ffn2_norm_g, m_ffn2_w_gate, m_ffn2_w_up, m_ffn2_w_down, m_final_norm_g, v_ffn1_norm_g, v_ffn1_w_gate, v_ffn1_w_up, v_ffn1_w_down, v_mix_norm_g, v_w_in, v_ret_norm_g, v_gla_w_a2, v_gla_b_a, v_gla_norm_g, v_w_out, v_ffn2_norm_g, v_ffn2_w_gate, v_ffn2_w_up, v_ffn2_w_down, v_final_norm_g):
    t = x.shape[1]
    xs = x.reshape(t, D_MODEL)
    target = loss_target.reshape(t, D_MODEL)
    chip = 2 * lax.axis_index("x") + lax.axis_index("y")
    c_arr = lax.axis_index("c").astype(jnp.int32).reshape(1)

    me_arr = chip.astype(jnp.int32).reshape(1)

    pad_rows = _pad_in_rows

    def own_block(gathered, shard):
        return lax.dynamic_update_slice(gathered, shard[None], (chip,) + (0,) * shard.ndim)

    ffn1_shard = _halves(jnp.stack([ffn1_w_gate[0].T, ffn1_w_up[0].T, ffn1_w_down[0]], axis=0).astype(BF16))
    rest_shards = [_halves(pad_rows(w_in[0].T).astype(BF16)[None]),
                   _halves(w_out.astype(BF16)),
                   jnp.concatenate([gla_w_a2.reshape(GATE_RANK, 64), jnp.zeros((GATE_RANK, 64), F32)],
                                   axis=1).reshape(1, 2, 8, LANES)]
    ffn2_shards = [_halves(w.astype(BF16)[None]) for w in (ffn2_w_gate[0].T, ffn2_w_up[0].T, ffn2_w_down[0])]
    def ffn1_weights(gathered):
        return lax.dynamic_update_slice(gathered, ffn1_shard[None], (0,) * 5).reshape(N_CHIPS, 3, FF_SHARD, D_MODEL)

    def rest_weights(gathered):
        win_all, wout_all, wa2_all = [own_block(g, s) for g, s in zip(gathered, rest_shards)]
        win_t = win_all.reshape(N_CHIPS, IN_ROWS, D_MODEL)
        w_in_pt = jnp.zeros((PROJ_P, D_MODEL), BF16)
        for p in range(N_CHIPS):
            w_in_pt = lax.dynamic_update_slice(w_in_pt, win_t[p, 0:IN_SHARD], (IN_SHARD * p, 0))
        wa2_p = jnp.pad(
            wa2_all.reshape(N_CHIPS, GATE_RANK, LANES).transpose(1, 0, 2).reshape(GATE_RANK, HEADS * LANES),
            ((0, LANES - GATE_RANK), (0, 0))).astype(BF16)
        return (None, w_in_pt, wout_all.reshape(D_MODEL, D_MODEL), wa2_p)

    def ffn2_weights(gathered):
        return [own_block(g, s).reshape(N_CHIPS, FF_SHARD, D_MODEL) for g, s in zip(gathered, ffn2_shards)]

    def by_halves(g):
        return g.reshape(g.shape[0], 2, g.shape[1] // 2, g.shape[2])

    def pair_adds(halves, recv, tag):
        return [_pair_add(g, r, c_arr, "pair_add_%s%d" % (tag, k)) for k, (g, r) in enumerate(zip(halves, recv))]

    def pair_sums(grads, tag):
        halves = [by_halves(g) for g in grads]
        return pair_adds(halves, _run_hosted(_pair_exchange_plan(halves), "pair_exchange_" + tag), tag)

    early_sums, ffn2_halves = [], []

    def ffn2_pairs(grads):
        ffn2_halves.extend(by_halves(g) for g in grads)
        return _pair_exchange_plan(ffn2_halves)

    def ffn2_pairs_done(recv):
        early_sums.extend(pair_adds(ffn2_halves, recv, "ffn2_"))

    def early(grads):
        early_sums.extend(pair_sums(grads, "early"))
        return _chip_exchange_plan(early_sums)

    late_sums = []

    def late(grad, number):
        late_sums.extend(pair_sums([grad], "late%d" % number))
        return _chip_exchange_plan(late_sums[-1:], by_peer=True)

    ba_p = jnp.pad(gla_b_a.reshape(HEADS, 64), ((0, 0), (0, 64))).reshape(1, HEADS * LANES)
    fb = _forward_backward(xs, target, None, None, ba_p, ffn1_norm_g, mix_norm_g, ret_norm_g, gla_norm_g,
                           ffn2_norm_g, final_norm_g.reshape(1, D_MODEL), ffn1_gather=(ffn1_shard, ffn1_weights),
                           rest_plan=_gather_plan(rest_shards), rest_weights=rest_weights,
                           ffn2_plans=[_gather_plan(ffn2_shards), None, None], ffn2_weights=ffn2_weights,
                           ffn2_pairs=ffn2_pairs, ffn2_pairs_done=ffn2_pairs_done, early=early, late=late,
                           small_plan=_small_gather_plan)
    (small_local, grad_x, _, _, g_down1, _, _, _, _, _, _, _, _, _, _, _, _, _, early_arrived, late_arrived) = fb
    small_all, late_arrived = late_arrived[0], late_arrived[1:]
    late_arrived = late_arrived + _run_hosted(late(g_down1, 3), "chip_exchange_late")
    mine = [_peer_sum(s, r, "chip_sum_%d" % k) for k, (s, r) in enumerate(zip(late_sums, late_arrived))]
    mine += [_chip_sum(s, r, me_arr, "chip_sum_%d" % (3 + k)) for k, (s, r) in enumerate(zip(early_sums, early_arrived))]
    other = _pair_share(mine)

    device = 2 * chip + lax.axis_index("c")
    small_sum = _sum_devices(lax.dynamic_update_slice(small_all, small_local[None], (device, 0, 0)))
    loss = small_sum[6, 0]

    def rows(n1, nm, n2, nf, nret, ngla, ba, wa2):
        return [n1, nm, n2, nf.reshape(1, D_MODEL), nret, ngla, ba, wa2.reshape(GATE_RANK, 64)]

    small = _small_update(
        small_sum, me_arr,
        rows(ffn1_norm_g, mix_norm_g, ffn2_norm_g, final_norm_g, ret_norm_g, gla_norm_g, gla_b_a, gla_w_a2),
        rows(m_ffn1_norm_g, m_mix_norm_g, m_ffn2_norm_g, m_final_norm_g, m_ret_norm_g, m_gla_norm_g, m_gla_b_a,
             m_gla_w_a2),
        rows(v_ffn1_norm_g, v_mix_norm_g, v_ffn2_norm_g, v_final_norm_g, v_ret_norm_g, v_gla_norm_g, v_gla_b_a,
             v_gla_w_a2))
    s_grad, s_delta, s_m, s_v = [
        [*o[0:3], o[3].reshape(D_MODEL), *o[4:7], o[7].reshape(1, GATE_RANK, 64)] for o in small]

    def big(k, w, m, v, name, to_2d, from_2d):
        outs4 = _adamw_halves(to_2d(w), mine[k], other[k], to_2d(m), to_2d(v), c_arr, name)
        return [from_2d(z) for z in outs4]

    plain = (lambda w: w[0], lambda z: z[None])
    transposed = (lambda w: w[0].T, lambda z: z.T[None])
    in_proj = (lambda w: pad_rows(w[0].T), lambda z: z[0:IN_SHARD].T[None])
    r_g1 = big(0, ffn1_w_gate, m_ffn1_w_gate, v_ffn1_w_gate, "adamw_ffn1_gate", *transposed)
    r_u1 = big(1, ffn1_w_up, m_ffn1_w_up, v_ffn1_w_up, "adamw_ffn1_up", *transposed)
    r_d1 = big(2, ffn1_w_down, m_ffn1_w_down, v_ffn1_w_down, "adamw_ffn1_down", *plain)
    r_g2 = big(3, ffn2_w_gate, m_ffn2_w_gate, v_ffn2_w_gate, "adamw_ffn2_gate", *transposed)
    r_u2 = big(4, ffn2_w_up, m_ffn2_w_up, v_ffn2_w_up, "adamw_ffn2_up", *transposed)
    r_d2 = big(5, ffn2_w_down, m_ffn2_w_down, v_ffn2_w_down, "adamw_ffn2_down", *plain)
    r_in = big(6, w_in, m_w_in, v_w_in, "adamw_w_in", *in_proj)
    r_out = big(7, w_out, m_w_out, v_w_out, "adamw_w_out", *plain)

    def leaves(k, smalls):
        n1, nm, n2, nf, nret, ngla, ba, wa2 = smalls
        return [n1, r_g1[k], r_u1[k], r_d1[k], nm, r_in[k], nret, wa2, ba, ngla, r_out[k], n2, r_g2[k], r_u2[k], r_d2[k], nf]

    outs = [loss, grad_x.reshape(x.shape)]
    outs += leaves(0, s_grad) + leaves(1, s_delta) + leaves(2, s_m) + leaves(3, s_v)
    return tuple(outs)
```

```python
import functools

import jax
import jax.numpy as jnp
from jax import lax
from jax.experimental import pallas as pl
from jax.experimental.pallas import tpu as pltpu

F32, BF16 = jnp.float32, jnp.bfloat16
MESH = pl.DeviceIdType.MESH
ANY = pl.BlockSpec(memory_space=pl.ANY)

D_MODEL = 1024
D_FF = 2816
N_CHIPS = 4
FF_SHARD = D_FF // N_CHIPS
IN_WIDTH = 3088
IN_SHARD = IN_WIDTH // N_CHIPS
IN_ROWS = 800
CHUNK = 64
HEADS = 4
LANES = 128
PROJ_P = 3072 + LANES
PROJ_TILES = 5
GATE_RANK = 16
QK_SCALE = 0.125
GATE_NORM = 16.0
RMS_EPS = 1e-6
ROPE_BASE = 10000.0
ADAM_LR, ADAM_B1, ADAM_B2, ADAM_EPS, ADAM_WD, ADAM_STEP = 0.001, 0.9, 0.999, 1e-08, 0.01, 10
SMALL_ROWS = 32
TOKEN_TILE = 512
ATTN_TILE = 512

_ARB2 = pltpu.CompilerParams(dimension_semantics=("arbitrary", "arbitrary"))
_ARB1 = pltpu.CompilerParams(dimension_semantics=("arbitrary",))
_ARB3 = pltpu.CompilerParams(dimension_semantics=("arbitrary", "arbitrary", "arbitrary"))


def _dot(a, b):
    return jnp.dot(a, b, preferred_element_type=F32)


def _dot_nt(a, b):
    return lax.dot_general(a, b, (((1,), (1,)), ((), ())), preferred_element_type=F32)


def _dot_tn(a, b):
    return lax.dot_general(a, b, (((0,), (0,)), ((), ())), preferred_element_type=F32)


def _rms_scale(xv):
    return lax.rsqrt(jnp.mean(xv * xv, axis=-1, keepdims=True) + RMS_EPS)


def _rms_bwd(dh, xv, g):
    r = _rms_scale(xv)
    xhat = xv * r
    dxhat = dh * g
    dx = r * (dxhat - xhat * jnp.mean(dxhat * xhat, axis=-1, keepdims=True))
    return dx, jnp.sum(dh * xhat, axis=0, keepdims=True)


def _silu_grad(a, sg):
    return sg * (1.0 + a * (1.0 - sg))


class _Hosted:
    def __init__(self, arrays, out_shapes, n_sems, start, finish, middle=None):
        self.arrays, self.out_shapes, self.n_sems = list(arrays), list(out_shapes), n_sems
        self.start, self.finish = start, finish
        self.middle = middle if middle is not None else (lambda *refs: None)


def _call(body, args, *, name, grid, in_specs, out_specs, out_shape, scratch_shapes, compiler_params, hosted=None):
    if hosted is None:
        outs = pl.pallas_call(body, name=name, grid=grid, in_specs=in_specs, out_specs=out_specs, out_shape=out_shape,
                              scratch_shapes=scratch_shapes, compiler_params=compiler_params)(*args)
        return list(outs), []
    n_in, n_out, n_sc, nh = len(in_specs), len(out_specs), len(scratch_shapes), len(hosted.arrays)

    def wrapped(*refs):
        ins, h_in = refs[:n_in], refs[n_in:n_in + nh]
        outs, h_out = refs[n_in + nh:n_in + nh + n_out], refs[n_in + nh + n_out:n_in + 2 * nh + n_out]
        rest = refs[n_in + 2 * nh + n_out:]
        scratch, (send_sems, recv_sems) = rest[:n_sc], rest[n_sc:]
        step = functools.reduce(lambda flat, d: flat * grid[d] + pl.program_id(d), range(len(grid)), 0)
        total = functools.reduce(lambda a, b: a * b, grid)

        @pl.when(step == 0)
        def _():
            hosted.start(h_in, h_out, send_sems, recv_sems)

        @pl.when(step == total // 2)
        def _():
            hosted.middle(h_in, h_out, send_sems, recv_sems)

        body(*ins, *outs, *scratch)
        last = step == total - 1

        @pl.when(last)
        def _():
            hosted.finish(h_in, h_out, send_sems, recv_sems)

    sems = [pltpu.SemaphoreType.DMA((hosted.n_sems,)), pltpu.SemaphoreType.DMA((hosted.n_sems,))]
    outs = pl.pallas_call(
        wrapped, name=name, grid=grid, in_specs=list(in_specs) + [ANY] * nh, out_specs=list(out_specs) + [ANY] * nh,
        out_shape=list(out_shape) + hosted.out_shapes, scratch_shapes=list(scratch_shapes) + sems,
        compiler_params=compiler_params)(*args, *hosted.arrays)
    return list(outs[:n_out]), list(outs[n_out:])


def _run_hosted(hosted, name):
    nh = len(hosted.arrays)

    def body(*refs):
        h_in, h_out, (send_sems, recv_sems) = refs[:nh], refs[nh:2 * nh], refs[2 * nh:]
        hosted.start(h_in, h_out, send_sems, recv_sems)
        hosted.middle(h_in, h_out, send_sems, recv_sems)
        hosted.finish(h_in, h_out, send_sems, recv_sems)

    sems = [pltpu.SemaphoreType.DMA((hosted.n_sems,)), pltpu.SemaphoreType.DMA((hosted.n_sems,))]
    return list(pl.pallas_call(body, name=name, in_specs=[ANY] * nh, out_specs=[ANY] * nh,
                               out_shape=hosted.out_shapes, scratch_shapes=sems)(*hosted.arrays))


def _ffn_weight_operands(ffn_w, chunk_maps):
    if isinstance(ffn_w, (list, tuple)):
        specs = [pl.BlockSpec((None, FF_SHARD, D_MODEL), lambda *g, m=m: (m(*g), 0, 0)) for m in chunk_maps]
        return list(ffn_w), specs
    specs = [pl.BlockSpec((None, None, FF_SHARD, D_MODEL), lambda *g, m=m, k=kind: (m(*g), k, 0, 0))
             for kind, m in enumerate(chunk_maps)]
    return [ffn_w] * 3, specs


def _pipeline_items(steps):
    def cur(s):
        c = jnp.minimum(s, steps - 1)
        return c // N_CHIPS, c % N_CHIPS

    def prev(s):
        p = jnp.maximum(s - 1, 0)
        return p // N_CHIPS, p % N_CHIPS

    return cur, prev


def _ffn_fwd(x, g, ffn_w, name, hosted=None):
    t = x.shape[0]
    tm = min(t, TOKEN_TILE)

    def body(x_ref, g_ref, wg_ref, wu_ref, wd_ref, xo_ref, a_ref, u_ref, h_ref, acc_ref):
        j = pl.program_id(1)

        @pl.when(j == 0)
        def _():
            xv = x_ref[...]
            h_ref[...] = ((xv * _rms_scale(xv)) * g_ref[...]).astype(BF16)
            acc_ref[...] = jnp.zeros_like(acc_ref)

        h = h_ref[...]
        a = _dot_nt(h, wg_ref[...])
        u = _dot_nt(h, wu_ref[...])
        a_ref[...] = a.astype(BF16)
        u_ref[...] = u.astype(BF16)
        hid = (a * jax.nn.sigmoid(a)) * u
        acc_ref[...] += _dot(hid.astype(BF16), wd_ref[...])

        @pl.when(j == N_CHIPS - 1)
        def _():
            xo_ref[...] = x_ref[...] + 0.5 * acc_ref[...]

    tok = pl.BlockSpec((tm, D_MODEL), lambda i, j: (i, 0))
    act = pl.BlockSpec((None, tm, FF_SHARD), lambda i, j: (j, i, 0))
    w_arrays, weights = _ffn_weight_operands(ffn_w, [lambda i, j: j] * 3)
    return _call(
        body, (x, g, *w_arrays), name=name, grid=(t // tm, N_CHIPS),
        in_specs=[tok, pl.BlockSpec((1, D_MODEL), lambda i, j: (0, 0))] + weights,
        out_specs=[tok, act, act, tok],
        out_shape=[jax.ShapeDtypeStruct((t, D_MODEL), F32),
                   jax.ShapeDtypeStruct((N_CHIPS, t, FF_SHARD), BF16),
                   jax.ShapeDtypeStruct((N_CHIPS, t, FF_SHARD), BF16),
                   jax.ShapeDtypeStruct((t, D_MODEL), BF16)],
        scratch_shapes=[pltpu.VMEM((tm, D_MODEL), F32)],
        compiler_params=_ARB2, hosted=hosted)


def _ffn1_fwd_gathering(x, g, shard, name, hosted=None):
    t = x.shape[0]
    tm = min(t, TOKEN_TILE)
    nt = t // tm
    nh = 0 if hosted is None else len(hosted.arrays)

    def body(*refs):
        x_ref, g_ref, shard_ref = refs[0:3]
        h_in = refs[3:3 + nh]
        xo_ref, a_ref, u_ref, h_ref, wall = refs[3 + nh:8 + nh]
        h_out = refs[8 + nh:8 + 2 * nh]
        acc, h_all, wbuf, load_sems, send_sems, recv_sems = refs[8 + 2 * nh:14 + 2 * nh]
        carried_sems = refs[14 + 2 * nh:]
        k, i = pl.program_id(0), pl.program_id(1)
        legs, _ = _gather_legs(shard_ref, wall, send_sems, recv_sems, 0, True)
        begin, pass_on, _ = _gather_steps(legs, True)

        def load(chunk, src):
            return pltpu.make_async_copy(src, wbuf.at[chunk % 2], load_sems.at[chunk % 2])

        @pl.when((k == 0) & (i == 0))
        def _():
            begin()
            load(0, shard_ref).start()
            load(0, shard_ref).wait()

        @pl.when((k == 1) & (i == 0))
        def _():
            pass_on()
            if hosted is not None:
                hosted.start(h_in, h_out, *carried_sems)
            legs["pass_y"][1].wait_recv()
            load(1, wall.at[PEER_SLOT[1]]).start()
            load(1, wall.at[PEER_SLOT[1]]).wait()

        @pl.when((k == 1) & (i == nt // 2))
        def _():
            legs["pass_x"][1].wait_recv()
            load(2, wall.at[PEER_SLOT[0]]).start()

        @pl.when((k == 2) & (i == 0))
        def _():
            load(2, wall.at[PEER_SLOT[0]]).wait()

        @pl.when((k == 2) & (i == nt // 2))
        def _():
            legs["fwd_y"][1].wait_recv()
            legs["pass_d0"][0].start()
            legs["fwd_x"][1].wait_recv()
            legs["pass_d1"][0].start()
            legs["pass_d0"][1].wait_recv()
            legs["pass_d1"][1].wait_recv()
            load(3, wall.at[PEER_SLOT[2]]).start()
            if hosted is not None:
                hosted.middle(h_in, h_out, *carried_sems)

        @pl.when((k == 3) & (i == 0))
        def _():
            load(3, wall.at[PEER_SLOT[2]]).wait()

        @pl.when(k == 0)
        def _():
            xv = x_ref[...]
            h0 = ((xv * _rms_scale(xv)) * g_ref[...]).astype(BF16)
            h_all[i] = h0
            h_ref[...] = h0

        h = h_all[i]
        wg, wu, wd = (wbuf[k % 2, kind].reshape(FF_SHARD, D_MODEL) for kind in range(3))
        a = _dot_nt(h, wg)
        u = _dot_nt(h, wu)
        a_ref[...] = a.astype(BF16)
        u_ref[...] = u.astype(BF16)
        part = _dot(((a * jax.nn.sigmoid(a)) * u).astype(BF16), wd)

        @pl.when(k == 0)
        def _():
            acc[i] = part

        @pl.when(k > 0)
        def _():
            acc[i] += part

        @pl.when(k == N_CHIPS - 1)
        def _():
            xo_ref[...] = x_ref[...] + 0.5 * acc[i]

        @pl.when((k == N_CHIPS - 1) & (i == nt - 1))
        def _():
            for pair in legs.values():
                pair[0].wait_send()
            if hosted is not None:
                hosted.finish(h_in, h_out, *carried_sems)

    def first_or_last(k):
        return (k == 0) | (k == N_CHIPS - 1)

    tok = lambda keep: pl.BlockSpec((tm, D_MODEL), lambda k, i: (jnp.where(keep(k), i, 0), 0))
    act = pl.BlockSpec((None, tm, FF_SHARD), lambda k, i: (k, i, 0))
    act_shape = jax.ShapeDtypeStruct((N_CHIPS, t, FF_SHARD), BF16)
    carried = [] if hosted is None else [pltpu.SemaphoreType.DMA((hosted.n_sems,))] * 2
    outs = pl.pallas_call(
        body, name=name, grid=(N_CHIPS, nt),
        in_specs=[tok(first_or_last), pl.BlockSpec((1, D_MODEL), lambda k, i: (0, 0)), ANY] + [ANY] * nh,
        out_specs=[tok(lambda k: k == N_CHIPS - 1), act, act,
                   pl.BlockSpec((tm, D_MODEL), lambda k, i: (jnp.where(k == 0, i, nt - 1), 0)), ANY] + [ANY] * nh,
        out_shape=[jax.ShapeDtypeStruct((t, D_MODEL), F32), act_shape, act_shape,
                   jax.ShapeDtypeStruct((t, D_MODEL), BF16),
                   jax.ShapeDtypeStruct((N_CHIPS,) + shard.shape, shard.dtype)]
                  + ([] if hosted is None else hosted.out_shapes),
        scratch_shapes=[pltpu.VMEM((nt, tm, D_MODEL), F32), pltpu.VMEM((nt, tm, D_MODEL), BF16),
                        pltpu.VMEM((2,) + shard.shape, shard.dtype), pltpu.SemaphoreType.DMA((2,)),
                        pltpu.SemaphoreType.DMA((8,)), pltpu.SemaphoreType.DMA((8,))] + carried,
        compiler_params=_ARB2,
    )(x, g, shard, *([] if hosted is None else hosted.arrays))
    return list(outs[:5]), list(outs[5:])


def _ffn_bwd(dxo, x, g, a4, u4, ffn_w, name, hosted=None):
    t = x.shape[0]
    tm = min(t, TOKEN_TILE)
    steps = (t // tm) * N_CHIPS
    cur, prev = _pipeline_items(steps)


    def body(dxo_ref, dxo_prev_ref, x_ref, g_ref, a_ref, u_ref, wg_ref, wu_ref, wd_ref,
             da_ref, du_ref, hid_ref, dob_ref, dx_ref, dg_ref, acc_ref, da_slots, du_slots):
        s = pl.program_id(0)
        jc, jp = cur(s)[1], prev(s)[1]
        slot = s % 2

        @pl.when(s == 0)
        def _():
            dg_ref[...] = jnp.zeros_like(dg_ref)
            acc_ref[...] = jnp.zeros_like(acc_ref)
            da_slots[...] = jnp.zeros_like(da_slots)
            du_slots[...] = jnp.zeros_like(du_slots)

        @pl.when(jc == 0)
        def _():
            dob_ref[...] = (0.5 * dxo_ref[...]).astype(BF16)

        dhid = _dot_nt(dob_ref[...], wd_ref[...])
        a = a_ref[...].astype(F32)
        u = u_ref[...].astype(F32)
        sg = jax.nn.sigmoid(a)
        sl = a * sg
        hid_ref[...] = (sl * u).astype(BF16)
        du = (dhid * sl).astype(BF16)
        da = (dhid * u * _silu_grad(a, sg)).astype(BF16)
        du_ref[...] = du
        da_ref[...] = da
        acc_ref[...] += _dot(da_slots[1 - slot], wg_ref[...]) + _dot(du_slots[1 - slot], wu_ref[...])
        da_slots[slot] = da
        du_slots[slot] = du

        @pl.when((jp == N_CHIPS - 1) & (s > 0))
        def _():
            dx, dg = _rms_bwd(acc_ref[...], x_ref[...], g_ref[...])
            dx_ref[...] = dxo_prev_ref[...] + dx
            dg_ref[...] += dg
            acc_ref[...] = jnp.zeros_like(acc_ref)

    tok_cur = pl.BlockSpec((tm, D_MODEL), lambda s: (cur(s)[0], 0))
    tok_prev = pl.BlockSpec((tm, D_MODEL), lambda s: (prev(s)[0], 0))
    act = pl.BlockSpec((None, tm, FF_SHARD), lambda s: (cur(s)[1], cur(s)[0], 0))
    row = pl.BlockSpec((1, D_MODEL), lambda s: (0, 0))
    w_arrays, weights = _ffn_weight_operands(ffn_w, [lambda s: prev(s)[1], lambda s: prev(s)[1], lambda s: cur(s)[1]])
    act_shape = jax.ShapeDtypeStruct((N_CHIPS, t, FF_SHARD), BF16)
    return _call(
        body, (dxo, dxo, x, g, a4, u4, *w_arrays), name=name, grid=(steps + 1,),
        in_specs=[tok_cur, tok_prev, tok_prev, row, act, act] + weights,
        out_specs=[act, act, act, tok_cur, tok_prev, row],
        out_shape=[act_shape, act_shape, act_shape,
                   jax.ShapeDtypeStruct((t, D_MODEL), BF16),
                   jax.ShapeDtypeStruct((t, D_MODEL), F32),
                   jax.ShapeDtypeStruct((1, D_MODEL), F32)],
        scratch_shapes=[pltpu.VMEM((tm, D_MODEL), F32), pltpu.VMEM((2, tm, FF_SHARD), BF16),
                        pltpu.VMEM((2, tm, FF_SHARD), BF16)],
        compiler_params=_ARB1, hosted=hosted)


def _matmul_tn(a, b, name, tka=None, out_dtype=F32, hosted=None):
    a3, b3 = a.ndim == 3, b.ndim == 3
    nb = a.shape[0] if a3 else (b.shape[0] if b3 else 1)
    t, ka, n = a.shape[-2], a.shape[-1], b.shape[-1]
    tka = ka if tka is None else tka
    tk = min(t, 4 * TOKEN_TILE)
    nk = t // tk

    def body(a_ref, b_ref, o_ref, acc_ref):
        k = pl.program_id(2)

        @pl.when(k == 0)
        def _():
            acc_ref[...] = jnp.zeros_like(acc_ref)

        acc_ref[...] += _dot_tn(a_ref[...].astype(BF16), b_ref[...].astype(BF16))

        @pl.when(k == nk - 1)
        def _():
            o_ref[...] = acc_ref[...].astype(out_dtype)

    a_spec = (pl.BlockSpec((None, tk, tka), lambda i, j, k: (i, k, j)) if a3
              else pl.BlockSpec((tk, tka), lambda i, j, k: (k, j)))
    b_spec = (pl.BlockSpec((None, tk, n), lambda i, j, k: (i, k, 0)) if b3
              else pl.BlockSpec((tk, n), lambda i, j, k: (k, 0)))
    outs, carried = _call(
        body, (a, b), name=name, grid=(nb, ka // tka, t // tk),
        in_specs=[a_spec, b_spec],
        out_specs=[pl.BlockSpec((None, tka, n), lambda i, j, k: (i, j, 0))],
        out_shape=[jax.ShapeDtypeStruct((nb, ka, n), out_dtype)],
        scratch_shapes=[pltpu.VMEM((tka, n), F32)],
        compiler_params=_ARB3, hosted=hosted)
    return outs[0] if hosted is None else (outs[0], carried)


def _matmul_nt(a, w, name, out_dtype=F32):
    t, k = a.shape
    n = w.shape[0]
    tm = min(t, TOKEN_TILE)

    def body(a_ref, w_ref, o_ref):
        o_ref[...] = _dot_nt(a_ref[...].astype(BF16), w_ref[...]).astype(out_dtype)

    return pl.pallas_call(
        body, name=name, grid=(t // tm,),
        in_specs=[pl.BlockSpec((tm, k), lambda i: (i, 0)), pl.BlockSpec((n, k), lambda i: (0, 0))],
        out_specs=pl.BlockSpec((tm, n), lambda i: (i, 0)),
        out_shape=jax.ShapeDtypeStruct((t, n), out_dtype),
        compiler_params=_ARB1,
    )(a, w)


def _mixer_in_bwd(dproj, w_in_pt, dres, x, g, name):
    t, k = dproj.shape
    tm = min(t, TOKEN_TILE)

    def body(a_ref, w_ref, dres_ref, x_ref, g_ref, dx_ref, dg_ref):
        @pl.when(pl.program_id(0) == 0)
        def _():
            dg_ref[...] = jnp.zeros_like(dg_ref)

        dh = _dot(a_ref[...], w_ref[...])
        dx, dg = _rms_bwd(dh, x_ref[...], g_ref[...])
        dx_ref[...] = dres_ref[...] + dx
        dg_ref[...] += dg

    tok = pl.BlockSpec((tm, D_MODEL), lambda i: (i, 0))
    row = pl.BlockSpec((1, D_MODEL), lambda i: (0, 0))
    return pl.pallas_call(
        body, name=name, grid=(t // tm,),
        in_specs=[pl.BlockSpec((tm, k), lambda i: (i, 0)), pl.BlockSpec((k, D_MODEL), lambda i: (0, 0)), tok, tok, row],
        out_specs=[tok, row],
        out_shape=[jax.ShapeDtypeStruct((t, D_MODEL), F32), jax.ShapeDtypeStruct((1, D_MODEL), F32)],
        compiler_params=_ARB1,
    )(dproj, w_in_pt, dres, x, g)


def _mixer_in_fwd(x, g, w_in_pt, name, hosted=None):
    t = x.shape[0]
    tm = min(t, TOKEN_TILE)
    tn = PROJ_P // PROJ_TILES

    def body(x_ref, g_ref, w_ref, p_ref, h_ref):
        @pl.when(pl.program_id(1) == 0)
        def _():
            xv = x_ref[...]
            h_ref[...] = ((xv * _rms_scale(xv)) * g_ref[...]).astype(BF16)

        p_ref[...] = _dot_nt(h_ref[...], w_ref[...])

    tok = pl.BlockSpec((tm, D_MODEL), lambda i, j: (i, 0))
    return _call(
        body, (x, g, w_in_pt), name=name, grid=(t // tm, PROJ_TILES),
        in_specs=[tok, pl.BlockSpec((1, D_MODEL), lambda i, j: (0, 0)),
                  pl.BlockSpec((tn, D_MODEL), lambda i, j: (j, 0))],
        out_specs=[pl.BlockSpec((tm, tn), lambda i, j: (i, j)), tok],
        out_shape=[jax.ShapeDtypeStruct((t, PROJ_P), F32), jax.ShapeDtypeStruct((t, D_MODEL), BF16)],
        scratch_shapes=[], compiler_params=_ARB2, hosted=hosted)


def _mixer_out_fwd(o_ret, o_gla, w_out, x, name):
    t = x.shape[0]
    tm = min(t, TOKEN_TILE)
    half = HEADS * LANES

    def body(a_ref, b_ref, w_ref, x_ref, o_ref):
        o_ref[...] = x_ref[...] + _dot(a_ref[...], w_ref[0:half, :]) + _dot(b_ref[...], w_ref[half:2 * half, :])

    tok = pl.BlockSpec((tm, D_MODEL), lambda i: (i, 0))
    hb = pl.BlockSpec((tm, half), lambda i: (i, 0))
    return pl.pallas_call(
        body, name=name, grid=(t // tm,),
        in_specs=[hb, hb, pl.BlockSpec((2 * half, D_MODEL), lambda i: (0, 0)), tok],
        out_specs=tok, out_shape=jax.ShapeDtypeStruct((t, D_MODEL), F32),
        compiler_params=_ARB1,
    )(o_ret, o_gla, w_out, x)


def _final_loss(x, g, target, name):
    t = x.shape[0]
    tm = min(t, TOKEN_TILE)

    def body(x_ref, g_ref, t_ref, l_ref, dx_ref, dg_ref):
        @pl.when(pl.program_id(0) == 0)
        def _():
            l_ref[...] = jnp.zeros_like(l_ref)
            dg_ref[...] = jnp.zeros_like(dg_ref)

        xv = x_ref[...]
        gv = g_ref[...]
        err = (xv * _rms_scale(xv)) * gv - t_ref[...]
        l_ref[...] += 0.5 * jnp.sum(jnp.mean(err * err, axis=-1, keepdims=True), axis=0, keepdims=True)
        dx, dg = _rms_bwd(err * (1.0 / D_MODEL), xv, gv)
        dx_ref[...] = dx
        dg_ref[...] += dg

    tok = pl.BlockSpec((tm, D_MODEL), lambda i: (i, 0))
    row = pl.BlockSpec((1, D_MODEL), lambda i: (0, 0))
    return pl.pallas_call(
        body, name=name, grid=(t // tm,),
        in_specs=[tok, row, tok],
        out_specs=[pl.BlockSpec((8, LANES), lambda i: (0, 0)), tok, row],
        out_shape=[jax.ShapeDtypeStruct((8, LANES), F32), jax.ShapeDtypeStruct((t, D_MODEL), F32),
                   jax.ShapeDtypeStruct((1, D_MODEL), F32)],
        compiler_params=_ARB1,
    )(x, g, target)


def _rot(v, cos, sa, sb):
    return v * cos + pltpu.roll(v, 96, 1) * sa + pltpu.roll(v, 32, 1) * sb


def _rot_t(d, cos, sa, sb):
    return d * cos + pltpu.roll(d * sa, 32, 1) + pltpu.roll(d * sb, 96, 1)


def _bmm(a, b):
    return jnp.einsum("cik,ckj->cij", a, b, preferred_element_type=F32)


def _bmm_nt(a, b):
    return jnp.einsum("cik,cjk->cij", a, b, preferred_element_type=F32)


def _bmm_tn(a, b):
    return jnp.einsum("cki,ckj->cij", a, b, preferred_element_type=F32)


def _masked_sum(mask, x):
    hi = x.astype(BF16)
    r1 = x - hi.astype(F32)
    mid = r1.astype(BF16)
    lo = (r1 - mid.astype(F32)).astype(BF16)
    return _bmm(mask, hi) + _bmm(mask, mid) + _bmm(mask, lo)


PAIR = 2


def _tile_inputs(is_ret, qkvg_refs, aux, nc):
    shape3 = (nc, CHUNK, LANES)
    q_ref, k_ref, v_ref, g_ref = qkvg_refs
    low_lanes = lax.broadcasted_iota(jnp.int32, (1, LANES), 1) < 64
    ri = lax.broadcasted_iota(jnp.int32, (PAIR * nc, CHUNK, CHUNK), 1)
    ci = lax.broadcasted_iota(jnp.int32, (PAIR * nc, CHUNK, CHUNK), 2)
    qs, ks, vs, bs, gates, extra = [], [], [], [], [], []
    for hd in range(PAIR):
        q_blk, k_blk = q_ref[...], k_ref[...]
        if hd == 1:
            q_blk, k_blk = pltpu.roll(q_blk, 64, 1), pltpu.roll(k_blk, 64, 1)
        q_raw, k_raw = jnp.where(low_lanes, q_blk, 0.0), jnp.where(low_lanes, k_blk, 0.0)
        vs.append(v_ref[:, LANES * hd:LANES * (hd + 1)].reshape(shape3))
        gates.append(g_ref[:, LANES * hd:LANES * (hd + 1)])
        if is_ret:
            cos_ref, sa_ref, sb_ref, lg_ref = aux
            cos, sa, sb = cos_ref[...], sa_ref[...], sb_ref[...]
            q = _rot(q_raw, cos, sa, sb)
            k = _rot(k_raw, cos, sa, sb) * QK_SCALE
            steps = (lax.broadcasted_iota(jnp.int32, shape3, 1) + 1).astype(F32)
            bs.append(steps * lg_ref[hd])
            extra.append(jnp.exp(jnp.abs(ri[0:nc] - ci[0:nc]).astype(F32) * lg_ref[hd][:, 0:CHUNK]))
        else:
            glow_ref, wa2_ref, ba_ref = aux
            lanes = slice(LANES * hd, LANES * (hd + 1))
            logit = _dot(glow_ref[...].astype(BF16), wa2_ref[:, lanes]) + ba_ref[:, lanes]
            la = (jnp.minimum(logit, 0.0) - jnp.log1p(jnp.exp(-jnp.abs(logit)))) * (1.0 / GATE_NORM)
            bs.append(_masked_sum((ci[0:nc] <= ri[0:nc]).astype(BF16), la.reshape(shape3)))
            extra.append(logit)
            q = q_raw * QK_SCALE
            k = k_raw
        qs.append(q.reshape(shape3))
        ks.append(k.reshape(shape3))
    cat = lambda parts: jnp.concatenate(parts, axis=0)
    return cat(qs), cat(ks), cat(vs), gates, cat(bs), extra, ri, ci


def _tile_scores(q, k, b, ri, ci):
    mid = b[:, CHUNK // 2 - 1:CHUNK // 2, :]
    ep = jnp.exp(b - mid)
    en = jnp.exp(mid - b)
    qt, kt, qh, kh = q * ep, k * en, q * en, k * ep
    low = _bmm_nt(qt.astype(BF16), kt.astype(BF16))
    upp = _bmm_nt(qh.astype(BF16), kh.astype(BF16))
    scores = jnp.where(ci <= ri, low, upp)
    return scores, ep, en, qt, kt, qh, kh


def _attn_specs(is_ret, t, tb, imap_t):
    nb = t // tb
    base = 0 if is_ret else 12
    wide = PAIR * LANES
    proj = [pl.BlockSpec((tb, LANES), lambda p, i: (imap_t(i), base + p)),
            pl.BlockSpec((tb, LANES), lambda p, i: (imap_t(i), base + 2 + p)),
            pl.BlockSpec((tb, wide), lambda p, i: (imap_t(i), (base + 4) // 2 + p)),
            pl.BlockSpec((tb, wide), lambda p, i: (imap_t(i), (base + 8) // 2 + p))]
    lane_t = pl.BlockSpec((tb, LANES), lambda p, i: (imap_t(i), 0))
    if is_ret:
        aux = [lane_t, lane_t, lane_t, pl.BlockSpec((PAIR, 1, LANES), lambda p, i: (p, 0, 0))]
    else:
        aux = [pl.BlockSpec((tb, LANES), lambda p, i: (imap_t(i), PROJ_P // LANES - 1)),
               pl.BlockSpec((LANES, wide), lambda p, i: (0, p)),
               pl.BlockSpec((1, wide), lambda p, i: (0, p))]
    gain = pl.BlockSpec((1, wide), lambda p, i: (0, p))
    pair_t = pl.BlockSpec((tb, wide), lambda p, i: (imap_t(i), p))
    narrow_t = pl.BlockSpec((tb, LANES), lambda p, i: (imap_t(i), p))
    state = pl.BlockSpec((PAIR, tb // CHUNK, LANES, LANES), lambda p, i: (p, imap_t(i), 0, 0))
    return nb, proj, aux, gain, pair_t, narrow_t, state


def _attn_fwd(is_ret, proj, aux_arrays, gain, name, hosted=None):
    t = proj.shape[0]
    tb = min(t, ATTN_TILE)
    nc = tb // CHUNK
    n_aux = 4 if is_ret else 3
    nb, proj_spec, aux_specs, gain_spec, pair_t, _, state_spec = _attn_specs(is_ret, t, tb, lambda i: i)

    def body(*refs):
        qkvg_refs = refs[0:4]
        aux = refs[4:4 + n_aux]
        gn_ref, ofin_ref, oraw_ref, st_ref, state = refs[4 + n_aux:]

        @pl.when(pl.program_id(1) == 0)
        def _():
            state[...] = jnp.zeros_like(state)

        q, k, v, gates, b, extra, ri, ci = _tile_inputs(is_ret, qkvg_refs, aux, nc)
        if is_ret:
            scores = _bmm_nt(q.astype(BF16), k.astype(BF16)) * jnp.concatenate(extra, axis=0)
        else:
            scores = _tile_scores(q, k, b, ri, ci)[0]
        vb = v.astype(BF16)
        intra = _bmm(scores.astype(BF16), vb)
        b_last = b[:, CHUNK - 1:CHUNK, :]
        e_last = jnp.exp(b_last)
        grow = _bmm_tn(vb, (k * jnp.exp(b_last - b)).astype(BF16))
        for hd in range(PAIR):
            st = state[hd]
            for c in range(nc):
                st_ref[hd, c] = st
                st = st * e_last[hd * nc + c] + grow[hd * nc + c]
            state[hd] = st
        starts = st_ref[...].reshape(PAIR * nc, LANES, LANES)
        out3 = intra + _bmm_nt((q * jnp.exp(b)).astype(BF16), starts.astype(BF16))
        for hd in range(PAIR):
            lanes = slice(LANES * hd, LANES * (hd + 1))
            out = out3[hd * nc:(hd + 1) * nc].reshape(tb, LANES)
            oraw_ref[:, lanes] = out
            normed = out * _rms_scale(out)
            gate = gates[hd]
            ofin_ref[:, lanes] = ((normed * gn_ref[:, lanes]) * (gate * jax.nn.sigmoid(gate))).astype(BF16)

    width = HEADS * LANES
    return _call(
        body, (proj, proj, proj, proj, *aux_arrays, gain), name=name, grid=(HEADS // PAIR, nb),
        in_specs=proj_spec + aux_specs + [gain_spec],
        out_specs=[pair_t, pair_t, state_spec],
        out_shape=[jax.ShapeDtypeStruct((t, width), BF16), jax.ShapeDtypeStruct((t, width), F32),
                   jax.ShapeDtypeStruct((HEADS, t // CHUNK, LANES, LANES), F32)],
        scratch_shapes=[pltpu.VMEM((PAIR, LANES, LANES), F32)],
        compiler_params=_ARB2, hosted=hosted)


def _attn_bwd(is_ret, proj, aux_arrays, gain, o_raw, states, d_out, name, hosted=None):
    t = proj.shape[0]
    tb = min(t, ATTN_TILE)
    nc = tb // CHUNK
    n_aux = 4 if is_ret else 3
    nblk = t // tb
    nb, proj_spec, aux_specs, gain_spec, pair_t, narrow_t, state_spec = _attn_specs(
        is_ret, t, tb, lambda i: nblk - 1 - i)
    base = 0 if is_ret else HEADS // PAIR
    dout_spec = pl.BlockSpec((tb, PAIR * LANES), lambda p, i: (nblk - 1 - i, base + p))

    def body(*refs):
        qkvg_refs = refs[0:4]
        aux = refs[4:4 + n_aux]
        gn_ref, oraw_ref, st_ref, dfin_ref = refs[4 + n_aux:8 + n_aux]
        dq_ref, dk_ref, dv_ref, dgate_ref, dgn_ref = refs[8 + n_aux:13 + n_aux]
        if is_ret:
            dstate, dafter_ref = refs[13 + n_aux:]
        else:
            dlogit_ref, dba_ref, dstate, dafter_ref = refs[13 + n_aux:]

        @pl.when(pl.program_id(1) == 0)
        def _():
            dstate[...] = jnp.zeros_like(dstate)
            dgn_ref[...] = jnp.zeros_like(dgn_ref)
            if not is_ret:
                dba_ref[...] = jnp.zeros_like(dba_ref)

        shape3 = (nc, CHUNK, LANES)
        q, k, v, gates, b, extra, ri, ci = _tile_inputs(is_ret, qkvg_refs, aux, nc)
        eb = jnp.exp(b)
        qe = q * eb
        b_last = b[:, CHUNK - 1:CHUNK, :]
        e_last = jnp.exp(b_last)
        ekd = jnp.exp(b_last - b)
        kd = k * ekd

        d_os = []
        for hd in range(PAIR):
            lanes = slice(LANES * hd, LANES * (hd + 1))
            gn, gate = gn_ref[:, lanes], gates[hd]
            out = oraw_ref[:, lanes]
            r = _rms_scale(out)
            normed = out * r
            sg = jax.nn.sigmoid(gate)
            dfin = dfin_ref[:, lanes]
            dgate_ref[:, lanes] = (dfin * (normed * gn) * _silu_grad(gate, sg)).astype(BF16)
            dpre = dfin * (gate * sg)
            dgn_ref[:, lanes] += jnp.sum(dpre * normed, axis=0, keepdims=True)
            dnormed = dpre * gn
            d_o = r * (dnormed - normed * jnp.mean(dnormed * normed, axis=-1, keepdims=True))
            d_os.append(d_o.reshape(shape3))
        dob, vb = jnp.concatenate(d_os, axis=0).astype(BF16), v.astype(BF16)

        dgrow = _bmm_tn(dob, qe.astype(BF16))
        for hd in range(PAIR):
            dst = dstate[hd]
            for c in reversed(range(nc)):
                dafter_ref[hd * nc + c] = dst
                dst = dst * e_last[hd * nc + c] + dgrow[hd * nc + c]
            dstate[hd] = dst
        st = st_ref[...].reshape(PAIR * nc, LANES, LANES)
        dafter = dafter_ref[...]
        stb, dafter_b = st.astype(BF16), dafter.astype(BF16)

        dsc = _bmm_nt(dob, vb)
        dsc_t = _bmm_nt(vb, dob)
        dqe = _bmm(dob, stb)
        dkd = _bmm(vb, dafter_b)
        if is_ret:
            decay, qb, kb = jnp.concatenate(extra, axis=0), q.astype(BF16), k.astype(BF16)
            scores_t = _bmm_nt(kb, qb) * decay
            dq = _bmm((dsc * decay).astype(BF16), kb) + dqe * eb
            dk = _bmm((dsc_t * decay).astype(BF16), qb) + dkd * ekd
        else:
            _, ep, en, qt, kt, qh, kh = _tile_scores(q, k, b, ri, ci)
            qtb, ktb, qhb, khb = qt.astype(BF16), kt.astype(BF16), qh.astype(BF16), kh.astype(BF16)
            scores_t = jnp.where(ci >= ri, _bmm_nt(ktb, qtb), _bmm_nt(khb, qhb))
            dqt = _bmm(jnp.where(ci <= ri, dsc, 0.0).astype(BF16), ktb)
            dqh = _bmm(jnp.where(ci <= ri, 0.0, dsc).astype(BF16), khb)
            dkt = _bmm(jnp.where(ci >= ri, dsc_t, 0.0).astype(BF16), qtb)
            dkh = _bmm(jnp.where(ci >= ri, 0.0, dsc_t).astype(BF16), qhb)
            dq = dqt * ep + dqh * en + dqe * eb
            dk = dkt * en + dkh * ep + dkd * ekd
        dv = _bmm(scores_t.astype(BF16), dob) + _bmm_nt(kd.astype(BF16), dafter_b)

        if not is_ret:
            db = dqt * qt - dkt * kt - dqh * qh + dkh * kh + dqe * qe - dkd * kd
            db_last = (jnp.sum(dkd * kd, axis=1, keepdims=True)
                       + jnp.sum(dafter * st, axis=1, keepdims=True) * e_last)
            last_row = lax.broadcasted_iota(jnp.int32, (PAIR * nc, CHUNK, LANES), 1) == CHUNK - 1
            db = db + jnp.where(last_row, db_last, 0.0)
            dla = _masked_sum((ci >= ri).astype(BF16), db)

        dq_pair, dk_pair = [], []
        for hd in range(PAIR):
            lanes = slice(LANES * hd, LANES * (hd + 1))
            rows3 = slice(hd * nc, (hd + 1) * nc)
            dq_h, dk_h = dq[rows3].reshape(tb, LANES), dk[rows3].reshape(tb, LANES)
            if is_ret:
                cos_ref, sa_ref, sb_ref, _ = aux
                cos, sa, sb = cos_ref[...], sa_ref[...], sb_ref[...]
                dq_h = _rot_t(dq_h, cos, sa, sb)
                dk_h = _rot_t(dk_h, cos, sa, sb) * QK_SCALE
            else:
                dq_h = dq_h * QK_SCALE
                dlogit = dla[rows3].reshape(tb, LANES) * (1.0 / GATE_NORM) * jax.nn.sigmoid(-extra[hd])
                dlogit_ref[:, lanes] = dlogit.astype(BF16)
                dba_ref[:, lanes] += jnp.sum(dlogit, axis=0, keepdims=True)
            dq_pair.append(dq_h)
            dk_pair.append(dk_h)
            dv_ref[:, lanes] = dv[rows3].reshape(tb, LANES).astype(BF16)
        dq_ref[...] = (dq_pair[0] + pltpu.roll(dq_pair[1], 64, 1)).astype(BF16)
        dk_ref[...] = (dk_pair[0] + pltpu.roll(dk_pair[1], 64, 1)).astype(BF16)

    width = HEADS * LANES
    row_out = pl.BlockSpec((1, PAIR * LANES), lambda p, i: (0, p))
    out_specs = [narrow_t, narrow_t, pair_t, pair_t, row_out]
    out_shape = ([jax.ShapeDtypeStruct((t, width // 2), BF16)] * 2 + [jax.ShapeDtypeStruct((t, width), BF16)] * 2
                 + [jax.ShapeDtypeStruct((1, width), F32)])
    if not is_ret:
        out_specs += [pair_t, row_out]
        out_shape += [jax.ShapeDtypeStruct((t, width), BF16), jax.ShapeDtypeStruct((1, width), F32)]
    return _call(
        body, (proj, proj, proj, proj, *aux_arrays, gain, o_raw, states, d_out), name=name,
        grid=(HEADS // PAIR, nblk),
        in_specs=proj_spec + aux_specs + [gain_spec, pair_t, state_spec, dout_spec],
        out_specs=out_specs, out_shape=out_shape,
        scratch_shapes=[pltpu.VMEM((PAIR, LANES, LANES), F32), pltpu.VMEM((PAIR * nc, LANES, LANES), F32)],
        compiler_params=_ARB2, hosted=hosted)


PEER_SLOT = (2, 1, 3)


def _place():
    x, y, c = lax.axis_index("x"), lax.axis_index("y"), lax.axis_index("c")
    chips = [(1 - x, y), (x, 1 - y), (1 - x, 1 - y)]
    return x, y, c, 2 * x + y, chips


def _route_split(rows, dtype):
    tile = 16 if dtype == BF16 else 8
    if rows < 2 * tile:
        return None
    return -(-(rows // 2) // tile) * tile


def _routes(by_peer):
    x, y, c, me, chips = _place()
    (xx, xy), (yx, yy), (dx, dy) = chips
    if by_peer:
        slots = dict(own=0, from_x=PEER_SLOT[0], from_y=PEER_SLOT[1], diag=PEER_SLOT[2],
                     mine_on_x=PEER_SLOT[0], mine_on_y=PEER_SLOT[1])
    else:
        slots = dict(own=me, from_x=2 * xx + xy, from_y=2 * yx + yy, diag=2 * dx + dy, mine_on_x=me, mine_on_y=me)
    return c, (xx, xy, c), (yx, yy, c), (dx, dy, c), (x, y, 1 - c), slots


def _gather_legs(src, out, send_sems, recv_sems, base, by_peer):
    c, to_x, to_y, to_d, sibling, s = _routes(by_peer)
    r0 = _route_split(src.shape[2], src.dtype)

    def cp(k, src_ref, dst_ref, to):
        return pltpu.make_async_remote_copy(src_ref=src_ref, dst_ref=dst_ref, send_sem=send_sems.at[base + k],
                                            recv_sem=recv_sems.at[base + k], device_id=to, device_id_type=MESH)

    mine = src.at[:, c]
    legs = dict(
        x=(cp(0, mine, out.at[s["mine_on_x"], :, c], to_x), cp(0, mine, out.at[s["from_x"], :, c], to_x)),
        y=(cp(1, mine, out.at[s["mine_on_y"], :, c], to_y), cp(1, mine, out.at[s["from_y"], :, c], to_y)),
        pass_x=(cp(4, out.at[s["from_x"], :, c], out.at[s["from_x"], :, c], sibling),
                cp(4, mine, out.at[s["from_x"], :, 1 - c], sibling)),
        pass_y=(cp(5, out.at[s["from_y"], :, c], out.at[s["from_y"], :, c], sibling),
                cp(5, mine, out.at[s["from_y"], :, 1 - c], sibling)))
    if r0 is None:
        mine_on_d = s["diag"] if by_peer else s["own"]
        legs["d"] = (cp(2, mine, out.at[mine_on_d, :, c], to_d), cp(2, mine, out.at[s["diag"], :, c], to_d))
        legs["pass_d"] = (cp(6, out.at[s["diag"], :, c], out.at[s["diag"], :, c], sibling),
                          cp(6, mine, out.at[s["diag"], :, 1 - c], sibling))
        return legs, False
    lo, hi = pl.ds(0, r0), pl.ds(r0, src.shape[2] - r0)
    fx_on_y = s["diag"] if by_peer else s["from_x"]
    fy_on_x = s["diag"] if by_peer else s["from_y"]
    legs.update(
        fwd_y=(cp(2, out.at[s["from_x"], :, c, lo], out.at[fx_on_y, :, c, lo], to_y),
               cp(2, mine.at[:, lo], out.at[s["diag"], :, c, lo], to_y)),
        fwd_x=(cp(3, out.at[s["from_y"], :, c, hi], out.at[fy_on_x, :, c, hi], to_x),
               cp(3, mine.at[:, hi], out.at[s["diag"], :, c, hi], to_x)),
        pass_d0=(cp(6, out.at[s["diag"], :, c, lo], out.at[s["diag"], :, c, lo], sibling),
                 cp(6, mine.at[:, lo], out.at[s["diag"], :, 1 - c, lo], sibling)),
        pass_d1=(cp(7, out.at[s["diag"], :, c, hi], out.at[s["diag"], :, c, hi], sibling),
                 cp(7, mine.at[:, hi], out.at[s["diag"], :, 1 - c, hi], sibling)))
    return legs, True


def _gather_steps(legs, routed):
    def start():
        legs["x"][0].start()
        legs["y"][0].start()
        if not routed:
            legs["d"][0].start()

    def middle():
        legs["x"][1].wait_recv()
        if routed:
            legs["fwd_y"][0].start()
        legs["pass_x"][0].start()
        legs["y"][1].wait_recv()
        if routed:
            legs["fwd_x"][0].start()
        legs["pass_y"][0].start()

    def finish():
        last = ["pass_d0", "pass_d1"] if routed else ["pass_d"]
        if routed:
            legs["fwd_y"][1].wait_recv()
            legs["pass_d0"][0].start()
            legs["fwd_x"][1].wait_recv()
            legs["pass_d1"][0].start()
        else:
            legs["d"][1].wait_recv()
            legs["pass_d"][0].start()
        for name in ["pass_x", "pass_y"] + last:
            legs[name][1].wait_recv()
        for name in ["x", "y", "pass_x", "pass_y"] + last + (["fwd_y", "fwd_x"] if routed else ["d"]):
            legs[name][0].wait_send()

    return start, middle, finish


def _gather_plan(arrs):
    na = len(arrs)

    def steps(ins, outs, send_sems, recv_sems):
        return [_gather_steps(*_gather_legs(ins[a], outs[a], send_sems, recv_sems, 8 * a, False)) for a in range(na)]

    def run(which):
        def hook(*refs):
            for step in steps(*refs):
                step[which]()
        return hook

    return _Hosted(arrs, [jax.ShapeDtypeStruct((N_CHIPS,) + a.shape, a.dtype) for a in arrs], 8 * na,
                   run(0), run(2), middle=run(1))


def _pair_exchange_plan(grads):
    na = len(grads)

    def copies(ins, outs, send_sems, recv_sems):
        x, y, c, _, _ = _place()
        return [pltpu.make_async_remote_copy(
            src_ref=ins[a].at[:, 1 - c], dst_ref=outs[a], send_sem=send_sems.at[a], recv_sem=recv_sems.at[a],
            device_id=(x, y, 1 - c), device_id_type=MESH) for a in range(na)]

    def start(*refs):
        for cp in copies(*refs):
            cp.start()

    def finish(*refs):
        for cp in copies(*refs):
            cp.wait()

    return _Hosted(grads, [jax.ShapeDtypeStruct(g.shape[:1] + g.shape[2:], g.dtype) for g in grads], na, start, finish)


def _small_gather_plan(block):
    def copies(ins, outs, send_sems, recv_sems):
        x, y, c, _, chips = _place()
        peers = [(x, y, 1 - c)] + [(px, py, pc) for px, py in chips for pc in (c, 1 - c)]
        sends = [pltpu.make_async_remote_copy(
            src_ref=ins[0], dst_ref=outs[0].at[4 * x + 2 * y + c], send_sem=send_sems.at[k], recv_sem=recv_sems.at[k],
            device_id=peer, device_id_type=MESH) for k, peer in enumerate(peers)]
        recvs = [pltpu.make_async_remote_copy(
            src_ref=ins[0], dst_ref=outs[0].at[4 * px + 2 * py + pc], send_sem=send_sems.at[k], recv_sem=recv_sems.at[k],
            device_id=(px, py, pc), device_id_type=MESH) for k, (px, py, pc) in enumerate(peers)]
        return sends, recvs

    def start(*refs):
        for cp in copies(*refs)[0]:
            cp.start()

    def finish(*refs):
        sends, recvs = copies(*refs)
        for cp in recvs:
            cp.wait_recv()
        for cp in sends:
            cp.wait_send()

    return _Hosted([block], [jax.ShapeDtypeStruct((8,) + block.shape, block.dtype)], 7, start, finish)


def _sum_devices(blocks):
    def body(b_ref, o_ref):
        acc = b_ref[0]
        for d in range(1, 8):
            acc = acc + b_ref[d]
        o_ref[...] = acc

    return pl.pallas_call(body, name="sum_devices", in_specs=[_VMEM], out_specs=_VMEM,
                          out_shape=jax.ShapeDtypeStruct(blocks.shape[1:], blocks.dtype))(blocks)


def _pair_add(grad, recv, c_arr, name):
    _, _, r, cols = grad.shape

    def body(c_ref, g_ref, r_ref, o_ref):
        o_ref[...] = (g_ref[...].astype(F32) + r_ref[...].astype(F32)).astype(BF16)

    return pl.pallas_call(
        body, name=name,
        grid_spec=pltpu.PrefetchScalarGridSpec(
            num_scalar_prefetch=1, grid=(N_CHIPS,),
            in_specs=[pl.BlockSpec((None, None, r, cols), lambda p, c_ref: (p, c_ref[0], 0, 0)),
                      pl.BlockSpec((None, r, cols), lambda p, c_ref: (p, 0, 0))],
            out_specs=pl.BlockSpec((None, r, cols), lambda p, c_ref: (p, 0, 0))),
        out_shape=jax.ShapeDtypeStruct((N_CHIPS, r, cols), BF16),
        compiler_params=_ARB1,
    )(c_arr, grad, recv)


def _chip_exchange_plan(sums, by_peer=False):
    na = len(sums)

    def copies(ins, outs, send_sems, recv_sems):
        x, y, c, me, chips = _place()

        def copy(a, j, px, py, block, slot):
            return pltpu.make_async_remote_copy(
                src_ref=ins[a].at[block], dst_ref=outs[a].at[slot],
                send_sem=send_sems.at[3 * a + j], recv_sem=recv_sems.at[3 * a + j],
                device_id=(px, py, c), device_id_type=MESH)

        peers = [(a, j, px, py) for a in range(na) for j, (px, py) in enumerate(chips)]
        return me, peers, copy

    def start(*refs):
        me, peers, copy = copies(*refs)
        for a, j, px, py in peers:
            if by_peer:
                copy(a, j, px, py, PEER_SLOT[j], PEER_SLOT[j]).start()
            else:
                copy(a, j, px, py, 2 * px + py, me).start()

    def finish(*refs):
        me, peers, copy = copies(*refs)
        for a, j, px, py in peers:
            if by_peer:
                copy(a, j, px, py, PEER_SLOT[j], PEER_SLOT[j]).wait_recv()
            else:
                copy(a, j, px, py, me, 2 * px + py).wait_recv()
        for a, j, px, py in peers:
            if by_peer:
                copy(a, j, px, py, PEER_SLOT[j], PEER_SLOT[j]).wait_send()
            else:
                copy(a, j, px, py, 2 * px + py, me).wait_send()

    return _Hosted(sums, [jax.ShapeDtypeStruct(s.shape, s.dtype) for s in sums], 3 * na, start, finish)


def _chip_sum(own, recv, me_arr, name):
    _, r, cols = recv.shape

    def body(me_ref, own_ref, r_ref, o_ref):
        o_ref[...] = jnp.zeros_like(o_ref)
        for q in range(N_CHIPS):
            @pl.when(me_ref[0] == q)
            def _():
                o_ref[...] += own_ref[...].astype(F32)

            @pl.when(me_ref[0] != q)
            def _():
                o_ref[...] += r_ref[q].astype(F32)

    return pl.pallas_call(
        body, name=name,
        grid_spec=pltpu.PrefetchScalarGridSpec(
            num_scalar_prefetch=1, grid=(1,),
            in_specs=[pl.BlockSpec((None, r, cols), lambda i, me_ref: (me_ref[0], 0, 0)),
                      pl.BlockSpec((N_CHIPS, r, cols), lambda i, me_ref: (0, 0, 0))],
            out_specs=pl.BlockSpec((r, cols), lambda i, me_ref: (0, 0))),
        out_shape=jax.ShapeDtypeStruct((r, cols), F32),
        compiler_params=_ARB1,
    )(me_arr, own, recv)


def _peer_sum(own, recv, name):
    _, r, cols = recv.shape

    def body(own_ref, r_ref, o_ref):
        acc = own_ref[...].astype(F32) + r_ref[1].astype(F32)
        acc = acc + r_ref[2].astype(F32)
        o_ref[...] = acc + r_ref[3].astype(F32)

    return pl.pallas_call(
        body, name=name, grid=(1,),
        in_specs=[pl.BlockSpec((None, r, cols), lambda i: (0, 0, 0)), pl.BlockSpec((N_CHIPS, r, cols), lambda i: (0, 0, 0))],
        out_specs=pl.BlockSpec((r, cols), lambda i: (0, 0)),
        out_shape=jax.ShapeDtypeStruct((r, cols), F32),
        compiler_params=_ARB1,
    )(own, recv)


def _pair_share(halves):
    na = len(halves)

    def body(*refs):
        ins, outs = refs[:na], refs[na:2 * na]
        send_sems, recv_sems = refs[2 * na:]
        x, y, c, _, _ = _place()
        copies = [pltpu.make_async_remote_copy(
            src_ref=ins[a], dst_ref=outs[a], send_sem=send_sems.at[a], recv_sem=recv_sems.at[a],
            device_id=(x, y, 1 - c), device_id_type=MESH) for a in range(na)]
        for cp in copies:
            cp.start()
        for cp in copies:
            cp.wait()

    return pl.pallas_call(
        body, name="pair_share",
        in_specs=[ANY] * na, out_specs=[ANY] * na,
        out_shape=[jax.ShapeDtypeStruct(h.shape, h.dtype) for h in halves],
        scratch_shapes=[pltpu.SemaphoreType.DMA((na,)), pltpu.SemaphoreType.DMA((na,))],
    )(*halves)


def _small_allreduce(block):
    m, n = block.shape

    def body(x_ref, all_ref, sum_ref, send_sems, recv_sems, local_sem):
        x, y, c, _, chips = _place()
        me, sibling = (x, y, c), (x, y, 1 - c)

        def rows(px, py, pc):
            return all_ref.at[pl.ds((4 * px + 2 * py + pc) * m, m), :]

        def copy(k, blk, to, src=None):
            return pltpu.make_async_remote_copy(
                src_ref=rows(*blk) if src is None else src, dst_ref=rows(*blk),
                send_sem=send_sems.at[k], recv_sem=recv_sems.at[k], device_id=to, device_id_type=MESH)

        mine = pltpu.make_async_copy(x_ref, rows(*me), local_sem)
        mine.start()
        first = [copy(0, me, sibling, src=x_ref)]
        first += [copy(1 + j, me, (*chip, c), src=x_ref) for j, chip in enumerate(chips)]
        for cp in first:
            cp.start()
        passed = [copy(4 + j, (*chip, c), sibling) for j, chip in enumerate(chips)]
        for j, chip in enumerate(chips):
            copy(1 + j, (*chip, c), me).wait_recv()
            passed[j].start()
        copy(0, sibling, me).wait_recv()
        for j, chip in enumerate(chips):
            copy(4 + j, (*chip, 1 - c), me).wait_recv()
        for cp in first + passed:
            cp.wait_send()
        mine.wait()
        acc = all_ref[0:m, :]
        for d in range(1, 8):
            acc = acc + all_ref[d * m:(d + 1) * m, :]
        sum_ref[...] = acc

    vmem = pl.BlockSpec(memory_space=pltpu.VMEM)
    return pl.pallas_call(
        body, name="small_allreduce",
        in_specs=[vmem], out_specs=[vmem, vmem],
        out_shape=[jax.ShapeDtypeStruct((8 * m, n), F32), jax.ShapeDtypeStruct((m, n), F32)],
        scratch_shapes=[pltpu.SemaphoreType.DMA((7,)), pltpu.SemaphoreType.DMA((7,)), pltpu.SemaphoreType.DMA],
    )(block)[1]


def _row_tile(rows):
    best = rows
    for cand in range(8, min(rows, 512) + 1, 8):
        if rows % cand == 0:
            best = cand
    return best


def _adamw_math(w, g, m, v):
    m2 = ADAM_B1 * m + (1.0 - ADAM_B1) * g
    v2 = ADAM_B2 * v + (1.0 - ADAM_B2) * (g * g)
    m_hat = m2 / (1.0 - ADAM_B1 ** ADAM_STEP)
    v_hat = v2 / (1.0 - ADAM_B2 ** ADAM_STEP)
    return -ADAM_LR * (m_hat / (jnp.sqrt(v_hat) + ADAM_EPS) + ADAM_WD * w), m2, v2


def _adamw_halves(w, g_mine, g_other, m, v, c_arr, name):
    rows, cols = w.shape
    r = rows // 2
    tr = _row_tile(r)
    nt = r // tr

    def body(c_ref, w_ref, gm_ref, go_ref, m_ref, v_ref, g_ref, d_ref, nm_ref, nv_ref):
        gv = jnp.where(pl.program_id(0) == c_ref[0], gm_ref[...], go_ref[...])
        g_ref[...] = gv
        d_ref[...], nm_ref[...], nv_ref[...] = _adamw_math(w_ref[...], gv, m_ref[...], v_ref[...])

    full = pl.BlockSpec((tr, cols), lambda h, i, c_ref: (h * nt + i, 0))
    half = pl.BlockSpec((tr, cols), lambda h, i, c_ref: (i, 0))
    shape = jax.ShapeDtypeStruct((rows, cols), F32)
    return pl.pallas_call(
        body, name=name,
        grid_spec=pltpu.PrefetchScalarGridSpec(
            num_scalar_prefetch=1, grid=(2, nt),
            in_specs=[full, half, half, full, full], out_specs=[full] * 4),
        out_shape=[shape] * 4,
        compiler_params=_ARB2,
    )(c_arr, w, g_mine, g_other, m, v)


def _adamw(w, g, m, v, name):
    rows, cols = w.shape
    tr = _row_tile(rows)

    def body(w_ref, g_ref, m_ref, v_ref, d_ref, nm_ref, nv_ref):
        d_ref[...], nm_ref[...], nv_ref[...] = _adamw_math(w_ref[...], g_ref[...], m_ref[...], v_ref[...])

    spec = pl.BlockSpec((tr, cols), lambda i: (i, 0))
    shape = jax.ShapeDtypeStruct((rows, cols), F32)
    return pl.pallas_call(
        body, name=name, grid=(rows // tr,),
        in_specs=[spec] * 4, out_specs=[spec] * 3, out_shape=[shape] * 3,
        compiler_params=_ARB1,
    )(w, g, m, v)


def _pad_w_in_t(w_in_t):
    return jnp.pad(w_in_t, ((0, PROJ_P - IN_WIDTH), (0, 0)))


def _unpad_w_in_t(w_pt):
    return w_pt[0:IN_WIDTH]


def _rope_tables(t):
    half = 32
    inv = ROPE_BASE ** (-jnp.arange(half, dtype=F32) * 2.0 / 64)
    ang = jnp.arange(t, dtype=F32)[:, None] * inv[None, :]
    cos, sin = jnp.cos(ang), jnp.sin(ang)
    z32, z64 = jnp.zeros((t, 32), F32), jnp.zeros((t, 64), F32)
    return (jnp.concatenate([cos, cos, z64], axis=1),
            jnp.concatenate([-sin, z32, z64], axis=1),
            jnp.concatenate([z32, sin, z64], axis=1))


def _halves(w):
    n, rows, cols = w.shape
    return w.reshape(n, 2, rows // 2, cols)


_VMEM = pl.BlockSpec(memory_space=pltpu.VMEM)


def _pack_small(n1, nm, n2, nf, nret, ngla, ba, wa2_p, loss_blk):
    def body(n1_ref, nm_ref, n2_ref, nf_ref, nret_ref, ngla_ref, ba_ref, wa2_ref, loss_ref, o_ref):
        o_ref[...] = jnp.zeros_like(o_ref)
        o_ref[0:1, :] = n1_ref[...]
        o_ref[1:2, :] = nm_ref[...]
        o_ref[2:3, :] = n2_ref[...]
        o_ref[3:4, :] = nf_ref[...]
        o_ref[4:5, 0:512] = nret_ref[...]
        o_ref[4:5, 512:1024] = ngla_ref[...]
        o_ref[5:6, 0:256] = ba_ref[...]
        o_ref[6:7, 0:LANES] = loss_ref[0:1, :]
        o_ref[8:8 + GATE_RANK, 0:HEADS * LANES] = wa2_ref[0:GATE_RANK, :]

    return pl.pallas_call(
        body, name="pack_small", in_specs=[_VMEM] * 9, out_specs=_VMEM,
        out_shape=jax.ShapeDtypeStruct((SMALL_ROWS, D_MODEL), F32),
    )(n1, nm, n2, nf, nret, ngla, ba, wa2_p, loss_blk)


def _small_update(summed, chip_arr, ws, ms, vs):
    n = len(ws)

    def body(chip_ref, s_ref, *refs):
        w_refs, m_refs, v_refs = refs[0:n], refs[n:2 * n], refs[2 * n:3 * n]
        outs = refs[3 * n:]
        wa2_all = s_ref[8:8 + GATE_RANK, 0:HEADS * LANES]
        wa2_g = jnp.zeros((GATE_RANK, 64), F32)
        for p in range(N_CHIPS):
            wa2_g = jnp.where(chip_ref[0] == p, wa2_all[:, LANES * p:LANES * p + 64], wa2_g)
        grads = [s_ref[0:1, :], s_ref[1:2, :], s_ref[2:3, :], s_ref[3:4, :], s_ref[4:5, 0:512],
                 s_ref[4:5, 512:1024], s_ref[5:6, 0:256], wa2_g]
        for k in range(n):
            d, m2, v2 = _adamw_math(w_refs[k][...], grads[k], m_refs[k][...], v_refs[k][...])
            outs[k][...] = grads[k]
            outs[n + k][...] = d
            outs[2 * n + k][...] = m2
            outs[3 * n + k][...] = v2

    shapes = [jax.ShapeDtypeStruct(w.shape, F32) for w in ws] * 4
    smem = pl.BlockSpec(memory_space=pltpu.SMEM)
    outs = pl.pallas_call(
        body, name="small_update", in_specs=[smem] + [_VMEM] * (1 + 3 * n), out_specs=[_VMEM] * (4 * n),
        out_shape=shapes,
    )(chip_arr, summed, *ws, *ms, *vs)
    return outs[0:n], outs[n:2 * n], outs[2 * n:3 * n], outs[3 * n:4 * n]


def _pad_in_rows(w_t):
    return jnp.pad(w_t, ((0, IN_ROWS - IN_SHARD), (0, 0)))


def _forward_backward(xs, target, ffn1_w, rest, ba_p, ffn1_norm_g, mix_norm_g, ret_norm_g, gla_norm_g, ffn2_norm_g,
                      final_norm_g, ffn1_gather=None, rest_plan=None, rest_weights=None, ffn2_plans=None,
                      ffn2_weights=None, ffn2_pairs=None, ffn2_pairs_done=None, early=None, late=None, small_plan=None):
    t = xs.shape[0]
    cos_t, sa_t, sb_t = _rope_tables(t)
    log_gamma = jnp.log(1.0 - 2.0 ** (-5.0 - jnp.arange(HEADS, dtype=F32)))
    lg_t = jnp.broadcast_to(log_gamma[:, None, None], (HEADS, 1, LANES))
    ret_aux = [cos_t, sa_t, sb_t, lg_t]

    if ffn1_gather is None:
        (x1, a1, u1, h1), gathered = _ffn_fwd(xs, ffn1_norm_g, ffn1_w, "ffn1_fwd", hosted=rest_plan)
    else:
        ffn1_shard, ffn1_weights = ffn1_gather
        (x1, a1, u1, h1, wall), gathered = _ffn1_fwd_gathering(xs, ffn1_norm_g, ffn1_shard, "ffn1_fwd",
                                                               hosted=rest_plan)
        ffn1_w = ffn1_weights(wall)
    ffn2_w, w_in_pt, w_out_full, wa2_p = rest if rest_plan is None else rest_weights(gathered)
    plans = [None] * 3 if ffn2_plans is None else ffn2_plans
    (proj, h_mix), got_gate = _mixer_in_fwd(x1, mix_norm_g, w_in_pt, "mixer_in_fwd", hosted=plans[0])
    gla_aux = [proj, wa2_p, ba_p]
    (o_ret, raw_ret, st_ret), got_up = _attn_fwd(True, proj, ret_aux, ret_norm_g, "ret_fwd", hosted=plans[1])
    (o_gla, raw_gla, st_gla), got_down = _attn_fwd(False, proj, gla_aux, gla_norm_g, "gla_fwd", hosted=plans[2])
    if ffn2_plans is not None:
        ffn2_w = ffn2_weights(got_gate + got_up + got_down)
    x2 = _mixer_out_fwd(o_ret, o_gla, w_out_full, x1, "mixer_out_fwd")
    (x3, a2, u2, h2), _ = _ffn_fwd(x2, ffn2_norm_g, ffn2_w, "ffn2_fwd")
    loss_blk, dx3, d_final_g = _final_loss(x3, final_norm_g, target, "final_loss")

    (da2, du2, hid2, dob2, dx2, d_ffn2_g), _ = _ffn_bwd(dx3, x2, ffn2_norm_g, a2, u2, ffn2_w, "ffn2_bwd")
    g_gate2 = _matmul_tn(da2, h2, "ffn2_dgate", out_dtype=BF16)
    g_up2 = _matmul_tn(du2, h2, "ffn2_dup", out_dtype=BF16)
    g_down2 = _matmul_tn(hid2, dob2, "ffn2_ddown", out_dtype=BF16)

    d_o = _matmul_nt(dx2, w_out_full, "mixer_out_bwd")
    g_wout_ret = _matmul_tn(o_ret, dx2, "wout_grad_ret", out_dtype=BF16)
    g_wout_gla = _matmul_tn(o_gla, dx2, "wout_grad_gla", out_dtype=BF16)
    pairs_plan = None if ffn2_pairs is None else ffn2_pairs([g_gate2, g_up2, g_down2])
    (*dproj_ret, d_ret_g), pair_recv = _attn_bwd(True, proj, ret_aux, ret_norm_g, raw_ret, st_ret, d_o, "ret_bwd",
                                                 hosted=pairs_plan)
    if ffn2_pairs is not None:
        ffn2_pairs_done(pair_recv)
    (*dproj_gla, d_gla_g, dlogit, d_ba_p), _ = _attn_bwd(False, proj, gla_aux, gla_norm_g, raw_gla, st_gla, d_o,
                                                        "gla_bwd")
    d_glow = _matmul_nt(dlogit, wa2_p, "gate_low_bwd", out_dtype=BF16)
    g_wa2_p = _matmul_tn(proj[:, PROJ_P - LANES:], dlogit, "gate_w_grad")
    dproj = jnp.concatenate(dproj_ret + dproj_gla + [d_glow], axis=1)
    g_win_p = _matmul_tn(dproj, h_mix, "w_in_grad", tka=PROJ_P // PROJ_TILES, out_dtype=BF16)
    dx1, d_mix_g = _mixer_in_bwd(dproj, w_in_pt, dx2, x1, mix_norm_g, "mixer_in_bwd")
    g_win_t = _unpad_w_in_t(g_win_p[0])
    g_win = jnp.stack([_pad_in_rows(g_win_t[IN_SHARD * p:IN_SHARD * (p + 1)]) for p in range(N_CHIPS)], axis=0)
    g_wout = jnp.concatenate([g_wout_ret[0], g_wout_gla[0]], axis=0).reshape(N_CHIPS, D_MODEL // N_CHIPS, D_MODEL)

    early_grads = [g_win, g_wout] if ffn2_pairs is not None else [g_gate2, g_up2, g_down2, g_win, g_wout]
    early_plan = None if early is None else early(early_grads)
    (da1, du1, hid1, dob1, grad_x, d_ffn1_g), arrived = _ffn_bwd(dx1, xs, ffn1_norm_g, a1, u1, ffn1_w, "ffn1_bwd",
                                                                hosted=early_plan)
    d_ba = d_ba_p.reshape(HEADS, LANES)[:, 0:64].reshape(1, 256)
    small_local = _pack_small(d_ffn1_g, d_mix_g, d_ffn2_g, d_final_g, d_ret_g, d_gla_g, d_ba, g_wa2_p[0], loss_blk)
    late_grads, late_arrived = [], []
    for lhs, rhs, name in ((da1, h1, "ffn1_dgate"), (du1, h1, "ffn1_dup"), (hid1, dob1, "ffn1_ddown")):
        if late is None:
            plan = None
        else:
            plan = late(late_grads[-1], len(late_grads)) if late_grads else small_plan(small_local)
        res = _matmul_tn(lhs, rhs, name, out_dtype=BF16, hosted=plan)
        if plan is not None:
            res, carried = res
            late_arrived += carried
        late_grads.append(res)
    g_gate1, g_up1, g_down1 = late_grads

    return (small_local, grad_x, g_gate1, g_up1, g_down1, g_gate2, g_up2, g_down2, g_win, g_wout, g_wa2_p,
            d_ba_p, d_ffn1_g, d_mix_g, d_ffn2_g, d_final_g, d_ret_g, d_gla_g, arrived, late_arrived)


def kernel(x, ffn1_norm_g, ffn1_w_gate, ffn1_w_up, ffn1_w_down, mix_norm_g, w_in, ret_norm_g, gla_w_a2, gla_b_a, gla_norm_g, w_out, ffn2_norm_g, ffn2_w_gate, ffn2_w_up, ffn2_w_down, final_norm_g, loss_target, m_ffn1_norm_g, m_ffn1_w_gate, m_ffn1_w_up, m_ffn1_w_down, m_mix_norm_g, m_w_in, m_ret_norm_g, m_gla_w_a2, m_gla_b_a, m_gla_norm_g, m_w_out, m_ffn2_norm_g, m_ffn2_w_gate, m_ffn2_w_up, m_ffn2_w_down, m_final_norm_g, v_ffn1_norm_g, v_ffn1_w_gate, v_ffn1_w_up, v_ffn1_w_down, v_mix_norm_g, v_w_in, v_ret_norm_g, v_gla_w_a2, v_gla_b_a, v_gla_norm_g, v_w_out, v_ffn2_norm_g, v_ffn2_w_gate, v_ffn2_w_up, v_ffn2_w_down, v_final_norm_g):
    t = x.shape[1]
    xs = x.reshape(t, D_MODEL)
    target = loss_target.reshape(t, D_MODEL)
    chip = 2 * lax.axis_index("x") + lax.axis_index("y")
    c_arr = lax.axis_index("c").astype(jnp.int32).reshape(1)

    me_arr = chip.astype(jnp.int32).reshape(1)

    pad_rows = _pad_in_rows

    def own_block(gathered, shard):
        return lax.dynamic_update_slice(gathered, shard[None], (chip,) + (0,) * shard.ndim)

    ffn1_shard = _halves(jnp.stack([ffn1_w_gate[0].T, ffn1_w_up[0].T, ffn1_w_down[0]], axis=0).astype(BF16))
    rest_shards = [_halves(pad_rows(w_in[0].T).astype(BF16)[None]),
                   _halves(w_out.astype(BF16)),
                   jnp.concatenate([gla_w_a2.reshape(GATE_RANK, 64), jnp.zeros((GATE_RANK, 64), F32)],
                                   axis=1).reshape(1, 2, 8, LANES)]
    ffn2_shards = [_halves(w.astype(BF16)[None]) for w in (ffn2_w_gate[0].T, ffn2_w_up[0].T, ffn2_w_down[0])]
    def ffn1_weights(gathered):
        return lax.dynamic_update_slice(gathered, ffn1_shard[None], (0,) * 5).reshape(N_CHIPS, 3, FF_SHARD, D_MODEL)

    def rest_weights(gathered):
        win_all, wout_all, wa2_all = [own_block(g, s) for g, s in zip(gathered, rest_shards)]
        win_t = win_all.reshape(N_CHIPS, IN_ROWS, D_MODEL)
        w_in_pt = jnp.zeros((PROJ_P, D_MODEL), BF16)
        for p in range(N_CHIPS):
            w_in_pt = lax.dynamic_update_slice(w_in_pt, win_t[p, 0:IN_SHARD], (IN_SHARD * p, 0))
        wa2_p = jnp.pad(
            wa2_all.reshape(N_CHIPS, GATE_RANK, LANES).transpose(1, 0, 2).reshape(GATE_RANK, HEADS * LANES),
            ((0, LANES - GATE_RANK), (0, 0))).astype(BF16)
        return (None, w_in_pt, wout_all.reshape(D_MODEL, D_MODEL), wa2_p)

    def ffn2_weights(gathered):
        return [own_block(g, s).reshape(N_CHIPS, FF_SHARD, D_MODEL) for g, s in zip(gathered, ffn2_shards)]

    def by_halves(g):
        return g.reshape(g.shape[0], 2, g.shape[1] // 2, g.shape[2])

    def pair_adds(halves, recv, tag):
        return [_pair_add(g, r, c_arr, "pair_add_%s%d" % (tag, k)) for k, (g, r) in enumerate(zip(halves, recv))]

    def pair_sums(grads, tag):
        halves = [by_halves(g) for g in grads]
        return pair_adds(halves, _run_hosted(_pair_exchange_plan(halves), "pair_exchange_" + tag), tag)

    early_sums, ffn2_halves = [], []

    def ffn2_pairs(grads):
        ffn2_halves.extend(by_halves(g) for g in grads)
        return _pair_exchange_plan(ffn2_halves)

    def ffn2_pairs_done(recv):
        early_sums.extend(pair_adds(ffn2_halves, recv, "ffn2_"))

    def early(grads):
        early_sums.extend(pair_sums(grads, "early"))
        return _chip_exchange_plan(early_sums)

    late_sums = []

    def late(grad, number):
        late_sums.extend(pair_sums([grad], "late%d" % number))
        return _chip_exchange_plan(late_sums[-1:], by_peer=True)

    ba_p = jnp.pad(gla_b_a.reshape(HEADS, 64), ((0, 0), (0, 64))).reshape(1, HEADS * LANES)
    fb = _forward_backward(xs, target, None, None, ba_p, ffn1_norm_g, mix_norm_g, ret_norm_g, gla_norm_g,
                           ffn2_norm_g, final_norm_g.reshape(1, D_MODEL), ffn1_gather=(ffn1_shard, ffn1_weights),
                           rest_plan=_gather_plan(rest_shards), rest_weights=rest_weights,
                           ffn2_plans=[_gather_plan(ffn2_shards[0:2]), None, _gather_plan(ffn2_shards[2:3])],
                           ffn2_weights=ffn2_weights,
                           ffn2_pairs=ffn2_pairs, ffn2_pairs_done=ffn2_pairs_done, early=early, late=late,
                           small_plan=_small_gather_plan)
    (small_local, grad_x, _, _, g_down1, _, _, _, _, _, _, _, _, _, _, _, _, _, early_arrived, late_arrived) = fb
    small_all, late_arrived = late_arrived[0], late_arrived[1:]
    late_arrived = late_arrived + _run_hosted(late(g_down1, 3), "chip_exchange_late")
    mine = [_peer_sum(s, r, "chip_sum_%d" % k) for k, (s, r) in enumerate(zip(late_sums, late_arrived))]
    mine += [_chip_sum(s, r, me_arr, "chip_sum_%d" % (3 + k)) for k, (s, r) in enumerate(zip(early_sums, early_arrived))]
    other = _pair_share(mine)

    device = 2 * chip + lax.axis_index("c")
    small_sum = _sum_devices(lax.dynamic_update_slice(small_all, small_local[None], (device, 0, 0)))
    loss = small_sum[6, 0]

    def rows(n1, nm, n2, nf, nret, ngla, ba, wa2):
        return [n1, nm, n2, nf.reshape(1, D_MODEL), nret, ngla, ba, wa2.reshape(GATE_RANK, 64)]

    small = _small_update(
        small_sum, me_arr,
        rows(ffn1_norm_g, mix_norm_g, ffn2_norm_g, final_norm_g, ret_norm_g, gla_norm_g, gla_b_a, gla_w_a2),
        rows(m_ffn1_norm_g, m_mix_norm_g, m_ffn2_norm_g, m_final_norm_g, m_ret_norm_g, m_gla_norm_g, m_gla_b_a,
             m_gla_w_a2),
        rows(v_ffn1_norm_g, v_mix_norm_g, v_ffn2_norm_g, v_final_norm_g, v_ret_norm_g, v_gla_norm_g, v_gla_b_a,
             v_gla_w_a2))
    s_grad, s_delta, s_m, s_v = [
        [*o[0:3], o[3].reshape(D_MODEL), *o[4:7], o[7].reshape(1, GATE_RANK, 64)] for o in small]

    def big(k, w, m, v, name, to_2d, from_2d):
        outs4 = _adamw_halves(to_2d(w), mine[k], other[k], to_2d(m), to_2d(v), c_arr, name)
        return [from_2d(z) for z in outs4]

    plain = (lambda w: w[0], lambda z: z[None])
    transposed = (lambda w: w[0].T, lambda z: z.T[None])
    in_proj = (lambda w: pad_rows(w[0].T), lambda z: z[0:IN_SHARD].T[None])
    r_g1 = big(0, ffn1_w_gate, m_ffn1_w_gate, v_ffn1_w_gate, "adamw_ffn1_gate", *transposed)
    r_u1 = big(1, ffn1_w_up, m_ffn1_w_up, v_ffn1_w_up, "adamw_ffn1_up", *transposed)
    r_d1 = big(2, ffn1_w_down, m_ffn1_w_down, v_ffn1_w_down, "adamw_ffn1_down", *plain)
    r_g2 = big(3, ffn2_w_gate, m_ffn2_w_gate, v_ffn2_w_gate, "adamw_ffn2_gate", *transposed)
    r_u2 = big(4, ffn2_w_up, m_ffn2_w_up, v_ffn2_w_up, "adamw_ffn2_up", *transposed)
    r_d2 = big(5, ffn2_w_down, m_ffn2_w_down, v_ffn2_w_down, "adamw_ffn2_down", *plain)
    r_in = big(6, w_in, m_w_in, v_w_in, "adamw_w_in", *in_proj)
    r_out = big(7, w_out, m_w_out, v_w_out, "adamw_w_out", *plain)

    def leaves(k, smalls):
        n1, nm, n2, nf, nret, ngla, ba, wa2 = smalls
        return [n1, r_g1[k], r_u1[k], r_d1[k], nm, r_in[k], nret, wa2, ba, ngla, r_out[k], n2, r_g2[k], r_u2[k], r_d2[k], nf]

    outs = [loss, grad_x.reshape(x.shape)]
    outs += leaves(0, s_grad) + leaves(1, s_delta) + leaves(2, s_m) + leaves(3, s_v)
    return tuple(outs)
```

```python
import functools

import jax
import jax.numpy as jnp
from jax import lax
from jax.experimental import pallas as pl
from jax.experimental.pallas import tpu as pltpu

F32, BF16 = jnp.float32, jnp.bfloat16
MESH = pl.DeviceIdType.MESH
ANY = pl.BlockSpec(memory_space=pl.ANY)

D_MODEL = 1024
D_FF = 2816
N_CHIPS = 4
FF_SHARD = D_FF // N_CHIPS
IN_WIDTH = 3088
IN_SHARD = IN_WIDTH // N_CHIPS
IN_ROWS = 800
CHUNK = 64
HEADS = 4
LANES = 128
PROJ_P = 3072 + LANES
PROJ_TILES = 5
GATE_RANK = 16
QK_SCALE = 0.125
GATE_NORM = 16.0
RMS_EPS = 1e-6
ROPE_BASE = 10000.0
ADAM_LR, ADAM_B1, ADAM_B2, ADAM_EPS, ADAM_WD, ADAM_STEP = 0.001, 0.9, 0.999, 1e-08, 0.01, 10
SMALL_ROWS = 32
TOKEN_TILE = 512
ATTN_TILE = 512

_ARB2 = pltpu.CompilerParams(dimension_semantics=("arbitrary", "arbitrary"))
_ARB1 = pltpu.CompilerParams(dimension_semantics=("arbitrary",))
_ARB3 = pltpu.CompilerParams(dimension_semantics=("arbitrary", "arbitrary", "arbitrary"))


def _dot(a, b):
    return jnp.dot(a, b, preferred_element_type=F32)


def _dot_nt(a, b):
    return lax.dot_general(a, b, (((1,), (1,)), ((), ())), preferred_element_type=F32)


def _dot_tn(a, b):
    return lax.dot_general(a, b, (((0,), (0,)), ((), ())), preferred_element_type=F32)


def _rms_scale(xv):
    return lax.rsqrt(jnp.mean(xv * xv, axis=-1, keepdims=True) + RMS_EPS)


def _rms_bwd(dh, xv, g):
    r = _rms_scale(xv)
    xhat = xv * r
    dxhat = dh * g
    dx = r * (dxhat - xhat * jnp.mean(dxhat * xhat, axis=-1, keepdims=True))
    return dx, jnp.sum(dh * xhat, axis=0, keepdims=True)


def _silu_grad(a, sg):
    return sg * (1.0 + a * (1.0 - sg))


class _Hosted:
    def __init__(self, arrays, out_shapes, n_sems, start, finish, middle=None):
        self.arrays, self.out_shapes, self.n_sems = list(arrays), list(out_shapes), n_sems
        self.start, self.finish = start, finish
        self.middle = middle if middle is not None else (lambda *refs: None)


def _call(body, args, *, name, grid, in_specs, out_specs, out_shape, scratch_shapes, compiler_params, hosted=None):
    if hosted is None:
        outs = pl.pallas_call(body, name=name, grid=grid, in_specs=in_specs, out_specs=out_specs, out_shape=out_shape,
                              scratch_shapes=scratch_shapes, compiler_params=compiler_params)(*args)
        return list(outs), []
    n_in, n_out, n_sc, nh = len(in_specs), len(out_specs), len(scratch_shapes), len(hosted.arrays)

    def wrapped(*refs):
        ins, h_in = refs[:n_in], refs[n_in:n_in + nh]
        outs, h_out = refs[n_in + nh:n_in + nh + n_out], refs[n_in + nh + n_out:n_in + 2 * nh + n_out]
        rest = refs[n_in + 2 * nh + n_out:]
        scratch, (send_sems, recv_sems) = rest[:n_sc], rest[n_sc:]
        step = functools.reduce(lambda flat, d: flat * grid[d] + pl.program_id(d), range(len(grid)), 0)
        total = functools.reduce(lambda a, b: a * b, grid)

        @pl.when(step == 0)
        def _():
            hosted.start(h_in, h_out, send_sems, recv_sems)

        @pl.when(step == total // 2)
        def _():
            hosted.middle(h_in, h_out, send_sems, recv_sems)

        body(*ins, *outs, *scratch)
        last = step == total - 1

        @pl.when(last)
        def _():
            hosted.finish(h_in, h_out, send_sems, recv_sems)

    sems = [pltpu.SemaphoreType.DMA((hosted.n_sems,)), pltpu.SemaphoreType.DMA((hosted.n_sems,))]
    outs = pl.pallas_call(
        wrapped, name=name, grid=grid, in_specs=list(in_specs) + [ANY] * nh, out_specs=list(out_specs) + [ANY] * nh,
        out_shape=list(out_shape) + hosted.out_shapes, scratch_shapes=list(scratch_shapes) + sems,
        compiler_params=compiler_params)(*args, *hosted.arrays)
    return list(outs[:n_out]), list(outs[n_out:])


SIBLING_BARRIER_ID = 0


def _run_hosted(hosted, name, sibling_only=False):
    nh = len(hosted.arrays)

    def body(*refs):
        h_in, h_out, (send_sems, recv_sems) = refs[:nh], refs[nh:2 * nh], refs[2 * nh:]
        if sibling_only:
            x, y, c, _, _ = _place()
            barrier = pltpu.get_barrier_semaphore()
            pl.semaphore_signal(barrier, inc=1, device_id=(x, y, 1 - c), device_id_type=MESH)
            pl.semaphore_wait(barrier, 1)
        hosted.start(h_in, h_out, send_sems, recv_sems)
        hosted.middle(h_in, h_out, send_sems, recv_sems)
        hosted.finish(h_in, h_out, send_sems, recv_sems)

    sems = [pltpu.SemaphoreType.DMA((hosted.n_sems,)), pltpu.SemaphoreType.DMA((hosted.n_sems,))]
    params = pltpu.CompilerParams(collective_id=SIBLING_BARRIER_ID) if sibling_only else None
    return list(pl.pallas_call(body, name=name, in_specs=[ANY] * nh, out_specs=[ANY] * nh, out_shape=hosted.out_shapes,
                               scratch_shapes=sems, compiler_params=params)(*hosted.arrays))


def _ffn_weight_operands(ffn_w, chunk_maps):
    if isinstance(ffn_w, (list, tuple)):
        specs = [pl.BlockSpec((None, FF_SHARD, D_MODEL), lambda *g, m=m: (m(*g), 0, 0)) for m in chunk_maps]
        return list(ffn_w), specs
    specs = [pl.BlockSpec((None, None, FF_SHARD, D_MODEL), lambda *g, m=m, k=kind: (m(*g), k, 0, 0))
             for kind, m in enumerate(chunk_maps)]
    return [ffn_w] * 3, specs


def _pipeline_items(steps):
    def cur(s):
        c = jnp.minimum(s, steps - 1)
        return c // N_CHIPS, c % N_CHIPS

    def prev(s):
        p = jnp.maximum(s - 1, 0)
        return p // N_CHIPS, p % N_CHIPS

    return cur, prev


def _ffn_fwd(x, g, ffn_w, name, hosted=None):
    t = x.shape[0]
    tm = min(t, TOKEN_TILE)

    def body(x_ref, g_ref, wg_ref, wu_ref, wd_ref, xo_ref, a_ref, u_ref, h_ref, acc_ref):
        j = pl.program_id(1)

        @pl.when(j == 0)
        def _():
            xv = x_ref[...]
            h_ref[...] = ((xv * _rms_scale(xv)) * g_ref[...]).astype(BF16)
            acc_ref[...] = jnp.zeros_like(acc_ref)

        h = h_ref[...]
        a = _dot_nt(h, wg_ref[...])
        u = _dot_nt(h, wu_ref[...])
        a_ref[...] = a.astype(BF16)
        u_ref[...] = u.astype(BF16)
        hid = (a * jax.nn.sigmoid(a)) * u
        acc_ref[...] += _dot(hid.astype(BF16), wd_ref[...])

        @pl.when(j == N_CHIPS - 1)
        def _():
            xo_ref[...] = x_ref[...] + 0.5 * acc_ref[...]

    tok = pl.BlockSpec((tm, D_MODEL), lambda i, j: (i, 0))
    act = pl.BlockSpec((None, tm, FF_SHARD), lambda i, j: (j, i, 0))
    w_arrays, weights = _ffn_weight_operands(ffn_w, [lambda i, j: j] * 3)
    return _call(
        body, (x, g, *w_arrays), name=name, grid=(t // tm, N_CHIPS),
        in_specs=[tok, pl.BlockSpec((1, D_MODEL), lambda i, j: (0, 0))] + weights,
        out_specs=[tok, act, act, tok],
        out_shape=[jax.ShapeDtypeStruct((t, D_MODEL), F32),
                   jax.ShapeDtypeStruct((N_CHIPS, t, FF_SHARD), BF16),
                   jax.ShapeDtypeStruct((N_CHIPS, t, FF_SHARD), BF16),
                   jax.ShapeDtypeStruct((t, D_MODEL), BF16)],
        scratch_shapes=[pltpu.VMEM((tm, D_MODEL), F32)],
        compiler_params=_ARB2, hosted=hosted)


def _ffn1_fwd_gathering(x, g, shard, name, hosted=None):
    t = x.shape[0]
    tm = min(t, TOKEN_TILE)
    nt = t // tm
    nh = 0 if hosted is None else len(hosted.arrays)

    def body(*refs):
        x_ref, g_ref, shard_ref = refs[0:3]
        h_in = refs[3:3 + nh]
        xo_ref, a_ref, u_ref, h_ref, wall = refs[3 + nh:8 + nh]
        h_out = refs[8 + nh:8 + 2 * nh]
        acc, h_all, wbuf, load_sems, send_sems, recv_sems = refs[8 + 2 * nh:14 + 2 * nh]
        carried_sems = refs[14 + 2 * nh:]
        k, i = pl.program_id(0), pl.program_id(1)
        legs, _ = _gather_legs(shard_ref, wall, send_sems, recv_sems, 0, True)
        begin, pass_on, _ = _gather_steps(legs, True)

        def load(chunk, src):
            return pltpu.make_async_copy(src, wbuf.at[chunk % 2], load_sems.at[chunk % 2])

        @pl.when((k == 0) & (i == 0))
        def _():
            begin()
            load(0, shard_ref).start()
            load(0, shard_ref).wait()

        @pl.when((k == 1) & (i == 0))
        def _():
            pass_on()
            if hosted is not None:
                hosted.start(h_in, h_out, *carried_sems)
            legs["pass_y"][1].wait_recv()
            load(1, wall.at[PEER_SLOT[1]]).start()
            load(1, wall.at[PEER_SLOT[1]]).wait()

        @pl.when((k == 1) & (i == nt // 2))
        def _():
            legs["pass_x"][1].wait_recv()
            load(2, wall.at[PEER_SLOT[0]]).start()

        @pl.when((k == 2) & (i == 0))
        def _():
            load(2, wall.at[PEER_SLOT[0]]).wait()

        @pl.when((k == 2) & (i == nt // 2))
        def _():
            legs["fwd_y"][1].wait_recv()
            legs["pass_d0"][0].start()
            legs["fwd_x"][1].wait_recv()
            legs["pass_d1"][0].start()
            legs["pass_d0"][1].wait_recv()
            legs["pass_d1"][1].wait_recv()
            load(3, wall.at[PEER_SLOT[2]]).start()
            if hosted is not None:
                hosted.middle(h_in, h_out, *carried_sems)

        @pl.when((k == 3) & (i == 0))
        def _():
            load(3, wall.at[PEER_SLOT[2]]).wait()

        @pl.when(k == 0)
        def _():
            xv = x_ref[...]
            h0 = ((xv * _rms_scale(xv)) * g_ref[...]).astype(BF16)
            h_all[i] = h0
            h_ref[...] = h0

        h = h_all[i]
        wg, wu, wd = (wbuf[k % 2, kind].reshape(FF_SHARD, D_MODEL) for kind in range(3))
        a = _dot_nt(h, wg)
        u = _dot_nt(h, wu)
        a_ref[...] = a.astype(BF16)
        u_ref[...] = u.astype(BF16)
        part = _dot(((a * jax.nn.sigmoid(a)) * u).astype(BF16), wd)

        @pl.when(k == 0)
        def _():
            acc[i] = part

        @pl.when(k > 0)
        def _():
            acc[i] += part

        @pl.when(k == N_CHIPS - 1)
        def _():
            xo_ref[...] = x_ref[...] + 0.5 * acc[i]

        @pl.when((k == N_CHIPS - 1) & (i == nt - 1))
        def _():
            for pair in legs.values():
                pair[0].wait_send()
            if hosted is not None:
                hosted.finish(h_in, h_out, *carried_sems)

    def first_or_last(k):
        return (k == 0) | (k == N_CHIPS - 1)

    tok = lambda keep: pl.BlockSpec((tm, D_MODEL), lambda k, i: (jnp.where(keep(k), i, 0), 0))
    act = pl.BlockSpec((None, tm, FF_SHARD), lambda k, i: (k, i, 0))
    act_shape = jax.ShapeDtypeStruct((N_CHIPS, t, FF_SHARD), BF16)
    carried = [] if hosted is None else [pltpu.SemaphoreType.DMA((hosted.n_sems,))] * 2
    outs = pl.pallas_call(
        body, name=name, grid=(N_CHIPS, nt),
        in_specs=[tok(first_or_last), pl.BlockSpec((1, D_MODEL), lambda k, i: (0, 0)), ANY] + [ANY] * nh,
        out_specs=[tok(lambda k: k == N_CHIPS - 1), act, act,
                   pl.BlockSpec((tm, D_MODEL), lambda k, i: (jnp.where(k == 0, i, nt - 1), 0)), ANY] + [ANY] * nh,
        out_shape=[jax.ShapeDtypeStruct((t, D_MODEL), F32), act_shape, act_shape,
                   jax.ShapeDtypeStruct((t, D_MODEL), BF16),
                   jax.ShapeDtypeStruct((N_CHIPS,) + shard.shape, shard.dtype)]
                  + ([] if hosted is None else hosted.out_shapes),
        scratch_shapes=[pltpu.VMEM((nt, tm, D_MODEL), F32), pltpu.VMEM((nt, tm, D_MODEL), BF16),
                        pltpu.VMEM((2,) + shard.shape, shard.dtype), pltpu.SemaphoreType.DMA((2,)),
                        pltpu.SemaphoreType.DMA((8,)), pltpu.SemaphoreType.DMA((8,))] + carried,
        compiler_params=_ARB2,
    )(x, g, shard, *([] if hosted is None else hosted.arrays))
    return list(outs[:5]), list(outs[5:])


def _ffn_bwd(dxo, x, g, a4, u4, ffn_w, name, hosted=None):
    t = x.shape[0]
    tm = min(t, TOKEN_TILE)
    steps = (t // tm) * N_CHIPS
    cur, prev = _pipeline_items(steps)


    def body(dxo_ref, dxo_prev_ref, x_ref, g_ref, a_ref, u_ref, wg_ref, wu_ref, wd_ref,
             da_ref, du_ref, hid_ref, dob_ref, dx_ref, dg_ref, acc_ref, da_slots, du_slots):
        s = pl.program_id(0)
        jc, jp = cur(s)[1], prev(s)[1]
        slot = s % 2

        @pl.when(s == 0)
        def _():
            dg_ref[...] = jnp.zeros_like(dg_ref)
            acc_ref[...] = jnp.zeros_like(acc_ref)
            da_slots[...] = jnp.zeros_like(da_slots)
            du_slots[...] = jnp.zeros_like(du_slots)

        @pl.when(jc == 0)
        def _():
            dob_ref[...] = (0.5 * dxo_ref[...]).astype(BF16)

        dhid = _dot_nt(dob_ref[...], wd_ref[...])
        a = a_ref[...].astype(F32)
        u = u_ref[...].astype(F32)
        sg = jax.nn.sigmoid(a)
        sl = a * sg
        hid_ref[...] = (sl * u).astype(BF16)
        du = (dhid * sl).astype(BF16)
        da = (dhid * u * _silu_grad(a, sg)).astype(BF16)
        du_ref[...] = du
        da_ref[...] = da
        acc_ref[...] += _dot(da_slots[1 - slot], wg_ref[...]) + _dot(du_slots[1 - slot], wu_ref[...])
        da_slots[slot] = da
        du_slots[slot] = du

        @pl.when((jp == N_CHIPS - 1) & (s > 0))
        def _():
            dx, dg = _rms_bwd(acc_ref[...], x_ref[...], g_ref[...])
            dx_ref[...] = dxo_prev_ref[...] + dx
            dg_ref[...] += dg
            acc_ref[...] = jnp.zeros_like(acc_ref)

    tok_cur = pl.BlockSpec((tm, D_MODEL), lambda s: (cur(s)[0], 0))
    tok_prev = pl.BlockSpec((tm, D_MODEL), lambda s: (prev(s)[0], 0))
    act = pl.BlockSpec((None, tm, FF_SHARD), lambda s: (cur(s)[1], cur(s)[0], 0))
    row = pl.BlockSpec((1, D_MODEL), lambda s: (0, 0))
    w_arrays, weights = _ffn_weight_operands(ffn_w, [lambda s: prev(s)[1], lambda s: prev(s)[1], lambda s: cur(s)[1]])
    act_shape = jax.ShapeDtypeStruct((N_CHIPS, t, FF_SHARD), BF16)
    return _call(
        body, (dxo, dxo, x, g, a4, u4, *w_arrays), name=name, grid=(steps + 1,),
        in_specs=[tok_cur, tok_prev, tok_prev, row, act, act] + weights,
        out_specs=[act, act, act, tok_cur, tok_prev, row],
        out_shape=[act_shape, act_shape, act_shape,
                   jax.ShapeDtypeStruct((t, D_MODEL), BF16),
                   jax.ShapeDtypeStruct((t, D_MODEL), F32),
                   jax.ShapeDtypeStruct((1, D_MODEL), F32)],
        scratch_shapes=[pltpu.VMEM((tm, D_MODEL), F32), pltpu.VMEM((2, tm, FF_SHARD), BF16),
                        pltpu.VMEM((2, tm, FF_SHARD), BF16)],
        compiler_params=_ARB1, hosted=hosted)


def _matmul_tn(a, b, name, tka=None, out_dtype=F32, hosted=None):
    a3, b3 = a.ndim == 3, b.ndim == 3
    nb = a.shape[0] if a3 else (b.shape[0] if b3 else 1)
    t, ka, n = a.shape[-2], a.shape[-1], b.shape[-1]
    tka = ka if tka is None else tka
    tk = min(t, 4 * TOKEN_TILE)
    nk = t // tk

    def body(a_ref, b_ref, o_ref, acc_ref):
        k = pl.program_id(2)

        @pl.when(k == 0)
        def _():
            acc_ref[...] = jnp.zeros_like(acc_ref)

        acc_ref[...] += _dot_tn(a_ref[...].astype(BF16), b_ref[...].astype(BF16))

        @pl.when(k == nk - 1)
        def _():
            o_ref[...] = acc_ref[...].astype(out_dtype)

    a_spec = (pl.BlockSpec((None, tk, tka), lambda i, j, k: (i, k, j)) if a3
              else pl.BlockSpec((tk, tka), lambda i, j, k: (k, j)))
    b_spec = (pl.BlockSpec((None, tk, n), lambda i, j, k: (i, k, 0)) if b3
              else pl.BlockSpec((tk, n), lambda i, j, k: (k, 0)))
    outs, carried = _call(
        body, (a, b), name=name, grid=(nb, ka // tka, t // tk),
        in_specs=[a_spec, b_spec],
        out_specs=[pl.BlockSpec((None, tka, n), lambda i, j, k: (i, j, 0))],
        out_shape=[jax.ShapeDtypeStruct((nb, ka, n), out_dtype)],
        scratch_shapes=[pltpu.VMEM((tka, n), F32)],
        compiler_params=_ARB3, hosted=hosted)
    return outs[0] if hosted is None else (outs[0], carried)


def _matmul_nt(a, w, name, out_dtype=F32):
    t, k = a.shape
    n = w.shape[0]
    tm = min(t, TOKEN_TILE)

    def body(a_ref, w_ref, o_ref):
        o_ref[...] = _dot_nt(a_ref[...].astype(BF16), w_ref[...]).astype(out_dtype)

    return pl.pallas_call(
        body, name=name, grid=(t // tm,),
        in_specs=[pl.BlockSpec((tm, k), lambda i: (i, 0)), pl.BlockSpec((n, k), lambda i: (0, 0))],
        out_specs=pl.BlockSpec((tm, n), lambda i: (i, 0)),
        out_shape=jax.ShapeDtypeStruct((t, n), out_dtype),
        compiler_params=_ARB1,
    )(a, w)


def _mixer_in_bwd(dproj, w_in_pt, dres, x, g, name):
    t, k = dproj.shape
    tm = min(t, TOKEN_TILE)

    def body(a_ref, w_ref, dres_ref, x_ref, g_ref, dx_ref, dg_ref):
        @pl.when(pl.program_id(0) == 0)
        def _():
            dg_ref[...] = jnp.zeros_like(dg_ref)

        dh = _dot(a_ref[...], w_ref[...])
        dx, dg = _rms_bwd(dh, x_ref[...], g_ref[...])
        dx_ref[...] = dres_ref[...] + dx
        dg_ref[...] += dg

    tok = pl.BlockSpec((tm, D_MODEL), lambda i: (i, 0))
    row = pl.BlockSpec((1, D_MODEL), lambda i: (0, 0))
    return pl.pallas_call(
        body, name=name, grid=(t // tm,),
        in_specs=[pl.BlockSpec((tm, k), lambda i: (i, 0)), pl.BlockSpec((k, D_MODEL), lambda i: (0, 0)), tok, tok, row],
        out_specs=[tok, row],
        out_shape=[jax.ShapeDtypeStruct((t, D_MODEL), F32), jax.ShapeDtypeStruct((1, D_MODEL), F32)],
        compiler_params=_ARB1,
    )(dproj, w_in_pt, dres, x, g)


def _mixer_in_fwd(x, g, w_in_pt, name, hosted=None):
    t = x.shape[0]
    tm = min(t, TOKEN_TILE)
    tn = PROJ_P // PROJ_TILES

    def body(x_ref, g_ref, w_ref, p_ref, h_ref):
        @pl.when(pl.program_id(1) == 0)
        def _():
            xv = x_ref[...]
            h_ref[...] = ((xv * _rms_scale(xv)) * g_ref[...]).astype(BF16)

        p_ref[...] = _dot_nt(h_ref[...], w_ref[...])

    tok = pl.BlockSpec((tm, D_MODEL), lambda i, j: (i, 0))
    return _call(
        body, (x, g, w_in_pt), name=name, grid=(t // tm, PROJ_TILES),
        in_specs=[tok, pl.BlockSpec((1, D_MODEL), lambda i, j: (0, 0)),
                  pl.BlockSpec((tn, D_MODEL), lambda i, j: (j, 0))],
        out_specs=[pl.BlockSpec((tm, tn), lambda i, j: (i, j)), tok],
        out_shape=[jax.ShapeDtypeStruct((t, PROJ_P), F32), jax.ShapeDtypeStruct((t, D_MODEL), BF16)],
        scratch_shapes=[], compiler_params=_ARB2, hosted=hosted)


def _mixer_out_fwd(o_ret, o_gla, w_out, x, name):
    t = x.shape[0]
    tm = min(t, TOKEN_TILE)
    half = HEADS * LANES

    def body(a_ref, b_ref, w_ref, x_ref, o_ref):
        o_ref[...] = x_ref[...] + _dot(a_ref[...], w_ref[0:half, :]) + _dot(b_ref[...], w_ref[half:2 * half, :])

    tok = pl.BlockSpec((tm, D_MODEL), lambda i: (i, 0))
    hb = pl.BlockSpec((tm, half), lambda i: (i, 0))
    return pl.pallas_call(
        body, name=name, grid=(t // tm,),
        in_specs=[hb, hb, pl.BlockSpec((2 * half, D_MODEL), lambda i: (0, 0)), tok],
        out_specs=tok, out_shape=jax.ShapeDtypeStruct((t, D_MODEL), F32),
        compiler_params=_ARB1,
    )(o_ret, o_gla, w_out, x)


def _final_loss(x, g, target, name):
    t = x.shape[0]
    tm = min(t, TOKEN_TILE)

    def body(x_ref, g_ref, t_ref, l_ref, dx_ref, dg_ref):
        @pl.when(pl.program_id(0) == 0)
        def _():
            l_ref[...] = jnp.zeros_like(l_ref)
            dg_ref[...] = jnp.zeros_like(dg_ref)

        xv = x_ref[...]
        gv = g_ref[...]
        err = (xv * _rms_scale(xv)) * gv - t_ref[...]
        l_ref[...] += 0.5 * jnp.sum(jnp.mean(err * err, axis=-1, keepdims=True), axis=0, keepdims=True)
        dx, dg = _rms_bwd(err * (1.0 / D_MODEL), xv, gv)
        dx_ref[...] = dx
        dg_ref[...] += dg

    tok = pl.BlockSpec((tm, D_MODEL), lambda i: (i, 0))
    row = pl.BlockSpec((1, D_MODEL), lambda i: (0, 0))
    return pl.pallas_call(
        body, name=name, grid=(t // tm,),
        in_specs=[tok, row, tok],
        out_specs=[pl.BlockSpec((8, LANES), lambda i: (0, 0)), tok, row],
        out_shape=[jax.ShapeDtypeStruct((8, LANES), F32), jax.ShapeDtypeStruct((t, D_MODEL), F32),
                   jax.ShapeDtypeStruct((1, D_MODEL), F32)],
        compiler_params=_ARB1,
    )(x, g, target)


def _rot(v, cos, sa, sb):
    return v * cos + pltpu.roll(v, 96, 1) * sa + pltpu.roll(v, 32, 1) * sb


def _rot_t(d, cos, sa, sb):
    return d * cos + pltpu.roll(d * sa, 32, 1) + pltpu.roll(d * sb, 96, 1)


def _bmm(a, b):
    return jnp.einsum("cik,ckj->cij", a, b, preferred_element_type=F32)


def _bmm_nt(a, b):
    return jnp.einsum("cik,cjk->cij", a, b, preferred_element_type=F32)


def _bmm_tn(a, b):
    return jnp.einsum("cki,ckj->cij", a, b, preferred_element_type=F32)


def _masked_sum(mask, x):
    hi = x.astype(BF16)
    r1 = x - hi.astype(F32)
    mid = r1.astype(BF16)
    lo = (r1 - mid.astype(F32)).astype(BF16)
    return _bmm(mask, hi) + _bmm(mask, mid) + _bmm(mask, lo)


PAIR = 2


def _tile_inputs(is_ret, qkvg_refs, aux, nc):
    shape3 = (nc, CHUNK, LANES)
    q_ref, k_ref, v_ref, g_ref = qkvg_refs
    low_lanes = lax.broadcasted_iota(jnp.int32, (1, LANES), 1) < 64
    ri = lax.broadcasted_iota(jnp.int32, (PAIR * nc, CHUNK, CHUNK), 1)
    ci = lax.broadcasted_iota(jnp.int32, (PAIR * nc, CHUNK, CHUNK), 2)
    qs, ks, vs, bs, gates, extra = [], [], [], [], [], []
    for hd in range(PAIR):
        q_blk, k_blk = q_ref[...], k_ref[...]
        if hd == 1:
            q_blk, k_blk = pltpu.roll(q_blk, 64, 1), pltpu.roll(k_blk, 64, 1)
        q_raw, k_raw = jnp.where(low_lanes, q_blk, 0.0), jnp.where(low_lanes, k_blk, 0.0)
        vs.append(v_ref[:, LANES * hd:LANES * (hd + 1)].reshape(shape3))
        gates.append(g_ref[:, LANES * hd:LANES * (hd + 1)])
        if is_ret:
            cos_ref, sa_ref, sb_ref, lg_ref = aux
            cos, sa, sb = cos_ref[...], sa_ref[...], sb_ref[...]
            q = _rot(q_raw, cos, sa, sb)
            k = _rot(k_raw, cos, sa, sb) * QK_SCALE
            steps = (lax.broadcasted_iota(jnp.int32, shape3, 1) + 1).astype(F32)
            bs.append(steps * lg_ref[hd])
            extra.append(jnp.exp(jnp.abs(ri[0:nc] - ci[0:nc]).astype(F32) * lg_ref[hd][:, 0:CHUNK]))
        else:
            glow_ref, wa2_ref, ba_ref = aux
            lanes = slice(LANES * hd, LANES * (hd + 1))
            logit = _dot(glow_ref[...].astype(BF16), wa2_ref[:, lanes]) + ba_ref[:, lanes]
            la = (jnp.minimum(logit, 0.0) - jnp.log1p(jnp.exp(-jnp.abs(logit)))) * (1.0 / GATE_NORM)
            bs.append(_masked_sum((ci[0:nc] <= ri[0:nc]).astype(BF16), la.reshape(shape3)))
            extra.append(logit)
            q = q_raw * QK_SCALE
            k = k_raw
        qs.append(q.reshape(shape3))
        ks.append(k.reshape(shape3))
    cat = lambda parts: jnp.concatenate(parts, axis=0)
    return cat(qs), cat(ks), cat(vs), gates, cat(bs), extra, ri, ci


def _tile_scores(q, k, b, ri, ci):
    mid = b[:, CHUNK // 2 - 1:CHUNK // 2, :]
    ep = jnp.exp(b - mid)
    en = jnp.exp(mid - b)
    qt, kt, qh, kh = q * ep, k * en, q * en, k * ep
    low = _bmm_nt(qt.astype(BF16), kt.astype(BF16))
    upp = _bmm_nt(qh.astype(BF16), kh.astype(BF16))
    scores = jnp.where(ci <= ri, low, upp)
    return scores, ep, en, qt, kt, qh, kh


def _attn_specs(is_ret, t, tb, imap_t):
    nb = t // tb
    base = 0 if is_ret else 12
    wide = PAIR * LANES
    proj = [pl.BlockSpec((tb, LANES), lambda p, i: (imap_t(i), base + p)),
            pl.BlockSpec((tb, LANES), lambda p, i: (imap_t(i), base + 2 + p)),
            pl.BlockSpec((tb, wide), lambda p, i: (imap_t(i), (base + 4) // 2 + p)),
            pl.BlockSpec((tb, wide), lambda p, i: (imap_t(i), (base + 8) // 2 + p))]
    lane_t = pl.BlockSpec((tb, LANES), lambda p, i: (imap_t(i), 0))
    if is_ret:
        aux = [lane_t, lane_t, lane_t, pl.BlockSpec((PAIR, 1, LANES), lambda p, i: (p, 0, 0))]
    else:
        aux = [pl.BlockSpec((tb, LANES), lambda p, i: (imap_t(i), PROJ_P // LANES - 1)),
               pl.BlockSpec((LANES, wide), lambda p, i: (0, p)),
               pl.BlockSpec((1, wide), lambda p, i: (0, p))]
    gain = pl.BlockSpec((1, wide), lambda p, i: (0, p))
    pair_t = pl.BlockSpec((tb, wide), lambda p, i: (imap_t(i), p))
    narrow_t = pl.BlockSpec((tb, LANES), lambda p, i: (imap_t(i), p))
    state = pl.BlockSpec((PAIR, tb // CHUNK, LANES, LANES), lambda p, i: (p, imap_t(i), 0, 0))
    return nb, proj, aux, gain, pair_t, narrow_t, state


def _attn_fwd(is_ret, proj, aux_arrays, gain, name, hosted=None):
    t = proj.shape[0]
    tb = min(t, ATTN_TILE)
    nc = tb // CHUNK
    n_aux = 4 if is_ret else 3
    nb, proj_spec, aux_specs, gain_spec, pair_t, _, state_spec = _attn_specs(is_ret, t, tb, lambda i: i)

    def body(*refs):
        qkvg_refs = refs[0:4]
        aux = refs[4:4 + n_aux]
        gn_ref, ofin_ref, oraw_ref, st_ref, state = refs[4 + n_aux:]

        @pl.when(pl.program_id(1) == 0)
        def _():
            state[...] = jnp.zeros_like(state)

        q, k, v, gates, b, extra, ri, ci = _tile_inputs(is_ret, qkvg_refs, aux, nc)
        if is_ret:
            scores = _bmm_nt(q.astype(BF16), k.astype(BF16)) * jnp.concatenate(extra, axis=0)
        else:
            scores = _tile_scores(q, k, b, ri, ci)[0]
        vb = v.astype(BF16)
        intra = _bmm(scores.astype(BF16), vb)
        b_last = b[:, CHUNK - 1:CHUNK, :]
        e_last = jnp.exp(b_last)
        grow = _bmm_tn(vb, (k * jnp.exp(b_last - b)).astype(BF16))
        for hd in range(PAIR):
            st = state[hd]
            for c in range(nc):
                st_ref[hd, c] = st
                st = st * e_last[hd * nc + c] + grow[hd * nc + c]
            state[hd] = st
        starts = st_ref[...].reshape(PAIR * nc, LANES, LANES)
        out3 = intra + _bmm_nt((q * jnp.exp(b)).astype(BF16), starts.astype(BF16))
        for hd in range(PAIR):
            lanes = slice(LANES * hd, LANES * (hd + 1))
            out = out3[hd * nc:(hd + 1) * nc].reshape(tb, LANES)
            oraw_ref[:, lanes] = out
            normed = out * _rms_scale(out)
            gate = gates[hd]
            ofin_ref[:, lanes] = ((normed * gn_ref[:, lanes]) * (gate * jax.nn.sigmoid(gate))).astype(BF16)

    width = HEADS * LANES
    return _call(
        body, (proj, proj, proj, proj, *aux_arrays, gain), name=name, grid=(HEADS // PAIR, nb),
        in_specs=proj_spec + aux_specs + [gain_spec],
        out_specs=[pair_t, pair_t, state_spec],
        out_shape=[jax.ShapeDtypeStruct((t, width), BF16), jax.ShapeDtypeStruct((t, width), F32),
                   jax.ShapeDtypeStruct((HEADS, t // CHUNK, LANES, LANES), F32)],
        scratch_shapes=[pltpu.VMEM((PAIR, LANES, LANES), F32)],
        compiler_params=_ARB2, hosted=hosted)


def _attn_bwd(is_ret, proj, aux_arrays, gain, o_raw, states, d_out, name, hosted=None):
    t = proj.shape[0]
    tb = min(t, ATTN_TILE)
    nc = tb // CHUNK
    n_aux = 4 if is_ret else 3
    nblk = t // tb
    nb, proj_spec, aux_specs, gain_spec, pair_t, narrow_t, state_spec = _attn_specs(
        is_ret, t, tb, lambda i: nblk - 1 - i)
    base = 0 if is_ret else HEADS // PAIR
    dout_spec = pl.BlockSpec((tb, PAIR * LANES), lambda p, i: (nblk - 1 - i, base + p))

    def body(*refs):
        qkvg_refs = refs[0:4]
        aux = refs[4:4 + n_aux]
        gn_ref, oraw_ref, st_ref, dfin_ref = refs[4 + n_aux:8 + n_aux]
        dq_ref, dk_ref, dv_ref, dgate_ref, dgn_ref = refs[8 + n_aux:13 + n_aux]
        if is_ret:
            dstate, dafter_ref = refs[13 + n_aux:]
        else:
            dlogit_ref, dba_ref, dstate, dafter_ref = refs[13 + n_aux:]

        @pl.when(pl.program_id(1) == 0)
        def _():
            dstate[...] = jnp.zeros_like(dstate)
            dgn_ref[...] = jnp.zeros_like(dgn_ref)
            if not is_ret:
                dba_ref[...] = jnp.zeros_like(dba_ref)

        shape3 = (nc, CHUNK, LANES)
        q, k, v, gates, b, extra, ri, ci = _tile_inputs(is_ret, qkvg_refs, aux, nc)
        eb = jnp.exp(b)
        qe = q * eb
        b_last = b[:, CHUNK - 1:CHUNK, :]
        e_last = jnp.exp(b_last)
        ekd = jnp.exp(b_last - b)
        kd = k * ekd

        d_os = []
        for hd in range(PAIR):
            lanes = slice(LANES * hd, LANES * (hd + 1))
            gn, gate = gn_ref[:, lanes], gates[hd]
            out = oraw_ref[:, lanes]
            r = _rms_scale(out)
            normed = out * r
            sg = jax.nn.sigmoid(gate)
            dfin = dfin_ref[:, lanes]
            dgate_ref[:, lanes] = (dfin * (normed * gn) * _silu_grad(gate, sg)).astype(BF16)
            dpre = dfin * (gate * sg)
            dgn_ref[:, lanes] += jnp.sum(dpre * normed, axis=0, keepdims=True)
            dnormed = dpre * gn
            d_o = r * (dnormed - normed * jnp.mean(dnormed * normed, axis=-1, keepdims=True))
            d_os.append(d_o.reshape(shape3))
        dob, vb = jnp.concatenate(d_os, axis=0).astype(BF16), v.astype(BF16)

        dgrow = _bmm_tn(dob, qe.astype(BF16))
        for hd in range(PAIR):
            dst = dstate[hd]
            for c in reversed(range(nc)):
                dafter_ref[hd * nc + c] = dst
                dst = dst * e_last[hd * nc + c] + dgrow[hd * nc + c]
            dstate[hd] = dst
        st = st_ref[...].reshape(PAIR * nc, LANES, LANES)
        dafter = dafter_ref[...]
        stb, dafter_b = st.astype(BF16), dafter.astype(BF16)

        dsc = _bmm_nt(dob, vb)
        dsc_t = _bmm_nt(vb, dob)
        dqe = _bmm(dob, stb)
        dkd = _bmm(vb, dafter_b)
        if is_ret:
            decay, qb, kb = jnp.concatenate(extra, axis=0), q.astype(BF16), k.astype(BF16)
            scores_t = _bmm_nt(kb, qb) * decay
            dq = _bmm((dsc * decay).astype(BF16), kb) + dqe * eb
            dk = _bmm((dsc_t * decay).astype(BF16), qb) + dkd * ekd
        else:
            _, ep, en, qt, kt, qh, kh = _tile_scores(q, k, b, ri, ci)
            qtb, ktb, qhb, khb = qt.astype(BF16), kt.astype(BF16), qh.astype(BF16), kh.astype(BF16)
            scores_t = jnp.where(ci >= ri, _bmm_nt(ktb, qtb), _bmm_nt(khb, qhb))
            dqt = _bmm(jnp.where(ci <= ri, dsc, 0.0).astype(BF16), ktb)
            dqh = _bmm(jnp.where(ci <= ri, 0.0, dsc).astype(BF16), khb)
            dkt = _bmm(jnp.where(ci >= ri, dsc_t, 0.0).astype(BF16), qtb)
            dkh = _bmm(jnp.where(ci >= ri, 0.0, dsc_t).astype(BF16), qhb)
            dq = dqt * ep + dqh * en + dqe * eb
            dk = dkt * en + dkh * ep + dkd * ekd
        dv = _bmm(scores_t.astype(BF16), dob) + _bmm_nt(kd.astype(BF16), dafter_b)

        if not is_ret:
            db = dqt * qt - dkt * kt - dqh * qh + dkh * kh + dqe * qe - dkd * kd
            db_last = (jnp.sum(dkd * kd, axis=1, keepdims=True)
                       + jnp.sum(dafter * st, axis=1, keepdims=True) * e_last)
            last_row = lax.broadcasted_iota(jnp.int32, (PAIR * nc, CHUNK, LANES), 1) == CHUNK - 1
            db = db + jnp.where(last_row, db_last, 0.0)
            dla = _masked_sum((ci >= ri).astype(BF16), db)

        dq_pair, dk_pair = [], []
        for hd in range(PAIR):
            lanes = slice(LANES * hd, LANES * (hd + 1))
            rows3 = slice(hd * nc, (hd + 1) * nc)
            dq_h, dk_h = dq[rows3].reshape(tb, LANES), dk[rows3].reshape(tb, LANES)
            if is_ret:
                cos_ref, sa_ref, sb_ref, _ = aux
                cos, sa, sb = cos_ref[...], sa_ref[...], sb_ref[...]
                dq_h = _rot_t(dq_h, cos, sa, sb)
                dk_h = _rot_t(dk_h, cos, sa, sb) * QK_SCALE
            else:
                dq_h = dq_h * QK_SCALE
                dlogit = dla[rows3].reshape(tb, LANES) * (1.0 / GATE_NORM) * jax.nn.sigmoid(-extra[hd])
                dlogit_ref[:, lanes] = dlogit.astype(BF16)
                dba_ref[:, lanes] += jnp.sum(dlogit, axis=0, keepdims=True)
            dq_pair.append(dq_h)
            dk_pair.append(dk_h)
            dv_ref[:, lanes] = dv[rows3].reshape(tb, LANES).astype(BF16)
        dq_ref[...] = (dq_pair[0] + pltpu.roll(dq_pair[1], 64, 1)).astype(BF16)
        dk_ref[...] = (dk_pair[0] + pltpu.roll(dk_pair[1], 64, 1)).astype(BF16)

    width = HEADS * LANES
    row_out = pl.BlockSpec((1, PAIR * LANES), lambda p, i: (0, p))
    out_specs = [narrow_t, narrow_t, pair_t, pair_t, row_out]
    out_shape = ([jax.ShapeDtypeStruct((t, width // 2), BF16)] * 2 + [jax.ShapeDtypeStruct((t, width), BF16)] * 2
                 + [jax.ShapeDtypeStruct((1, width), F32)])
    if not is_ret:
        out_specs += [pair_t, row_out]
        out_shape += [jax.ShapeDtypeStruct((t, width), BF16), jax.ShapeDtypeStruct((1, width), F32)]
    return _call(
        body, (proj, proj, proj, proj, *aux_arrays, gain, o_raw, states, d_out), name=name,
        grid=(HEADS // PAIR, nblk),
        in_specs=proj_spec + aux_specs + [gain_spec, pair_t, state_spec, dout_spec],
        out_specs=out_specs, out_shape=out_shape,
        scratch_shapes=[pltpu.VMEM((PAIR, LANES, LANES), F32), pltpu.VMEM((PAIR * nc, LANES, LANES), F32)],
        compiler_params=_ARB2, hosted=hosted)


PEER_SLOT = (2, 1, 3)


def _place():
    x, y, c = lax.axis_index("x"), lax.axis_index("y"), lax.axis_index("c")
    chips = [(1 - x, y), (x, 1 - y), (1 - x, 1 - y)]
    return x, y, c, 2 * x + y, chips


def _route_split(rows, dtype):
    tile = 16 if dtype == BF16 else 8
    if rows < 2 * tile:
        return None
    return -(-(rows // 2) // tile) * tile


def _routes(by_peer):
    x, y, c, me, chips = _place()
    (xx, xy), (yx, yy), (dx, dy) = chips
    if by_peer:
        slots = dict(own=0, from_x=PEER_SLOT[0], from_y=PEER_SLOT[1], diag=PEER_SLOT[2],
                     mine_on_x=PEER_SLOT[0], mine_on_y=PEER_SLOT[1])
    else:
        slots = dict(own=me, from_x=2 * xx + xy, from_y=2 * yx + yy, diag=2 * dx + dy, mine_on_x=me, mine_on_y=me)
    return c, (xx, xy, c), (yx, yy, c), (dx, dy, c), (x, y, 1 - c), slots


def _gather_legs(src, out, send_sems, recv_sems, base, by_peer):
    c, to_x, to_y, to_d, sibling, s = _routes(by_peer)
    r0 = _route_split(src.shape[2], src.dtype)

    def cp(k, src_ref, dst_ref, to):
        return pltpu.make_async_remote_copy(src_ref=src_ref, dst_ref=dst_ref, send_sem=send_sems.at[base + k],
                                            recv_sem=recv_sems.at[base + k], device_id=to, device_id_type=MESH)

    mine = src.at[:, c]
    legs = dict(
        x=(cp(0, mine, out.at[s["mine_on_x"], :, c], to_x), cp(0, mine, out.at[s["from_x"], :, c], to_x)),
        y=(cp(1, mine, out.at[s["mine_on_y"], :, c], to_y), cp(1, mine, out.at[s["from_y"], :, c], to_y)),
        pass_x=(cp(4, out.at[s["from_x"], :, c], out.at[s["from_x"], :, c], sibling),
                cp(4, mine, out.at[s["from_x"], :, 1 - c], sibling)),
        pass_y=(cp(5, out.at[s["from_y"], :, c], out.at[s["from_y"], :, c], sibling),
                cp(5, mine, out.at[s["from_y"], :, 1 - c], sibling)))
    if r0 is None:
        mine_on_d = s["diag"] if by_peer else s["own"]
        legs["d"] = (cp(2, mine, out.at[mine_on_d, :, c], to_d), cp(2, mine, out.at[s["diag"], :, c], to_d))
        legs["pass_d"] = (cp(6, out.at[s["diag"], :, c], out.at[s["diag"], :, c], sibling),
                          cp(6, mine, out.at[s["diag"], :, 1 - c], sibling))
        return legs, False
    lo, hi = pl.ds(0, r0), pl.ds(r0, src.shape[2] - r0)
    fx_on_y = s["diag"] if by_peer else s["from_x"]
    fy_on_x = s["diag"] if by_peer else s["from_y"]
    legs.update(
        fwd_y=(cp(2, out.at[s["from_x"], :, c, lo], out.at[fx_on_y, :, c, lo], to_y),
               cp(2, mine.at[:, lo], out.at[s["diag"], :, c, lo], to_y)),
        fwd_x=(cp(3, out.at[s["from_y"], :, c, hi], out.at[fy_on_x, :, c, hi], to_x),
               cp(3, mine.at[:, hi], out.at[s["diag"], :, c, hi], to_x)),
        pass_d0=(cp(6, out.at[s["diag"], :, c, lo], out.at[s["diag"], :, c, lo], sibling),
                 cp(6, mine.at[:, lo], out.at[s["diag"], :, 1 - c, lo], sibling)),
        pass_d1=(cp(7, out.at[s["diag"], :, c, hi], out.at[s["diag"], :, c, hi], sibling),
                 cp(7, mine.at[:, hi], out.at[s["diag"], :, 1 - c, hi], sibling)))
    return legs, True


def _gather_steps(legs, routed):
    def start():
        legs["x"][0].start()
        legs["y"][0].start()
        if not routed:
            legs["d"][0].start()

    def middle():
        legs["x"][1].wait_recv()
        if routed:
            legs["fwd_y"][0].start()
        legs["pass_x"][0].start()
        legs["y"][1].wait_recv()
        if routed:
            legs["fwd_x"][0].start()
        legs["pass_y"][0].start()

    def finish():
        last = ["pass_d0", "pass_d1"] if routed else ["pass_d"]
        if routed:
            legs["fwd_y"][1].wait_recv()
            legs["pass_d0"][0].start()
            legs["fwd_x"][1].wait_recv()
            legs["pass_d1"][0].start()
        else:
            legs["d"][1].wait_recv()
            legs["pass_d"][0].start()
        for name in ["pass_x", "pass_y"] + last:
            legs[name][1].wait_recv()
        for name in ["x", "y", "pass_x", "pass_y"] + last + (["fwd_y", "fwd_x"] if routed else ["d"]):
            legs[name][0].wait_send()

    return start, middle, finish


def _gather_plan(arrs):
    na = len(arrs)

    def steps(ins, outs, send_sems, recv_sems):
        return [_gather_steps(*_gather_legs(ins[a], outs[a], send_sems, recv_sems, 8 * a, False)) for a in range(na)]

    def run(which):
        def hook(*refs):
            for step in steps(*refs):
                step[which]()
        return hook

    return _Hosted(arrs, [jax.ShapeDtypeStruct((N_CHIPS,) + a.shape, a.dtype) for a in arrs], 8 * na,
                   run(0), run(2), middle=run(1))


def _pair_exchange_plan(grads):
    na = len(grads)

    def copies(ins, outs, send_sems, recv_sems):
        x, y, c, _, _ = _place()
        return [pltpu.make_async_remote_copy(
            src_ref=ins[a].at[:, 1 - c], dst_ref=outs[a], send_sem=send_sems.at[a], recv_sem=recv_sems.at[a],
            device_id=(x, y, 1 - c), device_id_type=MESH) for a in range(na)]

    def start(*refs):
        for cp in copies(*refs):
            cp.start()

    def finish(*refs):
        for cp in copies(*refs):
            cp.wait()

    return _Hosted(grads, [jax.ShapeDtypeStruct(g.shape[:1] + g.shape[2:], g.dtype) for g in grads], na, start, finish)


def _small_gather_plan(block):
    def copies(ins, outs, send_sems, recv_sems):
        x, y, c, _, chips = _place()
        peers = [(x, y, 1 - c)] + [(px, py, pc) for px, py in chips for pc in (c, 1 - c)]
        sends = [pltpu.make_async_remote_copy(
            src_ref=ins[0], dst_ref=outs[0].at[4 * x + 2 * y + c], send_sem=send_sems.at[k], recv_sem=recv_sems.at[k],
            device_id=peer, device_id_type=MESH) for k, peer in enumerate(peers)]
        recvs = [pltpu.make_async_remote_copy(
            src_ref=ins[0], dst_ref=outs[0].at[4 * px + 2 * py + pc], send_sem=send_sems.at[k], recv_sem=recv_sems.at[k],
            device_id=(px, py, pc), device_id_type=MESH) for k, (px, py, pc) in enumerate(peers)]
        return sends, recvs

    def start(*refs):
        for cp in copies(*refs)[0]:
            cp.start()

    def finish(*refs):
        sends, recvs = copies(*refs)
        for cp in recvs:
            cp.wait_recv()
        for cp in sends:
            cp.wait_send()

    return _Hosted([block], [jax.ShapeDtypeStruct((8,) + block.shape, block.dtype)], 7, start, finish)


def _sum_devices(blocks):
    def body(b_ref, o_ref):
        acc = b_ref[0]
        for d in range(1, 8):
            acc = acc + b_ref[d]
        o_ref[...] = acc

    return pl.pallas_call(body, name="sum_devices", in_specs=[_VMEM], out_specs=_VMEM,
                          out_shape=jax.ShapeDtypeStruct(blocks.shape[1:], blocks.dtype))(blocks)


def _pair_add(grad, recv, c_arr, name):
    _, _, r, cols = grad.shape

    def body(c_ref, g_ref, r_ref, o_ref):
        o_ref[...] = (g_ref[...].astype(F32) + r_ref[...].astype(F32)).astype(BF16)

    return pl.pallas_call(
        body, name=name,
        grid_spec=pltpu.PrefetchScalarGridSpec(
            num_scalar_prefetch=1, grid=(N_CHIPS,),
            in_specs=[pl.BlockSpec((None, None, r, cols), lambda p, c_ref: (p, c_ref[0], 0, 0)),
                      pl.BlockSpec((None, r, cols), lambda p, c_ref: (p, 0, 0))],
            out_specs=pl.BlockSpec((None, r, cols), lambda p, c_ref: (p, 0, 0))),
        out_shape=jax.ShapeDtypeStruct((N_CHIPS, r, cols), BF16),
        compiler_params=_ARB1,
    )(c_arr, grad, recv)


def _chip_exchange_plan(sums, by_peer=False):
    na = len(sums)

    def copies(ins, outs, send_sems, recv_sems):
        x, y, c, me, chips = _place()

        def copy(a, j, px, py, block, slot):
            return pltpu.make_async_remote_copy(
                src_ref=ins[a].at[block], dst_ref=outs[a].at[slot],
                send_sem=send_sems.at[3 * a + j], recv_sem=recv_sems.at[3 * a + j],
                device_id=(px, py, c), device_id_type=MESH)

        peers = [(a, j, px, py) for a in range(na) for j, (px, py) in enumerate(chips)]
        return me, peers, copy

    def start(*refs):
        me, peers, copy = copies(*refs)
        for a, j, px, py in peers:
            if by_peer:
                copy(a, j, px, py, PEER_SLOT[j], PEER_SLOT[j]).start()
            else:
                copy(a, j, px, py, 2 * px + py, me).start()

    def finish(*refs):
        me, peers, copy = copies(*refs)
        for a, j, px, py in peers:
            if by_peer:
                copy(a, j, px, py, PEER_SLOT[j], PEER_SLOT[j]).wait_recv()
            else:
                copy(a, j, px, py, me, 2 * px + py).wait_recv()
        for a, j, px, py in peers:
            if by_peer:
                copy(a, j, px, py, PEER_SLOT[j], PEER_SLOT[j]).wait_send()
            else:
                copy(a, j, px, py, 2 * px + py, me).wait_send()

    return _Hosted(sums, [jax.ShapeDtypeStruct(s.shape, s.dtype) for s in sums], 3 * na, start, finish)


def _chip_sum(own, recv, me_arr, name):
    _, r, cols = recv.shape

    def body(me_ref, own_ref, r_ref, o_ref):
        o_ref[...] = jnp.zeros_like(o_ref)
        for q in range(N_CHIPS):
            @pl.when(me_ref[0] == q)
            def _():
                o_ref[...] += own_ref[...].astype(F32)

            @pl.when(me_ref[0] != q)
            def _():
                o_ref[...] += r_ref[q].astype(F32)

    return pl.pallas_call(
        body, name=name,
        grid_spec=pltpu.PrefetchScalarGridSpec(
            num_scalar_prefetch=1, grid=(1,),
            in_specs=[pl.BlockSpec((None, r, cols), lambda i, me_ref: (me_ref[0], 0, 0)),
                      pl.BlockSpec((N_CHIPS, r, cols), lambda i, me_ref: (0, 0, 0))],
            out_specs=pl.BlockSpec((r, cols), lambda i, me_ref: (0, 0))),
        out_shape=jax.ShapeDtypeStruct((r, cols), F32),
        compiler_params=_ARB1,
    )(me_arr, own, recv)


def _peer_sum(own, recv, name):
    _, r, cols = recv.shape

    def body(own_ref, r_ref, o_ref):
        acc = own_ref[...].astype(F32) + r_ref[1].astype(F32)
        acc = acc + r_ref[2].astype(F32)
        o_ref[...] = acc + r_ref[3].astype(F32)

    return pl.pallas_call(
        body, name=name, grid=(1,),
        in_specs=[pl.BlockSpec((None, r, cols), lambda i: (0, 0, 0)), pl.BlockSpec((N_CHIPS, r, cols), lambda i: (0, 0, 0))],
        out_specs=pl.BlockSpec((r, cols), lambda i: (0, 0)),
        out_shape=jax.ShapeDtypeStruct((r, cols), F32),
        compiler_params=_ARB1,
    )(own, recv)


def _pair_share_plan(halves):
    na = len(halves)

    def copies(ins, outs, send_sems, recv_sems):
        x, y, c, _, _ = _place()
        return [pltpu.make_async_remote_copy(
            src_ref=ins[a], dst_ref=outs[a], send_sem=send_sems.at[a], recv_sem=recv_sems.at[a],
            device_id=(x, y, 1 - c), device_id_type=MESH) for a in range(na)]

    def start(*refs):
        for cp in copies(*refs):
            cp.start()

    def finish(*refs):
        for cp in copies(*refs):
            cp.wait()

    return _Hosted(halves, [jax.ShapeDtypeStruct(h.shape, h.dtype) for h in halves], na, start, finish)


def _small_allreduce(block):
    m, n = block.shape

    def body(x_ref, all_ref, sum_ref, send_sems, recv_sems, local_sem):
        x, y, c, _, chips = _place()
        me, sibling = (x, y, c), (x, y, 1 - c)

        def rows(px, py, pc):
            return all_ref.at[pl.ds((4 * px + 2 * py + pc) * m, m), :]

        def copy(k, blk, to, src=None):
            return pltpu.make_async_remote_copy(
                src_ref=rows(*blk) if src is None else src, dst_ref=rows(*blk),
                send_sem=send_sems.at[k], recv_sem=recv_sems.at[k], device_id=to, device_id_type=MESH)

        mine = pltpu.make_async_copy(x_ref, rows(*me), local_sem)
        mine.start()
        first = [copy(0, me, sibling, src=x_ref)]
        first += [copy(1 + j, me, (*chip, c), src=x_ref) for j, chip in enumerate(chips)]
        for cp in first:
            cp.start()
        passed = [copy(4 + j, (*chip, c), sibling) for j, chip in enumerate(chips)]
        for j, chip in enumerate(chips):
            copy(1 + j, (*chip, c), me).wait_recv()
            passed[j].start()
        copy(0, sibling, me).wait_recv()
        for j, chip in enumerate(chips):
            copy(4 + j, (*chip, 1 - c), me).wait_recv()
        for cp in first + passed:
            cp.wait_send()
        mine.wait()
        acc = all_ref[0:m, :]
        for d in range(1, 8):
            acc = acc + all_ref[d * m:(d + 1) * m, :]
        sum_ref[...] = acc

    vmem = pl.BlockSpec(memory_space=pltpu.VMEM)
    return pl.pallas_call(
        body, name="small_allreduce",
        in_specs=[vmem], out_specs=[vmem, vmem],
        out_shape=[jax.ShapeDtypeStruct((8 * m, n), F32), jax.ShapeDtypeStruct((m, n), F32)],
        scratch_shapes=[pltpu.SemaphoreType.DMA((7,)), pltpu.SemaphoreType.DMA((7,)), pltpu.SemaphoreType.DMA],
    )(block)[1]


def _row_tile(rows):
    best = rows
    for cand in range(8, min(rows, 512) + 1, 8):
        if rows % cand == 0:
            best = cand
    return best


def _adamw_math(w, g, m, v):
    m2 = ADAM_B1 * m + (1.0 - ADAM_B1) * g
    v2 = ADAM_B2 * v + (1.0 - ADAM_B2) * (g * g)
    m_hat = m2 / (1.0 - ADAM_B1 ** ADAM_STEP)
    v_hat = v2 / (1.0 - ADAM_B2 ** ADAM_STEP)
    return -ADAM_LR * (m_hat / (jnp.sqrt(v_hat) + ADAM_EPS) + ADAM_WD * w), m2, v2


def _adamw_halves(w, g_mine, g_other, m, v, c_arr, name):
    rows, cols = w.shape
    r = rows // 2
    tr = _row_tile(r)
    nt = r // tr

    def body(c_ref, w_ref, gm_ref, go_ref, m_ref, v_ref, g_ref, d_ref, nm_ref, nv_ref):
        gv = jnp.where(pl.program_id(0) == c_ref[0], gm_ref[...], go_ref[...])
        g_ref[...] = gv
        d_ref[...], nm_ref[...], nv_ref[...] = _adamw_math(w_ref[...], gv, m_ref[...], v_ref[...])

    full = pl.BlockSpec((tr, cols), lambda h, i, c_ref: (h * nt + i, 0))
    half = pl.BlockSpec((tr, cols), lambda h, i, c_ref: (i, 0))
    shape = jax.ShapeDtypeStruct((rows, cols), F32)
    return pl.pallas_call(
        body, name=name,
        grid_spec=pltpu.PrefetchScalarGridSpec(
            num_scalar_prefetch=1, grid=(2, nt),
            in_specs=[full, half, half, full, full], out_specs=[full] * 4),
        out_shape=[shape] * 4,
        compiler_params=_ARB2,
    )(c_arr, w, g_mine, g_other, m, v)


def _adamw(w, g, m, v, name):
    rows, cols = w.shape
    tr = _row_tile(rows)

    def body(w_ref, g_ref, m_ref, v_ref, d_ref, nm_ref, nv_ref):
        d_ref[...], nm_ref[...], nv_ref[...] = _adamw_math(w_ref[...], g_ref[...], m_ref[...], v_ref[...])

    spec = pl.BlockSpec((tr, cols), lambda i: (i, 0))
    shape = jax.ShapeDtypeStruct((rows, cols), F32)
    return pl.pallas_call(
        body, name=name, grid=(rows // tr,),
        in_specs=[spec] * 4, out_specs=[spec] * 3, out_shape=[shape] * 3,
        compiler_params=_ARB1,
    )(w, g, m, v)


def _pad_w_in_t(w_in_t):
    return jnp.pad(w_in_t, ((0, PROJ_P - IN_WIDTH), (0, 0)))


def _unpad_w_in_t(w_pt):
    return w_pt[0:IN_WIDTH]


def _rope_tables(t):
    half = 32
    inv = ROPE_BASE ** (-jnp.arange(half, dtype=F32) * 2.0 / 64)
    ang = jnp.arange(t, dtype=F32)[:, None] * inv[None, :]
    cos, sin = jnp.cos(ang), jnp.sin(ang)
    z32, z64 = jnp.zeros((t, 32), F32), jnp.zeros((t, 64), F32)
    return (jnp.concatenate([cos, cos, z64], axis=1),
            jnp.concatenate([-sin, z32, z64], axis=1),
            jnp.concatenate([z32, sin, z64], axis=1))


def _halves(w):
    n, rows, cols = w.shape
    return w.reshape(n, 2, rows // 2, cols)


_VMEM = pl.BlockSpec(memory_space=pltpu.VMEM)


def _pack_small(n1, nm, n2, nf, nret, ngla, ba, wa2_p, loss_blk):
    def body(n1_ref, nm_ref, n2_ref, nf_ref, nret_ref, ngla_ref, ba_ref, wa2_ref, loss_ref, o_ref):
        o_ref[...] = jnp.zeros_like(o_ref)
        o_ref[0:1, :] = n1_ref[...]
        o_ref[1:2, :] = nm_ref[...]
        o_ref[2:3, :] = n2_ref[...]
        o_ref[3:4, :] = nf_ref[...]
        o_ref[4:5, 0:512] = nret_ref[...]
        o_ref[4:5, 512:1024] = ngla_ref[...]
        o_ref[5:6, 0:256] = ba_ref[...]
        o_ref[6:7, 0:LANES] = loss_ref[0:1, :]
        o_ref[8:8 + GATE_RANK, 0:HEADS * LANES] = wa2_ref[0:GATE_RANK, :]

    return pl.pallas_call(
        body, name="pack_small", in_specs=[_VMEM] * 9, out_specs=_VMEM,
        out_shape=jax.ShapeDtypeStruct((SMALL_ROWS, D_MODEL), F32),
    )(n1, nm, n2, nf, nret, ngla, ba, wa2_p, loss_blk)


def _small_update(summed, chip_arr, ws, ms, vs):
    n = len(ws)

    def body(chip_ref, s_ref, *refs):
        w_refs, m_refs, v_refs = refs[0:n], refs[n:2 * n], refs[2 * n:3 * n]
        outs = refs[3 * n:]
        wa2_all = s_ref[8:8 + GATE_RANK, 0:HEADS * LANES]
        wa2_g = jnp.zeros((GATE_RANK, 64), F32)
        for p in range(N_CHIPS):
            wa2_g = jnp.where(chip_ref[0] == p, wa2_all[:, LANES * p:LANES * p + 64], wa2_g)
        grads = [s_ref[0:1, :], s_ref[1:2, :], s_ref[2:3, :], s_ref[3:4, :], s_ref[4:5, 0:512],
                 s_ref[4:5, 512:1024], s_ref[5:6, 0:256], wa2_g]
        for k in range(n):
            d, m2, v2 = _adamw_math(w_refs[k][...], grads[k], m_refs[k][...], v_refs[k][...])
            outs[k][...] = grads[k]
            outs[n + k][...] = d
            outs[2 * n + k][...] = m2
            outs[3 * n + k][...] = v2

    shapes = [jax.ShapeDtypeStruct(w.shape, F32) for w in ws] * 4
    smem = pl.BlockSpec(memory_space=pltpu.SMEM)
    outs = pl.pallas_call(
        body, name="small_update", in_specs=[smem] + [_VMEM] * (1 + 3 * n), out_specs=[_VMEM] * (4 * n),
        out_shape=shapes,
    )(chip_arr, summed, *ws, *ms, *vs)
    return outs[0:n], outs[n:2 * n], outs[2 * n:3 * n], outs[3 * n:4 * n]


def _pad_in_rows(w_t):
    return jnp.pad(w_t, ((0, IN_ROWS - IN_SHARD), (0, 0)))


def _forward_backward(xs, target, ffn1_w, rest, ba_p, ffn1_norm_g, mix_norm_g, ret_norm_g, gla_norm_g, ffn2_norm_g,
                      final_norm_g, ffn1_gather=None, rest_plan=None, rest_weights=None, ffn2_plans=None,
                      ffn2_weights=None, ffn2_pairs=None, ffn2_pairs_done=None, early=None, late=None, small_plan=None):
    t = xs.shape[0]
    cos_t, sa_t, sb_t = _rope_tables(t)
    log_gamma = jnp.log(1.0 - 2.0 ** (-5.0 - jnp.arange(HEADS, dtype=F32)))
    lg_t = jnp.broadcast_to(log_gamma[:, None, None], (HEADS, 1, LANES))
    ret_aux = [cos_t, sa_t, sb_t, lg_t]

    if ffn1_gather is None:
        (x1, a1, u1, h1), gathered = _ffn_fwd(xs, ffn1_norm_g, ffn1_w, "ffn1_fwd", hosted=rest_plan)
    else:
        ffn1_shard, ffn1_weights = ffn1_gather
        (x1, a1, u1, h1, wall), gathered = _ffn1_fwd_gathering(xs, ffn1_norm_g, ffn1_shard, "ffn1_fwd",
                                                               hosted=rest_plan)
        ffn1_w = ffn1_weights(wall)
    ffn2_w, w_in_pt, w_out_full, wa2_p = rest if rest_plan is None else rest_weights(gathered)
    plans = [None] * 3 if ffn2_plans is None else ffn2_plans
    (proj, h_mix), got_gate = _mixer_in_fwd(x1, mix_norm_g, w_in_pt, "mixer_in_fwd", hosted=plans[0])
    gla_aux = [proj, wa2_p, ba_p]
    (o_ret, raw_ret, st_ret), got_up = _attn_fwd(True, proj, ret_aux, ret_norm_g, "ret_fwd", hosted=plans[1])
    (o_gla, raw_gla, st_gla), got_down = _attn_fwd(False, proj, gla_aux, gla_norm_g, "gla_fwd", hosted=plans[2])
    if ffn2_plans is not None:
        ffn2_w = ffn2_weights(got_gate + got_up + got_down)
    x2 = _mixer_out_fwd(o_ret, o_gla, w_out_full, x1, "mixer_out_fwd")
    (x3, a2, u2, h2), _ = _ffn_fwd(x2, ffn2_norm_g, ffn2_w, "ffn2_fwd")
    loss_blk, dx3, d_final_g = _final_loss(x3, final_norm_g, target, "final_loss")

    (da2, du2, hid2, dob2, dx2, d_ffn2_g), _ = _ffn_bwd(dx3, x2, ffn2_norm_g, a2, u2, ffn2_w, "ffn2_bwd")
    g_gate2 = _matmul_tn(da2, h2, "ffn2_dgate", out_dtype=BF16)
    g_up2 = _matmul_tn(du2, h2, "ffn2_dup", out_dtype=BF16)
    g_down2 = _matmul_tn(hid2, dob2, "ffn2_ddown", out_dtype=BF16)

    d_o = _matmul_nt(dx2, w_out_full, "mixer_out_bwd")
    g_wout_ret = _matmul_tn(o_ret, dx2, "wout_grad_ret", out_dtype=BF16)
    g_wout_gla = _matmul_tn(o_gla, dx2, "wout_grad_gla", out_dtype=BF16)
    pairs_plan = None if ffn2_pairs is None else ffn2_pairs([g_gate2, g_up2, g_down2])
    (*dproj_ret, d_ret_g), pair_recv = _attn_bwd(True, proj, ret_aux, ret_norm_g, raw_ret, st_ret, d_o, "ret_bwd",
                                                 hosted=pairs_plan)
    if ffn2_pairs is not None:
        ffn2_pairs_done(pair_recv)
    (*dproj_gla, d_gla_g, dlogit, d_ba_p), _ = _attn_bwd(False, proj, gla_aux, gla_norm_g, raw_gla, st_gla, d_o,
                                                        "gla_bwd")
    d_glow = _matmul_nt(dlogit, wa2_p, "gate_low_bwd", out_dtype=BF16)
    g_wa2_p = _matmul_tn(proj[:, PROJ_P - LANES:], dlogit, "gate_w_grad")
    dproj = jnp.concatenate(dproj_ret + dproj_gla + [d_glow], axis=1)
    g_win_p = _matmul_tn(dproj, h_mix, "w_in_grad", tka=PROJ_P // PROJ_TILES, out_dtype=BF16)
    dx1, d_mix_g = _mixer_in_bwd(dproj, w_in_pt, dx2, x1, mix_norm_g, "mixer_in_bwd")
    g_win_t = _unpad_w_in_t(g_win_p[0])
    g_win = jnp.stack([_pad_in_rows(g_win_t[IN_SHARD * p:IN_SHARD * (p + 1)]) for p in range(N_CHIPS)], axis=0)
    g_wout = jnp.concatenate([g_wout_ret[0], g_wout_gla[0]], axis=0).reshape(N_CHIPS, D_MODEL // N_CHIPS, D_MODEL)

    early_grads = [g_win, g_wout] if ffn2_pairs is not None else [g_gate2, g_up2, g_down2, g_win, g_wout]
    early_plan = None if early is None else early(early_grads)
    (da1, du1, hid1, dob1, grad_x, d_ffn1_g), arrived = _ffn_bwd(dx1, xs, ffn1_norm_g, a1, u1, ffn1_w, "ffn1_bwd",
                                                                hosted=early_plan)
    d_ba = d_ba_p.reshape(HEADS, LANES)[:, 0:64].reshape(1, 256)
    small_local = _pack_small(d_ffn1_g, d_mix_g, d_ffn2_g, d_final_g, d_ret_g, d_gla_g, d_ba, g_wa2_p[0], loss_blk)
    late_grads, late_arrived = [], []
    for lhs, rhs, name in ((da1, h1, "ffn1_dgate"), (du1, h1, "ffn1_dup"), (hid1, dob1, "ffn1_ddown")):
        if late is None:
            plan = None
        else:
            plan = late(late_grads[-1], len(late_grads)) if late_grads else small_plan(small_local)
        res = _matmul_tn(lhs, rhs, name, out_dtype=BF16, hosted=plan)
        if plan is not None:
            res, carried = res
            late_arrived += carried
        late_grads.append(res)
    g_gate1, g_up1, g_down1 = late_grads

    return (small_local, grad_x, g_gate1, g_up1, g_down1, g_gate2, g_up2, g_down2, g_win, g_wout, g_wa2_p,
            d_ba_p, d_ffn1_g, d_mix_g, d_ffn2_g, d_final_g, d_ret_g, d_gla_g, arrived, late_arrived)


def kernel(x, ffn1_norm_g, ffn1_w_gate, ffn1_w_up, ffn1_w_down, mix_norm_g, w_in, ret_norm_g, gla_w_a2, gla_b_a, gla_norm_g, w_out, ffn2_norm_g, ffn2_w_gate, ffn2_w_up, ffn2_w_down, final_norm_g, loss_target, m_ffn1_norm_g, m_ffn1_w_gate, m_ffn1_w_up, m_ffn1_w_down, m_mix_norm_g, m_w_in, m_ret_norm_g, m_gla_w_a2, m_gla_b_a, m_gla_norm_g, m_w_out, m_ffn2_norm_g, m_ffn2_w_gate, m_ffn2_w_up, m_ffn2_w_down, m_final_norm_g, v_ffn1_norm_g, v_ffn1_w_gate, v_ffn1_w_up, v_ffn1_w_down, v_mix_norm_g, v_w_in, v_ret_norm_g, v_gla_w_a2, v_gla_b_a, v_gla_norm_g, v_w_out, v_ffn2_norm_g, v_ffn2_w_gate, v_ffn2_w_up, v_ffn2_w_down, v_final_norm_g):
    t = x.shape[1]
    xs = x.reshape(t, D_MODEL)
    target = loss_target.reshape(t, D_MODEL)
    chip = 2 * lax.axis_index("x") + lax.axis_index("y")
    c_arr = lax.axis_index("c").astype(jnp.int32).reshape(1)

    me_arr = chip.astype(jnp.int32).reshape(1)

    pad_rows = _pad_in_rows

    def own_block(gathered, shard):
        return lax.dynamic_update_slice(gathered, shard[None], (chip,) + (0,) * shard.ndim)

    ffn1_shard = _halves(jnp.stack([ffn1_w_gate[0].T, ffn1_w_up[0].T, ffn1_w_down[0]], axis=0).astype(BF16))
    rest_shards = [_halves(pad_rows(w_in[0].T).astype(BF16)[None]),
                   _halves(w_out.astype(BF16)),
                   jnp.concatenate([gla_w_a2.reshape(GATE_RANK, 64), jnp.zeros((GATE_RANK, 64), F32)],
                                   axis=1).reshape(1, 2, 8, LANES)]
    ffn2_shards = [_halves(w.astype(BF16)[None]) for w in (ffn2_w_gate[0].T, ffn2_w_up[0].T, ffn2_w_down[0])]
    def ffn1_weights(gathered):
        return lax.dynamic_update_slice(gathered, ffn1_shard[None], (0,) * 5).reshape(N_CHIPS, 3, FF_SHARD, D_MODEL)

    def rest_weights(gathered):
        win_all, wout_all, wa2_all = [own_block(g, s) for g, s in zip(gathered, rest_shards)]
        win_t = win_all.reshape(N_CHIPS, IN_ROWS, D_MODEL)
        w_in_pt = jnp.zeros((PROJ_P, D_MODEL), BF16)
        for p in range(N_CHIPS):
            w_in_pt = lax.dynamic_update_slice(w_in_pt, win_t[p, 0:IN_SHARD], (IN_SHARD * p, 0))
        wa2_p = jnp.pad(
            wa2_all.reshape(N_CHIPS, GATE_RANK, LANES).transpose(1, 0, 2).reshape(GATE_RANK, HEADS * LANES),
            ((0, LANES - GATE_RANK), (0, 0))).astype(BF16)
        return (None, w_in_pt, wout_all.reshape(D_MODEL, D_MODEL), wa2_p)

    def ffn2_weights(gathered):
        return [own_block(g, s).reshape(N_CHIPS, FF_SHARD, D_MODEL) for g, s in zip(gathered, ffn2_shards)]

    def by_halves(g):
        return g.reshape(g.shape[0], 2, g.shape[1] // 2, g.shape[2])

    def pair_adds(halves, recv, tag):
        return [_pair_add(g, r, c_arr, "pair_add_%s%d" % (tag, k)) for k, (g, r) in enumerate(zip(halves, recv))]

    def pair_sums(grads, tag):
        halves = [by_halves(g) for g in grads]
        recv = _run_hosted(_pair_exchange_plan(halves), "pair_exchange_" + tag, sibling_only=True)
        return pair_adds(halves, recv, tag)

    early_sums, ffn2_halves = [], []

    def ffn2_pairs(grads):
        ffn2_halves.extend(by_halves(g) for g in grads)
        return _pair_exchange_plan(ffn2_halves)

    def ffn2_pairs_done(recv):
        early_sums.extend(pair_adds(ffn2_halves, recv, "ffn2_"))

    def early(grads):
        early_sums.extend(pair_sums(grads, "early"))
        return _chip_exchange_plan(early_sums)

    late_sums = []

    def late(grad, number):
        late_sums.extend(pair_sums([grad], "late%d" % number))
        return _chip_exchange_plan(late_sums[-1:], by_peer=True)

    ba_p = jnp.pad(gla_b_a.reshape(HEADS, 64), ((0, 0), (0, 64))).reshape(1, HEADS * LANES)
    fb = _forward_backward(xs, target, None, None, ba_p, ffn1_norm_g, mix_norm_g, ret_norm_g, gla_norm_g,
                           ffn2_norm_g, final_norm_g.reshape(1, D_MODEL), ffn1_gather=(ffn1_shard, ffn1_weights),
                           rest_plan=_gather_plan(rest_shards), rest_weights=rest_weights,
                           ffn2_plans=[_gather_plan(ffn2_shards[0:2]), None, _gather_plan(ffn2_shards[2:3])],
                           ffn2_weights=ffn2_weights,
                           ffn2_pairs=ffn2_pairs, ffn2_pairs_done=ffn2_pairs_done, early=early, late=late,
                           small_plan=_small_gather_plan)
    (small_local, grad_x, _, _, g_down1, _, _, _, _, _, _, _, _, _, _, _, _, _, early_arrived, late_arrived) = fb
    small_all, late_arrived = late_arrived[0], late_arrived[1:]
    late_arrived = late_arrived + _run_hosted(late(g_down1, 3), "chip_exchange_late")
    mine = [_peer_sum(s, r, "chip_sum_%d" % k) for k, (s, r) in enumerate(zip(late_sums, late_arrived))]
    mine += [_chip_sum(s, r, me_arr, "chip_sum_%d" % (3 + k)) for k, (s, r) in enumerate(zip(early_sums, early_arrived))]
    other = _run_hosted(_pair_share_plan(mine), "pair_share", sibling_only=True)

    device = 2 * chip + lax.axis_index("c")
    small_sum = _sum_devices(lax.dynamic_update_slice(small_all, small_local[None], (device, 0, 0)))
    loss = small_sum[6, 0]

    def rows(n1, nm, n2, nf, nret, ngla, ba, wa2):
        return [n1, nm, n2, nf.reshape(1, D_MODEL), nret, ngla, ba, wa2.reshape(GATE_RANK, 64)]

    small = _small_update(
        small_sum, me_arr,
        rows(ffn1_norm_g, mix_norm_g, ffn2_norm_g, final_norm_g, ret_norm_g, gla_norm_g, gla_b_a, gla_w_a2),
        rows(m_ffn1_norm_g, m_mix_norm_g, m_ffn2_norm_g, m_final_norm_g, m_ret_norm_g, m_gla_norm_g, m_gla_b_a,
             m_gla_w_a2),
        rows(v_ffn1_norm_g, v_mix_norm_g, v_ffn2_norm_g, v_final_norm_g, v_ret_norm_g, v_gla_norm_g, v_gla_b_a,
             v_gla_w_a2))
    s_grad, s_delta, s_m, s_v = [
        [*o[0:3], o[3].reshape(D_MODEL), *o[4:7], o[7].reshape(1, GATE_RANK, 64)] for o in small]

    def big(k, w, m, v, name, to_2d, from_2d):
        outs4 = _adamw_halves(to_2d(w), mine[k], other[k], to_2d(m), to_2d(v), c_arr, name)
        return [from_2d(z) for z in outs4]

    plain = (lambda w: w[0], lambda z: z[None])
    transposed = (lambda w: w[0].T, lambda z: z.T[None])
    in_proj = (lambda w: pad_rows(w[0].T), lambda z: z[0:IN_SHARD].T[None])
    r_g1 = big(0, ffn1_w_gate, m_ffn1_w_gate, v_ffn1_w_gate, "adamw_ffn1_gate", *transposed)
    r_u1 = big(1, ffn1_w_up, m_ffn1_w_up, v_ffn1_w_up, "adamw_ffn1_up", *transposed)
    r_d1 = big(2, ffn1_w_down, m_ffn1_w_down, v_ffn1_w_down, "adamw_ffn1_down", *plain)
    r_g2 = big(3, ffn2_w_gate, m_ffn2_w_gate, v_ffn2_w_gate, "adamw_ffn2_gate", *transposed)
    r_u2 = big(4, ffn2_w_up, m_ffn2_w_up, v_ffn2_w_up, "adamw_ffn2_up", *transposed)
    r_d2 = big(5, ffn2_w_down, m_ffn2_w_down, v_ffn2_w_down, "adamw_ffn2_down", *plain)
    r_in = big(6, w_in, m_w_in, v_w_in, "adamw_w_in", *in_proj)
    r_out = big(7, w_out, m_w_out, v_w_out, "adamw_w_out", *plain)

    def leaves(k, smalls):
        n1, nm, n2, nf, nret, ngla, ba, wa2 = smalls
        return [n1, r_g1[k], r_u1[k], r_d1[k], nm, r_in[k], nret, wa2, ba, ngla, r_out[k], n2, r_g2[k], r_u2[k], r_d2[k], nf]

    outs = [loss, grad_x.reshape(x.shape)]
    outs += leaves(0, s_grad) + leaves(1, s_delta) + leaves(2, s_m) + leaves(3, s_v)
    return tuple(outs)
```

```python
import functools

import jax
import jax.numpy as jnp
from jax import lax
from jax.experimental import pallas as pl
from jax.experimental.pallas import tpu as pltpu

F32, BF16 = jnp.float32, jnp.bfloat16
MESH = pl.DeviceIdType.MESH
ANY = pl.BlockSpec(memory_space=pl.ANY)

D_MODEL = 1024
D_FF = 2816
N_CHIPS = 4
FF_SHARD = D_FF // N_CHIPS
IN_WIDTH = 3088
IN_SHARD = IN_WIDTH // N_CHIPS
IN_ROWS = 800
CHUNK = 64
HEADS = 4
LANES = 128
PROJ_P = 3072 + LANES
PROJ_TILES = 5
GATE_RANK = 16
QK_SCALE = 0.125
GATE_NORM = 16.0
RMS_EPS = 1e-6
ROPE_BASE = 10000.0
ADAM_LR, ADAM_B1, ADAM_B2, ADAM_EPS, ADAM_WD, ADAM_STEP = 0.001, 0.9, 0.999, 1e-08, 0.01, 10
SMALL_ROWS = 32
TOKEN_TILE = 512
ATTN_TILE = 512

_ARB2 = pltpu.CompilerParams(dimension_semantics=("arbitrary", "arbitrary"))
_ARB1 = pltpu.CompilerParams(dimension_semantics=("arbitrary",))
_ARB3 = pltpu.CompilerParams(dimension_semantics=("arbitrary", "arbitrary", "arbitrary"))


def _dot(a, b):
    return jnp.dot(a, b, preferred_element_type=F32)


def _dot_nt(a, b):
    return lax.dot_general(a, b, (((1,), (1,)), ((), ())), preferred_element_type=F32)


def _dot_tn(a, b):
    return lax.dot_general(a, b, (((0,), (0,)), ((), ())), preferred_element_type=F32)


def _rms_scale(xv):
    return lax.rsqrt(jnp.mean(xv * xv, axis=-1, keepdims=True) + RMS_EPS)


def _rms_bwd(dh, xv, g):
    r = _rms_scale(xv)
    xhat = xv * r
    dxhat = dh * g
    dx = r * (dxhat - xhat * jnp.mean(dxhat * xhat, axis=-1, keepdims=True))
    return dx, jnp.sum(dh * xhat, axis=0, keepdims=True)


def _silu_grad(a, sg):
    return sg * (1.0 + a * (1.0 - sg))


class _Hosted:
    def __init__(self, arrays, out_shapes, n_sems, start, finish, middle=None, peers=None):
        self.arrays, self.out_shapes, self.n_sems = list(arrays), list(out_shapes), n_sems
        self.start, self.finish = start, finish
        self.middle = middle if middle is not None else (lambda *refs: None)
        self.peers = peers


PEER_SETS = {
    "sibling": (0, lambda x, y, c: [(x, y, 1 - c)]),
    "chips": (1, lambda x, y, c: [(1 - x, y, c), (x, 1 - y, c), (1 - x, 1 - y, c)]),
    "neighbours": (2, lambda x, y, c: [(1 - x, y, c), (x, 1 - y, c), (x, y, 1 - c)]),
    "chips_sibling": (3, lambda x, y, c: [(1 - x, y, c), (x, 1 - y, c), (1 - x, 1 - y, c), (x, y, 1 - c)]),
}


def _handshake(kind):
    x, y, c, _, _ = _place()
    peers = PEER_SETS[kind][1](x, y, c)
    barrier = pltpu.get_barrier_semaphore()
    for peer in peers:
        pl.semaphore_signal(barrier, inc=1, device_id=peer, device_id_type=MESH)
    pl.semaphore_wait(barrier, len(peers))


def _with_barrier(compiler_params, kind):
    if kind is None:
        return compiler_params
    semantics = None if compiler_params is None else compiler_params.dimension_semantics
    return pltpu.CompilerParams(dimension_semantics=semantics, collective_id=PEER_SETS[kind][0])


def _call(body, args, *, name, grid, in_specs, out_specs, out_shape, scratch_shapes, compiler_params, hosted=None):
    if hosted is None:
        outs = pl.pallas_call(body, name=name, grid=grid, in_specs=in_specs, out_specs=out_specs, out_shape=out_shape,
                              scratch_shapes=scratch_shapes, compiler_params=compiler_params)(*args)
        return list(outs), []
    n_in, n_out, n_sc, nh = len(in_specs), len(out_specs), len(scratch_shapes), len(hosted.arrays)

    def wrapped(*refs):
        ins, h_in = refs[:n_in], refs[n_in:n_in + nh]
        outs, h_out = refs[n_in + nh:n_in + nh + n_out], refs[n_in + nh + n_out:n_in + 2 * nh + n_out]
        rest = refs[n_in + 2 * nh + n_out:]
        scratch, (send_sems, recv_sems) = rest[:n_sc], rest[n_sc:]
        step = functools.reduce(lambda flat, d: flat * grid[d] + pl.program_id(d), range(len(grid)), 0)
        total = functools.reduce(lambda a, b: a * b, grid)

        @pl.when(step == 0)
        def _():
            if hosted.peers is not None:
                _handshake(hosted.peers)
            hosted.start(h_in, h_out, send_sems, recv_sems)

        @pl.when(step == total // 2)
        def _():
            hosted.middle(h_in, h_out, send_sems, recv_sems)

        body(*ins, *outs, *scratch)
        last = step == total - 1

        @pl.when(last)
        def _():
            hosted.finish(h_in, h_out, send_sems, recv_sems)

    sems = [pltpu.SemaphoreType.DMA((hosted.n_sems,)), pltpu.SemaphoreType.DMA((hosted.n_sems,))]
    outs = pl.pallas_call(
        wrapped, name=name, grid=grid, in_specs=list(in_specs) + [ANY] * nh, out_specs=list(out_specs) + [ANY] * nh,
        out_shape=list(out_shape) + hosted.out_shapes, scratch_shapes=list(scratch_shapes) + sems,
        compiler_params=_with_barrier(compiler_params, hosted.peers))(*args, *hosted.arrays)
    return list(outs[:n_out]), list(outs[n_out:])


def _run_hosted(hosted, name):
    nh = len(hosted.arrays)

    def body(*refs):
        h_in, h_out, (send_sems, recv_sems) = refs[:nh], refs[nh:2 * nh], refs[2 * nh:]
        if hosted.peers is not None:
            _handshake(hosted.peers)
        hosted.start(h_in, h_out, send_sems, recv_sems)
        hosted.middle(h_in, h_out, send_sems, recv_sems)
        hosted.finish(h_in, h_out, send_sems, recv_sems)

    sems = [pltpu.SemaphoreType.DMA((hosted.n_sems,)), pltpu.SemaphoreType.DMA((hosted.n_sems,))]
    return list(pl.pallas_call(body, name=name, in_specs=[ANY] * nh, out_specs=[ANY] * nh, out_shape=hosted.out_shapes,
                               scratch_shapes=sems, compiler_params=_with_barrier(None, hosted.peers))(*hosted.arrays))


def _ffn_weight_operands(ffn_w, chunk_maps):
    if isinstance(ffn_w, (list, tuple)):
        specs = [pl.BlockSpec((None, FF_SHARD, D_MODEL), lambda *g, m=m: (m(*g), 0, 0)) for m in chunk_maps]
        return list(ffn_w), specs
    specs = [pl.BlockSpec((None, None, FF_SHARD, D_MODEL), lambda *g, m=m, k=kind: (m(*g), k, 0, 0))
             for kind, m in enumerate(chunk_maps)]
    return [ffn_w] * 3, specs


def _pipeline_items(steps):
    def cur(s):
        c = jnp.minimum(s, steps - 1)
        return c // N_CHIPS, c % N_CHIPS

    def prev(s):
        p = jnp.maximum(s - 1, 0)
        return p // N_CHIPS, p % N_CHIPS

    return cur, prev


def _ffn_fwd(x, g, ffn_w, name, hosted=None):
    t = x.shape[0]
    tm = min(t, TOKEN_TILE)

    def body(x_ref, g_ref, wg_ref, wu_ref, wd_ref, xo_ref, a_ref, u_ref, h_ref, acc_ref):
        j = pl.program_id(1)

        @pl.when(j == 0)
        def _():
            xv = x_ref[...]
            h_ref[...] = ((xv * _rms_scale(xv)) * g_ref[...]).astype(BF16)
            acc_ref[...] = jnp.zeros_like(acc_ref)

        h = h_ref[...]
        a = _dot_nt(h, wg_ref[...])
        u = _dot_nt(h, wu_ref[...])
        a_ref[...] = a.astype(BF16)
        u_ref[...] = u.astype(BF16)
        hid = (a * jax.nn.sigmoid(a)) * u
        acc_ref[...] += _dot(hid.astype(BF16), wd_ref[...])

        @pl.when(j == N_CHIPS - 1)
        def _():
            xo_ref[...] = x_ref[...] + 0.5 * acc_ref[...]

    tok = pl.BlockSpec((tm, D_MODEL), lambda i, j: (i, 0))
    act = pl.BlockSpec((None, tm, FF_SHARD), lambda i, j: (j, i, 0))
    w_arrays, weights = _ffn_weight_operands(ffn_w, [lambda i, j: j] * 3)
    return _call(
        body, (x, g, *w_arrays), name=name, grid=(t // tm, N_CHIPS),
        in_specs=[tok, pl.BlockSpec((1, D_MODEL), lambda i, j: (0, 0))] + weights,
        out_specs=[tok, act, act, tok],
        out_shape=[jax.ShapeDtypeStruct((t, D_MODEL), F32),
                   jax.ShapeDtypeStruct((N_CHIPS, t, FF_SHARD), BF16),
                   jax.ShapeDtypeStruct((N_CHIPS, t, FF_SHARD), BF16),
                   jax.ShapeDtypeStruct((t, D_MODEL), BF16)],
        scratch_shapes=[pltpu.VMEM((tm, D_MODEL), F32)],
        compiler_params=_ARB2, hosted=hosted)


def _ffn1_fwd_gathering(x, g, shard, name, hosted=None):
    t = x.shape[0]
    tm = min(t, TOKEN_TILE)
    nt = t // tm
    nh = 0 if hosted is None else len(hosted.arrays)
    peers = "neighbours" if hosted is None or hosted.peers == "neighbours" else "chips_sibling"
    assert hosted is None or hosted.peers in ("neighbours", "chips_sibling")

    def body(*refs):
        x_ref, g_ref, shard_ref = refs[0:3]
        h_in = refs[3:3 + nh]
        xo_ref, a_ref, u_ref, h_ref, wall = refs[3 + nh:8 + nh]
        h_out = refs[8 + nh:8 + 2 * nh]
        acc, h_all, wbuf, load_sems, send_sems, recv_sems = refs[8 + 2 * nh:14 + 2 * nh]
        carried_sems = refs[14 + 2 * nh:]
        k, i = pl.program_id(0), pl.program_id(1)
        legs, _ = _gather_legs(shard_ref, wall, send_sems, recv_sems, 0, True)
        begin, pass_on, _ = _gather_steps(legs, True)

        def load(chunk, src):
            return pltpu.make_async_copy(src, wbuf.at[chunk % 2], load_sems.at[chunk % 2])

        @pl.when((k == 0) & (i == 0))
        def _():
            _handshake(peers)
            begin()
            load(0, shard_ref).start()
            load(0, shard_ref).wait()

        @pl.when((k == 1) & (i == 0))
        def _():
            pass_on()
            if hosted is not None:
                hosted.start(h_in, h_out, *carried_sems)
            legs["pass_y"][1].wait_recv()
            load(1, wall.at[PEER_SLOT[1]]).start()
            load(1, wall.at[PEER_SLOT[1]]).wait()

        @pl.when((k == 1) & (i == nt // 2))
        def _():
            legs["pass_x"][1].wait_recv()
            load(2, wall.at[PEER_SLOT[0]]).start()

        @pl.when((k == 2) & (i == 0))
        def _():
            load(2, wall.at[PEER_SLOT[0]]).wait()

        @pl.when((k == 2) & (i == nt // 2))
        def _():
            legs["fwd_y"][1].wait_recv()
            legs["pass_d0"][0].start()
            legs["fwd_x"][1].wait_recv()
            legs["pass_d1"][0].start()
            legs["pass_d0"][1].wait_recv()
            legs["pass_d1"][1].wait_recv()
            load(3, wall.at[PEER_SLOT[2]]).start()
            if hosted is not None:
                hosted.middle(h_in, h_out, *carried_sems)

        @pl.when((k == 3) & (i == 0))
        def _():
            load(3, wall.at[PEER_SLOT[2]]).wait()

        @pl.when(k == 0)
        def _():
            xv = x_ref[...]
            h0 = ((xv * _rms_scale(xv)) * g_ref[...]).astype(BF16)
            h_all[i] = h0
            h_ref[...] = h0

        h = h_all[i]
        wg, wu, wd = (wbuf[k % 2, kind].reshape(FF_SHARD, D_MODEL) for kind in range(3))
        a = _dot_nt(h, wg)
        u = _dot_nt(h, wu)
        a_ref[...] = a.astype(BF16)
        u_ref[...] = u.astype(BF16)
        part = _dot(((a * jax.nn.sigmoid(a)) * u).astype(BF16), wd)

        @pl.when(k == 0)
        def _():
            acc[i] = part

        @pl.when(k > 0)
        def _():
            acc[i] += part

        @pl.when(k == N_CHIPS - 1)
        def _():
            xo_ref[...] = x_ref[...] + 0.5 * acc[i]

        @pl.when((k == N_CHIPS - 1) & (i == nt - 1))
        def _():
            for pair in legs.values():
                pair[0].wait_send()
            if hosted is not None:
                hosted.finish(h_in, h_out, *carried_sems)

    def first_or_last(k):
        return (k == 0) | (k == N_CHIPS - 1)

    tok = lambda keep: pl.BlockSpec((tm, D_MODEL), lambda k, i: (jnp.where(keep(k), i, 0), 0))
    act = pl.BlockSpec((None, tm, FF_SHARD), lambda k, i: (k, i, 0))
    act_shape = jax.ShapeDtypeStruct((N_CHIPS, t, FF_SHARD), BF16)
    carried = [] if hosted is None else [pltpu.SemaphoreType.DMA((hosted.n_sems,))] * 2
    outs = pl.pallas_call(
        body, name=name, grid=(N_CHIPS, nt),
        in_specs=[tok(first_or_last), pl.BlockSpec((1, D_MODEL), lambda k, i: (0, 0)), ANY] + [ANY] * nh,
        out_specs=[tok(lambda k: k == N_CHIPS - 1), act, act,
                   pl.BlockSpec((tm, D_MODEL), lambda k, i: (jnp.where(k == 0, i, nt - 1), 0)), ANY] + [ANY] * nh,
        out_shape=[jax.ShapeDtypeStruct((t, D_MODEL), F32), act_shape, act_shape,
                   jax.ShapeDtypeStruct((t, D_MODEL), BF16),
                   jax.ShapeDtypeStruct((N_CHIPS,) + shard.shape, shard.dtype)]
                  + ([] if hosted is None else hosted.out_shapes),
        scratch_shapes=[pltpu.VMEM((nt, tm, D_MODEL), F32), pltpu.VMEM((nt, tm, D_MODEL), BF16),
                        pltpu.VMEM((2,) + shard.shape, shard.dtype), pltpu.SemaphoreType.DMA((2,)),
                        pltpu.SemaphoreType.DMA((8,)), pltpu.SemaphoreType.DMA((8,))] + carried,
        compiler_params=_with_barrier(_ARB2, peers),
    )(x, g, shard, *([] if hosted is None else hosted.arrays))
    return list(outs[:5]), list(outs[5:])


def _ffn_bwd(dxo, x, g, a4, u4, ffn_w, name, hosted=None):
    t = x.shape[0]
    tm = min(t, TOKEN_TILE)
    steps = (t // tm) * N_CHIPS
    cur, prev = _pipeline_items(steps)


    def body(dxo_ref, dxo_prev_ref, x_ref, g_ref, a_ref, u_ref, wg_ref, wu_ref, wd_ref,
             da_ref, du_ref, hid_ref, dob_ref, dx_ref, dg_ref, acc_ref, da_slots, du_slots):
        s = pl.program_id(0)
        jc, jp = cur(s)[1], prev(s)[1]
        slot = s % 2

        @pl.when(s == 0)
        def _():
            dg_ref[...] = jnp.zeros_like(dg_ref)
            acc_ref[...] = jnp.zeros_like(acc_ref)
            da_slots[...] = jnp.zeros_like(da_slots)
            du_slots[...] = jnp.zeros_like(du_slots)

        @pl.when(jc == 0)
        def _():
            dob_ref[...] = (0.5 * dxo_ref[...]).astype(BF16)

        dhid = _dot_nt(dob_ref[...], wd_ref[...])
        a = a_ref[...].astype(F32)
        u = u_ref[...].astype(F32)
        sg = jax.nn.sigmoid(a)
        sl = a * sg
        hid_ref[...] = (sl * u).astype(BF16)
        du = (dhid * sl).astype(BF16)
        da = (dhid * u * _silu_grad(a, sg)).astype(BF16)
        du_ref[...] = du
        da_ref[...] = da
        acc_ref[...] += _dot(da_slots[1 - slot], wg_ref[...]) + _dot(du_slots[1 - slot], wu_ref[...])
        da_slots[slot] = da
        du_slots[slot] = du

        @pl.when((jp == N_CHIPS - 1) & (s > 0))
        def _():
            dx, dg = _rms_bwd(acc_ref[...], x_ref[...], g_ref[...])
            dx_ref[...] = dxo_prev_ref[...] + dx
            dg_ref[...] += dg
            acc_ref[...] = jnp.zeros_like(acc_ref)

    tok_cur = pl.BlockSpec((tm, D_MODEL), lambda s: (cur(s)[0], 0))
    tok_prev = pl.BlockSpec((tm, D_MODEL), lambda s: (prev(s)[0], 0))
    act = pl.BlockSpec((None, tm, FF_SHARD), lambda s: (cur(s)[1], cur(s)[0], 0))
    row = pl.BlockSpec((1, D_MODEL), lambda s: (0, 0))
    w_arrays, weights = _ffn_weight_operands(ffn_w, [lambda s: prev(s)[1], lambda s: prev(s)[1], lambda s: cur(s)[1]])
    act_shape = jax.ShapeDtypeStruct((N_CHIPS, t, FF_SHARD), BF16)
    return _call(
        body, (dxo, dxo, x, g, a4, u4, *w_arrays), name=name, grid=(steps + 1,),
        in_specs=[tok_cur, tok_prev, tok_prev, row, act, act] + weights,
        out_specs=[act, act, act, tok_cur, tok_prev, row],
        out_shape=[act_shape, act_shape, act_shape,
                   jax.ShapeDtypeStruct((t, D_MODEL), BF16),
                   jax.ShapeDtypeStruct((t, D_MODEL), F32),
                   jax.ShapeDtypeStruct((1, D_MODEL), F32)],
        scratch_shapes=[pltpu.VMEM((tm, D_MODEL), F32), pltpu.VMEM((2, tm, FF_SHARD), BF16),
                        pltpu.VMEM((2, tm, FF_SHARD), BF16)],
        compiler_params=_ARB1, hosted=hosted)


def _matmul_tn(a, b, name, tka=None, out_dtype=F32, hosted=None):
    a3, b3 = a.ndim == 3, b.ndim == 3
    nb = a.shape[0] if a3 else (b.shape[0] if b3 else 1)
    t, ka, n = a.shape[-2], a.shape[-1], b.shape[-1]
    tka = ka if tka is None else tka
    tk = min(t, 4 * TOKEN_TILE)
    nk = t // tk

    def body(a_ref, b_ref, o_ref, acc_ref):
        k = pl.program_id(2)

        @pl.when(k == 0)
        def _():
            acc_ref[...] = jnp.zeros_like(acc_ref)

        acc_ref[...] += _dot_tn(a_ref[...].astype(BF16), b_ref[...].astype(BF16))

        @pl.when(k == nk - 1)
        def _():
            o_ref[...] = acc_ref[...].astype(out_dtype)

    a_spec = (pl.BlockSpec((None, tk, tka), lambda i, j, k: (i, k, j)) if a3
              else pl.BlockSpec((tk, tka), lambda i, j, k: (k, j)))
    b_spec = (pl.BlockSpec((None, tk, n), lambda i, j, k: (i, k, 0)) if b3
              else pl.BlockSpec((tk, n), lambda i, j, k: (k, 0)))
    outs, carried = _call(
        body, (a, b), name=name, grid=(nb, ka // tka, t // tk),
        in_specs=[a_spec, b_spec],
        out_specs=[pl.BlockSpec((None, tka, n), lambda i, j, k: (i, j, 0))],
        out_shape=[jax.ShapeDtypeStruct((nb, ka, n), out_dtype)],
        scratch_shapes=[pltpu.VMEM((tka, n), F32)],
        compiler_params=_ARB3, hosted=hosted)
    return outs[0] if hosted is None else (outs[0], carried)


def _matmul_nt(a, w, name, out_dtype=F32):
    t, k = a.shape
    n = w.shape[0]
    tm = min(t, TOKEN_TILE)

    def body(a_ref, w_ref, o_ref):
        o_ref[...] = _dot_nt(a_ref[...].astype(BF16), w_ref[...]).astype(out_dtype)

    return pl.pallas_call(
        body, name=name, grid=(t // tm,),
        in_specs=[pl.BlockSpec((tm, k), lambda i: (i, 0)), pl.BlockSpec((n, k), lambda i: (0, 0))],
        out_specs=pl.BlockSpec((tm, n), lambda i: (i, 0)),
        out_shape=jax.ShapeDtypeStruct((t, n), out_dtype),
        compiler_params=_ARB1,
    )(a, w)


def _mixer_in_bwd(dproj, w_in_pt, dres, x, g, name):
    t, k = dproj.shape
    tm = min(t, TOKEN_TILE)

    def body(a_ref, w_ref, dres_ref, x_ref, g_ref, dx_ref, dg_ref):
        @pl.when(pl.program_id(0) == 0)
        def _():
            dg_ref[...] = jnp.zeros_like(dg_ref)

        dh = _dot(a_ref[...], w_ref[...])
        dx, dg = _rms_bwd(dh, x_ref[...], g_ref[...])
        dx_ref[...] = dres_ref[...] + dx
        dg_ref[...] += dg

    tok = pl.BlockSpec((tm, D_MODEL), lambda i: (i, 0))
    row = pl.BlockSpec((1, D_MODEL), lambda i: (0, 0))
    return pl.pallas_call(
        body, name=name, grid=(t // tm,),
        in_specs=[pl.BlockSpec((tm, k), lambda i: (i, 0)), pl.BlockSpec((k, D_MODEL), lambda i: (0, 0)), tok, tok, row],
        out_specs=[tok, row],
        out_shape=[jax.ShapeDtypeStruct((t, D_MODEL), F32), jax.ShapeDtypeStruct((1, D_MODEL), F32)],
        compiler_params=_ARB1,
    )(dproj, w_in_pt, dres, x, g)


def _mixer_in_fwd(x, g, w_in_pt, name, hosted=None):
    t = x.shape[0]
    tm = min(t, TOKEN_TILE)
    tn = PROJ_P // PROJ_TILES

    def body(x_ref, g_ref, w_ref, p_ref, h_ref):
        @pl.when(pl.program_id(1) == 0)
        def _():
            xv = x_ref[...]
            h_ref[...] = ((xv * _rms_scale(xv)) * g_ref[...]).astype(BF16)

        p_ref[...] = _dot_nt(h_ref[...], w_ref[...])

    tok = pl.BlockSpec((tm, D_MODEL), lambda i, j: (i, 0))
    return _call(
        body, (x, g, w_in_pt), name=name, grid=(t // tm, PROJ_TILES),
        in_specs=[tok, pl.BlockSpec((1, D_MODEL), lambda i, j: (0, 0)),
                  pl.BlockSpec((tn, D_MODEL), lambda i, j: (j, 0))],
        out_specs=[pl.BlockSpec((tm, tn), lambda i, j: (i, j)), tok],
        out_shape=[jax.ShapeDtypeStruct((t, PROJ_P), F32), jax.ShapeDtypeStruct((t, D_MODEL), BF16)],
        scratch_shapes=[], compiler_params=_ARB2, hosted=hosted)


def _mixer_out_fwd(o_ret, o_gla, w_out, x, name):
    t = x.shape[0]
    tm = min(t, TOKEN_TILE)
    half = HEADS * LANES

    def body(a_ref, b_ref, w_ref, x_ref, o_ref):
        o_ref[...] = x_ref[...] + _dot(a_ref[...], w_ref[0:half, :]) + _dot(b_ref[...], w_ref[half:2 * half, :])

    tok = pl.BlockSpec((tm, D_MODEL), lambda i: (i, 0))
    hb = pl.BlockSpec((tm, half), lambda i: (i, 0))
    return pl.pallas_call(
        body, name=name, grid=(t // tm,),
        in_specs=[hb, hb, pl.BlockSpec((2 * half, D_MODEL), lambda i: (0, 0)), tok],
        out_specs=tok, out_shape=jax.ShapeDtypeStruct((t, D_MODEL), F32),
        compiler_params=_ARB1,
    )(o_ret, o_gla, w_out, x)


def _final_loss(x, g, target, name):
    t = x.shape[0]
    tm = min(t, TOKEN_TILE)

    def body(x_ref, g_ref, t_ref, l_ref, dx_ref, dg_ref):
        @pl.when(pl.program_id(0) == 0)
        def _():
            l_ref[...] = jnp.zeros_like(l_ref)
            dg_ref[...] = jnp.zeros_like(dg_ref)

        xv = x_ref[...]
        gv = g_ref[...]
        err = (xv * _rms_scale(xv)) * gv - t_ref[...]
        l_ref[...] += 0.5 * jnp.sum(jnp.mean(err * err, axis=-1, keepdims=True), axis=0, keepdims=True)
        dx, dg = _rms_bwd(err * (1.0 / D_MODEL), xv, gv)
        dx_ref[...] = dx
        dg_ref[...] += dg

    tok = pl.BlockSpec((tm, D_MODEL), lambda i: (i, 0))
    row = pl.BlockSpec((1, D_MODEL), lambda i: (0, 0))
    return pl.pallas_call(
        body, name=name, grid=(t // tm,),
        in_specs=[tok, row, tok],
        out_specs=[pl.BlockSpec((8, LANES), lambda i: (0, 0)), tok, row],
        out_shape=[jax.ShapeDtypeStruct((8, LANES), F32), jax.ShapeDtypeStruct((t, D_MODEL), F32),
                   jax.ShapeDtypeStruct((1, D_MODEL), F32)],
        compiler_params=_ARB1,
    )(x, g, target)


def _rot(v, cos, sa, sb):
    return v * cos + pltpu.roll(v, 96, 1) * sa + pltpu.roll(v, 32, 1) * sb


def _rot_t(d, cos, sa, sb):
    return d * cos + pltpu.roll(d * sa, 32, 1) + pltpu.roll(d * sb, 96, 1)


def _bmm(a, b):
    return jnp.einsum("cik,ckj->cij", a, b, preferred_element_type=F32)


def _bmm_nt(a, b):
    return jnp.einsum("cik,cjk->cij", a, b, preferred_element_type=F32)


def _bmm_tn(a, b):
    return jnp.einsum("cki,ckj->cij", a, b, preferred_element_type=F32)


def _masked_sum(mask, x):
    hi = x.astype(BF16)
    r1 = x - hi.astype(F32)
    mid = r1.astype(BF16)
    lo = (r1 - mid.astype(F32)).astype(BF16)
    return _bmm(mask, hi) + _bmm(mask, mid) + _bmm(mask, lo)


PAIR = 2


def _tile_inputs(is_ret, qkvg_refs, aux, nc):
    shape3 = (nc, CHUNK, LANES)
    q_ref, k_ref, v_ref, g_ref = qkvg_refs
    low_lanes = lax.broadcasted_iota(jnp.int32, (1, LANES), 1) < 64
    ri = lax.broadcasted_iota(jnp.int32, (PAIR * nc, CHUNK, CHUNK), 1)
    ci = lax.broadcasted_iota(jnp.int32, (PAIR * nc, CHUNK, CHUNK), 2)
    qs, ks, vs, bs, gates, extra = [], [], [], [], [], []
    for hd in range(PAIR):
        q_blk, k_blk = q_ref[...], k_ref[...]
        if hd == 1:
            q_blk, k_blk = pltpu.roll(q_blk, 64, 1), pltpu.roll(k_blk, 64, 1)
        q_raw, k_raw = jnp.where(low_lanes, q_blk, 0.0), jnp.where(low_lanes, k_blk, 0.0)
        vs.append(v_ref[:, LANES * hd:LANES * (hd + 1)].reshape(shape3))
        gates.append(g_ref[:, LANES * hd:LANES * (hd + 1)])
        if is_ret:
            cos_ref, sa_ref, sb_ref, lg_ref = aux
            cos, sa, sb = cos_ref[...], sa_ref[...], sb_ref[...]
            q = _rot(q_raw, cos, sa, sb)
            k = _rot(k_raw, cos, sa, sb) * QK_SCALE
            steps = (lax.broadcasted_iota(jnp.int32, shape3, 1) + 1).astype(F32)
            bs.append(steps * lg_ref[hd])
            extra.append(jnp.exp(jnp.abs(ri[0:nc] - ci[0:nc]).astype(F32) * lg_ref[hd][:, 0:CHUNK]))
        else:
            glow_ref, wa2_ref, ba_ref = aux
            lanes = slice(LANES * hd, LANES * (hd + 1))
            logit = _dot(glow_ref[...].astype(BF16), wa2_ref[:, lanes]) + ba_ref[:, lanes]
            la = (jnp.minimum(logit, 0.0) - jnp.log1p(jnp.exp(-jnp.abs(logit)))) * (1.0 / GATE_NORM)
            bs.append(_masked_sum((ci[0:nc] <= ri[0:nc]).astype(BF16), la.reshape(shape3)))
            extra.append(logit)
            q = q_raw * QK_SCALE
            k = k_raw
        qs.append(q.reshape(shape3))
        ks.append(k.reshape(shape3))
    cat = lambda parts: jnp.concatenate(parts, axis=0)
    return cat(qs), cat(ks), cat(vs), gates, cat(bs), extra, ri, ci


def _tile_scores(q, k, b, ri, ci):
    mid = b[:, CHUNK // 2 - 1:CHUNK // 2, :]
    ep = jnp.exp(b - mid)
    en = jnp.exp(mid - b)
    qt, kt, qh, kh = q * ep, k * en, q * en, k * ep
    low = _bmm_nt(qt.astype(BF16), kt.astype(BF16))
    upp = _bmm_nt(qh.astype(BF16), kh.astype(BF16))
    scores = jnp.where(ci <= ri, low, upp)
    return scores, ep, en, qt, kt, qh, kh


def _attn_specs(is_ret, t, tb, imap_t):
    nb = t // tb
    base = 0 if is_ret else 12
    wide = PAIR * LANES
    proj = [pl.BlockSpec((tb, LANES), lambda p, i: (imap_t(i), base + p)),
            pl.BlockSpec((tb, LANES), lambda p, i: (imap_t(i), base + 2 + p)),
            pl.BlockSpec((tb, wide), lambda p, i: (imap_t(i), (base + 4) // 2 + p)),
            pl.BlockSpec((tb, wide), lambda p, i: (imap_t(i), (base + 8) // 2 + p))]
    lane_t = pl.BlockSpec((tb, LANES), lambda p, i: (imap_t(i), 0))
    if is_ret:
        aux = [lane_t, lane_t, lane_t, pl.BlockSpec((PAIR, 1, LANES), lambda p, i: (p, 0, 0))]
    else:
        aux = [pl.BlockSpec((tb, LANES), lambda p, i: (imap_t(i), PROJ_P // LANES - 1)),
               pl.BlockSpec((LANES, wide), lambda p, i: (0, p)),
               pl.BlockSpec((1, wide), lambda p, i: (0, p))]
    gain = pl.BlockSpec((1, wide), lambda p, i: (0, p))
    pair_t = pl.BlockSpec((tb, wide), lambda p, i: (imap_t(i), p))
    narrow_t = pl.BlockSpec((tb, LANES), lambda p, i: (imap_t(i), p))
    state = pl.BlockSpec((PAIR, tb // CHUNK, LANES, LANES), lambda p, i: (p, imap_t(i), 0, 0))
    return nb, proj, aux, gain, pair_t, narrow_t, state


def _attn_fwd(is_ret, proj, aux_arrays, gain, name, hosted=None):
    t = proj.shape[0]
    tb = min(t, ATTN_TILE)
    nc = tb // CHUNK
    n_aux = 4 if is_ret else 3
    nb, proj_spec, aux_specs, gain_spec, pair_t, _, state_spec = _attn_specs(is_ret, t, tb, lambda i: i)

    def body(*refs):
        qkvg_refs = refs[0:4]
        aux = refs[4:4 + n_aux]
        gn_ref, ofin_ref, oraw_ref, st_ref, state = refs[4 + n_aux:]

        @pl.when(pl.program_id(1) == 0)
        def _():
            state[...] = jnp.zeros_like(state)

        q, k, v, gates, b, extra, ri, ci = _tile_inputs(is_ret, qkvg_refs, aux, nc)
        if is_ret:
            scores = _bmm_nt(q.astype(BF16), k.astype(BF16)) * jnp.concatenate(extra, axis=0)
        else:
            scores = _tile_scores(q, k, b, ri, ci)[0]
        vb = v.astype(BF16)
        intra = _bmm(scores.astype(BF16), vb)
        b_last = b[:, CHUNK - 1:CHUNK, :]
        e_last = jnp.exp(b_last)
        grow = _bmm_tn(vb, (k * jnp.exp(b_last - b)).astype(BF16))
        for hd in range(PAIR):
            st = state[hd]
            for c in range(nc):
                st_ref[hd, c] = st
                st = st * e_last[hd * nc + c] + grow[hd * nc + c]
            state[hd] = st
        starts = st_ref[...].reshape(PAIR * nc, LANES, LANES)
        out3 = intra + _bmm_nt((q * jnp.exp(b)).astype(BF16), starts.astype(BF16))
        for hd in range(PAIR):
            lanes = slice(LANES * hd, LANES * (hd + 1))
            out = out3[hd * nc:(hd + 1) * nc].reshape(tb, LANES)
            oraw_ref[:, lanes] = out
            normed = out * _rms_scale(out)
            gate = gates[hd]
            ofin_ref[:, lanes] = ((normed * gn_ref[:, lanes]) * (gate * jax.nn.sigmoid(gate))).astype(BF16)

    width = HEADS * LANES
    return _call(
        body, (proj, proj, proj, proj, *aux_arrays, gain), name=name, grid=(HEADS // PAIR, nb),
        in_specs=proj_spec + aux_specs + [gain_spec],
        out_specs=[pair_t, pair_t, state_spec],
        out_shape=[jax.ShapeDtypeStruct((t, width), BF16), jax.ShapeDtypeStruct((t, width), F32),
                   jax.ShapeDtypeStruct((HEADS, t // CHUNK, LANES, LANES), F32)],
        scratch_shapes=[pltpu.VMEM((PAIR, LANES, LANES), F32)],
        compiler_params=_ARB2, hosted=hosted)


def _attn_bwd(is_ret, proj, aux_arrays, gain, o_raw, states, d_out, name, hosted=None):
    t = proj.shape[0]
    tb = min(t, ATTN_TILE)
    nc = tb // CHUNK
    n_aux = 4 if is_ret else 3
    nblk = t // tb
    nb, proj_spec, aux_specs, gain_spec, pair_t, narrow_t, state_spec = _attn_specs(
        is_ret, t, tb, lambda i: nblk - 1 - i)
    base = 0 if is_ret else HEADS // PAIR
    dout_spec = pl.BlockSpec((tb, PAIR * LANES), lambda p, i: (nblk - 1 - i, base + p))

    def body(*refs):
        qkvg_refs = refs[0:4]
        aux = refs[4:4 + n_aux]
        gn_ref, oraw_ref, st_ref, dfin_ref = refs[4 + n_aux:8 + n_aux]
        dq_ref, dk_ref, dv_ref, dgate_ref, dgn_ref = refs[8 + n_aux:13 + n_aux]
        if is_ret:
            dstate, dafter_ref = refs[13 + n_aux:]
        else:
            dlogit_ref, dba_ref, dstate, dafter_ref = refs[13 + n_aux:]

        @pl.when(pl.program_id(1) == 0)
        def _():
            dstate[...] = jnp.zeros_like(dstate)
            dgn_ref[...] = jnp.zeros_like(dgn_ref)
            if not is_ret:
                dba_ref[...] = jnp.zeros_like(dba_ref)

        shape3 = (nc, CHUNK, LANES)
        q, k, v, gates, b, extra, ri, ci = _tile_inputs(is_ret, qkvg_refs, aux, nc)
        eb = jnp.exp(b)
        qe = q * eb
        b_last = b[:, CHUNK - 1:CHUNK, :]
        e_last = jnp.exp(b_last)
        ekd = jnp.exp(b_last - b)
        kd = k * ekd

        d_os = []
        for hd in range(PAIR):
            lanes = slice(LANES * hd, LANES * (hd + 1))
            gn, gate = gn_ref[:, lanes], gates[hd]
            out = oraw_ref[:, lanes]
            r = _rms_scale(out)
            normed = out * r
            sg = jax.nn.sigmoid(gate)
            dfin = dfin_ref[:, lanes]
            dgate_ref[:, lanes] = (dfin * (normed * gn) * _silu_grad(gate, sg)).astype(BF16)
            dpre = dfin * (gate * sg)
            dgn_ref[:, lanes] += jnp.sum(dpre * normed, axis=0, keepdims=True)
            dnormed = dpre * gn
            d_o = r * (dnormed - normed * jnp.mean(dnormed * normed, axis=-1, keepdims=True))
            d_os.append(d_o.reshape(shape3))
        dob, vb = jnp.concatenate(d_os, axis=0).astype(BF16), v.astype(BF16)

        dgrow = _bmm_tn(dob, qe.astype(BF16))
        for hd in range(PAIR):
            dst = dstate[hd]
            for c in reversed(range(nc)):
                dafter_ref[hd * nc + c] = dst
                dst = dst * e_last[hd * nc + c] + dgrow[hd * nc + c]
            dstate[hd] = dst
        st = st_ref[...].reshape(PAIR * nc, LANES, LANES)
        dafter = dafter_ref[...]
        stb, dafter_b = st.astype(BF16), dafter.astype(BF16)

        dsc = _bmm_nt(dob, vb)
        dsc_t = _bmm_nt(vb, dob)
        dqe = _bmm(dob, stb)
        dkd = _bmm(vb, dafter_b)
        if is_ret:
            decay, qb, kb = jnp.concatenate(extra, axis=0), q.astype(BF16), k.astype(BF16)
            scores_t = _bmm_nt(kb, qb) * decay
            dq = _bmm((dsc * decay).astype(BF16), kb) + dqe * eb
            dk = _bmm((dsc_t * decay).astype(BF16), qb) + dkd * ekd
        else:
            _, ep, en, qt, kt, qh, kh = _tile_scores(q, k, b, ri, ci)
            qtb, ktb, qhb, khb = qt.astype(BF16), kt.astype(BF16), qh.astype(BF16), kh.astype(BF16)
            scores_t = jnp.where(ci >= ri, _bmm_nt(ktb, qtb), _bmm_nt(khb, qhb))
            dqt = _bmm(jnp.where(ci <= ri, dsc, 0.0).astype(BF16), ktb)
            dqh = _bmm(jnp.where(ci <= ri, 0.0, dsc).astype(BF16), khb)
            dkt = _bmm(jnp.where(ci >= ri, dsc_t, 0.0).astype(BF16), qtb)
            dkh = _bmm(jnp.where(ci >= ri, 0.0, dsc_t).astype(BF16), qhb)
            dq = dqt * ep + dqh * en + dqe * eb
            dk = dkt * en + dkh * ep + dkd * ekd
        dv = _bmm(scores_t.astype(BF16), dob) + _bmm_nt(kd.astype(BF16), dafter_b)

        if not is_ret:
            db = dqt * qt - dkt * kt - dqh * qh + dkh * kh + dqe * qe - dkd * kd
            db_last = (jnp.sum(dkd * kd, axis=1, keepdims=True)
                       + jnp.sum(dafter * st, axis=1, keepdims=True) * e_last)
            last_row = lax.broadcasted_iota(jnp.int32, (PAIR * nc, CHUNK, LANES), 1) == CHUNK - 1
            db = db + jnp.where(last_row, db_last, 0.0)
            dla = _masked_sum((ci >= ri).astype(BF16), db)

        dq_pair, dk_pair = [], []
        for hd in range(PAIR):
            lanes = slice(LANES * hd, LANES * (hd + 1))
            rows3 = slice(hd * nc, (hd + 1) * nc)
            dq_h, dk_h = dq[rows3].reshape(tb, LANES), dk[rows3].reshape(tb, LANES)
            if is_ret:
                cos_ref, sa_ref, sb_ref, _ = aux
                cos, sa, sb = cos_ref[...], sa_ref[...], sb_ref[...]
                dq_h = _rot_t(dq_h, cos, sa, sb)
                dk_h = _rot_t(dk_h, cos, sa, sb) * QK_SCALE
            else:
                dq_h = dq_h * QK_SCALE
                dlogit = dla[rows3].reshape(tb, LANES) * (1.0 / GATE_NORM) * jax.nn.sigmoid(-extra[hd])
                dlogit_ref[:, lanes] = dlogit.astype(BF16)
                dba_ref[:, lanes] += jnp.sum(dlogit, axis=0, keepdims=True)
            dq_pair.append(dq_h)
            dk_pair.append(dk_h)
            dv_ref[:, lanes] = dv[rows3].reshape(tb, LANES).astype(BF16)
        dq_ref[...] = (dq_pair[0] + pltpu.roll(dq_pair[1], 64, 1)).astype(BF16)
        dk_ref[...] = (dk_pair[0] + pltpu.roll(dk_pair[1], 64, 1)).astype(BF16)

    width = HEADS * LANES
    row_out = pl.BlockSpec((1, PAIR * LANES), lambda p, i: (0, p))
    out_specs = [narrow_t, narrow_t, pair_t, pair_t, row_out]
    out_shape = ([jax.ShapeDtypeStruct((t, width // 2), BF16)] * 2 + [jax.ShapeDtypeStruct((t, width), BF16)] * 2
                 + [jax.ShapeDtypeStruct((1, width), F32)])
    if not is_ret:
        out_specs += [pair_t, row_out]
        out_shape += [jax.ShapeDtypeStruct((t, width), BF16), jax.ShapeDtypeStruct((1, width), F32)]
    return _call(
        body, (proj, proj, proj, proj, *aux_arrays, gain, o_raw, states, d_out), name=name,
        grid=(HEADS // PAIR, nblk),
        in_specs=proj_spec + aux_specs + [gain_spec, pair_t, state_spec, dout_spec],
        out_specs=out_specs, out_shape=out_shape,
        scratch_shapes=[pltpu.VMEM((PAIR, LANES, LANES), F32), pltpu.VMEM((PAIR * nc, LANES, LANES), F32)],
        compiler_params=_ARB2, hosted=hosted)


PEER_SLOT = (2, 1, 3)


def _place():
    x, y, c = lax.axis_index("x"), lax.axis_index("y"), lax.axis_index("c")
    chips = [(1 - x, y), (x, 1 - y), (1 - x, 1 - y)]
    return x, y, c, 2 * x + y, chips


def _route_split(rows, dtype):
    tile = 16 if dtype == BF16 else 8
    if rows < 2 * tile:
        return None
    return -(-(rows // 2) // tile) * tile


def _routes(by_peer):
    x, y, c, me, chips = _place()
    (xx, xy), (yx, yy), (dx, dy) = chips
    if by_peer:
        slots = dict(own=0, from_x=PEER_SLOT[0], from_y=PEER_SLOT[1], diag=PEER_SLOT[2],
                     mine_on_x=PEER_SLOT[0], mine_on_y=PEER_SLOT[1])
    else:
        slots = dict(own=me, from_x=2 * xx + xy, from_y=2 * yx + yy, diag=2 * dx + dy, mine_on_x=me, mine_on_y=me)
    return c, (xx, xy, c), (yx, yy, c), (dx, dy, c), (x, y, 1 - c), slots


def _gather_legs(src, out, send_sems, recv_sems, base, by_peer):
    c, to_x, to_y, to_d, sibling, s = _routes(by_peer)
    r0 = _route_split(src.shape[2], src.dtype)

    def cp(k, src_ref, dst_ref, to):
        return pltpu.make_async_remote_copy(src_ref=src_ref, dst_ref=dst_ref, send_sem=send_sems.at[base + k],
                                            recv_sem=recv_sems.at[base + k], device_id=to, device_id_type=MESH)

    mine = src.at[:, c]
    legs = dict(
        x=(cp(0, mine, out.at[s["mine_on_x"], :, c], to_x), cp(0, mine, out.at[s["from_x"], :, c], to_x)),
        y=(cp(1, mine, out.at[s["mine_on_y"], :, c], to_y), cp(1, mine, out.at[s["from_y"], :, c], to_y)),
        pass_x=(cp(4, out.at[s["from_x"], :, c], out.at[s["from_x"], :, c], sibling),
                cp(4, mine, out.at[s["from_x"], :, 1 - c], sibling)),
        pass_y=(cp(5, out.at[s["from_y"], :, c], out.at[s["from_y"], :, c], sibling),
                cp(5, mine, out.at[s["from_y"], :, 1 - c], sibling)))
    if r0 is None:
        mine_on_d = s["diag"] if by_peer else s["own"]
        legs["d"] = (cp(2, mine, out.at[mine_on_d, :, c], to_d), cp(2, mine, out.at[s["diag"], :, c], to_d))
        legs["pass_d"] = (cp(6, out.at[s["diag"], :, c], out.at[s["diag"], :, c], sibling),
                          cp(6, mine, out.at[s["diag"], :, 1 - c], sibling))
        return legs, False
    lo, hi = pl.ds(0, r0), pl.ds(r0, src.shape[2] - r0)
    fx_on_y = s["diag"] if by_peer else s["from_x"]
    fy_on_x = s["diag"] if by_peer else s["from_y"]
    legs.update(
        fwd_y=(cp(2, out.at[s["from_x"], :, c, lo], out.at[fx_on_y, :, c, lo], to_y),
               cp(2, mine.at[:, lo], out.at[s["diag"], :, c, lo], to_y)),
        fwd_x=(cp(3, out.at[s["from_y"], :, c, hi], out.at[fy_on_x, :, c, hi], to_x),
               cp(3, mine.at[:, hi], out.at[s["diag"], :, c, hi], to_x)),
        pass_d0=(cp(6, out.at[s["diag"], :, c, lo], out.at[s["diag"], :, c, lo], sibling),
                 cp(6, mine.at[:, lo], out.at[s["diag"], :, 1 - c, lo], sibling)),
        pass_d1=(cp(7, out.at[s["diag"], :, c, hi], out.at[s["diag"], :, c, hi], sibling),
                 cp(7, mine.at[:, hi], out.at[s["diag"], :, 1 - c, hi], sibling)))
    return legs, True


def _gather_steps(legs, routed):
    def start():
        legs["x"][0].start()
        legs["y"][0].start()
        if not routed:
            legs["d"][0].start()

    def middle():
        legs["x"][1].wait_recv()
        if routed:
            legs["fwd_y"][0].start()
        legs["pass_x"][0].start()
        legs["y"][1].wait_recv()
        if routed:
            legs["fwd_x"][0].start()
        legs["pass_y"][0].start()

    def finish():
        last = ["pass_d0", "pass_d1"] if routed else ["pass_d"]
        if routed:
            legs["fwd_y"][1].wait_recv()
            legs["pass_d0"][0].start()
            legs["fwd_x"][1].wait_recv()
            legs["pass_d1"][0].start()
        else:
            legs["d"][1].wait_recv()
            legs["pass_d"][0].start()
        for name in ["pass_x", "pass_y"] + last:
            legs[name][1].wait_recv()
        for name in ["x", "y", "pass_x", "pass_y"] + last + (["fwd_y", "fwd_x"] if routed else ["d"]):
            legs[name][0].wait_send()

    return start, middle, finish


def _gather_plan(arrs):
    na = len(arrs)

    def steps(ins, outs, send_sems, recv_sems):
        return [_gather_steps(*_gather_legs(ins[a], outs[a], send_sems, recv_sems, 8 * a, False)) for a in range(na)]

    def run(which):
        def hook(*refs):
            for step in steps(*refs):
                step[which]()
        return hook

    routed = all(_route_split(a.shape[2], a.dtype) is not None for a in arrs)
    return _Hosted(arrs, [jax.ShapeDtypeStruct((N_CHIPS,) + a.shape, a.dtype) for a in arrs], 8 * na,
                   run(0), run(2), middle=run(1), peers="neighbours" if routed else "chips_sibling")


def _pair_exchange_plan(grads):
    na = len(grads)

    def copies(ins, outs, send_sems, recv_sems):
        x, y, c, _, _ = _place()
        return [pltpu.make_async_remote_copy(
            src_ref=ins[a].at[:, 1 - c], dst_ref=outs[a], send_sem=send_sems.at[a], recv_sem=recv_sems.at[a],
            device_id=(x, y, 1 - c), device_id_type=MESH) for a in range(na)]

    def start(*refs):
        for cp in copies(*refs):
            cp.start()

    def finish(*refs):
        for cp in copies(*refs):
            cp.wait()

    return _Hosted(grads, [jax.ShapeDtypeStruct(g.shape[:1] + g.shape[2:], g.dtype) for g in grads], na, start, finish,
                   peers="sibling")


def _small_gather_plan(block):
    def copies(ins, outs, send_sems, recv_sems):
        x, y, c, _, chips = _place()
        peers = [(x, y, 1 - c)] + [(px, py, pc) for px, py in chips for pc in (c, 1 - c)]
        sends = [pltpu.make_async_remote_copy(
            src_ref=ins[0], dst_ref=outs[0].at[4 * x + 2 * y + c], send_sem=send_sems.at[k], recv_sem=recv_sems.at[k],
            device_id=peer, device_id_type=MESH) for k, peer in enumerate(peers)]
        recvs = [pltpu.make_async_remote_copy(
            src_ref=ins[0], dst_ref=outs[0].at[4 * px + 2 * py + pc], send_sem=send_sems.at[k], recv_sem=recv_sems.at[k],
            device_id=(px, py, pc), device_id_type=MESH) for k, (px, py, pc) in enumerate(peers)]
        return sends, recvs

    def start(*refs):
        for cp in copies(*refs)[0]:
            cp.start()

    def finish(*refs):
        sends, recvs = copies(*refs)
        for cp in recvs:
            cp.wait_recv()
        for cp in sends:
            cp.wait_send()

    return _Hosted([block], [jax.ShapeDtypeStruct((8,) + block.shape, block.dtype)], 7, start, finish)


def _sum_devices(blocks):
    def body(b_ref, o_ref):
        acc = b_ref[0]
        for d in range(1, 8):
            acc = acc + b_ref[d]
        o_ref[...] = acc

    return pl.pallas_call(body, name="sum_devices", in_specs=[_VMEM], out_specs=_VMEM,
                          out_shape=jax.ShapeDtypeStruct(blocks.shape[1:], blocks.dtype))(blocks)


def _pair_add(grad, recv, c_arr, name):
    _, _, r, cols = grad.shape

    def body(c_ref, g_ref, r_ref, o_ref):
        o_ref[...] = (g_ref[...].astype(F32) + r_ref[...].astype(F32)).astype(BF16)

    return pl.pallas_call(
        body, name=name,
        grid_spec=pltpu.PrefetchScalarGridSpec(
            num_scalar_prefetch=1, grid=(N_CHIPS,),
            in_specs=[pl.BlockSpec((None, None, r, cols), lambda p, c_ref: (p, c_ref[0], 0, 0)),
                      pl.BlockSpec((None, r, cols), lambda p, c_ref: (p, 0, 0))],
            out_specs=pl.BlockSpec((None, r, cols), lambda p, c_ref: (p, 0, 0))),
        out_shape=jax.ShapeDtypeStruct((N_CHIPS, r, cols), BF16),
        compiler_params=_ARB1,
    )(c_arr, grad, recv)


def _chip_exchange_plan(sums, by_peer=False):
    na = len(sums)

    def copies(ins, outs, send_sems, recv_sems):
        x, y, c, me, chips = _place()

        def copy(a, j, px, py, block, slot):
            return pltpu.make_async_remote_copy(
                src_ref=ins[a].at[block], dst_ref=outs[a].at[slot],
                send_sem=send_sems.at[3 * a + j], recv_sem=recv_sems.at[3 * a + j],
                device_id=(px, py, c), device_id_type=MESH)

        peers = [(a, j, px, py) for a in range(na) for j, (px, py) in enumerate(chips)]
        return me, peers, copy

    def start(*refs):
        me, peers, copy = copies(*refs)
        for a, j, px, py in peers:
            if by_peer:
                copy(a, j, px, py, PEER_SLOT[j], PEER_SLOT[j]).start()
            else:
                copy(a, j, px, py, 2 * px + py, me).start()

    def finish(*refs):
        me, peers, copy = copies(*refs)
        for a, j, px, py in peers:
            if by_peer:
                copy(a, j, px, py, PEER_SLOT[j], PEER_SLOT[j]).wait_recv()
            else:
                copy(a, j, px, py, me, 2 * px + py).wait_recv()
        for a, j, px, py in peers:
            if by_peer:
                copy(a, j, px, py, PEER_SLOT[j], PEER_SLOT[j]).wait_send()
            else:
                copy(a, j, px, py, 2 * px + py, me).wait_send()

    return _Hosted(sums, [jax.ShapeDtypeStruct(s.shape, s.dtype) for s in sums], 3 * na, start, finish, peers="chips")


def _chip_sum(own, recv, me_arr, name):
    _, r, cols = recv.shape

    def body(me_ref, own_ref, r_ref, o_ref):
        o_ref[...] = jnp.zeros_like(o_ref)
        for q in range(N_CHIPS):
            @pl.when(me_ref[0] == q)
            def _():
                o_ref[...] += own_ref[...].astype(F32)

            @pl.when(me_ref[0] != q)
            def _():
                o_ref[...] += r_ref[q].astype(F32)

    return pl.pallas_call(
        body, name=name,
        grid_spec=pltpu.PrefetchScalarGridSpec(
            num_scalar_prefetch=1, grid=(1,),
            in_specs=[pl.BlockSpec((None, r, cols), lambda i, me_ref: (me_ref[0], 0, 0)),
                      pl.BlockSpec((N_CHIPS, r, cols), lambda i, me_ref: (0, 0, 0))],
            out_specs=pl.BlockSpec((r, cols), lambda i, me_ref: (0, 0))),
        out_shape=jax.ShapeDtypeStruct((r, cols), F32),
        compiler_params=_ARB1,
    )(me_arr, own, recv)


def _peer_sum(own, recv, name):
    _, r, cols = recv.shape

    def body(own_ref, r_ref, o_ref):
        acc = own_ref[...].astype(F32) + r_ref[1].astype(F32)
        acc = acc + r_ref[2].astype(F32)
        o_ref[...] = acc + r_ref[3].astype(F32)

    return pl.pallas_call(
        body, name=name, grid=(1,),
        in_specs=[pl.BlockSpec((None, r, cols), lambda i: (0, 0, 0)), pl.BlockSpec((N_CHIPS, r, cols), lambda i: (0, 0, 0))],
        out_specs=pl.BlockSpec((r, cols), lambda i: (0, 0)),
        out_shape=jax.ShapeDtypeStruct((r, cols), F32),
        compiler_params=_ARB1,
    )(own, recv)


def _pair_share_plan(halves):
    na = len(halves)

    def copies(ins, outs, send_sems, recv_sems):
        x, y, c, _, _ = _place()
        return [pltpu.make_async_remote_copy(
            src_ref=ins[a], dst_ref=outs[a], send_sem=send_sems.at[a], recv_sem=recv_sems.at[a],
            device_id=(x, y, 1 - c), device_id_type=MESH) for a in range(na)]

    def start(*refs):
        for cp in copies(*refs):
            cp.start()

    def finish(*refs):
        for cp in copies(*refs):
            cp.wait()

    return _Hosted(halves, [jax.ShapeDtypeStruct(h.shape, h.dtype) for h in halves], na, start, finish,
                   peers="sibling")


def _small_allreduce(block):
    m, n = block.shape

    def body(x_ref, all_ref, sum_ref, send_sems, recv_sems, local_sem):
        x, y, c, _, chips = _place()
        me, sibling = (x, y, c), (x, y, 1 - c)

        def rows(px, py, pc):
            return all_ref.at[pl.ds((4 * px + 2 * py + pc) * m, m), :]

        def copy(k, blk, to, src=None):
            return pltpu.make_async_remote_copy(
                src_ref=rows(*blk) if src is None else src, dst_ref=rows(*blk),
                send_sem=send_sems.at[k], recv_sem=recv_sems.at[k], device_id=to, device_id_type=MESH)

        mine = pltpu.make_async_copy(x_ref, rows(*me), local_sem)
        mine.start()
        first = [copy(0, me, sibling, src=x_ref)]
        first += [copy(1 + j, me, (*chip, c), src=x_ref) for j, chip in enumerate(chips)]
        for cp in first:
            cp.start()
        passed = [copy(4 + j, (*chip, c), sibling) for j, chip in enumerate(chips)]
        for j, chip in enumerate(chips):
            copy(1 + j, (*chip, c), me).wait_recv()
            passed[j].start()
        copy(0, sibling, me).wait_recv()
        for j, chip in enumerate(chips):
            copy(4 + j, (*chip, 1 - c), me).wait_recv()
        for cp in first + passed:
            cp.wait_send()
        mine.wait()
        acc = all_ref[0:m, :]
        for d in range(1, 8):
            acc = acc + all_ref[d * m:(d + 1) * m, :]
        sum_ref[...] = acc

    vmem = pl.BlockSpec(memory_space=pltpu.VMEM)
    return pl.pallas_call(
        body, name="small_allreduce",
        in_specs=[vmem], out_specs=[vmem, vmem],
        out_shape=[jax.ShapeDtypeStruct((8 * m, n), F32), jax.ShapeDtypeStruct((m, n), F32)],
        scratch_shapes=[pltpu.SemaphoreType.DMA((7,)), pltpu.SemaphoreType.DMA((7,)), pltpu.SemaphoreType.DMA],
    )(block)[1]


def _row_tile(rows):
    best = rows
    for cand in range(8, min(rows, 512) + 1, 8):
        if rows % cand == 0:
            best = cand
    return best


def _adamw_math(w, g, m, v):
    m2 = ADAM_B1 * m + (1.0 - ADAM_B1) * g
    v2 = ADAM_B2 * v + (1.0 - ADAM_B2) * (g * g)
    m_hat = m2 / (1.0 - ADAM_B1 ** ADAM_STEP)
    v_hat = v2 / (1.0 - ADAM_B2 ** ADAM_STEP)
    return -ADAM_LR * (m_hat / (jnp.sqrt(v_hat) + ADAM_EPS) + ADAM_WD * w), m2, v2


def _adamw_halves(w, g_mine, g_other, m, v, c_arr, name):
    rows, cols = w.shape
    r = rows // 2
    tr = _row_tile(r)
    nt = r // tr

    def body(c_ref, w_ref, gm_ref, go_ref, m_ref, v_ref, g_ref, d_ref, nm_ref, nv_ref):
        gv = jnp.where(pl.program_id(0) == c_ref[0], gm_ref[...], go_ref[...])
        g_ref[...] = gv
        d_ref[...], nm_ref[...], nv_ref[...] = _adamw_math(w_ref[...], gv, m_ref[...], v_ref[...])

    full = pl.BlockSpec((tr, cols), lambda h, i, c_ref: (h * nt + i, 0))
    half = pl.BlockSpec((tr, cols), lambda h, i, c_ref: (i, 0))
    shape = jax.ShapeDtypeStruct((rows, cols), F32)
    return pl.pallas_call(
        body, name=name,
        grid_spec=pltpu.PrefetchScalarGridSpec(
            num_scalar_prefetch=1, grid=(2, nt),
            in_specs=[full, half, half, full, full], out_specs=[full] * 4),
        out_shape=[shape] * 4,
        compiler_params=_ARB2,
    )(c_arr, w, g_mine, g_other, m, v)


def _adamw(w, g, m, v, name):
    rows, cols = w.shape
    tr = _row_tile(rows)

    def body(w_ref, g_ref, m_ref, v_ref, d_ref, nm_ref, nv_ref):
        d_ref[...], nm_ref[...], nv_ref[...] = _adamw_math(w_ref[...], g_ref[...], m_ref[...], v_ref[...])

    spec = pl.BlockSpec((tr, cols), lambda i: (i, 0))
    shape = jax.ShapeDtypeStruct((rows, cols), F32)
    return pl.pallas_call(
        body, name=name, grid=(rows // tr,),
        in_specs=[spec] * 4, out_specs=[spec] * 3, out_shape=[shape] * 3,
        compiler_params=_ARB1,
    )(w, g, m, v)


def _pad_w_in_t(w_in_t):
    return jnp.pad(w_in_t, ((0, PROJ_P - IN_WIDTH), (0, 0)))


def _unpad_w_in_t(w_pt):
    return w_pt[0:IN_WIDTH]


def _rope_tables(t):
    half = 32
    inv = ROPE_BASE ** (-jnp.arange(half, dtype=F32) * 2.0 / 64)
    ang = jnp.arange(t, dtype=F32)[:, None] * inv[None, :]
    cos, sin = jnp.cos(ang), jnp.sin(ang)
    z32, z64 = jnp.zeros((t, 32), F32), jnp.zeros((t, 64), F32)
    return (jnp.concatenate([cos, cos, z64], axis=1),
            jnp.concatenate([-sin, z32, z64], axis=1),
            jnp.concatenate([z32, sin, z64], axis=1))


def _halves(w):
    n, rows, cols = w.shape
    return w.reshape(n, 2, rows // 2, cols)


_VMEM = pl.BlockSpec(memory_space=pltpu.VMEM)


def _pack_small(n1, nm, n2, nf, nret, ngla, ba, wa2_p, loss_blk):
    def body(n1_ref, nm_ref, n2_ref, nf_ref, nret_ref, ngla_ref, ba_ref, wa2_ref, loss_ref, o_ref):
        o_ref[...] = jnp.zeros_like(o_ref)
        o_ref[0:1, :] = n1_ref[...]
        o_ref[1:2, :] = nm_ref[...]
        o_ref[2:3, :] = n2_ref[...]
        o_ref[3:4, :] = nf_ref[...]
        o_ref[4:5, 0:512] = nret_ref[...]
        o_ref[4:5, 512:1024] = ngla_ref[...]
        o_ref[5:6, 0:256] = ba_ref[...]
        o_ref[6:7, 0:LANES] = loss_ref[0:1, :]
        o_ref[8:8 + GATE_RANK, 0:HEADS * LANES] = wa2_ref[0:GATE_RANK, :]

    return pl.pallas_call(
        body, name="pack_small", in_specs=[_VMEM] * 9, out_specs=_VMEM,
        out_shape=jax.ShapeDtypeStruct((SMALL_ROWS, D_MODEL), F32),
    )(n1, nm, n2, nf, nret, ngla, ba, wa2_p, loss_blk)


def _small_update(summed, chip_arr, ws, ms, vs):
    n = len(ws)

    def body(chip_ref, s_ref, *refs):
        w_refs, m_refs, v_refs = refs[0:n], refs[n:2 * n], refs[2 * n:3 * n]
        outs = refs[3 * n:]
        wa2_all = s_ref[8:8 + GATE_RANK, 0:HEADS * LANES]
        wa2_g = jnp.zeros((GATE_RANK, 64), F32)
        for p in range(N_CHIPS):
            wa2_g = jnp.where(chip_ref[0] == p, wa2_all[:, LANES * p:LANES * p + 64], wa2_g)
        grads = [s_ref[0:1, :], s_ref[1:2, :], s_ref[2:3, :], s_ref[3:4, :], s_ref[4:5, 0:512],
                 s_ref[4:5, 512:1024], s_ref[5:6, 0:256], wa2_g]
        for k in range(n):
            d, m2, v2 = _adamw_math(w_refs[k][...], grads[k], m_refs[k][...], v_refs[k][...])
            outs[k][...] = grads[k]
            outs[n + k][...] = d
            outs[2 * n + k][...] = m2
            outs[3 * n + k][...] = v2

    shapes = [jax.ShapeDtypeStruct(w.shape, F32) for w in ws] * 4
    smem = pl.BlockSpec(memory_space=pltpu.SMEM)
    outs = pl.pallas_call(
        body, name="small_update", in_specs=[smem] + [_VMEM] * (1 + 3 * n), out_specs=[_VMEM] * (4 * n),
        out_shape=shapes,
    )(chip_arr, summed, *ws, *ms, *vs)
    return outs[0:n], outs[n:2 * n], outs[2 * n:3 * n], outs[3 * n:4 * n]


def _pad_in_rows(w_t):
    return jnp.pad(w_t, ((0, IN_ROWS - IN_SHARD), (0, 0)))


def _forward_backward(xs, target, ffn1_w, rest, ba_p, ffn1_norm_g, mix_norm_g, ret_norm_g, gla_norm_g, ffn2_norm_g,
                      final_norm_g, ffn1_gather=None, rest_plan=None, rest_weights=None, ffn2_plans=None,
                      ffn2_weights=None, ffn2_pairs=None, ffn2_pairs_done=None, early=None, late=None, small_plan=None):
    t = xs.shape[0]
    cos_t, sa_t, sb_t = _rope_tables(t)
    log_gamma = jnp.log(1.0 - 2.0 ** (-5.0 - jnp.arange(HEADS, dtype=F32)))
    lg_t = jnp.broadcast_to(log_gamma[:, None, None], (HEADS, 1, LANES))
    ret_aux = [cos_t, sa_t, sb_t, lg_t]

    if ffn1_gather is None:
        (x1, a1, u1, h1), gathered = _ffn_fwd(xs, ffn1_norm_g, ffn1_w, "ffn1_fwd", hosted=rest_plan)
    else:
        ffn1_shard, ffn1_weights = ffn1_gather
        (x1, a1, u1, h1, wall), gathered = _ffn1_fwd_gathering(xs, ffn1_norm_g, ffn1_shard, "ffn1_fwd",
                                                               hosted=rest_plan)
        ffn1_w = ffn1_weights(wall)
    ffn2_w, w_in_pt, w_out_full, wa2_p = rest if rest_plan is None else rest_weights(gathered)
    plans = [None] * 3 if ffn2_plans is None else ffn2_plans
    (proj, h_mix), got_gate = _mixer_in_fwd(x1, mix_norm_g, w_in_pt, "mixer_in_fwd", hosted=plans[0])
    gla_aux = [proj, wa2_p, ba_p]
    (o_ret, raw_ret, st_ret), got_up = _attn_fwd(True, proj, ret_aux, ret_norm_g, "ret_fwd", hosted=plans[1])
    (o_gla, raw_gla, st_gla), got_down = _attn_fwd(False, proj, gla_aux, gla_norm_g, "gla_fwd", hosted=plans[2])
    if ffn2_plans is not None:
        ffn2_w = ffn2_weights(got_gate + got_up + got_down)
    x2 = _mixer_out_fwd(o_ret, o_gla, w_out_full, x1, "mixer_out_fwd")
    (x3, a2, u2, h2), _ = _ffn_fwd(x2, ffn2_norm_g, ffn2_w, "ffn2_fwd")
    loss_blk, dx3, d_final_g = _final_loss(x3, final_norm_g, target, "final_loss")

    (da2, du2, hid2, dob2, dx2, d_ffn2_g), _ = _ffn_bwd(dx3, x2, ffn2_norm_g, a2, u2, ffn2_w, "ffn2_bwd")
    g_gate2 = _matmul_tn(da2, h2, "ffn2_dgate", out_dtype=BF16)
    g_up2 = _matmul_tn(du2, h2, "ffn2_dup", out_dtype=BF16)
    g_down2 = _matmul_tn(hid2, dob2, "ffn2_ddown", out_dtype=BF16)

    d_o = _matmul_nt(dx2, w_out_full, "mixer_out_bwd")
    g_wout_ret = _matmul_tn(o_ret, dx2, "wout_grad_ret", out_dtype=BF16)
    g_wout_gla = _matmul_tn(o_gla, dx2, "wout_grad_gla", out_dtype=BF16)
    pairs_plan = None if ffn2_pairs is None else ffn2_pairs([g_gate2, g_up2, g_down2])
    (*dproj_ret, d_ret_g), pair_recv = _attn_bwd(True, proj, ret_aux, ret_norm_g, raw_ret, st_ret, d_o, "ret_bwd",
                                                 hosted=pairs_plan)
    if ffn2_pairs is not None:
        ffn2_pairs_done(pair_recv)
    (*dproj_gla, d_gla_g, dlogit, d_ba_p), _ = _attn_bwd(False, proj, gla_aux, gla_norm_g, raw_gla, st_gla, d_o,
                                                        "gla_bwd")
    d_glow = _matmul_nt(dlogit, wa2_p, "gate_low_bwd", out_dtype=BF16)
    g_wa2_p = _matmul_tn(proj[:, PROJ_P - LANES:], dlogit, "gate_w_grad")
    dproj = jnp.concatenate(dproj_ret + dproj_gla + [d_glow], axis=1)
    g_win_p = _matmul_tn(dproj, h_mix, "w_in_grad", tka=PROJ_P // PROJ_TILES, out_dtype=BF16)
    dx1, d_mix_g = _mixer_in_bwd(dproj, w_in_pt, dx2, x1, mix_norm_g, "mixer_in_bwd")
    g_win_t = _unpad_w_in_t(g_win_p[0])
    g_win = jnp.stack([_pad_in_rows(g_win_t[IN_SHARD * p:IN_SHARD * (p + 1)]) for p in range(N_CHIPS)], axis=0)
    g_wout = jnp.concatenate([g_wout_ret[0], g_wout_gla[0]], axis=0).reshape(N_CHIPS, D_MODEL // N_CHIPS, D_MODEL)

    early_grads = [g_win, g_wout] if ffn2_pairs is not None else [g_gate2, g_up2, g_down2, g_win, g_wout]
    early_plan = None if early is None else early(early_grads)
    (da1, du1, hid1, dob1, grad_x, d_ffn1_g), arrived = _ffn_bwd(dx1, xs, ffn1_norm_g, a1, u1, ffn1_w, "ffn1_bwd",
                                                                hosted=early_plan)
    d_ba = d_ba_p.reshape(HEADS, LANES)[:, 0:64].reshape(1, 256)
    small_local = _pack_small(d_ffn1_g, d_mix_g, d_ffn2_g, d_final_g, d_ret_g, d_gla_g, d_ba, g_wa2_p[0], loss_blk)
    late_grads, late_arrived = [], []
    for lhs, rhs, name in ((da1, h1, "ffn1_dgate"), (du1, h1, "ffn1_dup"), (hid1, dob1, "ffn1_ddown")):
        if late is None:
            plan = None
        else:
            plan = late(late_grads[-1], len(late_grads)) if late_grads else small_plan(small_local)
        res = _matmul_tn(lhs, rhs, name, out_dtype=BF16, hosted=plan)
        if plan is not None:
            res, carried = res
            late_arrived += carried
        late_grads.append(res)
    g_gate1, g_up1, g_down1 = late_grads

    return (small_local, grad_x, g_gate1, g_up1, g_down1, g_gate2, g_up2, g_down2, g_win, g_wout, g_wa2_p,
            d_ba_p, d_ffn1_g, d_mix_g, d_ffn2_g, d_final_g, d_ret_g, d_gla_g, arrived, late_arrived)


def kernel(x, ffn1_norm_g, ffn1_w_gate, ffn1_w_up, ffn1_w_down, mix_norm_g, w_in, ret_norm_g, gla_w_a2, gla_b_a, gla_norm_g, w_out, ffn2_norm_g, ffn2_w_gate, ffn2_w_up, ffn2_w_down, final_norm_g, loss_target, m_ffn1_norm_g, m_ffn1_w_gate, m_ffn1_w_up, m_ffn1_w_down, m_mix_norm_g, m_w_in, m_ret_norm_g, m_gla_w_a2, m_gla_b_a, m_gla_norm_g, m_w_out, m_ffn2_norm_g, m_ffn2_w_gate, m_ffn2_w_up, m_ffn2_w_down, m_final_norm_g, v_ffn1_norm_g, v_ffn1_w_gate, v_ffn1_w_up, v_ffn1_w_down, v_mix_norm_g, v_w_in, v_ret_norm_g, v_gla_w_a2, v_gla_b_a, v_gla_norm_g, v_w_out, v_ffn2_norm_g, v_ffn2_w_gate, v_ffn2_w_up, v_ffn2_w_down, v_final_norm_g):
    t = x.shape[1]
    xs = x.reshape(t, D_MODEL)
    target = loss_target.reshape(t, D_MODEL)
    chip = 2 * lax.axis_index("x") + lax.axis_index("y")
    c_arr = lax.axis_index("c").astype(jnp.int32).reshape(1)

    me_arr = chip.astype(jnp.int32).reshape(1)

    pad_rows = _pad_in_rows

    def own_block(gathered, shard):
        return lax.dynamic_update_slice(gathered, shard[None], (chip,) + (0,) * shard.ndim)

    ffn1_shard = _halves(jnp.stack([ffn1_w_gate[0].T, ffn1_w_up[0].T, ffn1_w_down[0]], axis=0).astype(BF16))
    rest_shards = [_halves(pad_rows(w_in[0].T).astype(BF16)[None]),
                   _halves(w_out.astype(BF16)),
                   jnp.concatenate([gla_w_a2.reshape(GATE_RANK, 64), jnp.zeros((GATE_RANK, 64), F32)],
                                   axis=1).reshape(1, 2, 8, LANES)]
    ffn2_shards = [_halves(w.astype(BF16)[None]) for w in (ffn2_w_gate[0].T, ffn2_w_up[0].T, ffn2_w_down[0])]
    def ffn1_weights(gathered):
        return lax.dynamic_update_slice(gathered, ffn1_shard[None], (0,) * 5).reshape(N_CHIPS, 3, FF_SHARD, D_MODEL)

    def rest_weights(gathered):
        win_all, wout_all, wa2_all = [own_block(g, s) for g, s in zip(gathered, rest_shards)]
        win_t = win_all.reshape(N_CHIPS, IN_ROWS, D_MODEL)
        w_in_pt = jnp.zeros((PROJ_P, D_MODEL), BF16)
        for p in range(N_CHIPS):
            w_in_pt = lax.dynamic_update_slice(w_in_pt, win_t[p, 0:IN_SHARD], (IN_SHARD * p, 0))
        wa2_p = jnp.pad(
            wa2_all.reshape(N_CHIPS, GATE_RANK, LANES).transpose(1, 0, 2).reshape(GATE_RANK, HEADS * LANES),
            ((0, LANES - GATE_RANK), (0, 0))).astype(BF16)
        return (None, w_in_pt, wout_all.reshape(D_MODEL, D_MODEL), wa2_p)

    def ffn2_weights(gathered):
        return [own_block(g, s).reshape(N_CHIPS, FF_SHARD, D_MODEL) for g, s in zip(gathered, ffn2_shards)]

    def by_halves(g):
        return g.reshape(g.shape[0], 2, g.shape[1] // 2, g.shape[2])

    def pair_adds(halves, recv, tag):
        return [_pair_add(g, r, c_arr, "pair_add_%s%d" % (tag, k)) for k, (g, r) in enumerate(zip(halves, recv))]

    def pair_sums(grads, tag):
        halves = [by_halves(g) for g in grads]
        recv = _run_hosted(_pair_exchange_plan(halves), "pair_exchange_" + tag)
        return pair_adds(halves, recv, tag)

    early_sums, ffn2_halves = [], []

    def ffn2_pairs(grads):
        ffn2_halves.extend(by_halves(g) for g in grads)
        return _pair_exchange_plan(ffn2_halves)

    def ffn2_pairs_done(recv):
        early_sums.extend(pair_adds(ffn2_halves, recv, "ffn2_"))

    def early(grads):
        early_sums.extend(pair_sums(grads, "early"))
        return _chip_exchange_plan(early_sums)

    late_sums = []

    def late(grad, number):
        late_sums.extend(pair_sums([grad], "late%d" % number))
        return _chip_exchange_plan(late_sums[-1:], by_peer=True)

    ba_p = jnp.pad(gla_b_a.reshape(HEADS, 64), ((0, 0), (0, 64))).reshape(1, HEADS * LANES)
    fb = _forward_backward(xs, target, None, None, ba_p, ffn1_norm_g, mix_norm_g, ret_norm_g, gla_norm_g,
                           ffn2_norm_g, final_norm_g.reshape(1, D_MODEL), ffn1_gather=(ffn1_shard, ffn1_weights),
                           rest_plan=_gather_plan(rest_shards), rest_weights=rest_weights,
                           ffn2_plans=[_gather_plan(ffn2_shards[0:2]), None, _gather_plan(ffn2_shards[2:3])],
                           ffn2_weights=ffn2_weights,
                           ffn2_pairs=ffn2_pairs, ffn2_pairs_done=ffn2_pairs_done, early=early, late=late,
                           small_plan=_small_gather_plan)
    (small_local, grad_x, _, _, g_down1, _, _, _, _, _, _, _, _, _, _, _, _, _, early_arrived, late_arrived) = fb
    small_all, late_arrived = late_arrived[0], late_arrived[1:]
    late_arrived = late_arrived + _run_hosted(late(g_down1, 3), "chip_exchange_late")
    mine = [_peer_sum(s, r, "chip_sum_%d" % k) for k, (s, r) in enumerate(zip(late_sums, late_arrived))]
    mine += [_chip_sum(s, r, me_arr, "chip_sum_%d" % (3 + k)) for k, (s, r) in enumerate(zip(early_sums, early_arrived))]
    other = _run_hosted(_pair_share_plan(mine), "pair_share")

    device = 2 * chip + lax.axis_index("c")
    small_sum = _sum_devices(lax.dynamic_update_slice(small_all, small_local[None], (device, 0, 0)))
    loss = small_sum[6, 0]

    def rows(n1, nm, n2, nf, nret, ngla, ba, wa2):
        return [n1, nm, n2, nf.reshape(1, D_MODEL), nret, ngla, ba, wa2.reshape(GATE_RANK, 64)]

    small = _small_update(
        small_sum, me_arr,
        rows(ffn1_norm_g, mix_norm_g, ffn2_norm_g, final_norm_g, ret_norm_g, gla_norm_g, gla_b_a, gla_w_a2),
        rows(m_ffn1_norm_g, m_mix_norm_g, m_ffn2_norm_g, m_final_norm_g, m_ret_norm_g, m_gla_norm_g, m_gla_b_a,
             m_gla_w_a2),
        rows(v_ffn1_norm_g, v_mix_norm_g, v_ffn2_norm_g, v_final_norm_g, v_ret_norm_g, v_gla_norm_g, v_gla_b_a,
             v_gla_w_a2))
    s_grad, s_delta, s_m, s_v = [
        [*o[0:3], o[3].reshape(D_MODEL), *o[4:7], o[7].reshape(1, GATE_RANK, 64)] for o in small]

    def big(k, w, m, v, name, to_2d, from_2d):
        outs4 = _adamw_halves(to_2d(w), mine[k], other[k], to_2d(m), to_2d(v), c_arr, name)
        return [from_2d(z) for z in outs4]

    plain = (lambda w: w[0], lambda z: z[None])
    transposed = (lambda w: w[0].T, lambda z: z.T[None])
    in_proj = (lambda w: pad_rows(w[0].T), lambda z: z[0:IN_SHARD].T[None])
    r_g1 = big(0, ffn1_w_gate, m_ffn1_w_gate, v_ffn1_w_gate, "adamw_ffn1_gate", *transposed)
    r_u1 = big(1, ffn1_w_up, m_ffn1_w_up, v_ffn1_w_up, "adamw_ffn1_up", *transposed)
    r_d1 = big(2, ffn1_w_down, m_ffn1_w_down, v_ffn1_w_down, "adamw_ffn1_down", *plain)
    r_g2 = big(3, ffn2_w_gate, m_ffn2_w_gate, v_ffn2_w_gate, "adamw_ffn2_gate", *transposed)
    r_u2 = big(4, ffn2_w_up, m_ffn2_w_up, v_ffn2_w_up, "adamw_ffn2_up", *transposed)
    r_d2 = big(5, ffn2_w_down, m_ffn2_w_down, v_ffn2_w_down, "adamw_ffn2_down", *plain)
    r_in = big(6, w_in, m_w_in, v_w_in, "adamw_w_in", *in_proj)
    r_out = big(7, w_out, m_w_out, v_w_out, "adamw_w_out", *plain)

    def leaves(k, smalls):
        n1, nm, n2, nf, nret, ngla, ba, wa2 = smalls
        return [n1, r_g1[k], r_u1[k], r_d1[k], nm, r_in[k], nret, wa2, ba, ngla, r_out[k], n2, r_g2[k], r_u2[k], r_d2[k], nf]

    outs = [loss, grad_x.reshape(x.shape)]
    outs += leaves(0, s_grad) + leaves(1, s_delta) + leaves(2, s_m) + leaves(3, s_v)
    return tuple(outs)
```

```python
import functools

import jax
import jax.numpy as jnp
from jax import lax
from jax.experimental import pallas as pl
from jax.experimental.pallas import tpu as pltpu

F32, BF16 = jnp.float32, jnp.bfloat16
MESH = pl.DeviceIdType.MESH
ANY = pl.BlockSpec(memory_space=pl.ANY)

D_MODEL = 1024
D_FF = 2816
N_CHIPS = 4
FF_SHARD = D_FF // N_CHIPS
IN_WIDTH = 3088
IN_SHARD = IN_WIDTH // N_CHIPS
IN_ROWS = 800
CHUNK = 64
HEADS = 4
LANES = 128
PROJ_P = 3072 + LANES
PROJ_TILES = 5
GATE_RANK = 16
QK_SCALE = 0.125
GATE_NORM = 16.0
RMS_EPS = 1e-6
ROPE_BASE = 10000.0
ADAM_LR, ADAM_B1, ADAM_B2, ADAM_EPS, ADAM_WD, ADAM_STEP = 0.001, 0.9, 0.999, 1e-08, 0.01, 10
SMALL_ROWS = 32
TOKEN_TILE = 512
ATTN_TILE = 512

_ARB2 = pltpu.CompilerParams(dimension_semantics=("arbitrary", "arbitrary"))
_ARB1 = pltpu.CompilerParams(dimension_semantics=("arbitrary",))
_ARB3 = pltpu.CompilerParams(dimension_semantics=("arbitrary", "arbitrary", "arbitrary"))


def _dot(a, b):
    return jnp.dot(a, b, preferred_element_type=F32)


def _dot_nt(a, b):
    return lax.dot_general(a, b, (((1,), (1,)), ((), ())), preferred_element_type=F32)


def _dot_tn(a, b):
    return lax.dot_general(a, b, (((0,), (0,)), ((), ())), preferred_element_type=F32)


def _rms_scale(xv):
    return lax.rsqrt(jnp.mean(xv * xv, axis=-1, keepdims=True) + RMS_EPS)


def _rms_bwd(dh, xv, g):
    r = _rms_scale(xv)
    xhat = xv * r
    dxhat = dh * g
    dx = r * (dxhat - xhat * jnp.mean(dxhat * xhat, axis=-1, keepdims=True))
    return dx, jnp.sum(dh * xhat, axis=0, keepdims=True)


def _silu_grad(a, sg):
    return sg * (1.0 + a * (1.0 - sg))


class _Hosted:
    def __init__(self, arrays, out_shapes, n_sems, start, finish, middle=None, peers=None):
        self.arrays, self.out_shapes, self.n_sems = list(arrays), list(out_shapes), n_sems
        self.start, self.finish = start, finish
        self.middle = middle if middle is not None else (lambda *refs: None)
        self.peers = peers


PEER_SETS = {
    "sibling": (0, lambda x, y, c: [(x, y, 1 - c)]),
    "chips": (1, lambda x, y, c: [(1 - x, y, c), (x, 1 - y, c), (1 - x, 1 - y, c)]),
    "neighbours": (2, lambda x, y, c: [(1 - x, y, c), (x, 1 - y, c), (x, y, 1 - c)]),
    "chips_sibling": (3, lambda x, y, c: [(1 - x, y, c), (x, 1 - y, c), (1 - x, 1 - y, c), (x, y, 1 - c)]),
}


def _handshake(kind):
    x, y, c, _, _ = _place()
    peers = PEER_SETS[kind][1](x, y, c)
    barrier = pltpu.get_barrier_semaphore()
    for peer in peers:
        pl.semaphore_signal(barrier, inc=1, device_id=peer, device_id_type=MESH)
    pl.semaphore_wait(barrier, len(peers))


def _with_barrier(compiler_params, kind):
    if kind is None:
        return compiler_params
    semantics = None if compiler_params is None else compiler_params.dimension_semantics
    return pltpu.CompilerParams(dimension_semantics=semantics, collective_id=PEER_SETS[kind][0])


def _call(body, args, *, name, grid, in_specs, out_specs, out_shape, scratch_shapes, compiler_params, hosted=None):
    if hosted is None:
        outs = pl.pallas_call(body, name=name, grid=grid, in_specs=in_specs, out_specs=out_specs, out_shape=out_shape,
                              scratch_shapes=scratch_shapes, compiler_params=compiler_params)(*args)
        return list(outs), []
    n_in, n_out, n_sc, nh = len(in_specs), len(out_specs), len(scratch_shapes), len(hosted.arrays)

    def wrapped(*refs):
        ins, h_in = refs[:n_in], refs[n_in:n_in + nh]
        outs, h_out = refs[n_in + nh:n_in + nh + n_out], refs[n_in + nh + n_out:n_in + 2 * nh + n_out]
        rest = refs[n_in + 2 * nh + n_out:]
        scratch, (send_sems, recv_sems) = rest[:n_sc], rest[n_sc:]
        step = functools.reduce(lambda flat, d: flat * grid[d] + pl.program_id(d), range(len(grid)), 0)
        total = functools.reduce(lambda a, b: a * b, grid)

        @pl.when(step == 0)
        def _():
            if hosted.peers is not None:
                _handshake(hosted.peers)
            hosted.start(h_in, h_out, send_sems, recv_sems)

        @pl.when(step == total // 2)
        def _():
            hosted.middle(h_in, h_out, send_sems, recv_sems)

        body(*ins, *outs, *scratch)
        last = step == total - 1

        @pl.when(last)
        def _():
            hosted.finish(h_in, h_out, send_sems, recv_sems)

    sems = [pltpu.SemaphoreType.DMA((hosted.n_sems,)), pltpu.SemaphoreType.DMA((hosted.n_sems,))]
    outs = pl.pallas_call(
        wrapped, name=name, grid=grid, in_specs=list(in_specs) + [ANY] * nh, out_specs=list(out_specs) + [ANY] * nh,
        out_shape=list(out_shape) + hosted.out_shapes, scratch_shapes=list(scratch_shapes) + sems,
        compiler_params=_with_barrier(compiler_params, hosted.peers))(*args, *hosted.arrays)
    return list(outs[:n_out]), list(outs[n_out:])


def _run_hosted(hosted, name):
    nh = len(hosted.arrays)

    def body(*refs):
        h_in, h_out, (send_sems, recv_sems) = refs[:nh], refs[nh:2 * nh], refs[2 * nh:]
        if hosted.peers is not None:
            _handshake(hosted.peers)
        hosted.start(h_in, h_out, send_sems, recv_sems)
        hosted.middle(h_in, h_out, send_sems, recv_sems)
        hosted.finish(h_in, h_out, send_sems, recv_sems)

    sems = [pltpu.SemaphoreType.DMA((hosted.n_sems,)), pltpu.SemaphoreType.DMA((hosted.n_sems,))]
    return list(pl.pallas_call(body, name=name, in_specs=[ANY] * nh, out_specs=[ANY] * nh, out_shape=hosted.out_shapes,
                               scratch_shapes=sems, compiler_params=_with_barrier(None, hosted.peers))(*hosted.arrays))


def _ffn_weight_operands(ffn_w, chunk_maps):
    if isinstance(ffn_w, (list, tuple)):
        specs = [pl.BlockSpec((None, FF_SHARD, D_MODEL), lambda *g, m=m: (m(*g), 0, 0)) for m in chunk_maps]
        return list(ffn_w), specs
    specs = [pl.BlockSpec((None, None, FF_SHARD, D_MODEL), lambda *g, m=m, k=kind: (m(*g), k, 0, 0))
             for kind, m in enumerate(chunk_maps)]
    return [ffn_w] * 3, specs


def _pipeline_items(steps):
    def cur(s):
        c = jnp.minimum(s, steps - 1)
        return c // N_CHIPS, c % N_CHIPS

    def prev(s):
        p = jnp.maximum(s - 1, 0)
        return p // N_CHIPS, p % N_CHIPS

    return cur, prev


def _ffn_fwd(x, g, ffn_w, name, hosted=None, loss_head=None):
    t = x.shape[0]
    tm = min(t, TOKEN_TILE)
    n_head = 0 if loss_head is None else 2

    def body(*refs):
        x_ref, g_ref, wg_ref, wu_ref, wd_ref = refs[0:5]
        head_in = refs[5:5 + n_head]
        outs = refs[5 + n_head:-1]
        acc_ref = refs[-1]
        a_ref, u_ref, h_ref = outs[-3:]
        i, j = pl.program_id(0), pl.program_id(1)

        @pl.when(j == 0)
        def _():
            xv = x_ref[...]
            h_ref[...] = ((xv * _rms_scale(xv)) * g_ref[...]).astype(BF16)
            acc_ref[...] = jnp.zeros_like(acc_ref)

        h = h_ref[...]
        a = _dot_nt(h, wg_ref[...])
        u = _dot_nt(h, wu_ref[...])
        a_ref[...] = a.astype(BF16)
        u_ref[...] = u.astype(BF16)
        hid = (a * jax.nn.sigmoid(a)) * u
        acc_ref[...] += _dot(hid.astype(BF16), wd_ref[...])

        if loss_head is None:
            @pl.when(j == N_CHIPS - 1)
            def _():
                outs[0][...] = x_ref[...] + 0.5 * acc_ref[...]
        else:
            gf_ref, t_ref = head_in
            l_ref, dx_ref, dgf_ref = outs[0:3]

            @pl.when((i == 0) & (j == 0))
            def _():
                l_ref[...] = jnp.zeros_like(l_ref)
                dgf_ref[...] = jnp.zeros_like(dgf_ref)

            @pl.when(j == N_CHIPS - 1)
            def _():
                xv = x_ref[...] + 0.5 * acc_ref[...]
                gv = gf_ref[...]
                err = (xv * _rms_scale(xv)) * gv - t_ref[...]
                l_ref[...] += 0.5 * jnp.sum(jnp.mean(err * err, axis=-1, keepdims=True), axis=0, keepdims=True)
                dx, dg = _rms_bwd(err * (1.0 / D_MODEL), xv, gv)
                dx_ref[...] = dx
                dgf_ref[...] += dg

    tok = pl.BlockSpec((tm, D_MODEL), lambda i, j: (i, 0))
    row = pl.BlockSpec((1, D_MODEL), lambda i, j: (0, 0))
    act = pl.BlockSpec((None, tm, FF_SHARD), lambda i, j: (j, i, 0))
    act_shape = jax.ShapeDtypeStruct((N_CHIPS, t, FF_SHARD), BF16)
    w_arrays, weights = _ffn_weight_operands(ffn_w, [lambda i, j: j] * 3)
    if loss_head is None:
        first_specs, first_shapes, head_args, head_specs = [tok], [jax.ShapeDtypeStruct((t, D_MODEL), F32)], [], []
    else:
        first_specs = [pl.BlockSpec((8, LANES), lambda i, j: (0, 0)), tok, row]
        first_shapes = [jax.ShapeDtypeStruct((8, LANES), F32), jax.ShapeDtypeStruct((t, D_MODEL), F32),
                        jax.ShapeDtypeStruct((1, D_MODEL), F32)]
        head_args, head_specs = list(loss_head), [row, tok]
    return _call(
        body, (x, g, *w_arrays, *head_args), name=name, grid=(t // tm, N_CHIPS),
        in_specs=[tok, row] + weights + head_specs,
        out_specs=first_specs + [act, act, tok],
        out_shape=first_shapes + [act_shape, act_shape, jax.ShapeDtypeStruct((t, D_MODEL), BF16)],
        scratch_shapes=[pltpu.VMEM((tm, D_MODEL), F32)],
        compiler_params=_ARB2, hosted=hosted)


def _ffn1_fwd_gathering(x, g, shard, name, hosted=None):
    t = x.shape[0]
    tm = min(t, TOKEN_TILE)
    nt = t // tm
    nh = 0 if hosted is None else len(hosted.arrays)
    peers = "neighbours" if hosted is None or hosted.peers == "neighbours" else "chips_sibling"
    assert hosted is None or hosted.peers in ("neighbours", "chips_sibling")

    def body(*refs):
        x_ref, g_ref, shard_ref = refs[0:3]
        h_in = refs[3:3 + nh]
        xo_ref, a_ref, u_ref, h_ref, wall = refs[3 + nh:8 + nh]
        h_out = refs[8 + nh:8 + 2 * nh]
        acc, h_all, wbuf, load_sems, send_sems, recv_sems = refs[8 + 2 * nh:14 + 2 * nh]
        carried_sems = refs[14 + 2 * nh:]
        k, i = pl.program_id(0), pl.program_id(1)
        legs, _ = _gather_legs(shard_ref, wall, send_sems, recv_sems, 0, True)
        begin, pass_on, _ = _gather_steps(legs, True)

        def load(chunk, src):
            return pltpu.make_async_copy(src, wbuf.at[chunk % 2], load_sems.at[chunk % 2])

        @pl.when((k == 0) & (i == 0))
        def _():
            _handshake(peers)
            begin()
            load(0, shard_ref).start()
            load(0, shard_ref).wait()

        @pl.when((k == 1) & (i == 0))
        def _():
            pass_on()
            if hosted is not None:
                hosted.start(h_in, h_out, *carried_sems)
            legs["pass_y"][1].wait_recv()
            load(1, wall.at[PEER_SLOT[1]]).start()
            load(1, wall.at[PEER_SLOT[1]]).wait()

        @pl.when((k == 1) & (i == nt // 2))
        def _():
            legs["pass_x"][1].wait_recv()
            load(2, wall.at[PEER_SLOT[0]]).start()

        @pl.when((k == 2) & (i == 0))
        def _():
            load(2, wall.at[PEER_SLOT[0]]).wait()

        @pl.when((k == 2) & (i == nt // 2))
        def _():
            legs["fwd_y"][1].wait_recv()
            legs["pass_d0"][0].start()
            legs["fwd_x"][1].wait_recv()
            legs["pass_d1"][0].start()
            legs["pass_d0"][1].wait_recv()
            legs["pass_d1"][1].wait_recv()
            load(3, wall.at[PEER_SLOT[2]]).start()
            if hosted is not None:
                hosted.middle(h_in, h_out, *carried_sems)

        @pl.when((k == 3) & (i == 0))
        def _():
            load(3, wall.at[PEER_SLOT[2]]).wait()

        @pl.when(k == 0)
        def _():
            xv = x_ref[...]
            h0 = ((xv * _rms_scale(xv)) * g_ref[...]).astype(BF16)
            h_all[i] = h0
            h_ref[...] = h0

        h = h_all[i]
        wg, wu, wd = (wbuf[k % 2, kind].reshape(FF_SHARD, D_MODEL) for kind in range(3))
        a = _dot_nt(h, wg)
        u = _dot_nt(h, wu)
        a_ref[...] = a.astype(BF16)
        u_ref[...] = u.astype(BF16)
        part = _dot(((a * jax.nn.sigmoid(a)) * u).astype(BF16), wd)

        @pl.when(k == 0)
        def _():
            acc[i] = part

        @pl.when(k > 0)
        def _():
            acc[i] += part

        @pl.when(k == N_CHIPS - 1)
        def _():
            xo_ref[...] = x_ref[...] + 0.5 * acc[i]

        @pl.when((k == N_CHIPS - 1) & (i == nt - 1))
        def _():
            for pair in legs.values():
                pair[0].wait_send()
            if hosted is not None:
                hosted.finish(h_in, h_out, *carried_sems)

    def first_or_last(k):
        return (k == 0) | (k == N_CHIPS - 1)

    tok = lambda keep: pl.BlockSpec((tm, D_MODEL), lambda k, i: (jnp.where(keep(k), i, 0), 0))
    act = pl.BlockSpec((None, tm, FF_SHARD), lambda k, i: (k, i, 0))
    act_shape = jax.ShapeDtypeStruct((N_CHIPS, t, FF_SHARD), BF16)
    carried = [] if hosted is None else [pltpu.SemaphoreType.DMA((hosted.n_sems,))] * 2
    outs = pl.pallas_call(
        body, name=name, grid=(N_CHIPS, nt),
        in_specs=[tok(first_or_last), pl.BlockSpec((1, D_MODEL), lambda k, i: (0, 0)), ANY] + [ANY] * nh,
        out_specs=[tok(lambda k: k == N_CHIPS - 1), act, act,
                   pl.BlockSpec((tm, D_MODEL), lambda k, i: (jnp.where(k == 0, i, nt - 1), 0)), ANY] + [ANY] * nh,
        out_shape=[jax.ShapeDtypeStruct((t, D_MODEL), F32), act_shape, act_shape,
                   jax.ShapeDtypeStruct((t, D_MODEL), BF16),
                   jax.ShapeDtypeStruct((N_CHIPS,) + shard.shape, shard.dtype)]
                  + ([] if hosted is None else hosted.out_shapes),
        scratch_shapes=[pltpu.VMEM((nt, tm, D_MODEL), F32), pltpu.VMEM((nt, tm, D_MODEL), BF16),
                        pltpu.VMEM((2,) + shard.shape, shard.dtype), pltpu.SemaphoreType.DMA((2,)),
                        pltpu.SemaphoreType.DMA((8,)), pltpu.SemaphoreType.DMA((8,))] + carried,
        compiler_params=_with_barrier(_ARB2, peers),
    )(x, g, shard, *([] if hosted is None else hosted.arrays))
    return list(outs[:5]), list(outs[5:])


def _ffn_bwd(dxo, x, g, a4, u4, ffn_w, name, hosted=None):
    t = x.shape[0]
    tm = min(t, TOKEN_TILE)
    steps = (t // tm) * N_CHIPS
    cur, prev = _pipeline_items(steps)


    def body(dxo_ref, dxo_prev_ref, x_ref, g_ref, a_ref, u_ref, wg_ref, wu_ref, wd_ref,
             da_ref, du_ref, hid_ref, dob_ref, dx_ref, dg_ref, acc_ref, da_slots, du_slots):
        s = pl.program_id(0)
        jc, jp = cur(s)[1], prev(s)[1]
        slot = s % 2

        @pl.when(s == 0)
        def _():
            dg_ref[...] = jnp.zeros_like(dg_ref)
            acc_ref[...] = jnp.zeros_like(acc_ref)
            da_slots[...] = jnp.zeros_like(da_slots)
            du_slots[...] = jnp.zeros_like(du_slots)

        @pl.when(jc == 0)
        def _():
            dob_ref[...] = (0.5 * dxo_ref[...]).astype(BF16)

        dhid = _dot_nt(dob_ref[...], wd_ref[...])
        a = a_ref[...].astype(F32)
        u = u_ref[...].astype(F32)
        sg = jax.nn.sigmoid(a)
        sl = a * sg
        hid_ref[...] = (sl * u).astype(BF16)
        du = (dhid * sl).astype(BF16)
        da = (dhid * u * _silu_grad(a, sg)).astype(BF16)
        du_ref[...] = du
        da_ref[...] = da
        acc_ref[...] += _dot(da_slots[1 - slot], wg_ref[...]) + _dot(du_slots[1 - slot], wu_ref[...])
        da_slots[slot] = da
        du_slots[slot] = du

        @pl.when((jp == N_CHIPS - 1) & (s > 0))
        def _():
            dx, dg = _rms_bwd(acc_ref[...], x_ref[...], g_ref[...])
            dx_ref[...] = dxo_prev_ref[...] + dx
            dg_ref[...] += dg
            acc_ref[...] = jnp.zeros_like(acc_ref)

    tok_cur = pl.BlockSpec((tm, D_MODEL), lambda s: (cur(s)[0], 0))
    tok_prev = pl.BlockSpec((tm, D_MODEL), lambda s: (prev(s)[0], 0))
    act = pl.BlockSpec((None, tm, FF_SHARD), lambda s: (cur(s)[1], cur(s)[0], 0))
    row = pl.BlockSpec((1, D_MODEL), lambda s: (0, 0))
    w_arrays, weights = _ffn_weight_operands(ffn_w, [lambda s: prev(s)[1], lambda s: prev(s)[1], lambda s: cur(s)[1]])
    act_shape = jax.ShapeDtypeStruct((N_CHIPS, t, FF_SHARD), BF16)
    return _call(
        body, (dxo, dxo, x, g, a4, u4, *w_arrays), name=name, grid=(steps + 1,),
        in_specs=[tok_cur, tok_prev, tok_prev, row, act, act] + weights,
        out_specs=[act, act, act, tok_cur, tok_prev, row],
        out_shape=[act_shape, act_shape, act_shape,
                   jax.ShapeDtypeStruct((t, D_MODEL), BF16),
                   jax.ShapeDtypeStruct((t, D_MODEL), F32),
                   jax.ShapeDtypeStruct((1, D_MODEL), F32)],
        scratch_shapes=[pltpu.VMEM((tm, D_MODEL), F32), pltpu.VMEM((2, tm, FF_SHARD), BF16),
                        pltpu.VMEM((2, tm, FF_SHARD), BF16)],
        compiler_params=_ARB1, hosted=hosted)


def _matmul_tn(a, b, name, tka=None, out_dtype=F32, hosted=None):
    a3, b3 = a.ndim == 3, b.ndim == 3
    nb = a.shape[0] if a3 else (b.shape[0] if b3 else 1)
    t, ka, n = a.shape[-2], a.shape[-1], b.shape[-1]
    tka = ka if tka is None else tka
    tk = min(t, 4 * TOKEN_TILE)
    nk = t // tk

    def body(a_ref, b_ref, o_ref, acc_ref):
        k = pl.program_id(2)

        @pl.when(k == 0)
        def _():
            acc_ref[...] = jnp.zeros_like(acc_ref)

        acc_ref[...] += _dot_tn(a_ref[...].astype(BF16), b_ref[...].astype(BF16))

        @pl.when(k == nk - 1)
        def _():
            o_ref[...] = acc_ref[...].astype(out_dtype)

    a_spec = (pl.BlockSpec((None, tk, tka), lambda i, j, k: (i, k, j)) if a3
              else pl.BlockSpec((tk, tka), lambda i, j, k: (k, j)))
    b_spec = (pl.BlockSpec((None, tk, n), lambda i, j, k: (i, k, 0)) if b3
              else pl.BlockSpec((tk, n), lambda i, j, k: (k, 0)))
    outs, carried = _call(
        body, (a, b), name=name, grid=(nb, ka // tka, t // tk),
        in_specs=[a_spec, b_spec],
        out_specs=[pl.BlockSpec((None, tka, n), lambda i, j, k: (i, j, 0))],
        out_shape=[jax.ShapeDtypeStruct((nb, ka, n), out_dtype)],
        scratch_shapes=[pltpu.VMEM((tka, n), F32)],
        compiler_params=_ARB3, hosted=hosted)
    return outs[0] if hosted is None else (outs[0], carried)


def _matmul_nt(a, w, name, out_dtype=F32):
    t, k = a.shape
    n = w.shape[0]
    tm = min(t, TOKEN_TILE)

    def body(a_ref, w_ref, o_ref):
        o_ref[...] = _dot_nt(a_ref[...].astype(BF16), w_ref[...]).astype(out_dtype)

    return pl.pallas_call(
        body, name=name, grid=(t // tm,),
        in_specs=[pl.BlockSpec((tm, k), lambda i: (i, 0)), pl.BlockSpec((n, k), lambda i: (0, 0))],
        out_specs=pl.BlockSpec((tm, n), lambda i: (i, 0)),
        out_shape=jax.ShapeDtypeStruct((t, n), out_dtype),
        compiler_params=_ARB1,
    )(a, w)


def _mixer_in_bwd(dproj, w_in_pt, dres, x, g, name):
    t, k = dproj.shape
    tm = min(t, TOKEN_TILE)

    def body(a_ref, w_ref, dres_ref, x_ref, g_ref, dx_ref, dg_ref):
        @pl.when(pl.program_id(0) == 0)
        def _():
            dg_ref[...] = jnp.zeros_like(dg_ref)

        dh = _dot(a_ref[...], w_ref[...])
        dx, dg = _rms_bwd(dh, x_ref[...], g_ref[...])
        dx_ref[...] = dres_ref[...] + dx
        dg_ref[...] += dg

    tok = pl.BlockSpec((tm, D_MODEL), lambda i: (i, 0))
    row = pl.BlockSpec((1, D_MODEL), lambda i: (0, 0))
    return pl.pallas_call(
        body, name=name, grid=(t // tm,),
        in_specs=[pl.BlockSpec((tm, k), lambda i: (i, 0)), pl.BlockSpec((k, D_MODEL), lambda i: (0, 0)), tok, tok, row],
        out_specs=[tok, row],
        out_shape=[jax.ShapeDtypeStruct((t, D_MODEL), F32), jax.ShapeDtypeStruct((1, D_MODEL), F32)],
        compiler_params=_ARB1,
    )(dproj, w_in_pt, dres, x, g)


def _mixer_in_fwd(x, g, w_in_pt, name, hosted=None):
    t = x.shape[0]
    tm = min(t, TOKEN_TILE)
    tn = PROJ_P // PROJ_TILES

    def body(x_ref, g_ref, w_ref, p_ref, h_ref):
        @pl.when(pl.program_id(1) == 0)
        def _():
            xv = x_ref[...]
            h_ref[...] = ((xv * _rms_scale(xv)) * g_ref[...]).astype(BF16)

        p_ref[...] = _dot_nt(h_ref[...], w_ref[...])

    tok = pl.BlockSpec((tm, D_MODEL), lambda i, j: (i, 0))
    return _call(
        body, (x, g, w_in_pt), name=name, grid=(t // tm, PROJ_TILES),
        in_specs=[tok, pl.BlockSpec((1, D_MODEL), lambda i, j: (0, 0)),
                  pl.BlockSpec((tn, D_MODEL), lambda i, j: (j, 0))],
        out_specs=[pl.BlockSpec((tm, tn), lambda i, j: (i, j)), tok],
        out_shape=[jax.ShapeDtypeStruct((t, PROJ_P), F32), jax.ShapeDtypeStruct((t, D_MODEL), BF16)],
        scratch_shapes=[], compiler_params=_ARB2, hosted=hosted)


def _mixer_out_fwd(o_ret, o_gla, w_out, x, name):
    t = x.shape[0]
    tm = min(t, TOKEN_TILE)
    half = HEADS * LANES

    def body(a_ref, b_ref, w_ref, x_ref, o_ref):
        o_ref[...] = x_ref[...] + _dot(a_ref[...], w_ref[0:half, :]) + _dot(b_ref[...], w_ref[half:2 * half, :])

    tok = pl.BlockSpec((tm, D_MODEL), lambda i: (i, 0))
    hb = pl.BlockSpec((tm, half), lambda i: (i, 0))
    return pl.pallas_call(
        body, name=name, grid=(t // tm,),
        in_specs=[hb, hb, pl.BlockSpec((2 * half, D_MODEL), lambda i: (0, 0)), tok],
        out_specs=tok, out_shape=jax.ShapeDtypeStruct((t, D_MODEL), F32),
        compiler_params=_ARB1,
    )(o_ret, o_gla, w_out, x)


def _rot(v, cos, sa, sb):
    return v * cos + pltpu.roll(v, 96, 1) * sa + pltpu.roll(v, 32, 1) * sb


def _rot_t(d, cos, sa, sb):
    return d * cos + pltpu.roll(d * sa, 32, 1) + pltpu.roll(d * sb, 96, 1)


def _bmm(a, b):
    return jnp.einsum("cik,ckj->cij", a, b, preferred_element_type=F32)


def _bmm_nt(a, b):
    return jnp.einsum("cik,cjk->cij", a, b, preferred_element_type=F32)


def _bmm_tn(a, b):
    return jnp.einsum("cki,ckj->cij", a, b, preferred_element_type=F32)


def _masked_sum(mask, x):
    hi = x.astype(BF16)
    r1 = x - hi.astype(F32)
    mid = r1.astype(BF16)
    lo = (r1 - mid.astype(F32)).astype(BF16)
    return _bmm(mask, hi) + _bmm(mask, mid) + _bmm(mask, lo)


PAIR = 2


def _tile_inputs(is_ret, qkvg_refs, aux, nc):
    shape3 = (nc, CHUNK, LANES)
    q_ref, k_ref, v_ref, g_ref = qkvg_refs
    low_lanes = lax.broadcasted_iota(jnp.int32, (1, LANES), 1) < 64
    ri = lax.broadcasted_iota(jnp.int32, (PAIR * nc, CHUNK, CHUNK), 1)
    ci = lax.broadcasted_iota(jnp.int32, (PAIR * nc, CHUNK, CHUNK), 2)
    qs, ks, vs, bs, gates, extra = [], [], [], [], [], []
    for hd in range(PAIR):
        q_blk, k_blk = q_ref[...], k_ref[...]
        if hd == 1:
            q_blk, k_blk = pltpu.roll(q_blk, 64, 1), pltpu.roll(k_blk, 64, 1)
        q_raw, k_raw = jnp.where(low_lanes, q_blk, 0.0), jnp.where(low_lanes, k_blk, 0.0)
        vs.append(v_ref[:, LANES * hd:LANES * (hd + 1)].reshape(shape3))
        gates.append(g_ref[:, LANES * hd:LANES * (hd + 1)])
        if is_ret:
            cos_ref, sa_ref, sb_ref, lg_ref = aux
            cos, sa, sb = cos_ref[...], sa_ref[...], sb_ref[...]
            q = _rot(q_raw, cos, sa, sb)
            k = _rot(k_raw, cos, sa, sb) * QK_SCALE
            steps = (lax.broadcasted_iota(jnp.int32, shape3, 1) + 1).astype(F32)
            bs.append(steps * lg_ref[hd])
            extra.append(jnp.exp(jnp.abs(ri[0:nc] - ci[0:nc]).astype(F32) * lg_ref[hd][:, 0:CHUNK]))
        else:
            glow_ref, wa2_ref, ba_ref = aux
            lanes = slice(LANES * hd, LANES * (hd + 1))
            logit = _dot(glow_ref[...].astype(BF16), wa2_ref[:, lanes]) + ba_ref[:, lanes]
            la = (jnp.minimum(logit, 0.0) - jnp.log1p(jnp.exp(-jnp.abs(logit)))) * (1.0 / GATE_NORM)
            bs.append(_masked_sum((ci[0:nc] <= ri[0:nc]).astype(BF16), la.reshape(shape3)))
            extra.append(logit)
            q = q_raw * QK_SCALE
            k = k_raw
        qs.append(q.reshape(shape3))
        ks.append(k.reshape(shape3))
    cat = lambda parts: jnp.concatenate(parts, axis=0)
    return cat(qs), cat(ks), cat(vs), gates, cat(bs), extra, ri, ci


def _tile_scores(q, k, b, ri, ci):
    mid = b[:, CHUNK // 2 - 1:CHUNK // 2, :]
    ep = jnp.exp(b - mid)
    en = jnp.exp(mid - b)
    qt, kt, qh, kh = q * ep, k * en, q * en, k * ep
    low = _bmm_nt(qt.astype(BF16), kt.astype(BF16))
    upp = _bmm_nt(qh.astype(BF16), kh.astype(BF16))
    scores = jnp.where(ci <= ri, low, upp)
    return scores, ep, en, qt, kt, qh, kh


def _attn_specs(is_ret, t, tb, imap_t):
    nb = t // tb
    base = 0 if is_ret else 12
    wide = PAIR * LANES
    proj = [pl.BlockSpec((tb, LANES), lambda p, i: (imap_t(i), base + p)),
            pl.BlockSpec((tb, LANES), lambda p, i: (imap_t(i), base + 2 + p)),
            pl.BlockSpec((tb, wide), lambda p, i: (imap_t(i), (base + 4) // 2 + p)),
            pl.BlockSpec((tb, wide), lambda p, i: (imap_t(i), (base + 8) // 2 + p))]
    lane_t = pl.BlockSpec((tb, LANES), lambda p, i: (imap_t(i), 0))
    if is_ret:
        aux = [lane_t, lane_t, lane_t, pl.BlockSpec((PAIR, 1, LANES), lambda p, i: (p, 0, 0))]
    else:
        aux = [pl.BlockSpec((tb, LANES), lambda p, i: (imap_t(i), PROJ_P // LANES - 1)),
               pl.BlockSpec((LANES, wide), lambda p, i: (0, p)),
               pl.BlockSpec((1, wide), lambda p, i: (0, p))]
    gain = pl.BlockSpec((1, wide), lambda p, i: (0, p))
    pair_t = pl.BlockSpec((tb, wide), lambda p, i: (imap_t(i), p))
    narrow_t = pl.BlockSpec((tb, LANES), lambda p, i: (imap_t(i), p))
    state = pl.BlockSpec((PAIR, tb // CHUNK, LANES, LANES), lambda p, i: (p, imap_t(i), 0, 0))
    return nb, proj, aux, gain, pair_t, narrow_t, state


def _attn_fwd(is_ret, proj, aux_arrays, gain, name, hosted=None):
    t = proj.shape[0]
    tb = min(t, ATTN_TILE)
    nc = tb // CHUNK
    n_aux = 4 if is_ret else 3
    nb, proj_spec, aux_specs, gain_spec, pair_t, _, state_spec = _attn_specs(is_ret, t, tb, lambda i: i)

    def body(*refs):
        qkvg_refs = refs[0:4]
        aux = refs[4:4 + n_aux]
        gn_ref, ofin_ref, oraw_ref, st_ref, state = refs[4 + n_aux:]

        @pl.when(pl.program_id(1) == 0)
        def _():
            state[...] = jnp.zeros_like(state)

        q, k, v, gates, b, extra, ri, ci = _tile_inputs(is_ret, qkvg_refs, aux, nc)
        if is_ret:
            scores = _bmm_nt(q.astype(BF16), k.astype(BF16)) * jnp.concatenate(extra, axis=0)
        else:
            scores = _tile_scores(q, k, b, ri, ci)[0]
        vb = v.astype(BF16)
        intra = _bmm(scores.astype(BF16), vb)
        b_last = b[:, CHUNK - 1:CHUNK, :]
        e_last = jnp.exp(b_last)
        grow = _bmm_tn(vb, (k * jnp.exp(b_last - b)).astype(BF16))
        for hd in range(PAIR):
            st = state[hd]
            for c in range(nc):
                st_ref[hd, c] = st
                st = st * e_last[hd * nc + c] + grow[hd * nc + c]
            state[hd] = st
        starts = st_ref[...].reshape(PAIR * nc, LANES, LANES)
        out3 = intra + _bmm_nt((q * jnp.exp(b)).astype(BF16), starts.astype(BF16))
        for hd in range(PAIR):
            lanes = slice(LANES * hd, LANES * (hd + 1))
            out = out3[hd * nc:(hd + 1) * nc].reshape(tb, LANES)
            oraw_ref[:, lanes] = out
            normed = out * _rms_scale(out)
            gate = gates[hd]
            ofin_ref[:, lanes] = ((normed * gn_ref[:, lanes]) * (gate * jax.nn.sigmoid(gate))).astype(BF16)

    width = HEADS * LANES
    return _call(
        body, (proj, proj, proj, proj, *aux_arrays, gain), name=name, grid=(HEADS // PAIR, nb),
        in_specs=proj_spec + aux_specs + [gain_spec],
        out_specs=[pair_t, pair_t, state_spec],
        out_shape=[jax.ShapeDtypeStruct((t, width), BF16), jax.ShapeDtypeStruct((t, width), F32),
                   jax.ShapeDtypeStruct((HEADS, t // CHUNK, LANES, LANES), F32)],
        scratch_shapes=[pltpu.VMEM((PAIR, LANES, LANES), F32)],
        compiler_params=_ARB2, hosted=hosted)


def _attn_bwd(is_ret, proj, aux_arrays, gain, o_raw, states, d_out, name, hosted=None):
    t = proj.shape[0]
    tb = min(t, ATTN_TILE)
    nc = tb // CHUNK
    n_aux = 4 if is_ret else 3
    nblk = t // tb
    nb, proj_spec, aux_specs, gain_spec, pair_t, narrow_t, state_spec = _attn_specs(
        is_ret, t, tb, lambda i: nblk - 1 - i)
    base = 0 if is_ret else HEADS // PAIR
    dout_spec = pl.BlockSpec((tb, PAIR * LANES), lambda p, i: (nblk - 1 - i, base + p))

    def body(*refs):
        qkvg_refs = refs[0:4]
        aux = refs[4:4 + n_aux]
        gn_ref, oraw_ref, st_ref, dfin_ref = refs[4 + n_aux:8 + n_aux]
        dq_ref, dk_ref, dv_ref, dgate_ref, dgn_ref = refs[8 + n_aux:13 + n_aux]
        if is_ret:
            dstate, dafter_ref = refs[13 + n_aux:]
        else:
            dlogit_ref, dba_ref, dstate, dafter_ref = refs[13 + n_aux:]

        @pl.when(pl.program_id(1) == 0)
        def _():
            dstate[...] = jnp.zeros_like(dstate)
            dgn_ref[...] = jnp.zeros_like(dgn_ref)
            if not is_ret:
                dba_ref[...] = jnp.zeros_like(dba_ref)

        shape3 = (nc, CHUNK, LANES)
        q, k, v, gates, b, extra, ri, ci = _tile_inputs(is_ret, qkvg_refs, aux, nc)
        eb = jnp.exp(b)
        qe = q * eb
        b_last = b[:, CHUNK - 1:CHUNK, :]
        e_last = jnp.exp(b_last)
        ekd = jnp.exp(b_last - b)
        kd = k * ekd

        d_os = []
        for hd in range(PAIR):
            lanes = slice(LANES * hd, LANES * (hd + 1))
            gn, gate = gn_ref[:, lanes], gates[hd]
            out = oraw_ref[:, lanes]
            r = _rms_scale(out)
            normed = out * r
            sg = jax.nn.sigmoid(gate)
            dfin = dfin_ref[:, lanes]
            dgate_ref[:, lanes] = (dfin * (normed * gn) * _silu_grad(gate, sg)).astype(BF16)
            dpre = dfin * (gate * sg)
            dgn_ref[:, lanes] += jnp.sum(dpre * normed, axis=0, keepdims=True)
            dnormed = dpre * gn
            d_o = r * (dnormed - normed * jnp.mean(dnormed * normed, axis=-1, keepdims=True))
            d_os.append(d_o.reshape(shape3))
        dob, vb = jnp.concatenate(d_os, axis=0).astype(BF16), v.astype(BF16)

        dgrow = _bmm_tn(dob, qe.astype(BF16))
        for hd in range(PAIR):
            dst = dstate[hd]
            for c in reversed(range(nc)):
                dafter_ref[hd * nc + c] = dst
                dst = dst * e_last[hd * nc + c] + dgrow[hd * nc + c]
            dstate[hd] = dst
        st = st_ref[...].reshape(PAIR * nc, LANES, LANES)
        dafter = dafter_ref[...]
        stb, dafter_b = st.astype(BF16), dafter.astype(BF16)

        dsc = _bmm_nt(dob, vb)
        dsc_t = _bmm_nt(vb, dob)
        dqe = _bmm(dob, stb)
        dkd = _bmm(vb, dafter_b)
        if is_ret:
            decay, qb, kb = jnp.concatenate(extra, axis=0), q.astype(BF16), k.astype(BF16)
            scores_t = _bmm_nt(kb, qb) * decay
            dq = _bmm((dsc * decay).astype(BF16), kb) + dqe * eb
            dk = _bmm((dsc_t * decay).astype(BF16), qb) + dkd * ekd
        else:
            _, ep, en, qt, kt, qh, kh = _tile_scores(q, k, b, ri, ci)
            qtb, ktb, qhb, khb = qt.astype(BF16), kt.astype(BF16), qh.astype(BF16), kh.astype(BF16)
            scores_t = jnp.where(ci >= ri, _bmm_nt(ktb, qtb), _bmm_nt(khb, qhb))
            dqt = _bmm(jnp.where(ci <= ri, dsc, 0.0).astype(BF16), ktb)
            dqh = _bmm(jnp.where(ci <= ri, 0.0, dsc).astype(BF16), khb)
            dkt = _bmm(jnp.where(ci >= ri, dsc_t, 0.0).astype(BF16), qtb)
            dkh = _bmm(jnp.where(ci >= ri, 0.0, dsc_t).astype(BF16), qhb)
            dq = dqt * ep + dqh * en + dqe * eb
            dk = dkt * en + dkh * ep + dkd * ekd
        dv = _bmm(scores_t.astype(BF16), dob) + _bmm_nt(kd.astype(BF16), dafter_b)

        if not is_ret:
            db = dqt * qt - dkt * kt - dqh * qh + dkh * kh + dqe * qe - dkd * kd
            db_last = (jnp.sum(dkd * kd, axis=1, keepdims=True)
                       + jnp.sum(dafter * st, axis=1, keepdims=True) * e_last)
            last_row = lax.broadcasted_iota(jnp.int32, (PAIR * nc, CHUNK, LANES), 1) == CHUNK - 1
            db = db + jnp.where(last_row, db_last, 0.0)
            dla = _masked_sum((ci >= ri).astype(BF16), db)

        dq_pair, dk_pair = [], []
        for hd in range(PAIR):
            lanes = slice(LANES * hd, LANES * (hd + 1))
            rows3 = slice(hd * nc, (hd + 1) * nc)
            dq_h, dk_h = dq[rows3].reshape(tb, LANES), dk[rows3].reshape(tb, LANES)
            if is_ret:
                cos_ref, sa_ref, sb_ref, _ = aux
                cos, sa, sb = cos_ref[...], sa_ref[...], sb_ref[...]
                dq_h = _rot_t(dq_h, cos, sa, sb)
                dk_h = _rot_t(dk_h, cos, sa, sb) * QK_SCALE
            else:
                dq_h = dq_h * QK_SCALE
                dlogit = dla[rows3].reshape(tb, LANES) * (1.0 / GATE_NORM) * jax.nn.sigmoid(-extra[hd])
                dlogit_ref[:, lanes] = dlogit.astype(BF16)
                dba_ref[:, lanes] += jnp.sum(dlogit, axis=0, keepdims=True)
            dq_pair.append(dq_h)
            dk_pair.append(dk_h)
            dv_ref[:, lanes] = dv[rows3].reshape(tb, LANES).astype(BF16)
        dq_ref[...] = (dq_pair[0] + pltpu.roll(dq_pair[1], 64, 1)).astype(BF16)
        dk_ref[...] = (dk_pair[0] + pltpu.roll(dk_pair[1], 64, 1)).astype(BF16)

    width = HEADS * LANES
    row_out = pl.BlockSpec((1, PAIR * LANES), lambda p, i: (0, p))
    out_specs = [narrow_t, narrow_t, pair_t, pair_t, row_out]
    out_shape = ([jax.ShapeDtypeStruct((t, width // 2), BF16)] * 2 + [jax.ShapeDtypeStruct((t, width), BF16)] * 2
                 + [jax.ShapeDtypeStruct((1, width), F32)])
    if not is_ret:
        out_specs += [pair_t, row_out]
        out_shape += [jax.ShapeDtypeStruct((t, width), BF16), jax.ShapeDtypeStruct((1, width), F32)]
    return _call(
        body, (proj, proj, proj, proj, *aux_arrays, gain, o_raw, states, d_out), name=name,
        grid=(HEADS // PAIR, nblk),
        in_specs=proj_spec + aux_specs + [gain_spec, pair_t, state_spec, dout_spec],
        out_specs=out_specs, out_shape=out_shape,
        scratch_shapes=[pltpu.VMEM((PAIR, LANES, LANES), F32), pltpu.VMEM((PAIR * nc, LANES, LANES), F32)],
        compiler_params=_ARB2, hosted=hosted)


PEER_SLOT = (2, 1, 3)


def _place():
    x, y, c = lax.axis_index("x"), lax.axis_index("y"), lax.axis_index("c")
    chips = [(1 - x, y), (x, 1 - y), (1 - x, 1 - y)]
    return x, y, c, 2 * x + y, chips


def _route_split(rows, dtype):
    tile = 16 if dtype == BF16 else 8
    if rows < 2 * tile:
        return None
    return -(-(rows // 2) // tile) * tile


def _routes(by_peer):
    x, y, c, me, chips = _place()
    (xx, xy), (yx, yy), (dx, dy) = chips
    if by_peer:
        slots = dict(own=0, from_x=PEER_SLOT[0], from_y=PEER_SLOT[1], diag=PEER_SLOT[2],
                     mine_on_x=PEER_SLOT[0], mine_on_y=PEER_SLOT[1])
    else:
        slots = dict(own=me, from_x=2 * xx + xy, from_y=2 * yx + yy, diag=2 * dx + dy, mine_on_x=me, mine_on_y=me)
    return c, (xx, xy, c), (yx, yy, c), (dx, dy, c), (x, y, 1 - c), slots


def _gather_legs(src, out, send_sems, recv_sems, base, by_peer):
    c, to_x, to_y, to_d, sibling, s = _routes(by_peer)
    r0 = _route_split(src.shape[2], src.dtype)

    def cp(k, src_ref, dst_ref, to):
        return pltpu.make_async_remote_copy(src_ref=src_ref, dst_ref=dst_ref, send_sem=send_sems.at[base + k],
                                            recv_sem=recv_sems.at[base + k], device_id=to, device_id_type=MESH)

    mine = src.at[:, c]
    legs = dict(
        x=(cp(0, mine, out.at[s["mine_on_x"], :, c], to_x), cp(0, mine, out.at[s["from_x"], :, c], to_x)),
        y=(cp(1, mine, out.at[s["mine_on_y"], :, c], to_y), cp(1, mine, out.at[s["from_y"], :, c], to_y)),
        pass_x=(cp(4, out.at[s["from_x"], :, c], out.at[s["from_x"], :, c], sibling),
                cp(4, mine, out.at[s["from_x"], :, 1 - c], sibling)),
        pass_y=(cp(5, out.at[s["from_y"], :, c], out.at[s["from_y"], :, c], sibling),
                cp(5, mine, out.at[s["from_y"], :, 1 - c], sibling)))
    if r0 is None:
        mine_on_d = s["diag"] if by_peer else s["own"]
        legs["d"] = (cp(2, mine, out.at[mine_on_d, :, c], to_d), cp(2, mine, out.at[s["diag"], :, c], to_d))
        legs["pass_d"] = (cp(6, out.at[s["diag"], :, c], out.at[s["diag"], :, c], sibling),
                          cp(6, mine, out.at[s["diag"], :, 1 - c], sibling))
        return legs, False
    lo, hi = pl.ds(0, r0), pl.ds(r0, src.shape[2] - r0)
    fx_on_y = s["diag"] if by_peer else s["from_x"]
    fy_on_x = s["diag"] if by_peer else s["from_y"]
    legs.update(
        fwd_y=(cp(2, out.at[s["from_x"], :, c, lo], out.at[fx_on_y, :, c, lo], to_y),
               cp(2, mine.at[:, lo], out.at[s["diag"], :, c, lo], to_y)),
        fwd_x=(cp(3, out.at[s["from_y"], :, c, hi], out.at[fy_on_x, :, c, hi], to_x),
               cp(3, mine.at[:, hi], out.at[s["diag"], :, c, hi], to_x)),
        pass_d0=(cp(6, out.at[s["diag"], :, c, lo], out.at[s["diag"], :, c, lo], sibling),
                 cp(6, mine.at[:, lo], out.at[s["diag"], :, 1 - c, lo], sibling)),
        pass_d1=(cp(7, out.at[s["diag"], :, c, hi], out.at[s["diag"], :, c, hi], sibling),
                 cp(7, mine.at[:, hi], out.at[s["diag"], :, 1 - c, hi], sibling)))
    return legs, True


def _gather_steps(legs, routed):
    def start():
        legs["x"][0].start()
        legs["y"][0].start()
        if not routed:
            legs["d"][0].start()

    def middle():
        legs["x"][1].wait_recv()
        if routed:
            legs["fwd_y"][0].start()
        legs["pass_x"][0].start()
        legs["y"][1].wait_recv()
        if routed:
            legs["fwd_x"][0].start()
        legs["pass_y"][0].start()

    def finish():
        last = ["pass_d0", "pass_d1"] if routed else ["pass_d"]
        if routed:
            legs["fwd_y"][1].wait_recv()
            legs["pass_d0"][0].start()
            legs["fwd_x"][1].wait_recv()
            legs["pass_d1"][0].start()
        else:
            legs["d"][1].wait_recv()
            legs["pass_d"][0].start()
        for name in ["pass_x", "pass_y"] + last:
            legs[name][1].wait_recv()
        for name in ["x", "y", "pass_x", "pass_y"] + last + (["fwd_y", "fwd_x"] if routed else ["d"]):
            legs[name][0].wait_send()

    return start, middle, finish


def _gather_plan(arrs):
    na = len(arrs)

    def steps(ins, outs, send_sems, recv_sems):
        return [_gather_steps(*_gather_legs(ins[a], outs[a], send_sems, recv_sems, 8 * a, False)) for a in range(na)]

    def run(which):
        def hook(*refs):
            for step in steps(*refs):
                step[which]()
        return hook

    routed = all(_route_split(a.shape[2], a.dtype) is not None for a in arrs)
    return _Hosted(arrs, [jax.ShapeDtypeStruct((N_CHIPS,) + a.shape, a.dtype) for a in arrs], 8 * na,
                   run(0), run(2), middle=run(1), peers="neighbours" if routed else "chips_sibling")


def _pair_exchange_plan(grads):
    na = len(grads)

    def copies(ins, outs, send_sems, recv_sems):
        x, y, c, _, _ = _place()
        return [pltpu.make_async_remote_copy(
            src_ref=ins[a].at[:, 1 - c], dst_ref=outs[a], send_sem=send_sems.at[a], recv_sem=recv_sems.at[a],
            device_id=(x, y, 1 - c), device_id_type=MESH) for a in range(na)]

    def start(*refs):
        for cp in copies(*refs):
            cp.start()

    def finish(*refs):
        for cp in copies(*refs):
            cp.wait()

    return _Hosted(grads, [jax.ShapeDtypeStruct(g.shape[:1] + g.shape[2:], g.dtype) for g in grads], na, start, finish,
                   peers="sibling")


def _small_gather_plan(block):
    def copies(ins, outs, send_sems, recv_sems):
        x, y, c, _, chips = _place()
        peers = [(x, y, 1 - c)] + [(px, py, pc) for px, py in chips for pc in (c, 1 - c)]
        sends = [pltpu.make_async_remote_copy(
            src_ref=ins[0], dst_ref=outs[0].at[4 * x + 2 * y + c], send_sem=send_sems.at[k], recv_sem=recv_sems.at[k],
            device_id=peer, device_id_type=MESH) for k, peer in enumerate(peers)]
        recvs = [pltpu.make_async_remote_copy(
            src_ref=ins[0], dst_ref=outs[0].at[4 * px + 2 * py + pc], send_sem=send_sems.at[k], recv_sem=recv_sems.at[k],
            device_id=(px, py, pc), device_id_type=MESH) for k, (px, py, pc) in enumerate(peers)]
        return sends, recvs

    def start(*refs):
        for cp in copies(*refs)[0]:
            cp.start()

    def finish(*refs):
        sends, recvs = copies(*refs)
        for cp in recvs:
            cp.wait_recv()
        for cp in sends:
            cp.wait_send()

    return _Hosted([block], [jax.ShapeDtypeStruct((8,) + block.shape, block.dtype)], 7, start, finish)


def _sum_devices(blocks):
    def body(b_ref, o_ref):
        acc = b_ref[0]
        for d in range(1, 8):
            acc = acc + b_ref[d]
        o_ref[...] = acc

    return pl.pallas_call(body, name="sum_devices", in_specs=[_VMEM], out_specs=_VMEM,
                          out_shape=jax.ShapeDtypeStruct(blocks.shape[1:], blocks.dtype))(blocks)


def _pair_add(grad, recv, c_arr, name):
    _, _, r, cols = grad.shape

    def body(c_ref, g_ref, r_ref, o_ref):
        o_ref[...] = (g_ref[...].astype(F32) + r_ref[...].astype(F32)).astype(BF16)

    return pl.pallas_call(
        body, name=name,
        grid_spec=pltpu.PrefetchScalarGridSpec(
            num_scalar_prefetch=1, grid=(N_CHIPS,),
            in_specs=[pl.BlockSpec((None, None, r, cols), lambda p, c_ref: (p, c_ref[0], 0, 0)),
                      pl.BlockSpec((None, r, cols), lambda p, c_ref: (p, 0, 0))],
            out_specs=pl.BlockSpec((None, r, cols), lambda p, c_ref: (p, 0, 0))),
        out_shape=jax.ShapeDtypeStruct((N_CHIPS, r, cols), BF16),
        compiler_params=_ARB1,
    )(c_arr, grad, recv)


def _chip_exchange_plan(sums, by_peer=False):
    na = len(sums)

    def copies(ins, outs, send_sems, recv_sems):
        x, y, c, me, chips = _place()

        def copy(a, j, px, py, block, slot):
            return pltpu.make_async_remote_copy(
                src_ref=ins[a].at[block], dst_ref=outs[a].at[slot],
                send_sem=send_sems.at[3 * a + j], recv_sem=recv_sems.at[3 * a + j],
                device_id=(px, py, c), device_id_type=MESH)

        peers = [(a, j, px, py) for a in range(na) for j, (px, py) in enumerate(chips)]
        return me, peers, copy

    def start(*refs):
        me, peers, copy = copies(*refs)
        for a, j, px, py in peers:
            if by_peer:
                copy(a, j, px, py, PEER_SLOT[j], PEER_SLOT[j]).start()
            else:
                copy(a, j, px, py, 2 * px + py, me).start()

    def finish(*refs):
        me, peers, copy = copies(*refs)
        for a, j, px, py in peers:
            if by_peer:
                copy(a, j, px, py, PEER_SLOT[j], PEER_SLOT[j]).wait_recv()
            else:
                copy(a, j, px, py, me, 2 * px + py).wait_recv()
        for a, j, px, py in peers:
            if by_peer:
                copy(a, j, px, py, PEER_SLOT[j], PEER_SLOT[j]).wait_send()
            else:
                copy(a, j, px, py, 2 * px + py, me).wait_send()

    return _Hosted(sums, [jax.ShapeDtypeStruct(s.shape, s.dtype) for s in sums], 3 * na, start, finish, peers="chips")


def _chip_sum(own, recv, me_arr, name):
    _, r, cols = recv.shape

    def body(me_ref, own_ref, r_ref, o_ref):
        o_ref[...] = jnp.zeros_like(o_ref)
        for q in range(N_CHIPS):
            @pl.when(me_ref[0] == q)
            def _():
                o_ref[...] += own_ref[...].astype(F32)

            @pl.when(me_ref[0] != q)
            def _():
                o_ref[...] += r_ref[q].astype(F32)

    return pl.pallas_call(
        body, name=name,
        grid_spec=pltpu.PrefetchScalarGridSpec(
            num_scalar_prefetch=1, grid=(1,),
            in_specs=[pl.BlockSpec((None, r, cols), lambda i, me_ref: (me_ref[0], 0, 0)),
                      pl.BlockSpec((N_CHIPS, r, cols), lambda i, me_ref: (0, 0, 0))],
            out_specs=pl.BlockSpec((r, cols), lambda i, me_ref: (0, 0))),
        out_shape=jax.ShapeDtypeStruct((r, cols), F32),
        compiler_params=_ARB1,
    )(me_arr, own, recv)


def _peer_sum(own, recv, name):
    _, r, cols = recv.shape

    def body(own_ref, r_ref, o_ref):
        acc = own_ref[...].astype(F32) + r_ref[1].astype(F32)
        acc = acc + r_ref[2].astype(F32)
        o_ref[...] = acc + r_ref[3].astype(F32)

    return pl.pallas_call(
        body, name=name, grid=(1,),
        in_specs=[pl.BlockSpec((None, r, cols), lambda i: (0, 0, 0)), pl.BlockSpec((N_CHIPS, r, cols), lambda i: (0, 0, 0))],
        out_specs=pl.BlockSpec((r, cols), lambda i: (0, 0)),
        out_shape=jax.ShapeDtypeStruct((r, cols), F32),
        compiler_params=_ARB1,
    )(own, recv)


def _pair_share_plan(halves):
    na = len(halves)

    def copies(ins, outs, send_sems, recv_sems):
        x, y, c, _, _ = _place()
        return [pltpu.make_async_remote_copy(
            src_ref=ins[a], dst_ref=outs[a], send_sem=send_sems.at[a], recv_sem=recv_sems.at[a],
            device_id=(x, y, 1 - c), device_id_type=MESH) for a in range(na)]

    def start(*refs):
        for cp in copies(*refs):
            cp.start()

    def finish(*refs):
        for cp in copies(*refs):
            cp.wait()

    return _Hosted(halves, [jax.ShapeDtypeStruct(h.shape, h.dtype) for h in halves], na, start, finish,
                   peers="sibling")


def _row_tile(rows):
    best = rows
    for cand in range(8, min(rows, 512) + 1, 8):
        if rows % cand == 0:
            best = cand
    return best


def _adamw_math(w, g, m, v):
    m2 = ADAM_B1 * m + (1.0 - ADAM_B1) * g
    v2 = ADAM_B2 * v + (1.0 - ADAM_B2) * (g * g)
    m_hat = m2 / (1.0 - ADAM_B1 ** ADAM_STEP)
    v_hat = v2 / (1.0 - ADAM_B2 ** ADAM_STEP)
    return -ADAM_LR * (m_hat / (jnp.sqrt(v_hat) + ADAM_EPS) + ADAM_WD * w), m2, v2


def _adamw_halves(w, g_mine, g_other, m, v, c_arr, name):
    rows, cols = w.shape
    r = rows // 2
    tr = _row_tile(r)
    nt = r // tr

    def body(c_ref, w_ref, gm_ref, go_ref, m_ref, v_ref, g_ref, d_ref, nm_ref, nv_ref):
        gv = jnp.where(pl.program_id(0) == c_ref[0], gm_ref[...], go_ref[...])
        g_ref[...] = gv
        d_ref[...], nm_ref[...], nv_ref[...] = _adamw_math(w_ref[...], gv, m_ref[...], v_ref[...])

    full = pl.BlockSpec((tr, cols), lambda h, i, c_ref: (h * nt + i, 0))
    half = pl.BlockSpec((tr, cols), lambda h, i, c_ref: (i, 0))
    shape = jax.ShapeDtypeStruct((rows, cols), F32)
    return pl.pallas_call(
        body, name=name,
        grid_spec=pltpu.PrefetchScalarGridSpec(
            num_scalar_prefetch=1, grid=(2, nt),
            in_specs=[full, half, half, full, full], out_specs=[full] * 4),
        out_shape=[shape] * 4,
        compiler_params=_ARB2,
    )(c_arr, w, g_mine, g_other, m, v)


def _pad_w_in_t(w_in_t):
    return jnp.pad(w_in_t, ((0, PROJ_P - IN_WIDTH), (0, 0)))


def _unpad_w_in_t(w_pt):
    return w_pt[0:IN_WIDTH]


def _rope_tables(t):
    half = 32
    inv = ROPE_BASE ** (-jnp.arange(half, dtype=F32) * 2.0 / 64)
    ang = jnp.arange(t, dtype=F32)[:, None] * inv[None, :]
    cos, sin = jnp.cos(ang), jnp.sin(ang)
    z32, z64 = jnp.zeros((t, 32), F32), jnp.zeros((t, 64), F32)
    return (jnp.concatenate([cos, cos, z64], axis=1),
            jnp.concatenate([-sin, z32, z64], axis=1),
            jnp.concatenate([z32, sin, z64], axis=1))


def _halves(w):
    n, rows, cols = w.shape
    return w.reshape(n, 2, rows // 2, cols)


_VMEM = pl.BlockSpec(memory_space=pltpu.VMEM)


def _pack_small(n1, nm, n2, nf, nret, ngla, ba, wa2_p, loss_blk):
    def body(n1_ref, nm_ref, n2_ref, nf_ref, nret_ref, ngla_ref, ba_ref, wa2_ref, loss_ref, o_ref):
        o_ref[...] = jnp.zeros_like(o_ref)
        o_ref[0:1, :] = n1_ref[...]
        o_ref[1:2, :] = nm_ref[...]
        o_ref[2:3, :] = n2_ref[...]
        o_ref[3:4, :] = nf_ref[...]
        o_ref[4:5, 0:512] = nret_ref[...]
        o_ref[4:5, 512:1024] = ngla_ref[...]
        o_ref[5:6, 0:256] = ba_ref[...]
        o_ref[6:7, 0:LANES] = loss_ref[0:1, :]
        o_ref[8:8 + GATE_RANK, 0:HEADS * LANES] = wa2_ref[0:GATE_RANK, :]

    return pl.pallas_call(
        body, name="pack_small", in_specs=[_VMEM] * 9, out_specs=_VMEM,
        out_shape=jax.ShapeDtypeStruct((SMALL_ROWS, D_MODEL), F32),
    )(n1, nm, n2, nf, nret, ngla, ba, wa2_p, loss_blk)


def _small_update(summed, chip_arr, ws, ms, vs):
    n = len(ws)

    def body(chip_ref, s_ref, *refs):
        w_refs, m_refs, v_refs = refs[0:n], refs[n:2 * n], refs[2 * n:3 * n]
        outs = refs[3 * n:]
        wa2_all = s_ref[8:8 + GATE_RANK, 0:HEADS * LANES]
        wa2_g = jnp.zeros((GATE_RANK, 64), F32)
        for p in range(N_CHIPS):
            wa2_g = jnp.where(chip_ref[0] == p, wa2_all[:, LANES * p:LANES * p + 64], wa2_g)
        grads = [s_ref[0:1, :], s_ref[1:2, :], s_ref[2:3, :], s_ref[3:4, :], s_ref[4:5, 0:512],
                 s_ref[4:5, 512:1024], s_ref[5:6, 0:256], wa2_g]
        for k in range(n):
            d, m2, v2 = _adamw_math(w_refs[k][...], grads[k], m_refs[k][...], v_refs[k][...])
            outs[k][...] = grads[k]
            outs[n + k][...] = d
            outs[2 * n + k][...] = m2
            outs[3 * n + k][...] = v2

    shapes = [jax.ShapeDtypeStruct(w.shape, F32) for w in ws] * 4
    smem = pl.BlockSpec(memory_space=pltpu.SMEM)
    outs = pl.pallas_call(
        body, name="small_update", in_specs=[smem] + [_VMEM] * (1 + 3 * n), out_specs=[_VMEM] * (4 * n),
        out_shape=shapes,
    )(chip_arr, summed, *ws, *ms, *vs)
    return outs[0:n], outs[n:2 * n], outs[2 * n:3 * n], outs[3 * n:4 * n]


def _pad_in_rows(w_t):
    return jnp.pad(w_t, ((0, IN_ROWS - IN_SHARD), (0, 0)))


def _forward_backward(xs, target, ffn1_w, rest, ba_p, ffn1_norm_g, mix_norm_g, ret_norm_g, gla_norm_g, ffn2_norm_g,
                      final_norm_g, ffn1_gather=None, rest_plan=None, rest_weights=None, ffn2_plans=None,
                      ffn2_weights=None, ffn2_pairs=None, ffn2_pairs_done=None, early=None, late=None, small_plan=None):
    t = xs.shape[0]
    cos_t, sa_t, sb_t = _rope_tables(t)
    log_gamma = jnp.log(1.0 - 2.0 ** (-5.0 - jnp.arange(HEADS, dtype=F32)))
    lg_t = jnp.broadcast_to(log_gamma[:, None, None], (HEADS, 1, LANES))
    ret_aux = [cos_t, sa_t, sb_t, lg_t]

    if ffn1_gather is None:
        (x1, a1, u1, h1), gathered = _ffn_fwd(xs, ffn1_norm_g, ffn1_w, "ffn1_fwd", hosted=rest_plan)
    else:
        ffn1_shard, ffn1_weights = ffn1_gather
        (x1, a1, u1, h1, wall), gathered = _ffn1_fwd_gathering(xs, ffn1_norm_g, ffn1_shard, "ffn1_fwd",
                                                               hosted=rest_plan)
        ffn1_w = ffn1_weights(wall)
    ffn2_w, w_in_pt, w_out_full, wa2_p = rest if rest_plan is None else rest_weights(gathered)
    plans = [None] * 3 if ffn2_plans is None else ffn2_plans
    (proj, h_mix), got_gate = _mixer_in_fwd(x1, mix_norm_g, w_in_pt, "mixer_in_fwd", hosted=plans[0])
    gla_aux = [proj, wa2_p, ba_p]
    (o_ret, raw_ret, st_ret), got_up = _attn_fwd(True, proj, ret_aux, ret_norm_g, "ret_fwd", hosted=plans[1])
    (o_gla, raw_gla, st_gla), got_down = _attn_fwd(False, proj, gla_aux, gla_norm_g, "gla_fwd", hosted=plans[2])
    if ffn2_plans is not None:
        ffn2_w = ffn2_weights(got_gate + got_up + got_down)
    x2 = _mixer_out_fwd(o_ret, o_gla, w_out_full, x1, "mixer_out_fwd")
    (loss_blk, dx3, d_final_g, a2, u2, h2), _ = _ffn_fwd(x2, ffn2_norm_g, ffn2_w, "ffn2_fwd",
                                                       loss_head=(final_norm_g, target))

    (da2, du2, hid2, dob2, dx2, d_ffn2_g), _ = _ffn_bwd(dx3, x2, ffn2_norm_g, a2, u2, ffn2_w, "ffn2_bwd")
    g_gate2 = _matmul_tn(da2, h2, "ffn2_dgate", out_dtype=BF16)
    g_up2 = _matmul_tn(du2, h2, "ffn2_dup", out_dtype=BF16)
    g_down2 = _matmul_tn(hid2, dob2, "ffn2_ddown", out_dtype=BF16)

    d_o = _matmul_nt(dx2, w_out_full, "mixer_out_bwd")
    g_wout_ret = _matmul_tn(o_ret, dx2, "wout_grad_ret", out_dtype=BF16)
    g_wout_gla = _matmul_tn(o_gla, dx2, "wout_grad_gla", out_dtype=BF16)
    pairs_plan = None if ffn2_pairs is None else ffn2_pairs([g_gate2, g_up2, g_down2])
    (*dproj_ret, d_ret_g), pair_recv = _attn_bwd(True, proj, ret_aux, ret_norm_g, raw_ret, st_ret, d_o, "ret_bwd",
                                                 hosted=pairs_plan)
    if ffn2_pairs is not None:
        ffn2_pairs_done(pair_recv)
    (*dproj_gla, d_gla_g, dlogit, d_ba_p), _ = _attn_bwd(False, proj, gla_aux, gla_norm_g, raw_gla, st_gla, d_o,
                                                        "gla_bwd")
    d_glow = _matmul_nt(dlogit, wa2_p, "gate_low_bwd", out_dtype=BF16)
    g_wa2_p = _matmul_tn(proj[:, PROJ_P - LANES:], dlogit, "gate_w_grad")
    dproj = jnp.concatenate(dproj_ret + dproj_gla + [d_glow], axis=1)
    g_win_p = _matmul_tn(dproj, h_mix, "w_in_grad", tka=PROJ_P // PROJ_TILES, out_dtype=BF16)
    dx1, d_mix_g = _mixer_in_bwd(dproj, w_in_pt, dx2, x1, mix_norm_g, "mixer_in_bwd")
    g_win_t = _unpad_w_in_t(g_win_p[0])
    g_win = jnp.stack([_pad_in_rows(g_win_t[IN_SHARD * p:IN_SHARD * (p + 1)]) for p in range(N_CHIPS)], axis=0)
    g_wout = jnp.concatenate([g_wout_ret[0], g_wout_gla[0]], axis=0).reshape(N_CHIPS, D_MODEL // N_CHIPS, D_MODEL)

    early_grads = [g_win, g_wout] if ffn2_pairs is not None else [g_gate2, g_up2, g_down2, g_win, g_wout]
    early_plan = None if early is None else early(early_grads)
    (da1, du1, hid1, dob1, grad_x, d_ffn1_g), arrived = _ffn_bwd(dx1, xs, ffn1_norm_g, a1, u1, ffn1_w, "ffn1_bwd",
                                                                hosted=early_plan)
    d_ba = d_ba_p.reshape(HEADS, LANES)[:, 0:64].reshape(1, 256)
    small_local = _pack_small(d_ffn1_g, d_mix_g, d_ffn2_g, d_final_g, d_ret_g, d_gla_g, d_ba, g_wa2_p[0], loss_blk)
    late_grads, late_arrived = [], []
    for lhs, rhs, name in ((da1, h1, "ffn1_dgate"), (du1, h1, "ffn1_dup"), (hid1, dob1, "ffn1_ddown")):
        if late is None:
            plan = None
        else:
            plan = late(late_grads[-1], len(late_grads)) if late_grads else small_plan(small_local)
        res = _matmul_tn(lhs, rhs, name, out_dtype=BF16, hosted=plan)
        if plan is not None:
            res, carried = res
            late_arrived += carried
        late_grads.append(res)
    g_gate1, g_up1, g_down1 = late_grads

    return (small_local, grad_x, g_gate1, g_up1, g_down1, g_gate2, g_up2, g_down2, g_win, g_wout, g_wa2_p,
            d_ba_p, d_ffn1_g, d_mix_g, d_ffn2_g, d_final_g, d_ret_g, d_gla_g, arrived, late_arrived)


def kernel(x, ffn1_norm_g, ffn1_w_gate, ffn1_w_up, ffn1_w_down, mix_norm_g, w_in, ret_norm_g, gla_w_a2, gla_b_a, gla_norm_g, w_out, ffn2_norm_g, ffn2_w_gate, ffn2_w_up, ffn2_w_down, final_norm_g, loss_target, m_ffn1_norm_g, m_ffn1_w_gate, m_ffn1_w_up, m_ffn1_w_down, m_mix_norm_g, m_w_in, m_ret_norm_g, m_gla_w_a2, m_gla_b_a, m_gla_norm_g, m_w_out, m_ffn2_norm_g, m_ffn2_w_gate, m_ffn2_w_up, m_ffn2_w_down, m_final_norm_g, v_ffn1_norm_g, v_ffn1_w_gate, v_ffn1_w_up, v_ffn1_w_down, v_mix_norm_g, v_w_in, v_ret_norm_g, v_gla_w_a2, v_gla_b_a, v_gla_norm_g, v_w_out, v_ffn2_norm_g, v_ffn2_w_gate, v_ffn2_w_up, v_ffn2_w_down, v_final_norm_g):
    t = x.shape[1]
    xs = x.reshape(t, D_MODEL)
    target = loss_target.reshape(t, D_MODEL)
    chip = 2 * lax.axis_index("x") + lax.axis_index("y")
    c_arr = lax.axis_index("c").astype(jnp.int32).reshape(1)

    me_arr = chip.astype(jnp.int32).reshape(1)

    pad_rows = _pad_in_rows

    def own_block(gathered, shard):
        return lax.dynamic_update_slice(gathered, shard[None], (chip,) + (0,) * shard.ndim)

    ffn1_shard = _halves(jnp.stack([ffn1_w_gate[0].T, ffn1_w_up[0].T, ffn1_w_down[0]], axis=0).astype(BF16))
    rest_shards = [_halves(pad_rows(w_in[0].T).astype(BF16)[None]),
                   _halves(w_out.astype(BF16)),
                   jnp.concatenate([gla_w_a2.reshape(GATE_RANK, 64), jnp.zeros((GATE_RANK, 64), F32)],
                                   axis=1).reshape(1, 2, 8, LANES)]
    ffn2_shards = [_halves(w.astype(BF16)[None]) for w in (ffn2_w_gate[0].T, ffn2_w_up[0].T, ffn2_w_down[0])]
    def ffn1_weights(gathered):
        return lax.dynamic_update_slice(gathered, ffn1_shard[None], (0,) * 5).reshape(N_CHIPS, 3, FF_SHARD, D_MODEL)

    def rest_weights(gathered):
        win_all, wout_all, wa2_all = [own_block(g, s) for g, s in zip(gathered, rest_shards)]
        win_t = win_all.reshape(N_CHIPS, IN_ROWS, D_MODEL)
        w_in_pt = jnp.zeros((PROJ_P, D_MODEL), BF16)
        for p in range(N_CHIPS):
            w_in_pt = lax.dynamic_update_slice(w_in_pt, win_t[p, 0:IN_SHARD], (IN_SHARD * p, 0))
        wa2_p = jnp.pad(
            wa2_all.reshape(N_CHIPS, GATE_RANK, LANES).transpose(1, 0, 2).reshape(GATE_RANK, HEADS * LANES),
            ((0, LANES - GATE_RANK), (0, 0))).astype(BF16)
        return (None, w_in_pt, wout_all.reshape(D_MODEL, D_MODEL), wa2_p)

    def ffn2_weights(gathered):
        return [own_block(g, s).reshape(N_CHIPS, FF_SHARD, D_MODEL) for g, s in zip(gathered, ffn2_shards)]

    def by_halves(g):
        return g.reshape(g.shape[0], 2, g.shape[1] // 2, g.shape[2])

    def pair_adds(halves, recv, tag):
        return [_pair_add(g, r, c_arr, "pair_add_%s%d" % (tag, k)) for k, (g, r) in enumerate(zip(halves, recv))]

    def pair_sums(grads, tag):
        halves = [by_halves(g) for g in grads]
        recv = _run_hosted(_pair_exchange_plan(halves), "pair_exchange_" + tag)
        return pair_adds(halves, recv, tag)

    early_sums, ffn2_halves = [], []

    def ffn2_pairs(grads):
        ffn2_halves.extend(by_halves(g) for g in grads)
        return _pair_exchange_plan(ffn2_halves)

    def ffn2_pairs_done(recv):
        early_sums.extend(pair_adds(ffn2_halves, recv, "ffn2_"))

    def early(grads):
        early_sums.extend(pair_sums(grads, "early"))
        return _chip_exchange_plan(early_sums)

    late_sums = []

    def late(grad, number):
        late_sums.extend(pair_sums([grad], "late%d" % number))
        return _chip_exchange_plan(late_sums[-1:], by_peer=True)

    ba_p = jnp.pad(gla_b_a.reshape(HEADS, 64), ((0, 0), (0, 64))).reshape(1, HEADS * LANES)
    fb = _forward_backward(xs, target, None, None, ba_p, ffn1_norm_g, mix_norm_g, ret_norm_g, gla_norm_g,
                           ffn2_norm_g, final_norm_g.reshape(1, D_MODEL), ffn1_gather=(ffn1_shard, ffn1_weights),
                           rest_plan=_gather_plan(rest_shards), rest_weights=rest_weights,
                           ffn2_plans=[_gather_plan(ffn2_shards[0:2]), None, _gather_plan(ffn2_shards[2:3])],
                           ffn2_weights=ffn2_weights,
                           ffn2_pairs=ffn2_pairs, ffn2_pairs_done=ffn2_pairs_done, early=early, late=late,
                           small_plan=_small_gather_plan)
    (small_local, grad_x, _, _, g_down1, _, _, _, _, _, _, _, _, _, _, _, _, _, early_arrived, late_arrived) = fb
    small_all, late_arrived = late_arrived[0], late_arrived[1:]
    late_arrived = late_arrived + _run_hosted(late(g_down1, 3), "chip_exchange_late")
    mine = [_peer_sum(s, r, "chip_sum_%d" % k) for k, (s, r) in enumerate(zip(late_sums, late_arrived))]
    mine += [_chip_sum(s, r, me_arr, "chip_sum_%d" % (3 + k)) for k, (s, r) in enumerate(zip(early_sums, early_arrived))]
    other = _run_hosted(_pair_share_plan(mine), "pair_share")

    device = 2 * chip + lax.axis_index("c")
    small_sum = _sum_devices(lax.dynamic_update_slice(small_all, small_local[None], (device, 0, 0)))
    loss = small_sum[6, 0]

    def rows(n1, nm, n2, nf, nret, ngla, ba, wa2):
        return [n1, nm, n2, nf.reshape(1, D_MODEL), nret, ngla, ba, wa2.reshape(GATE_RANK, 64)]

    small = _small_update(
        small_sum, me_arr,
        rows(ffn1_norm_g, mix_norm_g, ffn2_norm_g, final_norm_g, ret_norm_g, gla_norm_g, gla_b_a, gla_w_a2),
        rows(m_ffn1_norm_g, m_mix_norm_g, m_ffn2_norm_g, m_final_norm_g, m_ret_norm_g, m_gla_norm_g, m_gla_b_a,
             m_gla_w_a2),
        rows(v_ffn1_norm_g, v_mix_norm_g, v_ffn2_norm_g, v_final_norm_g, v_ret_norm_g, v_gla_norm_g, v_gla_b_a,
             v_gla_w_a2))
    s_grad, s_delta, s_m, s_v = [
        [*o[0:3], o[3].reshape(D_MODEL), *o[4:7], o[7].reshape(1, GATE_RANK, 64)] for o in small]

    def big(k, w, m, v, name, to_2d, from_2d):
        outs4 = _adamw_halves(to_2d(w), mine[k], other[k], to_2d(m), to_2d(v), c_arr, name)
        return [from_2d(z) for z in outs4]

    plain = (lambda w: w[0], lambda z: z[None])
    transposed = (lambda w: w[0].T, lambda z: z.T[None])
    in_proj = (lambda w: pad_rows(w[0].T), lambda z: z[0:IN_SHARD].T[None])
    r_g1 = big(0, ffn1_w_gate, m_ffn1_w_gate, v_ffn1_w_gate, "adamw_ffn1_gate", *transposed)
    r_u1 = big(1, ffn1_w_up, m_ffn1_w_up, v_ffn1_w_up, "adamw_ffn1_up", *transposed)
    r_d1 = big(2, ffn1_w_down, m_ffn1_w_down, v_ffn1_w_down, "adamw_ffn1_down", *plain)
    r_g2 = big(3, ffn2_w_gate, m_ffn2_w_gate, v_ffn2_w_gate, "adamw_ffn2_gate", *transposed)
    r_u2 = big(4, ffn2_w_up, m_ffn2_w_up, v_ffn2_w_up, "adamw_ffn2_up", *transposed)
    r_d2 = big(5, ffn2_w_down, m_ffn2_w_down, v_ffn2_w_down, "adamw_ffn2_down", *plain)
    r_in = big(6, w_in, m_w_in, v_w_in, "adamw_w_in", *in_proj)
    r_out = big(7, w_out, m_w_out, v_w_out, "adamw_w_out", *plain)

    def leaves(k, smalls):
        n1, nm, n2, nf, nret, ngla, ba, wa2 = smalls
        return [n1, r_g1[k], r_u1[k], r_d1[k], nm, r_in[k], nret, wa2, ba, ngla, r_out[k], n2, r_g2[k], r_u2[k], r_d2[k], nf]

    outs = [loss, grad_x.reshape(x.shape)]
    outs += leaves(0, s_grad) + leaves(1, s_delta) + leaves(2, s_m) + leaves(3, s_v)
    return tuple(outs)
```

```python
import functools

import jax
import jax.numpy as jnp
from jax import lax
from jax.experimental import pallas as pl
from jax.experimental.pallas import tpu as pltpu

F32, BF16 = jnp.float32, jnp.bfloat16
MESH = pl.DeviceIdType.MESH
ANY = pl.BlockSpec(memory_space=pl.ANY)

D_MODEL = 1024
D_FF = 2816
N_CHIPS = 4
FF_SHARD = D_FF // N_CHIPS
IN_WIDTH = 3088
IN_SHARD = IN_WIDTH // N_CHIPS
IN_ROWS = 800
CHUNK = 64
HEADS = 4
LANES = 128
PROJ_P = 3072 + LANES
PROJ_TILES = 5
GATE_RANK = 16
QK_SCALE = 0.125
GATE_NORM = 16.0
RMS_EPS = 1e-6
ROPE_BASE = 10000.0
ADAM_LR, ADAM_B1, ADAM_B2, ADAM_EPS, ADAM_WD, ADAM_STEP = 0.001, 0.9, 0.999, 1e-08, 0.01, 10
SMALL_ROWS = 32
TOKEN_TILE = 512
ATTN_TILE = 512

_ARB2 = pltpu.CompilerParams(dimension_semantics=("arbitrary", "arbitrary"))
_ARB1 = pltpu.CompilerParams(dimension_semantics=("arbitrary",))
_ARB3 = pltpu.CompilerParams(dimension_semantics=("arbitrary", "arbitrary", "arbitrary"))


def _dot(a, b):
    return jnp.dot(a, b, preferred_element_type=F32)


def _dot_nt(a, b):
    return lax.dot_general(a, b, (((1,), (1,)), ((), ())), preferred_element_type=F32)


def _dot_tn(a, b):
    return lax.dot_general(a, b, (((0,), (0,)), ((), ())), preferred_element_type=F32)


def _rms_scale(xv):
    return lax.rsqrt(jnp.mean(xv * xv, axis=-1, keepdims=True) + RMS_EPS)


def _rms_bwd(dh, xv, g):
    r = _rms_scale(xv)
    xhat = xv * r
    dxhat = dh * g
    dx = r * (dxhat - xhat * jnp.mean(dxhat * xhat, axis=-1, keepdims=True))
    return dx, jnp.sum(dh * xhat, axis=0, keepdims=True)


def _silu_grad(a, sg):
    return sg * (1.0 + a * (1.0 - sg))


class _Hosted:
    def __init__(self, arrays, out_shapes, n_sems, start, finish, middle=None, peers=None):
        self.arrays, self.out_shapes, self.n_sems = list(arrays), list(out_shapes), n_sems
        self.start, self.finish = start, finish
        self.middle = middle if middle is not None else (lambda *refs: None)
        self.peers = peers


PEER_SETS = {
    "sibling": (0, lambda x, y, c: [(x, y, 1 - c)]),
    "chips": (1, lambda x, y, c: [(1 - x, y, c), (x, 1 - y, c), (1 - x, 1 - y, c)]),
    "neighbours": (2, lambda x, y, c: [(1 - x, y, c), (x, 1 - y, c), (x, y, 1 - c)]),
    "chips_sibling": (3, lambda x, y, c: [(1 - x, y, c), (x, 1 - y, c), (1 - x, 1 - y, c), (x, y, 1 - c)]),
}


def _handshake(kind):
    x, y, c, _, _ = _place()
    peers = PEER_SETS[kind][1](x, y, c)
    barrier = pltpu.get_barrier_semaphore()
    for peer in peers:
        pl.semaphore_signal(barrier, inc=1, device_id=peer, device_id_type=MESH)
    pl.semaphore_wait(barrier, len(peers))


def _with_barrier(compiler_params, kind):
    if kind is None:
        return compiler_params
    semantics = None if compiler_params is None else compiler_params.dimension_semantics
    return pltpu.CompilerParams(dimension_semantics=semantics, collective_id=PEER_SETS[kind][0])


def _call(body, args, *, name, grid, in_specs, out_specs, out_shape, scratch_shapes, compiler_params, hosted=None):
    if hosted is None:
        outs = pl.pallas_call(body, name=name, grid=grid, in_specs=in_specs, out_specs=out_specs, out_shape=out_shape,
                              scratch_shapes=scratch_shapes, compiler_params=compiler_params)(*args)
        return list(outs), []
    n_in, n_out, n_sc, nh = len(in_specs), len(out_specs), len(scratch_shapes), len(hosted.arrays)

    def wrapped(*refs):
        ins, h_in = refs[:n_in], refs[n_in:n_in + nh]
        outs, h_out = refs[n_in + nh:n_in + nh + n_out], refs[n_in + nh + n_out:n_in + 2 * nh + n_out]
        rest = refs[n_in + 2 * nh + n_out:]
        scratch, (send_sems, recv_sems) = rest[:n_sc], rest[n_sc:]
        step = functools.reduce(lambda flat, d: flat * grid[d] + pl.program_id(d), range(len(grid)), 0)
        total = functools.reduce(lambda a, b: a * b, grid)

        @pl.when(step == 0)
        def _():
            if hosted.peers is not None:
                _handshake(hosted.peers)
            hosted.start(h_in, h_out, send_sems, recv_sems)

        @pl.when(step == total // 2)
        def _():
            hosted.middle(h_in, h_out, send_sems, recv_sems)

        body(*ins, *outs, *scratch)
        last = step == total - 1

        @pl.when(last)
        def _():
            hosted.finish(h_in, h_out, send_sems, recv_sems)

    sems = [pltpu.SemaphoreType.DMA((hosted.n_sems,)), pltpu.SemaphoreType.DMA((hosted.n_sems,))]
    outs = pl.pallas_call(
        wrapped, name=name, grid=grid, in_specs=list(in_specs) + [ANY] * nh, out_specs=list(out_specs) + [ANY] * nh,
        out_shape=list(out_shape) + hosted.out_shapes, scratch_shapes=list(scratch_shapes) + sems,
        compiler_params=_with_barrier(compiler_params, hosted.peers))(*args, *hosted.arrays)
    return list(outs[:n_out]), list(outs[n_out:])


def _run_hosted(hosted, name):
    nh = len(hosted.arrays)

    def body(*refs):
        h_in, h_out, (send_sems, recv_sems) = refs[:nh], refs[nh:2 * nh], refs[2 * nh:]
        if hosted.peers is not None:
            _handshake(hosted.peers)
        hosted.start(h_in, h_out, send_sems, recv_sems)
        hosted.middle(h_in, h_out, send_sems, recv_sems)
        hosted.finish(h_in, h_out, send_sems, recv_sems)

    sems = [pltpu.SemaphoreType.DMA((hosted.n_sems,)), pltpu.SemaphoreType.DMA((hosted.n_sems,))]
    return list(pl.pallas_call(body, name=name, in_specs=[ANY] * nh, out_specs=[ANY] * nh, out_shape=hosted.out_shapes,
                               scratch_shapes=sems, compiler_params=_with_barrier(None, hosted.peers))(*hosted.arrays))


def _ffn_weight_operands(ffn_w, chunk_maps):
    if isinstance(ffn_w, (list, tuple)):
        specs = [pl.BlockSpec((None, FF_SHARD, D_MODEL), lambda *g, m=m: (m(*g), 0, 0)) for m in chunk_maps]
        return list(ffn_w), specs
    specs = [pl.BlockSpec((None, None, FF_SHARD, D_MODEL), lambda *g, m=m, k=kind: (m(*g), k, 0, 0))
             for kind, m in enumerate(chunk_maps)]
    return [ffn_w] * 3, specs


def _pipeline_items(steps):
    def cur(s):
        c = jnp.minimum(s, steps - 1)
        return c // N_CHIPS, c % N_CHIPS

    def prev(s):
        p = jnp.maximum(s - 1, 0)
        return p // N_CHIPS, p % N_CHIPS

    return cur, prev


def _ffn_fwd(x, g, ffn_w, name, hosted=None, loss_head=None):
    t = x.shape[0]
    tm = min(t, TOKEN_TILE)
    n_head = 0 if loss_head is None else 2

    def body(*refs):
        x_ref, g_ref, wg_ref, wu_ref, wd_ref = refs[0:5]
        head_in = refs[5:5 + n_head]
        outs = refs[5 + n_head:-1]
        acc_ref = refs[-1]
        a_ref, u_ref, h_ref = outs[-3:]
        i, j = pl.program_id(0), pl.program_id(1)

        @pl.when(j == 0)
        def _():
            xv = x_ref[...]
            h_ref[...] = ((xv * _rms_scale(xv)) * g_ref[...]).astype(BF16)
            acc_ref[...] = jnp.zeros_like(acc_ref)

        h = h_ref[...]
        a = _dot_nt(h, wg_ref[...])
        u = _dot_nt(h, wu_ref[...])
        a_ref[...] = a.astype(BF16)
        u_ref[...] = u.astype(BF16)
        hid = (a * jax.nn.sigmoid(a)) * u
        acc_ref[...] += _dot(hid.astype(BF16), wd_ref[...])

        if loss_head is None:
            @pl.when(j == N_CHIPS - 1)
            def _():
                outs[0][...] = x_ref[...] + 0.5 * acc_ref[...]
        else:
            gf_ref, t_ref = head_in
            l_ref, dx_ref, dgf_ref = outs[0:3]

            @pl.when((i == 0) & (j == 0))
            def _():
                l_ref[...] = jnp.zeros_like(l_ref)
                dgf_ref[...] = jnp.zeros_like(dgf_ref)

            @pl.when(j == N_CHIPS - 1)
            def _():
                xv = x_ref[...] + 0.5 * acc_ref[...]
                gv = gf_ref[...]
                err = (xv * _rms_scale(xv)) * gv - t_ref[...]
                l_ref[...] += 0.5 * jnp.sum(jnp.mean(err * err, axis=-1, keepdims=True), axis=0, keepdims=True)
                dx, dg = _rms_bwd(err * (1.0 / D_MODEL), xv, gv)
                dx_ref[...] = dx
                dgf_ref[...] += dg

    tok = pl.BlockSpec((tm, D_MODEL), lambda i, j: (i, 0))
    row = pl.BlockSpec((1, D_MODEL), lambda i, j: (0, 0))
    act = pl.BlockSpec((None, tm, FF_SHARD), lambda i, j: (j, i, 0))
    act_shape = jax.ShapeDtypeStruct((N_CHIPS, t, FF_SHARD), BF16)
    w_arrays, weights = _ffn_weight_operands(ffn_w, [lambda i, j: j] * 3)
    if loss_head is None:
        first_specs, first_shapes, head_args, head_specs = [tok], [jax.ShapeDtypeStruct((t, D_MODEL), F32)], [], []
    else:
        first_specs = [pl.BlockSpec((8, LANES), lambda i, j: (0, 0)), tok, row]
        first_shapes = [jax.ShapeDtypeStruct((8, LANES), F32), jax.ShapeDtypeStruct((t, D_MODEL), F32),
                        jax.ShapeDtypeStruct((1, D_MODEL), F32)]
        head_args, head_specs = list(loss_head), [row, tok]
    return _call(
        body, (x, g, *w_arrays, *head_args), name=name, grid=(t // tm, N_CHIPS),
        in_specs=[tok, row] + weights + head_specs,
        out_specs=first_specs + [act, act, tok],
        out_shape=first_shapes + [act_shape, act_shape, jax.ShapeDtypeStruct((t, D_MODEL), BF16)],
        scratch_shapes=[pltpu.VMEM((tm, D_MODEL), F32)],
        compiler_params=_ARB2, hosted=hosted)


def _ffn1_fwd_gathering(x, g, shard, name, hosted=None):
    t = x.shape[0]
    tm = min(t, TOKEN_TILE)
    nt = t // tm
    nh = 0 if hosted is None else len(hosted.arrays)
    peers = "neighbours" if hosted is None or hosted.peers == "neighbours" else "chips_sibling"
    assert hosted is None or hosted.peers in ("neighbours", "chips_sibling")

    def body(*refs):
        x_ref, g_ref, shard_ref = refs[0:3]
        h_in = refs[3:3 + nh]
        xo_ref, a_ref, u_ref, h_ref, wall = refs[3 + nh:8 + nh]
        h_out = refs[8 + nh:8 + 2 * nh]
        acc, h_all, wbuf, load_sems, send_sems, recv_sems = refs[8 + 2 * nh:14 + 2 * nh]
        carried_sems = refs[14 + 2 * nh:]
        k, i = pl.program_id(0), pl.program_id(1)
        legs, _ = _gather_legs(shard_ref, wall, send_sems, recv_sems, 0, True)
        begin, pass_on, _ = _gather_steps(legs, True)

        def load(chunk, src):
            return pltpu.make_async_copy(src, wbuf.at[chunk % 2], load_sems.at[chunk % 2])

        @pl.when((k == 0) & (i == 0))
        def _():
            _handshake(peers)
            begin()
            load(0, shard_ref).start()
            load(0, shard_ref).wait()

        @pl.when((k == 1) & (i == 0))
        def _():
            pass_on()
            if hosted is not None:
                hosted.start(h_in, h_out, *carried_sems)
            legs["pass_y"][1].wait_recv()
            load(1, wall.at[PEER_SLOT[1]]).start()
            load(1, wall.at[PEER_SLOT[1]]).wait()

        @pl.when((k == 1) & (i == nt // 2))
        def _():
            legs["pass_x"][1].wait_recv()
            load(2, wall.at[PEER_SLOT[0]]).start()

        @pl.when((k == 2) & (i == 0))
        def _():
            load(2, wall.at[PEER_SLOT[0]]).wait()

        @pl.when((k == 2) & (i == nt // 2))
        def _():
            legs["fwd_y"][1].wait_recv()
            legs["pass_d0"][0].start()
            legs["fwd_x"][1].wait_recv()
            legs["pass_d1"][0].start()
            legs["pass_d0"][1].wait_recv()
            legs["pass_d1"][1].wait_recv()
            load(3, wall.at[PEER_SLOT[2]]).start()
            if hosted is not None:
                hosted.middle(h_in, h_out, *carried_sems)

        @pl.when((k == 3) & (i == 0))
        def _():
            load(3, wall.at[PEER_SLOT[2]]).wait()

        @pl.when(k == 0)
        def _():
            xv = x_ref[...]
            h0 = ((xv * _rms_scale(xv)) * g_ref[...]).astype(BF16)
            h_all[i] = h0
            h_ref[...] = h0

        h = h_all[i]
        wg, wu, wd = (wbuf[k % 2, kind].reshape(FF_SHARD, D_MODEL) for kind in range(3))
        a = _dot_nt(h, wg)
        u = _dot_nt(h, wu)
        a_ref[...] = a.astype(BF16)
        u_ref[...] = u.astype(BF16)
        part = _dot(((a * jax.nn.sigmoid(a)) * u).astype(BF16), wd)

        @pl.when(k == 0)
        def _():
            acc[i] = part

        @pl.when(k > 0)
        def _():
            acc[i] += part

        @pl.when(k == N_CHIPS - 1)
        def _():
            xo_ref[...] = x_ref[...] + 0.5 * acc[i]

        @pl.when((k == N_CHIPS - 1) & (i == nt - 1))
        def _():
            for pair in legs.values():
                pair[0].wait_send()
            if hosted is not None:
                hosted.finish(h_in, h_out, *carried_sems)

    def first_or_last(k):
        return (k == 0) | (k == N_CHIPS - 1)

    tok = lambda keep: pl.BlockSpec((tm, D_MODEL), lambda k, i: (jnp.where(keep(k), i, 0), 0))
    act = pl.BlockSpec((None, tm, FF_SHARD), lambda k, i: (k, i, 0))
    act_shape = jax.ShapeDtypeStruct((N_CHIPS, t, FF_SHARD), BF16)
    carried = [] if hosted is None else [pltpu.SemaphoreType.DMA((hosted.n_sems,))] * 2
    outs = pl.pallas_call(
        body, name=name, grid=(N_CHIPS, nt),
        in_specs=[tok(first_or_last), pl.BlockSpec((1, D_MODEL), lambda k, i: (0, 0)), ANY] + [ANY] * nh,
        out_specs=[tok(lambda k: k == N_CHIPS - 1), act, act,
                   pl.BlockSpec((tm, D_MODEL), lambda k, i: (jnp.where(k == 0, i, nt - 1), 0)), ANY] + [ANY] * nh,
        out_shape=[jax.ShapeDtypeStruct((t, D_MODEL), F32), act_shape, act_shape,
                   jax.ShapeDtypeStruct((t, D_MODEL), BF16),
                   jax.ShapeDtypeStruct((N_CHIPS,) + shard.shape, shard.dtype)]
                  + ([] if hosted is None else hosted.out_shapes),
        scratch_shapes=[pltpu.VMEM((nt, tm, D_MODEL), F32), pltpu.VMEM((nt, tm, D_MODEL), BF16),
                        pltpu.VMEM((2,) + shard.shape, shard.dtype), pltpu.SemaphoreType.DMA((2,)),
                        pltpu.SemaphoreType.DMA((8,)), pltpu.SemaphoreType.DMA((8,))] + carried,
        compiler_params=_with_barrier(_ARB2, peers),
    )(x, g, shard, *([] if hosted is None else hosted.arrays))
    return list(outs[:5]), list(outs[5:])


def _ffn_bwd(dxo, x, g, a4, u4, ffn_w, name, hosted=None, back_w=None):
    t = x.shape[0]
    tm = min(t, TOKEN_TILE)
    steps = (t // tm) * N_CHIPS
    cur, prev = _pipeline_items(steps)


    def body(*refs):
        dxo_ref, dxo_prev_ref, x_ref, g_ref, a_ref, u_ref, wg_ref, wu_ref, wd_ref = refs[0:9]
        n_back = 0 if back_w is None else 1
        da_ref, du_ref, hid_ref, dob_ref, dx_ref, dg_ref = refs[9 + n_back:15 + n_back]
        acc_ref, da_slots, du_slots = refs[-3:]
        s = pl.program_id(0)
        jc, jp = cur(s)[1], prev(s)[1]
        slot = s % 2

        @pl.when(s == 0)
        def _():
            dg_ref[...] = jnp.zeros_like(dg_ref)
            acc_ref[...] = jnp.zeros_like(acc_ref)
            da_slots[...] = jnp.zeros_like(da_slots)
            du_slots[...] = jnp.zeros_like(du_slots)

        @pl.when(jc == 0)
        def _():
            dob_ref[...] = (0.5 * dxo_ref[...]).astype(BF16)

        dhid = _dot_nt(dob_ref[...], wd_ref[...])
        a = a_ref[...].astype(F32)
        u = u_ref[...].astype(F32)
        sg = jax.nn.sigmoid(a)
        sl = a * sg
        hid_ref[...] = (sl * u).astype(BF16)
        du = (dhid * sl).astype(BF16)
        da = (dhid * u * _silu_grad(a, sg)).astype(BF16)
        du_ref[...] = du
        da_ref[...] = da
        acc_ref[...] += _dot(da_slots[1 - slot], wg_ref[...]) + _dot(du_slots[1 - slot], wu_ref[...])
        da_slots[slot] = da
        du_slots[slot] = du

        @pl.when((jp == N_CHIPS - 1) & (s > 0))
        def _():
            dx, dg = _rms_bwd(acc_ref[...], x_ref[...], g_ref[...])
            dx = dxo_prev_ref[...] + dx
            dx_ref[...] = dx
            dg_ref[...] += dg
            acc_ref[...] = jnp.zeros_like(acc_ref)
            if back_w is not None:
                refs[15 + n_back][...] = _dot_nt(dx.astype(BF16), refs[9][...])

    tok_cur = pl.BlockSpec((tm, D_MODEL), lambda s: (cur(s)[0], 0))
    tok_prev = pl.BlockSpec((tm, D_MODEL), lambda s: (prev(s)[0], 0))
    act = pl.BlockSpec((None, tm, FF_SHARD), lambda s: (cur(s)[1], cur(s)[0], 0))
    row = pl.BlockSpec((1, D_MODEL), lambda s: (0, 0))
    w_arrays, weights = _ffn_weight_operands(ffn_w, [lambda s: prev(s)[1], lambda s: prev(s)[1], lambda s: cur(s)[1]])
    act_shape = jax.ShapeDtypeStruct((N_CHIPS, t, FF_SHARD), BF16)
    back_args, back_in, back_out, back_shape = [], [], [], []
    if back_w is not None:
        n_back_cols = back_w.shape[0]
        back_args, back_in = [back_w], [pl.BlockSpec(back_w.shape, lambda s: (0, 0))]
        back_out = [pl.BlockSpec((tm, n_back_cols), lambda s: (prev(s)[0], 0))]
        back_shape = [jax.ShapeDtypeStruct((t, n_back_cols), F32)]
    return _call(
        body, (dxo, dxo, x, g, a4, u4, *w_arrays, *back_args), name=name, grid=(steps + 1,),
        in_specs=[tok_cur, tok_prev, tok_prev, row, act, act] + weights + back_in,
        out_specs=[act, act, act, tok_cur, tok_prev, row] + back_out,
        out_shape=[act_shape, act_shape, act_shape,
                   jax.ShapeDtypeStruct((t, D_MODEL), BF16),
                   jax.ShapeDtypeStruct((t, D_MODEL), F32),
                   jax.ShapeDtypeStruct((1, D_MODEL), F32)] + back_shape,
        scratch_shapes=[pltpu.VMEM((tm, D_MODEL), F32), pltpu.VMEM((2, tm, FF_SHARD), BF16),
                        pltpu.VMEM((2, tm, FF_SHARD), BF16)],
        compiler_params=_ARB1, hosted=hosted)


def _matmul_tn(a, b, name, tka=None, out_dtype=F32, hosted=None):
    a3, b3 = a.ndim == 3, b.ndim == 3
    nb = a.shape[0] if a3 else (b.shape[0] if b3 else 1)
    t, ka, n = a.shape[-2], a.shape[-1], b.shape[-1]
    tka = ka if tka is None else tka
    tk = min(t, 4 * TOKEN_TILE)
    nk = t // tk

    def body(a_ref, b_ref, o_ref, acc_ref):
        k = pl.program_id(2)

        @pl.when(k == 0)
        def _():
            acc_ref[...] = jnp.zeros_like(acc_ref)

        acc_ref[...] += _dot_tn(a_ref[...].astype(BF16), b_ref[...].astype(BF16))

        @pl.when(k == nk - 1)
        def _():
            o_ref[...] = acc_ref[...].astype(out_dtype)

    a_spec = (pl.BlockSpec((None, tk, tka), lambda i, j, k: (i, k, j)) if a3
              else pl.BlockSpec((tk, tka), lambda i, j, k: (k, j)))
    b_spec = (pl.BlockSpec((None, tk, n), lambda i, j, k: (i, k, 0)) if b3
              else pl.BlockSpec((tk, n), lambda i, j, k: (k, 0)))
    outs, carried = _call(
        body, (a, b), name=name, grid=(nb, ka // tka, t // tk),
        in_specs=[a_spec, b_spec],
        out_specs=[pl.BlockSpec((None, tka, n), lambda i, j, k: (i, j, 0))],
        out_shape=[jax.ShapeDtypeStruct((nb, ka, n), out_dtype)],
        scratch_shapes=[pltpu.VMEM((tka, n), F32)],
        compiler_params=_ARB3, hosted=hosted)
    return outs[0] if hosted is None else (outs[0], carried)


def _matmul_nt(a, w, name, out_dtype=F32):
    t, k = a.shape
    n = w.shape[0]
    tm = min(t, TOKEN_TILE)

    def body(a_ref, w_ref, o_ref):
        o_ref[...] = _dot_nt(a_ref[...].astype(BF16), w_ref[...]).astype(out_dtype)

    return pl.pallas_call(
        body, name=name, grid=(t // tm,),
        in_specs=[pl.BlockSpec((tm, k), lambda i: (i, 0)), pl.BlockSpec((n, k), lambda i: (0, 0))],
        out_specs=pl.BlockSpec((tm, n), lambda i: (i, 0)),
        out_shape=jax.ShapeDtypeStruct((t, n), out_dtype),
        compiler_params=_ARB1,
    )(a, w)


def _mixer_in_bwd(dproj, w_in_pt, dres, x, g, name):
    t, k = dproj.shape
    tm = min(t, TOKEN_TILE)

    def body(a_ref, w_ref, dres_ref, x_ref, g_ref, dx_ref, dg_ref):
        @pl.when(pl.program_id(0) == 0)
        def _():
            dg_ref[...] = jnp.zeros_like(dg_ref)

        dh = _dot(a_ref[...], w_ref[...])
        dx, dg = _rms_bwd(dh, x_ref[...], g_ref[...])
        dx_ref[...] = dres_ref[...] + dx
        dg_ref[...] += dg

    tok = pl.BlockSpec((tm, D_MODEL), lambda i: (i, 0))
    row = pl.BlockSpec((1, D_MODEL), lambda i: (0, 0))
    return pl.pallas_call(
        body, name=name, grid=(t // tm,),
        in_specs=[pl.BlockSpec((tm, k), lambda i: (i, 0)), pl.BlockSpec((k, D_MODEL), lambda i: (0, 0)), tok, tok, row],
        out_specs=[tok, row],
        out_shape=[jax.ShapeDtypeStruct((t, D_MODEL), F32), jax.ShapeDtypeStruct((1, D_MODEL), F32)],
        compiler_params=_ARB1,
    )(dproj, w_in_pt, dres, x, g)


def _mixer_in_fwd(x, g, w_in_pt, name, hosted=None):
    t = x.shape[0]
    tm = min(t, TOKEN_TILE)
    tn = PROJ_P // PROJ_TILES

    def body(x_ref, g_ref, w_ref, p_ref, h_ref):
        @pl.when(pl.program_id(1) == 0)
        def _():
            xv = x_ref[...]
            h_ref[...] = ((xv * _rms_scale(xv)) * g_ref[...]).astype(BF16)

        p_ref[...] = _dot_nt(h_ref[...], w_ref[...])

    tok = pl.BlockSpec((tm, D_MODEL), lambda i, j: (i, 0))
    return _call(
        body, (x, g, w_in_pt), name=name, grid=(t // tm, PROJ_TILES),
        in_specs=[tok, pl.BlockSpec((1, D_MODEL), lambda i, j: (0, 0)),
                  pl.BlockSpec((tn, D_MODEL), lambda i, j: (j, 0))],
        out_specs=[pl.BlockSpec((tm, tn), lambda i, j: (i, j)), tok],
        out_shape=[jax.ShapeDtypeStruct((t, PROJ_P), F32), jax.ShapeDtypeStruct((t, D_MODEL), BF16)],
        scratch_shapes=[], compiler_params=_ARB2, hosted=hosted)


def _mixer_out_fwd(o_ret, o_gla, w_out, x, name):
    t = x.shape[0]
    tm = min(t, TOKEN_TILE)
    half = HEADS * LANES

    def body(a_ref, b_ref, w_ref, x_ref, o_ref):
        o_ref[...] = x_ref[...] + _dot(a_ref[...], w_ref[0:half, :]) + _dot(b_ref[...], w_ref[half:2 * half, :])

    tok = pl.BlockSpec((tm, D_MODEL), lambda i: (i, 0))
    hb = pl.BlockSpec((tm, half), lambda i: (i, 0))
    return pl.pallas_call(
        body, name=name, grid=(t // tm,),
        in_specs=[hb, hb, pl.BlockSpec((2 * half, D_MODEL), lambda i: (0, 0)), tok],
        out_specs=tok, out_shape=jax.ShapeDtypeStruct((t, D_MODEL), F32),
        compiler_params=_ARB1,
    )(o_ret, o_gla, w_out, x)


def _rot(v, cos, sa, sb):
    return v * cos + pltpu.roll(v, 96, 1) * sa + pltpu.roll(v, 32, 1) * sb


def _rot_t(d, cos, sa, sb):
    return d * cos + pltpu.roll(d * sa, 32, 1) + pltpu.roll(d * sb, 96, 1)


def _bmm(a, b):
    return jnp.einsum("cik,ckj->cij", a, b, preferred_element_type=F32)


def _bmm_nt(a, b):
    return jnp.einsum("cik,cjk->cij", a, b, preferred_element_type=F32)


def _bmm_tn(a, b):
    return jnp.einsum("cki,ckj->cij", a, b, preferred_element_type=F32)


def _masked_sum(mask, x):
    hi = x.astype(BF16)
    r1 = x - hi.astype(F32)
    mid = r1.astype(BF16)
    lo = (r1 - mid.astype(F32)).astype(BF16)
    return _bmm(mask, hi) + _bmm(mask, mid) + _bmm(mask, lo)


PAIR = 2


def _tile_inputs(is_ret, qkvg_refs, aux, nc):
    shape3 = (nc, CHUNK, LANES)
    q_ref, k_ref, v_ref, g_ref = qkvg_refs
    low_lanes = lax.broadcasted_iota(jnp.int32, (1, LANES), 1) < 64
    ri = lax.broadcasted_iota(jnp.int32, (PAIR * nc, CHUNK, CHUNK), 1)
    ci = lax.broadcasted_iota(jnp.int32, (PAIR * nc, CHUNK, CHUNK), 2)
    qs, ks, vs, bs, gates, extra = [], [], [], [], [], []
    for hd in range(PAIR):
        q_blk, k_blk = q_ref[...], k_ref[...]
        if hd == 1:
            q_blk, k_blk = pltpu.roll(q_blk, 64, 1), pltpu.roll(k_blk, 64, 1)
        q_raw, k_raw = jnp.where(low_lanes, q_blk, 0.0), jnp.where(low_lanes, k_blk, 0.0)
        vs.append(v_ref[:, LANES * hd:LANES * (hd + 1)].reshape(shape3))
        gates.append(g_ref[:, LANES * hd:LANES * (hd + 1)])
        if is_ret:
            cos_ref, sa_ref, sb_ref, lg_ref = aux
            cos, sa, sb = cos_ref[...], sa_ref[...], sb_ref[...]
            q = _rot(q_raw, cos, sa, sb)
            k = _rot(k_raw, cos, sa, sb) * QK_SCALE
            steps = (lax.broadcasted_iota(jnp.int32, shape3, 1) + 1).astype(F32)
            bs.append(steps * lg_ref[hd])
            extra.append(jnp.exp(jnp.abs(ri[0:nc] - ci[0:nc]).astype(F32) * lg_ref[hd][:, 0:CHUNK]))
        else:
            glow_ref, wa2_ref, ba_ref = aux
            lanes = slice(LANES * hd, LANES * (hd + 1))
            logit = _dot(glow_ref[...].astype(BF16), wa2_ref[:, lanes]) + ba_ref[:, lanes]
            la = (jnp.minimum(logit, 0.0) - jnp.log1p(jnp.exp(-jnp.abs(logit)))) * (1.0 / GATE_NORM)
            bs.append(_masked_sum((ci[0:nc] <= ri[0:nc]).astype(BF16), la.reshape(shape3)))
            extra.append(logit)
            q = q_raw * QK_SCALE
            k = k_raw
        qs.append(q.reshape(shape3))
        ks.append(k.reshape(shape3))
    cat = lambda parts: jnp.concatenate(parts, axis=0)
    return cat(qs), cat(ks), cat(vs), gates, cat(bs), extra, ri, ci


def _tile_scores(q, k, b, ri, ci):
    mid = b[:, CHUNK // 2 - 1:CHUNK // 2, :]
    ep = jnp.exp(b - mid)
    en = jnp.exp(mid - b)
    qt, kt, qh, kh = q * ep, k * en, q * en, k * ep
    low = _bmm_nt(qt.astype(BF16), kt.astype(BF16))
    upp = _bmm_nt(qh.astype(BF16), kh.astype(BF16))
    scores = jnp.where(ci <= ri, low, upp)
    return scores, ep, en, qt, kt, qh, kh


def _attn_specs(is_ret, t, tb, imap_t):
    nb = t // tb
    base = 0 if is_ret else 12
    wide = PAIR * LANES
    proj = [pl.BlockSpec((tb, LANES), lambda p, i: (imap_t(i), base + p)),
            pl.BlockSpec((tb, LANES), lambda p, i: (imap_t(i), base + 2 + p)),
            pl.BlockSpec((tb, wide), lambda p, i: (imap_t(i), (base + 4) // 2 + p)),
            pl.BlockSpec((tb, wide), lambda p, i: (imap_t(i), (base + 8) // 2 + p))]
    lane_t = pl.BlockSpec((tb, LANES), lambda p, i: (imap_t(i), 0))
    if is_ret:
        aux = [lane_t, lane_t, lane_t, pl.BlockSpec((PAIR, 1, LANES), lambda p, i: (p, 0, 0))]
    else:
        aux = [pl.BlockSpec((tb, LANES), lambda p, i: (imap_t(i), PROJ_P // LANES - 1)),
               pl.BlockSpec((LANES, wide), lambda p, i: (0, p)),
               pl.BlockSpec((1, wide), lambda p, i: (0, p))]
    gain = pl.BlockSpec((1, wide), lambda p, i: (0, p))
    pair_t = pl.BlockSpec((tb, wide), lambda p, i: (imap_t(i), p))
    narrow_t = pl.BlockSpec((tb, LANES), lambda p, i: (imap_t(i), p))
    state = pl.BlockSpec((PAIR, tb // CHUNK, LANES, LANES), lambda p, i: (p, imap_t(i), 0, 0))
    return nb, proj, aux, gain, pair_t, narrow_t, state


def _attn_fwd(is_ret, proj, aux_arrays, gain, name, hosted=None):
    t = proj.shape[0]
    tb = min(t, ATTN_TILE)
    nc = tb // CHUNK
    n_aux = 4 if is_ret else 3
    nb, proj_spec, aux_specs, gain_spec, pair_t, _, state_spec = _attn_specs(is_ret, t, tb, lambda i: i)

    def body(*refs):
        qkvg_refs = refs[0:4]
        aux = refs[4:4 + n_aux]
        gn_ref, ofin_ref, oraw_ref, st_ref, state = refs[4 + n_aux:]

        @pl.when(pl.program_id(1) == 0)
        def _():
            state[...] = jnp.zeros_like(state)

        q, k, v, gates, b, extra, ri, ci = _tile_inputs(is_ret, qkvg_refs, aux, nc)
        if is_ret:
            scores = _bmm_nt(q.astype(BF16), k.astype(BF16)) * jnp.concatenate(extra, axis=0)
        else:
            scores = _tile_scores(q, k, b, ri, ci)[0]
        vb = v.astype(BF16)
        intra = _bmm(scores.astype(BF16), vb)
        b_last = b[:, CHUNK - 1:CHUNK, :]
        e_last = jnp.exp(b_last)
        grow = _bmm_tn(vb, (k * jnp.exp(b_last - b)).astype(BF16))
        for hd in range(PAIR):
            st = state[hd]
            for c in range(nc):
                st_ref[hd, c] = st
                st = st * e_last[hd * nc + c] + grow[hd * nc + c]
            state[hd] = st
        starts = st_ref[...].reshape(PAIR * nc, LANES, LANES)
        out3 = intra + _bmm_nt((q * jnp.exp(b)).astype(BF16), starts.astype(BF16))
        for hd in range(PAIR):
            lanes = slice(LANES * hd, LANES * (hd + 1))
            out = out3[hd * nc:(hd + 1) * nc].reshape(tb, LANES)
            oraw_ref[:, lanes] = out
            normed = out * _rms_scale(out)
            gate = gates[hd]
            ofin_ref[:, lanes] = ((normed * gn_ref[:, lanes]) * (gate * jax.nn.sigmoid(gate))).astype(BF16)

    width = HEADS * LANES
    return _call(
        body, (proj, proj, proj, proj, *aux_arrays, gain), name=name, grid=(HEADS // PAIR, nb),
        in_specs=proj_spec + aux_specs + [gain_spec],
        out_specs=[pair_t, pair_t, state_spec],
        out_shape=[jax.ShapeDtypeStruct((t, width), BF16), jax.ShapeDtypeStruct((t, width), F32),
                   jax.ShapeDtypeStruct((HEADS, t // CHUNK, LANES, LANES), F32)],
        scratch_shapes=[pltpu.VMEM((PAIR, LANES, LANES), F32)],
        compiler_params=_ARB2, hosted=hosted)


def _attn_bwd(is_ret, proj, aux_arrays, gain, o_raw, states, d_out, name, hosted=None):
    t = proj.shape[0]
    tb = min(t, ATTN_TILE)
    nc = tb // CHUNK
    n_aux = 4 if is_ret else 3
    nblk = t // tb
    nb, proj_spec, aux_specs, gain_spec, pair_t, narrow_t, state_spec = _attn_specs(
        is_ret, t, tb, lambda i: nblk - 1 - i)
    base = 0 if is_ret else HEADS // PAIR
    dout_spec = pl.BlockSpec((tb, PAIR * LANES), lambda p, i: (nblk - 1 - i, base + p))

    def body(*refs):
        qkvg_refs = refs[0:4]
        aux = refs[4:4 + n_aux]
        gn_ref, oraw_ref, st_ref, dfin_ref = refs[4 + n_aux:8 + n_aux]
        dq_ref, dk_ref, dv_ref, dgate_ref, dgn_ref = refs[8 + n_aux:13 + n_aux]
        if is_ret:
            dstate, dafter_ref = refs[13 + n_aux:]
        else:
            dlogit_ref, dba_ref, dstate, dafter_ref = refs[13 + n_aux:]

        @pl.when(pl.program_id(1) == 0)
        def _():
            dstate[...] = jnp.zeros_like(dstate)
            dgn_ref[...] = jnp.zeros_like(dgn_ref)
            if not is_ret:
                dba_ref[...] = jnp.zeros_like(dba_ref)

        shape3 = (nc, CHUNK, LANES)
        q, k, v, gates, b, extra, ri, ci = _tile_inputs(is_ret, qkvg_refs, aux, nc)
        eb = jnp.exp(b)
        qe = q * eb
        b_last = b[:, CHUNK - 1:CHUNK, :]
        e_last = jnp.exp(b_last)
        ekd = jnp.exp(b_last - b)
        kd = k * ekd

        d_os = []
        for hd in range(PAIR):
            lanes = slice(LANES * hd, LANES * (hd + 1))
            gn, gate = gn_ref[:, lanes], gates[hd]
            out = oraw_ref[:, lanes]
            r = _rms_scale(out)
            normed = out * r
            sg = jax.nn.sigmoid(gate)
            dfin = dfin_ref[:, lanes]
            dgate_ref[:, lanes] = (dfin * (normed * gn) * _silu_grad(gate, sg)).astype(BF16)
            dpre = dfin * (gate * sg)
            dgn_ref[:, lanes] += jnp.sum(dpre * normed, axis=0, keepdims=True)
            dnormed = dpre * gn
            d_o = r * (dnormed - normed * jnp.mean(dnormed * normed, axis=-1, keepdims=True))
            d_os.append(d_o.reshape(shape3))
        dob, vb = jnp.concatenate(d_os, axis=0).astype(BF16), v.astype(BF16)

        dgrow = _bmm_tn(dob, qe.astype(BF16))
        for hd in range(PAIR):
            dst = dstate[hd]
            for c in reversed(range(nc)):
                dafter_ref[hd * nc + c] = dst
                dst = dst * e_last[hd * nc + c] + dgrow[hd * nc + c]
            dstate[hd] = dst
        st = st_ref[...].reshape(PAIR * nc, LANES, LANES)
        dafter = dafter_ref[...]
        stb, dafter_b = st.astype(BF16), dafter.astype(BF16)

        dsc = _bmm_nt(dob, vb)
        dsc_t = _bmm_nt(vb, dob)
        dqe = _bmm(dob, stb)
        dkd = _bmm(vb, dafter_b)
        if is_ret:
            decay, qb, kb = jnp.concatenate(extra, axis=0), q.astype(BF16), k.astype(BF16)
            scores_t = _bmm_nt(kb, qb) * decay
            dq = _bmm((dsc * decay).astype(BF16), kb) + dqe * eb
            dk = _bmm((dsc_t * decay).astype(BF16), qb) + dkd * ekd
        else:
            _, ep, en, qt, kt, qh, kh = _tile_scores(q, k, b, ri, ci)
            qtb, ktb, qhb, khb = qt.astype(BF16), kt.astype(BF16), qh.astype(BF16), kh.astype(BF16)
            scores_t = jnp.where(ci >= ri, _bmm_nt(ktb, qtb), _bmm_nt(khb, qhb))
            dqt = _bmm(jnp.where(ci <= ri, dsc, 0.0).astype(BF16), ktb)
            dqh = _bmm(jnp.where(ci <= ri, 0.0, dsc).astype(BF16), khb)
            dkt = _bmm(jnp.where(ci >= ri, dsc_t, 0.0).astype(BF16), qtb)
            dkh = _bmm(jnp.where(ci >= ri, 0.0, dsc_t).astype(BF16), qhb)
            dq = dqt * ep + dqh * en + dqe * eb
            dk = dkt * en + dkh * ep + dkd * ekd
        dv = _bmm(scores_t.astype(BF16), dob) + _bmm_nt(kd.astype(BF16), dafter_b)

        if not is_ret:
            db = dqt * qt - dkt * kt - dqh * qh + dkh * kh + dqe * qe - dkd * kd
            db_last = (jnp.sum(dkd * kd, axis=1, keepdims=True)
                       + jnp.sum(dafter * st, axis=1, keepdims=True) * e_last)
            last_row = lax.broadcasted_iota(jnp.int32, (PAIR * nc, CHUNK, LANES), 1) == CHUNK - 1
            db = db + jnp.where(last_row, db_last, 0.0)
            dla = _masked_sum((ci >= ri).astype(BF16), db)

        dq_pair, dk_pair = [], []
        for hd in range(PAIR):
            lanes = slice(LANES * hd, LANES * (hd + 1))
            rows3 = slice(hd * nc, (hd + 1) * nc)
            dq_h, dk_h = dq[rows3].reshape(tb, LANES), dk[rows3].reshape(tb, LANES)
            if is_ret:
                cos_ref, sa_ref, sb_ref, _ = aux
                cos, sa, sb = cos_ref[...], sa_ref[...], sb_ref[...]
                dq_h = _rot_t(dq_h, cos, sa, sb)
                dk_h = _rot_t(dk_h, cos, sa, sb) * QK_SCALE
            else:
                dq_h = dq_h * QK_SCALE
                dlogit = dla[rows3].reshape(tb, LANES) * (1.0 / GATE_NORM) * jax.nn.sigmoid(-extra[hd])
                dlogit_ref[:, lanes] = dlogit.astype(BF16)
                dba_ref[:, lanes] += jnp.sum(dlogit, axis=0, keepdims=True)
            dq_pair.append(dq_h)
            dk_pair.append(dk_h)
            dv_ref[:, lanes] = dv[rows3].reshape(tb, LANES).astype(BF16)
        dq_ref[...] = (dq_pair[0] + pltpu.roll(dq_pair[1], 64, 1)).astype(BF16)
        dk_ref[...] = (dk_pair[0] + pltpu.roll(dk_pair[1], 64, 1)).astype(BF16)

    width = HEADS * LANES
    row_out = pl.BlockSpec((1, PAIR * LANES), lambda p, i: (0, p))
    out_specs = [narrow_t, narrow_t, pair_t, pair_t, row_out]
    out_shape = ([jax.ShapeDtypeStruct((t, width // 2), BF16)] * 2 + [jax.ShapeDtypeStruct((t, width), BF16)] * 2
                 + [jax.ShapeDtypeStruct((1, width), F32)])
    if not is_ret:
        out_specs += [pair_t, row_out]
        out_shape += [jax.ShapeDtypeStruct((t, width), BF16), jax.ShapeDtypeStruct((1, width), F32)]
    return _call(
        body, (proj, proj, proj, proj, *aux_arrays, gain, o_raw, states, d_out), name=name,
        grid=(HEADS // PAIR, nblk),
        in_specs=proj_spec + aux_specs + [gain_spec, pair_t, state_spec, dout_spec],
        out_specs=out_specs, out_shape=out_shape,
        scratch_shapes=[pltpu.VMEM((PAIR, LANES, LANES), F32), pltpu.VMEM((PAIR * nc, LANES, LANES), F32)],
        compiler_params=_ARB2, hosted=hosted)


PEER_SLOT = (2, 1, 3)


def _place():
    x, y, c = lax.axis_index("x"), lax.axis_index("y"), lax.axis_index("c")
    chips = [(1 - x, y), (x, 1 - y), (1 - x, 1 - y)]
    return x, y, c, 2 * x + y, chips


def _route_split(rows, dtype):
    tile = 16 if dtype == BF16 else 8
    if rows < 2 * tile:
        return None
    return -(-(rows // 2) // tile) * tile


def _routes(by_peer):
    x, y, c, me, chips = _place()
    (xx, xy), (yx, yy), (dx, dy) = chips
    if by_peer:
        slots = dict(own=0, from_x=PEER_SLOT[0], from_y=PEER_SLOT[1], diag=PEER_SLOT[2],
                     mine_on_x=PEER_SLOT[0], mine_on_y=PEER_SLOT[1])
    else:
        slots = dict(own=me, from_x=2 * xx + xy, from_y=2 * yx + yy, diag=2 * dx + dy, mine_on_x=me, mine_on_y=me)
    return c, (xx, xy, c), (yx, yy, c), (dx, dy, c), (x, y, 1 - c), slots


def _gather_legs(src, out, send_sems, recv_sems, base, by_peer):
    c, to_x, to_y, to_d, sibling, s = _routes(by_peer)
    r0 = _route_split(src.shape[2], src.dtype)

    def cp(k, src_ref, dst_ref, to):
        return pltpu.make_async_remote_copy(src_ref=src_ref, dst_ref=dst_ref, send_sem=send_sems.at[base + k],
                                            recv_sem=recv_sems.at[base + k], device_id=to, device_id_type=MESH)

    mine = src.at[:, c]
    legs = dict(
        x=(cp(0, mine, out.at[s["mine_on_x"], :, c], to_x), cp(0, mine, out.at[s["from_x"], :, c], to_x)),
        y=(cp(1, mine, out.at[s["mine_on_y"], :, c], to_y), cp(1, mine, out.at[s["from_y"], :, c], to_y)),
        pass_x=(cp(4, out.at[s["from_x"], :, c], out.at[s["from_x"], :, c], sibling),
                cp(4, mine, out.at[s["from_x"], :, 1 - c], sibling)),
        pass_y=(cp(5, out.at[s["from_y"], :, c], out.at[s["from_y"], :, c], sibling),
                cp(5, mine, out.at[s["from_y"], :, 1 - c], sibling)))
    if r0 is None:
        mine_on_d = s["diag"] if by_peer else s["own"]
        legs["d"] = (cp(2, mine, out.at[mine_on_d, :, c], to_d), cp(2, mine, out.at[s["diag"], :, c], to_d))
        legs["pass_d"] = (cp(6, out.at[s["diag"], :, c], out.at[s["diag"], :, c], sibling),
                          cp(6, mine, out.at[s["diag"], :, 1 - c], sibling))
        return legs, False
    lo, hi = pl.ds(0, r0), pl.ds(r0, src.shape[2] - r0)
    fx_on_y = s["diag"] if by_peer else s["from_x"]
    fy_on_x = s["diag"] if by_peer else s["from_y"]
    legs.update(
        fwd_y=(cp(2, out.at[s["from_x"], :, c, lo], out.at[fx_on_y, :, c, lo], to_y),
               cp(2, mine.at[:, lo], out.at[s["diag"], :, c, lo], to_y)),
        fwd_x=(cp(3, out.at[s["from_y"], :, c, hi], out.at[fy_on_x, :, c, hi], to_x),
               cp(3, mine.at[:, hi], out.at[s["diag"], :, c, hi], to_x)),
        pass_d0=(cp(6, out.at[s["diag"], :, c, lo], out.at[s["diag"], :, c, lo], sibling),
                 cp(6, mine.at[:, lo], out.at[s["diag"], :, 1 - c, lo], sibling)),
        pass_d1=(cp(7, out.at[s["diag"], :, c, hi], out.at[s["diag"], :, c, hi], sibling),
                 cp(7, mine.at[:, hi], out.at[s["diag"], :, 1 - c, hi], sibling)))
    return legs, True


def _gather_steps(legs, routed):
    def start():
        legs["x"][0].start()
        legs["y"][0].start()
        if not routed:
            legs["d"][0].start()

    def middle():
        legs["x"][1].wait_recv()
        if routed:
            legs["fwd_y"][0].start()
        legs["pass_x"][0].start()
        legs["y"][1].wait_recv()
        if routed:
            legs["fwd_x"][0].start()
        legs["pass_y"][0].start()

    def finish():
        last = ["pass_d0", "pass_d1"] if routed else ["pass_d"]
        if routed:
            legs["fwd_y"][1].wait_recv()
            legs["pass_d0"][0].start()
            legs["fwd_x"][1].wait_recv()
            legs["pass_d1"][0].start()
        else:
            legs["d"][1].wait_recv()
            legs["pass_d"][0].start()
        for name in ["pass_x", "pass_y"] + last:
            legs[name][1].wait_recv()
        for name in ["x", "y", "pass_x", "pass_y"] + last + (["fwd_y", "fwd_x"] if routed else ["d"]):
            legs[name][0].wait_send()

    return start, middle, finish


def _gather_plan(arrs):
    na = len(arrs)

    def steps(ins, outs, send_sems, recv_sems):
        return [_gather_steps(*_gather_legs(ins[a], outs[a], send_sems, recv_sems, 8 * a, False)) for a in range(na)]

    def run(which):
        def hook(*refs):
            for step in steps(*refs):
                step[which]()
        return hook

    routed = all(_route_split(a.shape[2], a.dtype) is not None for a in arrs)
    return _Hosted(arrs, [jax.ShapeDtypeStruct((N_CHIPS,) + a.shape, a.dtype) for a in arrs], 8 * na,
                   run(0), run(2), middle=run(1), peers="neighbours" if routed else "chips_sibling")


def _pair_exchange_plan(grads):
    na = len(grads)

    def copies(ins, outs, send_sems, recv_sems):
        x, y, c, _, _ = _place()
        return [pltpu.make_async_remote_copy(
            src_ref=ins[a].at[:, 1 - c], dst_ref=outs[a], send_sem=send_sems.at[a], recv_sem=recv_sems.at[a],
            device_id=(x, y, 1 - c), device_id_type=MESH) for a in range(na)]

    def start(*refs):
        for cp in copies(*refs):
            cp.start()

    def finish(*refs):
        for cp in copies(*refs):
            cp.wait()

    return _Hosted(grads, [jax.ShapeDtypeStruct(g.shape[:1] + g.shape[2:], g.dtype) for g in grads], na, start, finish,
                   peers="sibling")


def _small_gather_plan(block):
    def copies(ins, outs, send_sems, recv_sems):
        x, y, c, _, chips = _place()
        peers = [(x, y, 1 - c)] + [(px, py, pc) for px, py in chips for pc in (c, 1 - c)]
        sends = [pltpu.make_async_remote_copy(
            src_ref=ins[0], dst_ref=outs[0].at[4 * x + 2 * y + c], send_sem=send_sems.at[k], recv_sem=recv_sems.at[k],
            device_id=peer, device_id_type=MESH) for k, peer in enumerate(peers)]
        recvs = [pltpu.make_async_remote_copy(
            src_ref=ins[0], dst_ref=outs[0].at[4 * px + 2 * py + pc], send_sem=send_sems.at[k], recv_sem=recv_sems.at[k],
            device_id=(px, py, pc), device_id_type=MESH) for k, (px, py, pc) in enumerate(peers)]
        return sends, recvs

    def start(*refs):
        for cp in copies(*refs)[0]:
            cp.start()

    def finish(*refs):
        sends, recvs = copies(*refs)
        for cp in recvs:
            cp.wait_recv()
        for cp in sends:
            cp.wait_send()

    return _Hosted([block], [jax.ShapeDtypeStruct((8,) + block.shape, block.dtype)], 7, start, finish)


def _sum_devices(blocks):
    def body(b_ref, o_ref):
        acc = b_ref[0]
        for d in range(1, 8):
            acc = acc + b_ref[d]
        o_ref[...] = acc

    return pl.pallas_call(body, name="sum_devices", in_specs=[_VMEM], out_specs=_VMEM,
                          out_shape=jax.ShapeDtypeStruct(blocks.shape[1:], blocks.dtype))(blocks)


def _pair_add(grad, recv, c_arr, name):
    _, _, r, cols = grad.shape

    def body(c_ref, g_ref, r_ref, o_ref):
        o_ref[...] = (g_ref[...].astype(F32) + r_ref[...].astype(F32)).astype(BF16)

    return pl.pallas_call(
        body, name=name,
        grid_spec=pltpu.PrefetchScalarGridSpec(
            num_scalar_prefetch=1, grid=(N_CHIPS,),
            in_specs=[pl.BlockSpec((None, None, r, cols), lambda p, c_ref: (p, c_ref[0], 0, 0)),
                      pl.BlockSpec((None, r, cols), lambda p, c_ref: (p, 0, 0))],
            out_specs=pl.BlockSpec((None, r, cols), lambda p, c_ref: (p, 0, 0))),
        out_shape=jax.ShapeDtypeStruct((N_CHIPS, r, cols), BF16),
        compiler_params=_ARB1,
    )(c_arr, grad, recv)


def _chip_exchange_plan(sums, by_peer=False):
    na = len(sums)

    def copies(ins, outs, send_sems, recv_sems):
        x, y, c, me, chips = _place()

        def copy(a, j, px, py, block, slot):
            return pltpu.make_async_remote_copy(
                src_ref=ins[a].at[block], dst_ref=outs[a].at[slot],
                send_sem=send_sems.at[3 * a + j], recv_sem=recv_sems.at[3 * a + j],
                device_id=(px, py, c), device_id_type=MESH)

        peers = [(a, j, px, py) for a in range(na) for j, (px, py) in enumerate(chips)]
        return me, peers, copy

    def start(*refs):
        me, peers, copy = copies(*refs)
        for a, j, px, py in peers:
            if by_peer:
                copy(a, j, px, py, PEER_SLOT[j], PEER_SLOT[j]).start()
            else:
                copy(a, j, px, py, 2 * px + py, me).start()

    def finish(*refs):
        me, peers, copy = copies(*refs)
        for a, j, px, py in peers:
            if by_peer:
                copy(a, j, px, py, PEER_SLOT[j], PEER_SLOT[j]).wait_recv()
            else:
                copy(a, j, px, py, me, 2 * px + py).wait_recv()
        for a, j, px, py in peers:
            if by_peer:
                copy(a, j, px, py, PEER_SLOT[j], PEER_SLOT[j]).wait_send()
            else:
                copy(a, j, px, py, 2 * px + py, me).wait_send()

    return _Hosted(sums, [jax.ShapeDtypeStruct(s.shape, s.dtype) for s in sums], 3 * na, start, finish, peers="chips")


def _chip_sum(own, recv, me_arr, name):
    _, r, cols = recv.shape

    def body(me_ref, own_ref, r_ref, o_ref):
        o_ref[...] = jnp.zeros_like(o_ref)
        for q in range(N_CHIPS):
            @pl.when(me_ref[0] == q)
            def _():
                o_ref[...] += own_ref[...].astype(F32)

            @pl.when(me_ref[0] != q)
            def _():
                o_ref[...] += r_ref[q].astype(F32)

    return pl.pallas_call(
        body, name=name,
        grid_spec=pltpu.PrefetchScalarGridSpec(
            num_scalar_prefetch=1, grid=(1,),
            in_specs=[pl.BlockSpec((None, r, cols), lambda i, me_ref: (me_ref[0], 0, 0)),
                      pl.BlockSpec((N_CHIPS, r, cols), lambda i, me_ref: (0, 0, 0))],
            out_specs=pl.BlockSpec((r, cols), lambda i, me_ref: (0, 0))),
        out_shape=jax.ShapeDtypeStruct((r, cols), F32),
        compiler_params=_ARB1,
    )(me_arr, own, recv)


def _peer_sum(own, recv, name):
    _, r, cols = recv.shape

    def body(own_ref, r_ref, o_ref):
        acc = own_ref[...].astype(F32) + r_ref[1].astype(F32)
        acc = acc + r_ref[2].astype(F32)
        o_ref[...] = acc + r_ref[3].astype(F32)

    return pl.pallas_call(
        body, name=name, grid=(1,),
        in_specs=[pl.BlockSpec((None, r, cols), lambda i: (0, 0, 0)), pl.BlockSpec((N_CHIPS, r, cols), lambda i: (0, 0, 0))],
        out_specs=pl.BlockSpec((r, cols), lambda i: (0, 0)),
        out_shape=jax.ShapeDtypeStruct((r, cols), F32),
        compiler_params=_ARB1,
    )(own, recv)


def _pair_share_plan(halves):
    na = len(halves)

    def copies(ins, outs, send_sems, recv_sems):
        x, y, c, _, _ = _place()
        return [pltpu.make_async_remote_copy(
            src_ref=ins[a], dst_ref=outs[a], send_sem=send_sems.at[a], recv_sem=recv_sems.at[a],
            device_id=(x, y, 1 - c), device_id_type=MESH) for a in range(na)]

    def start(*refs):
        for cp in copies(*refs):
            cp.start()

    def finish(*refs):
        for cp in copies(*refs):
            cp.wait()

    return _Hosted(halves, [jax.ShapeDtypeStruct(h.shape, h.dtype) for h in halves], na, start, finish,
                   peers="sibling")


def _row_tile(rows):
    best = rows
    for cand in range(8, min(rows, 512) + 1, 8):
        if rows % cand == 0:
            best = cand
    return best


def _adamw_math(w, g, m, v):
    m2 = ADAM_B1 * m + (1.0 - ADAM_B1) * g
    v2 = ADAM_B2 * v + (1.0 - ADAM_B2) * (g * g)
    m_hat = m2 / (1.0 - ADAM_B1 ** ADAM_STEP)
    v_hat = v2 / (1.0 - ADAM_B2 ** ADAM_STEP)
    return -ADAM_LR * (m_hat / (jnp.sqrt(v_hat) + ADAM_EPS) + ADAM_WD * w), m2, v2


def _adamw_halves(w, g_mine, g_other, m, v, c_arr, name):
    rows, cols = w.shape
    r = rows // 2
    tr = _row_tile(r)
    nt = r // tr

    def body(c_ref, w_ref, gm_ref, go_ref, m_ref, v_ref, g_ref, d_ref, nm_ref, nv_ref):
        gv = jnp.where(pl.program_id(0) == c_ref[0], gm_ref[...], go_ref[...])
        g_ref[...] = gv
        d_ref[...], nm_ref[...], nv_ref[...] = _adamw_math(w_ref[...], gv, m_ref[...], v_ref[...])

    full = pl.BlockSpec((tr, cols), lambda h, i, c_ref: (h * nt + i, 0))
    half = pl.BlockSpec((tr, cols), lambda h, i, c_ref: (i, 0))
    shape = jax.ShapeDtypeStruct((rows, cols), F32)
    return pl.pallas_call(
        body, name=name,
        grid_spec=pltpu.PrefetchScalarGridSpec(
            num_scalar_prefetch=1, grid=(2, nt),
            in_specs=[full, half, half, full, full], out_specs=[full] * 4),
        out_shape=[shape] * 4,
        compiler_params=_ARB2,
    )(c_arr, w, g_mine, g_other, m, v)


def _pad_w_in_t(w_in_t):
    return jnp.pad(w_in_t, ((0, PROJ_P - IN_WIDTH), (0, 0)))


def _unpad_w_in_t(w_pt):
    return w_pt[0:IN_WIDTH]


def _rope_tables(t):
    half = 32
    inv = ROPE_BASE ** (-jnp.arange(half, dtype=F32) * 2.0 / 64)
    ang = jnp.arange(t, dtype=F32)[:, None] * inv[None, :]
    cos, sin = jnp.cos(ang), jnp.sin(ang)
    z32, z64 = jnp.zeros((t, 32), F32), jnp.zeros((t, 64), F32)
    return (jnp.concatenate([cos, cos, z64], axis=1),
            jnp.concatenate([-sin, z32, z64], axis=1),
            jnp.concatenate([z32, sin, z64], axis=1))


def _halves(w):
    n, rows, cols = w.shape
    return w.reshape(n, 2, rows // 2, cols)


_VMEM = pl.BlockSpec(memory_space=pltpu.VMEM)


def _pack_small(n1, nm, n2, nf, nret, ngla, ba, wa2_p, loss_blk):
    def body(n1_ref, nm_ref, n2_ref, nf_ref, nret_ref, ngla_ref, ba_ref, wa2_ref, loss_ref, o_ref):
        o_ref[...] = jnp.zeros_like(o_ref)
        o_ref[0:1, :] = n1_ref[...]
        o_ref[1:2, :] = nm_ref[...]
        o_ref[2:3, :] = n2_ref[...]
        o_ref[3:4, :] = nf_ref[...]
        o_ref[4:5, 0:512] = nret_ref[...]
        o_ref[4:5, 512:1024] = ngla_ref[...]
        o_ref[5:6, 0:256] = ba_ref[...]
        o_ref[6:7, 0:LANES] = loss_ref[0:1, :]
        o_ref[8:8 + GATE_RANK, 0:HEADS * LANES] = wa2_ref[0:GATE_RANK, :]

    return pl.pallas_call(
        body, name="pack_small", in_specs=[_VMEM] * 9, out_specs=_VMEM,
        out_shape=jax.ShapeDtypeStruct((SMALL_ROWS, D_MODEL), F32),
    )(n1, nm, n2, nf, nret, ngla, ba, wa2_p, loss_blk)


def _small_update(summed, chip_arr, ws, ms, vs):
    n = len(ws)

    def body(chip_ref, s_ref, *refs):
        w_refs, m_refs, v_refs = refs[0:n], refs[n:2 * n], refs[2 * n:3 * n]
        outs = refs[3 * n:]
        wa2_all = s_ref[8:8 + GATE_RANK, 0:HEADS * LANES]
        wa2_g = jnp.zeros((GATE_RANK, 64), F32)
        for p in range(N_CHIPS):
            wa2_g = jnp.where(chip_ref[0] == p, wa2_all[:, LANES * p:LANES * p + 64], wa2_g)
        grads = [s_ref[0:1, :], s_ref[1:2, :], s_ref[2:3, :], s_ref[3:4, :], s_ref[4:5, 0:512],
                 s_ref[4:5, 512:1024], s_ref[5:6, 0:256], wa2_g]
        for k in range(n):
            d, m2, v2 = _adamw_math(w_refs[k][...], grads[k], m_refs[k][...], v_refs[k][...])
            outs[k][...] = grads[k]
            outs[n + k][...] = d
            outs[2 * n + k][...] = m2
            outs[3 * n + k][...] = v2

    shapes = [jax.ShapeDtypeStruct(w.shape, F32) for w in ws] * 4
    smem = pl.BlockSpec(memory_space=pltpu.SMEM)
    outs = pl.pallas_call(
        body, name="small_update", in_specs=[smem] + [_VMEM] * (1 + 3 * n), out_specs=[_VMEM] * (4 * n),
        out_shape=shapes,
    )(chip_arr, summed, *ws, *ms, *vs)
    return outs[0:n], outs[n:2 * n], outs[2 * n:3 * n], outs[3 * n:4 * n]


def _pad_in_rows(w_t):
    return jnp.pad(w_t, ((0, IN_ROWS - IN_SHARD), (0, 0)))


def _forward_backward(xs, target, ffn1_w, rest, ba_p, ffn1_norm_g, mix_norm_g, ret_norm_g, gla_norm_g, ffn2_norm_g,
                      final_norm_g, ffn1_gather=None, rest_plan=None, rest_weights=None, ffn2_plans=None,
                      ffn2_weights=None, ffn2_pairs=None, ffn2_pairs_done=None, early=None, late=None, small_plan=None):
    t = xs.shape[0]
    cos_t, sa_t, sb_t = _rope_tables(t)
    log_gamma = jnp.log(1.0 - 2.0 ** (-5.0 - jnp.arange(HEADS, dtype=F32)))
    lg_t = jnp.broadcast_to(log_gamma[:, None, None], (HEADS, 1, LANES))
    ret_aux = [cos_t, sa_t, sb_t, lg_t]

    if ffn1_gather is None:
        (x1, a1, u1, h1), gathered = _ffn_fwd(xs, ffn1_norm_g, ffn1_w, "ffn1_fwd", hosted=rest_plan)
    else:
        ffn1_shard, ffn1_weights = ffn1_gather
        (x1, a1, u1, h1, wall), gathered = _ffn1_fwd_gathering(xs, ffn1_norm_g, ffn1_shard, "ffn1_fwd",
                                                               hosted=rest_plan)
        ffn1_w = ffn1_weights(wall)
    ffn2_w, w_in_pt, w_out_full, wa2_p = rest if rest_plan is None else rest_weights(gathered)
    plans = [None] * 3 if ffn2_plans is None else ffn2_plans
    (proj, h_mix), got_gate = _mixer_in_fwd(x1, mix_norm_g, w_in_pt, "mixer_in_fwd", hosted=plans[0])
    gla_aux = [proj, wa2_p, ba_p]
    (o_ret, raw_ret, st_ret), got_up = _attn_fwd(True, proj, ret_aux, ret_norm_g, "ret_fwd", hosted=plans[1])
    (o_gla, raw_gla, st_gla), got_down = _attn_fwd(False, proj, gla_aux, gla_norm_g, "gla_fwd", hosted=plans[2])
    if ffn2_plans is not None:
        ffn2_w = ffn2_weights(got_gate + got_up + got_down)
    x2 = _mixer_out_fwd(o_ret, o_gla, w_out_full, x1, "mixer_out_fwd")
    (loss_blk, dx3, d_final_g, a2, u2, h2), _ = _ffn_fwd(x2, ffn2_norm_g, ffn2_w, "ffn2_fwd",
                                                       loss_head=(final_norm_g, target))

    (da2, du2, hid2, dob2, dx2, d_ffn2_g, d_o), _ = _ffn_bwd(dx3, x2, ffn2_norm_g, a2, u2, ffn2_w, "ffn2_bwd",
                                                            back_w=w_out_full)
    g_gate2 = _matmul_tn(da2, h2, "ffn2_dgate", out_dtype=BF16)
    g_up2 = _matmul_tn(du2, h2, "ffn2_dup", out_dtype=BF16)
    g_down2 = _matmul_tn(hid2, dob2, "ffn2_ddown", out_dtype=BF16)

    g_wout_ret = _matmul_tn(o_ret, dx2, "wout_grad_ret", out_dtype=BF16)
    g_wout_gla = _matmul_tn(o_gla, dx2, "wout_grad_gla", out_dtype=BF16)
    pairs_plan = None if ffn2_pairs is None else ffn2_pairs([g_gate2, g_up2, g_down2])
    (*dproj_ret, d_ret_g), pair_recv = _attn_bwd(True, proj, ret_aux, ret_norm_g, raw_ret, st_ret, d_o, "ret_bwd",
                                                 hosted=pairs_plan)
    if ffn2_pairs is not None:
        ffn2_pairs_done(pair_recv)
    (*dproj_gla, d_gla_g, dlogit, d_ba_p), _ = _attn_bwd(False, proj, gla_aux, gla_norm_g, raw_gla, st_gla, d_o,
                                                        "gla_bwd")
    d_glow = _matmul_nt(dlogit, wa2_p, "gate_low_bwd", out_dtype=BF16)
    g_wa2_p = _matmul_tn(proj[:, PROJ_P - LANES:], dlogit, "gate_w_grad")
    dproj = jnp.concatenate(dproj_ret + dproj_gla + [d_glow], axis=1)
    g_win_p = _matmul_tn(dproj, h_mix, "w_in_grad", tka=PROJ_P // PROJ_TILES, out_dtype=BF16)
    dx1, d_mix_g = _mixer_in_bwd(dproj, w_in_pt, dx2, x1, mix_norm_g, "mixer_in_bwd")
    g_win_t = _unpad_w_in_t(g_win_p[0])
    g_win = jnp.stack([_pad_in_rows(g_win_t[IN_SHARD * p:IN_SHARD * (p + 1)]) for p in range(N_CHIPS)], axis=0)
    g_wout = jnp.concatenate([g_wout_ret[0], g_wout_gla[0]], axis=0).reshape(N_CHIPS, D_MODEL // N_CHIPS, D_MODEL)

    early_grads = [g_win, g_wout] if ffn2_pairs is not None else [g_gate2, g_up2, g_down2, g_win, g_wout]
    early_plan = None if early is None else early(early_grads)
    (da1, du1, hid1, dob1, grad_x, d_ffn1_g), arrived = _ffn_bwd(dx1, xs, ffn1_norm_g, a1, u1, ffn1_w, "ffn1_bwd",
                                                                hosted=early_plan)
    d_ba = d_ba_p.reshape(HEADS, LANES)[:, 0:64].reshape(1, 256)
    small_local = _pack_small(d_ffn1_g, d_mix_g, d_ffn2_g, d_final_g, d_ret_g, d_gla_g, d_ba, g_wa2_p[0], loss_blk)
    late_grads, late_arrived = [], []
    for lhs, rhs, name in ((da1, h1, "ffn1_dgate"), (du1, h1, "ffn1_dup"), (hid1, dob1, "ffn1_ddown")):
        if late is None:
            plan = None
        else:
            plan = late(late_grads[-1], len(late_grads)) if late_grads else small_plan(small_local)
        res = _matmul_tn(lhs, rhs, name, out_dtype=BF16, hosted=plan)
        if plan is not None:
            res, carried = res
            late_arrived += carried
        late_grads.append(res)
    g_gate1, g_up1, g_down1 = late_grads

    return (small_local, grad_x, g_gate1, g_up1, g_down1, g_gate2, g_up2, g_down2, g_win, g_wout, g_wa2_p,
            d_ba_p, d_ffn1_g, d_mix_g, d_ffn2_g, d_final_g, d_ret_g, d_gla_g, arrived, late_arrived)


def kernel(x, ffn1_norm_g, ffn1_w_gate, ffn1_w_up, ffn1_w_down, mix_norm_g, w_in, ret_norm_g, gla_w_a2, gla_b_a, gla_norm_g, w_out, ffn2_norm_g, ffn2_w_gate, ffn2_w_up, ffn2_w_down, final_norm_g, loss_target, m_ffn1_norm_g, m_ffn1_w_gate, m_ffn1_w_up, m_ffn1_w_down, m_mix_norm_g, m_w_in, m_ret_norm_g, m_gla_w_a2, m_gla_b_a, m_gla_norm_g, m_w_out, m_ffn2_norm_g, m_ffn2_w_gate, m_ffn2_w_up, m_ffn2_w_down, m_final_norm_g, v_ffn1_norm_g, v_ffn1_w_gate, v_ffn1_w_up, v_ffn1_w_down, v_mix_norm_g, v_w_in, v_ret_norm_g, v_gla_w_a2, v_gla_b_a, v_gla_norm_g, v_w_out, v_ffn2_norm_g, v_ffn2_w_gate, v_ffn2_w_up, v_ffn2_w_down, v_final_norm_g):
    t = x.shape[1]
    xs = x.reshape(t, D_MODEL)
    target = loss_target.reshape(t, D_MODEL)
    chip = 2 * lax.axis_index("x") + lax.axis_index("y")
    c_arr = lax.axis_index("c").astype(jnp.int32).reshape(1)

    me_arr = chip.astype(jnp.int32).reshape(1)

    pad_rows = _pad_in_rows

    def own_block(gathered, shard):
        return lax.dynamic_update_slice(gathered, shard[None], (chip,) + (0,) * shard.ndim)

    ffn1_shard = _halves(jnp.stack([ffn1_w_gate[0].T, ffn1_w_up[0].T, ffn1_w_down[0]], axis=0).astype(BF16))
    rest_shards = [_halves(pad_rows(w_in[0].T).astype(BF16)[None]),
                   _halves(w_out.astype(BF16)),
                   jnp.concatenate([gla_w_a2.reshape(GATE_RANK, 64), jnp.zeros((GATE_RANK, 64), F32)],
                                   axis=1).reshape(1, 2, 8, LANES)]
    ffn2_shards = [_halves(w.astype(BF16)[None]) for w in (ffn2_w_gate[0].T, ffn2_w_up[0].T, ffn2_w_down[0])]
    def ffn1_weights(gathered):
        return lax.dynamic_update_slice(gathered, ffn1_shard[None], (0,) * 5).reshape(N_CHIPS, 3, FF_SHARD, D_MODEL)

    def rest_weights(gathered):
        win_all, wout_all, wa2_all = [own_block(g, s) for g, s in zip(gathered, rest_shards)]
        win_t = win_all.reshape(N_CHIPS, IN_ROWS, D_MODEL)
        w_in_pt = jnp.zeros((PROJ_P, D_MODEL), BF16)
        for p in range(N_CHIPS):
            w_in_pt = lax.dynamic_update_slice(w_in_pt, win_t[p, 0:IN_SHARD], (IN_SHARD * p, 0))
        wa2_p = jnp.pad(
            wa2_all.reshape(N_CHIPS, GATE_RANK, LANES).transpose(1, 0, 2).reshape(GATE_RANK, HEADS * LANES),
            ((0, LANES - GATE_RANK), (0, 0))).astype(BF16)
        return (None, w_in_pt, wout_all.reshape(D_MODEL, D_MODEL), wa2_p)

    def ffn2_weights(gathered):
        return [own_block(g, s).reshape(N_CHIPS, FF_SHARD, D_MODEL) for g, s in zip(gathered, ffn2_shards)]

    def by_halves(g):
        return g.reshape(g.shape[0], 2, g.shape[1] // 2, g.shape[2])

    def pair_adds(halves, recv, tag):
        return [_pair_add(g, r, c_arr, "pair_add_%s%d" % (tag, k)) for k, (g, r) in enumerate(zip(halves, recv))]

    def pair_sums(grads, tag):
        halves = [by_halves(g) for g in grads]
        recv = _run_hosted(_pair_exchange_plan(halves), "pair_exchange_" + tag)
        return pair_adds(halves, recv, tag)

    early_sums, ffn2_halves = [], []

    def ffn2_pairs(grads):
        ffn2_halves.extend(by_halves(g) for g in grads)
        return _pair_exchange_plan(ffn2_halves)

    def ffn2_pairs_done(recv):
        early_sums.extend(pair_adds(ffn2_halves, recv, "ffn2_"))

    def early(grads):
        early_sums.extend(pair_sums(grads, "early"))
        return _chip_exchange_plan(early_sums)

    late_sums = []

    def late(grad, number):
        late_sums.extend(pair_sums([grad], "late%d" % number))
        return _chip_exchange_plan(late_sums[-1:], by_peer=True)

    ba_p = jnp.pad(gla_b_a.reshape(HEADS, 64), ((0, 0), (0, 64))).reshape(1, HEADS * LANES)
    fb = _forward_backward(xs, target, None, None, ba_p, ffn1_norm_g, mix_norm_g, ret_norm_g, gla_norm_g,
                           ffn2_norm_g, final_norm_g.reshape(1, D_MODEL), ffn1_gather=(ffn1_shard, ffn1_weights),
                           rest_plan=_gather_plan(rest_shards), rest_weights=rest_weights,
                           ffn2_plans=[_gather_plan(ffn2_shards[0:2]), None, _gather_plan(ffn2_shards[2:3])],
                           ffn2_weights=ffn2_weights,
                           ffn2_pairs=ffn2_pairs, ffn2_pairs_done=ffn2_pairs_done, early=early, late=late,
                           small_plan=_small_gather_plan)
    (small_local, grad_x, _, _, g_down1, _, _, _, _, _, _, _, _, _, _, _, _, _, early_arrived, late_arrived) = fb
    small_all, late_arrived = late_arrived[0], late_arrived[1:]
    late_arrived = late_arrived + _run_hosted(late(g_down1, 3), "chip_exchange_late")
    mine = [_peer_sum(s, r, "chip_sum_%d" % k) for k, (s, r) in enumerate(zip(late_sums, late_arrived))]
    mine += [_chip_sum(s, r, me_arr, "chip_sum_%d" % (3 + k)) for k, (s, r) in enumerate(zip(early_sums, early_arrived))]
    other = _run_hosted(_pair_share_plan(mine), "pair_share")

    device = 2 * chip + lax.axis_index("c")
    small_sum = _sum_devices(lax.dynamic_update_slice(small_all, small_local[None], (device, 0, 0)))
    loss = small_sum[6, 0]

    def rows(n1, nm, n2, nf, nret, ngla, ba, wa2):
        return [n1, nm, n2, nf.reshape(1, D_MODEL), nret, ngla, ba, wa2.reshape(GATE_RANK, 64)]

    small = _small_update(
        small_sum, me_arr,
        rows(ffn1_norm_g, mix_norm_g, ffn2_norm_g, final_norm_g, ret_norm_g, gla_norm_g, gla_b_a, gla_w_a2),
        rows(m_ffn1_norm_g, m_mix_norm_g, m_ffn2_norm_g, m_final_norm_g, m_ret_norm_g, m_gla_norm_g, m_gla_b_a,
             m_gla_w_a2),
        rows(v_ffn1_norm_g, v_mix_norm_g, v_ffn2_norm_g, v_final_norm_g, v_ret_norm_g, v_gla_norm_g, v_gla_b_a,
             v_gla_w_a2))
    s_grad, s_delta, s_m, s_v = [
        [*o[0:3], o[3].reshape(D_MODEL), *o[4:7], o[7].reshape(1, GATE_RANK, 64)] for o in small]

    def big(k, w, m, v, name, to_2d, from_2d):
        outs4 = _adamw_halves(to_2d(w), mine[k], other[k], to_2d(m), to_2d(v), c_arr, name)
        return [from_2d(z) for z in outs4]

    plain = (lambda w: w[0], lambda z: z[None])
    transposed = (lambda w: w[0].T, lambda z: z.T[None])
    in_proj = (lambda w: pad_rows(w[0].T), lambda z: z[0:IN_SHARD].T[None])
    r_g1 = big(0, ffn1_w_gate, m_ffn1_w_gate, v_ffn1_w_gate, "adamw_ffn1_gate", *transposed)
    r_u1 = big(1, ffn1_w_up, m_ffn1_w_up, v_ffn1_w_up, "adamw_ffn1_up", *transposed)
    r_d1 = big(2, ffn1_w_down, m_ffn1_w_down, v_ffn1_w_down, "adamw_ffn1_down", *plain)
    r_g2 = big(3, ffn2_w_gate, m_ffn2_w_gate, v_ffn2_w_gate, "adamw_ffn2_gate", *transposed)
    r_u2 = big(4, ffn2_w_up, m_ffn2_w_up, v_ffn2_w_up, "adamw_ffn2_up", *transposed)
    r_d2 = big(5, ffn2_w_down, m_ffn2_w_down, v_ffn2_w_down, "adamw_ffn2_down", *plain)
    r_in = big(6, w_in, m_w_in, v_w_in, "adamw_w_in", *in_proj)
    r_out = big(7, w_out, m_w_out, v_w_out, "adamw_w_out", *plain)

    def leaves(k, smalls):
        n1, nm, n2, nf, nret, ngla, ba, wa2 = smalls
        return [n1, r_g1[k], r_u1[k], r_d1[k], nm, r_in[k], nret, wa2, ba, ngla, r_out[k], n2, r_g2[k], r_u2[k], r_d2[k], nf]

    outs = [loss, grad_x.reshape(x.shape)]
    outs += leaves(0, s_grad) + leaves(1, s_delta) + leaves(2, s_m) + leaves(3, s_v)
    return tuple(outs)
```

```python
import functools

import jax
import jax.numpy as jnp
from jax import lax
from jax.experimental import pallas as pl
from jax.experimental.pallas import tpu as pltpu

F32, BF16 = jnp.float32, jnp.bfloat16
MESH = pl.DeviceIdType.MESH
ANY = pl.BlockSpec(memory_space=pl.ANY)

D_MODEL = 1024
D_FF = 2816
N_CHIPS = 4
FF_SHARD = D_FF // N_CHIPS
IN_WIDTH = 3088
IN_SHARD = IN_WIDTH // N_CHIPS
IN_ROWS = 800
CHUNK = 64
HEADS = 4
LANES = 128
PROJ_P = 3072 + LANES
PROJ_TILES = 5
GATE_RANK = 16
QK_SCALE = 0.125
GATE_NORM = 16.0
RMS_EPS = 1e-6
ROPE_BASE = 10000.0
ADAM_LR, ADAM_B1, ADAM_B2, ADAM_EPS, ADAM_WD, ADAM_STEP = 0.001, 0.9, 0.999, 1e-08, 0.01, 10
SMALL_ROWS = 32
TOKEN_TILE = 512
ATTN_TILE = 512

_ARB2 = pltpu.CompilerParams(dimension_semantics=("arbitrary", "arbitrary"))
_ARB1 = pltpu.CompilerParams(dimension_semantics=("arbitrary",))
_ARB3 = pltpu.CompilerParams(dimension_semantics=("arbitrary", "arbitrary", "arbitrary"))


def _dot(a, b):
    return jnp.dot(a, b, preferred_element_type=F32)


def _dot_nt(a, b):
    return lax.dot_general(a, b, (((1,), (1,)), ((), ())), preferred_element_type=F32)


def _dot_tn(a, b):
    return lax.dot_general(a, b, (((0,), (0,)), ((), ())), preferred_element_type=F32)


def _rms_scale(xv):
    return lax.rsqrt(jnp.mean(xv * xv, axis=-1, keepdims=True) + RMS_EPS)


def _rms_bwd(dh, xv, g):
    r = _rms_scale(xv)
    xhat = xv * r
    dxhat = dh * g
    dx = r * (dxhat - xhat * jnp.mean(dxhat * xhat, axis=-1, keepdims=True))
    return dx, jnp.sum(dh * xhat, axis=0, keepdims=True)


def _silu_grad(a, sg):
    return sg * (1.0 + a * (1.0 - sg))


class _Hosted:
    def __init__(self, arrays, out_shapes, n_sems, start, finish, middle=None, peers=None):
        self.arrays, self.out_shapes, self.n_sems = list(arrays), list(out_shapes), n_sems
        self.start, self.finish = start, finish
        self.middle = middle if middle is not None else (lambda *refs: None)
        self.peers = peers


PEER_SETS = {
    "sibling": (0, lambda x, y, c: [(x, y, 1 - c)]),
    "chips": (1, lambda x, y, c: [(1 - x, y, c), (x, 1 - y, c), (1 - x, 1 - y, c)]),
    "neighbours": (2, lambda x, y, c: [(1 - x, y, c), (x, 1 - y, c), (x, y, 1 - c)]),
    "chips_sibling": (3, lambda x, y, c: [(1 - x, y, c), (x, 1 - y, c), (1 - x, 1 - y, c), (x, y, 1 - c)]),
}


def _handshake(kind):
    x, y, c, _, _ = _place()
    peers = PEER_SETS[kind][1](x, y, c)
    barrier = pltpu.get_barrier_semaphore()
    for peer in peers:
        pl.semaphore_signal(barrier, inc=1, device_id=peer, device_id_type=MESH)
    pl.semaphore_wait(barrier, len(peers))


def _with_barrier(compiler_params, kind):
    if kind is None:
        return compiler_params
    semantics = None if compiler_params is None else compiler_params.dimension_semantics
    return pltpu.CompilerParams(dimension_semantics=semantics, collective_id=PEER_SETS[kind][0])


def _call(body, args, *, name, grid, in_specs, out_specs, out_shape, scratch_shapes, compiler_params, hosted=None):
    if hosted is None:
        outs = pl.pallas_call(body, name=name, grid=grid, in_specs=in_specs, out_specs=out_specs, out_shape=out_shape,
                              scratch_shapes=scratch_shapes, compiler_params=compiler_params)(*args)
        return list(outs), []
    n_in, n_out, n_sc, nh = len(in_specs), len(out_specs), len(scratch_shapes), len(hosted.arrays)

    def wrapped(*refs):
        ins, h_in = refs[:n_in], refs[n_in:n_in + nh]
        outs, h_out = refs[n_in + nh:n_in + nh + n_out], refs[n_in + nh + n_out:n_in + 2 * nh + n_out]
        rest = refs[n_in + 2 * nh + n_out:]
        scratch, (send_sems, recv_sems) = rest[:n_sc], rest[n_sc:]
        step = functools.reduce(lambda flat, d: flat * grid[d] + pl.program_id(d), range(len(grid)), 0)
        total = functools.reduce(lambda a, b: a * b, grid)

        @pl.when(step == 0)
        def _():
            if hosted.peers is not None:
                _handshake(hosted.peers)
            hosted.start(h_in, h_out, send_sems, recv_sems)

        @pl.when(step == total // 2)
        def _():
            hosted.middle(h_in, h_out, send_sems, recv_sems)

        body(*ins, *outs, *scratch)
        last = step == total - 1

        @pl.when(last)
        def _():
            hosted.finish(h_in, h_out, send_sems, recv_sems)

    sems = [pltpu.SemaphoreType.DMA((hosted.n_sems,)), pltpu.SemaphoreType.DMA((hosted.n_sems,))]
    outs = pl.pallas_call(
        wrapped, name=name, grid=grid, in_specs=list(in_specs) + [ANY] * nh, out_specs=list(out_specs) + [ANY] * nh,
        out_shape=list(out_shape) + hosted.out_shapes, scratch_shapes=list(scratch_shapes) + sems,
        compiler_params=_with_barrier(compiler_params, hosted.peers))(*args, *hosted.arrays)
    return list(outs[:n_out]), list(outs[n_out:])


def _run_hosted(hosted, name):
    nh = len(hosted.arrays)

    def body(*refs):
        h_in, h_out, (send_sems, recv_sems) = refs[:nh], refs[nh:2 * nh], refs[2 * nh:]
        if hosted.peers is not None:
            _handshake(hosted.peers)
        hosted.start(h_in, h_out, send_sems, recv_sems)
        hosted.middle(h_in, h_out, send_sems, recv_sems)
        hosted.finish(h_in, h_out, send_sems, recv_sems)

    sems = [pltpu.SemaphoreType.DMA((hosted.n_sems,)), pltpu.SemaphoreType.DMA((hosted.n_sems,))]
    return list(pl.pallas_call(body, name=name, in_specs=[ANY] * nh, out_specs=[ANY] * nh, out_shape=hosted.out_shapes,
                               scratch_shapes=sems, compiler_params=_with_barrier(None, hosted.peers))(*hosted.arrays))


def _ffn_weight_operands(ffn_w, chunk_maps):
    if isinstance(ffn_w, (list, tuple)):
        specs = [pl.BlockSpec((None, FF_SHARD, D_MODEL), lambda *g, m=m: (m(*g), 0, 0)) for m in chunk_maps]
        return list(ffn_w), specs
    specs = [pl.BlockSpec((None, None, FF_SHARD, D_MODEL), lambda *g, m=m, k=kind: (m(*g), k, 0, 0))
             for kind, m in enumerate(chunk_maps)]
    return [ffn_w] * 3, specs


def _pipeline_item(steps, lag):
    def item(s):
        it = jnp.clip(s - lag, 0, steps - 1)
        return it // N_CHIPS, it % N_CHIPS

    return item


def _ffn_fwd(x, g, ffn_w, name, hosted=None, loss_head=None):
    t = x.shape[0]
    tm = min(t, TOKEN_TILE)
    n_head = 0 if loss_head is None else 2

    def body(*refs):
        x_ref, g_ref, wg_ref, wu_ref, wd_ref = refs[0:5]
        head_in = refs[5:5 + n_head]
        outs = refs[5 + n_head:-1]
        acc_ref = refs[-1]
        a_ref, u_ref, h_ref = outs[-3:]
        i, j = pl.program_id(0), pl.program_id(1)

        @pl.when(j == 0)
        def _():
            xv = x_ref[...]
            h_ref[...] = ((xv * _rms_scale(xv)) * g_ref[...]).astype(BF16)
            acc_ref[...] = jnp.zeros_like(acc_ref)

        h = h_ref[...]
        a = _dot_nt(h, wg_ref[...])
        u = _dot_nt(h, wu_ref[...])
        a_ref[...] = a.astype(BF16)
        u_ref[...] = u.astype(BF16)
        hid = (a * jax.nn.sigmoid(a)) * u
        acc_ref[...] += _dot(hid.astype(BF16), wd_ref[...])

        if loss_head is None:
            @pl.when(j == N_CHIPS - 1)
            def _():
                outs[0][...] = x_ref[...] + 0.5 * acc_ref[...]
        else:
            gf_ref, t_ref = head_in
            l_ref, dx_ref, dgf_ref = outs[0:3]

            @pl.when((i == 0) & (j == 0))
            def _():
                l_ref[...] = jnp.zeros_like(l_ref)
                dgf_ref[...] = jnp.zeros_like(dgf_ref)

            @pl.when(j == N_CHIPS - 1)
            def _():
                xv = x_ref[...] + 0.5 * acc_ref[...]
                gv = gf_ref[...]
                err = (xv * _rms_scale(xv)) * gv - t_ref[...]
                l_ref[...] += 0.5 * jnp.sum(jnp.mean(err * err, axis=-1, keepdims=True), axis=0, keepdims=True)
                dx, dg = _rms_bwd(err * (1.0 / D_MODEL), xv, gv)
                dx_ref[...] = dx
                dgf_ref[...] += dg

    tok = pl.BlockSpec((tm, D_MODEL), lambda i, j: (i, 0))
    row = pl.BlockSpec((1, D_MODEL), lambda i, j: (0, 0))
    act = pl.BlockSpec((None, tm, FF_SHARD), lambda i, j: (j, i, 0))
    act_shape = jax.ShapeDtypeStruct((N_CHIPS, t, FF_SHARD), BF16)
    w_arrays, weights = _ffn_weight_operands(ffn_w, [lambda i, j: j] * 3)
    if loss_head is None:
        first_specs, first_shapes, head_args, head_specs = [tok], [jax.ShapeDtypeStruct((t, D_MODEL), F32)], [], []
    else:
        first_specs = [pl.BlockSpec((8, LANES), lambda i, j: (0, 0)), tok, row]
        first_shapes = [jax.ShapeDtypeStruct((8, LANES), F32), jax.ShapeDtypeStruct((t, D_MODEL), F32),
                        jax.ShapeDtypeStruct((1, D_MODEL), F32)]
        head_args, head_specs = list(loss_head), [row, tok]
    return _call(
        body, (x, g, *w_arrays, *head_args), name=name, grid=(t // tm, N_CHIPS),
        in_specs=[tok, row] + weights + head_specs,
        out_specs=first_specs + [act, act, tok],
        out_shape=first_shapes + [act_shape, act_shape, jax.ShapeDtypeStruct((t, D_MODEL), BF16)],
        scratch_shapes=[pltpu.VMEM((tm, D_MODEL), F32)],
        compiler_params=_ARB2, hosted=hosted)


def _ffn1_fwd_gathering(x, g, shard, name, hosted=None):
    t = x.shape[0]
    tm = min(t, TOKEN_TILE)
    nt = t // tm
    nh = 0 if hosted is None else len(hosted.arrays)
    peers = "neighbours" if hosted is None or hosted.peers == "neighbours" else "chips_sibling"
    assert hosted is None or hosted.peers in ("neighbours", "chips_sibling")

    def body(*refs):
        x_ref, g_ref, shard_ref = refs[0:3]
        h_in = refs[3:3 + nh]
        xo_ref, a_ref, u_ref, h_ref, wall = refs[3 + nh:8 + nh]
        h_out = refs[8 + nh:8 + 2 * nh]
        acc, h_all, wbuf, load_sems, send_sems, recv_sems = refs[8 + 2 * nh:14 + 2 * nh]
        carried_sems = refs[14 + 2 * nh:]
        k, i = pl.program_id(0), pl.program_id(1)
        legs, _ = _gather_legs(shard_ref, wall, send_sems, recv_sems, 0, True)
        begin, pass_on, _ = _gather_steps(legs, True)

        def load(chunk, src):
            return pltpu.make_async_copy(src, wbuf.at[chunk % 2], load_sems.at[chunk % 2])

        @pl.when((k == 0) & (i == 0))
        def _():
            _handshake(peers)
            begin()
            load(0, shard_ref).start()
            load(0, shard_ref).wait()

        @pl.when((k == 1) & (i == 0))
        def _():
            pass_on()
            if hosted is not None:
                hosted.start(h_in, h_out, *carried_sems)
            legs["pass_y"][1].wait_recv()
            load(1, wall.at[PEER_SLOT[1]]).start()
            load(1, wall.at[PEER_SLOT[1]]).wait()

        @pl.when((k == 1) & (i == nt // 2))
        def _():
            legs["pass_x"][1].wait_recv()
            load(2, wall.at[PEER_SLOT[0]]).start()

        @pl.when((k == 2) & (i == 0))
        def _():
            load(2, wall.at[PEER_SLOT[0]]).wait()

        @pl.when((k == 2) & (i == nt // 2))
        def _():
            legs["fwd_y"][1].wait_recv()
            legs["pass_d0"][0].start()
            legs["fwd_x"][1].wait_recv()
            legs["pass_d1"][0].start()
            legs["pass_d0"][1].wait_recv()
            legs["pass_d1"][1].wait_recv()
            load(3, wall.at[PEER_SLOT[2]]).start()
            if hosted is not None:
                hosted.middle(h_in, h_out, *carried_sems)

        @pl.when((k == 3) & (i == 0))
        def _():
            load(3, wall.at[PEER_SLOT[2]]).wait()

        @pl.when(k == 0)
        def _():
            xv = x_ref[...]
            h0 = ((xv * _rms_scale(xv)) * g_ref[...]).astype(BF16)
            h_all[i] = h0
            h_ref[...] = h0

        h = h_all[i]
        wg, wu, wd = (wbuf[k % 2, kind].reshape(FF_SHARD, D_MODEL) for kind in range(3))
        a = _dot_nt(h, wg)
        u = _dot_nt(h, wu)
        a_ref[...] = a.astype(BF16)
        u_ref[...] = u.astype(BF16)
        part = _dot(((a * jax.nn.sigmoid(a)) * u).astype(BF16), wd)

        @pl.when(k == 0)
        def _():
            acc[i] = part

        @pl.when(k > 0)
        def _():
            acc[i] += part

        @pl.when(k == N_CHIPS - 1)
        def _():
            xo_ref[...] = x_ref[...] + 0.5 * acc[i]

        @pl.when((k == N_CHIPS - 1) & (i == nt - 1))
        def _():
            legs["own"][1].wait_recv()
            for pair in legs.values():
                pair[0].wait_send()
            if hosted is not None:
                hosted.finish(h_in, h_out, *carried_sems)

    def first_or_last(k):
        return (k == 0) | (k == N_CHIPS - 1)

    tok = lambda keep: pl.BlockSpec((tm, D_MODEL), lambda k, i: (jnp.where(keep(k), i, 0), 0))
    act = pl.BlockSpec((None, tm, FF_SHARD), lambda k, i: (k, i, 0))
    act_shape = jax.ShapeDtypeStruct((N_CHIPS, t, FF_SHARD), BF16)
    carried = [] if hosted is None else [pltpu.SemaphoreType.DMA((hosted.n_sems,))] * 2
    outs = pl.pallas_call(
        body, name=name, grid=(N_CHIPS, nt),
        in_specs=[tok(first_or_last), pl.BlockSpec((1, D_MODEL), lambda k, i: (0, 0)), ANY] + [ANY] * nh,
        out_specs=[tok(lambda k: k == N_CHIPS - 1), act, act,
                   pl.BlockSpec((tm, D_MODEL), lambda k, i: (jnp.where(k == 0, i, nt - 1), 0)), ANY] + [ANY] * nh,
        out_shape=[jax.ShapeDtypeStruct((t, D_MODEL), F32), act_shape, act_shape,
                   jax.ShapeDtypeStruct((t, D_MODEL), BF16),
                   jax.ShapeDtypeStruct((N_CHIPS,) + shard.shape, shard.dtype)]
                  + ([] if hosted is None else hosted.out_shapes),
        scratch_shapes=[pltpu.VMEM((nt, tm, D_MODEL), F32), pltpu.VMEM((nt, tm, D_MODEL), BF16),
                        pltpu.VMEM((2,) + shard.shape, shard.dtype), pltpu.SemaphoreType.DMA((2,)),
                        pltpu.SemaphoreType.DMA((9,)), pltpu.SemaphoreType.DMA((9,))] + carried,
        compiler_params=_with_barrier(_ARB2, peers),
    )(x, g, shard, *([] if hosted is None else hosted.arrays))
    return list(outs[:5]), list(outs[5:])


def _ffn_bwd(dxo, x, g, a4, u4, ffn_w, name, hosted=None, back_w=None):
    t = x.shape[0]
    tm = min(t, TOKEN_TILE)
    steps = (t // tm) * N_CHIPS
    cur, old = _pipeline_item(steps, 0), _pipeline_item(steps, 1)

    def body(*refs):
        dxo_ref, dxo_old_ref, x_ref, g_ref, a_ref, u_ref, wg_ref, wu_ref, wd_ref = refs[0:9]
        n_back = 0 if back_w is None else 1
        da_ref, du_ref, hid_ref, dob_ref, dx_ref, dg_ref = refs[9 + n_back:15 + n_back]
        acc_ref, da_slots, du_slots = refs[-3:]
        s = pl.program_id(0)
        jc, jo = cur(s)[1], old(s)[1]
        slot = s % 2

        @pl.when(s == 0)
        def _():
            dg_ref[...] = jnp.zeros_like(dg_ref)
            acc_ref[...] = jnp.zeros_like(acc_ref)
            da_slots[...] = jnp.zeros_like(da_slots)
            du_slots[...] = jnp.zeros_like(du_slots)

        @pl.when(jc == 0)
        def _():
            dob_ref[...] = (0.5 * dxo_ref[...]).astype(BF16)

        dhid = _dot_nt(dob_ref[...], wd_ref[...])
        a = a_ref[...].astype(F32)
        u = u_ref[...].astype(F32)
        sg = jax.nn.sigmoid(a)
        sl = a * sg
        hid_ref[...] = (sl * u).astype(BF16)
        du = (dhid * sl).astype(BF16)
        da = (dhid * u * _silu_grad(a, sg)).astype(BF16)
        du_ref[...] = du
        da_ref[...] = da
        acc_ref[...] += _dot(da_slots[1 - slot], wg_ref[...]) + _dot(du_slots[1 - slot], wu_ref[...])
        da_slots[slot] = da
        du_slots[slot] = du

        @pl.when((jo == N_CHIPS - 1) & (s > 0))
        def _():
            dx, dg = _rms_bwd(acc_ref[...], x_ref[...], g_ref[...])
            dx = dxo_old_ref[...] + dx
            dx_ref[...] = dx
            dg_ref[...] += dg
            acc_ref[...] = jnp.zeros_like(acc_ref)
            if back_w is not None:
                refs[15 + n_back][...] = _dot_nt(dx.astype(BF16), refs[9][...])

    tok_cur = pl.BlockSpec((tm, D_MODEL), lambda s: (cur(s)[0], 0))
    tok_old = pl.BlockSpec((tm, D_MODEL), lambda s: (old(s)[0], 0))
    act = pl.BlockSpec((None, tm, FF_SHARD), lambda s: (cur(s)[1], cur(s)[0], 0))
    row = pl.BlockSpec((1, D_MODEL), lambda s: (0, 0))
    w_arrays, weights = _ffn_weight_operands(ffn_w, [lambda s: old(s)[1], lambda s: old(s)[1], lambda s: cur(s)[1]])
    act_shape = jax.ShapeDtypeStruct((N_CHIPS, t, FF_SHARD), BF16)
    back_args, back_in, back_out, back_shape = [], [], [], []
    if back_w is not None:
        n_back_cols = back_w.shape[0]
        back_args, back_in = [back_w], [pl.BlockSpec(back_w.shape, lambda s: (0, 0))]
        back_out = [pl.BlockSpec((tm, n_back_cols), lambda s: (old(s)[0], 0))]
        back_shape = [jax.ShapeDtypeStruct((t, n_back_cols), F32)]
    return _call(
        body, (dxo, dxo, x, g, a4, u4, *w_arrays, *back_args), name=name, grid=(steps + 1,),
        in_specs=[tok_cur, tok_old, tok_old, row, act, act] + weights + back_in,
        out_specs=[act, act, act, tok_cur, tok_old, row] + back_out,
        out_shape=[act_shape, act_shape, act_shape,
                   jax.ShapeDtypeStruct((t, D_MODEL), BF16),
                   jax.ShapeDtypeStruct((t, D_MODEL), F32),
                   jax.ShapeDtypeStruct((1, D_MODEL), F32)] + back_shape,
        scratch_shapes=[pltpu.VMEM((tm, D_MODEL), F32), pltpu.VMEM((2, tm, FF_SHARD), BF16),
                        pltpu.VMEM((2, tm, FF_SHARD), BF16)],
        compiler_params=_ARB1, hosted=hosted)


def _matmul_tn(a, b, name, tka=None, out_dtype=F32, hosted=None):
    a3, b3 = a.ndim == 3, b.ndim == 3
    nb = a.shape[0] if a3 else (b.shape[0] if b3 else 1)
    t, ka, n = a.shape[-2], a.shape[-1], b.shape[-1]
    tka = ka if tka is None else tka
    tk = min(t, 4 * TOKEN_TILE)
    nk = t // tk

    def body(a_ref, b_ref, o_ref, acc_ref):
        k = pl.program_id(2)

        @pl.when(k == 0)
        def _():
            acc_ref[...] = jnp.zeros_like(acc_ref)

        acc_ref[...] += _dot_tn(a_ref[...].astype(BF16), b_ref[...].astype(BF16))

        @pl.when(k == nk - 1)
        def _():
            o_ref[...] = acc_ref[...].astype(out_dtype)

    a_spec = (pl.BlockSpec((None, tk, tka), lambda i, j, k: (i, k, j)) if a3
              else pl.BlockSpec((tk, tka), lambda i, j, k: (k, j)))
    b_spec = (pl.BlockSpec((None, tk, n), lambda i, j, k: (i, k, 0)) if b3
              else pl.BlockSpec((tk, n), lambda i, j, k: (k, 0)))
    outs, carried = _call(
        body, (a, b), name=name, grid=(nb, ka // tka, t // tk),
        in_specs=[a_spec, b_spec],
        out_specs=[pl.BlockSpec((None, tka, n), lambda i, j, k: (i, j, 0))],
        out_shape=[jax.ShapeDtypeStruct((nb, ka, n), out_dtype)],
        scratch_shapes=[pltpu.VMEM((tka, n), F32)],
        compiler_params=_ARB3, hosted=hosted)
    return outs[0] if hosted is None else (outs[0], carried)


def _matmul_nt(a, w, name, out_dtype=F32):
    t, k = a.shape
    n = w.shape[0]
    tm = min(t, TOKEN_TILE)

    def body(a_ref, w_ref, o_ref):
        o_ref[...] = _dot_nt(a_ref[...].astype(BF16), w_ref[...]).astype(out_dtype)

    return pl.pallas_call(
        body, name=name, grid=(t // tm,),
        in_specs=[pl.BlockSpec((tm, k), lambda i: (i, 0)), pl.BlockSpec((n, k), lambda i: (0, 0))],
        out_specs=pl.BlockSpec((tm, n), lambda i: (i, 0)),
        out_shape=jax.ShapeDtypeStruct((t, n), out_dtype),
        compiler_params=_ARB1,
    )(a, w)


def _mixer_in_bwd(dproj, w_in_pt, dres, x, g, name):
    t, k = dproj.shape
    tm = min(t, TOKEN_TILE)

    def body(a_ref, w_ref, dres_ref, x_ref, g_ref, dx_ref, dg_ref):
        @pl.when(pl.program_id(0) == 0)
        def _():
            dg_ref[...] = jnp.zeros_like(dg_ref)

        dh = _dot(a_ref[...], w_ref[...])
        dx, dg = _rms_bwd(dh, x_ref[...], g_ref[...])
        dx_ref[...] = dres_ref[...] + dx
        dg_ref[...] += dg

    tok = pl.BlockSpec((tm, D_MODEL), lambda i: (i, 0))
    row = pl.BlockSpec((1, D_MODEL), lambda i: (0, 0))
    return pl.pallas_call(
        body, name=name, grid=(t // tm,),
        in_specs=[pl.BlockSpec((tm, k), lambda i: (i, 0)), pl.BlockSpec((k, D_MODEL), lambda i: (0, 0)), tok, tok, row],
        out_specs=[tok, row],
        out_shape=[jax.ShapeDtypeStruct((t, D_MODEL), F32), jax.ShapeDtypeStruct((1, D_MODEL), F32)],
        compiler_params=_ARB1,
    )(dproj, w_in_pt, dres, x, g)


def _mixer_in_fwd(x, g, w_in_pt, name, hosted=None):
    t = x.shape[0]
    tm = min(t, TOKEN_TILE)
    tn = PROJ_P // PROJ_TILES

    def body(x_ref, g_ref, w_ref, p_ref, h_ref):
        @pl.when(pl.program_id(1) == 0)
        def _():
            xv = x_ref[...]
            h_ref[...] = ((xv * _rms_scale(xv)) * g_ref[...]).astype(BF16)

        p_ref[...] = _dot_nt(h_ref[...], w_ref[...])

    tok = pl.BlockSpec((tm, D_MODEL), lambda i, j: (i, 0))
    return _call(
        body, (x, g, w_in_pt), name=name, grid=(t // tm, PROJ_TILES),
        in_specs=[tok, pl.BlockSpec((1, D_MODEL), lambda i, j: (0, 0)),
                  pl.BlockSpec((tn, D_MODEL), lambda i, j: (j, 0))],
        out_specs=[pl.BlockSpec((tm, tn), lambda i, j: (i, j)), tok],
        out_shape=[jax.ShapeDtypeStruct((t, PROJ_P), F32), jax.ShapeDtypeStruct((t, D_MODEL), BF16)],
        scratch_shapes=[], compiler_params=_ARB2, hosted=hosted)


def _mixer_out_fwd(o_ret, o_gla, w_out, x, name):
    t = x.shape[0]
    tm = min(t, TOKEN_TILE)
    half = HEADS * LANES

    def body(a_ref, b_ref, w_ref, x_ref, o_ref):
        o_ref[...] = x_ref[...] + _dot(a_ref[...], w_ref[0:half, :]) + _dot(b_ref[...], w_ref[half:2 * half, :])

    tok = pl.BlockSpec((tm, D_MODEL), lambda i: (i, 0))
    hb = pl.BlockSpec((tm, half), lambda i: (i, 0))
    return pl.pallas_call(
        body, name=name, grid=(t // tm,),
        in_specs=[hb, hb, pl.BlockSpec((2 * half, D_MODEL), lambda i: (0, 0)), tok],
        out_specs=tok, out_shape=jax.ShapeDtypeStruct((t, D_MODEL), F32),
        compiler_params=_ARB1,
    )(o_ret, o_gla, w_out, x)


def _rot(v, cos, sa, sb):
    return v * cos + pltpu.roll(v, 96, 1) * sa + pltpu.roll(v, 32, 1) * sb


def _rot_t(d, cos, sa, sb):
    return d * cos + pltpu.roll(d * sa, 32, 1) + pltpu.roll(d * sb, 96, 1)


def _bmm(a, b):
    return jnp.einsum("cik,ckj->cij", a, b, preferred_element_type=F32)


def _bmm_nt(a, b):
    return jnp.einsum("cik,cjk->cij", a, b, preferred_element_type=F32)


def _bmm_tn(a, b):
    return jnp.einsum("cki,ckj->cij", a, b, preferred_element_type=F32)


def _masked_sum(mask, x):
    hi = x.astype(BF16)
    r1 = x - hi.astype(F32)
    mid = r1.astype(BF16)
    lo = (r1 - mid.astype(F32)).astype(BF16)
    return _bmm(mask, hi) + _bmm(mask, mid) + _bmm(mask, lo)


PAIR = 2


def _tile_inputs(is_ret, qkvg_refs, aux, nc):
    shape3 = (nc, CHUNK, LANES)
    q_ref, k_ref, v_ref, g_ref = qkvg_refs
    low_lanes = lax.broadcasted_iota(jnp.int32, (1, LANES), 1) < 64
    ri = lax.broadcasted_iota(jnp.int32, (PAIR * nc, CHUNK, CHUNK), 1)
    ci = lax.broadcasted_iota(jnp.int32, (PAIR * nc, CHUNK, CHUNK), 2)
    qs, ks, vs, bs, gates, extra = [], [], [], [], [], []
    for hd in range(PAIR):
        q_blk, k_blk = q_ref[...], k_ref[...]
        if hd == 1:
            q_blk, k_blk = pltpu.roll(q_blk, 64, 1), pltpu.roll(k_blk, 64, 1)
        q_raw, k_raw = jnp.where(low_lanes, q_blk, 0.0), jnp.where(low_lanes, k_blk, 0.0)
        vs.append(v_ref[:, LANES * hd:LANES * (hd + 1)].reshape(shape3))
        gates.append(g_ref[:, LANES * hd:LANES * (hd + 1)])
        if is_ret:
            cos_ref, sa_ref, sb_ref, lg_ref = aux
            cos, sa, sb = cos_ref[...], sa_ref[...], sb_ref[...]
            q = _rot(q_raw, cos, sa, sb)
            k = _rot(k_raw, cos, sa, sb) * QK_SCALE
            steps = (lax.broadcasted_iota(jnp.int32, shape3, 1) + 1).astype(F32)
            bs.append(steps * lg_ref[hd])
            extra.append(jnp.exp(jnp.abs(ri[0:nc] - ci[0:nc]).astype(F32) * lg_ref[hd][:, 0:CHUNK]))
        else:
            glow_ref, wa2_ref, ba_ref = aux
            lanes = slice(LANES * hd, LANES * (hd + 1))
            logit = _dot(glow_ref[...].astype(BF16), wa2_ref[:, lanes]) + ba_ref[:, lanes]
            la = (jnp.minimum(logit, 0.0) - jnp.log1p(jnp.exp(-jnp.abs(logit)))) * (1.0 / GATE_NORM)
            bs.append(_masked_sum((ci[0:nc] <= ri[0:nc]).astype(BF16), la.reshape(shape3)))
            extra.append(logit)
            q = q_raw * QK_SCALE
            k = k_raw
        qs.append(q.reshape(shape3))
        ks.append(k.reshape(shape3))
    cat = lambda parts: jnp.concatenate(parts, axis=0)
    return cat(qs), cat(ks), cat(vs), gates, cat(bs), extra, ri, ci


def _tile_scores(q, k, b, ri, ci):
    mid = b[:, CHUNK // 2 - 1:CHUNK // 2, :]
    ep = jnp.exp(b - mid)
    en = jnp.exp(mid - b)
    qt, kt, qh, kh = q * ep, k * en, q * en, k * ep
    low = _bmm_nt(qt.astype(BF16), kt.astype(BF16))
    upp = _bmm_nt(qh.astype(BF16), kh.astype(BF16))
    scores = jnp.where(ci <= ri, low, upp)
    return scores, ep, en, qt, kt, qh, kh


def _attn_specs(is_ret, t, tb, imap_t):
    nb = t // tb
    base = 0 if is_ret else 12
    wide = PAIR * LANES
    proj = [pl.BlockSpec((tb, LANES), lambda p, i: (imap_t(i), base + p)),
            pl.BlockSpec((tb, LANES), lambda p, i: (imap_t(i), base + 2 + p)),
            pl.BlockSpec((tb, wide), lambda p, i: (imap_t(i), (base + 4) // 2 + p)),
            pl.BlockSpec((tb, wide), lambda p, i: (imap_t(i), (base + 8) // 2 + p))]
    lane_t = pl.BlockSpec((tb, LANES), lambda p, i: (imap_t(i), 0))
    if is_ret:
        aux = [lane_t, lane_t, lane_t, pl.BlockSpec((PAIR, 1, LANES), lambda p, i: (p, 0, 0))]
    else:
        aux = [pl.BlockSpec((tb, LANES), lambda p, i: (imap_t(i), PROJ_P // LANES - 1)),
               pl.BlockSpec((LANES, wide), lambda p, i: (0, p)),
               pl.BlockSpec((1, wide), lambda p, i: (0, p))]
    gain = pl.BlockSpec((1, wide), lambda p, i: (0, p))
    pair_t = pl.BlockSpec((tb, wide), lambda p, i: (imap_t(i), p))
    narrow_t = pl.BlockSpec((tb, LANES), lambda p, i: (imap_t(i), p))
    state = pl.BlockSpec((PAIR, tb // CHUNK, LANES, LANES), lambda p, i: (p, imap_t(i), 0, 0))
    return nb, proj, aux, gain, pair_t, narrow_t, state


def _attn_fwd(is_ret, proj, aux_arrays, gain, name, hosted=None):
    t = proj.shape[0]
    tb = min(t, ATTN_TILE)
    nc = tb // CHUNK
    n_aux = 4 if is_ret else 3
    nb, proj_spec, aux_specs, gain_spec, pair_t, _, state_spec = _attn_specs(is_ret, t, tb, lambda i: i)

    def body(*refs):
        qkvg_refs = refs[0:4]
        aux = refs[4:4 + n_aux]
        gn_ref, ofin_ref, oraw_ref, st_ref, state = refs[4 + n_aux:]

        @pl.when(pl.program_id(1) == 0)
        def _():
            state[...] = jnp.zeros_like(state)

        q, k, v, gates, b, extra, ri, ci = _tile_inputs(is_ret, qkvg_refs, aux, nc)
        if is_ret:
            scores = _bmm_nt(q.astype(BF16), k.astype(BF16)) * jnp.concatenate(extra, axis=0)
        else:
            scores = _tile_scores(q, k, b, ri, ci)[0]
        vb = v.astype(BF16)
        intra = _bmm(scores.astype(BF16), vb)
        b_last = b[:, CHUNK - 1:CHUNK, :]
        e_last = jnp.exp(b_last)
        grow = _bmm_tn(vb, (k * jnp.exp(b_last - b)).astype(BF16))
        for hd in range(PAIR):
            st = state[hd]
            for c in range(nc):
                st_ref[hd, c] = st
                st = st * e_last[hd * nc + c] + grow[hd * nc + c]
            state[hd] = st
        starts = st_ref[...].reshape(PAIR * nc, LANES, LANES)
        out3 = intra + _bmm_nt((q * jnp.exp(b)).astype(BF16), starts.astype(BF16))
        for hd in range(PAIR):
            lanes = slice(LANES * hd, LANES * (hd + 1))
            out = out3[hd * nc:(hd + 1) * nc].reshape(tb, LANES)
            oraw_ref[:, lanes] = out
            normed = out * _rms_scale(out)
            gate = gates[hd]
            ofin_ref[:, lanes] = ((normed * gn_ref[:, lanes]) * (gate * jax.nn.sigmoid(gate))).astype(BF16)

    width = HEADS * LANES
    return _call(
        body, (proj, proj, proj, proj, *aux_arrays, gain), name=name, grid=(HEADS // PAIR, nb),
        in_specs=proj_spec + aux_specs + [gain_spec],
        out_specs=[pair_t, pair_t, state_spec],
        out_shape=[jax.ShapeDtypeStruct((t, width), BF16), jax.ShapeDtypeStruct((t, width), F32),
                   jax.ShapeDtypeStruct((HEADS, t // CHUNK, LANES, LANES), F32)],
        scratch_shapes=[pltpu.VMEM((PAIR, LANES, LANES), F32)],
        compiler_params=_ARB2, hosted=hosted)


def _attn_bwd(is_ret, proj, aux_arrays, gain, o_raw, states, d_out, name, hosted=None):
    t = proj.shape[0]
    tb = min(t, ATTN_TILE)
    nc = tb // CHUNK
    n_aux = 4 if is_ret else 3
    nblk = t // tb
    nb, proj_spec, aux_specs, gain_spec, pair_t, narrow_t, state_spec = _attn_specs(
        is_ret, t, tb, lambda i: nblk - 1 - i)
    base = 0 if is_ret else HEADS // PAIR
    dout_spec = pl.BlockSpec((tb, PAIR * LANES), lambda p, i: (nblk - 1 - i, base + p))

    def body(*refs):
        qkvg_refs = refs[0:4]
        aux = refs[4:4 + n_aux]
        gn_ref, oraw_ref, st_ref, dfin_ref = refs[4 + n_aux:8 + n_aux]
        dq_ref, dk_ref, dv_ref, dgate_ref, dgn_ref = refs[8 + n_aux:13 + n_aux]
        if is_ret:
            dstate, dafter_ref = refs[13 + n_aux:]
        else:
            dlogit_ref, dba_ref, dstate, dafter_ref = refs[13 + n_aux:]

        @pl.when(pl.program_id(1) == 0)
        def _():
            dstate[...] = jnp.zeros_like(dstate)
            dgn_ref[...] = jnp.zeros_like(dgn_ref)
            if not is_ret:
                dba_ref[...] = jnp.zeros_like(dba_ref)

        shape3 = (nc, CHUNK, LANES)
        q, k, v, gates, b, extra, ri, ci = _tile_inputs(is_ret, qkvg_refs, aux, nc)
        eb = jnp.exp(b)
        qe = q * eb
        b_last = b[:, CHUNK - 1:CHUNK, :]
        e_last = jnp.exp(b_last)
        ekd = jnp.exp(b_last - b)
        kd = k * ekd

        d_os = []
        for hd in range(PAIR):
            lanes = slice(LANES * hd, LANES * (hd + 1))
            gn, gate = gn_ref[:, lanes], gates[hd]
            out = oraw_ref[:, lanes]
            r = _rms_scale(out)
            normed = out * r
            sg = jax.nn.sigmoid(gate)
            dfin = dfin_ref[:, lanes]
            dgate_ref[:, lanes] = (dfin * (normed * gn) * _silu_grad(gate, sg)).astype(BF16)
            dpre = dfin * (gate * sg)
            dgn_ref[:, lanes] += jnp.sum(dpre * normed, axis=0, keepdims=True)
            dnormed = dpre * gn
            d_o = r * (dnormed - normed * jnp.mean(dnormed * normed, axis=-1, keepdims=True))
            d_os.append(d_o.reshape(shape3))
        dob, vb = jnp.concatenate(d_os, axis=0).astype(BF16), v.astype(BF16)

        dgrow = _bmm_tn(dob, qe.astype(BF16))
        for hd in range(PAIR):
            dst = dstate[hd]
            for c in reversed(range(nc)):
                dafter_ref[hd * nc + c] = dst
                dst = dst * e_last[hd * nc + c] + dgrow[hd * nc + c]
            dstate[hd] = dst
        st = st_ref[...].reshape(PAIR * nc, LANES, LANES)
        dafter = dafter_ref[...]
        stb, dafter_b = st.astype(BF16), dafter.astype(BF16)

        dsc = _bmm_nt(dob, vb)
        dsc_t = _bmm_nt(vb, dob)
        dqe = _bmm(dob, stb)
        dkd = _bmm(vb, dafter_b)
        if is_ret:
            decay, qb, kb = jnp.concatenate(extra, axis=0), q.astype(BF16), k.astype(BF16)
            scores_t = _bmm_nt(kb, qb) * decay
            dq = _bmm((dsc * decay).astype(BF16), kb) + dqe * eb
            dk = _bmm((dsc_t * decay).astype(BF16), qb) + dkd * ekd
        else:
            _, ep, en, qt, kt, qh, kh = _tile_scores(q, k, b, ri, ci)
            qtb, ktb, qhb, khb = qt.astype(BF16), kt.astype(BF16), qh.astype(BF16), kh.astype(BF16)
            scores_t = jnp.where(ci >= ri, _bmm_nt(ktb, qtb), _bmm_nt(khb, qhb))
            dqt = _bmm(jnp.where(ci <= ri, dsc, 0.0).astype(BF16), ktb)
            dqh = _bmm(jnp.where(ci <= ri, 0.0, dsc).astype(BF16), khb)
            dkt = _bmm(jnp.where(ci >= ri, dsc_t, 0.0).astype(BF16), qtb)
            dkh = _bmm(jnp.where(ci >= ri, 0.0, dsc_t).astype(BF16), qhb)
            dq = dqt * ep + dqh * en + dqe * eb
            dk = dkt * en + dkh * ep + dkd * ekd
        dv = _bmm(scores_t.astype(BF16), dob) + _bmm_nt(kd.astype(BF16), dafter_b)

        if not is_ret:
            db = dqt * qt - dkt * kt - dqh * qh + dkh * kh + dqe * qe - dkd * kd
            db_last = (jnp.sum(dkd * kd, axis=1, keepdims=True)
                       + jnp.sum(dafter * st, axis=1, keepdims=True) * e_last)
            last_row = lax.broadcasted_iota(jnp.int32, (PAIR * nc, CHUNK, LANES), 1) == CHUNK - 1
            db = db + jnp.where(last_row, db_last, 0.0)
            dla = _masked_sum((ci >= ri).astype(BF16), db)

        dq_pair, dk_pair = [], []
        for hd in range(PAIR):
            lanes = slice(LANES * hd, LANES * (hd + 1))
            rows3 = slice(hd * nc, (hd + 1) * nc)
            dq_h, dk_h = dq[rows3].reshape(tb, LANES), dk[rows3].reshape(tb, LANES)
            if is_ret:
                cos_ref, sa_ref, sb_ref, _ = aux
                cos, sa, sb = cos_ref[...], sa_ref[...], sb_ref[...]
                dq_h = _rot_t(dq_h, cos, sa, sb)
                dk_h = _rot_t(dk_h, cos, sa, sb) * QK_SCALE
            else:
                dq_h = dq_h * QK_SCALE
                dlogit = dla[rows3].reshape(tb, LANES) * (1.0 / GATE_NORM) * jax.nn.sigmoid(-extra[hd])
                dlogit_ref[:, lanes] = dlogit.astype(BF16)
                dba_ref[:, lanes] += jnp.sum(dlogit, axis=0, keepdims=True)
            dq_pair.append(dq_h)
            dk_pair.append(dk_h)
            dv_ref[:, lanes] = dv[rows3].reshape(tb, LANES).astype(BF16)
        dq_ref[...] = (dq_pair[0] + pltpu.roll(dq_pair[1], 64, 1)).astype(BF16)
        dk_ref[...] = (dk_pair[0] + pltpu.roll(dk_pair[1], 64, 1)).astype(BF16)

    width = HEADS * LANES
    row_out = pl.BlockSpec((1, PAIR * LANES), lambda p, i: (0, p))
    out_specs = [narrow_t, narrow_t, pair_t, pair_t, row_out]
    out_shape = ([jax.ShapeDtypeStruct((t, width // 2), BF16)] * 2 + [jax.ShapeDtypeStruct((t, width), BF16)] * 2
                 + [jax.ShapeDtypeStruct((1, width), F32)])
    if not is_ret:
        out_specs += [pair_t, row_out]
        out_shape += [jax.ShapeDtypeStruct((t, width), BF16), jax.ShapeDtypeStruct((1, width), F32)]
    return _call(
        body, (proj, proj, proj, proj, *aux_arrays, gain, o_raw, states, d_out), name=name,
        grid=(HEADS // PAIR, nblk),
        in_specs=proj_spec + aux_specs + [gain_spec, pair_t, state_spec, dout_spec],
        out_specs=out_specs, out_shape=out_shape,
        scratch_shapes=[pltpu.VMEM((PAIR, LANES, LANES), F32), pltpu.VMEM((PAIR * nc, LANES, LANES), F32)],
        compiler_params=_ARB2, hosted=hosted)


PEER_SLOT = (2, 1, 3)


def _place():
    x, y, c = lax.axis_index("x"), lax.axis_index("y"), lax.axis_index("c")
    chips = [(1 - x, y), (x, 1 - y), (1 - x, 1 - y)]
    return x, y, c, 2 * x + y, chips


def _route_split(rows, dtype):
    tile = 16 if dtype == BF16 else 8
    if rows < 2 * tile:
        return None
    return -(-(rows // 2) // tile) * tile


def _routes(by_peer):
    x, y, c, me, chips = _place()
    (xx, xy), (yx, yy), (dx, dy) = chips
    if by_peer:
        slots = dict(own=0, from_x=PEER_SLOT[0], from_y=PEER_SLOT[1], diag=PEER_SLOT[2],
                     mine_on_x=PEER_SLOT[0], mine_on_y=PEER_SLOT[1])
    else:
        slots = dict(own=me, from_x=2 * xx + xy, from_y=2 * yx + yy, diag=2 * dx + dy, mine_on_x=me, mine_on_y=me)
    return c, (xx, xy, c), (yx, yy, c), (dx, dy, c), (x, y, 1 - c), slots


def _gather_legs(src, out, send_sems, recv_sems, base, by_peer):
    c, to_x, to_y, to_d, sibling, s = _routes(by_peer)
    r0 = _route_split(src.shape[2], src.dtype)

    def cp(k, src_ref, dst_ref, to):
        return pltpu.make_async_remote_copy(src_ref=src_ref, dst_ref=dst_ref, send_sem=send_sems.at[base + k],
                                            recv_sem=recv_sems.at[base + k], device_id=to, device_id_type=MESH)

    mine = src.at[:, c]
    legs = dict(
        x=(cp(0, mine, out.at[s["mine_on_x"], :, c], to_x), cp(0, mine, out.at[s["from_x"], :, c], to_x)),
        y=(cp(1, mine, out.at[s["mine_on_y"], :, c], to_y), cp(1, mine, out.at[s["from_y"], :, c], to_y)),
        pass_x=(cp(4, out.at[s["from_x"], :, c], out.at[s["from_x"], :, c], sibling),
                cp(4, mine, out.at[s["from_x"], :, 1 - c], sibling)),
        pass_y=(cp(5, out.at[s["from_y"], :, c], out.at[s["from_y"], :, c], sibling),
                cp(5, mine, out.at[s["from_y"], :, 1 - c], sibling)),
        own=(cp(8, src, out.at[s["own"]], sibling), cp(8, src, out.at[s["own"]], sibling)))
    if r0 is None:
        mine_on_d = s["diag"] if by_peer else s["own"]
        legs["d"] = (cp(2, mine, out.at[mine_on_d, :, c], to_d), cp(2, mine, out.at[s["diag"], :, c], to_d))
        legs["pass_d"] = (cp(6, out.at[s["diag"], :, c], out.at[s["diag"], :, c], sibling),
                          cp(6, mine, out.at[s["diag"], :, 1 - c], sibling))
        return legs, False
    lo, hi = pl.ds(0, r0), pl.ds(r0, src.shape[2] - r0)
    fx_on_y = s["diag"] if by_peer else s["from_x"]
    fy_on_x = s["diag"] if by_peer else s["from_y"]
    legs.update(
        fwd_y=(cp(2, out.at[s["from_x"], :, c, lo], out.at[fx_on_y, :, c, lo], to_y),
               cp(2, mine.at[:, lo], out.at[s["diag"], :, c, lo], to_y)),
        fwd_x=(cp(3, out.at[s["from_y"], :, c, hi], out.at[fy_on_x, :, c, hi], to_x),
               cp(3, mine.at[:, hi], out.at[s["diag"], :, c, hi], to_x)),
        pass_d0=(cp(6, out.at[s["diag"], :, c, lo], out.at[s["diag"], :, c, lo], sibling),
                 cp(6, mine.at[:, lo], out.at[s["diag"], :, 1 - c, lo], sibling)),
        pass_d1=(cp(7, out.at[s["diag"], :, c, hi], out.at[s["diag"], :, c, hi], sibling),
                 cp(7, mine.at[:, hi], out.at[s["diag"], :, 1 - c, hi], sibling)))
    return legs, True


def _gather_steps(legs, routed):
    def start():
        legs["x"][0].start()
        legs["y"][0].start()
        legs["own"][0].start()
        if not routed:
            legs["d"][0].start()

    def middle():
        legs["x"][1].wait_recv()
        if routed:
            legs["fwd_y"][0].start()
        legs["pass_x"][0].start()
        legs["y"][1].wait_recv()
        if routed:
            legs["fwd_x"][0].start()
        legs["pass_y"][0].start()

    def finish():
        last = ["pass_d0", "pass_d1"] if routed else ["pass_d"]
        if routed:
            legs["fwd_y"][1].wait_recv()
            legs["pass_d0"][0].start()
            legs["fwd_x"][1].wait_recv()
            legs["pass_d1"][0].start()
        else:
            legs["d"][1].wait_recv()
            legs["pass_d"][0].start()
        for name in ["own", "pass_x", "pass_y"] + last:
            legs[name][1].wait_recv()
        for name in ["x", "y", "own", "pass_x", "pass_y"] + last + (["fwd_y", "fwd_x"] if routed else ["d"]):
            legs[name][0].wait_send()

    return start, middle, finish


def _gather_plan(arrs):
    na = len(arrs)

    def steps(ins, outs, send_sems, recv_sems):
        return [_gather_steps(*_gather_legs(ins[a], outs[a], send_sems, recv_sems, 9 * a, False)) for a in range(na)]

    def run(which):
        def hook(*refs):
            for step in steps(*refs):
                step[which]()
        return hook

    routed = all(_route_split(a.shape[2], a.dtype) is not None for a in arrs)
    return _Hosted(arrs, [jax.ShapeDtypeStruct((N_CHIPS,) + a.shape, a.dtype) for a in arrs], 9 * na,
                   run(0), run(2), middle=run(1), peers="neighbours" if routed else "chips_sibling")


def _pair_exchange_plan(grads):
    na = len(grads)

    def copies(ins, outs, send_sems, recv_sems):
        x, y, c, _, _ = _place()
        return [pltpu.make_async_remote_copy(
            src_ref=ins[a].at[:, 1 - c], dst_ref=outs[a], send_sem=send_sems.at[a], recv_sem=recv_sems.at[a],
            device_id=(x, y, 1 - c), device_id_type=MESH) for a in range(na)]

    def start(*refs):
        for cp in copies(*refs):
            cp.start()

    def finish(*refs):
        for cp in copies(*refs):
            cp.wait()

    return _Hosted(grads, [jax.ShapeDtypeStruct(g.shape[:1] + g.shape[2:], g.dtype) for g in grads], na, start, finish,
                   peers="sibling")


def _small_gather_plan(block):
    def copies(ins, outs, send_sems, recv_sems):
        x, y, c, _, chips = _place()
        peers = [(x, y, 1 - c)] + [(px, py, pc) for px, py in chips for pc in (c, 1 - c)]
        sends = [pltpu.make_async_remote_copy(
            src_ref=ins[0], dst_ref=outs[0].at[4 * x + 2 * y + c], send_sem=send_sems.at[k], recv_sem=recv_sems.at[k],
            device_id=peer, device_id_type=MESH) for k, peer in enumerate(peers)]
        recvs = [pltpu.make_async_remote_copy(
            src_ref=ins[0], dst_ref=outs[0].at[4 * px + 2 * py + pc], send_sem=send_sems.at[k], recv_sem=recv_sems.at[k],
            device_id=(px, py, pc), device_id_type=MESH) for k, (px, py, pc) in enumerate(peers)]
        return sends, recvs

    def start(*refs):
        for cp in copies(*refs)[0]:
            cp.start()

    def finish(*refs):
        sends, recvs = copies(*refs)
        for cp in recvs:
            cp.wait_recv()
        for cp in sends:
            cp.wait_send()

    return _Hosted([block], [jax.ShapeDtypeStruct((8,) + block.shape, block.dtype)], 7, start, finish)


def _sum_devices(blocks):
    def body(b_ref, o_ref):
        acc = b_ref[0]
        for d in range(1, 8):
            acc = acc + b_ref[d]
        o_ref[...] = acc

    return pl.pallas_call(body, name="sum_devices", in_specs=[_VMEM], out_specs=_VMEM,
                          out_shape=jax.ShapeDtypeStruct(blocks.shape[1:], blocks.dtype))(blocks)


def _pair_add(grad, recv, c_arr, name):
    _, _, r, cols = grad.shape

    def body(c_ref, g_ref, r_ref, o_ref):
        o_ref[...] = (g_ref[...].astype(F32) + r_ref[...].astype(F32)).astype(BF16)

    return pl.pallas_call(
        body, name=name,
        grid_spec=pltpu.PrefetchScalarGridSpec(
            num_scalar_prefetch=1, grid=(N_CHIPS,),
            in_specs=[pl.BlockSpec((None, None, r, cols), lambda p, c_ref: (p, c_ref[0], 0, 0)),
                      pl.BlockSpec((None, r, cols), lambda p, c_ref: (p, 0, 0))],
            out_specs=pl.BlockSpec((None, r, cols), lambda p, c_ref: (p, 0, 0))),
        out_shape=jax.ShapeDtypeStruct((N_CHIPS, r, cols), BF16),
        compiler_params=_ARB1,
    )(c_arr, grad, recv)


def _chip_exchange_plan(sums, by_peer=False):
    na = len(sums)

    def copies(ins, outs, send_sems, recv_sems):
        x, y, c, me, chips = _place()

        def copy(a, j, px, py, block, slot):
            return pltpu.make_async_remote_copy(
                src_ref=ins[a].at[block], dst_ref=outs[a].at[slot],
                send_sem=send_sems.at[3 * a + j], recv_sem=recv_sems.at[3 * a + j],
                device_id=(px, py, c), device_id_type=MESH)

        peers = [(a, j, px, py) for a in range(na) for j, (px, py) in enumerate(chips)]
        return me, peers, copy

    def start(*refs):
        me, peers, copy = copies(*refs)
        for a, j, px, py in peers:
            if by_peer:
                copy(a, j, px, py, PEER_SLOT[j], PEER_SLOT[j]).start()
            else:
                copy(a, j, px, py, 2 * px + py, me).start()

    def finish(*refs):
        me, peers, copy = copies(*refs)
        for a, j, px, py in peers:
            if by_peer:
                copy(a, j, px, py, PEER_SLOT[j], PEER_SLOT[j]).wait_recv()
            else:
                copy(a, j, px, py, me, 2 * px + py).wait_recv()
        for a, j, px, py in peers:
            if by_peer:
                copy(a, j, px, py, PEER_SLOT[j], PEER_SLOT[j]).wait_send()
            else:
                copy(a, j, px, py, 2 * px + py, me).wait_send()

    return _Hosted(sums, [jax.ShapeDtypeStruct(s.shape, s.dtype) for s in sums], 3 * na, start, finish, peers="chips")


def _chip_sum(own, recv, me_arr, name):
    _, r, cols = recv.shape

    def body(me_ref, own_ref, r_ref, o_ref):
        o_ref[...] = jnp.zeros_like(o_ref)
        for q in range(N_CHIPS):
            @pl.when(me_ref[0] == q)
            def _():
                o_ref[...] += own_ref[...].astype(F32)

            @pl.when(me_ref[0] != q)
            def _():
                o_ref[...] += r_ref[q].astype(F32)

    return pl.pallas_call(
        body, name=name,
        grid_spec=pltpu.PrefetchScalarGridSpec(
            num_scalar_prefetch=1, grid=(1,),
            in_specs=[pl.BlockSpec((None, r, cols), lambda i, me_ref: (me_ref[0], 0, 0)),
                      pl.BlockSpec((N_CHIPS, r, cols), lambda i, me_ref: (0, 0, 0))],
            out_specs=pl.BlockSpec((r, cols), lambda i, me_ref: (0, 0))),
        out_shape=jax.ShapeDtypeStruct((r, cols), F32),
        compiler_params=_ARB1,
    )(me_arr, own, recv)


def _peer_sum(own, recv, name):
    _, r, cols = recv.shape

    def body(own_ref, r_ref, o_ref):
        acc = own_ref[...].astype(F32) + r_ref[1].astype(F32)
        acc = acc + r_ref[2].astype(F32)
        o_ref[...] = acc + r_ref[3].astype(F32)

    return pl.pallas_call(
        body, name=name, grid=(1,),
        in_specs=[pl.BlockSpec((None, r, cols), lambda i: (0, 0, 0)), pl.BlockSpec((N_CHIPS, r, cols), lambda i: (0, 0, 0))],
        out_specs=pl.BlockSpec((r, cols), lambda i: (0, 0)),
        out_shape=jax.ShapeDtypeStruct((r, cols), F32),
        compiler_params=_ARB1,
    )(own, recv)


def _pair_share_plan(halves):
    na = len(halves)

    def copies(ins, outs, send_sems, recv_sems):
        x, y, c, _, _ = _place()
        return [pltpu.make_async_remote_copy(
            src_ref=ins[a], dst_ref=outs[a], send_sem=send_sems.at[a], recv_sem=recv_sems.at[a],
            device_id=(x, y, 1 - c), device_id_type=MESH) for a in range(na)]

    def start(*refs):
        for cp in copies(*refs):
            cp.start()

    def finish(*refs):
        for cp in copies(*refs):
            cp.wait()

    return _Hosted(halves, [jax.ShapeDtypeStruct(h.shape, h.dtype) for h in halves], na, start, finish,
                   peers="sibling")


def _row_tile(rows):
    best = rows
    for cand in range(8, min(rows, 512) + 1, 8):
        if rows % cand == 0:
            best = cand
    return best


def _adamw_math(w, g, m, v):
    m2 = ADAM_B1 * m + (1.0 - ADAM_B1) * g
    v2 = ADAM_B2 * v + (1.0 - ADAM_B2) * (g * g)
    m_hat = m2 / (1.0 - ADAM_B1 ** ADAM_STEP)
    v_hat = v2 / (1.0 - ADAM_B2 ** ADAM_STEP)
    return -ADAM_LR * (m_hat / (jnp.sqrt(v_hat) + ADAM_EPS) + ADAM_WD * w), m2, v2


def _adamw_halves(w, g_mine, g_other, m, v, c_arr, name):
    rows, cols = w.shape
    r = rows // 2
    tr = _row_tile(r)
    nt = r // tr

    def body(c_ref, w_ref, gm_ref, go_ref, m_ref, v_ref, g_ref, d_ref, nm_ref, nv_ref):
        gv = jnp.where(pl.program_id(0) == c_ref[0], gm_ref[...], go_ref[...])
        g_ref[...] = gv
        d_ref[...], nm_ref[...], nv_ref[...] = _adamw_math(w_ref[...], gv, m_ref[...], v_ref[...])

    full = pl.BlockSpec((tr, cols), lambda h, i, c_ref: (h * nt + i, 0))
    half = pl.BlockSpec((tr, cols), lambda h, i, c_ref: (i, 0))
    shape = jax.ShapeDtypeStruct((rows, cols), F32)
    return pl.pallas_call(
        body, name=name,
        grid_spec=pltpu.PrefetchScalarGridSpec(
            num_scalar_prefetch=1, grid=(2, nt),
            in_specs=[full, half, half, full, full], out_specs=[full] * 4),
        out_shape=[shape] * 4,
        compiler_params=_ARB2,
    )(c_arr, w, g_mine, g_other, m, v)


def _pad_w_in_t(w_in_t):
    return jnp.pad(w_in_t, ((0, PROJ_P - IN_WIDTH), (0, 0)))


def _unpad_w_in_t(w_pt):
    return w_pt[0:IN_WIDTH]


def _rope_tables(t):
    half = 32
    inv = ROPE_BASE ** (-jnp.arange(half, dtype=F32) * 2.0 / 64)
    ang = jnp.arange(t, dtype=F32)[:, None] * inv[None, :]
    cos, sin = jnp.cos(ang), jnp.sin(ang)
    z32, z64 = jnp.zeros((t, 32), F32), jnp.zeros((t, 64), F32)
    return (jnp.concatenate([cos, cos, z64], axis=1),
            jnp.concatenate([-sin, z32, z64], axis=1),
            jnp.concatenate([z32, sin, z64], axis=1))


def _halves(w):
    n, rows, cols = w.shape
    return w.reshape(n, 2, rows // 2, cols)


_VMEM = pl.BlockSpec(memory_space=pltpu.VMEM)


def _pack_small(n1, nm, n2, nf, nret, ngla, ba, wa2_p, loss_blk):
    def body(n1_ref, nm_ref, n2_ref, nf_ref, nret_ref, ngla_ref, ba_ref, wa2_ref, loss_ref, o_ref):
        o_ref[...] = jnp.zeros_like(o_ref)
        o_ref[0:1, :] = n1_ref[...]
        o_ref[1:2, :] = nm_ref[...]
        o_ref[2:3, :] = n2_ref[...]
        o_ref[3:4, :] = nf_ref[...]
        o_ref[4:5, 0:512] = nret_ref[...]
        o_ref[4:5, 512:1024] = ngla_ref[...]
        o_ref[5:6, 0:256] = ba_ref[...]
        o_ref[6:7, 0:LANES] = loss_ref[0:1, :]
        o_ref[8:8 + GATE_RANK, 0:HEADS * LANES] = wa2_ref[0:GATE_RANK, :]

    return pl.pallas_call(
        body, name="pack_small", in_specs=[_VMEM] * 9, out_specs=_VMEM,
        out_shape=jax.ShapeDtypeStruct((SMALL_ROWS, D_MODEL), F32),
    )(n1, nm, n2, nf, nret, ngla, ba, wa2_p, loss_blk)


def _small_update(summed, chip_arr, ws, ms, vs):
    n = len(ws)

    def body(chip_ref, s_ref, *refs):
        w_refs, m_refs, v_refs = refs[0:n], refs[n:2 * n], refs[2 * n:3 * n]
        outs = refs[3 * n:]
        wa2_all = s_ref[8:8 + GATE_RANK, 0:HEADS * LANES]
        wa2_g = jnp.zeros((GATE_RANK, 64), F32)
        for p in range(N_CHIPS):
            wa2_g = jnp.where(chip_ref[0] == p, wa2_all[:, LANES * p:LANES * p + 64], wa2_g)
        grads = [s_ref[0:1, :], s_ref[1:2, :], s_ref[2:3, :], s_ref[3:4, :], s_ref[4:5, 0:512],
                 s_ref[4:5, 512:1024], s_ref[5:6, 0:256], wa2_g]
        for k in range(n):
            d, m2, v2 = _adamw_math(w_refs[k][...], grads[k], m_refs[k][...], v_refs[k][...])
            outs[k][...] = grads[k]
            outs[n + k][...] = d
            outs[2 * n + k][...] = m2
            outs[3 * n + k][...] = v2

    shapes = [jax.ShapeDtypeStruct(w.shape, F32) for w in ws] * 4
    smem = pl.BlockSpec(memory_space=pltpu.SMEM)
    outs = pl.pallas_call(
        body, name="small_update", in_specs=[smem] + [_VMEM] * (1 + 3 * n), out_specs=[_VMEM] * (4 * n),
        out_shape=shapes,
    )(chip_arr, summed, *ws, *ms, *vs)
    return outs[0:n], outs[n:2 * n], outs[2 * n:3 * n], outs[3 * n:4 * n]


def _pad_in_rows(w_t):
    return jnp.pad(w_t, ((0, IN_ROWS - IN_SHARD), (0, 0)))


def _forward_backward(xs, target, ffn1_w, rest, ba_p, ffn1_norm_g, mix_norm_g, ret_norm_g, gla_norm_g, ffn2_norm_g,
                      final_norm_g, ffn1_gather=None, rest_plan=None, rest_weights=None, ffn2_plans=None,
                      ffn2_weights=None, ffn2_pairs=None, ffn2_pairs_done=None, early=None, late=None, small_plan=None):
    t = xs.shape[0]
    cos_t, sa_t, sb_t = _rope_tables(t)
    log_gamma = jnp.log(1.0 - 2.0 ** (-5.0 - jnp.arange(HEADS, dtype=F32)))
    lg_t = jnp.broadcast_to(log_gamma[:, None, None], (HEADS, 1, LANES))
    ret_aux = [cos_t, sa_t, sb_t, lg_t]

    if ffn1_gather is None:
        (x1, a1, u1, h1), gathered = _ffn_fwd(xs, ffn1_norm_g, ffn1_w, "ffn1_fwd", hosted=rest_plan)
    else:
        ffn1_shard, ffn1_weights = ffn1_gather
        (x1, a1, u1, h1, wall), gathered = _ffn1_fwd_gathering(xs, ffn1_norm_g, ffn1_shard, "ffn1_fwd",
                                                               hosted=rest_plan)
        ffn1_w = ffn1_weights(wall)
    ffn2_w, w_in_pt, w_out_full, wa2_p = rest if rest_plan is None else rest_weights(gathered)
    plans = [None] * 3 if ffn2_plans is None else ffn2_plans
    (proj, h_mix), got_gate = _mixer_in_fwd(x1, mix_norm_g, w_in_pt, "mixer_in_fwd", hosted=plans[0])
    gla_aux = [proj, wa2_p, ba_p]
    (o_ret, raw_ret, st_ret), got_up = _attn_fwd(True, proj, ret_aux, ret_norm_g, "ret_fwd", hosted=plans[1])
    (o_gla, raw_gla, st_gla), got_down = _attn_fwd(False, proj, gla_aux, gla_norm_g, "gla_fwd", hosted=plans[2])
    if ffn2_plans is not None:
        ffn2_w = ffn2_weights(got_gate + got_up + got_down)
    x2 = _mixer_out_fwd(o_ret, o_gla, w_out_full, x1, "mixer_out_fwd")
    (loss_blk, dx3, d_final_g, a2, u2, h2), _ = _ffn_fwd(x2, ffn2_norm_g, ffn2_w, "ffn2_fwd",
                                                       loss_head=(final_norm_g, target))

    (da2, du2, hid2, dob2, dx2, d_ffn2_g, d_o), _ = _ffn_bwd(dx3, x2, ffn2_norm_g, a2, u2, ffn2_w, "ffn2_bwd",
                                                            back_w=w_out_full)
    g_gate2 = _matmul_tn(da2, h2, "ffn2_dgate", out_dtype=BF16)
    g_up2 = _matmul_tn(du2, h2, "ffn2_dup", out_dtype=BF16)
    g_down2 = _matmul_tn(hid2, dob2, "ffn2_ddown", out_dtype=BF16)

    g_wout_ret = _matmul_tn(o_ret, dx2, "wout_grad_ret", out_dtype=BF16)
    g_wout_gla = _matmul_tn(o_gla, dx2, "wout_grad_gla", out_dtype=BF16)
    pairs_plan = None if ffn2_pairs is None else ffn2_pairs([g_gate2, g_up2, g_down2])
    (*dproj_ret, d_ret_g), pair_recv = _attn_bwd(True, proj, ret_aux, ret_norm_g, raw_ret, st_ret, d_o, "ret_bwd",
                                                 hosted=pairs_plan)
    if ffn2_pairs is not None:
        ffn2_pairs_done(pair_recv)
    (*dproj_gla, d_gla_g, dlogit, d_ba_p), _ = _attn_bwd(False, proj, gla_aux, gla_norm_g, raw_gla, st_gla, d_o,
                                                        "gla_bwd")
    d_glow = _matmul_nt(dlogit, wa2_p, "gate_low_bwd", out_dtype=BF16)
    g_wa2_p = _matmul_tn(proj[:, PROJ_P - LANES:], dlogit, "gate_w_grad")
    dproj = jnp.concatenate(dproj_ret + dproj_gla + [d_glow], axis=1)
    g_win_p = _matmul_tn(dproj, h_mix, "w_in_grad", tka=PROJ_P // PROJ_TILES, out_dtype=BF16)
    dx1, d_mix_g = _mixer_in_bwd(dproj, w_in_pt, dx2, x1, mix_norm_g, "mixer_in_bwd")
    g_win_t = _unpad_w_in_t(g_win_p[0])
    g_win = jnp.stack([_pad_in_rows(g_win_t[IN_SHARD * p:IN_SHARD * (p + 1)]) for p in range(N_CHIPS)], axis=0)
    g_wout = jnp.concatenate([g_wout_ret[0], g_wout_gla[0]], axis=0).reshape(N_CHIPS, D_MODEL // N_CHIPS, D_MODEL)

    early_grads = [g_win, g_wout] if ffn2_pairs is not None else [g_gate2, g_up2, g_down2, g_win, g_wout]
    early_plan = None if early is None else early(early_grads)
    (da1, du1, hid1, dob1, grad_x, d_ffn1_g), arrived = _ffn_bwd(dx1, xs, ffn1_norm_g, a1, u1, ffn1_w, "ffn1_bwd",
                                                                hosted=early_plan)
    d_ba = d_ba_p.reshape(HEADS, LANES)[:, 0:64].reshape(1, 256)
    small_local = _pack_small(d_ffn1_g, d_mix_g, d_ffn2_g, d_final_g, d_ret_g, d_gla_g, d_ba, g_wa2_p[0], loss_blk)
    late_grads, late_arrived = [], []
    for lhs, rhs, name in ((da1, h1, "ffn1_dgate"), (du1, h1, "ffn1_dup"), (hid1, dob1, "ffn1_ddown")):
        if late is None:
            plan = None
        else:
            plan = late(late_grads[-1], len(late_grads)) if late_grads else small_plan(small_local)
        res = _matmul_tn(lhs, rhs, name, out_dtype=BF16, hosted=plan)
        if plan is not None:
            res, carried = res
            late_arrived += carried
        late_grads.append(res)
    g_gate1, g_up1, g_down1 = late_grads

    return (small_local, grad_x, g_gate1, g_up1, g_down1, g_gate2, g_up2, g_down2, g_win, g_wout, g_wa2_p,
            d_ba_p, d_ffn1_g, d_mix_g, d_ffn2_g, d_final_g, d_ret_g, d_gla_g, arrived, late_arrived)


def kernel(x, ffn1_norm_g, ffn1_w_gate, ffn1_w_up, ffn1_w_down, mix_norm_g, w_in, ret_norm_g, gla_w_a2, gla_b_a, gla_norm_g, w_out, ffn2_norm_g, ffn2_w_gate, ffn2_w_up, ffn2_w_down, final_norm_g, loss_target, m_ffn1_norm_g, m_ffn1_w_gate, m_ffn1_w_up, m_ffn1_w_down, m_mix_norm_g, m_w_in, m_ret_norm_g, m_gla_w_a2, m_gla_b_a, m_gla_norm_g, m_w_out, m_ffn2_norm_g, m_ffn2_w_gate, m_ffn2_w_up, m_ffn2_w_down, m_final_norm_g, v_ffn1_norm_g, v_ffn1_w_gate, v_ffn1_w_up, v_ffn1_w_down, v_mix_norm_g, v_w_in, v_ret_norm_g, v_gla_w_a2, v_gla_b_a, v_gla_norm_g, v_w_out, v_ffn2_norm_g, v_ffn2_w_gate, v_ffn2_w_up, v_ffn2_w_down, v_final_norm_g):
    t = x.shape[1]
    xs = x.reshape(t, D_MODEL)
    target = loss_target.reshape(t, D_MODEL)
    chip = 2 * lax.axis_index("x") + lax.axis_index("y")
    c_arr = lax.axis_index("c").astype(jnp.int32).reshape(1)

    me_arr = chip.astype(jnp.int32).reshape(1)

    pad_rows = _pad_in_rows

    ffn1_shard = _halves(jnp.stack([ffn1_w_gate[0].T, ffn1_w_up[0].T, ffn1_w_down[0]], axis=0).astype(BF16))
    rest_shards = [_halves(pad_rows(w_in[0].T).astype(BF16)[None]),
                   _halves(w_out.astype(BF16)),
                   jnp.concatenate([gla_w_a2.reshape(GATE_RANK, 64), jnp.zeros((GATE_RANK, 64), F32)],
                                   axis=1).reshape(1, 2, 8, LANES)]
    ffn2_shards = [_halves(w.astype(BF16)[None]) for w in (ffn2_w_gate[0].T, ffn2_w_up[0].T, ffn2_w_down[0])]
    def ffn1_weights(gathered):
        return gathered.reshape(N_CHIPS, 3, FF_SHARD, D_MODEL)

    def rest_weights(gathered):
        win_all, wout_all, wa2_all = gathered
        win_t = win_all.reshape(N_CHIPS, IN_ROWS, D_MODEL)
        w_in_pt = jnp.zeros((PROJ_P, D_MODEL), BF16)
        for p in range(N_CHIPS):
            w_in_pt = lax.dynamic_update_slice(w_in_pt, win_t[p, 0:IN_SHARD], (IN_SHARD * p, 0))
        wa2_p = jnp.pad(
            wa2_all.reshape(N_CHIPS, GATE_RANK, LANES).transpose(1, 0, 2).reshape(GATE_RANK, HEADS * LANES),
            ((0, LANES - GATE_RANK), (0, 0))).astype(BF16)
        return (None, w_in_pt, wout_all.reshape(D_MODEL, D_MODEL), wa2_p)

    def ffn2_weights(gathered):
        return [g.reshape(N_CHIPS, FF_SHARD, D_MODEL) for g in gathered]

    def by_halves(g):
        return g.reshape(g.shape[0], 2, g.shape[1] // 2, g.shape[2])

    def pair_adds(halves, recv, tag):
        return [_pair_add(g, r, c_arr, "pair_add_%s%d" % (tag, k)) for k, (g, r) in enumerate(zip(halves, recv))]

    def pair_sums(grads, tag):
        halves = [by_halves(g) for g in grads]
        recv = _run_hosted(_pair_exchange_plan(halves), "pair_exchange_" + tag)
        return pair_adds(halves, recv, tag)

    early_sums, ffn2_halves = [], []

    def ffn2_pairs(grads):
        ffn2_halves.extend(by_halves(g) for g in grads)
        return _pair_exchange_plan(ffn2_halves)

    def ffn2_pairs_done(recv):
        early_sums.extend(pair_adds(ffn2_halves, recv, "ffn2_"))

    def early(grads):
        early_sums.extend(pair_sums(grads, "early"))
        return _chip_exchange_plan(early_sums)

    late_sums = []

    def late(grad, number):
        late_sums.extend(pair_sums([grad], "late%d" % number))
        return _chip_exchange_plan(late_sums[-1:], by_peer=True)

    ba_p = jnp.pad(gla_b_a.reshape(HEADS, 64), ((0, 0), (0, 64))).reshape(1, HEADS * LANES)
    fb = _forward_backward(xs, target, None, None, ba_p, ffn1_norm_g, mix_norm_g, ret_norm_g, gla_norm_g,
                           ffn2_norm_g, final_norm_g.reshape(1, D_MODEL), ffn1_gather=(ffn1_shard, ffn1_weights),
                           rest_plan=_gather_plan(rest_shards), rest_weights=rest_weights,
                           ffn2_plans=[_gather_plan(ffn2_shards[0:2]), None, _gather_plan(ffn2_shards[2:3])],
                           ffn2_weights=ffn2_weights,
                           ffn2_pairs=ffn2_pairs, ffn2_pairs_done=ffn2_pairs_done, early=early, late=late,
                           small_plan=_small_gather_plan)
    (small_local, grad_x, _, _, g_down1, _, _, _, _, _, _, _, _, _, _, _, _, _, early_arrived, late_arrived) = fb
    small_all, late_arrived = late_arrived[0], late_arrived[1:]
    late_arrived = late_arrived + _run_hosted(late(g_down1, 3), "chip_exchange_late")
    mine = [_peer_sum(s, r, "chip_sum_%d" % k) for k, (s, r) in enumerate(zip(late_sums, late_arrived))]
    mine += [_chip_sum(s, r, me_arr, "chip_sum_%d" % (3 + k)) for k, (s, r) in enumerate(zip(early_sums, early_arrived))]
    other = _run_hosted(_pair_share_plan(mine), "pair_share")

    device = 2 * chip + lax.axis_index("c")
    small_sum = _sum_devices(lax.dynamic_update_slice(small_all, small_local[None], (device, 0, 0)))
    loss = small_sum[6, 0]

    def rows(n1, nm, n2, nf, nret, ngla, ba, wa2):
        return [n1, nm, n2, nf.reshape(1, D_MODEL), nret, ngla, ba, wa2.reshape(GATE_RANK, 64)]

    small = _small_update(
        small_sum, me_arr,
        rows(ffn1_norm_g, mix_norm_g, ffn2_norm_g, final_norm_g, ret_norm_g, gla_norm_g, gla_b_a, gla_w_a2),
        rows(m_ffn1_norm_g, m_mix_norm_g, m_ffn2_norm_g, m_final_norm_g, m_ret_norm_g, m_gla_norm_g, m_gla_b_a,
             m_gla_w_a2),
        rows(v_ffn1_norm_g, v_mix_norm_g, v_ffn2_norm_g, v_final_norm_g, v_ret_norm_g, v_gla_norm_g, v_gla_b_a,
             v_gla_w_a2))
    s_grad, s_delta, s_m, s_v = [
        [*o[0:3], o[3].reshape(D_MODEL), *o[4:7], o[7].reshape(1, GATE_RANK, 64)] for o in small]

    def big(k, w, m, v, name, to_2d, from_2d):
        outs4 = _adamw_halves(to_2d(w), mine[k], other[k], to_2d(m), to_2d(v), c_arr, name)
        return [from_2d(z) for z in outs4]

    plain = (lambda w: w[0], lambda z: z[None])
    transposed = (lambda w: w[0].T, lambda z: z.T[None])
    in_proj = (lambda w: pad_rows(w[0].T), lambda z: z[0:IN_SHARD].T[None])
    r_g1 = big(0, ffn1_w_gate, m_ffn1_w_gate, v_ffn1_w_gate, "adamw_ffn1_gate", *transposed)
    r_u1 = big(1, ffn1_w_up, m_ffn1_w_up, v_ffn1_w_up, "adamw_ffn1_up", *transposed)
    r_d1 = big(2, ffn1_w_down, m_ffn1_w_down, v_ffn1_w_down, "adamw_ffn1_down", *plain)
    r_g2 = big(3, ffn2_w_gate, m_ffn2_w_gate, v_ffn2_w_gate, "adamw_ffn2_gate", *transposed)
    r_u2 = big(4, ffn2_w_up, m_ffn2_w_up, v_ffn2_w_up, "adamw_ffn2_up", *transposed)
    r_d2 = big(5, ffn2_w_down, m_ffn2_w_down, v_ffn2_w_down, "adamw_ffn2_down", *plain)
    r_in = big(6, w_in, m_w_in, v_w_in, "adamw_w_in", *in_proj)
    r_out = big(7, w_out, m_w_out, v_w_out, "adamw_w_out", *plain)

    def leaves(k, smalls):
        n1, nm, n2, nf, nret, ngla, ba, wa2 = smalls
        return [n1, r_g1[k], r_u1[k], r_d1[k], nm, r_in[k], nret, wa2, ba, ngla, r_out[k], n2, r_g2[k], r_u2[k], r_d2[k], nf]

    outs = [loss, grad_x.reshape(x.shape)]
    outs += leaves(0, s_grad) + leaves(1, s_delta) + leaves(2, s_m) + leaves(3, s_v)
    return tuple(outs)
```

```python
import functools

import jax
import jax.numpy as jnp
from jax import lax
from jax.experimental import pallas as pl
from jax.experimental.pallas import tpu as pltpu

F32, BF16 = jnp.float32, jnp.bfloat16
MESH = pl.DeviceIdType.MESH
ANY = pl.BlockSpec(memory_space=pl.ANY)

D_MODEL = 1024
D_FF = 2816
N_CHIPS = 4
FF_SHARD = D_FF // N_CHIPS
IN_WIDTH = 3088
IN_SHARD = IN_WIDTH // N_CHIPS
IN_ROWS = 800
CHUNK = 64
HEADS = 4
LANES = 128
PROJ_P = 3072 + LANES
PROJ_TILES = 5
GATE_RANK = 16
QK_SCALE = 0.125
GATE_NORM = 16.0
RMS_EPS = 1e-6
ROPE_BASE = 10000.0
ADAM_LR, ADAM_B1, ADAM_B2, ADAM_EPS, ADAM_WD, ADAM_STEP = 0.001, 0.9, 0.999, 1e-08, 0.01, 10
SMALL_ROWS = 32
TOKEN_TILE = 512
ATTN_TILE = 512

_ARB2 = pltpu.CompilerParams(dimension_semantics=("arbitrary", "arbitrary"))
_ARB1 = pltpu.CompilerParams(dimension_semantics=("arbitrary",))
_ARB3 = pltpu.CompilerParams(dimension_semantics=("arbitrary", "arbitrary", "arbitrary"))


def _dot(a, b):
    return jnp.dot(a, b, preferred_element_type=F32)


def _dot_nt(a, b):
    return lax.dot_general(a, b, (((1,), (1,)), ((), ())), preferred_element_type=F32)


def _dot_tn(a, b):
    return lax.dot_general(a, b, (((0,), (0,)), ((), ())), preferred_element_type=F32)


def _rms_scale(xv):
    return lax.rsqrt(jnp.mean(xv * xv, axis=-1, keepdims=True) + RMS_EPS)


def _rms_bwd(dh, xv, g):
    r = _rms_scale(xv)
    xhat = xv * r
    dxhat = dh * g
    dx = r * (dxhat - xhat * jnp.mean(dxhat * xhat, axis=-1, keepdims=True))
    return dx, jnp.sum(dh * xhat, axis=0, keepdims=True)


def _silu_grad(a, sg):
    return sg * (1.0 + a * (1.0 - sg))


class _Hosted:
    def __init__(self, arrays, out_shapes, n_sems, start, finish, middle=None, peers=None):
        self.arrays, self.out_shapes, self.n_sems = list(arrays), list(out_shapes), n_sems
        self.start, self.finish = start, finish
        self.middle = middle if middle is not None else (lambda *refs: None)
        self.peers = peers


PEER_SETS = {
    "sibling": (0, lambda x, y, c: [(x, y, 1 - c)]),
    "chips": (1, lambda x, y, c: [(1 - x, y, c), (x, 1 - y, c), (1 - x, 1 - y, c)]),
    "neighbours": (2, lambda x, y, c: [(1 - x, y, c), (x, 1 - y, c), (x, y, 1 - c)]),
    "chips_sibling": (3, lambda x, y, c: [(1 - x, y, c), (x, 1 - y, c), (1 - x, 1 - y, c), (x, y, 1 - c)]),
}


def _handshake(kind):
    x, y, c, _, _ = _place()
    peers = PEER_SETS[kind][1](x, y, c)
    barrier = pltpu.get_barrier_semaphore()
    for peer in peers:
        pl.semaphore_signal(barrier, inc=1, device_id=peer, device_id_type=MESH)
    pl.semaphore_wait(barrier, len(peers))


def _with_barrier(compiler_params, kind):
    if kind is None:
        return compiler_params
    semantics = None if compiler_params is None else compiler_params.dimension_semantics
    return pltpu.CompilerParams(dimension_semantics=semantics, collective_id=PEER_SETS[kind][0])


def _call(body, args, *, name, grid, in_specs, out_specs, out_shape, scratch_shapes, compiler_params, hosted=None):
    if hosted is None:
        outs = pl.pallas_call(body, name=name, grid=grid, in_specs=in_specs, out_specs=out_specs, out_shape=out_shape,
                              scratch_shapes=scratch_shapes, compiler_params=compiler_params)(*args)
        return list(outs), []
    n_in, n_out, n_sc, nh = len(in_specs), len(out_specs), len(scratch_shapes), len(hosted.arrays)

    def wrapped(*refs):
        ins, h_in = refs[:n_in], refs[n_in:n_in + nh]
        outs, h_out = refs[n_in + nh:n_in + nh + n_out], refs[n_in + nh + n_out:n_in + 2 * nh + n_out]
        rest = refs[n_in + 2 * nh + n_out:]
        scratch, (send_sems, recv_sems) = rest[:n_sc], rest[n_sc:]
        step = functools.reduce(lambda flat, d: flat * grid[d] + pl.program_id(d), range(len(grid)), 0)
        total = functools.reduce(lambda a, b: a * b, grid)

        @pl.when(step == 0)
        def _():
            if hosted.peers is not None:
                _handshake(hosted.peers)
            hosted.start(h_in, h_out, send_sems, recv_sems)

        @pl.when(step == total // 2)
        def _():
            hosted.middle(h_in, h_out, send_sems, recv_sems)

        body(*ins, *outs, *scratch)
        last = step == total - 1

        @pl.when(last)
        def _():
            hosted.finish(h_in, h_out, send_sems, recv_sems)

    sems = [pltpu.SemaphoreType.DMA((hosted.n_sems,)), pltpu.SemaphoreType.DMA((hosted.n_sems,))]
    outs = pl.pallas_call(
        wrapped, name=name, grid=grid, in_specs=list(in_specs) + [ANY] * nh, out_specs=list(out_specs) + [ANY] * nh,
        out_shape=list(out_shape) + hosted.out_shapes, scratch_shapes=list(scratch_shapes) + sems,
        compiler_params=_with_barrier(compiler_params, hosted.peers))(*args, *hosted.arrays)
    return list(outs[:n_out]), list(outs[n_out:])


def _run_hosted(hosted, name):
    nh = len(hosted.arrays)

    def body(*refs):
        h_in, h_out, (send_sems, recv_sems) = refs[:nh], refs[nh:2 * nh], refs[2 * nh:]
        if hosted.peers is not None:
            _handshake(hosted.peers)
        hosted.start(h_in, h_out, send_sems, recv_sems)
        hosted.middle(h_in, h_out, send_sems, recv_sems)
        hosted.finish(h_in, h_out, send_sems, recv_sems)

    sems = [pltpu.SemaphoreType.DMA((hosted.n_sems,)), pltpu.SemaphoreType.DMA((hosted.n_sems,))]
    return list(pl.pallas_call(body, name=name, in_specs=[ANY] * nh, out_specs=[ANY] * nh, out_shape=hosted.out_shapes,
                               scratch_shapes=sems, compiler_params=_with_barrier(None, hosted.peers))(*hosted.arrays))


def _ffn_weight_operands(ffn_w, chunk_maps):
    if isinstance(ffn_w, (list, tuple)):
        specs = [pl.BlockSpec((None, FF_SHARD, D_MODEL), lambda *g, m=m: (m(*g), 0, 0)) for m in chunk_maps]
        return list(ffn_w), specs
    specs = [pl.BlockSpec((None, None, FF_SHARD, D_MODEL), lambda *g, m=m, k=kind: (m(*g), k, 0, 0))
             for kind, m in enumerate(chunk_maps)]
    return [ffn_w] * 3, specs


def _pipeline_item(steps, lag):
    def item(s):
        it = jnp.clip(s - lag, 0, steps - 1)
        return it // N_CHIPS, it % N_CHIPS

    return item


def _ffn_fwd(x, g, ffn_w, name, hosted=None, loss_head=None):
    t = x.shape[0]
    tm = min(t, TOKEN_TILE)
    n_head = 0 if loss_head is None else 2

    def body(*refs):
        x_ref, g_ref, wg_ref, wu_ref, wd_ref = refs[0:5]
        head_in = refs[5:5 + n_head]
        outs = refs[5 + n_head:-1]
        acc_ref = refs[-1]
        a_ref, u_ref, h_ref = outs[-3:]
        i, j = pl.program_id(0), pl.program_id(1)

        @pl.when(j == 0)
        def _():
            xv = x_ref[...]
            h_ref[...] = ((xv * _rms_scale(xv)) * g_ref[...]).astype(BF16)
            acc_ref[...] = jnp.zeros_like(acc_ref)

        h = h_ref[...]
        a = _dot_nt(h, wg_ref[...])
        u = _dot_nt(h, wu_ref[...])
        a_ref[...] = a.astype(BF16)
        u_ref[...] = u.astype(BF16)
        hid = (a * jax.nn.sigmoid(a)) * u
        acc_ref[...] += _dot(hid.astype(BF16), wd_ref[...])

        if loss_head is None:
            @pl.when(j == N_CHIPS - 1)
            def _():
                outs[0][...] = x_ref[...] + 0.5 * acc_ref[...]
        else:
            gf_ref, t_ref = head_in
            l_ref, dx_ref, dgf_ref = outs[0:3]

            @pl.when((i == 0) & (j == 0))
            def _():
                l_ref[...] = jnp.zeros_like(l_ref)
                dgf_ref[...] = jnp.zeros_like(dgf_ref)

            @pl.when(j == N_CHIPS - 1)
            def _():
                xv = x_ref[...] + 0.5 * acc_ref[...]
                gv = gf_ref[...]
                err = (xv * _rms_scale(xv)) * gv - t_ref[...]
                l_ref[...] += 0.5 * jnp.sum(jnp.mean(err * err, axis=-1, keepdims=True), axis=0, keepdims=True)
                dx, dg = _rms_bwd(err * (1.0 / D_MODEL), xv, gv)
                dx_ref[...] = dx
                dgf_ref[...] += dg

    tok = pl.BlockSpec((tm, D_MODEL), lambda i, j: (i, 0))
    row = pl.BlockSpec((1, D_MODEL), lambda i, j: (0, 0))
    act = pl.BlockSpec((None, tm, FF_SHARD), lambda i, j: (j, i, 0))
    act_shape = jax.ShapeDtypeStruct((N_CHIPS, t, FF_SHARD), BF16)
    w_arrays, weights = _ffn_weight_operands(ffn_w, [lambda i, j: j] * 3)
    if loss_head is None:
        first_specs, first_shapes, head_args, head_specs = [tok], [jax.ShapeDtypeStruct((t, D_MODEL), F32)], [], []
    else:
        first_specs = [pl.BlockSpec((8, LANES), lambda i, j: (0, 0)), tok, row]
        first_shapes = [jax.ShapeDtypeStruct((8, LANES), F32), jax.ShapeDtypeStruct((t, D_MODEL), F32),
                        jax.ShapeDtypeStruct((1, D_MODEL), F32)]
        head_args, head_specs = list(loss_head), [row, tok]
    return _call(
        body, (x, g, *w_arrays, *head_args), name=name, grid=(t // tm, N_CHIPS),
        in_specs=[tok, row] + weights + head_specs,
        out_specs=first_specs + [act, act, tok],
        out_shape=first_shapes + [act_shape, act_shape, jax.ShapeDtypeStruct((t, D_MODEL), BF16)],
        scratch_shapes=[pltpu.VMEM((tm, D_MODEL), F32)],
        compiler_params=_ARB2, hosted=hosted)


def _ffn1_fwd_gathering(x, g, shard, name, hosted=None):
    t = x.shape[0]
    tm = min(t, TOKEN_TILE)
    nt = t // tm
    nh = 0 if hosted is None else len(hosted.arrays)
    peers = "neighbours" if hosted is None or hosted.peers == "neighbours" else "chips_sibling"
    assert hosted is None or hosted.peers in ("neighbours", "chips_sibling")

    def body(*refs):
        x_ref, g_ref, shard_ref = refs[0:3]
        h_in = refs[3:3 + nh]
        xo_ref, a_ref, u_ref, h_ref, wall = refs[3 + nh:8 + nh]
        h_out = refs[8 + nh:8 + 2 * nh]
        acc, h_all, wbuf, load_sems, send_sems, recv_sems = refs[8 + 2 * nh:14 + 2 * nh]
        carried_sems = refs[14 + 2 * nh:]
        k, i = pl.program_id(0), pl.program_id(1)
        legs, _ = _gather_legs(shard_ref, wall, send_sems, recv_sems, 0, True)
        begin, pass_on, _ = _gather_steps(legs, True)

        def load(chunk, src):
            return pltpu.make_async_copy(src, wbuf.at[chunk % 2], load_sems.at[chunk % 2])

        @pl.when((k == 0) & (i == 0))
        def _():
            _handshake(peers)
            begin()
            load(0, shard_ref).start()
            load(0, shard_ref).wait()

        @pl.when((k == 1) & (i == 0))
        def _():
            pass_on()
            if hosted is not None:
                hosted.start(h_in, h_out, *carried_sems)
            legs["pass_y"][1].wait_recv()
            load(1, wall.at[PEER_SLOT[1]]).start()
            load(1, wall.at[PEER_SLOT[1]]).wait()

        @pl.when((k == 1) & (i == nt // 2))
        def _():
            legs["pass_x"][1].wait_recv()
            load(2, wall.at[PEER_SLOT[0]]).start()

        @pl.when((k == 2) & (i == 0))
        def _():
            load(2, wall.at[PEER_SLOT[0]]).wait()

        @pl.when((k == 2) & (i == nt // 2))
        def _():
            legs["fwd_y"][1].wait_recv()
            legs["pass_d0"][0].start()
            legs["fwd_x"][1].wait_recv()
            legs["pass_d1"][0].start()
            legs["pass_d0"][1].wait_recv()
            legs["pass_d1"][1].wait_recv()
            load(3, wall.at[PEER_SLOT[2]]).start()
            if hosted is not None:
                hosted.middle(h_in, h_out, *carried_sems)

        @pl.when((k == 3) & (i == 0))
        def _():
            load(3, wall.at[PEER_SLOT[2]]).wait()

        @pl.when(k == 0)
        def _():
            xv = x_ref[...]
            h0 = ((xv * _rms_scale(xv)) * g_ref[...]).astype(BF16)
            h_all[i] = h0
            h_ref[...] = h0

        h = h_all[i]
        wg, wu, wd = (wbuf[k % 2, kind].reshape(FF_SHARD, D_MODEL) for kind in range(3))
        a = _dot_nt(h, wg)
        u = _dot_nt(h, wu)
        a_ref[...] = a.astype(BF16)
        u_ref[...] = u.astype(BF16)
        part = _dot(((a * jax.nn.sigmoid(a)) * u).astype(BF16), wd)

        @pl.when(k == 0)
        def _():
            acc[i] = part

        @pl.when(k > 0)
        def _():
            acc[i] += part

        @pl.when(k == N_CHIPS - 1)
        def _():
            xo_ref[...] = x_ref[...] + 0.5 * acc[i]

        @pl.when((k == N_CHIPS - 1) & (i == nt - 1))
        def _():
            legs["own"][1].wait_recv()
            for pair in legs.values():
                pair[0].wait_send()
            if hosted is not None:
                hosted.finish(h_in, h_out, *carried_sems)

    def first_or_last(k):
        return (k == 0) | (k == N_CHIPS - 1)

    tok = lambda keep: pl.BlockSpec((tm, D_MODEL), lambda k, i: (jnp.where(keep(k), i, 0), 0))
    act = pl.BlockSpec((None, tm, FF_SHARD), lambda k, i: (k, i, 0))
    act_shape = jax.ShapeDtypeStruct((N_CHIPS, t, FF_SHARD), BF16)
    carried = [] if hosted is None else [pltpu.SemaphoreType.DMA((hosted.n_sems,))] * 2
    outs = pl.pallas_call(
        body, name=name, grid=(N_CHIPS, nt),
        in_specs=[tok(first_or_last), pl.BlockSpec((1, D_MODEL), lambda k, i: (0, 0)), ANY] + [ANY] * nh,
        out_specs=[tok(lambda k: k == N_CHIPS - 1), act, act,
                   pl.BlockSpec((tm, D_MODEL), lambda k, i: (jnp.where(k == 0, i, nt - 1), 0)), ANY] + [ANY] * nh,
        out_shape=[jax.ShapeDtypeStruct((t, D_MODEL), F32), act_shape, act_shape,
                   jax.ShapeDtypeStruct((t, D_MODEL), BF16),
                   jax.ShapeDtypeStruct((N_CHIPS,) + shard.shape, shard.dtype)]
                  + ([] if hosted is None else hosted.out_shapes),
        scratch_shapes=[pltpu.VMEM((nt, tm, D_MODEL), F32), pltpu.VMEM((nt, tm, D_MODEL), BF16),
                        pltpu.VMEM((2,) + shard.shape, shard.dtype), pltpu.SemaphoreType.DMA((2,)),
                        pltpu.SemaphoreType.DMA((9,)), pltpu.SemaphoreType.DMA((9,))] + carried,
        compiler_params=_with_barrier(_ARB2, peers),
    )(x, g, shard, *([] if hosted is None else hosted.arrays))
    return list(outs[:5]), list(outs[5:])


def _ffn_bwd(dxo, x, g, a4, u4, ffn_w, name, hosted=None, back_w=None):
    t = x.shape[0]
    tm = min(t, TOKEN_TILE)
    steps = (t // tm) * N_CHIPS
    cur, old = _pipeline_item(steps, 0), _pipeline_item(steps, 1)

    def body(*refs):
        dxo_ref, dxo_old_ref, x_ref, g_ref, a_ref, u_ref, wg_ref, wu_ref, wd_ref = refs[0:9]
        n_back = 0 if back_w is None else 1
        da_ref, du_ref, hid_ref, dob_ref, dx_ref, dg_ref = refs[9 + n_back:15 + n_back]
        acc_ref, da_slots, du_slots = refs[-3:]
        s = pl.program_id(0)
        jc, jo = cur(s)[1], old(s)[1]
        slot = s % 2

        @pl.when(s == 0)
        def _():
            dg_ref[...] = jnp.zeros_like(dg_ref)
            acc_ref[...] = jnp.zeros_like(acc_ref)
            da_slots[...] = jnp.zeros_like(da_slots)
            du_slots[...] = jnp.zeros_like(du_slots)

        @pl.when(jc == 0)
        def _():
            dob_ref[...] = (0.5 * dxo_ref[...]).astype(BF16)

        dhid = _dot_nt(dob_ref[...], wd_ref[...])
        a = a_ref[...].astype(F32)
        u = u_ref[...].astype(F32)
        sg = jax.nn.sigmoid(a)
        sl = a * sg
        hid_ref[...] = (sl * u).astype(BF16)
        du = (dhid * sl).astype(BF16)
        da = (dhid * u * _silu_grad(a, sg)).astype(BF16)
        du_ref[...] = du
        da_ref[...] = da
        acc_ref[...] += _dot(da_slots[1 - slot], wg_ref[...]) + _dot(du_slots[1 - slot], wu_ref[...])
        da_slots[slot] = da
        du_slots[slot] = du

        @pl.when((jo == N_CHIPS - 1) & (s > 0))
        def _():
            dx, dg = _rms_bwd(acc_ref[...], x_ref[...], g_ref[...])
            dx = dxo_old_ref[...] + dx
            dx_ref[...] = dx
            dg_ref[...] += dg
            acc_ref[...] = jnp.zeros_like(acc_ref)
            if back_w is not None:
                refs[15 + n_back][...] = _dot_nt(dx.astype(BF16), refs[9][...])

    tok_cur = pl.BlockSpec((tm, D_MODEL), lambda s: (cur(s)[0], 0))
    tok_old = pl.BlockSpec((tm, D_MODEL), lambda s: (old(s)[0], 0))
    act = pl.BlockSpec((None, tm, FF_SHARD), lambda s: (cur(s)[1], cur(s)[0], 0))
    row = pl.BlockSpec((1, D_MODEL), lambda s: (0, 0))
    w_arrays, weights = _ffn_weight_operands(ffn_w, [lambda s: old(s)[1], lambda s: old(s)[1], lambda s: cur(s)[1]])
    act_shape = jax.ShapeDtypeStruct((N_CHIPS, t, FF_SHARD), BF16)
    back_args, back_in, back_out, back_shape = [], [], [], []
    if back_w is not None:
        n_back_cols = back_w.shape[0]
        back_args, back_in = [back_w], [pl.BlockSpec(back_w.shape, lambda s: (0, 0))]
        back_out = [pl.BlockSpec((tm, n_back_cols), lambda s: (old(s)[0], 0))]
        back_shape = [jax.ShapeDtypeStruct((t, n_back_cols), F32)]
    return _call(
        body, (dxo, dxo, x, g, a4, u4, *w_arrays, *back_args), name=name, grid=(steps + 1,),
        in_specs=[tok_cur, tok_old, tok_old, row, act, act] + weights + back_in,
        out_specs=[act, act, act, tok_cur, tok_old, row] + back_out,
        out_shape=[act_shape, act_shape, act_shape,
                   jax.ShapeDtypeStruct((t, D_MODEL), BF16),
                   jax.ShapeDtypeStruct((t, D_MODEL), F32),
                   jax.ShapeDtypeStruct((1, D_MODEL), F32)] + back_shape,
        scratch_shapes=[pltpu.VMEM((tm, D_MODEL), F32), pltpu.VMEM((2, tm, FF_SHARD), BF16),
                        pltpu.VMEM((2, tm, FF_SHARD), BF16)],
        compiler_params=_ARB1, hosted=hosted)


def _matmul_tn(a, b, name, tka=None, out_dtype=F32, hosted=None):
    a3, b3 = a.ndim == 3, b.ndim == 3
    nb = a.shape[0] if a3 else (b.shape[0] if b3 else 1)
    t, ka, n = a.shape[-2], a.shape[-1], b.shape[-1]
    tka = ka if tka is None else tka
    tk = min(t, 4 * TOKEN_TILE)
    nk = t // tk

    def body(a_ref, b_ref, o_ref, acc_ref):
        k = pl.program_id(2)

        @pl.when(k == 0)
        def _():
            acc_ref[...] = jnp.zeros_like(acc_ref)

        acc_ref[...] += _dot_tn(a_ref[...].astype(BF16), b_ref[...].astype(BF16))

        @pl.when(k == nk - 1)
        def _():
            o_ref[...] = acc_ref[...].astype(out_dtype)

    a_spec = (pl.BlockSpec((None, tk, tka), lambda i, j, k: (i, k, j)) if a3
              else pl.BlockSpec((tk, tka), lambda i, j, k: (k, j)))
    b_spec = (pl.BlockSpec((None, tk, n), lambda i, j, k: (i, k, 0)) if b3
              else pl.BlockSpec((tk, n), lambda i, j, k: (k, 0)))
    outs, carried = _call(
        body, (a, b), name=name, grid=(nb, ka // tka, t // tk),
        in_specs=[a_spec, b_spec],
        out_specs=[pl.BlockSpec((None, tka, n), lambda i, j, k: (i, j, 0))],
        out_shape=[jax.ShapeDtypeStruct((nb, ka, n), out_dtype)],
        scratch_shapes=[pltpu.VMEM((tka, n), F32)],
        compiler_params=_ARB3, hosted=hosted)
    return outs[0] if hosted is None else (outs[0], carried)


def _matmul_nt(a, w, name, out_dtype=F32):
    t, k = a.shape
    n = w.shape[0]
    tm = min(t, TOKEN_TILE)

    def body(a_ref, w_ref, o_ref):
        o_ref[...] = _dot_nt(a_ref[...].astype(BF16), w_ref[...]).astype(out_dtype)

    return pl.pallas_call(
        body, name=name, grid=(t // tm,),
        in_specs=[pl.BlockSpec((tm, k), lambda i: (i, 0)), pl.BlockSpec((n, k), lambda i: (0, 0))],
        out_specs=pl.BlockSpec((tm, n), lambda i: (i, 0)),
        out_shape=jax.ShapeDtypeStruct((t, n), out_dtype),
        compiler_params=_ARB1,
    )(a, w)


def _mixer_in_bwd(dproj, w_in_pt, dres, x, g, name):
    t, k = dproj.shape
    tm = min(t, TOKEN_TILE)

    def body(a_ref, w_ref, dres_ref, x_ref, g_ref, dx_ref, dg_ref):
        @pl.when(pl.program_id(0) == 0)
        def _():
            dg_ref[...] = jnp.zeros_like(dg_ref)

        dh = _dot(a_ref[...], w_ref[...])
        dx, dg = _rms_bwd(dh, x_ref[...], g_ref[...])
        dx_ref[...] = dres_ref[...] + dx
        dg_ref[...] += dg

    tok = pl.BlockSpec((tm, D_MODEL), lambda i: (i, 0))
    row = pl.BlockSpec((1, D_MODEL), lambda i: (0, 0))
    return pl.pallas_call(
        body, name=name, grid=(t // tm,),
        in_specs=[pl.BlockSpec((tm, k), lambda i: (i, 0)), pl.BlockSpec((k, D_MODEL), lambda i: (0, 0)), tok, tok, row],
        out_specs=[tok, row],
        out_shape=[jax.ShapeDtypeStruct((t, D_MODEL), F32), jax.ShapeDtypeStruct((1, D_MODEL), F32)],
        compiler_params=_ARB1,
    )(dproj, w_in_pt, dres, x, g)


def _mixer_in_fwd(x, g, w_in_pt, name, hosted=None):
    t = x.shape[0]
    tm = min(t, TOKEN_TILE)
    tn = PROJ_P // PROJ_TILES

    def body(x_ref, g_ref, w_ref, p_ref, h_ref):
        @pl.when(pl.program_id(1) == 0)
        def _():
            xv = x_ref[...]
            h_ref[...] = ((xv * _rms_scale(xv)) * g_ref[...]).astype(BF16)

        p_ref[...] = _dot_nt(h_ref[...], w_ref[...])

    tok = pl.BlockSpec((tm, D_MODEL), lambda i, j: (i, 0))
    return _call(
        body, (x, g, w_in_pt), name=name, grid=(t // tm, PROJ_TILES),
        in_specs=[tok, pl.BlockSpec((1, D_MODEL), lambda i, j: (0, 0)),
                  pl.BlockSpec((tn, D_MODEL), lambda i, j: (j, 0))],
        out_specs=[pl.BlockSpec((tm, tn), lambda i, j: (i, j)), tok],
        out_shape=[jax.ShapeDtypeStruct((t, PROJ_P), F32), jax.ShapeDtypeStruct((t, D_MODEL), BF16)],
        scratch_shapes=[], compiler_params=_ARB2, hosted=hosted)


def _mixer_out_fwd(o_ret, o_gla, w_out, x, name):
    t = x.shape[0]
    tm = min(t, TOKEN_TILE)
    half = HEADS * LANES

    def body(a_ref, b_ref, w_ref, x_ref, o_ref):
        o_ref[...] = x_ref[...] + _dot(a_ref[...], w_ref[0:half, :]) + _dot(b_ref[...], w_ref[half:2 * half, :])

    tok = pl.BlockSpec((tm, D_MODEL), lambda i: (i, 0))
    hb = pl.BlockSpec((tm, half), lambda i: (i, 0))
    return pl.pallas_call(
        body, name=name, grid=(t // tm,),
        in_specs=[hb, hb, pl.BlockSpec((2 * half, D_MODEL), lambda i: (0, 0)), tok],
        out_specs=tok, out_shape=jax.ShapeDtypeStruct((t, D_MODEL), F32),
        compiler_params=_ARB1,
    )(o_ret, o_gla, w_out, x)


def _rot(v, cos, sa, sb):
    return v * cos + pltpu.roll(v, 96, 1) * sa + pltpu.roll(v, 32, 1) * sb


def _rot_t(d, cos, sa, sb):
    return d * cos + pltpu.roll(d * sa, 32, 1) + pltpu.roll(d * sb, 96, 1)


def _bmm(a, b):
    return jnp.einsum("cik,ckj->cij", a, b, preferred_element_type=F32)


def _bmm_nt(a, b):
    return jnp.einsum("cik,cjk->cij", a, b, preferred_element_type=F32)


def _bmm_tn(a, b):
    return jnp.einsum("cki,ckj->cij", a, b, preferred_element_type=F32)


def _masked_sum(mask, x):
    hi = x.astype(BF16)
    r1 = x - hi.astype(F32)
    mid = r1.astype(BF16)
    lo = (r1 - mid.astype(F32)).astype(BF16)
    return _bmm(mask, hi) + _bmm(mask, mid) + _bmm(mask, lo)


PAIR = 2


def _tile_inputs(is_ret, qkvg_refs, aux, nc):
    shape3 = (nc, CHUNK, LANES)
    q_ref, k_ref, v_ref, g_ref = qkvg_refs
    low_lanes = lax.broadcasted_iota(jnp.int32, (1, LANES), 1) < 64
    ri = lax.broadcasted_iota(jnp.int32, (PAIR * nc, CHUNK, CHUNK), 1)
    ci = lax.broadcasted_iota(jnp.int32, (PAIR * nc, CHUNK, CHUNK), 2)
    qs, ks, vs, bs, gates, extra = [], [], [], [], [], []
    for hd in range(PAIR):
        q_blk, k_blk = q_ref[...], k_ref[...]
        if hd == 1:
            q_blk, k_blk = pltpu.roll(q_blk, 64, 1), pltpu.roll(k_blk, 64, 1)
        q_raw, k_raw = jnp.where(low_lanes, q_blk, 0.0), jnp.where(low_lanes, k_blk, 0.0)
        vs.append(v_ref[:, LANES * hd:LANES * (hd + 1)].reshape(shape3))
        gates.append(g_ref[:, LANES * hd:LANES * (hd + 1)])
        if is_ret:
            cos_ref, sa_ref, sb_ref, lg_ref = aux
            cos, sa, sb = cos_ref[...], sa_ref[...], sb_ref[...]
            q = _rot(q_raw, cos, sa, sb)
            k = _rot(k_raw, cos, sa, sb) * QK_SCALE
            steps = (lax.broadcasted_iota(jnp.int32, shape3, 1) + 1).astype(F32)
            bs.append(steps * lg_ref[hd])
            extra.append(jnp.exp(jnp.abs(ri[0:nc] - ci[0:nc]).astype(F32) * lg_ref[hd][:, 0:CHUNK]))
        else:
            glow_ref, wa2_ref, ba_ref = aux
            lanes = slice(LANES * hd, LANES * (hd + 1))
            logit = _dot(glow_ref[...].astype(BF16), wa2_ref[:, lanes]) + ba_ref[:, lanes]
            la = (jnp.minimum(logit, 0.0) - jnp.log1p(jnp.exp(-jnp.abs(logit)))) * (1.0 / GATE_NORM)
            bs.append(_masked_sum((ci[0:nc] <= ri[0:nc]).astype(BF16), la.reshape(shape3)))
            extra.append(logit)
            q = q_raw * QK_SCALE
            k = k_raw
        qs.append(q.reshape(shape3))
        ks.append(k.reshape(shape3))
    cat = lambda parts: jnp.concatenate(parts, axis=0)
    return cat(qs), cat(ks), cat(vs), gates, cat(bs), extra, ri, ci


def _tile_scores(q, k, b, ri, ci):
    mid = b[:, CHUNK // 2 - 1:CHUNK // 2, :]
    ep = jnp.exp(b - mid)
    en = jnp.exp(mid - b)
    qt, kt, qh, kh = q * ep, k * en, q * en, k * ep
    low = _bmm_nt(qt.astype(BF16), kt.astype(BF16))
    upp = _bmm_nt(qh.astype(BF16), kh.astype(BF16))
    scores = jnp.where(ci <= ri, low, upp)
    return scores, ep, en, qt, kt, qh, kh


def _attn_specs(is_ret, t, tb, imap_t):
    nb = t // tb
    base = 0 if is_ret else 12
    wide = PAIR * LANES
    proj = [pl.BlockSpec((tb, LANES), lambda p, i: (imap_t(i), base + p)),
            pl.BlockSpec((tb, LANES), lambda p, i: (imap_t(i), base + 2 + p)),
            pl.BlockSpec((tb, wide), lambda p, i: (imap_t(i), (base + 4) // 2 + p)),
            pl.BlockSpec((tb, wide), lambda p, i: (imap_t(i), (base + 8) // 2 + p))]
    lane_t = pl.BlockSpec((tb, LANES), lambda p, i: (imap_t(i), 0))
    if is_ret:
        aux = [lane_t, lane_t, lane_t, pl.BlockSpec((PAIR, 1, LANES), lambda p, i: (p, 0, 0))]
    else:
        aux = [pl.BlockSpec((tb, LANES), lambda p, i: (imap_t(i), PROJ_P // LANES - 1)),
               pl.BlockSpec((LANES, wide), lambda p, i: (0, p)),
               pl.BlockSpec((1, wide), lambda p, i: (0, p))]
    gain = pl.BlockSpec((1, wide), lambda p, i: (0, p))
    pair_t = pl.BlockSpec((tb, wide), lambda p, i: (imap_t(i), p))
    narrow_t = pl.BlockSpec((tb, LANES), lambda p, i: (imap_t(i), p))
    state = pl.BlockSpec((PAIR, tb // CHUNK, LANES, LANES), lambda p, i: (p, imap_t(i), 0, 0))
    return nb, proj, aux, gain, pair_t, narrow_t, state


def _attn_fwd(is_ret, proj, aux_arrays, gain, name, hosted=None):
    t = proj.shape[0]
    tb = min(t, ATTN_TILE)
    nc = tb // CHUNK
    n_aux = 4 if is_ret else 3
    nb, proj_spec, aux_specs, gain_spec, pair_t, _, state_spec = _attn_specs(is_ret, t, tb, lambda i: i)

    def body(*refs):
        qkvg_refs = refs[0:4]
        aux = refs[4:4 + n_aux]
        gn_ref, ofin_ref, oraw_ref, st_ref, state = refs[4 + n_aux:]

        @pl.when(pl.program_id(1) == 0)
        def _():
            state[...] = jnp.zeros_like(state)

        q, k, v, gates, b, extra, ri, ci = _tile_inputs(is_ret, qkvg_refs, aux, nc)
        if is_ret:
            scores = _bmm_nt(q.astype(BF16), k.astype(BF16)) * jnp.concatenate(extra, axis=0)
        else:
            scores = _tile_scores(q, k, b, ri, ci)[0]
        vb = v.astype(BF16)
        intra = _bmm(scores.astype(BF16), vb)
        b_last = b[:, CHUNK - 1:CHUNK, :]
        e_last = jnp.exp(b_last)
        grow = _bmm_tn(vb, (k * jnp.exp(b_last - b)).astype(BF16))
        for hd in range(PAIR):
            st = state[hd]
            for c in range(nc):
                st_ref[hd, c] = st
                st = st * e_last[hd * nc + c] + grow[hd * nc + c]
            state[hd] = st
        starts = st_ref[...].reshape(PAIR * nc, LANES, LANES)
        out3 = intra + _bmm_nt((q * jnp.exp(b)).astype(BF16), starts.astype(BF16))
        for hd in range(PAIR):
            lanes = slice(LANES * hd, LANES * (hd + 1))
            out = out3[hd * nc:(hd + 1) * nc].reshape(tb, LANES)
            oraw_ref[:, lanes] = out
            normed = out * _rms_scale(out)
            gate = gates[hd]
            ofin_ref[:, lanes] = ((normed * gn_ref[:, lanes]) * (gate * jax.nn.sigmoid(gate))).astype(BF16)

    width = HEADS * LANES
    return _call(
        body, (proj, proj, proj, proj, *aux_arrays, gain), name=name, grid=(HEADS // PAIR, nb),
        in_specs=proj_spec + aux_specs + [gain_spec],
        out_specs=[pair_t, pair_t, state_spec],
        out_shape=[jax.ShapeDtypeStruct((t, width), BF16), jax.ShapeDtypeStruct((t, width), F32),
                   jax.ShapeDtypeStruct((HEADS, t // CHUNK, LANES, LANES), F32)],
        scratch_shapes=[pltpu.VMEM((PAIR, LANES, LANES), F32)],
        compiler_params=_ARB2, hosted=hosted)


def _attn_bwd(is_ret, proj, aux_arrays, gain, o_raw, states, d_out, name, hosted=None):
    t = proj.shape[0]
    tb = min(t, ATTN_TILE)
    nc = tb // CHUNK
    n_aux = 4 if is_ret else 3
    nblk = t // tb
    nb, proj_spec, aux_specs, gain_spec, pair_t, narrow_t, state_spec = _attn_specs(
        is_ret, t, tb, lambda i: nblk - 1 - i)
    base = 0 if is_ret else HEADS // PAIR
    dout_spec = pl.BlockSpec((tb, PAIR * LANES), lambda p, i: (nblk - 1 - i, base + p))

    def body(*refs):
        qkvg_refs = refs[0:4]
        aux = refs[4:4 + n_aux]
        gn_ref, oraw_ref, st_ref, dfin_ref = refs[4 + n_aux:8 + n_aux]
        dq_ref, dk_ref, dv_ref, dgate_ref, dgn_ref = refs[8 + n_aux:13 + n_aux]
        if is_ret:
            dstate, dafter_ref = refs[13 + n_aux:]
        else:
            dlogit_ref, dba_ref, dstate, dafter_ref = refs[13 + n_aux:]

        @pl.when(pl.program_id(1) == 0)
        def _():
            dstate[...] = jnp.zeros_like(dstate)
            dgn_ref[...] = jnp.zeros_like(dgn_ref)
            if not is_ret:
                dba_ref[...] = jnp.zeros_like(dba_ref)

        shape3 = (nc, CHUNK, LANES)
        q, k, v, gates, b, extra, ri, ci = _tile_inputs(is_ret, qkvg_refs, aux, nc)
        eb = jnp.exp(b)
        qe = q * eb
        b_last = b[:, CHUNK - 1:CHUNK, :]
        e_last = jnp.exp(b_last)
        ekd = jnp.exp(b_last - b)
        kd = k * ekd

        d_os = []
        for hd in range(PAIR):
            lanes = slice(LANES * hd, LANES * (hd + 1))
            gn, gate = gn_ref[:, lanes], gates[hd]
            out = oraw_ref[:, lanes]
            r = _rms_scale(out)
            normed = out * r
            sg = jax.nn.sigmoid(gate)
            dfin = dfin_ref[:, lanes]
            dgate_ref[:, lanes] = (dfin * (normed * gn) * _silu_grad(gate, sg)).astype(BF16)
            dpre = dfin * (gate * sg)
            dgn_ref[:, lanes] += jnp.sum(dpre * normed, axis=0, keepdims=True)
            dnormed = dpre * gn
            d_o = r * (dnormed - normed * jnp.mean(dnormed * normed, axis=-1, keepdims=True))
            d_os.append(d_o.reshape(shape3))
        dob, vb = jnp.concatenate(d_os, axis=0).astype(BF16), v.astype(BF16)

        dgrow = _bmm_tn(dob, qe.astype(BF16))
        for hd in range(PAIR):
            dst = dstate[hd]
            for c in reversed(range(nc)):
                dafter_ref[hd * nc + c] = dst
                dst = dst * e_last[hd * nc + c] + dgrow[hd * nc + c]
            dstate[hd] = dst
        st = st_ref[...].reshape(PAIR * nc, LANES, LANES)
        dafter = dafter_ref[...]
        stb, dafter_b = st.astype(BF16), dafter.astype(BF16)

        dsc = _bmm_nt(dob, vb)
        dsc_t = _bmm_nt(vb, dob)
        dqe = _bmm(dob, stb)
        dkd = _bmm(vb, dafter_b)
        if is_ret:
            decay, qb, kb = jnp.concatenate(extra, axis=0), q.astype(BF16), k.astype(BF16)
            scores_t = _bmm_nt(kb, qb) * decay
            dq = _bmm((dsc * decay).astype(BF16), kb) + dqe * eb
            dk = _bmm((dsc_t * decay).astype(BF16), qb) + dkd * ekd
        else:
            _, ep, en, qt, kt, qh, kh = _tile_scores(q, k, b, ri, ci)
            qtb, ktb, qhb, khb = qt.astype(BF16), kt.astype(BF16), qh.astype(BF16), kh.astype(BF16)
            scores_t = jnp.where(ci >= ri, _bmm_nt(ktb, qtb), _bmm_nt(khb, qhb))
            dqt = _bmm(jnp.where(ci <= ri, dsc, 0.0).astype(BF16), ktb)
            dqh = _bmm(jnp.where(ci <= ri, 0.0, dsc).astype(BF16), khb)
            dkt = _bmm(jnp.where(ci >= ri, dsc_t, 0.0).astype(BF16), qtb)
            dkh = _bmm(jnp.where(ci >= ri, 0.0, dsc_t).astype(BF16), qhb)
            dq = dqt * ep + dqh * en + dqe * eb
            dk = dkt * en + dkh * ep + dkd * ekd
        dv = _bmm(scores_t.astype(BF16), dob) + _bmm_nt(kd.astype(BF16), dafter_b)

        if not is_ret:
            db = dqt * qt - dkt * kt - dqh * qh + dkh * kh + dqe * qe - dkd * kd
            db_last = (jnp.sum(dkd * kd, axis=1, keepdims=True)
                       + jnp.sum(dafter * st, axis=1, keepdims=True) * e_last)
            last_row = lax.broadcasted_iota(jnp.int32, (PAIR * nc, CHUNK, LANES), 1) == CHUNK - 1
            db = db + jnp.where(last_row, db_last, 0.0)
            dla = _masked_sum((ci >= ri).astype(BF16), db)

        dq_pair, dk_pair = [], []
        for hd in range(PAIR):
            lanes = slice(LANES * hd, LANES * (hd + 1))
            rows3 = slice(hd * nc, (hd + 1) * nc)
            dq_h, dk_h = dq[rows3].reshape(tb, LANES), dk[rows3].reshape(tb, LANES)
            if is_ret:
                cos_ref, sa_ref, sb_ref, _ = aux
                cos, sa, sb = cos_ref[...], sa_ref[...], sb_ref[...]
                dq_h = _rot_t(dq_h, cos, sa, sb)
                dk_h = _rot_t(dk_h, cos, sa, sb) * QK_SCALE
            else:
                dq_h = dq_h * QK_SCALE
                dlogit = dla[rows3].reshape(tb, LANES) * (1.0 / GATE_NORM) * jax.nn.sigmoid(-extra[hd])
                dlogit_ref[:, lanes] = dlogit.astype(BF16)
                dba_ref[:, lanes] += jnp.sum(dlogit, axis=0, keepdims=True)
            dq_pair.append(dq_h)
            dk_pair.append(dk_h)
            dv_ref[:, lanes] = dv[rows3].reshape(tb, LANES).astype(BF16)
        dq_ref[...] = (dq_pair[0] + pltpu.roll(dq_pair[1], 64, 1)).astype(BF16)
        dk_ref[...] = (dk_pair[0] + pltpu.roll(dk_pair[1], 64, 1)).astype(BF16)

    width = HEADS * LANES
    row_out = pl.BlockSpec((1, PAIR * LANES), lambda p, i: (0, p))
    out_specs = [narrow_t, narrow_t, pair_t, pair_t, row_out]
    out_shape = ([jax.ShapeDtypeStruct((t, width // 2), BF16)] * 2 + [jax.ShapeDtypeStruct((t, width), BF16)] * 2
                 + [jax.ShapeDtypeStruct((1, width), F32)])
    if not is_ret:
        out_specs += [pair_t, row_out]
        out_shape += [jax.ShapeDtypeStruct((t, width), BF16), jax.ShapeDtypeStruct((1, width), F32)]
    return _call(
        body, (proj, proj, proj, proj, *aux_arrays, gain, o_raw, states, d_out), name=name,
        grid=(HEADS // PAIR, nblk),
        in_specs=proj_spec + aux_specs + [gain_spec, pair_t, state_spec, dout_spec],
        out_specs=out_specs, out_shape=out_shape,
        scratch_shapes=[pltpu.VMEM((PAIR, LANES, LANES), F32), pltpu.VMEM((PAIR * nc, LANES, LANES), F32)],
        compiler_params=_ARB2, hosted=hosted)


PEER_SLOT = (2, 1, 3)


def _place():
    x, y, c = lax.axis_index("x"), lax.axis_index("y"), lax.axis_index("c")
    chips = [(1 - x, y), (x, 1 - y), (1 - x, 1 - y)]
    return x, y, c, 2 * x + y, chips


def _route_split(rows, dtype):
    tile = 16 if dtype == BF16 else 8
    if rows < 2 * tile:
        return None
    return -(-(rows // 2) // tile) * tile


def _routes(by_peer):
    x, y, c, me, chips = _place()
    (xx, xy), (yx, yy), (dx, dy) = chips
    if by_peer:
        slots = dict(own=0, from_x=PEER_SLOT[0], from_y=PEER_SLOT[1], diag=PEER_SLOT[2],
                     mine_on_x=PEER_SLOT[0], mine_on_y=PEER_SLOT[1])
    else:
        slots = dict(own=me, from_x=2 * xx + xy, from_y=2 * yx + yy, diag=2 * dx + dy, mine_on_x=me, mine_on_y=me)
    return c, (xx, xy, c), (yx, yy, c), (dx, dy, c), (x, y, 1 - c), slots


def _gather_legs(src, out, send_sems, recv_sems, base, by_peer):
    c, to_x, to_y, to_d, sibling, s = _routes(by_peer)
    r0 = _route_split(src.shape[2], src.dtype)

    def cp(k, src_ref, dst_ref, to):
        return pltpu.make_async_remote_copy(src_ref=src_ref, dst_ref=dst_ref, send_sem=send_sems.at[base + k],
                                            recv_sem=recv_sems.at[base + k], device_id=to, device_id_type=MESH)

    mine = src.at[:, c]
    legs = dict(
        x=(cp(0, mine, out.at[s["mine_on_x"], :, c], to_x), cp(0, mine, out.at[s["from_x"], :, c], to_x)),
        y=(cp(1, mine, out.at[s["mine_on_y"], :, c], to_y), cp(1, mine, out.at[s["from_y"], :, c], to_y)),
        pass_x=(cp(4, out.at[s["from_x"], :, c], out.at[s["from_x"], :, c], sibling),
                cp(4, mine, out.at[s["from_x"], :, 1 - c], sibling)),
        pass_y=(cp(5, out.at[s["from_y"], :, c], out.at[s["from_y"], :, c], sibling),
                cp(5, mine, out.at[s["from_y"], :, 1 - c], sibling)),
        own=(cp(8, src, out.at[s["own"]], sibling), cp(8, src, out.at[s["own"]], sibling)))
    if r0 is None:
        mine_on_d = s["diag"] if by_peer else s["own"]
        legs["d"] = (cp(2, mine, out.at[mine_on_d, :, c], to_d), cp(2, mine, out.at[s["diag"], :, c], to_d))
        legs["pass_d"] = (cp(6, out.at[s["diag"], :, c], out.at[s["diag"], :, c], sibling),
                          cp(6, mine, out.at[s["diag"], :, 1 - c], sibling))
        return legs, False
    lo, hi = pl.ds(0, r0), pl.ds(r0, src.shape[2] - r0)
    fx_on_y = s["diag"] if by_peer else s["from_x"]
    fy_on_x = s["diag"] if by_peer else s["from_y"]
    legs.update(
        fwd_y=(cp(2, out.at[s["from_x"], :, c, lo], out.at[fx_on_y, :, c, lo], to_y),
               cp(2, mine.at[:, lo], out.at[s["diag"], :, c, lo], to_y)),
        fwd_x=(cp(3, out.at[s["from_y"], :, c, hi], out.at[fy_on_x, :, c, hi], to_x),
               cp(3, mine.at[:, hi], out.at[s["diag"], :, c, hi], to_x)),
        pass_d0=(cp(6, out.at[s["diag"], :, c, lo], out.at[s["diag"], :, c, lo], sibling),
                 cp(6, mine.at[:, lo], out.at[s["diag"], :, 1 - c, lo], sibling)),
        pass_d1=(cp(7, out.at[s["diag"], :, c, hi], out.at[s["diag"], :, c, hi], sibling),
                 cp(7, mine.at[:, hi], out.at[s["diag"], :, 1 - c, hi], sibling)))
    return legs, True


def _gather_steps(legs, routed):
    def start():
        legs["x"][0].start()
        legs["y"][0].start()
        legs["own"][0].start()
        if not routed:
            legs["d"][0].start()

    def middle():
        legs["x"][1].wait_recv()
        if routed:
            legs["fwd_y"][0].start()
        legs["pass_x"][0].start()
        legs["y"][1].wait_recv()
        if routed:
            legs["fwd_x"][0].start()
        legs["pass_y"][0].start()

    def finish():
        last = ["pass_d0", "pass_d1"] if routed else ["pass_d"]
        if routed:
            legs["fwd_y"][1].wait_recv()
            legs["pass_d0"][0].start()
            legs["fwd_x"][1].wait_recv()
            legs["pass_d1"][0].start()
        else:
            legs["d"][1].wait_recv()
            legs["pass_d"][0].start()
        for name in ["own", "pass_x", "pass_y"] + last:
            legs[name][1].wait_recv()
        for name in ["x", "y", "own", "pass_x", "pass_y"] + last + (["fwd_y", "fwd_x"] if routed else ["d"]):
            legs[name][0].wait_send()

    return start, middle, finish


def _gather_plan(arrs):
    na = len(arrs)

    def steps(ins, outs, send_sems, recv_sems):
        return [_gather_steps(*_gather_legs(ins[a], outs[a], send_sems, recv_sems, 9 * a, False)) for a in range(na)]

    def run(which):
        def hook(*refs):
            for step in steps(*refs):
                step[which]()
        return hook

    routed = all(_route_split(a.shape[2], a.dtype) is not None for a in arrs)
    return _Hosted(arrs, [jax.ShapeDtypeStruct((N_CHIPS,) + a.shape, a.dtype) for a in arrs], 9 * na,
                   run(0), run(2), middle=run(1), peers="neighbours" if routed else "chips_sibling")


def _pair_exchange_plan(grads):
    na = len(grads)

    def copies(ins, outs, send_sems, recv_sems):
        x, y, c, _, _ = _place()
        return [pltpu.make_async_remote_copy(
            src_ref=ins[a].at[:, 1 - c], dst_ref=outs[a], send_sem=send_sems.at[a], recv_sem=recv_sems.at[a],
            device_id=(x, y, 1 - c), device_id_type=MESH) for a in range(na)]

    def start(*refs):
        for cp in copies(*refs):
            cp.start()

    def finish(*refs):
        for cp in copies(*refs):
            cp.wait()

    return _Hosted(grads, [jax.ShapeDtypeStruct(g.shape[:1] + g.shape[2:], g.dtype) for g in grads], na, start, finish,
                   peers="sibling")


def _small_gather_plan(block):
    def copies(ins, outs, send_sems, recv_sems):
        x, y, c, _, chips = _place()
        peers = [(x, y, 1 - c)] + [(px, py, pc) for px, py in chips for pc in (c, 1 - c)]
        sends = [pltpu.make_async_remote_copy(
            src_ref=ins[0], dst_ref=outs[0].at[4 * x + 2 * y + c], send_sem=send_sems.at[k], recv_sem=recv_sems.at[k],
            device_id=peer, device_id_type=MESH) for k, peer in enumerate(peers)]
        recvs = [pltpu.make_async_remote_copy(
            src_ref=ins[0], dst_ref=outs[0].at[4 * px + 2 * py + pc], send_sem=send_sems.at[k], recv_sem=recv_sems.at[k],
            device_id=(px, py, pc), device_id_type=MESH) for k, (px, py, pc) in enumerate(peers)]
        return sends, recvs

    def start(*refs):
        for cp in copies(*refs)[0]:
            cp.start()

    def finish(*refs):
        sends, recvs = copies(*refs)
        for cp in recvs:
            cp.wait_recv()
        for cp in sends:
            cp.wait_send()

    return _Hosted([block], [jax.ShapeDtypeStruct((8,) + block.shape, block.dtype)], 7, start, finish)


def _sum_devices(blocks):
    def body(b_ref, o_ref):
        acc = b_ref[0]
        for d in range(1, 8):
            acc = acc + b_ref[d]
        o_ref[...] = acc

    return pl.pallas_call(body, name="sum_devices", in_specs=[_VMEM], out_specs=_VMEM,
                          out_shape=jax.ShapeDtypeStruct(blocks.shape[1:], blocks.dtype))(blocks)


def _pair_add(grad, recv, c_arr, name):
    _, _, r, cols = grad.shape

    def body(c_ref, g_ref, r_ref, o_ref):
        o_ref[...] = (g_ref[...].astype(F32) + r_ref[...].astype(F32)).astype(BF16)

    return pl.pallas_call(
        body, name=name,
        grid_spec=pltpu.PrefetchScalarGridSpec(
            num_scalar_prefetch=1, grid=(N_CHIPS,),
            in_specs=[pl.BlockSpec((None, None, r, cols), lambda p, c_ref: (p, c_ref[0], 0, 0)),
                      pl.BlockSpec((None, r, cols), lambda p, c_ref: (p, 0, 0))],
            out_specs=pl.BlockSpec((None, r, cols), lambda p, c_ref: (p, 0, 0))),
        out_shape=jax.ShapeDtypeStruct((N_CHIPS, r, cols), BF16),
        compiler_params=_ARB1,
    )(c_arr, grad, recv)


def _chip_exchange_plan(sums, by_peer=False):
    na = len(sums)

    def copies(ins, outs, send_sems, recv_sems):
        x, y, c, me, chips = _place()

        def copy(a, j, px, py, block, slot):
            return pltpu.make_async_remote_copy(
                src_ref=ins[a].at[block], dst_ref=outs[a].at[slot],
                send_sem=send_sems.at[3 * a + j], recv_sem=recv_sems.at[3 * a + j],
                device_id=(px, py, c), device_id_type=MESH)

        peers = [(a, j, px, py) for a in range(na) for j, (px, py) in enumerate(chips)]
        return me, peers, copy

    def start(*refs):
        me, peers, copy = copies(*refs)
        for a, j, px, py in peers:
            if by_peer:
                copy(a, j, px, py, PEER_SLOT[j], PEER_SLOT[j]).start()
            else:
                copy(a, j, px, py, 2 * px + py, me).start()

    def finish(*refs):
        me, peers, copy = copies(*refs)
        for a, j, px, py in peers:
            if by_peer:
                copy(a, j, px, py, PEER_SLOT[j], PEER_SLOT[j]).wait_recv()
            else:
                copy(a, j, px, py, me, 2 * px + py).wait_recv()
        for a, j, px, py in peers:
            if by_peer:
                copy(a, j, px, py, PEER_SLOT[j], PEER_SLOT[j]).wait_send()
            else:
                copy(a, j, px, py, 2 * px + py, me).wait_send()

    return _Hosted(sums, [jax.ShapeDtypeStruct(s.shape, s.dtype) for s in sums], 3 * na, start, finish, peers="chips")


def _chip_sum(own, recv, me_arr, name):
    _, r, cols = recv.shape

    def body(me_ref, own_ref, r_ref, o_ref):
        o_ref[...] = jnp.zeros_like(o_ref)
        for q in range(N_CHIPS):
            @pl.when(me_ref[0] == q)
            def _():
                o_ref[...] += own_ref[...].astype(F32)

            @pl.when(me_ref[0] != q)
            def _():
                o_ref[...] += r_ref[q].astype(F32)

    return pl.pallas_call(
        body, name=name,
        grid_spec=pltpu.PrefetchScalarGridSpec(
            num_scalar_prefetch=1, grid=(1,),
            in_specs=[pl.BlockSpec((None, r, cols), lambda i, me_ref: (me_ref[0], 0, 0)),
                      pl.BlockSpec((N_CHIPS, r, cols), lambda i, me_ref: (0, 0, 0))],
            out_specs=pl.BlockSpec((r, cols), lambda i, me_ref: (0, 0))),
        out_shape=jax.ShapeDtypeStruct((r, cols), F32),
        compiler_params=_ARB1,
    )(me_arr, own, recv)


def _peer_sum(own, recv, name):
    _, r, cols = recv.shape

    def body(own_ref, r_ref, o_ref):
        acc = own_ref[...].astype(F32) + r_ref[1].astype(F32)
        acc = acc + r_ref[2].astype(F32)
        o_ref[...] = acc + r_ref[3].astype(F32)

    return pl.pallas_call(
        body, name=name, grid=(1,),
        in_specs=[pl.BlockSpec((None, r, cols), lambda i: (0, 0, 0)), pl.BlockSpec((N_CHIPS, r, cols), lambda i: (0, 0, 0))],
        out_specs=pl.BlockSpec((r, cols), lambda i: (0, 0)),
        out_shape=jax.ShapeDtypeStruct((r, cols), F32),
        compiler_params=_ARB1,
    )(own, recv)


def _pair_share_plan(halves):
    na = len(halves)

    def copies(ins, outs, send_sems, recv_sems):
        x, y, c, _, _ = _place()
        return [pltpu.make_async_remote_copy(
            src_ref=ins[a], dst_ref=outs[a], send_sem=send_sems.at[a], recv_sem=recv_sems.at[a],
            device_id=(x, y, 1 - c), device_id_type=MESH) for a in range(na)]

    def start(*refs):
        for cp in copies(*refs):
            cp.start()

    def finish(*refs):
        for cp in copies(*refs):
            cp.wait()

    return _Hosted(halves, [jax.ShapeDtypeStruct(h.shape, h.dtype) for h in halves], na, start, finish,
                   peers="sibling")


def _row_tile(rows):
    best = rows
    for cand in range(8, min(rows, 512) + 1, 8):
        if rows % cand == 0:
            best = cand
    return best


def _adamw_math(w, g, m, v):
    m2 = ADAM_B1 * m + (1.0 - ADAM_B1) * g
    v2 = ADAM_B2 * v + (1.0 - ADAM_B2) * (g * g)
    m_hat = m2 / (1.0 - ADAM_B1 ** ADAM_STEP)
    v_hat = v2 / (1.0 - ADAM_B2 ** ADAM_STEP)
    return -ADAM_LR * (m_hat / (jnp.sqrt(v_hat) + ADAM_EPS) + ADAM_WD * w), m2, v2


def _adamw_halves(w, g_mine, g_other, m, v, c_arr, name):
    rows, cols = w.shape
    r = rows // 2
    tr = _row_tile(r)
    nt = r // tr

    def body(c_ref, w_ref, gm_ref, go_ref, m_ref, v_ref, g_ref, d_ref, nm_ref, nv_ref):
        gv = jnp.where(pl.program_id(0) == c_ref[0], gm_ref[...], go_ref[...])
        g_ref[...] = gv
        d_ref[...], nm_ref[...], nv_ref[...] = _adamw_math(w_ref[...], gv, m_ref[...], v_ref[...])

    full = pl.BlockSpec((tr, cols), lambda h, i, c_ref: (h * nt + i, 0))
    half = pl.BlockSpec((tr, cols), lambda h, i, c_ref: (i, 0))
    shape = jax.ShapeDtypeStruct((rows, cols), F32)
    return pl.pallas_call(
        body, name=name,
        grid_spec=pltpu.PrefetchScalarGridSpec(
            num_scalar_prefetch=1, grid=(2, nt),
            in_specs=[full, half, half, full, full], out_specs=[full] * 4),
        out_shape=[shape] * 4,
        compiler_params=_ARB2,
    )(c_arr, w, g_mine, g_other, m, v)


def _pad_w_in_t(w_in_t):
    return jnp.pad(w_in_t, ((0, PROJ_P - IN_WIDTH), (0, 0)))


def _unpad_w_in_t(w_pt):
    return w_pt[0:IN_WIDTH]


def _rope_tables(t):
    half = 32
    inv = ROPE_BASE ** (-jnp.arange(half, dtype=F32) * 2.0 / 64)
    ang = jnp.arange(t, dtype=F32)[:, None] * inv[None, :]
    cos, sin = jnp.cos(ang), jnp.sin(ang)
    z32, z64 = jnp.zeros((t, 32), F32), jnp.zeros((t, 64), F32)
    return (jnp.concatenate([cos, cos, z64], axis=1),
            jnp.concatenate([-sin, z32, z64], axis=1),
            jnp.concatenate([z32, sin, z64], axis=1))


def _halves(w):
    n, rows, cols = w.shape
    return w.reshape(n, 2, rows // 2, cols)


_VMEM = pl.BlockSpec(memory_space=pltpu.VMEM)


def _pack_small(n1, nm, n2, nf, nret, ngla, ba, wa2_p, loss_blk):
    def body(n1_ref, nm_ref, n2_ref, nf_ref, nret_ref, ngla_ref, ba_ref, wa2_ref, loss_ref, o_ref):
        o_ref[...] = jnp.zeros_like(o_ref)
        o_ref[0:1, :] = n1_ref[...]
        o_ref[1:2, :] = nm_ref[...]
        o_ref[2:3, :] = n2_ref[...]
        o_ref[3:4, :] = nf_ref[...]
        o_ref[4:5, 0:512] = nret_ref[...]
        o_ref[4:5, 512:1024] = ngla_ref[...]
        o_ref[5:6, 0:256] = ba_ref[...]
        o_ref[6:7, 0:LANES] = loss_ref[0:1, :]
        o_ref[8:8 + GATE_RANK, 0:HEADS * LANES] = wa2_ref[0:GATE_RANK, :]

    return pl.pallas_call(
        body, name="pack_small", in_specs=[_VMEM] * 9, out_specs=_VMEM,
        out_shape=jax.ShapeDtypeStruct((SMALL_ROWS, D_MODEL), F32),
    )(n1, nm, n2, nf, nret, ngla, ba, wa2_p, loss_blk)


def _small_update(summed, chip_arr, ws, ms, vs):
    n = len(ws)

    def body(chip_ref, s_ref, *refs):
        w_refs, m_refs, v_refs = refs[0:n], refs[n:2 * n], refs[2 * n:3 * n]
        outs = refs[3 * n:]
        wa2_all = s_ref[8:8 + GATE_RANK, 0:HEADS * LANES]
        wa2_g = jnp.zeros((GATE_RANK, 64), F32)
        for p in range(N_CHIPS):
            wa2_g = jnp.where(chip_ref[0] == p, wa2_all[:, LANES * p:LANES * p + 64], wa2_g)
        grads = [s_ref[0:1, :], s_ref[1:2, :], s_ref[2:3, :], s_ref[3:4, :], s_ref[4:5, 0:512],
                 s_ref[4:5, 512:1024], s_ref[5:6, 0:256], wa2_g]
        for k in range(n):
            d, m2, v2 = _adamw_math(w_refs[k][...], grads[k], m_refs[k][...], v_refs[k][...])
            outs[k][...] = grads[k]
            outs[n + k][...] = d
            outs[2 * n + k][...] = m2
            outs[3 * n + k][...] = v2

    shapes = [jax.ShapeDtypeStruct(w.shape, F32) for w in ws] * 4
    smem = pl.BlockSpec(memory_space=pltpu.SMEM)
    outs = pl.pallas_call(
        body, name="small_update", in_specs=[smem] + [_VMEM] * (1 + 3 * n), out_specs=[_VMEM] * (4 * n),
        out_shape=shapes,
    )(chip_arr, summed, *ws, *ms, *vs)
    return outs[0:n], outs[n:2 * n], outs[2 * n:3 * n], outs[3 * n:4 * n]


def _pad_in_rows(w_t):
    return jnp.pad(w_t, ((0, IN_ROWS - IN_SHARD), (0, 0)))


def _forward_backward(xs, target, ffn1_w, rest, ba_p, ffn1_norm_g, mix_norm_g, ret_norm_g, gla_norm_g, ffn2_norm_g,
                      final_norm_g, ffn1_gather=None, rest_plan=None, rest_weights=None, ffn2_plans=None,
                      ffn2_weights=None, ffn2_pairs=None, ffn2_pairs_done=None, early=None, late=None, small_plan=None):
    t = xs.shape[0]
    cos_t, sa_t, sb_t = _rope_tables(t)
    log_gamma = jnp.log(1.0 - 2.0 ** (-5.0 - jnp.arange(HEADS, dtype=F32)))
    lg_t = jnp.broadcast_to(log_gamma[:, None, None], (HEADS, 1, LANES))
    ret_aux = [cos_t, sa_t, sb_t, lg_t]

    if ffn1_gather is None:
        (x1, a1, u1, h1), gathered = _ffn_fwd(xs, ffn1_norm_g, ffn1_w, "ffn1_fwd", hosted=rest_plan)
    else:
        ffn1_shard, ffn1_weights = ffn1_gather
        (x1, a1, u1, h1, wall), gathered = _ffn1_fwd_gathering(xs, ffn1_norm_g, ffn1_shard, "ffn1_fwd",
                                                               hosted=rest_plan)
        ffn1_w = ffn1_weights(wall)
    ffn2_w, w_in_pt, w_out_full, wa2_p = rest if rest_plan is None else rest_weights(gathered)
    plans = [None] * 3 if ffn2_plans is None else ffn2_plans
    (proj, h_mix), got_gate = _mixer_in_fwd(x1, mix_norm_g, w_in_pt, "mixer_in_fwd", hosted=plans[0])
    gla_aux = [proj, wa2_p, ba_p]
    (o_ret, raw_ret, st_ret), got_up = _attn_fwd(True, proj, ret_aux, ret_norm_g, "ret_fwd", hosted=plans[1])
    (o_gla, raw_gla, st_gla), got_down = _attn_fwd(False, proj, gla_aux, gla_norm_g, "gla_fwd", hosted=plans[2])
    if ffn2_plans is not None:
        ffn2_w = ffn2_weights(ffn2_w, got_gate + got_up + got_down)
    x2 = _mixer_out_fwd(o_ret, o_gla, w_out_full, x1, "mixer_out_fwd")
    (loss_blk, dx3, d_final_g, a2, u2, h2), _ = _ffn_fwd(x2, ffn2_norm_g, ffn2_w, "ffn2_fwd",
                                                       loss_head=(final_norm_g, target))

    (da2, du2, hid2, dob2, dx2, d_ffn2_g, d_o), _ = _ffn_bwd(dx3, x2, ffn2_norm_g, a2, u2, ffn2_w, "ffn2_bwd",
                                                            back_w=w_out_full)
    g_gate2 = _matmul_tn(da2, h2, "ffn2_dgate", out_dtype=BF16)
    g_up2 = _matmul_tn(du2, h2, "ffn2_dup", out_dtype=BF16)
    g_down2 = _matmul_tn(hid2, dob2, "ffn2_ddown", out_dtype=BF16)

    g_wout_ret = _matmul_tn(o_ret, dx2, "wout_grad_ret", out_dtype=BF16)
    g_wout_gla = _matmul_tn(o_gla, dx2, "wout_grad_gla", out_dtype=BF16)
    pairs_plan = None if ffn2_pairs is None else ffn2_pairs([g_gate2, g_up2, g_down2])
    (*dproj_ret, d_ret_g), pair_recv = _attn_bwd(True, proj, ret_aux, ret_norm_g, raw_ret, st_ret, d_o, "ret_bwd",
                                                 hosted=pairs_plan)
    if ffn2_pairs is not None:
        ffn2_pairs_done(pair_recv)
    (*dproj_gla, d_gla_g, dlogit, d_ba_p), _ = _attn_bwd(False, proj, gla_aux, gla_norm_g, raw_gla, st_gla, d_o,
                                                        "gla_bwd")
    d_glow = _matmul_nt(dlogit, wa2_p, "gate_low_bwd", out_dtype=BF16)
    g_wa2_p = _matmul_tn(proj[:, PROJ_P - LANES:], dlogit, "gate_w_grad")
    dproj = jnp.concatenate(dproj_ret + dproj_gla + [d_glow], axis=1)
    g_win_p = _matmul_tn(dproj, h_mix, "w_in_grad", tka=PROJ_P // PROJ_TILES, out_dtype=BF16)
    dx1, d_mix_g = _mixer_in_bwd(dproj, w_in_pt, dx2, x1, mix_norm_g, "mixer_in_bwd")
    g_win_t = _unpad_w_in_t(g_win_p[0])
    g_win = jnp.stack([_pad_in_rows(g_win_t[IN_SHARD * p:IN_SHARD * (p + 1)]) for p in range(N_CHIPS)], axis=0)
    g_wout = jnp.concatenate([g_wout_ret[0], g_wout_gla[0]], axis=0).reshape(N_CHIPS, D_MODEL // N_CHIPS, D_MODEL)

    early_grads = [g_win, g_wout] if ffn2_pairs is not None else [g_gate2, g_up2, g_down2, g_win, g_wout]
    early_plan = None if early is None else early(early_grads)
    (da1, du1, hid1, dob1, grad_x, d_ffn1_g), arrived = _ffn_bwd(dx1, xs, ffn1_norm_g, a1, u1, ffn1_w, "ffn1_bwd",
                                                                hosted=early_plan)
    d_ba = d_ba_p.reshape(HEADS, LANES)[:, 0:64].reshape(1, 256)
    small_local = _pack_small(d_ffn1_g, d_mix_g, d_ffn2_g, d_final_g, d_ret_g, d_gla_g, d_ba, g_wa2_p[0], loss_blk)
    late_grads, late_arrived = [], []
    for lhs, rhs, name in ((da1, h1, "ffn1_dgate"), (du1, h1, "ffn1_dup"), (hid1, dob1, "ffn1_ddown")):
        if late is None:
            plan = None
        else:
            plan = late(late_grads[-1], len(late_grads)) if late_grads else small_plan(small_local)
        res = _matmul_tn(lhs, rhs, name, out_dtype=BF16, hosted=plan)
        if plan is not None:
            res, carried = res
            late_arrived += carried
        late_grads.append(res)
    g_gate1, g_up1, g_down1 = late_grads

    return (small_local, grad_x, g_gate1, g_up1, g_down1, g_gate2, g_up2, g_down2, g_win, g_wout, g_wa2_p,
            d_ba_p, d_ffn1_g, d_mix_g, d_ffn2_g, d_final_g, d_ret_g, d_gla_g, arrived, late_arrived)


def kernel(x, ffn1_norm_g, ffn1_w_gate, ffn1_w_up, ffn1_w_down, mix_norm_g, w_in, ret_norm_g, gla_w_a2, gla_b_a, gla_norm_g, w_out, ffn2_norm_g, ffn2_w_gate, ffn2_w_up, ffn2_w_down, final_norm_g, loss_target, m_ffn1_norm_g, m_ffn1_w_gate, m_ffn1_w_up, m_ffn1_w_down, m_mix_norm_g, m_w_in, m_ret_norm_g, m_gla_w_a2, m_gla_b_a, m_gla_norm_g, m_w_out, m_ffn2_norm_g, m_ffn2_w_gate, m_ffn2_w_up, m_ffn2_w_down, m_final_norm_g, v_ffn1_norm_g, v_ffn1_w_gate, v_ffn1_w_up, v_ffn1_w_down, v_mix_norm_g, v_w_in, v_ret_norm_g, v_gla_w_a2, v_gla_b_a, v_gla_norm_g, v_w_out, v_ffn2_norm_g, v_ffn2_w_gate, v_ffn2_w_up, v_ffn2_w_down, v_final_norm_g):
    t = x.shape[1]
    xs = x.reshape(t, D_MODEL)
    target = loss_target.reshape(t, D_MODEL)
    chip = 2 * lax.axis_index("x") + lax.axis_index("y")
    c_arr = lax.axis_index("c").astype(jnp.int32).reshape(1)

    me_arr = chip.astype(jnp.int32).reshape(1)

    pad_rows = _pad_in_rows

    ffn1_shard = _halves(jnp.stack([ffn1_w_gate[0].T, ffn1_w_up[0].T, ffn1_w_down[0]], axis=0).astype(BF16))
    rest_shards = [_halves(pad_rows(w_in[0].T).astype(BF16)[None]),
                   _halves(w_out.astype(BF16)),
                   jnp.concatenate([gla_w_a2.reshape(GATE_RANK, 64), jnp.zeros((GATE_RANK, 64), F32)],
                                   axis=1).reshape(1, 2, 8, LANES)]
    ffn2_shards = [_halves(w.astype(BF16)[None]) for w in (ffn2_w_gate[0].T, ffn2_w_up[0].T, ffn2_w_down[0])]
    rest_shards.append(ffn2_shards[0])

    def ffn1_weights(gathered):
        return gathered.reshape(N_CHIPS, 3, FF_SHARD, D_MODEL)

    def rest_weights(gathered):
        win_all, wout_all, wa2_all, ffn2_gate = gathered
        win_t = win_all.reshape(N_CHIPS, IN_ROWS, D_MODEL)
        w_in_pt = jnp.zeros((PROJ_P, D_MODEL), BF16)
        for p in range(N_CHIPS):
            w_in_pt = lax.dynamic_update_slice(w_in_pt, win_t[p, 0:IN_SHARD], (IN_SHARD * p, 0))
        wa2_p = jnp.pad(
            wa2_all.reshape(N_CHIPS, GATE_RANK, LANES).transpose(1, 0, 2).reshape(GATE_RANK, HEADS * LANES),
            ((0, LANES - GATE_RANK), (0, 0))).astype(BF16)
        return ([ffn2_gate], w_in_pt, wout_all.reshape(D_MODEL, D_MODEL), wa2_p)

    def ffn2_weights(early_part, gathered):
        return [g.reshape(N_CHIPS, FF_SHARD, D_MODEL) for g in early_part + gathered]

    def by_halves(g):
        return g.reshape(g.shape[0], 2, g.shape[1] // 2, g.shape[2])

    def pair_adds(halves, recv, tag):
        return [_pair_add(g, r, c_arr, "pair_add_%s%d" % (tag, k)) for k, (g, r) in enumerate(zip(halves, recv))]

    def pair_sums(grads, tag):
        halves = [by_halves(g) for g in grads]
        recv = _run_hosted(_pair_exchange_plan(halves), "pair_exchange_" + tag)
        return pair_adds(halves, recv, tag)

    early_sums, ffn2_halves = [], []

    def ffn2_pairs(grads):
        ffn2_halves.extend(by_halves(g) for g in grads)
        return _pair_exchange_plan(ffn2_halves)

    def ffn2_pairs_done(recv):
        early_sums.extend(pair_adds(ffn2_halves, recv, "ffn2_"))

    def early(grads):
        early_sums.extend(pair_sums(grads, "early"))
        return _chip_exchange_plan(early_sums)

    late_sums = []

    def late(grad, number):
        late_sums.extend(pair_sums([grad], "late%d" % number))
        return _chip_exchange_plan(late_sums[-1:], by_peer=True)

    ba_p = jnp.pad(gla_b_a.reshape(HEADS, 64), ((0, 0), (0, 64))).reshape(1, HEADS * LANES)
    fb = _forward_backward(xs, target, None, None, ba_p, ffn1_norm_g, mix_norm_g, ret_norm_g, gla_norm_g,
                           ffn2_norm_g, final_norm_g.reshape(1, D_MODEL), ffn1_gather=(ffn1_shard, ffn1_weights),
                           rest_plan=_gather_plan(rest_shards), rest_weights=rest_weights,
                           ffn2_plans=[_gather_plan(ffn2_shards[1:2]), None, _gather_plan(ffn2_shards[2:3])],
                           ffn2_weights=ffn2_weights,
                           ffn2_pairs=ffn2_pairs, ffn2_pairs_done=ffn2_pairs_done, early=early, late=late,
                           small_plan=_small_gather_plan)
    (small_local, grad_x, _, _, g_down1, _, _, _, _, _, _, _, _, _, _, _, _, _, early_arrived, late_arrived) = fb
    small_all, late_arrived = late_arrived[0], late_arrived[1:]
    late_arrived = late_arrived + _run_hosted(late(g_down1, 3), "chip_exchange_late")
    mine = [_peer_sum(s, r, "chip_sum_%d" % k) for k, (s, r) in enumerate(zip(late_sums, late_arrived))]
    mine += [_chip_sum(s, r, me_arr, "chip_sum_%d" % (3 + k)) for k, (s, r) in enumerate(zip(early_sums, early_arrived))]
    other = _run_hosted(_pair_share_plan(mine), "pair_share")

    device = 2 * chip + lax.axis_index("c")
    small_sum = _sum_devices(lax.dynamic_update_slice(small_all, small_local[None], (device, 0, 0)))
    loss = small_sum[6, 0]

    def rows(n1, nm, n2, nf, nret, ngla, ba, wa2):
        return [n1, nm, n2, nf.reshape(1, D_MODEL), nret, ngla, ba, wa2.reshape(GATE_RANK, 64)]

    small = _small_update(
        small_sum, me_arr,
        rows(ffn1_norm_g, mix_norm_g, ffn2_norm_g, final_norm_g, ret_norm_g, gla_norm_g, gla_b_a, gla_w_a2),
        rows(m_ffn1_norm_g, m_mix_norm_g, m_ffn2_norm_g, m_final_norm_g, m_ret_norm_g, m_gla_norm_g, m_gla_b_a,
             m_gla_w_a2),
        rows(v_ffn1_norm_g, v_mix_norm_g, v_ffn2_norm_g, v_final_norm_g, v_ret_norm_g, v_gla_norm_g, v_gla_b_a,
             v_gla_w_a2))
    s_grad, s_delta, s_m, s_v = [
        [*o[0:3], o[3].reshape(D_MODEL), *o[4:7], o[7].reshape(1, GATE_RANK, 64)] for o in small]

    def big(k, w, m, v, name, to_2d, from_2d):
        outs4 = _adamw_halves(to_2d(w), mine[k], other[k], to_2d(m), to_2d(v), c_arr, name)
        return [from_2d(z) for z in outs4]

    plain = (lambda w: w[0], lambda z: z[None])
    transposed = (lambda w: w[0].T, lambda z: z.T[None])
    in_proj = (lambda w: pad_rows(w[0].T), lambda z: z[0:IN_SHARD].T[None])
    r_g1 = big(0, ffn1_w_gate, m_ffn1_w_gate, v_ffn1_w_gate, "adamw_ffn1_gate", *transposed)
    r_u1 = big(1, ffn1_w_up, m_ffn1_w_up, v_ffn1_w_up, "adamw_ffn1_up", *transposed)
    r_d1 = big(2, ffn1_w_down, m_ffn1_w_down, v_ffn1_w_down, "adamw_ffn1_down", *plain)
    r_g2 = big(3, ffn2_w_gate, m_ffn2_w_gate, v_ffn2_w_gate, "adamw_ffn2_gate", *transposed)
    r_u2 = big(4, ffn2_w_up, m_ffn2_w_up, v_ffn2_w_up, "adamw_ffn2_up", *transposed)
    r_d2 = big(5, ffn2_w_down, m_ffn2_w_down, v_ffn2_w_down, "adamw_ffn2_down", *plain)
    r_in = big(6, w_in, m_w_in, v_w_in, "adamw_w_in", *in_proj)
    r_out = big(7, w_out, m_w_out, v_w_out, "adamw_w_out", *plain)

    def leaves(k, smalls):
        n1, nm, n2, nf, nret, ngla, ba, wa2 = smalls
        return [n1, r_g1[k], r_u1[k], r_d1[k], nm, r_in[k], nret, wa2, ba, ngla, r_out[k], n2, r_g2[k], r_u2[k], r_d2[k], nf]

    outs = [loss, grad_x.reshape(x.shape)]
    outs += leaves(0, s_grad) + leaves(1, s_delta) + leaves(2, s_m) + leaves(3, s_v)
    return tuple(outs)
```

```python
import functools

import jax
import jax.numpy as jnp
from jax import lax
from jax.experimental import pallas as pl
from jax.experimental.pallas import tpu as pltpu

F32, BF16 = jnp.float32, jnp.bfloat16
MESH = pl.DeviceIdType.MESH
ANY = pl.BlockSpec(memory_space=pl.ANY)

D_MODEL = 1024
D_FF = 2816
N_CHIPS = 4
FF_SHARD = D_FF // N_CHIPS
IN_WIDTH = 3088
IN_SHARD = IN_WIDTH // N_CHIPS
IN_ROWS = 800
CHUNK = 64
HEADS = 4
LANES = 128
PROJ_P = 3072 + LANES
PROJ_TILES = 5
GATE_RANK = 16
QK_SCALE = 0.125
GATE_NORM = 16.0
RMS_EPS = 1e-6
ROPE_BASE = 10000.0
ADAM_LR, ADAM_B1, ADAM_B2, ADAM_EPS, ADAM_WD, ADAM_STEP = 0.001, 0.9, 0.999, 1e-08, 0.01, 10
SMALL_ROWS = 32
TOKEN_TILE = 512
ATTN_TILE = 512

_ARB2 = pltpu.CompilerParams(dimension_semantics=("arbitrary", "arbitrary"))
_ARB1 = pltpu.CompilerParams(dimension_semantics=("arbitrary",))
_ARB3 = pltpu.CompilerParams(dimension_semantics=("arbitrary", "arbitrary", "arbitrary"))


def _dot(a, b):
    return jnp.dot(a, b, preferred_element_type=F32)


def _dot_nt(a, b):
    return lax.dot_general(a, b, (((1,), (1,)), ((), ())), preferred_element_type=F32)


def _dot_tn(a, b):
    return lax.dot_general(a, b, (((0,), (0,)), ((), ())), preferred_element_type=F32)


def _rms_scale(xv):
    return lax.rsqrt(jnp.mean(xv * xv, axis=-1, keepdims=True) + RMS_EPS)


def _rms_bwd(dh, xv, g):
    r = _rms_scale(xv)
    xhat = xv * r
    dxhat = dh * g
    dx = r * (dxhat - xhat * jnp.mean(dxhat * xhat, axis=-1, keepdims=True))
    return dx, jnp.sum(dh * xhat, axis=0, keepdims=True)


def _silu_grad(a, sg):
    return sg * (1.0 + a * (1.0 - sg))


class _Hosted:
    def __init__(self, arrays, out_shapes, n_sems, start, finish, middle=None, late=None, peers=None):
        self.arrays, self.out_shapes, self.n_sems = list(arrays), list(out_shapes), n_sems
        self.start, self.finish = start, finish
        self.middle = middle if middle is not None else (lambda *refs: None)
        self.late = late if late is not None else (lambda *refs: None)
        self.peers = peers


PEER_SETS = {
    "sibling": (0, lambda x, y, c: [(x, y, 1 - c)]),
    "chips": (1, lambda x, y, c: [(1 - x, y, c), (x, 1 - y, c), (1 - x, 1 - y, c)]),
    "neighbours": (2, lambda x, y, c: [(1 - x, y, c), (x, 1 - y, c), (x, y, 1 - c)]),
    "chips_sibling": (3, lambda x, y, c: [(1 - x, y, c), (x, 1 - y, c), (1 - x, 1 - y, c), (x, y, 1 - c)]),
}


def _handshake(kind):
    x, y, c, _, _ = _place()
    peers = PEER_SETS[kind][1](x, y, c)
    barrier = pltpu.get_barrier_semaphore()
    for peer in peers:
        pl.semaphore_signal(barrier, inc=1, device_id=peer, device_id_type=MESH)
    pl.semaphore_wait(barrier, len(peers))


def _with_barrier(compiler_params, kind):
    if kind is None:
        return compiler_params
    semantics = None if compiler_params is None else compiler_params.dimension_semantics
    return pltpu.CompilerParams(dimension_semantics=semantics, collective_id=PEER_SETS[kind][0])


def _call(body, args, *, name, grid, in_specs, out_specs, out_shape, scratch_shapes, compiler_params, hosted=None):
    if hosted is None:
        outs = pl.pallas_call(body, name=name, grid=grid, in_specs=in_specs, out_specs=out_specs, out_shape=out_shape,
                              scratch_shapes=scratch_shapes, compiler_params=compiler_params)(*args)
        return list(outs), []
    n_in, n_out, n_sc, nh = len(in_specs), len(out_specs), len(scratch_shapes), len(hosted.arrays)

    def wrapped(*refs):
        ins, h_in = refs[:n_in], refs[n_in:n_in + nh]
        outs, h_out = refs[n_in + nh:n_in + nh + n_out], refs[n_in + nh + n_out:n_in + 2 * nh + n_out]
        rest = refs[n_in + 2 * nh + n_out:]
        scratch, (send_sems, recv_sems) = rest[:n_sc], rest[n_sc:]
        step = functools.reduce(lambda flat, d: flat * grid[d] + pl.program_id(d), range(len(grid)), 0)
        total = functools.reduce(lambda a, b: a * b, grid)

        @pl.when(step == 0)
        def _():
            if hosted.peers is not None:
                _handshake(hosted.peers)
            hosted.start(h_in, h_out, send_sems, recv_sems)

        @pl.when(step == total // 3)
        def _():
            hosted.middle(h_in, h_out, send_sems, recv_sems)

        @pl.when(step == (2 * total) // 3)
        def _():
            hosted.late(h_in, h_out, send_sems, recv_sems)

        body(*ins, *outs, *scratch)
        last = step == total - 1

        @pl.when(last)
        def _():
            hosted.finish(h_in, h_out, send_sems, recv_sems)

    sems = [pltpu.SemaphoreType.DMA((hosted.n_sems,)), pltpu.SemaphoreType.DMA((hosted.n_sems,))]
    outs = pl.pallas_call(
        wrapped, name=name, grid=grid, in_specs=list(in_specs) + [ANY] * nh, out_specs=list(out_specs) + [ANY] * nh,
        out_shape=list(out_shape) + hosted.out_shapes, scratch_shapes=list(scratch_shapes) + sems,
        compiler_params=_with_barrier(compiler_params, hosted.peers))(*args, *hosted.arrays)
    return list(outs[:n_out]), list(outs[n_out:])


def _run_hosted(hosted, name):
    nh = len(hosted.arrays)

    def body(*refs):
        h_in, h_out, (send_sems, recv_sems) = refs[:nh], refs[nh:2 * nh], refs[2 * nh:]
        if hosted.peers is not None:
            _handshake(hosted.peers)
        hosted.start(h_in, h_out, send_sems, recv_sems)
        hosted.middle(h_in, h_out, send_sems, recv_sems)
        hosted.late(h_in, h_out, send_sems, recv_sems)
        hosted.finish(h_in, h_out, send_sems, recv_sems)

    sems = [pltpu.SemaphoreType.DMA((hosted.n_sems,)), pltpu.SemaphoreType.DMA((hosted.n_sems,))]
    return list(pl.pallas_call(body, name=name, in_specs=[ANY] * nh, out_specs=[ANY] * nh, out_shape=hosted.out_shapes,
                               scratch_shapes=sems, compiler_params=_with_barrier(None, hosted.peers))(*hosted.arrays))


def _ffn_weight_operands(ffn_w, chunk_maps):
    if isinstance(ffn_w, (list, tuple)):
        specs = [pl.BlockSpec((None, FF_SHARD, D_MODEL), lambda *g, m=m: (m(*g), 0, 0)) for m in chunk_maps]
        return list(ffn_w), specs
    specs = [pl.BlockSpec((None, None, FF_SHARD, D_MODEL), lambda *g, m=m, k=kind: (m(*g), k, 0, 0))
             for kind, m in enumerate(chunk_maps)]
    return [ffn_w] * 3, specs


def _pipeline_item(steps, lag):
    def item(s):
        it = jnp.clip(s - lag, 0, steps - 1)
        return it // N_CHIPS, it % N_CHIPS

    return item


def _ffn_fwd(x, g, ffn_w, name, hosted=None, loss_head=None):
    t = x.shape[0]
    tm = min(t, TOKEN_TILE)
    n_head = 0 if loss_head is None else 2

    def body(*refs):
        x_ref, g_ref, wg_ref, wu_ref, wd_ref = refs[0:5]
        head_in = refs[5:5 + n_head]
        outs = refs[5 + n_head:-1]
        acc_ref = refs[-1]
        a_ref, u_ref, h_ref = outs[-3:]
        i, j = pl.program_id(0), pl.program_id(1)

        @pl.when(j == 0)
        def _():
            xv = x_ref[...]
            h_ref[...] = ((xv * _rms_scale(xv)) * g_ref[...]).astype(BF16)
            acc_ref[...] = jnp.zeros_like(acc_ref)

        h = h_ref[...]
        a = _dot_nt(h, wg_ref[...])
        u = _dot_nt(h, wu_ref[...])
        a_ref[...] = a.astype(BF16)
        u_ref[...] = u.astype(BF16)
        hid = (a * jax.nn.sigmoid(a)) * u
        acc_ref[...] += _dot(hid.astype(BF16), wd_ref[...])

        if loss_head is None:
            @pl.when(j == N_CHIPS - 1)
            def _():
                outs[0][...] = x_ref[...] + 0.5 * acc_ref[...]
        else:
            gf_ref, t_ref = head_in
            l_ref, dx_ref, dgf_ref = outs[0:3]

            @pl.when((i == 0) & (j == 0))
            def _():
                l_ref[...] = jnp.zeros_like(l_ref)
                dgf_ref[...] = jnp.zeros_like(dgf_ref)

            @pl.when(j == N_CHIPS - 1)
            def _():
                xv = x_ref[...] + 0.5 * acc_ref[...]
                gv = gf_ref[...]
                err = (xv * _rms_scale(xv)) * gv - t_ref[...]
                l_ref[...] += 0.5 * jnp.sum(jnp.mean(err * err, axis=-1, keepdims=True), axis=0, keepdims=True)
                dx, dg = _rms_bwd(err * (1.0 / D_MODEL), xv, gv)
                dx_ref[...] = dx
                dgf_ref[...] += dg

    tok = pl.BlockSpec((tm, D_MODEL), lambda i, j: (i, 0))
    row = pl.BlockSpec((1, D_MODEL), lambda i, j: (0, 0))
    act = pl.BlockSpec((None, tm, FF_SHARD), lambda i, j: (j, i, 0))
    act_shape = jax.ShapeDtypeStruct((N_CHIPS, t, FF_SHARD), BF16)
    w_arrays, weights = _ffn_weight_operands(ffn_w, [lambda i, j: j] * 3)
    if loss_head is None:
        first_specs, first_shapes, head_args, head_specs = [tok], [jax.ShapeDtypeStruct((t, D_MODEL), F32)], [], []
    else:
        first_specs = [pl.BlockSpec((8, LANES), lambda i, j: (0, 0)), tok, row]
        first_shapes = [jax.ShapeDtypeStruct((8, LANES), F32), jax.ShapeDtypeStruct((t, D_MODEL), F32),
                        jax.ShapeDtypeStruct((1, D_MODEL), F32)]
        head_args, head_specs = list(loss_head), [row, tok]
    return _call(
        body, (x, g, *w_arrays, *head_args), name=name, grid=(t // tm, N_CHIPS),
        in_specs=[tok, row] + weights + head_specs,
        out_specs=first_specs + [act, act, tok],
        out_shape=first_shapes + [act_shape, act_shape, jax.ShapeDtypeStruct((t, D_MODEL), BF16)],
        scratch_shapes=[pltpu.VMEM((tm, D_MODEL), F32)],
        compiler_params=_ARB2, hosted=hosted)


def _ffn1_fwd_gathering(x, g, shard, name, hosted=None):
    t = x.shape[0]
    tm = min(t, TOKEN_TILE)
    nt = t // tm
    nh = 0 if hosted is None else len(hosted.arrays)
    peers = "neighbours" if hosted is None or hosted.peers == "neighbours" else "chips_sibling"
    assert hosted is None or hosted.peers in ("neighbours", "chips_sibling")

    def body(*refs):
        x_ref, g_ref, shard_ref = refs[0:3]
        h_in = refs[3:3 + nh]
        xo_ref, a_ref, u_ref, h_ref, wall = refs[3 + nh:8 + nh]
        h_out = refs[8 + nh:8 + 2 * nh]
        acc, h_all, wbuf, load_sems, send_sems, recv_sems = refs[8 + 2 * nh:14 + 2 * nh]
        carried_sems = refs[14 + 2 * nh:]
        k, i = pl.program_id(0), pl.program_id(1)
        legs, _ = _gather_legs(shard_ref, wall, send_sems, recv_sems, 0, True)
        begin, pass_on, _, _ = _gather_steps(legs, True)

        def load(chunk, src):
            return pltpu.make_async_copy(src, wbuf.at[chunk % 2], load_sems.at[chunk % 2])

        @pl.when((k == 0) & (i == 0))
        def _():
            _handshake(peers)
            begin()
            load(0, shard_ref).start()
            load(0, shard_ref).wait()

        @pl.when((k == 1) & (i == 0))
        def _():
            pass_on()
            if hosted is not None:
                hosted.start(h_in, h_out, *carried_sems)
            legs["pass_y"][1].wait_recv()
            load(1, wall.at[PEER_SLOT[1]]).start()
            load(1, wall.at[PEER_SLOT[1]]).wait()

        @pl.when((k == 1) & (i == nt // 2))
        def _():
            legs["pass_x"][1].wait_recv()
            load(2, wall.at[PEER_SLOT[0]]).start()

        @pl.when((k == 2) & (i == 0))
        def _():
            load(2, wall.at[PEER_SLOT[0]]).wait()

        @pl.when((k == 2) & (i == nt // 2))
        def _():
            legs["fwd_y"][1].wait_recv()
            legs["pass_d0"][0].start()
            legs["fwd_x"][1].wait_recv()
            legs["pass_d1"][0].start()
            legs["pass_d0"][1].wait_recv()
            legs["pass_d1"][1].wait_recv()
            load(3, wall.at[PEER_SLOT[2]]).start()
            if hosted is not None:
                hosted.middle(h_in, h_out, *carried_sems)

        @pl.when((k == 3) & (i == 0))
        def _():
            load(3, wall.at[PEER_SLOT[2]]).wait()

        if hosted is not None:
            @pl.when((k == 3) & (i == nt // 2))
            def _():
                hosted.late(h_in, h_out, *carried_sems)

        @pl.when(k == 0)
        def _():
            xv = x_ref[...]
            h0 = ((xv * _rms_scale(xv)) * g_ref[...]).astype(BF16)
            h_all[i] = h0
            h_ref[...] = h0

        h = h_all[i]
        wg, wu, wd = (wbuf[k % 2, kind].reshape(FF_SHARD, D_MODEL) for kind in range(3))
        a = _dot_nt(h, wg)
        u = _dot_nt(h, wu)
        a_ref[...] = a.astype(BF16)
        u_ref[...] = u.astype(BF16)
        part = _dot(((a * jax.nn.sigmoid(a)) * u).astype(BF16), wd)

        @pl.when(k == 0)
        def _():
            acc[i] = part

        @pl.when(k > 0)
        def _():
            acc[i] += part

        @pl.when(k == N_CHIPS - 1)
        def _():
            xo_ref[...] = x_ref[...] + 0.5 * acc[i]

        @pl.when((k == N_CHIPS - 1) & (i == nt - 1))
        def _():
            legs["own"][1].wait_recv()
            for pair in legs.values():
                pair[0].wait_send()
            if hosted is not None:
                hosted.finish(h_in, h_out, *carried_sems)

    def first_or_last(k):
        return (k == 0) | (k == N_CHIPS - 1)

    tok = lambda keep: pl.BlockSpec((tm, D_MODEL), lambda k, i: (jnp.where(keep(k), i, 0), 0))
    act = pl.BlockSpec((None, tm, FF_SHARD), lambda k, i: (k, i, 0))
    act_shape = jax.ShapeDtypeStruct((N_CHIPS, t, FF_SHARD), BF16)
    carried = [] if hosted is None else [pltpu.SemaphoreType.DMA((hosted.n_sems,))] * 2
    outs = pl.pallas_call(
        body, name=name, grid=(N_CHIPS, nt),
        in_specs=[tok(first_or_last), pl.BlockSpec((1, D_MODEL), lambda k, i: (0, 0)), ANY] + [ANY] * nh,
        out_specs=[tok(lambda k: k == N_CHIPS - 1), act, act,
                   pl.BlockSpec((tm, D_MODEL), lambda k, i: (jnp.where(k == 0, i, nt - 1), 0)), ANY] + [ANY] * nh,
        out_shape=[jax.ShapeDtypeStruct((t, D_MODEL), F32), act_shape, act_shape,
                   jax.ShapeDtypeStruct((t, D_MODEL), BF16),
                   jax.ShapeDtypeStruct((N_CHIPS,) + shard.shape, shard.dtype)]
                  + ([] if hosted is None else hosted.out_shapes),
        scratch_shapes=[pltpu.VMEM((nt, tm, D_MODEL), F32), pltpu.VMEM((nt, tm, D_MODEL), BF16),
                        pltpu.VMEM((2,) + shard.shape, shard.dtype), pltpu.SemaphoreType.DMA((2,)),
                        pltpu.SemaphoreType.DMA((9,)), pltpu.SemaphoreType.DMA((9,))] + carried,
        compiler_params=_with_barrier(_ARB2, peers),
    )(x, g, shard, *([] if hosted is None else hosted.arrays))
    return list(outs[:5]), list(outs[5:])


def _ffn_bwd(dxo, x, g, a4, u4, ffn_w, name, hosted=None, back_w=None):
    t = x.shape[0]
    tm = min(t, TOKEN_TILE)
    steps = (t // tm) * N_CHIPS
    cur, old = _pipeline_item(steps, 0), _pipeline_item(steps, 1)

    def body(*refs):
        dxo_ref, dxo_old_ref, x_ref, g_ref, a_ref, u_ref, wg_ref, wu_ref, wd_ref = refs[0:9]
        n_back = 0 if back_w is None else 1
        da_ref, du_ref, hid_ref, dob_ref, dx_ref, dg_ref = refs[9 + n_back:15 + n_back]
        acc_ref, da_slots, du_slots = refs[-3:]
        s = pl.program_id(0)
        jc, jo = cur(s)[1], old(s)[1]
        slot = s % 2

        @pl.when(s == 0)
        def _():
            dg_ref[...] = jnp.zeros_like(dg_ref)
            acc_ref[...] = jnp.zeros_like(acc_ref)
            da_slots[...] = jnp.zeros_like(da_slots)
            du_slots[...] = jnp.zeros_like(du_slots)

        @pl.when(jc == 0)
        def _():
            dob_ref[...] = (0.5 * dxo_ref[...]).astype(BF16)

        dhid = _dot_nt(dob_ref[...], wd_ref[...])
        a = a_ref[...].astype(F32)
        u = u_ref[...].astype(F32)
        sg = jax.nn.sigmoid(a)
        sl = a * sg
        hid_ref[...] = (sl * u).astype(BF16)
        du = (dhid * sl).astype(BF16)
        da = (dhid * u * _silu_grad(a, sg)).astype(BF16)
        du_ref[...] = du
        da_ref[...] = da
        acc_ref[...] += _dot(da_slots[1 - slot], wg_ref[...]) + _dot(du_slots[1 - slot], wu_ref[...])
        da_slots[slot] = da
        du_slots[slot] = du

        @pl.when((jo == N_CHIPS - 1) & (s > 0))
        def _():
            dx, dg = _rms_bwd(acc_ref[...], x_ref[...], g_ref[...])
            dx = dxo_old_ref[...] + dx
            dx_ref[...] = dx
            dg_ref[...] += dg
            acc_ref[...] = jnp.zeros_like(acc_ref)
            if back_w is not None:
                refs[15 + n_back][...] = _dot_nt(dx.astype(BF16), refs[9][...])

    tok_cur = pl.BlockSpec((tm, D_MODEL), lambda s: (cur(s)[0], 0))
    tok_old = pl.BlockSpec((tm, D_MODEL), lambda s: (old(s)[0], 0))
    act = pl.BlockSpec((None, tm, FF_SHARD), lambda s: (cur(s)[1], cur(s)[0], 0))
    row = pl.BlockSpec((1, D_MODEL), lambda s: (0, 0))
    w_arrays, weights = _ffn_weight_operands(ffn_w, [lambda s: old(s)[1], lambda s: old(s)[1], lambda s: cur(s)[1]])
    act_shape = jax.ShapeDtypeStruct((N_CHIPS, t, FF_SHARD), BF16)
    back_args, back_in, back_out, back_shape = [], [], [], []
    if back_w is not None:
        n_back_cols = back_w.shape[0]
        back_args, back_in = [back_w], [pl.BlockSpec(back_w.shape, lambda s: (0, 0))]
        back_out = [pl.BlockSpec((tm, n_back_cols), lambda s: (old(s)[0], 0))]
        back_shape = [jax.ShapeDtypeStruct((t, n_back_cols), F32)]
    return _call(
        body, (dxo, dxo, x, g, a4, u4, *w_arrays, *back_args), name=name, grid=(steps + 1,),
        in_specs=[tok_cur, tok_old, tok_old, row, act, act] + weights + back_in,
        out_specs=[act, act, act, tok_cur, tok_old, row] + back_out,
        out_shape=[act_shape, act_shape, act_shape,
                   jax.ShapeDtypeStruct((t, D_MODEL), BF16),
                   jax.ShapeDtypeStruct((t, D_MODEL), F32),
                   jax.ShapeDtypeStruct((1, D_MODEL), F32)] + back_shape,
        scratch_shapes=[pltpu.VMEM((tm, D_MODEL), F32), pltpu.VMEM((2, tm, FF_SHARD), BF16),
                        pltpu.VMEM((2, tm, FF_SHARD), BF16)],
        compiler_params=_ARB1, hosted=hosted)


def _matmul_tn(a, b, name, tka=None, out_dtype=F32, hosted=None):
    a3, b3 = a.ndim == 3, b.ndim == 3
    nb = a.shape[0] if a3 else (b.shape[0] if b3 else 1)
    t, ka, n = a.shape[-2], a.shape[-1], b.shape[-1]
    tka = ka if tka is None else tka
    tk = min(t, 4 * TOKEN_TILE)
    nk = t // tk

    def body(a_ref, b_ref, o_ref, acc_ref):
        k = pl.program_id(2)

        @pl.when(k == 0)
        def _():
            acc_ref[...] = jnp.zeros_like(acc_ref)

        acc_ref[...] += _dot_tn(a_ref[...].astype(BF16), b_ref[...].astype(BF16))

        @pl.when(k == nk - 1)
        def _():
            o_ref[...] = acc_ref[...].astype(out_dtype)

    a_spec = (pl.BlockSpec((None, tk, tka), lambda i, j, k: (i, k, j)) if a3
              else pl.BlockSpec((tk, tka), lambda i, j, k: (k, j)))
    b_spec = (pl.BlockSpec((None, tk, n), lambda i, j, k: (i, k, 0)) if b3
              else pl.BlockSpec((tk, n), lambda i, j, k: (k, 0)))
    outs, carried = _call(
        body, (a, b), name=name, grid=(nb, ka // tka, t // tk),
        in_specs=[a_spec, b_spec],
        out_specs=[pl.BlockSpec((None, tka, n), lambda i, j, k: (i, j, 0))],
        out_shape=[jax.ShapeDtypeStruct((nb, ka, n), out_dtype)],
        scratch_shapes=[pltpu.VMEM((tka, n), F32)],
        compiler_params=_ARB3, hosted=hosted)
    return outs[0] if hosted is None else (outs[0], carried)


def _matmul_nt(a, w, name, out_dtype=F32):
    t, k = a.shape
    n = w.shape[0]
    tm = min(t, TOKEN_TILE)

    def body(a_ref, w_ref, o_ref):
        o_ref[...] = _dot_nt(a_ref[...].astype(BF16), w_ref[...]).astype(out_dtype)

    return pl.pallas_call(
        body, name=name, grid=(t // tm,),
        in_specs=[pl.BlockSpec((tm, k), lambda i: (i, 0)), pl.BlockSpec((n, k), lambda i: (0, 0))],
        out_specs=pl.BlockSpec((tm, n), lambda i: (i, 0)),
        out_shape=jax.ShapeDtypeStruct((t, n), out_dtype),
        compiler_params=_ARB1,
    )(a, w)


def _mixer_in_bwd(dproj, w_in_pt, dres, x, g, name):
    t, k = dproj.shape
    tm = min(t, TOKEN_TILE)

    def body(a_ref, w_ref, dres_ref, x_ref, g_ref, dx_ref, dg_ref):
        @pl.when(pl.program_id(0) == 0)
        def _():
            dg_ref[...] = jnp.zeros_like(dg_ref)

        dh = _dot(a_ref[...], w_ref[...])
        dx, dg = _rms_bwd(dh, x_ref[...], g_ref[...])
        dx_ref[...] = dres_ref[...] + dx
        dg_ref[...] += dg

    tok = pl.BlockSpec((tm, D_MODEL), lambda i: (i, 0))
    row = pl.BlockSpec((1, D_MODEL), lambda i: (0, 0))
    return pl.pallas_call(
        body, name=name, grid=(t // tm,),
        in_specs=[pl.BlockSpec((tm, k), lambda i: (i, 0)), pl.BlockSpec((k, D_MODEL), lambda i: (0, 0)), tok, tok, row],
        out_specs=[tok, row],
        out_shape=[jax.ShapeDtypeStruct((t, D_MODEL), F32), jax.ShapeDtypeStruct((1, D_MODEL), F32)],
        compiler_params=_ARB1,
    )(dproj, w_in_pt, dres, x, g)


def _mixer_in_fwd(x, g, w_in_pt, name, hosted=None):
    t = x.shape[0]
    tm = min(t, TOKEN_TILE)
    tn = PROJ_P // PROJ_TILES

    def body(x_ref, g_ref, w_ref, p_ref, h_ref):
        @pl.when(pl.program_id(1) == 0)
        def _():
            xv = x_ref[...]
            h_ref[...] = ((xv * _rms_scale(xv)) * g_ref[...]).astype(BF16)

        p_ref[...] = _dot_nt(h_ref[...], w_ref[...])

    tok = pl.BlockSpec((tm, D_MODEL), lambda i, j: (i, 0))
    return _call(
        body, (x, g, w_in_pt), name=name, grid=(t // tm, PROJ_TILES),
        in_specs=[tok, pl.BlockSpec((1, D_MODEL), lambda i, j: (0, 0)),
                  pl.BlockSpec((tn, D_MODEL), lambda i, j: (j, 0))],
        out_specs=[pl.BlockSpec((tm, tn), lambda i, j: (i, j)), tok],
        out_shape=[jax.ShapeDtypeStruct((t, PROJ_P), F32), jax.ShapeDtypeStruct((t, D_MODEL), BF16)],
        scratch_shapes=[], compiler_params=_ARB2, hosted=hosted)


def _mixer_out_fwd(o_ret, o_gla, w_out, x, name):
    t = x.shape[0]
    tm = min(t, TOKEN_TILE)
    half = HEADS * LANES

    def body(a_ref, b_ref, w_ref, x_ref, o_ref):
        o_ref[...] = x_ref[...] + _dot(a_ref[...], w_ref[0:half, :]) + _dot(b_ref[...], w_ref[half:2 * half, :])

    tok = pl.BlockSpec((tm, D_MODEL), lambda i: (i, 0))
    hb = pl.BlockSpec((tm, half), lambda i: (i, 0))
    return pl.pallas_call(
        body, name=name, grid=(t // tm,),
        in_specs=[hb, hb, pl.BlockSpec((2 * half, D_MODEL), lambda i: (0, 0)), tok],
        out_specs=tok, out_shape=jax.ShapeDtypeStruct((t, D_MODEL), F32),
        compiler_params=_ARB1,
    )(o_ret, o_gla, w_out, x)


def _rot(v, cos, sa, sb):
    return v * cos + pltpu.roll(v, 96, 1) * sa + pltpu.roll(v, 32, 1) * sb


def _rot_t(d, cos, sa, sb):
    return d * cos + pltpu.roll(d * sa, 32, 1) + pltpu.roll(d * sb, 96, 1)


def _bmm(a, b):
    return jnp.einsum("cik,ckj->cij", a, b, preferred_element_type=F32)


def _bmm_nt(a, b):
    return jnp.einsum("cik,cjk->cij", a, b, preferred_element_type=F32)


def _bmm_tn(a, b):
    return jnp.einsum("cki,ckj->cij", a, b, preferred_element_type=F32)


def _masked_sum(mask, x):
    hi = x.astype(BF16)
    r1 = x - hi.astype(F32)
    mid = r1.astype(BF16)
    lo = (r1 - mid.astype(F32)).astype(BF16)
    return _bmm(mask, hi) + _bmm(mask, mid) + _bmm(mask, lo)


PAIR = 2


def _tile_inputs(is_ret, qkvg_refs, aux, nc):
    shape3 = (nc, CHUNK, LANES)
    q_ref, k_ref, v_ref, g_ref = qkvg_refs
    low_lanes = lax.broadcasted_iota(jnp.int32, (1, LANES), 1) < 64
    ri = lax.broadcasted_iota(jnp.int32, (PAIR * nc, CHUNK, CHUNK), 1)
    ci = lax.broadcasted_iota(jnp.int32, (PAIR * nc, CHUNK, CHUNK), 2)
    qs, ks, vs, bs, gates, extra = [], [], [], [], [], []
    for hd in range(PAIR):
        q_blk, k_blk = q_ref[...], k_ref[...]
        if hd == 1:
            q_blk, k_blk = pltpu.roll(q_blk, 64, 1), pltpu.roll(k_blk, 64, 1)
        q_raw, k_raw = jnp.where(low_lanes, q_blk, 0.0), jnp.where(low_lanes, k_blk, 0.0)
        vs.append(v_ref[:, LANES * hd:LANES * (hd + 1)].reshape(shape3))
        gates.append(g_ref[:, LANES * hd:LANES * (hd + 1)])
        if is_ret:
            cos_ref, sa_ref, sb_ref, lg_ref = aux
            cos, sa, sb = cos_ref[...], sa_ref[...], sb_ref[...]
            q = _rot(q_raw, cos, sa, sb)
            k = _rot(k_raw, cos, sa, sb) * QK_SCALE
            steps = (lax.broadcasted_iota(jnp.int32, shape3, 1) + 1).astype(F32)
            bs.append(steps * lg_ref[hd])
            extra.append(jnp.exp(jnp.abs(ri[0:nc] - ci[0:nc]).astype(F32) * lg_ref[hd][:, 0:CHUNK]))
        else:
            glow_ref, wa2_ref, ba_ref = aux
            lanes = slice(LANES * hd, LANES * (hd + 1))
            logit = _dot(glow_ref[...].astype(BF16), wa2_ref[:, lanes]) + ba_ref[:, lanes]
            la = (jnp.minimum(logit, 0.0) - jnp.log1p(jnp.exp(-jnp.abs(logit)))) * (1.0 / GATE_NORM)
            bs.append(_masked_sum((ci[0:nc] <= ri[0:nc]).astype(BF16), la.reshape(shape3)))
            extra.append(logit)
            q = q_raw * QK_SCALE
            k = k_raw
        qs.append(q.reshape(shape3))
        ks.append(k.reshape(shape3))
    cat = lambda parts: jnp.concatenate(parts, axis=0)
    return cat(qs), cat(ks), cat(vs), gates, cat(bs), extra, ri, ci


def _tile_scores(q, k, b, ri, ci):
    mid = b[:, CHUNK // 2 - 1:CHUNK // 2, :]
    ep = jnp.exp(b - mid)
    en = jnp.exp(mid - b)
    qt, kt, qh, kh = q * ep, k * en, q * en, k * ep
    low = _bmm_nt(qt.astype(BF16), kt.astype(BF16))
    upp = _bmm_nt(qh.astype(BF16), kh.astype(BF16))
    scores = jnp.where(ci <= ri, low, upp)
    return scores, ep, en, qt, kt, qh, kh


def _attn_specs(is_ret, t, tb, imap_t):
    nb = t // tb
    base = 0 if is_ret else 12
    wide = PAIR * LANES
    proj = [pl.BlockSpec((tb, LANES), lambda p, i: (imap_t(i), base + p)),
            pl.BlockSpec((tb, LANES), lambda p, i: (imap_t(i), base + 2 + p)),
            pl.BlockSpec((tb, wide), lambda p, i: (imap_t(i), (base + 4) // 2 + p)),
            pl.BlockSpec((tb, wide), lambda p, i: (imap_t(i), (base + 8) // 2 + p))]
    lane_t = pl.BlockSpec((tb, LANES), lambda p, i: (imap_t(i), 0))
    if is_ret:
        aux = [lane_t, lane_t, lane_t, pl.BlockSpec((PAIR, 1, LANES), lambda p, i: (p, 0, 0))]
    else:
        aux = [pl.BlockSpec((tb, LANES), lambda p, i: (imap_t(i), PROJ_P // LANES - 1)),
               pl.BlockSpec((LANES, wide), lambda p, i: (0, p)),
               pl.BlockSpec((1, wide), lambda p, i: (0, p))]
    gain = pl.BlockSpec((1, wide), lambda p, i: (0, p))
    pair_t = pl.BlockSpec((tb, wide), lambda p, i: (imap_t(i), p))
    narrow_t = pl.BlockSpec((tb, LANES), lambda p, i: (imap_t(i), p))
    state = pl.BlockSpec((PAIR, tb // CHUNK, LANES, LANES), lambda p, i: (p, imap_t(i), 0, 0))
    return nb, proj, aux, gain, pair_t, narrow_t, state


def _attn_fwd(is_ret, proj, aux_arrays, gain, name, hosted=None):
    t = proj.shape[0]
    tb = min(t, ATTN_TILE)
    nc = tb // CHUNK
    n_aux = 4 if is_ret else 3
    nb, proj_spec, aux_specs, gain_spec, pair_t, _, state_spec = _attn_specs(is_ret, t, tb, lambda i: i)

    def body(*refs):
        qkvg_refs = refs[0:4]
        aux = refs[4:4 + n_aux]
        gn_ref, ofin_ref, oraw_ref, st_ref, state = refs[4 + n_aux:]

        @pl.when(pl.program_id(1) == 0)
        def _():
            state[...] = jnp.zeros_like(state)

        q, k, v, gates, b, extra, ri, ci = _tile_inputs(is_ret, qkvg_refs, aux, nc)
        if is_ret:
            scores = _bmm_nt(q.astype(BF16), k.astype(BF16)) * jnp.concatenate(extra, axis=0)
        else:
            scores = _tile_scores(q, k, b, ri, ci)[0]
        vb = v.astype(BF16)
        intra = _bmm(scores.astype(BF16), vb)
        b_last = b[:, CHUNK - 1:CHUNK, :]
        e_last = jnp.exp(b_last)
        grow = _bmm_tn(vb, (k * jnp.exp(b_last - b)).astype(BF16))
        for hd in range(PAIR):
            st = state[hd]
            for c in range(nc):
                st_ref[hd, c] = st
                st = st * e_last[hd * nc + c] + grow[hd * nc + c]
            state[hd] = st
        starts = st_ref[...].reshape(PAIR * nc, LANES, LANES)
        out3 = intra + _bmm_nt((q * jnp.exp(b)).astype(BF16), starts.astype(BF16))
        for hd in range(PAIR):
            lanes = slice(LANES * hd, LANES * (hd + 1))
            out = out3[hd * nc:(hd + 1) * nc].reshape(tb, LANES)
            oraw_ref[:, lanes] = out
            normed = out * _rms_scale(out)
            gate = gates[hd]
            ofin_ref[:, lanes] = ((normed * gn_ref[:, lanes]) * (gate * jax.nn.sigmoid(gate))).astype(BF16)

    width = HEADS * LANES
    return _call(
        body, (proj, proj, proj, proj, *aux_arrays, gain), name=name, grid=(HEADS // PAIR, nb),
        in_specs=proj_spec + aux_specs + [gain_spec],
        out_specs=[pair_t, pair_t, state_spec],
        out_shape=[jax.ShapeDtypeStruct((t, width), BF16), jax.ShapeDtypeStruct((t, width), F32),
                   jax.ShapeDtypeStruct((HEADS, t // CHUNK, LANES, LANES), F32)],
        scratch_shapes=[pltpu.VMEM((PAIR, LANES, LANES), F32)],
        compiler_params=_ARB2, hosted=hosted)


def _attn_bwd(is_ret, proj, aux_arrays, gain, o_raw, states, d_out, name, hosted=None):
    t = proj.shape[0]
    tb = min(t, ATTN_TILE)
    nc = tb // CHUNK
    n_aux = 4 if is_ret else 3
    nblk = t // tb
    nb, proj_spec, aux_specs, gain_spec, pair_t, narrow_t, state_spec = _attn_specs(
        is_ret, t, tb, lambda i: nblk - 1 - i)
    base = 0 if is_ret else HEADS // PAIR
    dout_spec = pl.BlockSpec((tb, PAIR * LANES), lambda p, i: (nblk - 1 - i, base + p))

    def body(*refs):
        qkvg_refs = refs[0:4]
        aux = refs[4:4 + n_aux]
        gn_ref, oraw_ref, st_ref, dfin_ref = refs[4 + n_aux:8 + n_aux]
        dq_ref, dk_ref, dv_ref, dgate_ref, dgn_ref = refs[8 + n_aux:13 + n_aux]
        if is_ret:
            dstate, dafter_ref = refs[13 + n_aux:]
        else:
            dlogit_ref, dba_ref, dstate, dafter_ref = refs[13 + n_aux:]

        @pl.when(pl.program_id(1) == 0)
        def _():
            dstate[...] = jnp.zeros_like(dstate)
            dgn_ref[...] = jnp.zeros_like(dgn_ref)
            if not is_ret:
                dba_ref[...] = jnp.zeros_like(dba_ref)

        shape3 = (nc, CHUNK, LANES)
        q, k, v, gates, b, extra, ri, ci = _tile_inputs(is_ret, qkvg_refs, aux, nc)
        eb = jnp.exp(b)
        qe = q * eb
        b_last = b[:, CHUNK - 1:CHUNK, :]
        e_last = jnp.exp(b_last)
        ekd = jnp.exp(b_last - b)
        kd = k * ekd

        d_os = []
        for hd in range(PAIR):
            lanes = slice(LANES * hd, LANES * (hd + 1))
            gn, gate = gn_ref[:, lanes], gates[hd]
            out = oraw_ref[:, lanes]
            r = _rms_scale(out)
            normed = out * r
            sg = jax.nn.sigmoid(gate)
            dfin = dfin_ref[:, lanes]
            dgate_ref[:, lanes] = (dfin * (normed * gn) * _silu_grad(gate, sg)).astype(BF16)
            dpre = dfin * (gate * sg)
            dgn_ref[:, lanes] += jnp.sum(dpre * normed, axis=0, keepdims=True)
            dnormed = dpre * gn
            d_o = r * (dnormed - normed * jnp.mean(dnormed * normed, axis=-1, keepdims=True))
            d_os.append(d_o.reshape(shape3))
        dob, vb = jnp.concatenate(d_os, axis=0).astype(BF16), v.astype(BF16)

        dgrow = _bmm_tn(dob, qe.astype(BF16))
        for hd in range(PAIR):
            dst = dstate[hd]
            for c in reversed(range(nc)):
                dafter_ref[hd * nc + c] = dst
                dst = dst * e_last[hd * nc + c] + dgrow[hd * nc + c]
            dstate[hd] = dst
        st = st_ref[...].reshape(PAIR * nc, LANES, LANES)
        dafter = dafter_ref[...]
        stb, dafter_b = st.astype(BF16), dafter.astype(BF16)

        dsc = _bmm_nt(dob, vb)
        dsc_t = _bmm_nt(vb, dob)
        dqe = _bmm(dob, stb)
        dkd = _bmm(vb, dafter_b)
        if is_ret:
            decay, qb, kb = jnp.concatenate(extra, axis=0), q.astype(BF16), k.astype(BF16)
            scores_t = _bmm_nt(kb, qb) * decay
            dq = _bmm((dsc * decay).astype(BF16), kb) + dqe * eb
            dk = _bmm((dsc_t * decay).astype(BF16), qb) + dkd * ekd
        else:
            _, ep, en, qt, kt, qh, kh = _tile_scores(q, k, b, ri, ci)
            qtb, ktb, qhb, khb = qt.astype(BF16), kt.astype(BF16), qh.astype(BF16), kh.astype(BF16)
            scores_t = jnp.where(ci >= ri, _bmm_nt(ktb, qtb), _bmm_nt(khb, qhb))
            dqt = _bmm(jnp.where(ci <= ri, dsc, 0.0).astype(BF16), ktb)
            dqh = _bmm(jnp.where(ci <= ri, 0.0, dsc).astype(BF16), khb)
            dkt = _bmm(jnp.where(ci >= ri, dsc_t, 0.0).astype(BF16), qtb)
            dkh = _bmm(jnp.where(ci >= ri, 0.0, dsc_t).astype(BF16), qhb)
            dq = dqt * ep + dqh * en + dqe * eb
            dk = dkt * en + dkh * ep + dkd * ekd
        dv = _bmm(scores_t.astype(BF16), dob) + _bmm_nt(kd.astype(BF16), dafter_b)

        if not is_ret:
            db = dqt * qt - dkt * kt - dqh * qh + dkh * kh + dqe * qe - dkd * kd
            db_last = (jnp.sum(dkd * kd, axis=1, keepdims=True)
                       + jnp.sum(dafter * st, axis=1, keepdims=True) * e_last)
            last_row = lax.broadcasted_iota(jnp.int32, (PAIR * nc, CHUNK, LANES), 1) == CHUNK - 1
            db = db + jnp.where(last_row, db_last, 0.0)
            dla = _masked_sum((ci >= ri).astype(BF16), db)

        dq_pair, dk_pair = [], []
        for hd in range(PAIR):
            lanes = slice(LANES * hd, LANES * (hd + 1))
            rows3 = slice(hd * nc, (hd + 1) * nc)
            dq_h, dk_h = dq[rows3].reshape(tb, LANES), dk[rows3].reshape(tb, LANES)
            if is_ret:
                cos_ref, sa_ref, sb_ref, _ = aux
                cos, sa, sb = cos_ref[...], sa_ref[...], sb_ref[...]
                dq_h = _rot_t(dq_h, cos, sa, sb)
                dk_h = _rot_t(dk_h, cos, sa, sb) * QK_SCALE
            else:
                dq_h = dq_h * QK_SCALE
                dlogit = dla[rows3].reshape(tb, LANES) * (1.0 / GATE_NORM) * jax.nn.sigmoid(-extra[hd])
                dlogit_ref[:, lanes] = dlogit.astype(BF16)
                dba_ref[:, lanes] += jnp.sum(dlogit, axis=0, keepdims=True)
            dq_pair.append(dq_h)
            dk_pair.append(dk_h)
            dv_ref[:, lanes] = dv[rows3].reshape(tb, LANES).astype(BF16)
        dq_ref[...] = (dq_pair[0] + pltpu.roll(dq_pair[1], 64, 1)).astype(BF16)
        dk_ref[...] = (dk_pair[0] + pltpu.roll(dk_pair[1], 64, 1)).astype(BF16)

    width = HEADS * LANES
    row_out = pl.BlockSpec((1, PAIR * LANES), lambda p, i: (0, p))
    out_specs = [narrow_t, narrow_t, pair_t, pair_t, row_out]
    out_shape = ([jax.ShapeDtypeStruct((t, width // 2), BF16)] * 2 + [jax.ShapeDtypeStruct((t, width), BF16)] * 2
                 + [jax.ShapeDtypeStruct((1, width), F32)])
    if not is_ret:
        out_specs += [pair_t, row_out]
        out_shape += [jax.ShapeDtypeStruct((t, width), BF16), jax.ShapeDtypeStruct((1, width), F32)]
    return _call(
        body, (proj, proj, proj, proj, *aux_arrays, gain, o_raw, states, d_out), name=name,
        grid=(HEADS // PAIR, nblk),
        in_specs=proj_spec + aux_specs + [gain_spec, pair_t, state_spec, dout_spec],
        out_specs=out_specs, out_shape=out_shape,
        scratch_shapes=[pltpu.VMEM((PAIR, LANES, LANES), F32), pltpu.VMEM((PAIR * nc, LANES, LANES), F32)],
        compiler_params=_ARB2, hosted=hosted)


PEER_SLOT = (2, 1, 3)


def _place():
    x, y, c = lax.axis_index("x"), lax.axis_index("y"), lax.axis_index("c")
    chips = [(1 - x, y), (x, 1 - y), (1 - x, 1 - y)]
    return x, y, c, 2 * x + y, chips


def _route_split(rows, dtype):
    tile = 16 if dtype == BF16 else 8
    if rows < 2 * tile:
        return None
    return -(-(rows // 2) // tile) * tile


def _routes(by_peer):
    x, y, c, me, chips = _place()
    (xx, xy), (yx, yy), (dx, dy) = chips
    if by_peer:
        slots = dict(own=0, from_x=PEER_SLOT[0], from_y=PEER_SLOT[1], diag=PEER_SLOT[2],
                     mine_on_x=PEER_SLOT[0], mine_on_y=PEER_SLOT[1])
    else:
        slots = dict(own=me, from_x=2 * xx + xy, from_y=2 * yx + yy, diag=2 * dx + dy, mine_on_x=me, mine_on_y=me)
    return c, (xx, xy, c), (yx, yy, c), (dx, dy, c), (x, y, 1 - c), slots


def _gather_legs(src, out, send_sems, recv_sems, base, by_peer):
    c, to_x, to_y, to_d, sibling, s = _routes(by_peer)
    r0 = _route_split(src.shape[2], src.dtype)

    def cp(k, src_ref, dst_ref, to):
        return pltpu.make_async_remote_copy(src_ref=src_ref, dst_ref=dst_ref, send_sem=send_sems.at[base + k],
                                            recv_sem=recv_sems.at[base + k], device_id=to, device_id_type=MESH)

    mine = src.at[:, c]
    legs = dict(
        x=(cp(0, mine, out.at[s["mine_on_x"], :, c], to_x), cp(0, mine, out.at[s["from_x"], :, c], to_x)),
        y=(cp(1, mine, out.at[s["mine_on_y"], :, c], to_y), cp(1, mine, out.at[s["from_y"], :, c], to_y)),
        pass_x=(cp(4, out.at[s["from_x"], :, c], out.at[s["from_x"], :, c], sibling),
                cp(4, mine, out.at[s["from_x"], :, 1 - c], sibling)),
        pass_y=(cp(5, out.at[s["from_y"], :, c], out.at[s["from_y"], :, c], sibling),
                cp(5, mine, out.at[s["from_y"], :, 1 - c], sibling)),
        own=(cp(8, src, out.at[s["own"]], sibling), cp(8, src, out.at[s["own"]], sibling)))
    if r0 is None:
        mine_on_d = s["diag"] if by_peer else s["own"]
        legs["d"] = (cp(2, mine, out.at[mine_on_d, :, c], to_d), cp(2, mine, out.at[s["diag"], :, c], to_d))
        legs["pass_d"] = (cp(6, out.at[s["diag"], :, c], out.at[s["diag"], :, c], sibling),
                          cp(6, mine, out.at[s["diag"], :, 1 - c], sibling))
        return legs, False
    lo, hi = pl.ds(0, r0), pl.ds(r0, src.shape[2] - r0)
    fx_on_y = s["diag"] if by_peer else s["from_x"]
    fy_on_x = s["diag"] if by_peer else s["from_y"]
    legs.update(
        fwd_y=(cp(2, out.at[s["from_x"], :, c, lo], out.at[fx_on_y, :, c, lo], to_y),
               cp(2, mine.at[:, lo], out.at[s["diag"], :, c, lo], to_y)),
        fwd_x=(cp(3, out.at[s["from_y"], :, c, hi], out.at[fy_on_x, :, c, hi], to_x),
               cp(3, mine.at[:, hi], out.at[s["diag"], :, c, hi], to_x)),
        pass_d0=(cp(6, out.at[s["diag"], :, c, lo], out.at[s["diag"], :, c, lo], sibling),
                 cp(6, mine.at[:, lo], out.at[s["diag"], :, 1 - c, lo], sibling)),
        pass_d1=(cp(7, out.at[s["diag"], :, c, hi], out.at[s["diag"], :, c, hi], sibling),
                 cp(7, mine.at[:, hi], out.at[s["diag"], :, 1 - c, hi], sibling)))
    return legs, True


def _gather_steps(legs, routed):
    def start():
        legs["x"][0].start()
        legs["y"][0].start()
        legs["own"][0].start()
        if not routed:
            legs["d"][0].start()

    def middle():
        legs["x"][1].wait_recv()
        if routed:
            legs["fwd_y"][0].start()
        legs["pass_x"][0].start()
        legs["y"][1].wait_recv()
        if routed:
            legs["fwd_x"][0].start()
        legs["pass_y"][0].start()

    last = ["pass_d0", "pass_d1"] if routed else ["pass_d"]

    def late():
        if routed:
            legs["fwd_y"][1].wait_recv()
            legs["pass_d0"][0].start()
            legs["fwd_x"][1].wait_recv()
            legs["pass_d1"][0].start()
        else:
            legs["d"][1].wait_recv()
            legs["pass_d"][0].start()

    def finish():
        for name in ["own", "pass_x", "pass_y"] + last:
            legs[name][1].wait_recv()
        for name in ["x", "y", "own", "pass_x", "pass_y"] + last + (["fwd_y", "fwd_x"] if routed else ["d"]):
            legs[name][0].wait_send()

    return start, middle, late, finish


def _gather_plan(arrs):
    na = len(arrs)

    def steps(ins, outs, send_sems, recv_sems):
        return [_gather_steps(*_gather_legs(ins[a], outs[a], send_sems, recv_sems, 9 * a, False)) for a in range(na)]

    def run(which):
        def hook(*refs):
            for step in steps(*refs):
                step[which]()
        return hook

    routed = all(_route_split(a.shape[2], a.dtype) is not None for a in arrs)
    return _Hosted(arrs, [jax.ShapeDtypeStruct((N_CHIPS,) + a.shape, a.dtype) for a in arrs], 9 * na,
                   run(0), run(3), middle=run(1), late=run(2), peers="neighbours" if routed else "chips_sibling")


def _pair_exchange_plan(grads):
    na = len(grads)

    def copies(ins, outs, send_sems, recv_sems):
        x, y, c, _, _ = _place()
        return [pltpu.make_async_remote_copy(
            src_ref=ins[a].at[:, 1 - c], dst_ref=outs[a], send_sem=send_sems.at[a], recv_sem=recv_sems.at[a],
            device_id=(x, y, 1 - c), device_id_type=MESH) for a in range(na)]

    def start(*refs):
        for cp in copies(*refs):
            cp.start()

    def finish(*refs):
        for cp in copies(*refs):
            cp.wait()

    return _Hosted(grads, [jax.ShapeDtypeStruct(g.shape[:1] + g.shape[2:], g.dtype) for g in grads], na, start, finish,
                   peers="sibling")


def _small_gather_plan(block):
    def copies(ins, outs, send_sems, recv_sems):
        x, y, c, _, chips = _place()
        peers = [(x, y, 1 - c)] + [(px, py, pc) for px, py in chips for pc in (c, 1 - c)]
        sends = [pltpu.make_async_remote_copy(
            src_ref=ins[0], dst_ref=outs[0].at[4 * x + 2 * y + c], send_sem=send_sems.at[k], recv_sem=recv_sems.at[k],
            device_id=peer, device_id_type=MESH) for k, peer in enumerate(peers)]
        recvs = [pltpu.make_async_remote_copy(
            src_ref=ins[0], dst_ref=outs[0].at[4 * px + 2 * py + pc], send_sem=send_sems.at[k], recv_sem=recv_sems.at[k],
            device_id=(px, py, pc), device_id_type=MESH) for k, (px, py, pc) in enumerate(peers)]
        return sends, recvs

    def start(*refs):
        for cp in copies(*refs)[0]:
            cp.start()

    def finish(*refs):
        sends, recvs = copies(*refs)
        for cp in recvs:
            cp.wait_recv()
        for cp in sends:
            cp.wait_send()

    return _Hosted([block], [jax.ShapeDtypeStruct((8,) + block.shape, block.dtype)], 7, start, finish)


def _sum_devices(blocks):
    def body(b_ref, o_ref):
        acc = b_ref[0]
        for d in range(1, 8):
            acc = acc + b_ref[d]
        o_ref[...] = acc

    return pl.pallas_call(body, name="sum_devices", in_specs=[_VMEM], out_specs=_VMEM,
                          out_shape=jax.ShapeDtypeStruct(blocks.shape[1:], blocks.dtype))(blocks)


def _pair_add(grad, recv, c_arr, name):
    _, _, r, cols = grad.shape

    def body(c_ref, g_ref, r_ref, o_ref):
        o_ref[...] = (g_ref[...].astype(F32) + r_ref[...].astype(F32)).astype(BF16)

    return pl.pallas_call(
        body, name=name,
        grid_spec=pltpu.PrefetchScalarGridSpec(
            num_scalar_prefetch=1, grid=(N_CHIPS,),
            in_specs=[pl.BlockSpec((None, None, r, cols), lambda p, c_ref: (p, c_ref[0], 0, 0)),
                      pl.BlockSpec((None, r, cols), lambda p, c_ref: (p, 0, 0))],
            out_specs=pl.BlockSpec((None, r, cols), lambda p, c_ref: (p, 0, 0))),
        out_shape=jax.ShapeDtypeStruct((N_CHIPS, r, cols), BF16),
        compiler_params=_ARB1,
    )(c_arr, grad, recv)


def _chip_exchange_plan(sums, by_peer=False):
    na = len(sums)

    def copies(ins, outs, send_sems, recv_sems):
        x, y, c, me, chips = _place()

        def copy(a, j, px, py, block, slot):
            return pltpu.make_async_remote_copy(
                src_ref=ins[a].at[block], dst_ref=outs[a].at[slot],
                send_sem=send_sems.at[3 * a + j], recv_sem=recv_sems.at[3 * a + j],
                device_id=(px, py, c), device_id_type=MESH)

        peers = [(a, j, px, py) for a in range(na) for j, (px, py) in enumerate(chips)]
        return me, peers, copy

    def start(*refs):
        me, peers, copy = copies(*refs)
        for a, j, px, py in peers:
            if by_peer:
                copy(a, j, px, py, PEER_SLOT[j], PEER_SLOT[j]).start()
            else:
                copy(a, j, px, py, 2 * px + py, me).start()

    def finish(*refs):
        me, peers, copy = copies(*refs)
        for a, j, px, py in peers:
            if by_peer:
                copy(a, j, px, py, PEER_SLOT[j], PEER_SLOT[j]).wait_recv()
            else:
                copy(a, j, px, py, me, 2 * px + py).wait_recv()
        for a, j, px, py in peers:
            if by_peer:
                copy(a, j, px, py, PEER_SLOT[j], PEER_SLOT[j]).wait_send()
            else:
                copy(a, j, px, py, 2 * px + py, me).wait_send()

    return _Hosted(sums, [jax.ShapeDtypeStruct(s.shape, s.dtype) for s in sums], 3 * na, start, finish, peers="chips")


def _chip_sum(own, recv, me_arr, name):
    _, r, cols = recv.shape

    def body(me_ref, own_ref, r_ref, o_ref):
        o_ref[...] = jnp.zeros_like(o_ref)
        for q in range(N_CHIPS):
            @pl.when(me_ref[0] == q)
            def _():
                o_ref[...] += own_ref[...].astype(F32)

            @pl.when(me_ref[0] != q)
            def _():
                o_ref[...] += r_ref[q].astype(F32)

    return pl.pallas_call(
        body, name=name,
        grid_spec=pltpu.PrefetchScalarGridSpec(
            num_scalar_prefetch=1, grid=(1,),
            in_specs=[pl.BlockSpec((None, r, cols), lambda i, me_ref: (me_ref[0], 0, 0)),
                      pl.BlockSpec((N_CHIPS, r, cols), lambda i, me_ref: (0, 0, 0))],
            out_specs=pl.BlockSpec((r, cols), lambda i, me_ref: (0, 0))),
        out_shape=jax.ShapeDtypeStruct((r, cols), F32),
        compiler_params=_ARB1,
    )(me_arr, own, recv)


def _peer_sum(own, recv, name):
    _, r, cols = recv.shape

    def body(own_ref, r_ref, o_ref):
        acc = own_ref[...].astype(F32) + r_ref[1].astype(F32)
        acc = acc + r_ref[2].astype(F32)
        o_ref[...] = acc + r_ref[3].astype(F32)

    return pl.pallas_call(
        body, name=name, grid=(1,),
        in_specs=[pl.BlockSpec((None, r, cols), lambda i: (0, 0, 0)), pl.BlockSpec((N_CHIPS, r, cols), lambda i: (0, 0, 0))],
        out_specs=pl.BlockSpec((r, cols), lambda i: (0, 0)),
        out_shape=jax.ShapeDtypeStruct((r, cols), F32),
        compiler_params=_ARB1,
    )(own, recv)


def _pair_share_plan(halves):
    na = len(halves)

    def copies(ins, outs, send_sems, recv_sems):
        x, y, c, _, _ = _place()
        return [pltpu.make_async_remote_copy(
            src_ref=ins[a], dst_ref=outs[a], send_sem=send_sems.at[a], recv_sem=recv_sems.at[a],
            device_id=(x, y, 1 - c), device_id_type=MESH) for a in range(na)]

    def start(*refs):
        for cp in copies(*refs):
            cp.start()

    def finish(*refs):
        for cp in copies(*refs):
            cp.wait()

    return _Hosted(halves, [jax.ShapeDtypeStruct(h.shape, h.dtype) for h in halves], na, start, finish,
                   peers="sibling")


def _row_tile(rows):
    best = rows
    for cand in range(8, min(rows, 512) + 1, 8):
        if rows % cand == 0:
            best = cand
    return best


def _adamw_math(w, g, m, v):
    m2 = ADAM_B1 * m + (1.0 - ADAM_B1) * g
    v2 = ADAM_B2 * v + (1.0 - ADAM_B2) * (g * g)
    m_hat = m2 / (1.0 - ADAM_B1 ** ADAM_STEP)
    v_hat = v2 / (1.0 - ADAM_B2 ** ADAM_STEP)
    return -ADAM_LR * (m_hat / (jnp.sqrt(v_hat) + ADAM_EPS) + ADAM_WD * w), m2, v2


def _adamw_halves(w, g_mine, g_other, m, v, c_arr, name):
    rows, cols = w.shape
    r = rows // 2
    tr = _row_tile(r)
    nt = r // tr

    def body(c_ref, w_ref, gm_ref, go_ref, m_ref, v_ref, g_ref, d_ref, nm_ref, nv_ref):
        gv = jnp.where(pl.program_id(0) == c_ref[0], gm_ref[...], go_ref[...])
        g_ref[...] = gv
        d_ref[...], nm_ref[...], nv_ref[...] = _adamw_math(w_ref[...], gv, m_ref[...], v_ref[...])

    full = pl.BlockSpec((tr, cols), lambda h, i, c_ref: (h * nt + i, 0))
    half = pl.BlockSpec((tr, cols), lambda h, i, c_ref: (i, 0))
    shape = jax.ShapeDtypeStruct((rows, cols), F32)
    return pl.pallas_call(
        body, name=name,
        grid_spec=pltpu.PrefetchScalarGridSpec(
            num_scalar_prefetch=1, grid=(2, nt),
            in_specs=[full, half, half, full, full], out_specs=[full] * 4),
        out_shape=[shape] * 4,
        compiler_params=_ARB2,
    )(c_arr, w, g_mine, g_other, m, v)


def _pad_w_in_t(w_in_t):
    return jnp.pad(w_in_t, ((0, PROJ_P - IN_WIDTH), (0, 0)))


def _unpad_w_in_t(w_pt):
    return w_pt[0:IN_WIDTH]


def _rope_tables(t):
    half = 32
    inv = ROPE_BASE ** (-jnp.arange(half, dtype=F32) * 2.0 / 64)
    ang = jnp.arange(t, dtype=F32)[:, None] * inv[None, :]
    cos, sin = jnp.cos(ang), jnp.sin(ang)
    z32, z64 = jnp.zeros((t, 32), F32), jnp.zeros((t, 64), F32)
    return (jnp.concatenate([cos, cos, z64], axis=1),
            jnp.concatenate([-sin, z32, z64], axis=1),
            jnp.concatenate([z32, sin, z64], axis=1))


def _halves(w):
    n, rows, cols = w.shape
    return w.reshape(n, 2, rows // 2, cols)


_VMEM = pl.BlockSpec(memory_space=pltpu.VMEM)


def _pack_small(n1, nm, n2, nf, nret, ngla, ba, wa2_p, loss_blk):
    def body(n1_ref, nm_ref, n2_ref, nf_ref, nret_ref, ngla_ref, ba_ref, wa2_ref, loss_ref, o_ref):
        o_ref[...] = jnp.zeros_like(o_ref)
        o_ref[0:1, :] = n1_ref[...]
        o_ref[1:2, :] = nm_ref[...]
        o_ref[2:3, :] = n2_ref[...]
        o_ref[3:4, :] = nf_ref[...]
        o_ref[4:5, 0:512] = nret_ref[...]
        o_ref[4:5, 512:1024] = ngla_ref[...]
        o_ref[5:6, 0:256] = ba_ref[...]
        o_ref[6:7, 0:LANES] = loss_ref[0:1, :]
        o_ref[8:8 + GATE_RANK, 0:HEADS * LANES] = wa2_ref[0:GATE_RANK, :]

    return pl.pallas_call(
        body, name="pack_small", in_specs=[_VMEM] * 9, out_specs=_VMEM,
        out_shape=jax.ShapeDtypeStruct((SMALL_ROWS, D_MODEL), F32),
    )(n1, nm, n2, nf, nret, ngla, ba, wa2_p, loss_blk)


def _small_update(summed, chip_arr, ws, ms, vs):
    n = len(ws)

    def body(chip_ref, s_ref, *refs):
        w_refs, m_refs, v_refs = refs[0:n], refs[n:2 * n], refs[2 * n:3 * n]
        outs = refs[3 * n:]
        wa2_all = s_ref[8:8 + GATE_RANK, 0:HEADS * LANES]
        wa2_g = jnp.zeros((GATE_RANK, 64), F32)
        for p in range(N_CHIPS):
            wa2_g = jnp.where(chip_ref[0] == p, wa2_all[:, LANES * p:LANES * p + 64], wa2_g)
        grads = [s_ref[0:1, :], s_ref[1:2, :], s_ref[2:3, :], s_ref[3:4, :], s_ref[4:5, 0:512],
                 s_ref[4:5, 512:1024], s_ref[5:6, 0:256], wa2_g]
        for k in range(n):
            d, m2, v2 = _adamw_math(w_refs[k][...], grads[k], m_refs[k][...], v_refs[k][...])
            outs[k][...] = grads[k]
            outs[n + k][...] = d
            outs[2 * n + k][...] = m2
            outs[3 * n + k][...] = v2

    shapes = [jax.ShapeDtypeStruct(w.shape, F32) for w in ws] * 4
    smem = pl.BlockSpec(memory_space=pltpu.SMEM)
    outs = pl.pallas_call(
        body, name="small_update", in_specs=[smem] + [_VMEM] * (1 + 3 * n), out_specs=[_VMEM] * (4 * n),
        out_shape=shapes,
    )(chip_arr, summed, *ws, *ms, *vs)
    return outs[0:n], outs[n:2 * n], outs[2 * n:3 * n], outs[3 * n:4 * n]


def _pad_in_rows(w_t):
    return jnp.pad(w_t, ((0, IN_ROWS - IN_SHARD), (0, 0)))


def _forward_backward(xs, target, ffn1_w, rest, ba_p, ffn1_norm_g, mix_norm_g, ret_norm_g, gla_norm_g, ffn2_norm_g,
                      final_norm_g, ffn1_gather=None, rest_plan=None, rest_weights=None, ffn2_plans=None,
                      ffn2_weights=None, ffn2_pairs=None, ffn2_pairs_done=None, early=None, late=None, small_plan=None):
    t = xs.shape[0]
    cos_t, sa_t, sb_t = _rope_tables(t)
    log_gamma = jnp.log(1.0 - 2.0 ** (-5.0 - jnp.arange(HEADS, dtype=F32)))
    lg_t = jnp.broadcast_to(log_gamma[:, None, None], (HEADS, 1, LANES))
    ret_aux = [cos_t, sa_t, sb_t, lg_t]

    if ffn1_gather is None:
        (x1, a1, u1, h1), gathered = _ffn_fwd(xs, ffn1_norm_g, ffn1_w, "ffn1_fwd", hosted=rest_plan)
    else:
        ffn1_shard, ffn1_weights = ffn1_gather
        (x1, a1, u1, h1, wall), gathered = _ffn1_fwd_gathering(xs, ffn1_norm_g, ffn1_shard, "ffn1_fwd",
                                                               hosted=rest_plan)
        ffn1_w = ffn1_weights(wall)
    ffn2_w, w_in_pt, w_out_full, wa2_p = rest if rest_plan is None else rest_weights(gathered)
    plans = [None] * 3 if ffn2_plans is None else ffn2_plans
    (proj, h_mix), got_gate = _mixer_in_fwd(x1, mix_norm_g, w_in_pt, "mixer_in_fwd", hosted=plans[0])
    gla_aux = [proj, wa2_p, ba_p]
    (o_ret, raw_ret, st_ret), got_up = _attn_fwd(True, proj, ret_aux, ret_norm_g, "ret_fwd", hosted=plans[1])
    (o_gla, raw_gla, st_gla), got_down = _attn_fwd(False, proj, gla_aux, gla_norm_g, "gla_fwd", hosted=plans[2])
    if ffn2_plans is not None:
        ffn2_w = ffn2_weights(got_gate + got_up + got_down)
    x2 = _mixer_out_fwd(o_ret, o_gla, w_out_full, x1, "mixer_out_fwd")
    (loss_blk, dx3, d_final_g, a2, u2, h2), _ = _ffn_fwd(x2, ffn2_norm_g, ffn2_w, "ffn2_fwd",
                                                       loss_head=(final_norm_g, target))

    (da2, du2, hid2, dob2, dx2, d_ffn2_g, d_o), _ = _ffn_bwd(dx3, x2, ffn2_norm_g, a2, u2, ffn2_w, "ffn2_bwd",
                                                            back_w=w_out_full)
    g_gate2 = _matmul_tn(da2, h2, "ffn2_dgate", out_dtype=BF16)
    g_up2 = _matmul_tn(du2, h2, "ffn2_dup", out_dtype=BF16)
    g_down2 = _matmul_tn(hid2, dob2, "ffn2_ddown", out_dtype=BF16)

    g_wout_ret = _matmul_tn(o_ret, dx2, "wout_grad_ret", out_dtype=BF16)
    g_wout_gla = _matmul_tn(o_gla, dx2, "wout_grad_gla", out_dtype=BF16)
    pairs_plan = None if ffn2_pairs is None else ffn2_pairs([g_gate2, g_up2, g_down2])
    (*dproj_ret, d_ret_g), pair_recv = _attn_bwd(True, proj, ret_aux, ret_norm_g, raw_ret, st_ret, d_o, "ret_bwd",
                                                 hosted=pairs_plan)
    if ffn2_pairs is not None:
        ffn2_pairs_done(pair_recv)
    (*dproj_gla, d_gla_g, dlogit, d_ba_p), _ = _attn_bwd(False, proj, gla_aux, gla_norm_g, raw_gla, st_gla, d_o,
                                                        "gla_bwd")
    d_glow = _matmul_nt(dlogit, wa2_p, "gate_low_bwd", out_dtype=BF16)
    g_wa2_p = _matmul_tn(proj[:, PROJ_P - LANES:], dlogit, "gate_w_grad")
    dproj = jnp.concatenate(dproj_ret + dproj_gla + [d_glow], axis=1)
    g_win_p = _matmul_tn(dproj, h_mix, "w_in_grad", tka=PROJ_P // PROJ_TILES, out_dtype=BF16)
    dx1, d_mix_g = _mixer_in_bwd(dproj, w_in_pt, dx2, x1, mix_norm_g, "mixer_in_bwd")
    g_win_t = _unpad_w_in_t(g_win_p[0])
    g_win = jnp.stack([_pad_in_rows(g_win_t[IN_SHARD * p:IN_SHARD * (p + 1)]) for p in range(N_CHIPS)], axis=0)
    g_wout = jnp.concatenate([g_wout_ret[0], g_wout_gla[0]], axis=0).reshape(N_CHIPS, D_MODEL // N_CHIPS, D_MODEL)

    early_grads = [g_win, g_wout] if ffn2_pairs is not None else [g_gate2, g_up2, g_down2, g_win, g_wout]
    early_plan = None if early is None else early(early_grads)
    (da1, du1, hid1, dob1, grad_x, d_ffn1_g), arrived = _ffn_bwd(dx1, xs, ffn1_norm_g, a1, u1, ffn1_w, "ffn1_bwd",
                                                                hosted=early_plan)
    d_ba = d_ba_p.reshape(HEADS, LANES)[:, 0:64].reshape(1, 256)
    small_local = _pack_small(d_ffn1_g, d_mix_g, d_ffn2_g, d_final_g, d_ret_g, d_gla_g, d_ba, g_wa2_p[0], loss_blk)
    late_grads, late_arrived = [], []
    for lhs, rhs, name in ((da1, h1, "ffn1_dgate"), (du1, h1, "ffn1_dup"), (hid1, dob1, "ffn1_ddown")):
        if late is None:
            plan = None
        else:
            plan = late(late_grads[-1], len(late_grads)) if late_grads else small_plan(small_local)
        res = _matmul_tn(lhs, rhs, name, out_dtype=BF16, hosted=plan)
        if plan is not None:
            res, carried = res
            late_arrived += carried
        late_grads.append(res)
    g_gate1, g_up1, g_down1 = late_grads

    return (small_local, grad_x, g_gate1, g_up1, g_down1, g_gate2, g_up2, g_down2, g_win, g_wout, g_wa2_p,
            d_ba_p, d_ffn1_g, d_mix_g, d_ffn2_g, d_final_g, d_ret_g, d_gla_g, arrived, late_arrived)


def kernel(x, ffn1_norm_g, ffn1_w_gate, ffn1_w_up, ffn1_w_down, mix_norm_g, w_in, ret_norm_g, gla_w_a2, gla_b_a, gla_norm_g, w_out, ffn2_norm_g, ffn2_w_gate, ffn2_w_up, ffn2_w_down, final_norm_g, loss_target, m_ffn1_norm_g, m_ffn1_w_gate, m_ffn1_w_up, m_ffn1_w_down, m_mix_norm_g, m_w_in, m_ret_norm_g, m_gla_w_a2, m_gla_b_a, m_gla_norm_g, m_w_out, m_ffn2_norm_g, m_ffn2_w_gate, m_ffn2_w_up, m_ffn2_w_down, m_final_norm_g, v_ffn1_norm_g, v_ffn1_w_gate, v_ffn1_w_up, v_ffn1_w_down, v_mix_norm_g, v_w_in, v_ret_norm_g, v_gla_w_a2, v_gla_b_a, v_gla_norm_g, v_w_out, v_ffn2_norm_g, v_ffn2_w_gate, v_ffn2_w_up, v_ffn2_w_down, v_final_norm_g):
    t = x.shape[1]
    xs = x.reshape(t, D_MODEL)
    target = loss_target.reshape(t, D_MODEL)
    chip = 2 * lax.axis_index("x") + lax.axis_index("y")
    c_arr = lax.axis_index("c").astype(jnp.int32).reshape(1)

    me_arr = chip.astype(jnp.int32).reshape(1)

    pad_rows = _pad_in_rows

    ffn1_shard = _halves(jnp.stack([ffn1_w_gate[0].T, ffn1_w_up[0].T, ffn1_w_down[0]], axis=0).astype(BF16))
    rest_shards = [_halves(pad_rows(w_in[0].T).astype(BF16)[None]),
                   _halves(w_out.astype(BF16)),
                   jnp.concatenate([gla_w_a2.reshape(GATE_RANK, 64), jnp.zeros((GATE_RANK, 64), F32)],
                                   axis=1).reshape(1, 2, 8, LANES)]
    ffn2_shards = [_halves(w.astype(BF16)[None]) for w in (ffn2_w_gate[0].T, ffn2_w_up[0].T, ffn2_w_down[0])]
    def ffn1_weights(gathered):
        return gathered.reshape(N_CHIPS, 3, FF_SHARD, D_MODEL)

    def rest_weights(gathered):
        win_all, wout_all, wa2_all = gathered
        win_t = win_all.reshape(N_CHIPS, IN_ROWS, D_MODEL)
        w_in_pt = jnp.zeros((PROJ_P, D_MODEL), BF16)
        for p in range(N_CHIPS):
            w_in_pt = lax.dynamic_update_slice(w_in_pt, win_t[p, 0:IN_SHARD], (IN_SHARD * p, 0))
        wa2_p = jnp.pad(
            wa2_all.reshape(N_CHIPS, GATE_RANK, LANES).transpose(1, 0, 2).reshape(GATE_RANK, HEADS * LANES),
            ((0, LANES - GATE_RANK), (0, 0))).astype(BF16)
        return (None, w_in_pt, wout_all.reshape(D_MODEL, D_MODEL), wa2_p)

    def ffn2_weights(gathered):
        return [g.reshape(N_CHIPS, FF_SHARD, D_MODEL) for g in gathered]

    def by_halves(g):
        return g.reshape(g.shape[0], 2, g.shape[1] // 2, g.shape[2])

    def pair_adds(halves, recv, tag):
        return [_pair_add(g, r, c_arr, "pair_add_%s%d" % (tag, k)) for k, (g, r) in enumerate(zip(halves, recv))]

    def pair_sums(grads, tag):
        halves = [by_halves(g) for g in grads]
        recv = _run_hosted(_pair_exchange_plan(halves), "pair_exchange_" + tag)
        return pair_adds(halves, recv, tag)

    early_sums, ffn2_halves = [], []

    def ffn2_pairs(grads):
        ffn2_halves.extend(by_halves(g) for g in grads)
        return _pair_exchange_plan(ffn2_halves)

    def ffn2_pairs_done(recv):
        early_sums.extend(pair_adds(ffn2_halves, recv, "ffn2_"))

    def early(grads):
        early_sums.extend(pair_sums(grads, "early"))
        return _chip_exchange_plan(early_sums)

    late_sums = []

    def late(grad, number):
        late_sums.extend(pair_sums([grad], "late%d" % number))
        return _chip_exchange_plan(late_sums[-1:], by_peer=True)

    ba_p = jnp.pad(gla_b_a.reshape(HEADS, 64), ((0, 0), (0, 64))).reshape(1, HEADS * LANES)
    fb = _forward_backward(xs, target, None, None, ba_p, ffn1_norm_g, mix_norm_g, ret_norm_g, gla_norm_g,
                           ffn2_norm_g, final_norm_g.reshape(1, D_MODEL), ffn1_gather=(ffn1_shard, ffn1_weights),
                           rest_plan=_gather_plan(rest_shards), rest_weights=rest_weights,
                           ffn2_plans=[_gather_plan(ffn2_shards[0:2]), None, _gather_plan(ffn2_shards[2:3])],
                           ffn2_weights=ffn2_weights,
                           ffn2_pairs=ffn2_pairs, ffn2_pairs_done=ffn2_pairs_done, early=early, late=late,
                           small_plan=_small_gather_plan)
    (small_local, grad_x, _, _, g_down1, _, _, _, _, _, _, _, _, _, _, _, _, _, early_arrived, late_arrived) = fb
    small_all, late_arrived = late_arrived[0], late_arrived[1:]
    late_arrived = late_arrived + _run_hosted(late(g_down1, 3), "chip_exchange_late")
    mine = [_peer_sum(s, r, "chip_sum_%d" % k) for k, (s, r) in enumerate(zip(late_sums, late_arrived))]
    mine += [_chip_sum(s, r, me_arr, "chip_sum_%d" % (3 + k)) for k, (s, r) in enumerate(zip(early_sums, early_arrived))]
    other = _run_hosted(_pair_share_plan(mine), "pair_share")

    device = 2 * chip + lax.axis_index("c")
    small_sum = _sum_devices(lax.dynamic_update_slice(small_all, small_local[None], (device, 0, 0)))
    loss = small_sum[6, 0]

    def rows(n1, nm, n2, nf, nret, ngla, ba, wa2):
        return [n1, nm, n2, nf.reshape(1, D_MODEL), nret, ngla, ba, wa2.reshape(GATE_RANK, 64)]

    small = _small_update(
        small_sum, me_arr,
        rows(ffn1_norm_g, mix_norm_g, ffn2_norm_g, final_norm_g, ret_norm_g, gla_norm_g, gla_b_a, gla_w_a2),
        rows(m_ffn1_norm_g, m_mix_norm_g, m_ffn2_norm_g, m_final_norm_g, m_ret_norm_g, m_gla_norm_g, m_gla_b_a,
             m_gla_w_a2),
        rows(v_ffn1_norm_g, v_mix_norm_g, v_ffn2_norm_g, v_final_norm_g, v_ret_norm_g, v_gla_norm_g, v_gla_b_a,
             v_gla_w_a2))
    s_grad, s_delta, s_m, s_v = [
        [*o[0:3], o[3].reshape(D_MODEL), *o[4:7], o[7].reshape(1, GATE_RANK, 64)] for o in small]

    def big(k, w, m, v, name, to_2d, from_2d):
        outs4 = _adamw_halves(to_2d(w), mine[k], other[k], to_2d(m), to_2d(v), c_arr, name)
        return [from_2d(z) for z in outs4]

    plain = (lambda w: w[0], lambda z: z[None])
    transposed = (lambda w: w[0].T, lambda z: z.T[None])
    in_proj = (lambda w: pad_rows(w[0].T), lambda z: z[0:IN_SHARD].T[None])
    r_g1 = big(0, ffn1_w_gate, m_ffn1_w_gate, v_ffn1_w_gate, "adamw_ffn1_gate", *transposed)
    r_u1 = big(1, ffn1_w_up, m_ffn1_w_up, v_ffn1_w_up, "adamw_ffn1_up", *transposed)
    r_d1 = big(2, ffn1_w_down, m_ffn1_w_down, v_ffn1_w_down, "adamw_ffn1_down", *plain)
    r_g2 = big(3, ffn2_w_gate, m_ffn2_w_gate, v_ffn2_w_gate, "adamw_ffn2_gate", *transposed)
    r_u2 = big(4, ffn2_w_up, m_ffn2_w_up, v_ffn2_w_up, "adamw_ffn2_up", *transposed)
    r_d2 = big(5, ffn2_w_down, m_ffn2_w_down, v_ffn2_w_down, "adamw_ffn2_down", *plain)
    r_in = big(6, w_in, m_w_in, v_w_in, "adamw_w_in", *in_proj)
    r_out = big(7, w_out, m_w_out, v_w_out, "adamw_w_out", *plain)

    def leaves(k, smalls):
        n1, nm, n2, nf, nret, ngla, ba, wa2 = smalls
        return [n1, r_g1[k], r_u1[k], r_d1[k], nm, r_in[k], nret, wa2, ba, ngla, r_out[k], n2, r_g2[k], r_u2[k], r_d2[k], nf]

    outs = [loss, grad_x.reshape(x.shape)]
    outs += leaves(0, s_grad) + leaves(1, s_delta) + leaves(2, s_m) + leaves(3, s_v)
    return tuple(outs)
```

```python
import functools

import jax
import jax.numpy as jnp
from jax import lax
from jax.experimental import pallas as pl
from jax.experimental.pallas import tpu as pltpu

F32, BF16 = jnp.float32, jnp.bfloat16
MESH = pl.DeviceIdType.MESH
ANY = pl.BlockSpec(memory_space=pl.ANY)

D_MODEL = 1024
D_FF = 2816
N_CHIPS = 4
FF_SHARD = D_FF // N_CHIPS
IN_WIDTH = 3088
IN_SHARD = IN_WIDTH // N_CHIPS
IN_ROWS = 800
CHUNK = 64
HEADS = 4
LANES = 128
PROJ_P = 3072 + LANES
PROJ_TILES = 5
GATE_RANK = 16
QK_SCALE = 0.125
GATE_NORM = 16.0
RMS_EPS = 1e-6
ROPE_BASE = 10000.0
ADAM_LR, ADAM_B1, ADAM_B2, ADAM_EPS, ADAM_WD, ADAM_STEP = 0.001, 0.9, 0.999, 1e-08, 0.01, 10
SMALL_ROWS = 32
TOKEN_TILE = 512
ATTN_TILE = 512

_ARB2 = pltpu.CompilerParams(dimension_semantics=("arbitrary", "arbitrary"))
_ARB1 = pltpu.CompilerParams(dimension_semantics=("arbitrary",))
_ARB3 = pltpu.CompilerParams(dimension_semantics=("arbitrary", "arbitrary", "arbitrary"))


def _dot(a, b):
    return jnp.dot(a, b, preferred_element_type=F32)


def _dot_nt(a, b):
    return lax.dot_general(a, b, (((1,), (1,)), ((), ())), preferred_element_type=F32)


def _dot_tn(a, b):
    return lax.dot_general(a, b, (((0,), (0,)), ((), ())), preferred_element_type=F32)


def _rms_scale(xv):
    return lax.rsqrt(jnp.mean(xv * xv, axis=-1, keepdims=True) + RMS_EPS)


def _rms_bwd(dh, xv, g):
    r = _rms_scale(xv)
    xhat = xv * r
    dxhat = dh * g
    dx = r * (dxhat - xhat * jnp.mean(dxhat * xhat, axis=-1, keepdims=True))
    return dx, jnp.sum(dh * xhat, axis=0, keepdims=True)


def _silu_grad(a, sg):
    return sg * (1.0 + a * (1.0 - sg))


class _Hosted:
    def __init__(self, arrays, out_shapes, n_sems, start, finish, middle=None, late=None, peers=None):
        self.arrays, self.out_shapes, self.n_sems = list(arrays), list(out_shapes), n_sems
        self.start, self.finish = start, finish
        self.middle = middle if middle is not None else (lambda *refs: None)
        self.late = late if late is not None else (lambda *refs: None)
        self.peers = peers


PEER_SETS = {
    "sibling": (0, lambda x, y, c: [(x, y, 1 - c)]),
    "chips": (1, lambda x, y, c: [(1 - x, y, c), (x, 1 - y, c), (1 - x, 1 - y, c)]),
    "neighbours": (2, lambda x, y, c: [(1 - x, y, c), (x, 1 - y, c), (x, y, 1 - c)]),
    "chips_sibling": (3, lambda x, y, c: [(1 - x, y, c), (x, 1 - y, c), (1 - x, 1 - y, c), (x, y, 1 - c)]),
}


def _handshake(kind):
    x, y, c, _, _ = _place()
    peers = PEER_SETS[kind][1](x, y, c)
    barrier = pltpu.get_barrier_semaphore()
    for peer in peers:
        pl.semaphore_signal(barrier, inc=1, device_id=peer, device_id_type=MESH)
    pl.semaphore_wait(barrier, len(peers))


def _with_barrier(compiler_params, kind):
    if kind is None:
        return compiler_params
    semantics = None if compiler_params is None else compiler_params.dimension_semantics
    return pltpu.CompilerParams(dimension_semantics=semantics, collective_id=PEER_SETS[kind][0])


def _call(body, args, *, name, grid, in_specs, out_specs, out_shape, scratch_shapes, compiler_params, hosted=None):
    if hosted is None:
        outs = pl.pallas_call(body, name=name, grid=grid, in_specs=in_specs, out_specs=out_specs, out_shape=out_shape,
                              scratch_shapes=scratch_shapes, compiler_params=compiler_params)(*args)
        return list(outs), []
    n_in, n_out, n_sc, nh = len(in_specs), len(out_specs), len(scratch_shapes), len(hosted.arrays)

    def wrapped(*refs):
        ins, h_in = refs[:n_in], refs[n_in:n_in + nh]
        outs, h_out = refs[n_in + nh:n_in + nh + n_out], refs[n_in + nh + n_out:n_in + 2 * nh + n_out]
        rest = refs[n_in + 2 * nh + n_out:]
        scratch, (send_sems, recv_sems) = rest[:n_sc], rest[n_sc:]
        step = functools.reduce(lambda flat, d: flat * grid[d] + pl.program_id(d), range(len(grid)), 0)
        total = functools.reduce(lambda a, b: a * b, grid)

        @pl.when(step == 0)
        def _():
            if hosted.peers is not None:
                _handshake(hosted.peers)
            hosted.start(h_in, h_out, send_sems, recv_sems)

        @pl.when(step == total // 2)
        def _():
            hosted.middle(h_in, h_out, send_sems, recv_sems)

        @pl.when(step == total - 1)
        def _():
            hosted.late(h_in, h_out, send_sems, recv_sems)

        body(*ins, *outs, *scratch)
        last = step == total - 1

        @pl.when(last)
        def _():
            hosted.finish(h_in, h_out, send_sems, recv_sems)

    sems = [pltpu.SemaphoreType.DMA((hosted.n_sems,)), pltpu.SemaphoreType.DMA((hosted.n_sems,))]
    outs = pl.pallas_call(
        wrapped, name=name, grid=grid, in_specs=list(in_specs) + [ANY] * nh, out_specs=list(out_specs) + [ANY] * nh,
        out_shape=list(out_shape) + hosted.out_shapes, scratch_shapes=list(scratch_shapes) + sems,
        compiler_params=_with_barrier(compiler_params, hosted.peers))(*args, *hosted.arrays)
    return list(outs[:n_out]), list(outs[n_out:])


def _run_hosted(hosted, name):
    nh = len(hosted.arrays)

    def body(*refs):
        h_in, h_out, (send_sems, recv_sems) = refs[:nh], refs[nh:2 * nh], refs[2 * nh:]
        if hosted.peers is not None:
            _handshake(hosted.peers)
        hosted.start(h_in, h_out, send_sems, recv_sems)
        hosted.middle(h_in, h_out, send_sems, recv_sems)
        hosted.late(h_in, h_out, send_sems, recv_sems)
        hosted.finish(h_in, h_out, send_sems, recv_sems)

    sems = [pltpu.SemaphoreType.DMA((hosted.n_sems,)), pltpu.SemaphoreType.DMA((hosted.n_sems,))]
    return list(pl.pallas_call(body, name=name, in_specs=[ANY] * nh, out_specs=[ANY] * nh, out_shape=hosted.out_shapes,
                               scratch_shapes=sems, compiler_params=_with_barrier(None, hosted.peers))(*hosted.arrays))


def _ffn_weight_operands(ffn_w, chunk_maps):
    if isinstance(ffn_w, (list, tuple)):
        specs = [pl.BlockSpec((None, FF_SHARD, D_MODEL), lambda *g, m=m: (m(*g), 0, 0)) for m in chunk_maps]
        return list(ffn_w), specs
    specs = [pl.BlockSpec((None, None, FF_SHARD, D_MODEL), lambda *g, m=m, k=kind: (m(*g), k, 0, 0))
             for kind, m in enumerate(chunk_maps)]
    return [ffn_w] * 3, specs


def _pipeline_item(steps, lag):
    def item(s):
        it = jnp.clip(s - lag, 0, steps - 1)
        return it // N_CHIPS, it % N_CHIPS

    return item


def _ffn_fwd(x, g, ffn_w, name, hosted=None, loss_head=None):
    t = x.shape[0]
    tm = min(t, TOKEN_TILE)
    n_head = 0 if loss_head is None else 2

    def body(*refs):
        x_ref, g_ref, wg_ref, wu_ref, wd_ref = refs[0:5]
        head_in = refs[5:5 + n_head]
        outs = refs[5 + n_head:-1]
        acc_ref = refs[-1]
        a_ref, u_ref, h_ref = outs[-3:]
        i, j = pl.program_id(0), pl.program_id(1)

        @pl.when(j == 0)
        def _():
            xv = x_ref[...]
            h_ref[...] = ((xv * _rms_scale(xv)) * g_ref[...]).astype(BF16)
            acc_ref[...] = jnp.zeros_like(acc_ref)

        h = h_ref[...]
        a = _dot_nt(h, wg_ref[...])
        u = _dot_nt(h, wu_ref[...])
        a_ref[...] = a.astype(BF16)
        u_ref[...] = u.astype(BF16)
        hid = (a * jax.nn.sigmoid(a)) * u
        acc_ref[...] += _dot(hid.astype(BF16), wd_ref[...])

        if loss_head is None:
            @pl.when(j == N_CHIPS - 1)
            def _():
                outs[0][...] = x_ref[...] + 0.5 * acc_ref[...]
        else:
            gf_ref, t_ref = head_in
            l_ref, dx_ref, dgf_ref = outs[0:3]

            @pl.when((i == 0) & (j == 0))
            def _():
                l_ref[...] = jnp.zeros_like(l_ref)
                dgf_ref[...] = jnp.zeros_like(dgf_ref)

            @pl.when(j == N_CHIPS - 1)
            def _():
                xv = x_ref[...] + 0.5 * acc_ref[...]
                gv = gf_ref[...]
                err = (xv * _rms_scale(xv)) * gv - t_ref[...]
                l_ref[...] += 0.5 * jnp.sum(jnp.mean(err * err, axis=-1, keepdims=True), axis=0, keepdims=True)
                dx, dg = _rms_bwd(err * (1.0 / D_MODEL), xv, gv)
                dx_ref[...] = dx
                dgf_ref[...] += dg

    tok = pl.BlockSpec((tm, D_MODEL), lambda i, j: (i, 0))
    row = pl.BlockSpec((1, D_MODEL), lambda i, j: (0, 0))
    act = pl.BlockSpec((None, tm, FF_SHARD), lambda i, j: (j, i, 0))
    act_shape = jax.ShapeDtypeStruct((N_CHIPS, t, FF_SHARD), BF16)
    w_arrays, weights = _ffn_weight_operands(ffn_w, [lambda i, j: j] * 3)
    if loss_head is None:
        first_specs, first_shapes, head_args, head_specs = [tok], [jax.ShapeDtypeStruct((t, D_MODEL), F32)], [], []
    else:
        first_specs = [pl.BlockSpec((8, LANES), lambda i, j: (0, 0)), tok, row]
        first_shapes = [jax.ShapeDtypeStruct((8, LANES), F32), jax.ShapeDtypeStruct((t, D_MODEL), F32),
                        jax.ShapeDtypeStruct((1, D_MODEL), F32)]
        head_args, head_specs = list(loss_head), [row, tok]
    return _call(
        body, (x, g, *w_arrays, *head_args), name=name, grid=(t // tm, N_CHIPS),
        in_specs=[tok, row] + weights + head_specs,
        out_specs=first_specs + [act, act, tok],
        out_shape=first_shapes + [act_shape, act_shape, jax.ShapeDtypeStruct((t, D_MODEL), BF16)],
        scratch_shapes=[pltpu.VMEM((tm, D_MODEL), F32)],
        compiler_params=_ARB2, hosted=hosted)


def _ffn1_fwd_gathering(x, g, shard, name, hosted=None):
    t = x.shape[0]
    tm = min(t, TOKEN_TILE)
    nt = t // tm
    nh = 0 if hosted is None else len(hosted.arrays)
    peers = "neighbours" if hosted is None or hosted.peers == "neighbours" else "chips_sibling"
    assert hosted is None or hosted.peers in ("neighbours", "chips_sibling")

    def body(*refs):
        x_ref, g_ref, shard_ref = refs[0:3]
        h_in = refs[3:3 + nh]
        xo_ref, a_ref, u_ref, h_ref, wall = refs[3 + nh:8 + nh]
        h_out = refs[8 + nh:8 + 2 * nh]
        acc, h_all, wbuf, load_sems, send_sems, recv_sems = refs[8 + 2 * nh:14 + 2 * nh]
        carried_sems = refs[14 + 2 * nh:]
        k, i = pl.program_id(0), pl.program_id(1)
        legs, _ = _gather_legs(shard_ref, wall, send_sems, recv_sems, 0, True)
        begin, pass_on, _, _ = _gather_steps(legs, True)

        def load(chunk, src):
            return pltpu.make_async_copy(src, wbuf.at[chunk % 2], load_sems.at[chunk % 2])

        @pl.when((k == 0) & (i == 0))
        def _():
            _handshake(peers)
            begin()
            load(0, shard_ref).start()
            load(0, shard_ref).wait()

        @pl.when((k == 1) & (i == 0))
        def _():
            pass_on()
            if hosted is not None:
                hosted.start(h_in, h_out, *carried_sems)
            legs["pass_y"][1].wait_recv()
            load(1, wall.at[PEER_SLOT[1]]).start()
            load(1, wall.at[PEER_SLOT[1]]).wait()

        @pl.when((k == 1) & (i == nt // 2))
        def _():
            legs["pass_x"][1].wait_recv()
            load(2, wall.at[PEER_SLOT[0]]).start()

        @pl.when((k == 2) & (i == 0))
        def _():
            load(2, wall.at[PEER_SLOT[0]]).wait()

        @pl.when((k == 2) & (i == nt // 2))
        def _():
            legs["fwd_y"][1].wait_recv()
            legs["pass_d0"][0].start()
            legs["fwd_x"][1].wait_recv()
            legs["pass_d1"][0].start()
            legs["pass_d0"][1].wait_recv()
            legs["pass_d1"][1].wait_recv()
            load(3, wall.at[PEER_SLOT[2]]).start()
            if hosted is not None:
                hosted.middle(h_in, h_out, *carried_sems)

        @pl.when((k == 3) & (i == 0))
        def _():
            load(3, wall.at[PEER_SLOT[2]]).wait()

        if hosted is not None:
            @pl.when((k == 3) & (i == nt // 2))
            def _():
                hosted.late(h_in, h_out, *carried_sems)

        @pl.when(k == 0)
        def _():
            xv = x_ref[...]
            h0 = ((xv * _rms_scale(xv)) * g_ref[...]).astype(BF16)
            h_all[i] = h0
            h_ref[...] = h0

        h = h_all[i]
        wg, wu, wd = (wbuf[k % 2, kind].reshape(FF_SHARD, D_MODEL) for kind in range(3))
        a = _dot_nt(h, wg)
        u = _dot_nt(h, wu)
        a_ref[...] = a.astype(BF16)
        u_ref[...] = u.astype(BF16)
        part = _dot(((a * jax.nn.sigmoid(a)) * u).astype(BF16), wd)

        @pl.when(k == 0)
        def _():
            acc[i] = part

        @pl.when(k > 0)
        def _():
            acc[i] += part

        @pl.when(k == N_CHIPS - 1)
        def _():
            xo_ref[...] = x_ref[...] + 0.5 * acc[i]

        @pl.when((k == N_CHIPS - 1) & (i == nt - 1))
        def _():
            legs["own"][1].wait_recv()
            for pair in legs.values():
                pair[0].wait_send()
            if hosted is not None:
                hosted.finish(h_in, h_out, *carried_sems)

    def first_or_last(k):
        return (k == 0) | (k == N_CHIPS - 1)

    tok = lambda keep: pl.BlockSpec((tm, D_MODEL), lambda k, i: (jnp.where(keep(k), i, 0), 0))
    act = pl.BlockSpec((None, tm, FF_SHARD), lambda k, i: (k, i, 0))
    act_shape = jax.ShapeDtypeStruct((N_CHIPS, t, FF_SHARD), BF16)
    carried = [] if hosted is None else [pltpu.SemaphoreType.DMA((hosted.n_sems,))] * 2
    outs = pl.pallas_call(
        body, name=name, grid=(N_CHIPS, nt),
        in_specs=[tok(first_or_last), pl.BlockSpec((1, D_MODEL), lambda k, i: (0, 0)), ANY] + [ANY] * nh,
        out_specs=[tok(lambda k: k == N_CHIPS - 1), act, act,
                   pl.BlockSpec((tm, D_MODEL), lambda k, i: (jnp.where(k == 0, i, nt - 1), 0)), ANY] + [ANY] * nh,
        out_shape=[jax.ShapeDtypeStruct((t, D_MODEL), F32), act_shape, act_shape,
                   jax.ShapeDtypeStruct((t, D_MODEL), BF16),
                   jax.ShapeDtypeStruct((N_CHIPS,) + shard.shape, shard.dtype)]
                  + ([] if hosted is None else hosted.out_shapes),
        scratch_shapes=[pltpu.VMEM((nt, tm, D_MODEL), F32), pltpu.VMEM((nt, tm, D_MODEL), BF16),
                        pltpu.VMEM((2,) + shard.shape, shard.dtype), pltpu.SemaphoreType.DMA((2,)),
                        pltpu.SemaphoreType.DMA((9,)), pltpu.SemaphoreType.DMA((9,))] + carried,
        compiler_params=_with_barrier(_ARB2, peers),
    )(x, g, shard, *([] if hosted is None else hosted.arrays))
    return list(outs[:5]), list(outs[5:])


def _ffn_bwd(dxo, x, g, a4, u4, ffn_w, name, hosted=None, back_w=None):
    t = x.shape[0]
    tm = min(t, TOKEN_TILE)
    steps = (t // tm) * N_CHIPS
    cur, old = _pipeline_item(steps, 0), _pipeline_item(steps, 1)

    def body(*refs):
        dxo_ref, dxo_old_ref, x_ref, g_ref, a_ref, u_ref, wg_ref, wu_ref, wd_ref = refs[0:9]
        n_back = 0 if back_w is None else 1
        da_ref, du_ref, hid_ref, dob_ref, dx_ref, dg_ref = refs[9 + n_back:15 + n_back]
        acc_ref, da_slots, du_slots = refs[-3:]
        s = pl.program_id(0)
        jc, jo = cur(s)[1], old(s)[1]
        slot = s % 2

        @pl.when(s == 0)
        def _():
            dg_ref[...] = jnp.zeros_like(dg_ref)
            acc_ref[...] = jnp.zeros_like(acc_ref)
            da_slots[...] = jnp.zeros_like(da_slots)
            du_slots[...] = jnp.zeros_like(du_slots)

        @pl.when(jc == 0)
        def _():
            dob_ref[...] = (0.5 * dxo_ref[...]).astype(BF16)

        dhid = _dot_nt(dob_ref[...], wd_ref[...])
        a = a_ref[...].astype(F32)
        u = u_ref[...].astype(F32)
        sg = jax.nn.sigmoid(a)
        sl = a * sg
        hid_ref[...] = (sl * u).astype(BF16)
        du = (dhid * sl).astype(BF16)
        da = (dhid * u * _silu_grad(a, sg)).astype(BF16)
        du_ref[...] = du
        da_ref[...] = da
        acc_ref[...] += _dot(da_slots[1 - slot], wg_ref[...]) + _dot(du_slots[1 - slot], wu_ref[...])
        da_slots[slot] = da
        du_slots[slot] = du

        @pl.when((jo == N_CHIPS - 1) & (s > 0))
        def _():
            dx, dg = _rms_bwd(acc_ref[...], x_ref[...], g_ref[...])
            dx = dxo_old_ref[...] + dx
            dx_ref[...] = dx
            dg_ref[...] += dg
            acc_ref[...] = jnp.zeros_like(acc_ref)
            if back_w is not None:
                refs[15 + n_back][...] = _dot_nt(dx.astype(BF16), refs[9][...])

    tok_cur = pl.BlockSpec((tm, D_MODEL), lambda s: (cur(s)[0], 0))
    tok_old = pl.BlockSpec((tm, D_MODEL), lambda s: (old(s)[0], 0))
    act = pl.BlockSpec((None, tm, FF_SHARD), lambda s: (cur(s)[1], cur(s)[0], 0))
    row = pl.BlockSpec((1, D_MODEL), lambda s: (0, 0))
    w_arrays, weights = _ffn_weight_operands(ffn_w, [lambda s: old(s)[1], lambda s: old(s)[1], lambda s: cur(s)[1]])
    act_shape = jax.ShapeDtypeStruct((N_CHIPS, t, FF_SHARD), BF16)
    back_args, back_in, back_out, back_shape = [], [], [], []
    if back_w is not None:
        n_back_cols = back_w.shape[0]
        back_args, back_in = [back_w], [pl.BlockSpec(back_w.shape, lambda s: (0, 0))]
        back_out = [pl.BlockSpec((tm, n_back_cols), lambda s: (old(s)[0], 0))]
        back_shape = [jax.ShapeDtypeStruct((t, n_back_cols), F32)]
    return _call(
        body, (dxo, dxo, x, g, a4, u4, *w_arrays, *back_args), name=name, grid=(steps + 1,),
        in_specs=[tok_cur, tok_old, tok_old, row, act, act] + weights + back_in,
        out_specs=[act, act, act, tok_cur, tok_old, row] + back_out,
        out_shape=[act_shape, act_shape, act_shape,
                   jax.ShapeDtypeStruct((t, D_MODEL), BF16),
                   jax.ShapeDtypeStruct((t, D_MODEL), F32),
                   jax.ShapeDtypeStruct((1, D_MODEL), F32)] + back_shape,
        scratch_shapes=[pltpu.VMEM((tm, D_MODEL), F32), pltpu.VMEM((2, tm, FF_SHARD), BF16),
                        pltpu.VMEM((2, tm, FF_SHARD), BF16)],
        compiler_params=_ARB1, hosted=hosted)


def _matmul_tn(a, b, name, tka=None, out_dtype=F32, hosted=None):
    a3, b3 = a.ndim == 3, b.ndim == 3
    nb = a.shape[0] if a3 else (b.shape[0] if b3 else 1)
    t, ka, n = a.shape[-2], a.shape[-1], b.shape[-1]
    tka = ka if tka is None else tka
    tk = min(t, 4 * TOKEN_TILE)
    nk = t // tk

    def body(a_ref, b_ref, o_ref, acc_ref):
        k = pl.program_id(2)

        @pl.when(k == 0)
        def _():
            acc_ref[...] = jnp.zeros_like(acc_ref)

        acc_ref[...] += _dot_tn(a_ref[...].astype(BF16), b_ref[...].astype(BF16))

        @pl.when(k == nk - 1)
        def _():
            o_ref[...] = acc_ref[...].astype(out_dtype)

    a_spec = (pl.BlockSpec((None, tk, tka), lambda i, j, k: (i, k, j)) if a3
              else pl.BlockSpec((tk, tka), lambda i, j, k: (k, j)))
    b_spec = (pl.BlockSpec((None, tk, n), lambda i, j, k: (i, k, 0)) if b3
              else pl.BlockSpec((tk, n), lambda i, j, k: (k, 0)))
    outs, carried = _call(
        body, (a, b), name=name, grid=(nb, ka // tka, t // tk),
        in_specs=[a_spec, b_spec],
        out_specs=[pl.BlockSpec((None, tka, n), lambda i, j, k: (i, j, 0))],
        out_shape=[jax.ShapeDtypeStruct((nb, ka, n), out_dtype)],
        scratch_shapes=[pltpu.VMEM((tka, n), F32)],
        compiler_params=_ARB3, hosted=hosted)
    return outs[0] if hosted is None else (outs[0], carried)


def _matmul_nt(a, w, name, out_dtype=F32):
    t, k = a.shape
    n = w.shape[0]
    tm = min(t, TOKEN_TILE)

    def body(a_ref, w_ref, o_ref):
        o_ref[...] = _dot_nt(a_ref[...].astype(BF16), w_ref[...]).astype(out_dtype)

    return pl.pallas_call(
        body, name=name, grid=(t // tm,),
        in_specs=[pl.BlockSpec((tm, k), lambda i: (i, 0)), pl.BlockSpec((n, k), lambda i: (0, 0))],
        out_specs=pl.BlockSpec((tm, n), lambda i: (i, 0)),
        out_shape=jax.ShapeDtypeStruct((t, n), out_dtype),
        compiler_params=_ARB1,
    )(a, w)


def _mixer_in_bwd(dproj, w_in_pt, dres, x, g, name):
    t, k = dproj.shape
    tm = min(t, TOKEN_TILE)

    def body(a_ref, w_ref, dres_ref, x_ref, g_ref, dx_ref, dg_ref):
        @pl.when(pl.program_id(0) == 0)
        def _():
            dg_ref[...] = jnp.zeros_like(dg_ref)

        dh = _dot(a_ref[...], w_ref[...])
        dx, dg = _rms_bwd(dh, x_ref[...], g_ref[...])
        dx_ref[...] = dres_ref[...] + dx
        dg_ref[...] += dg

    tok = pl.BlockSpec((tm, D_MODEL), lambda i: (i, 0))
    row = pl.BlockSpec((1, D_MODEL), lambda i: (0, 0))
    return pl.pallas_call(
        body, name=name, grid=(t // tm,),
        in_specs=[pl.BlockSpec((tm, k), lambda i: (i, 0)), pl.BlockSpec((k, D_MODEL), lambda i: (0, 0)), tok, tok, row],
        out_specs=[tok, row],
        out_shape=[jax.ShapeDtypeStruct((t, D_MODEL), F32), jax.ShapeDtypeStruct((1, D_MODEL), F32)],
        compiler_params=_ARB1,
    )(dproj, w_in_pt, dres, x, g)


def _mixer_in_fwd(x, g, w_in_pt, name, hosted=None):
    t = x.shape[0]
    tm = min(t, TOKEN_TILE)
    tn = PROJ_P // PROJ_TILES

    def body(x_ref, g_ref, w_ref, p_ref, h_ref):
        @pl.when(pl.program_id(1) == 0)
        def _():
            xv = x_ref[...]
            h_ref[...] = ((xv * _rms_scale(xv)) * g_ref[...]).astype(BF16)

        p_ref[...] = _dot_nt(h_ref[...], w_ref[...])

    tok = pl.BlockSpec((tm, D_MODEL), lambda i, j: (i, 0))
    return _call(
        body, (x, g, w_in_pt), name=name, grid=(t // tm, PROJ_TILES),
        in_specs=[tok, pl.BlockSpec((1, D_MODEL), lambda i, j: (0, 0)),
                  pl.BlockSpec((tn, D_MODEL), lambda i, j: (j, 0))],
        out_specs=[pl.BlockSpec((tm, tn), lambda i, j: (i, j)), tok],
        out_shape=[jax.ShapeDtypeStruct((t, PROJ_P), F32), jax.ShapeDtypeStruct((t, D_MODEL), BF16)],
        scratch_shapes=[], compiler_params=_ARB2, hosted=hosted)


def _mixer_out_fwd(o_ret, o_gla, w_out, x, name):
    t = x.shape[0]
    tm = min(t, TOKEN_TILE)
    half = HEADS * LANES

    def body(a_ref, b_ref, w_ref, x_ref, o_ref):
        o_ref[...] = x_ref[...] + _dot(a_ref[...], w_ref[0:half, :]) + _dot(b_ref[...], w_ref[half:2 * half, :])

    tok = pl.BlockSpec((tm, D_MODEL), lambda i: (i, 0))
    hb = pl.BlockSpec((tm, half), lambda i: (i, 0))
    return pl.pallas_call(
        body, name=name, grid=(t // tm,),
        in_specs=[hb, hb, pl.BlockSpec((2 * half, D_MODEL), lambda i: (0, 0)), tok],
        out_specs=tok, out_shape=jax.ShapeDtypeStruct((t, D_MODEL), F32),
        compiler_params=_ARB1,
    )(o_ret, o_gla, w_out, x)


def _rot(v, cos, sa, sb):
    return v * cos + pltpu.roll(v, 96, 1) * sa + pltpu.roll(v, 32, 1) * sb


def _rot_t(d, cos, sa, sb):
    return d * cos + pltpu.roll(d * sa, 32, 1) + pltpu.roll(d * sb, 96, 1)


def _bmm(a, b):
    return jnp.einsum("cik,ckj->cij", a, b, preferred_element_type=F32)


def _bmm_nt(a, b):
    return jnp.einsum("cik,cjk->cij", a, b, preferred_element_type=F32)


def _bmm_tn(a, b):
    return jnp.einsum("cki,ckj->cij", a, b, preferred_element_type=F32)


def _masked_sum(mask, x):
    hi = x.astype(BF16)
    r1 = x - hi.astype(F32)
    mid = r1.astype(BF16)
    lo = (r1 - mid.astype(F32)).astype(BF16)
    return _bmm(mask, hi) + _bmm(mask, mid) + _bmm(mask, lo)


PAIR = 2


def _tile_inputs(is_ret, qkvg_refs, aux, nc):
    shape3 = (nc, CHUNK, LANES)
    q_ref, k_ref, v_ref, g_ref = qkvg_refs
    low_lanes = lax.broadcasted_iota(jnp.int32, (1, LANES), 1) < 64
    ri = lax.broadcasted_iota(jnp.int32, (PAIR * nc, CHUNK, CHUNK), 1)
    ci = lax.broadcasted_iota(jnp.int32, (PAIR * nc, CHUNK, CHUNK), 2)
    qs, ks, vs, bs, gates, extra = [], [], [], [], [], []
    for hd in range(PAIR):
        q_blk, k_blk = q_ref[...], k_ref[...]
        if hd == 1:
            q_blk, k_blk = pltpu.roll(q_blk, 64, 1), pltpu.roll(k_blk, 64, 1)
        q_raw, k_raw = jnp.where(low_lanes, q_blk, 0.0), jnp.where(low_lanes, k_blk, 0.0)
        vs.append(v_ref[:, LANES * hd:LANES * (hd + 1)].reshape(shape3))
        gates.append(g_ref[:, LANES * hd:LANES * (hd + 1)])
        if is_ret:
            cos_ref, sa_ref, sb_ref, lg_ref = aux
            cos, sa, sb = cos_ref[...], sa_ref[...], sb_ref[...]
            q = _rot(q_raw, cos, sa, sb)
            k = _rot(k_raw, cos, sa, sb) * QK_SCALE
            steps = (lax.broadcasted_iota(jnp.int32, shape3, 1) + 1).astype(F32)
            bs.append(steps * lg_ref[hd])
            extra.append(jnp.exp(jnp.abs(ri[0:nc] - ci[0:nc]).astype(F32) * lg_ref[hd][:, 0:CHUNK]))
        else:
            glow_ref, wa2_ref, ba_ref = aux
            lanes = slice(LANES * hd, LANES * (hd + 1))
            logit = _dot(glow_ref[...].astype(BF16), wa2_ref[:, lanes]) + ba_ref[:, lanes]
            la = (jnp.minimum(logit, 0.0) - jnp.log1p(jnp.exp(-jnp.abs(logit)))) * (1.0 / GATE_NORM)
            bs.append(_masked_sum((ci[0:nc] <= ri[0:nc]).astype(BF16), la.reshape(shape3)))
            extra.append(logit)
            q = q_raw * QK_SCALE
            k = k_raw
        qs.append(q.reshape(shape3))
        ks.append(k.reshape(shape3))
    cat = lambda parts: jnp.concatenate(parts, axis=0)
    return cat(qs), cat(ks), cat(vs), gates, cat(bs), extra, ri, ci


def _tile_scores(q, k, b, ri, ci):
    mid = b[:, CHUNK // 2 - 1:CHUNK // 2, :]
    ep = jnp.exp(b - mid)
    en = jnp.exp(mid - b)
    qt, kt, qh, kh = q * ep, k * en, q * en, k * ep
    low = _bmm_nt(qt.astype(BF16), kt.astype(BF16))
    upp = _bmm_nt(qh.astype(BF16), kh.astype(BF16))
    scores = jnp.where(ci <= ri, low, upp)
    return scores, ep, en, qt, kt, qh, kh


def _attn_specs(is_ret, t, tb, imap_t):
    nb = t // tb
    base = 0 if is_ret else 12
    wide = PAIR * LANES
    proj = [pl.BlockSpec((tb, LANES), lambda p, i: (imap_t(i), base + p)),
            pl.BlockSpec((tb, LANES), lambda p, i: (imap_t(i), base + 2 + p)),
            pl.BlockSpec((tb, wide), lambda p, i: (imap_t(i), (base + 4) // 2 + p)),
            pl.BlockSpec((tb, wide), lambda p, i: (imap_t(i), (base + 8) // 2 + p))]
    lane_t = pl.BlockSpec((tb, LANES), lambda p, i: (imap_t(i), 0))
    if is_ret:
        aux = [lane_t, lane_t, lane_t, pl.BlockSpec((PAIR, 1, LANES), lambda p, i: (p, 0, 0))]
    else:
        aux = [pl.BlockSpec((tb, LANES), lambda p, i: (imap_t(i), PROJ_P // LANES - 1)),
               pl.BlockSpec((LANES, wide), lambda p, i: (0, p)),
               pl.BlockSpec((1, wide), lambda p, i: (0, p))]
    gain = pl.BlockSpec((1, wide), lambda p, i: (0, p))
    pair_t = pl.BlockSpec((tb, wide), lambda p, i: (imap_t(i), p))
    narrow_t = pl.BlockSpec((tb, LANES), lambda p, i: (imap_t(i), p))
    state = pl.BlockSpec((PAIR, tb // CHUNK, LANES, LANES), lambda p, i: (p, imap_t(i), 0, 0))
    return nb, proj, aux, gain, pair_t, narrow_t, state


def _attn_fwd(is_ret, proj, aux_arrays, gain, name, hosted=None):
    t = proj.shape[0]
    tb = min(t, ATTN_TILE)
    nc = tb // CHUNK
    n_aux = 4 if is_ret else 3
    nb, proj_spec, aux_specs, gain_spec, pair_t, _, state_spec = _attn_specs(is_ret, t, tb, lambda i: i)

    def body(*refs):
        qkvg_refs = refs[0:4]
        aux = refs[4:4 + n_aux]
        gn_ref, ofin_ref, oraw_ref, st_ref, state = refs[4 + n_aux:]

        @pl.when(pl.program_id(1) == 0)
        def _():
            state[...] = jnp.zeros_like(state)

        q, k, v, gates, b, extra, ri, ci = _tile_inputs(is_ret, qkvg_refs, aux, nc)
        if is_ret:
            scores = _bmm_nt(q.astype(BF16), k.astype(BF16)) * jnp.concatenate(extra, axis=0)
        else:
            scores = _tile_scores(q, k, b, ri, ci)[0]
        vb = v.astype(BF16)
        intra = _bmm(scores.astype(BF16), vb)
        b_last = b[:, CHUNK - 1:CHUNK, :]
        e_last = jnp.exp(b_last)
        grow = _bmm_tn(vb, (k * jnp.exp(b_last - b)).astype(BF16))
        for hd in range(PAIR):
            st = state[hd]
            for c in range(nc):
                st_ref[hd, c] = st
                st = st * e_last[hd * nc + c] + grow[hd * nc + c]
            state[hd] = st
        starts = st_ref[...].reshape(PAIR * nc, LANES, LANES)
        out3 = intra + _bmm_nt((q * jnp.exp(b)).astype(BF16), starts.astype(BF16))
        for hd in range(PAIR):
            lanes = slice(LANES * hd, LANES * (hd + 1))
            out = out3[hd * nc:(hd + 1) * nc].reshape(tb, LANES)
            oraw_ref[:, lanes] = out
            normed = out * _rms_scale(out)
            gate = gates[hd]
            ofin_ref[:, lanes] = ((normed * gn_ref[:, lanes]) * (gate * jax.nn.sigmoid(gate))).astype(BF16)

    width = HEADS * LANES
    return _call(
        body, (proj, proj, proj, proj, *aux_arrays, gain), name=name, grid=(HEADS // PAIR, nb),
        in_specs=proj_spec + aux_specs + [gain_spec],
        out_specs=[pair_t, pair_t, state_spec],
        out_shape=[jax.ShapeDtypeStruct((t, width), BF16), jax.ShapeDtypeStruct((t, width), F32),
                   jax.ShapeDtypeStruct((HEADS, t // CHUNK, LANES, LANES), F32)],
        scratch_shapes=[pltpu.VMEM((PAIR, LANES, LANES), F32)],
        compiler_params=_ARB2, hosted=hosted)


def _attn_bwd(is_ret, proj, aux_arrays, gain, o_raw, states, d_out, name, hosted=None):
    t = proj.shape[0]
    tb = min(t, ATTN_TILE)
    nc = tb // CHUNK
    n_aux = 4 if is_ret else 3
    nblk = t // tb
    nb, proj_spec, aux_specs, gain_spec, pair_t, narrow_t, state_spec = _attn_specs(
        is_ret, t, tb, lambda i: nblk - 1 - i)
    base = 0 if is_ret else HEADS // PAIR
    dout_spec = pl.BlockSpec((tb, PAIR * LANES), lambda p, i: (nblk - 1 - i, base + p))

    def body(*refs):
        qkvg_refs = refs[0:4]
        aux = refs[4:4 + n_aux]
        gn_ref, oraw_ref, st_ref, dfin_ref = refs[4 + n_aux:8 + n_aux]
        dq_ref, dk_ref, dv_ref, dgate_ref, dgn_ref = refs[8 + n_aux:13 + n_aux]
        if is_ret:
            dstate, dafter_ref = refs[13 + n_aux:]
        else:
            dlogit_ref, dba_ref, dstate, dafter_ref = refs[13 + n_aux:]

        @pl.when(pl.program_id(1) == 0)
        def _():
            dstate[...] = jnp.zeros_like(dstate)
            dgn_ref[...] = jnp.zeros_like(dgn_ref)
            if not is_ret:
                dba_ref[...] = jnp.zeros_like(dba_ref)

        shape3 = (nc, CHUNK, LANES)
        q, k, v, gates, b, extra, ri, ci = _tile_inputs(is_ret, qkvg_refs, aux, nc)
        eb = jnp.exp(b)
        qe = q * eb
        b_last = b[:, CHUNK - 1:CHUNK, :]
        e_last = jnp.exp(b_last)
        ekd = jnp.exp(b_last - b)
        kd = k * ekd

        d_os = []
        for hd in range(PAIR):
            lanes = slice(LANES * hd, LANES * (hd + 1))
            gn, gate = gn_ref[:, lanes], gates[hd]
            out = oraw_ref[:, lanes]
            r = _rms_scale(out)
            normed = out * r
            sg = jax.nn.sigmoid(gate)
            dfin = dfin_ref[:, lanes]
            dgate_ref[:, lanes] = (dfin * (normed * gn) * _silu_grad(gate, sg)).astype(BF16)
            dpre = dfin * (gate * sg)
            dgn_ref[:, lanes] += jnp.sum(dpre * normed, axis=0, keepdims=True)
            dnormed = dpre * gn
            d_o = r * (dnormed - normed * jnp.mean(dnormed * normed, axis=-1, keepdims=True))
            d_os.append(d_o.reshape(shape3))
        dob, vb = jnp.concatenate(d_os, axis=0).astype(BF16), v.astype(BF16)

        dgrow = _bmm_tn(dob, qe.astype(BF16))
        for hd in range(PAIR):
            dst = dstate[hd]
            for c in reversed(range(nc)):
                dafter_ref[hd * nc + c] = dst
                dst = dst * e_last[hd * nc + c] + dgrow[hd * nc + c]
            dstate[hd] = dst
        st = st_ref[...].reshape(PAIR * nc, LANES, LANES)
        dafter = dafter_ref[...]
        stb, dafter_b = st.astype(BF16), dafter.astype(BF16)

        dsc = _bmm_nt(dob, vb)
        dsc_t = _bmm_nt(vb, dob)
        dqe = _bmm(dob, stb)
        dkd = _bmm(vb, dafter_b)
        if is_ret:
            decay, qb, kb = jnp.concatenate(extra, axis=0), q.astype(BF16), k.astype(BF16)
            scores_t = _bmm_nt(kb, qb) * decay
            dq = _bmm((dsc * decay).astype(BF16), kb) + dqe * eb
            dk = _bmm((dsc_t * decay).astype(BF16), qb) + dkd * ekd
        else:
            _, ep, en, qt, kt, qh, kh = _tile_scores(q, k, b, ri, ci)
            qtb, ktb, qhb, khb = qt.astype(BF16), kt.astype(BF16), qh.astype(BF16), kh.astype(BF16)
            scores_t = jnp.where(ci >= ri, _bmm_nt(ktb, qtb), _bmm_nt(khb, qhb))
            dqt = _bmm(jnp.where(ci <= ri, dsc, 0.0).astype(BF16), ktb)
            dqh = _bmm(jnp.where(ci <= ri, 0.0, dsc).astype(BF16), khb)
            dkt = _bmm(jnp.where(ci >= ri, dsc_t, 0.0).astype(BF16), qtb)
            dkh = _bmm(jnp.where(ci >= ri, 0.0, dsc_t).astype(BF16), qhb)
            dq = dqt * ep + dqh * en + dqe * eb
            dk = dkt * en + dkh * ep + dkd * ekd
        dv = _bmm(scores_t.astype(BF16), dob) + _bmm_nt(kd.astype(BF16), dafter_b)

        if not is_ret:
            db = dqt * qt - dkt * kt - dqh * qh + dkh * kh + dqe * qe - dkd * kd
            db_last = (jnp.sum(dkd * kd, axis=1, keepdims=True)
                       + jnp.sum(dafter * st, axis=1, keepdims=True) * e_last)
            last_row = lax.broadcasted_iota(jnp.int32, (PAIR * nc, CHUNK, LANES), 1) == CHUNK - 1
            db = db + jnp.where(last_row, db_last, 0.0)
            dla = _masked_sum((ci >= ri).astype(BF16), db)

        dq_pair, dk_pair = [], []
        for hd in range(PAIR):
            lanes = slice(LANES * hd, LANES * (hd + 1))
            rows3 = slice(hd * nc, (hd + 1) * nc)
            dq_h, dk_h = dq[rows3].reshape(tb, LANES), dk[rows3].reshape(tb, LANES)
            if is_ret:
                cos_ref, sa_ref, sb_ref, _ = aux
                cos, sa, sb = cos_ref[...], sa_ref[...], sb_ref[...]
                dq_h = _rot_t(dq_h, cos, sa, sb)
                dk_h = _rot_t(dk_h, cos, sa, sb) * QK_SCALE
            else:
                dq_h = dq_h * QK_SCALE
                dlogit = dla[rows3].reshape(tb, LANES) * (1.0 / GATE_NORM) * jax.nn.sigmoid(-extra[hd])
                dlogit_ref[:, lanes] = dlogit.astype(BF16)
                dba_ref[:, lanes] += jnp.sum(dlogit, axis=0, keepdims=True)
            dq_pair.append(dq_h)
            dk_pair.append(dk_h)
            dv_ref[:, lanes] = dv[rows3].reshape(tb, LANES).astype(BF16)
        dq_ref[...] = (dq_pair[0] + pltpu.roll(dq_pair[1], 64, 1)).astype(BF16)
        dk_ref[...] = (dk_pair[0] + pltpu.roll(dk_pair[1], 64, 1)).astype(BF16)

    width = HEADS * LANES
    row_out = pl.BlockSpec((1, PAIR * LANES), lambda p, i: (0, p))
    out_specs = [narrow_t, narrow_t, pair_t, pair_t, row_out]
    out_shape = ([jax.ShapeDtypeStruct((t, width // 2), BF16)] * 2 + [jax.ShapeDtypeStruct((t, width), BF16)] * 2
                 + [jax.ShapeDtypeStruct((1, width), F32)])
    if not is_ret:
        out_specs += [pair_t, row_out]
        out_shape += [jax.ShapeDtypeStruct((t, width), BF16), jax.ShapeDtypeStruct((1, width), F32)]
    return _call(
        body, (proj, proj, proj, proj, *aux_arrays, gain, o_raw, states, d_out), name=name,
        grid=(HEADS // PAIR, nblk),
        in_specs=proj_spec + aux_specs + [gain_spec, pair_t, state_spec, dout_spec],
        out_specs=out_specs, out_shape=out_shape,
        scratch_shapes=[pltpu.VMEM((PAIR, LANES, LANES), F32), pltpu.VMEM((PAIR * nc, LANES, LANES), F32)],
        compiler_params=_ARB2, hosted=hosted)


PEER_SLOT = (2, 1, 3)


def _place():
    x, y, c = lax.axis_index("x"), lax.axis_index("y"), lax.axis_index("c")
    chips = [(1 - x, y), (x, 1 - y), (1 - x, 1 - y)]
    return x, y, c, 2 * x + y, chips


def _route_split(rows, dtype):
    tile = 16 if dtype == BF16 else 8
    if rows < 2 * tile:
        return None
    return -(-(rows // 2) // tile) * tile


def _routes(by_peer):
    x, y, c, me, chips = _place()
    (xx, xy), (yx, yy), (dx, dy) = chips
    if by_peer:
        slots = dict(own=0, from_x=PEER_SLOT[0], from_y=PEER_SLOT[1], diag=PEER_SLOT[2],
                     mine_on_x=PEER_SLOT[0], mine_on_y=PEER_SLOT[1])
    else:
        slots = dict(own=me, from_x=2 * xx + xy, from_y=2 * yx + yy, diag=2 * dx + dy, mine_on_x=me, mine_on_y=me)
    return c, (xx, xy, c), (yx, yy, c), (dx, dy, c), (x, y, 1 - c), slots


def _gather_legs(src, out, send_sems, recv_sems, base, by_peer):
    c, to_x, to_y, to_d, sibling, s = _routes(by_peer)
    r0 = _route_split(src.shape[2], src.dtype)

    def cp(k, src_ref, dst_ref, to):
        return pltpu.make_async_remote_copy(src_ref=src_ref, dst_ref=dst_ref, send_sem=send_sems.at[base + k],
                                            recv_sem=recv_sems.at[base + k], device_id=to, device_id_type=MESH)

    mine = src.at[:, c]
    legs = dict(
        x=(cp(0, mine, out.at[s["mine_on_x"], :, c], to_x), cp(0, mine, out.at[s["from_x"], :, c], to_x)),
        y=(cp(1, mine, out.at[s["mine_on_y"], :, c], to_y), cp(1, mine, out.at[s["from_y"], :, c], to_y)),
        pass_x=(cp(4, out.at[s["from_x"], :, c], out.at[s["from_x"], :, c], sibling),
                cp(4, mine, out.at[s["from_x"], :, 1 - c], sibling)),
        pass_y=(cp(5, out.at[s["from_y"], :, c], out.at[s["from_y"], :, c], sibling),
                cp(5, mine, out.at[s["from_y"], :, 1 - c], sibling)),
        own=(cp(8, src, out.at[s["own"]], sibling), cp(8, src, out.at[s["own"]], sibling)))
    if r0 is None:
        mine_on_d = s["diag"] if by_peer else s["own"]
        legs["d"] = (cp(2, mine, out.at[mine_on_d, :, c], to_d), cp(2, mine, out.at[s["diag"], :, c], to_d))
        legs["pass_d"] = (cp(6, out.at[s["diag"], :, c], out.at[s["diag"], :, c], sibling),
                          cp(6, mine, out.at[s["diag"], :, 1 - c], sibling))
        return legs, False
    lo, hi = pl.ds(0, r0), pl.ds(r0, src.shape[2] - r0)
    fx_on_y = s["diag"] if by_peer else s["from_x"]
    fy_on_x = s["diag"] if by_peer else s["from_y"]
    legs.update(
        fwd_y=(cp(2, out.at[s["from_x"], :, c, lo], out.at[fx_on_y, :, c, lo], to_y),
               cp(2, mine.at[:, lo], out.at[s["diag"], :, c, lo], to_y)),
        fwd_x=(cp(3, out.at[s["from_y"], :, c, hi], out.at[fy_on_x, :, c, hi], to_x),
               cp(3, mine.at[:, hi], out.at[s["diag"], :, c, hi], to_x)),
        pass_d0=(cp(6, out.at[s["diag"], :, c, lo], out.at[s["diag"], :, c, lo], sibling),
                 cp(6, mine.at[:, lo], out.at[s["diag"], :, 1 - c, lo], sibling)),
        pass_d1=(cp(7, out.at[s["diag"], :, c, hi], out.at[s["diag"], :, c, hi], sibling),
                 cp(7, mine.at[:, hi], out.at[s["diag"], :, 1 - c, hi], sibling)))
    return legs, True


def _gather_steps(legs, routed):
    def start():
        legs["x"][0].start()
        legs["y"][0].start()
        legs["own"][0].start()
        if not routed:
            legs["d"][0].start()

    def middle():
        legs["x"][1].wait_recv()
        if routed:
            legs["fwd_y"][0].start()
        legs["pass_x"][0].start()
        legs["y"][1].wait_recv()
        if routed:
            legs["fwd_x"][0].start()
        legs["pass_y"][0].start()

    last = ["pass_d0", "pass_d1"] if routed else ["pass_d"]

    def late():
        if routed:
            legs["fwd_y"][1].wait_recv()
            legs["pass_d0"][0].start()
            legs["fwd_x"][1].wait_recv()
            legs["pass_d1"][0].start()
        else:
            legs["d"][1].wait_recv()
            legs["pass_d"][0].start()

    def finish():
        for name in ["own", "pass_x", "pass_y"] + last:
            legs[name][1].wait_recv()
        for name in ["x", "y", "own", "pass_x", "pass_y"] + last + (["fwd_y", "fwd_x"] if routed else ["d"]):
            legs[name][0].wait_send()

    return start, middle, late, finish


def _gather_plan(arrs):
    na = len(arrs)

    def steps(ins, outs, send_sems, recv_sems):
        return [_gather_steps(*_gather_legs(ins[a], outs[a], send_sems, recv_sems, 9 * a, False)) for a in range(na)]

    def run(which):
        def hook(*refs):
            for step in steps(*refs):
                step[which]()
        return hook

    routed = all(_route_split(a.shape[2], a.dtype) is not None for a in arrs)
    return _Hosted(arrs, [jax.ShapeDtypeStruct((N_CHIPS,) + a.shape, a.dtype) for a in arrs], 9 * na,
                   run(0), run(3), middle=run(1), late=run(2), peers="neighbours" if routed else "chips_sibling")


def _pair_exchange_plan(grads):
    na = len(grads)

    def copies(ins, outs, send_sems, recv_sems):
        x, y, c, _, _ = _place()
        return [pltpu.make_async_remote_copy(
            src_ref=ins[a].at[:, 1 - c], dst_ref=outs[a], send_sem=send_sems.at[a], recv_sem=recv_sems.at[a],
            device_id=(x, y, 1 - c), device_id_type=MESH) for a in range(na)]

    def start(*refs):
        for cp in copies(*refs):
            cp.start()

    def finish(*refs):
        for cp in copies(*refs):
            cp.wait()

    return _Hosted(grads, [jax.ShapeDtypeStruct(g.shape[:1] + g.shape[2:], g.dtype) for g in grads], na, start, finish,
                   peers="sibling")


def _small_gather_plan(block):
    def copies(ins, outs, send_sems, recv_sems):
        x, y, c, _, chips = _place()
        peers = [(x, y, 1 - c)] + [(px, py, pc) for px, py in chips for pc in (c, 1 - c)]
        sends = [pltpu.make_async_remote_copy(
            src_ref=ins[0], dst_ref=outs[0].at[4 * x + 2 * y + c], send_sem=send_sems.at[k], recv_sem=recv_sems.at[k],
            device_id=peer, device_id_type=MESH) for k, peer in enumerate(peers)]
        recvs = [pltpu.make_async_remote_copy(
            src_ref=ins[0], dst_ref=outs[0].at[4 * px + 2 * py + pc], send_sem=send_sems.at[k], recv_sem=recv_sems.at[k],
            device_id=(px, py, pc), device_id_type=MESH) for k, (px, py, pc) in enumerate(peers)]
        return sends, recvs

    def start(*refs):
        for cp in copies(*refs)[0]:
            cp.start()

    def finish(*refs):
        sends, recvs = copies(*refs)
        for cp in recvs:
            cp.wait_recv()
        for cp in sends:
            cp.wait_send()

    return _Hosted([block], [jax.ShapeDtypeStruct((8,) + block.shape, block.dtype)], 7, start, finish)


def _sum_devices(blocks):
    def body(b_ref, o_ref):
        acc = b_ref[0]
        for d in range(1, 8):
            acc = acc + b_ref[d]
        o_ref[...] = acc

    return pl.pallas_call(body, name="sum_devices", in_specs=[_VMEM], out_specs=_VMEM,
                          out_shape=jax.ShapeDtypeStruct(blocks.shape[1:], blocks.dtype))(blocks)


def _pair_add(grad, recv, c_arr, name):
    _, _, r, cols = grad.shape

    def body(c_ref, g_ref, r_ref, o_ref):
        o_ref[...] = (g_ref[...].astype(F32) + r_ref[...].astype(F32)).astype(BF16)

    return pl.pallas_call(
        body, name=name,
        grid_spec=pltpu.PrefetchScalarGridSpec(
            num_scalar_prefetch=1, grid=(N_CHIPS,),
            in_specs=[pl.BlockSpec((None, None, r, cols), lambda p, c_ref: (p, c_ref[0], 0, 0)),
                      pl.BlockSpec((None, r, cols), lambda p, c_ref: (p, 0, 0))],
            out_specs=pl.BlockSpec((None, r, cols), lambda p, c_ref: (p, 0, 0))),
        out_shape=jax.ShapeDtypeStruct((N_CHIPS, r, cols), BF16),
        compiler_params=_ARB1,
    )(c_arr, grad, recv)


def _chip_exchange_plan(sums, by_peer=False):
    na = len(sums)

    def copies(ins, outs, send_sems, recv_sems):
        x, y, c, me, chips = _place()

        def copy(a, j, px, py, block, slot):
            return pltpu.make_async_remote_copy(
                src_ref=ins[a].at[block], dst_ref=outs[a].at[slot],
                send_sem=send_sems.at[3 * a + j], recv_sem=recv_sems.at[3 * a + j],
                device_id=(px, py, c), device_id_type=MESH)

        peers = [(a, j, px, py) for a in range(na) for j, (px, py) in enumerate(chips)]
        return me, peers, copy

    def start(*refs):
        me, peers, copy = copies(*refs)
        for a, j, px, py in peers:
            if by_peer:
                copy(a, j, px, py, PEER_SLOT[j], PEER_SLOT[j]).start()
            else:
                copy(a, j, px, py, 2 * px + py, me).start()

    def finish(*refs):
        me, peers, copy = copies(*refs)
        for a, j, px, py in peers:
            if by_peer:
                copy(a, j, px, py, PEER_SLOT[j], PEER_SLOT[j]).wait_recv()
            else:
                copy(a, j, px, py, me, 2 * px + py).wait_recv()
        for a, j, px, py in peers:
            if by_peer:
                copy(a, j, px, py, PEER_SLOT[j], PEER_SLOT[j]).wait_send()
            else:
                copy(a, j, px, py, 2 * px + py, me).wait_send()

    return _Hosted(sums, [jax.ShapeDtypeStruct(s.shape, s.dtype) for s in sums], 3 * na, start, finish, peers="chips")


def _chip_sum(own, recv, me_arr, name):
    _, r, cols = recv.shape

    def body(me_ref, own_ref, r_ref, o_ref):
        o_ref[...] = jnp.zeros_like(o_ref)
        for q in range(N_CHIPS):
            @pl.when(me_ref[0] == q)
            def _():
                o_ref[...] += own_ref[...].astype(F32)

            @pl.when(me_ref[0] != q)
            def _():
                o_ref[...] += r_ref[q].astype(F32)

    return pl.pallas_call(
        body, name=name,
        grid_spec=pltpu.PrefetchScalarGridSpec(
            num_scalar_prefetch=1, grid=(1,),
            in_specs=[pl.BlockSpec((None, r, cols), lambda i, me_ref: (me_ref[0], 0, 0)),
                      pl.BlockSpec((N_CHIPS, r, cols), lambda i, me_ref: (0, 0, 0))],
            out_specs=pl.BlockSpec((r, cols), lambda i, me_ref: (0, 0))),
        out_shape=jax.ShapeDtypeStruct((r, cols), F32),
        compiler_params=_ARB1,
    )(me_arr, own, recv)


def _peer_sum(own, recv, name):
    _, r, cols = recv.shape

    def body(own_ref, r_ref, o_ref):
        acc = own_ref[...].astype(F32) + r_ref[1].astype(F32)
        acc = acc + r_ref[2].astype(F32)
        o_ref[...] = acc + r_ref[3].astype(F32)

    return pl.pallas_call(
        body, name=name, grid=(1,),
        in_specs=[pl.BlockSpec((None, r, cols), lambda i: (0, 0, 0)), pl.BlockSpec((N_CHIPS, r, cols), lambda i: (0, 0, 0))],
        out_specs=pl.BlockSpec((r, cols), lambda i: (0, 0)),
        out_shape=jax.ShapeDtypeStruct((r, cols), F32),
        compiler_params=_ARB1,
    )(own, recv)


def _pair_share_plan(halves):
    na = len(halves)

    def copies(ins, outs, send_sems, recv_sems):
        x, y, c, _, _ = _place()
        return [pltpu.make_async_remote_copy(
            src_ref=ins[a], dst_ref=outs[a], send_sem=send_sems.at[a], recv_sem=recv_sems.at[a],
            device_id=(x, y, 1 - c), device_id_type=MESH) for a in range(na)]

    def start(*refs):
        for cp in copies(*refs):
            cp.start()

    def finish(*refs):
        for cp in copies(*refs):
            cp.wait()

    return _Hosted(halves, [jax.ShapeDtypeStruct(h.shape, h.dtype) for h in halves], na, start, finish,
                   peers="sibling")


def _row_tile(rows):
    best = rows
    for cand in range(8, min(rows, 512) + 1, 8):
        if rows % cand == 0:
            best = cand
    return best


def _adamw_math(w, g, m, v):
    m2 = ADAM_B1 * m + (1.0 - ADAM_B1) * g
    v2 = ADAM_B2 * v + (1.0 - ADAM_B2) * (g * g)
    m_hat = m2 / (1.0 - ADAM_B1 ** ADAM_STEP)
    v_hat = v2 / (1.0 - ADAM_B2 ** ADAM_STEP)
    return -ADAM_LR * (m_hat / (jnp.sqrt(v_hat) + ADAM_EPS) + ADAM_WD * w), m2, v2


def _adamw_halves(w, g_mine, g_other, m, v, c_arr, name):
    rows, cols = w.shape
    r = rows // 2
    tr = _row_tile(r)
    nt = r // tr

    def body(c_ref, w_ref, gm_ref, go_ref, m_ref, v_ref, g_ref, d_ref, nm_ref, nv_ref):
        gv = jnp.where(pl.program_id(0) == c_ref[0], gm_ref[...], go_ref[...])
        g_ref[...] = gv
        d_ref[...], nm_ref[...], nv_ref[...] = _adamw_math(w_ref[...], gv, m_ref[...], v_ref[...])

    full = pl.BlockSpec((tr, cols), lambda h, i, c_ref: (h * nt + i, 0))
    half = pl.BlockSpec((tr, cols), lambda h, i, c_ref: (i, 0))
    shape = jax.ShapeDtypeStruct((rows, cols), F32)
    return pl.pallas_call(
        body, name=name,
        grid_spec=pltpu.PrefetchScalarGridSpec(
            num_scalar_prefetch=1, grid=(2, nt),
            in_specs=[full, half, half, full, full], out_specs=[full] * 4),
        out_shape=[shape] * 4,
        compiler_params=_ARB2,
    )(c_arr, w, g_mine, g_other, m, v)


def _pad_w_in_t(w_in_t):
    return jnp.pad(w_in_t, ((0, PROJ_P - IN_WIDTH), (0, 0)))


def _unpad_w_in_t(w_pt):
    return w_pt[0:IN_WIDTH]


def _rope_tables(t):
    half = 32
    inv = ROPE_BASE ** (-jnp.arange(half, dtype=F32) * 2.0 / 64)
    ang = jnp.arange(t, dtype=F32)[:, None] * inv[None, :]
    cos, sin = jnp.cos(ang), jnp.sin(ang)
    z32, z64 = jnp.zeros((t, 32), F32), jnp.zeros((t, 64), F32)
    return (jnp.concatenate([cos, cos, z64], axis=1),
            jnp.concatenate([-sin, z32, z64], axis=1),
            jnp.concatenate([z32, sin, z64], axis=1))


def _halves(w):
    n, rows, cols = w.shape
    return w.reshape(n, 2, rows // 2, cols)


_VMEM = pl.BlockSpec(memory_space=pltpu.VMEM)


def _pack_small(n1, nm, n2, nf, nret, ngla, ba, wa2_p, loss_blk):
    def body(n1_ref, nm_ref, n2_ref, nf_ref, nret_ref, ngla_ref, ba_ref, wa2_ref, loss_ref, o_ref):
        o_ref[...] = jnp.zeros_like(o_ref)
        o_ref[0:1, :] = n1_ref[...]
        o_ref[1:2, :] = nm_ref[...]
        o_ref[2:3, :] = n2_ref[...]
        o_ref[3:4, :] = nf_ref[...]
        o_ref[4:5, 0:512] = nret_ref[...]
        o_ref[4:5, 512:1024] = ngla_ref[...]
        o_ref[5:6, 0:256] = ba_ref[...]
        o_ref[6:7, 0:LANES] = loss_ref[0:1, :]
        o_ref[8:8 + GATE_RANK, 0:HEADS * LANES] = wa2_ref[0:GATE_RANK, :]

    return pl.pallas_call(
        body, name="pack_small", in_specs=[_VMEM] * 9, out_specs=_VMEM,
        out_shape=jax.ShapeDtypeStruct((SMALL_ROWS, D_MODEL), F32),
    )(n1, nm, n2, nf, nret, ngla, ba, wa2_p, loss_blk)


def _small_update(summed, chip_arr, ws, ms, vs):
    n = len(ws)

    def body(chip_ref, s_ref, *refs):
        w_refs, m_refs, v_refs = refs[0:n], refs[n:2 * n], refs[2 * n:3 * n]
        outs = refs[3 * n:]
        wa2_all = s_ref[8:8 + GATE_RANK, 0:HEADS * LANES]
        wa2_g = jnp.zeros((GATE_RANK, 64), F32)
        for p in range(N_CHIPS):
            wa2_g = jnp.where(chip_ref[0] == p, wa2_all[:, LANES * p:LANES * p + 64], wa2_g)
        grads = [s_ref[0:1, :], s_ref[1:2, :], s_ref[2:3, :], s_ref[3:4, :], s_ref[4:5, 0:512],
                 s_ref[4:5, 512:1024], s_ref[5:6, 0:256], wa2_g]
        for k in range(n):
            d, m2, v2 = _adamw_math(w_refs[k][...], grads[k], m_refs[k][...], v_refs[k][...])
            outs[k][...] = grads[k]
            outs[n + k][...] = d
            outs[2 * n + k][...] = m2
            outs[3 * n + k][...] = v2

    shapes = [jax.ShapeDtypeStruct(w.shape, F32) for w in ws] * 4
    smem = pl.BlockSpec(memory_space=pltpu.SMEM)
    outs = pl.pallas_call(
        body, name="small_update", in_specs=[smem] + [_VMEM] * (1 + 3 * n), out_specs=[_VMEM] * (4 * n),
        out_shape=shapes,
    )(chip_arr, summed, *ws, *ms, *vs)
    return outs[0:n], outs[n:2 * n], outs[2 * n:3 * n], outs[3 * n:4 * n]


def _pad_in_rows(w_t):
    return jnp.pad(w_t, ((0, IN_ROWS - IN_SHARD), (0, 0)))


def _forward_backward(xs, target, ffn1_w, rest, ba_p, ffn1_norm_g, mix_norm_g, ret_norm_g, gla_norm_g, ffn2_norm_g,
                      final_norm_g, ffn1_gather=None, rest_plan=None, rest_weights=None, ffn2_plans=None,
                      ffn2_weights=None, ffn2_pairs=None, ffn2_pairs_done=None, early=None, late=None, small_plan=None):
    t = xs.shape[0]
    cos_t, sa_t, sb_t = _rope_tables(t)
    log_gamma = jnp.log(1.0 - 2.0 ** (-5.0 - jnp.arange(HEADS, dtype=F32)))
    lg_t = jnp.broadcast_to(log_gamma[:, None, None], (HEADS, 1, LANES))
    ret_aux = [cos_t, sa_t, sb_t, lg_t]

    if ffn1_gather is None:
        (x1, a1, u1, h1), gathered = _ffn_fwd(xs, ffn1_norm_g, ffn1_w, "ffn1_fwd", hosted=rest_plan)
    else:
        ffn1_shard, ffn1_weights = ffn1_gather
        (x1, a1, u1, h1, wall), gathered = _ffn1_fwd_gathering(xs, ffn1_norm_g, ffn1_shard, "ffn1_fwd",
                                                               hosted=rest_plan)
        ffn1_w = ffn1_weights(wall)
    ffn2_w, w_in_pt, w_out_full, wa2_p = rest if rest_plan is None else rest_weights(gathered)
    plans = [None] * 3 if ffn2_plans is None else ffn2_plans
    (proj, h_mix), got_gate = _mixer_in_fwd(x1, mix_norm_g, w_in_pt, "mixer_in_fwd", hosted=plans[0])
    gla_aux = [proj, wa2_p, ba_p]
    (o_ret, raw_ret, st_ret), got_up = _attn_fwd(True, proj, ret_aux, ret_norm_g, "ret_fwd", hosted=plans[1])
    (o_gla, raw_gla, st_gla), got_down = _attn_fwd(False, proj, gla_aux, gla_norm_g, "gla_fwd", hosted=plans[2])
    if ffn2_plans is not None:
        ffn2_w = ffn2_weights(got_gate + got_up + got_down)
    x2 = _mixer_out_fwd(o_ret, o_gla, w_out_full, x1, "mixer_out_fwd")
    (loss_blk, dx3, d_final_g, a2, u2, h2), _ = _ffn_fwd(x2, ffn2_norm_g, ffn2_w, "ffn2_fwd",
                                                       loss_head=(final_norm_g, target))

    (da2, du2, hid2, dob2, dx2, d_ffn2_g, d_o), _ = _ffn_bwd(dx3, x2, ffn2_norm_g, a2, u2, ffn2_w, "ffn2_bwd",
                                                            back_w=w_out_full)
    g_gate2 = _matmul_tn(da2, h2, "ffn2_dgate", out_dtype=BF16)
    g_up2 = _matmul_tn(du2, h2, "ffn2_dup", out_dtype=BF16)
    g_down2 = _matmul_tn(hid2, dob2, "ffn2_ddown", out_dtype=BF16)

    g_wout_ret = _matmul_tn(o_ret, dx2, "wout_grad_ret", out_dtype=BF16)
    g_wout_gla = _matmul_tn(o_gla, dx2, "wout_grad_gla", out_dtype=BF16)
    pairs_plan = None if ffn2_pairs is None else ffn2_pairs([g_gate2, g_up2, g_down2])
    (*dproj_ret, d_ret_g), pair_recv = _attn_bwd(True, proj, ret_aux, ret_norm_g, raw_ret, st_ret, d_o, "ret_bwd",
                                                 hosted=pairs_plan)
    if ffn2_pairs is not None:
        ffn2_pairs_done(pair_recv)
    (*dproj_gla, d_gla_g, dlogit, d_ba_p), _ = _attn_bwd(False, proj, gla_aux, gla_norm_g, raw_gla, st_gla, d_o,
                                                        "gla_bwd")
    d_glow = _matmul_nt(dlogit, wa2_p, "gate_low_bwd", out_dtype=BF16)
    g_wa2_p = _matmul_tn(proj[:, PROJ_P - LANES:], dlogit, "gate_w_grad")
    dproj = jnp.concatenate(dproj_ret + dproj_gla + [d_glow], axis=1)
    g_win_p = _matmul_tn(dproj, h_mix, "w_in_grad", tka=PROJ_P // PROJ_TILES, out_dtype=BF16)
    dx1, d_mix_g = _mixer_in_bwd(dproj, w_in_pt, dx2, x1, mix_norm_g, "mixer_in_bwd")
    g_win_t = _unpad_w_in_t(g_win_p[0])
    g_win = jnp.stack([_pad_in_rows(g_win_t[IN_SHARD * p:IN_SHARD * (p + 1)]) for p in range(N_CHIPS)], axis=0)
    g_wout = jnp.concatenate([g_wout_ret[0], g_wout_gla[0]], axis=0).reshape(N_CHIPS, D_MODEL // N_CHIPS, D_MODEL)

    early_grads = [g_win, g_wout] if ffn2_pairs is not None else [g_gate2, g_up2, g_down2, g_win, g_wout]
    early_plan = None if early is None else early(early_grads)
    (da1, du1, hid1, dob1, grad_x, d_ffn1_g), arrived = _ffn_bwd(dx1, xs, ffn1_norm_g, a1, u1, ffn1_w, "ffn1_bwd",
                                                                hosted=early_plan)
    d_ba = d_ba_p.reshape(HEADS, LANES)[:, 0:64].reshape(1, 256)
    small_local = _pack_small(d_ffn1_g, d_mix_g, d_ffn2_g, d_final_g, d_ret_g, d_gla_g, d_ba, g_wa2_p[0], loss_blk)
    late_grads, late_arrived = [], []
    for lhs, rhs, name in ((da1, h1, "ffn1_dgate"), (du1, h1, "ffn1_dup"), (hid1, dob1, "ffn1_ddown")):
        if late is None:
            plan = None
        else:
            plan = late(late_grads[-1], len(late_grads)) if late_grads else small_plan(small_local)
        res = _matmul_tn(lhs, rhs, name, out_dtype=BF16, hosted=plan)
        if plan is not None:
            res, carried = res
            late_arrived += carried
        late_grads.append(res)
    g_gate1, g_up1, g_down1 = late_grads

    return (small_local, grad_x, g_gate1, g_up1, g_down1, g_gate2, g_up2, g_down2, g_win, g_wout, g_wa2_p,
            d_ba_p, d_ffn1_g, d_mix_g, d_ffn2_g, d_final_g, d_ret_g, d_gla_g, arrived, late_arrived)


def kernel(x, ffn1_norm_g, ffn1_w_gate, ffn1_w_up, ffn1_w_down, mix_norm_g, w_in, ret_norm_g, gla_w_a2, gla_b_a, gla_norm_g, w_out, ffn2_norm_g, ffn2_w_gate, ffn2_w_up, ffn2_w_down, final_norm_g, loss_target, m_ffn1_norm_g, m_ffn1_w_gate, m_ffn1_w_up, m_ffn1_w_down, m_mix_norm_g, m_w_in, m_ret_norm_g, m_gla_w_a2, m_gla_b_a, m_gla_norm_g, m_w_out, m_ffn2_norm_g, m_ffn2_w_gate, m_ffn2_w_up, m_ffn2_w_down, m_final_norm_g, v_ffn1_norm_g, v_ffn1_w_gate, v_ffn1_w_up, v_ffn1_w_down, v_mix_norm_g, v_w_in, v_ret_norm_g, v_gla_w_a2, v_gla_b_a, v_gla_norm_g, v_w_out, v_ffn2_norm_g, v_ffn2_w_gate, v_ffn2_w_up, v_ffn2_w_down, v_final_norm_g):
    t = x.shape[1]
    xs = x.reshape(t, D_MODEL)
    target = loss_target.reshape(t, D_MODEL)
    chip = 2 * lax.axis_index("x") + lax.axis_index("y")
    c_arr = lax.axis_index("c").astype(jnp.int32).reshape(1)

    me_arr = chip.astype(jnp.int32).reshape(1)

    pad_rows = _pad_in_rows

    ffn1_shard = _halves(jnp.stack([ffn1_w_gate[0].T, ffn1_w_up[0].T, ffn1_w_down[0]], axis=0).astype(BF16))
    rest_shards = [_halves(pad_rows(w_in[0].T).astype(BF16)[None]),
                   _halves(w_out.astype(BF16)),
                   jnp.concatenate([gla_w_a2.reshape(GATE_RANK, 64), jnp.zeros((GATE_RANK, 64), F32)],
                                   axis=1).reshape(1, 2, 8, LANES)]
    ffn2_shards = [_halves(w.astype(BF16)[None]) for w in (ffn2_w_gate[0].T, ffn2_w_up[0].T, ffn2_w_down[0])]
    def ffn1_weights(gathered):
        return gathered.reshape(N_CHIPS, 3, FF_SHARD, D_MODEL)

    def rest_weights(gathered):
        win_all, wout_all, wa2_all = gathered
        win_t = win_all.reshape(N_CHIPS, IN_ROWS, D_MODEL)
        w_in_pt = jnp.zeros((PROJ_P, D_MODEL), BF16)
        for p in range(N_CHIPS):
            w_in_pt = lax.dynamic_update_slice(w_in_pt, win_t[p, 0:IN_SHARD], (IN_SHARD * p, 0))
        wa2_p = jnp.pad(
            wa2_all.reshape(N_CHIPS, GATE_RANK, LANES).transpose(1, 0, 2).reshape(GATE_RANK, HEADS * LANES),
            ((0, LANES - GATE_RANK), (0, 0))).astype(BF16)
        return (None, w_in_pt, wout_all.reshape(D_MODEL, D_MODEL), wa2_p)

    def ffn2_weights(gathered):
        return [g.reshape(N_CHIPS, FF_SHARD, D_MODEL) for g in gathered]

    def by_halves(g):
        return g.reshape(g.shape[0], 2, g.shape[1] // 2, g.shape[2])

    def pair_adds(halves, recv, tag):
        return [_pair_add(g, r, c_arr, "pair_add_%s%d" % (tag, k)) for k, (g, r) in enumerate(zip(halves, recv))]

    def pair_sums(grads, tag):
        halves = [by_halves(g) for g in grads]
        recv = _run_hosted(_pair_exchange_plan(halves), "pair_exchange_" + tag)
        return pair_adds(halves, recv, tag)

    early_sums, ffn2_halves = [], []

    def ffn2_pairs(grads):
        ffn2_halves.extend(by_halves(g) for g in grads)
        return _pair_exchange_plan(ffn2_halves)

    def ffn2_pairs_done(recv):
        early_sums.extend(pair_adds(ffn2_halves, recv, "ffn2_"))

    def early(grads):
        early_sums.extend(pair_sums(grads, "early"))
        return _chip_exchange_plan(early_sums)

    late_sums = []

    def late(grad, number):
        late_sums.extend(pair_sums([grad], "late%d" % number))
        return _chip_exchange_plan(late_sums[-1:], by_peer=True)

    ba_p = jnp.pad(gla_b_a.reshape(HEADS, 64), ((0, 0), (0, 64))).reshape(1, HEADS * LANES)
    fb = _forward_backward(xs, target, None, None, ba_p, ffn1_norm_g, mix_norm_g, ret_norm_g, gla_norm_g,
                           ffn2_norm_g, final_norm_g.reshape(1, D_MODEL), ffn1_gather=(ffn1_shard, ffn1_weights),
                           rest_plan=_gather_plan(rest_shards), rest_weights=rest_weights,
                           ffn2_plans=[_gather_plan(ffn2_shards[0:2]), None, _gather_plan(ffn2_shards[2:3])],
                           ffn2_weights=ffn2_weights,
                           ffn2_pairs=ffn2_pairs, ffn2_pairs_done=ffn2_pairs_done, early=early, late=late,
                           small_plan=_small_gather_plan)
    (small_local, grad_x, _, _, g_down1, _, _, _, _, _, _, _, _, _, _, _, _, _, early_arrived, late_arrived) = fb
    small_all, late_arrived = late_arrived[0], late_arrived[1:]
    late_arrived = late_arrived + _run_hosted(late(g_down1, 3), "chip_exchange_late")
    mine = [_peer_sum(s, r, "chip_sum_%d" % k) for k, (s, r) in enumerate(zip(late_sums, late_arrived))]
    mine += [_chip_sum(s, r, me_arr, "chip_sum_%d" % (3 + k)) for k, (s, r) in enumerate(zip(early_sums, early_arrived))]
    other = _run_hosted(_pair_share_plan(mine), "pair_share")

    device = 2 * chip + lax.axis_index("c")
    small_sum = _sum_devices(lax.dynamic_update_slice(small_all, small_local[None], (device, 0, 0)))
    loss = small_sum[6, 0]

    def rows(n1, nm, n2, nf, nret, ngla, ba, wa2):
        return [n1, nm, n2, nf.reshape(1, D_MODEL), nret, ngla, ba, wa2.reshape(GATE_RANK, 64)]

    small = _small_update(
        small_sum, me_arr,
        rows(ffn1_norm_g, mix_norm_g, ffn2_norm_g, final_norm_g, ret_norm_g, gla_norm_g, gla_b_a, gla_w_a2),
        rows(m_ffn1_norm_g, m_mix_norm_g, m_ffn2_norm_g, m_final_norm_g, m_ret_norm_g, m_gla_norm_g, m_gla_b_a,
             m_gla_w_a2),
        rows(v_ffn1_norm_g, v_mix_norm_g, v_ffn2_norm_g, v_final_norm_g, v_ret_norm_g, v_gla_norm_g, v_gla_b_a,
             v_gla_w_a2))
    s_grad, s_delta, s_m, s_v = [
        [*o[0:3], o[3].reshape(D_MODEL), *o[4:7], o[7].reshape(1, GATE_RANK, 64)] for o in small]

    def big(k, w, m, v, name, to_2d, from_2d):
        outs4 = _adamw_halves(to_2d(w), mine[k], other[k], to_2d(m), to_2d(v), c_arr, name)
        return [from_2d(z) for z in outs4]

    plain = (lambda w: w[0], lambda z: z[None])
    transposed = (lambda w: w[0].T, lambda z: z.T[None])
    in_proj = (lambda w: pad_rows(w[0].T), lambda z: z[0:IN_SHARD].T[None])
    r_g1 = big(0, ffn1_w_gate, m_ffn1_w_gate, v_ffn1_w_gate, "adamw_ffn1_gate", *transposed)
    r_u1 = big(1, ffn1_w_up, m_ffn1_w_up, v_ffn1_w_up, "adamw_ffn1_up", *transposed)
    r_d1 = big(2, ffn1_w_down, m_ffn1_w_down, v_ffn1_w_down, "adamw_ffn1_down", *plain)
    r_g2 = big(3, ffn2_w_gate, m_ffn2_w_gate, v_ffn2_w_gate, "adamw_ffn2_gate", *transposed)
    r_u2 = big(4, ffn2_w_up, m_ffn2_w_up, v_ffn2_w_up, "adamw_ffn2_up", *transposed)
    r_d2 = big(5, ffn2_w_down, m_ffn2_w_down, v_ffn2_w_down, "adamw_ffn2_down", *plain)
    r_in = big(6, w_in, m_w_in, v_w_in, "adamw_w_in", *in_proj)
    r_out = big(7, w_out, m_w_out, v_w_out, "adamw_w_out", *plain)

    def leaves(k, smalls):
        n1, nm, n2, nf, nret, ngla, ba, wa2 = smalls
        return [n1, r_g1[k], r_u1[k], r_d1[k], nm, r_in[k], nret, wa2, ba, ngla, r_out[k], n2, r_g2[k], r_u2[k], r_d2[k], nf]

    outs = [loss, grad_x.reshape(x.shape)]
    outs += leaves(0, s_grad) + leaves(1, s_delta) + leaves(2, s_m) + leaves(3, s_v)
    return tuple(outs)
```
